```python
import math
import jax, jax.numpy as jnp
from jax import lax
import numpy as np

D_MODEL = 1024
BATCH = 8
SEQ = 2048
DEPTH = 1

GLA_HEADS = 4
GLA_DK = D_MODEL // 2
GLA_DV = D_MODEL
GLA_HK = GLA_DK // GLA_HEADS
GLA_HV = GLA_DV // GLA_HEADS
GLA_RANK = 16
GLA_TAU = 16.0
GLA_CHUNK = 64
SB_HEADS = 8
SB_WIDTH = D_MODEL
SB_HD = SB_WIDTH // SB_HEADS
SB_BLOCK = 128
N_BRANCH = 2
EPS = 1e-6

SPLIT_SIZES = [GLA_DK, GLA_DK, GLA_DV, GLA_DV, GLA_RANK,
               SB_WIDTH, SB_WIDTH, SB_WIDTH, SB_WIDTH, N_BRANCH * D_MODEL]
SPLIT_IDX = [int(v) for v in np.cumsum(SPLIT_SIZES)[:-1]]
IN_COLS = int(sum(SPLIT_SIZES))

kernel_name = "gla_stickbreaking_gated_hybrid"


def rmsnorm(x, g):
    xf = x.astype(jnp.float32)
    y = xf * lax.rsqrt(jnp.mean(xf * xf, axis=-1, keepdims=True) + EPS)
    return y.astype(x.dtype) * g


def gla_chunked(q, k, v, log_a):
    B, T, H, dk = q.shape
    dv = v.shape[-1]
    C = GLA_CHUNK
    n = T // C
    def to_chunks(a):
        return a.reshape(B, n, C, H, a.shape[-1]).transpose(0, 3, 1, 2, 4).astype(jnp.float32)
    qc, kc, vc, la = to_chunks(q), to_chunks(k), to_chunks(v), to_chunks(log_a)
    bcum = jnp.cumsum(la, axis=3)
    b_last = bcum[:, :, :, -1:, :]
    qe = qc * jnp.exp(bcum) * (dk ** -0.5)
    ke = kc * jnp.exp(-bcum)
    kd = kc * jnp.exp(b_last - bcum)
    mask = jnp.tril(jnp.ones((C, C), jnp.float32))
    attn = jnp.einsum('bhnid,bhnjd->bhnij', qe, ke) * mask
    o_intra = jnp.einsum('bhnij,bhnjv->bhniv', attn, vc)

    def step(S, inp):
        q_i, k_i, v_i, d_i = inp
        o = jnp.einsum('bhid,bhdv->bhiv', q_i, S)
        S = d_i[..., None] * S + jnp.einsum('bhjd,bhjv->bhdv', k_i, v_i)
        return S, o

    xs = (jnp.moveaxis(qe, 2, 0), jnp.moveaxis(kd, 2, 0), jnp.moveaxis(vc, 2, 0),
          jnp.moveaxis(jnp.exp(b_last[:, :, :, 0, :]), 2, 0))
    S0 = jnp.zeros((B, H, dk, dv), jnp.float32)
    _, o_inter = lax.scan(step, S0, xs)
    o = o_intra + jnp.moveaxis(o_inter, 0, 2)
    return o.transpose(0, 2, 3, 1, 4).reshape(B, T, H, dv)


def stick_breaking(q, k, v):
    B, T, H, d = q.shape
    qh = q.transpose(0, 2, 1, 3)
    kh = k.transpose(0, 2, 1, 3)
    vh = v.transpose(0, 2, 1, 3)
    scale = 1.0 / math.sqrt(d)
    outs = []
    for i in range(T // SB_BLOCK):
        L = (i + 1) * SB_BLOCK
        qb = qh[:, :, i * SB_BLOCK:L]
        z = jnp.einsum('bhqd,bhkd->bhqk', qb, kh[:, :, :L]).astype(jnp.float32) * scale
        tpos = i * SB_BLOCK + jnp.arange(SB_BLOCK)[:, None]
        spos = jnp.arange(L)[None, :]
        causal = spos < tpos
        log1m = jnp.where(causal, jax.nn.log_sigmoid(-z), 0.0)
        between = lax.cumsum(log1m, axis=3, reverse=True) - log1m
        A = jnp.where(causal, jnp.exp(jax.nn.log_sigmoid(z) + between), 0.0)
        outs.append(jnp.einsum('bhqk,bhkd->bhqd', A.astype(vh.dtype), vh[:, :, :L]))
    o = jnp.concatenate(outs, axis=2)
    return o.transpose(0, 2, 1, 3)


def _fwd_setup_inputs(seed: int = 0) -> dict:
    key = jax.random.key(seed)
    ks = jax.random.split(key, 12)
    f = jnp.float32
    x = jax.random.normal(ks[0], (BATCH, SEQ, D_MODEL), f)
    norm_g = 1.0 + 0.02 * jax.random.normal(ks[1], (D_MODEL,), f)
    w_in = jax.random.normal(ks[2], (D_MODEL, IN_COLS), f) * D_MODEL ** -0.5
    w_dec_up = jax.random.normal(ks[3], (GLA_RANK, GLA_DK), f) * GLA_RANK ** -0.5
    b_dec = 0.1 * jax.random.normal(ks[4], (GLA_DK,), f)
    gla_norm_g = 1.0 + 0.02 * jax.random.normal(ks[5], (GLA_HV,), f)
    w_pa = jax.random.normal(ks[6], (GLA_DV, D_MODEL), f) * GLA_DV ** -0.5
    w_pb = jax.random.normal(ks[7], (SB_WIDTH, D_MODEL), f) * SB_WIDTH ** -0.5
    b_gate = 0.01 * jax.random.normal(ks[8], (N_BRANCH * D_MODEL,), f)
    w_o = jax.random.normal(ks[9], (D_MODEL, D_MODEL), f) * D_MODEL ** -0.5
    final_g = 1.0 + 0.02 * jax.random.normal(ks[10], (D_MODEL,), f)
    return {"x": x, "norm_g": norm_g, "w_in": w_in, "w_dec_up": w_dec_up, "b_dec": b_dec,
            "gla_norm_g": gla_norm_g, "w_pa": w_pa, "w_pb": w_pb, "b_gate": b_gate,
            "w_o": w_o, "final_g": final_g}


def _fwd_reference(x, norm_g, w_in, w_dec_up, b_dec, gla_norm_g, w_pa, w_pb, b_gate, w_o, final_g):
    B, T, _ = x.shape
    for _layer in range(DEPTH):
        h = rmsnorm(x, norm_g)
        proj = h @ w_in
        (g_q, g_k, g_v, g_gate, g_rank,
         s_q, s_k, s_v, s_gate, m_logits) = jnp.split(proj, SPLIT_IDX, axis=-1)

        log_a = jax.nn.log_sigmoid((g_rank @ w_dec_up + b_dec).astype(jnp.float32)) / GLA_TAU
        o_gla = gla_chunked(g_q.reshape(B, T, GLA_HEADS, GLA_HK),
                            g_k.reshape(B, T, GLA_HEADS, GLA_HK),
                            g_v.reshape(B, T, GLA_HEADS, GLA_HV),
                            log_a.reshape(B, T, GLA_HEADS, GLA_HK)).astype(x.dtype)
        o_gla = rmsnorm(o_gla, gla_norm_g).reshape(B, T, GLA_DV) * jax.nn.silu(g_gate)
        y_a = o_gla @ w_pa

        o_sb = stick_breaking(s_q.reshape(B, T, SB_HEADS, SB_HD),
                              s_k.reshape(B, T, SB_HEADS, SB_HD),
                              s_v.reshape(B, T, SB_HEADS, SB_HD))
        o_sb = o_sb.reshape(B, T, SB_WIDTH) * jax.nn.silu(s_gate)
        y_b = o_sb @ w_pb

        gates = jax.nn.sigmoid(m_logits + b_gate).reshape(B, T, N_BRANCH, D_MODEL)
        merged = gates[:, :, 0] * y_a + gates[:, :, 1] * y_b
        x = x + merged @ w_o
    return rmsnorm(x, final_g)


import jax as _jax
import jax.numpy as _jnp

TWIN_FORMAT = 'train_step'
FWD_PARAMS = ['x', 'norm_g', 'w_in', 'w_dec_up', 'b_dec', 'gla_norm_g', 'w_pa', 'w_pb', 'b_gate', 'w_o', 'final_g']
TWIN_WEIGHTS = ['norm_g', 'w_in', 'w_dec_up', 'b_dec', 'gla_norm_g', 'w_pa', 'w_pb', 'b_gate', 'w_o', 'final_g']
TWIN_DIFF_INPUT = 'x'
TWIN_INPUTS = ['x', 'norm_g', 'w_in', 'w_dec_up', 'b_dec', 'gla_norm_g', 'w_pa', 'w_pb', 'b_gate', 'w_o', 'final_g', 'loss_target', 'm_norm_g', 'm_w_in', 'm_w_dec_up', 'm_b_dec', 'm_gla_norm_g', 'm_w_pa', 'm_w_pb', 'm_b_gate', 'm_w_o', 'm_final_g', 'v_norm_g', 'v_w_in', 'v_w_dec_up', 'v_b_dec', 'v_gla_norm_g', 'v_w_pa', 'v_w_pb', 'v_b_gate', 'v_w_o', 'v_final_g']
TWIN_OUTPUTS = ['loss', 'grad_x', 'grad_norm_g', 'grad_w_in', 'grad_w_dec_up', 'grad_b_dec', 'grad_gla_norm_g', 'grad_w_pa', 'grad_w_pb', 'grad_b_gate', 'grad_w_o', 'grad_final_g', 'delta_norm_g', 'delta_w_in', 'delta_w_dec_up', 'delta_b_dec', 'delta_gla_norm_g', 'delta_w_pa', 'delta_w_pb', 'delta_b_gate', 'delta_w_o', 'delta_final_g', 'new_m_norm_g', 'new_m_w_in', 'new_m_w_dec_up', 'new_m_b_dec', 'new_m_gla_norm_g', 'new_m_w_pa', 'new_m_w_pb', 'new_m_b_gate', 'new_m_w_o', 'new_m_final_g', 'new_v_norm_g', 'new_v_w_in', 'new_v_w_dec_up', 'new_v_b_dec', 'new_v_gla_norm_g', 'new_v_w_pa', 'new_v_w_pb', 'new_v_b_gate', 'new_v_w_o', 'new_v_final_g']
TWIN_LEAF_KINDS = {'loss': 'loss', 'grad_x': 'grad_x', 'grad_norm_g': 'grad_w', 'grad_w_in': 'grad_w', 'grad_w_dec_up': 'grad_w', 'grad_b_dec': 'grad_w', 'grad_gla_norm_g': 'grad_w', 'grad_w_pa': 'grad_w', 'grad_w_pb': 'grad_w', 'grad_b_gate': 'grad_w', 'grad_w_o': 'grad_w', 'grad_final_g': 'grad_w', 'delta_norm_g': 'delta_w', 'delta_w_in': 'delta_w', 'delta_w_dec_up': 'delta_w', 'delta_b_dec': 'delta_w', 'delta_gla_norm_g': 'delta_w', 'delta_w_pa': 'delta_w', 'delta_w_pb': 'delta_w', 'delta_b_gate': 'delta_w', 'delta_w_o': 'delta_w', 'delta_final_g': 'delta_w', 'new_m_norm_g': 'new_m', 'new_m_w_in': 'new_m', 'new_m_w_dec_up': 'new_m', 'new_m_b_dec': 'new_m', 'new_m_gla_norm_g': 'new_m', 'new_m_w_pa': 'new_m', 'new_m_w_pb': 'new_m', 'new_m_b_gate': 'new_m', 'new_m_w_o': 'new_m', 'new_m_final_g': 'new_m', 'new_v_norm_g': 'new_v', 'new_v_w_in': 'new_v', 'new_v_w_dec_up': 'new_v', 'new_v_b_dec': 'new_v', 'new_v_gla_norm_g': 'new_v', 'new_v_w_pa': 'new_v', 'new_v_w_pb': 'new_v', 'new_v_b_gate': 'new_v', 'new_v_w_o': 'new_v', 'new_v_final_g': 'new_v'}


def _forward(args):
    return _fwd_reference(*[args[k] for k in FWD_PARAMS])


def _output_shape():
    out = _jax.eval_shape(lambda: _forward(_fwd_setup_inputs(0)))
    return out.shape, out.dtype

N_MICROBATCH = 1
ADAM_LR = 0.001
ADAM_B1 = 0.9
ADAM_B2 = 0.999
ADAM_EPS = 1e-08
ADAM_WD = 0.01
ADAM_STEP = 10
PER_EXAMPLE_BATCH_AXIS = {'x': 0, 'loss_target': 0}
SHARED_INPUTS = []
_WEIGHT_DTYPES = {'norm_g': _jnp.float32, 'w_in': _jnp.float32, 'w_dec_up': _jnp.float32, 'b_dec': _jnp.float32, 'gla_norm_g': _jnp.float32, 'w_pa': _jnp.float32, 'w_pb': _jnp.float32, 'b_gate': _jnp.float32, 'w_o': _jnp.float32, 'final_g': _jnp.float32}
MOMENT_SCALE = {'norm_g': 8.906866e-02, 'w_in': 2.940848e-02, 'w_dec_up': 6.359427e-03, 'b_dec': 2.532301e-02, 'gla_norm_g': 8.206095e-02, 'w_pa': 3.790380e-02, 'w_pb': 2.442693e-02, 'b_gate': 1.254061e-02, 'w_o': 4.507516e-02, 'final_g': 1.600876e+01}


def _to_microbatches(a, axis):
    t = _jnp.moveaxis(a, axis, 0)
    t = t.reshape((N_MICROBATCH, t.shape[0] // N_MICROBATCH) + t.shape[1:])
    return _jnp.moveaxis(t, 1, axis + 1)


def setup_inputs(seed: int = 0) -> dict:
    inp = _fwd_setup_inputs(seed)
    key = _jax.random.fold_in(_jax.random.key(seed), 7919)
    shape, _ = _output_shape()
    out = dict(inp)
    out["loss_target"] = _jax.random.normal(_jax.random.fold_in(key, 0), shape, _jnp.float32)
    for i, name in enumerate(TWIN_WEIGHTS):
        w = inp[name].astype(_jnp.float32)
        if MOMENT_SCALE is None:
            s = _jnp.sqrt(_jnp.mean(_jnp.square(w)) + 1e-30)
        else:
            s = MOMENT_SCALE[name]
        km, kv = _jax.random.split(_jax.random.fold_in(key, i + 1))
        out[name] = w
        out["m_" + name] = s * _jax.random.normal(km, w.shape, _jnp.float32)
        out["v_" + name] = (s * s) * _jax.random.uniform(kv, w.shape, _jnp.float32, 0.5, 1.5)
    if N_MICROBATCH > 1:
        for name, axis in PER_EXAMPLE_BATCH_AXIS.items():
            out[name] = _to_microbatches(out[name], axis)
    return {'x': out['x'], 'norm_g': out['norm_g'], 'w_in': out['w_in'], 'w_dec_up': out['w_dec_up'], 'b_dec': out['b_dec'], 'gla_norm_g': out['gla_norm_g'], 'w_pa': out['w_pa'], 'w_pb': out['w_pb'], 'b_gate': out['b_gate'], 'w_o': out['w_o'], 'final_g': out['final_g'], 'loss_target': out['loss_target'], 'm_norm_g': out['m_norm_g'], 'm_w_in': out['m_w_in'], 'm_w_dec_up': out['m_w_dec_up'], 'm_b_dec': out['m_b_dec'], 'm_gla_norm_g': out['m_gla_norm_g'], 'm_w_pa': out['m_w_pa'], 'm_w_pb': out['m_w_pb'], 'm_b_gate': out['m_b_gate'], 'm_w_o': out['m_w_o'], 'm_final_g': out['m_final_g'], 'v_norm_g': out['v_norm_g'], 'v_w_in': out['v_w_in'], 'v_w_dec_up': out['v_w_dec_up'], 'v_b_dec': out['v_b_dec'], 'v_gla_norm_g': out['v_gla_norm_g'], 'v_w_pa': out['v_w_pa'], 'v_w_pb': out['v_w_pb'], 'v_b_gate': out['v_b_gate'], 'v_w_o': out['v_w_o'], 'v_final_g': out['v_final_g']}


def _loss(weights, diff, rest, loss_target):
    with _jax.named_scope("forward"):
        args = {**rest, TWIN_DIFF_INPUT: diff, **{k: w.astype(_WEIGHT_DTYPES[k]) for k, w in weights.items()}}
        y = _forward(args)
    with _jax.named_scope("loss_head"):
        err = _jnp.square(y.astype(_jnp.float32) - loss_target)
        return 0.5 * _jnp.sum(_jnp.mean(err, axis=-1)) if err.ndim else 0.5 * err


def _adamw(w, g, m, v):
    m = ADAM_B1 * m + (1.0 - ADAM_B1) * g
    v = ADAM_B2 * v + (1.0 - ADAM_B2) * _jnp.square(g)
    m_hat = m / (1.0 - ADAM_B1 ** ADAM_STEP)
    v_hat = v / (1.0 - ADAM_B2 ** ADAM_STEP)
    delta = -ADAM_LR * (m_hat / (_jnp.sqrt(v_hat) + ADAM_EPS) + ADAM_WD * w)
    return delta, m, v


def reference(x, norm_g, w_in, w_dec_up, b_dec, gla_norm_g, w_pa, w_pb, b_gate, w_o, final_g, loss_target, m_norm_g, m_w_in, m_w_dec_up, m_b_dec, m_gla_norm_g, m_w_pa, m_w_pb, m_b_gate, m_w_o, m_final_g, v_norm_g, v_w_in, v_w_dec_up, v_b_dec, v_gla_norm_g, v_w_pa, v_w_pb, v_b_gate, v_w_o, v_final_g):
    given = dict(x=x, norm_g=norm_g, w_in=w_in, w_dec_up=w_dec_up, b_dec=b_dec, gla_norm_g=gla_norm_g, w_pa=w_pa, w_pb=w_pb, b_gate=b_gate, w_o=w_o, final_g=final_g, loss_target=loss_target, m_norm_g=m_norm_g, m_w_in=m_w_in, m_w_dec_up=m_w_dec_up, m_b_dec=m_b_dec, m_gla_norm_g=m_gla_norm_g, m_w_pa=m_w_pa, m_w_pb=m_w_pb, m_b_gate=m_b_gate, m_w_o=m_w_o, m_final_g=m_final_g, v_norm_g=v_norm_g, v_w_in=v_w_in, v_w_dec_up=v_w_dec_up, v_b_dec=v_b_dec, v_gla_norm_g=v_gla_norm_g, v_w_pa=v_w_pa, v_w_pb=v_w_pb, v_b_gate=v_b_gate, v_w_o=v_w_o, v_final_g=v_final_g)
    weights = {n: given[n] for n in TWIN_WEIGHTS}
    shared = {n: given[n] for n in SHARED_INPUTS}
    per_example = {n: given[n] for n in ['x']}
    grad_fn = _jax.value_and_grad(_loss, argnums=(0, 1))

    def one_microbatch(ex, loss_target):
        ex = dict(ex)
        diff = ex.pop(TWIN_DIFF_INPUT)
        return grad_fn(weights, diff, {**shared, **ex}, loss_target)

    if N_MICROBATCH == 1:
        loss, (grad_w, grad_x) = one_microbatch(per_example, given["loss_target"])
    else:
        def body(carry, xs):
            loss_sum, grad_sum = carry
            l_k, (gw_k, gx_k) = one_microbatch(xs[0], xs[1])
            with _jax.named_scope("update"):
                return (loss_sum + l_k, _jax.tree.map(_jnp.add, grad_sum, gw_k)), gx_k

        init = (_jnp.zeros((), _jnp.float32), _jax.tree.map(_jnp.zeros_like, weights))
        (loss, grad_w), grad_x = _jax.lax.scan(body, init, (per_example, given["loss_target"]))
    with _jax.named_scope("update"):
        delta_w, new_m, new_v = {}, {}, {}
        for n in TWIN_WEIGHTS:
            delta_w[n], new_m[n], new_v[n] = _adamw(weights[n], grad_w[n], given["m_" + n], given["v_" + n])
    return (loss, grad_x, *[grad_w[n] for n in TWIN_WEIGHTS], *[delta_w[n] for n in TWIN_WEIGHTS],
            *[new_m[n] for n in TWIN_WEIGHTS], *[new_v[n] for n in TWIN_WEIGHTS])
```

```python
import functools
import math

import jax
import jax.numpy as jnp
from jax import lax
from jax.experimental import pallas as pl
from jax.experimental.pallas import tpu as pltpu

F32 = jnp.float32
BF16 = jnp.bfloat16

N_DEV = 8
D_MODEL = 1024
GLA_HEADS = 4
GLA_HK = 128
GLA_HV = 256
GLA_DK = 512
GLA_RANK = 16
GLA_TAU = 16.0
GLA_CHUNK = 64
SB_HEADS = 8
SB_HD = 128
SB_BLOCK = 128
EPS = 1e-6
N_GROUPS = 9
RANK_COL = 3072
IN_COLS = 9232
SHARD_COLS = IN_COLS // N_DEV

ADAM_LR = 0.001
ADAM_B1 = 0.9
ADAM_B2 = 0.999
ADAM_EPS = 1e-08
ADAM_WD = 0.01
ADAM_STEP = 10

VMEM_LIMIT = 56 * 1024 * 1024


def _cparams(sem=None):
    return pltpu.CompilerParams(dimension_semantics=sem, vmem_limit_bytes=VMEM_LIMIT)


def _dot(a, b):
    return jnp.dot(a, b, preferred_element_type=F32)


def _dot_nt(a, b):
    return lax.dot_general(a, b, (((1,), (1,)), ((), ())), preferred_element_type=F32)


def _dot_tn(a, b):
    return lax.dot_general(a, b, (((0,), (0,)), ((), ())), preferred_element_type=F32)


def _bf(x):
    return x.astype(BF16)


def _split3(x):
    hi = x.astype(BF16)
    r = x - hi.astype(F32)
    mid = r.astype(BF16)
    lo = (r - mid.astype(F32)).astype(BF16)
    return hi, mid, lo


def _tri_left(tri, x):
    hi, mid, lo = _split3(x)
    return _dot(tri, hi) + _dot(tri, mid) + _dot(tri, lo)


def _tri_right(x, tri):
    hi, mid, lo = _split3(x)
    return _dot(hi, tri) + _dot(mid, tri) + _dot(lo, tri)


def _iota2(n, m, dim):
    return lax.broadcasted_iota(jnp.int32, (n, m), dim)


def _sigmoid(x):
    return 1.0 / (1.0 + jnp.exp(-x))


def _softplus_neg_abs(z):
    return jnp.log(1.0 + jnp.exp(-jnp.abs(z)))


def _exchange(arrs, scatter, name):
    n = len(arrs)

    def body(*refs):
        ins = refs[:n]
        outs = refs[n:2 * n]
        send_sems, recv_sems, loc_sems = refs[2 * n:]
        x = lax.axis_index("x")
        y = lax.axis_index("y")
        c = lax.axis_index("c")
        me = 4 * x + 2 * y + c
        local = []
        for a in range(n):
            src = ins[a].at[me] if scatter else ins[a]
            cp = pltpu.make_async_copy(src, outs[a].at[me], loc_sems.at[a])
            cp.start()
            local.append(cp)
        remote = []
        for k in (1, 2, 4, 3, 5, 6, 7):
            px = 1 - x if (k >> 2) & 1 else x
            py = 1 - y if (k >> 1) & 1 else y
            pc = 1 - c if k & 1 else c
            pid = 4 * px + 2 * py + pc
            for a in range(n):
                src = ins[a].at[pid] if scatter else ins[a]
                send = pltpu.make_async_remote_copy(
                    src_ref=src, dst_ref=outs[a].at[me],
                    send_sem=send_sems.at[a, k - 1], recv_sem=recv_sems.at[a, k - 1],
                    device_id=(px, py, pc), device_id_type=pl.DeviceIdType.MESH)
                send.start()
                recv = pltpu.make_async_remote_copy(
                    src_ref=src, dst_ref=outs[a].at[pid],
                    send_sem=send_sems.at[a, k - 1], recv_sem=recv_sems.at[a, k - 1],
                    device_id=(px, py, pc), device_id_type=pl.DeviceIdType.MESH)
                remote.append((send, recv))
        for send, recv in remote:
            recv.wait_recv()
        for send, recv in remote:
            send.wait_send()
        for cp in local:
            cp.wait()

    out_shape = []
    for a in arrs:
        shp = a.shape if scatter else (N_DEV,) + a.shape
        out_shape.append(jax.ShapeDtypeStruct(shp, a.dtype))
    any_spec = pl.BlockSpec(memory_space=pl.ANY)
    return pl.pallas_call(
        body, name=name,
        out_shape=tuple(out_shape),
        in_specs=[any_spec] * n,
        out_specs=tuple([any_spec] * n),
        scratch_shapes=[pltpu.SemaphoreType.DMA((n, N_DEV - 1)),
                        pltpu.SemaphoreType.DMA((n, N_DEV - 1)),
                        pltpu.SemaphoreType.DMA((n,))],
    )(*arrs)


def _proj_call(x, norm_g, w3, wr):
    T, D = x.shape
    tm = min(512, T)

    def body(x_ref, g_ref, w_ref, wr_ref, proj_ref, rank_ref, ht_ref, h_scr):
        @pl.when(pl.program_id(1) == 0)
        def _():
            xv = x_ref[...]
            r = lax.rsqrt(jnp.mean(xv * xv, axis=-1, keepdims=True) + EPS)
            h = (xv * r) * g_ref[...]
            hb = _bf(h)
            h_scr[...] = hb
            ht_ref[...] = _bf(h.T)
            rank_ref[...] = _dot(hb, wr_ref[...])
        proj_ref[...] = _dot(h_scr[...], w_ref[...])

    return pl.pallas_call(
        body, name="proj",
        grid=(T // tm, N_GROUPS),
        in_specs=[pl.BlockSpec((tm, D), lambda i, j: (i, 0)),
                  pl.BlockSpec((1, D), lambda i, j: (0, 0)),
                  pl.BlockSpec((None, D, 1024), lambda i, j: (j, 0, 0)),
                  pl.BlockSpec((D, 128), lambda i, j: (0, 0))],
        out_specs=(pl.BlockSpec((None, tm, 1024), lambda i, j: (j, i, 0)),
                   pl.BlockSpec((tm, 128), lambda i, j: (i, 0)),
                   pl.BlockSpec((D, tm), lambda i, j: (0, i))),
        out_shape=(jax.ShapeDtypeStruct((N_GROUPS, T, 1024), F32),
                   jax.ShapeDtypeStruct((T, 128), F32),
                   jax.ShapeDtypeStruct((D, T), BF16)),
        scratch_shapes=[pltpu.VMEM((tm, D), BF16)],
        compiler_params=_cparams(("arbitrary", "arbitrary")),
    )(x, norm_g, w3, wr)


def _gla_chunk_terms(la_h, q, k):
    C = GLA_CHUNK
    low = _bf((_iota2(C, C, 0) >= _iota2(C, C, 1)).astype(F32))
    b = _tri_left(low, la_h)
    bl = b[C - 1:C, :]
    eb = jnp.exp(b)
    enb = jnp.exp(-b)
    ebl_b = jnp.exp(bl - b)
    scale = GLA_HK ** -0.5
    qe = q * eb * scale
    ke = k * enb
    kd = k * ebl_b
    return b, bl, eb, enb, ebl_b, qe, ke, kd


def _gla_fwd_call(proj3, rank, wdec, bdec):
    T = proj3.shape[1]
    C = GLA_CHUNK
    n_chunks = T // C

    def body(qk_ref, v_ref, rank_ref, wd_ref, bd_ref, o_ref, st_ref, la_ref, st_scr):
        @pl.when(pl.program_id(0) == 0)
        def _():
            st_scr[...] = jnp.zeros_like(st_scr)

        dec = _dot(_bf(rank_ref[...]), _bf(wd_ref[...])) + bd_ref[...]
        la = (jnp.minimum(dec, 0.0) - _softplus_neg_abs(dec)) / GLA_TAU
        la_ref[...] = la
        mask = _iota2(C, C, 0) >= _iota2(C, C, 1)
        for hh in range(GLA_HEADS):
            la_h = la[:, hh * GLA_HK:(hh + 1) * GLA_HK]
            q = qk_ref[:, hh * GLA_HK:(hh + 1) * GLA_HK]
            k = qk_ref[:, GLA_DK + hh * GLA_HK:GLA_DK + (hh + 1) * GLA_HK]
            v = _bf(v_ref[:, hh * GLA_HV:(hh + 1) * GLA_HV])
            _, bl, _, _, _, qe, ke, kd = _gla_chunk_terms(la_h, q, k)
            st = st_scr[hh]
            st_ref[hh] = st
            p = jnp.where(mask, _dot_nt(_bf(qe), _bf(ke)), 0.0)
            o = _dot(_bf(p), v) + _dot_nt(_bf(qe), _bf(st))
            o_ref[:, hh * GLA_HV:(hh + 1) * GLA_HV] = o
            st_scr[hh] = st * jnp.exp(bl) + _dot_tn(v, _bf(kd))

    return pl.pallas_call(
        body, name="gla_fwd",
        grid=(n_chunks,),
        in_specs=[pl.BlockSpec((None, C, 1024), lambda n: (0, n, 0)),
                  pl.BlockSpec((None, C, 1024), lambda n: (1, n, 0)),
                  pl.BlockSpec((C, 128), lambda n: (n, 0)),
                  pl.BlockSpec((128, GLA_DK), lambda n: (0, 0)),
                  pl.BlockSpec((1, GLA_DK), lambda n: (0, 0))],
        out_specs=(pl.BlockSpec((C, 1024), lambda n: (n, 0)),
                   pl.BlockSpec((None, GLA_HEADS, GLA_HV, GLA_HK), lambda n: (n, 0, 0, 0)),
                   pl.BlockSpec((C, GLA_DK), lambda n: (n, 0))),
        out_shape=(jax.ShapeDtypeStruct((T, 1024), F32),
                   jax.ShapeDtypeStruct((n_chunks, GLA_HEADS, GLA_HV, GLA_HK), F32),
                   jax.ShapeDtypeStruct((T, GLA_DK), F32)),
        scratch_shapes=[pltpu.VMEM((GLA_HEADS, GLA_HV, GLA_HK), F32)],
        compiler_params=_cparams(("arbitrary",)),
    )(proj3, proj3, rank, wdec, bdec)


def _gla_bwd_call(proj3, la, do_gla, st_all, rank, wdec):
    T = proj3.shape[1]
    C = GLA_CHUNK
    n_chunks = T // C
    last = n_chunks - 1

    def body(qk_ref, v_ref, la_ref, do_ref, st_ref, rank_ref, wd_ref,
             dqk_ref, dv_ref, drank_ref, dwd_ref, dbd_ref, dst_scr):
        @pl.when(pl.program_id(0) == 0)
        def _():
            dst_scr[...] = jnp.zeros_like(dst_scr)
            dwd_ref[...] = jnp.zeros_like(dwd_ref)
            dbd_ref[...] = jnp.zeros_like(dbd_ref)

        mask = _iota2(C, C, 0) >= _iota2(C, C, 1)
        upp = _bf((_iota2(C, C, 0) <= _iota2(C, C, 1)).astype(F32))
        scale = GLA_HK ** -0.5
        la = la_ref[...]
        ddec_parts = []
        for hh in range(GLA_HEADS):
            la_h = la[:, hh * GLA_HK:(hh + 1) * GLA_HK]
            q = qk_ref[:, hh * GLA_HK:(hh + 1) * GLA_HK]
            k = qk_ref[:, GLA_DK + hh * GLA_HK:GLA_DK + (hh + 1) * GLA_HK]
            v = _bf(v_ref[:, hh * GLA_HV:(hh + 1) * GLA_HV])
            do = _bf(do_ref[:, hh * GLA_HV:(hh + 1) * GLA_HV])
            _, bl, eb, enb, ebl_b, qe, ke, kd = _gla_chunk_terms(la_h, q, k)
            qeb, keb, kdb = _bf(qe), _bf(ke), _bf(kd)
            st = st_ref[hh]
            dstn = dst_scr[hh]
            dstnb = _bf(dstn)
            ebl = jnp.exp(bl)
            p = jnp.where(mask, _dot_nt(qeb, keb), 0.0)
            dp = _bf(jnp.where(mask, _dot_nt(do, v), 0.0))
            dv = _dot_tn(_bf(p), do) + _dot_nt(kdb, dstnb)
            dqe = _dot(dp, keb) + _dot(do, _bf(st))
            dke = _dot_tn(dp, qeb)
            dkd = _dot(v, dstnb)
            dst_scr[hh] = _dot_tn(do, qeb) + dstn * ebl
            debl = jnp.sum(dstn * st, axis=0, keepdims=True)
            dkd_kd = dkd * kd
            db = dqe * qe - dke * ke - dkd_kd
            dbl = jnp.sum(dkd_kd, axis=0, keepdims=True) + ebl * debl
            dla = _tri_left(upp, db) + dbl
            dq = dqe * eb * scale
            dk = dke * enb + dkd * ebl_b
            dqk_ref[:, hh * GLA_HK:(hh + 1) * GLA_HK] = _bf(dq)
            dqk_ref[:, GLA_DK + hh * GLA_HK:GLA_DK + (hh + 1) * GLA_HK] = _bf(dk)
            dv_ref[:, hh * GLA_HV:(hh + 1) * GLA_HV] = _bf(dv)
            ddec_parts.append(dla * (1.0 / GLA_TAU) * (1.0 - jnp.exp(GLA_TAU * la_h)))
        ddec = jnp.concatenate(ddec_parts, axis=1)
        ddecb = _bf(ddec)
        drank_ref[...] = _bf(_dot_nt(ddecb, _bf(wd_ref[...])))
        dwd_ref[...] += _dot_tn(_bf(rank_ref[...]), ddecb)
        dbd_ref[...] += jnp.sum(ddec, axis=0, keepdims=True)

    return pl.pallas_call(
        body, name="gla_bwd",
        grid=(n_chunks,),
        in_specs=[pl.BlockSpec((None, C, 1024), lambda n: (0, last - n, 0)),
                  pl.BlockSpec((None, C, 1024), lambda n: (1, last - n, 0)),
                  pl.BlockSpec((C, GLA_DK), lambda n: (last - n, 0)),
                  pl.BlockSpec((C, 1024), lambda n: (last - n, 0)),
                  pl.BlockSpec((None, GLA_HEADS, GLA_HV, GLA_HK), lambda n: (last - n, 0, 0, 0)),
                  pl.BlockSpec((C, 128), lambda n: (last - n, 0)),
                  pl.BlockSpec((128, GLA_DK), lambda n: (0, 0))],
        out_specs=(pl.BlockSpec((C, 1024), lambda n: (last - n, 0)),
                   pl.BlockSpec((C, 1024), lambda n: (last - n, 0)),
                   pl.BlockSpec((C, 128), lambda n: (last - n, 0)),
                   pl.BlockSpec((128, GLA_DK), lambda n: (0, 0)),
                   pl.BlockSpec((1, GLA_DK), lambda n: (0, 0))),
        out_shape=(jax.ShapeDtypeStruct((T, 1024), BF16),
                   jax.ShapeDtypeStruct((T, 1024), BF16),
                   jax.ShapeDtypeStruct((T, 128), BF16),
                   jax.ShapeDtypeStruct((128, GLA_DK), F32),
                   jax.ShapeDtypeStruct((1, GLA_DK), F32)),
        scratch_shapes=[pltpu.VMEM((GLA_HEADS, GLA_HV, GLA_HK), F32)],
        compiler_params=_cparams(("arbitrary",)),
    )(proj3, proj3, la, do_gla, st_all, rank, wdec)


def _sb_logs(z):
    sp = _softplus_neg_abs(z)
    return jnp.minimum(z, 0.0) - sp, -jnp.maximum(z, 0.0) - sp


def _sb_fwd_call(proj3):
    T = proj3.shape[1]
    B = SB_BLOCK
    scale = 1.0 / math.sqrt(SB_HD)

    def body(q_ref, k_ref, v_ref, o_ref):
        i = pl.program_id(1)
        q = _bf(q_ref[...])
        strict = _iota2(B, B, 1) < _iota2(B, B, 0)
        after = _bf((_iota2(B, B, 0) > _iota2(B, B, 1)).astype(F32))

        def block(j, carry, masked):
            acc, cb = carry
            off = pl.multiple_of(j * B, B)
            ks = _bf(k_ref[pl.ds(off, B), :])
            vs = _bf(v_ref[pl.ds(off, B), :])
            z = _dot_nt(q, ks) * scale
            lsz, l1m = _sb_logs(z)
            if masked:
                l1m = jnp.where(strict, l1m, 0.0)
            a = jnp.exp(lsz + cb + _tri_right(l1m, after))
            if masked:
                a = jnp.where(strict, a, 0.0)
            acc = acc + _dot(_bf(a), vs)
            cb = cb + jnp.sum(l1m, axis=1, keepdims=True)
            return acc, cb

        carry = (jnp.zeros((B, SB_HD), F32), jnp.zeros((B, 1), F32))
        carry = block(i, carry, True)
        carry = lax.fori_loop(1, i + 1, lambda jj, cr: block(i - jj, cr, False), carry)
        o_ref[...] = carry[0]

    return pl.pallas_call(
        body, name="sb_fwd",
        grid=(SB_HEADS, T // B),
        in_specs=[pl.BlockSpec((None, B, SB_HD), lambda h, i: (3, i, h)),
                  pl.BlockSpec((None, T, SB_HD), lambda h, i: (4, 0, h)),
                  pl.BlockSpec((None, T, SB_HD), lambda h, i: (5, 0, h))],
        out_specs=pl.BlockSpec((B, SB_HD), lambda h, i: (i, h)),
        out_shape=jax.ShapeDtypeStruct((T, 1024), F32),
        compiler_params=_cparams(("arbitrary", "arbitrary")),
    )(proj3, proj3, proj3)


def _sb_bwd_call(proj3, o_sb, do_sb):
    T = proj3.shape[1]
    B = SB_BLOCK
    nb = T // B
    scale = 1.0 / math.sqrt(SB_HD)

    def body(q_ref, k_ref, v_ref, o_ref, do_ref, dq_ref, dk_ref, dv_ref,
             dk_scr, dv_scr, beta_scr, g_scr):
        i = pl.program_id(1)

        @pl.when(i == 0)
        def _():
            dk_scr[...] = jnp.zeros_like(dk_scr)
            dv_scr[...] = jnp.zeros_like(dv_scr)

        q = _bf(q_ref[...])
        do = _bf(do_ref[...])
        strict = _iota2(B, B, 1) < _iota2(B, B, 0)
        after = _bf((_iota2(B, B, 0) > _iota2(B, B, 1)).astype(F32))
        before = _bf((_iota2(B, B, 0) < _iota2(B, B, 1)).astype(F32))

        def pass1(j, cb, masked):
            off = pl.multiple_of(j * B, B)
            ks = _bf(k_ref[pl.ds(off, B), :])
            vs = _bf(v_ref[pl.ds(off, B), :])
            z = _dot_nt(q, ks) * scale
            lsz, l1m = _sb_logs(z)
            if masked:
                l1m = jnp.where(strict, l1m, 0.0)
            a = jnp.exp(lsz + cb + _tri_right(l1m, after))
            if masked:
                a = jnp.where(strict, a, 0.0)
            g = a * _dot_nt(do, vs)
            beta_scr[j] = jnp.exp(lsz)
            g_scr[j] = g
            dv_scr[pl.ds(off, B), :] += _dot_tn(_bf(a), do)
            return cb + jnp.sum(l1m, axis=1, keepdims=True)

        cb = pass1(i, jnp.zeros((B, 1), F32), True)
        lax.fori_loop(1, i + 1, lambda jj, cr: pass1(i - jj, cr, False), cb)

        def pass2(j, carry, masked):
            dq, cg = carry
            off = pl.multiple_of(j * B, B)
            ks = _bf(k_ref[pl.ds(off, B), :])
            g = g_scr[j]
            beta = beta_scr[j]
            dz = g * (1.0 - beta) - beta * (cg + _tri_right(g, before))
            if masked:
                dz = jnp.where(strict, dz, 0.0)
            dzb = _bf(dz * scale)
            dq = dq + _dot(dzb, ks)
            dk_scr[pl.ds(off, B), :] += _dot_tn(dzb, q)
            return dq, cg + jnp.sum(g, axis=1, keepdims=True)

        carry = (jnp.zeros((B, SB_HD), F32), jnp.zeros((B, 1), F32))
        carry = lax.fori_loop(0, i, lambda j, cr: pass2(j, cr, False), carry)
        carry = pass2(i, carry, True)
        dq_ref[...] = _bf(carry[0])

        @pl.when(i == nb - 1)
        def _():
            dk_ref[...] = _bf(dk_scr[...])
            dv_ref[...] = _bf(dv_scr[...])

    return pl.pallas_call(
        body, name="sb_bwd",
        grid=(SB_HEADS, nb),
        in_specs=[pl.BlockSpec((None, B, SB_HD), lambda h, i: (3, i, h)),
                  pl.BlockSpec((None, T, SB_HD), lambda h, i: (4, 0, h)),
                  pl.BlockSpec((None, T, SB_HD), lambda h, i: (5, 0, h)),
                  pl.BlockSpec((B, SB_HD), lambda h, i: (i, h)),
                  pl.BlockSpec((B, SB_HD), lambda h, i: (i, h))],
        out_specs=(pl.BlockSpec((B, SB_HD), lambda h, i: (i, h)),
                   pl.BlockSpec((T, SB_HD), lambda h, i: (0, h)),
                   pl.BlockSpec((T, SB_HD), lambda h, i: (0, h))),
        out_shape=(jax.ShapeDtypeStruct((T, 1024), BF16),
                   jax.ShapeDtypeStruct((T, 1024), BF16),
                   jax.ShapeDtypeStruct((T, 1024), BF16)),
        scratch_shapes=[pltpu.VMEM((T, SB_HD), F32), pltpu.VMEM((T, SB_HD), F32),
                        pltpu.VMEM((nb, B, B), F32), pltpu.VMEM((nb, B, B), F32)],
        compiler_params=_cparams(("arbitrary", "arbitrary")),
    )(proj3, proj3, proj3, o_sb, do_sb)


def _mid_call(o_gla, o_sb, proj3, x, target, wpa, wpb, wo, gla_g, b_gate, final_g):
    T, D = x.shape
    tm = min(256, T)

    def body(og_ref, ggate_ref, osb_ref, sgate_ref, ma_ref, mb_ref, x_ref, tgt_ref,
             wpa_ref, wpb_ref, wo_ref, glag_ref, bg_ref, fg_ref,
             dx2_ref, dogla_ref, dosb_ref, dggate_ref, dsgate_ref, dm_ref,
             mt_ref, ogt_ref, obt_ref, dx2b_ref, dya_ref, dyb_ref,
             dfg_ref, dbg_ref, dglag_ref, loss_ref):
        @pl.when(pl.program_id(0) == 0)
        def _():
            dfg_ref[...] = jnp.zeros_like(dfg_ref)
            dbg_ref[...] = jnp.zeros_like(dbg_ref)
            dglag_ref[...] = jnp.zeros_like(dglag_ref)
            loss_ref[...] = jnp.zeros_like(loss_ref)

        glag = glag_ref[...]
        ggate = ggate_ref[...]
        sg = _sigmoid(ggate)
        silu_g = ggate * sg
        ohat, rinv, nrm = [], [], []
        for hh in range(GLA_HEADS):
            oh = og_ref[:, hh * GLA_HV:(hh + 1) * GLA_HV]
            r = lax.rsqrt(jnp.mean(oh * oh, axis=-1, keepdims=True) + EPS)
            ohat.append(oh * r)
            rinv.append(r)
            nrm.append(ohat[-1] * glag)
        n_all = jnp.concatenate(nrm, axis=1)
        og = n_all * silu_g
        ogb = _bf(og)
        ya = _dot(ogb, wpa_ref[...])
        sgate = sgate_ref[...]
        ss = _sigmoid(sgate)
        silu_s = sgate * ss
        osb = osb_ref[...]
        ob = osb * silu_s
        obb = _bf(ob)
        yb = _dot(obb, wpb_ref[...])
        ga = _sigmoid(ma_ref[...] + bg_ref[:, :D])
        gb = _sigmoid(mb_ref[...] + bg_ref[:, D:])
        merged = ga * ya + gb * yb
        mgb = _bf(merged)
        x2 = x_ref[...] + _dot(mgb, wo_ref[...])
        r2 = lax.rsqrt(jnp.mean(x2 * x2, axis=-1, keepdims=True) + EPS)
        xh2 = x2 * r2
        fg = fg_ref[...]
        err = xh2 * fg - tgt_ref[...]
        loss_ref[...] += jnp.broadcast_to(
            0.5 * jnp.sum(jnp.mean(err * err, axis=-1, keepdims=True), axis=0, keepdims=True), (1, 128))
        dy = err * (1.0 / D)
        dfg_ref[...] += jnp.sum(dy * xh2, axis=0, keepdims=True)
        dxh = dy * fg
        dx2 = r2 * (dxh - xh2 * jnp.mean(dxh * xh2, axis=-1, keepdims=True))
        dx2_ref[...] = dx2
        dx2b = _bf(dx2)
        dx2b_ref[...] = dx2b
        dmerged = _dot_nt(dx2b, wo_ref[...])
        dya = dmerged * ga
        dyb = dmerged * gb
        dma = dmerged * ya * ga * (1.0 - ga)
        dmb = dmerged * yb * gb * (1.0 - gb)
        dm_ref[:, :D] = _bf(dma)
        dm_ref[:, D:] = _bf(dmb)
        dbg_ref[:, :D] += jnp.sum(dma, axis=0, keepdims=True)
        dbg_ref[:, D:] += jnp.sum(dmb, axis=0, keepdims=True)
        dyab = _bf(dya)
        dybb = _bf(dyb)
        dya_ref[...] = dyab
        dyb_ref[...] = dybb
        dog = _dot_nt(dyab, wpa_ref[...])
        dob = _dot_nt(dybb, wpb_ref[...])
        dosb_ref[...] = dob * silu_s
        dsgate_ref[...] = _bf(dob * osb * (ss * (1.0 + sgate * (1.0 - ss))))
        dn = dog * silu_g
        dggate_ref[...] = _bf(dog * n_all * (sg * (1.0 + ggate * (1.0 - sg))))
        dglag = jnp.zeros((1, GLA_HV), F32)
        for hh in range(GLA_HEADS):
            dnh = dn[:, hh * GLA_HV:(hh + 1) * GLA_HV]
            dglag = dglag + jnp.sum(dnh * ohat[hh], axis=0, keepdims=True)
            dohat = dnh * glag
            dogla_ref[:, hh * GLA_HV:(hh + 1) * GLA_HV] = rinv[hh] * (
                dohat - ohat[hh] * jnp.mean(dohat * ohat[hh], axis=-1, keepdims=True))
        dglag_ref[...] += dglag
        mt_ref[...] = _bf(merged.T)
        ogt_ref[...] = _bf(og.T)
        obt_ref[...] = _bf(ob.T)

    row = lambda i: (i, 0)
    const = lambda i: (0, 0)
    tile = pl.BlockSpec((tm, D), row)
    tile_t = pl.BlockSpec((D, tm), lambda i: (0, i))
    wspec = pl.BlockSpec((D, D), const)
    return pl.pallas_call(
        body, name="mid",
        grid=(T // tm,),
        in_specs=[tile,
                  pl.BlockSpec((None, tm, D), lambda i: (2, i, 0)),
                  tile,
                  pl.BlockSpec((None, tm, D), lambda i: (6, i, 0)),
                  pl.BlockSpec((None, tm, D), lambda i: (7, i, 0)),
                  pl.BlockSpec((None, tm, D), lambda i: (8, i, 0)),
                  tile, tile, wspec, wspec, wspec,
                  pl.BlockSpec((1, GLA_HV), const),
                  pl.BlockSpec((1, 2 * D), const),
                  pl.BlockSpec((1, D), const)],
        out_specs=(tile, tile, tile, tile, tile,
                   pl.BlockSpec((tm, 2 * D), row),
                   tile_t, tile_t, tile_t, tile, tile, tile,
                   pl.BlockSpec((1, D), const),
                   pl.BlockSpec((1, 2 * D), const),
                   pl.BlockSpec((1, GLA_HV), const),
                   pl.BlockSpec((1, 128), const)),
        out_shape=(jax.ShapeDtypeStruct((T, D), F32),
                   jax.ShapeDtypeStruct((T, D), F32),
                   jax.ShapeDtypeStruct((T, D), F32),
                   jax.ShapeDtypeStruct((T, D), BF16),
                   jax.ShapeDtypeStruct((T, D), BF16),
                   jax.ShapeDtypeStruct((T, 2 * D), BF16),
                   jax.ShapeDtypeStruct((D, T), BF16),
                   jax.ShapeDtypeStruct((D, T), BF16),
                   jax.ShapeDtypeStruct((D, T), BF16),
                   jax.ShapeDtypeStruct((T, D), BF16),
                   jax.ShapeDtypeStruct((T, D), BF16),
                   jax.ShapeDtypeStruct((T, D), BF16),
                   jax.ShapeDtypeStruct((1, D), F32),
                   jax.ShapeDtypeStruct((1, 2 * D), F32),
                   jax.ShapeDtypeStruct((1, GLA_HV), F32),
                   jax.ShapeDtypeStruct((1, 128), F32)),
        compiler_params=_cparams(("arbitrary",)),
    )(o_gla, proj3, o_sb, proj3, proj3, proj3, x, target, wpa, wpb, wo, gla_g, b_gate, final_g)


def _dh_call(pieces, dmlog, drank, w3, wr, x, dx2, norm_g, ht):
    T, D = x.shape
    tm = min(256, T)
    npc = len(pieces)

    def body(*refs):
        pcs = refs[:npc]
        (dm_ref, dr_ref, w_hbm, wr_ref, x_ref, dx2_ref, g_ref, ht_ref,
         gx_ref, dg_ref, dwr_ref, w_scr, sem) = refs[npc:]

        @pl.when(pl.program_id(0) == 0)
        def _():
            cp = pltpu.make_async_copy(w_hbm, w_scr, sem)
            cp.start()
            cp.wait()
            dg_ref[...] = jnp.zeros_like(dg_ref)
            dwr_ref[...] = jnp.zeros_like(dwr_ref)

        dr = dr_ref[...]
        dh = _dot_nt(dr, wr_ref[...])
        for g in range(npc):
            dh = dh + _dot_nt(pcs[g][...], w_scr[g])
        dh = dh + _dot_nt(dm_ref[:, :D], w_scr[npc])
        dh = dh + _dot_nt(dm_ref[:, D:], w_scr[npc + 1])
        xv = x_ref[...]
        r = lax.rsqrt(jnp.mean(xv * xv, axis=-1, keepdims=True) + EPS)
        xhat = xv * r
        dg_ref[...] += jnp.sum(dh * xhat, axis=0, keepdims=True)
        dxhat = dh * g_ref[...]
        gx_ref[...] = r * (dxhat - xhat * jnp.mean(dxhat * xhat, axis=-1, keepdims=True)) + dx2_ref[...]
        dwr_ref[...] += _dot(ht_ref[...], dr)

    row = lambda i: (i, 0)
    const = lambda i: (0, 0)
    tile = pl.BlockSpec((tm, D), row)
    return pl.pallas_call(
        body, name="dh",
        grid=(T // tm,),
        in_specs=[tile] * npc + [
            pl.BlockSpec((tm, 2 * D), row),
            pl.BlockSpec((tm, 128), row),
            pl.BlockSpec(memory_space=pl.ANY),
            pl.BlockSpec((D, 128), const),
            tile, tile,
            pl.BlockSpec((1, D), const),
            pl.BlockSpec((D, tm), lambda i: (0, i))],
        out_specs=(tile, pl.BlockSpec((1, D), const), pl.BlockSpec((D, 128), const)),
        out_shape=(jax.ShapeDtypeStruct((T, D), F32),
                   jax.ShapeDtypeStruct((1, D), F32),
                   jax.ShapeDtypeStruct((D, 128), F32)),
        scratch_shapes=[pltpu.VMEM((N_GROUPS, D, 1024), BF16), pltpu.SemaphoreType.DMA],
        compiler_params=_cparams(("arbitrary",)),
    )(*pieces, dmlog, drank, w3, wr, x, dx2, norm_g, ht)


def _wgrad_call(lhs_list, lhs_of_group, rhs_list, rhs_of_group):
    n_groups = len(rhs_of_group)
    D, T = lhs_list[0].shape
    tk = min(256, T)
    nk = T // tk
    nl = len(lhs_list)

    def body(*refs):
        lhs = refs[:nl]
        rhs = refs[nl:nl + n_groups]
        out_ref, acc = refs[nl + n_groups:]
        g = pl.program_id(0)
        i = pl.program_id(1)

        @pl.when(i == 0)
        def _():
            acc[...] = jnp.zeros_like(acc)

        for p in range(n_groups):
            @pl.when(g == p)
            def _(p=p):
                acc[...] += _dot(lhs[lhs_of_group[p]][...], rhs[p][...])

        @pl.when(i == nk - 1)
        def _():
            out_ref[...] = _bf(acc[...])

    def lhs_spec(a):
        groups = [g for g in range(n_groups) if lhs_of_group[g] == a]
        lo, hi = min(groups), max(groups)
        assert groups == list(range(lo, hi + 1))
        return pl.BlockSpec((D, tk), lambda g, i: (0, jnp.where((g >= lo) & (g <= hi), i, 0)))

    def rhs_spec(p):
        cb = rhs_of_group[p][1]
        return pl.BlockSpec((tk, 1024), lambda g, i: (jnp.where(g == p, i, 0), cb))

    return pl.pallas_call(
        body, name="wgrad",
        grid=(n_groups, nk),
        in_specs=[lhs_spec(a) for a in range(nl)] + [rhs_spec(p) for p in range(n_groups)],
        out_specs=pl.BlockSpec((None, D, 1024), lambda g, i: (g, 0, 0)),
        out_shape=jax.ShapeDtypeStruct((n_groups, D, 1024), BF16),
        scratch_shapes=[pltpu.VMEM((D, 1024), F32)],
        compiler_params=_cparams(("arbitrary", "arbitrary")),
    )(*lhs_list, *[rhs_list[rhs_of_group[p][0]] for p in range(n_groups)])


def _adamw_call(parts, w, m, v, name):
    R, C = w.shape
    tr = 128 if R % 128 == 0 else R

    def body(p_ref, w_ref, m_ref, v_ref, g_ref, d_ref, nm_ref, nv_ref):
        g = p_ref[0].astype(F32)
        for k in range(1, N_DEV):
            g = g + p_ref[k].astype(F32)
        mm = ADAM_B1 * m_ref[...] + (1.0 - ADAM_B1) * g
        vv = ADAM_B2 * v_ref[...] + (1.0 - ADAM_B2) * (g * g)
        m_hat = mm / (1.0 - ADAM_B1 ** ADAM_STEP)
        v_hat = vv / (1.0 - ADAM_B2 ** ADAM_STEP)
        d_ref[...] = -ADAM_LR * (m_hat / (jnp.sqrt(v_hat) + ADAM_EPS) + ADAM_WD * w_ref[...])
        g_ref[...] = g
        nm_ref[...] = mm
        nv_ref[...] = vv

    blk = pl.BlockSpec((tr, C), lambda i: (i, 0))
    sds = jax.ShapeDtypeStruct((R, C), F32)
    return pl.pallas_call(
        body, name=name,
        grid=(R // tr,),
        in_specs=[pl.BlockSpec((N_DEV, tr, C), lambda i: (0, i, 0)), blk, blk, blk],
        out_specs=(blk, blk, blk, blk),
        out_shape=(sds, sds, sds, sds),
        compiler_params=_cparams(("arbitrary",)),
    )(parts, w, m, v)


def _local_step(x, target, w3, wr, wdec, bdec, wpa, wpb, wo, norm_g, gla_g, b_gate, final_g):
    proj3, rank, ht = _proj_call(x, norm_g, w3, wr)
    o_gla, st_all, la = _gla_fwd_call(proj3, rank, wdec, bdec)
    o_sb = _sb_fwd_call(proj3)
    (dx2, do_gla, do_sb, dggate, dsgate, dmlog, mt, ogt, obt, dx2b, dya, dyb,
     dfinal_g, db_gate, dgla_g, loss) = _mid_call(o_gla, o_sb, proj3, x, target, wpa, wpb, wo,
                                                 gla_g, b_gate, final_g)
    dsq, dsk, dsv = _sb_bwd_call(proj3, o_sb, do_sb)
    dqk, dgv, drank, dwdec, dbdec = _gla_bwd_call(proj3, la, do_gla, st_all, rank, wdec)
    pieces = [dqk, dgv, dggate, dsq, dsk, dsv, dsgate]
    grad_x, dnorm_g, dwr = _dh_call(pieces, dmlog, drank, w3, wr, x, dx2, norm_g, ht)
    rhs_list = pieces + [dmlog, dx2b, dya, dyb]
    rhs_of_group = [(g, 0) for g in range(7)] + [(7, 0), (7, 1), (8, 0), (9, 0), (10, 0)]
    lhs_of_group = [0] * 9 + [1, 2, 3]
    dw_all = _wgrad_call([ht, mt, ogt, obt], lhs_of_group, rhs_list, rhs_of_group)
    return grad_x, dw_all, dwr, dwdec, dbdec, dnorm_g, dgla_g, db_gate, dfinal_g, loss


_SM_NORM = 0
_SM_BDEC = _SM_NORM + D_MODEL
_SM_GLAG = _SM_BDEC + GLA_DK
_SM_BGATE = _SM_GLAG + GLA_HV
_SM_FINAL = _SM_BGATE + 2 * D_MODEL
_SM_REPL = _SM_FINAL + D_MODEL
_SM_LOSS = _SM_REPL
_SM_WDEC = _SM_LOSS + 128
_SM_LEN = _SM_WDEC + GLA_RANK * GLA_DK


def kernel(x, norm_g, w_in, w_dec_up, b_dec, gla_norm_g, w_pa, w_pb, b_gate, w_o, final_g, loss_target, m_norm_g, m_w_in, m_w_dec_up, m_b_dec, m_gla_norm_g, m_w_pa, m_w_pb, m_b_gate, m_w_o, m_final_g, v_norm_g, v_w_in, v_w_dec_up, v_b_dec, v_gla_norm_g, v_w_pa, v_w_pb, v_b_gate, v_w_o, v_final_g):
    D = D_MODEL
    me = 4 * lax.axis_index("x") + 2 * lax.axis_index("y") + lax.axis_index("c")

    wp_shard = jnp.stack([w_pa, w_pb, w_o]).astype(BF16)
    win_all, wp_all, wdec_all = _exchange([w_in.astype(BF16), wp_shard, w_dec_up], False, "gather_w")
    w_full = win_all.transpose(1, 0, 2).reshape(D, IN_COLS)
    w_main = jnp.concatenate([w_full[:, :RANK_COL], w_full[:, RANK_COL + GLA_RANK:]], axis=1)
    w3 = w_main.reshape(D, N_GROUPS, 1024).transpose(1, 0, 2)
    wr = jnp.pad(w_full[:, RANK_COL:RANK_COL + GLA_RANK], ((0, 0), (0, 128 - GLA_RANK)))
    wp_full = wp_all.transpose(1, 0, 2, 3).reshape(3, D, D)
    wdec_full = wdec_all.transpose(1, 0, 2).reshape(GLA_RANK, GLA_DK)
    wdec = jnp.pad(wdec_full, ((0, 128 - GLA_RANK), (0, 0)))

    (grad_x, dw_all, dwr, dwdec, dbdec, dnorm_g, dgla_g, db_gate, dfinal_g, loss) = _local_step(
        x[0], loss_target[0], w3, wr, wdec, b_dec.reshape(1, -1), wp_full[0], wp_full[1], wp_full[2],
        norm_g.reshape(1, -1), gla_norm_g.reshape(1, -1), b_gate.reshape(1, -1), final_g.reshape(1, -1))

    dmain = dw_all[:N_GROUPS].transpose(1, 0, 2).reshape(D, N_GROUPS * 1024)
    dfull = jnp.concatenate([dmain[:, :RANK_COL], dwr[:, :GLA_RANK].astype(BF16), dmain[:, RANK_COL:]], axis=1)
    g_in = dfull.reshape(D, N_DEV, SHARD_COLS).transpose(1, 0, 2)
    g_p = jnp.stack([dw_all[10], dw_all[11], dw_all[9]]).reshape(3, N_DEV, D // N_DEV, D).transpose(1, 0, 2, 3)
    small = jnp.concatenate([
        dnorm_g.reshape(-1), dbdec.reshape(-1), dgla_g.reshape(-1), db_gate.reshape(-1), dfinal_g.reshape(-1),
        loss.reshape(-1), dwdec[:GLA_RANK].reshape(-1)])
    r_in, r_p = _exchange([g_in, g_p], True, "scatter_g")
    (r_small,) = _exchange([small.reshape(1, _SM_LEN)], False, "gather_small")

    gw_in, d_in, nm_in, nv_in = _adamw_call(r_in, w_in, m_w_in, v_w_in, "adamw_in")
    wp_f32 = jnp.concatenate([w_pa, w_pb, w_o], axis=0)
    mp = jnp.concatenate([m_w_pa, m_w_pb, m_w_o], axis=0)
    vp = jnp.concatenate([v_w_pa, v_w_pb, v_w_o], axis=0)
    gp, dp, nmp, nvp = _adamw_call(r_p.reshape(N_DEV, 3 * (D // N_DEV), D), wp_f32, mp, vp, "adamw_p")
    rows = D // N_DEV

    def split3(a):
        return a[:rows], a[rows:2 * rows], a[2 * rows:]

    g_pa, g_pb, g_o = split3(gp)
    d_pa, d_pb, d_o = split3(dp)
    nm_pa, nm_pb, nm_o = split3(nmp)
    nv_pa, nv_pb, nv_o = split3(nvp)

    w_rep = jnp.concatenate([norm_g, b_dec, gla_norm_g, b_gate, final_g]).reshape(1, _SM_REPL)
    m_rep = jnp.concatenate([m_norm_g, m_b_dec, m_gla_norm_g, m_b_gate, m_final_g]).reshape(1, _SM_REPL)
    v_rep = jnp.concatenate([v_norm_g, v_b_dec, v_gla_norm_g, v_b_gate, v_final_g]).reshape(1, _SM_REPL)
    g_rep, d_rep, nm_rep, nv_rep = _adamw_call(r_small[:, :, :_SM_REPL], w_rep, m_rep, v_rep, "adamw_rep")

    def split_rep(a):
        a = a.reshape(-1)
        return (a[_SM_NORM:_SM_BDEC], a[_SM_BDEC:_SM_GLAG], a[_SM_GLAG:_SM_BGATE],
                a[_SM_BGATE:_SM_FINAL], a[_SM_FINAL:_SM_REPL])

    g_norm, g_bdec, g_glag, g_bgate, g_final = split_rep(g_rep)
    d_norm, d_bdec, d_glag, d_bgate, d_final = split_rep(d_rep)
    nm_norm, nm_bdec, nm_glag, nm_bgate, nm_final = split_rep(nm_rep)
    nv_norm, nv_bdec, nv_glag, nv_bgate, nv_final = split_rep(nv_rep)

    wdec_parts = r_small[:, 0, _SM_WDEC:].reshape(N_DEV, GLA_RANK, GLA_DK)
    cols = GLA_DK // N_DEV
    wdec_mine = lax.dynamic_slice_in_dim(wdec_parts, me * cols, cols, axis=2)
    g_wdec, d_wdec, nm_wdec, nv_wdec = _adamw_call(wdec_mine, w_dec_up, m_w_dec_up, v_w_dec_up, "adamw_dec")

    loss_total = jnp.sum(r_small[:, 0, _SM_LOSS])

    return (loss_total, grad_x[None],
            g_norm, gw_in, g_wdec, g_bdec, g_glag, g_pa, g_pb, g_bgate, g_o, g_final,
            d_norm, d_in, d_wdec, d_bdec, d_glag, d_pa, d_pb, d_bgate, d_o, d_final,
            nm_norm, nm_in, nm_wdec, nm_bdec, nm_glag, nm_pa, nm_pb, nm_bgate, nm_o, nm_final,
            nv_norm, nv_in, nv_wdec, nv_bdec, nv_glag, nv_pa, nv_pb, nv_bgate, nv_o, nv_final)
```

```python
import functools
import math

import jax
import jax.numpy as jnp
from jax import lax
from jax.experimental import pallas as pl
from jax.experimental.pallas import tpu as pltpu

F32 = jnp.float32
BF16 = jnp.bfloat16

N_DEV = 8
D_MODEL = 1024
GLA_HEADS = 4
GLA_HK = 128
GLA_HV = 256
GLA_DK = 512
GLA_RANK = 16
GLA_TAU = 16.0
GLA_CHUNK = 64
SB_HEADS = 8
SB_HD = 128
SB_BLOCK = 128
EPS = 1e-6
N_GROUPS = 9
RANK_COL = 3072
IN_COLS = 9232
SHARD_COLS = IN_COLS // N_DEV

ADAM_LR = 0.001
ADAM_B1 = 0.9
ADAM_B2 = 0.999
ADAM_EPS = 1e-08
ADAM_WD = 0.01
ADAM_STEP = 10

VMEM_LIMIT = 56 * 1024 * 1024


def _cparams(sem=None):
    return pltpu.CompilerParams(dimension_semantics=sem, vmem_limit_bytes=VMEM_LIMIT)


def _dot(a, b):
    return jnp.dot(a, b, preferred_element_type=F32)


def _dot_nt(a, b):
    return lax.dot_general(a, b, (((1,), (1,)), ((), ())), preferred_element_type=F32)


def _dot_tn(a, b):
    return lax.dot_general(a, b, (((0,), (0,)), ((), ())), preferred_element_type=F32)


def _bf(x):
    return x.astype(BF16)


def _split3(x):
    hi = x.astype(BF16)
    r = x - hi.astype(F32)
    mid = r.astype(BF16)
    lo = (r - mid.astype(F32)).astype(BF16)
    return hi, mid, lo


def _tri_left(tri, x):
    hi, mid, lo = _split3(x)
    return _dot(tri, hi) + _dot(tri, mid) + _dot(tri, lo)


def _split2(x):
    hi = x.astype(BF16)
    lo = (x - hi.astype(F32)).astype(BF16)
    return hi, lo


def _tri2_left(tri, x):
    hi, lo = _split2(x)
    return _dot(tri, hi) + _dot(tri, lo)


def _tri2_right(x, tri):
    hi, lo = _split2(x)
    return _dot(hi, tri) + _dot(lo, tri)


def _iota2(n, m, dim):
    return lax.broadcasted_iota(jnp.int32, (n, m), dim)


def _sigmoid(x):
    return 1.0 / (1.0 + jnp.exp(-x))


def _softplus_neg_abs(z):
    return jnp.log(1.0 + jnp.exp(-jnp.abs(z)))


def _exchange(arrs, scatter, name):
    n = len(arrs)

    def body(*refs):
        ins = refs[:n]
        outs = refs[n:2 * n]
        send_sems, recv_sems, loc_sems = refs[2 * n:]
        x = lax.axis_index("x")
        y = lax.axis_index("y")
        c = lax.axis_index("c")
        me = 4 * x + 2 * y + c
        local = []
        for a in range(n):
            src = ins[a].at[me] if scatter else ins[a]
            cp = pltpu.make_async_copy(src, outs[a].at[me], loc_sems.at[a])
            cp.start()
            local.append(cp)
        remote = []
        for k in (1, 2, 4, 3, 5, 6, 7):
            px = 1 - x if (k >> 2) & 1 else x
            py = 1 - y if (k >> 1) & 1 else y
            pc = 1 - c if k & 1 else c
            pid = 4 * px + 2 * py + pc
            for a in range(n):
                src = ins[a].at[pid] if scatter else ins[a]
                send = pltpu.make_async_remote_copy(
                    src_ref=src, dst_ref=outs[a].at[me],
                    send_sem=send_sems.at[a, k - 1], recv_sem=recv_sems.at[a, k - 1],
                    device_id=(px, py, pc), device_id_type=pl.DeviceIdType.MESH)
                send.start()
                recv = pltpu.make_async_remote_copy(
                    src_ref=src, dst_ref=outs[a].at[pid],
                    send_sem=send_sems.at[a, k - 1], recv_sem=recv_sems.at[a, k - 1],
                    device_id=(px, py, pc), device_id_type=pl.DeviceIdType.MESH)
                remote.append((send, recv))
        for send, recv in remote:
            recv.wait_recv()
        for send, recv in remote:
            send.wait_send()
        for cp in local:
            cp.wait()

    out_shape = []
    for a in arrs:
        shp = a.shape if scatter else (N_DEV,) + a.shape
        out_shape.append(jax.ShapeDtypeStruct(shp, a.dtype))
    any_spec = pl.BlockSpec(memory_space=pl.ANY)
    return pl.pallas_call(
        body, name=name,
        out_shape=tuple(out_shape),
        in_specs=[any_spec] * n,
        out_specs=tuple([any_spec] * n),
        scratch_shapes=[pltpu.SemaphoreType.DMA((n, N_DEV - 1)),
                        pltpu.SemaphoreType.DMA((n, N_DEV - 1)),
                        pltpu.SemaphoreType.DMA((n,))],
    )(*arrs)


def _proj_call(x, norm_g, w3, wr):
    T, D = x.shape
    tm = min(512, T)

    def f_slot(j):
        return ((j >= 2).astype(jnp.int32) + (j >= 6).astype(jnp.int32)
                + (j >= 7).astype(jnp.int32) + (j >= 8).astype(jnp.int32))

    def b_slot(j):
        return (j >= 3).astype(jnp.int32) + (j >= 4).astype(jnp.int32) + (j >= 5).astype(jnp.int32)

    def body(x_ref, g_ref, w_ref, wr_ref, pf_ref, pb_ref, rank_ref, ht_ref, h_scr):
        j = pl.program_id(1)

        @pl.when(j == 0)
        def _():
            xv = x_ref[...]
            r = lax.rsqrt(jnp.mean(xv * xv, axis=-1, keepdims=True) + EPS)
            h = (xv * r) * g_ref[...]
            hb = _bf(h)
            h_scr[...] = hb
            ht_ref[...] = _bf(h.T)
            rank_ref[...] = _dot(hb, wr_ref[...])

        is_b = (j == 1) | ((j >= 3) & (j <= 5))

        @pl.when(is_b)
        def _():
            pb_ref[...] = _bf(_dot(h_scr[...], w_ref[...]))

        @pl.when(jnp.logical_not(is_b))
        def _():
            pf_ref[...] = _dot(h_scr[...], w_ref[...])

    return pl.pallas_call(
        body, name="proj",
        grid=(T // tm, N_GROUPS),
        in_specs=[pl.BlockSpec((tm, D), lambda i, j: (i, 0)),
                  pl.BlockSpec((1, D), lambda i, j: (0, 0)),
                  pl.BlockSpec((None, D, 1024), lambda i, j: (j, 0, 0)),
                  pl.BlockSpec((D, 128), lambda i, j: (0, 0))],
        out_specs=(pl.BlockSpec((None, tm, 1024), lambda i, j: (f_slot(j), i, 0)),
                   pl.BlockSpec((None, tm, 1024), lambda i, j: (b_slot(j), i, 0)),
                   pl.BlockSpec((tm, 128), lambda i, j: (i, 0)),
                   pl.BlockSpec((D, tm), lambda i, j: (0, i))),
        out_shape=(jax.ShapeDtypeStruct((5, T, 1024), F32),
                   jax.ShapeDtypeStruct((4, T, 1024), BF16),
                   jax.ShapeDtypeStruct((T, 128), F32),
                   jax.ShapeDtypeStruct((D, T), BF16)),
        scratch_shapes=[pltpu.VMEM((tm, D), BF16)],
        compiler_params=_cparams(("arbitrary", "arbitrary")),
    )(x, norm_g, w3, wr)


def _gla_chunk_terms(la_h, q, k):
    C = GLA_CHUNK
    low = _bf((_iota2(C, C, 0) >= _iota2(C, C, 1)).astype(F32))
    b = _tri_left(low, la_h)
    bl = b[C - 1:C, :]
    eb = jnp.exp(b)
    enb = jnp.exp(-b)
    ebl_b = jnp.exp(bl - b)
    scale = GLA_HK ** -0.5
    qe = q * eb * scale
    ke = k * enb
    kd = k * ebl_b
    return b, bl, eb, enb, ebl_b, qe, ke, kd


def _gla_fwd_call(projf, projb, rank, wdec, bdec):
    T = projf.shape[1]
    C = GLA_CHUNK
    n_chunks = T // C

    def body(qk_ref, v_ref, rank_ref, wd_ref, bd_ref, o_ref, st_ref, la_ref, st_scr):
        @pl.when(pl.program_id(0) == 0)
        def _():
            st_scr[...] = jnp.zeros_like(st_scr)

        dec = _dot(_bf(rank_ref[...]), _bf(wd_ref[...])) + bd_ref[...]
        la = (jnp.minimum(dec, 0.0) - _softplus_neg_abs(dec)) / GLA_TAU
        la_ref[...] = la
        mask = _iota2(C, C, 0) >= _iota2(C, C, 1)
        for hh in range(GLA_HEADS):
            la_h = la[:, hh * GLA_HK:(hh + 1) * GLA_HK]
            q = qk_ref[:, hh * GLA_HK:(hh + 1) * GLA_HK]
            k = qk_ref[:, GLA_DK + hh * GLA_HK:GLA_DK + (hh + 1) * GLA_HK]
            v = _bf(v_ref[:, hh * GLA_HV:(hh + 1) * GLA_HV])
            _, bl, _, _, _, qe, ke, kd = _gla_chunk_terms(la_h, q, k)
            st = st_scr[hh]
            st_ref[hh] = st
            p = jnp.where(mask, _dot_nt(_bf(qe), _bf(ke)), 0.0)
            o = _dot(_bf(p), v) + _dot_nt(_bf(qe), _bf(st))
            o_ref[:, hh * GLA_HV:(hh + 1) * GLA_HV] = o
            st_scr[hh] = st * jnp.exp(bl) + _dot_tn(v, _bf(kd))

    return pl.pallas_call(
        body, name="gla_fwd",
        grid=(n_chunks,),
        in_specs=[pl.BlockSpec((None, C, 1024), lambda n: (0, n, 0)),
                  pl.BlockSpec((None, C, 1024), lambda n: (0, n, 0)),
                  pl.BlockSpec((C, 128), lambda n: (n, 0)),
                  pl.BlockSpec((128, GLA_DK), lambda n: (0, 0)),
                  pl.BlockSpec((1, GLA_DK), lambda n: (0, 0))],
        out_specs=(pl.BlockSpec((C, 1024), lambda n: (n, 0)),
                   pl.BlockSpec((None, GLA_HEADS, GLA_HV, GLA_HK), lambda n: (n, 0, 0, 0)),
                   pl.BlockSpec((C, GLA_DK), lambda n: (n, 0))),
        out_shape=(jax.ShapeDtypeStruct((T, 1024), F32),
                   jax.ShapeDtypeStruct((n_chunks, GLA_HEADS, GLA_HV, GLA_HK), F32),
                   jax.ShapeDtypeStruct((T, GLA_DK), F32)),
        scratch_shapes=[pltpu.VMEM((GLA_HEADS, GLA_HV, GLA_HK), F32)],
        compiler_params=_cparams(("arbitrary",)),
    )(projf, projb, rank, wdec, bdec)


def _gla_bwd_call(projf, projb, la, do_gla, st_all, rank, wdec):
    T = projf.shape[1]
    C = GLA_CHUNK
    n_chunks = T // C
    last = n_chunks - 1

    def body(qk_ref, v_ref, la_ref, do_ref, st_ref, rank_ref, wd_ref,
             dqk_ref, dv_ref, drank_ref, dwd_ref, dbd_ref, dst_scr):
        @pl.when(pl.program_id(0) == 0)
        def _():
            dst_scr[...] = jnp.zeros_like(dst_scr)
            dwd_ref[...] = jnp.zeros_like(dwd_ref)
            dbd_ref[...] = jnp.zeros_like(dbd_ref)

        mask = _iota2(C, C, 0) >= _iota2(C, C, 1)
        upp = _bf((_iota2(C, C, 0) <= _iota2(C, C, 1)).astype(F32))
        scale = GLA_HK ** -0.5
        la = la_ref[...]
        ddec_parts = []
        for hh in range(GLA_HEADS):
            la_h = la[:, hh * GLA_HK:(hh + 1) * GLA_HK]
            q = qk_ref[:, hh * GLA_HK:(hh + 1) * GLA_HK]
            k = qk_ref[:, GLA_DK + hh * GLA_HK:GLA_DK + (hh + 1) * GLA_HK]
            v = _bf(v_ref[:, hh * GLA_HV:(hh + 1) * GLA_HV])
            do = _bf(do_ref[:, hh * GLA_HV:(hh + 1) * GLA_HV])
            _, bl, eb, enb, ebl_b, qe, ke, kd = _gla_chunk_terms(la_h, q, k)
            qeb, keb, kdb = _bf(qe), _bf(ke), _bf(kd)
            st = st_ref[hh]
            dstn = dst_scr[hh]
            dstnb = _bf(dstn)
            ebl = jnp.exp(bl)
            p = jnp.where(mask, _dot_nt(qeb, keb), 0.0)
            dp = _bf(jnp.where(mask, _dot_nt(do, v), 0.0))
            dv = _dot_tn(_bf(p), do) + _dot_nt(kdb, dstnb)
            dqe = _dot(dp, keb) + _dot(do, _bf(st))
            dke = _dot_tn(dp, qeb)
            dkd = _dot(v, dstnb)
            dst_scr[hh] = _dot_tn(do, qeb) + dstn * ebl
            debl = jnp.sum(dstn * st, axis=0, keepdims=True)
            dkd_kd = dkd * kd
            db = dqe * qe - dke * ke - dkd_kd
            dbl = jnp.sum(dkd_kd, axis=0, keepdims=True) + ebl * debl
            dla = _tri_left(upp, db) + dbl
            dq = dqe * eb * scale
            dk = dke * enb + dkd * ebl_b
            dqk_ref[:, hh * GLA_HK:(hh + 1) * GLA_HK] = _bf(dq)
            dqk_ref[:, GLA_DK + hh * GLA_HK:GLA_DK + (hh + 1) * GLA_HK] = _bf(dk)
            dv_ref[:, hh * GLA_HV:(hh + 1) * GLA_HV] = _bf(dv)
            ddec_parts.append(dla * (1.0 / GLA_TAU) * (1.0 - jnp.exp(GLA_TAU * la_h)))
        ddec = jnp.concatenate(ddec_parts, axis=1)
        ddecb = _bf(ddec)
        drank_ref[...] = _bf(_dot_nt(ddecb, _bf(wd_ref[...])))
        dwd_ref[...] += _dot_tn(_bf(rank_ref[...]), ddecb)
        dbd_ref[...] += jnp.sum(ddec, axis=0, keepdims=True)

    return pl.pallas_call(
        body, name="gla_bwd",
        grid=(n_chunks,),
        in_specs=[pl.BlockSpec((None, C, 1024), lambda n: (0, last - n, 0)),
                  pl.BlockSpec((None, C, 1024), lambda n: (0, last - n, 0)),
                  pl.BlockSpec((C, GLA_DK), lambda n: (last - n, 0)),
                  pl.BlockSpec((C, 1024), lambda n: (last - n, 0)),
                  pl.BlockSpec((None, GLA_HEADS, GLA_HV, GLA_HK), lambda n: (last - n, 0, 0, 0)),
                  pl.BlockSpec((C, 128), lambda n: (last - n, 0)),
                  pl.BlockSpec((128, GLA_DK), lambda n: (0, 0))],
        out_specs=(pl.BlockSpec((C, 1024), lambda n: (last - n, 0)),
                   pl.BlockSpec((C, 1024), lambda n: (last - n, 0)),
                   pl.BlockSpec((C, 128), lambda n: (last - n, 0)),
                   pl.BlockSpec((128, GLA_DK), lambda n: (0, 0)),
                   pl.BlockSpec((1, GLA_DK), lambda n: (0, 0))),
        out_shape=(jax.ShapeDtypeStruct((T, 1024), BF16),
                   jax.ShapeDtypeStruct((T, 1024), BF16),
                   jax.ShapeDtypeStruct((T, 128), BF16),
                   jax.ShapeDtypeStruct((128, GLA_DK), F32),
                   jax.ShapeDtypeStruct((1, GLA_DK), F32)),
        scratch_shapes=[pltpu.VMEM((GLA_HEADS, GLA_HV, GLA_HK), F32)],
        compiler_params=_cparams(("arbitrary",)),
    )(projf, projb, la, do_gla, st_all, rank, wdec)


def _sb_logs(z):
    sp = _softplus_neg_abs(z)
    return jnp.minimum(z, 0.0) - sp, -jnp.maximum(z, 0.0) - sp


SB_HG_FWD = 8
SB_HG_BWD = 4


def _sb_fwd_call(projb):
    T = projb.shape[1]
    B = SB_BLOCK
    HG = SB_HG_FWD
    W = HG * SB_HD
    scale = 1.0 / math.sqrt(SB_HD)

    def body(q_ref, k_ref, v_ref, o_ref, cb_scr):
        i = pl.program_id(1)
        rows = HG * B
        strict = _iota2(rows, B, 1) < (_iota2(rows, B, 0) & (B - 1))
        after = (_iota2(B, B, 0) > _iota2(B, B, 1)).astype(F32)
        tri = _bf(jnp.concatenate([after, jnp.ones((B, B), F32)], axis=1))
        o_ref[...] = jnp.zeros_like(o_ref)
        cb_scr[...] = jnp.zeros_like(cb_scr)

        def block(j, masked):
            off = pl.multiple_of(j * B, B)
            z = jnp.concatenate(
                [_dot_nt(q_ref[:, hh * SB_HD:(hh + 1) * SB_HD], k_ref[pl.ds(off, B), hh * SB_HD:(hh + 1) * SB_HD])
                 for hh in range(HG)], axis=0) * scale
            lsz, l1m = _sb_logs(z)
            if masked:
                l1m = jnp.where(strict, l1m, 0.0)
            r = _tri2_right(l1m, tri)
            cb = cb_scr[...]
            a = jnp.exp(lsz + cb + r[:, :B])
            if masked:
                a = jnp.where(strict, a, 0.0)
            cb_scr[...] = cb + r[:, B:]
            ab = _bf(a)
            for hh in range(HG):
                cs = slice(hh * SB_HD, (hh + 1) * SB_HD)
                o_ref[:, cs] += _dot(ab[hh * B:(hh + 1) * B, :], v_ref[pl.ds(off, B), cs])

        block(i, True)

        def step(jj, c):
            block(i - jj, False)
            return c

        lax.fori_loop(1, i + 1, step, 0)

    return pl.pallas_call(
        body, name="sb_fwd",
        grid=(SB_HEADS // HG, T // B),
        in_specs=[pl.BlockSpec((None, B, W), lambda h, i: (1, i, h)),
                  pl.BlockSpec((None, T, W), lambda h, i: (2, 0, h)),
                  pl.BlockSpec((None, T, W), lambda h, i: (3, 0, h))],
        out_specs=pl.BlockSpec((B, W), lambda h, i: (i, h)),
        out_shape=jax.ShapeDtypeStruct((T, 1024), F32),
        scratch_shapes=[pltpu.VMEM((HG * B, B), F32)],
        compiler_params=_cparams(("arbitrary", "arbitrary")),
    )(projb, projb, projb)


def _sb_bwd_call(projb, do_sb):
    T = projb.shape[1]
    B = SB_BLOCK
    nb = T // B
    HG = SB_HG_BWD
    W = HG * SB_HD
    scale = 1.0 / math.sqrt(SB_HD)

    def body(q_ref, k_ref, v_ref, do_ref, dq_ref, dk_ref, dv_ref,
             dk_scr, dv_scr, kt_scr, beta_scr, g_scr, dqt_scr):
        i = pl.program_id(1)

        @pl.when(i == 0)
        def _():
            dk_scr[...] = jnp.zeros_like(dk_scr)
            dv_scr[...] = jnp.zeros_like(dv_scr)
            for hh in range(HG):
                for jb in range(nb):
                    kt_scr[hh, jb] = _bf(
                        k_ref[jb * B:(jb + 1) * B, hh * SB_HD:(hh + 1) * SB_HD].astype(F32).T)

        dqt_scr[...] = jnp.zeros_like(dqt_scr)
        strict = _iota2(B, W, 0) < (_iota2(B, W, 1) & (B - 1))
        later = _bf((_iota2(B, B, 1) > _iota2(B, B, 0)).astype(F32))
        earlier = _bf((_iota2(B, B, 1) < _iota2(B, B, 0)).astype(F32))
        dob = _bf(do_ref[...])

        def heads(fn):
            return [fn(slice(hh * SB_HD, (hh + 1) * SB_HD)) for hh in range(HG)]

        def pass1(j, cb, masked):
            off = pl.multiple_of(j * B, B)
            z = jnp.concatenate(heads(lambda cs: _dot_nt(k_ref[pl.ds(off, B), cs], q_ref[:, cs])), axis=1) * scale
            da = jnp.concatenate(heads(lambda cs: _dot_nt(v_ref[pl.ds(off, B), cs], dob[:, cs])), axis=1)
            lsz, l1m = _sb_logs(z)
            if masked:
                l1m = jnp.where(strict, l1m, 0.0)
            a = jnp.exp(lsz + cb + _tri2_left(later, l1m))
            if masked:
                a = jnp.where(strict, a, 0.0)
            g_scr[j] = a * da
            beta_scr[j] = jnp.exp(lsz)
            ab = _bf(a)
            for hh in range(HG):
                cs = slice(hh * SB_HD, (hh + 1) * SB_HD)
                dv_scr[pl.ds(off, B), cs] += _dot(ab[:, cs], dob[:, cs])
            return cb + jnp.sum(l1m, axis=0, keepdims=True)

        zero = jnp.zeros((1, W), F32)
        cb = pass1(i, zero, True)
        lax.fori_loop(1, i + 1, lambda jj, cr: pass1(i - jj, cr, False), cb)

        def pass2(j, cg, masked):
            off = pl.multiple_of(j * B, B)
            g = g_scr[j]
            beta = beta_scr[j]
            dz = g * (1.0 - beta) - beta * (cg + _tri2_left(earlier, g))
            if masked:
                dz = jnp.where(strict, dz, 0.0)
            dzb = _bf(dz * scale)
            for hh in range(HG):
                cs = slice(hh * SB_HD, (hh + 1) * SB_HD)
                dk_scr[pl.ds(off, B), cs] += _dot(dzb[:, cs], q_ref[:, cs])
                dqt_scr[hh] += _dot(kt_scr[hh, j], dzb[:, cs])
            return cg + jnp.sum(g, axis=0, keepdims=True)

        cg = lax.fori_loop(0, i, lambda j, cr: pass2(j, cr, False), zero)
        pass2(i, cg, True)
        for hh in range(HG):
            dq_ref[:, hh * SB_HD:(hh + 1) * SB_HD] = _bf(dqt_scr[hh].T)

        @pl.when(i == nb - 1)
        def _():
            dk_ref[...] = _bf(dk_scr[...])
            dv_ref[...] = _bf(dv_scr[...])

    return pl.pallas_call(
        body, name="sb_bwd",
        grid=(SB_HEADS // HG, nb),
        in_specs=[pl.BlockSpec((None, B, W), lambda h, i: (1, i, h)),
                  pl.BlockSpec((None, T, W), lambda h, i: (2, 0, h)),
                  pl.BlockSpec((None, T, W), lambda h, i: (3, 0, h)),
                  pl.BlockSpec((B, W), lambda h, i: (i, h))],
        out_specs=(pl.BlockSpec((B, W), lambda h, i: (i, h)),
                   pl.BlockSpec((T, W), lambda h, i: (0, h)),
                   pl.BlockSpec((T, W), lambda h, i: (0, h))),
        out_shape=(jax.ShapeDtypeStruct((T, 1024), BF16),
                   jax.ShapeDtypeStruct((T, 1024), BF16),
                   jax.ShapeDtypeStruct((T, 1024), BF16)),
        scratch_shapes=[pltpu.VMEM((T, W), F32), pltpu.VMEM((T, W), F32),
                        pltpu.VMEM((HG, nb, SB_HD, B), BF16),
                        pltpu.VMEM((nb, B, W), F32), pltpu.VMEM((nb, B, W), F32),
                        pltpu.VMEM((HG, SB_HD, B), F32)],
        compiler_params=_cparams(("arbitrary", "arbitrary")),
    )(projb, projb, projb, do_sb)


def _mid_call(o_gla, o_sb, projf, x, target, wpa, wpb, wo, gla_g, b_gate, final_g):
    T, D = x.shape
    tm = min(256, T)

    def body(og_ref, ggate_ref, osb_ref, sgate_ref, ma_ref, mb_ref, x_ref, tgt_ref,
             wpa_ref, wpb_ref, wo_ref, glag_ref, bg_ref, fg_ref,
             dx2_ref, dogla_ref, dosb_ref, dggate_ref, dsgate_ref, dm_ref,
             mt_ref, ogt_ref, obt_ref, dx2b_ref, dya_ref, dyb_ref,
             dfg_ref, dbg_ref, dglag_ref, loss_ref):
        @pl.when(pl.program_id(0) == 0)
        def _():
            dfg_ref[...] = jnp.zeros_like(dfg_ref)
            dbg_ref[...] = jnp.zeros_like(dbg_ref)
            dglag_ref[...] = jnp.zeros_like(dglag_ref)
            loss_ref[...] = jnp.zeros_like(loss_ref)

        glag = glag_ref[...]
        ggate = ggate_ref[...]
        sg = _sigmoid(ggate)
        silu_g = ggate * sg
        ohat, rinv, nrm = [], [], []
        for hh in range(GLA_HEADS):
            oh = og_ref[:, hh * GLA_HV:(hh + 1) * GLA_HV]
            r = lax.rsqrt(jnp.mean(oh * oh, axis=-1, keepdims=True) + EPS)
            ohat.append(oh * r)
            rinv.append(r)
            nrm.append(ohat[-1] * glag)
        n_all = jnp.concatenate(nrm, axis=1)
        og = n_all * silu_g
        ogb = _bf(og)
        ya = _dot(ogb, wpa_ref[...])
        sgate = sgate_ref[...]
        ss = _sigmoid(sgate)
        silu_s = sgate * ss
        osb = osb_ref[...]
        ob = osb * silu_s
        obb = _bf(ob)
        yb = _dot(obb, wpb_ref[...])
        ga = _sigmoid(ma_ref[...] + bg_ref[:, :D])
        gb = _sigmoid(mb_ref[...] + bg_ref[:, D:])
        merged = ga * ya + gb * yb
        mgb = _bf(merged)
        x2 = x_ref[...] + _dot(mgb, wo_ref[...])
        r2 = lax.rsqrt(jnp.mean(x2 * x2, axis=-1, keepdims=True) + EPS)
        xh2 = x2 * r2
        fg = fg_ref[...]
        err = xh2 * fg - tgt_ref[...]
        loss_ref[...] += jnp.broadcast_to(
            0.5 * jnp.sum(jnp.mean(err * err, axis=-1, keepdims=True), axis=0, keepdims=True), (1, 128))
        dy = err * (1.0 / D)
        dfg_ref[...] += jnp.sum(dy * xh2, axis=0, keepdims=True)
        dxh = dy * fg
        dx2 = r2 * (dxh - xh2 * jnp.mean(dxh * xh2, axis=-1, keepdims=True))
        dx2_ref[...] = dx2
        dx2b = _bf(dx2)
        dx2b_ref[...] = dx2b
        dmerged = _dot_nt(dx2b, wo_ref[...])
        dya = dmerged * ga
        dyb = dmerged * gb
        dma = dmerged * ya * ga * (1.0 - ga)
        dmb = dmerged * yb * gb * (1.0 - gb)
        dm_ref[:, :D] = _bf(dma)
        dm_ref[:, D:] = _bf(dmb)
        dbg_ref[:, :D] += jnp.sum(dma, axis=0, keepdims=True)
        dbg_ref[:, D:] += jnp.sum(dmb, axis=0, keepdims=True)
        dyab = _bf(dya)
        dybb = _bf(dyb)
        dya_ref[...] = dyab
        dyb_ref[...] = dybb
        dog = _dot_nt(dyab, wpa_ref[...])
        dob = _dot_nt(dybb, wpb_ref[...])
        dosb_ref[...] = dob * silu_s
        dsgate_ref[...] = _bf(dob * osb * (ss * (1.0 + sgate * (1.0 - ss))))
        dn = dog * silu_g
        dggate_ref[...] = _bf(dog * n_all * (sg * (1.0 + ggate * (1.0 - sg))))
        dglag = jnp.zeros((1, GLA_HV), F32)
        for hh in range(GLA_HEADS):
            dnh = dn[:, hh * GLA_HV:(hh + 1) * GLA_HV]
            dglag = dglag + jnp.sum(dnh * ohat[hh], axis=0, keepdims=True)
            dohat = dnh * glag
            dogla_ref[:, hh * GLA_HV:(hh + 1) * GLA_HV] = rinv[hh] * (
                dohat - ohat[hh] * jnp.mean(dohat * ohat[hh], axis=-1, keepdims=True))
        dglag_ref[...] += dglag
        mt_ref[...] = _bf(merged.T)
        ogt_ref[...] = _bf(og.T)
        obt_ref[...] = _bf(ob.T)

    row = lambda i: (i, 0)
    const = lambda i: (0, 0)
    tile = pl.BlockSpec((tm, D), row)
    tile_t = pl.BlockSpec((D, tm), lambda i: (0, i))
    wspec = pl.BlockSpec((D, D), const)
    return pl.pallas_call(
        body, name="mid",
        grid=(T // tm,),
        in_specs=[tile,
                  pl.BlockSpec((None, tm, D), lambda i: (1, i, 0)),
                  tile,
                  pl.BlockSpec((None, tm, D), lambda i: (2, i, 0)),
                  pl.BlockSpec((None, tm, D), lambda i: (3, i, 0)),
                  pl.BlockSpec((None, tm, D), lambda i: (4, i, 0)),
                  tile, tile, wspec, wspec, wspec,
                  pl.BlockSpec((1, GLA_HV), const),
                  pl.BlockSpec((1, 2 * D), const),
                  pl.BlockSpec((1, D), const)],
        out_specs=(tile, tile, tile, tile, tile,
                   pl.BlockSpec((tm, 2 * D), row),
                   tile_t, tile_t, tile_t, tile, tile, tile,
                   pl.BlockSpec((1, D), const),
                   pl.BlockSpec((1, 2 * D), const),
                   pl.BlockSpec((1, GLA_HV), const),
                   pl.BlockSpec((1, 128), const)),
        out_shape=(jax.ShapeDtypeStruct((T, D), F32),
                   jax.ShapeDtypeStruct((T, D), F32),
                   jax.ShapeDtypeStruct((T, D), F32),
                   jax.ShapeDtypeStruct((T, D), BF16),
                   jax.ShapeDtypeStruct((T, D), BF16),
                   jax.ShapeDtypeStruct((T, 2 * D), BF16),
                   jax.ShapeDtypeStruct((D, T), BF16),
                   jax.ShapeDtypeStruct((D, T), BF16),
                   jax.ShapeDtypeStruct((D, T), BF16),
                   jax.ShapeDtypeStruct((T, D), BF16),
                   jax.ShapeDtypeStruct((T, D), BF16),
                   jax.ShapeDtypeStruct((T, D), BF16),
                   jax.ShapeDtypeStruct((1, D), F32),
                   jax.ShapeDtypeStruct((1, 2 * D), F32),
                   jax.ShapeDtypeStruct((1, GLA_HV), F32),
                   jax.ShapeDtypeStruct((1, 128), F32)),
        compiler_params=_cparams(("arbitrary",)),
    )(o_gla, projf, o_sb, projf, projf, projf, x, target, wpa, wpb, wo, gla_g, b_gate, final_g)


def _dh_call(pieces, dmlog, drank, w3, wr, x, dx2, norm_g, ht):
    T, D = x.shape
    tm = min(256, T)
    npc = len(pieces)

    def body(*refs):
        pcs = refs[:npc]
        (dm_ref, dr_ref, w_hbm, wr_ref, x_ref, dx2_ref, g_ref, ht_ref,
         gx_ref, dg_ref, dwr_ref, w_scr, sem) = refs[npc:]

        @pl.when(pl.program_id(0) == 0)
        def _():
            cp = pltpu.make_async_copy(w_hbm, w_scr, sem)
            cp.start()
            cp.wait()
            dg_ref[...] = jnp.zeros_like(dg_ref)
            dwr_ref[...] = jnp.zeros_like(dwr_ref)

        dr = dr_ref[...]
        dh = _dot_nt(dr, wr_ref[...])
        for g in range(npc):
            dh = dh + _dot_nt(pcs[g][...], w_scr[g])
        dh = dh + _dot_nt(dm_ref[:, :D], w_scr[npc])
        dh = dh + _dot_nt(dm_ref[:, D:], w_scr[npc + 1])
        xv = x_ref[...]
        r = lax.rsqrt(jnp.mean(xv * xv, axis=-1, keepdims=True) + EPS)
        xhat = xv * r
        dg_ref[...] += jnp.sum(dh * xhat, axis=0, keepdims=True)
        dxhat = dh * g_ref[...]
        gx_ref[...] = r * (dxhat - xhat * jnp.mean(dxhat * xhat, axis=-1, keepdims=True)) + dx2_ref[...]
        dwr_ref[...] += _dot(ht_ref[...], dr)

    row = lambda i: (i, 0)
    const = lambda i: (0, 0)
    tile = pl.BlockSpec((tm, D), row)
    return pl.pallas_call(
        body, name="dh",
        grid=(T // tm,),
        in_specs=[tile] * npc + [
            pl.BlockSpec((tm, 2 * D), row),
            pl.BlockSpec((tm, 128), row),
            pl.BlockSpec(memory_space=pl.ANY),
            pl.BlockSpec((D, 128), const),
            tile, tile,
            pl.BlockSpec((1, D), const),
            pl.BlockSpec((D, tm), lambda i: (0, i))],
        out_specs=(tile, pl.BlockSpec((1, D), const), pl.BlockSpec((D, 128), const)),
        out_shape=(jax.ShapeDtypeStruct((T, D), F32),
                   jax.ShapeDtypeStruct((1, D), F32),
                   jax.ShapeDtypeStruct((D, 128), F32)),
        scratch_shapes=[pltpu.VMEM((N_GROUPS, D, 1024), BF16), pltpu.SemaphoreType.DMA],
        compiler_params=_cparams(("arbitrary",)),
    )(*pieces, dmlog, drank, w3, wr, x, dx2, norm_g, ht)


def _wgrad_call(lhs_list, lhs_of_group, rhs_list, rhs_of_group):
    n_groups = len(rhs_of_group)
    D, T = lhs_list[0].shape
    tk = min(256, T)
    nk = T // tk
    nl = len(lhs_list)

    def body(*refs):
        lhs = refs[:nl]
        rhs = refs[nl:nl + n_groups]
        out_ref, acc = refs[nl + n_groups:]
        g = pl.program_id(0)
        i = pl.program_id(1)

        @pl.when(i == 0)
        def _():
            acc[...] = jnp.zeros_like(acc)

        for p in range(n_groups):
            @pl.when(g == p)
            def _(p=p):
                acc[...] += _dot(lhs[lhs_of_group[p]][...], rhs[p][...])

        @pl.when(i == nk - 1)
        def _():
            out_ref[...] = _bf(acc[...])

    def lhs_spec(a):
        groups = [g for g in range(n_groups) if lhs_of_group[g] == a]
        lo, hi = min(groups), max(groups)
        assert groups == list(range(lo, hi + 1))
        return pl.BlockSpec((D, tk), lambda g, i: (0, jnp.where((g >= lo) & (g <= hi), i, 0)))

    def rhs_spec(p):
        cb = rhs_of_group[p][1]
        return pl.BlockSpec((tk, 1024), lambda g, i: (jnp.where(g == p, i, 0), cb))

    return pl.pallas_call(
        body, name="wgrad",
        grid=(n_groups, nk),
        in_specs=[lhs_spec(a) for a in range(nl)] + [rhs_spec(p) for p in range(n_groups)],
        out_specs=pl.BlockSpec((None, D, 1024), lambda g, i: (g, 0, 0)),
        out_shape=jax.ShapeDtypeStruct((n_groups, D, 1024), BF16),
        scratch_shapes=[pltpu.VMEM((D, 1024), F32)],
        compiler_params=_cparams(("arbitrary", "arbitrary")),
    )(*lhs_list, *[rhs_list[rhs_of_group[p][0]] for p in range(n_groups)])


def _adamw_call(parts, w, m, v, name):
    R, C = w.shape
    tr = 128 if R % 128 == 0 else R

    def body(p_ref, w_ref, m_ref, v_ref, g_ref, d_ref, nm_ref, nv_ref):
        g = p_ref[0].astype(F32)
        for k in range(1, N_DEV):
            g = g + p_ref[k].astype(F32)
        mm = ADAM_B1 * m_ref[...] + (1.0 - ADAM_B1) * g
        vv = ADAM_B2 * v_ref[...] + (1.0 - ADAM_B2) * (g * g)
        m_hat = mm / (1.0 - ADAM_B1 ** ADAM_STEP)
        v_hat = vv / (1.0 - ADAM_B2 ** ADAM_STEP)
        d_ref[...] = -ADAM_LR * (m_hat / (jnp.sqrt(v_hat) + ADAM_EPS) + ADAM_WD * w_ref[...])
        g_ref[...] = g
        nm_ref[...] = mm
        nv_ref[...] = vv

    blk = pl.BlockSpec((tr, C), lambda i: (i, 0))
    sds = jax.ShapeDtypeStruct((R, C), F32)
    return pl.pallas_call(
        body, name=name,
        grid=(R // tr,),
        in_specs=[pl.BlockSpec((N_DEV, tr, C), lambda i: (0, i, 0)), blk, blk, blk],
        out_specs=(blk, blk, blk, blk),
        out_shape=(sds, sds, sds, sds),
        compiler_params=_cparams(("arbitrary",)),
    )(parts, w, m, v)


def _local_step(x, target, w3, wr, wdec, bdec, wpa, wpb, wo, norm_g, gla_g, b_gate, final_g):
    projf, projb, rank, ht = _proj_call(x, norm_g, w3, wr)
    o_gla, st_all, la = _gla_fwd_call(projf, projb, rank, wdec, bdec)
    o_sb = _sb_fwd_call(projb)
    (dx2, do_gla, do_sb, dggate, dsgate, dmlog, mt, ogt, obt, dx2b, dya, dyb,
     dfinal_g, db_gate, dgla_g, loss) = _mid_call(o_gla, o_sb, projf, x, target, wpa, wpb, wo,
                                                 gla_g, b_gate, final_g)
    dsq, dsk, dsv = _sb_bwd_call(projb, do_sb)
    dqk, dgv, drank, dwdec, dbdec = _gla_bwd_call(projf, projb, la, do_gla, st_all, rank, wdec)
    pieces = [dqk, dgv, dggate, dsq, dsk, dsv, dsgate]
    grad_x, dnorm_g, dwr = _dh_call(pieces, dmlog, drank, w3, wr, x, dx2, norm_g, ht)
    rhs_list = pieces + [dmlog, dx2b, dya, dyb]
    rhs_of_group = [(g, 0) for g in range(7)] + [(7, 0), (7, 1), (8, 0), (9, 0), (10, 0)]
    lhs_of_group = [0] * 9 + [1, 2, 3]
    dw_all = _wgrad_call([ht, mt, ogt, obt], lhs_of_group, rhs_list, rhs_of_group)
    return grad_x, dw_all, dwr, dwdec, dbdec, dnorm_g, dgla_g, db_gate, dfinal_g, loss


_SM_NORM = 0
_SM_BDEC = _SM_NORM + D_MODEL
_SM_GLAG = _SM_BDEC + GLA_DK
_SM_BGATE = _SM_GLAG + GLA_HV
_SM_FINAL = _SM_BGATE + 2 * D_MODEL
_SM_REPL = _SM_FINAL + D_MODEL
_SM_LOSS = _SM_REPL
_SM_WDEC = _SM_LOSS + 128
_SM_LEN = _SM_WDEC + GLA_RANK * GLA_DK


def kernel(x, norm_g, w_in, w_dec_up, b_dec, gla_norm_g, w_pa, w_pb, b_gate, w_o, final_g, loss_target, m_norm_g, m_w_in, m_w_dec_up, m_b_dec, m_gla_norm_g, m_w_pa, m_w_pb, m_b_gate, m_w_o, m_final_g, v_norm_g, v_w_in, v_w_dec_up, v_b_dec, v_gla_norm_g, v_w_pa, v_w_pb, v_b_gate, v_w_o, v_final_g):
    D = D_MODEL
    me = 4 * lax.axis_index("x") + 2 * lax.axis_index("y") + lax.axis_index("c")

    wp_shard = jnp.stack([w_pa, w_pb, w_o]).astype(BF16)
    win_all, wp_all, wdec_all = _exchange([w_in.astype(BF16), wp_shard, w_dec_up], False, "gather_w")
    w_full = win_all.transpose(1, 0, 2).reshape(D, IN_COLS)
    w_main = jnp.concatenate([w_full[:, :RANK_COL], w_full[:, RANK_COL + GLA_RANK:]], axis=1)
    w3 = w_main.reshape(D, N_GROUPS, 1024).transpose(1, 0, 2)
    wr = jnp.pad(w_full[:, RANK_COL:RANK_COL + GLA_RANK], ((0, 0), (0, 128 - GLA_RANK)))
    wp_full = wp_all.transpose(1, 0, 2, 3).reshape(3, D, D)
    wdec_full = wdec_all.transpose(1, 0, 2).reshape(GLA_RANK, GLA_DK)
    wdec = jnp.pad(wdec_full, ((0, 128 - GLA_RANK), (0, 0)))

    (grad_x, dw_all, dwr, dwdec, dbdec, dnorm_g, dgla_g, db_gate, dfinal_g, loss) = _local_step(
        x[0], loss_target[0], w3, wr, wdec, b_dec.reshape(1, -1), wp_full[0], wp_full[1], wp_full[2],
        norm_g.reshape(1, -1), gla_norm_g.reshape(1, -1), b_gate.reshape(1, -1), final_g.reshape(1, -1))

    dmain = dw_all[:N_GROUPS].transpose(1, 0, 2).reshape(D, N_GROUPS * 1024)
    dfull = jnp.concatenate([dmain[:, :RANK_COL], dwr[:, :GLA_RANK].astype(BF16), dmain[:, RANK_COL:]], axis=1)
    g_in = dfull.reshape(D, N_DEV, SHARD_COLS).transpose(1, 0, 2)
    g_p = jnp.stack([dw_all[10], dw_all[11], dw_all[9]]).reshape(3, N_DEV, D // N_DEV, D).transpose(1, 0, 2, 3)
    small = jnp.concatenate([
        dnorm_g.reshape(-1), dbdec.reshape(-1), dgla_g.reshape(-1), db_gate.reshape(-1), dfinal_g.reshape(-1),
        loss.reshape(-1), dwdec[:GLA_RANK].reshape(-1)])
    r_in, r_p = _exchange([g_in, g_p], True, "scatter_g")
    (r_small,) = _exchange([small.reshape(1, _SM_LEN)], False, "gather_small")

    gw_in, d_in, nm_in, nv_in = _adamw_call(r_in, w_in, m_w_in, v_w_in, "adamw_in")
    wp_f32 = jnp.concatenate([w_pa, w_pb, w_o], axis=0)
    mp = jnp.concatenate([m_w_pa, m_w_pb, m_w_o], axis=0)
    vp = jnp.concatenate([v_w_pa, v_w_pb, v_w_o], axis=0)
    gp, dp, nmp, nvp = _adamw_call(r_p.reshape(N_DEV, 3 * (D // N_DEV), D), wp_f32, mp, vp, "adamw_p")
    rows = D // N_DEV

    def split3(a):
        return a[:rows], a[rows:2 * rows], a[2 * rows:]

    g_pa, g_pb, g_o = split3(gp)
    d_pa, d_pb, d_o = split3(dp)
    nm_pa, nm_pb, nm_o = split3(nmp)
    nv_pa, nv_pb, nv_o = split3(nvp)

    w_rep = jnp.concatenate([norm_g, b_dec, gla_norm_g, b_gate, final_g]).reshape(1, _SM_REPL)
    m_rep = jnp.concatenate([m_norm_g, m_b_dec, m_gla_norm_g, m_b_gate, m_final_g]).reshape(1, _SM_REPL)
    v_rep = jnp.concatenate([v_norm_g, v_b_dec, v_gla_norm_g, v_b_gate, v_final_g]).reshape(1, _SM_REPL)
    g_rep, d_rep, nm_rep, nv_rep = _adamw_call(r_small[:, :, :_SM_REPL], w_rep, m_rep, v_rep, "adamw_rep")

    def split_rep(a):
        a = a.reshape(-1)
        return (a[_SM_NORM:_SM_BDEC], a[_SM_BDEC:_SM_GLAG], a[_SM_GLAG:_SM_BGATE],
                a[_SM_BGATE:_SM_FINAL], a[_SM_FINAL:_SM_REPL])

    g_norm, g_bdec, g_glag, g_bgate, g_final = split_rep(g_rep)
    d_norm, d_bdec, d_glag, d_bgate, d_final = split_rep(d_rep)
    nm_norm, nm_bdec, nm_glag, nm_bgate, nm_final = split_rep(nm_rep)
    nv_norm, nv_bdec, nv_glag, nv_bgate, nv_final = split_rep(nv_rep)

    wdec_parts = r_small[:, 0, _SM_WDEC:].reshape(N_DEV, GLA_RANK, GLA_DK)
    cols = GLA_DK // N_DEV
    wdec_mine = lax.dynamic_slice_in_dim(wdec_parts, me * cols, cols, axis=2)
    g_wdec, d_wdec, nm_wdec, nv_wdec = _adamw_call(wdec_mine, w_dec_up, m_w_dec_up, v_w_dec_up, "adamw_dec")

    loss_total = jnp.sum(r_small[:, 0, _SM_LOSS])

    return (loss_total, grad_x[None],
            g_norm, gw_in, g_wdec, g_bdec, g_glag, g_pa, g_pb, g_bgate, g_o, g_final,
            d_norm, d_in, d_wdec, d_bdec, d_glag, d_pa, d_pb, d_bgate, d_o, d_final,
            nm_norm, nm_in, nm_wdec, nm_bdec, nm_glag, nm_pa, nm_pb, nm_bgate, nm_o, nm_final,
            nv_norm, nv_in, nv_wdec, nv_bdec, nv_glag, nv_pa, nv_pb, nv_bgate, nv_o, nv_final)
```

```python
import functools
import math

import jax
import jax.numpy as jnp
from jax import lax
from jax.experimental import pallas as pl
from jax.experimental.pallas import tpu as pltpu

F32 = jnp.float32
BF16 = jnp.bfloat16

N_DEV = 8
D_MODEL = 1024
GLA_HEADS = 4
GLA_HK = 128
GLA_HV = 256
GLA_DK = 512
GLA_RANK = 16
GLA_TAU = 16.0
GLA_CHUNK = 64
SB_HEADS = 8
SB_HD = 128
SB_BLOCK = 128
EPS = 1e-6
N_GROUPS = 9
RANK_COL = 3072
IN_COLS = 9232
SHARD_COLS = IN_COLS // N_DEV

ADAM_LR = 0.001
ADAM_B1 = 0.9
ADAM_B2 = 0.999
ADAM_EPS = 1e-08
ADAM_WD = 0.01
ADAM_STEP = 10

VMEM_LIMIT = 56 * 1024 * 1024


def _cparams(sem=None):
    return pltpu.CompilerParams(dimension_semantics=sem, vmem_limit_bytes=VMEM_LIMIT)


def _tiling_2d(rows, cols):
    if rows * cols <= 128 * 1024:
        return (rows, cols), (1,), lambda i: (0, 0)
    if rows % 128 == 0:
        return (128, cols), (rows // 128,), lambda i: (i, 0)
    tc = 256 if cols % 256 == 0 else cols
    return (rows, tc), (cols // tc,), lambda i: (0, i)


def _dot(a, b):
    return jnp.dot(a, b, preferred_element_type=F32)


def _dot_nt(a, b):
    return lax.dot_general(a, b, (((1,), (1,)), ((), ())), preferred_element_type=F32)


def _dot_tn(a, b):
    return lax.dot_general(a, b, (((0,), (0,)), ((), ())), preferred_element_type=F32)


def _bf(x):
    return x.astype(BF16)


def _split3(x):
    hi = x.astype(BF16)
    r = x - hi.astype(F32)
    mid = r.astype(BF16)
    lo = (r - mid.astype(F32)).astype(BF16)
    return hi, mid, lo


def _tri_left(tri, x):
    hi, mid, lo = _split3(x)
    return _dot(tri, hi) + _dot(tri, mid) + _dot(tri, lo)


def _split2(x):
    hi = x.astype(BF16)
    lo = (x - hi.astype(F32)).astype(BF16)
    return hi, lo


def _tri2_left(tri, x):
    hi, lo = _split2(x)
    return _dot(tri, hi) + _dot(tri, lo)


def _tri2_right(x, tri):
    hi, lo = _split2(x)
    return _dot(hi, tri) + _dot(lo, tri)


def _iota2(n, m, dim):
    return lax.broadcasted_iota(jnp.int32, (n, m), dim)


def _sigmoid(x):
    return 1.0 / (1.0 + jnp.exp(-x))


def _softplus_neg_abs(z):
    return jnp.log(1.0 + jnp.exp(-jnp.abs(z)))


_ANY = pl.BlockSpec(memory_space=pl.ANY)


def _mesh_pos():
    return lax.axis_index("x"), lax.axis_index("y"), lax.axis_index("c")


def _other_chips(x, y):
    return [(1 - x, y), (x, 1 - y), (1 - x, 1 - y)]


def _rcopy(src, dst, send_sem, recv_sem, to):
    return pltpu.make_async_remote_copy(src_ref=src, dst_ref=dst, send_sem=send_sem, recv_sem=recv_sem,
                                        device_id=to, device_id_type=pl.DeviceIdType.MESH)


def _all_gather(arrs, name):
    n = len(arrs)

    def body(*refs):
        ins = refs[:n]
        outs = refs[n:2 * n]
        send_sems, recv_sems, loc_sems = refs[2 * n:]
        x, y, c = _mesh_pos()
        sib = (x, y, 1 - c)
        chips = _other_chips(x, y)

        def slot(px, py, pc):
            return 4 * px + 2 * py + pc

        def copy(a, k, block, to, src=None):
            dst = outs[a].at[slot(*block)]
            return _rcopy(dst if src is None else src, dst, send_sems.at[a, k], recv_sems.at[a, k], to)

        mine = [pltpu.make_async_copy(ins[a], outs[a].at[slot(x, y, c)], loc_sems.at[a]) for a in range(n)]
        for cp in mine:
            cp.start()
        first = [copy(a, 0, (x, y, c), sib, src=ins[a]) for a in range(n)]
        for j, chip in enumerate(chips):
            first += [copy(a, 1 + j, (x, y, c), (*chip, c), src=ins[a]) for a in range(n)]
        for cp in first:
            cp.start()
        passed = []
        for j, chip in enumerate(chips):
            for a in range(n):
                copy(a, 1 + j, (*chip, c), (x, y, c)).wait_recv()
                fwd = copy(a, 4 + j, (*chip, c), sib)
                fwd.start()
                passed.append(fwd)
        for a in range(n):
            copy(a, 0, sib, (x, y, c)).wait_recv()
        for j, chip in enumerate(chips):
            for a in range(n):
                copy(a, 4 + j, (*chip, 1 - c), (x, y, c)).wait_recv()
        for cp in first + passed:
            cp.wait_send()
        for cp in mine:
            cp.wait()

    return pl.pallas_call(
        body, name=name,
        out_shape=tuple(jax.ShapeDtypeStruct((N_DEV,) + a.shape, a.dtype) for a in arrs),
        in_specs=[_ANY] * n,
        out_specs=tuple([_ANY] * n),
        scratch_shapes=[pltpu.SemaphoreType.DMA((n, 7)), pltpu.SemaphoreType.DMA((n, 7)),
                        pltpu.SemaphoreType.DMA((n,))],
    )(*arrs)


def _pair_exchange(arrs, name):
    n = len(arrs)

    def body(*refs):
        ins = refs[:n]
        outs = refs[n:2 * n]
        send_sems, recv_sems = refs[2 * n:]
        x, y, c = _mesh_pos()
        copies = []
        for a in range(n):
            for q in range(4):
                cp = _rcopy(ins[a].at[2 * q + (1 - c)], outs[a].at[q], send_sems.at[a, q], recv_sems.at[a, q],
                            (x, y, 1 - c))
                cp.start()
                copies.append(cp)
        for cp in copies:
            cp.wait_recv()
        for cp in copies:
            cp.wait_send()

    return pl.pallas_call(
        body, name=name,
        out_shape=tuple(jax.ShapeDtypeStruct((4,) + a.shape[1:], a.dtype) for a in arrs),
        in_specs=[_ANY] * n,
        out_specs=tuple([_ANY] * n),
        scratch_shapes=[pltpu.SemaphoreType.DMA((n, 4)), pltpu.SemaphoreType.DMA((n, 4))],
    )(*arrs)


def _pair_add_call(parts, recv, c_idx, name):
    _, R, C = parts.shape
    (tr, tc), (steps,), idx = _tiling_2d(R, C)

    def body(c_ref, p_ref, r_ref, o_ref):
        o_ref[...] = (p_ref[...].astype(F32) + r_ref[...].astype(F32)).astype(o_ref.dtype)

    return pl.pallas_call(
        body, name=name,
        grid_spec=pltpu.PrefetchScalarGridSpec(
            num_scalar_prefetch=1,
            grid=(4, steps),
            in_specs=[pl.BlockSpec((None, tr, tc), lambda q, i, c_ref: (2 * q + c_ref[0],) + idx(i)),
                      pl.BlockSpec((None, tr, tc), lambda q, i, c_ref: (q,) + idx(i))],
            out_specs=pl.BlockSpec((None, tr, tc), lambda q, i, c_ref: (q,) + idx(i))),
        out_shape=jax.ShapeDtypeStruct((4, R, C), parts.dtype),
        compiler_params=_cparams(("arbitrary", "arbitrary")),
    )(c_idx, parts, recv)


def _chip_exchange(arrs, name):
    n = len(arrs)

    def body(*refs):
        ins = refs[:n]
        outs = refs[n:2 * n]
        send_sems, recv_sems, loc_sems = refs[2 * n:]
        x, y, c = _mesh_pos()
        mine = [pltpu.make_async_copy(ins[a].at[2 * x + y], outs[a].at[3], loc_sems.at[a]) for a in range(n)]
        for cp in mine:
            cp.start()
        copies = []
        for j, (px, py) in enumerate(_other_chips(x, y)):
            for a in range(n):
                cp = _rcopy(ins[a].at[2 * px + py], outs[a].at[j], send_sems.at[a, j], recv_sems.at[a, j],
                            (px, py, c))
                cp.start()
                copies.append(cp)
        for cp in copies:
            cp.wait_recv()
        for cp in copies:
            cp.wait_send()
        for cp in mine:
            cp.wait()

    return pl.pallas_call(
        body, name=name,
        out_shape=tuple(jax.ShapeDtypeStruct(a.shape, a.dtype) for a in arrs),
        in_specs=[_ANY] * n,
        out_specs=tuple([_ANY] * n),
        scratch_shapes=[pltpu.SemaphoreType.DMA((n, 3)), pltpu.SemaphoreType.DMA((n, 3)),
                        pltpu.SemaphoreType.DMA((n,))],
    )(*arrs)


def _proj_call(x, norm_g, w3, wr):
    T, D = x.shape
    tm = min(512, T)

    def f_slot(j):
        return ((j >= 2).astype(jnp.int32) + (j >= 6).astype(jnp.int32)
                + (j >= 7).astype(jnp.int32) + (j >= 8).astype(jnp.int32))

    def b_slot(j):
        return (j >= 3).astype(jnp.int32) + (j >= 4).astype(jnp.int32) + (j >= 5).astype(jnp.int32)

    def body(x_ref, g_ref, w_ref, wr_ref, pf_ref, pb_ref, rank_ref, ht_ref, h_scr):
        j = pl.program_id(1)

        @pl.when(j == 0)
        def _():
            xv = x_ref[...]
            r = lax.rsqrt(jnp.mean(xv * xv, axis=-1, keepdims=True) + EPS)
            h = (xv * r) * g_ref[...]
            hb = _bf(h)
            h_scr[...] = hb
            ht_ref[...] = _bf(h.T)
            rank_ref[...] = _dot_nt(hb, wr_ref[...])

        is_b = (j == 1) | ((j >= 3) & (j <= 5))

        @pl.when(is_b)
        def _():
            pb_ref[...] = _bf(_dot_nt(h_scr[...], w_ref[...]))

        @pl.when(jnp.logical_not(is_b))
        def _():
            pf_ref[...] = _dot_nt(h_scr[...], w_ref[...])

    return pl.pallas_call(
        body, name="proj",
        grid=(T // tm, N_GROUPS),
        in_specs=[pl.BlockSpec((tm, D), lambda i, j: (i, 0)),
                  pl.BlockSpec((1, D), lambda i, j: (0, 0)),
                  pl.BlockSpec((None, 1024, D), lambda i, j: (j, 0, 0)),
                  pl.BlockSpec((128, D), lambda i, j: (0, 0))],
        out_specs=(pl.BlockSpec((None, tm, 1024), lambda i, j: (f_slot(j), i, 0)),
                   pl.BlockSpec((None, tm, 1024), lambda i, j: (b_slot(j), i, 0)),
                   pl.BlockSpec((tm, 128), lambda i, j: (i, 0)),
                   pl.BlockSpec((D, tm), lambda i, j: (0, i))),
        out_shape=(jax.ShapeDtypeStruct((5, T, 1024), F32),
                   jax.ShapeDtypeStruct((4, T, 1024), BF16),
                   jax.ShapeDtypeStruct((T, 128), F32),
                   jax.ShapeDtypeStruct((D, T), BF16)),
        scratch_shapes=[pltpu.VMEM((tm, D), BF16)],
        compiler_params=_cparams(("arbitrary", "arbitrary")),
    )(x, norm_g, w3, wr)


def _gla_chunk_terms(la_h, q, k):
    C = GLA_CHUNK
    low = _bf((_iota2(C, C, 0) >= _iota2(C, C, 1)).astype(F32))
    b = _tri_left(low, la_h)
    bl = b[C - 1:C, :]
    eb = jnp.exp(b)
    enb = jnp.exp(-b)
    ebl_b = jnp.exp(bl - b)
    scale = GLA_HK ** -0.5
    qe = q * eb * scale
    ke = k * enb
    kd = k * ebl_b
    return b, bl, eb, enb, ebl_b, qe, ke, kd


def _gla_fwd_call(projf, projb, rank, wdec, bdec):
    T = projf.shape[1]
    C = GLA_CHUNK
    n_chunks = T // C

    def body(qk_ref, v_ref, rank_ref, wd_ref, bd_ref, o_ref, st_ref, la_ref, st_scr):
        @pl.when(pl.program_id(0) == 0)
        def _():
            st_scr[...] = jnp.zeros_like(st_scr)

        dec = _dot(_bf(rank_ref[...]), _bf(wd_ref[...])) + bd_ref[...]
        la = (jnp.minimum(dec, 0.0) - _softplus_neg_abs(dec)) / GLA_TAU
        la_ref[...] = la
        mask = _iota2(C, C, 0) >= _iota2(C, C, 1)
        for hh in range(GLA_HEADS):
            la_h = la[:, hh * GLA_HK:(hh + 1) * GLA_HK]
            q = qk_ref[:, hh * GLA_HK:(hh + 1) * GLA_HK]
            k = qk_ref[:, GLA_DK + hh * GLA_HK:GLA_DK + (hh + 1) * GLA_HK]
            v = _bf(v_ref[:, hh * GLA_HV:(hh + 1) * GLA_HV])
            _, bl, _, _, _, qe, ke, kd = _gla_chunk_terms(la_h, q, k)
            st = st_scr[hh]
            st_ref[hh] = st
            p = jnp.where(mask, _dot_nt(_bf(qe), _bf(ke)), 0.0)
            o = _dot(_bf(p), v) + _dot_nt(_bf(qe), _bf(st))
            o_ref[:, hh * GLA_HV:(hh + 1) * GLA_HV] = o
            st_scr[hh] = st * jnp.exp(bl) + _dot_tn(v, _bf(kd))

    return pl.pallas_call(
        body, name="gla_fwd",
        grid=(n_chunks,),
        in_specs=[pl.BlockSpec((None, C, 1024), lambda n: (0, n, 0)),
                  pl.BlockSpec((None, C, 1024), lambda n: (0, n, 0)),
                  pl.BlockSpec((C, 128), lambda n: (n, 0)),
                  pl.BlockSpec((128, GLA_DK), lambda n: (0, 0)),
                  pl.BlockSpec((1, GLA_DK), lambda n: (0, 0))],
        out_specs=(pl.BlockSpec((C, 1024), lambda n: (n, 0)),
                   pl.BlockSpec((None, GLA_HEADS, GLA_HV, GLA_HK), lambda n: (n, 0, 0, 0)),
                   pl.BlockSpec((C, GLA_DK), lambda n: (n, 0))),
        out_shape=(jax.ShapeDtypeStruct((T, 1024), F32),
                   jax.ShapeDtypeStruct((n_chunks, GLA_HEADS, GLA_HV, GLA_HK), F32),
                   jax.ShapeDtypeStruct((T, GLA_DK), F32)),
        scratch_shapes=[pltpu.VMEM((GLA_HEADS, GLA_HV, GLA_HK), F32)],
        compiler_params=_cparams(("arbitrary",)),
    )(projf, projb, rank, wdec, bdec)


def _gla_bwd_call(projf, projb, la, do_gla, st_all, rank, wdec):
    T = projf.shape[1]
    C = GLA_CHUNK
    n_chunks = T // C
    last = n_chunks - 1

    def body(qk_ref, v_ref, la_ref, do_ref, st_ref, rank_ref, wd_ref,
             dqk_ref, dv_ref, drank_ref, dwd_ref, dbd_ref, dst_scr):
        @pl.when(pl.program_id(0) == 0)
        def _():
            dst_scr[...] = jnp.zeros_like(dst_scr)
            dwd_ref[...] = jnp.zeros_like(dwd_ref)
            dbd_ref[...] = jnp.zeros_like(dbd_ref)

        mask = _iota2(C, C, 0) >= _iota2(C, C, 1)
        upp = _bf((_iota2(C, C, 0) <= _iota2(C, C, 1)).astype(F32))
        scale = GLA_HK ** -0.5
        la = la_ref[...]
        ddec_parts = []
        for hh in range(GLA_HEADS):
            la_h = la[:, hh * GLA_HK:(hh + 1) * GLA_HK]
            q = qk_ref[:, hh * GLA_HK:(hh + 1) * GLA_HK]
            k = qk_ref[:, GLA_DK + hh * GLA_HK:GLA_DK + (hh + 1) * GLA_HK]
            v = _bf(v_ref[:, hh * GLA_HV:(hh + 1) * GLA_HV])
            do = _bf(do_ref[:, hh * GLA_HV:(hh + 1) * GLA_HV])
            _, bl, eb, enb, ebl_b, qe, ke, kd = _gla_chunk_terms(la_h, q, k)
            qeb, keb, kdb = _bf(qe), _bf(ke), _bf(kd)
            st = st_ref[hh]
            dstn = dst_scr[hh]
            dstnb = _bf(dstn)
            ebl = jnp.exp(bl)
            p = jnp.where(mask, _dot_nt(qeb, keb), 0.0)
            dp = _bf(jnp.where(mask, _dot_nt(do, v), 0.0))
            dv = _dot_tn(_bf(p), do) + _dot_nt(kdb, dstnb)
            dqe = _dot(dp, keb) + _dot(do, _bf(st))
            dke = _dot_tn(dp, qeb)
            dkd = _dot(v, dstnb)
            dst_scr[hh] = _dot_tn(do, qeb) + dstn * ebl
            debl = jnp.sum(dstn * st, axis=0, keepdims=True)
            dkd_kd = dkd * kd
            db = dqe * qe - dke * ke - dkd_kd
            dbl = jnp.sum(dkd_kd, axis=0, keepdims=True) + ebl * debl
            dla = _tri_left(upp, db) + dbl
            dq = dqe * eb * scale
            dk = dke * enb + dkd * ebl_b
            dqk_ref[:, hh * GLA_HK:(hh + 1) * GLA_HK] = _bf(dq)
            dqk_ref[:, GLA_DK + hh * GLA_HK:GLA_DK + (hh + 1) * GLA_HK] = _bf(dk)
            dv_ref[:, hh * GLA_HV:(hh + 1) * GLA_HV] = _bf(dv)
            ddec_parts.append(dla * (1.0 / GLA_TAU) * (1.0 - jnp.exp(GLA_TAU * la_h)))
        ddec = jnp.concatenate(ddec_parts, axis=1)
        ddecb = _bf(ddec)
        drank_ref[...] = _bf(_dot_nt(ddecb, _bf(wd_ref[...])))
        dwd_ref[...] += _dot_tn(_bf(rank_ref[...]), ddecb)
        dbd_ref[...] += jnp.sum(ddec, axis=0, keepdims=True)

    return pl.pallas_call(
        body, name="gla_bwd",
        grid=(n_chunks,),
        in_specs=[pl.BlockSpec((None, C, 1024), lambda n: (0, last - n, 0)),
                  pl.BlockSpec((None, C, 1024), lambda n: (0, last - n, 0)),
                  pl.BlockSpec((C, GLA_DK), lambda n: (last - n, 0)),
                  pl.BlockSpec((C, 1024), lambda n: (last - n, 0)),
                  pl.BlockSpec((None, GLA_HEADS, GLA_HV, GLA_HK), lambda n: (last - n, 0, 0, 0)),
                  pl.BlockSpec((C, 128), lambda n: (last - n, 0)),
                  pl.BlockSpec((128, GLA_DK), lambda n: (0, 0))],
        out_specs=(pl.BlockSpec((C, 1024), lambda n: (last - n, 0)),
                   pl.BlockSpec((C, 1024), lambda n: (last - n, 0)),
                   pl.BlockSpec((C, 128), lambda n: (last - n, 0)),
                   pl.BlockSpec((128, GLA_DK), lambda n: (0, 0)),
                   pl.BlockSpec((1, GLA_DK), lambda n: (0, 0))),
        out_shape=(jax.ShapeDtypeStruct((T, 1024), BF16),
                   jax.ShapeDtypeStruct((T, 1024), BF16),
                   jax.ShapeDtypeStruct((T, 128), BF16),
                   jax.ShapeDtypeStruct((128, GLA_DK), F32),
                   jax.ShapeDtypeStruct((1, GLA_DK), F32)),
        scratch_shapes=[pltpu.VMEM((GLA_HEADS, GLA_HV, GLA_HK), F32)],
        compiler_params=_cparams(("arbitrary",)),
    )(projf, projb, la, do_gla, st_all, rank, wdec)


def _sb_logs(z):
    sp = _softplus_neg_abs(z)
    return jnp.minimum(z, 0.0) - sp, -jnp.maximum(z, 0.0) - sp


SB_HG_FWD = 8
SB_HG_BWD = 4


def _sb_fwd_call(projb):
    T = projb.shape[1]
    B = SB_BLOCK
    HG = SB_HG_FWD
    W = HG * SB_HD
    scale = 1.0 / math.sqrt(SB_HD)

    def body(q_ref, k_ref, v_ref, o_ref, cb_scr):
        i = pl.program_id(1)
        rows = HG * B
        strict = _iota2(rows, B, 1) < (_iota2(rows, B, 0) & (B - 1))
        after = (_iota2(B, B, 0) > _iota2(B, B, 1)).astype(F32)
        tri = _bf(jnp.concatenate([after, jnp.ones((B, B), F32)], axis=1))
        o_ref[...] = jnp.zeros_like(o_ref)
        cb_scr[...] = jnp.zeros_like(cb_scr)

        def block(j, masked):
            off = pl.multiple_of(j * B, B)
            z = jnp.concatenate(
                [_dot_nt(q_ref[:, hh * SB_HD:(hh + 1) * SB_HD], k_ref[pl.ds(off, B), hh * SB_HD:(hh + 1) * SB_HD])
                 for hh in range(HG)], axis=0) * scale
            lsz, l1m = _sb_logs(z)
            if masked:
                l1m = jnp.where(strict, l1m, 0.0)
            r = _tri2_right(l1m, tri)
            cb = cb_scr[...]
            a = jnp.exp(lsz + cb + r[:, :B])
            if masked:
                a = jnp.where(strict, a, 0.0)
            cb_scr[...] = cb + r[:, B:]
            ab = _bf(a)
            for hh in range(HG):
                cs = slice(hh * SB_HD, (hh + 1) * SB_HD)
                o_ref[:, cs] += _dot(ab[hh * B:(hh + 1) * B, :], v_ref[pl.ds(off, B), cs])

        block(i, True)

        def step(jj, c):
            block(i - jj, False)
            return c

        lax.fori_loop(1, i + 1, step, 0)

    return pl.pallas_call(
        body, name="sb_fwd",
        grid=(SB_HEADS // HG, T // B),
        in_specs=[pl.BlockSpec((None, B, W), lambda h, i: (1, i, h)),
                  pl.BlockSpec((None, T, W), lambda h, i: (2, 0, h)),
                  pl.BlockSpec((None, T, W), lambda h, i: (3, 0, h))],
        out_specs=pl.BlockSpec((B, W), lambda h, i: (i, h)),
        out_shape=jax.ShapeDtypeStruct((T, 1024), F32),
        scratch_shapes=[pltpu.VMEM((HG * B, B), F32)],
        compiler_params=_cparams(("arbitrary", "arbitrary")),
    )(projb, projb, projb)


def _sb_bwd_call(projb, do_sb):
    T = projb.shape[1]
    B = SB_BLOCK
    nb = T // B
    HG = SB_HG_BWD
    W = HG * SB_HD
    scale = 1.0 / math.sqrt(SB_HD)

    def body(q_ref, k_ref, v_ref, do_ref, dq_ref, dk_ref, dv_ref,
             dk_scr, dv_scr, kt_scr, beta_scr, g_scr, dqt_scr):
        i = pl.program_id(1)

        @pl.when(i == 0)
        def _():
            dk_scr[...] = jnp.zeros_like(dk_scr)
            dv_scr[...] = jnp.zeros_like(dv_scr)
            for hh in range(HG):
                for jb in range(nb):
                    kt_scr[hh, jb] = _bf(
                        k_ref[jb * B:(jb + 1) * B, hh * SB_HD:(hh + 1) * SB_HD].astype(F32).T)

        dqt_scr[...] = jnp.zeros_like(dqt_scr)
        strict = _iota2(B, W, 0) < (_iota2(B, W, 1) & (B - 1))
        later = _bf((_iota2(B, B, 1) > _iota2(B, B, 0)).astype(F32))
        earlier = _bf((_iota2(B, B, 1) < _iota2(B, B, 0)).astype(F32))
        dob = _bf(do_ref[...])

        def heads(fn):
            return [fn(slice(hh * SB_HD, (hh + 1) * SB_HD)) for hh in range(HG)]

        def pass1(j, cb, masked):
            off = pl.multiple_of(j * B, B)
            z = jnp.concatenate(heads(lambda cs: _dot_nt(k_ref[pl.ds(off, B), cs], q_ref[:, cs])), axis=1) * scale
            da = jnp.concatenate(heads(lambda cs: _dot_nt(v_ref[pl.ds(off, B), cs], dob[:, cs])), axis=1)
            lsz, l1m = _sb_logs(z)
            if masked:
                l1m = jnp.where(strict, l1m, 0.0)
            a = jnp.exp(lsz + cb + _tri2_left(later, l1m))
            if masked:
                a = jnp.where(strict, a, 0.0)
            g_scr[j] = a * da
            beta_scr[j] = jnp.exp(lsz)
            ab = _bf(a)
            for hh in range(HG):
                cs = slice(hh * SB_HD, (hh + 1) * SB_HD)
                dv_scr[pl.ds(off, B), cs] += _dot(ab[:, cs], dob[:, cs])
            return cb + jnp.sum(l1m, axis=0, keepdims=True)

        zero = jnp.zeros((1, W), F32)
        cb = pass1(i, zero, True)
        lax.fori_loop(1, i + 1, lambda jj, cr: pass1(i - jj, cr, False), cb)

        def pass2(j, cg, masked):
            off = pl.multiple_of(j * B, B)
            g = g_scr[j]
            beta = beta_scr[j]
            dz = g * (1.0 - beta) - beta * (cg + _tri2_left(earlier, g))
            if masked:
                dz = jnp.where(strict, dz, 0.0)
            dzb = _bf(dz * scale)
            for hh in range(HG):
                cs = slice(hh * SB_HD, (hh + 1) * SB_HD)
                dk_scr[pl.ds(off, B), cs] += _dot(dzb[:, cs], q_ref[:, cs])
                dqt_scr[hh] += _dot(kt_scr[hh, j], dzb[:, cs])
            return cg + jnp.sum(g, axis=0, keepdims=True)

        cg = lax.fori_loop(0, i, lambda j, cr: pass2(j, cr, False), zero)
        pass2(i, cg, True)
        for hh in range(HG):
            dq_ref[:, hh * SB_HD:(hh + 1) * SB_HD] = _bf(dqt_scr[hh].T)

        @pl.when(i == nb - 1)
        def _():
            dk_ref[...] = _bf(dk_scr[...])
            dv_ref[...] = _bf(dv_scr[...])

    return pl.pallas_call(
        body, name="sb_bwd",
        grid=(SB_HEADS // HG, nb),
        in_specs=[pl.BlockSpec((None, B, W), lambda h, i: (1, i, h)),
                  pl.BlockSpec((None, T, W), lambda h, i: (2, 0, h)),
                  pl.BlockSpec((None, T, W), lambda h, i: (3, 0, h)),
                  pl.BlockSpec((B, W), lambda h, i: (i, h))],
        out_specs=(pl.BlockSpec((B, W), lambda h, i: (i, h)),
                   pl.BlockSpec((T, W), lambda h, i: (0, h)),
                   pl.BlockSpec((T, W), lambda h, i: (0, h))),
        out_shape=(jax.ShapeDtypeStruct((T, 1024), BF16),
                   jax.ShapeDtypeStruct((T, 1024), BF16),
                   jax.ShapeDtypeStruct((T, 1024), BF16)),
        scratch_shapes=[pltpu.VMEM((T, W), F32), pltpu.VMEM((T, W), F32),
                        pltpu.VMEM((HG, nb, SB_HD, B), BF16),
                        pltpu.VMEM((nb, B, W), F32), pltpu.VMEM((nb, B, W), F32),
                        pltpu.VMEM((HG, SB_HD, B), F32)],
        compiler_params=_cparams(("arbitrary", "arbitrary")),
    )(projb, projb, projb, do_sb)


def _mid_call(o_gla, o_sb, projf, x, target, wpa, wpb, wo, gla_g, b_gate, final_g):
    T, D = x.shape
    tm = min(256, T)

    def body(og_ref, ggate_ref, osb_ref, sgate_ref, ma_ref, mb_ref, x_ref, tgt_ref,
             wpa_ref, wpb_ref, wo_ref, glag_ref, bg_ref, fg_ref,
             dx2_ref, dogla_ref, dosb_ref, dggate_ref, dsgate_ref, dm_ref,
             mt_ref, ogt_ref, obt_ref, dx2b_ref, dya_ref, dyb_ref,
             dfg_ref, dbg_ref, dglag_ref, loss_ref):
        @pl.when(pl.program_id(0) == 0)
        def _():
            dfg_ref[...] = jnp.zeros_like(dfg_ref)
            dbg_ref[...] = jnp.zeros_like(dbg_ref)
            dglag_ref[...] = jnp.zeros_like(dglag_ref)
            loss_ref[...] = jnp.zeros_like(loss_ref)

        glag = glag_ref[...]
        ggate = ggate_ref[...]
        sg = _sigmoid(ggate)
        silu_g = ggate * sg
        ohat, rinv, nrm = [], [], []
        for hh in range(GLA_HEADS):
            oh = og_ref[:, hh * GLA_HV:(hh + 1) * GLA_HV]
            r = lax.rsqrt(jnp.mean(oh * oh, axis=-1, keepdims=True) + EPS)
            ohat.append(oh * r)
            rinv.append(r)
            nrm.append(ohat[-1] * glag)
        n_all = jnp.concatenate(nrm, axis=1)
        og = n_all * silu_g
        ogb = _bf(og)
        ya = _dot(ogb, wpa_ref[...])
        sgate = sgate_ref[...]
        ss = _sigmoid(sgate)
        silu_s = sgate * ss
        osb = osb_ref[...]
        ob = osb * silu_s
        obb = _bf(ob)
        yb = _dot(obb, wpb_ref[...])
        ga = _sigmoid(ma_ref[...] + bg_ref[:, :D])
        gb = _sigmoid(mb_ref[...] + bg_ref[:, D:])
        merged = ga * ya + gb * yb
        mgb = _bf(merged)
        x2 = x_ref[...] + _dot(mgb, wo_ref[...])
        r2 = lax.rsqrt(jnp.mean(x2 * x2, axis=-1, keepdims=True) + EPS)
        xh2 = x2 * r2
        fg = fg_ref[...]
        err = xh2 * fg - tgt_ref[...]
        loss_ref[...] += jnp.broadcast_to(
            0.5 * jnp.sum(jnp.mean(err * err, axis=-1, keepdims=True), axis=0, keepdims=True), (1, 128))
        dy = err * (1.0 / D)
        dfg_ref[...] += jnp.sum(dy * xh2, axis=0, keepdims=True)
        dxh = dy * fg
        dx2 = r2 * (dxh - xh2 * jnp.mean(dxh * xh2, axis=-1, keepdims=True))
        dx2_ref[...] = dx2
        dx2b = _bf(dx2)
        dx2b_ref[...] = dx2b
        dmerged = _dot_nt(dx2b, wo_ref[...])
        dya = dmerged * ga
        dyb = dmerged * gb
        dma = dmerged * ya * ga * (1.0 - ga)
        dmb = dmerged * yb * gb * (1.0 - gb)
        dm_ref[:, :D] = _bf(dma)
        dm_ref[:, D:] = _bf(dmb)
        dbg_ref[:, :D] += jnp.sum(dma, axis=0, keepdims=True)
        dbg_ref[:, D:] += jnp.sum(dmb, axis=0, keepdims=True)
        dyab = _bf(dya)
        dybb = _bf(dyb)
        dya_ref[...] = dyab
        dyb_ref[...] = dybb
        dog = _dot_nt(dyab, wpa_ref[...])
        dob = _dot_nt(dybb, wpb_ref[...])
        dosb_ref[...] = dob * silu_s
        dsgate_ref[...] = _bf(dob * osb * (ss * (1.0 + sgate * (1.0 - ss))))
        dn = dog * silu_g
        dggate_ref[...] = _bf(dog * n_all * (sg * (1.0 + ggate * (1.0 - sg))))
        dglag = jnp.zeros((1, GLA_HV), F32)
        for hh in range(GLA_HEADS):
            dnh = dn[:, hh * GLA_HV:(hh + 1) * GLA_HV]
            dglag = dglag + jnp.sum(dnh * ohat[hh], axis=0, keepdims=True)
            dohat = dnh * glag
            dogla_ref[:, hh * GLA_HV:(hh + 1) * GLA_HV] = rinv[hh] * (
                dohat - ohat[hh] * jnp.mean(dohat * ohat[hh], axis=-1, keepdims=True))
        dglag_ref[...] += dglag
        mt_ref[...] = _bf(merged.T)
        ogt_ref[...] = _bf(og.T)
        obt_ref[...] = _bf(ob.T)

    row = lambda i: (i, 0)
    const = lambda i: (0, 0)
    tile = pl.BlockSpec((tm, D), row)
    tile_t = pl.BlockSpec((D, tm), lambda i: (0, i))
    wspec = pl.BlockSpec((D, D), const)
    return pl.pallas_call(
        body, name="mid",
        grid=(T // tm,),
        in_specs=[tile,
                  pl.BlockSpec((None, tm, D), lambda i: (1, i, 0)),
                  tile,
                  pl.BlockSpec((None, tm, D), lambda i: (2, i, 0)),
                  pl.BlockSpec((None, tm, D), lambda i: (3, i, 0)),
                  pl.BlockSpec((None, tm, D), lambda i: (4, i, 0)),
                  tile, tile, wspec, wspec, wspec,
                  pl.BlockSpec((1, GLA_HV), const),
                  pl.BlockSpec((1, 2 * D), const),
                  pl.BlockSpec((1, D), const)],
        out_specs=(tile, tile, tile, tile, tile,
                   pl.BlockSpec((tm, 2 * D), row),
                   tile_t, tile_t, tile_t, tile, tile, tile,
                   pl.BlockSpec((1, D), const),
                   pl.BlockSpec((1, 2 * D), const),
                   pl.BlockSpec((1, GLA_HV), const),
                   pl.BlockSpec((1, 128), const)),
        out_shape=(jax.ShapeDtypeStruct((T, D), F32),
                   jax.ShapeDtypeStruct((T, D), F32),
                   jax.ShapeDtypeStruct((T, D), F32),
                   jax.ShapeDtypeStruct((T, D), BF16),
                   jax.ShapeDtypeStruct((T, D), BF16),
                   jax.ShapeDtypeStruct((T, 2 * D), BF16),
                   jax.ShapeDtypeStruct((D, T), BF16),
                   jax.ShapeDtypeStruct((D, T), BF16),
                   jax.ShapeDtypeStruct((D, T), BF16),
                   jax.ShapeDtypeStruct((T, D), BF16),
                   jax.ShapeDtypeStruct((T, D), BF16),
                   jax.ShapeDtypeStruct((T, D), BF16),
                   jax.ShapeDtypeStruct((1, D), F32),
                   jax.ShapeDtypeStruct((1, 2 * D), F32),
                   jax.ShapeDtypeStruct((1, GLA_HV), F32),
                   jax.ShapeDtypeStruct((1, 128), F32)),
        compiler_params=_cparams(("arbitrary",)),
    )(o_gla, projf, o_sb, projf, projf, projf, x, target, wpa, wpb, wo, gla_g, b_gate, final_g)


def _dh_call(pieces, dmlog, drank, w3, wr, x, dx2, norm_g):
    T, D = x.shape
    tm = min(256, T)
    npc = len(pieces)

    def body(*refs):
        pcs = refs[:npc]
        (dm_ref, dr_ref, w_hbm, wr_ref, x_ref, dx2_ref, g_ref,
         gx_ref, dg_ref, dwr_ref, w_scr, sem) = refs[npc:]

        @pl.when(pl.program_id(0) == 0)
        def _():
            cp = pltpu.make_async_copy(w_hbm, w_scr, sem)
            cp.start()
            cp.wait()
            dg_ref[...] = jnp.zeros_like(dg_ref)
            dwr_ref[...] = jnp.zeros_like(dwr_ref)

        dr = dr_ref[...]
        dh = _dot(dr, wr_ref[...])
        for g in range(npc):
            dh = dh + _dot(pcs[g][...], w_scr[g])
        dh = dh + _dot(dm_ref[:, :D], w_scr[npc])
        dh = dh + _dot(dm_ref[:, D:], w_scr[npc + 1])
        xv = x_ref[...]
        r = lax.rsqrt(jnp.mean(xv * xv, axis=-1, keepdims=True) + EPS)
        xhat = xv * r
        g = g_ref[...]
        dg_ref[...] += jnp.sum(dh * xhat, axis=0, keepdims=True)
        dxhat = dh * g
        gx_ref[...] = r * (dxhat - xhat * jnp.mean(dxhat * xhat, axis=-1, keepdims=True)) + dx2_ref[...]
        dwr_ref[...] += _dot_tn(dr, _bf(xhat * g))

    row = lambda i: (i, 0)
    const = lambda i: (0, 0)
    tile = pl.BlockSpec((tm, D), row)
    return pl.pallas_call(
        body, name="dh",
        grid=(T // tm,),
        in_specs=[tile] * npc + [
            pl.BlockSpec((tm, 2 * D), row),
            pl.BlockSpec((tm, 128), row),
            pl.BlockSpec(memory_space=pl.ANY),
            pl.BlockSpec((128, D), const),
            tile, tile,
            pl.BlockSpec((1, D), const)],
        out_specs=(tile, pl.BlockSpec((1, D), const), pl.BlockSpec((128, D), const)),
        out_shape=(jax.ShapeDtypeStruct((T, D), F32),
                   jax.ShapeDtypeStruct((1, D), F32),
                   jax.ShapeDtypeStruct((128, D), F32)),
        scratch_shapes=[pltpu.VMEM((N_GROUPS, 1024, D), BF16), pltpu.SemaphoreType.DMA],
        compiler_params=_cparams(("arbitrary",)),
    )(*pieces, dmlog, drank, w3, wr, x, dx2, norm_g)


def _wgrad_call(lhs_list, lhs_of_group, rhs_list, rhs_of_group, n_transposed):
    n_groups = len(rhs_of_group)
    D, T = lhs_list[0].shape
    tk = min(256, T)
    nk = T // tk
    nl = len(lhs_list)

    def body(*refs):
        lhs = refs[:nl]
        rhs = refs[nl:nl + n_groups]
        out_ref, acc = refs[nl + n_groups:]
        g = pl.program_id(0)
        i = pl.program_id(1)

        @pl.when(i == 0)
        def _():
            acc[...] = jnp.zeros_like(acc)

        for p in range(n_groups):
            @pl.when(g == p)
            def _(p=p):
                acc[...] += _dot(lhs[lhs_of_group[p]][...], rhs[p][...])

        @pl.when((i == nk - 1) & (g < n_transposed))
        def _():
            out_ref[...] = _bf(acc[...].T)

        @pl.when((i == nk - 1) & (g >= n_transposed))
        def _():
            out_ref[...] = _bf(acc[...])

    def lhs_spec(a):
        groups = [g for g in range(n_groups) if lhs_of_group[g] == a]
        lo, hi = min(groups), max(groups)
        assert groups == list(range(lo, hi + 1))
        return pl.BlockSpec((D, tk), lambda g, i: (0, jnp.where((g >= lo) & (g <= hi), i, 0)))

    def rhs_spec(p):
        cb = rhs_of_group[p][1]
        return pl.BlockSpec((tk, 1024), lambda g, i: (jnp.where(g == p, i, 0), cb))

    return pl.pallas_call(
        body, name="wgrad",
        grid=(n_groups, nk),
        in_specs=[lhs_spec(a) for a in range(nl)] + [rhs_spec(p) for p in range(n_groups)],
        out_specs=pl.BlockSpec((None, D, 1024), lambda g, i: (g, 0, 0)),
        out_shape=jax.ShapeDtypeStruct((n_groups, D, 1024), BF16),
        scratch_shapes=[pltpu.VMEM((D, 1024), F32)],
        compiler_params=_cparams(("arbitrary", "arbitrary")),
    )(*lhs_list, *[rhs_list[rhs_of_group[p][0]] for p in range(n_groups)])


def _adamw_call(parts, w, m, v, name):
    R, C = w.shape
    n_parts = parts.shape[0]
    (tr, tc), grid, idx = _tiling_2d(R, C)

    def body(p_ref, w_ref, m_ref, v_ref, g_ref, d_ref, nm_ref, nv_ref):
        g = p_ref[n_parts - 1].astype(F32)
        for k in range(n_parts - 1):
            g = g + p_ref[k].astype(F32)
        mm = ADAM_B1 * m_ref[...] + (1.0 - ADAM_B1) * g
        vv = ADAM_B2 * v_ref[...] + (1.0 - ADAM_B2) * (g * g)
        m_hat = mm / (1.0 - ADAM_B1 ** ADAM_STEP)
        v_hat = vv / (1.0 - ADAM_B2 ** ADAM_STEP)
        d_ref[...] = -ADAM_LR * (m_hat / (jnp.sqrt(v_hat) + ADAM_EPS) + ADAM_WD * w_ref[...])
        g_ref[...] = g
        nm_ref[...] = mm
        nv_ref[...] = vv

    blk = pl.BlockSpec((tr, tc), idx)
    sds = jax.ShapeDtypeStruct((R, C), F32)
    return pl.pallas_call(
        body, name=name,
        grid=grid,
        in_specs=[pl.BlockSpec((n_parts, tr, tc), lambda i: (0,) + idx(i)), blk, blk, blk],
        out_specs=(blk, blk, blk, blk),
        out_shape=(sds, sds, sds, sds),
        compiler_params=_cparams(("arbitrary",)),
    )(parts, w, m, v)


def _local_step(x, target, w3, wr, wdec, bdec, wpa, wpb, wo, norm_g, gla_g, b_gate, final_g):
    projf, projb, rank, ht = _proj_call(x, norm_g, w3, wr)
    o_gla, st_all, la = _gla_fwd_call(projf, projb, rank, wdec, bdec)
    o_sb = _sb_fwd_call(projb)
    (dx2, do_gla, do_sb, dggate, dsgate, dmlog, mt, ogt, obt, dx2b, dya, dyb,
     dfinal_g, db_gate, dgla_g, loss) = _mid_call(o_gla, o_sb, projf, x, target, wpa, wpb, wo,
                                                 gla_g, b_gate, final_g)
    dsq, dsk, dsv = _sb_bwd_call(projb, do_sb)
    dqk, dgv, drank, dwdec, dbdec = _gla_bwd_call(projf, projb, la, do_gla, st_all, rank, wdec)
    pieces = [dqk, dgv, dggate, dsq, dsk, dsv, dsgate]
    grad_x, dnorm_g, dwr = _dh_call(pieces, dmlog, drank, w3, wr, x, dx2, norm_g)
    rhs_list = pieces + [dmlog, dx2b, dya, dyb]
    rhs_of_group = [(g, 0) for g in range(7)] + [(7, 0), (7, 1), (8, 0), (9, 0), (10, 0)]
    lhs_of_group = [0] * 9 + [1, 2, 3]
    dw_all = _wgrad_call([ht, mt, ogt, obt], lhs_of_group, rhs_list, rhs_of_group, N_GROUPS)
    return grad_x, dw_all, dwr, dwdec, dbdec, dnorm_g, dgla_g, db_gate, dfinal_g, loss


_SM_NORM = 0
_SM_BDEC = _SM_NORM + D_MODEL
_SM_GLAG = _SM_BDEC + GLA_DK
_SM_BGATE = _SM_GLAG + GLA_HV
_SM_FINAL = _SM_BGATE + 2 * D_MODEL
_SM_REPL = _SM_FINAL + D_MODEL
_SM_LOSS = _SM_REPL
_SM_WDEC = _SM_LOSS + 128
_SM_LEN = _SM_WDEC + GLA_RANK * GLA_DK


def kernel(x, norm_g, w_in, w_dec_up, b_dec, gla_norm_g, w_pa, w_pb, b_gate, w_o, final_g, loss_target, m_norm_g, m_w_in, m_w_dec_up, m_b_dec, m_gla_norm_g, m_w_pa, m_w_pb, m_b_gate, m_w_o, m_final_g, v_norm_g, v_w_in, v_w_dec_up, v_b_dec, v_gla_norm_g, v_w_pa, v_w_pb, v_b_gate, v_w_o, v_final_g):
    D = D_MODEL
    me = 4 * lax.axis_index("x") + 2 * lax.axis_index("y") + lax.axis_index("c")

    wp_shard = jnp.stack([w_pa, w_pb, w_o]).astype(BF16)
    win_all, wp_all, wdec_all = _all_gather([w_in.T.astype(BF16), wp_shard, w_dec_up], "gather_w")
    wt_full = win_all.reshape(IN_COLS, D)
    w3 = jnp.concatenate([wt_full[:RANK_COL], wt_full[RANK_COL + GLA_RANK:]], axis=0).reshape(N_GROUPS, 1024, D)
    wr = jnp.pad(wt_full[RANK_COL:RANK_COL + GLA_RANK], ((0, 128 - GLA_RANK), (0, 0)))
    wp_full = wp_all.transpose(1, 0, 2, 3).reshape(3, D, D)
    wdec_full = wdec_all.transpose(1, 0, 2).reshape(GLA_RANK, GLA_DK)
    wdec = jnp.pad(wdec_full, ((0, 128 - GLA_RANK), (0, 0)))

    (grad_x, dw_all, dwr, dwdec, dbdec, dnorm_g, dgla_g, db_gate, dfinal_g, loss) = _local_step(
        x[0], loss_target[0], w3, wr, wdec, b_dec.reshape(1, -1), wp_full[0], wp_full[1], wp_full[2],
        norm_g.reshape(1, -1), gla_norm_g.reshape(1, -1), b_gate.reshape(1, -1), final_g.reshape(1, -1))

    dmain = dw_all[:N_GROUPS].reshape(N_GROUPS * 1024, D)
    dfull = jnp.concatenate([dmain[:RANK_COL], dwr[:GLA_RANK].astype(BF16), dmain[RANK_COL:]], axis=0)
    g_in = dfull.reshape(N_DEV, SHARD_COLS, D)
    g_p = jnp.stack([dw_all[10], dw_all[11], dw_all[9]]).reshape(3, N_DEV, D // N_DEV, D).transpose(1, 0, 2, 3)
    small = jnp.concatenate([
        dnorm_g.reshape(-1), dbdec.reshape(-1), dgla_g.reshape(-1), db_gate.reshape(-1), dfinal_g.reshape(-1),
        loss.reshape(-1), dwdec[:GLA_RANK].reshape(-1)])
    g_p = g_p.reshape(N_DEV, 3 * (D // N_DEV), D)
    c_idx = lax.axis_index("c").astype(jnp.int32).reshape(1)
    p_in, p_p = _pair_exchange([g_in, g_p], "pair_g")
    s_in = _pair_add_call(g_in, p_in, c_idx, "pair_add_in")
    s_p = _pair_add_call(g_p, p_p, c_idx, "pair_add_p")
    r_in, r_p = _chip_exchange([s_in, s_p], "scatter_g")
    (r_small,) = _all_gather([small.reshape(1, _SM_LEN)], "gather_small")

    gw_in, d_in, nm_in, nv_in = (a.T for a in _adamw_call(r_in, w_in.T, m_w_in.T, v_w_in.T, "adamw_in"))
    wp_f32 = jnp.concatenate([w_pa, w_pb, w_o], axis=0)
    mp = jnp.concatenate([m_w_pa, m_w_pb, m_w_o], axis=0)
    vp = jnp.concatenate([v_w_pa, v_w_pb, v_w_o], axis=0)
    gp, dp, nmp, nvp = _adamw_call(r_p, wp_f32, mp, vp, "adamw_p")
    rows = D // N_DEV

    def split3(a):
        return a[:rows], a[rows:2 * rows], a[2 * rows:]

    g_pa, g_pb, g_o = split3(gp)
    d_pa, d_pb, d_o = split3(dp)
    nm_pa, nm_pb, nm_o = split3(nmp)
    nv_pa, nv_pb, nv_o = split3(nvp)

    w_rep = jnp.concatenate([norm_g, b_dec, gla_norm_g, b_gate, final_g]).reshape(1, _SM_REPL)
    m_rep = jnp.concatenate([m_norm_g, m_b_dec, m_gla_norm_g, m_b_gate, m_final_g]).reshape(1, _SM_REPL)
    v_rep = jnp.concatenate([v_norm_g, v_b_dec, v_gla_norm_g, v_b_gate, v_final_g]).reshape(1, _SM_REPL)
    g_rep, d_rep, nm_rep, nv_rep = _adamw_call(r_small[:, :, :_SM_REPL], w_rep, m_rep, v_rep, "adamw_rep")

    def split_rep(a):
        a = a.reshape(-1)
        return (a[_SM_NORM:_SM_BDEC], a[_SM_BDEC:_SM_GLAG], a[_SM_GLAG:_SM_BGATE],
                a[_SM_BGATE:_SM_FINAL], a[_SM_FINAL:_SM_REPL])

    g_norm, g_bdec, g_glag, g_bgate, g_final = split_rep(g_rep)
    d_norm, d_bdec, d_glag, d_bgate, d_final = split_rep(d_rep)
    nm_norm, nm_bdec, nm_glag, nm_bgate, nm_final = split_rep(nm_rep)
    nv_norm, nv_bdec, nv_glag, nv_bgate, nv_final = split_rep(nv_rep)

    wdec_parts = r_small[:, 0, _SM_WDEC:].reshape(N_DEV, GLA_RANK, GLA_DK)
    cols = GLA_DK // N_DEV
    wdec_mine = lax.dynamic_slice_in_dim(wdec_parts, me * cols, cols, axis=2)
    g_wdec, d_wdec, nm_wdec, nv_wdec = _adamw_call(wdec_mine, w_dec_up, m_w_dec_up, v_w_dec_up, "adamw_dec")

    loss_total = jnp.sum(r_small[:, 0, _SM_LOSS])

    return (loss_total, grad_x[None],
            g_norm, gw_in, g_wdec, g_bdec, g_glag, g_pa, g_pb, g_bgate, g_o, g_final,
            d_norm, d_in, d_wdec, d_bdec, d_glag, d_pa, d_pb, d_bgate, d_o, d_final,
            nm_norm, nm_in, nm_wdec, nm_bdec, nm_glag, nm_pa, nm_pb, nm_bgate, nm_o, nm_final,
            nv_norm, nv_in, nv_wdec, nv_bdec, nv_glag, nv_pa, nv_pb, nv_bgate, nv_o, nv_final)
```

```python
import functools
import math

import jax
import jax.numpy as jnp
from jax import lax
from jax.experimental import pallas as pl
from jax.experimental.pallas import tpu as pltpu

F32 = jnp.float32
BF16 = jnp.bfloat16

N_DEV = 8
D_MODEL = 1024
GLA_HEADS = 4
GLA_HK = 128
GLA_HV = 256
GLA_DK = 512
GLA_RANK = 16
GLA_TAU = 16.0
GLA_CHUNK = 64
SB_HEADS = 8
SB_HD = 128
SB_BLOCK = 128
EPS = 1e-6
N_GROUPS = 9
RANK_COL = 3072
IN_COLS = 9232
SHARD_COLS = IN_COLS // N_DEV

ADAM_LR = 0.001
ADAM_B1 = 0.9
ADAM_B2 = 0.999
ADAM_EPS = 1e-08
ADAM_WD = 0.01
ADAM_STEP = 10

VMEM_LIMIT = 56 * 1024 * 1024
TBLK = 256


def _cparams(sem=None):
    return pltpu.CompilerParams(dimension_semantics=sem, vmem_limit_bytes=VMEM_LIMIT)


def _tiling_2d(rows, cols):
    if rows * cols <= 128 * 1024:
        return (rows, cols), (1,), lambda i: (0, 0)
    if rows % 128 == 0:
        return (128, cols), (rows // 128,), lambda i: (i, 0)
    tc = 256 if cols % 256 == 0 else cols
    return (rows, tc), (cols // tc,), lambda i: (0, i)


def _dot(a, b):
    return jnp.dot(a, b, preferred_element_type=F32)


def _dot_nt(a, b):
    return lax.dot_general(a, b, (((1,), (1,)), ((), ())), preferred_element_type=F32)


def _dot_tn(a, b):
    return lax.dot_general(a, b, (((0,), (0,)), ((), ())), preferred_element_type=F32)


def _bf(x):
    return x.astype(BF16)


def _split3(x):
    hi = x.astype(BF16)
    r = x - hi.astype(F32)
    mid = r.astype(BF16)
    lo = (r - mid.astype(F32)).astype(BF16)
    return hi, mid, lo


def _tri_left(tri, x):
    hi, mid, lo = _split3(x)
    return _dot(tri, hi) + _dot(tri, mid) + _dot(tri, lo)


def _split2(x):
    hi = lax.bitcast_convert_type(lax.bitcast_convert_type(x, jnp.uint32) & jnp.uint32(0xFFFF0000), F32)
    return hi.astype(BF16), (x - hi).astype(BF16)


def _tri2_left(tri, x):
    hi, lo = _split2(x)
    return _dot(tri, hi) + _dot(tri, lo)


def _tri2_right(x, tri):
    hi, lo = _split2(x)
    return _dot(hi, tri) + _dot(lo, tri)


def _iota2(n, m, dim):
    return lax.broadcasted_iota(jnp.int32, (n, m), dim)


def _sigmoid(x):
    return 1.0 / (1.0 + jnp.exp(-x))


def _softplus_neg_abs(z):
    return jnp.log(1.0 + jnp.exp(-jnp.abs(z)))


_ANY = pl.BlockSpec(memory_space=pl.ANY)


def _mesh_pos():
    return lax.axis_index("x"), lax.axis_index("y"), lax.axis_index("c")


def _other_chips(x, y):
    return [(1 - x, y), (x, 1 - y), (1 - x, 1 - y)]


def _rcopy(src, dst, send_sem, recv_sem, to):
    return pltpu.make_async_remote_copy(src_ref=src, dst_ref=dst, send_sem=send_sem, recv_sem=recv_sem,
                                        device_id=to, device_id_type=pl.DeviceIdType.MESH)


def _all_gather(arrs, name):
    n = len(arrs)

    def body(*refs):
        ins = refs[:n]
        outs = refs[n:2 * n]
        send_sems, recv_sems, loc_sems = refs[2 * n:]
        x, y, c = _mesh_pos()
        sib = (x, y, 1 - c)
        chips = _other_chips(x, y)

        def slot(px, py, pc):
            return 4 * px + 2 * py + pc

        def copy(a, k, block, to, src=None):
            dst = outs[a].at[slot(*block)]
            return _rcopy(dst if src is None else src, dst, send_sems.at[a, k], recv_sems.at[a, k], to)

        mine = [pltpu.make_async_copy(ins[a], outs[a].at[slot(x, y, c)], loc_sems.at[a]) for a in range(n)]
        for cp in mine:
            cp.start()
        first = [copy(a, 0, (x, y, c), sib, src=ins[a]) for a in range(n)]
        for j, chip in enumerate(chips):
            first += [copy(a, 1 + j, (x, y, c), (*chip, c), src=ins[a]) for a in range(n)]
        for cp in first:
            cp.start()
        passed = []
        for j, chip in enumerate(chips):
            for a in range(n):
                copy(a, 1 + j, (*chip, c), (x, y, c)).wait_recv()
                fwd = copy(a, 4 + j, (*chip, c), sib)
                fwd.start()
                passed.append(fwd)
        for a in range(n):
            copy(a, 0, sib, (x, y, c)).wait_recv()
        for j, chip in enumerate(chips):
            for a in range(n):
                copy(a, 4 + j, (*chip, 1 - c), (x, y, c)).wait_recv()
        for cp in first + passed:
            cp.wait_send()
        for cp in mine:
            cp.wait()

    return pl.pallas_call(
        body, name=name,
        out_shape=tuple(jax.ShapeDtypeStruct((N_DEV,) + a.shape, a.dtype) for a in arrs),
        in_specs=[_ANY] * n,
        out_specs=tuple([_ANY] * n),
        scratch_shapes=[pltpu.SemaphoreType.DMA((n, 7)), pltpu.SemaphoreType.DMA((n, 7)),
                        pltpu.SemaphoreType.DMA((n,))],
    )(*arrs)


def _pair_exchange(arrs, name):
    n = len(arrs)

    def body(*refs):
        ins = refs[:n]
        outs = refs[n:2 * n]
        send_sems, recv_sems = refs[2 * n:]
        x, y, c = _mesh_pos()
        copies = []
        for a in range(n):
            for q in range(4):
                cp = _rcopy(ins[a].at[2 * q + (1 - c)], outs[a].at[q], send_sems.at[a, q], recv_sems.at[a, q],
                            (x, y, 1 - c))
                cp.start()
                copies.append(cp)
        for cp in copies:
            cp.wait_recv()
        for cp in copies:
            cp.wait_send()

    return pl.pallas_call(
        body, name=name,
        out_shape=tuple(jax.ShapeDtypeStruct((4,) + a.shape[1:], a.dtype) for a in arrs),
        in_specs=[_ANY] * n,
        out_specs=tuple([_ANY] * n),
        scratch_shapes=[pltpu.SemaphoreType.DMA((n, 4)), pltpu.SemaphoreType.DMA((n, 4))],
    )(*arrs)


def _pair_add_call(parts, recv, c_idx, name):
    _, R, C = parts.shape
    (tr, tc), (steps,), idx = _tiling_2d(R, C)

    def body(c_ref, p_ref, r_ref, o_ref):
        o_ref[...] = (p_ref[...].astype(F32) + r_ref[...].astype(F32)).astype(o_ref.dtype)

    return pl.pallas_call(
        body, name=name,
        grid_spec=pltpu.PrefetchScalarGridSpec(
            num_scalar_prefetch=1,
            grid=(4, steps),
            in_specs=[pl.BlockSpec((None, tr, tc), lambda q, i, c_ref: (2 * q + c_ref[0],) + idx(i)),
                      pl.BlockSpec((None, tr, tc), lambda q, i, c_ref: (q,) + idx(i))],
            out_specs=pl.BlockSpec((None, tr, tc), lambda q, i, c_ref: (q,) + idx(i))),
        out_shape=jax.ShapeDtypeStruct((4, R, C), parts.dtype),
        compiler_params=_cparams(("arbitrary", "arbitrary")),
    )(c_idx, parts, recv)


def _chip_exchange(arrs, name):
    n = len(arrs)

    def body(*refs):
        ins = refs[:n]
        outs = refs[n:2 * n]
        send_sems, recv_sems, loc_sems = refs[2 * n:]
        x, y, c = _mesh_pos()
        mine = [pltpu.make_async_copy(ins[a].at[2 * x + y], outs[a].at[3], loc_sems.at[a]) for a in range(n)]
        for cp in mine:
            cp.start()
        copies = []
        for j, (px, py) in enumerate(_other_chips(x, y)):
            for a in range(n):
                cp = _rcopy(ins[a].at[2 * px + py], outs[a].at[j], send_sems.at[a, j], recv_sems.at[a, j],
                            (px, py, c))
                cp.start()
                copies.append(cp)
        for cp in copies:
            cp.wait_recv()
        for cp in copies:
            cp.wait_send()
        for cp in mine:
            cp.wait()

    return pl.pallas_call(
        body, name=name,
        out_shape=tuple(jax.ShapeDtypeStruct(a.shape, a.dtype) for a in arrs),
        in_specs=[_ANY] * n,
        out_specs=tuple([_ANY] * n),
        scratch_shapes=[pltpu.SemaphoreType.DMA((n, 3)), pltpu.SemaphoreType.DMA((n, 3)),
                        pltpu.SemaphoreType.DMA((n,))],
    )(*arrs)


def _proj_call(x, norm_g, w3, wr):
    T, D = x.shape
    tm = min(512, T)
    assert tm % TBLK == 0

    def f_slot(j):
        return ((j >= 2).astype(jnp.int32) + (j >= 6).astype(jnp.int32)
                + (j >= 7).astype(jnp.int32) + (j >= 8).astype(jnp.int32))

    def b_slot(j):
        return (j >= 3).astype(jnp.int32) + (j >= 4).astype(jnp.int32) + (j >= 5).astype(jnp.int32)

    def body(x_ref, g_ref, w_ref, wr_ref, pf_ref, pb_ref, rank_ref, ht_ref, h_scr):
        j = pl.program_id(1)

        @pl.when(j == 0)
        def _():
            xv = x_ref[...]
            r = lax.rsqrt(jnp.mean(xv * xv, axis=-1, keepdims=True) + EPS)
            h = (xv * r) * g_ref[...]
            hb = _bf(h)
            h_scr[...] = hb
            for b in range(tm // TBLK):
                ht_ref[b] = _bf(h[b * TBLK:(b + 1) * TBLK].T)
            rank_ref[...] = _dot_nt(hb, wr_ref[...])

        is_b = (j == 1) | ((j >= 3) & (j <= 5))

        @pl.when(is_b)
        def _():
            pb_ref[...] = _bf(_dot_nt(h_scr[...], w_ref[...]))

        @pl.when(jnp.logical_not(is_b))
        def _():
            pf_ref[...] = _dot_nt(h_scr[...], w_ref[...])

    return pl.pallas_call(
        body, name="proj",
        grid=(T // tm, N_GROUPS),
        in_specs=[pl.BlockSpec((tm, D), lambda i, j: (i, 0)),
                  pl.BlockSpec((1, D), lambda i, j: (0, 0)),
                  pl.BlockSpec((None, 1024, D), lambda i, j: (j, 0, 0)),
                  pl.BlockSpec((128, D), lambda i, j: (0, 0))],
        out_specs=(pl.BlockSpec((None, tm, 1024), lambda i, j: (f_slot(j), i, 0)),
                   pl.BlockSpec((None, tm, 1024), lambda i, j: (b_slot(j), i, 0)),
                   pl.BlockSpec((tm, 128), lambda i, j: (i, 0)),
                   pl.BlockSpec((tm // TBLK, D, TBLK), lambda i, j: (i, 0, 0))),
        out_shape=(jax.ShapeDtypeStruct((5, T, 1024), F32),
                   jax.ShapeDtypeStruct((4, T, 1024), BF16),
                   jax.ShapeDtypeStruct((T, 128), F32),
                   jax.ShapeDtypeStruct((T // TBLK, D, TBLK), BF16)),
        scratch_shapes=[pltpu.VMEM((tm, D), BF16)],
        compiler_params=_cparams(("arbitrary", "arbitrary")),
    )(x, norm_g, w3, wr)


def _gla_chunk_terms(la_h, q, k):
    C = GLA_CHUNK
    low = _bf((_iota2(C, C, 0) >= _iota2(C, C, 1)).astype(F32))
    b = _tri_left(low, la_h)
    bl = b[C - 1:C, :]
    eb = jnp.exp(b)
    enb = jnp.exp(-b)
    ebl_b = jnp.exp(bl - b)
    scale = GLA_HK ** -0.5
    qe = q * eb * scale
    ke = k * enb
    kd = k * ebl_b
    return b, bl, eb, enb, ebl_b, qe, ke, kd


def _gla_fwd_call(projf, projb, rank, wdec, bdec):
    T = projf.shape[1]
    C = GLA_CHUNK
    n_chunks = T // C

    def body(qk_ref, v_ref, rank_ref, wd_ref, bd_ref, o_ref, st_ref, la_ref, st_scr):
        @pl.when(pl.program_id(0) == 0)
        def _():
            st_scr[...] = jnp.zeros_like(st_scr)

        dec = _dot(_bf(rank_ref[...]), _bf(wd_ref[...])) + bd_ref[...]
        la = (jnp.minimum(dec, 0.0) - _softplus_neg_abs(dec)) / GLA_TAU
        la_ref[...] = la
        mask = _iota2(C, C, 0) >= _iota2(C, C, 1)
        for hh in range(GLA_HEADS):
            la_h = la[:, hh * GLA_HK:(hh + 1) * GLA_HK]
            q = qk_ref[:, hh * GLA_HK:(hh + 1) * GLA_HK]
            k = qk_ref[:, GLA_DK + hh * GLA_HK:GLA_DK + (hh + 1) * GLA_HK]
            v = _bf(v_ref[:, hh * GLA_HV:(hh + 1) * GLA_HV])
            _, bl, _, _, _, qe, ke, kd = _gla_chunk_terms(la_h, q, k)
            st = st_scr[hh]
            st_ref[hh] = st
            p = jnp.where(mask, _dot_nt(_bf(qe), _bf(ke)), 0.0)
            o = _dot(_bf(p), v) + _dot_nt(_bf(qe), _bf(st))
            o_ref[:, hh * GLA_HV:(hh + 1) * GLA_HV] = o
            st_scr[hh] = st * jnp.exp(bl) + _dot_tn(v, _bf(kd))

    return pl.pallas_call(
        body, name="gla_fwd",
        grid=(n_chunks,),
        in_specs=[pl.BlockSpec((None, C, 1024), lambda n: (0, n, 0)),
                  pl.BlockSpec((None, C, 1024), lambda n: (0, n, 0)),
                  pl.BlockSpec((C, 128), lambda n: (n, 0)),
                  pl.BlockSpec((128, GLA_DK), lambda n: (0, 0)),
                  pl.BlockSpec((1, GLA_DK), lambda n: (0, 0))],
        out_specs=(pl.BlockSpec((C, 1024), lambda n: (n, 0)),
                   pl.BlockSpec((None, GLA_HEADS, GLA_HV, GLA_HK), lambda n: (n, 0, 0, 0)),
                   pl.BlockSpec((C, GLA_DK), lambda n: (n, 0))),
        out_shape=(jax.ShapeDtypeStruct((T, 1024), F32),
                   jax.ShapeDtypeStruct((n_chunks, GLA_HEADS, GLA_HV, GLA_HK), F32),
                   jax.ShapeDtypeStruct((T, GLA_DK), F32)),
        scratch_shapes=[pltpu.VMEM((GLA_HEADS, GLA_HV, GLA_HK), F32)],
        compiler_params=_cparams(("arbitrary",)),
    )(projf, projb, rank, wdec, bdec)


def _gla_bwd_call(projf, projb, la, do_gla, st_all, rank, wdec):
    T = projf.shape[1]
    C = GLA_CHUNK
    n_chunks = T // C
    last = n_chunks - 1

    def body(qk_ref, v_ref, la_ref, do_ref, st_ref, rank_ref, wd_ref,
             dqk_ref, dv_ref, drank_ref, dwd_ref, dbd_ref, dst_scr):
        @pl.when(pl.program_id(0) == 0)
        def _():
            dst_scr[...] = jnp.zeros_like(dst_scr)
            dwd_ref[...] = jnp.zeros_like(dwd_ref)
            dbd_ref[...] = jnp.zeros_like(dbd_ref)

        mask = _iota2(C, C, 0) >= _iota2(C, C, 1)
        upp = _bf((_iota2(C, C, 0) <= _iota2(C, C, 1)).astype(F32))
        scale = GLA_HK ** -0.5
        la = la_ref[...]
        ddec_parts = []
        for hh in range(GLA_HEADS):
            la_h = la[:, hh * GLA_HK:(hh + 1) * GLA_HK]
            q = qk_ref[:, hh * GLA_HK:(hh + 1) * GLA_HK]
            k = qk_ref[:, GLA_DK + hh * GLA_HK:GLA_DK + (hh + 1) * GLA_HK]
            v = _bf(v_ref[:, hh * GLA_HV:(hh + 1) * GLA_HV])
            do = _bf(do_ref[:, hh * GLA_HV:(hh + 1) * GLA_HV])
            _, bl, eb, enb, ebl_b, qe, ke, kd = _gla_chunk_terms(la_h, q, k)
            qeb, keb, kdb = _bf(qe), _bf(ke), _bf(kd)
            st = st_ref[hh]
            dstn = dst_scr[hh]
            dstnb = _bf(dstn)
            ebl = jnp.exp(bl)
            p = jnp.where(mask, _dot_nt(qeb, keb), 0.0)
            dp = _bf(jnp.where(mask, _dot_nt(do, v), 0.0))
            dv = _dot_tn(_bf(p), do) + _dot_nt(kdb, dstnb)
            dqe = _dot(dp, keb) + _dot(do, _bf(st))
            dke = _dot_tn(dp, qeb)
            dkd = _dot(v, dstnb)
            dst_scr[hh] = _dot_tn(do, qeb) + dstn * ebl
            debl = jnp.sum(dstn * st, axis=0, keepdims=True)
            dkd_kd = dkd * kd
            db = dqe * qe - dke * ke - dkd_kd
            dbl = jnp.sum(dkd_kd, axis=0, keepdims=True) + ebl * debl
            dla = _tri_left(upp, db) + dbl
            dq = dqe * eb * scale
            dk = dke * enb + dkd * ebl_b
            dqk_ref[:, hh * GLA_HK:(hh + 1) * GLA_HK] = _bf(dq)
            dqk_ref[:, GLA_DK + hh * GLA_HK:GLA_DK + (hh + 1) * GLA_HK] = _bf(dk)
            dv_ref[:, hh * GLA_HV:(hh + 1) * GLA_HV] = _bf(dv)
            ddec_parts.append(dla * (1.0 / GLA_TAU) * (1.0 - jnp.exp(GLA_TAU * la_h)))
        ddec = jnp.concatenate(ddec_parts, axis=1)
        ddecb = _bf(ddec)
        drank_ref[...] = _bf(_dot_nt(ddecb, _bf(wd_ref[...])))
        dwd_ref[...] += _dot_tn(_bf(rank_ref[...]), ddecb)
        dbd_ref[...] += jnp.sum(ddec, axis=0, keepdims=True)

    return pl.pallas_call(
        body, name="gla_bwd",
        grid=(n_chunks,),
        in_specs=[pl.BlockSpec((None, C, 1024), lambda n: (0, last - n, 0)),
                  pl.BlockSpec((None, C, 1024), lambda n: (0, last - n, 0)),
                  pl.BlockSpec((C, GLA_DK), lambda n: (last - n, 0)),
                  pl.BlockSpec((C, 1024), lambda n: (last - n, 0)),
                  pl.BlockSpec((None, GLA_HEADS, GLA_HV, GLA_HK), lambda n: (last - n, 0, 0, 0)),
                  pl.BlockSpec((C, 128), lambda n: (last - n, 0)),
                  pl.BlockSpec((128, GLA_DK), lambda n: (0, 0))],
        out_specs=(pl.BlockSpec((C, 1024), lambda n: (last - n, 0)),
                   pl.BlockSpec((C, 1024), lambda n: (last - n, 0)),
                   pl.BlockSpec((C, 128), lambda n: (last - n, 0)),
                   pl.BlockSpec((128, GLA_DK), lambda n: (0, 0)),
                   pl.BlockSpec((1, GLA_DK), lambda n: (0, 0))),
        out_shape=(jax.ShapeDtypeStruct((T, 1024), BF16),
                   jax.ShapeDtypeStruct((T, 1024), BF16),
                   jax.ShapeDtypeStruct((T, 128), BF16),
                   jax.ShapeDtypeStruct((128, GLA_DK), F32),
                   jax.ShapeDtypeStruct((1, GLA_DK), F32)),
        scratch_shapes=[pltpu.VMEM((GLA_HEADS, GLA_HV, GLA_HK), F32)],
        compiler_params=_cparams(("arbitrary",)),
    )(projf, projb, la, do_gla, st_all, rank, wdec)


def _sb_logs(z):
    lsz = jnp.minimum(z, 0.0) - _softplus_neg_abs(z)
    return lsz, lsz - z


SB_HG_FWD = 8
SB_HG_BWD = 4
SB_KEYS = 256


def _sb_fwd_call(projb):
    T = projb.shape[1]
    B = SB_BLOCK
    HG = SB_HG_FWD
    W = HG * SB_HD
    scale = 1.0 / math.sqrt(SB_HD)

    KB = min(SB_KEYS, T)

    def body(q_ref, k_ref, v_ref, o_ref, cb_scr):
        i = pl.program_id(1)
        rows = HG * B
        after = (_iota2(KB, KB, 0) > _iota2(KB, KB, 1)).astype(F32)
        tri = _bf(jnp.concatenate([after, jnp.ones((KB, KB), F32)], axis=1))
        o_ref[...] = jnp.zeros_like(o_ref)
        cb_scr[...] = jnp.zeros_like(cb_scr)

        def block(jp, masked):
            off = pl.multiple_of(jp * KB, KB)
            z = jnp.concatenate(
                [_dot_nt(q_ref[:, hh * SB_HD:(hh + 1) * SB_HD], k_ref[pl.ds(off, KB), hh * SB_HD:(hh + 1) * SB_HD])
                 for hh in range(HG)], axis=0) * scale
            lsz, l1m = _sb_logs(z)
            if masked:
                strict = (jp * KB + _iota2(rows, KB, 1)) < (i * B + (_iota2(rows, KB, 0) & (B - 1)))
                l1m = jnp.where(strict, l1m, 0.0)
            r = _tri2_right(l1m, tri)
            cb = cb_scr[...]
            a = jnp.exp(lsz + cb + r[:, :KB])
            if masked:
                a = jnp.where(strict, a, 0.0)
            cb_scr[...] = cb + r[:, KB:]
            ab = _bf(a)
            for hh in range(HG):
                cs = slice(hh * SB_HD, (hh + 1) * SB_HD)
                o_ref[:, cs] += _dot(ab[hh * B:(hh + 1) * B, :], v_ref[pl.ds(off, KB), cs])

        jp0 = (i * B) // KB
        block(jp0, True)

        def step(jj, c):
            block(jp0 - jj, False)
            return c

        lax.fori_loop(1, jp0 + 1, step, 0)

    return pl.pallas_call(
        body, name="sb_fwd",
        grid=(SB_HEADS // HG, T // B),
        in_specs=[pl.BlockSpec((None, B, W), lambda h, i: (1, i, h)),
                  pl.BlockSpec((None, T, W), lambda h, i: (2, 0, h)),
                  pl.BlockSpec((None, T, W), lambda h, i: (3, 0, h))],
        out_specs=pl.BlockSpec((B, W), lambda h, i: (i, h)),
        out_shape=jax.ShapeDtypeStruct((T, 1024), F32),
        scratch_shapes=[pltpu.VMEM((HG * B, KB), F32)],
        compiler_params=_cparams(("arbitrary", "arbitrary")),
    )(projb, projb, projb)


def _sb_bwd_call(projb, do_sb):
    T = projb.shape[1]
    B = SB_BLOCK
    nb = T // B
    HG = SB_HG_BWD
    W = HG * SB_HD
    KB = min(SB_KEYS, T)
    nkb = T // KB
    scale = 1.0 / math.sqrt(SB_HD)

    def body(q_ref, k_ref, v_ref, do_ref, dq_ref, dk_ref, dv_ref,
             dk_scr, dv_scr, kt_scr, beta_scr, g_scr, dqt_scr):
        i = pl.program_id(1)

        @pl.when(i == 0)
        def _():
            dk_scr[...] = jnp.zeros_like(dk_scr)
            dv_scr[...] = jnp.zeros_like(dv_scr)
            for hh in range(HG):
                for jb in range(nkb):
                    kt_scr[hh, jb] = _bf(
                        k_ref[jb * KB:(jb + 1) * KB, hh * SB_HD:(hh + 1) * SB_HD].astype(F32).T)

        dqt_scr[...] = jnp.zeros_like(dqt_scr)
        later = _bf((_iota2(KB, KB, 1) > _iota2(KB, KB, 0)).astype(F32))
        earlier = _bf((_iota2(KB, KB, 1) < _iota2(KB, KB, 0)).astype(F32))
        dob = _bf(do_ref[...])
        jp0 = (i * B) // KB

        def strict_mask():
            return (jp0 * KB + _iota2(KB, W, 0)) < (i * B + (_iota2(KB, W, 1) & (B - 1)))

        def heads(fn):
            return [fn(slice(hh * SB_HD, (hh + 1) * SB_HD)) for hh in range(HG)]

        def pass1(jp, cb, masked):
            off = pl.multiple_of(jp * KB, KB)
            z = jnp.concatenate(heads(lambda cs: _dot_nt(k_ref[pl.ds(off, KB), cs], q_ref[:, cs])), axis=1) * scale
            da = jnp.concatenate(heads(lambda cs: _dot_nt(v_ref[pl.ds(off, KB), cs], dob[:, cs])), axis=1)
            lsz, l1m = _sb_logs(z)
            if masked:
                strict = strict_mask()
                l1m = jnp.where(strict, l1m, 0.0)
            a = jnp.exp(lsz + cb + _tri2_left(later, l1m))
            if masked:
                a = jnp.where(strict, a, 0.0)
            g_scr[jp] = a * da
            beta_scr[jp] = jnp.exp(lsz)
            ab = _bf(a)
            for hh in range(HG):
                cs = slice(hh * SB_HD, (hh + 1) * SB_HD)
                dv_scr[pl.ds(off, KB), cs] += _dot(ab[:, cs], dob[:, cs])
            return cb + jnp.sum(l1m, axis=0, keepdims=True)

        zero = jnp.zeros((1, W), F32)
        cb = pass1(jp0, zero, True)
        lax.fori_loop(1, jp0 + 1, lambda jj, cr: pass1(jp0 - jj, cr, False), cb)

        def pass2(jp, cg, masked):
            off = pl.multiple_of(jp * KB, KB)
            g = g_scr[jp]
            beta = beta_scr[jp]
            dz = g * (1.0 - beta) - beta * (cg + _tri2_left(earlier, g))
            if masked:
                dz = jnp.where(strict_mask(), dz, 0.0)
            dzb = _bf(dz * scale)
            for hh in range(HG):
                cs = slice(hh * SB_HD, (hh + 1) * SB_HD)
                dk_scr[pl.ds(off, KB), cs] += _dot(dzb[:, cs], q_ref[:, cs])
                dqt_scr[hh] += _dot(kt_scr[hh, jp], dzb[:, cs])
            return cg + jnp.sum(g, axis=0, keepdims=True)

        cg = lax.fori_loop(0, jp0, lambda jp, cr: pass2(jp, cr, False), zero)
        pass2(jp0, cg, True)
        for hh in range(HG):
            dq_ref[:, hh * SB_HD:(hh + 1) * SB_HD] = _bf(dqt_scr[hh].T)

        @pl.when(i == nb - 1)
        def _():
            dk_ref[...] = _bf(dk_scr[...])
            dv_ref[...] = _bf(dv_scr[...])

    return pl.pallas_call(
        body, name="sb_bwd",
        grid=(SB_HEADS // HG, nb),
        in_specs=[pl.BlockSpec((None, B, W), lambda h, i: (1, i, h)),
                  pl.BlockSpec((None, T, W), lambda h, i: (2, 0, h)),
                  pl.BlockSpec((None, T, W), lambda h, i: (3, 0, h)),
                  pl.BlockSpec((B, W), lambda h, i: (i, h))],
        out_specs=(pl.BlockSpec((B, W), lambda h, i: (i, h)),
                   pl.BlockSpec((T, W), lambda h, i: (0, h)),
                   pl.BlockSpec((T, W), lambda h, i: (0, h))),
        out_shape=(jax.ShapeDtypeStruct((T, 1024), BF16),
                   jax.ShapeDtypeStruct((T, 1024), BF16),
                   jax.ShapeDtypeStruct((T, 1024), BF16)),
        scratch_shapes=[pltpu.VMEM((T, W), F32), pltpu.VMEM((T, W), F32),
                        pltpu.VMEM((HG, nkb, SB_HD, KB), BF16),
                        pltpu.VMEM((nkb, KB, W), F32), pltpu.VMEM((nkb, KB, W), F32),
                        pltpu.VMEM((HG, SB_HD, B), F32)],
        compiler_params=_cparams(("arbitrary", "arbitrary")),
    )(projb, projb, projb, do_sb)


def _mid_call(o_gla, o_sb, projf, x, target, wpa, wpb, wo, gla_g, b_gate, final_g):
    T, D = x.shape
    tm = min(TBLK, T)

    def body(og_ref, ggate_ref, osb_ref, sgate_ref, ma_ref, mb_ref, x_ref, tgt_ref,
             wpa_ref, wpb_ref, wo_ref, glag_ref, bg_ref, fg_ref,
             dx2_ref, dogla_ref, dosb_ref, dggate_ref, dsgate_ref, dm_ref,
             mt_ref, ogt_ref, obt_ref, dx2b_ref, dya_ref, dyb_ref,
             dfg_ref, dbg_ref, dglag_ref, loss_ref):
        @pl.when(pl.program_id(0) == 0)
        def _():
            dfg_ref[...] = jnp.zeros_like(dfg_ref)
            dbg_ref[...] = jnp.zeros_like(dbg_ref)
            dglag_ref[...] = jnp.zeros_like(dglag_ref)
            loss_ref[...] = jnp.zeros_like(loss_ref)

        glag = glag_ref[...]
        ggate = ggate_ref[...]
        sg = _sigmoid(ggate)
        silu_g = ggate * sg
        ohat, rinv, nrm = [], [], []
        for hh in range(GLA_HEADS):
            oh = og_ref[:, hh * GLA_HV:(hh + 1) * GLA_HV]
            r = lax.rsqrt(jnp.mean(oh * oh, axis=-1, keepdims=True) + EPS)
            ohat.append(oh * r)
            rinv.append(r)
            nrm.append(ohat[-1] * glag)
        n_all = jnp.concatenate(nrm, axis=1)
        og = n_all * silu_g
        ogb = _bf(og)
        ya = _dot(ogb, wpa_ref[...])
        sgate = sgate_ref[...]
        ss = _sigmoid(sgate)
        silu_s = sgate * ss
        osb = osb_ref[...]
        ob = osb * silu_s
        obb = _bf(ob)
        yb = _dot(obb, wpb_ref[...])
        ga = _sigmoid(ma_ref[...] + bg_ref[:, :D])
        gb = _sigmoid(mb_ref[...] + bg_ref[:, D:])
        merged = ga * ya + gb * yb
        mgb = _bf(merged)
        x2 = x_ref[...] + _dot(mgb, wo_ref[...])
        r2 = lax.rsqrt(jnp.mean(x2 * x2, axis=-1, keepdims=True) + EPS)
        xh2 = x2 * r2
        fg = fg_ref[...]
        err = xh2 * fg - tgt_ref[...]
        loss_ref[...] += jnp.broadcast_to(
            0.5 * jnp.sum(jnp.mean(err * err, axis=-1, keepdims=True), axis=0, keepdims=True), (1, 128))
        dy = err * (1.0 / D)
        dfg_ref[...] += jnp.sum(dy * xh2, axis=0, keepdims=True)
        dxh = dy * fg
        dx2 = r2 * (dxh - xh2 * jnp.mean(dxh * xh2, axis=-1, keepdims=True))
        dx2_ref[...] = dx2
        dx2b = _bf(dx2)
        dx2b_ref[...] = dx2b
        dmerged = _dot_nt(dx2b, wo_ref[...])
        dya = dmerged * ga
        dyb = dmerged * gb
        dma = dmerged * ya * ga * (1.0 - ga)
        dmb = dmerged * yb * gb * (1.0 - gb)
        dm_ref[:, :D] = _bf(dma)
        dm_ref[:, D:] = _bf(dmb)
        dbg_ref[:, :D] += jnp.sum(dma, axis=0, keepdims=True)
        dbg_ref[:, D:] += jnp.sum(dmb, axis=0, keepdims=True)
        dyab = _bf(dya)
        dybb = _bf(dyb)
        dya_ref[...] = dyab
        dyb_ref[...] = dybb
        dog = _dot_nt(dyab, wpa_ref[...])
        dob = _dot_nt(dybb, wpb_ref[...])
        dosb_ref[...] = dob * silu_s
        dsgate_ref[...] = _bf(dob * osb * (ss * (1.0 + sgate * (1.0 - ss))))
        dn = dog * silu_g
        dggate_ref[...] = _bf(dog * n_all * (sg * (1.0 + ggate * (1.0 - sg))))
        dglag = jnp.zeros((1, GLA_HV), F32)
        for hh in range(GLA_HEADS):
            dnh = dn[:, hh * GLA_HV:(hh + 1) * GLA_HV]
            dglag = dglag + jnp.sum(dnh * ohat[hh], axis=0, keepdims=True)
            dohat = dnh * glag
            dogla_ref[:, hh * GLA_HV:(hh + 1) * GLA_HV] = rinv[hh] * (
                dohat - ohat[hh] * jnp.mean(dohat * ohat[hh], axis=-1, keepdims=True))
        dglag_ref[...] += dglag
        mt_ref[...] = _bf(merged.T)
        ogt_ref[...] = _bf(og.T)
        obt_ref[...] = _bf(ob.T)

    row = lambda i: (i, 0)
    const = lambda i: (0, 0)
    tile = pl.BlockSpec((tm, D), row)
    tile_t = pl.BlockSpec((None, D, tm), lambda i: (i, 0, 0))
    wspec = pl.BlockSpec((D, D), const)
    return pl.pallas_call(
        body, name="mid",
        grid=(T // tm,),
        in_specs=[tile,
                  pl.BlockSpec((None, tm, D), lambda i: (1, i, 0)),
                  tile,
                  pl.BlockSpec((None, tm, D), lambda i: (2, i, 0)),
                  pl.BlockSpec((None, tm, D), lambda i: (3, i, 0)),
                  pl.BlockSpec((None, tm, D), lambda i: (4, i, 0)),
                  tile, tile, wspec, wspec, wspec,
                  pl.BlockSpec((1, GLA_HV), const),
                  pl.BlockSpec((1, 2 * D), const),
                  pl.BlockSpec((1, D), const)],
        out_specs=(tile, tile, tile, tile, tile,
                   pl.BlockSpec((tm, 2 * D), row),
                   tile_t, tile_t, tile_t, tile, tile, tile,
                   pl.BlockSpec((1, D), const),
                   pl.BlockSpec((1, 2 * D), const),
                   pl.BlockSpec((1, GLA_HV), const),
                   pl.BlockSpec((1, 128), const)),
        out_shape=(jax.ShapeDtypeStruct((T, D), F32),
                   jax.ShapeDtypeStruct((T, D), F32),
                   jax.ShapeDtypeStruct((T, D), F32),
                   jax.ShapeDtypeStruct((T, D), BF16),
                   jax.ShapeDtypeStruct((T, D), BF16),
                   jax.ShapeDtypeStruct((T, 2 * D), BF16),
                   jax.ShapeDtypeStruct((T // tm, D, tm), BF16),
                   jax.ShapeDtypeStruct((T // tm, D, tm), BF16),
                   jax.ShapeDtypeStruct((T // tm, D, tm), BF16),
                   jax.ShapeDtypeStruct((T, D), BF16),
                   jax.ShapeDtypeStruct((T, D), BF16),
                   jax.ShapeDtypeStruct((T, D), BF16),
                   jax.ShapeDtypeStruct((1, D), F32),
                   jax.ShapeDtypeStruct((1, 2 * D), F32),
                   jax.ShapeDtypeStruct((1, GLA_HV), F32),
                   jax.ShapeDtypeStruct((1, 128), F32)),
        compiler_params=_cparams(("arbitrary",)),
    )(o_gla, projf, o_sb, projf, projf, projf, x, target, wpa, wpb, wo, gla_g, b_gate, final_g)


def _dh_call(pieces, dmlog, drank, w3, wr, x, dx2, norm_g):
    T, D = x.shape
    tm = min(256, T)
    npc = len(pieces)

    def body(*refs):
        pcs = refs[:npc]
        (dm_ref, dr_ref, w_hbm, wr_ref, x_ref, dx2_ref, g_ref,
         gx_ref, dg_ref, dwr_ref, w_scr, sem) = refs[npc:]

        @pl.when(pl.program_id(0) == 0)
        def _():
            cp = pltpu.make_async_copy(w_hbm, w_scr, sem)
            cp.start()
            cp.wait()
            dg_ref[...] = jnp.zeros_like(dg_ref)
            dwr_ref[...] = jnp.zeros_like(dwr_ref)

        dr = dr_ref[...]
        dh = _dot(dr, wr_ref[...])
        for g in range(npc):
            dh = dh + _dot(pcs[g][...], w_scr[g])
        dh = dh + _dot(dm_ref[:, :D], w_scr[npc])
        dh = dh + _dot(dm_ref[:, D:], w_scr[npc + 1])
        xv = x_ref[...]
        r = lax.rsqrt(jnp.mean(xv * xv, axis=-1, keepdims=True) + EPS)
        xhat = xv * r
        g = g_ref[...]
        dg_ref[...] += jnp.sum(dh * xhat, axis=0, keepdims=True)
        dxhat = dh * g
        gx_ref[...] = r * (dxhat - xhat * jnp.mean(dxhat * xhat, axis=-1, keepdims=True)) + dx2_ref[...]
        dwr_ref[...] += _dot_tn(dr, _bf(xhat * g))

    row = lambda i: (i, 0)
    const = lambda i: (0, 0)
    tile = pl.BlockSpec((tm, D), row)
    return pl.pallas_call(
        body, name="dh",
        grid=(T // tm,),
        in_specs=[tile] * npc + [
            pl.BlockSpec((tm, 2 * D), row),
            pl.BlockSpec((tm, 128), row),
            pl.BlockSpec(memory_space=pl.ANY),
            pl.BlockSpec((128, D), const),
            tile, tile,
            pl.BlockSpec((1, D), const)],
        out_specs=(tile, pl.BlockSpec((1, D), const), pl.BlockSpec((128, D), const)),
        out_shape=(jax.ShapeDtypeStruct((T, D), F32),
                   jax.ShapeDtypeStruct((1, D), F32),
                   jax.ShapeDtypeStruct((128, D), F32)),
        scratch_shapes=[pltpu.VMEM((N_GROUPS, 1024, D), BF16), pltpu.SemaphoreType.DMA],
        compiler_params=_cparams(("arbitrary",)),
    )(*pieces, dmlog, drank, w3, wr, x, dx2, norm_g)


def _wgrad_call(lhs_list, lhs_of_group, rhs_list, rhs_of_group, n_transposed):
    n_groups = len(rhs_of_group)
    n_tb, D, tb = lhs_list[0].shape
    T = n_tb * tb
    per = min(2, n_tb)
    tk = per * tb
    nk = T // tk
    nl = len(lhs_list)

    def body(*refs):
        lhs = refs[:nl]
        rhs = refs[nl:nl + n_groups]
        out_ref, acc = refs[nl + n_groups:]
        g = pl.program_id(0)
        i = pl.program_id(1)

        @pl.when(i == 0)
        def _():
            acc[...] = jnp.zeros_like(acc)

        for p in range(n_groups):
            @pl.when(g == p)
            def _(p=p):
                lref = lhs[lhs_of_group[p]]
                part = _dot(lref[0], rhs[p][0:tb, :])
                for b in range(1, per):
                    part = part + _dot(lref[b], rhs[p][b * tb:(b + 1) * tb, :])
                acc[...] += part

        @pl.when((i == nk - 1) & (g < n_transposed))
        def _():
            out_ref[...] = _bf(acc[...].T)

        @pl.when((i == nk - 1) & (g >= n_transposed))
        def _():
            out_ref[...] = _bf(acc[...])

    def lhs_spec(a):
        groups = [g for g in range(n_groups) if lhs_of_group[g] == a]
        lo, hi = min(groups), max(groups)
        assert groups == list(range(lo, hi + 1))
        return pl.BlockSpec((per, D, tb), lambda g, i: (jnp.where((g >= lo) & (g <= hi), i, 0), 0, 0))

    def rhs_spec(p):
        cb = rhs_of_group[p][1]
        return pl.BlockSpec((tk, 1024), lambda g, i: (jnp.where(g == p, i, 0), cb))

    return pl.pallas_call(
        body, name="wgrad",
        grid=(n_groups, nk),
        in_specs=[lhs_spec(a) for a in range(nl)] + [rhs_spec(p) for p in range(n_groups)],
        out_specs=pl.BlockSpec((None, D, 1024), lambda g, i: (g, 0, 0)),
        out_shape=jax.ShapeDtypeStruct((n_groups, D, 1024), BF16),
        scratch_shapes=[pltpu.VMEM((D, 1024), F32)],
        compiler_params=_cparams(("arbitrary", "arbitrary")),
    )(*lhs_list, *[rhs_list[rhs_of_group[p][0]] for p in range(n_groups)])


def _adamw_call(parts, w, m, v, name):
    R, C = w.shape
    n_parts = parts.shape[0]
    (tr, tc), grid, idx = _tiling_2d(R, C)

    def body(p_ref, w_ref, m_ref, v_ref, g_ref, d_ref, nm_ref, nv_ref):
        g = p_ref[n_parts - 1].astype(F32)
        for k in range(n_parts - 1):
            g = g + p_ref[k].astype(F32)
        mm = ADAM_B1 * m_ref[...] + (1.0 - ADAM_B1) * g
        vv = ADAM_B2 * v_ref[...] + (1.0 - ADAM_B2) * (g * g)
        m_hat = mm / (1.0 - ADAM_B1 ** ADAM_STEP)
        v_hat = vv / (1.0 - ADAM_B2 ** ADAM_STEP)
        d_ref[...] = -ADAM_LR * (m_hat / (jnp.sqrt(v_hat) + ADAM_EPS) + ADAM_WD * w_ref[...])
        g_ref[...] = g
        nm_ref[...] = mm
        nv_ref[...] = vv

    blk = pl.BlockSpec((tr, tc), idx)
    sds = jax.ShapeDtypeStruct((R, C), F32)
    return pl.pallas_call(
        body, name=name,
        grid=grid,
        in_specs=[pl.BlockSpec((n_parts, tr, tc), lambda i: (0,) + idx(i)), blk, blk, blk],
        out_specs=(blk, blk, blk, blk),
        out_shape=(sds, sds, sds, sds),
        compiler_params=_cparams(("arbitrary",)),
    )(parts, w, m, v)


def _local_step(x, target, w3, wr, wdec, bdec, wpa, wpb, wo, norm_g, gla_g, b_gate, final_g):
    projf, projb, rank, ht = _proj_call(x, norm_g, w3, wr)
    o_gla, st_all, la = _gla_fwd_call(projf, projb, rank, wdec, bdec)
    o_sb = _sb_fwd_call(projb)
    (dx2, do_gla, do_sb, dggate, dsgate, dmlog, mt, ogt, obt, dx2b, dya, dyb,
     dfinal_g, db_gate, dgla_g, loss) = _mid_call(o_gla, o_sb, projf, x, target, wpa, wpb, wo,
                                                 gla_g, b_gate, final_g)
    dsq, dsk, dsv = _sb_bwd_call(projb, do_sb)
    dqk, dgv, drank, dwdec, dbdec = _gla_bwd_call(projf, projb, la, do_gla, st_all, rank, wdec)
    pieces = [dqk, dgv, dggate, dsq, dsk, dsv, dsgate]
    grad_x, dnorm_g, dwr = _dh_call(pieces, dmlog, drank, w3, wr, x, dx2, norm_g)
    rhs_list = pieces + [dmlog, dx2b, dya, dyb]
    rhs_of_group = [(g, 0) for g in range(7)] + [(7, 0), (7, 1), (8, 0), (9, 0), (10, 0)]
    lhs_of_group = [0] * 9 + [1, 2, 3]
    dw_all = _wgrad_call([ht, mt, ogt, obt], lhs_of_group, rhs_list, rhs_of_group, N_GROUPS)
    return grad_x, dw_all, dwr, dwdec, dbdec, dnorm_g, dgla_g, db_gate, dfinal_g, loss


_SM_NORM = 0
_SM_BDEC = _SM_NORM + D_MODEL
_SM_GLAG = _SM_BDEC + GLA_DK
_SM_BGATE = _SM_GLAG + GLA_HV
_SM_FINAL = _SM_BGATE + 2 * D_MODEL
_SM_REPL = _SM_FINAL + D_MODEL
_SM_LOSS = _SM_REPL
_SM_WDEC = _SM_LOSS + 128
_SM_LEN = _SM_WDEC + GLA_RANK * GLA_DK


def kernel(x, norm_g, w_in, w_dec_up, b_dec, gla_norm_g, w_pa, w_pb, b_gate, w_o, final_g, loss_target, m_norm_g, m_w_in, m_w_dec_up, m_b_dec, m_gla_norm_g, m_w_pa, m_w_pb, m_b_gate, m_w_o, m_final_g, v_norm_g, v_w_in, v_w_dec_up, v_b_dec, v_gla_norm_g, v_w_pa, v_w_pb, v_b_gate, v_w_o, v_final_g):
    D = D_MODEL
    me = 4 * lax.axis_index("x") + 2 * lax.axis_index("y") + lax.axis_index("c")

    wp_shard = jnp.stack([w_pa, w_pb, w_o]).astype(BF16)
    win_all, wp_all, wdec_all = _all_gather([w_in.T.astype(BF16), wp_shard, w_dec_up], "gather_w")
    wt_full = win_all.reshape(IN_COLS, D)
    w3 = jnp.concatenate([wt_full[:RANK_COL], wt_full[RANK_COL + GLA_RANK:]], axis=0).reshape(N_GROUPS, 1024, D)
    wr = jnp.pad(wt_full[RANK_COL:RANK_COL + GLA_RANK], ((0, 128 - GLA_RANK), (0, 0)))
    wp_full = wp_all.transpose(1, 0, 2, 3).reshape(3, D, D)
    wdec_full = wdec_all.transpose(1, 0, 2).reshape(GLA_RANK, GLA_DK)
    wdec = jnp.pad(wdec_full, ((0, 128 - GLA_RANK), (0, 0)))

    (grad_x, dw_all, dwr, dwdec, dbdec, dnorm_g, dgla_g, db_gate, dfinal_g, loss) = _local_step(
        x[0], loss_target[0], w3, wr, wdec, b_dec.reshape(1, -1), wp_full[0], wp_full[1], wp_full[2],
        norm_g.reshape(1, -1), gla_norm_g.reshape(1, -1), b_gate.reshape(1, -1), final_g.reshape(1, -1))

    dmain = dw_all[:N_GROUPS].reshape(N_GROUPS * 1024, D)
    dfull = jnp.concatenate([dmain[:RANK_COL], dwr[:GLA_RANK].astype(BF16), dmain[RANK_COL:]], axis=0)
    g_in = dfull.reshape(N_DEV, SHARD_COLS, D)
    g_p = jnp.stack([dw_all[10], dw_all[11], dw_all[9]]).reshape(3, N_DEV, D // N_DEV, D).transpose(1, 0, 2, 3)
    small = jnp.concatenate([
        dnorm_g.reshape(-1), dbdec.reshape(-1), dgla_g.reshape(-1), db_gate.reshape(-1), dfinal_g.reshape(-1),
        loss.reshape(-1), dwdec[:GLA_RANK].reshape(-1)])
    g_p = g_p.reshape(N_DEV, 3 * (D // N_DEV), D)
    c_idx = lax.axis_index("c").astype(jnp.int32).reshape(1)
    p_in, p_p = _pair_exchange([g_in, g_p], "pair_g")
    s_in = _pair_add_call(g_in, p_in, c_idx, "pair_add_in")
    s_p = _pair_add_call(g_p, p_p, c_idx, "pair_add_p")
    r_in, r_p = _chip_exchange([s_in, s_p], "scatter_g")
    (r_small,) = _all_gather([small.reshape(1, _SM_LEN)], "gather_small")

    gw_in, d_in, nm_in, nv_in = (a.T for a in _adamw_call(r_in, w_in.T, m_w_in.T, v_w_in.T, "adamw_in"))
    wp_f32 = jnp.concatenate([w_pa, w_pb, w_o], axis=0)
    mp = jnp.concatenate([m_w_pa, m_w_pb, m_w_o], axis=0)
    vp = jnp.concatenate([v_w_pa, v_w_pb, v_w_o], axis=0)
    gp, dp, nmp, nvp = _adamw_call(r_p, wp_f32, mp, vp, "adamw_p")
    rows = D // N_DEV

    def split3(a):
        return a[:rows], a[rows:2 * rows], a[2 * rows:]

    g_pa, g_pb, g_o = split3(gp)
    d_pa, d_pb, d_o = split3(dp)
    nm_pa, nm_pb, nm_o = split3(nmp)
    nv_pa, nv_pb, nv_o = split3(nvp)

    w_rep = jnp.concatenate([norm_g, b_dec, gla_norm_g, b_gate, final_g]).reshape(1, _SM_REPL)
    m_rep = jnp.concatenate([m_norm_g, m_b_dec, m_gla_norm_g, m_b_gate, m_final_g]).reshape(1, _SM_REPL)
    v_rep = jnp.concatenate([v_norm_g, v_b_dec, v_gla_norm_g, v_b_gate, v_final_g]).reshape(1, _SM_REPL)
    g_rep, d_rep, nm_rep, nv_rep = _adamw_call(r_small[:, :, :_SM_REPL], w_rep, m_rep, v_rep, "adamw_rep")

    def split_rep(a):
        a = a.reshape(-1)
        return (a[_SM_NORM:_SM_BDEC], a[_SM_BDEC:_SM_GLAG], a[_SM_GLAG:_SM_BGATE],
                a[_SM_BGATE:_SM_FINAL], a[_SM_FINAL:_SM_REPL])

    g_norm, g_bdec, g_glag, g_bgate, g_final = split_rep(g_rep)
    d_norm, d_bdec, d_glag, d_bgate, d_final = split_rep(d_rep)
    nm_norm, nm_bdec, nm_glag, nm_bgate, nm_final = split_rep(nm_rep)
    nv_norm, nv_bdec, nv_glag, nv_bgate, nv_final = split_rep(nv_rep)

    wdec_parts = r_small[:, 0, _SM_WDEC:].reshape(N_DEV, GLA_RANK, GLA_DK)
    cols = GLA_DK // N_DEV
    wdec_mine = lax.dynamic_slice_in_dim(wdec_parts, me * cols, cols, axis=2)
    g_wdec, d_wdec, nm_wdec, nv_wdec = _adamw_call(wdec_mine, w_dec_up, m_w_dec_up, v_w_dec_up, "adamw_dec")

    loss_total = jnp.sum(r_small[:, 0, _SM_LOSS])

    return (loss_total, grad_x[None],
            g_norm, gw_in, g_wdec, g_bdec, g_glag, g_pa, g_pb, g_bgate, g_o, g_final,
            d_norm, d_in, d_wdec, d_bdec, d_glag, d_pa, d_pb, d_bgate, d_o, d_final,
            nm_norm, nm_in, nm_wdec, nm_bdec, nm_glag, nm_pa, nm_pb, nm_bgate, nm_o, nm_final,
            nv_norm, nv_in, nv_wdec, nv_bdec, nv_glag, nv_pa, nv_pb, nv_bgate, nv_o, nv_final)
```

```python
import functools
import math

import jax
import jax.numpy as jnp
from jax import lax
from jax.experimental import pallas as pl
from jax.experimental.pallas import tpu as pltpu

F32 = jnp.float32
BF16 = jnp.bfloat16

N_DEV = 8
D_MODEL = 1024
GLA_HEADS = 4
GLA_HK = 128
GLA_HV = 256
GLA_DK = 512
GLA_RANK = 16
GLA_TAU = 16.0
GLA_CHUNK = 64
SB_HEADS = 8
SB_HD = 128
SB_BLOCK = 128
EPS = 1e-6
N_GROUPS = 9
RANK_COL = 3072
IN_COLS = 9232
SHARD_COLS = IN_COLS // N_DEV

ADAM_LR = 0.001
ADAM_B1 = 0.9
ADAM_B2 = 0.999
ADAM_EPS = 1e-08
ADAM_WD = 0.01
ADAM_STEP = 10

VMEM_LIMIT = 56 * 1024 * 1024
TBLK = 256


def _cparams(sem=None):
    return pltpu.CompilerParams(dimension_semantics=sem, vmem_limit_bytes=VMEM_LIMIT)


def _tiling_2d(rows, cols):
    if rows * cols <= 128 * 1024:
        return (rows, cols), (1,), lambda i: (0, 0)
    if rows % 128 == 0:
        return (128, cols), (rows // 128,), lambda i: (i, 0)
    tc = 256 if cols % 256 == 0 else cols
    return (rows, tc), (cols // tc,), lambda i: (0, i)


def _dot(a, b):
    return jnp.dot(a, b, preferred_element_type=F32)


def _dot_nt(a, b):
    return lax.dot_general(a, b, (((1,), (1,)), ((), ())), preferred_element_type=F32)


def _dot_tn(a, b):
    return lax.dot_general(a, b, (((0,), (0,)), ((), ())), preferred_element_type=F32)


def _bf(x):
    return x.astype(BF16)


def _split3(x):
    hi = x.astype(BF16)
    r = x - hi.astype(F32)
    mid = r.astype(BF16)
    lo = (r - mid.astype(F32)).astype(BF16)
    return hi, mid, lo


def _tri_left(tri, x):
    hi, mid, lo = _split3(x)
    return _dot(tri, hi) + _dot(tri, mid) + _dot(tri, lo)


def _split2(x):
    hi = lax.bitcast_convert_type(lax.bitcast_convert_type(x, jnp.uint32) & jnp.uint32(0xFFFF0000), F32)
    return hi.astype(BF16), (x - hi).astype(BF16)


def _tri2_left(tri, x):
    hi, lo = _split2(x)
    return _dot(tri, hi) + _dot(tri, lo)


def _tri2_right(x, tri):
    hi, lo = _split2(x)
    return _dot(hi, tri) + _dot(lo, tri)


def _iota2(n, m, dim):
    return lax.broadcasted_iota(jnp.int32, (n, m), dim)


def _sigmoid(x):
    return 1.0 / (1.0 + jnp.exp(-x))


def _softplus_neg_abs(z):
    return jnp.log(1.0 + jnp.exp(-jnp.abs(z)))


_ANY = pl.BlockSpec(memory_space=pl.ANY)


def _mesh_pos():
    return lax.axis_index("x"), lax.axis_index("y"), lax.axis_index("c")


def _other_chips(x, y):
    return [(1 - x, y), (x, 1 - y), (1 - x, 1 - y)]


def _rcopy(src, dst, send_sem, recv_sem, to):
    return pltpu.make_async_remote_copy(src_ref=src, dst_ref=dst, send_sem=send_sem, recv_sem=recv_sem,
                                        device_id=to, device_id_type=pl.DeviceIdType.MESH)


def _push_copies(src_ref, dst_ref, send_sems, recv_sems, loc_sem, scatter):
    x, y, c = _mesh_pos()
    me = 4 * x + 2 * y + c
    own = pltpu.make_async_copy(src_ref.at[me] if scatter else src_ref, dst_ref.at[me], loc_sem)
    pairs = []
    for k in range(1, N_DEV):
        px = 1 - x if k & 4 else x
        py = 1 - y if k & 2 else y
        pc = 1 - c if k & 1 else c
        pid = 4 * px + 2 * py + pc
        src = src_ref.at[pid] if scatter else src_ref
        send = _rcopy(src, dst_ref.at[me], send_sems.at[k - 1], recv_sems.at[k - 1], (px, py, pc))
        recv = _rcopy(src, dst_ref.at[pid], send_sems.at[k - 1], recv_sems.at[k - 1], (px, py, pc))
        pairs.append((send, recv))
    return own, pairs


def _push_start(own, pairs):
    own.start()
    for send, _ in pairs:
        send.start()


def _push_wait(own, pairs):
    for _, recv in pairs:
        recv.wait_recv()
    for send, _ in pairs:
        send.wait_send()
    own.wait()


_PUSH_SEMS = [pltpu.SemaphoreType.DMA((N_DEV - 1,)), pltpu.SemaphoreType.DMA((N_DEV - 1,)),
              pltpu.SemaphoreType.DMA]


def _all_gather(arrs, name):
    n = len(arrs)

    def body(*refs):
        ins = refs[:n]
        outs = refs[n:2 * n]
        send_sems, recv_sems, loc_sems = refs[2 * n:]
        x, y, c = _mesh_pos()
        sib = (x, y, 1 - c)
        chips = _other_chips(x, y)

        def slot(px, py, pc):
            return 4 * px + 2 * py + pc

        def copy(a, k, block, to, src=None):
            dst = outs[a].at[slot(*block)]
            return _rcopy(dst if src is None else src, dst, send_sems.at[a, k], recv_sems.at[a, k], to)

        mine = [pltpu.make_async_copy(ins[a], outs[a].at[slot(x, y, c)], loc_sems.at[a]) for a in range(n)]
        for cp in mine:
            cp.start()
        first = [copy(a, 0, (x, y, c), sib, src=ins[a]) for a in range(n)]
        for j, chip in enumerate(chips):
            first += [copy(a, 1 + j, (x, y, c), (*chip, c), src=ins[a]) for a in range(n)]
        for cp in first:
            cp.start()
        passed = []
        for j, chip in enumerate(chips):
            for a in range(n):
                copy(a, 1 + j, (*chip, c), (x, y, c)).wait_recv()
                fwd = copy(a, 4 + j, (*chip, c), sib)
                fwd.start()
                passed.append(fwd)
        for a in range(n):
            copy(a, 0, sib, (x, y, c)).wait_recv()
        for j, chip in enumerate(chips):
            for a in range(n):
                copy(a, 4 + j, (*chip, 1 - c), (x, y, c)).wait_recv()
        for cp in first + passed:
            cp.wait_send()
        for cp in mine:
            cp.wait()

    return pl.pallas_call(
        body, name=name,
        out_shape=tuple(jax.ShapeDtypeStruct((N_DEV,) + a.shape, a.dtype) for a in arrs),
        in_specs=[_ANY] * n,
        out_specs=tuple([_ANY] * n),
        scratch_shapes=[pltpu.SemaphoreType.DMA((n, 7)), pltpu.SemaphoreType.DMA((n, 7)),
                        pltpu.SemaphoreType.DMA((n,))],
    )(*arrs)


def _pair_exchange(arrs, name):
    n = len(arrs)

    def body(*refs):
        ins = refs[:n]
        outs = refs[n:2 * n]
        send_sems, recv_sems = refs[2 * n:]
        x, y, c = _mesh_pos()
        copies = []
        for a in range(n):
            for q in range(4):
                cp = _rcopy(ins[a].at[2 * q + (1 - c)], outs[a].at[q], send_sems.at[a, q], recv_sems.at[a, q],
                            (x, y, 1 - c))
                cp.start()
                copies.append(cp)
        for cp in copies:
            cp.wait_recv()
        for cp in copies:
            cp.wait_send()

    return pl.pallas_call(
        body, name=name,
        out_shape=tuple(jax.ShapeDtypeStruct((4,) + a.shape[1:], a.dtype) for a in arrs),
        in_specs=[_ANY] * n,
        out_specs=tuple([_ANY] * n),
        scratch_shapes=[pltpu.SemaphoreType.DMA((n, 4)), pltpu.SemaphoreType.DMA((n, 4))],
    )(*arrs)


def _pair_add_call(parts, recv, c_idx, name):
    _, R, C = parts.shape
    (tr, tc), (steps,), idx = _tiling_2d(R, C)

    def body(c_ref, p_ref, r_ref, o_ref):
        o_ref[...] = (p_ref[...].astype(F32) + r_ref[...].astype(F32)).astype(o_ref.dtype)

    return pl.pallas_call(
        body, name=name,
        grid_spec=pltpu.PrefetchScalarGridSpec(
            num_scalar_prefetch=1,
            grid=(4, steps),
            in_specs=[pl.BlockSpec((None, tr, tc), lambda q, i, c_ref: (2 * q + c_ref[0],) + idx(i)),
                      pl.BlockSpec((None, tr, tc), lambda q, i, c_ref: (q,) + idx(i))],
            out_specs=pl.BlockSpec((None, tr, tc), lambda q, i, c_ref: (q,) + idx(i))),
        out_shape=jax.ShapeDtypeStruct((4, R, C), parts.dtype),
        compiler_params=_cparams(("arbitrary", "arbitrary")),
    )(c_idx, parts, recv)


def _chip_exchange(arrs, name):
    n = len(arrs)

    def body(*refs):
        ins = refs[:n]
        outs = refs[n:2 * n]
        send_sems, recv_sems, loc_sems = refs[2 * n:]
        x, y, c = _mesh_pos()
        mine = [pltpu.make_async_copy(ins[a].at[2 * x + y], outs[a].at[3], loc_sems.at[a]) for a in range(n)]
        for cp in mine:
            cp.start()
        copies = []
        for j, (px, py) in enumerate(_other_chips(x, y)):
            for a in range(n):
                cp = _rcopy(ins[a].at[2 * px + py], outs[a].at[j], send_sems.at[a, j], recv_sems.at[a, j],
                            (px, py, c))
                cp.start()
                copies.append(cp)
        for cp in copies:
            cp.wait_recv()
        for cp in copies:
            cp.wait_send()
        for cp in mine:
            cp.wait()

    return pl.pallas_call(
        body, name=name,
        out_shape=tuple(jax.ShapeDtypeStruct(a.shape, a.dtype) for a in arrs),
        in_specs=[_ANY] * n,
        out_specs=tuple([_ANY] * n),
        scratch_shapes=[pltpu.SemaphoreType.DMA((n, 3)), pltpu.SemaphoreType.DMA((n, 3)),
                        pltpu.SemaphoreType.DMA((n,))],
    )(*arrs)


def _proj_call(x, norm_g, w3, wr, wp_shard):
    T, D = x.shape
    tm = min(512, T)
    assert tm % TBLK == 0
    n_i = T // tm

    def f_slot(j):
        return ((j >= 2).astype(jnp.int32) + (j >= 6).astype(jnp.int32)
                + (j >= 7).astype(jnp.int32) + (j >= 8).astype(jnp.int32))

    def b_slot(j):
        return (j >= 3).astype(jnp.int32) + (j >= 4).astype(jnp.int32) + (j >= 5).astype(jnp.int32)

    def body(x_ref, g_ref, w_ref, wr_ref, wp_ref, pf_ref, pb_ref, rank_ref, ht_ref, wpall_ref,
             h_scr, send_sems, recv_sems, loc_sem):
        i = pl.program_id(0)
        j = pl.program_id(1)
        own, pairs = _push_copies(wp_ref, wpall_ref, send_sems, recv_sems, loc_sem, scatter=False)

        @pl.when((i == 0) & (j == 0))
        def _():
            _push_start(own, pairs)

        @pl.when(j == 0)
        def _():
            xv = x_ref[...]
            r = lax.rsqrt(jnp.mean(xv * xv, axis=-1, keepdims=True) + EPS)
            h = (xv * r) * g_ref[...]
            hb = _bf(h)
            h_scr[...] = hb
            for b in range(tm // TBLK):
                ht_ref[b] = _bf(h[b * TBLK:(b + 1) * TBLK].T)
            rank_ref[...] = _dot_nt(hb, wr_ref[...])

        is_b = (j == 1) | ((j >= 3) & (j <= 5))

        @pl.when(is_b)
        def _():
            pb_ref[...] = _bf(_dot_nt(h_scr[...], w_ref[...]))

        @pl.when(jnp.logical_not(is_b))
        def _():
            pf_ref[...] = _dot_nt(h_scr[...], w_ref[...])

        @pl.when((i == n_i - 1) & (j == N_GROUPS - 1))
        def _():
            _push_wait(own, pairs)

    return pl.pallas_call(
        body, name="proj",
        grid=(n_i, N_GROUPS),
        in_specs=[pl.BlockSpec((tm, D), lambda i, j: (i, 0)),
                  pl.BlockSpec((1, D), lambda i, j: (0, 0)),
                  pl.BlockSpec((None, 1024, D), lambda i, j: (j, 0, 0)),
                  pl.BlockSpec((128, D), lambda i, j: (0, 0)),
                  _ANY],
        out_specs=(pl.BlockSpec((None, tm, 1024), lambda i, j: (f_slot(j), i, 0)),
                   pl.BlockSpec((None, tm, 1024), lambda i, j: (b_slot(j), i, 0)),
                   pl.BlockSpec((tm, 128), lambda i, j: (i, 0)),
                   pl.BlockSpec((tm // TBLK, D, TBLK), lambda i, j: (i, 0, 0)),
                   _ANY),
        out_shape=(jax.ShapeDtypeStruct((5, T, 1024), F32),
                   jax.ShapeDtypeStruct((4, T, 1024), BF16),
                   jax.ShapeDtypeStruct((T, 128), F32),
                   jax.ShapeDtypeStruct((T // TBLK, D, TBLK), BF16),
                   jax.ShapeDtypeStruct((N_DEV,) + wp_shard.shape, wp_shard.dtype)),
        scratch_shapes=[pltpu.VMEM((tm, D), BF16)] + _PUSH_SEMS,
        compiler_params=_cparams(("arbitrary", "arbitrary")),
    )(x, norm_g, w3, wr, wp_shard)


def _gla_chunk_terms(la_h, q, k):
    C = GLA_CHUNK
    low = _bf((_iota2(C, C, 0) >= _iota2(C, C, 1)).astype(F32))
    b = _tri_left(low, la_h)
    bl = b[C - 1:C, :]
    eb = jnp.exp(b)
    enb = jnp.exp(-b)
    ebl_b = jnp.exp(bl - b)
    scale = GLA_HK ** -0.5
    qe = q * eb * scale
    ke = k * enb
    kd = k * ebl_b
    return b, bl, eb, enb, ebl_b, qe, ke, kd


def _gla_fwd_call(projf, projb, rank, wdec, bdec):
    T = projf.shape[1]
    C = GLA_CHUNK
    n_chunks = T // C

    def body(qk_ref, v_ref, rank_ref, wd_ref, bd_ref, o_ref, st_ref, la_ref, st_scr):
        @pl.when(pl.program_id(0) == 0)
        def _():
            st_scr[...] = jnp.zeros_like(st_scr)

        dec = _dot(_bf(rank_ref[...]), _bf(wd_ref[...])) + bd_ref[...]
        la = (jnp.minimum(dec, 0.0) - _softplus_neg_abs(dec)) / GLA_TAU
        la_ref[...] = la
        mask = _iota2(C, C, 0) >= _iota2(C, C, 1)
        for hh in range(GLA_HEADS):
            la_h = la[:, hh * GLA_HK:(hh + 1) * GLA_HK]
            q = qk_ref[:, hh * GLA_HK:(hh + 1) * GLA_HK]
            k = qk_ref[:, GLA_DK + hh * GLA_HK:GLA_DK + (hh + 1) * GLA_HK]
            v = _bf(v_ref[:, hh * GLA_HV:(hh + 1) * GLA_HV])
            _, bl, _, _, _, qe, ke, kd = _gla_chunk_terms(la_h, q, k)
            st = st_scr[hh]
            st_ref[hh] = st
            p = jnp.where(mask, _dot_nt(_bf(qe), _bf(ke)), 0.0)
            o = _dot(_bf(p), v) + _dot_nt(_bf(qe), _bf(st))
            o_ref[:, hh * GLA_HV:(hh + 1) * GLA_HV] = o
            st_scr[hh] = st * jnp.exp(bl) + _dot_tn(v, _bf(kd))

    return pl.pallas_call(
        body, name="gla_fwd",
        grid=(n_chunks,),
        in_specs=[pl.BlockSpec((None, C, 1024), lambda n: (0, n, 0)),
                  pl.BlockSpec((None, C, 1024), lambda n: (0, n, 0)),
                  pl.BlockSpec((C, 128), lambda n: (n, 0)),
                  pl.BlockSpec((128, GLA_DK), lambda n: (0, 0)),
                  pl.BlockSpec((1, GLA_DK), lambda n: (0, 0))],
        out_specs=(pl.BlockSpec((C, 1024), lambda n: (n, 0)),
                   pl.BlockSpec((None, GLA_HEADS, GLA_HV, GLA_HK), lambda n: (n, 0, 0, 0)),
                   pl.BlockSpec((C, GLA_DK), lambda n: (n, 0))),
        out_shape=(jax.ShapeDtypeStruct((T, 1024), F32),
                   jax.ShapeDtypeStruct((n_chunks, GLA_HEADS, GLA_HV, GLA_HK), F32),
                   jax.ShapeDtypeStruct((T, GLA_DK), F32)),
        scratch_shapes=[pltpu.VMEM((GLA_HEADS, GLA_HV, GLA_HK), F32)],
        compiler_params=_cparams(("arbitrary",)),
    )(projf, projb, rank, wdec, bdec)


def _gla_bwd_call(projf, projb, la, do_gla, st_all, rank, wdec, g_p):
    T = projf.shape[1]
    C = GLA_CHUNK
    n_chunks = T // C
    last = n_chunks - 1

    def body(qk_ref, v_ref, la_ref, do_ref, st_ref, rank_ref, wd_ref, gp_ref,
             dqk_ref, dv_ref, drank_ref, dwd_ref, dbd_ref, rp_ref,
             dst_scr, send_sems, recv_sems, loc_sem):
        own, pairs = _push_copies(gp_ref, rp_ref, send_sems, recv_sems, loc_sem, scatter=True)

        @pl.when(pl.program_id(0) == 0)
        def _():
            _push_start(own, pairs)
            dst_scr[...] = jnp.zeros_like(dst_scr)
            dwd_ref[...] = jnp.zeros_like(dwd_ref)
            dbd_ref[...] = jnp.zeros_like(dbd_ref)

        mask = _iota2(C, C, 0) >= _iota2(C, C, 1)
        upp = _bf((_iota2(C, C, 0) <= _iota2(C, C, 1)).astype(F32))
        scale = GLA_HK ** -0.5
        la = la_ref[...]
        ddec_parts = []
        for hh in range(GLA_HEADS):
            la_h = la[:, hh * GLA_HK:(hh + 1) * GLA_HK]
            q = qk_ref[:, hh * GLA_HK:(hh + 1) * GLA_HK]
            k = qk_ref[:, GLA_DK + hh * GLA_HK:GLA_DK + (hh + 1) * GLA_HK]
            v = _bf(v_ref[:, hh * GLA_HV:(hh + 1) * GLA_HV])
            do = _bf(do_ref[:, hh * GLA_HV:(hh + 1) * GLA_HV])
            _, bl, eb, enb, ebl_b, qe, ke, kd = _gla_chunk_terms(la_h, q, k)
            qeb, keb, kdb = _bf(qe), _bf(ke), _bf(kd)
            st = st_ref[hh]
            dstn = dst_scr[hh]
            dstnb = _bf(dstn)
            ebl = jnp.exp(bl)
            p = jnp.where(mask, _dot_nt(qeb, keb), 0.0)
            dp = _bf(jnp.where(mask, _dot_nt(do, v), 0.0))
            dv = _dot_tn(_bf(p), do) + _dot_nt(kdb, dstnb)
            dqe = _dot(dp, keb) + _dot(do, _bf(st))
            dke = _dot_tn(dp, qeb)
            dkd = _dot(v, dstnb)
            dst_scr[hh] = _dot_tn(do, qeb) + dstn * ebl
            debl = jnp.sum(dstn * st, axis=0, keepdims=True)
            dkd_kd = dkd * kd
            db = dqe * qe - dke * ke - dkd_kd
            dbl = jnp.sum(dkd_kd, axis=0, keepdims=True) + ebl * debl
            dla = _tri_left(upp, db) + dbl
            dq = dqe * eb * scale
            dk = dke * enb + dkd * ebl_b
            dqk_ref[:, hh * GLA_HK:(hh + 1) * GLA_HK] = _bf(dq)
            dqk_ref[:, GLA_DK + hh * GLA_HK:GLA_DK + (hh + 1) * GLA_HK] = _bf(dk)
            dv_ref[:, hh * GLA_HV:(hh + 1) * GLA_HV] = _bf(dv)
            ddec_parts.append(dla * (1.0 / GLA_TAU) * (1.0 - jnp.exp(GLA_TAU * la_h)))
        ddec = jnp.concatenate(ddec_parts, axis=1)
        ddecb = _bf(ddec)
        drank_ref[...] = _bf(_dot_nt(ddecb, _bf(wd_ref[...])))
        dwd_ref[...] += _dot_tn(_bf(rank_ref[...]), ddecb)
        dbd_ref[...] += jnp.sum(ddec, axis=0, keepdims=True)

        @pl.when(pl.program_id(0) == last)
        def _():
            _push_wait(own, pairs)

    return pl.pallas_call(
        body, name="gla_bwd",
        grid=(n_chunks,),
        in_specs=[pl.BlockSpec((None, C, 1024), lambda n: (0, last - n, 0)),
                  pl.BlockSpec((None, C, 1024), lambda n: (0, last - n, 0)),
                  pl.BlockSpec((C, GLA_DK), lambda n: (last - n, 0)),
                  pl.BlockSpec((C, 1024), lambda n: (last - n, 0)),
                  pl.BlockSpec((None, GLA_HEADS, GLA_HV, GLA_HK), lambda n: (last - n, 0, 0, 0)),
                  pl.BlockSpec((C, 128), lambda n: (last - n, 0)),
                  pl.BlockSpec((128, GLA_DK), lambda n: (0, 0)),
                  _ANY],
        out_specs=(pl.BlockSpec((C, 1024), lambda n: (last - n, 0)),
                   pl.BlockSpec((C, 1024), lambda n: (last - n, 0)),
                   pl.BlockSpec((C, 128), lambda n: (last - n, 0)),
                   pl.BlockSpec((128, GLA_DK), lambda n: (0, 0)),
                   pl.BlockSpec((1, GLA_DK), lambda n: (0, 0)),
                   _ANY),
        out_shape=(jax.ShapeDtypeStruct((T, 1024), BF16),
                   jax.ShapeDtypeStruct((T, 1024), BF16),
                   jax.ShapeDtypeStruct((T, 128), BF16),
                   jax.ShapeDtypeStruct((128, GLA_DK), F32),
                   jax.ShapeDtypeStruct((1, GLA_DK), F32),
                   jax.ShapeDtypeStruct(g_p.shape, g_p.dtype)),
        scratch_shapes=[pltpu.VMEM((GLA_HEADS, GLA_HV, GLA_HK), F32)] + _PUSH_SEMS,
        compiler_params=_cparams(("arbitrary",)),
    )(projf, projb, la, do_gla, st_all, rank, wdec, g_p)


def _sb_logs(z):
    lsz = jnp.minimum(z, 0.0) - _softplus_neg_abs(z)
    return lsz, lsz - z


SB_HG_FWD = 8
SB_HG_BWD = 4
SB_KEYS = 256


def _sb_fwd_call(projb):
    T = projb.shape[1]
    B = SB_BLOCK
    HG = SB_HG_FWD
    W = HG * SB_HD
    scale = 1.0 / math.sqrt(SB_HD)

    KB = min(SB_KEYS, T)

    def body(q_ref, k_ref, v_ref, o_ref, cb_scr):
        i = pl.program_id(1)
        rows = HG * B
        after = (_iota2(KB, KB, 0) > _iota2(KB, KB, 1)).astype(F32)
        tri = _bf(jnp.concatenate([after, jnp.ones((KB, KB), F32)], axis=1))
        o_ref[...] = jnp.zeros_like(o_ref)
        cb_scr[...] = jnp.zeros_like(cb_scr)

        def block(jp, masked):
            off = pl.multiple_of(jp * KB, KB)
            z = jnp.concatenate(
                [_dot_nt(q_ref[:, hh * SB_HD:(hh + 1) * SB_HD], k_ref[pl.ds(off, KB), hh * SB_HD:(hh + 1) * SB_HD])
                 for hh in range(HG)], axis=0) * scale
            lsz, l1m = _sb_logs(z)
            if masked:
                strict = (jp * KB + _iota2(rows, KB, 1)) < (i * B + (_iota2(rows, KB, 0) & (B - 1)))
                l1m = jnp.where(strict, l1m, 0.0)
            r = _tri2_right(l1m, tri)
            cb = cb_scr[...]
            a = jnp.exp(lsz + cb + r[:, :KB])
            if masked:
                a = jnp.where(strict, a, 0.0)
            cb_scr[...] = cb + r[:, KB:]
            ab = _bf(a)
            for hh in range(HG):
                cs = slice(hh * SB_HD, (hh + 1) * SB_HD)
                o_ref[:, cs] += _dot(ab[hh * B:(hh + 1) * B, :], v_ref[pl.ds(off, KB), cs])

        jp0 = (i * B) // KB
        block(jp0, True)

        def step(jj, c):
            block(jp0 - jj, False)
            return c

        lax.fori_loop(1, jp0 + 1, step, 0)

    return pl.pallas_call(
        body, name="sb_fwd",
        grid=(SB_HEADS // HG, T // B),
        in_specs=[pl.BlockSpec((None, B, W), lambda h, i: (1, i, h)),
                  pl.BlockSpec((None, T, W), lambda h, i: (2, 0, h)),
                  pl.BlockSpec((None, T, W), lambda h, i: (3, 0, h))],
        out_specs=pl.BlockSpec((B, W), lambda h, i: (i, h)),
        out_shape=jax.ShapeDtypeStruct((T, 1024), F32),
        scratch_shapes=[pltpu.VMEM((HG * B, KB), F32)],
        compiler_params=_cparams(("arbitrary", "arbitrary")),
    )(projb, projb, projb)


def _sb_bwd_call(projb, do_sb):
    T = projb.shape[1]
    B = SB_BLOCK
    nb = T // B
    HG = SB_HG_BWD
    W = HG * SB_HD
    KB = min(SB_KEYS, T)
    nkb = T // KB
    scale = 1.0 / math.sqrt(SB_HD)

    def body(q_ref, k_ref, v_ref, do_ref, dq_ref, dk_ref, dv_ref,
             dk_scr, dv_scr, kt_scr, beta_scr, g_scr, dqt_scr):
        i = pl.program_id(1)

        @pl.when(i == 0)
        def _():
            dk_scr[...] = jnp.zeros_like(dk_scr)
            dv_scr[...] = jnp.zeros_like(dv_scr)
            for hh in range(HG):
                for jb in range(nkb):
                    kt_scr[hh, jb] = _bf(
                        k_ref[jb * KB:(jb + 1) * KB, hh * SB_HD:(hh + 1) * SB_HD].astype(F32).T)

        dqt_scr[...] = jnp.zeros_like(dqt_scr)
        later = _bf((_iota2(KB, KB, 1) > _iota2(KB, KB, 0)).astype(F32))
        earlier = _bf((_iota2(KB, KB, 1) < _iota2(KB, KB, 0)).astype(F32))
        dob = _bf(do_ref[...])
        jp0 = (i * B) // KB

        def strict_mask():
            return (jp0 * KB + _iota2(KB, W, 0)) < (i * B + (_iota2(KB, W, 1) & (B - 1)))

        def heads(fn):
            return [fn(slice(hh * SB_HD, (hh + 1) * SB_HD)) for hh in range(HG)]

        def pass1(jp, cb, masked):
            off = pl.multiple_of(jp * KB, KB)
            z = jnp.concatenate(heads(lambda cs: _dot_nt(k_ref[pl.ds(off, KB), cs], q_ref[:, cs])), axis=1) * scale
            da = jnp.concatenate(heads(lambda cs: _dot_nt(v_ref[pl.ds(off, KB), cs], dob[:, cs])), axis=1)
            lsz, l1m = _sb_logs(z)
            if masked:
                strict = strict_mask()
                l1m = jnp.where(strict, l1m, 0.0)
            a = jnp.exp(lsz + cb + _tri2_left(later, l1m))
            if masked:
                a = jnp.where(strict, a, 0.0)
            g_scr[jp] = a * da
            beta_scr[jp] = jnp.exp(lsz)
            ab = _bf(a)
            for hh in range(HG):
                cs = slice(hh * SB_HD, (hh + 1) * SB_HD)
                dv_scr[pl.ds(off, KB), cs] += _dot(ab[:, cs], dob[:, cs])
            return cb + jnp.sum(l1m, axis=0, keepdims=True)

        zero = jnp.zeros((1, W), F32)
        cb = pass1(jp0, zero, True)
        lax.fori_loop(1, jp0 + 1, lambda jj, cr: pass1(jp0 - jj, cr, False), cb)

        def pass2(jp, cg, masked):
            off = pl.multiple_of(jp * KB, KB)
            g = g_scr[jp]
            beta = beta_scr[jp]
            dz = g * (1.0 - beta) - beta * (cg + _tri2_left(earlier, g))
            if masked:
                dz = jnp.where(strict_mask(), dz, 0.0)
            dzb = _bf(dz * scale)
            for hh in range(HG):
                cs = slice(hh * SB_HD, (hh + 1) * SB_HD)
                dk_scr[pl.ds(off, KB), cs] += _dot(dzb[:, cs], q_ref[:, cs])
                dqt_scr[hh] += _dot(kt_scr[hh, jp], dzb[:, cs])
            return cg + jnp.sum(g, axis=0, keepdims=True)

        cg = lax.fori_loop(0, jp0, lambda jp, cr: pass2(jp, cr, False), zero)
        pass2(jp0, cg, True)
        for hh in range(HG):
            dq_ref[:, hh * SB_HD:(hh + 1) * SB_HD] = _bf(dqt_scr[hh].T)

        @pl.when(i == nb - 1)
        def _():
            dk_ref[...] = _bf(dk_scr[...])
            dv_ref[...] = _bf(dv_scr[...])

    return pl.pallas_call(
        body, name="sb_bwd",
        grid=(SB_HEADS // HG, nb),
        in_specs=[pl.BlockSpec((None, B, W), lambda h, i: (1, i, h)),
                  pl.BlockSpec((None, T, W), lambda h, i: (2, 0, h)),
                  pl.BlockSpec((None, T, W), lambda h, i: (3, 0, h)),
                  pl.BlockSpec((B, W), lambda h, i: (i, h))],
        out_specs=(pl.BlockSpec((B, W), lambda h, i: (i, h)),
                   pl.BlockSpec((T, W), lambda h, i: (0, h)),
                   pl.BlockSpec((T, W), lambda h, i: (0, h))),
        out_shape=(jax.ShapeDtypeStruct((T, 1024), BF16),
                   jax.ShapeDtypeStruct((T, 1024), BF16),
                   jax.ShapeDtypeStruct((T, 1024), BF16)),
        scratch_shapes=[pltpu.VMEM((T, W), F32), pltpu.VMEM((T, W), F32),
                        pltpu.VMEM((HG, nkb, SB_HD, KB), BF16),
                        pltpu.VMEM((nkb, KB, W), F32), pltpu.VMEM((nkb, KB, W), F32),
                        pltpu.VMEM((HG, SB_HD, B), F32)],
        compiler_params=_cparams(("arbitrary", "arbitrary")),
    )(projb, projb, projb, do_sb)


def _mid_call(o_gla, o_sb, projf, x, target, wpa, wpb, wo, gla_g, b_gate, final_g):
    T, D = x.shape
    tm = min(TBLK, T)

    def body(og_ref, ggate_ref, osb_ref, sgate_ref, ma_ref, mb_ref, x_ref, tgt_ref,
             wpa_ref, wpb_ref, wo_ref, glag_ref, bg_ref, fg_ref,
             dx2_ref, dogla_ref, dosb_ref, dggate_ref, dsgate_ref, dm_ref,
             mt_ref, ogt_ref, obt_ref, dx2b_ref, dya_ref, dyb_ref,
             dfg_ref, dbg_ref, dglag_ref, loss_ref):
        @pl.when(pl.program_id(0) == 0)
        def _():
            dfg_ref[...] = jnp.zeros_like(dfg_ref)
            dbg_ref[...] = jnp.zeros_like(dbg_ref)
            dglag_ref[...] = jnp.zeros_like(dglag_ref)
            loss_ref[...] = jnp.zeros_like(loss_ref)

        glag = glag_ref[...]
        ggate = ggate_ref[...]
        sg = _sigmoid(ggate)
        silu_g = ggate * sg
        ohat, rinv, nrm = [], [], []
        for hh in range(GLA_HEADS):
            oh = og_ref[:, hh * GLA_HV:(hh + 1) * GLA_HV]
            r = lax.rsqrt(jnp.mean(oh * oh, axis=-1, keepdims=True) + EPS)
            ohat.append(oh * r)
            rinv.append(r)
            nrm.append(ohat[-1] * glag)
        n_all = jnp.concatenate(nrm, axis=1)
        og = n_all * silu_g
        ogb = _bf(og)
        ya = _dot(ogb, wpa_ref[...])
        sgate = sgate_ref[...]
        ss = _sigmoid(sgate)
        silu_s = sgate * ss
        osb = osb_ref[...]
        ob = osb * silu_s
        obb = _bf(ob)
        yb = _dot(obb, wpb_ref[...])
        ga = _sigmoid(ma_ref[...] + bg_ref[:, :D])
        gb = _sigmoid(mb_ref[...] + bg_ref[:, D:])
        merged = ga * ya + gb * yb
        mgb = _bf(merged)
        x2 = x_ref[...] + _dot(mgb, wo_ref[...])
        r2 = lax.rsqrt(jnp.mean(x2 * x2, axis=-1, keepdims=True) + EPS)
        xh2 = x2 * r2
        fg = fg_ref[...]
        err = xh2 * fg - tgt_ref[...]
        loss_ref[...] += jnp.broadcast_to(
            0.5 * jnp.sum(jnp.mean(err * err, axis=-1, keepdims=True), axis=0, keepdims=True), (1, 128))
        dy = err * (1.0 / D)
        dfg_ref[...] += jnp.sum(dy * xh2, axis=0, keepdims=True)
        dxh = dy * fg
        dx2 = r2 * (dxh - xh2 * jnp.mean(dxh * xh2, axis=-1, keepdims=True))
        dx2_ref[...] = dx2
        dx2b = _bf(dx2)
        dx2b_ref[...] = dx2b
        dmerged = _dot_nt(dx2b, wo_ref[...])
        dya = dmerged * ga
        dyb = dmerged * gb
        dma = dmerged * ya * ga * (1.0 - ga)
        dmb = dmerged * yb * gb * (1.0 - gb)
        dm_ref[:, :D] = _bf(dma)
        dm_ref[:, D:] = _bf(dmb)
        dbg_ref[:, :D] += jnp.sum(dma, axis=0, keepdims=True)
        dbg_ref[:, D:] += jnp.sum(dmb, axis=0, keepdims=True)
        dyab = _bf(dya)
        dybb = _bf(dyb)
        dya_ref[...] = dyab
        dyb_ref[...] = dybb
        dog = _dot_nt(dyab, wpa_ref[...])
        dob = _dot_nt(dybb, wpb_ref[...])
        dosb_ref[...] = dob * silu_s
        dsgate_ref[...] = _bf(dob * osb * (ss * (1.0 + sgate * (1.0 - ss))))
        dn = dog * silu_g
        dggate_ref[...] = _bf(dog * n_all * (sg * (1.0 + ggate * (1.0 - sg))))
        dglag = jnp.zeros((1, GLA_HV), F32)
        for hh in range(GLA_HEADS):
            dnh = dn[:, hh * GLA_HV:(hh + 1) * GLA_HV]
            dglag = dglag + jnp.sum(dnh * ohat[hh], axis=0, keepdims=True)
            dohat = dnh * glag
            dogla_ref[:, hh * GLA_HV:(hh + 1) * GLA_HV] = rinv[hh] * (
                dohat - ohat[hh] * jnp.mean(dohat * ohat[hh], axis=-1, keepdims=True))
        dglag_ref[...] += dglag
        mt_ref[...] = _bf(merged.T)
        ogt_ref[...] = _bf(og.T)
        obt_ref[...] = _bf(ob.T)

    row = lambda i: (i, 0)
    const = lambda i: (0, 0)
    tile = pl.BlockSpec((tm, D), row)
    tile_t = pl.BlockSpec((None, D, tm), lambda i: (i, 0, 0))
    wspec = pl.BlockSpec((D, D), const)
    return pl.pallas_call(
        body, name="mid",
        grid=(T // tm,),
        in_specs=[tile,
                  pl.BlockSpec((None, tm, D), lambda i: (1, i, 0)),
                  tile,
                  pl.BlockSpec((None, tm, D), lambda i: (2, i, 0)),
                  pl.BlockSpec((None, tm, D), lambda i: (3, i, 0)),
                  pl.BlockSpec((None, tm, D), lambda i: (4, i, 0)),
                  tile, tile, wspec, wspec, wspec,
                  pl.BlockSpec((1, GLA_HV), const),
                  pl.BlockSpec((1, 2 * D), const),
                  pl.BlockSpec((1, D), const)],
        out_specs=(tile, tile, tile, tile, tile,
                   pl.BlockSpec((tm, 2 * D), row),
                   tile_t, tile_t, tile_t, tile, tile, tile,
                   pl.BlockSpec((1, D), const),
                   pl.BlockSpec((1, 2 * D), const),
                   pl.BlockSpec((1, GLA_HV), const),
                   pl.BlockSpec((1, 128), const)),
        out_shape=(jax.ShapeDtypeStruct((T, D), F32),
                   jax.ShapeDtypeStruct((T, D), F32),
                   jax.ShapeDtypeStruct((T, D), F32),
                   jax.ShapeDtypeStruct((T, D), BF16),
                   jax.ShapeDtypeStruct((T, D), BF16),
                   jax.ShapeDtypeStruct((T, 2 * D), BF16),
                   jax.ShapeDtypeStruct((T // tm, D, tm), BF16),
                   jax.ShapeDtypeStruct((T // tm, D, tm), BF16),
                   jax.ShapeDtypeStruct((T // tm, D, tm), BF16),
                   jax.ShapeDtypeStruct((T, D), BF16),
                   jax.ShapeDtypeStruct((T, D), BF16),
                   jax.ShapeDtypeStruct((T, D), BF16),
                   jax.ShapeDtypeStruct((1, D), F32),
                   jax.ShapeDtypeStruct((1, 2 * D), F32),
                   jax.ShapeDtypeStruct((1, GLA_HV), F32),
                   jax.ShapeDtypeStruct((1, 128), F32)),
        compiler_params=_cparams(("arbitrary",)),
    )(o_gla, projf, o_sb, projf, projf, projf, x, target, wpa, wpb, wo, gla_g, b_gate, final_g)


def _dh_call(pieces, dmlog, drank, w3, wr, x, dx2, norm_g):
    T, D = x.shape
    tm = min(256, T)
    npc = len(pieces)

    def body(*refs):
        pcs = refs[:npc]
        (dm_ref, dr_ref, w_hbm, wr_ref, x_ref, dx2_ref, g_ref,
         gx_ref, dg_ref, dwr_ref, w_scr, sem) = refs[npc:]

        @pl.when(pl.program_id(0) == 0)
        def _():
            cp = pltpu.make_async_copy(w_hbm, w_scr, sem)
            cp.start()
            cp.wait()
            dg_ref[...] = jnp.zeros_like(dg_ref)
            dwr_ref[...] = jnp.zeros_like(dwr_ref)

        dr = dr_ref[...]
        dh = _dot(dr, wr_ref[...])
        for g in range(npc):
            dh = dh + _dot(pcs[g][...], w_scr[g])
        dh = dh + _dot(dm_ref[:, :D], w_scr[npc])
        dh = dh + _dot(dm_ref[:, D:], w_scr[npc + 1])
        xv = x_ref[...]
        r = lax.rsqrt(jnp.mean(xv * xv, axis=-1, keepdims=True) + EPS)
        xhat = xv * r
        g = g_ref[...]
        dg_ref[...] += jnp.sum(dh * xhat, axis=0, keepdims=True)
        dxhat = dh * g
        gx_ref[...] = r * (dxhat - xhat * jnp.mean(dxhat * xhat, axis=-1, keepdims=True)) + dx2_ref[...]
        dwr_ref[...] += _dot_tn(dr, _bf(xhat * g))

    row = lambda i: (i, 0)
    const = lambda i: (0, 0)
    tile = pl.BlockSpec((tm, D), row)
    return pl.pallas_call(
        body, name="dh",
        grid=(T // tm,),
        in_specs=[tile] * npc + [
            pl.BlockSpec((tm, 2 * D), row),
            pl.BlockSpec((tm, 128), row),
            pl.BlockSpec(memory_space=pl.ANY),
            pl.BlockSpec((128, D), const),
            tile, tile,
            pl.BlockSpec((1, D), const)],
        out_specs=(tile, pl.BlockSpec((1, D), const), pl.BlockSpec((128, D), const)),
        out_shape=(jax.ShapeDtypeStruct((T, D), F32),
                   jax.ShapeDtypeStruct((1, D), F32),
                   jax.ShapeDtypeStruct((128, D), F32)),
        scratch_shapes=[pltpu.VMEM((N_GROUPS, 1024, D), BF16), pltpu.SemaphoreType.DMA],
        compiler_params=_cparams(("arbitrary",)),
    )(*pieces, dmlog, drank, w3, wr, x, dx2, norm_g)


def _wgrad_call(lhs_list, lhs_of_group, rhs_list, rhs_of_group, n_transposed, name):
    n_groups = len(rhs_of_group)
    n_tb, D, tb = lhs_list[0].shape
    T = n_tb * tb
    per = min(2, n_tb)
    tk = per * tb
    nk = T // tk
    nl = len(lhs_list)

    def body(*refs):
        lhs = refs[:nl]
        rhs = refs[nl:nl + n_groups]
        out_ref, acc = refs[nl + n_groups:]
        g = pl.program_id(0)
        i = pl.program_id(1)

        @pl.when(i == 0)
        def _():
            acc[...] = jnp.zeros_like(acc)

        for p in range(n_groups):
            @pl.when(g == p)
            def _(p=p):
                lref = lhs[lhs_of_group[p]]
                part = _dot(lref[0], rhs[p][0:tb, :])
                for b in range(1, per):
                    part = part + _dot(lref[b], rhs[p][b * tb:(b + 1) * tb, :])
                acc[...] += part

        @pl.when((i == nk - 1) & (g < n_transposed))
        def _():
            out_ref[...] = _bf(acc[...].T)

        @pl.when((i == nk - 1) & (g >= n_transposed))
        def _():
            out_ref[...] = _bf(acc[...])

    def lhs_spec(a):
        groups = [g for g in range(n_groups) if lhs_of_group[g] == a]
        lo, hi = min(groups), max(groups)
        assert groups == list(range(lo, hi + 1))
        return pl.BlockSpec((per, D, tb), lambda g, i: (jnp.where((g >= lo) & (g <= hi), i, 0), 0, 0))

    def rhs_spec(p):
        cb = rhs_of_group[p][1]
        return pl.BlockSpec((tk, 1024), lambda g, i: (jnp.where(g == p, i, 0), cb))

    return pl.pallas_call(
        body, name=name,
        grid=(n_groups, nk),
        in_specs=[lhs_spec(a) for a in range(nl)] + [rhs_spec(p) for p in range(n_groups)],
        out_specs=pl.BlockSpec((None, D, 1024), lambda g, i: (g, 0, 0)),
        out_shape=jax.ShapeDtypeStruct((n_groups, D, 1024), BF16),
        scratch_shapes=[pltpu.VMEM((D, 1024), F32)],
        compiler_params=_cparams(("arbitrary", "arbitrary")),
    )(*lhs_list, *[rhs_list[rhs_of_group[p][0]] for p in range(n_groups)])


def _adamw_call(parts, w, m, v, name):
    R, C = w.shape
    n_parts = parts.shape[0]
    (tr, tc), grid, idx = _tiling_2d(R, C)

    def body(p_ref, w_ref, m_ref, v_ref, g_ref, d_ref, nm_ref, nv_ref):
        g = p_ref[n_parts - 1].astype(F32)
        for k in range(n_parts - 1):
            g = g + p_ref[k].astype(F32)
        mm = ADAM_B1 * m_ref[...] + (1.0 - ADAM_B1) * g
        vv = ADAM_B2 * v_ref[...] + (1.0 - ADAM_B2) * (g * g)
        m_hat = mm / (1.0 - ADAM_B1 ** ADAM_STEP)
        v_hat = vv / (1.0 - ADAM_B2 ** ADAM_STEP)
        d_ref[...] = -ADAM_LR * (m_hat / (jnp.sqrt(v_hat) + ADAM_EPS) + ADAM_WD * w_ref[...])
        g_ref[...] = g
        nm_ref[...] = mm
        nv_ref[...] = vv

    blk = pl.BlockSpec((tr, tc), idx)
    sds = jax.ShapeDtypeStruct((R, C), F32)
    return pl.pallas_call(
        body, name=name,
        grid=grid,
        in_specs=[pl.BlockSpec((n_parts, tr, tc), lambda i: (0,) + idx(i)), blk, blk, blk],
        out_specs=(blk, blk, blk, blk),
        out_shape=(sds, sds, sds, sds),
        compiler_params=_cparams(("arbitrary",)),
    )(parts, w, m, v)


def _local_step(x, target, w3, wr, wdec, bdec, wp_shard, norm_g, gla_g, b_gate, final_g):
    D = x.shape[1]
    projf, projb, rank, ht, wp_all = _proj_call(x, norm_g, w3, wr, wp_shard)
    wp_full = wp_all.transpose(1, 0, 2, 3).reshape(3, D, D)
    o_gla, st_all, la = _gla_fwd_call(projf, projb, rank, wdec, bdec)
    o_sb = _sb_fwd_call(projb)
    (dx2, do_gla, do_sb, dggate, dsgate, dmlog, mt, ogt, obt, dx2b, dya, dyb,
     dfinal_g, db_gate, dgla_g, loss) = _mid_call(o_gla, o_sb, projf, x, target, wp_full[0], wp_full[1],
                                                 wp_full[2], gla_g, b_gate, final_g)
    dw_p = _wgrad_call([ogt, obt, mt], [0, 1, 2], [dya, dyb, dx2b], [(0, 0), (1, 0), (2, 0)], 0, "wgrad_p")
    g_p = dw_p.reshape(3, N_DEV, D // N_DEV, D).transpose(1, 0, 2, 3).reshape(N_DEV, 3 * (D // N_DEV), D)
    dqk, dgv, drank, dwdec, dbdec, r_p = _gla_bwd_call(projf, projb, la, do_gla, st_all, rank, wdec, g_p)
    dsq, dsk, dsv = _sb_bwd_call(projb, do_sb)
    pieces = [dqk, dgv, dggate, dsq, dsk, dsv, dsgate]
    grad_x, dnorm_g, dwr = _dh_call(pieces, dmlog, drank, w3, wr, x, dx2, norm_g)
    rhs_of_group = [(g, 0) for g in range(7)] + [(7, 0), (7, 1)]
    dw_in = _wgrad_call([ht], [0] * N_GROUPS, pieces + [dmlog], rhs_of_group, N_GROUPS, "wgrad_in")
    return grad_x, dw_in, dwr, r_p, dwdec, dbdec, dnorm_g, dgla_g, db_gate, dfinal_g, loss


_SM_NORM = 0
_SM_BDEC = _SM_NORM + D_MODEL
_SM_GLAG = _SM_BDEC + GLA_DK
_SM_BGATE = _SM_GLAG + GLA_HV
_SM_FINAL = _SM_BGATE + 2 * D_MODEL
_SM_REPL = _SM_FINAL + D_MODEL
_SM_LOSS = _SM_REPL
_SM_WDEC = _SM_LOSS + 128
_SM_LEN = _SM_WDEC + GLA_RANK * GLA_DK


def kernel(x, norm_g, w_in, w_dec_up, b_dec, gla_norm_g, w_pa, w_pb, b_gate, w_o, final_g, loss_target, m_norm_g, m_w_in, m_w_dec_up, m_b_dec, m_gla_norm_g, m_w_pa, m_w_pb, m_b_gate, m_w_o, m_final_g, v_norm_g, v_w_in, v_w_dec_up, v_b_dec, v_gla_norm_g, v_w_pa, v_w_pb, v_b_gate, v_w_o, v_final_g):
    D = D_MODEL
    me = 4 * lax.axis_index("x") + 2 * lax.axis_index("y") + lax.axis_index("c")

    wp_shard = jnp.stack([w_pa, w_pb, w_o]).astype(BF16)
    win_all, wdec_all = _all_gather([w_in.T.astype(BF16), w_dec_up], "gather_w")
    wt_full = win_all.reshape(IN_COLS, D)
    w3 = jnp.concatenate([wt_full[:RANK_COL], wt_full[RANK_COL + GLA_RANK:]], axis=0).reshape(N_GROUPS, 1024, D)
    wr = jnp.pad(wt_full[RANK_COL:RANK_COL + GLA_RANK], ((0, 128 - GLA_RANK), (0, 0)))
    wdec_full = wdec_all.transpose(1, 0, 2).reshape(GLA_RANK, GLA_DK)
    wdec = jnp.pad(wdec_full, ((0, 128 - GLA_RANK), (0, 0)))

    (grad_x, dw_in, dwr, r_p, dwdec, dbdec, dnorm_g, dgla_g, db_gate, dfinal_g, loss) = _local_step(
        x[0], loss_target[0], w3, wr, wdec, b_dec.reshape(1, -1), wp_shard,
        norm_g.reshape(1, -1), gla_norm_g.reshape(1, -1), b_gate.reshape(1, -1), final_g.reshape(1, -1))

    dmain = dw_in.reshape(N_GROUPS * 1024, D)
    dfull = jnp.concatenate([dmain[:RANK_COL], dwr[:GLA_RANK].astype(BF16), dmain[RANK_COL:]], axis=0)
    g_in = dfull.reshape(N_DEV, SHARD_COLS, D)
    small = jnp.concatenate([
        dnorm_g.reshape(-1), dbdec.reshape(-1), dgla_g.reshape(-1), db_gate.reshape(-1), dfinal_g.reshape(-1),
        loss.reshape(-1), dwdec[:GLA_RANK].reshape(-1)])
    c_idx = lax.axis_index("c").astype(jnp.int32).reshape(1)
    (p_in,) = _pair_exchange([g_in], "pair_g")
    s_in = _pair_add_call(g_in, p_in, c_idx, "pair_add_in")
    (r_in,) = _chip_exchange([s_in], "scatter_g")
    (r_small,) = _all_gather([small.reshape(1, _SM_LEN)], "gather_small")

    gw_in, d_in, nm_in, nv_in = (a.T for a in _adamw_call(r_in, w_in.T, m_w_in.T, v_w_in.T, "adamw_in"))
    wp_f32 = jnp.concatenate([w_pa, w_pb, w_o], axis=0)
    mp = jnp.concatenate([m_w_pa, m_w_pb, m_w_o], axis=0)
    vp = jnp.concatenate([v_w_pa, v_w_pb, v_w_o], axis=0)
    gp, dp, nmp, nvp = _adamw_call(r_p, wp_f32, mp, vp, "adamw_p")
    rows = D // N_DEV

    def split3(a):
        return a[:rows], a[rows:2 * rows], a[2 * rows:]

    g_pa, g_pb, g_o = split3(gp)
    d_pa, d_pb, d_o = split3(dp)
    nm_pa, nm_pb, nm_o = split3(nmp)
    nv_pa, nv_pb, nv_o = split3(nvp)

    w_rep = jnp.concatenate([norm_g, b_dec, gla_norm_g, b_gate, final_g]).reshape(1, _SM_REPL)
    m_rep = jnp.concatenate([m_norm_g, m_b_dec, m_gla_norm_g, m_b_gate, m_final_g]).reshape(1, _SM_REPL)
    v_rep = jnp.concatenate([v_norm_g, v_b_dec, v_gla_norm_g, v_b_gate, v_final_g]).reshape(1, _SM_REPL)
    g_rep, d_rep, nm_rep, nv_rep = _adamw_call(r_small[:, :, :_SM_REPL], w_rep, m_rep, v_rep, "adamw_rep")

    def split_rep(a):
        a = a.reshape(-1)
        return (a[_SM_NORM:_SM_BDEC], a[_SM_BDEC:_SM_GLAG], a[_SM_GLAG:_SM_BGATE],
                a[_SM_BGATE:_SM_FINAL], a[_SM_FINAL:_SM_REPL])

    g_norm, g_bdec, g_glag, g_bgate, g_final = split_rep(g_rep)
    d_norm, d_bdec, d_glag, d_bgate, d_final = split_rep(d_rep)
    nm_norm, nm_bdec, nm_glag, nm_bgate, nm_final = split_rep(nm_rep)
    nv_norm, nv_bdec, nv_glag, nv_bgate, nv_final = split_rep(nv_rep)

    wdec_parts = r_small[:, 0, _SM_WDEC:].reshape(N_DEV, GLA_RANK, GLA_DK)
    cols = GLA_DK // N_DEV
    wdec_mine = lax.dynamic_slice_in_dim(wdec_parts, me * cols, cols, axis=2)
    g_wdec, d_wdec, nm_wdec, nv_wdec = _adamw_call(wdec_mine, w_dec_up, m_w_dec_up, v_w_dec_up, "adamw_dec")

    loss_total = jnp.sum(r_small[:, 0, _SM_LOSS])

    return (loss_total, grad_x[None],
            g_norm, gw_in, g_wdec, g_bdec, g_glag, g_pa, g_pb, g_bgate, g_o, g_final,
            d_norm, d_in, d_wdec, d_bdec, d_glag, d_pa, d_pb, d_bgate, d_o, d_final,
            nm_norm, nm_in, nm_wdec, nm_bdec, nm_glag, nm_pa, nm_pb, nm_bgate, nm_o, nm_final,
            nv_norm, nv_in, nv_wdec, nv_bdec, nv_glag, nv_pa, nv_pb, nv_bgate, nv_o, nv_final)
```

```python
import functools
import math

import jax
import jax.numpy as jnp
from jax import lax
from jax.experimental import pallas as pl
from jax.experimental.pallas import tpu as pltpu

F32 = jnp.float32
BF16 = jnp.bfloat16

N_DEV = 8
D_MODEL = 1024
GLA_HEADS = 4
GLA_HK = 128
GLA_HV = 256
GLA_DK = 512
GLA_RANK = 16
GLA_TAU = 16.0
GLA_CHUNK = 64
SB_HEADS = 8
SB_HD = 128
SB_BLOCK = 128
EPS = 1e-6
N_GROUPS = 9
RANK_COL = 3072
IN_COLS = 9232
SHARD_COLS = IN_COLS // N_DEV

ADAM_LR = 0.001
ADAM_B1 = 0.9
ADAM_B2 = 0.999
ADAM_EPS = 1e-08
ADAM_WD = 0.01
ADAM_STEP = 10

VMEM_LIMIT = 56 * 1024 * 1024
TBLK = 256


def _cparams(sem=None):
    return pltpu.CompilerParams(dimension_semantics=sem, vmem_limit_bytes=VMEM_LIMIT)


def _tiling_2d(rows, cols):
    if rows * cols <= 128 * 1024:
        return (rows, cols), (1,), lambda i: (0, 0)
    if rows % 128 == 0:
        return (128, cols), (rows // 128,), lambda i: (i, 0)
    tc = 256 if cols % 256 == 0 else cols
    return (rows, tc), (cols // tc,), lambda i: (0, i)


def _dot(a, b):
    return jnp.dot(a, b, preferred_element_type=F32)


def _dot_nt(a, b):
    return lax.dot_general(a, b, (((1,), (1,)), ((), ())), preferred_element_type=F32)


def _dot_tn(a, b):
    return lax.dot_general(a, b, (((0,), (0,)), ((), ())), preferred_element_type=F32)


def _bf(x):
    return x.astype(BF16)


def _split3(x):
    hi = x.astype(BF16)
    r = x - hi.astype(F32)
    mid = r.astype(BF16)
    lo = (r - mid.astype(F32)).astype(BF16)
    return hi, mid, lo


def _tri_left(tri, x):
    hi, mid, lo = _split3(x)
    return _dot(tri, hi) + _dot(tri, mid) + _dot(tri, lo)


def _split2(x):
    hi = lax.bitcast_convert_type(lax.bitcast_convert_type(x, jnp.uint32) & jnp.uint32(0xFFFF0000), F32)
    return hi.astype(BF16), (x - hi).astype(BF16)


def _tri2_left(tri, x):
    hi, lo = _split2(x)
    return _dot(tri, hi) + _dot(tri, lo)


def _tri2_right(x, tri):
    hi, lo = _split2(x)
    return _dot(hi, tri) + _dot(lo, tri)


def _iota2(n, m, dim):
    return lax.broadcasted_iota(jnp.int32, (n, m), dim)


def _sigmoid(x):
    return 1.0 / (1.0 + jnp.exp(-x))


def _softplus_neg_abs(z):
    return jnp.log(1.0 + jnp.exp(-jnp.abs(z)))


_ANY = pl.BlockSpec(memory_space=pl.ANY)


def _mesh_pos():
    return lax.axis_index("x"), lax.axis_index("y"), lax.axis_index("c")


def _other_chips(x, y):
    return [(1 - x, y), (x, 1 - y), (1 - x, 1 - y)]


def _rcopy(src, dst, send_sem, recv_sem, to):
    return pltpu.make_async_remote_copy(src_ref=src, dst_ref=dst, send_sem=send_sem, recv_sem=recv_sem,
                                        device_id=to, device_id_type=pl.DeviceIdType.MESH)


def _push_copies(src_ref, dst_ref, send_sems, recv_sems, loc_sem, scatter):
    x, y, c = _mesh_pos()
    me = 4 * x + 2 * y + c
    own = pltpu.make_async_copy(src_ref.at[me] if scatter else src_ref, dst_ref.at[me], loc_sem)
    pairs = []
    for k in range(1, N_DEV):
        px = 1 - x if k & 4 else x
        py = 1 - y if k & 2 else y
        pc = 1 - c if k & 1 else c
        pid = 4 * px + 2 * py + pc
        src = src_ref.at[pid] if scatter else src_ref
        send = _rcopy(src, dst_ref.at[me], send_sems.at[k - 1], recv_sems.at[k - 1], (px, py, pc))
        recv = _rcopy(src, dst_ref.at[pid], send_sems.at[k - 1], recv_sems.at[k - 1], (px, py, pc))
        pairs.append((send, recv))
    return own, pairs


def _push_start(own, pairs):
    own.start()
    for send, _ in pairs:
        send.start()


def _push_wait(own, pairs):
    for _, recv in pairs:
        recv.wait_recv()
    for send, _ in pairs:
        send.wait_send()
    own.wait()


_PUSH_SEMS = [pltpu.SemaphoreType.DMA((N_DEV - 1,)), pltpu.SemaphoreType.DMA((N_DEV - 1,)),
              pltpu.SemaphoreType.DMA]


def _all_gather(arrs, name):
    n = len(arrs)

    def body(*refs):
        ins = refs[:n]
        outs = refs[n:2 * n]
        send_sems, recv_sems, loc_sems = refs[2 * n:]
        x, y, c = _mesh_pos()
        sib = (x, y, 1 - c)
        chips = _other_chips(x, y)

        def place(a, px, py, pc):
            return outs[a].at[4 * px + 2 * py + pc]

        def copy(a, k, block, to, src=None):
            dst = place(a, *block)
            return _rcopy(dst if src is None else src, dst, send_sems.at[a, k], recv_sems.at[a, k], to)

        mine = [pltpu.make_async_copy(ins[a], place(a, x, y, c), loc_sems.at[a]) for a in range(n)]
        for cp in mine:
            cp.start()
        first = [copy(a, 0, (x, y, c), sib, src=ins[a]) for a in range(n)]
        for j, chip in enumerate(chips):
            first += [copy(a, 1 + j, (x, y, c), (*chip, c), src=ins[a]) for a in range(n)]
        for cp in first:
            cp.start()
        passed = []
        for j, chip in enumerate(chips):
            for a in range(n):
                copy(a, 1 + j, (*chip, c), (x, y, c)).wait_recv()
                fwd = copy(a, 4 + j, (*chip, c), sib)
                fwd.start()
                passed.append(fwd)
        for a in range(n):
            copy(a, 0, sib, (x, y, c)).wait_recv()
        for j, chip in enumerate(chips):
            for a in range(n):
                copy(a, 4 + j, (*chip, 1 - c), (x, y, c)).wait_recv()
        for cp in first + passed:
            cp.wait_send()
        for cp in mine:
            cp.wait()

    return pl.pallas_call(
        body, name=name,
        out_shape=tuple(jax.ShapeDtypeStruct((N_DEV,) + a.shape, a.dtype) for a in arrs),
        in_specs=[_ANY] * n,
        out_specs=tuple([_ANY] * n),
        scratch_shapes=[pltpu.SemaphoreType.DMA((n, 7)), pltpu.SemaphoreType.DMA((n, 7)),
                        pltpu.SemaphoreType.DMA((n,))],
    )(*arrs)


def _pair_exchange(arrs, name):
    n = len(arrs)

    def body(*refs):
        ins = refs[:n]
        outs = refs[n:2 * n]
        send_sems, recv_sems = refs[2 * n:]
        x, y, c = _mesh_pos()
        copies = []
        for a in range(n):
            for q in range(4):
                cp = _rcopy(ins[a].at[2 * q + (1 - c)], outs[a].at[q], send_sems.at[a, q], recv_sems.at[a, q],
                            (x, y, 1 - c))
                cp.start()
                copies.append(cp)
        for cp in copies:
            cp.wait_recv()
        for cp in copies:
            cp.wait_send()

    return pl.pallas_call(
        body, name=name,
        out_shape=tuple(jax.ShapeDtypeStruct((4,) + a.shape[1:], a.dtype) for a in arrs),
        in_specs=[_ANY] * n,
        out_specs=tuple([_ANY] * n),
        scratch_shapes=[pltpu.SemaphoreType.DMA((n, 4)), pltpu.SemaphoreType.DMA((n, 4))],
    )(*arrs)


def _pair_add_call(parts, recv, c_idx, name):
    _, R, C = parts.shape
    (tr, tc), (steps,), idx = _tiling_2d(R, C)

    def body(c_ref, p_ref, r_ref, o_ref):
        o_ref[...] = (p_ref[...].astype(F32) + r_ref[...].astype(F32)).astype(o_ref.dtype)

    return pl.pallas_call(
        body, name=name,
        grid_spec=pltpu.PrefetchScalarGridSpec(
            num_scalar_prefetch=1,
            grid=(4, steps),
            in_specs=[pl.BlockSpec((None, tr, tc), lambda q, i, c_ref: (2 * q + c_ref[0],) + idx(i)),
                      pl.BlockSpec((None, tr, tc), lambda q, i, c_ref: (q,) + idx(i))],
            out_specs=pl.BlockSpec((None, tr, tc), lambda q, i, c_ref: (q,) + idx(i))),
        out_shape=jax.ShapeDtypeStruct((4, R, C), parts.dtype),
        compiler_params=_cparams(("arbitrary", "arbitrary")),
    )(c_idx, parts, recv)


def _chip_exchange(arrs, name):
    n = len(arrs)

    def body(*refs):
        ins = refs[:n]
        outs = refs[n:2 * n]
        send_sems, recv_sems, loc_sems = refs[2 * n:]
        x, y, c = _mesh_pos()
        mine = [pltpu.make_async_copy(ins[a].at[2 * x + y], outs[a].at[3], loc_sems.at[a]) for a in range(n)]
        for cp in mine:
            cp.start()
        copies = []
        for j, (px, py) in enumerate(_other_chips(x, y)):
            for a in range(n):
                cp = _rcopy(ins[a].at[2 * px + py], outs[a].at[j], send_sems.at[a, j], recv_sems.at[a, j],
                            (px, py, c))
                cp.start()
                copies.append(cp)
        for cp in copies:
            cp.wait_recv()
        for cp in copies:
            cp.wait_send()
        for cp in mine:
            cp.wait()

    return pl.pallas_call(
        body, name=name,
        out_shape=tuple(jax.ShapeDtypeStruct(a.shape, a.dtype) for a in arrs),
        in_specs=[_ANY] * n,
        out_specs=tuple([_ANY] * n),
        scratch_shapes=[pltpu.SemaphoreType.DMA((n, 3)), pltpu.SemaphoreType.DMA((n, 3)),
                        pltpu.SemaphoreType.DMA((n,))],
    )(*arrs)


def _proj_call(x, norm_g, w3, wr, wp_shard):
    T, D = x.shape
    tm = min(512, T)
    assert tm % TBLK == 0
    n_i = T // tm

    def f_slot(j):
        return ((j >= 2).astype(jnp.int32) + (j >= 6).astype(jnp.int32)
                + (j >= 7).astype(jnp.int32) + (j >= 8).astype(jnp.int32))

    def b_slot(j):
        return (j >= 3).astype(jnp.int32) + (j >= 4).astype(jnp.int32) + (j >= 5).astype(jnp.int32)

    def body(x_ref, g_ref, w_ref, wr_ref, wp_ref, pf_ref, pb_ref, rank_ref, ht_ref, wpall_ref,
             h_scr, send_sems, recv_sems, loc_sem):
        i = pl.program_id(0)
        j = pl.program_id(1)
        own, pairs = _push_copies(wp_ref, wpall_ref, send_sems, recv_sems, loc_sem, scatter=False)

        @pl.when((i == 0) & (j == 0))
        def _():
            _push_start(own, pairs)

        @pl.when(j == 0)
        def _():
            xv = x_ref[...]
            r = lax.rsqrt(jnp.mean(xv * xv, axis=-1, keepdims=True) + EPS)
            h = (xv * r) * g_ref[...]
            hb = _bf(h)
            h_scr[...] = hb
            for b in range(tm // TBLK):
                ht_ref[b] = _bf(h[b * TBLK:(b + 1) * TBLK].T)
            rank_ref[...] = _dot_nt(hb, wr_ref[...])

        is_b = (j == 1) | ((j >= 3) & (j <= 5))

        @pl.when(is_b)
        def _():
            pb_ref[...] = _bf(_dot_nt(h_scr[...], w_ref[...]))

        @pl.when(jnp.logical_not(is_b))
        def _():
            pf_ref[...] = _dot_nt(h_scr[...], w_ref[...])

        @pl.when((i == n_i - 1) & (j == N_GROUPS - 1))
        def _():
            _push_wait(own, pairs)

    return pl.pallas_call(
        body, name="proj",
        grid=(n_i, N_GROUPS),
        in_specs=[pl.BlockSpec((tm, D), lambda i, j: (i, 0)),
                  pl.BlockSpec((1, D), lambda i, j: (0, 0)),
                  pl.BlockSpec((None, 1024, D), lambda i, j: (j, 0, 0)),
                  pl.BlockSpec((128, D), lambda i, j: (0, 0)),
                  _ANY],
        out_specs=(pl.BlockSpec((None, tm, 1024), lambda i, j: (f_slot(j), i, 0)),
                   pl.BlockSpec((None, tm, 1024), lambda i, j: (b_slot(j), i, 0)),
                   pl.BlockSpec((tm, 128), lambda i, j: (i, 0)),
                   pl.BlockSpec((tm // TBLK, D, TBLK), lambda i, j: (i, 0, 0)),
                   _ANY),
        out_shape=(jax.ShapeDtypeStruct((5, T, 1024), F32),
                   jax.ShapeDtypeStruct((4, T, 1024), BF16),
                   jax.ShapeDtypeStruct((T, 128), F32),
                   jax.ShapeDtypeStruct((T // TBLK, D, TBLK), BF16),
                   jax.ShapeDtypeStruct((N_DEV,) + wp_shard.shape, wp_shard.dtype)),
        scratch_shapes=[pltpu.VMEM((tm, D), BF16)] + _PUSH_SEMS,
        compiler_params=_cparams(("arbitrary", "arbitrary")),
    )(x, norm_g, w3, wr, wp_shard)


def _gla_chunk_terms(la_h, q, k):
    C = GLA_CHUNK
    low = _bf((_iota2(C, C, 0) >= _iota2(C, C, 1)).astype(F32))
    b = _tri_left(low, la_h)
    bl = b[C - 1:C, :]
    eb = jnp.exp(b)
    enb = jnp.exp(-b)
    ebl_b = jnp.exp(bl - b)
    scale = GLA_HK ** -0.5
    qe = q * eb * scale
    ke = k * enb
    kd = k * ebl_b
    return b, bl, eb, enb, ebl_b, qe, ke, kd


def _gla_fwd_call(projf, projb, rank, wdec, bdec):
    T = projf.shape[1]
    C = GLA_CHUNK
    n_chunks = T // C

    def body(qk_ref, v_ref, rank_ref, wd_ref, bd_ref, o_ref, st_ref, la_ref, st_scr):
        @pl.when(pl.program_id(0) == 0)
        def _():
            st_scr[...] = jnp.zeros_like(st_scr)

        dec = _dot(_bf(rank_ref[...]), _bf(wd_ref[...])) + bd_ref[...]
        la = (jnp.minimum(dec, 0.0) - _softplus_neg_abs(dec)) / GLA_TAU
        la_ref[...] = la
        mask = _iota2(C, C, 0) >= _iota2(C, C, 1)
        for hh in range(GLA_HEADS):
            la_h = la[:, hh * GLA_HK:(hh + 1) * GLA_HK]
            q = qk_ref[:, hh * GLA_HK:(hh + 1) * GLA_HK]
            k = qk_ref[:, GLA_DK + hh * GLA_HK:GLA_DK + (hh + 1) * GLA_HK]
            v = _bf(v_ref[:, hh * GLA_HV:(hh + 1) * GLA_HV])
            _, bl, _, _, _, qe, ke, kd = _gla_chunk_terms(la_h, q, k)
            st = st_scr[hh]
            st_ref[hh] = st
            p = jnp.where(mask, _dot_nt(_bf(qe), _bf(ke)), 0.0)
            o = _dot(_bf(p), v) + _dot_nt(_bf(qe), _bf(st))
            o_ref[:, hh * GLA_HV:(hh + 1) * GLA_HV] = o
            st_scr[hh] = st * jnp.exp(bl) + _dot_tn(v, _bf(kd))

    return pl.pallas_call(
        body, name="gla_fwd",
        grid=(n_chunks,),
        in_specs=[pl.BlockSpec((None, C, 1024), lambda n: (0, n, 0)),
                  pl.BlockSpec((None, C, 1024), lambda n: (0, n, 0)),
                  pl.BlockSpec((C, 128), lambda n: (n, 0)),
                  pl.BlockSpec((128, GLA_DK), lambda n: (0, 0)),
                  pl.BlockSpec((1, GLA_DK), lambda n: (0, 0))],
        out_specs=(pl.BlockSpec((C, 1024), lambda n: (n, 0)),
                   pl.BlockSpec((None, GLA_HEADS, GLA_HV, GLA_HK), lambda n: (n, 0, 0, 0)),
                   pl.BlockSpec((C, GLA_DK), lambda n: (n, 0))),
        out_shape=(jax.ShapeDtypeStruct((T, 1024), F32),
                   jax.ShapeDtypeStruct((n_chunks, GLA_HEADS, GLA_HV, GLA_HK), F32),
                   jax.ShapeDtypeStruct((T, GLA_DK), F32)),
        scratch_shapes=[pltpu.VMEM((GLA_HEADS, GLA_HV, GLA_HK), F32)],
        compiler_params=_cparams(("arbitrary",)),
    )(projf, projb, rank, wdec, bdec)


def _gla_bwd_call(projf, projb, la, do_gla, st_all, rank, wdec, g_p):
    T = projf.shape[1]
    C = GLA_CHUNK
    n_chunks = T // C
    last = n_chunks - 1

    def body(qk_ref, v_ref, la_ref, do_ref, st_ref, rank_ref, wd_ref, gp_ref,
             dqk_ref, dv_ref, drank_ref, dwd_ref, dbd_ref, rp_ref,
             dst_scr, send_sems, recv_sems, loc_sem):
        own, pairs = _push_copies(gp_ref, rp_ref, send_sems, recv_sems, loc_sem, scatter=True)

        @pl.when(pl.program_id(0) == 0)
        def _():
            _push_start(own, pairs)
            dst_scr[...] = jnp.zeros_like(dst_scr)
            dwd_ref[...] = jnp.zeros_like(dwd_ref)
            dbd_ref[...] = jnp.zeros_like(dbd_ref)

        mask = _iota2(C, C, 0) >= _iota2(C, C, 1)
        upp = _bf((_iota2(C, C, 0) <= _iota2(C, C, 1)).astype(F32))
        scale = GLA_HK ** -0.5
        la = la_ref[...]
        ddec_parts = []
        for hh in range(GLA_HEADS):
            la_h = la[:, hh * GLA_HK:(hh + 1) * GLA_HK]
            q = qk_ref[:, hh * GLA_HK:(hh + 1) * GLA_HK]
            k = qk_ref[:, GLA_DK + hh * GLA_HK:GLA_DK + (hh + 1) * GLA_HK]
            v = _bf(v_ref[:, hh * GLA_HV:(hh + 1) * GLA_HV])
            do = _bf(do_ref[:, hh * GLA_HV:(hh + 1) * GLA_HV])
            _, bl, eb, enb, ebl_b, qe, ke, kd = _gla_chunk_terms(la_h, q, k)
            qeb, keb, kdb = _bf(qe), _bf(ke), _bf(kd)
            st = st_ref[hh]
            dstn = dst_scr[hh]
            dstnb = _bf(dstn)
            ebl = jnp.exp(bl)
            p = jnp.where(mask, _dot_nt(qeb, keb), 0.0)
            dp = _bf(jnp.where(mask, _dot_nt(do, v), 0.0))
            dv = _dot_tn(_bf(p), do) + _dot_nt(kdb, dstnb)
            dqe = _dot(dp, keb) + _dot(do, _bf(st))
            dke = _dot_tn(dp, qeb)
            dkd = _dot(v, dstnb)
            dst_scr[hh] = _dot_tn(do, qeb) + dstn * ebl
            debl = jnp.sum(dstn * st, axis=0, keepdims=True)
            dkd_kd = dkd * kd
            db = dqe * qe - dke * ke - dkd_kd
            dbl = jnp.sum(dkd_kd, axis=0, keepdims=True) + ebl * debl
            dla = _tri_left(upp, db) + dbl
            dq = dqe * eb * scale
            dk = dke * enb + dkd * ebl_b
            dqk_ref[:, hh * GLA_HK:(hh + 1) * GLA_HK] = _bf(dq)
            dqk_ref[:, GLA_DK + hh * GLA_HK:GLA_DK + (hh + 1) * GLA_HK] = _bf(dk)
            dv_ref[:, hh * GLA_HV:(hh + 1) * GLA_HV] = _bf(dv)
            ddec_parts.append(dla * (1.0 / GLA_TAU) * (1.0 - jnp.exp(GLA_TAU * la_h)))
        ddec = jnp.concatenate(ddec_parts, axis=1)
        ddecb = _bf(ddec)
        drank_ref[...] = _bf(_dot_nt(ddecb, _bf(wd_ref[...])))
        dwd_ref[...] += _dot_tn(_bf(rank_ref[...]), ddecb)
        dbd_ref[...] += jnp.sum(ddec, axis=0, keepdims=True)

        @pl.when(pl.program_id(0) == last)
        def _():
            _push_wait(own, pairs)

    return pl.pallas_call(
        body, name="gla_bwd",
        grid=(n_chunks,),
        in_specs=[pl.BlockSpec((None, C, 1024), lambda n: (0, last - n, 0)),
                  pl.BlockSpec((None, C, 1024), lambda n: (0, last - n, 0)),
                  pl.BlockSpec((C, GLA_DK), lambda n: (last - n, 0)),
                  pl.BlockSpec((C, 1024), lambda n: (last - n, 0)),
                  pl.BlockSpec((None, GLA_HEADS, GLA_HV, GLA_HK), lambda n: (last - n, 0, 0, 0)),
                  pl.BlockSpec((C, 128), lambda n: (last - n, 0)),
                  pl.BlockSpec((128, GLA_DK), lambda n: (0, 0)),
                  _ANY],
        out_specs=(pl.BlockSpec((C, 1024), lambda n: (last - n, 0)),
                   pl.BlockSpec((C, 1024), lambda n: (last - n, 0)),
                   pl.BlockSpec((C, 128), lambda n: (last - n, 0)),
                   pl.BlockSpec((128, GLA_DK), lambda n: (0, 0)),
                   pl.BlockSpec((1, GLA_DK), lambda n: (0, 0)),
                   _ANY),
        out_shape=(jax.ShapeDtypeStruct((T, 1024), BF16),
                   jax.ShapeDtypeStruct((T, 1024), BF16),
                   jax.ShapeDtypeStruct((T, 128), BF16),
                   jax.ShapeDtypeStruct((128, GLA_DK), F32),
                   jax.ShapeDtypeStruct((1, GLA_DK), F32),
                   jax.ShapeDtypeStruct(g_p.shape, g_p.dtype)),
        scratch_shapes=[pltpu.VMEM((GLA_HEADS, GLA_HV, GLA_HK), F32)] + _PUSH_SEMS,
        compiler_params=_cparams(("arbitrary",)),
    )(projf, projb, la, do_gla, st_all, rank, wdec, g_p)


def _sb_logs(z):
    lsz = jnp.minimum(z, 0.0) - _softplus_neg_abs(z)
    return lsz, lsz - z


SB_HG_FWD = 8
SB_HG_BWD = 4
SB_KEYS = 256


def _sb_fwd_call(projb):
    T = projb.shape[1]
    B = SB_BLOCK
    HG = SB_HG_FWD
    W = HG * SB_HD
    scale = 1.0 / math.sqrt(SB_HD)

    KB = min(SB_KEYS, T)

    def body(q_ref, k_ref, v_ref, o_ref, cb_scr):
        i = pl.program_id(1)
        rows = HG * B
        after = (_iota2(KB, KB, 0) > _iota2(KB, KB, 1)).astype(F32)
        tri = _bf(jnp.concatenate([after, jnp.ones((KB, KB), F32)], axis=1))
        o_ref[...] = jnp.zeros_like(o_ref)
        cb_scr[...] = jnp.zeros_like(cb_scr)

        def block(jp, masked):
            off = pl.multiple_of(jp * KB, KB)
            z = jnp.concatenate(
                [_dot_nt(q_ref[:, hh * SB_HD:(hh + 1) * SB_HD], k_ref[pl.ds(off, KB), hh * SB_HD:(hh + 1) * SB_HD])
                 for hh in range(HG)], axis=0) * scale
            lsz, l1m = _sb_logs(z)
            if masked:
                strict = (jp * KB + _iota2(rows, KB, 1)) < (i * B + (_iota2(rows, KB, 0) & (B - 1)))
                l1m = jnp.where(strict, l1m, 0.0)
            r = _tri2_right(l1m, tri)
            cb = cb_scr[...]
            a = jnp.exp(lsz + cb + r[:, :KB])
            if masked:
                a = jnp.where(strict, a, 0.0)
            cb_scr[...] = cb + r[:, KB:]
            ab = _bf(a)
            for hh in range(HG):
                cs = slice(hh * SB_HD, (hh + 1) * SB_HD)
                o_ref[:, cs] += _dot(ab[hh * B:(hh + 1) * B, :], v_ref[pl.ds(off, KB), cs])

        jp0 = (i * B) // KB
        block(jp0, True)

        def step(jj, c):
            block(jp0 - jj, False)
            return c

        lax.fori_loop(1, jp0 + 1, step, 0)

    return pl.pallas_call(
        body, name="sb_fwd",
        grid=(SB_HEADS // HG, T // B),
        in_specs=[pl.BlockSpec((None, B, W), lambda h, i: (1, i, h)),
                  pl.BlockSpec((None, T, W), lambda h, i: (2, 0, h)),
                  pl.BlockSpec((None, T, W), lambda h, i: (3, 0, h))],
        out_specs=pl.BlockSpec((B, W), lambda h, i: (i, h)),
        out_shape=jax.ShapeDtypeStruct((T, 1024), F32),
        scratch_shapes=[pltpu.VMEM((HG * B, KB), F32)],
        compiler_params=_cparams(("arbitrary", "arbitrary")),
    )(projb, projb, projb)


def _sb_bwd_call(projb, do_sb):
    T = projb.shape[1]
    B = SB_BLOCK
    nb = T // B
    HG = SB_HG_BWD
    W = HG * SB_HD
    KB = min(SB_KEYS, T)
    nkb = T // KB
    scale = 1.0 / math.sqrt(SB_HD)

    def body(q_ref, k_ref, v_ref, do_ref, dq_ref, dk_ref, dv_ref,
             dk_scr, dv_scr, kt_scr, beta_scr, g_scr, dqt_scr):
        i = pl.program_id(1)

        @pl.when(i == 0)
        def _():
            dk_scr[...] = jnp.zeros_like(dk_scr)
            dv_scr[...] = jnp.zeros_like(dv_scr)
            for hh in range(HG):
                for jb in range(nkb):
                    kt_scr[hh, jb] = _bf(
                        k_ref[jb * KB:(jb + 1) * KB, hh * SB_HD:(hh + 1) * SB_HD].astype(F32).T)

        dqt_scr[...] = jnp.zeros_like(dqt_scr)
        later = _bf((_iota2(KB, KB, 1) > _iota2(KB, KB, 0)).astype(F32))
        earlier = _bf((_iota2(KB, KB, 1) < _iota2(KB, KB, 0)).astype(F32))
        dob = _bf(do_ref[...])
        jp0 = (i * B) // KB

        def strict_mask():
            return (jp0 * KB + _iota2(KB, W, 0)) < (i * B + (_iota2(KB, W, 1) & (B - 1)))

        def heads(fn):
            return [fn(slice(hh * SB_HD, (hh + 1) * SB_HD)) for hh in range(HG)]

        def pass1(jp, cb, masked):
            off = pl.multiple_of(jp * KB, KB)
            z = jnp.concatenate(heads(lambda cs: _dot_nt(k_ref[pl.ds(off, KB), cs], q_ref[:, cs])), axis=1) * scale
            da = jnp.concatenate(heads(lambda cs: _dot_nt(v_ref[pl.ds(off, KB), cs], dob[:, cs])), axis=1)
            lsz, l1m = _sb_logs(z)
            if masked:
                strict = strict_mask()
                l1m = jnp.where(strict, l1m, 0.0)
            a = jnp.exp(lsz + cb + _tri2_left(later, l1m))
            if masked:
                a = jnp.where(strict, a, 0.0)
            g_scr[jp] = a * da
            beta_scr[jp] = jnp.exp(lsz)
            ab = _bf(a)
            for hh in range(HG):
                cs = slice(hh * SB_HD, (hh + 1) * SB_HD)
                dv_scr[pl.ds(off, KB), cs] += _dot(ab[:, cs], dob[:, cs])
            return cb + jnp.sum(l1m, axis=0, keepdims=True)

        zero = jnp.zeros((1, W), F32)
        cb = pass1(jp0, zero, True)
        lax.fori_loop(1, jp0 + 1, lambda jj, cr: pass1(jp0 - jj, cr, False), cb)

        def pass2(jp, cg, masked):
            off = pl.multiple_of(jp * KB, KB)
            g = g_scr[jp]
            beta = beta_scr[jp]
            dz = g * (1.0 - beta) - beta * (cg + _tri2_left(earlier, g))
            if masked:
                dz = jnp.where(strict_mask(), dz, 0.0)
            dzb = _bf(dz * scale)
            for hh in range(HG):
                cs = slice(hh * SB_HD, (hh + 1) * SB_HD)
                dk_scr[pl.ds(off, KB), cs] += _dot(dzb[:, cs], q_ref[:, cs])
                dqt_scr[hh] += _dot(kt_scr[hh, jp], dzb[:, cs])
            return cg + jnp.sum(g, axis=0, keepdims=True)

        cg = lax.fori_loop(0, jp0, lambda jp, cr: pass2(jp, cr, False), zero)
        pass2(jp0, cg, True)
        for hh in range(HG):
            dq_ref[:, hh * SB_HD:(hh + 1) * SB_HD] = _bf(dqt_scr[hh].T)

        @pl.when(i == nb - 1)
        def _():
            dk_ref[...] = _bf(dk_scr[...])
            dv_ref[...] = _bf(dv_scr[...])

    return pl.pallas_call(
        body, name="sb_bwd",
        grid=(SB_HEADS // HG, nb),
        in_specs=[pl.BlockSpec((None, B, W), lambda h, i: (1, i, h)),
                  pl.BlockSpec((None, T, W), lambda h, i: (2, 0, h)),
                  pl.BlockSpec((None, T, W), lambda h, i: (3, 0, h)),
                  pl.BlockSpec((B, W), lambda h, i: (i, h))],
        out_specs=(pl.BlockSpec((B, W), lambda h, i: (i, h)),
                   pl.BlockSpec((T, W), lambda h, i: (0, h)),
                   pl.BlockSpec((T, W), lambda h, i: (0, h))),
        out_shape=(jax.ShapeDtypeStruct((T, 1024), BF16),
                   jax.ShapeDtypeStruct((T, 1024), BF16),
                   jax.ShapeDtypeStruct((T, 1024), BF16)),
        scratch_shapes=[pltpu.VMEM((T, W), F32), pltpu.VMEM((T, W), F32),
                        pltpu.VMEM((HG, nkb, SB_HD, KB), BF16),
                        pltpu.VMEM((nkb, KB, W), F32), pltpu.VMEM((nkb, KB, W), F32),
                        pltpu.VMEM((HG, SB_HD, B), F32)],
        compiler_params=_cparams(("arbitrary", "arbitrary")),
    )(projb, projb, projb, do_sb)


def _mid_call(o_gla, o_sb, projf, x, target, wpa, wpb, wo, gla_g, b_gate, final_g):
    T, D = x.shape
    tm = min(TBLK, T)

    def body(og_ref, ggate_ref, osb_ref, sgate_ref, ma_ref, mb_ref, x_ref, tgt_ref,
             wpa_ref, wpb_ref, wo_ref, glag_ref, bg_ref, fg_ref,
             dx2_ref, dogla_ref, dosb_ref, dggate_ref, dsgate_ref, dm_ref,
             mt_ref, ogt_ref, obt_ref, dx2b_ref, dya_ref, dyb_ref,
             dfg_ref, dbg_ref, dglag_ref, loss_ref):
        @pl.when(pl.program_id(0) == 0)
        def _():
            dfg_ref[...] = jnp.zeros_like(dfg_ref)
            dbg_ref[...] = jnp.zeros_like(dbg_ref)
            dglag_ref[...] = jnp.zeros_like(dglag_ref)
            loss_ref[...] = jnp.zeros_like(loss_ref)

        glag = glag_ref[...]
        ggate = ggate_ref[...]
        sg = _sigmoid(ggate)
        silu_g = ggate * sg
        ohat, rinv, nrm = [], [], []
        for hh in range(GLA_HEADS):
            oh = og_ref[:, hh * GLA_HV:(hh + 1) * GLA_HV]
            r = lax.rsqrt(jnp.mean(oh * oh, axis=-1, keepdims=True) + EPS)
            ohat.append(oh * r)
            rinv.append(r)
            nrm.append(ohat[-1] * glag)
        n_all = jnp.concatenate(nrm, axis=1)
        og = n_all * silu_g
        ogb = _bf(og)
        ya = _dot(ogb, wpa_ref[...])
        sgate = sgate_ref[...]
        ss = _sigmoid(sgate)
        silu_s = sgate * ss
        osb = osb_ref[...]
        ob = osb * silu_s
        obb = _bf(ob)
        yb = _dot(obb, wpb_ref[...])
        ga = _sigmoid(ma_ref[...] + bg_ref[:, :D])
        gb = _sigmoid(mb_ref[...] + bg_ref[:, D:])
        merged = ga * ya + gb * yb
        mgb = _bf(merged)
        x2 = x_ref[...] + _dot(mgb, wo_ref[...])
        r2 = lax.rsqrt(jnp.mean(x2 * x2, axis=-1, keepdims=True) + EPS)
        xh2 = x2 * r2
        fg = fg_ref[...]
        err = xh2 * fg - tgt_ref[...]
        loss_ref[...] += jnp.broadcast_to(
            0.5 * jnp.sum(jnp.mean(err * err, axis=-1, keepdims=True), axis=0, keepdims=True), (1, 128))
        dy = err * (1.0 / D)
        dfg_ref[...] += jnp.sum(dy * xh2, axis=0, keepdims=True)
        dxh = dy * fg
        dx2 = r2 * (dxh - xh2 * jnp.mean(dxh * xh2, axis=-1, keepdims=True))
        dx2_ref[...] = dx2
        dx2b = _bf(dx2)
        dx2b_ref[...] = dx2b
        dmerged = _dot_nt(dx2b, wo_ref[...])
        dya = dmerged * ga
        dyb = dmerged * gb
        dma = dmerged * ya * ga * (1.0 - ga)
        dmb = dmerged * yb * gb * (1.0 - gb)
        dm_ref[:, :D] = _bf(dma)
        dm_ref[:, D:] = _bf(dmb)
        dbg_ref[:, :D] += jnp.sum(dma, axis=0, keepdims=True)
        dbg_ref[:, D:] += jnp.sum(dmb, axis=0, keepdims=True)
        dyab = _bf(dya)
        dybb = _bf(dyb)
        dya_ref[...] = dyab
        dyb_ref[...] = dybb
        dog = _dot_nt(dyab, wpa_ref[...])
        dob = _dot_nt(dybb, wpb_ref[...])
        dosb_ref[...] = dob * silu_s
        dsgate_ref[...] = _bf(dob * osb * (ss * (1.0 + sgate * (1.0 - ss))))
        dn = dog * silu_g
        dggate_ref[...] = _bf(dog * n_all * (sg * (1.0 + ggate * (1.0 - sg))))
        dglag = jnp.zeros((1, GLA_HV), F32)
        for hh in range(GLA_HEADS):
            dnh = dn[:, hh * GLA_HV:(hh + 1) * GLA_HV]
            dglag = dglag + jnp.sum(dnh * ohat[hh], axis=0, keepdims=True)
            dohat = dnh * glag
            dogla_ref[:, hh * GLA_HV:(hh + 1) * GLA_HV] = rinv[hh] * (
                dohat - ohat[hh] * jnp.mean(dohat * ohat[hh], axis=-1, keepdims=True))
        dglag_ref[...] += dglag
        mt_ref[...] = _bf(merged.T)
        ogt_ref[...] = _bf(og.T)
        obt_ref[...] = _bf(ob.T)

    row = lambda i: (i, 0)
    const = lambda i: (0, 0)
    tile = pl.BlockSpec((tm, D), row)
    tile_t = pl.BlockSpec((None, D, tm), lambda i: (i, 0, 0))
    wspec = pl.BlockSpec((D, D), const)
    return pl.pallas_call(
        body, name="mid",
        grid=(T // tm,),
        in_specs=[tile,
                  pl.BlockSpec((None, tm, D), lambda i: (1, i, 0)),
                  tile,
                  pl.BlockSpec((None, tm, D), lambda i: (2, i, 0)),
                  pl.BlockSpec((None, tm, D), lambda i: (3, i, 0)),
                  pl.BlockSpec((None, tm, D), lambda i: (4, i, 0)),
                  tile, tile, wspec, wspec, wspec,
                  pl.BlockSpec((1, GLA_HV), const),
                  pl.BlockSpec((1, 2 * D), const),
                  pl.BlockSpec((1, D), const)],
        out_specs=(tile, tile, tile, tile, tile,
                   pl.BlockSpec((tm, 2 * D), row),
                   tile_t, tile_t, tile_t, tile, tile, tile,
                   pl.BlockSpec((1, D), const),
                   pl.BlockSpec((1, 2 * D), const),
                   pl.BlockSpec((1, GLA_HV), const),
                   pl.BlockSpec((1, 128), const)),
        out_shape=(jax.ShapeDtypeStruct((T, D), F32),
                   jax.ShapeDtypeStruct((T, D), F32),
                   jax.ShapeDtypeStruct((T, D), F32),
                   jax.ShapeDtypeStruct((T, D), BF16),
                   jax.ShapeDtypeStruct((T, D), BF16),
                   jax.ShapeDtypeStruct((T, 2 * D), BF16),
                   jax.ShapeDtypeStruct((T // tm, D, tm), BF16),
                   jax.ShapeDtypeStruct((T // tm, D, tm), BF16),
                   jax.ShapeDtypeStruct((T // tm, D, tm), BF16),
                   jax.ShapeDtypeStruct((T, D), BF16),
                   jax.ShapeDtypeStruct((T, D), BF16),
                   jax.ShapeDtypeStruct((T, D), BF16),
                   jax.ShapeDtypeStruct((1, D), F32),
                   jax.ShapeDtypeStruct((1, 2 * D), F32),
                   jax.ShapeDtypeStruct((1, GLA_HV), F32),
                   jax.ShapeDtypeStruct((1, 128), F32)),
        compiler_params=_cparams(("arbitrary",)),
    )(o_gla, projf, o_sb, projf, projf, projf, x, target, wpa, wpb, wo, gla_g, b_gate, final_g)


def _dh_call(pieces, dmlog, drank, w3, wr, x, dx2, norm_g):
    T, D = x.shape
    tm = min(256, T)
    npc = len(pieces)

    def body(*refs):
        pcs = refs[:npc]
        (dm_ref, dr_ref, w_hbm, wr_ref, x_ref, dx2_ref, g_ref,
         gx_ref, dg_ref, dwr_ref, w_scr, sem) = refs[npc:]

        @pl.when(pl.program_id(0) == 0)
        def _():
            cp = pltpu.make_async_copy(w_hbm, w_scr, sem)
            cp.start()
            cp.wait()
            dg_ref[...] = jnp.zeros_like(dg_ref)
            dwr_ref[...] = jnp.zeros_like(dwr_ref)

        dr = dr_ref[...]
        dh = _dot(dr, wr_ref[...])
        for g in range(npc):
            dh = dh + _dot(pcs[g][...], w_scr[g])
        dh = dh + _dot(dm_ref[:, :D], w_scr[npc])
        dh = dh + _dot(dm_ref[:, D:], w_scr[npc + 1])
        xv = x_ref[...]
        r = lax.rsqrt(jnp.mean(xv * xv, axis=-1, keepdims=True) + EPS)
        xhat = xv * r
        g = g_ref[...]
        dg_ref[...] += jnp.sum(dh * xhat, axis=0, keepdims=True)
        dxhat = dh * g
        gx_ref[...] = r * (dxhat - xhat * jnp.mean(dxhat * xhat, axis=-1, keepdims=True)) + dx2_ref[...]
        dwr_ref[...] += _dot_tn(dr, _bf(xhat * g))

    row = lambda i: (i, 0)
    const = lambda i: (0, 0)
    tile = pl.BlockSpec((tm, D), row)
    return pl.pallas_call(
        body, name="dh",
        grid=(T // tm,),
        in_specs=[tile] * npc + [
            pl.BlockSpec((tm, 2 * D), row),
            pl.BlockSpec((tm, 128), row),
            pl.BlockSpec(memory_space=pl.ANY),
            pl.BlockSpec((128, D), const),
            tile, tile,
            pl.BlockSpec((1, D), const)],
        out_specs=(tile, pl.BlockSpec((1, D), const), pl.BlockSpec((128, D), const)),
        out_shape=(jax.ShapeDtypeStruct((T, D), F32),
                   jax.ShapeDtypeStruct((1, D), F32),
                   jax.ShapeDtypeStruct((128, D), F32)),
        scratch_shapes=[pltpu.VMEM((N_GROUPS, 1024, D), BF16), pltpu.SemaphoreType.DMA],
        compiler_params=_cparams(("arbitrary",)),
    )(*pieces, dmlog, drank, w3, wr, x, dx2, norm_g)


def _wgrad_call(lhs_list, lhs_of_group, rhs_list, rhs_of_group, n_transposed, name):
    n_groups = len(rhs_of_group)
    n_tb, D, tb = lhs_list[0].shape
    T = n_tb * tb
    per = min(2, n_tb)
    tk = per * tb
    nk = T // tk
    nl = len(lhs_list)

    def body(*refs):
        lhs = refs[:nl]
        rhs = refs[nl:nl + n_groups]
        out_ref, acc = refs[nl + n_groups:]
        g = pl.program_id(0)
        i = pl.program_id(1)

        @pl.when(i == 0)
        def _():
            acc[...] = jnp.zeros_like(acc)

        for p in range(n_groups):
            @pl.when(g == p)
            def _(p=p):
                lref = lhs[lhs_of_group[p]]
                part = _dot(lref[0], rhs[p][0:tb, :])
                for b in range(1, per):
                    part = part + _dot(lref[b], rhs[p][b * tb:(b + 1) * tb, :])
                acc[...] += part

        @pl.when((i == nk - 1) & (g < n_transposed))
        def _():
            out_ref[...] = _bf(acc[...].T)

        @pl.when((i == nk - 1) & (g >= n_transposed))
        def _():
            out_ref[...] = _bf(acc[...])

    def lhs_spec(a):
        groups = [g for g in range(n_groups) if lhs_of_group[g] == a]
        lo, hi = min(groups), max(groups)
        assert groups == list(range(lo, hi + 1))
        return pl.BlockSpec((per, D, tb), lambda g, i: (jnp.where((g >= lo) & (g <= hi), i, 0), 0, 0))

    def rhs_spec(p):
        cb = rhs_of_group[p][1]
        return pl.BlockSpec((tk, 1024), lambda g, i: (jnp.where(g == p, i, 0), cb))

    return pl.pallas_call(
        body, name=name,
        grid=(n_groups, nk),
        in_specs=[lhs_spec(a) for a in range(nl)] + [rhs_spec(p) for p in range(n_groups)],
        out_specs=pl.BlockSpec((None, D, 1024), lambda g, i: (g, 0, 0)),
        out_shape=jax.ShapeDtypeStruct((n_groups, D, 1024), BF16),
        scratch_shapes=[pltpu.VMEM((D, 1024), F32)],
        compiler_params=_cparams(("arbitrary", "arbitrary")),
    )(*lhs_list, *[rhs_list[rhs_of_group[p][0]] for p in range(n_groups)])


def _adamw_call(parts, w, m, v, name):
    R, C = w.shape
    n_parts = parts.shape[0]
    (tr, tc), grid, idx = _tiling_2d(R, C)

    def body(p_ref, w_ref, m_ref, v_ref, g_ref, d_ref, nm_ref, nv_ref):
        g = p_ref[n_parts - 1].astype(F32)
        for k in range(n_parts - 1):
            g = g + p_ref[k].astype(F32)
        mm = ADAM_B1 * m_ref[...] + (1.0 - ADAM_B1) * g
        vv = ADAM_B2 * v_ref[...] + (1.0 - ADAM_B2) * (g * g)
        m_hat = mm / (1.0 - ADAM_B1 ** ADAM_STEP)
        v_hat = vv / (1.0 - ADAM_B2 ** ADAM_STEP)
        d_ref[...] = -ADAM_LR * (m_hat / (jnp.sqrt(v_hat) + ADAM_EPS) + ADAM_WD * w_ref[...])
        g_ref[...] = g
        nm_ref[...] = mm
        nv_ref[...] = vv

    blk = pl.BlockSpec((tr, tc), idx)
    sds = jax.ShapeDtypeStruct((R, C), F32)
    return pl.pallas_call(
        body, name=name,
        grid=grid,
        in_specs=[pl.BlockSpec((n_parts, tr, tc), lambda i: (0,) + idx(i)), blk, blk, blk],
        out_specs=(blk, blk, blk, blk),
        out_shape=(sds, sds, sds, sds),
        compiler_params=_cparams(("arbitrary",)),
    )(parts, w, m, v)


def _local_step(x, target, w3, wr, wdec, bdec, wp_shard, norm_g, gla_g, b_gate, final_g):
    D = x.shape[1]
    projf, projb, rank, ht, wp_all = _proj_call(x, norm_g, w3, wr, wp_shard)
    wp_full = wp_all.transpose(1, 0, 2, 3).reshape(3, D, D)
    o_gla, st_all, la = _gla_fwd_call(projf, projb, rank, wdec, bdec)
    o_sb = _sb_fwd_call(projb)
    (dx2, do_gla, do_sb, dggate, dsgate, dmlog, mt, ogt, obt, dx2b, dya, dyb,
     dfinal_g, db_gate, dgla_g, loss) = _mid_call(o_gla, o_sb, projf, x, target, wp_full[0], wp_full[1],
                                                 wp_full[2], gla_g, b_gate, final_g)
    dw_p = _wgrad_call([ogt, obt, mt], [0, 1, 2], [dya, dyb, dx2b], [(0, 0), (1, 0), (2, 0)], 0, "wgrad_p")
    g_p = dw_p.reshape(3, N_DEV, D // N_DEV, D).transpose(1, 0, 2, 3).reshape(N_DEV, 3 * (D // N_DEV), D)
    dqk, dgv, drank, dwdec, dbdec, r_p = _gla_bwd_call(projf, projb, la, do_gla, st_all, rank, wdec, g_p)
    dsq, dsk, dsv = _sb_bwd_call(projb, do_sb)
    pieces = [dqk, dgv, dggate, dsq, dsk, dsv, dsgate]
    grad_x, dnorm_g, dwr = _dh_call(pieces, dmlog, drank, w3, wr, x, dx2, norm_g)
    rhs_of_group = [(g, 0) for g in range(7)] + [(7, 0), (7, 1)]
    dw_in = _wgrad_call([ht], [0] * N_GROUPS, pieces + [dmlog], rhs_of_group, N_GROUPS, "wgrad_in")
    return grad_x, dw_in, dwr, r_p, dwdec, dbdec, dnorm_g, dgla_g, db_gate, dfinal_g, loss


def _row_pieces(lo, hi):
    out = []
    while lo < hi:
        p = lo // SHARD_COLS
        end = min(hi, (p + 1) * SHARD_COLS)
        out.append((p, lo - p * SHARD_COLS, end - p * SHARD_COLS))
        lo = end
    return out


_SM_NORM = 0
_SM_BDEC = _SM_NORM + D_MODEL
_SM_GLAG = _SM_BDEC + GLA_DK
_SM_BGATE = _SM_GLAG + GLA_HV
_SM_FINAL = _SM_BGATE + 2 * D_MODEL
_SM_REPL = _SM_FINAL + D_MODEL
_SM_LOSS = _SM_REPL
_SM_WDEC = _SM_LOSS + 128
_SM_LEN = _SM_WDEC + GLA_RANK * GLA_DK


def kernel(x, norm_g, w_in, w_dec_up, b_dec, gla_norm_g, w_pa, w_pb, b_gate, w_o, final_g, loss_target, m_norm_g, m_w_in, m_w_dec_up, m_b_dec, m_gla_norm_g, m_w_pa, m_w_pb, m_b_gate, m_w_o, m_final_g, v_norm_g, v_w_in, v_w_dec_up, v_b_dec, v_gla_norm_g, v_w_pa, v_w_pb, v_b_gate, v_w_o, v_final_g):
    D = D_MODEL
    me = 4 * lax.axis_index("x") + 2 * lax.axis_index("y") + lax.axis_index("c")

    wp_shard = jnp.stack([w_pa, w_pb, w_o]).astype(BF16)
    win_all, wdec_all = _all_gather([w_in.T.astype(BF16), w_dec_up], "gather_w")
    main_rows = _row_pieces(0, RANK_COL) + _row_pieces(RANK_COL + GLA_RANK, IN_COLS)
    w3 = jnp.concatenate([win_all[p, a:b] for p, a, b in main_rows], axis=0).reshape(N_GROUPS, 1024, D)
    ((rp, ra, rb),) = _row_pieces(RANK_COL, RANK_COL + GLA_RANK)
    wr = jnp.pad(win_all[rp, ra:rb], ((0, 128 - GLA_RANK), (0, 0)))
    wdec_full = wdec_all.transpose(1, 0, 2).reshape(GLA_RANK, GLA_DK)
    wdec = jnp.pad(wdec_full, ((0, 128 - GLA_RANK), (0, 0)))

    (grad_x, dw_in, dwr, r_p, dwdec, dbdec, dnorm_g, dgla_g, db_gate, dfinal_g, loss) = _local_step(
        x[0], loss_target[0], w3, wr, wdec, b_dec.reshape(1, -1), wp_shard,
        norm_g.reshape(1, -1), gla_norm_g.reshape(1, -1), b_gate.reshape(1, -1), final_g.reshape(1, -1))

    dmain = dw_in.reshape(N_GROUPS * 1024, D)
    drank = dwr[:GLA_RANK].astype(BF16)

    def part_for(p):
        lo, hi = p * SHARD_COLS, (p + 1) * SHARD_COLS
        pieces = []
        if lo < RANK_COL:
            pieces.append(dmain[lo:min(hi, RANK_COL)])
        if lo < RANK_COL + GLA_RANK and hi > RANK_COL:
            pieces.append(drank[max(lo, RANK_COL) - RANK_COL:min(hi, RANK_COL + GLA_RANK) - RANK_COL])
        if hi > RANK_COL + GLA_RANK:
            pieces.append(dmain[max(lo, RANK_COL + GLA_RANK) - GLA_RANK:hi - GLA_RANK])
        return pieces[0] if len(pieces) == 1 else jnp.concatenate(pieces, axis=0)

    g_in = jnp.stack([part_for(p) for p in range(N_DEV)])
    small = jnp.concatenate([
        dnorm_g.reshape(-1), dbdec.reshape(-1), dgla_g.reshape(-1), db_gate.reshape(-1), dfinal_g.reshape(-1),
        loss.reshape(-1), dwdec[:GLA_RANK].reshape(-1)])
    c_idx = lax.axis_index("c").astype(jnp.int32).reshape(1)
    (p_in,) = _pair_exchange([g_in], "pair_g")
    s_in = _pair_add_call(g_in, p_in, c_idx, "pair_add_in")
    (r_in,) = _chip_exchange([s_in], "scatter_g")
    (r_small,) = _all_gather([small.reshape(1, _SM_LEN)], "gather_small")

    gw_in, d_in, nm_in, nv_in = (a.T for a in _adamw_call(r_in, w_in.T, m_w_in.T, v_w_in.T, "adamw_in"))
    wp_f32 = jnp.concatenate([w_pa, w_pb, w_o], axis=0)
    mp = jnp.concatenate([m_w_pa, m_w_pb, m_w_o], axis=0)
    vp = jnp.concatenate([v_w_pa, v_w_pb, v_w_o], axis=0)
    gp, dp, nmp, nvp = _adamw_call(r_p, wp_f32, mp, vp, "adamw_p")
    rows = D // N_DEV

    def split3(a):
        return a[:rows], a[rows:2 * rows], a[2 * rows:]

    g_pa, g_pb, g_o = split3(gp)
    d_pa, d_pb, d_o = split3(dp)
    nm_pa, nm_pb, nm_o = split3(nmp)
    nv_pa, nv_pb, nv_o = split3(nvp)

    w_rep = jnp.concatenate([norm_g, b_dec, gla_norm_g, b_gate, final_g]).reshape(1, _SM_REPL)
    m_rep = jnp.concatenate([m_norm_g, m_b_dec, m_gla_norm_g, m_b_gate, m_final_g]).reshape(1, _SM_REPL)
    v_rep = jnp.concatenate([v_norm_g, v_b_dec, v_gla_norm_g, v_b_gate, v_final_g]).reshape(1, _SM_REPL)
    g_rep, d_rep, nm_rep, nv_rep = _adamw_call(r_small[:, :, :_SM_REPL], w_rep, m_rep, v_rep, "adamw_rep")

    def split_rep(a):
        a = a.reshape(-1)
        return (a[_SM_NORM:_SM_BDEC], a[_SM_BDEC:_SM_GLAG], a[_SM_GLAG:_SM_BGATE],
                a[_SM_BGATE:_SM_FINAL], a[_SM_FINAL:_SM_REPL])

    g_norm, g_bdec, g_glag, g_bgate, g_final = split_rep(g_rep)
    d_norm, d_bdec, d_glag, d_bgate, d_final = split_rep(d_rep)
    nm_norm, nm_bdec, nm_glag, nm_bgate, nm_final = split_rep(nm_rep)
    nv_norm, nv_bdec, nv_glag, nv_bgate, nv_final = split_rep(nv_rep)

    wdec_parts = r_small[:, 0, _SM_WDEC:].reshape(N_DEV, GLA_RANK, GLA_DK)
    cols = GLA_DK // N_DEV
    wdec_mine = lax.dynamic_slice_in_dim(wdec_parts, me * cols, cols, axis=2)
    g_wdec, d_wdec, nm_wdec, nv_wdec = _adamw_call(wdec_mine, w_dec_up, m_w_dec_up, v_w_dec_up, "adamw_dec")

    loss_total = jnp.sum(r_small[:, 0, _SM_LOSS])

    return (loss_total, grad_x[None],
            g_norm, gw_in, g_wdec, g_bdec, g_glag, g_pa, g_pb, g_bgate, g_o, g_final,
            d_norm, d_in, d_wdec, d_bdec, d_glag, d_pa, d_pb, d_bgate, d_o, d_final,
            nm_norm, nm_in, nm_wdec, nm_bdec, nm_glag, nm_pa, nm_pb, nm_bgate, nm_o, nm_final,
            nv_norm, nv_in, nv_wdec, nv_bdec, nv_glag, nv_pa, nv_pb, nv_bgate, nv_o, nv_final)
```

```python
import functools
import math

import jax
import jax.numpy as jnp
from jax import lax
from jax.experimental import pallas as pl
from jax.experimental.pallas import tpu as pltpu

F32 = jnp.float32
BF16 = jnp.bfloat16

N_DEV = 8
D_MODEL = 1024
GLA_HEADS = 4
GLA_HK = 128
GLA_HV = 256
GLA_DK = 512
GLA_RANK = 16
GLA_TAU = 16.0
GLA_CHUNK = 64
SB_HEADS = 8
SB_HD = 128
SB_BLOCK = 128
EPS = 1e-6
N_GROUPS = 9
RANK_COL = 3072
IN_COLS = 9232
SHARD_COLS = IN_COLS // N_DEV

ADAM_LR = 0.001
ADAM_B1 = 0.9
ADAM_B2 = 0.999
ADAM_EPS = 1e-08
ADAM_WD = 0.01
ADAM_STEP = 10

VMEM_LIMIT = 56 * 1024 * 1024
TBLK = 256


def _cparams(sem=None):
    return pltpu.CompilerParams(dimension_semantics=sem, vmem_limit_bytes=VMEM_LIMIT)


def _tiling_2d(rows, cols):
    if rows * cols <= 128 * 1024:
        return (rows, cols), (1,), lambda i: (0, 0)
    if rows % 128 == 0:
        return (128, cols), (rows // 128,), lambda i: (i, 0)
    tc = 256 if cols % 256 == 0 else cols
    return (rows, tc), (cols // tc,), lambda i: (0, i)


def _dot(a, b):
    return jnp.dot(a, b, preferred_element_type=F32)


def _dot_nt(a, b):
    return lax.dot_general(a, b, (((1,), (1,)), ((), ())), preferred_element_type=F32)


def _dot_tn(a, b):
    return lax.dot_general(a, b, (((0,), (0,)), ((), ())), preferred_element_type=F32)


def _bf(x):
    return x.astype(BF16)


def _split3(x):
    hi = x.astype(BF16)
    r = x - hi.astype(F32)
    mid = r.astype(BF16)
    lo = (r - mid.astype(F32)).astype(BF16)
    return hi, mid, lo


def _tri_left(tri, x):
    hi, mid, lo = _split3(x)
    return _dot(tri, hi) + _dot(tri, mid) + _dot(tri, lo)


def _split2(x):
    hi = lax.bitcast_convert_type(lax.bitcast_convert_type(x, jnp.uint32) & jnp.uint32(0xFFFF0000), F32)
    return hi.astype(BF16), (x - hi).astype(BF16)


def _tri2_left(tri, x):
    hi, lo = _split2(x)
    return _dot(tri, hi) + _dot(tri, lo)


def _tri2_right(x, tri):
    hi, lo = _split2(x)
    return _dot(hi, tri) + _dot(lo, tri)


def _iota2(n, m, dim):
    return lax.broadcasted_iota(jnp.int32, (n, m), dim)


def _sigmoid(x):
    return 1.0 / (1.0 + jnp.exp(-x))


def _softplus_neg_abs(z):
    return jnp.log(1.0 + jnp.exp(-jnp.abs(z)))


_ANY = pl.BlockSpec(memory_space=pl.ANY)


def _mesh_pos():
    return lax.axis_index("x"), lax.axis_index("y"), lax.axis_index("c")


def _other_chips(x, y):
    return [(1 - x, y), (x, 1 - y), (1 - x, 1 - y)]


def _rcopy(src, dst, send_sem, recv_sem, to):
    return pltpu.make_async_remote_copy(src_ref=src, dst_ref=dst, send_sem=send_sem, recv_sem=recv_sem,
                                        device_id=to, device_id_type=pl.DeviceIdType.MESH)


def _push_copies(src_ref, dst_ref, send_sems, recv_sems, loc_sem, scatter):
    x, y, c = _mesh_pos()
    me = 4 * x + 2 * y + c
    own = pltpu.make_async_copy(src_ref.at[me] if scatter else src_ref, dst_ref.at[me], loc_sem)
    pairs = []
    for k in range(1, N_DEV):
        px = 1 - x if k & 4 else x
        py = 1 - y if k & 2 else y
        pc = 1 - c if k & 1 else c
        pid = 4 * px + 2 * py + pc
        src = src_ref.at[pid] if scatter else src_ref
        send = _rcopy(src, dst_ref.at[me], send_sems.at[k - 1], recv_sems.at[k - 1], (px, py, pc))
        recv = _rcopy(src, dst_ref.at[pid], send_sems.at[k - 1], recv_sems.at[k - 1], (px, py, pc))
        pairs.append((send, recv))
    return own, pairs


def _push_start(own, pairs):
    own.start()
    for send, _ in pairs:
        send.start()


def _push_wait(own, pairs):
    for _, recv in pairs:
        recv.wait_recv()
    for send, _ in pairs:
        send.wait_send()
    own.wait()


_PUSH_SEMS = [pltpu.SemaphoreType.DMA((N_DEV - 1,)), pltpu.SemaphoreType.DMA((N_DEV - 1,)),
              pltpu.SemaphoreType.DMA]


def _all_gather(arrs, name):
    n = len(arrs)

    def body(*refs):
        ins = refs[:n]
        outs = refs[n:2 * n]
        send_sems, recv_sems, loc_sems = refs[2 * n:]
        x, y, c = _mesh_pos()
        sib = (x, y, 1 - c)
        chips = _other_chips(x, y)

        def place(a, px, py, pc):
            return outs[a].at[4 * px + 2 * py + pc]

        def copy(a, k, block, to, src=None):
            dst = place(a, *block)
            return _rcopy(dst if src is None else src, dst, send_sems.at[a, k], recv_sems.at[a, k], to)

        mine = [pltpu.make_async_copy(ins[a], place(a, x, y, c), loc_sems.at[a]) for a in range(n)]
        for cp in mine:
            cp.start()
        first = [copy(a, 0, (x, y, c), sib, src=ins[a]) for a in range(n)]
        for j, chip in enumerate(chips):
            first += [copy(a, 1 + j, (x, y, c), (*chip, c), src=ins[a]) for a in range(n)]
        for cp in first:
            cp.start()
        passed = []
        for j, chip in enumerate(chips):
            for a in range(n):
                copy(a, 1 + j, (*chip, c), (x, y, c)).wait_recv()
                fwd = copy(a, 4 + j, (*chip, c), sib)
                fwd.start()
                passed.append(fwd)
        for a in range(n):
            copy(a, 0, sib, (x, y, c)).wait_recv()
        for j, chip in enumerate(chips):
            for a in range(n):
                copy(a, 4 + j, (*chip, 1 - c), (x, y, c)).wait_recv()
        for cp in first + passed:
            cp.wait_send()
        for cp in mine:
            cp.wait()

    return pl.pallas_call(
        body, name=name,
        out_shape=tuple(jax.ShapeDtypeStruct((N_DEV,) + a.shape, a.dtype) for a in arrs),
        in_specs=[_ANY] * n,
        out_specs=tuple([_ANY] * n),
        scratch_shapes=[pltpu.SemaphoreType.DMA((n, 7)), pltpu.SemaphoreType.DMA((n, 7)),
                        pltpu.SemaphoreType.DMA((n,))],
    )(*arrs)


def _pair_exchange(arrs, name):
    n = len(arrs)

    def body(*refs):
        ins = refs[:n]
        outs = refs[n:2 * n]
        send_sems, recv_sems = refs[2 * n:]
        x, y, c = _mesh_pos()
        copies = []
        for a in range(n):
            for q in range(4):
                cp = _rcopy(ins[a].at[2 * q + (1 - c)], outs[a].at[q], send_sems.at[a, q], recv_sems.at[a, q],
                            (x, y, 1 - c))
                cp.start()
                copies.append(cp)
        for cp in copies:
            cp.wait_recv()
        for cp in copies:
            cp.wait_send()

    return pl.pallas_call(
        body, name=name,
        out_shape=tuple(jax.ShapeDtypeStruct((4,) + a.shape[1:], a.dtype) for a in arrs),
        in_specs=[_ANY] * n,
        out_specs=tuple([_ANY] * n),
        scratch_shapes=[pltpu.SemaphoreType.DMA((n, 4)), pltpu.SemaphoreType.DMA((n, 4))],
    )(*arrs)


def _pair_add_call(parts, recv, c_idx, name):
    _, R, C = parts.shape
    (tr, tc), (steps,), idx = _tiling_2d(R, C)

    def body(c_ref, p_ref, r_ref, o_ref):
        o_ref[...] = (p_ref[...].astype(F32) + r_ref[...].astype(F32)).astype(o_ref.dtype)

    return pl.pallas_call(
        body, name=name,
        grid_spec=pltpu.PrefetchScalarGridSpec(
            num_scalar_prefetch=1,
            grid=(4, steps),
            in_specs=[pl.BlockSpec((None, tr, tc), lambda q, i, c_ref: (2 * q + c_ref[0],) + idx(i)),
                      pl.BlockSpec((None, tr, tc), lambda q, i, c_ref: (q,) + idx(i))],
            out_specs=pl.BlockSpec((None, tr, tc), lambda q, i, c_ref: (q,) + idx(i))),
        out_shape=jax.ShapeDtypeStruct((4, R, C), parts.dtype),
        compiler_params=_cparams(("arbitrary", "arbitrary")),
    )(c_idx, parts, recv)


def _chip_exchange(arrs, name):
    n = len(arrs)

    def body(*refs):
        ins = refs[:n]
        outs = refs[n:2 * n]
        send_sems, recv_sems, loc_sems = refs[2 * n:]
        x, y, c = _mesh_pos()
        mine = [pltpu.make_async_copy(ins[a].at[2 * x + y], outs[a].at[3], loc_sems.at[a]) for a in range(n)]
        for cp in mine:
            cp.start()
        copies = []
        for j, (px, py) in enumerate(_other_chips(x, y)):
            for a in range(n):
                cp = _rcopy(ins[a].at[2 * px + py], outs[a].at[j], send_sems.at[a, j], recv_sems.at[a, j],
                            (px, py, c))
                cp.start()
                copies.append(cp)
        for cp in copies:
            cp.wait_recv()
        for cp in copies:
            cp.wait_send()
        for cp in mine:
            cp.wait()

    return pl.pallas_call(
        body, name=name,
        out_shape=tuple(jax.ShapeDtypeStruct(a.shape, a.dtype) for a in arrs),
        in_specs=[_ANY] * n,
        out_specs=tuple([_ANY] * n),
        scratch_shapes=[pltpu.SemaphoreType.DMA((n, 3)), pltpu.SemaphoreType.DMA((n, 3)),
                        pltpu.SemaphoreType.DMA((n,))],
    )(*arrs)


def _group_row(g):
    return GLA_RANK * (g * (1024 // GLA_RANK) + (g >= RANK_COL // 1024))


def _proj_call(x, norm_g, wt, wr, wp_shard):
    T, D = x.shape
    tm = min(512, T)
    assert tm % TBLK == 0
    n_i = T // tm

    def f_slot(j):
        return ((j >= 2).astype(jnp.int32) + (j >= 6).astype(jnp.int32)
                + (j >= 7).astype(jnp.int32) + (j >= 8).astype(jnp.int32))

    def b_slot(j):
        return (j >= 3).astype(jnp.int32) + (j >= 4).astype(jnp.int32) + (j >= 5).astype(jnp.int32)

    def body(x_ref, g_ref, w_ref, wr_ref, wp_ref, pf_ref, pb_ref, rank_ref, ht_ref, wpall_ref,
             h_scr, send_sems, recv_sems, loc_sem):
        i = pl.program_id(0)
        j = pl.program_id(1)
        own, pairs = _push_copies(wp_ref, wpall_ref, send_sems, recv_sems, loc_sem, scatter=False)

        @pl.when((i == 0) & (j == 0))
        def _():
            _push_start(own, pairs)

        @pl.when(j == 0)
        def _():
            xv = x_ref[...]
            r = lax.rsqrt(jnp.mean(xv * xv, axis=-1, keepdims=True) + EPS)
            h = (xv * r) * g_ref[...]
            hb = _bf(h)
            h_scr[...] = hb
            for b in range(tm // TBLK):
                ht_ref[b] = _bf(h[b * TBLK:(b + 1) * TBLK].T)
            rank_ref[...] = _dot_nt(hb, wr_ref[...])

        is_b = (j == 1) | ((j >= 3) & (j <= 5))

        @pl.when(is_b)
        def _():
            pb_ref[...] = _bf(_dot_nt(h_scr[...], w_ref[...]))

        @pl.when(jnp.logical_not(is_b))
        def _():
            pf_ref[...] = _dot_nt(h_scr[...], w_ref[...])

        @pl.when((i == n_i - 1) & (j == N_GROUPS - 1))
        def _():
            _push_wait(own, pairs)

    return pl.pallas_call(
        body, name="proj",
        grid=(n_i, N_GROUPS),
        in_specs=[pl.BlockSpec((tm, D), lambda i, j: (i, 0)),
                  pl.BlockSpec((1, D), lambda i, j: (0, 0)),
                  pl.BlockSpec((pl.Element(1024), pl.Element(D)), lambda i, j: (_group_row(j), 0)),
                  pl.BlockSpec((128, D), lambda i, j: (0, 0)),
                  _ANY],
        out_specs=(pl.BlockSpec((None, tm, 1024), lambda i, j: (f_slot(j), i, 0)),
                   pl.BlockSpec((None, tm, 1024), lambda i, j: (b_slot(j), i, 0)),
                   pl.BlockSpec((tm, 128), lambda i, j: (i, 0)),
                   pl.BlockSpec((tm // TBLK, D, TBLK), lambda i, j: (i, 0, 0)),
                   _ANY),
        out_shape=(jax.ShapeDtypeStruct((5, T, 1024), F32),
                   jax.ShapeDtypeStruct((4, T, 1024), BF16),
                   jax.ShapeDtypeStruct((T, 128), F32),
                   jax.ShapeDtypeStruct((T // TBLK, D, TBLK), BF16),
                   jax.ShapeDtypeStruct((N_DEV,) + wp_shard.shape, wp_shard.dtype)),
        scratch_shapes=[pltpu.VMEM((tm, D), BF16)] + _PUSH_SEMS,
        compiler_params=_cparams(("arbitrary", "arbitrary")),
    )(x, norm_g, wt, wr, wp_shard)


def _gla_chunk_terms(la_h, q, k):
    C = GLA_CHUNK
    low = _bf((_iota2(C, C, 0) >= _iota2(C, C, 1)).astype(F32))
    b = _tri_left(low, la_h)
    bl = b[C - 1:C, :]
    eb = jnp.exp(b)
    enb = jnp.exp(-b)
    ebl_b = jnp.exp(bl - b)
    scale = GLA_HK ** -0.5
    qe = q * eb * scale
    ke = k * enb
    kd = k * ebl_b
    return b, bl, eb, enb, ebl_b, qe, ke, kd


def _gla_fwd_call(projf, projb, rank, wdec, bdec):
    T = projf.shape[1]
    C = GLA_CHUNK
    n_chunks = T // C

    def body(qk_ref, v_ref, rank_ref, wd_ref, bd_ref, o_ref, st_ref, la_ref, st_scr):
        @pl.when(pl.program_id(0) == 0)
        def _():
            st_scr[...] = jnp.zeros_like(st_scr)

        dec = _dot(_bf(rank_ref[...]), _bf(wd_ref[...])) + bd_ref[...]
        la = (jnp.minimum(dec, 0.0) - _softplus_neg_abs(dec)) / GLA_TAU
        la_ref[...] = la
        mask = _iota2(C, C, 0) >= _iota2(C, C, 1)
        for hh in range(GLA_HEADS):
            la_h = la[:, hh * GLA_HK:(hh + 1) * GLA_HK]
            q = qk_ref[:, hh * GLA_HK:(hh + 1) * GLA_HK]
            k = qk_ref[:, GLA_DK + hh * GLA_HK:GLA_DK + (hh + 1) * GLA_HK]
            v = _bf(v_ref[:, hh * GLA_HV:(hh + 1) * GLA_HV])
            _, bl, _, _, _, qe, ke, kd = _gla_chunk_terms(la_h, q, k)
            st = st_scr[hh]
            st_ref[hh] = st
            p = jnp.where(mask, _dot_nt(_bf(qe), _bf(ke)), 0.0)
            o = _dot(_bf(p), v) + _dot_nt(_bf(qe), _bf(st))
            o_ref[:, hh * GLA_HV:(hh + 1) * GLA_HV] = o
            st_scr[hh] = st * jnp.exp(bl) + _dot_tn(v, _bf(kd))

    return pl.pallas_call(
        body, name="gla_fwd",
        grid=(n_chunks,),
        in_specs=[pl.BlockSpec((None, C, 1024), lambda n: (0, n, 0)),
                  pl.BlockSpec((None, C, 1024), lambda n: (0, n, 0)),
                  pl.BlockSpec((C, 128), lambda n: (n, 0)),
                  pl.BlockSpec((128, GLA_DK), lambda n: (0, 0)),
                  pl.BlockSpec((1, GLA_DK), lambda n: (0, 0))],
        out_specs=(pl.BlockSpec((C, 1024), lambda n: (n, 0)),
                   pl.BlockSpec((None, GLA_HEADS, GLA_HV, GLA_HK), lambda n: (n, 0, 0, 0)),
                   pl.BlockSpec((C, GLA_DK), lambda n: (n, 0))),
        out_shape=(jax.ShapeDtypeStruct((T, 1024), F32),
                   jax.ShapeDtypeStruct((n_chunks, GLA_HEADS, GLA_HV, GLA_HK), F32),
                   jax.ShapeDtypeStruct((T, GLA_DK), F32)),
        scratch_shapes=[pltpu.VMEM((GLA_HEADS, GLA_HV, GLA_HK), F32)],
        compiler_params=_cparams(("arbitrary",)),
    )(projf, projb, rank, wdec, bdec)


def _gla_bwd_call(projf, projb, la, do_gla, st_all, rank, wdec, g_p):
    T = projf.shape[1]
    C = GLA_CHUNK
    n_chunks = T // C
    last = n_chunks - 1

    def body(qk_ref, v_ref, la_ref, do_ref, st_ref, rank_ref, wd_ref, gp_ref,
             dqk_ref, dv_ref, drank_ref, dwd_ref, dbd_ref, rp_ref,
             dst_scr, send_sems, recv_sems, loc_sem):
        own, pairs = _push_copies(gp_ref, rp_ref, send_sems, recv_sems, loc_sem, scatter=True)

        @pl.when(pl.program_id(0) == 0)
        def _():
            _push_start(own, pairs)
            dst_scr[...] = jnp.zeros_like(dst_scr)
            dwd_ref[...] = jnp.zeros_like(dwd_ref)
            dbd_ref[...] = jnp.zeros_like(dbd_ref)

        mask = _iota2(C, C, 0) >= _iota2(C, C, 1)
        upp = _bf((_iota2(C, C, 0) <= _iota2(C, C, 1)).astype(F32))
        scale = GLA_HK ** -0.5
        la = la_ref[...]
        ddec_parts = []
        for hh in range(GLA_HEADS):
            la_h = la[:, hh * GLA_HK:(hh + 1) * GLA_HK]
            q = qk_ref[:, hh * GLA_HK:(hh + 1) * GLA_HK]
            k = qk_ref[:, GLA_DK + hh * GLA_HK:GLA_DK + (hh + 1) * GLA_HK]
            v = _bf(v_ref[:, hh * GLA_HV:(hh + 1) * GLA_HV])
            do = _bf(do_ref[:, hh * GLA_HV:(hh + 1) * GLA_HV])
            _, bl, eb, enb, ebl_b, qe, ke, kd = _gla_chunk_terms(la_h, q, k)
            qeb, keb, kdb = _bf(qe), _bf(ke), _bf(kd)
            st = st_ref[hh]
            dstn = dst_scr[hh]
            dstnb = _bf(dstn)
            ebl = jnp.exp(bl)
            p = jnp.where(mask, _dot_nt(qeb, keb), 0.0)
            dp = _bf(jnp.where(mask, _dot_nt(do, v), 0.0))
            dv = _dot_tn(_bf(p), do) + _dot_nt(kdb, dstnb)
            dqe = _dot(dp, keb) + _dot(do, _bf(st))
            dke = _dot_tn(dp, qeb)
            dkd = _dot(v, dstnb)
            dst_scr[hh] = _dot_tn(do, qeb) + dstn * ebl
            debl = jnp.sum(dstn * st, axis=0, keepdims=True)
            dkd_kd = dkd * kd
            db = dqe * qe - dke * ke - dkd_kd
            dbl = jnp.sum(dkd_kd, axis=0, keepdims=True) + ebl * debl
            dla = _tri_left(upp, db) + dbl
            dq = dqe * eb * scale
            dk = dke * enb + dkd * ebl_b
            dqk_ref[:, hh * GLA_HK:(hh + 1) * GLA_HK] = _bf(dq)
            dqk_ref[:, GLA_DK + hh * GLA_HK:GLA_DK + (hh + 1) * GLA_HK] = _bf(dk)
            dv_ref[:, hh * GLA_HV:(hh + 1) * GLA_HV] = _bf(dv)
            ddec_parts.append(dla * (1.0 / GLA_TAU) * (1.0 - jnp.exp(GLA_TAU * la_h)))
        ddec = jnp.concatenate(ddec_parts, axis=1)
        ddecb = _bf(ddec)
        drank_ref[...] = _bf(_dot_nt(ddecb, _bf(wd_ref[...])))
        dwd_ref[...] += _dot_tn(_bf(rank_ref[...]), ddecb)
        dbd_ref[...] += jnp.sum(ddec, axis=0, keepdims=True)

        @pl.when(pl.program_id(0) == last)
        def _():
            _push_wait(own, pairs)

    return pl.pallas_call(
        body, name="gla_bwd",
        grid=(n_chunks,),
        in_specs=[pl.BlockSpec((None, C, 1024), lambda n: (0, last - n, 0)),
                  pl.BlockSpec((None, C, 1024), lambda n: (0, last - n, 0)),
                  pl.BlockSpec((C, GLA_DK), lambda n: (last - n, 0)),
                  pl.BlockSpec((C, 1024), lambda n: (last - n, 0)),
                  pl.BlockSpec((None, GLA_HEADS, GLA_HV, GLA_HK), lambda n: (last - n, 0, 0, 0)),
                  pl.BlockSpec((C, 128), lambda n: (last - n, 0)),
                  pl.BlockSpec((128, GLA_DK), lambda n: (0, 0)),
                  _ANY],
        out_specs=(pl.BlockSpec((C, 1024), lambda n: (last - n, 0)),
                   pl.BlockSpec((C, 1024), lambda n: (last - n, 0)),
                   pl.BlockSpec((C, 128), lambda n: (last - n, 0)),
                   pl.BlockSpec((128, GLA_DK), lambda n: (0, 0)),
                   pl.BlockSpec((1, GLA_DK), lambda n: (0, 0)),
                   _ANY),
        out_shape=(jax.ShapeDtypeStruct((T, 1024), BF16),
                   jax.ShapeDtypeStruct((T, 1024), BF16),
                   jax.ShapeDtypeStruct((T, 128), BF16),
                   jax.ShapeDtypeStruct((128, GLA_DK), F32),
                   jax.ShapeDtypeStruct((1, GLA_DK), F32),
                   jax.ShapeDtypeStruct(g_p.shape, g_p.dtype)),
        scratch_shapes=[pltpu.VMEM((GLA_HEADS, GLA_HV, GLA_HK), F32)] + _PUSH_SEMS,
        compiler_params=_cparams(("arbitrary",)),
    )(projf, projb, la, do_gla, st_all, rank, wdec, g_p)


def _sb_logs(z):
    lsz = jnp.minimum(z, 0.0) - _softplus_neg_abs(z)
    return lsz, lsz - z


SB_HG_FWD = 8
SB_HG_BWD = 4
SB_KEYS = 256


def _sb_fwd_call(projb):
    T = projb.shape[1]
    B = SB_BLOCK
    HG = SB_HG_FWD
    W = HG * SB_HD
    scale = 1.0 / math.sqrt(SB_HD)

    KB = min(SB_KEYS, T)

    def body(q_ref, k_ref, v_ref, o_ref, cb_scr):
        i = pl.program_id(1)
        rows = HG * B
        after = (_iota2(KB, KB, 0) > _iota2(KB, KB, 1)).astype(F32)
        tri = _bf(jnp.concatenate([after, jnp.ones((KB, KB), F32)], axis=1))
        o_ref[...] = jnp.zeros_like(o_ref)
        cb_scr[...] = jnp.zeros_like(cb_scr)

        def block(jp, masked):
            off = pl.multiple_of(jp * KB, KB)
            z = jnp.concatenate(
                [_dot_nt(q_ref[:, hh * SB_HD:(hh + 1) * SB_HD], k_ref[pl.ds(off, KB), hh * SB_HD:(hh + 1) * SB_HD])
                 for hh in range(HG)], axis=0) * scale
            lsz, l1m = _sb_logs(z)
            if masked:
                strict = (jp * KB + _iota2(rows, KB, 1)) < (i * B + (_iota2(rows, KB, 0) & (B - 1)))
                l1m = jnp.where(strict, l1m, 0.0)
            r = _tri2_right(l1m, tri)
            cb = cb_scr[...]
            a = jnp.exp(lsz + cb + r[:, :KB])
            if masked:
                a = jnp.where(strict, a, 0.0)
            cb_scr[...] = cb + r[:, KB:]
            ab = _bf(a)
            for hh in range(HG):
                cs = slice(hh * SB_HD, (hh + 1) * SB_HD)
                o_ref[:, cs] += _dot(ab[hh * B:(hh + 1) * B, :], v_ref[pl.ds(off, KB), cs])

        jp0 = (i * B) // KB
        block(jp0, True)

        def step(jj, c):
            block(jp0 - jj, False)
            return c

        lax.fori_loop(1, jp0 + 1, step, 0)

    return pl.pallas_call(
        body, name="sb_fwd",
        grid=(SB_HEADS // HG, T // B),
        in_specs=[pl.BlockSpec((None, B, W), lambda h, i: (1, i, h)),
                  pl.BlockSpec((None, T, W), lambda h, i: (2, 0, h)),
                  pl.BlockSpec((None, T, W), lambda h, i: (3, 0, h))],
        out_specs=pl.BlockSpec((B, W), lambda h, i: (i, h)),
        out_shape=jax.ShapeDtypeStruct((T, 1024), F32),
        scratch_shapes=[pltpu.VMEM((HG * B, KB), F32)],
        compiler_params=_cparams(("arbitrary", "arbitrary")),
    )(projb, projb, projb)


def _sb_bwd_call(projb, do_sb):
    T = projb.shape[1]
    B = SB_BLOCK
    nb = T // B
    HG = SB_HG_BWD
    W = HG * SB_HD
    KB = min(SB_KEYS, T)
    nkb = T // KB
    scale = 1.0 / math.sqrt(SB_HD)

    def body(q_ref, k_ref, v_ref, do_ref, dq_ref, dk_ref, dv_ref,
             dk_scr, dv_scr, kt_scr, beta_scr, g_scr, dqt_scr):
        i = pl.program_id(1)

        @pl.when(i == 0)
        def _():
            dk_scr[...] = jnp.zeros_like(dk_scr)
            dv_scr[...] = jnp.zeros_like(dv_scr)
            for hh in range(HG):
                for jb in range(nkb):
                    kt_scr[hh, jb] = _bf(
                        k_ref[jb * KB:(jb + 1) * KB, hh * SB_HD:(hh + 1) * SB_HD].astype(F32).T)

        dqt_scr[...] = jnp.zeros_like(dqt_scr)
        later = _bf((_iota2(KB, KB, 1) > _iota2(KB, KB, 0)).astype(F32))
        earlier = _bf((_iota2(KB, KB, 1) < _iota2(KB, KB, 0)).astype(F32))
        dob = _bf(do_ref[...])
        jp0 = (i * B) // KB

        def strict_mask():
            return (jp0 * KB + _iota2(KB, W, 0)) < (i * B + (_iota2(KB, W, 1) & (B - 1)))

        def heads(fn):
            return [fn(slice(hh * SB_HD, (hh + 1) * SB_HD)) for hh in range(HG)]

        def pass1(jp, cb, masked):
            off = pl.multiple_of(jp * KB, KB)
            z = jnp.concatenate(heads(lambda cs: _dot_nt(k_ref[pl.ds(off, KB), cs], q_ref[:, cs])), axis=1) * scale
            da = jnp.concatenate(heads(lambda cs: _dot_nt(v_ref[pl.ds(off, KB), cs], dob[:, cs])), axis=1)
            lsz, l1m = _sb_logs(z)
            if masked:
                strict = strict_mask()
                l1m = jnp.where(strict, l1m, 0.0)
            a = jnp.exp(lsz + cb + _tri2_left(later, l1m))
            if masked:
                a = jnp.where(strict, a, 0.0)
            g_scr[jp] = a * da
            beta_scr[jp] = jnp.exp(lsz)
            ab = _bf(a)
            for hh in range(HG):
                cs = slice(hh * SB_HD, (hh + 1) * SB_HD)
                dv_scr[pl.ds(off, KB), cs] += _dot(ab[:, cs], dob[:, cs])
            return cb + jnp.sum(l1m, axis=0, keepdims=True)

        zero = jnp.zeros((1, W), F32)
        cb = pass1(jp0, zero, True)
        lax.fori_loop(1, jp0 + 1, lambda jj, cr: pass1(jp0 - jj, cr, False), cb)

        def pass2(jp, cg, masked):
            off = pl.multiple_of(jp * KB, KB)
            g = g_scr[jp]
            beta = beta_scr[jp]
            dz = g * (1.0 - beta) - beta * (cg + _tri2_left(earlier, g))
            if masked:
                dz = jnp.where(strict_mask(), dz, 0.0)
            dzb = _bf(dz * scale)
            for hh in range(HG):
                cs = slice(hh * SB_HD, (hh + 1) * SB_HD)
                dk_scr[pl.ds(off, KB), cs] += _dot(dzb[:, cs], q_ref[:, cs])
                dqt_scr[hh] += _dot(kt_scr[hh, jp], dzb[:, cs])
            return cg + jnp.sum(g, axis=0, keepdims=True)

        cg = lax.fori_loop(0, jp0, lambda jp, cr: pass2(jp, cr, False), zero)
        pass2(jp0, cg, True)
        for hh in range(HG):
            dq_ref[:, hh * SB_HD:(hh + 1) * SB_HD] = _bf(dqt_scr[hh].T)

        @pl.when(i == nb - 1)
        def _():
            dk_ref[...] = _bf(dk_scr[...])
            dv_ref[...] = _bf(dv_scr[...])

    return pl.pallas_call(
        body, name="sb_bwd",
        grid=(SB_HEADS // HG, nb),
        in_specs=[pl.BlockSpec((None, B, W), lambda h, i: (1, i, h)),
                  pl.BlockSpec((None, T, W), lambda h, i: (2, 0, h)),
                  pl.BlockSpec((None, T, W), lambda h, i: (3, 0, h)),
                  pl.BlockSpec((B, W), lambda h, i: (i, h))],
        out_specs=(pl.BlockSpec((B, W), lambda h, i: (i, h)),
                   pl.BlockSpec((T, W), lambda h, i: (0, h)),
                   pl.BlockSpec((T, W), lambda h, i: (0, h))),
        out_shape=(jax.ShapeDtypeStruct((T, 1024), BF16),
                   jax.ShapeDtypeStruct((T, 1024), BF16),
                   jax.ShapeDtypeStruct((T, 1024), BF16)),
        scratch_shapes=[pltpu.VMEM((T, W), F32), pltpu.VMEM((T, W), F32),
                        pltpu.VMEM((HG, nkb, SB_HD, KB), BF16),
                        pltpu.VMEM((nkb, KB, W), F32), pltpu.VMEM((nkb, KB, W), F32),
                        pltpu.VMEM((HG, SB_HD, B), F32)],
        compiler_params=_cparams(("arbitrary", "arbitrary")),
    )(projb, projb, projb, do_sb)


def _mid_call(o_gla, o_sb, projf, x, target, wpa, wpb, wo, gla_g, b_gate, final_g):
    T, D = x.shape
    tm = min(TBLK, T)

    def body(og_ref, ggate_ref, osb_ref, sgate_ref, ma_ref, mb_ref, x_ref, tgt_ref,
             wpa_ref, wpb_ref, wo_ref, glag_ref, bg_ref, fg_ref,
             dx2_ref, dogla_ref, dosb_ref, dggate_ref, dsgate_ref, dm_ref,
             mt_ref, ogt_ref, obt_ref, dx2b_ref, dya_ref, dyb_ref,
             dfg_ref, dbg_ref, dglag_ref, loss_ref):
        @pl.when(pl.program_id(0) == 0)
        def _():
            dfg_ref[...] = jnp.zeros_like(dfg_ref)
            dbg_ref[...] = jnp.zeros_like(dbg_ref)
            dglag_ref[...] = jnp.zeros_like(dglag_ref)
            loss_ref[...] = jnp.zeros_like(loss_ref)

        glag = glag_ref[...]
        ggate = ggate_ref[...]
        sg = _sigmoid(ggate)
        silu_g = ggate * sg
        ohat, rinv, nrm = [], [], []
        for hh in range(GLA_HEADS):
            oh = og_ref[:, hh * GLA_HV:(hh + 1) * GLA_HV]
            r = lax.rsqrt(jnp.mean(oh * oh, axis=-1, keepdims=True) + EPS)
            ohat.append(oh * r)
            rinv.append(r)
            nrm.append(ohat[-1] * glag)
        n_all = jnp.concatenate(nrm, axis=1)
        og = n_all * silu_g
        ogb = _bf(og)
        ya = _dot(ogb, wpa_ref[...])
        sgate = sgate_ref[...]
        ss = _sigmoid(sgate)
        silu_s = sgate * ss
        osb = osb_ref[...]
        ob = osb * silu_s
        obb = _bf(ob)
        yb = _dot(obb, wpb_ref[...])
        ga = _sigmoid(ma_ref[...] + bg_ref[:, :D])
        gb = _sigmoid(mb_ref[...] + bg_ref[:, D:])
        merged = ga * ya + gb * yb
        mgb = _bf(merged)
        x2 = x_ref[...] + _dot(mgb, wo_ref[...])
        r2 = lax.rsqrt(jnp.mean(x2 * x2, axis=-1, keepdims=True) + EPS)
        xh2 = x2 * r2
        fg = fg_ref[...]
        err = xh2 * fg - tgt_ref[...]
        loss_ref[...] += jnp.broadcast_to(
            0.5 * jnp.sum(jnp.mean(err * err, axis=-1, keepdims=True), axis=0, keepdims=True), (1, 128))
        dy = err * (1.0 / D)
        dfg_ref[...] += jnp.sum(dy * xh2, axis=0, keepdims=True)
        dxh = dy * fg
        dx2 = r2 * (dxh - xh2 * jnp.mean(dxh * xh2, axis=-1, keepdims=True))
        dx2_ref[...] = dx2
        dx2b = _bf(dx2)
        dx2b_ref[...] = dx2b
        dmerged = _dot_nt(dx2b, wo_ref[...])
        dya = dmerged * ga
        dyb = dmerged * gb
        dma = dmerged * ya * ga * (1.0 - ga)
        dmb = dmerged * yb * gb * (1.0 - gb)
        dm_ref[:, :D] = _bf(dma)
        dm_ref[:, D:] = _bf(dmb)
        dbg_ref[:, :D] += jnp.sum(dma, axis=0, keepdims=True)
        dbg_ref[:, D:] += jnp.sum(dmb, axis=0, keepdims=True)
        dyab = _bf(dya)
        dybb = _bf(dyb)
        dya_ref[...] = dyab
        dyb_ref[...] = dybb
        dog = _dot_nt(dyab, wpa_ref[...])
        dob = _dot_nt(dybb, wpb_ref[...])
        dosb_ref[...] = dob * silu_s
        dsgate_ref[...] = _bf(dob * osb * (ss * (1.0 + sgate * (1.0 - ss))))
        dn = dog * silu_g
        dggate_ref[...] = _bf(dog * n_all * (sg * (1.0 + ggate * (1.0 - sg))))
        dglag = jnp.zeros((1, GLA_HV), F32)
        for hh in range(GLA_HEADS):
            dnh = dn[:, hh * GLA_HV:(hh + 1) * GLA_HV]
            dglag = dglag + jnp.sum(dnh * ohat[hh], axis=0, keepdims=True)
            dohat = dnh * glag
            dogla_ref[:, hh * GLA_HV:(hh + 1) * GLA_HV] = rinv[hh] * (
                dohat - ohat[hh] * jnp.mean(dohat * ohat[hh], axis=-1, keepdims=True))
        dglag_ref[...] += dglag
        mt_ref[...] = _bf(merged.T)
        ogt_ref[...] = _bf(og.T)
        obt_ref[...] = _bf(ob.T)

    row = lambda i: (i, 0)
    const = lambda i: (0, 0)
    tile = pl.BlockSpec((tm, D), row)
    tile_t = pl.BlockSpec((None, D, tm), lambda i: (i, 0, 0))
    wspec = pl.BlockSpec((D, D), const)
    return pl.pallas_call(
        body, name="mid",
        grid=(T // tm,),
        in_specs=[tile,
                  pl.BlockSpec((None, tm, D), lambda i: (1, i, 0)),
                  tile,
                  pl.BlockSpec((None, tm, D), lambda i: (2, i, 0)),
                  pl.BlockSpec((None, tm, D), lambda i: (3, i, 0)),
                  pl.BlockSpec((None, tm, D), lambda i: (4, i, 0)),
                  tile, tile, wspec, wspec, wspec,
                  pl.BlockSpec((1, GLA_HV), const),
                  pl.BlockSpec((1, 2 * D), const),
                  pl.BlockSpec((1, D), const)],
        out_specs=(tile, tile, tile, tile, tile,
                   pl.BlockSpec((tm, 2 * D), row),
                   tile_t, tile_t, tile_t, tile, tile, tile,
                   pl.BlockSpec((1, D), const),
                   pl.BlockSpec((1, 2 * D), const),
                   pl.BlockSpec((1, GLA_HV), const),
                   pl.BlockSpec((1, 128), const)),
        out_shape=(jax.ShapeDtypeStruct((T, D), F32),
                   jax.ShapeDtypeStruct((T, D), F32),
                   jax.ShapeDtypeStruct((T, D), F32),
                   jax.ShapeDtypeStruct((T, D), BF16),
                   jax.ShapeDtypeStruct((T, D), BF16),
                   jax.ShapeDtypeStruct((T, 2 * D), BF16),
                   jax.ShapeDtypeStruct((T // tm, D, tm), BF16),
                   jax.ShapeDtypeStruct((T // tm, D, tm), BF16),
                   jax.ShapeDtypeStruct((T // tm, D, tm), BF16),
                   jax.ShapeDtypeStruct((T, D), BF16),
                   jax.ShapeDtypeStruct((T, D), BF16),
                   jax.ShapeDtypeStruct((T, D), BF16),
                   jax.ShapeDtypeStruct((1, D), F32),
                   jax.ShapeDtypeStruct((1, 2 * D), F32),
                   jax.ShapeDtypeStruct((1, GLA_HV), F32),
                   jax.ShapeDtypeStruct((1, 128), F32)),
        compiler_params=_cparams(("arbitrary",)),
    )(o_gla, projf, o_sb, projf, projf, projf, x, target, wpa, wpb, wo, gla_g, b_gate, final_g)


def _dh_call(pieces, dmlog, drank, wt, wr, x, dx2, norm_g):
    T, D = x.shape
    tm = min(256, T)
    npc = len(pieces)
    n_main = N_GROUPS * 1024

    def body(*refs):
        pcs = refs[:npc]
        (dm_ref, dr_ref, w_hbm, wr_ref, x_ref, dx2_ref, g_ref,
         gx_ref, dg_ref, dwr_ref, w_scr, sems) = refs[npc:]

        @pl.when(pl.program_id(0) == 0)
        def _():
            lo = pltpu.make_async_copy(w_hbm.at[pl.ds(0, RANK_COL)], w_scr.at[pl.ds(0, RANK_COL)], sems.at[0])
            hi = pltpu.make_async_copy(w_hbm.at[pl.ds(RANK_COL + GLA_RANK, n_main - RANK_COL)],
                                       w_scr.at[pl.ds(RANK_COL, n_main - RANK_COL)], sems.at[1])
            lo.start()
            hi.start()
            dg_ref[...] = jnp.zeros_like(dg_ref)
            dwr_ref[...] = jnp.zeros_like(dwr_ref)
            lo.wait()
            hi.wait()

        def w_group(g):
            return w_scr[g * 1024:(g + 1) * 1024, :]

        dr = dr_ref[...]
        dh = _dot(dr, wr_ref[...])
        for g in range(npc):
            dh = dh + _dot(pcs[g][...], w_group(g))
        dh = dh + _dot(dm_ref[:, :D], w_group(npc))
        dh = dh + _dot(dm_ref[:, D:], w_group(npc + 1))
        xv = x_ref[...]
        r = lax.rsqrt(jnp.mean(xv * xv, axis=-1, keepdims=True) + EPS)
        xhat = xv * r
        g = g_ref[...]
        dg_ref[...] += jnp.sum(dh * xhat, axis=0, keepdims=True)
        dxhat = dh * g
        gx_ref[...] = r * (dxhat - xhat * jnp.mean(dxhat * xhat, axis=-1, keepdims=True)) + dx2_ref[...]
        dwr_ref[...] += _dot_tn(dr, _bf(xhat * g))

    row = lambda i: (i, 0)
    const = lambda i: (0, 0)
    tile = pl.BlockSpec((tm, D), row)
    return pl.pallas_call(
        body, name="dh",
        grid=(T // tm,),
        in_specs=[tile] * npc + [
            pl.BlockSpec((tm, 2 * D), row),
            pl.BlockSpec((tm, 128), row),
            pl.BlockSpec(memory_space=pl.ANY),
            pl.BlockSpec((128, D), const),
            tile, tile,
            pl.BlockSpec((1, D), const)],
        out_specs=(tile, pl.BlockSpec((1, D), const), pl.BlockSpec((128, D), const)),
        out_shape=(jax.ShapeDtypeStruct((T, D), F32),
                   jax.ShapeDtypeStruct((1, D), F32),
                   jax.ShapeDtypeStruct((128, D), F32)),
        scratch_shapes=[pltpu.VMEM((n_main, D), BF16), pltpu.SemaphoreType.DMA((2,))],
        compiler_params=_cparams(("arbitrary",)),
    )(*pieces, dmlog, drank, wt, wr, x, dx2, norm_g)


def _wgrad_call(lhs_list, lhs_of_group, rhs_list, rhs_of_group, n_transposed, name):
    n_groups = len(rhs_of_group)
    n_tb, D, tb = lhs_list[0].shape
    T = n_tb * tb
    per = min(2, n_tb)
    tk = per * tb
    nk = T // tk
    nl = len(lhs_list)

    def body(*refs):
        lhs = refs[:nl]
        rhs = refs[nl:nl + n_groups]
        out_ref, acc = refs[nl + n_groups:]
        g = pl.program_id(0)
        i = pl.program_id(1)

        @pl.when(i == 0)
        def _():
            acc[...] = jnp.zeros_like(acc)

        for p in range(n_groups):
            @pl.when(g == p)
            def _(p=p):
                lref = lhs[lhs_of_group[p]]
                part = _dot(lref[0], rhs[p][0:tb, :])
                for b in range(1, per):
                    part = part + _dot(lref[b], rhs[p][b * tb:(b + 1) * tb, :])
                acc[...] += part

        @pl.when((i == nk - 1) & (g < n_transposed))
        def _():
            out_ref[...] = _bf(acc[...].T)

        @pl.when((i == nk - 1) & (g >= n_transposed))
        def _():
            out_ref[...] = _bf(acc[...])

    def lhs_spec(a):
        groups = [g for g in range(n_groups) if lhs_of_group[g] == a]
        lo, hi = min(groups), max(groups)
        assert groups == list(range(lo, hi + 1))
        return pl.BlockSpec((per, D, tb), lambda g, i: (jnp.where((g >= lo) & (g <= hi), i, 0), 0, 0))

    def rhs_spec(p):
        cb = rhs_of_group[p][1]
        return pl.BlockSpec((tk, 1024), lambda g, i: (jnp.where(g == p, i, 0), cb))

    return pl.pallas_call(
        body, name=name,
        grid=(n_groups, nk),
        in_specs=[lhs_spec(a) for a in range(nl)] + [rhs_spec(p) for p in range(n_groups)],
        out_specs=pl.BlockSpec((None, D, 1024), lambda g, i: (g, 0, 0)),
        out_shape=jax.ShapeDtypeStruct((n_groups, D, 1024), BF16),
        scratch_shapes=[pltpu.VMEM((D, 1024), F32)],
        compiler_params=_cparams(("arbitrary", "arbitrary")),
    )(*lhs_list, *[rhs_list[rhs_of_group[p][0]] for p in range(n_groups)])


def _adamw_call(parts, w, m, v, name):
    R, C = w.shape
    n_parts = parts.shape[0]
    (tr, tc), grid, idx = _tiling_2d(R, C)

    def body(p_ref, w_ref, m_ref, v_ref, g_ref, d_ref, nm_ref, nv_ref):
        g = p_ref[n_parts - 1].astype(F32)
        for k in range(n_parts - 1):
            g = g + p_ref[k].astype(F32)
        mm = ADAM_B1 * m_ref[...] + (1.0 - ADAM_B1) * g
        vv = ADAM_B2 * v_ref[...] + (1.0 - ADAM_B2) * (g * g)
        m_hat = mm / (1.0 - ADAM_B1 ** ADAM_STEP)
        v_hat = vv / (1.0 - ADAM_B2 ** ADAM_STEP)
        d_ref[...] = -ADAM_LR * (m_hat / (jnp.sqrt(v_hat) + ADAM_EPS) + ADAM_WD * w_ref[...])
        g_ref[...] = g
        nm_ref[...] = mm
        nv_ref[...] = vv

    blk = pl.BlockSpec((tr, tc), idx)
    sds = jax.ShapeDtypeStruct((R, C), F32)
    return pl.pallas_call(
        body, name=name,
        grid=grid,
        in_specs=[pl.BlockSpec((n_parts, tr, tc), lambda i: (0,) + idx(i)), blk, blk, blk],
        out_specs=(blk, blk, blk, blk),
        out_shape=(sds, sds, sds, sds),
        compiler_params=_cparams(("arbitrary",)),
    )(parts, w, m, v)


def _local_step(x, target, wt, wr, wdec, bdec, wp_shard, norm_g, gla_g, b_gate, final_g):
    D = x.shape[1]
    projf, projb, rank, ht, wp_all = _proj_call(x, norm_g, wt, wr, wp_shard)
    wp_full = wp_all.transpose(1, 0, 2, 3).reshape(3, D, D)
    o_gla, st_all, la = _gla_fwd_call(projf, projb, rank, wdec, bdec)
    o_sb = _sb_fwd_call(projb)
    (dx2, do_gla, do_sb, dggate, dsgate, dmlog, mt, ogt, obt, dx2b, dya, dyb,
     dfinal_g, db_gate, dgla_g, loss) = _mid_call(o_gla, o_sb, projf, x, target, wp_full[0], wp_full[1],
                                                 wp_full[2], gla_g, b_gate, final_g)
    dw_p = _wgrad_call([ogt, obt, mt], [0, 1, 2], [dya, dyb, dx2b], [(0, 0), (1, 0), (2, 0)], 0, "wgrad_p")
    g_p = dw_p.reshape(3, N_DEV, D // N_DEV, D).transpose(1, 0, 2, 3).reshape(N_DEV, 3 * (D // N_DEV), D)
    dqk, dgv, drank, dwdec, dbdec, r_p = _gla_bwd_call(projf, projb, la, do_gla, st_all, rank, wdec, g_p)
    dsq, dsk, dsv = _sb_bwd_call(projb, do_sb)
    pieces = [dqk, dgv, dggate, dsq, dsk, dsv, dsgate]
    grad_x, dnorm_g, dwr = _dh_call(pieces, dmlog, drank, wt, wr, x, dx2, norm_g)
    rhs_of_group = [(g, 0) for g in range(7)] + [(7, 0), (7, 1)]
    dw_in = _wgrad_call([ht], [0] * N_GROUPS, pieces + [dmlog], rhs_of_group, N_GROUPS, "wgrad_in")
    return grad_x, dw_in, dwr, r_p, dwdec, dbdec, dnorm_g, dgla_g, db_gate, dfinal_g, loss


_SM_NORM = 0
_SM_BDEC = _SM_NORM + D_MODEL
_SM_GLAG = _SM_BDEC + GLA_DK
_SM_BGATE = _SM_GLAG + GLA_HV
_SM_FINAL = _SM_BGATE + 2 * D_MODEL
_SM_REPL = _SM_FINAL + D_MODEL
_SM_LOSS = _SM_REPL
_SM_WDEC = _SM_LOSS + 128
_SM_LEN = _SM_WDEC + GLA_RANK * GLA_DK


def kernel(x, norm_g, w_in, w_dec_up, b_dec, gla_norm_g, w_pa, w_pb, b_gate, w_o, final_g, loss_target, m_norm_g, m_w_in, m_w_dec_up, m_b_dec, m_gla_norm_g, m_w_pa, m_w_pb, m_b_gate, m_w_o, m_final_g, v_norm_g, v_w_in, v_w_dec_up, v_b_dec, v_gla_norm_g, v_w_pa, v_w_pb, v_b_gate, v_w_o, v_final_g):
    D = D_MODEL
    me = 4 * lax.axis_index("x") + 2 * lax.axis_index("y") + lax.axis_index("c")

    wp_shard = jnp.stack([w_pa, w_pb, w_o]).astype(BF16)
    win_all, wdec_all = _all_gather([w_in.T.astype(BF16), w_dec_up], "gather_w")
    wt = win_all.reshape(IN_COLS, D)
    wr = jnp.pad(wt[RANK_COL:RANK_COL + GLA_RANK], ((0, 128 - GLA_RANK), (0, 0)))
    wdec_full = wdec_all.transpose(1, 0, 2).reshape(GLA_RANK, GLA_DK)
    wdec = jnp.pad(wdec_full, ((0, 128 - GLA_RANK), (0, 0)))

    (grad_x, dw_in, dwr, r_p, dwdec, dbdec, dnorm_g, dgla_g, db_gate, dfinal_g, loss) = _local_step(
        x[0], loss_target[0], wt, wr, wdec, b_dec.reshape(1, -1), wp_shard,
        norm_g.reshape(1, -1), gla_norm_g.reshape(1, -1), b_gate.reshape(1, -1), final_g.reshape(1, -1))

    dmain = dw_in.reshape(N_GROUPS * 1024, D)
    drank = dwr[:GLA_RANK].astype(BF16)

    def part_for(p):
        lo, hi = p * SHARD_COLS, (p + 1) * SHARD_COLS
        pieces = []
        if lo < RANK_COL:
            pieces.append(dmain[lo:min(hi, RANK_COL)])
        if lo < RANK_COL + GLA_RANK and hi > RANK_COL:
            pieces.append(drank[max(lo, RANK_COL) - RANK_COL:min(hi, RANK_COL + GLA_RANK) - RANK_COL])
        if hi > RANK_COL + GLA_RANK:
            pieces.append(dmain[max(lo, RANK_COL + GLA_RANK) - GLA_RANK:hi - GLA_RANK])
        return pieces[0] if len(pieces) == 1 else jnp.concatenate(pieces, axis=0)

    g_in = jnp.stack([part_for(p) for p in range(N_DEV)])
    small = jnp.concatenate([
        dnorm_g.reshape(-1), dbdec.reshape(-1), dgla_g.reshape(-1), db_gate.reshape(-1), dfinal_g.reshape(-1),
        loss.reshape(-1), dwdec[:GLA_RANK].reshape(-1)])
    c_idx = lax.axis_index("c").astype(jnp.int32).reshape(1)
    (p_in,) = _pair_exchange([g_in], "pair_g")
    s_in = _pair_add_call(g_in, p_in, c_idx, "pair_add_in")
    (r_in,) = _chip_exchange([s_in], "scatter_g")
    (r_small,) = _all_gather([small.reshape(1, _SM_LEN)], "gather_small")

    gw_in, d_in, nm_in, nv_in = (a.T for a in _adamw_call(r_in, w_in.T, m_w_in.T, v_w_in.T, "adamw_in"))
    wp_f32 = jnp.concatenate([w_pa, w_pb, w_o], axis=0)
    mp = jnp.concatenate([m_w_pa, m_w_pb, m_w_o], axis=0)
    vp = jnp.concatenate([v_w_pa, v_w_pb, v_w_o], axis=0)
    gp, dp, nmp, nvp = _adamw_call(r_p, wp_f32, mp, vp, "adamw_p")
    rows = D // N_DEV

    def split3(a):
        return a[:rows], a[rows:2 * rows], a[2 * rows:]

    g_pa, g_pb, g_o = split3(gp)
    d_pa, d_pb, d_o = split3(dp)
    nm_pa, nm_pb, nm_o = split3(nmp)
    nv_pa, nv_pb, nv_o = split3(nvp)

    w_rep = jnp.concatenate([norm_g, b_dec, gla_norm_g, b_gate, final_g]).reshape(1, _SM_REPL)
    m_rep = jnp.concatenate([m_norm_g, m_b_dec, m_gla_norm_g, m_b_gate, m_final_g]).reshape(1, _SM_REPL)
    v_rep = jnp.concatenate([v_norm_g, v_b_dec, v_gla_norm_g, v_b_gate, v_final_g]).reshape(1, _SM_REPL)
    g_rep, d_rep, nm_rep, nv_rep = _adamw_call(r_small[:, :, :_SM_REPL], w_rep, m_rep, v_rep, "adamw_rep")

    def split_rep(a):
        a = a.reshape(-1)
        return (a[_SM_NORM:_SM_BDEC], a[_SM_BDEC:_SM_GLAG], a[_SM_GLAG:_SM_BGATE],
                a[_SM_BGATE:_SM_FINAL], a[_SM_FINAL:_SM_REPL])

    g_norm, g_bdec, g_glag, g_bgate, g_final = split_rep(g_rep)
    d_norm, d_bdec, d_glag, d_bgate, d_final = split_rep(d_rep)
    nm_norm, nm_bdec, nm_glag, nm_bgate, nm_final = split_rep(nm_rep)
    nv_norm, nv_bdec, nv_glag, nv_bgate, nv_final = split_rep(nv_rep)

    wdec_parts = r_small[:, 0, _SM_WDEC:].reshape(N_DEV, GLA_RANK, GLA_DK)
    cols = GLA_DK // N_DEV
    wdec_mine = lax.dynamic_slice_in_dim(wdec_parts, me * cols, cols, axis=2)
    g_wdec, d_wdec, nm_wdec, nv_wdec = _adamw_call(wdec_mine, w_dec_up, m_w_dec_up, v_w_dec_up, "adamw_dec")

    loss_total = jnp.sum(r_small[:, 0, _SM_LOSS])

    return (loss_total, grad_x[None],
            g_norm, gw_in, g_wdec, g_bdec, g_glag, g_pa, g_pb, g_bgate, g_o, g_final,
            d_norm, d_in, d_wdec, d_bdec, d_glag, d_pa, d_pb, d_bgate, d_o, d_final,
            nm_norm, nm_in, nm_wdec, nm_bdec, nm_glag, nm_pa, nm_pb, nm_bgate, nm_o, nm_final,
            nv_norm, nv_in, nv_wdec, nv_bdec, nv_glag, nv_pa, nv_pb, nv_bgate, nv_o, nv_final)
```

```python
import functools
import math

import jax
import jax.numpy as jnp
from jax import lax
from jax.experimental import pallas as pl
from jax.experimental.pallas import tpu as pltpu

F32 = jnp.float32
BF16 = jnp.bfloat16

N_DEV = 8
D_MODEL = 1024
GLA_HEADS = 4
GLA_HK = 128
GLA_HV = 256
GLA_DK = 512
GLA_RANK = 16
GLA_TAU = 16.0
GLA_CHUNK = 64
SB_HEADS = 8
SB_HD = 128
SB_BLOCK = 128
EPS = 1e-6
N_GROUPS = 9
RANK_COL = 3072
IN_COLS = 9232
SHARD_COLS = IN_COLS // N_DEV

ADAM_LR = 0.001
ADAM_B1 = 0.9
ADAM_B2 = 0.999
ADAM_EPS = 1e-08
ADAM_WD = 0.01
ADAM_STEP = 10

VMEM_LIMIT = 56 * 1024 * 1024
TBLK = 256


def _cparams(sem=None):
    return pltpu.CompilerParams(dimension_semantics=sem, vmem_limit_bytes=VMEM_LIMIT)


def _tiling_2d(rows, cols):
    if rows * cols <= 128 * 1024:
        return (rows, cols), (1,), lambda i: (0, 0)
    if rows % 128 == 0:
        return (128, cols), (rows // 128,), lambda i: (i, 0)
    tc = 256 if cols % 256 == 0 else cols
    return (rows, tc), (cols // tc,), lambda i: (0, i)


def _dot(a, b):
    return jnp.dot(a, b, preferred_element_type=F32)


def _dot_nt(a, b):
    return lax.dot_general(a, b, (((1,), (1,)), ((), ())), preferred_element_type=F32)


def _dot_tn(a, b):
    return lax.dot_general(a, b, (((0,), (0,)), ((), ())), preferred_element_type=F32)


def _bf(x):
    return x.astype(BF16)


def _split3(x):
    hi = x.astype(BF16)
    r = x - hi.astype(F32)
    mid = r.astype(BF16)
    lo = (r - mid.astype(F32)).astype(BF16)
    return hi, mid, lo


def _tri_left(tri, x):
    hi, mid, lo = _split3(x)
    return _dot(tri, hi) + _dot(tri, mid) + _dot(tri, lo)


def _split2(x):
    hi = lax.bitcast_convert_type(lax.bitcast_convert_type(x, jnp.uint32) & jnp.uint32(0xFFFF0000), F32)
    return hi.astype(BF16), (x - hi).astype(BF16)


def _tri2_left(tri, x):
    hi, lo = _split2(x)
    return _dot(tri, hi) + _dot(tri, lo)


def _tri2_right(x, tri):
    hi, lo = _split2(x)
    return _dot(hi, tri) + _dot(lo, tri)


def _iota2(n, m, dim):
    return lax.broadcasted_iota(jnp.int32, (n, m), dim)


def _sigmoid(x):
    return 1.0 / (1.0 + jnp.exp(-x))


def _softplus_neg_abs(z):
    return jnp.log(1.0 + jnp.exp(-jnp.abs(z)))


_ANY = pl.BlockSpec(memory_space=pl.ANY)


def _mesh_pos():
    return lax.axis_index("x"), lax.axis_index("y"), lax.axis_index("c")


def _other_chips(x, y):
    return [(1 - x, y), (x, 1 - y), (1 - x, 1 - y)]


def _rcopy(src, dst, send_sem, recv_sem, to):
    return pltpu.make_async_remote_copy(src_ref=src, dst_ref=dst, send_sem=send_sem, recv_sem=recv_sem,
                                        device_id=to, device_id_type=pl.DeviceIdType.MESH)


def _push_copies(src_ref, dst_ref, send_sems, recv_sems, loc_sem, scatter):
    x, y, c = _mesh_pos()
    me = 4 * x + 2 * y + c
    own = pltpu.make_async_copy(src_ref.at[me] if scatter else src_ref, dst_ref.at[me], loc_sem)
    pairs = []
    for k in range(1, N_DEV):
        px = 1 - x if k & 4 else x
        py = 1 - y if k & 2 else y
        pc = 1 - c if k & 1 else c
        pid = 4 * px + 2 * py + pc
        src = src_ref.at[pid] if scatter else src_ref
        send = _rcopy(src, dst_ref.at[me], send_sems.at[k - 1], recv_sems.at[k - 1], (px, py, pc))
        recv = _rcopy(src, dst_ref.at[pid], send_sems.at[k - 1], recv_sems.at[k - 1], (px, py, pc))
        pairs.append((send, recv))
    return own, pairs


def _push_start(own, pairs):
    own.start()
    for send, _ in pairs:
        send.start()


def _push_wait(own, pairs):
    for _, recv in pairs:
        recv.wait_recv()
    for send, _ in pairs:
        send.wait_send()
    own.wait()


_PUSH_SEMS = [pltpu.SemaphoreType.DMA((N_DEV - 1,)), pltpu.SemaphoreType.DMA((N_DEV - 1,)),
              pltpu.SemaphoreType.DMA]


def _all_gather(arrs, name):
    n = len(arrs)

    def body(*refs):
        ins = refs[:n]
        outs = refs[n:2 * n]
        send_sems, recv_sems, loc_sems = refs[2 * n:]
        x, y, c = _mesh_pos()
        sib = (x, y, 1 - c)
        chips = _other_chips(x, y)

        def place(a, px, py, pc):
            return outs[a].at[4 * px + 2 * py + pc]

        def copy(a, k, block, to, src=None):
            dst = place(a, *block)
            return _rcopy(dst if src is None else src, dst, send_sems.at[a, k], recv_sems.at[a, k], to)

        mine = [pltpu.make_async_copy(ins[a], place(a, x, y, c), loc_sems.at[a]) for a in range(n)]
        for cp in mine:
            cp.start()
        first = [copy(a, 0, (x, y, c), sib, src=ins[a]) for a in range(n)]
        for j, chip in enumerate(chips):
            first += [copy(a, 1 + j, (x, y, c), (*chip, c), src=ins[a]) for a in range(n)]
        for cp in first:
            cp.start()
        passed = []
        for j, chip in enumerate(chips):
            for a in range(n):
                copy(a, 1 + j, (*chip, c), (x, y, c)).wait_recv()
                fwd = copy(a, 4 + j, (*chip, c), sib)
                fwd.start()
                passed.append(fwd)
        for a in range(n):
            copy(a, 0, sib, (x, y, c)).wait_recv()
        for j, chip in enumerate(chips):
            for a in range(n):
                copy(a, 4 + j, (*chip, 1 - c), (x, y, c)).wait_recv()
        for cp in first + passed:
            cp.wait_send()
        for cp in mine:
            cp.wait()

    return pl.pallas_call(
        body, name=name,
        out_shape=tuple(jax.ShapeDtypeStruct((N_DEV,) + a.shape, a.dtype) for a in arrs),
        in_specs=[_ANY] * n,
        out_specs=tuple([_ANY] * n),
        scratch_shapes=[pltpu.SemaphoreType.DMA((n, 7)), pltpu.SemaphoreType.DMA((n, 7)),
                        pltpu.SemaphoreType.DMA((n,))],
    )(*arrs)


def _pair_exchange(arrs, name):
    n = len(arrs)

    def body(*refs):
        ins = refs[:n]
        outs = refs[n:2 * n]
        send_sems, recv_sems = refs[2 * n:]
        x, y, c = _mesh_pos()
        copies = []
        for a in range(n):
            for q in range(4):
                cp = _rcopy(ins[a].at[2 * q + (1 - c)], outs[a].at[q], send_sems.at[a, q], recv_sems.at[a, q],
                            (x, y, 1 - c))
                cp.start()
                copies.append(cp)
        for cp in copies:
            cp.wait_recv()
        for cp in copies:
            cp.wait_send()

    return pl.pallas_call(
        body, name=name,
        out_shape=tuple(jax.ShapeDtypeStruct((4,) + a.shape[1:], a.dtype) for a in arrs),
        in_specs=[_ANY] * n,
        out_specs=tuple([_ANY] * n),
        scratch_shapes=[pltpu.SemaphoreType.DMA((n, 4)), pltpu.SemaphoreType.DMA((n, 4))],
    )(*arrs)


def _pair_add_call(parts, recv, c_idx, name):
    _, R, C = parts.shape
    (tr, tc), (steps,), idx = _tiling_2d(R, C)

    def body(c_ref, p_ref, r_ref, o_ref):
        o_ref[...] = (p_ref[...].astype(F32) + r_ref[...].astype(F32)).astype(o_ref.dtype)

    return pl.pallas_call(
        body, name=name,
        grid_spec=pltpu.PrefetchScalarGridSpec(
            num_scalar_prefetch=1,
            grid=(4, steps),
            in_specs=[pl.BlockSpec((None, tr, tc), lambda q, i, c_ref: (2 * q + c_ref[0],) + idx(i)),
                      pl.BlockSpec((None, tr, tc), lambda q, i, c_ref: (q,) + idx(i))],
            out_specs=pl.BlockSpec((None, tr, tc), lambda q, i, c_ref: (q,) + idx(i))),
        out_shape=jax.ShapeDtypeStruct((4, R, C), parts.dtype),
        compiler_params=_cparams(("arbitrary", "arbitrary")),
    )(c_idx, parts, recv)


def _chip_exchange(arrs, name):
    n = len(arrs)

    def body(*refs):
        ins = refs[:n]
        outs = refs[n:2 * n]
        send_sems, recv_sems, loc_sems = refs[2 * n:]
        x, y, c = _mesh_pos()
        mine = [pltpu.make_async_copy(ins[a].at[2 * x + y], outs[a].at[3], loc_sems.at[a]) for a in range(n)]
        for cp in mine:
            cp.start()
        copies = []
        for j, (px, py) in enumerate(_other_chips(x, y)):
            for a in range(n):
                cp = _rcopy(ins[a].at[2 * px + py], outs[a].at[j], send_sems.at[a, j], recv_sems.at[a, j],
                            (px, py, c))
                cp.start()
                copies.append(cp)
        for cp in copies:
            cp.wait_recv()
        for cp in copies:
            cp.wait_send()
        for cp in mine:
            cp.wait()

    return pl.pallas_call(
        body, name=name,
        out_shape=tuple(jax.ShapeDtypeStruct(a.shape, a.dtype) for a in arrs),
        in_specs=[_ANY] * n,
        out_specs=tuple([_ANY] * n),
        scratch_shapes=[pltpu.SemaphoreType.DMA((n, 3)), pltpu.SemaphoreType.DMA((n, 3)),
                        pltpu.SemaphoreType.DMA((n,))],
    )(*arrs)


def _group_row(g):
    return GLA_RANK * (g * (1024 // GLA_RANK) + (g >= RANK_COL // 1024))


def _proj_call(x, norm_g, wt, wr, wp_shard):
    T, D = x.shape
    tm = min(512, T)
    assert tm % TBLK == 0
    n_i = T // tm

    def f_slot(j):
        return ((j >= 2).astype(jnp.int32) + (j >= 6).astype(jnp.int32)
                + (j >= 7).astype(jnp.int32) + (j >= 8).astype(jnp.int32))

    def b_slot(j):
        return (j >= 3).astype(jnp.int32) + (j >= 4).astype(jnp.int32) + (j >= 5).astype(jnp.int32)

    def body(x_ref, g_ref, w_ref, wr_ref, wp_ref, pf_ref, pb_ref, rank_ref, ht_ref, wpall_ref,
             h_scr, send_sems, recv_sems, loc_sem):
        i = pl.program_id(0)
        j = pl.program_id(1)
        own, pairs = _push_copies(wp_ref, wpall_ref, send_sems, recv_sems, loc_sem, scatter=False)

        @pl.when((i == 0) & (j == 0))
        def _():
            _push_start(own, pairs)

        @pl.when(j == 0)
        def _():
            xv = x_ref[...]
            r = lax.rsqrt(jnp.mean(xv * xv, axis=-1, keepdims=True) + EPS)
            h = (xv * r) * g_ref[...]
            hb = _bf(h)
            h_scr[...] = hb
            for b in range(tm // TBLK):
                ht_ref[b] = _bf(h[b * TBLK:(b + 1) * TBLK].T)
            rank_ref[...] = _dot_nt(hb, wr_ref[...])

        is_b = (j == 1) | ((j >= 3) & (j <= 5))

        @pl.when(is_b)
        def _():
            pb_ref[...] = _bf(_dot_nt(h_scr[...], w_ref[...]))

        @pl.when(jnp.logical_not(is_b))
        def _():
            pf_ref[...] = _dot_nt(h_scr[...], w_ref[...])

        @pl.when((i == n_i - 1) & (j == N_GROUPS - 1))
        def _():
            _push_wait(own, pairs)

    return pl.pallas_call(
        body, name="proj",
        grid=(n_i, N_GROUPS),
        in_specs=[pl.BlockSpec((tm, D), lambda i, j: (i, 0)),
                  pl.BlockSpec((1, D), lambda i, j: (0, 0)),
                  pl.BlockSpec((pl.Element(1024), pl.Element(D)), lambda i, j: (_group_row(j), 0)),
                  pl.BlockSpec((128, D), lambda i, j: (0, 0)),
                  _ANY],
        out_specs=(pl.BlockSpec((None, tm, 1024), lambda i, j: (f_slot(j), i, 0)),
                   pl.BlockSpec((None, tm, 1024), lambda i, j: (b_slot(j), i, 0)),
                   pl.BlockSpec((tm, 128), lambda i, j: (i, 0)),
                   pl.BlockSpec((tm // TBLK, D, TBLK), lambda i, j: (i, 0, 0)),
                   _ANY),
        out_shape=(jax.ShapeDtypeStruct((5, T, 1024), F32),
                   jax.ShapeDtypeStruct((4, T, 1024), BF16),
                   jax.ShapeDtypeStruct((T, 128), F32),
                   jax.ShapeDtypeStruct((T // TBLK, D, TBLK), BF16),
                   jax.ShapeDtypeStruct((N_DEV,) + wp_shard.shape, wp_shard.dtype)),
        scratch_shapes=[pltpu.VMEM((tm, D), BF16)] + _PUSH_SEMS,
        compiler_params=_cparams(("arbitrary", "arbitrary")),
    )(x, norm_g, wt, wr, wp_shard)


def _gla_chunk_terms(la_h, q, k):
    C = GLA_CHUNK
    low = _bf((_iota2(C, C, 0) >= _iota2(C, C, 1)).astype(F32))
    b = _tri_left(low, la_h)
    bl = b[C - 1:C, :]
    eb = jnp.exp(b)
    enb = jnp.exp(-b)
    ebl_b = jnp.exp(bl - b)
    scale = GLA_HK ** -0.5
    qe = q * eb * scale
    ke = k * enb
    kd = k * ebl_b
    return b, bl, eb, enb, ebl_b, qe, ke, kd


def _gla_fwd_call(projf, projb, rank, wdec, bdec):
    T = projf.shape[1]
    C = GLA_CHUNK
    n_chunks = T // C

    def body(qk_ref, v_ref, rank_ref, wd_ref, bd_ref, o_ref, st_ref, la_ref, st_scr):
        @pl.when(pl.program_id(0) == 0)
        def _():
            st_scr[...] = jnp.zeros_like(st_scr)

        dec = _dot(_bf(rank_ref[...]), _bf(wd_ref[...])) + bd_ref[...]
        la = (jnp.minimum(dec, 0.0) - _softplus_neg_abs(dec)) / GLA_TAU
        la_ref[...] = la
        mask = _iota2(C, C, 0) >= _iota2(C, C, 1)
        _, bl, _, _, _, qe, ke, kd = _gla_chunk_terms(la, qk_ref[:, :GLA_DK], qk_ref[:, GLA_DK:])
        qeb, keb, kdb = _bf(qe), _bf(ke), _bf(kd)
        ebl = jnp.exp(bl)
        heads = range(GLA_HEADS)
        ks = [slice(hh * GLA_HK, (hh + 1) * GLA_HK) for hh in heads]
        vs = [slice(hh * GLA_HV, (hh + 1) * GLA_HV) for hh in heads]
        st = [st_scr[hh] for hh in heads]
        p = [_bf(jnp.where(mask, _dot_nt(qeb[:, ks[hh]], keb[:, ks[hh]]), 0.0)) for hh in heads]
        inter = [_dot_nt(qeb[:, ks[hh]], _bf(st[hh])) for hh in heads]
        upd = [_dot_tn(v_ref[:, vs[hh]], kdb[:, ks[hh]]) for hh in heads]
        intra = [_dot(p[hh], v_ref[:, vs[hh]]) for hh in heads]
        for hh in heads:
            st_ref[hh] = st[hh]
            o_ref[:, vs[hh]] = intra[hh] + inter[hh]
            st_scr[hh] = st[hh] * ebl[:, ks[hh]] + upd[hh]

    return pl.pallas_call(
        body, name="gla_fwd",
        grid=(n_chunks,),
        in_specs=[pl.BlockSpec((None, C, 1024), lambda n: (0, n, 0)),
                  pl.BlockSpec((None, C, 1024), lambda n: (0, n, 0)),
                  pl.BlockSpec((C, 128), lambda n: (n, 0)),
                  pl.BlockSpec((128, GLA_DK), lambda n: (0, 0)),
                  pl.BlockSpec((1, GLA_DK), lambda n: (0, 0))],
        out_specs=(pl.BlockSpec((C, 1024), lambda n: (n, 0)),
                   pl.BlockSpec((None, GLA_HEADS, GLA_HV, GLA_HK), lambda n: (n, 0, 0, 0)),
                   pl.BlockSpec((C, GLA_DK), lambda n: (n, 0))),
        out_shape=(jax.ShapeDtypeStruct((T, 1024), F32),
                   jax.ShapeDtypeStruct((n_chunks, GLA_HEADS, GLA_HV, GLA_HK), F32),
                   jax.ShapeDtypeStruct((T, GLA_DK), F32)),
        scratch_shapes=[pltpu.VMEM((GLA_HEADS, GLA_HV, GLA_HK), F32)],
        compiler_params=_cparams(("arbitrary",)),
    )(projf, projb, rank, wdec, bdec)


def _gla_bwd_call(projf, projb, la, do_gla, st_all, rank, wdec, g_p):
    T = projf.shape[1]
    C = GLA_CHUNK
    n_chunks = T // C
    last = n_chunks - 1

    def body(qk_ref, v_ref, la_ref, do_ref, st_ref, rank_ref, wd_ref, gp_ref,
             dqk_ref, dv_ref, drank_ref, dwd_ref, dbd_ref, rp_ref,
             dst_scr, send_sems, recv_sems, loc_sem):
        own, pairs = _push_copies(gp_ref, rp_ref, send_sems, recv_sems, loc_sem, scatter=True)

        @pl.when(pl.program_id(0) == 0)
        def _():
            _push_start(own, pairs)
            dst_scr[...] = jnp.zeros_like(dst_scr)
            dwd_ref[...] = jnp.zeros_like(dwd_ref)
            dbd_ref[...] = jnp.zeros_like(dbd_ref)

        mask = _iota2(C, C, 0) >= _iota2(C, C, 1)
        upp = _bf((_iota2(C, C, 0) <= _iota2(C, C, 1)).astype(F32))
        scale = GLA_HK ** -0.5
        la = la_ref[...]
        _, bl, eb, enb, ebl_b, qe, ke, kd = _gla_chunk_terms(la, qk_ref[:, :GLA_DK], qk_ref[:, GLA_DK:])
        qeb, keb, kdb = _bf(qe), _bf(ke), _bf(kd)
        ebl = jnp.exp(bl)
        heads = range(GLA_HEADS)
        ks = [slice(hh * GLA_HK, (hh + 1) * GLA_HK) for hh in heads]
        vs = [slice(hh * GLA_HV, (hh + 1) * GLA_HV) for hh in heads]
        v = [v_ref[:, vs[hh]] for hh in heads]
        do = [_bf(do_ref[:, vs[hh]]) for hh in heads]
        st = [st_ref[hh] for hh in heads]
        dstn = [dst_scr[hh] for hh in heads]
        dstnb = [_bf(dstn[hh]) for hh in heads]
        p = [_bf(jnp.where(mask, _dot_nt(qeb[:, ks[hh]], keb[:, ks[hh]]), 0.0)) for hh in heads]
        dp = [_bf(jnp.where(mask, _dot_nt(do[hh], v[hh]), 0.0)) for hh in heads]
        dkd = [_dot(v[hh], dstnb[hh]) for hh in heads]
        dv_inter = [_dot_nt(kdb[:, ks[hh]], dstnb[hh]) for hh in heads]
        dqe_inter = [_dot(do[hh], _bf(st[hh])) for hh in heads]
        dst_new = [_dot_tn(do[hh], qeb[:, ks[hh]]) + dstn[hh] * ebl[:, ks[hh]] for hh in heads]
        debl = jnp.concatenate([jnp.sum(dstn[hh] * st[hh], axis=0, keepdims=True) for hh in heads], axis=1)
        dv = [_dot_tn(p[hh], do[hh]) + dv_inter[hh] for hh in heads]
        dqe = jnp.concatenate([_dot(dp[hh], keb[:, ks[hh]]) + dqe_inter[hh] for hh in heads], axis=1)
        dke = jnp.concatenate([_dot_tn(dp[hh], qeb[:, ks[hh]]) for hh in heads], axis=1)
        dkd = jnp.concatenate(dkd, axis=1)
        for hh in heads:
            dst_scr[hh] = dst_new[hh]
            dv_ref[:, vs[hh]] = _bf(dv[hh])
        dkd_kd = dkd * kd
        db = dqe * qe - dke * ke - dkd_kd
        dbl = jnp.sum(dkd_kd, axis=0, keepdims=True) + ebl * debl
        dla = _tri_left(upp, db) + dbl
        dqk_ref[:, :GLA_DK] = _bf(dqe * eb * scale)
        dqk_ref[:, GLA_DK:] = _bf(dke * enb + dkd * ebl_b)
        ddec = dla * (1.0 / GLA_TAU) * (1.0 - jnp.exp(GLA_TAU * la))
        ddecb = _bf(ddec)
        drank_ref[...] = _bf(_dot_nt(ddecb, _bf(wd_ref[...])))
        dwd_ref[...] += _dot_tn(_bf(rank_ref[...]), ddecb)
        dbd_ref[...] += jnp.sum(ddec, axis=0, keepdims=True)

        @pl.when(pl.program_id(0) == last)
        def _():
            _push_wait(own, pairs)

    return pl.pallas_call(
        body, name="gla_bwd",
        grid=(n_chunks,),
        in_specs=[pl.BlockSpec((None, C, 1024), lambda n: (0, last - n, 0)),
                  pl.BlockSpec((None, C, 1024), lambda n: (0, last - n, 0)),
                  pl.BlockSpec((C, GLA_DK), lambda n: (last - n, 0)),
                  pl.BlockSpec((C, 1024), lambda n: (last - n, 0)),
                  pl.BlockSpec((None, GLA_HEADS, GLA_HV, GLA_HK), lambda n: (last - n, 0, 0, 0)),
                  pl.BlockSpec((C, 128), lambda n: (last - n, 0)),
                  pl.BlockSpec((128, GLA_DK), lambda n: (0, 0)),
                  _ANY],
        out_specs=(pl.BlockSpec((C, 1024), lambda n: (last - n, 0)),
                   pl.BlockSpec((C, 1024), lambda n: (last - n, 0)),
                   pl.BlockSpec((C, 128), lambda n: (last - n, 0)),
                   pl.BlockSpec((128, GLA_DK), lambda n: (0, 0)),
                   pl.BlockSpec((1, GLA_DK), lambda n: (0, 0)),
                   _ANY),
        out_shape=(jax.ShapeDtypeStruct((T, 1024), BF16),
                   jax.ShapeDtypeStruct((T, 1024), BF16),
                   jax.ShapeDtypeStruct((T, 128), BF16),
                   jax.ShapeDtypeStruct((128, GLA_DK), F32),
                   jax.ShapeDtypeStruct((1, GLA_DK), F32),
                   jax.ShapeDtypeStruct(g_p.shape, g_p.dtype)),
        scratch_shapes=[pltpu.VMEM((GLA_HEADS, GLA_HV, GLA_HK), F32)] + _PUSH_SEMS,
        compiler_params=_cparams(("arbitrary",)),
    )(projf, projb, la, do_gla, st_all, rank, wdec, g_p)


def _sb_logs(z):
    lsz = jnp.minimum(z, 0.0) - _softplus_neg_abs(z)
    return lsz, lsz - z


SB_HG_FWD = 8
SB_HG_BWD = 4
SB_KEYS = 256


def _sb_fwd_call(projb):
    T = projb.shape[1]
    B = SB_BLOCK
    HG = SB_HG_FWD
    W = HG * SB_HD
    scale = 1.0 / math.sqrt(SB_HD)

    KB = min(SB_KEYS, T)

    def body(q_ref, k_ref, v_ref, o_ref, cb_scr):
        i = pl.program_id(1)
        rows = HG * B
        after = (_iota2(KB, KB, 0) > _iota2(KB, KB, 1)).astype(F32)
        tri = _bf(jnp.concatenate([after, jnp.ones((KB, KB), F32)], axis=1))
        o_ref[...] = jnp.zeros_like(o_ref)
        cb_scr[...] = jnp.zeros_like(cb_scr)

        def block(jp, masked):
            off = pl.multiple_of(jp * KB, KB)
            z = jnp.concatenate(
                [_dot_nt(q_ref[:, hh * SB_HD:(hh + 1) * SB_HD], k_ref[pl.ds(off, KB), hh * SB_HD:(hh + 1) * SB_HD])
                 for hh in range(HG)], axis=0) * scale
            lsz, l1m = _sb_logs(z)
            if masked:
                strict = (jp * KB + _iota2(rows, KB, 1)) < (i * B + (_iota2(rows, KB, 0) & (B - 1)))
                l1m = jnp.where(strict, l1m, 0.0)
            r = _tri2_right(l1m, tri)
            cb = cb_scr[...]
            a = jnp.exp(lsz + cb + r[:, :KB])
            if masked:
                a = jnp.where(strict, a, 0.0)
            cb_scr[...] = cb + r[:, KB:]
            ab = _bf(a)
            for hh in range(HG):
                cs = slice(hh * SB_HD, (hh + 1) * SB_HD)
                o_ref[:, cs] += _dot(ab[hh * B:(hh + 1) * B, :], v_ref[pl.ds(off, KB), cs])

        jp0 = (i * B) // KB
        block(jp0, True)

        def step(jj, c):
            block(jp0 - jj, False)
            return c

        lax.fori_loop(1, jp0 + 1, step, 0)

    return pl.pallas_call(
        body, name="sb_fwd",
        grid=(SB_HEADS // HG, T // B),
        in_specs=[pl.BlockSpec((None, B, W), lambda h, i: (1, i, h)),
                  pl.BlockSpec((None, T, W), lambda h, i: (2, 0, h)),
                  pl.BlockSpec((None, T, W), lambda h, i: (3, 0, h))],
        out_specs=pl.BlockSpec((B, W), lambda h, i: (i, h)),
        out_shape=jax.ShapeDtypeStruct((T, 1024), F32),
        scratch_shapes=[pltpu.VMEM((HG * B, KB), F32)],
        compiler_params=_cparams(("arbitrary", "arbitrary")),
    )(projb, projb, projb)


def _sb_bwd_call(projb, do_sb):
    T = projb.shape[1]
    B = SB_BLOCK
    nb = T // B
    HG = SB_HG_BWD
    W = HG * SB_HD
    KB = min(SB_KEYS, T)
    nkb = T // KB
    scale = 1.0 / math.sqrt(SB_HD)

    def body(q_ref, k_ref, v_ref, do_ref, dq_ref, dk_ref, dv_ref,
             dk_scr, dv_scr, kt_scr, beta_scr, g_scr, dqt_scr):
        i = pl.program_id(1)

        @pl.when(i == 0)
        def _():
            dk_scr[...] = jnp.zeros_like(dk_scr)
            dv_scr[...] = jnp.zeros_like(dv_scr)
            for hh in range(HG):
                for jb in range(nkb):
                    kt_scr[hh, jb] = _bf(
                        k_ref[jb * KB:(jb + 1) * KB, hh * SB_HD:(hh + 1) * SB_HD].astype(F32).T)

        dqt_scr[...] = jnp.zeros_like(dqt_scr)
        later = _bf((_iota2(KB, KB, 1) > _iota2(KB, KB, 0)).astype(F32))
        earlier = _bf((_iota2(KB, KB, 1) < _iota2(KB, KB, 0)).astype(F32))
        dob = _bf(do_ref[...])
        jp0 = (i * B) // KB

        def strict_mask():
            return (jp0 * KB + _iota2(KB, W, 0)) < (i * B + (_iota2(KB, W, 1) & (B - 1)))

        def heads(fn):
            return [fn(slice(hh * SB_HD, (hh + 1) * SB_HD)) for hh in range(HG)]

        def pass1(jp, cb, masked):
            off = pl.multiple_of(jp * KB, KB)
            z = jnp.concatenate(heads(lambda cs: _dot_nt(k_ref[pl.ds(off, KB), cs], q_ref[:, cs])), axis=1) * scale
            da = jnp.concatenate(heads(lambda cs: _dot_nt(v_ref[pl.ds(off, KB), cs], dob[:, cs])), axis=1)
            lsz, l1m = _sb_logs(z)
            if masked:
                strict = strict_mask()
                l1m = jnp.where(strict, l1m, 0.0)
            a = jnp.exp(lsz + cb + _tri2_left(later, l1m))
            if masked:
                a = jnp.where(strict, a, 0.0)
            g_scr[jp] = a * da
            beta_scr[jp] = jnp.exp(lsz)
            ab = _bf(a)
            for hh in range(HG):
                cs = slice(hh * SB_HD, (hh + 1) * SB_HD)
                dv_scr[pl.ds(off, KB), cs] += _dot(ab[:, cs], dob[:, cs])
            return cb + jnp.sum(l1m, axis=0, keepdims=True)

        zero = jnp.zeros((1, W), F32)
        cb = pass1(jp0, zero, True)
        lax.fori_loop(1, jp0 + 1, lambda jj, cr: pass1(jp0 - jj, cr, False), cb)

        def pass2(jp, cg, masked):
            off = pl.multiple_of(jp * KB, KB)
            g = g_scr[jp]
            beta = beta_scr[jp]
            dz = g * (1.0 - beta) - beta * (cg + _tri2_left(earlier, g))
            if masked:
                dz = jnp.where(strict_mask(), dz, 0.0)
            dzb = _bf(dz * scale)
            for hh in range(HG):
                cs = slice(hh * SB_HD, (hh + 1) * SB_HD)
                dk_scr[pl.ds(off, KB), cs] += _dot(dzb[:, cs], q_ref[:, cs])
                dqt_scr[hh] += _dot(kt_scr[hh, jp], dzb[:, cs])
            return cg + jnp.sum(g, axis=0, keepdims=True)

        cg = lax.fori_loop(0, jp0, lambda jp, cr: pass2(jp, cr, False), zero)
        pass2(jp0, cg, True)
        for hh in range(HG):
            dq_ref[:, hh * SB_HD:(hh + 1) * SB_HD] = _bf(dqt_scr[hh].T)

        @pl.when(i == nb - 1)
        def _():
            dk_ref[...] = _bf(dk_scr[...])
            dv_ref[...] = _bf(dv_scr[...])

    return pl.pallas_call(
        body, name="sb_bwd",
        grid=(SB_HEADS // HG, nb),
        in_specs=[pl.BlockSpec((None, B, W), lambda h, i: (1, i, h)),
                  pl.BlockSpec((None, T, W), lambda h, i: (2, 0, h)),
                  pl.BlockSpec((None, T, W), lambda h, i: (3, 0, h)),
                  pl.BlockSpec((B, W), lambda h, i: (i, h))],
        out_specs=(pl.BlockSpec((B, W), lambda h, i: (i, h)),
                   pl.BlockSpec((T, W), lambda h, i: (0, h)),
                   pl.BlockSpec((T, W), lambda h, i: (0, h))),
        out_shape=(jax.ShapeDtypeStruct((T, 1024), BF16),
                   jax.ShapeDtypeStruct((T, 1024), BF16),
                   jax.ShapeDtypeStruct((T, 1024), BF16)),
        scratch_shapes=[pltpu.VMEM((T, W), F32), pltpu.VMEM((T, W), F32),
                        pltpu.VMEM((HG, nkb, SB_HD, KB), BF16),
                        pltpu.VMEM((nkb, KB, W), F32), pltpu.VMEM((nkb, KB, W), F32),
                        pltpu.VMEM((HG, SB_HD, B), F32)],
        compiler_params=_cparams(("arbitrary", "arbitrary")),
    )(projb, projb, projb, do_sb)


def _mid_call(o_gla, o_sb, projf, x, target, wpa, wpb, wo, gla_g, b_gate, final_g):
    T, D = x.shape
    tm = min(TBLK, T)

    def body(og_ref, ggate_ref, osb_ref, sgate_ref, ma_ref, mb_ref, x_ref, tgt_ref,
             wpa_ref, wpb_ref, wo_ref, glag_ref, bg_ref, fg_ref,
             dx2_ref, dogla_ref, dosb_ref, dggate_ref, dsgate_ref, dm_ref,
             mt_ref, ogt_ref, obt_ref, dx2b_ref, dya_ref, dyb_ref,
             dfg_ref, dbg_ref, dglag_ref, loss_ref):
        @pl.when(pl.program_id(0) == 0)
        def _():
            dfg_ref[...] = jnp.zeros_like(dfg_ref)
            dbg_ref[...] = jnp.zeros_like(dbg_ref)
            dglag_ref[...] = jnp.zeros_like(dglag_ref)
            loss_ref[...] = jnp.zeros_like(loss_ref)

        glag = glag_ref[...]
        ggate = ggate_ref[...]
        sg = _sigmoid(ggate)
        silu_g = ggate * sg
        ohat, rinv, nrm = [], [], []
        for hh in range(GLA_HEADS):
            oh = og_ref[:, hh * GLA_HV:(hh + 1) * GLA_HV]
            r = lax.rsqrt(jnp.mean(oh * oh, axis=-1, keepdims=True) + EPS)
            ohat.append(oh * r)
            rinv.append(r)
            nrm.append(ohat[-1] * glag)
        n_all = jnp.concatenate(nrm, axis=1)
        og = n_all * silu_g
        ogb = _bf(og)
        ya = _dot(ogb, wpa_ref[...])
        sgate = sgate_ref[...]
        ss = _sigmoid(sgate)
        silu_s = sgate * ss
        osb = osb_ref[...]
        ob = osb * silu_s
        obb = _bf(ob)
        yb = _dot(obb, wpb_ref[...])
        ga = _sigmoid(ma_ref[...] + bg_ref[:, :D])
        gb = _sigmoid(mb_ref[...] + bg_ref[:, D:])
        merged = ga * ya + gb * yb
        mgb = _bf(merged)
        x2 = x_ref[...] + _dot(mgb, wo_ref[...])
        r2 = lax.rsqrt(jnp.mean(x2 * x2, axis=-1, keepdims=True) + EPS)
        xh2 = x2 * r2
        fg = fg_ref[...]
        err = xh2 * fg - tgt_ref[...]
        loss_ref[...] += jnp.broadcast_to(
            0.5 * jnp.sum(jnp.mean(err * err, axis=-1, keepdims=True), axis=0, keepdims=True), (1, 128))
        dy = err * (1.0 / D)
        dfg_ref[...] += jnp.sum(dy * xh2, axis=0, keepdims=True)
        dxh = dy * fg
        dx2 = r2 * (dxh - xh2 * jnp.mean(dxh * xh2, axis=-1, keepdims=True))
        dx2_ref[...] = dx2
        dx2b = _bf(dx2)
        dx2b_ref[...] = dx2b
        dmerged = _dot_nt(dx2b, wo_ref[...])
        dya = dmerged * ga
        dyb = dmerged * gb
        dma = dmerged * ya * ga * (1.0 - ga)
        dmb = dmerged * yb * gb * (1.0 - gb)
        dm_ref[:, :D] = _bf(dma)
        dm_ref[:, D:] = _bf(dmb)
        dbg_ref[:, :D] += jnp.sum(dma, axis=0, keepdims=True)
        dbg_ref[:, D:] += jnp.sum(dmb, axis=0, keepdims=True)
        dyab = _bf(dya)
        dybb = _bf(dyb)
        dya_ref[...] = dyab
        dyb_ref[...] = dybb
        dog = _dot_nt(dyab, wpa_ref[...])
        dob = _dot_nt(dybb, wpb_ref[...])
        dosb_ref[...] = dob * silu_s
        dsgate_ref[...] = _bf(dob * osb * (ss * (1.0 + sgate * (1.0 - ss))))
        dn = dog * silu_g
        dggate_ref[...] = _bf(dog * n_all * (sg * (1.0 + ggate * (1.0 - sg))))
        dglag = jnp.zeros((1, GLA_HV), F32)
        for hh in range(GLA_HEADS):
            dnh = dn[:, hh * GLA_HV:(hh + 1) * GLA_HV]
            dglag = dglag + jnp.sum(dnh * ohat[hh], axis=0, keepdims=True)
            dohat = dnh * glag
            dogla_ref[:, hh * GLA_HV:(hh + 1) * GLA_HV] = rinv[hh] * (
                dohat - ohat[hh] * jnp.mean(dohat * ohat[hh], axis=-1, keepdims=True))
        dglag_ref[...] += dglag
        mt_ref[...] = _bf(merged.T)
        ogt_ref[...] = _bf(og.T)
        obt_ref[...] = _bf(ob.T)

    row = lambda i: (i, 0)
    const = lambda i: (0, 0)
    tile = pl.BlockSpec((tm, D), row)
    tile_t = pl.BlockSpec((None, D, tm), lambda i: (i, 0, 0))
    wspec = pl.BlockSpec((D, D), const)
    return pl.pallas_call(
        body, name="mid",
        grid=(T // tm,),
        in_specs=[tile,
                  pl.BlockSpec((None, tm, D), lambda i: (1, i, 0)),
                  tile,
                  pl.BlockSpec((None, tm, D), lambda i: (2, i, 0)),
                  pl.BlockSpec((None, tm, D), lambda i: (3, i, 0)),
                  pl.BlockSpec((None, tm, D), lambda i: (4, i, 0)),
                  tile, tile, wspec, wspec, wspec,
                  pl.BlockSpec((1, GLA_HV), const),
                  pl.BlockSpec((1, 2 * D), const),
                  pl.BlockSpec((1, D), const)],
        out_specs=(tile, tile, tile, tile, tile,
                   pl.BlockSpec((tm, 2 * D), row),
                   tile_t, tile_t, tile_t, tile, tile, tile,
                   pl.BlockSpec((1, D), const),
                   pl.BlockSpec((1, 2 * D), const),
                   pl.BlockSpec((1, GLA_HV), const),
                   pl.BlockSpec((1, 128), const)),
        out_shape=(jax.ShapeDtypeStruct((T, D), F32),
                   jax.ShapeDtypeStruct((T, D), F32),
                   jax.ShapeDtypeStruct((T, D), F32),
                   jax.ShapeDtypeStruct((T, D), BF16),
                   jax.ShapeDtypeStruct((T, D), BF16),
                   jax.ShapeDtypeStruct((T, 2 * D), BF16),
                   jax.ShapeDtypeStruct((T // tm, D, tm), BF16),
                   jax.ShapeDtypeStruct((T // tm, D, tm), BF16),
                   jax.ShapeDtypeStruct((T // tm, D, tm), BF16),
                   jax.ShapeDtypeStruct((T, D), BF16),
                   jax.ShapeDtypeStruct((T, D), BF16),
                   jax.ShapeDtypeStruct((T, D), BF16),
                   jax.ShapeDtypeStruct((1, D), F32),
                   jax.ShapeDtypeStruct((1, 2 * D), F32),
                   jax.ShapeDtypeStruct((1, GLA_HV), F32),
                   jax.ShapeDtypeStruct((1, 128), F32)),
        compiler_params=_cparams(("arbitrary",)),
    )(o_gla, projf, o_sb, projf, projf, projf, x, target, wpa, wpb, wo, gla_g, b_gate, final_g)


def _dh_call(pieces, dmlog, drank, wt, wr, x, dx2, norm_g):
    T, D = x.shape
    tm = min(256, T)
    npc = len(pieces)
    n_main = N_GROUPS * 1024

    def body(*refs):
        pcs = refs[:npc]
        (dm_ref, dr_ref, w_hbm, wr_ref, x_ref, dx2_ref, g_ref,
         gx_ref, dg_ref, dwr_ref, w_scr, sems) = refs[npc:]

        @pl.when(pl.program_id(0) == 0)
        def _():
            lo = pltpu.make_async_copy(w_hbm.at[pl.ds(0, RANK_COL)], w_scr.at[pl.ds(0, RANK_COL)], sems.at[0])
            hi = pltpu.make_async_copy(w_hbm.at[pl.ds(RANK_COL + GLA_RANK, n_main - RANK_COL)],
                                       w_scr.at[pl.ds(RANK_COL, n_main - RANK_COL)], sems.at[1])
            lo.start()
            hi.start()
            dg_ref[...] = jnp.zeros_like(dg_ref)
            dwr_ref[...] = jnp.zeros_like(dwr_ref)
            lo.wait()
            hi.wait()

        def w_group(g):
            return w_scr[g * 1024:(g + 1) * 1024, :]

        dr = dr_ref[...]
        dh = _dot(dr, wr_ref[...])
        for g in range(npc):
            dh = dh + _dot(pcs[g][...], w_group(g))
        dh = dh + _dot(dm_ref[:, :D], w_group(npc))
        dh = dh + _dot(dm_ref[:, D:], w_group(npc + 1))
        xv = x_ref[...]
        r = lax.rsqrt(jnp.mean(xv * xv, axis=-1, keepdims=True) + EPS)
        xhat = xv * r
        g = g_ref[...]
        dg_ref[...] += jnp.sum(dh * xhat, axis=0, keepdims=True)
        dxhat = dh * g
        gx_ref[...] = r * (dxhat - xhat * jnp.mean(dxhat * xhat, axis=-1, keepdims=True)) + dx2_ref[...]
        dwr_ref[...] += _dot_tn(dr, _bf(xhat * g))

    row = lambda i: (i, 0)
    const = lambda i: (0, 0)
    tile = pl.BlockSpec((tm, D), row)
    return pl.pallas_call(
        body, name="dh",
        grid=(T // tm,),
        in_specs=[tile] * npc + [
            pl.BlockSpec((tm, 2 * D), row),
            pl.BlockSpec((tm, 128), row),
            pl.BlockSpec(memory_space=pl.ANY),
            pl.BlockSpec((128, D), const),
            tile, tile,
            pl.BlockSpec((1, D), const)],
        out_specs=(tile, pl.BlockSpec((1, D), const), pl.BlockSpec((128, D), const)),
        out_shape=(jax.ShapeDtypeStruct((T, D), F32),
                   jax.ShapeDtypeStruct((1, D), F32),
                   jax.ShapeDtypeStruct((128, D), F32)),
        scratch_shapes=[pltpu.VMEM((n_main, D), BF16), pltpu.SemaphoreType.DMA((2,))],
        compiler_params=_cparams(("arbitrary",)),
    )(*pieces, dmlog, drank, wt, wr, x, dx2, norm_g)


def _wgrad_call(lhs_list, lhs_of_group, rhs_list, rhs_of_group, n_transposed, name):
    n_groups = len(rhs_of_group)
    n_tb, D, tb = lhs_list[0].shape
    T = n_tb * tb
    per = min(2, n_tb)
    tk = per * tb
    nk = T // tk
    nl = len(lhs_list)

    def body(*refs):
        lhs = refs[:nl]
        rhs = refs[nl:nl + n_groups]
        out_ref, acc = refs[nl + n_groups:]
        g = pl.program_id(0)
        i = pl.program_id(1)

        @pl.when(i == 0)
        def _():
            acc[...] = jnp.zeros_like(acc)

        for p in range(n_groups):
            @pl.when(g == p)
            def _(p=p):
                lref = lhs[lhs_of_group[p]]
                part = _dot(lref[0], rhs[p][0:tb, :])
                for b in range(1, per):
                    part = part + _dot(lref[b], rhs[p][b * tb:(b + 1) * tb, :])
                acc[...] += part

        @pl.when((i == nk - 1) & (g < n_transposed))
        def _():
            out_ref[...] = _bf(acc[...].T)

        @pl.when((i == nk - 1) & (g >= n_transposed))
        def _():
            out_ref[...] = _bf(acc[...])

    def lhs_spec(a):
        groups = [g for g in range(n_groups) if lhs_of_group[g] == a]
        lo, hi = min(groups), max(groups)
        assert groups == list(range(lo, hi + 1))
        return pl.BlockSpec((per, D, tb), lambda g, i: (jnp.where((g >= lo) & (g <= hi), i, 0), 0, 0))

    def rhs_spec(p):
        cb = rhs_of_group[p][1]
        return pl.BlockSpec((tk, 1024), lambda g, i: (jnp.where(g == p, i, 0), cb))

    return pl.pallas_call(
        body, name=name,
        grid=(n_groups, nk),
        in_specs=[lhs_spec(a) for a in range(nl)] + [rhs_spec(p) for p in range(n_groups)],
        out_specs=pl.BlockSpec((None, D, 1024), lambda g, i: (g, 0, 0)),
        out_shape=jax.ShapeDtypeStruct((n_groups, D, 1024), BF16),
        scratch_shapes=[pltpu.VMEM((D, 1024), F32)],
        compiler_params=_cparams(("arbitrary", "arbitrary")),
    )(*lhs_list, *[rhs_list[rhs_of_group[p][0]] for p in range(n_groups)])


def _adamw_call(parts, w, m, v, name):
    R, C = w.shape
    n_parts = parts.shape[0]
    (tr, tc), grid, idx = _tiling_2d(R, C)

    def body(p_ref, w_ref, m_ref, v_ref, g_ref, d_ref, nm_ref, nv_ref):
        g = p_ref[n_parts - 1].astype(F32)
        for k in range(n_parts - 1):
            g = g + p_ref[k].astype(F32)
        mm = ADAM_B1 * m_ref[...] + (1.0 - ADAM_B1) * g
        vv = ADAM_B2 * v_ref[...] + (1.0 - ADAM_B2) * (g * g)
        m_hat = mm / (1.0 - ADAM_B1 ** ADAM_STEP)
        v_hat = vv / (1.0 - ADAM_B2 ** ADAM_STEP)
        d_ref[...] = -ADAM_LR * (m_hat / (jnp.sqrt(v_hat) + ADAM_EPS) + ADAM_WD * w_ref[...])
        g_ref[...] = g
        nm_ref[...] = mm
        nv_ref[...] = vv

    blk = pl.BlockSpec((tr, tc), idx)
    sds = jax.ShapeDtypeStruct((R, C), F32)
    return pl.pallas_call(
        body, name=name,
        grid=grid,
        in_specs=[pl.BlockSpec((n_parts, tr, tc), lambda i: (0,) + idx(i)), blk, blk, blk],
        out_specs=(blk, blk, blk, blk),
        out_shape=(sds, sds, sds, sds),
        compiler_params=_cparams(("arbitrary",)),
    )(parts, w, m, v)


def _local_step(x, target, wt, wr, wdec, bdec, wp_shard, norm_g, gla_g, b_gate, final_g):
    D = x.shape[1]
    projf, projb, rank, ht, wp_all = _proj_call(x, norm_g, wt, wr, wp_shard)
    wp_full = wp_all.transpose(1, 0, 2, 3).reshape(3, D, D)
    o_gla, st_all, la = _gla_fwd_call(projf, projb, rank, wdec, bdec)
    o_sb = _sb_fwd_call(projb)
    (dx2, do_gla, do_sb, dggate, dsgate, dmlog, mt, ogt, obt, dx2b, dya, dyb,
     dfinal_g, db_gate, dgla_g, loss) = _mid_call(o_gla, o_sb, projf, x, target, wp_full[0], wp_full[1],
                                                 wp_full[2], gla_g, b_gate, final_g)
    dw_p = _wgrad_call([ogt, obt, mt], [0, 1, 2], [dya, dyb, dx2b], [(0, 0), (1, 0), (2, 0)], 0, "wgrad_p")
    g_p = dw_p.reshape(3, N_DEV, D // N_DEV, D).transpose(1, 0, 2, 3).reshape(N_DEV, 3 * (D // N_DEV), D)
    dqk, dgv, drank, dwdec, dbdec, r_p = _gla_bwd_call(projf, projb, la, do_gla, st_all, rank, wdec, g_p)
    dsq, dsk, dsv = _sb_bwd_call(projb, do_sb)
    pieces = [dqk, dgv, dggate, dsq, dsk, dsv, dsgate]
    grad_x, dnorm_g, dwr = _dh_call(pieces, dmlog, drank, wt, wr, x, dx2, norm_g)
    rhs_of_group = [(g, 0) for g in range(7)] + [(7, 0), (7, 1)]
    dw_in = _wgrad_call([ht], [0] * N_GROUPS, pieces + [dmlog], rhs_of_group, N_GROUPS, "wgrad_in")
    return grad_x, dw_in, dwr, r_p, dwdec, dbdec, dnorm_g, dgla_g, db_gate, dfinal_g, loss


_SM_NORM = 0
_SM_BDEC = _SM_NORM + D_MODEL
_SM_GLAG = _SM_BDEC + GLA_DK
_SM_BGATE = _SM_GLAG + GLA_HV
_SM_FINAL = _SM_BGATE + 2 * D_MODEL
_SM_REPL = _SM_FINAL + D_MODEL
_SM_LOSS = _SM_REPL
_SM_WDEC = _SM_LOSS + 128
_SM_LEN = _SM_WDEC + GLA_RANK * GLA_DK


def kernel(x, norm_g, w_in, w_dec_up, b_dec, gla_norm_g, w_pa, w_pb, b_gate, w_o, final_g, loss_target, m_norm_g, m_w_in, m_w_dec_up, m_b_dec, m_gla_norm_g, m_w_pa, m_w_pb, m_b_gate, m_w_o, m_final_g, v_norm_g, v_w_in, v_w_dec_up, v_b_dec, v_gla_norm_g, v_w_pa, v_w_pb, v_b_gate, v_w_o, v_final_g):
    D = D_MODEL
    me = 4 * lax.axis_index("x") + 2 * lax.axis_index("y") + lax.axis_index("c")

    wp_shard = jnp.stack([w_pa, w_pb, w_o]).astype(BF16)
    win_all, wdec_all = _all_gather([w_in.T.astype(BF16), w_dec_up], "gather_w")
    wt = win_all.reshape(IN_COLS, D)
    wr = jnp.pad(wt[RANK_COL:RANK_COL + GLA_RANK], ((0, 128 - GLA_RANK), (0, 0)))
    wdec_full = wdec_all.transpose(1, 0, 2).reshape(GLA_RANK, GLA_DK)
    wdec = jnp.pad(wdec_full, ((0, 128 - GLA_RANK), (0, 0)))

    (grad_x, dw_in, dwr, r_p, dwdec, dbdec, dnorm_g, dgla_g, db_gate, dfinal_g, loss) = _local_step(
        x[0], loss_target[0], wt, wr, wdec, b_dec.reshape(1, -1), wp_shard,
        norm_g.reshape(1, -1), gla_norm_g.reshape(1, -1), b_gate.reshape(1, -1), final_g.reshape(1, -1))

    dmain = dw_in.reshape(N_GROUPS * 1024, D)
    drank = dwr[:GLA_RANK].astype(BF16)

    def part_for(p):
        lo, hi = p * SHARD_COLS, (p + 1) * SHARD_COLS
        pieces = []
        if lo < RANK_COL:
            pieces.append(dmain[lo:min(hi, RANK_COL)])
        if lo < RANK_COL + GLA_RANK and hi > RANK_COL:
            pieces.append(drank[max(lo, RANK_COL) - RANK_COL:min(hi, RANK_COL + GLA_RANK) - RANK_COL])
        if hi > RANK_COL + GLA_RANK:
            pieces.append(dmain[max(lo, RANK_COL + GLA_RANK) - GLA_RANK:hi - GLA_RANK])
        return pieces[0] if len(pieces) == 1 else jnp.concatenate(pieces, axis=0)

    g_in = jnp.stack([part_for(p) for p in range(N_DEV)])
    small = jnp.concatenate([
        dnorm_g.reshape(-1), dbdec.reshape(-1), dgla_g.reshape(-1), db_gate.reshape(-1), dfinal_g.reshape(-1),
        loss.reshape(-1), dwdec[:GLA_RANK].reshape(-1)])
    c_idx = lax.axis_index("c").astype(jnp.int32).reshape(1)
    (p_in,) = _pair_exchange([g_in], "pair_g")
    s_in = _pair_add_call(g_in, p_in, c_idx, "pair_add_in")
    (r_in,) = _chip_exchange([s_in], "scatter_g")
    (r_small,) = _all_gather([small.reshape(1, _SM_LEN)], "gather_small")

    gw_in, d_in, nm_in, nv_in = (a.T for a in _adamw_call(r_in, w_in.T, m_w_in.T, v_w_in.T, "adamw_in"))
    wp_f32 = jnp.concatenate([w_pa, w_pb, w_o], axis=0)
    mp = jnp.concatenate([m_w_pa, m_w_pb, m_w_o], axis=0)
    vp = jnp.concatenate([v_w_pa, v_w_pb, v_w_o], axis=0)
    gp, dp, nmp, nvp = _adamw_call(r_p, wp_f32, mp, vp, "adamw_p")
    rows = D // N_DEV

    def split3(a):
        return a[:rows], a[rows:2 * rows], a[2 * rows:]

    g_pa, g_pb, g_o = split3(gp)
    d_pa, d_pb, d_o = split3(dp)
    nm_pa, nm_pb, nm_o = split3(nmp)
    nv_pa, nv_pb, nv_o = split3(nvp)

    w_rep = jnp.concatenate([norm_g, b_dec, gla_norm_g, b_gate, final_g]).reshape(1, _SM_REPL)
    m_rep = jnp.concatenate([m_norm_g, m_b_dec, m_gla_norm_g, m_b_gate, m_final_g]).reshape(1, _SM_REPL)
    v_rep = jnp.concatenate([v_norm_g, v_b_dec, v_gla_norm_g, v_b_gate, v_final_g]).reshape(1, _SM_REPL)
    g_rep, d_rep, nm_rep, nv_rep = _adamw_call(r_small[:, :, :_SM_REPL], w_rep, m_rep, v_rep, "adamw_rep")

    def split_rep(a):
        a = a.reshape(-1)
        return (a[_SM_NORM:_SM_BDEC], a[_SM_BDEC:_SM_GLAG], a[_SM_GLAG:_SM_BGATE],
                a[_SM_BGATE:_SM_FINAL], a[_SM_FINAL:_SM_REPL])

    g_norm, g_bdec, g_glag, g_bgate, g_final = split_rep(g_rep)
    d_norm, d_bdec, d_glag, d_bgate, d_final = split_rep(d_rep)
    nm_norm, nm_bdec, nm_glag, nm_bgate, nm_final = split_rep(nm_rep)
    nv_norm, nv_bdec, nv_glag, nv_bgate, nv_final = split_rep(nv_rep)

    wdec_parts = r_small[:, 0, _SM_WDEC:].reshape(N_DEV, GLA_RANK, GLA_DK)
    cols = GLA_DK // N_DEV
    wdec_mine = lax.dynamic_slice_in_dim(wdec_parts, me * cols, cols, axis=2)
    g_wdec, d_wdec, nm_wdec, nv_wdec = _adamw_call(wdec_mine, w_dec_up, m_w_dec_up, v_w_dec_up, "adamw_dec")

    loss_total = jnp.sum(r_small[:, 0, _SM_LOSS])

    return (loss_total, grad_x[None],
            g_norm, gw_in, g_wdec, g_bdec, g_glag, g_pa, g_pb, g_bgate, g_o, g_final,
            d_norm, d_in, d_wdec, d_bdec, d_glag, d_pa, d_pb, d_bgate, d_o, d_final,
            nm_norm, nm_in, nm_wdec, nm_bdec, nm_glag, nm_pa, nm_pb, nm_bgate, nm_o, nm_final,
            nv_norm, nv_in, nv_wdec, nv_bdec, nv_glag, nv_pa, nv_pb, nv_bgate, nv_o, nv_final)
```

```python
import functools
import math

import jax
import jax.numpy as jnp
from jax import lax
from jax.experimental import pallas as pl
from jax.experimental.pallas import tpu as pltpu

F32 = jnp.float32
BF16 = jnp.bfloat16

N_DEV = 8
D_MODEL = 1024
GLA_HEADS = 4
GLA_HK = 128
GLA_HV = 256
GLA_DK = 512
GLA_RANK = 16
GLA_TAU = 16.0
GLA_CHUNK = 64
SB_HEADS = 8
SB_HD = 128
SB_BLOCK = 128
EPS = 1e-6
N_GROUPS = 9
RANK_COL = 3072
IN_COLS = 9232
SHARD_COLS = IN_COLS // N_DEV

ADAM_LR = 0.001
ADAM_B1 = 0.9
ADAM_B2 = 0.999
ADAM_EPS = 1e-08
ADAM_WD = 0.01
ADAM_STEP = 10

VMEM_LIMIT = 56 * 1024 * 1024
TBLK = 256


def _cparams(sem=None):
    return pltpu.CompilerParams(dimension_semantics=sem, vmem_limit_bytes=VMEM_LIMIT)


def _tiling_2d(rows, cols):
    if rows * cols <= 128 * 1024:
        return (rows, cols), (1,), lambda i: (0, 0)
    if rows % 128 == 0:
        return (128, cols), (rows // 128,), lambda i: (i, 0)
    tc = 256 if cols % 256 == 0 else cols
    return (rows, tc), (cols // tc,), lambda i: (0, i)


def _dot(a, b):
    return jnp.dot(a, b, preferred_element_type=F32)


def _dot_nt(a, b):
    return lax.dot_general(a, b, (((1,), (1,)), ((), ())), preferred_element_type=F32)


def _dot_tn(a, b):
    return lax.dot_general(a, b, (((0,), (0,)), ((), ())), preferred_element_type=F32)


def _bf(x):
    return x.astype(BF16)


def _split3(x):
    hi = x.astype(BF16)
    r = x - hi.astype(F32)
    mid = r.astype(BF16)
    lo = (r - mid.astype(F32)).astype(BF16)
    return hi, mid, lo


def _tri_left(tri, x):
    hi, mid, lo = _split3(x)
    return _dot(tri, hi) + _dot(tri, mid) + _dot(tri, lo)


def _split2(x):
    hi = lax.bitcast_convert_type(lax.bitcast_convert_type(x, jnp.uint32) & jnp.uint32(0xFFFF0000), F32)
    return hi.astype(BF16), (x - hi).astype(BF16)


def _tri2_left(tri, x):
    hi, lo = _split2(x)
    return _dot(tri, hi) + _dot(tri, lo)


def _tri2_right(x, tri):
    hi, lo = _split2(x)
    return _dot(hi, tri) + _dot(lo, tri)


def _iota2(n, m, dim):
    return lax.broadcasted_iota(jnp.int32, (n, m), dim)


def _sigmoid(x):
    return 1.0 / (1.0 + jnp.exp(-x))


def _softplus_neg_abs(z):
    return jnp.log(1.0 + jnp.exp(-jnp.abs(z)))


_ANY = pl.BlockSpec(memory_space=pl.ANY)


def _mesh_pos():
    return lax.axis_index("x"), lax.axis_index("y"), lax.axis_index("c")


def _other_chips(x, y):
    return [(1 - x, y), (x, 1 - y), (1 - x, 1 - y)]


def _rcopy(src, dst, send_sem, recv_sem, to):
    return pltpu.make_async_remote_copy(src_ref=src, dst_ref=dst, send_sem=send_sem, recv_sem=recv_sem,
                                        device_id=to, device_id_type=pl.DeviceIdType.MESH)


def _push_copies(src_ref, dst_ref, send_sems, recv_sems, loc_sem, scatter):
    x, y, c = _mesh_pos()
    me = 4 * x + 2 * y + c
    own = pltpu.make_async_copy(src_ref.at[me] if scatter else src_ref, dst_ref.at[me], loc_sem)
    pairs = []
    for k in range(1, N_DEV):
        px = 1 - x if k & 4 else x
        py = 1 - y if k & 2 else y
        pc = 1 - c if k & 1 else c
        pid = 4 * px + 2 * py + pc
        src = src_ref.at[pid] if scatter else src_ref
        send = _rcopy(src, dst_ref.at[me], send_sems.at[k - 1], recv_sems.at[k - 1], (px, py, pc))
        recv = _rcopy(src, dst_ref.at[pid], send_sems.at[k - 1], recv_sems.at[k - 1], (px, py, pc))
        pairs.append((send, recv))
    return own, pairs


def _push_start(own, pairs):
    own.start()
    for send, _ in pairs:
        send.start()


def _push_wait(own, pairs):
    for _, recv in pairs:
        recv.wait_recv()
    for send, _ in pairs:
        send.wait_send()
    own.wait()


_PUSH_SEMS = [pltpu.SemaphoreType.DMA((N_DEV - 1,)), pltpu.SemaphoreType.DMA((N_DEV - 1,)),
              pltpu.SemaphoreType.DMA]


def _all_gather(arrs, name):
    n = len(arrs)

    def body(*refs):
        ins = refs[:n]
        outs = refs[n:2 * n]
        send_sems, recv_sems, loc_sems = refs[2 * n:]
        x, y, c = _mesh_pos()
        sib = (x, y, 1 - c)
        chips = _other_chips(x, y)

        def place(a, px, py, pc):
            return outs[a].at[4 * px + 2 * py + pc]

        def copy(a, k, block, to, src=None):
            dst = place(a, *block)
            return _rcopy(dst if src is None else src, dst, send_sems.at[a, k], recv_sems.at[a, k], to)

        mine = [pltpu.make_async_copy(ins[a], place(a, x, y, c), loc_sems.at[a]) for a in range(n)]
        for cp in mine:
            cp.start()
        first = [copy(a, 0, (x, y, c), sib, src=ins[a]) for a in range(n)]
        for j, chip in enumerate(chips):
            first += [copy(a, 1 + j, (x, y, c), (*chip, c), src=ins[a]) for a in range(n)]
        for cp in first:
            cp.start()
        passed = []
        for j, chip in enumerate(chips):
            for a in range(n):
                copy(a, 1 + j, (*chip, c), (x, y, c)).wait_recv()
                fwd = copy(a, 4 + j, (*chip, c), sib)
                fwd.start()
                passed.append(fwd)
        for a in range(n):
            copy(a, 0, sib, (x, y, c)).wait_recv()
        for j, chip in enumerate(chips):
            for a in range(n):
                copy(a, 4 + j, (*chip, 1 - c), (x, y, c)).wait_recv()
        for cp in first + passed:
            cp.wait_send()
        for cp in mine:
            cp.wait()

    return pl.pallas_call(
        body, name=name,
        out_shape=tuple(jax.ShapeDtypeStruct((N_DEV,) + a.shape, a.dtype) for a in arrs),
        in_specs=[_ANY] * n,
        out_specs=tuple([_ANY] * n),
        scratch_shapes=[pltpu.SemaphoreType.DMA((n, 7)), pltpu.SemaphoreType.DMA((n, 7)),
                        pltpu.SemaphoreType.DMA((n,))],
    )(*arrs)


def _pair_exchange(arrs, name):
    n = len(arrs)

    def body(*refs):
        ins = refs[:n]
        outs = refs[n:2 * n]
        send_sems, recv_sems = refs[2 * n:]
        x, y, c = _mesh_pos()
        copies = []
        for a in range(n):
            for q in range(4):
                cp = _rcopy(ins[a].at[2 * q + (1 - c)], outs[a].at[q], send_sems.at[a, q], recv_sems.at[a, q],
                            (x, y, 1 - c))
                cp.start()
                copies.append(cp)
        for cp in copies:
            cp.wait_recv()
        for cp in copies:
            cp.wait_send()

    return pl.pallas_call(
        body, name=name,
        out_shape=tuple(jax.ShapeDtypeStruct((4,) + a.shape[1:], a.dtype) for a in arrs),
        in_specs=[_ANY] * n,
        out_specs=tuple([_ANY] * n),
        scratch_shapes=[pltpu.SemaphoreType.DMA((n, 4)), pltpu.SemaphoreType.DMA((n, 4))],
    )(*arrs)


def _pair_add_call(parts, recv, c_idx, name):
    _, R, C = parts.shape
    (tr, tc), (steps,), idx = _tiling_2d(R, C)

    def body(c_ref, p_ref, r_ref, o_ref):
        o_ref[...] = (p_ref[...].astype(F32) + r_ref[...].astype(F32)).astype(o_ref.dtype)

    return pl.pallas_call(
        body, name=name,
        grid_spec=pltpu.PrefetchScalarGridSpec(
            num_scalar_prefetch=1,
            grid=(4, steps),
            in_specs=[pl.BlockSpec((None, tr, tc), lambda q, i, c_ref: (2 * q + c_ref[0],) + idx(i)),
                      pl.BlockSpec((None, tr, tc), lambda q, i, c_ref: (q,) + idx(i))],
            out_specs=pl.BlockSpec((None, tr, tc), lambda q, i, c_ref: (q,) + idx(i))),
        out_shape=jax.ShapeDtypeStruct((4, R, C), parts.dtype),
        compiler_params=_cparams(("arbitrary", "arbitrary")),
    )(c_idx, parts, recv)


def _chip_exchange(arrs, name):
    n = len(arrs)

    def body(*refs):
        ins = refs[:n]
        outs = refs[n:2 * n]
        send_sems, recv_sems, loc_sems = refs[2 * n:]
        x, y, c = _mesh_pos()
        mine = [pltpu.make_async_copy(ins[a].at[2 * x + y], outs[a].at[3], loc_sems.at[a]) for a in range(n)]
        for cp in mine:
            cp.start()
        copies = []
        for j, (px, py) in enumerate(_other_chips(x, y)):
            for a in range(n):
                cp = _rcopy(ins[a].at[2 * px + py], outs[a].at[j], send_sems.at[a, j], recv_sems.at[a, j],
                            (px, py, c))
                cp.start()
                copies.append(cp)
        for cp in copies:
            cp.wait_recv()
        for cp in copies:
            cp.wait_send()
        for cp in mine:
            cp.wait()

    return pl.pallas_call(
        body, name=name,
        out_shape=tuple(jax.ShapeDtypeStruct(a.shape, a.dtype) for a in arrs),
        in_specs=[_ANY] * n,
        out_specs=tuple([_ANY] * n),
        scratch_shapes=[pltpu.SemaphoreType.DMA((n, 3)), pltpu.SemaphoreType.DMA((n, 3)),
                        pltpu.SemaphoreType.DMA((n,))],
    )(*arrs)


def _group_row(g):
    return GLA_RANK * (g * (1024 // GLA_RANK) + (g >= RANK_COL // 1024))


def _proj_call(x, norm_g, wt, wr):
    T, D = x.shape
    tm = min(1024, T)
    assert tm % TBLK == 0
    n_i = T // tm

    def f_slot(j):
        return ((j >= 2).astype(jnp.int32) + (j >= 6).astype(jnp.int32)
                + (j >= 7).astype(jnp.int32) + (j >= 8).astype(jnp.int32))

    def b_slot(j):
        return (j >= 3).astype(jnp.int32) + (j >= 4).astype(jnp.int32) + (j >= 5).astype(jnp.int32)

    def body(x_ref, g_ref, w_ref, wr_ref, pf_ref, pb_ref, rank_ref, ht_ref, h_scr):
        j = pl.program_id(1)

        @pl.when(j == 0)
        def _():
            xv = x_ref[...]
            r = lax.rsqrt(jnp.mean(xv * xv, axis=-1, keepdims=True) + EPS)
            h = (xv * r) * g_ref[...]
            hb = _bf(h)
            h_scr[...] = hb
            for b in range(tm // TBLK):
                ht_ref[b] = _bf(h[b * TBLK:(b + 1) * TBLK].T)
            rank_ref[...] = _dot_nt(hb, wr_ref[...])

        is_b = (j == 1) | ((j >= 3) & (j <= 5))

        @pl.when(is_b)
        def _():
            pb_ref[...] = _bf(_dot_nt(h_scr[...], w_ref[...]))

        @pl.when(jnp.logical_not(is_b))
        def _():
            pf_ref[...] = _dot_nt(h_scr[...], w_ref[...])

    return pl.pallas_call(
        body, name="proj",
        grid=(n_i, N_GROUPS),
        in_specs=[pl.BlockSpec((tm, D), lambda i, j: (i, 0)),
                  pl.BlockSpec((1, D), lambda i, j: (0, 0)),
                  pl.BlockSpec((pl.Element(1024), pl.Element(D)), lambda i, j: (_group_row(j), 0)),
                  pl.BlockSpec((128, D), lambda i, j: (0, 0))],
        out_specs=(pl.BlockSpec((None, tm, 1024), lambda i, j: (f_slot(j), i, 0)),
                   pl.BlockSpec((None, tm, 1024), lambda i, j: (b_slot(j), i, 0)),
                   pl.BlockSpec((tm, 128), lambda i, j: (i, 0)),
                   pl.BlockSpec((tm // TBLK, D, TBLK), lambda i, j: (i, 0, 0))),
        out_shape=(jax.ShapeDtypeStruct((5, T, 1024), F32),
                   jax.ShapeDtypeStruct((4, T, 1024), BF16),
                   jax.ShapeDtypeStruct((T, 128), F32),
                   jax.ShapeDtypeStruct((T // TBLK, D, TBLK), BF16)),
        scratch_shapes=[pltpu.VMEM((tm, D), BF16)],
        compiler_params=_cparams(("arbitrary", "arbitrary")),
    )(x, norm_g, wt, wr)


def _gla_chunk_terms(la_h, q, k):
    C = GLA_CHUNK
    low = _bf((_iota2(C, C, 0) >= _iota2(C, C, 1)).astype(F32))
    b = _tri_left(low, la_h)
    bl = b[C - 1:C, :]
    eb = jnp.exp(b)
    enb = jnp.exp(-b)
    ebl_b = jnp.exp(bl - b)
    scale = GLA_HK ** -0.5
    qe = q * eb * scale
    ke = k * enb
    kd = k * ebl_b
    return b, bl, eb, enb, ebl_b, qe, ke, kd


def _gla_fwd_call(projf, projb, rank, wdec, bdec):
    T = projf.shape[1]
    C = GLA_CHUNK
    n_chunks = T // C

    def body(qk_ref, v_ref, rank_ref, wd_ref, bd_ref, o_ref, st_ref, la_ref, st_scr):
        @pl.when(pl.program_id(0) == 0)
        def _():
            st_scr[...] = jnp.zeros_like(st_scr)

        dec = _dot(_bf(rank_ref[...]), _bf(wd_ref[...])) + bd_ref[...]
        la = (jnp.minimum(dec, 0.0) - _softplus_neg_abs(dec)) / GLA_TAU
        la_ref[...] = la
        mask = _iota2(C, C, 0) >= _iota2(C, C, 1)
        _, bl, _, _, _, qe, ke, kd = _gla_chunk_terms(la, qk_ref[:, :GLA_DK], qk_ref[:, GLA_DK:])
        qeb, keb, kdb = _bf(qe), _bf(ke), _bf(kd)
        ebl = jnp.exp(bl)
        heads = range(GLA_HEADS)
        ks = [slice(hh * GLA_HK, (hh + 1) * GLA_HK) for hh in heads]
        vs = [slice(hh * GLA_HV, (hh + 1) * GLA_HV) for hh in heads]
        st = [st_scr[hh] for hh in heads]
        p = [_bf(jnp.where(mask, _dot_nt(qeb[:, ks[hh]], keb[:, ks[hh]]), 0.0)) for hh in heads]
        inter = [_dot_nt(qeb[:, ks[hh]], _bf(st[hh])) for hh in heads]
        upd = [_dot_tn(v_ref[:, vs[hh]], kdb[:, ks[hh]]) for hh in heads]
        intra = [_dot(p[hh], v_ref[:, vs[hh]]) for hh in heads]
        for hh in heads:
            st_ref[hh] = st[hh]
            o_ref[:, vs[hh]] = intra[hh] + inter[hh]
            st_scr[hh] = st[hh] * ebl[:, ks[hh]] + upd[hh]

    return pl.pallas_call(
        body, name="gla_fwd",
        grid=(n_chunks,),
        in_specs=[pl.BlockSpec((None, C, 1024), lambda n: (0, n, 0)),
                  pl.BlockSpec((None, C, 1024), lambda n: (0, n, 0)),
                  pl.BlockSpec((C, 128), lambda n: (n, 0)),
                  pl.BlockSpec((128, GLA_DK), lambda n: (0, 0)),
                  pl.BlockSpec((1, GLA_DK), lambda n: (0, 0))],
        out_specs=(pl.BlockSpec((C, 1024), lambda n: (n, 0)),
                   pl.BlockSpec((None, GLA_HEADS, GLA_HV, GLA_HK), lambda n: (n, 0, 0, 0)),
                   pl.BlockSpec((C, GLA_DK), lambda n: (n, 0))),
        out_shape=(jax.ShapeDtypeStruct((T, 1024), F32),
                   jax.ShapeDtypeStruct((n_chunks, GLA_HEADS, GLA_HV, GLA_HK), F32),
                   jax.ShapeDtypeStruct((T, GLA_DK), F32)),
        scratch_shapes=[pltpu.VMEM((GLA_HEADS, GLA_HV, GLA_HK), F32)],
        compiler_params=_cparams(("arbitrary",)),
    )(projf, projb, rank, wdec, bdec)


def _gla_bwd_call(projf, projb, la, do_gla, st_all, rank, wdec):
    T = projf.shape[1]
    C = GLA_CHUNK
    n_chunks = T // C
    last = n_chunks - 1

    def body(qk_ref, v_ref, la_ref, do_ref, st_ref, rank_ref, wd_ref,
             dqk_ref, dv_ref, drank_ref, dwd_ref, dbd_ref, dst_scr):
        @pl.when(pl.program_id(0) == 0)
        def _():
            dst_scr[...] = jnp.zeros_like(dst_scr)
            dwd_ref[...] = jnp.zeros_like(dwd_ref)
            dbd_ref[...] = jnp.zeros_like(dbd_ref)

        mask = _iota2(C, C, 0) >= _iota2(C, C, 1)
        upp = _bf((_iota2(C, C, 0) <= _iota2(C, C, 1)).astype(F32))
        scale = GLA_HK ** -0.5
        la = la_ref[...]
        _, bl, eb, enb, ebl_b, qe, ke, kd = _gla_chunk_terms(la, qk_ref[:, :GLA_DK], qk_ref[:, GLA_DK:])
        qeb, keb, kdb = _bf(qe), _bf(ke), _bf(kd)
        ebl = jnp.exp(bl)
        heads = range(GLA_HEADS)
        ks = [slice(hh * GLA_HK, (hh + 1) * GLA_HK) for hh in heads]
        vs = [slice(hh * GLA_HV, (hh + 1) * GLA_HV) for hh in heads]
        v = [v_ref[:, vs[hh]] for hh in heads]
        do = [_bf(do_ref[:, vs[hh]]) for hh in heads]
        st = [st_ref[hh] for hh in heads]
        dstn = [dst_scr[hh] for hh in heads]
        dstnb = [_bf(dstn[hh]) for hh in heads]
        p = [_bf(jnp.where(mask, _dot_nt(qeb[:, ks[hh]], keb[:, ks[hh]]), 0.0)) for hh in heads]
        dp = [_bf(jnp.where(mask, _dot_nt(do[hh], v[hh]), 0.0)) for hh in heads]
        dkd = [_dot(v[hh], dstnb[hh]) for hh in heads]
        dv_inter = [_dot_nt(kdb[:, ks[hh]], dstnb[hh]) for hh in heads]
        dqe_inter = [_dot(do[hh], _bf(st[hh])) for hh in heads]
        dst_new = [_dot_tn(do[hh], qeb[:, ks[hh]]) + dstn[hh] * ebl[:, ks[hh]] for hh in heads]
        debl = jnp.concatenate([jnp.sum(dstn[hh] * st[hh], axis=0, keepdims=True) for hh in heads], axis=1)
        dv = [_dot_tn(p[hh], do[hh]) + dv_inter[hh] for hh in heads]
        dqe = jnp.concatenate([_dot(dp[hh], keb[:, ks[hh]]) + dqe_inter[hh] for hh in heads], axis=1)
        dke = jnp.concatenate([_dot_tn(dp[hh], qeb[:, ks[hh]]) for hh in heads], axis=1)
        dkd = jnp.concatenate(dkd, axis=1)
        for hh in heads:
            dst_scr[hh] = dst_new[hh]
            dv_ref[:, vs[hh]] = _bf(dv[hh])
        dkd_kd = dkd * kd
        db = dqe * qe - dke * ke - dkd_kd
        dbl = jnp.sum(dkd_kd, axis=0, keepdims=True) + ebl * debl
        dla = _tri_left(upp, db) + dbl
        dqk_ref[:, :GLA_DK] = _bf(dqe * eb * scale)
        dqk_ref[:, GLA_DK:] = _bf(dke * enb + dkd * ebl_b)
        ddec = dla * (1.0 / GLA_TAU) * (1.0 - jnp.exp(GLA_TAU * la))
        ddecb = _bf(ddec)
        drank_ref[...] = _bf(_dot_nt(ddecb, _bf(wd_ref[...])))
        dwd_ref[...] += _dot_tn(_bf(rank_ref[...]), ddecb)
        dbd_ref[...] += jnp.sum(ddec, axis=0, keepdims=True)

    return pl.pallas_call(
        body, name="gla_bwd",
        grid=(n_chunks,),
        in_specs=[pl.BlockSpec((None, C, 1024), lambda n: (0, last - n, 0)),
                  pl.BlockSpec((None, C, 1024), lambda n: (0, last - n, 0)),
                  pl.BlockSpec((C, GLA_DK), lambda n: (last - n, 0)),
                  pl.BlockSpec((C, 1024), lambda n: (last - n, 0)),
                  pl.BlockSpec((None, GLA_HEADS, GLA_HV, GLA_HK), lambda n: (last - n, 0, 0, 0)),
                  pl.BlockSpec((C, 128), lambda n: (last - n, 0)),
                  pl.BlockSpec((128, GLA_DK), lambda n: (0, 0))],
        out_specs=(pl.BlockSpec((C, 1024), lambda n: (last - n, 0)),
                   pl.BlockSpec((C, 1024), lambda n: (last - n, 0)),
                   pl.BlockSpec((C, 128), lambda n: (last - n, 0)),
                   pl.BlockSpec((128, GLA_DK), lambda n: (0, 0)),
                   pl.BlockSpec((1, GLA_DK), lambda n: (0, 0))),
        out_shape=(jax.ShapeDtypeStruct((T, 1024), BF16),
                   jax.ShapeDtypeStruct((T, 1024), BF16),
                   jax.ShapeDtypeStruct((T, 128), BF16),
                   jax.ShapeDtypeStruct((128, GLA_DK), F32),
                   jax.ShapeDtypeStruct((1, GLA_DK), F32)),
        scratch_shapes=[pltpu.VMEM((GLA_HEADS, GLA_HV, GLA_HK), F32)],
        compiler_params=_cparams(("arbitrary",)),
    )(projf, projb, la, do_gla, st_all, rank, wdec)


def _sb_logs(z):
    lsz = jnp.minimum(z, 0.0) - _softplus_neg_abs(z)
    return lsz, lsz - z


SB_HG_FWD = 8
SB_HG_BWD = 4
SB_KEYS = 256


def _sb_fwd_call(projb, wp_shard):
    T = projb.shape[1]
    B = SB_BLOCK
    HG = SB_HG_FWD
    W = HG * SB_HD
    scale = 1.0 / math.sqrt(SB_HD)
    KB = min(SB_KEYS, T)
    n_h, n_i = SB_HEADS // HG, T // B

    def body(q_ref, k_ref, v_ref, wp_ref, o_ref, wpall_ref, cb_scr, send_sems, recv_sems, loc_sem):
        i = pl.program_id(1)
        own, pairs = _push_copies(wp_ref, wpall_ref, send_sems, recv_sems, loc_sem, scatter=False)

        @pl.when((pl.program_id(0) == 0) & (i == 0))
        def _():
            _push_start(own, pairs)

        rows = HG * B
        after = (_iota2(KB, KB, 0) > _iota2(KB, KB, 1)).astype(F32)
        tri = _bf(jnp.concatenate([after, jnp.ones((KB, KB), F32)], axis=1))
        o_ref[...] = jnp.zeros_like(o_ref)
        cb_scr[...] = jnp.zeros_like(cb_scr)

        def block(jp, masked):
            off = pl.multiple_of(jp * KB, KB)
            z = jnp.concatenate(
                [_dot_nt(q_ref[:, hh * SB_HD:(hh + 1) * SB_HD], k_ref[pl.ds(off, KB), hh * SB_HD:(hh + 1) * SB_HD])
                 for hh in range(HG)], axis=0) * scale
            lsz, l1m = _sb_logs(z)
            if masked:
                strict = (jp * KB + _iota2(rows, KB, 1)) < (i * B + (_iota2(rows, KB, 0) & (B - 1)))
                l1m = jnp.where(strict, l1m, 0.0)
            r = _tri2_right(l1m, tri)
            cb = cb_scr[...]
            a = jnp.exp(lsz + cb + r[:, :KB])
            if masked:
                a = jnp.where(strict, a, 0.0)
            cb_scr[...] = cb + r[:, KB:]
            ab = _bf(a)
            for hh in range(HG):
                cs = slice(hh * SB_HD, (hh + 1) * SB_HD)
                o_ref[:, cs] += _dot(ab[hh * B:(hh + 1) * B, :], v_ref[pl.ds(off, KB), cs])

        jp0 = (i * B) // KB
        block(jp0, True)

        def step(jj, c):
            block(jp0 - jj, False)
            return c

        lax.fori_loop(1, jp0 + 1, step, 0)

        @pl.when((pl.program_id(0) == n_h - 1) & (i == n_i - 1))
        def _():
            _push_wait(own, pairs)

    return pl.pallas_call(
        body, name="sb_fwd",
        grid=(n_h, n_i),
        in_specs=[pl.BlockSpec((None, B, W), lambda h, i: (1, i, h)),
                  pl.BlockSpec((None, T, W), lambda h, i: (2, 0, h)),
                  pl.BlockSpec((None, T, W), lambda h, i: (3, 0, h)),
                  _ANY],
        out_specs=(pl.BlockSpec((B, W), lambda h, i: (i, h)), _ANY),
        out_shape=(jax.ShapeDtypeStruct((T, 1024), F32),
                   jax.ShapeDtypeStruct((N_DEV,) + wp_shard.shape, wp_shard.dtype)),
        scratch_shapes=[pltpu.VMEM((HG * B, KB), F32)] + _PUSH_SEMS,
        compiler_params=_cparams(("arbitrary", "arbitrary")),
    )(projb, projb, projb, wp_shard)


def _sb_bwd_call(projb, do_sb, g_p):
    T = projb.shape[1]
    B = SB_BLOCK
    nb = T // B
    HG = SB_HG_BWD
    W = HG * SB_HD
    KB = min(SB_KEYS, T)
    nkb = T // KB
    n_h = SB_HEADS // HG
    scale = 1.0 / math.sqrt(SB_HD)

    def body(q_ref, k_ref, v_ref, do_ref, gp_ref, dq_ref, dk_ref, dv_ref, rp_ref,
             dk_scr, dv_scr, kt_scr, beta_scr, g_scr, dqt_scr, send_sems, recv_sems, loc_sem):
        i = pl.program_id(1)
        own, pairs = _push_copies(gp_ref, rp_ref, send_sems, recv_sems, loc_sem, scatter=True)

        @pl.when((pl.program_id(0) == 0) & (i == 0))
        def _():
            _push_start(own, pairs)

        @pl.when(i == 0)
        def _():
            dk_scr[...] = jnp.zeros_like(dk_scr)
            dv_scr[...] = jnp.zeros_like(dv_scr)
            for hh in range(HG):
                for jb in range(nkb):
                    kt_scr[hh, jb] = _bf(
                        k_ref[jb * KB:(jb + 1) * KB, hh * SB_HD:(hh + 1) * SB_HD].astype(F32).T)

        dqt_scr[...] = jnp.zeros_like(dqt_scr)
        later = _bf((_iota2(KB, KB, 1) > _iota2(KB, KB, 0)).astype(F32))
        earlier = _bf((_iota2(KB, KB, 1) < _iota2(KB, KB, 0)).astype(F32))
        dob = _bf(do_ref[...])
        jp0 = (i * B) // KB

        def strict_mask():
            return (jp0 * KB + _iota2(KB, W, 0)) < (i * B + (_iota2(KB, W, 1) & (B - 1)))

        def heads(fn):
            return [fn(slice(hh * SB_HD, (hh + 1) * SB_HD)) for hh in range(HG)]

        def pass1(jp, cb, masked):
            off = pl.multiple_of(jp * KB, KB)
            z = jnp.concatenate(heads(lambda cs: _dot_nt(k_ref[pl.ds(off, KB), cs], q_ref[:, cs])), axis=1) * scale
            da = jnp.concatenate(heads(lambda cs: _dot_nt(v_ref[pl.ds(off, KB), cs], dob[:, cs])), axis=1)
            lsz, l1m = _sb_logs(z)
            if masked:
                strict = strict_mask()
                l1m = jnp.where(strict, l1m, 0.0)
            a = jnp.exp(lsz + cb + _tri2_left(later, l1m))
            if masked:
                a = jnp.where(strict, a, 0.0)
            g_scr[jp] = a * da
            beta_scr[jp] = jnp.exp(lsz)
            ab = _bf(a)
            for hh in range(HG):
                cs = slice(hh * SB_HD, (hh + 1) * SB_HD)
                dv_scr[pl.ds(off, KB), cs] += _dot(ab[:, cs], dob[:, cs])
            return cb + jnp.sum(l1m, axis=0, keepdims=True)

        zero = jnp.zeros((1, W), F32)
        cb = pass1(jp0, zero, True)
        lax.fori_loop(1, jp0 + 1, lambda jj, cr: pass1(jp0 - jj, cr, False), cb)

        def pass2(jp, cg, masked):
            off = pl.multiple_of(jp * KB, KB)
            g = g_scr[jp]
            beta = beta_scr[jp]
            dz = g * (1.0 - beta) - beta * (cg + _tri2_left(earlier, g))
            if masked:
                dz = jnp.where(strict_mask(), dz, 0.0)
            dzb = _bf(dz * scale)
            for hh in range(HG):
                cs = slice(hh * SB_HD, (hh + 1) * SB_HD)
                dk_scr[pl.ds(off, KB), cs] += _dot(dzb[:, cs], q_ref[:, cs])
                dqt_scr[hh] += _dot(kt_scr[hh, jp], dzb[:, cs])
            return cg + jnp.sum(g, axis=0, keepdims=True)

        cg = lax.fori_loop(0, jp0, lambda jp, cr: pass2(jp, cr, False), zero)
        pass2(jp0, cg, True)
        for hh in range(HG):
            dq_ref[:, hh * SB_HD:(hh + 1) * SB_HD] = _bf(dqt_scr[hh].T)

        @pl.when(i == nb - 1)
        def _():
            dk_ref[...] = _bf(dk_scr[...])
            dv_ref[...] = _bf(dv_scr[...])

        @pl.when((pl.program_id(0) == n_h - 1) & (i == nb - 1))
        def _():
            _push_wait(own, pairs)

    return pl.pallas_call(
        body, name="sb_bwd",
        grid=(n_h, nb),
        in_specs=[pl.BlockSpec((None, B, W), lambda h, i: (1, i, h)),
                  pl.BlockSpec((None, T, W), lambda h, i: (2, 0, h)),
                  pl.BlockSpec((None, T, W), lambda h, i: (3, 0, h)),
                  pl.BlockSpec((B, W), lambda h, i: (i, h)),
                  _ANY],
        out_specs=(pl.BlockSpec((B, W), lambda h, i: (i, h)),
                   pl.BlockSpec((T, W), lambda h, i: (0, h)),
                   pl.BlockSpec((T, W), lambda h, i: (0, h)),
                   _ANY),
        out_shape=(jax.ShapeDtypeStruct((T, 1024), BF16),
                   jax.ShapeDtypeStruct((T, 1024), BF16),
                   jax.ShapeDtypeStruct((T, 1024), BF16),
                   jax.ShapeDtypeStruct(g_p.shape, g_p.dtype)),
        scratch_shapes=[pltpu.VMEM((T, W), F32), pltpu.VMEM((T, W), F32),
                        pltpu.VMEM((HG, nkb, SB_HD, KB), BF16),
                        pltpu.VMEM((nkb, KB, W), F32), pltpu.VMEM((nkb, KB, W), F32),
                        pltpu.VMEM((HG, SB_HD, B), F32)] + _PUSH_SEMS,
        compiler_params=_cparams(("arbitrary", "arbitrary")),
    )(projb, projb, projb, do_sb, g_p)


def _mid_call(o_gla, o_sb, projf, x, target, wpa, wpb, wo, gla_g, b_gate, final_g):
    T, D = x.shape
    tm = min(TBLK, T)

    def body(og_ref, ggate_ref, osb_ref, sgate_ref, ma_ref, mb_ref, x_ref, tgt_ref,
             wpa_ref, wpb_ref, wo_ref, glag_ref, bg_ref, fg_ref,
             dx2_ref, dogla_ref, dosb_ref, dggate_ref, dsgate_ref, dm_ref,
             mt_ref, ogt_ref, obt_ref, dx2b_ref, dya_ref, dyb_ref,
             dfg_ref, dbg_ref, dglag_ref, loss_ref):
        @pl.when(pl.program_id(0) == 0)
        def _():
            dfg_ref[...] = jnp.zeros_like(dfg_ref)
            dbg_ref[...] = jnp.zeros_like(dbg_ref)
            dglag_ref[...] = jnp.zeros_like(dglag_ref)
            loss_ref[...] = jnp.zeros_like(loss_ref)

        glag = glag_ref[...]
        ggate = ggate_ref[...]
        sg = _sigmoid(ggate)
        silu_g = ggate * sg
        ohat, rinv, nrm = [], [], []
        for hh in range(GLA_HEADS):
            oh = og_ref[:, hh * GLA_HV:(hh + 1) * GLA_HV]
            r = lax.rsqrt(jnp.mean(oh * oh, axis=-1, keepdims=True) + EPS)
            ohat.append(oh * r)
            rinv.append(r)
            nrm.append(ohat[-1] * glag)
        n_all = jnp.concatenate(nrm, axis=1)
        og = n_all * silu_g
        ogb = _bf(og)
        ya = _dot(ogb, wpa_ref[...])
        sgate = sgate_ref[...]
        ss = _sigmoid(sgate)
        silu_s = sgate * ss
        osb = osb_ref[...]
        ob = osb * silu_s
        obb = _bf(ob)
        yb = _dot(obb, wpb_ref[...])
        ga = _sigmoid(ma_ref[...] + bg_ref[:, :D])
        gb = _sigmoid(mb_ref[...] + bg_ref[:, D:])
        merged = ga * ya + gb * yb
        mgb = _bf(merged)
        x2 = x_ref[...] + _dot(mgb, wo_ref[...])
        r2 = lax.rsqrt(jnp.mean(x2 * x2, axis=-1, keepdims=True) + EPS)
        xh2 = x2 * r2
        fg = fg_ref[...]
        err = xh2 * fg - tgt_ref[...]
        loss_ref[...] += jnp.broadcast_to(
            0.5 * jnp.sum(jnp.mean(err * err, axis=-1, keepdims=True), axis=0, keepdims=True), (1, 128))
        dy = err * (1.0 / D)
        dfg_ref[...] += jnp.sum(dy * xh2, axis=0, keepdims=True)
        dxh = dy * fg
        dx2 = r2 * (dxh - xh2 * jnp.mean(dxh * xh2, axis=-1, keepdims=True))
        dx2_ref[...] = dx2
        dx2b = _bf(dx2)
        dx2b_ref[...] = dx2b
        dmerged = _dot_nt(dx2b, wo_ref[...])
        dya = dmerged * ga
        dyb = dmerged * gb
        dma = dmerged * ya * ga * (1.0 - ga)
        dmb = dmerged * yb * gb * (1.0 - gb)
        dm_ref[:, :D] = _bf(dma)
        dm_ref[:, D:] = _bf(dmb)
        dbg_ref[:, :D] += jnp.sum(dma, axis=0, keepdims=True)
        dbg_ref[:, D:] += jnp.sum(dmb, axis=0, keepdims=True)
        dyab = _bf(dya)
        dybb = _bf(dyb)
        dya_ref[...] = dyab
        dyb_ref[...] = dybb
        dog = _dot_nt(dyab, wpa_ref[...])
        dob = _dot_nt(dybb, wpb_ref[...])
        dosb_ref[...] = dob * silu_s
        dsgate_ref[...] = _bf(dob * osb * (ss * (1.0 + sgate * (1.0 - ss))))
        dn = dog * silu_g
        dggate_ref[...] = _bf(dog * n_all * (sg * (1.0 + ggate * (1.0 - sg))))
        dglag = jnp.zeros((1, GLA_HV), F32)
        for hh in range(GLA_HEADS):
            dnh = dn[:, hh * GLA_HV:(hh + 1) * GLA_HV]
            dglag = dglag + jnp.sum(dnh * ohat[hh], axis=0, keepdims=True)
            dohat = dnh * glag
            dogla_ref[:, hh * GLA_HV:(hh + 1) * GLA_HV] = rinv[hh] * (
                dohat - ohat[hh] * jnp.mean(dohat * ohat[hh], axis=-1, keepdims=True))
        dglag_ref[...] += dglag
        mt_ref[...] = _bf(merged.T)
        ogt_ref[...] = _bf(og.T)
        obt_ref[...] = _bf(ob.T)

    row = lambda i: (i, 0)
    const = lambda i: (0, 0)
    tile = pl.BlockSpec((tm, D), row)
    tile_t = pl.BlockSpec((None, D, tm), lambda i: (i, 0, 0))
    wspec = pl.BlockSpec((D, D), const)
    return pl.pallas_call(
        body, name="mid",
        grid=(T // tm,),
        in_specs=[tile,
                  pl.BlockSpec((None, tm, D), lambda i: (1, i, 0)),
                  tile,
                  pl.BlockSpec((None, tm, D), lambda i: (2, i, 0)),
                  pl.BlockSpec((None, tm, D), lambda i: (3, i, 0)),
                  pl.BlockSpec((None, tm, D), lambda i: (4, i, 0)),
                  tile, tile, wspec, wspec, wspec,
                  pl.BlockSpec((1, GLA_HV), const),
                  pl.BlockSpec((1, 2 * D), const),
                  pl.BlockSpec((1, D), const)],
        out_specs=(tile, tile, tile, tile, tile,
                   pl.BlockSpec((tm, 2 * D), row),
                   tile_t, tile_t, tile_t, tile, tile, tile,
                   pl.BlockSpec((1, D), const),
                   pl.BlockSpec((1, 2 * D), const),
                   pl.BlockSpec((1, GLA_HV), const),
                   pl.BlockSpec((1, 128), const)),
        out_shape=(jax.ShapeDtypeStruct((T, D), F32),
                   jax.ShapeDtypeStruct((T, D), F32),
                   jax.ShapeDtypeStruct((T, D), F32),
                   jax.ShapeDtypeStruct((T, D), BF16),
                   jax.ShapeDtypeStruct((T, D), BF16),
                   jax.ShapeDtypeStruct((T, 2 * D), BF16),
                   jax.ShapeDtypeStruct((T // tm, D, tm), BF16),
                   jax.ShapeDtypeStruct((T // tm, D, tm), BF16),
                   jax.ShapeDtypeStruct((T // tm, D, tm), BF16),
                   jax.ShapeDtypeStruct((T, D), BF16),
                   jax.ShapeDtypeStruct((T, D), BF16),
                   jax.ShapeDtypeStruct((T, D), BF16),
                   jax.ShapeDtypeStruct((1, D), F32),
                   jax.ShapeDtypeStruct((1, 2 * D), F32),
                   jax.ShapeDtypeStruct((1, GLA_HV), F32),
                   jax.ShapeDtypeStruct((1, 128), F32)),
        compiler_params=_cparams(("arbitrary",)),
    )(o_gla, projf, o_sb, projf, projf, projf, x, target, wpa, wpb, wo, gla_g, b_gate, final_g)


def _dh_call(pieces, dmlog, drank, wt, wr, x, dx2, norm_g):
    T, D = x.shape
    tm = min(256, T)
    npc = len(pieces)
    n_main = N_GROUPS * 1024

    def body(*refs):
        pcs = refs[:npc]
        (dm_ref, dr_ref, w_hbm, wr_ref, x_ref, dx2_ref, g_ref,
         gx_ref, dg_ref, dwr_ref, w_scr, sems) = refs[npc:]

        @pl.when(pl.program_id(0) == 0)
        def _():
            lo = pltpu.make_async_copy(w_hbm.at[pl.ds(0, RANK_COL)], w_scr.at[pl.ds(0, RANK_COL)], sems.at[0])
            hi = pltpu.make_async_copy(w_hbm.at[pl.ds(RANK_COL + GLA_RANK, n_main - RANK_COL)],
                                       w_scr.at[pl.ds(RANK_COL, n_main - RANK_COL)], sems.at[1])
            lo.start()
            hi.start()
            dg_ref[...] = jnp.zeros_like(dg_ref)
            dwr_ref[...] = jnp.zeros_like(dwr_ref)
            lo.wait()
            hi.wait()

        def w_group(g):
            return w_scr[g * 1024:(g + 1) * 1024, :]

        dr = dr_ref[...]
        dh = _dot(dr, wr_ref[...])
        for g in range(npc):
            dh = dh + _dot(pcs[g][...], w_group(g))
        dh = dh + _dot(dm_ref[:, :D], w_group(npc))
        dh = dh + _dot(dm_ref[:, D:], w_group(npc + 1))
        xv = x_ref[...]
        r = lax.rsqrt(jnp.mean(xv * xv, axis=-1, keepdims=True) + EPS)
        xhat = xv * r
        g = g_ref[...]
        dg_ref[...] += jnp.sum(dh * xhat, axis=0, keepdims=True)
        dxhat = dh * g
        gx_ref[...] = r * (dxhat - xhat * jnp.mean(dxhat * xhat, axis=-1, keepdims=True)) + dx2_ref[...]
        dwr_ref[...] += _dot_tn(dr, _bf(xhat * g))

    row = lambda i: (i, 0)
    const = lambda i: (0, 0)
    tile = pl.BlockSpec((tm, D), row)
    return pl.pallas_call(
        body, name="dh",
        grid=(T // tm,),
        in_specs=[tile] * npc + [
            pl.BlockSpec((tm, 2 * D), row),
            pl.BlockSpec((tm, 128), row),
            pl.BlockSpec(memory_space=pl.ANY),
            pl.BlockSpec((128, D), const),
            tile, tile,
            pl.BlockSpec((1, D), const)],
        out_specs=(tile, pl.BlockSpec((1, D), const), pl.BlockSpec((128, D), const)),
        out_shape=(jax.ShapeDtypeStruct((T, D), F32),
                   jax.ShapeDtypeStruct((1, D), F32),
                   jax.ShapeDtypeStruct((128, D), F32)),
        scratch_shapes=[pltpu.VMEM((n_main, D), BF16), pltpu.SemaphoreType.DMA((2,))],
        compiler_params=_cparams(("arbitrary",)),
    )(*pieces, dmlog, drank, wt, wr, x, dx2, norm_g)


def _wgrad_call(lhs_list, lhs_of_group, rhs_list, rhs_of_group, n_transposed, name, gathered=None):
    n_groups = len(rhs_of_group)
    n_tb, D, tb = lhs_list[0].shape
    T = n_tb * tb
    per = min(2, n_tb)
    tk = per * tb
    nk = T // tk
    nl = len(lhs_list)
    carry = gathered is not None

    def body(*refs):
        lhs = refs[:nl]
        rhs = refs[nl:nl + n_groups]
        g = pl.program_id(0)
        i = pl.program_id(1)
        if carry:
            src_ref, out_ref, dst_ref, acc, send_sems, recv_sems, loc_sem = refs[nl + n_groups:]
            own, pairs = _push_copies(src_ref, dst_ref, send_sems, recv_sems, loc_sem, scatter=False)

            @pl.when((g == 0) & (i == 0))
            def _():
                _push_start(own, pairs)
        else:
            out_ref, acc = refs[nl + n_groups:]

        @pl.when(i == 0)
        def _():
            acc[...] = jnp.zeros_like(acc)

        for p in range(n_groups):
            @pl.when(g == p)
            def _(p=p):
                lref = lhs[lhs_of_group[p]]
                part = _dot(lref[0], rhs[p][0:tb, :])
                for b in range(1, per):
                    part = part + _dot(lref[b], rhs[p][b * tb:(b + 1) * tb, :])
                acc[...] += part

        @pl.when((i == nk - 1) & (g < n_transposed))
        def _():
            out_ref[...] = _bf(acc[...].T)

        @pl.when((i == nk - 1) & (g >= n_transposed))
        def _():
            out_ref[...] = _bf(acc[...])

        if carry:
            @pl.when((g == n_groups - 1) & (i == nk - 1))
            def _():
                _push_wait(own, pairs)

    def lhs_spec(a):
        groups = [g for g in range(n_groups) if lhs_of_group[g] == a]
        lo, hi = min(groups), max(groups)
        assert groups == list(range(lo, hi + 1))
        return pl.BlockSpec((per, D, tb), lambda g, i: (jnp.where((g >= lo) & (g <= hi), i, 0), 0, 0))

    def rhs_spec(p):
        cb = rhs_of_group[p][1]
        return pl.BlockSpec((tk, 1024), lambda g, i: (jnp.where(g == p, i, 0), cb))

    in_specs = [lhs_spec(a) for a in range(nl)] + [rhs_spec(p) for p in range(n_groups)]
    out_specs = pl.BlockSpec((None, D, 1024), lambda g, i: (g, 0, 0))
    out_shape = jax.ShapeDtypeStruct((n_groups, D, 1024), BF16)
    scratch = [pltpu.VMEM((D, 1024), F32)]
    operands = list(lhs_list) + [rhs_list[rhs_of_group[p][0]] for p in range(n_groups)]
    if carry:
        in_specs.append(_ANY)
        out_specs = (out_specs, _ANY)
        out_shape = (out_shape, jax.ShapeDtypeStruct((N_DEV,) + gathered.shape, gathered.dtype))
        scratch += _PUSH_SEMS
        operands.append(gathered)
    return pl.pallas_call(
        body, name=name,
        grid=(n_groups, nk),
        in_specs=in_specs, out_specs=out_specs, out_shape=out_shape,
        scratch_shapes=scratch,
        compiler_params=_cparams(("arbitrary", "arbitrary")),
    )(*operands)


def _adamw_call(parts, w, m, v, name):
    R, C = w.shape
    n_parts = parts.shape[0]
    (tr, tc), grid, idx = _tiling_2d(R, C)

    def body(p_ref, w_ref, m_ref, v_ref, g_ref, d_ref, nm_ref, nv_ref):
        g = p_ref[n_parts - 1].astype(F32)
        for k in range(n_parts - 1):
            g = g + p_ref[k].astype(F32)
        mm = ADAM_B1 * m_ref[...] + (1.0 - ADAM_B1) * g
        vv = ADAM_B2 * v_ref[...] + (1.0 - ADAM_B2) * (g * g)
        m_hat = mm / (1.0 - ADAM_B1 ** ADAM_STEP)
        v_hat = vv / (1.0 - ADAM_B2 ** ADAM_STEP)
        d_ref[...] = -ADAM_LR * (m_hat / (jnp.sqrt(v_hat) + ADAM_EPS) + ADAM_WD * w_ref[...])
        g_ref[...] = g
        nm_ref[...] = mm
        nv_ref[...] = vv

    blk = pl.BlockSpec((tr, tc), idx)
    sds = jax.ShapeDtypeStruct((R, C), F32)
    return pl.pallas_call(
        body, name=name,
        grid=grid,
        in_specs=[pl.BlockSpec((n_parts, tr, tc), lambda i: (0,) + idx(i)), blk, blk, blk],
        out_specs=(blk, blk, blk, blk),
        out_shape=(sds, sds, sds, sds),
        compiler_params=_cparams(("arbitrary",)),
    )(parts, w, m, v)


def _local_step(x, target, wt, wr, wdec, bdec, wp_shard, norm_g, gla_g, b_gate, final_g):
    D = x.shape[1]
    projf, projb, rank, ht = _proj_call(x, norm_g, wt, wr)
    o_gla, st_all, la = _gla_fwd_call(projf, projb, rank, wdec, bdec)
    o_sb, wp_all = _sb_fwd_call(projb, wp_shard)
    wp_full = wp_all.transpose(1, 0, 2, 3).reshape(3, D, D)
    (dx2, do_gla, do_sb, dggate, dsgate, dmlog, mt, ogt, obt, dx2b, dya, dyb,
     dfinal_g, db_gate, dgla_g, loss) = _mid_call(o_gla, o_sb, projf, x, target, wp_full[0], wp_full[1],
                                                 wp_full[2], gla_g, b_gate, final_g)
    dw_p = _wgrad_call([ogt, obt, mt], [0, 1, 2], [dya, dyb, dx2b], [(0, 0), (1, 0), (2, 0)], 0, "wgrad_p")
    g_p = dw_p.reshape(3, N_DEV, D // N_DEV, D).transpose(1, 0, 2, 3).reshape(N_DEV, 3 * (D // N_DEV), D)
    dqk, dgv, drank, dwdec, dbdec = _gla_bwd_call(projf, projb, la, do_gla, st_all, rank, wdec)
    dsq, dsk, dsv, r_p = _sb_bwd_call(projb, do_sb, g_p)
    pieces = [dqk, dgv, dggate, dsq, dsk, dsv, dsgate]
    grad_x, dnorm_g, dwr = _dh_call(pieces, dmlog, drank, wt, wr, x, dx2, norm_g)
    small = jnp.concatenate([
        dnorm_g.reshape(-1), dbdec.reshape(-1), dgla_g.reshape(-1), db_gate.reshape(-1), dfinal_g.reshape(-1),
        loss.reshape(-1), dwdec[:GLA_RANK].reshape(-1)]).reshape(1, _SM_LEN)
    rhs_of_group = [(g, 0) for g in range(7)] + [(7, 0), (7, 1)]
    dw_in, r_small = _wgrad_call([ht], [0] * N_GROUPS, pieces + [dmlog], rhs_of_group, N_GROUPS, "wgrad_in",
                                 gathered=small)
    return grad_x, dw_in, dwr, r_p, r_small


_SM_NORM = 0
_SM_BDEC = _SM_NORM + D_MODEL
_SM_GLAG = _SM_BDEC + GLA_DK
_SM_BGATE = _SM_GLAG + GLA_HV
_SM_FINAL = _SM_BGATE + 2 * D_MODEL
_SM_REPL = _SM_FINAL + D_MODEL
_SM_LOSS = _SM_REPL
_SM_WDEC = _SM_LOSS + 128
_SM_LEN = _SM_WDEC + GLA_RANK * GLA_DK


def kernel(x, norm_g, w_in, w_dec_up, b_dec, gla_norm_g, w_pa, w_pb, b_gate, w_o, final_g, loss_target, m_norm_g, m_w_in, m_w_dec_up, m_b_dec, m_gla_norm_g, m_w_pa, m_w_pb, m_b_gate, m_w_o, m_final_g, v_norm_g, v_w_in, v_w_dec_up, v_b_dec, v_gla_norm_g, v_w_pa, v_w_pb, v_b_gate, v_w_o, v_final_g):
    D = D_MODEL
    me = 4 * lax.axis_index("x") + 2 * lax.axis_index("y") + lax.axis_index("c")

    wp_shard = jnp.stack([w_pa, w_pb, w_o]).astype(BF16)
    win_all, wdec_all = _all_gather([w_in.T.astype(BF16), w_dec_up], "gather_w")
    wt = win_all.reshape(IN_COLS, D)
    wr = jnp.pad(wt[RANK_COL:RANK_COL + GLA_RANK], ((0, 128 - GLA_RANK), (0, 0)))
    wdec_full = wdec_all.transpose(1, 0, 2).reshape(GLA_RANK, GLA_DK)
    wdec = jnp.pad(wdec_full, ((0, 128 - GLA_RANK), (0, 0)))

    grad_x, dw_in, dwr, r_p, r_small = _local_step(
        x[0], loss_target[0], wt, wr, wdec, b_dec.reshape(1, -1), wp_shard,
        norm_g.reshape(1, -1), gla_norm_g.reshape(1, -1), b_gate.reshape(1, -1), final_g.reshape(1, -1))

    dmain = dw_in.reshape(N_GROUPS * 1024, D)
    drank = dwr[:GLA_RANK].astype(BF16)

    def part_for(p):
        lo, hi = p * SHARD_COLS, (p + 1) * SHARD_COLS
        pieces = []
        if lo < RANK_COL:
            pieces.append(dmain[lo:min(hi, RANK_COL)])
        if lo < RANK_COL + GLA_RANK and hi > RANK_COL:
            pieces.append(drank[max(lo, RANK_COL) - RANK_COL:min(hi, RANK_COL + GLA_RANK) - RANK_COL])
        if hi > RANK_COL + GLA_RANK:
            pieces.append(dmain[max(lo, RANK_COL + GLA_RANK) - GLA_RANK:hi - GLA_RANK])
        return pieces[0] if len(pieces) == 1 else jnp.concatenate(pieces, axis=0)

    g_in = jnp.stack([part_for(p) for p in range(N_DEV)])
    c_idx = lax.axis_index("c").astype(jnp.int32).reshape(1)
    (p_in,) = _pair_exchange([g_in], "pair_g")
    s_in = _pair_add_call(g_in, p_in, c_idx, "pair_add_in")
    (r_in,) = _chip_exchange([s_in], "scatter_g")

    gw_in, d_in, nm_in, nv_in = (a.T for a in _adamw_call(r_in, w_in.T, m_w_in.T, v_w_in.T, "adamw_in"))
    wp_f32 = jnp.concatenate([w_pa, w_pb, w_o], axis=0)
    mp = jnp.concatenate([m_w_pa, m_w_pb, m_w_o], axis=0)
    vp = jnp.concatenate([v_w_pa, v_w_pb, v_w_o], axis=0)
    gp, dp, nmp, nvp = _adamw_call(r_p, wp_f32, mp, vp, "adamw_p")
    rows = D // N_DEV

    def split3(a):
        return a[:rows], a[rows:2 * rows], a[2 * rows:]

    g_pa, g_pb, g_o = split3(gp)
    d_pa, d_pb, d_o = split3(dp)
    nm_pa, nm_pb, nm_o = split3(nmp)
    nv_pa, nv_pb, nv_o = split3(nvp)

    w_rep = jnp.concatenate([norm_g, b_dec, gla_norm_g, b_gate, final_g]).reshape(1, _SM_REPL)
    m_rep = jnp.concatenate([m_norm_g, m_b_dec, m_gla_norm_g, m_b_gate, m_final_g]).reshape(1, _SM_REPL)
    v_rep = jnp.concatenate([v_norm_g, v_b_dec, v_gla_norm_g, v_b_gate, v_final_g]).reshape(1, _SM_REPL)
    g_rep, d_rep, nm_rep, nv_rep = _adamw_call(r_small[:, :, :_SM_REPL], w_rep, m_rep, v_rep, "adamw_rep")

    def split_rep(a):
        a = a.reshape(-1)
        return (a[_SM_NORM:_SM_BDEC], a[_SM_BDEC:_SM_GLAG], a[_SM_GLAG:_SM_BGATE],
                a[_SM_BGATE:_SM_FINAL], a[_SM_FINAL:_SM_REPL])

    g_norm, g_bdec, g_glag, g_bgate, g_final = split_rep(g_rep)
    d_norm, d_bdec, d_glag, d_bgate, d_final = split_rep(d_rep)
    nm_norm, nm_bdec, nm_glag, nm_bgate, nm_final = split_rep(nm_rep)
    nv_norm, nv_bdec, nv_glag, nv_bgate, nv_final = split_rep(nv_rep)

    wdec_parts = r_small[:, 0, _SM_WDEC:].reshape(N_DEV, GLA_RANK, GLA_DK)
    cols = GLA_DK // N_DEV
    wdec_mine = lax.dynamic_slice_in_dim(wdec_parts, me * cols, cols, axis=2)
    g_wdec, d_wdec, nm_wdec, nv_wdec = _adamw_call(wdec_mine, w_dec_up, m_w_dec_up, v_w_dec_up, "adamw_dec")

    loss_total = jnp.sum(r_small[:, 0, _SM_LOSS])

    return (loss_total, grad_x[None],
            g_norm, gw_in, g_wdec, g_bdec, g_glag, g_pa, g_pb, g_bgate, g_o, g_final,
            d_norm, d_in, d_wdec, d_bdec, d_glag, d_pa, d_pb, d_bgate, d_o, d_final,
            nm_norm, nm_in, nm_wdec, nm_bdec, nm_glag, nm_pa, nm_pb, nm_bgate, nm_o, nm_final,
            nv_norm, nv_in, nv_wdec, nv_bdec, nv_glag, nv_pa, nv_pb, nv_bgate, nv_o, nv_final)
```

```python
import functools
import math

import jax
import jax.numpy as jnp
from jax import lax
from jax.experimental import pallas as pl
from jax.experimental.pallas import tpu as pltpu

F32 = jnp.float32
BF16 = jnp.bfloat16

N_DEV = 8
D_MODEL = 1024
GLA_HEADS = 4
GLA_HK = 128
GLA_HV = 256
GLA_DK = 512
GLA_RANK = 16
GLA_TAU = 16.0
GLA_CHUNK = 64
SB_HEADS = 8
SB_HD = 128
SB_BLOCK = 128
EPS = 1e-6
N_GROUPS = 9
RANK_COL = 3072
IN_COLS = 9232
SHARD_COLS = IN_COLS // N_DEV

ADAM_LR = 0.001
ADAM_B1 = 0.9
ADAM_B2 = 0.999
ADAM_EPS = 1e-08
ADAM_WD = 0.01
ADAM_STEP = 10

VMEM_LIMIT = 56 * 1024 * 1024
TBLK = 256


def _cparams(sem=None):
    return pltpu.CompilerParams(dimension_semantics=sem, vmem_limit_bytes=VMEM_LIMIT)


def _tiling_2d(rows, cols):
    if rows * cols <= 128 * 1024:
        return (rows, cols), (1,), lambda i: (0, 0)
    if rows % 128 == 0:
        return (128, cols), (rows // 128,), lambda i: (i, 0)
    tc = 256 if cols % 256 == 0 else cols
    return (rows, tc), (cols // tc,), lambda i: (0, i)


def _dot(a, b):
    return jnp.dot(a, b, preferred_element_type=F32)


def _dot_nt(a, b):
    return lax.dot_general(a, b, (((1,), (1,)), ((), ())), preferred_element_type=F32)


def _dot_tn(a, b):
    return lax.dot_general(a, b, (((0,), (0,)), ((), ())), preferred_element_type=F32)


def _bf(x):
    return x.astype(BF16)


def _split3(x):
    hi = x.astype(BF16)
    r = x - hi.astype(F32)
    mid = r.astype(BF16)
    lo = (r - mid.astype(F32)).astype(BF16)
    return hi, mid, lo


def _tri_left(tri, x):
    hi, mid, lo = _split3(x)
    return _dot(tri, hi) + _dot(tri, mid) + _dot(tri, lo)


def _split2(x):
    hi = lax.bitcast_convert_type(lax.bitcast_convert_type(x, jnp.uint32) & jnp.uint32(0xFFFF0000), F32)
    return hi.astype(BF16), (x - hi).astype(BF16)


def _tri2_left(tri, x):
    hi, lo = _split2(x)
    return _dot(tri, hi) + _dot(tri, lo)


def _tri2_right(x, tri):
    hi, lo = _split2(x)
    return _dot(hi, tri) + _dot(lo, tri)


def _iota2(n, m, dim):
    return lax.broadcasted_iota(jnp.int32, (n, m), dim)


def _sigmoid(x):
    return 1.0 / (1.0 + jnp.exp(-x))


def _softplus_neg_abs(z):
    return jnp.log(1.0 + jnp.exp(-jnp.abs(z)))


_ANY = pl.BlockSpec(memory_space=pl.ANY)


def _mesh_pos():
    return lax.axis_index("x"), lax.axis_index("y"), lax.axis_index("c")


def _other_chips(x, y):
    return [(1 - x, y), (x, 1 - y), (1 - x, 1 - y)]


def _rcopy(src, dst, send_sem, recv_sem, to):
    return pltpu.make_async_remote_copy(src_ref=src, dst_ref=dst, send_sem=send_sem, recv_sem=recv_sem,
                                        device_id=to, device_id_type=pl.DeviceIdType.MESH)


def _push_copies(src_ref, dst_ref, send_sems, recv_sems, loc_sem, scatter):
    x, y, c = _mesh_pos()
    me = 4 * x + 2 * y + c
    own = pltpu.make_async_copy(src_ref.at[me] if scatter else src_ref, dst_ref.at[me], loc_sem)
    pairs = []
    for k in range(1, N_DEV):
        px = 1 - x if k & 4 else x
        py = 1 - y if k & 2 else y
        pc = 1 - c if k & 1 else c
        pid = 4 * px + 2 * py + pc
        src = src_ref.at[pid] if scatter else src_ref
        send = _rcopy(src, dst_ref.at[me], send_sems.at[k - 1], recv_sems.at[k - 1], (px, py, pc))
        recv = _rcopy(src, dst_ref.at[pid], send_sems.at[k - 1], recv_sems.at[k - 1], (px, py, pc))
        pairs.append((send, recv))
    return own, pairs


def _push_start(own, pairs):
    own.start()
    for send, _ in pairs:
        send.start()


def _push_wait(own, pairs):
    for _, recv in pairs:
        recv.wait_recv()
    for send, _ in pairs:
        send.wait_send()
    own.wait()


_PUSH_SEMS = [pltpu.SemaphoreType.DMA((N_DEV - 1,)), pltpu.SemaphoreType.DMA((N_DEV - 1,)),
              pltpu.SemaphoreType.DMA]


def _all_gather(arrs, name):
    n = len(arrs)

    def body(*refs):
        ins = refs[:n]
        outs = refs[n:2 * n]
        send_sems, recv_sems, loc_sems = refs[2 * n:]
        x, y, c = _mesh_pos()
        sib = (x, y, 1 - c)
        chips = _other_chips(x, y)

        def place(a, px, py, pc):
            return outs[a].at[4 * px + 2 * py + pc]

        def copy(a, k, block, to, src=None):
            dst = place(a, *block)
            return _rcopy(dst if src is None else src, dst, send_sems.at[a, k], recv_sems.at[a, k], to)

        mine = [pltpu.make_async_copy(ins[a], place(a, x, y, c), loc_sems.at[a]) for a in range(n)]
        for cp in mine:
            cp.start()
        first = [copy(a, 0, (x, y, c), sib, src=ins[a]) for a in range(n)]
        for j, chip in enumerate(chips):
            first += [copy(a, 1 + j, (x, y, c), (*chip, c), src=ins[a]) for a in range(n)]
        for cp in first:
            cp.start()
        passed = []
        for j, chip in enumerate(chips):
            for a in range(n):
                copy(a, 1 + j, (*chip, c), (x, y, c)).wait_recv()
                fwd = copy(a, 4 + j, (*chip, c), sib)
                fwd.start()
                passed.append(fwd)
        for a in range(n):
            copy(a, 0, sib, (x, y, c)).wait_recv()
        for j, chip in enumerate(chips):
            for a in range(n):
                copy(a, 4 + j, (*chip, 1 - c), (x, y, c)).wait_recv()
        for cp in first + passed:
            cp.wait_send()
        for cp in mine:
            cp.wait()

    return pl.pallas_call(
        body, name=name,
        out_shape=tuple(jax.ShapeDtypeStruct((N_DEV,) + a.shape, a.dtype) for a in arrs),
        in_specs=[_ANY] * n,
        out_specs=tuple([_ANY] * n),
        scratch_shapes=[pltpu.SemaphoreType.DMA((n, 7)), pltpu.SemaphoreType.DMA((n, 7)),
                        pltpu.SemaphoreType.DMA((n,))],
    )(*arrs)


def _pair_exchange(arrs, name):
    n = len(arrs)

    def body(*refs):
        ins = refs[:n]
        outs = refs[n:2 * n]
        send_sems, recv_sems = refs[2 * n:]
        x, y, c = _mesh_pos()
        copies = []
        for a in range(n):
            for q in range(4):
                cp = _rcopy(ins[a].at[2 * q + (1 - c)], outs[a].at[q], send_sems.at[a, q], recv_sems.at[a, q],
                            (x, y, 1 - c))
                cp.start()
                copies.append(cp)
        for cp in copies:
            cp.wait_recv()
        for cp in copies:
            cp.wait_send()

    return pl.pallas_call(
        body, name=name,
        out_shape=tuple(jax.ShapeDtypeStruct((4,) + a.shape[1:], a.dtype) for a in arrs),
        in_specs=[_ANY] * n,
        out_specs=tuple([_ANY] * n),
        scratch_shapes=[pltpu.SemaphoreType.DMA((n, 4)), pltpu.SemaphoreType.DMA((n, 4))],
    )(*arrs)


def _pair_add_call(parts, recv, c_idx, name):
    _, R, C = parts.shape
    (tr, tc), (steps,), idx = _tiling_2d(R, C)

    def body(c_ref, p_ref, r_ref, o_ref):
        o_ref[...] = (p_ref[...].astype(F32) + r_ref[...].astype(F32)).astype(o_ref.dtype)

    return pl.pallas_call(
        body, name=name,
        grid_spec=pltpu.PrefetchScalarGridSpec(
            num_scalar_prefetch=1,
            grid=(4, steps),
            in_specs=[pl.BlockSpec((None, tr, tc), lambda q, i, c_ref: (2 * q + c_ref[0],) + idx(i)),
                      pl.BlockSpec((None, tr, tc), lambda q, i, c_ref: (q,) + idx(i))],
            out_specs=pl.BlockSpec((None, tr, tc), lambda q, i, c_ref: (q,) + idx(i))),
        out_shape=jax.ShapeDtypeStruct((4, R, C), parts.dtype),
        compiler_params=_cparams(("arbitrary", "arbitrary")),
    )(c_idx, parts, recv)


def _chip_exchange(arrs, name):
    n = len(arrs)

    def body(*refs):
        ins = refs[:n]
        outs = refs[n:2 * n]
        send_sems, recv_sems, loc_sems = refs[2 * n:]
        x, y, c = _mesh_pos()
        mine = [pltpu.make_async_copy(ins[a].at[2 * x + y], outs[a].at[3], loc_sems.at[a]) for a in range(n)]
        for cp in mine:
            cp.start()
        copies = []
        for j, (px, py) in enumerate(_other_chips(x, y)):
            for a in range(n):
                cp = _rcopy(ins[a].at[2 * px + py], outs[a].at[j], send_sems.at[a, j], recv_sems.at[a, j],
                            (px, py, c))
                cp.start()
                copies.append(cp)
        for cp in copies:
            cp.wait_recv()
        for cp in copies:
            cp.wait_send()
        for cp in mine:
            cp.wait()

    return pl.pallas_call(
        body, name=name,
        out_shape=tuple(jax.ShapeDtypeStruct(a.shape, a.dtype) for a in arrs),
        in_specs=[_ANY] * n,
        out_specs=tuple([_ANY] * n),
        scratch_shapes=[pltpu.SemaphoreType.DMA((n, 3)), pltpu.SemaphoreType.DMA((n, 3)),
                        pltpu.SemaphoreType.DMA((n,))],
    )(*arrs)


def _group_row(g):
    return GLA_RANK * (g * (1024 // GLA_RANK) + (g >= RANK_COL // 1024))


def _proj_call(x, norm_g, wt, wr):
    T, D = x.shape
    tm = min(1024, T)
    assert tm % TBLK == 0
    n_i = T // tm

    def f_slot(j):
        return ((j >= 2).astype(jnp.int32) + (j >= 6).astype(jnp.int32)
                + (j >= 7).astype(jnp.int32) + (j >= 8).astype(jnp.int32))

    def b_slot(j):
        return (j >= 3).astype(jnp.int32) + (j >= 4).astype(jnp.int32) + (j >= 5).astype(jnp.int32)

    def body(x_ref, g_ref, w_ref, wr_ref, pf_ref, pb_ref, rank_ref, ht_ref, h_scr):
        j = pl.program_id(1)

        @pl.when(j == 0)
        def _():
            xv = x_ref[...]
            r = lax.rsqrt(jnp.mean(xv * xv, axis=-1, keepdims=True) + EPS)
            h = (xv * r) * g_ref[...]
            hb = _bf(h)
            h_scr[...] = hb
            for b in range(tm // TBLK):
                ht_ref[b] = _bf(h[b * TBLK:(b + 1) * TBLK].T)
            rank_ref[...] = _dot_nt(hb, wr_ref[...])

        is_b = (j == 1) | ((j >= 3) & (j <= 5))

        @pl.when(is_b)
        def _():
            pb_ref[...] = _bf(_dot_nt(h_scr[...], w_ref[...]))

        @pl.when(jnp.logical_not(is_b))
        def _():
            pf_ref[...] = _dot_nt(h_scr[...], w_ref[...])

    return pl.pallas_call(
        body, name="proj",
        grid=(n_i, N_GROUPS),
        in_specs=[pl.BlockSpec((tm, D), lambda i, j: (i, 0)),
                  pl.BlockSpec((1, D), lambda i, j: (0, 0)),
                  pl.BlockSpec((pl.Element(1024), pl.Element(D)), lambda i, j: (_group_row(j), 0)),
                  pl.BlockSpec((128, D), lambda i, j: (0, 0))],
        out_specs=(pl.BlockSpec((None, tm, 1024), lambda i, j: (f_slot(j), i, 0)),
                   pl.BlockSpec((None, tm, 1024), lambda i, j: (b_slot(j), i, 0)),
                   pl.BlockSpec((tm, 128), lambda i, j: (i, 0)),
                   pl.BlockSpec((tm // TBLK, D, TBLK), lambda i, j: (i, 0, 0))),
        out_shape=(jax.ShapeDtypeStruct((5, T, 1024), F32),
                   jax.ShapeDtypeStruct((4, T, 1024), BF16),
                   jax.ShapeDtypeStruct((T, 128), F32),
                   jax.ShapeDtypeStruct((T // TBLK, D, TBLK), BF16)),
        scratch_shapes=[pltpu.VMEM((tm, D), BF16)],
        compiler_params=_cparams(("arbitrary", "arbitrary")),
    )(x, norm_g, wt, wr)


def _gla_chunk_terms(la_h, q, k):
    C = GLA_CHUNK
    low = _bf((_iota2(C, C, 0) >= _iota2(C, C, 1)).astype(F32))
    b = _tri_left(low, la_h)
    bl = b[C - 1:C, :]
    eb = jnp.exp(b)
    enb = jnp.exp(-b)
    ebl_b = jnp.exp(bl - b)
    scale = GLA_HK ** -0.5
    qe = q * eb * scale
    ke = k * enb
    kd = k * ebl_b
    return b, bl, eb, enb, ebl_b, qe, ke, kd


def _gla_fwd_call(projf, projb, rank, wdec, bdec):
    T = projf.shape[1]
    C = GLA_CHUNK
    n_chunks = T // C

    def body(qk_ref, v_ref, rank_ref, wd_ref, bd_ref, o_ref, st_ref, la_ref, st_scr):
        @pl.when(pl.program_id(0) == 0)
        def _():
            st_scr[...] = jnp.zeros_like(st_scr)

        dec = _dot(_bf(rank_ref[...]), _bf(wd_ref[...])) + bd_ref[...]
        la = (jnp.minimum(dec, 0.0) - _softplus_neg_abs(dec)) / GLA_TAU
        la_ref[...] = la
        mask = _iota2(C, C, 0) >= _iota2(C, C, 1)
        _, bl, _, _, _, qe, ke, kd = _gla_chunk_terms(la, qk_ref[:, :GLA_DK], qk_ref[:, GLA_DK:])
        qeb, keb, kdb = _bf(qe), _bf(ke), _bf(kd)
        ebl = jnp.exp(bl)
        heads = range(GLA_HEADS)
        ks = [slice(hh * GLA_HK, (hh + 1) * GLA_HK) for hh in heads]
        vs = [slice(hh * GLA_HV, (hh + 1) * GLA_HV) for hh in heads]
        st = [st_scr[hh] for hh in heads]
        p = [_bf(jnp.where(mask, _dot_nt(qeb[:, ks[hh]], keb[:, ks[hh]]), 0.0)) for hh in heads]
        inter = [_dot_nt(qeb[:, ks[hh]], _bf(st[hh])) for hh in heads]
        upd = [_dot_tn(v_ref[:, vs[hh]], kdb[:, ks[hh]]) for hh in heads]
        intra = [_dot(p[hh], v_ref[:, vs[hh]]) for hh in heads]
        for hh in heads:
            st_ref[hh] = st[hh]
            o_ref[:, vs[hh]] = intra[hh] + inter[hh]
            st_scr[hh] = st[hh] * ebl[:, ks[hh]] + upd[hh]

    return pl.pallas_call(
        body, name="gla_fwd",
        grid=(n_chunks,),
        in_specs=[pl.BlockSpec((None, C, 1024), lambda n: (0, n, 0)),
                  pl.BlockSpec((None, C, 1024), lambda n: (0, n, 0)),
                  pl.BlockSpec((C, 128), lambda n: (n, 0)),
                  pl.BlockSpec((128, GLA_DK), lambda n: (0, 0)),
                  pl.BlockSpec((1, GLA_DK), lambda n: (0, 0))],
        out_specs=(pl.BlockSpec((C, 1024), lambda n: (n, 0)),
                   pl.BlockSpec((None, GLA_HEADS, GLA_HV, GLA_HK), lambda n: (n, 0, 0, 0)),
                   pl.BlockSpec((C, GLA_DK), lambda n: (n, 0))),
        out_shape=(jax.ShapeDtypeStruct((T, 1024), F32),
                   jax.ShapeDtypeStruct((n_chunks, GLA_HEADS, GLA_HV, GLA_HK), F32),
                   jax.ShapeDtypeStruct((T, GLA_DK), F32)),
        scratch_shapes=[pltpu.VMEM((GLA_HEADS, GLA_HV, GLA_HK), F32)],
        compiler_params=_cparams(("arbitrary",)),
    )(projf, projb, rank, wdec, bdec)


def _gla_bwd_call(projf, projb, la, do_gla, st_all, rank, wdec):
    T = projf.shape[1]
    C = GLA_CHUNK
    n_chunks = T // C
    last = n_chunks - 1

    def body(qk_ref, v_ref, la_ref, do_ref, st_ref, rank_ref, wd_ref,
             dqk_ref, dv_ref, drank_ref, dwd_ref, dbd_ref, dst_scr):
        @pl.when(pl.program_id(0) == 0)
        def _():
            dst_scr[...] = jnp.zeros_like(dst_scr)
            dwd_ref[...] = jnp.zeros_like(dwd_ref)
            dbd_ref[...] = jnp.zeros_like(dbd_ref)

        mask = _iota2(C, C, 0) >= _iota2(C, C, 1)
        upp = _bf((_iota2(C, C, 0) <= _iota2(C, C, 1)).astype(F32))
        scale = GLA_HK ** -0.5
        la = la_ref[...]
        _, bl, eb, enb, ebl_b, qe, ke, kd = _gla_chunk_terms(la, qk_ref[:, :GLA_DK], qk_ref[:, GLA_DK:])
        qeb, keb, kdb = _bf(qe), _bf(ke), _bf(kd)
        ebl = jnp.exp(bl)
        heads = range(GLA_HEADS)
        ks = [slice(hh * GLA_HK, (hh + 1) * GLA_HK) for hh in heads]
        vs = [slice(hh * GLA_HV, (hh + 1) * GLA_HV) for hh in heads]
        v = [v_ref[:, vs[hh]] for hh in heads]
        do = [_bf(do_ref[:, vs[hh]]) for hh in heads]
        st = [st_ref[hh] for hh in heads]
        dstn = [dst_scr[hh] for hh in heads]
        dstnb = [_bf(dstn[hh]) for hh in heads]
        p = [_bf(jnp.where(mask, _dot_nt(qeb[:, ks[hh]], keb[:, ks[hh]]), 0.0)) for hh in heads]
        dp = [_bf(jnp.where(mask, _dot_nt(do[hh], v[hh]), 0.0)) for hh in heads]
        dkd = [_dot(v[hh], dstnb[hh]) for hh in heads]
        dv_inter = [_dot_nt(kdb[:, ks[hh]], dstnb[hh]) for hh in heads]
        dqe_inter = [_dot(do[hh], _bf(st[hh])) for hh in heads]
        dst_new = [_dot_tn(do[hh], qeb[:, ks[hh]]) + dstn[hh] * ebl[:, ks[hh]] for hh in heads]
        debl = jnp.concatenate([jnp.sum(dstn[hh] * st[hh], axis=0, keepdims=True) for hh in heads], axis=1)
        dv = [_dot_tn(p[hh], do[hh]) + dv_inter[hh] for hh in heads]
        dqe = jnp.concatenate([_dot(dp[hh], keb[:, ks[hh]]) + dqe_inter[hh] for hh in heads], axis=1)
        dke = jnp.concatenate([_dot_tn(dp[hh], qeb[:, ks[hh]]) for hh in heads], axis=1)
        dkd = jnp.concatenate(dkd, axis=1)
        for hh in heads:
            dst_scr[hh] = dst_new[hh]
            dv_ref[:, vs[hh]] = _bf(dv[hh])
        dkd_kd = dkd * kd
        db = dqe * qe - dke * ke - dkd_kd
        dbl = jnp.sum(dkd_kd, axis=0, keepdims=True) + ebl * debl
        dla = _tri_left(upp, db) + dbl
        dqk_ref[:, :GLA_DK] = _bf(dqe * eb * scale)
        dqk_ref[:, GLA_DK:] = _bf(dke * enb + dkd * ebl_b)
        ddec = dla * (1.0 / GLA_TAU) * (1.0 - jnp.exp(GLA_TAU * la))
        ddecb = _bf(ddec)
        drank_ref[...] = _bf(_dot_nt(ddecb, _bf(wd_ref[...])))
        dwd_ref[...] += _dot_tn(_bf(rank_ref[...]), ddecb)
        dbd_ref[...] += jnp.sum(ddec, axis=0, keepdims=True)

    return pl.pallas_call(
        body, name="gla_bwd",
        grid=(n_chunks,),
        in_specs=[pl.BlockSpec((None, C, 1024), lambda n: (0, last - n, 0)),
                  pl.BlockSpec((None, C, 1024), lambda n: (0, last - n, 0)),
                  pl.BlockSpec((C, GLA_DK), lambda n: (last - n, 0)),
                  pl.BlockSpec((C, 1024), lambda n: (last - n, 0)),
                  pl.BlockSpec((None, GLA_HEADS, GLA_HV, GLA_HK), lambda n: (last - n, 0, 0, 0)),
                  pl.BlockSpec((C, 128), lambda n: (last - n, 0)),
                  pl.BlockSpec((128, GLA_DK), lambda n: (0, 0))],
        out_specs=(pl.BlockSpec((C, 1024), lambda n: (last - n, 0)),
                   pl.BlockSpec((C, 1024), lambda n: (last - n, 0)),
                   pl.BlockSpec((C, 128), lambda n: (last - n, 0)),
                   pl.BlockSpec((128, GLA_DK), lambda n: (0, 0)),
                   pl.BlockSpec((1, GLA_DK), lambda n: (0, 0))),
        out_shape=(jax.ShapeDtypeStruct((T, 1024), BF16),
                   jax.ShapeDtypeStruct((T, 1024), BF16),
                   jax.ShapeDtypeStruct((T, 128), BF16),
                   jax.ShapeDtypeStruct((128, GLA_DK), F32),
                   jax.ShapeDtypeStruct((1, GLA_DK), F32)),
        scratch_shapes=[pltpu.VMEM((GLA_HEADS, GLA_HV, GLA_HK), F32)],
        compiler_params=_cparams(("arbitrary",)),
    )(projf, projb, la, do_gla, st_all, rank, wdec)


def _sb_logs(z):
    lsz = jnp.minimum(z, 0.0) - _softplus_neg_abs(z)
    return lsz, lsz - z


SB_HG_FWD = 8
SB_HG_BWD = 4
SB_KEYS = 256
SB_DEAD = -105.0


def _sb_fwd_call(projb, wp_shard):
    T = projb.shape[1]
    B = SB_BLOCK
    HG = SB_HG_FWD
    W = HG * SB_HD
    scale = 1.0 / math.sqrt(SB_HD)
    KB = min(SB_KEYS, T)
    n_h, n_i = SB_HEADS // HG, T // B

    def body(q_ref, k_ref, v_ref, wp_ref, o_ref, wpall_ref, cb_scr, send_sems, recv_sems, loc_sem):
        i = pl.program_id(1)
        own, pairs = _push_copies(wp_ref, wpall_ref, send_sems, recv_sems, loc_sem, scatter=False)

        @pl.when((pl.program_id(0) == 0) & (i == 0))
        def _():
            _push_start(own, pairs)

        rows = HG * B
        after = (_iota2(KB, KB, 0) > _iota2(KB, KB, 1)).astype(F32)
        tri = _bf(jnp.concatenate([after, jnp.ones((KB, KB), F32)], axis=1))
        o_ref[...] = jnp.zeros_like(o_ref)
        cb_scr[...] = jnp.zeros_like(cb_scr)

        def block(jp, masked):
            off = pl.multiple_of(jp * KB, KB)
            z = jnp.concatenate(
                [_dot_nt(q_ref[:, hh * SB_HD:(hh + 1) * SB_HD], k_ref[pl.ds(off, KB), hh * SB_HD:(hh + 1) * SB_HD])
                 for hh in range(HG)], axis=0) * scale
            lsz, l1m = _sb_logs(z)
            if masked:
                strict = (jp * KB + _iota2(rows, KB, 1)) < (i * B + (_iota2(rows, KB, 0) & (B - 1)))
                l1m = jnp.where(strict, l1m, 0.0)
            r = _tri2_right(l1m, tri)
            cb = cb_scr[...]
            a = jnp.exp(lsz + cb + r[:, :KB])
            if masked:
                a = jnp.where(strict, a, 0.0)
            cb_scr[...] = cb + r[:, KB:]
            ab = _bf(a)
            for hh in range(HG):
                cs = slice(hh * SB_HD, (hh + 1) * SB_HD)
                o_ref[:, cs] += _dot(ab[hh * B:(hh + 1) * B, :], v_ref[pl.ds(off, KB), cs])

        jp0 = (i * B) // KB
        block(jp0, True)

        def live(state):
            jj, dead = state
            return (jj <= jp0) & jnp.logical_not(dead)

        def step(state):
            jj, _ = state
            block(jp0 - jj, False)
            return jj + 1, jnp.max(cb_scr[:, :B]) < SB_DEAD

        lax.while_loop(live, step, (jnp.int32(1), jnp.max(cb_scr[:, :B]) < SB_DEAD))

        @pl.when((pl.program_id(0) == n_h - 1) & (i == n_i - 1))
        def _():
            _push_wait(own, pairs)

    return pl.pallas_call(
        body, name="sb_fwd",
        grid=(n_h, n_i),
        in_specs=[pl.BlockSpec((None, B, W), lambda h, i: (1, i, h)),
                  pl.BlockSpec((None, T, W), lambda h, i: (2, 0, h)),
                  pl.BlockSpec((None, T, W), lambda h, i: (3, 0, h)),
                  _ANY],
        out_specs=(pl.BlockSpec((B, W), lambda h, i: (i, h)), _ANY),
        out_shape=(jax.ShapeDtypeStruct((T, 1024), F32),
                   jax.ShapeDtypeStruct((N_DEV,) + wp_shard.shape, wp_shard.dtype)),
        scratch_shapes=[pltpu.VMEM((HG * B, KB), F32)] + _PUSH_SEMS,
        compiler_params=_cparams(("arbitrary", "arbitrary")),
    )(projb, projb, projb, wp_shard)


def _sb_bwd_call(projb, do_sb, g_p):
    T = projb.shape[1]
    B = SB_BLOCK
    nb = T // B
    HG = SB_HG_BWD
    W = HG * SB_HD
    KB = min(SB_KEYS, T)
    nkb = T // KB
    n_h = SB_HEADS // HG
    scale = 1.0 / math.sqrt(SB_HD)

    def body(q_ref, k_ref, v_ref, do_ref, gp_ref, dq_ref, dk_ref, dv_ref, rp_ref,
             dk_scr, dv_scr, kt_scr, beta_scr, g_scr, dqt_scr, send_sems, recv_sems, loc_sem):
        i = pl.program_id(1)
        own, pairs = _push_copies(gp_ref, rp_ref, send_sems, recv_sems, loc_sem, scatter=True)

        @pl.when((pl.program_id(0) == 0) & (i == 0))
        def _():
            _push_start(own, pairs)

        @pl.when(i == 0)
        def _():
            dk_scr[...] = jnp.zeros_like(dk_scr)
            dv_scr[...] = jnp.zeros_like(dv_scr)
            for hh in range(HG):
                for jb in range(nkb):
                    kt_scr[hh, jb] = _bf(
                        k_ref[jb * KB:(jb + 1) * KB, hh * SB_HD:(hh + 1) * SB_HD].astype(F32).T)

        dqt_scr[...] = jnp.zeros_like(dqt_scr)
        later = _bf((_iota2(KB, KB, 1) > _iota2(KB, KB, 0)).astype(F32))
        earlier = _bf((_iota2(KB, KB, 1) < _iota2(KB, KB, 0)).astype(F32))
        dob = _bf(do_ref[...])
        jp0 = (i * B) // KB

        def strict_mask():
            return (jp0 * KB + _iota2(KB, W, 0)) < (i * B + (_iota2(KB, W, 1) & (B - 1)))

        def heads(fn):
            return [fn(slice(hh * SB_HD, (hh + 1) * SB_HD)) for hh in range(HG)]

        def pass1(jp, cb, masked):
            off = pl.multiple_of(jp * KB, KB)
            z = jnp.concatenate(heads(lambda cs: _dot_nt(k_ref[pl.ds(off, KB), cs], q_ref[:, cs])), axis=1) * scale
            da = jnp.concatenate(heads(lambda cs: _dot_nt(v_ref[pl.ds(off, KB), cs], dob[:, cs])), axis=1)
            lsz, l1m = _sb_logs(z)
            if masked:
                strict = strict_mask()
                l1m = jnp.where(strict, l1m, 0.0)
            a = jnp.exp(lsz + cb + _tri2_left(later, l1m))
            if masked:
                a = jnp.where(strict, a, 0.0)
            g_scr[jp] = a * da
            beta_scr[jp] = jnp.exp(lsz)
            ab = _bf(a)
            for hh in range(HG):
                cs = slice(hh * SB_HD, (hh + 1) * SB_HD)
                dv_scr[pl.ds(off, KB), cs] += _dot(ab[:, cs], dob[:, cs])
            return cb + jnp.sum(l1m, axis=0, keepdims=True)

        zero = jnp.zeros((1, W), F32)
        cb = pass1(jp0, zero, True)

        def live(state):
            jj, _, dead = state
            return (jj <= jp0) & jnp.logical_not(dead)

        def step(state):
            jj, cr, _ = state
            cr = pass1(jp0 - jj, cr, False)
            return jj + 1, cr, jnp.max(cr) < SB_DEAD

        n_done, _, _ = lax.while_loop(live, step, (jnp.int32(1), cb, jnp.max(cb) < SB_DEAD))
        jp_first = jp0 - (n_done - 1)

        def pass2(jp, cg, masked):
            off = pl.multiple_of(jp * KB, KB)
            g = g_scr[jp]
            beta = beta_scr[jp]
            dz = g * (1.0 - beta) - beta * (cg + _tri2_left(earlier, g))
            if masked:
                dz = jnp.where(strict_mask(), dz, 0.0)
            dzb = _bf(dz * scale)
            for hh in range(HG):
                cs = slice(hh * SB_HD, (hh + 1) * SB_HD)
                dk_scr[pl.ds(off, KB), cs] += _dot(dzb[:, cs], q_ref[:, cs])
                dqt_scr[hh] += _dot(kt_scr[hh, jp], dzb[:, cs])
            return cg + jnp.sum(g, axis=0, keepdims=True)

        cg = lax.fori_loop(jp_first, jp0, lambda jp, cr: pass2(jp, cr, False), zero)
        pass2(jp0, cg, True)
        for hh in range(HG):
            dq_ref[:, hh * SB_HD:(hh + 1) * SB_HD] = _bf(dqt_scr[hh].T)

        @pl.when(i == nb - 1)
        def _():
            dk_ref[...] = _bf(dk_scr[...])
            dv_ref[...] = _bf(dv_scr[...])

        @pl.when((pl.program_id(0) == n_h - 1) & (i == nb - 1))
        def _():
            _push_wait(own, pairs)

    return pl.pallas_call(
        body, name="sb_bwd",
        grid=(n_h, nb),
        in_specs=[pl.BlockSpec((None, B, W), lambda h, i: (1, i, h)),
                  pl.BlockSpec((None, T, W), lambda h, i: (2, 0, h)),
                  pl.BlockSpec((None, T, W), lambda h, i: (3, 0, h)),
                  pl.BlockSpec((B, W), lambda h, i: (i, h)),
                  _ANY],
        out_specs=(pl.BlockSpec((B, W), lambda h, i: (i, h)),
                   pl.BlockSpec((T, W), lambda h, i: (0, h)),
                   pl.BlockSpec((T, W), lambda h, i: (0, h)),
                   _ANY),
        out_shape=(jax.ShapeDtypeStruct((T, 1024), BF16),
                   jax.ShapeDtypeStruct((T, 1024), BF16),
                   jax.ShapeDtypeStruct((T, 1024), BF16),
                   jax.ShapeDtypeStruct(g_p.shape, g_p.dtype)),
        scratch_shapes=[pltpu.VMEM((T, W), F32), pltpu.VMEM((T, W), F32),
                        pltpu.VMEM((HG, nkb, SB_HD, KB), BF16),
                        pltpu.VMEM((nkb, KB, W), F32), pltpu.VMEM((nkb, KB, W), F32),
                        pltpu.VMEM((HG, SB_HD, B), F32)] + _PUSH_SEMS,
        compiler_params=_cparams(("arbitrary", "arbitrary")),
    )(projb, projb, projb, do_sb, g_p)


def _mid_call(o_gla, o_sb, projf, x, target, wpa, wpb, wo, gla_g, b_gate, final_g):
    T, D = x.shape
    tm = min(TBLK, T)

    def body(og_ref, ggate_ref, osb_ref, sgate_ref, ma_ref, mb_ref, x_ref, tgt_ref,
             wpa_ref, wpb_ref, wo_ref, glag_ref, bg_ref, fg_ref,
             dx2_ref, dogla_ref, dosb_ref, dggate_ref, dsgate_ref, dm_ref,
             mt_ref, ogt_ref, obt_ref, dx2b_ref, dya_ref, dyb_ref,
             dfg_ref, dbg_ref, dglag_ref, loss_ref):
        @pl.when(pl.program_id(0) == 0)
        def _():
            dfg_ref[...] = jnp.zeros_like(dfg_ref)
            dbg_ref[...] = jnp.zeros_like(dbg_ref)
            dglag_ref[...] = jnp.zeros_like(dglag_ref)
            loss_ref[...] = jnp.zeros_like(loss_ref)

        glag = glag_ref[...]
        ggate = ggate_ref[...]
        sg = _sigmoid(ggate)
        silu_g = ggate * sg
        ohat, rinv, nrm = [], [], []
        for hh in range(GLA_HEADS):
            oh = og_ref[:, hh * GLA_HV:(hh + 1) * GLA_HV]
            r = lax.rsqrt(jnp.mean(oh * oh, axis=-1, keepdims=True) + EPS)
            ohat.append(oh * r)
            rinv.append(r)
            nrm.append(ohat[-1] * glag)
        n_all = jnp.concatenate(nrm, axis=1)
        og = n_all * silu_g
        ogb = _bf(og)
        ya = _dot(ogb, wpa_ref[...])
        sgate = sgate_ref[...]
        ss = _sigmoid(sgate)
        silu_s = sgate * ss
        osb = osb_ref[...]
        ob = osb * silu_s
        obb = _bf(ob)
        yb = _dot(obb, wpb_ref[...])
        ga = _sigmoid(ma_ref[...] + bg_ref[:, :D])
        gb = _sigmoid(mb_ref[...] + bg_ref[:, D:])
        merged = ga * ya + gb * yb
        mgb = _bf(merged)
        x2 = x_ref[...] + _dot(mgb, wo_ref[...])
        r2 = lax.rsqrt(jnp.mean(x2 * x2, axis=-1, keepdims=True) + EPS)
        xh2 = x2 * r2
        fg = fg_ref[...]
        err = xh2 * fg - tgt_ref[...]
        loss_ref[...] += jnp.broadcast_to(
            0.5 * jnp.sum(jnp.mean(err * err, axis=-1, keepdims=True), axis=0, keepdims=True), (1, 128))
        dy = err * (1.0 / D)
        dfg_ref[...] += jnp.sum(dy * xh2, axis=0, keepdims=True)
        dxh = dy * fg
        dx2 = r2 * (dxh - xh2 * jnp.mean(dxh * xh2, axis=-1, keepdims=True))
        dx2_ref[...] = dx2
        dx2b = _bf(dx2)
        dx2b_ref[...] = dx2b
        dmerged = _dot_nt(dx2b, wo_ref[...])
        dya = dmerged * ga
        dyb = dmerged * gb
        dma = dmerged * ya * ga * (1.0 - ga)
        dmb = dmerged * yb * gb * (1.0 - gb)
        dm_ref[:, :D] = _bf(dma)
        dm_ref[:, D:] = _bf(dmb)
        dbg_ref[:, :D] += jnp.sum(dma, axis=0, keepdims=True)
        dbg_ref[:, D:] += jnp.sum(dmb, axis=0, keepdims=True)
        dyab = _bf(dya)
        dybb = _bf(dyb)
        dya_ref[...] = dyab
        dyb_ref[...] = dybb
        dog = _dot_nt(dyab, wpa_ref[...])
        dob = _dot_nt(dybb, wpb_ref[...])
        dosb_ref[...] = dob * silu_s
        dsgate_ref[...] = _bf(dob * osb * (ss * (1.0 + sgate * (1.0 - ss))))
        dn = dog * silu_g
        dggate_ref[...] = _bf(dog * n_all * (sg * (1.0 + ggate * (1.0 - sg))))
        dglag = jnp.zeros((1, GLA_HV), F32)
        for hh in range(GLA_HEADS):
            dnh = dn[:, hh * GLA_HV:(hh + 1) * GLA_HV]
            dglag = dglag + jnp.sum(dnh * ohat[hh], axis=0, keepdims=True)
            dohat = dnh * glag
            dogla_ref[:, hh * GLA_HV:(hh + 1) * GLA_HV] = rinv[hh] * (
                dohat - ohat[hh] * jnp.mean(dohat * ohat[hh], axis=-1, keepdims=True))
        dglag_ref[...] += dglag
        mt_ref[...] = _bf(merged.T)
        ogt_ref[...] = _bf(og.T)
        obt_ref[...] = _bf(ob.T)

    row = lambda i: (i, 0)
    const = lambda i: (0, 0)
    tile = pl.BlockSpec((tm, D), row)
    tile_t = pl.BlockSpec((None, D, tm), lambda i: (i, 0, 0))
    wspec = pl.BlockSpec((D, D), const)
    return pl.pallas_call(
        body, name="mid",
        grid=(T // tm,),
        in_specs=[tile,
                  pl.BlockSpec((None, tm, D), lambda i: (1, i, 0)),
                  tile,
                  pl.BlockSpec((None, tm, D), lambda i: (2, i, 0)),
                  pl.BlockSpec((None, tm, D), lambda i: (3, i, 0)),
                  pl.BlockSpec((None, tm, D), lambda i: (4, i, 0)),
                  tile, tile, wspec, wspec, wspec,
                  pl.BlockSpec((1, GLA_HV), const),
                  pl.BlockSpec((1, 2 * D), const),
                  pl.BlockSpec((1, D), const)],
        out_specs=(tile, tile, tile, tile, tile,
                   pl.BlockSpec((tm, 2 * D), row),
                   tile_t, tile_t, tile_t, tile, tile, tile,
                   pl.BlockSpec((1, D), const),
                   pl.BlockSpec((1, 2 * D), const),
                   pl.BlockSpec((1, GLA_HV), const),
                   pl.BlockSpec((1, 128), const)),
        out_shape=(jax.ShapeDtypeStruct((T, D), F32),
                   jax.ShapeDtypeStruct((T, D), F32),
                   jax.ShapeDtypeStruct((T, D), F32),
                   jax.ShapeDtypeStruct((T, D), BF16),
                   jax.ShapeDtypeStruct((T, D), BF16),
                   jax.ShapeDtypeStruct((T, 2 * D), BF16),
                   jax.ShapeDtypeStruct((T // tm, D, tm), BF16),
                   jax.ShapeDtypeStruct((T // tm, D, tm), BF16),
                   jax.ShapeDtypeStruct((T // tm, D, tm), BF16),
                   jax.ShapeDtypeStruct((T, D), BF16),
                   jax.ShapeDtypeStruct((T, D), BF16),
                   jax.ShapeDtypeStruct((T, D), BF16),
                   jax.ShapeDtypeStruct((1, D), F32),
                   jax.ShapeDtypeStruct((1, 2 * D), F32),
                   jax.ShapeDtypeStruct((1, GLA_HV), F32),
                   jax.ShapeDtypeStruct((1, 128), F32)),
        compiler_params=_cparams(("arbitrary",)),
    )(o_gla, projf, o_sb, projf, projf, projf, x, target, wpa, wpb, wo, gla_g, b_gate, final_g)


def _dh_call(pieces, dmlog, drank, wt, wr, x, dx2, norm_g):
    T, D = x.shape
    tm = min(256, T)
    npc = len(pieces)
    n_main = N_GROUPS * 1024

    def body(*refs):
        pcs = refs[:npc]
        (dm_ref, dr_ref, w_hbm, wr_ref, x_ref, dx2_ref, g_ref,
         gx_ref, dg_ref, dwr_ref, w_scr, sems) = refs[npc:]

        @pl.when(pl.program_id(0) == 0)
        def _():
            lo = pltpu.make_async_copy(w_hbm.at[pl.ds(0, RANK_COL)], w_scr.at[pl.ds(0, RANK_COL)], sems.at[0])
            hi = pltpu.make_async_copy(w_hbm.at[pl.ds(RANK_COL + GLA_RANK, n_main - RANK_COL)],
                                       w_scr.at[pl.ds(RANK_COL, n_main - RANK_COL)], sems.at[1])
            lo.start()
            hi.start()
            dg_ref[...] = jnp.zeros_like(dg_ref)
            dwr_ref[...] = jnp.zeros_like(dwr_ref)
            lo.wait()
            hi.wait()

        def w_group(g):
            return w_scr[g * 1024:(g + 1) * 1024, :]

        dr = dr_ref[...]
        dh = _dot(dr, wr_ref[...])
        for g in range(npc):
            dh = dh + _dot(pcs[g][...], w_group(g))
        dh = dh + _dot(dm_ref[:, :D], w_group(npc))
        dh = dh + _dot(dm_ref[:, D:], w_group(npc + 1))
        xv = x_ref[...]
        r = lax.rsqrt(jnp.mean(xv * xv, axis=-1, keepdims=True) + EPS)
        xhat = xv * r
        g = g_ref[...]
        dg_ref[...] += jnp.sum(dh * xhat, axis=0, keepdims=True)
        dxhat = dh * g
        gx_ref[...] = r * (dxhat - xhat * jnp.mean(dxhat * xhat, axis=-1, keepdims=True)) + dx2_ref[...]
        dwr_ref[...] += _dot_tn(dr, _bf(xhat * g))

    row = lambda i: (i, 0)
    const = lambda i: (0, 0)
    tile = pl.BlockSpec((tm, D), row)
    return pl.pallas_call(
        body, name="dh",
        grid=(T // tm,),
        in_specs=[tile] * npc + [
            pl.BlockSpec((tm, 2 * D), row),
            pl.BlockSpec((tm, 128), row),
            pl.BlockSpec(memory_space=pl.ANY),
            pl.BlockSpec((128, D), const),
            tile, tile,
            pl.BlockSpec((1, D), const)],
        out_specs=(tile, pl.BlockSpec((1, D), const), pl.BlockSpec((128, D), const)),
        out_shape=(jax.ShapeDtypeStruct((T, D), F32),
                   jax.ShapeDtypeStruct((1, D), F32),
                   jax.ShapeDtypeStruct((128, D), F32)),
        scratch_shapes=[pltpu.VMEM((n_main, D), BF16), pltpu.SemaphoreType.DMA((2,))],
        compiler_params=_cparams(("arbitrary",)),
    )(*pieces, dmlog, drank, wt, wr, x, dx2, norm_g)


def _wgrad_call(lhs_list, lhs_of_group, rhs_list, rhs_of_group, n_transposed, name, gathered=None):
    n_groups = len(rhs_of_group)
    n_tb, D, tb = lhs_list[0].shape
    T = n_tb * tb
    per = min(2, n_tb)
    tk = per * tb
    nk = T // tk
    nl = len(lhs_list)
    carry = gathered is not None

    def body(*refs):
        lhs = refs[:nl]
        rhs = refs[nl:nl + n_groups]
        g = pl.program_id(0)
        i = pl.program_id(1)
        if carry:
            src_ref, out_ref, dst_ref, acc, send_sems, recv_sems, loc_sem = refs[nl + n_groups:]
            own, pairs = _push_copies(src_ref, dst_ref, send_sems, recv_sems, loc_sem, scatter=False)

            @pl.when((g == 0) & (i == 0))
            def _():
                _push_start(own, pairs)
        else:
            out_ref, acc = refs[nl + n_groups:]

        @pl.when(i == 0)
        def _():
            acc[...] = jnp.zeros_like(acc)

        for p in range(n_groups):
            @pl.when(g == p)
            def _(p=p):
                lref = lhs[lhs_of_group[p]]
                part = _dot(lref[0], rhs[p][0:tb, :])
                for b in range(1, per):
                    part = part + _dot(lref[b], rhs[p][b * tb:(b + 1) * tb, :])
                acc[...] += part

        @pl.when((i == nk - 1) & (g < n_transposed))
        def _():
            out_ref[...] = _bf(acc[...].T)

        @pl.when((i == nk - 1) & (g >= n_transposed))
        def _():
            out_ref[...] = _bf(acc[...])

        if carry:
            @pl.when((g == n_groups - 1) & (i == nk - 1))
            def _():
                _push_wait(own, pairs)

    def lhs_spec(a):
        groups = [g for g in range(n_groups) if lhs_of_group[g] == a]
        lo, hi = min(groups), max(groups)
        assert groups == list(range(lo, hi + 1))
        return pl.BlockSpec((per, D, tb), lambda g, i: (jnp.where((g >= lo) & (g <= hi), i, 0), 0, 0))

    def rhs_spec(p):
        cb = rhs_of_group[p][1]
        return pl.BlockSpec((tk, 1024), lambda g, i: (jnp.where(g == p, i, 0), cb))

    in_specs = [lhs_spec(a) for a in range(nl)] + [rhs_spec(p) for p in range(n_groups)]
    out_specs = pl.BlockSpec((None, D, 1024), lambda g, i: (g, 0, 0))
    out_shape = jax.ShapeDtypeStruct((n_groups, D, 1024), BF16)
    scratch = [pltpu.VMEM((D, 1024), F32)]
    operands = list(lhs_list) + [rhs_list[rhs_of_group[p][0]] for p in range(n_groups)]
    if carry:
        in_specs.append(_ANY)
        out_specs = (out_specs, _ANY)
        out_shape = (out_shape, jax.ShapeDtypeStruct((N_DEV,) + gathered.shape, gathered.dtype))
        scratch += _PUSH_SEMS
        operands.append(gathered)
    return pl.pallas_call(
        body, name=name,
        grid=(n_groups, nk),
        in_specs=in_specs, out_specs=out_specs, out_shape=out_shape,
        scratch_shapes=scratch,
        compiler_params=_cparams(("arbitrary", "arbitrary")),
    )(*operands)


def _adamw_call(parts, w, m, v, name):
    R, C = w.shape
    n_parts = parts.shape[0]
    (tr, tc), grid, idx = _tiling_2d(R, C)

    def body(p_ref, w_ref, m_ref, v_ref, g_ref, d_ref, nm_ref, nv_ref):
        g = p_ref[n_parts - 1].astype(F32)
        for k in range(n_parts - 1):
            g = g + p_ref[k].astype(F32)
        mm = ADAM_B1 * m_ref[...] + (1.0 - ADAM_B1) * g
        vv = ADAM_B2 * v_ref[...] + (1.0 - ADAM_B2) * (g * g)
        m_hat = mm / (1.0 - ADAM_B1 ** ADAM_STEP)
        v_hat = vv / (1.0 - ADAM_B2 ** ADAM_STEP)
        d_ref[...] = -ADAM_LR * (m_hat / (jnp.sqrt(v_hat) + ADAM_EPS) + ADAM_WD * w_ref[...])
        g_ref[...] = g
        nm_ref[...] = mm
        nv_ref[...] = vv

    blk = pl.BlockSpec((tr, tc), idx)
    sds = jax.ShapeDtypeStruct((R, C), F32)
    return pl.pallas_call(
        body, name=name,
        grid=grid,
        in_specs=[pl.BlockSpec((n_parts, tr, tc), lambda i: (0,) + idx(i)), blk, blk, blk],
        out_specs=(blk, blk, blk, blk),
        out_shape=(sds, sds, sds, sds),
        compiler_params=_cparams(("arbitrary",)),
    )(parts, w, m, v)


def _local_step(x, target, wt, wr, wdec, bdec, wp_shard, norm_g, gla_g, b_gate, final_g):
    D = x.shape[1]
    projf, projb, rank, ht = _proj_call(x, norm_g, wt, wr)
    o_gla, st_all, la = _gla_fwd_call(projf, projb, rank, wdec, bdec)
    o_sb, wp_all = _sb_fwd_call(projb, wp_shard)
    wp_full = wp_all.transpose(1, 0, 2, 3).reshape(3, D, D)
    (dx2, do_gla, do_sb, dggate, dsgate, dmlog, mt, ogt, obt, dx2b, dya, dyb,
     dfinal_g, db_gate, dgla_g, loss) = _mid_call(o_gla, o_sb, projf, x, target, wp_full[0], wp_full[1],
                                                 wp_full[2], gla_g, b_gate, final_g)
    dw_p = _wgrad_call([ogt, obt, mt], [0, 1, 2], [dya, dyb, dx2b], [(0, 0), (1, 0), (2, 0)], 0, "wgrad_p")
    g_p = dw_p.reshape(3, N_DEV, D // N_DEV, D).transpose(1, 0, 2, 3).reshape(N_DEV, 3 * (D // N_DEV), D)
    dqk, dgv, drank, dwdec, dbdec = _gla_bwd_call(projf, projb, la, do_gla, st_all, rank, wdec)
    dsq, dsk, dsv, r_p = _sb_bwd_call(projb, do_sb, g_p)
    pieces = [dqk, dgv, dggate, dsq, dsk, dsv, dsgate]
    grad_x, dnorm_g, dwr = _dh_call(pieces, dmlog, drank, wt, wr, x, dx2, norm_g)
    small = jnp.concatenate([
        dnorm_g.reshape(-1), dbdec.reshape(-1), dgla_g.reshape(-1), db_gate.reshape(-1), dfinal_g.reshape(-1),
        loss.reshape(-1), dwdec[:GLA_RANK].reshape(-1)]).reshape(1, _SM_LEN)
    rhs_of_group = [(g, 0) for g in range(7)] + [(7, 0), (7, 1)]
    dw_in, r_small = _wgrad_call([ht], [0] * N_GROUPS, pieces + [dmlog], rhs_of_group, N_GROUPS, "wgrad_in",
                                 gathered=small)
    return grad_x, dw_in, dwr, r_p, r_small


_SM_NORM = 0
_SM_BDEC = _SM_NORM + D_MODEL
_SM_GLAG = _SM_BDEC + GLA_DK
_SM_BGATE = _SM_GLAG + GLA_HV
_SM_FINAL = _SM_BGATE + 2 * D_MODEL
_SM_REPL = _SM_FINAL + D_MODEL
_SM_LOSS = _SM_REPL
_SM_WDEC = _SM_LOSS + 128
_SM_LEN = _SM_WDEC + GLA_RANK * GLA_DK


def kernel(x, norm_g, w_in, w_dec_up, b_dec, gla_norm_g, w_pa, w_pb, b_gate, w_o, final_g, loss_target, m_norm_g, m_w_in, m_w_dec_up, m_b_dec, m_gla_norm_g, m_w_pa, m_w_pb, m_b_gate, m_w_o, m_final_g, v_norm_g, v_w_in, v_w_dec_up, v_b_dec, v_gla_norm_g, v_w_pa, v_w_pb, v_b_gate, v_w_o, v_final_g):
    D = D_MODEL
    me = 4 * lax.axis_index("x") + 2 * lax.axis_index("y") + lax.axis_index("c")

    wp_shard = jnp.stack([w_pa, w_pb, w_o]).astype(BF16)
    win_all, wdec_all = _all_gather([w_in.T.astype(BF16), w_dec_up], "gather_w")
    wt = win_all.reshape(IN_COLS, D)
    wr = jnp.pad(wt[RANK_COL:RANK_COL + GLA_RANK], ((0, 128 - GLA_RANK), (0, 0)))
    wdec_full = wdec_all.transpose(1, 0, 2).reshape(GLA_RANK, GLA_DK)
    wdec = jnp.pad(wdec_full, ((0, 128 - GLA_RANK), (0, 0)))

    grad_x, dw_in, dwr, r_p, r_small = _local_step(
        x[0], loss_target[0], wt, wr, wdec, b_dec.reshape(1, -1), wp_shard,
        norm_g.reshape(1, -1), gla_norm_g.reshape(1, -1), b_gate.reshape(1, -1), final_g.reshape(1, -1))

    dmain = dw_in.reshape(N_GROUPS * 1024, D)
    drank = dwr[:GLA_RANK].astype(BF16)

    def part_for(p):
        lo, hi = p * SHARD_COLS, (p + 1) * SHARD_COLS
        pieces = []
        if lo < RANK_COL:
            pieces.append(dmain[lo:min(hi, RANK_COL)])
        if lo < RANK_COL + GLA_RANK and hi > RANK_COL:
            pieces.append(drank[max(lo, RANK_COL) - RANK_COL:min(hi, RANK_COL + GLA_RANK) - RANK_COL])
        if hi > RANK_COL + GLA_RANK:
            pieces.append(dmain[max(lo, RANK_COL + GLA_RANK) - GLA_RANK:hi - GLA_RANK])
        return pieces[0] if len(pieces) == 1 else jnp.concatenate(pieces, axis=0)

    g_in = jnp.stack([part_for(p) for p in range(N_DEV)])
    c_idx = lax.axis_index("c").astype(jnp.int32).reshape(1)
    (p_in,) = _pair_exchange([g_in], "pair_g")
    s_in = _pair_add_call(g_in, p_in, c_idx, "pair_add_in")
    (r_in,) = _chip_exchange([s_in], "scatter_g")

    gw_in, d_in, nm_in, nv_in = (a.T for a in _adamw_call(r_in, w_in.T, m_w_in.T, v_w_in.T, "adamw_in"))
    wp_f32 = jnp.concatenate([w_pa, w_pb, w_o], axis=0)
    mp = jnp.concatenate([m_w_pa, m_w_pb, m_w_o], axis=0)
    vp = jnp.concatenate([v_w_pa, v_w_pb, v_w_o], axis=0)
    gp, dp, nmp, nvp = _adamw_call(r_p, wp_f32, mp, vp, "adamw_p")
    rows = D // N_DEV

    def split3(a):
        return a[:rows], a[rows:2 * rows], a[2 * rows:]

    g_pa, g_pb, g_o = split3(gp)
    d_pa, d_pb, d_o = split3(dp)
    nm_pa, nm_pb, nm_o = split3(nmp)
    nv_pa, nv_pb, nv_o = split3(nvp)

    w_rep = jnp.concatenate([norm_g, b_dec, gla_norm_g, b_gate, final_g]).reshape(1, _SM_REPL)
    m_rep = jnp.concatenate([m_norm_g, m_b_dec, m_gla_norm_g, m_b_gate, m_final_g]).reshape(1, _SM_REPL)
    v_rep = jnp.concatenate([v_norm_g, v_b_dec, v_gla_norm_g, v_b_gate, v_final_g]).reshape(1, _SM_REPL)
    g_rep, d_rep, nm_rep, nv_rep = _adamw_call(r_small[:, :, :_SM_REPL], w_rep, m_rep, v_rep, "adamw_rep")

    def split_rep(a):
        a = a.reshape(-1)
        return (a[_SM_NORM:_SM_BDEC], a[_SM_BDEC:_SM_GLAG], a[_SM_GLAG:_SM_BGATE],
                a[_SM_BGATE:_SM_FINAL], a[_SM_FINAL:_SM_REPL])

    g_norm, g_bdec, g_glag, g_bgate, g_final = split_rep(g_rep)
    d_norm, d_bdec, d_glag, d_bgate, d_final = split_rep(d_rep)
    nm_norm, nm_bdec, nm_glag, nm_bgate, nm_final = split_rep(nm_rep)
    nv_norm, nv_bdec, nv_glag, nv_bgate, nv_final = split_rep(nv_rep)

    wdec_parts = r_small[:, 0, _SM_WDEC:].reshape(N_DEV, GLA_RANK, GLA_DK)
    cols = GLA_DK // N_DEV
    wdec_mine = lax.dynamic_slice_in_dim(wdec_parts, me * cols, cols, axis=2)
    g_wdec, d_wdec, nm_wdec, nv_wdec = _adamw_call(wdec_mine, w_dec_up, m_w_dec_up, v_w_dec_up, "adamw_dec")

    loss_total = jnp.sum(r_small[:, 0, _SM_LOSS])

    return (loss_total, grad_x[None],
            g_norm, gw_in, g_wdec, g_bdec, g_glag, g_pa, g_pb, g_bgate, g_o, g_final,
            d_norm, d_in, d_wdec, d_bdec, d_glag, d_pa, d_pb, d_bgate, d_o, d_final,
            nm_norm, nm_in, nm_wdec, nm_bdec, nm_glag, nm_pa, nm_pb, nm_bgate, nm_o, nm_final,
            nv_norm, nv_in, nv_wdec, nv_bdec, nv_glag, nv_pa, nv_pb, nv_bgate, nv_o, nv_final)
```

```python
import functools
import math

import jax
import jax.numpy as jnp
from jax import lax
from jax.experimental import pallas as pl
from jax.experimental.pallas import tpu as pltpu

F32 = jnp.float32
BF16 = jnp.bfloat16

N_DEV = 8
D_MODEL = 1024
GLA_HEADS = 4
GLA_HK = 128
GLA_HV = 256
GLA_DK = 512
GLA_RANK = 16
GLA_TAU = 16.0
GLA_CHUNK = 64
SB_HEADS = 8
SB_HD = 128
SB_BLOCK = 128
EPS = 1e-6
N_GROUPS = 9
RANK_COL = 3072
IN_COLS = 9232
SHARD_COLS = IN_COLS // N_DEV

ADAM_LR = 0.001
ADAM_B1 = 0.9
ADAM_B2 = 0.999
ADAM_EPS = 1e-08
ADAM_WD = 0.01
ADAM_STEP = 10

VMEM_LIMIT = 56 * 1024 * 1024
TBLK = 256


def _cparams(sem=None):
    return pltpu.CompilerParams(dimension_semantics=sem, vmem_limit_bytes=VMEM_LIMIT)


def _tiling_2d(rows, cols):
    if rows * cols <= 128 * 1024:
        return (rows, cols), (1,), lambda i: (0, 0)
    if rows % 128 == 0:
        return (128, cols), (rows // 128,), lambda i: (i, 0)
    tc = 256 if cols % 256 == 0 else cols
    return (rows, tc), (cols // tc,), lambda i: (0, i)


def _dot(a, b):
    return jnp.dot(a, b, preferred_element_type=F32)


def _dot_nt(a, b):
    return lax.dot_general(a, b, (((1,), (1,)), ((), ())), preferred_element_type=F32)


def _dot_tn(a, b):
    return lax.dot_general(a, b, (((0,), (0,)), ((), ())), preferred_element_type=F32)


def _bf(x):
    return x.astype(BF16)


def _split3(x):
    hi = x.astype(BF16)
    r = x - hi.astype(F32)
    mid = r.astype(BF16)
    lo = (r - mid.astype(F32)).astype(BF16)
    return hi, mid, lo


def _tri_left(tri, x):
    hi, mid, lo = _split3(x)
    return _dot(tri, hi) + _dot(tri, mid) + _dot(tri, lo)


def _split2(x):
    hi = lax.bitcast_convert_type(lax.bitcast_convert_type(x, jnp.uint32) & jnp.uint32(0xFFFF0000), F32)
    return hi.astype(BF16), (x - hi).astype(BF16)


def _tri2_left(tri, x):
    hi, lo = _split2(x)
    return _dot(tri, hi) + _dot(tri, lo)


def _tri2_right(x, tri):
    hi, lo = _split2(x)
    return _dot(hi, tri) + _dot(lo, tri)


def _iota2(n, m, dim):
    return lax.broadcasted_iota(jnp.int32, (n, m), dim)


def _sigmoid(x):
    return 1.0 / (1.0 + jnp.exp(-x))


def _softplus_neg_abs(z):
    return jnp.log(1.0 + jnp.exp(-jnp.abs(z)))


_ANY = pl.BlockSpec(memory_space=pl.ANY)


def _mesh_pos():
    return lax.axis_index("x"), lax.axis_index("y"), lax.axis_index("c")


def _other_chips(x, y):
    return [(1 - x, y), (x, 1 - y), (1 - x, 1 - y)]


def _rcopy(src, dst, send_sem, recv_sem, to):
    return pltpu.make_async_remote_copy(src_ref=src, dst_ref=dst, send_sem=send_sem, recv_sem=recv_sem,
                                        device_id=to, device_id_type=pl.DeviceIdType.MESH)


def _push_copies(src_ref, dst_ref, send_sems, recv_sems, loc_sem, scatter):
    x, y, c = _mesh_pos()
    me = 4 * x + 2 * y + c
    own = pltpu.make_async_copy(src_ref.at[me] if scatter else src_ref, dst_ref.at[me], loc_sem)
    pairs = []
    for k in range(1, N_DEV):
        px = 1 - x if k & 4 else x
        py = 1 - y if k & 2 else y
        pc = 1 - c if k & 1 else c
        pid = 4 * px + 2 * py + pc
        src = src_ref.at[pid] if scatter else src_ref
        send = _rcopy(src, dst_ref.at[me], send_sems.at[k - 1], recv_sems.at[k - 1], (px, py, pc))
        recv = _rcopy(src, dst_ref.at[pid], send_sems.at[k - 1], recv_sems.at[k - 1], (px, py, pc))
        pairs.append((send, recv))
    return own, pairs


def _push_start(own, pairs):
    own.start()
    for send, _ in pairs:
        send.start()


def _push_wait(own, pairs):
    for _, recv in pairs:
        recv.wait_recv()
    for send, _ in pairs:
        send.wait_send()
    own.wait()


_PUSH_SEMS = [pltpu.SemaphoreType.DMA((N_DEV - 1,)), pltpu.SemaphoreType.DMA((N_DEV - 1,)),
              pltpu.SemaphoreType.DMA]


def _chip_copies(src_ref, dst_ref, send_sems, recv_sems, loc_sem):
    x, y, c = _mesh_pos()
    own = pltpu.make_async_copy(src_ref.at[2 * x + y], dst_ref.at[3], loc_sem)
    pairs = []
    for j, (px, py) in enumerate(_other_chips(x, y)):
        cp = _rcopy(src_ref.at[2 * px + py], dst_ref.at[j], send_sems.at[j], recv_sems.at[j], (px, py, c))
        pairs.append((cp, cp))
    return own, pairs


_CHIP_SEMS = [pltpu.SemaphoreType.DMA((3,)), pltpu.SemaphoreType.DMA((3,)), pltpu.SemaphoreType.DMA]


def _all_gather(arrs, name):
    n = len(arrs)

    def body(*refs):
        ins = refs[:n]
        outs = refs[n:2 * n]
        send_sems, recv_sems, loc_sems = refs[2 * n:]
        x, y, c = _mesh_pos()
        sib = (x, y, 1 - c)
        chips = _other_chips(x, y)

        def place(a, px, py, pc):
            return outs[a].at[4 * px + 2 * py + pc]

        def copy(a, k, block, to, src=None):
            dst = place(a, *block)
            return _rcopy(dst if src is None else src, dst, send_sems.at[a, k], recv_sems.at[a, k], to)

        mine = [pltpu.make_async_copy(ins[a], place(a, x, y, c), loc_sems.at[a]) for a in range(n)]
        for cp in mine:
            cp.start()
        first = [copy(a, 0, (x, y, c), sib, src=ins[a]) for a in range(n)]
        for j, chip in enumerate(chips):
            first += [copy(a, 1 + j, (x, y, c), (*chip, c), src=ins[a]) for a in range(n)]
        for cp in first:
            cp.start()
        passed = []
        for j, chip in enumerate(chips):
            for a in range(n):
                copy(a, 1 + j, (*chip, c), (x, y, c)).wait_recv()
                fwd = copy(a, 4 + j, (*chip, c), sib)
                fwd.start()
                passed.append(fwd)
        for a in range(n):
            copy(a, 0, sib, (x, y, c)).wait_recv()
        for j, chip in enumerate(chips):
            for a in range(n):
                copy(a, 4 + j, (*chip, 1 - c), (x, y, c)).wait_recv()
        for cp in first + passed:
            cp.wait_send()
        for cp in mine:
            cp.wait()

    return pl.pallas_call(
        body, name=name,
        out_shape=tuple(jax.ShapeDtypeStruct((N_DEV,) + a.shape, a.dtype) for a in arrs),
        in_specs=[_ANY] * n,
        out_specs=tuple([_ANY] * n),
        scratch_shapes=[pltpu.SemaphoreType.DMA((n, 7)), pltpu.SemaphoreType.DMA((n, 7)),
                        pltpu.SemaphoreType.DMA((n,))],
    )(*arrs)


def _pair_exchange(arrs, name):
    n = len(arrs)

    def body(*refs):
        ins = refs[:n]
        outs = refs[n:2 * n]
        send_sems, recv_sems = refs[2 * n:]
        x, y, c = _mesh_pos()
        copies = []
        for a in range(n):
            for q in range(4):
                cp = _rcopy(ins[a].at[2 * q + (1 - c)], outs[a].at[q], send_sems.at[a, q], recv_sems.at[a, q],
                            (x, y, 1 - c))
                cp.start()
                copies.append(cp)
        for cp in copies:
            cp.wait_recv()
        for cp in copies:
            cp.wait_send()

    return pl.pallas_call(
        body, name=name,
        out_shape=tuple(jax.ShapeDtypeStruct((4,) + a.shape[1:], a.dtype) for a in arrs),
        in_specs=[_ANY] * n,
        out_specs=tuple([_ANY] * n),
        scratch_shapes=[pltpu.SemaphoreType.DMA((n, 4)), pltpu.SemaphoreType.DMA((n, 4))],
    )(*arrs)


def _pair_add_call(parts, recv, c_idx, name):
    _, R, C = parts.shape
    (tr, tc), (steps,), idx = _tiling_2d(R, C)

    def body(c_ref, p_ref, r_ref, o_ref):
        o_ref[...] = (p_ref[...].astype(F32) + r_ref[...].astype(F32)).astype(o_ref.dtype)

    return pl.pallas_call(
        body, name=name,
        grid_spec=pltpu.PrefetchScalarGridSpec(
            num_scalar_prefetch=1,
            grid=(4, steps),
            in_specs=[pl.BlockSpec((None, tr, tc), lambda q, i, c_ref: (2 * q + c_ref[0],) + idx(i)),
                      pl.BlockSpec((None, tr, tc), lambda q, i, c_ref: (q,) + idx(i))],
            out_specs=pl.BlockSpec((None, tr, tc), lambda q, i, c_ref: (q,) + idx(i))),
        out_shape=jax.ShapeDtypeStruct((4, R, C), parts.dtype),
        compiler_params=_cparams(("arbitrary", "arbitrary")),
    )(c_idx, parts, recv)


def _group_row(g):
    return GLA_RANK * (g * (1024 // GLA_RANK) + (g >= RANK_COL // 1024))


def _proj_call(x, norm_g, wt, wr):
    T, D = x.shape
    tm = min(1024, T)
    assert tm % TBLK == 0
    n_i = T // tm

    def f_slot(j):
        return ((j >= 2).astype(jnp.int32) + (j >= 6).astype(jnp.int32)
                + (j >= 7).astype(jnp.int32) + (j >= 8).astype(jnp.int32))

    def b_slot(j):
        return (j >= 3).astype(jnp.int32) + (j >= 4).astype(jnp.int32) + (j >= 5).astype(jnp.int32)

    def body(x_ref, g_ref, w_ref, wr_ref, pf_ref, pb_ref, rank_ref, ht_ref, h_scr):
        j = pl.program_id(1)

        @pl.when(j == 0)
        def _():
            xv = x_ref[...]
            r = lax.rsqrt(jnp.mean(xv * xv, axis=-1, keepdims=True) + EPS)
            h = (xv * r) * g_ref[...]
            hb = _bf(h)
            h_scr[...] = hb
            for b in range(tm // TBLK):
                ht_ref[b] = _bf(h[b * TBLK:(b + 1) * TBLK].T)
            rank_ref[...] = _dot_nt(hb, wr_ref[...])

        is_b = (j == 1) | ((j >= 3) & (j <= 5))

        @pl.when(is_b)
        def _():
            pb_ref[...] = _bf(_dot_nt(h_scr[...], w_ref[...]))

        @pl.when(jnp.logical_not(is_b))
        def _():
            pf_ref[...] = _dot_nt(h_scr[...], w_ref[...])

    return pl.pallas_call(
        body, name="proj",
        grid=(n_i, N_GROUPS),
        in_specs=[pl.BlockSpec((tm, D), lambda i, j: (i, 0)),
                  pl.BlockSpec((1, D), lambda i, j: (0, 0)),
                  pl.BlockSpec((pl.Element(1024), pl.Element(D)), lambda i, j: (_group_row(j), 0)),
                  pl.BlockSpec((128, D), lambda i, j: (0, 0))],
        out_specs=(pl.BlockSpec((None, tm, 1024), lambda i, j: (f_slot(j), i, 0)),
                   pl.BlockSpec((None, tm, 1024), lambda i, j: (b_slot(j), i, 0)),
                   pl.BlockSpec((tm, 128), lambda i, j: (i, 0)),
                   pl.BlockSpec((tm // TBLK, D, TBLK), lambda i, j: (i, 0, 0))),
        out_shape=(jax.ShapeDtypeStruct((5, T, 1024), F32),
                   jax.ShapeDtypeStruct((4, T, 1024), BF16),
                   jax.ShapeDtypeStruct((T, 128), F32),
                   jax.ShapeDtypeStruct((T // TBLK, D, TBLK), BF16)),
        scratch_shapes=[pltpu.VMEM((tm, D), BF16)],
        compiler_params=_cparams(("arbitrary", "arbitrary")),
    )(x, norm_g, wt, wr)


def _gla_chunk_terms(la_h, q, k):
    C = GLA_CHUNK
    low = _bf((_iota2(C, C, 0) >= _iota2(C, C, 1)).astype(F32))
    b = _tri_left(low, la_h)
    bl = b[C - 1:C, :]
    eb = jnp.exp(b)
    enb = jnp.exp(-b)
    ebl_b = jnp.exp(bl - b)
    scale = GLA_HK ** -0.5
    qe = q * eb * scale
    ke = k * enb
    kd = k * ebl_b
    return b, bl, eb, enb, ebl_b, qe, ke, kd


def _gla_fwd_call(projf, projb, rank, wdec, bdec):
    T = projf.shape[1]
    C = GLA_CHUNK
    n_chunks = T // C

    def body(qk_ref, v_ref, rank_ref, wd_ref, bd_ref, o_ref, st_ref, la_ref, st_scr):
        @pl.when(pl.program_id(0) == 0)
        def _():
            st_scr[...] = jnp.zeros_like(st_scr)

        dec = _dot(_bf(rank_ref[...]), _bf(wd_ref[...])) + bd_ref[...]
        la = (jnp.minimum(dec, 0.0) - _softplus_neg_abs(dec)) / GLA_TAU
        la_ref[...] = la
        mask = _iota2(C, C, 0) >= _iota2(C, C, 1)
        _, bl, _, _, _, qe, ke, kd = _gla_chunk_terms(la, qk_ref[:, :GLA_DK], qk_ref[:, GLA_DK:])
        qeb, keb, kdb = _bf(qe), _bf(ke), _bf(kd)
        ebl = jnp.exp(bl)
        heads = range(GLA_HEADS)
        ks = [slice(hh * GLA_HK, (hh + 1) * GLA_HK) for hh in heads]
        vs = [slice(hh * GLA_HV, (hh + 1) * GLA_HV) for hh in heads]
        st = [st_scr[hh] for hh in heads]
        p = [_bf(jnp.where(mask, _dot_nt(qeb[:, ks[hh]], keb[:, ks[hh]]), 0.0)) for hh in heads]
        inter = [_dot_nt(qeb[:, ks[hh]], _bf(st[hh])) for hh in heads]
        upd = [_dot_tn(v_ref[:, vs[hh]], kdb[:, ks[hh]]) for hh in heads]
        intra = [_dot(p[hh], v_ref[:, vs[hh]]) for hh in heads]
        for hh in heads:
            st_ref[hh] = st[hh]
            o_ref[:, vs[hh]] = intra[hh] + inter[hh]
            st_scr[hh] = st[hh] * ebl[:, ks[hh]] + upd[hh]

    return pl.pallas_call(
        body, name="gla_fwd",
        grid=(n_chunks,),
        in_specs=[pl.BlockSpec((None, C, 1024), lambda n: (0, n, 0)),
                  pl.BlockSpec((None, C, 1024), lambda n: (0, n, 0)),
                  pl.BlockSpec((C, 128), lambda n: (n, 0)),
                  pl.BlockSpec((128, GLA_DK), lambda n: (0, 0)),
                  pl.BlockSpec((1, GLA_DK), lambda n: (0, 0))],
        out_specs=(pl.BlockSpec((C, 1024), lambda n: (n, 0)),
                   pl.BlockSpec((None, GLA_HEADS, GLA_HV, GLA_HK), lambda n: (n, 0, 0, 0)),
                   pl.BlockSpec((C, GLA_DK), lambda n: (n, 0))),
        out_shape=(jax.ShapeDtypeStruct((T, 1024), F32),
                   jax.ShapeDtypeStruct((n_chunks, GLA_HEADS, GLA_HV, GLA_HK), F32),
                   jax.ShapeDtypeStruct((T, GLA_DK), F32)),
        scratch_shapes=[pltpu.VMEM((GLA_HEADS, GLA_HV, GLA_HK), F32)],
        compiler_params=_cparams(("arbitrary",)),
    )(projf, projb, rank, wdec, bdec)


def _gla_bwd_call(projf, projb, la, do_gla, st_all, rank, wdec):
    T = projf.shape[1]
    C = GLA_CHUNK
    n_chunks = T // C
    last = n_chunks - 1

    def body(qk_ref, v_ref, la_ref, do_ref, st_ref, rank_ref, wd_ref,
             dqk_ref, dv_ref, drank_ref, dwd_ref, dbd_ref, dst_scr):
        @pl.when(pl.program_id(0) == 0)
        def _():
            dst_scr[...] = jnp.zeros_like(dst_scr)
            dwd_ref[...] = jnp.zeros_like(dwd_ref)
            dbd_ref[...] = jnp.zeros_like(dbd_ref)

        mask = _iota2(C, C, 0) >= _iota2(C, C, 1)
        upp = _bf((_iota2(C, C, 0) <= _iota2(C, C, 1)).astype(F32))
        scale = GLA_HK ** -0.5
        la = la_ref[...]
        _, bl, eb, enb, ebl_b, qe, ke, kd = _gla_chunk_terms(la, qk_ref[:, :GLA_DK], qk_ref[:, GLA_DK:])
        qeb, keb, kdb = _bf(qe), _bf(ke), _bf(kd)
        ebl = jnp.exp(bl)
        heads = range(GLA_HEADS)
        ks = [slice(hh * GLA_HK, (hh + 1) * GLA_HK) for hh in heads]
        vs = [slice(hh * GLA_HV, (hh + 1) * GLA_HV) for hh in heads]
        v = [v_ref[:, vs[hh]] for hh in heads]
        do = [_bf(do_ref[:, vs[hh]]) for hh in heads]
        st = [st_ref[hh] for hh in heads]
        dstn = [dst_scr[hh] for hh in heads]
        dstnb = [_bf(dstn[hh]) for hh in heads]
        p = [_bf(jnp.where(mask, _dot_nt(qeb[:, ks[hh]], keb[:, ks[hh]]), 0.0)) for hh in heads]
        dp = [_bf(jnp.where(mask, _dot_nt(do[hh], v[hh]), 0.0)) for hh in heads]
        dkd = [_dot(v[hh], dstnb[hh]) for hh in heads]
        dv_inter = [_dot_nt(kdb[:, ks[hh]], dstnb[hh]) for hh in heads]
        dqe_inter = [_dot(do[hh], _bf(st[hh])) for hh in heads]
        dst_new = [_dot_tn(do[hh], qeb[:, ks[hh]]) + dstn[hh] * ebl[:, ks[hh]] for hh in heads]
        debl = jnp.concatenate([jnp.sum(dstn[hh] * st[hh], axis=0, keepdims=True) for hh in heads], axis=1)
        dv = [_dot_tn(p[hh], do[hh]) + dv_inter[hh] for hh in heads]
        dqe = jnp.concatenate([_dot(dp[hh], keb[:, ks[hh]]) + dqe_inter[hh] for hh in heads], axis=1)
        dke = jnp.concatenate([_dot_tn(dp[hh], qeb[:, ks[hh]]) for hh in heads], axis=1)
        dkd = jnp.concatenate(dkd, axis=1)
        for hh in heads:
            dst_scr[hh] = dst_new[hh]
            dv_ref[:, vs[hh]] = _bf(dv[hh])
        dkd_kd = dkd * kd
        db = dqe * qe - dke * ke - dkd_kd
        dbl = jnp.sum(dkd_kd, axis=0, keepdims=True) + ebl * debl
        dla = _tri_left(upp, db) + dbl
        dqk_ref[:, :GLA_DK] = _bf(dqe * eb * scale)
        dqk_ref[:, GLA_DK:] = _bf(dke * enb + dkd * ebl_b)
        ddec = dla * (1.0 / GLA_TAU) * (1.0 - jnp.exp(GLA_TAU * la))
        ddecb = _bf(ddec)
        drank_ref[...] = _bf(_dot_nt(ddecb, _bf(wd_ref[...])))
        dwd_ref[...] += _dot_tn(_bf(rank_ref[...]), ddecb)
        dbd_ref[...] += jnp.sum(ddec, axis=0, keepdims=True)

    return pl.pallas_call(
        body, name="gla_bwd",
        grid=(n_chunks,),
        in_specs=[pl.BlockSpec((None, C, 1024), lambda n: (0, last - n, 0)),
                  pl.BlockSpec((None, C, 1024), lambda n: (0, last - n, 0)),
                  pl.BlockSpec((C, GLA_DK), lambda n: (last - n, 0)),
                  pl.BlockSpec((C, 1024), lambda n: (last - n, 0)),
                  pl.BlockSpec((None, GLA_HEADS, GLA_HV, GLA_HK), lambda n: (last - n, 0, 0, 0)),
                  pl.BlockSpec((C, 128), lambda n: (last - n, 0)),
                  pl.BlockSpec((128, GLA_DK), lambda n: (0, 0))],
        out_specs=(pl.BlockSpec((C, 1024), lambda n: (last - n, 0)),
                   pl.BlockSpec((C, 1024), lambda n: (last - n, 0)),
                   pl.BlockSpec((C, 128), lambda n: (last - n, 0)),
                   pl.BlockSpec((128, GLA_DK), lambda n: (0, 0)),
                   pl.BlockSpec((1, GLA_DK), lambda n: (0, 0))),
        out_shape=(jax.ShapeDtypeStruct((T, 1024), BF16),
                   jax.ShapeDtypeStruct((T, 1024), BF16),
                   jax.ShapeDtypeStruct((T, 128), BF16),
                   jax.ShapeDtypeStruct((128, GLA_DK), F32),
                   jax.ShapeDtypeStruct((1, GLA_DK), F32)),
        scratch_shapes=[pltpu.VMEM((GLA_HEADS, GLA_HV, GLA_HK), F32)],
        compiler_params=_cparams(("arbitrary",)),
    )(projf, projb, la, do_gla, st_all, rank, wdec)


def _sb_logs(z):
    lsz = jnp.minimum(z, 0.0) - _softplus_neg_abs(z)
    return lsz, lsz - z


SB_HG_FWD = 8
SB_HG_BWD = 4
SB_KEYS = 256
SB_DEAD = -105.0


def _sb_fwd_call(projb, wp_shard):
    T = projb.shape[1]
    B = SB_BLOCK
    HG = SB_HG_FWD
    W = HG * SB_HD
    scale = 1.0 / math.sqrt(SB_HD)
    KB = min(SB_KEYS, T)
    n_h, n_i = SB_HEADS // HG, T // B

    def body(q_ref, k_ref, v_ref, wp_ref, o_ref, wpall_ref, cb_scr, send_sems, recv_sems, loc_sem):
        i = pl.program_id(1)
        own, pairs = _push_copies(wp_ref, wpall_ref, send_sems, recv_sems, loc_sem, scatter=False)

        @pl.when((pl.program_id(0) == 0) & (i == 0))
        def _():
            _push_start(own, pairs)

        rows = HG * B
        after = (_iota2(KB, KB, 0) > _iota2(KB, KB, 1)).astype(F32)
        tri = _bf(jnp.concatenate([after, jnp.ones((KB, KB), F32)], axis=1))
        o_ref[...] = jnp.zeros_like(o_ref)
        cb_scr[...] = jnp.zeros_like(cb_scr)

        def block(jp, masked):
            off = pl.multiple_of(jp * KB, KB)
            z = jnp.concatenate(
                [_dot_nt(q_ref[:, hh * SB_HD:(hh + 1) * SB_HD], k_ref[pl.ds(off, KB), hh * SB_HD:(hh + 1) * SB_HD])
                 for hh in range(HG)], axis=0) * scale
            lsz, l1m = _sb_logs(z)
            if masked:
                strict = (jp * KB + _iota2(rows, KB, 1)) < (i * B + (_iota2(rows, KB, 0) & (B - 1)))
                l1m = jnp.where(strict, l1m, 0.0)
            r = _tri2_right(l1m, tri)
            cb = cb_scr[...]
            a = jnp.exp(lsz + cb + r[:, :KB])
            if masked:
                a = jnp.where(strict, a, 0.0)
            cb_scr[...] = cb + r[:, KB:]
            ab = _bf(a)
            for hh in range(HG):
                cs = slice(hh * SB_HD, (hh + 1) * SB_HD)
                o_ref[:, cs] += _dot(ab[hh * B:(hh + 1) * B, :], v_ref[pl.ds(off, KB), cs])

        jp0 = (i * B) // KB
        block(jp0, True)

        def live(state):
            jj, dead = state
            return (jj <= jp0) & jnp.logical_not(dead)

        def step(state):
            jj, _ = state
            block(jp0 - jj, False)
            return jj + 1, jnp.max(cb_scr[:, :B]) < SB_DEAD

        lax.while_loop(live, step, (jnp.int32(1), jnp.max(cb_scr[:, :B]) < SB_DEAD))

        @pl.when((pl.program_id(0) == n_h - 1) & (i == n_i - 1))
        def _():
            _push_wait(own, pairs)

    return pl.pallas_call(
        body, name="sb_fwd",
        grid=(n_h, n_i),
        in_specs=[pl.BlockSpec((None, B, W), lambda h, i: (1, i, h)),
                  pl.BlockSpec((None, T, W), lambda h, i: (2, 0, h)),
                  pl.BlockSpec((None, T, W), lambda h, i: (3, 0, h)),
                  _ANY],
        out_specs=(pl.BlockSpec((B, W), lambda h, i: (i, h)), _ANY),
        out_shape=(jax.ShapeDtypeStruct((T, 1024), F32),
                   jax.ShapeDtypeStruct((N_DEV,) + wp_shard.shape, wp_shard.dtype)),
        scratch_shapes=[pltpu.VMEM((HG * B, KB), F32)] + _PUSH_SEMS,
        compiler_params=_cparams(("arbitrary", "arbitrary")),
    )(projb, projb, projb, wp_shard)


def _sb_bwd_call(projb, do_sb, g_p):
    T = projb.shape[1]
    B = SB_BLOCK
    nb = T // B
    HG = SB_HG_BWD
    W = HG * SB_HD
    KB = min(SB_KEYS, T)
    nkb = T // KB
    n_h = SB_HEADS // HG
    scale = 1.0 / math.sqrt(SB_HD)

    def body(q_ref, k_ref, v_ref, do_ref, gp_ref, dq_ref, dk_ref, dv_ref, rp_ref,
             dk_scr, dv_scr, kt_scr, beta_scr, g_scr, dqt_scr, send_sems, recv_sems, loc_sem):
        i = pl.program_id(1)
        own, pairs = _push_copies(gp_ref, rp_ref, send_sems, recv_sems, loc_sem, scatter=True)

        @pl.when((pl.program_id(0) == 0) & (i == 0))
        def _():
            _push_start(own, pairs)

        @pl.when(i == 0)
        def _():
            dk_scr[...] = jnp.zeros_like(dk_scr)
            dv_scr[...] = jnp.zeros_like(dv_scr)
            for hh in range(HG):
                for jb in range(nkb):
                    kt_scr[hh, jb] = _bf(
                        k_ref[jb * KB:(jb + 1) * KB, hh * SB_HD:(hh + 1) * SB_HD].astype(F32).T)

        dqt_scr[...] = jnp.zeros_like(dqt_scr)
        later = _bf((_iota2(KB, KB, 1) > _iota2(KB, KB, 0)).astype(F32))
        earlier = _bf((_iota2(KB, KB, 1) < _iota2(KB, KB, 0)).astype(F32))
        dob = _bf(do_ref[...])
        jp0 = (i * B) // KB

        def strict_mask():
            return (jp0 * KB + _iota2(KB, W, 0)) < (i * B + (_iota2(KB, W, 1) & (B - 1)))

        def heads(fn):
            return [fn(slice(hh * SB_HD, (hh + 1) * SB_HD)) for hh in range(HG)]

        def pass1(jp, cb, masked):
            off = pl.multiple_of(jp * KB, KB)
            z = jnp.concatenate(heads(lambda cs: _dot_nt(k_ref[pl.ds(off, KB), cs], q_ref[:, cs])), axis=1) * scale
            da = jnp.concatenate(heads(lambda cs: _dot_nt(v_ref[pl.ds(off, KB), cs], dob[:, cs])), axis=1)
            lsz, l1m = _sb_logs(z)
            if masked:
                strict = strict_mask()
                l1m = jnp.where(strict, l1m, 0.0)
            a = jnp.exp(lsz + cb + _tri2_left(later, l1m))
            if masked:
                a = jnp.where(strict, a, 0.0)
            g_scr[jp] = a * da
            beta_scr[jp] = jnp.exp(lsz)
            ab = _bf(a)
            for hh in range(HG):
                cs = slice(hh * SB_HD, (hh + 1) * SB_HD)
                dv_scr[pl.ds(off, KB), cs] += _dot(ab[:, cs], dob[:, cs])
            return cb + jnp.sum(l1m, axis=0, keepdims=True)

        zero = jnp.zeros((1, W), F32)
        cb = pass1(jp0, zero, True)

        def live(state):
            jj, _, dead = state
            return (jj <= jp0) & jnp.logical_not(dead)

        def step(state):
            jj, cr, _ = state
            cr = pass1(jp0 - jj, cr, False)
            return jj + 1, cr, jnp.max(cr) < SB_DEAD

        n_done, _, _ = lax.while_loop(live, step, (jnp.int32(1), cb, jnp.max(cb) < SB_DEAD))
        jp_first = jp0 - (n_done - 1)

        def pass2(jp, cg, masked):
            off = pl.multiple_of(jp * KB, KB)
            g = g_scr[jp]
            beta = beta_scr[jp]
            dz = g * (1.0 - beta) - beta * (cg + _tri2_left(earlier, g))
            if masked:
                dz = jnp.where(strict_mask(), dz, 0.0)
            dzb = _bf(dz * scale)
            for hh in range(HG):
                cs = slice(hh * SB_HD, (hh + 1) * SB_HD)
                dk_scr[pl.ds(off, KB), cs] += _dot(dzb[:, cs], q_ref[:, cs])
                dqt_scr[hh] += _dot(kt_scr[hh, jp], dzb[:, cs])
            return cg + jnp.sum(g, axis=0, keepdims=True)

        cg = lax.fori_loop(jp_first, jp0, lambda jp, cr: pass2(jp, cr, False), zero)
        pass2(jp0, cg, True)
        for hh in range(HG):
            dq_ref[:, hh * SB_HD:(hh + 1) * SB_HD] = _bf(dqt_scr[hh].T)

        @pl.when(i == nb - 1)
        def _():
            dk_ref[...] = _bf(dk_scr[...])
            dv_ref[...] = _bf(dv_scr[...])

        @pl.when((pl.program_id(0) == n_h - 1) & (i == nb - 1))
        def _():
            _push_wait(own, pairs)

    return pl.pallas_call(
        body, name="sb_bwd",
        grid=(n_h, nb),
        in_specs=[pl.BlockSpec((None, B, W), lambda h, i: (1, i, h)),
                  pl.BlockSpec((None, T, W), lambda h, i: (2, 0, h)),
                  pl.BlockSpec((None, T, W), lambda h, i: (3, 0, h)),
                  pl.BlockSpec((B, W), lambda h, i: (i, h)),
                  _ANY],
        out_specs=(pl.BlockSpec((B, W), lambda h, i: (i, h)),
                   pl.BlockSpec((T, W), lambda h, i: (0, h)),
                   pl.BlockSpec((T, W), lambda h, i: (0, h)),
                   _ANY),
        out_shape=(jax.ShapeDtypeStruct((T, 1024), BF16),
                   jax.ShapeDtypeStruct((T, 1024), BF16),
                   jax.ShapeDtypeStruct((T, 1024), BF16),
                   jax.ShapeDtypeStruct(g_p.shape, g_p.dtype)),
        scratch_shapes=[pltpu.VMEM((T, W), F32), pltpu.VMEM((T, W), F32),
                        pltpu.VMEM((HG, nkb, SB_HD, KB), BF16),
                        pltpu.VMEM((nkb, KB, W), F32), pltpu.VMEM((nkb, KB, W), F32),
                        pltpu.VMEM((HG, SB_HD, B), F32)] + _PUSH_SEMS,
        compiler_params=_cparams(("arbitrary", "arbitrary")),
    )(projb, projb, projb, do_sb, g_p)


def _mid_call(o_gla, o_sb, projf, x, target, wpa, wpb, wo, gla_g, b_gate, final_g):
    T, D = x.shape
    tm = min(TBLK, T)

    def body(og_ref, ggate_ref, osb_ref, sgate_ref, ma_ref, mb_ref, x_ref, tgt_ref,
             wpa_ref, wpb_ref, wo_ref, glag_ref, bg_ref, fg_ref,
             dx2_ref, dogla_ref, dosb_ref, dggate_ref, dsgate_ref, dm_ref,
             mt_ref, ogt_ref, obt_ref, dx2b_ref, dya_ref, dyb_ref,
             dfg_ref, dbg_ref, dglag_ref, loss_ref):
        @pl.when(pl.program_id(0) == 0)
        def _():
            dfg_ref[...] = jnp.zeros_like(dfg_ref)
            dbg_ref[...] = jnp.zeros_like(dbg_ref)
            dglag_ref[...] = jnp.zeros_like(dglag_ref)
            loss_ref[...] = jnp.zeros_like(loss_ref)

        glag = glag_ref[...]
        ggate = ggate_ref[...]
        sg = _sigmoid(ggate)
        silu_g = ggate * sg
        ohat, rinv, nrm = [], [], []
        for hh in range(GLA_HEADS):
            oh = og_ref[:, hh * GLA_HV:(hh + 1) * GLA_HV]
            r = lax.rsqrt(jnp.mean(oh * oh, axis=-1, keepdims=True) + EPS)
            ohat.append(oh * r)
            rinv.append(r)
            nrm.append(ohat[-1] * glag)
        n_all = jnp.concatenate(nrm, axis=1)
        og = n_all * silu_g
        ogb = _bf(og)
        ya = _dot(ogb, wpa_ref[...])
        sgate = sgate_ref[...]
        ss = _sigmoid(sgate)
        silu_s = sgate * ss
        osb = osb_ref[...]
        ob = osb * silu_s
        obb = _bf(ob)
        yb = _dot(obb, wpb_ref[...])
        ga = _sigmoid(ma_ref[...] + bg_ref[:, :D])
        gb = _sigmoid(mb_ref[...] + bg_ref[:, D:])
        merged = ga * ya + gb * yb
        mgb = _bf(merged)
        x2 = x_ref[...] + _dot(mgb, wo_ref[...])
        r2 = lax.rsqrt(jnp.mean(x2 * x2, axis=-1, keepdims=True) + EPS)
        xh2 = x2 * r2
        fg = fg_ref[...]
        err = xh2 * fg - tgt_ref[...]
        loss_ref[...] += jnp.broadcast_to(
            0.5 * jnp.sum(jnp.mean(err * err, axis=-1, keepdims=True), axis=0, keepdims=True), (1, 128))
        dy = err * (1.0 / D)
        dfg_ref[...] += jnp.sum(dy * xh2, axis=0, keepdims=True)
        dxh = dy * fg
        dx2 = r2 * (dxh - xh2 * jnp.mean(dxh * xh2, axis=-1, keepdims=True))
        dx2_ref[...] = dx2
        dx2b = _bf(dx2)
        dx2b_ref[...] = dx2b
        dmerged = _dot_nt(dx2b, wo_ref[...])
        dya = dmerged * ga
        dyb = dmerged * gb
        dma = dmerged * ya * ga * (1.0 - ga)
        dmb = dmerged * yb * gb * (1.0 - gb)
        dm_ref[:, :D] = _bf(dma)
        dm_ref[:, D:] = _bf(dmb)
        dbg_ref[:, :D] += jnp.sum(dma, axis=0, keepdims=True)
        dbg_ref[:, D:] += jnp.sum(dmb, axis=0, keepdims=True)
        dyab = _bf(dya)
        dybb = _bf(dyb)
        dya_ref[...] = dyab
        dyb_ref[...] = dybb
        dog = _dot_nt(dyab, wpa_ref[...])
        dob = _dot_nt(dybb, wpb_ref[...])
        dosb_ref[...] = dob * silu_s
        dsgate_ref[...] = _bf(dob * osb * (ss * (1.0 + sgate * (1.0 - ss))))
        dn = dog * silu_g
        dggate_ref[...] = _bf(dog * n_all * (sg * (1.0 + ggate * (1.0 - sg))))
        dglag = jnp.zeros((1, GLA_HV), F32)
        for hh in range(GLA_HEADS):
            dnh = dn[:, hh * GLA_HV:(hh + 1) * GLA_HV]
            dglag = dglag + jnp.sum(dnh * ohat[hh], axis=0, keepdims=True)
            dohat = dnh * glag
            dogla_ref[:, hh * GLA_HV:(hh + 1) * GLA_HV] = rinv[hh] * (
                dohat - ohat[hh] * jnp.mean(dohat * ohat[hh], axis=-1, keepdims=True))
        dglag_ref[...] += dglag
        mt_ref[...] = _bf(merged.T)
        ogt_ref[...] = _bf(og.T)
        obt_ref[...] = _bf(ob.T)

    row = lambda i: (i, 0)
    const = lambda i: (0, 0)
    tile = pl.BlockSpec((tm, D), row)
    tile_t = pl.BlockSpec((None, D, tm), lambda i: (i, 0, 0))
    wspec = pl.BlockSpec((D, D), const)
    return pl.pallas_call(
        body, name="mid",
        grid=(T // tm,),
        in_specs=[tile,
                  pl.BlockSpec((None, tm, D), lambda i: (1, i, 0)),
                  tile,
                  pl.BlockSpec((None, tm, D), lambda i: (2, i, 0)),
                  pl.BlockSpec((None, tm, D), lambda i: (3, i, 0)),
                  pl.BlockSpec((None, tm, D), lambda i: (4, i, 0)),
                  tile, tile, wspec, wspec, wspec,
                  pl.BlockSpec((1, GLA_HV), const),
                  pl.BlockSpec((1, 2 * D), const),
                  pl.BlockSpec((1, D), const)],
        out_specs=(tile, tile, tile, tile, tile,
                   pl.BlockSpec((tm, 2 * D), row),
                   tile_t, tile_t, tile_t, tile, tile, tile,
                   pl.BlockSpec((1, D), const),
                   pl.BlockSpec((1, 2 * D), const),
                   pl.BlockSpec((1, GLA_HV), const),
                   pl.BlockSpec((1, 128), const)),
        out_shape=(jax.ShapeDtypeStruct((T, D), F32),
                   jax.ShapeDtypeStruct((T, D), F32),
                   jax.ShapeDtypeStruct((T, D), F32),
                   jax.ShapeDtypeStruct((T, D), BF16),
                   jax.ShapeDtypeStruct((T, D), BF16),
                   jax.ShapeDtypeStruct((T, 2 * D), BF16),
                   jax.ShapeDtypeStruct((T // tm, D, tm), BF16),
                   jax.ShapeDtypeStruct((T // tm, D, tm), BF16),
                   jax.ShapeDtypeStruct((T // tm, D, tm), BF16),
                   jax.ShapeDtypeStruct((T, D), BF16),
                   jax.ShapeDtypeStruct((T, D), BF16),
                   jax.ShapeDtypeStruct((T, D), BF16),
                   jax.ShapeDtypeStruct((1, D), F32),
                   jax.ShapeDtypeStruct((1, 2 * D), F32),
                   jax.ShapeDtypeStruct((1, GLA_HV), F32),
                   jax.ShapeDtypeStruct((1, 128), F32)),
        compiler_params=_cparams(("arbitrary",)),
    )(o_gla, projf, o_sb, projf, projf, projf, x, target, wpa, wpb, wo, gla_g, b_gate, final_g)


def _dh_call(pieces, dmlog, drank, wt, wr, x, dx2, norm_g, s_in):
    T, D = x.shape
    tm = min(256, T)
    npc = len(pieces)
    n_main = N_GROUPS * 1024
    n_i = T // tm

    def body(*refs):
        pcs = refs[:npc]
        (dm_ref, dr_ref, w_hbm, wr_ref, x_ref, dx2_ref, g_ref, sin_ref,
         gx_ref, dg_ref, rin_ref, w_scr, sems, send_sems, recv_sems, loc_sem) = refs[npc:]
        own, pairs = _chip_copies(sin_ref, rin_ref, send_sems, recv_sems, loc_sem)

        @pl.when(pl.program_id(0) == 0)
        def _():
            _push_start(own, pairs)
            lo = pltpu.make_async_copy(w_hbm.at[pl.ds(0, RANK_COL)], w_scr.at[pl.ds(0, RANK_COL)], sems.at[0])
            hi = pltpu.make_async_copy(w_hbm.at[pl.ds(RANK_COL + GLA_RANK, n_main - RANK_COL)],
                                       w_scr.at[pl.ds(RANK_COL, n_main - RANK_COL)], sems.at[1])
            lo.start()
            hi.start()
            dg_ref[...] = jnp.zeros_like(dg_ref)
            lo.wait()
            hi.wait()

        def w_group(g):
            return w_scr[g * 1024:(g + 1) * 1024, :]

        dr = dr_ref[...]
        dh = _dot(dr, wr_ref[...])
        for g in range(npc):
            dh = dh + _dot(pcs[g][...], w_group(g))
        dh = dh + _dot(dm_ref[:, :D], w_group(npc))
        dh = dh + _dot(dm_ref[:, D:], w_group(npc + 1))
        xv = x_ref[...]
        r = lax.rsqrt(jnp.mean(xv * xv, axis=-1, keepdims=True) + EPS)
        xhat = xv * r
        g = g_ref[...]
        dg_ref[...] += jnp.sum(dh * xhat, axis=0, keepdims=True)
        dxhat = dh * g
        gx_ref[...] = r * (dxhat - xhat * jnp.mean(dxhat * xhat, axis=-1, keepdims=True)) + dx2_ref[...]

        @pl.when(pl.program_id(0) == n_i - 1)
        def _():
            _push_wait(own, pairs)

    row = lambda i: (i, 0)
    const = lambda i: (0, 0)
    tile = pl.BlockSpec((tm, D), row)
    return pl.pallas_call(
        body, name="dh",
        grid=(n_i,),
        in_specs=[tile] * npc + [
            pl.BlockSpec((tm, 2 * D), row),
            pl.BlockSpec((tm, 128), row),
            _ANY,
            pl.BlockSpec((128, D), const),
            tile, tile,
            pl.BlockSpec((1, D), const),
            _ANY],
        out_specs=(tile, pl.BlockSpec((1, D), const), _ANY),
        out_shape=(jax.ShapeDtypeStruct((T, D), F32),
                   jax.ShapeDtypeStruct((1, D), F32),
                   jax.ShapeDtypeStruct(s_in.shape, s_in.dtype)),
        scratch_shapes=[pltpu.VMEM((n_main, D), BF16), pltpu.SemaphoreType.DMA((2,))] + _CHIP_SEMS,
        compiler_params=_cparams(("arbitrary",)),
    )(*pieces, dmlog, drank, wt, wr, x, dx2, norm_g, s_in)


def _wgrad_rank_call(ht, drank):
    n_tb, D, tb = ht.shape

    def body(ht_ref, dr_ref, o_ref):
        @pl.when(pl.program_id(0) == 0)
        def _():
            o_ref[...] = jnp.zeros_like(o_ref)

        o_ref[...] += _dot(ht_ref[...], dr_ref[...])

    return pl.pallas_call(
        body, name="wgrad_rank",
        grid=(n_tb,),
        in_specs=[pl.BlockSpec((None, D, tb), lambda i: (i, 0, 0)),
                  pl.BlockSpec((tb, 128), lambda i: (i, 0))],
        out_specs=pl.BlockSpec((D, 128), lambda i: (0, 0)),
        out_shape=jax.ShapeDtypeStruct((D, 128), F32),
        compiler_params=_cparams(("arbitrary",)),
    )(ht, drank)


def _wgrad_call(lhs_list, lhs_of_group, rhs_list, rhs_of_group, n_transposed, name):
    n_groups = len(rhs_of_group)
    n_tb, D, tb = lhs_list[0].shape
    T = n_tb * tb
    per = min(2, n_tb)
    tk = per * tb
    nk = T // tk
    nl = len(lhs_list)

    def body(*refs):
        lhs = refs[:nl]
        rhs = refs[nl:nl + n_groups]
        out_ref, acc = refs[nl + n_groups:]
        g = pl.program_id(0)
        i = pl.program_id(1)

        @pl.when(i == 0)
        def _():
            acc[...] = jnp.zeros_like(acc)

        for p in range(n_groups):
            @pl.when(g == p)
            def _(p=p):
                lref = lhs[lhs_of_group[p]]
                part = _dot(lref[0], rhs[p][0:tb, :])
                for b in range(1, per):
                    part = part + _dot(lref[b], rhs[p][b * tb:(b + 1) * tb, :])
                acc[...] += part

        @pl.when((i == nk - 1) & (g < n_transposed))
        def _():
            out_ref[...] = _bf(acc[...].T)

        @pl.when((i == nk - 1) & (g >= n_transposed))
        def _():
            out_ref[...] = _bf(acc[...])

    def lhs_spec(a):
        groups = [g for g in range(n_groups) if lhs_of_group[g] == a]
        lo, hi = min(groups), max(groups)
        assert groups == list(range(lo, hi + 1))
        return pl.BlockSpec((per, D, tb), lambda g, i: (jnp.where((g >= lo) & (g <= hi), i, 0), 0, 0))

    def rhs_spec(p):
        cb = rhs_of_group[p][1]
        return pl.BlockSpec((tk, 1024), lambda g, i: (jnp.where(g == p, i, 0), cb))

    return pl.pallas_call(
        body, name=name,
        grid=(n_groups, nk),
        in_specs=[lhs_spec(a) for a in range(nl)] + [rhs_spec(p) for p in range(n_groups)],
        out_specs=pl.BlockSpec((None, D, 1024), lambda g, i: (g, 0, 0)),
        out_shape=jax.ShapeDtypeStruct((n_groups, D, 1024), BF16),
        scratch_shapes=[pltpu.VMEM((D, 1024), F32)],
        compiler_params=_cparams(("arbitrary", "arbitrary")),
    )(*lhs_list, *[rhs_list[rhs_of_group[p][0]] for p in range(n_groups)])


def _adamw_call(parts, w, m, v, name, gathered=None):
    R, C = w.shape
    n_parts = parts.shape[0]
    (tr, tc), grid, idx = _tiling_2d(R, C)
    carry = gathered is not None

    def body(*refs):
        if carry:
            (p_ref, w_ref, m_ref, v_ref, src_ref, g_ref, d_ref, nm_ref, nv_ref, dst_ref,
             send_sems, recv_sems, loc_sem) = refs
            own, pairs = _push_copies(src_ref, dst_ref, send_sems, recv_sems, loc_sem, scatter=False)

            @pl.when(pl.program_id(0) == 0)
            def _():
                _push_start(own, pairs)
        else:
            p_ref, w_ref, m_ref, v_ref, g_ref, d_ref, nm_ref, nv_ref = refs
        g = p_ref[n_parts - 1].astype(F32)
        for k in range(n_parts - 1):
            g = g + p_ref[k].astype(F32)
        mm = ADAM_B1 * m_ref[...] + (1.0 - ADAM_B1) * g
        vv = ADAM_B2 * v_ref[...] + (1.0 - ADAM_B2) * (g * g)
        m_hat = mm / (1.0 - ADAM_B1 ** ADAM_STEP)
        v_hat = vv / (1.0 - ADAM_B2 ** ADAM_STEP)
        d_ref[...] = -ADAM_LR * (m_hat / (jnp.sqrt(v_hat) + ADAM_EPS) + ADAM_WD * w_ref[...])
        g_ref[...] = g
        nm_ref[...] = mm
        nv_ref[...] = vv
        if carry:
            @pl.when(pl.program_id(0) == grid[0] - 1)
            def _():
                _push_wait(own, pairs)

    blk = pl.BlockSpec((tr, tc), idx)
    sds = jax.ShapeDtypeStruct((R, C), F32)
    in_specs = [pl.BlockSpec((n_parts, tr, tc), lambda i: (0,) + idx(i)), blk, blk, blk]
    out_specs = [blk, blk, blk, blk]
    out_shape = [sds, sds, sds, sds]
    operands = [parts, w, m, v]
    scratch = []
    if carry:
        in_specs.append(_ANY)
        out_specs.append(_ANY)
        out_shape.append(jax.ShapeDtypeStruct((N_DEV,) + gathered.shape, gathered.dtype))
        operands.append(gathered)
        scratch = _PUSH_SEMS
    return pl.pallas_call(
        body, name=name,
        grid=grid,
        in_specs=in_specs, out_specs=tuple(out_specs), out_shape=tuple(out_shape),
        scratch_shapes=scratch,
        compiler_params=_cparams(("arbitrary",)),
    )(*operands)


def _local_step(x, target, wt, wr, wdec, bdec, wp_shard, norm_g, gla_g, b_gate, final_g):
    D = x.shape[1]
    projf, projb, rank, ht = _proj_call(x, norm_g, wt, wr)
    o_gla, st_all, la = _gla_fwd_call(projf, projb, rank, wdec, bdec)
    o_sb, wp_all = _sb_fwd_call(projb, wp_shard)
    wp_full = wp_all.transpose(1, 0, 2, 3).reshape(3, D, D)
    (dx2, do_gla, do_sb, dggate, dsgate, dmlog, mt, ogt, obt, dx2b, dya, dyb,
     dfinal_g, db_gate, dgla_g, loss) = _mid_call(o_gla, o_sb, projf, x, target, wp_full[0], wp_full[1],
                                                 wp_full[2], gla_g, b_gate, final_g)
    dw_p = _wgrad_call([ogt, obt, mt], [0, 1, 2], [dya, dyb, dx2b], [(0, 0), (1, 0), (2, 0)], 0, "wgrad_p")
    g_p = dw_p.reshape(3, N_DEV, D // N_DEV, D).transpose(1, 0, 2, 3).reshape(N_DEV, 3 * (D // N_DEV), D)
    dqk, dgv, drank, dwdec, dbdec = _gla_bwd_call(projf, projb, la, do_gla, st_all, rank, wdec)
    dsq, dsk, dsv, r_p = _sb_bwd_call(projb, do_sb, g_p)
    pieces = [dqk, dgv, dggate, dsq, dsk, dsv, dsgate]
    rhs_of_group = [(g, 0) for g in range(7)] + [(7, 0), (7, 1)]
    dw_in = _wgrad_call([ht], [0] * N_GROUPS, pieces + [dmlog], rhs_of_group, N_GROUPS, "wgrad_in")
    dwr = _wgrad_rank_call(ht, drank)
    g_in = _parts_by_device(dw_in.reshape(N_GROUPS * 1024, D), dwr[:, :GLA_RANK].T.astype(BF16))
    c_idx = lax.axis_index("c").astype(jnp.int32).reshape(1)
    (p_in,) = _pair_exchange([g_in], "pair_g")
    s_in = _pair_add_call(g_in, p_in, c_idx, "pair_add_in")
    grad_x, dnorm_g, r_in = _dh_call(pieces, dmlog, drank, wt, wr, x, dx2, norm_g, s_in)
    small = jnp.concatenate([
        dnorm_g.reshape(-1), dbdec.reshape(-1), dgla_g.reshape(-1), db_gate.reshape(-1), dfinal_g.reshape(-1),
        loss.reshape(-1), dwdec[:GLA_RANK].reshape(-1)]).reshape(1, _SM_LEN)
    return grad_x, r_in, r_p, small


def _parts_by_device(dmain, drank):
    def part_for(p):
        lo, hi = p * SHARD_COLS, (p + 1) * SHARD_COLS
        pieces = []
        if lo < RANK_COL:
            pieces.append(dmain[lo:min(hi, RANK_COL)])
        if lo < RANK_COL + GLA_RANK and hi > RANK_COL:
            pieces.append(drank[max(lo, RANK_COL) - RANK_COL:min(hi, RANK_COL + GLA_RANK) - RANK_COL])
        if hi > RANK_COL + GLA_RANK:
            pieces.append(dmain[max(lo, RANK_COL + GLA_RANK) - GLA_RANK:hi - GLA_RANK])
        return pieces[0] if len(pieces) == 1 else jnp.concatenate(pieces, axis=0)

    return jnp.stack([part_for(p) for p in range(N_DEV)])


_SM_NORM = 0
_SM_BDEC = _SM_NORM + D_MODEL
_SM_GLAG = _SM_BDEC + GLA_DK
_SM_BGATE = _SM_GLAG + GLA_HV
_SM_FINAL = _SM_BGATE + 2 * D_MODEL
_SM_REPL = _SM_FINAL + D_MODEL
_SM_LOSS = _SM_REPL
_SM_WDEC = _SM_LOSS + 128
_SM_LEN = _SM_WDEC + GLA_RANK * GLA_DK


def kernel(x, norm_g, w_in, w_dec_up, b_dec, gla_norm_g, w_pa, w_pb, b_gate, w_o, final_g, loss_target, m_norm_g, m_w_in, m_w_dec_up, m_b_dec, m_gla_norm_g, m_w_pa, m_w_pb, m_b_gate, m_w_o, m_final_g, v_norm_g, v_w_in, v_w_dec_up, v_b_dec, v_gla_norm_g, v_w_pa, v_w_pb, v_b_gate, v_w_o, v_final_g):
    D = D_MODEL
    me = 4 * lax.axis_index("x") + 2 * lax.axis_index("y") + lax.axis_index("c")

    wp_shard = jnp.stack([w_pa, w_pb, w_o]).astype(BF16)
    win_all, wdec_all = _all_gather([w_in.T.astype(BF16), w_dec_up], "gather_w")
    wt = win_all.reshape(IN_COLS, D)
    wr = jnp.pad(wt[RANK_COL:RANK_COL + GLA_RANK], ((0, 128 - GLA_RANK), (0, 0)))
    wdec_full = wdec_all.transpose(1, 0, 2).reshape(GLA_RANK, GLA_DK)
    wdec = jnp.pad(wdec_full, ((0, 128 - GLA_RANK), (0, 0)))

    grad_x, r_in, r_p, small = _local_step(
        x[0], loss_target[0], wt, wr, wdec, b_dec.reshape(1, -1), wp_shard,
        norm_g.reshape(1, -1), gla_norm_g.reshape(1, -1), b_gate.reshape(1, -1), final_g.reshape(1, -1))

    gw_in, d_in, nm_in, nv_in, r_small = _adamw_call(r_in, w_in.T, m_w_in.T, v_w_in.T, "adamw_in",
                                                     gathered=small)
    gw_in, d_in, nm_in, nv_in = gw_in.T, d_in.T, nm_in.T, nv_in.T
    wp_f32 = jnp.concatenate([w_pa, w_pb, w_o], axis=0)
    mp = jnp.concatenate([m_w_pa, m_w_pb, m_w_o], axis=0)
    vp = jnp.concatenate([v_w_pa, v_w_pb, v_w_o], axis=0)
    gp, dp, nmp, nvp = _adamw_call(r_p, wp_f32, mp, vp, "adamw_p")
    rows = D // N_DEV

    def split3(a):
        return a[:rows], a[rows:2 * rows], a[2 * rows:]

    g_pa, g_pb, g_o = split3(gp)
    d_pa, d_pb, d_o = split3(dp)
    nm_pa, nm_pb, nm_o = split3(nmp)
    nv_pa, nv_pb, nv_o = split3(nvp)

    w_rep = jnp.concatenate([norm_g, b_dec, gla_norm_g, b_gate, final_g]).reshape(1, _SM_REPL)
    m_rep = jnp.concatenate([m_norm_g, m_b_dec, m_gla_norm_g, m_b_gate, m_final_g]).reshape(1, _SM_REPL)
    v_rep = jnp.concatenate([v_norm_g, v_b_dec, v_gla_norm_g, v_b_gate, v_final_g]).reshape(1, _SM_REPL)
    g_rep, d_rep, nm_rep, nv_rep = _adamw_call(r_small[:, :, :_SM_REPL], w_rep, m_rep, v_rep, "adamw_rep")

    def split_rep(a):
        a = a.reshape(-1)
        return (a[_SM_NORM:_SM_BDEC], a[_SM_BDEC:_SM_GLAG], a[_SM_GLAG:_SM_BGATE],
                a[_SM_BGATE:_SM_FINAL], a[_SM_FINAL:_SM_REPL])

    g_norm, g_bdec, g_glag, g_bgate, g_final = split_rep(g_rep)
    d_norm, d_bdec, d_glag, d_bgate, d_final = split_rep(d_rep)
    nm_norm, nm_bdec, nm_glag, nm_bgate, nm_final = split_rep(nm_rep)
    nv_norm, nv_bdec, nv_glag, nv_bgate, nv_final = split_rep(nv_rep)

    wdec_parts = r_small[:, 0, _SM_WDEC:].reshape(N_DEV, GLA_RANK, GLA_DK)
    cols = GLA_DK // N_DEV
    wdec_mine = lax.dynamic_slice_in_dim(wdec_parts, me * cols, cols, axis=2)
    g_wdec, d_wdec, nm_wdec, nv_wdec = _adamw_call(wdec_mine, w_dec_up, m_w_dec_up, v_w_dec_up, "adamw_dec")

    loss_total = jnp.sum(r_small[:, 0, _SM_LOSS])

    return (loss_total, grad_x[None],
            g_norm, gw_in, g_wdec, g_bdec, g_glag, g_pa, g_pb, g_bgate, g_o, g_final,
            d_norm, d_in, d_wdec, d_bdec, d_glag, d_pa, d_pb, d_bgate, d_o, d_final,
            nm_norm, nm_in, nm_wdec, nm_bdec, nm_glag, nm_pa, nm_pb, nm_bgate, nm_o, nm_final,
            nv_norm, nv_in, nv_wdec, nv_bdec, nv_glag, nv_pa, nv_pb, nv_bgate, nv_o, nv_final)
```

```python
import functools
import math

import jax
import jax.numpy as jnp
from jax import lax
from jax.experimental import pallas as pl
from jax.experimental.pallas import tpu as pltpu

F32 = jnp.float32
BF16 = jnp.bfloat16

N_DEV = 8
D_MODEL = 1024
GLA_HEADS = 4
GLA_HK = 128
GLA_HV = 256
GLA_DK = 512
GLA_RANK = 16
GLA_TAU = 16.0
GLA_CHUNK = 64
SB_HEADS = 8
SB_HD = 128
SB_BLOCK = 128
EPS = 1e-6
N_GROUPS = 9
RANK_COL = 3072
IN_COLS = 9232
SHARD_COLS = IN_COLS // N_DEV

ADAM_LR = 0.001
ADAM_B1 = 0.9
ADAM_B2 = 0.999
ADAM_EPS = 1e-08
ADAM_WD = 0.01
ADAM_STEP = 10

VMEM_LIMIT = 56 * 1024 * 1024
TBLK = 256


def _cparams(sem=None):
    return pltpu.CompilerParams(dimension_semantics=sem, vmem_limit_bytes=VMEM_LIMIT)


def _tiling_2d(rows, cols):
    if rows * cols <= 128 * 1024:
        return (rows, cols), (1,), lambda i: (0, 0)
    if rows % 128 == 0:
        return (128, cols), (rows // 128,), lambda i: (i, 0)
    tc = 256 if cols % 256 == 0 else cols
    return (rows, tc), (cols // tc,), lambda i: (0, i)


def _dot(a, b):
    return jnp.dot(a, b, preferred_element_type=F32)


def _dot_nt(a, b):
    return lax.dot_general(a, b, (((1,), (1,)), ((), ())), preferred_element_type=F32)


def _dot_tn(a, b):
    return lax.dot_general(a, b, (((0,), (0,)), ((), ())), preferred_element_type=F32)


def _bf(x):
    return x.astype(BF16)


def _split3(x):
    hi = x.astype(BF16)
    r = x - hi.astype(F32)
    mid = r.astype(BF16)
    lo = (r - mid.astype(F32)).astype(BF16)
    return hi, mid, lo


def _tri_left(tri, x):
    hi, mid, lo = _split3(x)
    return _dot(tri, hi) + _dot(tri, mid) + _dot(tri, lo)


def _split2(x):
    hi = lax.bitcast_convert_type(lax.bitcast_convert_type(x, jnp.uint32) & jnp.uint32(0xFFFF0000), F32)
    return hi.astype(BF16), (x - hi).astype(BF16)


def _tri2_left(tri, x):
    hi, lo = _split2(x)
    return _dot(tri, hi) + _dot(tri, lo)


def _tri2_right(x, tri):
    hi, lo = _split2(x)
    return _dot(hi, tri) + _dot(lo, tri)


def _iota2(n, m, dim):
    return lax.broadcasted_iota(jnp.int32, (n, m), dim)


def _sigmoid(x):
    return 1.0 / (1.0 + jnp.exp(-x))


def _softplus_neg_abs(z):
    return jnp.log(1.0 + jnp.exp(-jnp.abs(z)))


_ANY = pl.BlockSpec(memory_space=pl.ANY)


def _mesh_pos():
    return lax.axis_index("x"), lax.axis_index("y"), lax.axis_index("c")


def _other_chips(x, y):
    return [(1 - x, y), (x, 1 - y), (1 - x, 1 - y)]


def _rcopy(src, dst, send_sem, recv_sem, to):
    return pltpu.make_async_remote_copy(src_ref=src, dst_ref=dst, send_sem=send_sem, recv_sem=recv_sem,
                                        device_id=to, device_id_type=pl.DeviceIdType.MESH)


def _push_copies(src_ref, dst_ref, send_sems, recv_sems, loc_sem, scatter):
    x, y, c = _mesh_pos()
    me = 4 * x + 2 * y + c
    own = pltpu.make_async_copy(src_ref.at[me] if scatter else src_ref, dst_ref.at[me], loc_sem)
    pairs = []
    for k in range(1, N_DEV):
        px = 1 - x if k & 4 else x
        py = 1 - y if k & 2 else y
        pc = 1 - c if k & 1 else c
        pid = 4 * px + 2 * py + pc
        src = src_ref.at[pid] if scatter else src_ref
        send = _rcopy(src, dst_ref.at[me], send_sems.at[k - 1], recv_sems.at[k - 1], (px, py, pc))
        recv = _rcopy(src, dst_ref.at[pid], send_sems.at[k - 1], recv_sems.at[k - 1], (px, py, pc))
        pairs.append((send, recv))
    return own, pairs


def _push_start(own, pairs):
    own.start()
    for send, _ in pairs:
        send.start()


def _push_wait(own, pairs):
    for _, recv in pairs:
        recv.wait_recv()
    for send, _ in pairs:
        send.wait_send()
    own.wait()


_PUSH_SEMS = [pltpu.SemaphoreType.DMA((N_DEV - 1,)), pltpu.SemaphoreType.DMA((N_DEV - 1,)),
              pltpu.SemaphoreType.DMA]


def _chip_copies(src_ref, dst_ref, send_sems, recv_sems, loc_sem):
    x, y, c = _mesh_pos()
    own = pltpu.make_async_copy(src_ref.at[2 * x + y], dst_ref.at[3], loc_sem)
    pairs = []
    for j, (px, py) in enumerate(_other_chips(x, y)):
        cp = _rcopy(src_ref.at[2 * px + py], dst_ref.at[j], send_sems.at[j], recv_sems.at[j], (px, py, c))
        pairs.append((cp, cp))
    return own, pairs


_CHIP_SEMS = [pltpu.SemaphoreType.DMA((3,)), pltpu.SemaphoreType.DMA((3,)), pltpu.SemaphoreType.DMA]


def _all_gather(arrs, name):
    n = len(arrs)

    def body(*refs):
        ins = refs[:n]
        outs = refs[n:2 * n]
        send_sems, recv_sems, loc_sems = refs[2 * n:]
        x, y, c = _mesh_pos()
        sib = (x, y, 1 - c)
        chips = _other_chips(x, y)

        def place(a, px, py, pc):
            return outs[a].at[4 * px + 2 * py + pc]

        def copy(a, k, block, to, src=None):
            dst = place(a, *block)
            return _rcopy(dst if src is None else src, dst, send_sems.at[a, k], recv_sems.at[a, k], to)

        mine = [pltpu.make_async_copy(ins[a], place(a, x, y, c), loc_sems.at[a]) for a in range(n)]
        for cp in mine:
            cp.start()
        first = [copy(a, 0, (x, y, c), sib, src=ins[a]) for a in range(n)]
        for j, chip in enumerate(chips):
            first += [copy(a, 1 + j, (x, y, c), (*chip, c), src=ins[a]) for a in range(n)]
        for cp in first:
            cp.start()
        passed = []
        for j, chip in enumerate(chips):
            for a in range(n):
                copy(a, 1 + j, (*chip, c), (x, y, c)).wait_recv()
                fwd = copy(a, 4 + j, (*chip, c), sib)
                fwd.start()
                passed.append(fwd)
        for a in range(n):
            copy(a, 0, sib, (x, y, c)).wait_recv()
        for j, chip in enumerate(chips):
            for a in range(n):
                copy(a, 4 + j, (*chip, 1 - c), (x, y, c)).wait_recv()
        for cp in first + passed:
            cp.wait_send()
        for cp in mine:
            cp.wait()

    return pl.pallas_call(
        body, name=name,
        out_shape=tuple(jax.ShapeDtypeStruct((N_DEV,) + a.shape, a.dtype) for a in arrs),
        in_specs=[_ANY] * n,
        out_specs=tuple([_ANY] * n),
        scratch_shapes=[pltpu.SemaphoreType.DMA((n, 7)), pltpu.SemaphoreType.DMA((n, 7)),
                        pltpu.SemaphoreType.DMA((n,))],
    )(*arrs)


def _pair_exchange(arrs, name):
    n = len(arrs)

    def body(*refs):
        ins = refs[:n]
        outs = refs[n:2 * n]
        send_sems, recv_sems = refs[2 * n:]
        x, y, c = _mesh_pos()
        copies = []
        for a in range(n):
            for q in range(4):
                cp = _rcopy(ins[a].at[2 * q + (1 - c)], outs[a].at[q], send_sems.at[a, q], recv_sems.at[a, q],
                            (x, y, 1 - c))
                cp.start()
                copies.append(cp)
        for cp in copies:
            cp.wait_recv()
        for cp in copies:
            cp.wait_send()

    return pl.pallas_call(
        body, name=name,
        out_shape=tuple(jax.ShapeDtypeStruct((4,) + a.shape[1:], a.dtype) for a in arrs),
        in_specs=[_ANY] * n,
        out_specs=tuple([_ANY] * n),
        scratch_shapes=[pltpu.SemaphoreType.DMA((n, 4)), pltpu.SemaphoreType.DMA((n, 4))],
    )(*arrs)


def _pair_add_call(parts, recv, c_idx, name):
    _, R, C = parts.shape
    (tr, tc), (steps,), idx = _tiling_2d(R, C)

    def body(c_ref, p_ref, r_ref, o_ref):
        o_ref[...] = (p_ref[...].astype(F32) + r_ref[...].astype(F32)).astype(o_ref.dtype)

    return pl.pallas_call(
        body, name=name,
        grid_spec=pltpu.PrefetchScalarGridSpec(
            num_scalar_prefetch=1,
            grid=(4, steps),
            in_specs=[pl.BlockSpec((None, tr, tc), lambda q, i, c_ref: (2 * q + c_ref[0],) + idx(i)),
                      pl.BlockSpec((None, tr, tc), lambda q, i, c_ref: (q,) + idx(i))],
            out_specs=pl.BlockSpec((None, tr, tc), lambda q, i, c_ref: (q,) + idx(i))),
        out_shape=jax.ShapeDtypeStruct((4, R, C), parts.dtype),
        compiler_params=_cparams(("arbitrary", "arbitrary")),
    )(c_idx, parts, recv)


def _group_row(g):
    return GLA_RANK * (g * (1024 // GLA_RANK) + (g >= RANK_COL // 1024))


def _proj_call(x, norm_g, wt, wr, wp_part):
    T, D = x.shape
    tm = min(1024, T)
    assert tm % TBLK == 0
    n_i = T // tm

    def f_slot(j):
        return ((j >= 2).astype(jnp.int32) + (j >= 6).astype(jnp.int32)
                + (j >= 7).astype(jnp.int32) + (j >= 8).astype(jnp.int32))

    def b_slot(j):
        return (j >= 3).astype(jnp.int32) + (j >= 4).astype(jnp.int32) + (j >= 5).astype(jnp.int32)

    def body(x_ref, g_ref, w_ref, wr_ref, wp_ref, pf_ref, pb_ref, rank_ref, ht_ref, wpall_ref,
             h_scr, send_sems, recv_sems, loc_sem):
        i = pl.program_id(0)
        j = pl.program_id(1)
        own, pairs = _push_copies(wp_ref, wpall_ref, send_sems, recv_sems, loc_sem, scatter=False)

        @pl.when((i == 0) & (j == 0))
        def _():
            _push_start(own, pairs)

        @pl.when(j == 0)
        def _():
            xv = x_ref[...]
            r = lax.rsqrt(jnp.mean(xv * xv, axis=-1, keepdims=True) + EPS)
            h = (xv * r) * g_ref[...]
            hb = _bf(h)
            h_scr[...] = hb
            for b in range(tm // TBLK):
                ht_ref[b] = _bf(h[b * TBLK:(b + 1) * TBLK].T)
            rank_ref[...] = _dot_nt(hb, wr_ref[...])

        is_b = (j == 1) | ((j >= 3) & (j <= 5))

        @pl.when(is_b)
        def _():
            pb_ref[...] = _bf(_dot_nt(h_scr[...], w_ref[...]))

        @pl.when(jnp.logical_not(is_b))
        def _():
            pf_ref[...] = _dot_nt(h_scr[...], w_ref[...])

        @pl.when((i == n_i - 1) & (j == N_GROUPS - 1))
        def _():
            _push_wait(own, pairs)

    return pl.pallas_call(
        body, name="proj",
        grid=(n_i, N_GROUPS),
        in_specs=[pl.BlockSpec((tm, D), lambda i, j: (i, 0)),
                  pl.BlockSpec((1, D), lambda i, j: (0, 0)),
                  pl.BlockSpec((pl.Element(1024), pl.Element(D)), lambda i, j: (_group_row(j), 0)),
                  pl.BlockSpec((128, D), lambda i, j: (0, 0)),
                  _ANY],
        out_specs=(pl.BlockSpec((None, tm, 1024), lambda i, j: (f_slot(j), i, 0)),
                   pl.BlockSpec((None, tm, 1024), lambda i, j: (b_slot(j), i, 0)),
                   pl.BlockSpec((tm, 128), lambda i, j: (i, 0)),
                   pl.BlockSpec((tm // TBLK, D, TBLK), lambda i, j: (i, 0, 0)),
                   _ANY),
        out_shape=(jax.ShapeDtypeStruct((5, T, 1024), F32),
                   jax.ShapeDtypeStruct((4, T, 1024), BF16),
                   jax.ShapeDtypeStruct((T, 128), F32),
                   jax.ShapeDtypeStruct((T // TBLK, D, TBLK), BF16),
                   jax.ShapeDtypeStruct((N_DEV,) + wp_part.shape, wp_part.dtype)),
        scratch_shapes=[pltpu.VMEM((tm, D), BF16)] + _PUSH_SEMS,
        compiler_params=_cparams(("arbitrary", "arbitrary")),
    )(x, norm_g, wt, wr, wp_part)


def _gla_chunk_terms(la_h, q, k):
    C = GLA_CHUNK
    low = _bf((_iota2(C, C, 0) >= _iota2(C, C, 1)).astype(F32))
    b = _tri_left(low, la_h)
    bl = b[C - 1:C, :]
    eb = jnp.exp(b)
    enb = jnp.exp(-b)
    ebl_b = jnp.exp(bl - b)
    scale = GLA_HK ** -0.5
    qe = q * eb * scale
    ke = k * enb
    kd = k * ebl_b
    return b, bl, eb, enb, ebl_b, qe, ke, kd


def _gla_fwd_call(projf, projb, rank, wdec, bdec):
    T = projf.shape[1]
    C = GLA_CHUNK
    n_chunks = T // C

    def body(qk_ref, v_ref, rank_ref, wd_ref, bd_ref, o_ref, st_ref, la_ref, st_scr):
        @pl.when(pl.program_id(0) == 0)
        def _():
            st_scr[...] = jnp.zeros_like(st_scr)

        dec = _dot(_bf(rank_ref[...]), _bf(wd_ref[...])) + bd_ref[...]
        la = (jnp.minimum(dec, 0.0) - _softplus_neg_abs(dec)) / GLA_TAU
        la_ref[...] = la
        mask = _iota2(C, C, 0) >= _iota2(C, C, 1)
        _, bl, _, _, _, qe, ke, kd = _gla_chunk_terms(la, qk_ref[:, :GLA_DK], qk_ref[:, GLA_DK:])
        qeb, keb, kdb = _bf(qe), _bf(ke), _bf(kd)
        ebl = jnp.exp(bl)
        heads = range(GLA_HEADS)
        ks = [slice(hh * GLA_HK, (hh + 1) * GLA_HK) for hh in heads]
        vs = [slice(hh * GLA_HV, (hh + 1) * GLA_HV) for hh in heads]
        st = [st_scr[hh] for hh in heads]
        p = [_bf(jnp.where(mask, _dot_nt(qeb[:, ks[hh]], keb[:, ks[hh]]), 0.0)) for hh in heads]
        inter = [_dot_nt(qeb[:, ks[hh]], _bf(st[hh])) for hh in heads]
        upd = [_dot_tn(v_ref[:, vs[hh]], kdb[:, ks[hh]]) for hh in heads]
        intra = [_dot(p[hh], v_ref[:, vs[hh]]) for hh in heads]
        for hh in heads:
            st_ref[hh] = st[hh]
            o_ref[:, vs[hh]] = intra[hh] + inter[hh]
            st_scr[hh] = st[hh] * ebl[:, ks[hh]] + upd[hh]

    return pl.pallas_call(
        body, name="gla_fwd",
        grid=(n_chunks,),
        in_specs=[pl.BlockSpec((None, C, 1024), lambda n: (0, n, 0)),
                  pl.BlockSpec((None, C, 1024), lambda n: (0, n, 0)),
                  pl.BlockSpec((C, 128), lambda n: (n, 0)),
                  pl.BlockSpec((128, GLA_DK), lambda n: (0, 0)),
                  pl.BlockSpec((1, GLA_DK), lambda n: (0, 0))],
        out_specs=(pl.BlockSpec((C, 1024), lambda n: (n, 0)),
                   pl.BlockSpec((None, GLA_HEADS, GLA_HV, GLA_HK), lambda n: (n, 0, 0, 0)),
                   pl.BlockSpec((C, GLA_DK), lambda n: (n, 0))),
        out_shape=(jax.ShapeDtypeStruct((T, 1024), F32),
                   jax.ShapeDtypeStruct((n_chunks, GLA_HEADS, GLA_HV, GLA_HK), F32),
                   jax.ShapeDtypeStruct((T, GLA_DK), F32)),
        scratch_shapes=[pltpu.VMEM((GLA_HEADS, GLA_HV, GLA_HK), F32)],
        compiler_params=_cparams(("arbitrary",)),
    )(projf, projb, rank, wdec, bdec)


def _gla_bwd_call(projf, projb, la, do_gla, st_all, rank, wdec):
    T = projf.shape[1]
    C = GLA_CHUNK
    n_chunks = T // C
    last = n_chunks - 1

    def body(qk_ref, v_ref, la_ref, do_ref, st_ref, rank_ref, wd_ref,
             dqk_ref, dv_ref, drank_ref, dwd_ref, dbd_ref, dst_scr):
        @pl.when(pl.program_id(0) == 0)
        def _():
            dst_scr[...] = jnp.zeros_like(dst_scr)
            dwd_ref[...] = jnp.zeros_like(dwd_ref)
            dbd_ref[...] = jnp.zeros_like(dbd_ref)

        mask = _iota2(C, C, 0) >= _iota2(C, C, 1)
        upp = _bf((_iota2(C, C, 0) <= _iota2(C, C, 1)).astype(F32))
        scale = GLA_HK ** -0.5
        la = la_ref[...]
        _, bl, eb, enb, ebl_b, qe, ke, kd = _gla_chunk_terms(la, qk_ref[:, :GLA_DK], qk_ref[:, GLA_DK:])
        qeb, keb, kdb = _bf(qe), _bf(ke), _bf(kd)
        ebl = jnp.exp(bl)
        heads = range(GLA_HEADS)
        ks = [slice(hh * GLA_HK, (hh + 1) * GLA_HK) for hh in heads]
        vs = [slice(hh * GLA_HV, (hh + 1) * GLA_HV) for hh in heads]
        v = [v_ref[:, vs[hh]] for hh in heads]
        do = [_bf(do_ref[:, vs[hh]]) for hh in heads]
        st = [st_ref[hh] for hh in heads]
        dstn = [dst_scr[hh] for hh in heads]
        dstnb = [_bf(dstn[hh]) for hh in heads]
        p = [_bf(jnp.where(mask, _dot_nt(qeb[:, ks[hh]], keb[:, ks[hh]]), 0.0)) for hh in heads]
        dp = [_bf(jnp.where(mask, _dot_nt(do[hh], v[hh]), 0.0)) for hh in heads]
        dkd = [_dot(v[hh], dstnb[hh]) for hh in heads]
        dv_inter = [_dot_nt(kdb[:, ks[hh]], dstnb[hh]) for hh in heads]
        dqe_inter = [_dot(do[hh], _bf(st[hh])) for hh in heads]
        dst_new = [_dot_tn(do[hh], qeb[:, ks[hh]]) + dstn[hh] * ebl[:, ks[hh]] for hh in heads]
        debl = jnp.concatenate([jnp.sum(dstn[hh] * st[hh], axis=0, keepdims=True) for hh in heads], axis=1)
        dv = [_dot_tn(p[hh], do[hh]) + dv_inter[hh] for hh in heads]
        dqe = jnp.concatenate([_dot(dp[hh], keb[:, ks[hh]]) + dqe_inter[hh] for hh in heads], axis=1)
        dke = jnp.concatenate([_dot_tn(dp[hh], qeb[:, ks[hh]]) for hh in heads], axis=1)
        dkd = jnp.concatenate(dkd, axis=1)
        for hh in heads:
            dst_scr[hh] = dst_new[hh]
            dv_ref[:, vs[hh]] = _bf(dv[hh])
        dkd_kd = dkd * kd
        db = dqe * qe - dke * ke - dkd_kd
        dbl = jnp.sum(dkd_kd, axis=0, keepdims=True) + ebl * debl
        dla = _tri_left(upp, db) + dbl
        dqk_ref[:, :GLA_DK] = _bf(dqe * eb * scale)
        dqk_ref[:, GLA_DK:] = _bf(dke * enb + dkd * ebl_b)
        ddec = dla * (1.0 / GLA_TAU) * (1.0 - jnp.exp(GLA_TAU * la))
        ddecb = _bf(ddec)
        drank_ref[...] = _bf(_dot_nt(ddecb, _bf(wd_ref[...])))
        dwd_ref[...] += _dot_tn(_bf(rank_ref[...]), ddecb)
        dbd_ref[...] += jnp.sum(ddec, axis=0, keepdims=True)

    return pl.pallas_call(
        body, name="gla_bwd",
        grid=(n_chunks,),
        in_specs=[pl.BlockSpec((None, C, 1024), lambda n: (0, last - n, 0)),
                  pl.BlockSpec((None, C, 1024), lambda n: (0, last - n, 0)),
                  pl.BlockSpec((C, GLA_DK), lambda n: (last - n, 0)),
                  pl.BlockSpec((C, 1024), lambda n: (last - n, 0)),
                  pl.BlockSpec((None, GLA_HEADS, GLA_HV, GLA_HK), lambda n: (last - n, 0, 0, 0)),
                  pl.BlockSpec((C, 128), lambda n: (last - n, 0)),
                  pl.BlockSpec((128, GLA_DK), lambda n: (0, 0))],
        out_specs=(pl.BlockSpec((C, 1024), lambda n: (last - n, 0)),
                   pl.BlockSpec((C, 1024), lambda n: (last - n, 0)),
                   pl.BlockSpec((C, 128), lambda n: (last - n, 0)),
                   pl.BlockSpec((128, GLA_DK), lambda n: (0, 0)),
                   pl.BlockSpec((1, GLA_DK), lambda n: (0, 0))),
        out_shape=(jax.ShapeDtypeStruct((T, 1024), BF16),
                   jax.ShapeDtypeStruct((T, 1024), BF16),
                   jax.ShapeDtypeStruct((T, 128), BF16),
                   jax.ShapeDtypeStruct((128, GLA_DK), F32),
                   jax.ShapeDtypeStruct((1, GLA_DK), F32)),
        scratch_shapes=[pltpu.VMEM((GLA_HEADS, GLA_HV, GLA_HK), F32)],
        compiler_params=_cparams(("arbitrary",)),
    )(projf, projb, la, do_gla, st_all, rank, wdec)


def _sb_logs(z):
    lsz = jnp.minimum(z, 0.0) - _softplus_neg_abs(z)
    return lsz, lsz - z


SB_HG_FWD = 8
SB_HG_BWD = 4
SB_KEYS = 256
SB_DEAD = -105.0


def _sb_fwd_call(projb, wp_shard):
    T = projb.shape[1]
    B = SB_BLOCK
    HG = SB_HG_FWD
    W = HG * SB_HD
    scale = 1.0 / math.sqrt(SB_HD)
    KB = min(SB_KEYS, T)
    n_h, n_i = SB_HEADS // HG, T // B

    def body(q_ref, k_ref, v_ref, wp_ref, o_ref, wpall_ref, cb_scr, send_sems, recv_sems, loc_sem):
        i = pl.program_id(1)
        own, pairs = _push_copies(wp_ref, wpall_ref, send_sems, recv_sems, loc_sem, scatter=False)

        @pl.when((pl.program_id(0) == 0) & (i == 0))
        def _():
            _push_start(own, pairs)

        rows = HG * B
        after = (_iota2(KB, KB, 0) > _iota2(KB, KB, 1)).astype(F32)
        tri = _bf(jnp.concatenate([after, jnp.ones((KB, KB), F32)], axis=1))
        o_ref[...] = jnp.zeros_like(o_ref)
        cb_scr[...] = jnp.zeros_like(cb_scr)

        def block(jp, masked):
            off = pl.multiple_of(jp * KB, KB)
            z = jnp.concatenate(
                [_dot_nt(q_ref[:, hh * SB_HD:(hh + 1) * SB_HD], k_ref[pl.ds(off, KB), hh * SB_HD:(hh + 1) * SB_HD])
                 for hh in range(HG)], axis=0) * scale
            lsz, l1m = _sb_logs(z)
            if masked:
                strict = (jp * KB + _iota2(rows, KB, 1)) < (i * B + (_iota2(rows, KB, 0) & (B - 1)))
                l1m = jnp.where(strict, l1m, 0.0)
            r = _tri2_right(l1m, tri)
            cb = cb_scr[...]
            a = jnp.exp(lsz + cb + r[:, :KB])
            if masked:
                a = jnp.where(strict, a, 0.0)
            cb_scr[...] = cb + r[:, KB:]
            ab = _bf(a)
            for hh in range(HG):
                cs = slice(hh * SB_HD, (hh + 1) * SB_HD)
                o_ref[:, cs] += _dot(ab[hh * B:(hh + 1) * B, :], v_ref[pl.ds(off, KB), cs])

        jp0 = (i * B) // KB
        block(jp0, True)

        def live(state):
            jj, dead = state
            return (jj <= jp0) & jnp.logical_not(dead)

        def step(state):
            jj, _ = state
            block(jp0 - jj, False)
            return jj + 1, jnp.max(cb_scr[:, :B]) < SB_DEAD

        lax.while_loop(live, step, (jnp.int32(1), jnp.max(cb_scr[:, :B]) < SB_DEAD))

        @pl.when((pl.program_id(0) == n_h - 1) & (i == n_i - 1))
        def _():
            _push_wait(own, pairs)

    return pl.pallas_call(
        body, name="sb_fwd",
        grid=(n_h, n_i),
        in_specs=[pl.BlockSpec((None, B, W), lambda h, i: (1, i, h)),
                  pl.BlockSpec((None, T, W), lambda h, i: (2, 0, h)),
                  pl.BlockSpec((None, T, W), lambda h, i: (3, 0, h)),
                  _ANY],
        out_specs=(pl.BlockSpec((B, W), lambda h, i: (i, h)), _ANY),
        out_shape=(jax.ShapeDtypeStruct((T, 1024), F32),
                   jax.ShapeDtypeStruct((N_DEV,) + wp_shard.shape, wp_shard.dtype)),
        scratch_shapes=[pltpu.VMEM((HG * B, KB), F32)] + _PUSH_SEMS,
        compiler_params=_cparams(("arbitrary", "arbitrary")),
    )(projb, projb, projb, wp_shard)


def _sb_bwd_call(projb, do_sb, g_p):
    T = projb.shape[1]
    B = SB_BLOCK
    nb = T // B
    HG = SB_HG_BWD
    W = HG * SB_HD
    KB = min(SB_KEYS, T)
    nkb = T // KB
    n_h = SB_HEADS // HG
    scale = 1.0 / math.sqrt(SB_HD)

    def body(q_ref, k_ref, v_ref, do_ref, gp_ref, dq_ref, dk_ref, dv_ref, rp_ref,
             dk_scr, dv_scr, kt_scr, beta_scr, g_scr, dqt_scr, send_sems, recv_sems, loc_sem):
        i = pl.program_id(1)
        own, pairs = _push_copies(gp_ref, rp_ref, send_sems, recv_sems, loc_sem, scatter=True)

        @pl.when((pl.program_id(0) == 0) & (i == 0))
        def _():
            _push_start(own, pairs)

        @pl.when(i == 0)
        def _():
            dk_scr[...] = jnp.zeros_like(dk_scr)
            dv_scr[...] = jnp.zeros_like(dv_scr)
            for hh in range(HG):
                for jb in range(nkb):
                    kt_scr[hh, jb] = _bf(
                        k_ref[jb * KB:(jb + 1) * KB, hh * SB_HD:(hh + 1) * SB_HD].astype(F32).T)

        dqt_scr[...] = jnp.zeros_like(dqt_scr)
        later = _bf((_iota2(KB, KB, 1) > _iota2(KB, KB, 0)).astype(F32))
        earlier = _bf((_iota2(KB, KB, 1) < _iota2(KB, KB, 0)).astype(F32))
        dob = _bf(do_ref[...])
        jp0 = (i * B) // KB

        def strict_mask():
            return (jp0 * KB + _iota2(KB, W, 0)) < (i * B + (_iota2(KB, W, 1) & (B - 1)))

        def heads(fn):
            return [fn(slice(hh * SB_HD, (hh + 1) * SB_HD)) for hh in range(HG)]

        def pass1(jp, cb, masked):
            off = pl.multiple_of(jp * KB, KB)
            z = jnp.concatenate(heads(lambda cs: _dot_nt(k_ref[pl.ds(off, KB), cs], q_ref[:, cs])), axis=1) * scale
            da = jnp.concatenate(heads(lambda cs: _dot_nt(v_ref[pl.ds(off, KB), cs], dob[:, cs])), axis=1)
            lsz, l1m = _sb_logs(z)
            if masked:
                strict = strict_mask()
                l1m = jnp.where(strict, l1m, 0.0)
            a = jnp.exp(lsz + cb + _tri2_left(later, l1m))
            if masked:
                a = jnp.where(strict, a, 0.0)
            g_scr[jp] = a * da
            beta_scr[jp] = jnp.exp(lsz)
            ab = _bf(a)
            for hh in range(HG):
                cs = slice(hh * SB_HD, (hh + 1) * SB_HD)
                dv_scr[pl.ds(off, KB), cs] += _dot(ab[:, cs], dob[:, cs])
            return cb + jnp.sum(l1m, axis=0, keepdims=True)

        zero = jnp.zeros((1, W), F32)
        cb = pass1(jp0, zero, True)

        def live(state):
            jj, _, dead = state
            return (jj <= jp0) & jnp.logical_not(dead)

        def step(state):
            jj, cr, _ = state
            cr = pass1(jp0 - jj, cr, False)
            return jj + 1, cr, jnp.max(cr) < SB_DEAD

        n_done, _, _ = lax.while_loop(live, step, (jnp.int32(1), cb, jnp.max(cb) < SB_DEAD))
        jp_first = jp0 - (n_done - 1)

        def pass2(jp, cg, masked):
            off = pl.multiple_of(jp * KB, KB)
            g = g_scr[jp]
            beta = beta_scr[jp]
            dz = g * (1.0 - beta) - beta * (cg + _tri2_left(earlier, g))
            if masked:
                dz = jnp.where(strict_mask(), dz, 0.0)
            dzb = _bf(dz * scale)
            for hh in range(HG):
                cs = slice(hh * SB_HD, (hh + 1) * SB_HD)
                dk_scr[pl.ds(off, KB), cs] += _dot(dzb[:, cs], q_ref[:, cs])
                dqt_scr[hh] += _dot(kt_scr[hh, jp], dzb[:, cs])
            return cg + jnp.sum(g, axis=0, keepdims=True)

        cg = lax.fori_loop(jp_first, jp0, lambda jp, cr: pass2(jp, cr, False), zero)
        pass2(jp0, cg, True)
        for hh in range(HG):
            dq_ref[:, hh * SB_HD:(hh + 1) * SB_HD] = _bf(dqt_scr[hh].T)

        @pl.when(i == nb - 1)
        def _():
            dk_ref[...] = _bf(dk_scr[...])
            dv_ref[...] = _bf(dv_scr[...])

        @pl.when((pl.program_id(0) == n_h - 1) & (i == nb - 1))
        def _():
            _push_wait(own, pairs)

    return pl.pallas_call(
        body, name="sb_bwd",
        grid=(n_h, nb),
        in_specs=[pl.BlockSpec((None, B, W), lambda h, i: (1, i, h)),
                  pl.BlockSpec((None, T, W), lambda h, i: (2, 0, h)),
                  pl.BlockSpec((None, T, W), lambda h, i: (3, 0, h)),
                  pl.BlockSpec((B, W), lambda h, i: (i, h)),
                  _ANY],
        out_specs=(pl.BlockSpec((B, W), lambda h, i: (i, h)),
                   pl.BlockSpec((T, W), lambda h, i: (0, h)),
                   pl.BlockSpec((T, W), lambda h, i: (0, h)),
                   _ANY),
        out_shape=(jax.ShapeDtypeStruct((T, 1024), BF16),
                   jax.ShapeDtypeStruct((T, 1024), BF16),
                   jax.ShapeDtypeStruct((T, 1024), BF16),
                   jax.ShapeDtypeStruct(g_p.shape, g_p.dtype)),
        scratch_shapes=[pltpu.VMEM((T, W), F32), pltpu.VMEM((T, W), F32),
                        pltpu.VMEM((HG, nkb, SB_HD, KB), BF16),
                        pltpu.VMEM((nkb, KB, W), F32), pltpu.VMEM((nkb, KB, W), F32),
                        pltpu.VMEM((HG, SB_HD, B), F32)] + _PUSH_SEMS,
        compiler_params=_cparams(("arbitrary", "arbitrary")),
    )(projb, projb, projb, do_sb, g_p)


def _mid_call(o_gla, o_sb, projf, x, target, wpa, wpb, wo, gla_g, b_gate, final_g):
    T, D = x.shape
    tm = min(TBLK, T)

    def body(og_ref, ggate_ref, osb_ref, sgate_ref, ma_ref, mb_ref, x_ref, tgt_ref,
             wpa_ref, wpb_ref, wo_ref, glag_ref, bg_ref, fg_ref,
             dx2_ref, dogla_ref, dosb_ref, dggate_ref, dsgate_ref, dm_ref,
             mt_ref, ogt_ref, obt_ref, dx2b_ref, dya_ref, dyb_ref,
             dfg_ref, dbg_ref, dglag_ref, loss_ref):
        @pl.when(pl.program_id(0) == 0)
        def _():
            dfg_ref[...] = jnp.zeros_like(dfg_ref)
            dbg_ref[...] = jnp.zeros_like(dbg_ref)
            dglag_ref[...] = jnp.zeros_like(dglag_ref)
            loss_ref[...] = jnp.zeros_like(loss_ref)

        glag = glag_ref[...]
        ggate = ggate_ref[...]
        sg = _sigmoid(ggate)
        silu_g = ggate * sg
        ohat, rinv, nrm = [], [], []
        for hh in range(GLA_HEADS):
            oh = og_ref[:, hh * GLA_HV:(hh + 1) * GLA_HV]
            r = lax.rsqrt(jnp.mean(oh * oh, axis=-1, keepdims=True) + EPS)
            ohat.append(oh * r)
            rinv.append(r)
            nrm.append(ohat[-1] * glag)
        n_all = jnp.concatenate(nrm, axis=1)
        og = n_all * silu_g
        ogb = _bf(og)
        ya = _dot(ogb, wpa_ref[...])
        sgate = sgate_ref[...]
        ss = _sigmoid(sgate)
        silu_s = sgate * ss
        osb = osb_ref[...]
        ob = osb * silu_s
        obb = _bf(ob)
        yb = _dot(obb, wpb_ref[...])
        ga = _sigmoid(ma_ref[...] + bg_ref[:, :D])
        gb = _sigmoid(mb_ref[...] + bg_ref[:, D:])
        merged = ga * ya + gb * yb
        mgb = _bf(merged)
        x2 = x_ref[...] + _dot(mgb, wo_ref[...])
        r2 = lax.rsqrt(jnp.mean(x2 * x2, axis=-1, keepdims=True) + EPS)
        xh2 = x2 * r2
        fg = fg_ref[...]
        err = xh2 * fg - tgt_ref[...]
        loss_ref[...] += jnp.broadcast_to(
            0.5 * jnp.sum(jnp.mean(err * err, axis=-1, keepdims=True), axis=0, keepdims=True), (1, 128))
        dy = err * (1.0 / D)
        dfg_ref[...] += jnp.sum(dy * xh2, axis=0, keepdims=True)
        dxh = dy * fg
        dx2 = r2 * (dxh - xh2 * jnp.mean(dxh * xh2, axis=-1, keepdims=True))
        dx2_ref[...] = dx2
        dx2b = _bf(dx2)
        dx2b_ref[...] = dx2b
        dmerged = _dot_nt(dx2b, wo_ref[...])
        dya = dmerged * ga
        dyb = dmerged * gb
        dma = dmerged * ya * ga * (1.0 - ga)
        dmb = dmerged * yb * gb * (1.0 - gb)
        dm_ref[:, :D] = _bf(dma)
        dm_ref[:, D:] = _bf(dmb)
        dbg_ref[:, :D] += jnp.sum(dma, axis=0, keepdims=True)
        dbg_ref[:, D:] += jnp.sum(dmb, axis=0, keepdims=True)
        dyab = _bf(dya)
        dybb = _bf(dyb)
        dya_ref[...] = dyab
        dyb_ref[...] = dybb
        dog = _dot_nt(dyab, wpa_ref[...])
        dob = _dot_nt(dybb, wpb_ref[...])
        dosb_ref[...] = dob * silu_s
        dsgate_ref[...] = _bf(dob * osb * (ss * (1.0 + sgate * (1.0 - ss))))
        dn = dog * silu_g
        dggate_ref[...] = _bf(dog * n_all * (sg * (1.0 + ggate * (1.0 - sg))))
        dglag = jnp.zeros((1, GLA_HV), F32)
        for hh in range(GLA_HEADS):
            dnh = dn[:, hh * GLA_HV:(hh + 1) * GLA_HV]
            dglag = dglag + jnp.sum(dnh * ohat[hh], axis=0, keepdims=True)
            dohat = dnh * glag
            dogla_ref[:, hh * GLA_HV:(hh + 1) * GLA_HV] = rinv[hh] * (
                dohat - ohat[hh] * jnp.mean(dohat * ohat[hh], axis=-1, keepdims=True))
        dglag_ref[...] += dglag
        mt_ref[...] = _bf(merged.T)
        ogt_ref[...] = _bf(og.T)
        obt_ref[...] = _bf(ob.T)

    row = lambda i: (i, 0)
    const = lambda i: (0, 0)
    tile = pl.BlockSpec((tm, D), row)
    tile_t = pl.BlockSpec((None, D, tm), lambda i: (i, 0, 0))
    wspec = pl.BlockSpec((D, D), const)
    return pl.pallas_call(
        body, name="mid",
        grid=(T // tm,),
        in_specs=[tile,
                  pl.BlockSpec((None, tm, D), lambda i: (1, i, 0)),
                  tile,
                  pl.BlockSpec((None, tm, D), lambda i: (2, i, 0)),
                  pl.BlockSpec((None, tm, D), lambda i: (3, i, 0)),
                  pl.BlockSpec((None, tm, D), lambda i: (4, i, 0)),
                  tile, tile, wspec, wspec, wspec,
                  pl.BlockSpec((1, GLA_HV), const),
                  pl.BlockSpec((1, 2 * D), const),
                  pl.BlockSpec((1, D), const)],
        out_specs=(tile, tile, tile, tile, tile,
                   pl.BlockSpec((tm, 2 * D), row),
                   tile_t, tile_t, tile_t, tile, tile, tile,
                   pl.BlockSpec((1, D), const),
                   pl.BlockSpec((1, 2 * D), const),
                   pl.BlockSpec((1, GLA_HV), const),
                   pl.BlockSpec((1, 128), const)),
        out_shape=(jax.ShapeDtypeStruct((T, D), F32),
                   jax.ShapeDtypeStruct((T, D), F32),
                   jax.ShapeDtypeStruct((T, D), F32),
                   jax.ShapeDtypeStruct((T, D), BF16),
                   jax.ShapeDtypeStruct((T, D), BF16),
                   jax.ShapeDtypeStruct((T, 2 * D), BF16),
                   jax.ShapeDtypeStruct((T // tm, D, tm), BF16),
                   jax.ShapeDtypeStruct((T // tm, D, tm), BF16),
                   jax.ShapeDtypeStruct((T // tm, D, tm), BF16),
                   jax.ShapeDtypeStruct((T, D), BF16),
                   jax.ShapeDtypeStruct((T, D), BF16),
                   jax.ShapeDtypeStruct((T, D), BF16),
                   jax.ShapeDtypeStruct((1, D), F32),
                   jax.ShapeDtypeStruct((1, 2 * D), F32),
                   jax.ShapeDtypeStruct((1, GLA_HV), F32),
                   jax.ShapeDtypeStruct((1, 128), F32)),
        compiler_params=_cparams(("arbitrary",)),
    )(o_gla, projf, o_sb, projf, projf, projf, x, target, wpa, wpb, wo, gla_g, b_gate, final_g)


def _dh_call(pieces, dmlog, drank, wt, wr, x, dx2, norm_g, s_in):
    T, D = x.shape
    tm = min(256, T)
    npc = len(pieces)
    n_main = N_GROUPS * 1024
    n_i = T // tm

    def body(*refs):
        pcs = refs[:npc]
        (dm_ref, dr_ref, w_hbm, wr_ref, x_ref, dx2_ref, g_ref, sin_ref,
         gx_ref, dg_ref, rin_ref, w_scr, sems, send_sems, recv_sems, loc_sem) = refs[npc:]
        own, pairs = _chip_copies(sin_ref, rin_ref, send_sems, recv_sems, loc_sem)

        @pl.when(pl.program_id(0) == 0)
        def _():
            _push_start(own, pairs)
            lo = pltpu.make_async_copy(w_hbm.at[pl.ds(0, RANK_COL)], w_scr.at[pl.ds(0, RANK_COL)], sems.at[0])
            hi = pltpu.make_async_copy(w_hbm.at[pl.ds(RANK_COL + GLA_RANK, n_main - RANK_COL)],
                                       w_scr.at[pl.ds(RANK_COL, n_main - RANK_COL)], sems.at[1])
            lo.start()
            hi.start()
            dg_ref[...] = jnp.zeros_like(dg_ref)
            lo.wait()
            hi.wait()

        def w_group(g):
            return w_scr[g * 1024:(g + 1) * 1024, :]

        dr = dr_ref[...]
        dh = _dot(dr, wr_ref[...])
        for g in range(npc):
            dh = dh + _dot(pcs[g][...], w_group(g))
        dh = dh + _dot(dm_ref[:, :D], w_group(npc))
        dh = dh + _dot(dm_ref[:, D:], w_group(npc + 1))
        xv = x_ref[...]
        r = lax.rsqrt(jnp.mean(xv * xv, axis=-1, keepdims=True) + EPS)
        xhat = xv * r
        g = g_ref[...]
        dg_ref[...] += jnp.sum(dh * xhat, axis=0, keepdims=True)
        dxhat = dh * g
        gx_ref[...] = r * (dxhat - xhat * jnp.mean(dxhat * xhat, axis=-1, keepdims=True)) + dx2_ref[...]

        @pl.when(pl.program_id(0) == n_i - 1)
        def _():
            _push_wait(own, pairs)

    row = lambda i: (i, 0)
    const = lambda i: (0, 0)
    tile = pl.BlockSpec((tm, D), row)
    return pl.pallas_call(
        body, name="dh",
        grid=(n_i,),
        in_specs=[tile] * npc + [
            pl.BlockSpec((tm, 2 * D), row),
            pl.BlockSpec((tm, 128), row),
            _ANY,
            pl.BlockSpec((128, D), const),
            tile, tile,
            pl.BlockSpec((1, D), const),
            _ANY],
        out_specs=(tile, pl.BlockSpec((1, D), const), _ANY),
        out_shape=(jax.ShapeDtypeStruct((T, D), F32),
                   jax.ShapeDtypeStruct((1, D), F32),
                   jax.ShapeDtypeStruct(s_in.shape, s_in.dtype)),
        scratch_shapes=[pltpu.VMEM((n_main, D), BF16), pltpu.SemaphoreType.DMA((2,))] + _CHIP_SEMS,
        compiler_params=_cparams(("arbitrary",)),
    )(*pieces, dmlog, drank, wt, wr, x, dx2, norm_g, s_in)


def _wgrad_rank_call(ht, drank):
    n_tb, D, tb = ht.shape

    def body(ht_ref, dr_ref, o_ref):
        @pl.when(pl.program_id(0) == 0)
        def _():
            o_ref[...] = jnp.zeros_like(o_ref)

        o_ref[...] += _dot(ht_ref[...], dr_ref[...])

    return pl.pallas_call(
        body, name="wgrad_rank",
        grid=(n_tb,),
        in_specs=[pl.BlockSpec((None, D, tb), lambda i: (i, 0, 0)),
                  pl.BlockSpec((tb, 128), lambda i: (i, 0))],
        out_specs=pl.BlockSpec((D, 128), lambda i: (0, 0)),
        out_shape=jax.ShapeDtypeStruct((D, 128), F32),
        compiler_params=_cparams(("arbitrary",)),
    )(ht, drank)


def _wgrad_call(lhs_list, lhs_of_group, rhs_list, rhs_of_group, n_transposed, name):
    n_groups = len(rhs_of_group)
    n_tb, D, tb = lhs_list[0].shape
    T = n_tb * tb
    per = min(2, n_tb)
    tk = per * tb
    nk = T // tk
    nl = len(lhs_list)

    def body(*refs):
        lhs = refs[:nl]
        rhs = refs[nl:nl + n_groups]
        out_ref, acc = refs[nl + n_groups:]
        g = pl.program_id(0)
        i = pl.program_id(1)

        @pl.when(i == 0)
        def _():
            acc[...] = jnp.zeros_like(acc)

        for p in range(n_groups):
            @pl.when(g == p)
            def _(p=p):
                lref = lhs[lhs_of_group[p]]
                part = _dot(lref[0], rhs[p][0:tb, :])
                for b in range(1, per):
                    part = part + _dot(lref[b], rhs[p][b * tb:(b + 1) * tb, :])
                acc[...] += part

        @pl.when((i == nk - 1) & (g < n_transposed))
        def _():
            out_ref[...] = _bf(acc[...].T)

        @pl.when((i == nk - 1) & (g >= n_transposed))
        def _():
            out_ref[...] = _bf(acc[...])

    def lhs_spec(a):
        groups = [g for g in range(n_groups) if lhs_of_group[g] == a]
        lo, hi = min(groups), max(groups)
        assert groups == list(range(lo, hi + 1))
        return pl.BlockSpec((per, D, tb), lambda g, i: (jnp.where((g >= lo) & (g <= hi), i, 0), 0, 0))

    def rhs_spec(p):
        cb = rhs_of_group[p][1]
        return pl.BlockSpec((tk, 1024), lambda g, i: (jnp.where(g == p, i, 0), cb))

    return pl.pallas_call(
        body, name=name,
        grid=(n_groups, nk),
        in_specs=[lhs_spec(a) for a in range(nl)] + [rhs_spec(p) for p in range(n_groups)],
        out_specs=pl.BlockSpec((None, D, 1024), lambda g, i: (g, 0, 0)),
        out_shape=jax.ShapeDtypeStruct((n_groups, D, 1024), BF16),
        scratch_shapes=[pltpu.VMEM((D, 1024), F32)],
        compiler_params=_cparams(("arbitrary", "arbitrary")),
    )(*lhs_list, *[rhs_list[rhs_of_group[p][0]] for p in range(n_groups)])


def _adamw_call(parts, w, m, v, name, gathered=None):
    R, C = w.shape
    n_parts = parts.shape[0]
    (tr, tc), grid, idx = _tiling_2d(R, C)
    carry = gathered is not None

    def body(*refs):
        if carry:
            (p_ref, w_ref, m_ref, v_ref, src_ref, g_ref, d_ref, nm_ref, nv_ref, dst_ref,
             send_sems, recv_sems, loc_sem) = refs
            own, pairs = _push_copies(src_ref, dst_ref, send_sems, recv_sems, loc_sem, scatter=False)

            @pl.when(pl.program_id(0) == 0)
            def _():
                _push_start(own, pairs)
        else:
            p_ref, w_ref, m_ref, v_ref, g_ref, d_ref, nm_ref, nv_ref = refs
        g = p_ref[n_parts - 1].astype(F32)
        for k in range(n_parts - 1):
            g = g + p_ref[k].astype(F32)
        mm = ADAM_B1 * m_ref[...] + (1.0 - ADAM_B1) * g
        vv = ADAM_B2 * v_ref[...] + (1.0 - ADAM_B2) * (g * g)
        m_hat = mm / (1.0 - ADAM_B1 ** ADAM_STEP)
        v_hat = vv / (1.0 - ADAM_B2 ** ADAM_STEP)
        d_ref[...] = -ADAM_LR * (m_hat / (jnp.sqrt(v_hat) + ADAM_EPS) + ADAM_WD * w_ref[...])
        g_ref[...] = g
        nm_ref[...] = mm
        nv_ref[...] = vv
        if carry:
            @pl.when(pl.program_id(0) == grid[0] - 1)
            def _():
                _push_wait(own, pairs)

    blk = pl.BlockSpec((tr, tc), idx)
    sds = jax.ShapeDtypeStruct((R, C), F32)
    in_specs = [pl.BlockSpec((n_parts, tr, tc), lambda i: (0,) + idx(i)), blk, blk, blk]
    out_specs = [blk, blk, blk, blk]
    out_shape = [sds, sds, sds, sds]
    operands = [parts, w, m, v]
    scratch = []
    if carry:
        in_specs.append(_ANY)
        out_specs.append(_ANY)
        out_shape.append(jax.ShapeDtypeStruct((N_DEV,) + gathered.shape, gathered.dtype))
        operands.append(gathered)
        scratch = _PUSH_SEMS
    return pl.pallas_call(
        body, name=name,
        grid=grid,
        in_specs=in_specs, out_specs=tuple(out_specs), out_shape=tuple(out_shape),
        scratch_shapes=scratch,
        compiler_params=_cparams(("arbitrary",)),
    )(*operands)


def _local_step(x, target, wt, wr, wdec, bdec, wp_shard, norm_g, gla_g, b_gate, final_g):
    D = x.shape[1]
    half = wp_shard.shape[1] // 2
    projf, projb, rank, ht, wp_lo = _proj_call(x, norm_g, wt, wr, wp_shard[:, :half])
    o_gla, st_all, la = _gla_fwd_call(projf, projb, rank, wdec, bdec)
    o_sb, wp_hi = _sb_fwd_call(projb, wp_shard[:, half:])
    wp_full = jnp.concatenate([wp_lo, wp_hi], axis=2).transpose(1, 0, 2, 3).reshape(3, D, D)
    (dx2, do_gla, do_sb, dggate, dsgate, dmlog, mt, ogt, obt, dx2b, dya, dyb,
     dfinal_g, db_gate, dgla_g, loss) = _mid_call(o_gla, o_sb, projf, x, target, wp_full[0], wp_full[1],
                                                 wp_full[2], gla_g, b_gate, final_g)
    dw_p = _wgrad_call([ogt, obt, mt], [0, 1, 2], [dya, dyb, dx2b], [(0, 0), (1, 0), (2, 0)], 0, "wgrad_p")
    g_p = dw_p.reshape(3, N_DEV, D // N_DEV, D).transpose(1, 0, 2, 3).reshape(N_DEV, 3 * (D // N_DEV), D)
    dqk, dgv, drank, dwdec, dbdec = _gla_bwd_call(projf, projb, la, do_gla, st_all, rank, wdec)
    dsq, dsk, dsv, r_p = _sb_bwd_call(projb, do_sb, g_p)
    pieces = [dqk, dgv, dggate, dsq, dsk, dsv, dsgate]
    rhs_of_group = [(g, 0) for g in range(7)] + [(7, 0), (7, 1)]
    dw_in = _wgrad_call([ht], [0] * N_GROUPS, pieces + [dmlog], rhs_of_group, N_GROUPS, "wgrad_in")
    dwr = _wgrad_rank_call(ht, drank)
    g_in = _parts_by_device(dw_in.reshape(N_GROUPS * 1024, D), dwr[:, :GLA_RANK].T.astype(BF16))
    c_idx = lax.axis_index("c").astype(jnp.int32).reshape(1)
    (p_in,) = _pair_exchange([g_in], "pair_g")
    s_in = _pair_add_call(g_in, p_in, c_idx, "pair_add_in")
    grad_x, dnorm_g, r_in = _dh_call(pieces, dmlog, drank, wt, wr, x, dx2, norm_g, s_in)
    small = jnp.concatenate([
        dnorm_g.reshape(-1), dbdec.reshape(-1), dgla_g.reshape(-1), db_gate.reshape(-1), dfinal_g.reshape(-1),
        loss.reshape(-1), dwdec[:GLA_RANK].reshape(-1)]).reshape(1, _SM_LEN)
    return grad_x, r_in, r_p, small


def _parts_by_device(dmain, drank):
    def part_for(p):
        lo, hi = p * SHARD_COLS, (p + 1) * SHARD_COLS
        pieces = []
        if lo < RANK_COL:
            pieces.append(dmain[lo:min(hi, RANK_COL)])
        if lo < RANK_COL + GLA_RANK and hi > RANK_COL:
            pieces.append(drank[max(lo, RANK_COL) - RANK_COL:min(hi, RANK_COL + GLA_RANK) - RANK_COL])
        if hi > RANK_COL + GLA_RANK:
            pieces.append(dmain[max(lo, RANK_COL + GLA_RANK) - GLA_RANK:hi - GLA_RANK])
        return pieces[0] if len(pieces) == 1 else jnp.concatenate(pieces, axis=0)

    return jnp.stack([part_for(p) for p in range(N_DEV)])


_SM_NORM = 0
_SM_BDEC = _SM_NORM + D_MODEL
_SM_GLAG = _SM_BDEC + GLA_DK
_SM_BGATE = _SM_GLAG + GLA_HV
_SM_FINAL = _SM_BGATE + 2 * D_MODEL
_SM_REPL = _SM_FINAL + D_MODEL
_SM_LOSS = _SM_REPL
_SM_WDEC = _SM_LOSS + 128
_SM_LEN = _SM_WDEC + GLA_RANK * GLA_DK


def kernel(x, norm_g, w_in, w_dec_up, b_dec, gla_norm_g, w_pa, w_pb, b_gate, w_o, final_g, loss_target, m_norm_g, m_w_in, m_w_dec_up, m_b_dec, m_gla_norm_g, m_w_pa, m_w_pb, m_b_gate, m_w_o, m_final_g, v_norm_g, v_w_in, v_w_dec_up, v_b_dec, v_gla_norm_g, v_w_pa, v_w_pb, v_b_gate, v_w_o, v_final_g):
    D = D_MODEL
    me = 4 * lax.axis_index("x") + 2 * lax.axis_index("y") + lax.axis_index("c")

    wp_shard = jnp.stack([w_pa, w_pb, w_o]).astype(BF16)
    win_all, wdec_all = _all_gather([w_in.T.astype(BF16), w_dec_up], "gather_w")
    wt = win_all.reshape(IN_COLS, D)
    wr = jnp.pad(wt[RANK_COL:RANK_COL + GLA_RANK], ((0, 128 - GLA_RANK), (0, 0)))
    wdec_full = wdec_all.transpose(1, 0, 2).reshape(GLA_RANK, GLA_DK)
    wdec = jnp.pad(wdec_full, ((0, 128 - GLA_RANK), (0, 0)))

    grad_x, r_in, r_p, small = _local_step(
        x[0], loss_target[0], wt, wr, wdec, b_dec.reshape(1, -1), wp_shard,
        norm_g.reshape(1, -1), gla_norm_g.reshape(1, -1), b_gate.reshape(1, -1), final_g.reshape(1, -1))

    gw_in, d_in, nm_in, nv_in = (a.T for a in _adamw_call(r_in, w_in.T, m_w_in.T, v_w_in.T, "adamw_in"))
    wp_f32 = jnp.concatenate([w_pa, w_pb, w_o], axis=0)
    mp = jnp.concatenate([m_w_pa, m_w_pb, m_w_o], axis=0)
    vp = jnp.concatenate([v_w_pa, v_w_pb, v_w_o], axis=0)
    gp, dp, nmp, nvp, r_small = _adamw_call(r_p, wp_f32, mp, vp, "adamw_p", gathered=small)
    rows = D // N_DEV

    def split3(a):
        return a[:rows], a[rows:2 * rows], a[2 * rows:]

    g_pa, g_pb, g_o = split3(gp)
    d_pa, d_pb, d_o = split3(dp)
    nm_pa, nm_pb, nm_o = split3(nmp)
    nv_pa, nv_pb, nv_o = split3(nvp)

    w_rep = jnp.concatenate([norm_g, b_dec, gla_norm_g, b_gate, final_g]).reshape(1, _SM_REPL)
    m_rep = jnp.concatenate([m_norm_g, m_b_dec, m_gla_norm_g, m_b_gate, m_final_g]).reshape(1, _SM_REPL)
    v_rep = jnp.concatenate([v_norm_g, v_b_dec, v_gla_norm_g, v_b_gate, v_final_g]).reshape(1, _SM_REPL)
    g_rep, d_rep, nm_rep, nv_rep = _adamw_call(r_small[:, :, :_SM_REPL], w_rep, m_rep, v_rep, "adamw_rep")

    def split_rep(a):
        a = a.reshape(-1)
        return (a[_SM_NORM:_SM_BDEC], a[_SM_BDEC:_SM_GLAG], a[_SM_GLAG:_SM_BGATE],
                a[_SM_BGATE:_SM_FINAL], a[_SM_FINAL:_SM_REPL])

    g_norm, g_bdec, g_glag, g_bgate, g_final = split_rep(g_rep)
    d_norm, d_bdec, d_glag, d_bgate, d_final = split_rep(d_rep)
    nm_norm, nm_bdec, nm_glag, nm_bgate, nm_final = split_rep(nm_rep)
    nv_norm, nv_bdec, nv_glag, nv_bgate, nv_final = split_rep(nv_rep)

    wdec_parts = r_small[:, 0, _SM_WDEC:].reshape(N_DEV, GLA_RANK, GLA_DK)
    cols = GLA_DK // N_DEV
    wdec_mine = lax.dynamic_slice_in_dim(wdec_parts, me * cols, cols, axis=2)
    g_wdec, d_wdec, nm_wdec, nv_wdec = _adamw_call(wdec_mine, w_dec_up, m_w_dec_up, v_w_dec_up, "adamw_dec")

    loss_total = jnp.sum(r_small[:, 0, _SM_LOSS])

    return (loss_total, grad_x[None],
            g_norm, gw_in, g_wdec, g_bdec, g_glag, g_pa, g_pb, g_bgate, g_o, g_final,
            d_norm, d_in, d_wdec, d_bdec, d_glag, d_pa, d_pb, d_bgate, d_o, d_final,
            nm_norm, nm_in, nm_wdec, nm_bdec, nm_glag, nm_pa, nm_pb, nm_bgate, nm_o, nm_final,
            nv_norm, nv_in, nv_wdec, nv_bdec, nv_glag, nv_pa, nv_pb, nv_bgate, nv_o, nv_final)
```

```python
import functools
import math

import jax
import jax.numpy as jnp
from jax import lax
from jax.experimental import pallas as pl
from jax.experimental.pallas import tpu as pltpu

F32 = jnp.float32
BF16 = jnp.bfloat16

N_DEV = 8
D_MODEL = 1024
GLA_HEADS = 4
GLA_HK = 128
GLA_HV = 256
GLA_DK = 512
GLA_RANK = 16
GLA_TAU = 16.0
GLA_CHUNK = 64
SB_HEADS = 8
SB_HD = 128
SB_BLOCK = 128
EPS = 1e-6
N_GROUPS = 9
RANK_COL = 3072
IN_COLS = 9232
SHARD_COLS = IN_COLS // N_DEV

ADAM_LR = 0.001
ADAM_B1 = 0.9
ADAM_B2 = 0.999
ADAM_EPS = 1e-08
ADAM_WD = 0.01
ADAM_STEP = 10

VMEM_LIMIT = 56 * 1024 * 1024
TBLK = 256


def _cparams(sem=None):
    return pltpu.CompilerParams(dimension_semantics=sem, vmem_limit_bytes=VMEM_LIMIT)


def _tiling_2d(rows, cols):
    if rows * cols <= 128 * 1024:
        return (rows, cols), (1,), lambda i: (0, 0)
    if rows % 128 == 0:
        return (128, cols), (rows // 128,), lambda i: (i, 0)
    tc = 256 if cols % 256 == 0 else cols
    return (rows, tc), (cols // tc,), lambda i: (0, i)


def _dot(a, b):
    return jnp.dot(a, b, preferred_element_type=F32)


def _dot_nt(a, b):
    return lax.dot_general(a, b, (((1,), (1,)), ((), ())), preferred_element_type=F32)


def _dot_tn(a, b):
    return lax.dot_general(a, b, (((0,), (0,)), ((), ())), preferred_element_type=F32)


def _bf(x):
    return x.astype(BF16)


def _split3(x):
    hi = x.astype(BF16)
    r = x - hi.astype(F32)
    mid = r.astype(BF16)
    lo = (r - mid.astype(F32)).astype(BF16)
    return hi, mid, lo


def _tri_left(tri, x):
    hi, mid, lo = _split3(x)
    return _dot(tri, hi) + _dot(tri, mid) + _dot(tri, lo)


def _split2(x):
    hi = lax.bitcast_convert_type(lax.bitcast_convert_type(x, jnp.uint32) & jnp.uint32(0xFFFF0000), F32)
    return hi.astype(BF16), (x - hi).astype(BF16)


def _tri2_left(tri, x):
    hi, lo = _split2(x)
    return _dot(tri, hi) + _dot(tri, lo)


def _tri2_right(x, tri):
    hi, lo = _split2(x)
    return _dot(hi, tri) + _dot(lo, tri)


def _iota2(n, m, dim):
    return lax.broadcasted_iota(jnp.int32, (n, m), dim)


def _sigmoid(x):
    return 1.0 / (1.0 + jnp.exp(-x))


def _softplus_neg_abs(z):
    return jnp.log(1.0 + jnp.exp(-jnp.abs(z)))


_ANY = pl.BlockSpec(memory_space=pl.ANY)


def _mesh_pos():
    return lax.axis_index("x"), lax.axis_index("y"), lax.axis_index("c")


def _other_chips(x, y):
    return [(1 - x, y), (x, 1 - y), (1 - x, 1 - y)]


def _rcopy(src, dst, send_sem, recv_sem, to):
    return pltpu.make_async_remote_copy(src_ref=src, dst_ref=dst, send_sem=send_sem, recv_sem=recv_sem,
                                        device_id=to, device_id_type=pl.DeviceIdType.MESH)


def _push_copies(src_ref, dst_ref, send_sems, recv_sems, loc_sem, scatter):
    x, y, c = _mesh_pos()
    me = 4 * x + 2 * y + c
    own = pltpu.make_async_copy(src_ref.at[me] if scatter else src_ref, dst_ref.at[me], loc_sem)
    pairs = []
    for k in range(1, N_DEV):
        px = 1 - x if k & 4 else x
        py = 1 - y if k & 2 else y
        pc = 1 - c if k & 1 else c
        pid = 4 * px + 2 * py + pc
        src = src_ref.at[pid] if scatter else src_ref
        send = _rcopy(src, dst_ref.at[me], send_sems.at[k - 1], recv_sems.at[k - 1], (px, py, pc))
        recv = _rcopy(src, dst_ref.at[pid], send_sems.at[k - 1], recv_sems.at[k - 1], (px, py, pc))
        pairs.append((send, recv))
    return own, pairs


def _push_start(own, pairs):
    own.start()
    for send, _ in pairs:
        send.start()


def _push_wait(own, pairs):
    for _, recv in pairs:
        recv.wait_recv()
    for send, _ in pairs:
        send.wait_send()
    own.wait()


_PUSH_SEMS = [pltpu.SemaphoreType.DMA((N_DEV - 1,)), pltpu.SemaphoreType.DMA((N_DEV - 1,)),
              pltpu.SemaphoreType.DMA]


def _chip_copies(src_ref, dst_ref, send_sems, recv_sems, loc_sem):
    x, y, c = _mesh_pos()
    own = pltpu.make_async_copy(src_ref.at[2 * x + y], dst_ref.at[3], loc_sem)
    pairs = []
    for j, (px, py) in enumerate(_other_chips(x, y)):
        cp = _rcopy(src_ref.at[2 * px + py], dst_ref.at[j], send_sems.at[j], recv_sems.at[j], (px, py, c))
        pairs.append((cp, cp))
    return own, pairs


_CHIP_SEMS = [pltpu.SemaphoreType.DMA((3,)), pltpu.SemaphoreType.DMA((3,)), pltpu.SemaphoreType.DMA]


def _all_gather(arrs, name):
    n = len(arrs)

    def body(*refs):
        ins = refs[:n]
        outs = refs[n:2 * n]
        send_sems, recv_sems, loc_sems = refs[2 * n:]
        x, y, c = _mesh_pos()
        sib = (x, y, 1 - c)
        chips = _other_chips(x, y)

        def place(a, px, py, pc):
            return outs[a].at[4 * px + 2 * py + pc]

        def copy(a, k, block, to, src=None):
            dst = place(a, *block)
            return _rcopy(dst if src is None else src, dst, send_sems.at[a, k], recv_sems.at[a, k], to)

        mine = [pltpu.make_async_copy(ins[a], place(a, x, y, c), loc_sems.at[a]) for a in range(n)]
        for cp in mine:
            cp.start()
        first = [copy(a, 0, (x, y, c), sib, src=ins[a]) for a in range(n)]
        for j, chip in enumerate(chips):
            first += [copy(a, 1 + j, (x, y, c), (*chip, c), src=ins[a]) for a in range(n)]
        for cp in first:
            cp.start()
        passed = []
        for j, chip in enumerate(chips):
            for a in range(n):
                copy(a, 1 + j, (*chip, c), (x, y, c)).wait_recv()
                fwd = copy(a, 4 + j, (*chip, c), sib)
                fwd.start()
                passed.append(fwd)
        for a in range(n):
            copy(a, 0, sib, (x, y, c)).wait_recv()
        for j, chip in enumerate(chips):
            for a in range(n):
                copy(a, 4 + j, (*chip, 1 - c), (x, y, c)).wait_recv()
        for cp in first + passed:
            cp.wait_send()
        for cp in mine:
            cp.wait()

    return pl.pallas_call(
        body, name=name,
        out_shape=tuple(jax.ShapeDtypeStruct((N_DEV,) + a.shape, a.dtype) for a in arrs),
        in_specs=[_ANY] * n,
        out_specs=tuple([_ANY] * n),
        scratch_shapes=[pltpu.SemaphoreType.DMA((n, 7)), pltpu.SemaphoreType.DMA((n, 7)),
                        pltpu.SemaphoreType.DMA((n,))],
    )(*arrs)


def _pair_exchange(arrs, name):
    n = len(arrs)

    def body(*refs):
        ins = refs[:n]
        outs = refs[n:2 * n]
        send_sems, recv_sems = refs[2 * n:]
        x, y, c = _mesh_pos()
        copies = []
        for a in range(n):
            for q in range(4):
                cp = _rcopy(ins[a].at[2 * q + (1 - c)], outs[a].at[q], send_sems.at[a, q], recv_sems.at[a, q],
                            (x, y, 1 - c))
                cp.start()
                copies.append(cp)
        for cp in copies:
            cp.wait_recv()
        for cp in copies:
            cp.wait_send()

    return pl.pallas_call(
        body, name=name,
        out_shape=tuple(jax.ShapeDtypeStruct((4,) + a.shape[1:], a.dtype) for a in arrs),
        in_specs=[_ANY] * n,
        out_specs=tuple([_ANY] * n),
        scratch_shapes=[pltpu.SemaphoreType.DMA((n, 4)), pltpu.SemaphoreType.DMA((n, 4))],
    )(*arrs)


def _pair_add_call(parts, recv, c_idx, name):
    _, R, C = parts.shape
    (tr, tc), (steps,), idx = _tiling_2d(R, C)

    def body(c_ref, p_ref, r_ref, o_ref):
        o_ref[...] = (p_ref[...].astype(F32) + r_ref[...].astype(F32)).astype(o_ref.dtype)

    return pl.pallas_call(
        body, name=name,
        grid_spec=pltpu.PrefetchScalarGridSpec(
            num_scalar_prefetch=1,
            grid=(4, steps),
            in_specs=[pl.BlockSpec((None, tr, tc), lambda q, i, c_ref: (2 * q + c_ref[0],) + idx(i)),
                      pl.BlockSpec((None, tr, tc), lambda q, i, c_ref: (q,) + idx(i))],
            out_specs=pl.BlockSpec((None, tr, tc), lambda q, i, c_ref: (q,) + idx(i))),
        out_shape=jax.ShapeDtypeStruct((4, R, C), parts.dtype),
        compiler_params=_cparams(("arbitrary", "arbitrary")),
    )(c_idx, parts, recv)


def _group_row(g):
    return GLA_RANK * (g * (1024 // GLA_RANK) + (g >= RANK_COL // 1024))


def _proj_call(x, norm_g, wt, wr, wp_part):
    T, D = x.shape
    tm = min(1024, T)
    assert tm % TBLK == 0
    n_i = T // tm

    def f_slot(j):
        return ((j >= 2).astype(jnp.int32) + (j >= 6).astype(jnp.int32)
                + (j >= 7).astype(jnp.int32) + (j >= 8).astype(jnp.int32))

    def b_slot(j):
        return (j >= 3).astype(jnp.int32) + (j >= 4).astype(jnp.int32) + (j >= 5).astype(jnp.int32)

    def body(x_ref, g_ref, w_ref, wr_ref, wp_ref, pf_ref, pb_ref, rank_ref, ht_ref, wpall_ref,
             h_scr, send_sems, recv_sems, loc_sem):
        i = pl.program_id(0)
        j = pl.program_id(1)
        own, pairs = _push_copies(wp_ref, wpall_ref, send_sems, recv_sems, loc_sem, scatter=False)

        @pl.when((i == 0) & (j == 0))
        def _():
            _push_start(own, pairs)

        @pl.when(j == 0)
        def _():
            xv = x_ref[...]
            r = lax.rsqrt(jnp.mean(xv * xv, axis=-1, keepdims=True) + EPS)
            h = (xv * r) * g_ref[...]
            hb = _bf(h)
            h_scr[...] = hb
            for b in range(tm // TBLK):
                ht_ref[b] = _bf(h[b * TBLK:(b + 1) * TBLK].T)
            rank_ref[...] = _dot_nt(hb, wr_ref[...])

        is_b = (j == 1) | ((j >= 3) & (j <= 5))

        @pl.when(is_b)
        def _():
            pb_ref[...] = _bf(_dot_nt(h_scr[...], w_ref[...]))

        @pl.when(jnp.logical_not(is_b))
        def _():
            pf_ref[...] = _dot_nt(h_scr[...], w_ref[...])

        @pl.when((i == n_i - 1) & (j == N_GROUPS - 1))
        def _():
            _push_wait(own, pairs)

    return pl.pallas_call(
        body, name="proj",
        grid=(n_i, N_GROUPS),
        in_specs=[pl.BlockSpec((tm, D), lambda i, j: (i, 0)),
                  pl.BlockSpec((1, D), lambda i, j: (0, 0)),
                  pl.BlockSpec((pl.Element(1024), pl.Element(D)), lambda i, j: (_group_row(j), 0)),
                  pl.BlockSpec((128, D), lambda i, j: (0, 0)),
                  _ANY],
        out_specs=(pl.BlockSpec((None, tm, 1024), lambda i, j: (f_slot(j), i, 0)),
                   pl.BlockSpec((None, tm, 1024), lambda i, j: (b_slot(j), i, 0)),
                   pl.BlockSpec((tm, 128), lambda i, j: (i, 0)),
                   pl.BlockSpec((tm // TBLK, D, TBLK), lambda i, j: (i, 0, 0)),
                   _ANY),
        out_shape=(jax.ShapeDtypeStruct((5, T, 1024), F32),
                   jax.ShapeDtypeStruct((4, T, 1024), BF16),
                   jax.ShapeDtypeStruct((T, 128), F32),
                   jax.ShapeDtypeStruct((T // TBLK, D, TBLK), BF16),
                   jax.ShapeDtypeStruct((N_DEV,) + wp_part.shape, wp_part.dtype)),
        scratch_shapes=[pltpu.VMEM((tm, D), BF16)] + _PUSH_SEMS,
        compiler_params=_cparams(("arbitrary", "arbitrary")),
    )(x, norm_g, wt, wr, wp_part)


def _gla_chunk_terms(la_h, q, k):
    C = GLA_CHUNK
    low = _bf((_iota2(C, C, 0) >= _iota2(C, C, 1)).astype(F32))
    b = _tri_left(low, la_h)
    bl = b[C - 1:C, :]
    eb = jnp.exp(b)
    enb = jnp.exp(-b)
    ebl_b = jnp.exp(bl - b)
    scale = GLA_HK ** -0.5
    qe = q * eb * scale
    ke = k * enb
    kd = k * ebl_b
    return b, bl, eb, enb, ebl_b, qe, ke, kd


def _gla_fwd_call(projf, projb, rank, wdec, bdec):
    T = projf.shape[1]
    C = GLA_CHUNK
    n_chunks = T // C

    def body(qk_ref, v_ref, rank_ref, wd_ref, bd_ref, o_ref, st_ref, la_ref, st_scr):
        @pl.when(pl.program_id(0) == 0)
        def _():
            st_scr[...] = jnp.zeros_like(st_scr)

        dec = _dot(_bf(rank_ref[...]), _bf(wd_ref[...])) + bd_ref[...]
        la = (jnp.minimum(dec, 0.0) - _softplus_neg_abs(dec)) / GLA_TAU
        la_ref[...] = la
        mask = _iota2(C, C, 0) >= _iota2(C, C, 1)
        _, bl, _, _, _, qe, ke, kd = _gla_chunk_terms(la, qk_ref[:, :GLA_DK], qk_ref[:, GLA_DK:])
        qeb, keb, kdb = _bf(qe), _bf(ke), _bf(kd)
        ebl = jnp.exp(bl)
        heads = range(GLA_HEADS)
        ks = [slice(hh * GLA_HK, (hh + 1) * GLA_HK) for hh in heads]
        vs = [slice(hh * GLA_HV, (hh + 1) * GLA_HV) for hh in heads]
        st = [st_scr[hh] for hh in heads]
        p = [_bf(jnp.where(mask, _dot_nt(qeb[:, ks[hh]], keb[:, ks[hh]]), 0.0)) for hh in heads]
        inter = [_dot_nt(qeb[:, ks[hh]], _bf(st[hh])) for hh in heads]
        upd = [_dot_tn(v_ref[:, vs[hh]], kdb[:, ks[hh]]) for hh in heads]
        intra = [_dot(p[hh], v_ref[:, vs[hh]]) for hh in heads]
        for hh in heads:
            st_ref[hh] = st[hh]
            o_ref[:, vs[hh]] = intra[hh] + inter[hh]
            st_scr[hh] = st[hh] * ebl[:, ks[hh]] + upd[hh]

    return pl.pallas_call(
        body, name="gla_fwd",
        grid=(n_chunks,),
        in_specs=[pl.BlockSpec((None, C, 1024), lambda n: (0, n, 0)),
                  pl.BlockSpec((None, C, 1024), lambda n: (0, n, 0)),
                  pl.BlockSpec((C, 128), lambda n: (n, 0)),
                  pl.BlockSpec((128, GLA_DK), lambda n: (0, 0)),
                  pl.BlockSpec((1, GLA_DK), lambda n: (0, 0))],
        out_specs=(pl.BlockSpec((C, 1024), lambda n: (n, 0)),
                   pl.BlockSpec((None, GLA_HEADS, GLA_HV, GLA_HK), lambda n: (n, 0, 0, 0)),
                   pl.BlockSpec((C, GLA_DK), lambda n: (n, 0))),
        out_shape=(jax.ShapeDtypeStruct((T, 1024), F32),
                   jax.ShapeDtypeStruct((n_chunks, GLA_HEADS, GLA_HV, GLA_HK), F32),
                   jax.ShapeDtypeStruct((T, GLA_DK), F32)),
        scratch_shapes=[pltpu.VMEM((GLA_HEADS, GLA_HV, GLA_HK), F32)],
        compiler_params=_cparams(("arbitrary",)),
    )(projf, projb, rank, wdec, bdec)


def _gla_bwd_call(projf, projb, la, do_gla, st_all, rank, wdec):
    T = projf.shape[1]
    C = GLA_CHUNK
    n_chunks = T // C
    last = n_chunks - 1

    def body(qk_ref, v_ref, la_ref, do_ref, st_ref, rank_ref, wd_ref,
             dqk_ref, dv_ref, drank_ref, dwd_ref, dbd_ref, dst_scr):
        @pl.when(pl.program_id(0) == 0)
        def _():
            dst_scr[...] = jnp.zeros_like(dst_scr)
            dwd_ref[...] = jnp.zeros_like(dwd_ref)
            dbd_ref[...] = jnp.zeros_like(dbd_ref)

        mask = _iota2(C, C, 0) >= _iota2(C, C, 1)
        upp = _bf((_iota2(C, C, 0) <= _iota2(C, C, 1)).astype(F32))
        scale = GLA_HK ** -0.5
        la = la_ref[...]
        _, bl, eb, enb, ebl_b, qe, ke, kd = _gla_chunk_terms(la, qk_ref[:, :GLA_DK], qk_ref[:, GLA_DK:])
        qeb, keb, kdb = _bf(qe), _bf(ke), _bf(kd)
        ebl = jnp.exp(bl)
        heads = range(GLA_HEADS)
        ks = [slice(hh * GLA_HK, (hh + 1) * GLA_HK) for hh in heads]
        vs = [slice(hh * GLA_HV, (hh + 1) * GLA_HV) for hh in heads]
        v = [v_ref[:, vs[hh]] for hh in heads]
        do = [_bf(do_ref[:, vs[hh]]) for hh in heads]
        st = [st_ref[hh] for hh in heads]
        dstn = [dst_scr[hh] for hh in heads]
        dstnb = [_bf(dstn[hh]) for hh in heads]
        p = [_bf(jnp.where(mask, _dot_nt(qeb[:, ks[hh]], keb[:, ks[hh]]), 0.0)) for hh in heads]
        dp = [_bf(jnp.where(mask, _dot_nt(do[hh], v[hh]), 0.0)) for hh in heads]
        dkd = [_dot(v[hh], dstnb[hh]) for hh in heads]
        dv_inter = [_dot_nt(kdb[:, ks[hh]], dstnb[hh]) for hh in heads]
        dqe_inter = [_dot(do[hh], _bf(st[hh])) for hh in heads]
        dst_new = [_dot_tn(do[hh], qeb[:, ks[hh]]) + dstn[hh] * ebl[:, ks[hh]] for hh in heads]
        debl = jnp.concatenate([jnp.sum(dstn[hh] * st[hh], axis=0, keepdims=True) for hh in heads], axis=1)
        dv = [_dot_tn(p[hh], do[hh]) + dv_inter[hh] for hh in heads]
        dqe = jnp.concatenate([_dot(dp[hh], keb[:, ks[hh]]) + dqe_inter[hh] for hh in heads], axis=1)
        dke = jnp.concatenate([_dot_tn(dp[hh], qeb[:, ks[hh]]) for hh in heads], axis=1)
        dkd = jnp.concatenate(dkd, axis=1)
        for hh in heads:
            dst_scr[hh] = dst_new[hh]
            dv_ref[:, vs[hh]] = _bf(dv[hh])
        dkd_kd = dkd * kd
        db = dqe * qe - dke * ke - dkd_kd
        dbl = jnp.sum(dkd_kd, axis=0, keepdims=True) + ebl * debl
        dla = _tri_left(upp, db) + dbl
        dqk_ref[:, :GLA_DK] = _bf(dqe * eb * scale)
        dqk_ref[:, GLA_DK:] = _bf(dke * enb + dkd * ebl_b)
        ddec = dla * (1.0 / GLA_TAU) * (1.0 - jnp.exp(GLA_TAU * la))
        ddecb = _bf(ddec)
        drank_ref[...] = _bf(_dot_nt(ddecb, _bf(wd_ref[...])))
        dwd_ref[...] += _dot_tn(_bf(rank_ref[...]), ddecb)
        dbd_ref[...] += jnp.sum(ddec, axis=0, keepdims=True)

    return pl.pallas_call(
        body, name="gla_bwd",
        grid=(n_chunks,),
        in_specs=[pl.BlockSpec((None, C, 1024), lambda n: (0, last - n, 0)),
                  pl.BlockSpec((None, C, 1024), lambda n: (0, last - n, 0)),
                  pl.BlockSpec((C, GLA_DK), lambda n: (last - n, 0)),
                  pl.BlockSpec((C, 1024), lambda n: (last - n, 0)),
                  pl.BlockSpec((None, GLA_HEADS, GLA_HV, GLA_HK), lambda n: (last - n, 0, 0, 0)),
                  pl.BlockSpec((C, 128), lambda n: (last - n, 0)),
                  pl.BlockSpec((128, GLA_DK), lambda n: (0, 0))],
        out_specs=(pl.BlockSpec((C, 1024), lambda n: (last - n, 0)),
                   pl.BlockSpec((C, 1024), lambda n: (last - n, 0)),
                   pl.BlockSpec((C, 128), lambda n: (last - n, 0)),
                   pl.BlockSpec((128, GLA_DK), lambda n: (0, 0)),
                   pl.BlockSpec((1, GLA_DK), lambda n: (0, 0))),
        out_shape=(jax.ShapeDtypeStruct((T, 1024), BF16),
                   jax.ShapeDtypeStruct((T, 1024), BF16),
                   jax.ShapeDtypeStruct((T, 128), BF16),
                   jax.ShapeDtypeStruct((128, GLA_DK), F32),
                   jax.ShapeDtypeStruct((1, GLA_DK), F32)),
        scratch_shapes=[pltpu.VMEM((GLA_HEADS, GLA_HV, GLA_HK), F32)],
        compiler_params=_cparams(("arbitrary",)),
    )(projf, projb, la, do_gla, st_all, rank, wdec)


def _sb_logs(z):
    lsz = jnp.minimum(z, 0.0) - _softplus_neg_abs(z)
    return lsz, lsz - z


SB_HG_FWD = 8
SB_HG_BWD = 4
SB_QUERIES_BWD = 256
SB_KEYS = 256
SB_DEAD = -105.0


def _sb_fwd_call(projb, wp_shard):
    T = projb.shape[1]
    B = SB_BLOCK
    HG = SB_HG_FWD
    W = HG * SB_HD
    scale = 1.0 / math.sqrt(SB_HD)
    KB = min(SB_KEYS, T)
    n_h, n_i = SB_HEADS // HG, T // B

    def body(q_ref, k_ref, v_ref, wp_ref, o_ref, wpall_ref, cb_scr, send_sems, recv_sems, loc_sem):
        i = pl.program_id(1)
        own, pairs = _push_copies(wp_ref, wpall_ref, send_sems, recv_sems, loc_sem, scatter=False)

        @pl.when((pl.program_id(0) == 0) & (i == 0))
        def _():
            _push_start(own, pairs)

        rows = HG * B
        after = (_iota2(KB, KB, 0) > _iota2(KB, KB, 1)).astype(F32)
        tri = _bf(jnp.concatenate([after, jnp.ones((KB, KB), F32)], axis=1))
        o_ref[...] = jnp.zeros_like(o_ref)
        cb_scr[...] = jnp.zeros_like(cb_scr)

        def block(jp, masked):
            off = pl.multiple_of(jp * KB, KB)
            z = jnp.concatenate(
                [_dot_nt(q_ref[:, hh * SB_HD:(hh + 1) * SB_HD], k_ref[pl.ds(off, KB), hh * SB_HD:(hh + 1) * SB_HD])
                 for hh in range(HG)], axis=0) * scale
            lsz, l1m = _sb_logs(z)
            if masked:
                strict = (jp * KB + _iota2(rows, KB, 1)) < (i * B + (_iota2(rows, KB, 0) & (B - 1)))
                l1m = jnp.where(strict, l1m, 0.0)
            r = _tri2_right(l1m, tri)
            cb = cb_scr[...]
            a = jnp.exp(lsz + cb + r[:, :KB])
            if masked:
                a = jnp.where(strict, a, 0.0)
            cb_scr[...] = cb + r[:, KB:]
            ab = _bf(a)
            for hh in range(HG):
                cs = slice(hh * SB_HD, (hh + 1) * SB_HD)
                o_ref[:, cs] += _dot(ab[hh * B:(hh + 1) * B, :], v_ref[pl.ds(off, KB), cs])

        jp0 = (i * B) // KB
        block(jp0, True)

        def live(state):
            jj, dead = state
            return (jj <= jp0) & jnp.logical_not(dead)

        def step(state):
            jj, _ = state
            block(jp0 - jj, False)
            return jj + 1, jnp.max(cb_scr[:, :B]) < SB_DEAD

        lax.while_loop(live, step, (jnp.int32(1), jnp.max(cb_scr[:, :B]) < SB_DEAD))

        @pl.when((pl.program_id(0) == n_h - 1) & (i == n_i - 1))
        def _():
            _push_wait(own, pairs)

    return pl.pallas_call(
        body, name="sb_fwd",
        grid=(n_h, n_i),
        in_specs=[pl.BlockSpec((None, B, W), lambda h, i: (1, i, h)),
                  pl.BlockSpec((None, T, W), lambda h, i: (2, 0, h)),
                  pl.BlockSpec((None, T, W), lambda h, i: (3, 0, h)),
                  _ANY],
        out_specs=(pl.BlockSpec((B, W), lambda h, i: (i, h)), _ANY),
        out_shape=(jax.ShapeDtypeStruct((T, 1024), F32),
                   jax.ShapeDtypeStruct((N_DEV,) + wp_shard.shape, wp_shard.dtype)),
        scratch_shapes=[pltpu.VMEM((HG * B, KB), F32)] + _PUSH_SEMS,
        compiler_params=_cparams(("arbitrary", "arbitrary")),
    )(projb, projb, projb, wp_shard)


def _sb_bwd_call(projb, do_sb, g_p):
    T = projb.shape[1]
    B = min(SB_QUERIES_BWD, T)
    nb = T // B
    HG = SB_HG_BWD
    W = HG * SB_HD
    WQ = HG * B
    KB = min(SB_KEYS, T)
    nkb = T // KB
    n_h = SB_HEADS // HG
    scale = 1.0 / math.sqrt(SB_HD)

    def body(q_ref, k_ref, v_ref, do_ref, gp_ref, dq_ref, dk_ref, dv_ref, rp_ref,
             dk_scr, dv_scr, kt_scr, beta_scr, g_scr, dqt_scr, send_sems, recv_sems, loc_sem):
        i = pl.program_id(1)
        own, pairs = _push_copies(gp_ref, rp_ref, send_sems, recv_sems, loc_sem, scatter=True)

        @pl.when((pl.program_id(0) == 0) & (i == 0))
        def _():
            _push_start(own, pairs)

        @pl.when(i == 0)
        def _():
            dk_scr[...] = jnp.zeros_like(dk_scr)
            dv_scr[...] = jnp.zeros_like(dv_scr)
            for hh in range(HG):
                for jb in range(nkb):
                    kt_scr[hh, jb] = _bf(
                        k_ref[jb * KB:(jb + 1) * KB, hh * SB_HD:(hh + 1) * SB_HD].astype(F32).T)

        dqt_scr[...] = jnp.zeros_like(dqt_scr)
        later = _bf((_iota2(KB, KB, 1) > _iota2(KB, KB, 0)).astype(F32))
        earlier = _bf((_iota2(KB, KB, 1) < _iota2(KB, KB, 0)).astype(F32))
        dob = _bf(do_ref[...])
        jp0 = (i * B) // KB

        def strict_mask():
            return (jp0 * KB + _iota2(KB, WQ, 0)) < (i * B + (_iota2(KB, WQ, 1) & (B - 1)))

        def heads(fn):
            return [fn(slice(hh * SB_HD, (hh + 1) * SB_HD)) for hh in range(HG)]

        def pass1(jp, cb, masked):
            off = pl.multiple_of(jp * KB, KB)
            z = jnp.concatenate(heads(lambda cs: _dot_nt(k_ref[pl.ds(off, KB), cs], q_ref[:, cs])), axis=1) * scale
            da = jnp.concatenate(heads(lambda cs: _dot_nt(v_ref[pl.ds(off, KB), cs], dob[:, cs])), axis=1)
            lsz, l1m = _sb_logs(z)
            if masked:
                strict = strict_mask()
                l1m = jnp.where(strict, l1m, 0.0)
            a = jnp.exp(lsz + cb + _tri2_left(later, l1m))
            if masked:
                a = jnp.where(strict, a, 0.0)
            g_scr[jp] = a * da
            beta_scr[jp] = jnp.exp(lsz)
            ab = _bf(a)
            for hh in range(HG):
                cs = slice(hh * SB_HD, (hh + 1) * SB_HD)
                dv_scr[pl.ds(off, KB), cs] += _dot(ab[:, hh * B:(hh + 1) * B], dob[:, cs])
            return cb + jnp.sum(l1m, axis=0, keepdims=True)

        zero = jnp.zeros((1, WQ), F32)
        cb = pass1(jp0, zero, True)

        def live(state):
            jj, _, dead = state
            return (jj <= jp0) & jnp.logical_not(dead)

        def step(state):
            jj, cr, _ = state
            cr = pass1(jp0 - jj, cr, False)
            return jj + 1, cr, jnp.max(cr) < SB_DEAD

        n_done, _, _ = lax.while_loop(live, step, (jnp.int32(1), cb, jnp.max(cb) < SB_DEAD))
        jp_first = jp0 - (n_done - 1)

        def pass2(jp, cg, masked):
            off = pl.multiple_of(jp * KB, KB)
            g = g_scr[jp]
            beta = beta_scr[jp]
            dz = g * (1.0 - beta) - beta * (cg + _tri2_left(earlier, g))
            if masked:
                dz = jnp.where(strict_mask(), dz, 0.0)
            dzb = _bf(dz * scale)
            for hh in range(HG):
                cs = slice(hh * SB_HD, (hh + 1) * SB_HD)
                dk_scr[pl.ds(off, KB), cs] += _dot(dzb[:, hh * B:(hh + 1) * B], q_ref[:, cs])
                dqt_scr[hh] += _dot(kt_scr[hh, jp], dzb[:, hh * B:(hh + 1) * B])
            return cg + jnp.sum(g, axis=0, keepdims=True)

        cg = lax.fori_loop(jp_first, jp0, lambda jp, cr: pass2(jp, cr, False), zero)
        pass2(jp0, cg, True)
        for hh in range(HG):
            dq_ref[:, hh * SB_HD:(hh + 1) * SB_HD] = _bf(dqt_scr[hh].T)

        @pl.when(i == nb - 1)
        def _():
            dk_ref[...] = _bf(dk_scr[...])
            dv_ref[...] = _bf(dv_scr[...])

        @pl.when((pl.program_id(0) == n_h - 1) & (i == nb - 1))
        def _():
            _push_wait(own, pairs)

    return pl.pallas_call(
        body, name="sb_bwd",
        grid=(n_h, nb),
        in_specs=[pl.BlockSpec((None, B, W), lambda h, i: (1, i, h)),
                  pl.BlockSpec((None, T, W), lambda h, i: (2, 0, h)),
                  pl.BlockSpec((None, T, W), lambda h, i: (3, 0, h)),
                  pl.BlockSpec((B, W), lambda h, i: (i, h)),
                  _ANY],
        out_specs=(pl.BlockSpec((B, W), lambda h, i: (i, h)),
                   pl.BlockSpec((T, W), lambda h, i: (0, h)),
                   pl.BlockSpec((T, W), lambda h, i: (0, h)),
                   _ANY),
        out_shape=(jax.ShapeDtypeStruct((T, 1024), BF16),
                   jax.ShapeDtypeStruct((T, 1024), BF16),
                   jax.ShapeDtypeStruct((T, 1024), BF16),
                   jax.ShapeDtypeStruct(g_p.shape, g_p.dtype)),
        scratch_shapes=[pltpu.VMEM((T, W), F32), pltpu.VMEM((T, W), F32),
                        pltpu.VMEM((HG, nkb, SB_HD, KB), BF16),
                        pltpu.VMEM((nkb, KB, WQ), F32), pltpu.VMEM((nkb, KB, WQ), F32),
                        pltpu.VMEM((HG, SB_HD, B), F32)] + _PUSH_SEMS,
        compiler_params=_cparams(("arbitrary", "arbitrary")),
    )(projb, projb, projb, do_sb, g_p)


def _mid_call(o_gla, o_sb, projf, x, target, wpa, wpb, wo, gla_g, b_gate, final_g):
    T, D = x.shape
    tm = min(TBLK, T)

    def body(og_ref, ggate_ref, osb_ref, sgate_ref, ma_ref, mb_ref, x_ref, tgt_ref,
             wpa_ref, wpb_ref, wo_ref, glag_ref, bg_ref, fg_ref,
             dx2_ref, dogla_ref, dosb_ref, dggate_ref, dsgate_ref, dm_ref,
             mt_ref, ogt_ref, obt_ref, dx2b_ref, dya_ref, dyb_ref,
             dfg_ref, dbg_ref, dglag_ref, loss_ref):
        @pl.when(pl.program_id(0) == 0)
        def _():
            dfg_ref[...] = jnp.zeros_like(dfg_ref)
            dbg_ref[...] = jnp.zeros_like(dbg_ref)
            dglag_ref[...] = jnp.zeros_like(dglag_ref)
            loss_ref[...] = jnp.zeros_like(loss_ref)

        glag = glag_ref[...]
        ggate = ggate_ref[...]
        sg = _sigmoid(ggate)
        silu_g = ggate * sg
        ohat, rinv, nrm = [], [], []
        for hh in range(GLA_HEADS):
            oh = og_ref[:, hh * GLA_HV:(hh + 1) * GLA_HV]
            r = lax.rsqrt(jnp.mean(oh * oh, axis=-1, keepdims=True) + EPS)
            ohat.append(oh * r)
            rinv.append(r)
            nrm.append(ohat[-1] * glag)
        n_all = jnp.concatenate(nrm, axis=1)
        og = n_all * silu_g
        ogb = _bf(og)
        ya = _dot(ogb, wpa_ref[...])
        sgate = sgate_ref[...]
        ss = _sigmoid(sgate)
        silu_s = sgate * ss
        osb = osb_ref[...]
        ob = osb * silu_s
        obb = _bf(ob)
        yb = _dot(obb, wpb_ref[...])
        ga = _sigmoid(ma_ref[...] + bg_ref[:, :D])
        gb = _sigmoid(mb_ref[...] + bg_ref[:, D:])
        merged = ga * ya + gb * yb
        mgb = _bf(merged)
        x2 = x_ref[...] + _dot(mgb, wo_ref[...])
        r2 = lax.rsqrt(jnp.mean(x2 * x2, axis=-1, keepdims=True) + EPS)
        xh2 = x2 * r2
        fg = fg_ref[...]
        err = xh2 * fg - tgt_ref[...]
        loss_ref[...] += jnp.broadcast_to(
            0.5 * jnp.sum(jnp.mean(err * err, axis=-1, keepdims=True), axis=0, keepdims=True), (1, 128))
        dy = err * (1.0 / D)
        dfg_ref[...] += jnp.sum(dy * xh2, axis=0, keepdims=True)
        dxh = dy * fg
        dx2 = r2 * (dxh - xh2 * jnp.mean(dxh * xh2, axis=-1, keepdims=True))
        dx2_ref[...] = dx2
        dx2b = _bf(dx2)
        dx2b_ref[...] = dx2b
        dmerged = _dot_nt(dx2b, wo_ref[...])
        dya = dmerged * ga
        dyb = dmerged * gb
        dma = dmerged * ya * ga * (1.0 - ga)
        dmb = dmerged * yb * gb * (1.0 - gb)
        dm_ref[:, :D] = _bf(dma)
        dm_ref[:, D:] = _bf(dmb)
        dbg_ref[:, :D] += jnp.sum(dma, axis=0, keepdims=True)
        dbg_ref[:, D:] += jnp.sum(dmb, axis=0, keepdims=True)
        dyab = _bf(dya)
        dybb = _bf(dyb)
        dya_ref[...] = dyab
        dyb_ref[...] = dybb
        dog = _dot_nt(dyab, wpa_ref[...])
        dob = _dot_nt(dybb, wpb_ref[...])
        dosb_ref[...] = dob * silu_s
        dsgate_ref[...] = _bf(dob * osb * (ss * (1.0 + sgate * (1.0 - ss))))
        dn = dog * silu_g
        dggate_ref[...] = _bf(dog * n_all * (sg * (1.0 + ggate * (1.0 - sg))))
        dglag = jnp.zeros((1, GLA_HV), F32)
        for hh in range(GLA_HEADS):
            dnh = dn[:, hh * GLA_HV:(hh + 1) * GLA_HV]
            dglag = dglag + jnp.sum(dnh * ohat[hh], axis=0, keepdims=True)
            dohat = dnh * glag
            dogla_ref[:, hh * GLA_HV:(hh + 1) * GLA_HV] = rinv[hh] * (
                dohat - ohat[hh] * jnp.mean(dohat * ohat[hh], axis=-1, keepdims=True))
        dglag_ref[...] += dglag
        mt_ref[...] = _bf(merged.T)
        ogt_ref[...] = _bf(og.T)
        obt_ref[...] = _bf(ob.T)

    row = lambda i: (i, 0)
    const = lambda i: (0, 0)
    tile = pl.BlockSpec((tm, D), row)
    tile_t = pl.BlockSpec((None, D, tm), lambda i: (i, 0, 0))
    wspec = pl.BlockSpec((D, D), const)
    return pl.pallas_call(
        body, name="mid",
        grid=(T // tm,),
        in_specs=[tile,
                  pl.BlockSpec((None, tm, D), lambda i: (1, i, 0)),
                  tile,
                  pl.BlockSpec((None, tm, D), lambda i: (2, i, 0)),
                  pl.BlockSpec((None, tm, D), lambda i: (3, i, 0)),
                  pl.BlockSpec((None, tm, D), lambda i: (4, i, 0)),
                  tile, tile, wspec, wspec, wspec,
                  pl.BlockSpec((1, GLA_HV), const),
                  pl.BlockSpec((1, 2 * D), const),
                  pl.BlockSpec((1, D), const)],
        out_specs=(tile, tile, tile, tile, tile,
                   pl.BlockSpec((tm, 2 * D), row),
                   tile_t, tile_t, tile_t, tile, tile, tile,
                   pl.BlockSpec((1, D), const),
                   pl.BlockSpec((1, 2 * D), const),
                   pl.BlockSpec((1, GLA_HV), const),
                   pl.BlockSpec((1, 128), const)),
        out_shape=(jax.ShapeDtypeStruct((T, D), F32),
                   jax.ShapeDtypeStruct((T, D), F32),
                   jax.ShapeDtypeStruct((T, D), F32),
                   jax.ShapeDtypeStruct((T, D), BF16),
                   jax.ShapeDtypeStruct((T, D), BF16),
                   jax.ShapeDtypeStruct((T, 2 * D), BF16),
                   jax.ShapeDtypeStruct((T // tm, D, tm), BF16),
                   jax.ShapeDtypeStruct((T // tm, D, tm), BF16),
                   jax.ShapeDtypeStruct((T // tm, D, tm), BF16),
                   jax.ShapeDtypeStruct((T, D), BF16),
                   jax.ShapeDtypeStruct((T, D), BF16),
                   jax.ShapeDtypeStruct((T, D), BF16),
                   jax.ShapeDtypeStruct((1, D), F32),
                   jax.ShapeDtypeStruct((1, 2 * D), F32),
                   jax.ShapeDtypeStruct((1, GLA_HV), F32),
                   jax.ShapeDtypeStruct((1, 128), F32)),
        compiler_params=_cparams(("arbitrary",)),
    )(o_gla, projf, o_sb, projf, projf, projf, x, target, wpa, wpb, wo, gla_g, b_gate, final_g)


def _dh_call(pieces, dmlog, drank, wt, wr, x, dx2, norm_g, s_in):
    T, D = x.shape
    tm = min(256, T)
    npc = len(pieces)
    n_main = N_GROUPS * 1024
    n_i = T // tm

    def body(*refs):
        pcs = refs[:npc]
        (dm_ref, dr_ref, w_hbm, wr_ref, x_ref, dx2_ref, g_ref, sin_ref,
         gx_ref, dg_ref, rin_ref, w_scr, sems, send_sems, recv_sems, loc_sem) = refs[npc:]
        own, pairs = _chip_copies(sin_ref, rin_ref, send_sems, recv_sems, loc_sem)

        @pl.when(pl.program_id(0) == 0)
        def _():
            _push_start(own, pairs)
            lo = pltpu.make_async_copy(w_hbm.at[pl.ds(0, RANK_COL)], w_scr.at[pl.ds(0, RANK_COL)], sems.at[0])
            hi = pltpu.make_async_copy(w_hbm.at[pl.ds(RANK_COL + GLA_RANK, n_main - RANK_COL)],
                                       w_scr.at[pl.ds(RANK_COL, n_main - RANK_COL)], sems.at[1])
            lo.start()
            hi.start()
            dg_ref[...] = jnp.zeros_like(dg_ref)
            lo.wait()
            hi.wait()

        def w_group(g):
            return w_scr[g * 1024:(g + 1) * 1024, :]

        dr = dr_ref[...]
        dh = _dot(dr, wr_ref[...])
        for g in range(npc):
            dh = dh + _dot(pcs[g][...], w_group(g))
        dh = dh + _dot(dm_ref[:, :D], w_group(npc))
        dh = dh + _dot(dm_ref[:, D:], w_group(npc + 1))
        xv = x_ref[...]
        r = lax.rsqrt(jnp.mean(xv * xv, axis=-1, keepdims=True) + EPS)
        xhat = xv * r
        g = g_ref[...]
        dg_ref[...] += jnp.sum(dh * xhat, axis=0, keepdims=True)
        dxhat = dh * g
        gx_ref[...] = r * (dxhat - xhat * jnp.mean(dxhat * xhat, axis=-1, keepdims=True)) + dx2_ref[...]

        @pl.when(pl.program_id(0) == n_i - 1)
        def _():
            _push_wait(own, pairs)

    row = lambda i: (i, 0)
    const = lambda i: (0, 0)
    tile = pl.BlockSpec((tm, D), row)
    return pl.pallas_call(
        body, name="dh",
        grid=(n_i,),
        in_specs=[tile] * npc + [
            pl.BlockSpec((tm, 2 * D), row),
            pl.BlockSpec((tm, 128), row),
            _ANY,
            pl.BlockSpec((128, D), const),
            tile, tile,
            pl.BlockSpec((1, D), const),
            _ANY],
        out_specs=(tile, pl.BlockSpec((1, D), const), _ANY),
        out_shape=(jax.ShapeDtypeStruct((T, D), F32),
                   jax.ShapeDtypeStruct((1, D), F32),
                   jax.ShapeDtypeStruct(s_in.shape, s_in.dtype)),
        scratch_shapes=[pltpu.VMEM((n_main, D), BF16), pltpu.SemaphoreType.DMA((2,))] + _CHIP_SEMS,
        compiler_params=_cparams(("arbitrary",)),
    )(*pieces, dmlog, drank, wt, wr, x, dx2, norm_g, s_in)


def _wgrad_rank_call(ht, drank):
    n_tb, D, tb = ht.shape

    def body(ht_ref, dr_ref, o_ref):
        @pl.when(pl.program_id(0) == 0)
        def _():
            o_ref[...] = jnp.zeros_like(o_ref)

        o_ref[...] += _dot(ht_ref[...], dr_ref[...])

    return pl.pallas_call(
        body, name="wgrad_rank",
        grid=(n_tb,),
        in_specs=[pl.BlockSpec((None, D, tb), lambda i: (i, 0, 0)),
                  pl.BlockSpec((tb, 128), lambda i: (i, 0))],
        out_specs=pl.BlockSpec((D, 128), lambda i: (0, 0)),
        out_shape=jax.ShapeDtypeStruct((D, 128), F32),
        compiler_params=_cparams(("arbitrary",)),
    )(ht, drank)


def _wgrad_call(lhs_list, lhs_of_group, rhs_list, rhs_of_group, n_transposed, name):
    n_groups = len(rhs_of_group)
    n_tb, D, tb = lhs_list[0].shape
    T = n_tb * tb
    per = min(2, n_tb)
    tk = per * tb
    nk = T // tk
    nl = len(lhs_list)

    def body(*refs):
        lhs = refs[:nl]
        rhs = refs[nl:nl + n_groups]
        out_ref, acc = refs[nl + n_groups:]
        g = pl.program_id(0)
        i = pl.program_id(1)

        @pl.when(i == 0)
        def _():
            acc[...] = jnp.zeros_like(acc)

        for p in range(n_groups):
            @pl.when(g == p)
            def _(p=p):
                lref = lhs[lhs_of_group[p]]
                part = _dot(lref[0], rhs[p][0:tb, :])
                for b in range(1, per):
                    part = part + _dot(lref[b], rhs[p][b * tb:(b + 1) * tb, :])
                acc[...] += part

        @pl.when((i == nk - 1) & (g < n_transposed))
        def _():
            out_ref[...] = _bf(acc[...].T)

        @pl.when((i == nk - 1) & (g >= n_transposed))
        def _():
            out_ref[...] = _bf(acc[...])

    def lhs_spec(a):
        groups = [g for g in range(n_groups) if lhs_of_group[g] == a]
        lo, hi = min(groups), max(groups)
        assert groups == list(range(lo, hi + 1))
        return pl.BlockSpec((per, D, tb), lambda g, i: (jnp.where((g >= lo) & (g <= hi), i, 0), 0, 0))

    def rhs_spec(p):
        cb = rhs_of_group[p][1]
        return pl.BlockSpec((tk, 1024), lambda g, i: (jnp.where(g == p, i, 0), cb))

    return pl.pallas_call(
        body, name=name,
        grid=(n_groups, nk),
        in_specs=[lhs_spec(a) for a in range(nl)] + [rhs_spec(p) for p in range(n_groups)],
        out_specs=pl.BlockSpec((None, D, 1024), lambda g, i: (g, 0, 0)),
        out_shape=jax.ShapeDtypeStruct((n_groups, D, 1024), BF16),
        scratch_shapes=[pltpu.VMEM((D, 1024), F32)],
        compiler_params=_cparams(("arbitrary", "arbitrary")),
    )(*lhs_list, *[rhs_list[rhs_of_group[p][0]] for p in range(n_groups)])


def _adamw_call(parts, w, m, v, name, gathered=None):
    R, C = w.shape
    n_parts = parts.shape[0]
    (tr, tc), grid, idx = _tiling_2d(R, C)
    carry = gathered is not None

    def body(*refs):
        if carry:
            (p_ref, w_ref, m_ref, v_ref, src_ref, g_ref, d_ref, nm_ref, nv_ref, dst_ref,
             send_sems, recv_sems, loc_sem) = refs
            own, pairs = _push_copies(src_ref, dst_ref, send_sems, recv_sems, loc_sem, scatter=False)

            @pl.when(pl.program_id(0) == 0)
            def _():
                _push_start(own, pairs)
        else:
            p_ref, w_ref, m_ref, v_ref, g_ref, d_ref, nm_ref, nv_ref = refs
        g = p_ref[n_parts - 1].astype(F32)
        for k in range(n_parts - 1):
            g = g + p_ref[k].astype(F32)
        mm = ADAM_B1 * m_ref[...] + (1.0 - ADAM_B1) * g
        vv = ADAM_B2 * v_ref[...] + (1.0 - ADAM_B2) * (g * g)
        m_hat = mm / (1.0 - ADAM_B1 ** ADAM_STEP)
        v_hat = vv / (1.0 - ADAM_B2 ** ADAM_STEP)
        d_ref[...] = -ADAM_LR * (m_hat / (jnp.sqrt(v_hat) + ADAM_EPS) + ADAM_WD * w_ref[...])
        g_ref[...] = g
        nm_ref[...] = mm
        nv_ref[...] = vv
        if carry:
            @pl.when(pl.program_id(0) == grid[0] - 1)
            def _():
                _push_wait(own, pairs)

    blk = pl.BlockSpec((tr, tc), idx)
    sds = jax.ShapeDtypeStruct((R, C), F32)
    in_specs = [pl.BlockSpec((n_parts, tr, tc), lambda i: (0,) + idx(i)), blk, blk, blk]
    out_specs = [blk, blk, blk, blk]
    out_shape = [sds, sds, sds, sds]
    operands = [parts, w, m, v]
    scratch = []
    if carry:
        in_specs.append(_ANY)
        out_specs.append(_ANY)
        out_shape.append(jax.ShapeDtypeStruct((N_DEV,) + gathered.shape, gathered.dtype))
        operands.append(gathered)
        scratch = _PUSH_SEMS
    return pl.pallas_call(
        body, name=name,
        grid=grid,
        in_specs=in_specs, out_specs=tuple(out_specs), out_shape=tuple(out_shape),
        scratch_shapes=scratch,
        compiler_params=_cparams(("arbitrary",)),
    )(*operands)


def _local_step(x, target, wt, wr, wdec, bdec, wp_shard, norm_g, gla_g, b_gate, final_g):
    D = x.shape[1]
    half = wp_shard.shape[1] // 2
    projf, projb, rank, ht, wp_lo = _proj_call(x, norm_g, wt, wr, wp_shard[:, :half])
    o_gla, st_all, la = _gla_fwd_call(projf, projb, rank, wdec, bdec)
    o_sb, wp_hi = _sb_fwd_call(projb, wp_shard[:, half:])
    wp_full = jnp.concatenate([wp_lo, wp_hi], axis=2).transpose(1, 0, 2, 3).reshape(3, D, D)
    (dx2, do_gla, do_sb, dggate, dsgate, dmlog, mt, ogt, obt, dx2b, dya, dyb,
     dfinal_g, db_gate, dgla_g, loss) = _mid_call(o_gla, o_sb, projf, x, target, wp_full[0], wp_full[1],
                                                 wp_full[2], gla_g, b_gate, final_g)
    dw_p = _wgrad_call([ogt, obt, mt], [0, 1, 2], [dya, dyb, dx2b], [(0, 0), (1, 0), (2, 0)], 0, "wgrad_p")
    g_p = dw_p.reshape(3, N_DEV, D // N_DEV, D).transpose(1, 0, 2, 3).reshape(N_DEV, 3 * (D // N_DEV), D)
    dqk, dgv, drank, dwdec, dbdec = _gla_bwd_call(projf, projb, la, do_gla, st_all, rank, wdec)
    dsq, dsk, dsv, r_p = _sb_bwd_call(projb, do_sb, g_p)
    pieces = [dqk, dgv, dggate, dsq, dsk, dsv, dsgate]
    rhs_of_group = [(g, 0) for g in range(7)] + [(7, 0), (7, 1)]
    dw_in = _wgrad_call([ht], [0] * N_GROUPS, pieces + [dmlog], rhs_of_group, N_GROUPS, "wgrad_in")
    dwr = _wgrad_rank_call(ht, drank)
    g_in = _parts_by_device(dw_in.reshape(N_GROUPS * 1024, D), dwr[:, :GLA_RANK].T.astype(BF16))
    c_idx = lax.axis_index("c").astype(jnp.int32).reshape(1)
    (p_in,) = _pair_exchange([g_in], "pair_g")
    s_in = _pair_add_call(g_in, p_in, c_idx, "pair_add_in")
    grad_x, dnorm_g, r_in = _dh_call(pieces, dmlog, drank, wt, wr, x, dx2, norm_g, s_in)
    small = jnp.concatenate([
        dnorm_g.reshape(-1), dbdec.reshape(-1), dgla_g.reshape(-1), db_gate.reshape(-1), dfinal_g.reshape(-1),
        loss.reshape(-1), dwdec[:GLA_RANK].reshape(-1)]).reshape(1, _SM_LEN)
    return grad_x, r_in, r_p, small


def _parts_by_device(dmain, drank):
    def part_for(p):
        lo, hi = p * SHARD_COLS, (p + 1) * SHARD_COLS
        pieces = []
        if lo < RANK_COL:
            pieces.append(dmain[lo:min(hi, RANK_COL)])
        if lo < RANK_COL + GLA_RANK and hi > RANK_COL:
            pieces.append(drank[max(lo, RANK_COL) - RANK_COL:min(hi, RANK_COL + GLA_RANK) - RANK_COL])
        if hi > RANK_COL + GLA_RANK:
            pieces.append(dmain[max(lo, RANK_COL + GLA_RANK) - GLA_RANK:hi - GLA_RANK])
        return pieces[0] if len(pieces) == 1 else jnp.concatenate(pieces, axis=0)

    return jnp.stack([part_for(p) for p in range(N_DEV)])


_SM_NORM = 0
_SM_BDEC = _SM_NORM + D_MODEL
_SM_GLAG = _SM_BDEC + GLA_DK
_SM_BGATE = _SM_GLAG + GLA_HV
_SM_FINAL = _SM_BGATE + 2 * D_MODEL
_SM_REPL = _SM_FINAL + D_MODEL
_SM_LOSS = _SM_REPL
_SM_WDEC = _SM_LOSS + 128
_SM_LEN = _SM_WDEC + GLA_RANK * GLA_DK


def kernel(x, norm_g, w_in, w_dec_up, b_dec, gla_norm_g, w_pa, w_pb, b_gate, w_o, final_g, loss_target, m_norm_g, m_w_in, m_w_dec_up, m_b_dec, m_gla_norm_g, m_w_pa, m_w_pb, m_b_gate, m_w_o, m_final_g, v_norm_g, v_w_in, v_w_dec_up, v_b_dec, v_gla_norm_g, v_w_pa, v_w_pb, v_b_gate, v_w_o, v_final_g):
    D = D_MODEL
    me = 4 * lax.axis_index("x") + 2 * lax.axis_index("y") + lax.axis_index("c")

    wp_shard = jnp.stack([w_pa, w_pb, w_o]).astype(BF16)
    win_all, wdec_all = _all_gather([w_in.T.astype(BF16), w_dec_up], "gather_w")
    wt = win_all.reshape(IN_COLS, D)
    wr = jnp.pad(wt[RANK_COL:RANK_COL + GLA_RANK], ((0, 128 - GLA_RANK), (0, 0)))
    wdec_full = wdec_all.transpose(1, 0, 2).reshape(GLA_RANK, GLA_DK)
    wdec = jnp.pad(wdec_full, ((0, 128 - GLA_RANK), (0, 0)))

    grad_x, r_in, r_p, small = _local_step(
        x[0], loss_target[0], wt, wr, wdec, b_dec.reshape(1, -1), wp_shard,
        norm_g.reshape(1, -1), gla_norm_g.reshape(1, -1), b_gate.reshape(1, -1), final_g.reshape(1, -1))

    gw_in, d_in, nm_in, nv_in = (a.T for a in _adamw_call(r_in, w_in.T, m_w_in.T, v_w_in.T, "adamw_in"))
    wp_f32 = jnp.concatenate([w_pa, w_pb, w_o], axis=0)
    mp = jnp.concatenate([m_w_pa, m_w_pb, m_w_o], axis=0)
    vp = jnp.concatenate([v_w_pa, v_w_pb, v_w_o], axis=0)
    gp, dp, nmp, nvp, r_small = _adamw_call(r_p, wp_f32, mp, vp, "adamw_p", gathered=small)
    rows = D // N_DEV

    def split3(a):
        return a[:rows], a[rows:2 * rows], a[2 * rows:]

    g_pa, g_pb, g_o = split3(gp)
    d_pa, d_pb, d_o = split3(dp)
    nm_pa, nm_pb, nm_o = split3(nmp)
    nv_pa, nv_pb, nv_o = split3(nvp)

    w_rep = jnp.concatenate([norm_g, b_dec, gla_norm_g, b_gate, final_g]).reshape(1, _SM_REPL)
    m_rep = jnp.concatenate([m_norm_g, m_b_dec, m_gla_norm_g, m_b_gate, m_final_g]).reshape(1, _SM_REPL)
    v_rep = jnp.concatenate([v_norm_g, v_b_dec, v_gla_norm_g, v_b_gate, v_final_g]).reshape(1, _SM_REPL)
    g_rep, d_rep, nm_rep, nv_rep = _adamw_call(r_small[:, :, :_SM_REPL], w_rep, m_rep, v_rep, "adamw_rep")

    def split_rep(a):
        a = a.reshape(-1)
        return (a[_SM_NORM:_SM_BDEC], a[_SM_BDEC:_SM_GLAG], a[_SM_GLAG:_SM_BGATE],
                a[_SM_BGATE:_SM_FINAL], a[_SM_FINAL:_SM_REPL])

    g_norm, g_bdec, g_glag, g_bgate, g_final = split_rep(g_rep)
    d_norm, d_bdec, d_glag, d_bgate, d_final = split_rep(d_rep)
    nm_norm, nm_bdec, nm_glag, nm_bgate, nm_final = split_rep(nm_rep)
    nv_norm, nv_bdec, nv_glag, nv_bgate, nv_final = split_rep(nv_rep)

    wdec_parts = r_small[:, 0, _SM_WDEC:].reshape(N_DEV, GLA_RANK, GLA_DK)
    cols = GLA_DK // N_DEV
    wdec_mine = lax.dynamic_slice_in_dim(wdec_parts, me * cols, cols, axis=2)
    g_wdec, d_wdec, nm_wdec, nv_wdec = _adamw_call(wdec_mine, w_dec_up, m_w_dec_up, v_w_dec_up, "adamw_dec")

    loss_total = jnp.sum(r_small[:, 0, _SM_LOSS])

    return (loss_total, grad_x[None],
            g_norm, gw_in, g_wdec, g_bdec, g_glag, g_pa, g_pb, g_bgate, g_o, g_final,
            d_norm, d_in, d_wdec, d_bdec, d_glag, d_pa, d_pb, d_bgate, d_o, d_final,
            nm_norm, nm_in, nm_wdec, nm_bdec, nm_glag, nm_pa, nm_pb, nm_bgate, nm_o, nm_final,
            nv_norm, nv_in, nv_wdec, nv_bdec, nv_glag, nv_pa, nv_pb, nv_bgate, nv_o, nv_final)
```

```python
import functools
import math

import jax
import jax.numpy as jnp
from jax import lax
from jax.experimental import pallas as pl
from jax.experimental.pallas import tpu as pltpu

F32 = jnp.float32
BF16 = jnp.bfloat16

N_DEV = 8
D_MODEL = 1024
GLA_HEADS = 4
GLA_HK = 128
GLA_HV = 256
GLA_DK = 512
GLA_RANK = 16
GLA_TAU = 16.0
GLA_CHUNK = 64
SB_HEADS = 8
SB_HD = 128
EPS = 1e-6
N_GROUPS = 9
RANK_COL = 3072
IN_COLS = 9232
SHARD_COLS = IN_COLS // N_DEV

ADAM_LR = 0.001
ADAM_B1 = 0.9
ADAM_B2 = 0.999
ADAM_EPS = 1e-08
ADAM_WD = 0.01
ADAM_STEP = 10

VMEM_LIMIT = 56 * 1024 * 1024
TBLK = 256


def _cparams(sem=None):
    return pltpu.CompilerParams(dimension_semantics=sem, vmem_limit_bytes=VMEM_LIMIT)


def _tiling_2d(rows, cols):
    if rows * cols <= 128 * 1024:
        return (rows, cols), (1,), lambda i: (0, 0)
    if rows % 128 == 0:
        return (128, cols), (rows // 128,), lambda i: (i, 0)
    tc = 256 if cols % 256 == 0 else cols
    return (rows, tc), (cols // tc,), lambda i: (0, i)


def _dot(a, b):
    return jnp.dot(a, b, preferred_element_type=F32)


def _dot_nt(a, b):
    return lax.dot_general(a, b, (((1,), (1,)), ((), ())), preferred_element_type=F32)


def _dot_tn(a, b):
    return lax.dot_general(a, b, (((0,), (0,)), ((), ())), preferred_element_type=F32)


def _bf(x):
    return x.astype(BF16)


def _split3(x):
    hi = x.astype(BF16)
    r = x - hi.astype(F32)
    mid = r.astype(BF16)
    lo = (r - mid.astype(F32)).astype(BF16)
    return hi, mid, lo


def _tri_left(tri, x):
    hi, mid, lo = _split3(x)
    return _dot(tri, hi) + _dot(tri, mid) + _dot(tri, lo)


def _split2(x):
    hi = lax.bitcast_convert_type(lax.bitcast_convert_type(x, jnp.uint32) & jnp.uint32(0xFFFF0000), F32)
    return hi.astype(BF16), (x - hi).astype(BF16)


def _tri2_left(tri, x):
    hi, lo = _split2(x)
    return _dot(tri, hi) + _dot(tri, lo)


def _tri2_right(x, tri):
    hi, lo = _split2(x)
    return _dot(hi, tri) + _dot(lo, tri)


def _iota2(n, m, dim):
    return lax.broadcasted_iota(jnp.int32, (n, m), dim)


def _sigmoid(x):
    return 1.0 / (1.0 + jnp.exp(-x))


def _softplus_neg_abs(z):
    return jnp.log(1.0 + jnp.exp(-jnp.abs(z)))


_ANY = pl.BlockSpec(memory_space=pl.ANY)


def _mesh_pos():
    return lax.axis_index("x"), lax.axis_index("y"), lax.axis_index("c")


def _other_chips(x, y):
    return [(1 - x, y), (x, 1 - y), (1 - x, 1 - y)]


def _rcopy(src, dst, send_sem, recv_sem, to):
    return pltpu.make_async_remote_copy(src_ref=src, dst_ref=dst, send_sem=send_sem, recv_sem=recv_sem,
                                        device_id=to, device_id_type=pl.DeviceIdType.MESH)


def _push_copies(src_ref, dst_ref, send_sems, recv_sems, loc_sem, scatter):
    x, y, c = _mesh_pos()
    me = 4 * x + 2 * y + c
    own = pltpu.make_async_copy(src_ref.at[me] if scatter else src_ref, dst_ref.at[me], loc_sem)
    pairs = []
    for k in range(1, N_DEV):
        px = 1 - x if k & 4 else x
        py = 1 - y if k & 2 else y
        pc = 1 - c if k & 1 else c
        pid = 4 * px + 2 * py + pc
        src = src_ref.at[pid] if scatter else src_ref
        send = _rcopy(src, dst_ref.at[me], send_sems.at[k - 1], recv_sems.at[k - 1], (px, py, pc))
        recv = _rcopy(src, dst_ref.at[pid], send_sems.at[k - 1], recv_sems.at[k - 1], (px, py, pc))
        pairs.append((send, recv))
    return own, pairs


def _push_start(own, pairs):
    own.start()
    for send, _ in pairs:
        send.start()


def _push_wait(own, pairs):
    for _, recv in pairs:
        recv.wait_recv()
    for send, _ in pairs:
        send.wait_send()
    own.wait()


_PUSH_SEMS = [pltpu.SemaphoreType.DMA((N_DEV - 1,)), pltpu.SemaphoreType.DMA((N_DEV - 1,)),
              pltpu.SemaphoreType.DMA]


def _chip_copies(src_ref, dst_ref, send_sems, recv_sems, loc_sem):
    x, y, c = _mesh_pos()
    own = pltpu.make_async_copy(src_ref.at[2 * x + y], dst_ref.at[3], loc_sem)
    pairs = []
    for j, (px, py) in enumerate(_other_chips(x, y)):
        cp = _rcopy(src_ref.at[2 * px + py], dst_ref.at[j], send_sems.at[j], recv_sems.at[j], (px, py, c))
        pairs.append((cp, cp))
    return own, pairs


_CHIP_SEMS = [pltpu.SemaphoreType.DMA((3,)), pltpu.SemaphoreType.DMA((3,)), pltpu.SemaphoreType.DMA]


def _all_gather(arrs, name):
    n = len(arrs)

    def body(*refs):
        ins = refs[:n]
        outs = refs[n:2 * n]
        send_sems, recv_sems, loc_sems = refs[2 * n:]
        x, y, c = _mesh_pos()
        sib = (x, y, 1 - c)
        chips = _other_chips(x, y)

        def place(a, px, py, pc):
            return outs[a].at[4 * px + 2 * py + pc]

        def copy(a, k, block, to, src=None):
            dst = place(a, *block)
            return _rcopy(dst if src is None else src, dst, send_sems.at[a, k], recv_sems.at[a, k], to)

        mine = [pltpu.make_async_copy(ins[a], place(a, x, y, c), loc_sems.at[a]) for a in range(n)]
        for cp in mine:
            cp.start()
        first = [copy(a, 0, (x, y, c), sib, src=ins[a]) for a in range(n)]
        for j, chip in enumerate(chips):
            first += [copy(a, 1 + j, (x, y, c), (*chip, c), src=ins[a]) for a in range(n)]
        for cp in first:
            cp.start()
        passed = []
        for j, chip in enumerate(chips):
            for a in range(n):
                copy(a, 1 + j, (*chip, c), (x, y, c)).wait_recv()
                fwd = copy(a, 4 + j, (*chip, c), sib)
                fwd.start()
                passed.append(fwd)
        for a in range(n):
            copy(a, 0, sib, (x, y, c)).wait_recv()
        for j, chip in enumerate(chips):
            for a in range(n):
                copy(a, 4 + j, (*chip, 1 - c), (x, y, c)).wait_recv()
        for cp in first + passed:
            cp.wait_send()
        for cp in mine:
            cp.wait()

    return pl.pallas_call(
        body, name=name,
        out_shape=tuple(jax.ShapeDtypeStruct((N_DEV,) + a.shape, a.dtype) for a in arrs),
        in_specs=[_ANY] * n,
        out_specs=tuple([_ANY] * n),
        scratch_shapes=[pltpu.SemaphoreType.DMA((n, 7)), pltpu.SemaphoreType.DMA((n, 7)),
                        pltpu.SemaphoreType.DMA((n,))],
    )(*arrs)


def _pair_exchange(arrs, name):
    n = len(arrs)

    def body(*refs):
        ins = refs[:n]
        outs = refs[n:2 * n]
        send_sems, recv_sems = refs[2 * n:]
        x, y, c = _mesh_pos()
        copies = []
        for a in range(n):
            for q in range(4):
                cp = _rcopy(ins[a].at[2 * q + (1 - c)], outs[a].at[q], send_sems.at[a, q], recv_sems.at[a, q],
                            (x, y, 1 - c))
                cp.start()
                copies.append(cp)
        for cp in copies:
            cp.wait_recv()
        for cp in copies:
            cp.wait_send()

    return pl.pallas_call(
        body, name=name,
        out_shape=tuple(jax.ShapeDtypeStruct((4,) + a.shape[1:], a.dtype) for a in arrs),
        in_specs=[_ANY] * n,
        out_specs=tuple([_ANY] * n),
        scratch_shapes=[pltpu.SemaphoreType.DMA((n, 4)), pltpu.SemaphoreType.DMA((n, 4))],
    )(*arrs)


def _pair_add_call(parts, recv, c_idx, name):
    _, R, C = parts.shape
    (tr, tc), (steps,), idx = _tiling_2d(R, C)

    def body(c_ref, p_ref, r_ref, o_ref):
        o_ref[...] = (p_ref[...].astype(F32) + r_ref[...].astype(F32)).astype(o_ref.dtype)

    return pl.pallas_call(
        body, name=name,
        grid_spec=pltpu.PrefetchScalarGridSpec(
            num_scalar_prefetch=1,
            grid=(4, steps),
            in_specs=[pl.BlockSpec((None, tr, tc), lambda q, i, c_ref: (2 * q + c_ref[0],) + idx(i)),
                      pl.BlockSpec((None, tr, tc), lambda q, i, c_ref: (q,) + idx(i))],
            out_specs=pl.BlockSpec((None, tr, tc), lambda q, i, c_ref: (q,) + idx(i))),
        out_shape=jax.ShapeDtypeStruct((4, R, C), parts.dtype),
        compiler_params=_cparams(("arbitrary", "arbitrary")),
    )(c_idx, parts, recv)


def _group_row(g):
    return GLA_RANK * (g * (1024 // GLA_RANK) + (g >= RANK_COL // 1024))


def _proj_call(x, norm_g, wt, wr, wp_part):
    T, D = x.shape
    tm = min(1024, T)
    assert tm % TBLK == 0
    n_i = T // tm

    def f_slot(j):
        return ((j >= 2).astype(jnp.int32) + (j >= 6).astype(jnp.int32)
                + (j >= 7).astype(jnp.int32) + (j >= 8).astype(jnp.int32))

    def b_slot(j):
        return (j >= 3).astype(jnp.int32) + (j >= 4).astype(jnp.int32) + (j >= 5).astype(jnp.int32)

    def body(x_ref, g_ref, w_ref, wr_ref, wp_ref, pf_ref, pb_ref, rank_ref, ht_ref, wpall_ref,
             h_scr, send_sems, recv_sems, loc_sem):
        i = pl.program_id(0)
        j = pl.program_id(1)
        own, pairs = _push_copies(wp_ref, wpall_ref, send_sems, recv_sems, loc_sem, scatter=False)

        @pl.when((i == 0) & (j == 0))
        def _():
            _push_start(own, pairs)

        @pl.when(j == 0)
        def _():
            xv = x_ref[...]
            r = lax.rsqrt(jnp.mean(xv * xv, axis=-1, keepdims=True) + EPS)
            h = (xv * r) * g_ref[...]
            hb = _bf(h)
            h_scr[...] = hb
            for b in range(tm // TBLK):
                ht_ref[b] = _bf(h[b * TBLK:(b + 1) * TBLK].T)
            rank_ref[...] = _dot_nt(hb, wr_ref[...])

        is_b = (j == 1) | ((j >= 3) & (j <= 5))

        @pl.when(is_b)
        def _():
            pb_ref[...] = _bf(_dot_nt(h_scr[...], w_ref[...]))

        @pl.when(jnp.logical_not(is_b))
        def _():
            pf_ref[...] = _dot_nt(h_scr[...], w_ref[...])

        @pl.when((i == n_i - 1) & (j == N_GROUPS - 1))
        def _():
            _push_wait(own, pairs)

    return pl.pallas_call(
        body, name="proj",
        grid=(n_i, N_GROUPS),
        in_specs=[pl.BlockSpec((tm, D), lambda i, j: (i, 0)),
                  pl.BlockSpec((1, D), lambda i, j: (0, 0)),
                  pl.BlockSpec((pl.Element(1024), pl.Element(D)), lambda i, j: (_group_row(j), 0)),
                  pl.BlockSpec((128, D), lambda i, j: (0, 0)),
                  _ANY],
        out_specs=(pl.BlockSpec((None, tm, 1024), lambda i, j: (f_slot(j), i, 0)),
                   pl.BlockSpec((None, tm, 1024), lambda i, j: (b_slot(j), i, 0)),
                   pl.BlockSpec((tm, 128), lambda i, j: (i, 0)),
                   pl.BlockSpec((tm // TBLK, D, TBLK), lambda i, j: (i, 0, 0)),
                   _ANY),
        out_shape=(jax.ShapeDtypeStruct((5, T, 1024), F32),
                   jax.ShapeDtypeStruct((4, T, 1024), BF16),
                   jax.ShapeDtypeStruct((T, 128), F32),
                   jax.ShapeDtypeStruct((T // TBLK, D, TBLK), BF16),
                   jax.ShapeDtypeStruct((N_DEV,) + wp_part.shape, wp_part.dtype)),
        scratch_shapes=[pltpu.VMEM((tm, D), BF16)] + _PUSH_SEMS,
        compiler_params=_cparams(("arbitrary", "arbitrary")),
    )(x, norm_g, wt, wr, wp_part)


def _gla_chunk_terms(la_h, q, k):
    C = GLA_CHUNK
    low = _bf((_iota2(C, C, 0) >= _iota2(C, C, 1)).astype(F32))
    b = _tri_left(low, la_h)
    bl = b[C - 1:C, :]
    eb = jnp.exp(b)
    enb = jnp.exp(-b)
    ebl_b = jnp.exp(bl - b)
    scale = GLA_HK ** -0.5
    qe = q * eb * scale
    ke = k * enb
    kd = k * ebl_b
    return b, bl, eb, enb, ebl_b, qe, ke, kd


def _gla_fwd_call(projf, projb, rank, wdec, bdec):
    T = projf.shape[1]
    C = GLA_CHUNK
    n_chunks = T // C

    def body(qk_ref, v_ref, rank_ref, wd_ref, bd_ref, o_ref, st_ref, la_ref, st_scr):
        @pl.when(pl.program_id(0) == 0)
        def _():
            st_scr[...] = jnp.zeros_like(st_scr)

        dec = _dot(_bf(rank_ref[...]), _bf(wd_ref[...])) + bd_ref[...]
        la = (jnp.minimum(dec, 0.0) - _softplus_neg_abs(dec)) / GLA_TAU
        la_ref[...] = la
        mask = _iota2(C, C, 0) >= _iota2(C, C, 1)
        _, bl, _, _, _, qe, ke, kd = _gla_chunk_terms(la, qk_ref[:, :GLA_DK], qk_ref[:, GLA_DK:])
        qeb, keb, kdb = _bf(qe), _bf(ke), _bf(kd)
        ebl = jnp.exp(bl)
        heads = range(GLA_HEADS)
        ks = [slice(hh * GLA_HK, (hh + 1) * GLA_HK) for hh in heads]
        vs = [slice(hh * GLA_HV, (hh + 1) * GLA_HV) for hh in heads]
        st = [st_scr[hh] for hh in heads]
        p = [_bf(jnp.where(mask, _dot_nt(qeb[:, ks[hh]], keb[:, ks[hh]]), 0.0)) for hh in heads]
        inter = [_dot_nt(qeb[:, ks[hh]], _bf(st[hh])) for hh in heads]
        upd = [_dot_tn(v_ref[:, vs[hh]], kdb[:, ks[hh]]) for hh in heads]
        intra = [_dot(p[hh], v_ref[:, vs[hh]]) for hh in heads]
        for hh in heads:
            st_ref[hh] = st[hh]
            o_ref[:, vs[hh]] = intra[hh] + inter[hh]
            st_scr[hh] = st[hh] * ebl[:, ks[hh]] + upd[hh]

    return pl.pallas_call(
        body, name="gla_fwd",
        grid=(n_chunks,),
        in_specs=[pl.BlockSpec((None, C, 1024), lambda n: (0, n, 0)),
                  pl.BlockSpec((None, C, 1024), lambda n: (0, n, 0)),
                  pl.BlockSpec((C, 128), lambda n: (n, 0)),
                  pl.BlockSpec((128, GLA_DK), lambda n: (0, 0)),
                  pl.BlockSpec((1, GLA_DK), lambda n: (0, 0))],
        out_specs=(pl.BlockSpec((C, 1024), lambda n: (n, 0)),
                   pl.BlockSpec((None, GLA_HEADS, GLA_HV, GLA_HK), lambda n: (n, 0, 0, 0)),
                   pl.BlockSpec((C, GLA_DK), lambda n: (n, 0))),
        out_shape=(jax.ShapeDtypeStruct((T, 1024), F32),
                   jax.ShapeDtypeStruct((n_chunks, GLA_HEADS, GLA_HV, GLA_HK), F32),
                   jax.ShapeDtypeStruct((T, GLA_DK), F32)),
        scratch_shapes=[pltpu.VMEM((GLA_HEADS, GLA_HV, GLA_HK), F32)],
        compiler_params=_cparams(("arbitrary",)),
    )(projf, projb, rank, wdec, bdec)


def _gla_bwd_call(projf, projb, la, do_gla, st_all, rank, wdec):
    T = projf.shape[1]
    C = GLA_CHUNK
    n_chunks = T // C
    last = n_chunks - 1

    def body(qk_ref, v_ref, la_ref, do_ref, st_ref, rank_ref, wd_ref,
             dqk_ref, dv_ref, drank_ref, dwd_ref, dbd_ref, dst_scr):
        @pl.when(pl.program_id(0) == 0)
        def _():
            dst_scr[...] = jnp.zeros_like(dst_scr)
            dwd_ref[...] = jnp.zeros_like(dwd_ref)
            dbd_ref[...] = jnp.zeros_like(dbd_ref)

        mask = _iota2(C, C, 0) >= _iota2(C, C, 1)
        upp = _bf((_iota2(C, C, 0) <= _iota2(C, C, 1)).astype(F32))
        scale = GLA_HK ** -0.5
        la = la_ref[...]
        _, bl, eb, enb, ebl_b, qe, ke, kd = _gla_chunk_terms(la, qk_ref[:, :GLA_DK], qk_ref[:, GLA_DK:])
        qeb, keb, kdb = _bf(qe), _bf(ke), _bf(kd)
        ebl = jnp.exp(bl)
        heads = range(GLA_HEADS)
        ks = [slice(hh * GLA_HK, (hh + 1) * GLA_HK) for hh in heads]
        vs = [slice(hh * GLA_HV, (hh + 1) * GLA_HV) for hh in heads]
        v = [v_ref[:, vs[hh]] for hh in heads]
        do = [_bf(do_ref[:, vs[hh]]) for hh in heads]
        st = [st_ref[hh] for hh in heads]
        dstn = [dst_scr[hh] for hh in heads]
        dstnb = [_bf(dstn[hh]) for hh in heads]
        p = [_bf(jnp.where(mask, _dot_nt(qeb[:, ks[hh]], keb[:, ks[hh]]), 0.0)) for hh in heads]
        dp = [_bf(jnp.where(mask, _dot_nt(do[hh], v[hh]), 0.0)) for hh in heads]
        dkd = [_dot(v[hh], dstnb[hh]) for hh in heads]
        dv_inter = [_dot_nt(kdb[:, ks[hh]], dstnb[hh]) for hh in heads]
        dqe_inter = [_dot(do[hh], _bf(st[hh])) for hh in heads]
        dst_new = [_dot_tn(do[hh], qeb[:, ks[hh]]) + dstn[hh] * ebl[:, ks[hh]] for hh in heads]
        debl = jnp.concatenate([jnp.sum(dstn[hh] * st[hh], axis=0, keepdims=True) for hh in heads], axis=1)
        dv = [_dot_tn(p[hh], do[hh]) + dv_inter[hh] for hh in heads]
        dqe = jnp.concatenate([_dot(dp[hh], keb[:, ks[hh]]) + dqe_inter[hh] for hh in heads], axis=1)
        dke = jnp.concatenate([_dot_tn(dp[hh], qeb[:, ks[hh]]) for hh in heads], axis=1)
        dkd = jnp.concatenate(dkd, axis=1)
        for hh in heads:
            dst_scr[hh] = dst_new[hh]
            dv_ref[:, vs[hh]] = _bf(dv[hh])
        dkd_kd = dkd * kd
        db = dqe * qe - dke * ke - dkd_kd
        dbl = jnp.sum(dkd_kd, axis=0, keepdims=True) + ebl * debl
        dla = _tri_left(upp, db) + dbl
        dqk_ref[:, :GLA_DK] = _bf(dqe * eb * scale)
        dqk_ref[:, GLA_DK:] = _bf(dke * enb + dkd * ebl_b)
        ddec = dla * (1.0 / GLA_TAU) * (1.0 - jnp.exp(GLA_TAU * la))
        ddecb = _bf(ddec)
        drank_ref[...] = _bf(_dot_nt(ddecb, _bf(wd_ref[...])))
        dwd_ref[...] += _dot_tn(_bf(rank_ref[...]), ddecb)
        dbd_ref[...] += jnp.sum(ddec, axis=0, keepdims=True)

    return pl.pallas_call(
        body, name="gla_bwd",
        grid=(n_chunks,),
        in_specs=[pl.BlockSpec((None, C, 1024), lambda n: (0, last - n, 0)),
                  pl.BlockSpec((None, C, 1024), lambda n: (0, last - n, 0)),
                  pl.BlockSpec((C, GLA_DK), lambda n: (last - n, 0)),
                  pl.BlockSpec((C, 1024), lambda n: (last - n, 0)),
                  pl.BlockSpec((None, GLA_HEADS, GLA_HV, GLA_HK), lambda n: (last - n, 0, 0, 0)),
                  pl.BlockSpec((C, 128), lambda n: (last - n, 0)),
                  pl.BlockSpec((128, GLA_DK), lambda n: (0, 0))],
        out_specs=(pl.BlockSpec((C, 1024), lambda n: (last - n, 0)),
                   pl.BlockSpec((C, 1024), lambda n: (last - n, 0)),
                   pl.BlockSpec((C, 128), lambda n: (last - n, 0)),
                   pl.BlockSpec((128, GLA_DK), lambda n: (0, 0)),
                   pl.BlockSpec((1, GLA_DK), lambda n: (0, 0))),
        out_shape=(jax.ShapeDtypeStruct((T, 1024), BF16),
                   jax.ShapeDtypeStruct((T, 1024), BF16),
                   jax.ShapeDtypeStruct((T, 128), BF16),
                   jax.ShapeDtypeStruct((128, GLA_DK), F32),
                   jax.ShapeDtypeStruct((1, GLA_DK), F32)),
        scratch_shapes=[pltpu.VMEM((GLA_HEADS, GLA_HV, GLA_HK), F32)],
        compiler_params=_cparams(("arbitrary",)),
    )(projf, projb, la, do_gla, st_all, rank, wdec)


def _sb_logs(z):
    lsz = jnp.minimum(z, 0.0) - _softplus_neg_abs(z)
    return lsz, lsz - z


SB_HG_FWD = 8
SB_HG_BWD = 4
SB_QUERIES = 256
SB_KEYS = 256
SB_DEAD = -105.0


def _sb_fwd_call(projb, wp_shard):
    T = projb.shape[1]
    B = min(SB_QUERIES, T)
    HG = SB_HG_FWD
    W = HG * SB_HD
    scale = 1.0 / math.sqrt(SB_HD)
    KB = min(SB_KEYS, T)
    n_h, n_i = SB_HEADS // HG, T // B

    def body(q_ref, k_ref, v_ref, wp_ref, o_ref, wpall_ref, cb_scr, send_sems, recv_sems, loc_sem):
        i = pl.program_id(1)
        own, pairs = _push_copies(wp_ref, wpall_ref, send_sems, recv_sems, loc_sem, scatter=False)

        @pl.when((pl.program_id(0) == 0) & (i == 0))
        def _():
            _push_start(own, pairs)

        rows = HG * B
        after = (_iota2(KB, KB, 0) > _iota2(KB, KB, 1)).astype(F32)
        tri = _bf(jnp.concatenate([after, jnp.ones((KB, KB), F32)], axis=1))
        o_ref[...] = jnp.zeros_like(o_ref)
        cb_scr[...] = jnp.zeros_like(cb_scr)

        def block(jp, masked):
            off = pl.multiple_of(jp * KB, KB)
            z = jnp.concatenate(
                [_dot_nt(q_ref[:, hh * SB_HD:(hh + 1) * SB_HD], k_ref[pl.ds(off, KB), hh * SB_HD:(hh + 1) * SB_HD])
                 for hh in range(HG)], axis=0) * scale
            lsz, l1m = _sb_logs(z)
            if masked:
                strict = (jp * KB + _iota2(rows, KB, 1)) < (i * B + (_iota2(rows, KB, 0) & (B - 1)))
                l1m = jnp.where(strict, l1m, 0.0)
            r = _tri2_right(l1m, tri)
            cb = cb_scr[...]
            a = jnp.exp(lsz + cb + r[:, :KB])
            if masked:
                a = jnp.where(strict, a, 0.0)
            cb_scr[...] = cb + r[:, KB:]
            ab = _bf(a)
            for hh in range(HG):
                cs = slice(hh * SB_HD, (hh + 1) * SB_HD)
                o_ref[:, cs] += _dot(ab[hh * B:(hh + 1) * B, :], v_ref[pl.ds(off, KB), cs])

        jp0 = (i * B) // KB
        block(jp0, True)

        def live(state):
            jj, dead = state
            return (jj <= jp0) & jnp.logical_not(dead)

        def step(state):
            jj, _ = state
            block(jp0 - jj, False)
            return jj + 1, jnp.max(cb_scr[:, :SB_HD]) < SB_DEAD

        lax.while_loop(live, step, (jnp.int32(1), jnp.max(cb_scr[:, :SB_HD]) < SB_DEAD))

        @pl.when((pl.program_id(0) == n_h - 1) & (i == n_i - 1))
        def _():
            _push_wait(own, pairs)

    return pl.pallas_call(
        body, name="sb_fwd",
        grid=(n_h, n_i),
        in_specs=[pl.BlockSpec((None, B, W), lambda h, i: (1, i, h)),
                  pl.BlockSpec((None, T, W), lambda h, i: (2, 0, h)),
                  pl.BlockSpec((None, T, W), lambda h, i: (3, 0, h)),
                  _ANY],
        out_specs=(pl.BlockSpec((B, W), lambda h, i: (i, h)), _ANY),
        out_shape=(jax.ShapeDtypeStruct((T, 1024), F32),
                   jax.ShapeDtypeStruct((N_DEV,) + wp_shard.shape, wp_shard.dtype)),
        scratch_shapes=[pltpu.VMEM((HG * B, KB), F32)] + _PUSH_SEMS,
        compiler_params=_cparams(("arbitrary", "arbitrary")),
    )(projb, projb, projb, wp_shard)


def _sb_bwd_call(projb, do_sb, g_p):
    T = projb.shape[1]
    B = min(SB_QUERIES, T)
    nb = T // B
    HG = SB_HG_BWD
    W = HG * SB_HD
    WQ = HG * B
    KB = min(SB_KEYS, T)
    nkb = T // KB
    n_h = SB_HEADS // HG
    scale = 1.0 / math.sqrt(SB_HD)

    def body(q_ref, k_ref, v_ref, do_ref, gp_ref, dq_ref, dk_ref, dv_ref, rp_ref,
             dk_scr, dv_scr, kt_scr, beta_scr, g_scr, dqt_scr, send_sems, recv_sems, loc_sem):
        i = pl.program_id(1)
        own, pairs = _push_copies(gp_ref, rp_ref, send_sems, recv_sems, loc_sem, scatter=True)

        @pl.when((pl.program_id(0) == 0) & (i == 0))
        def _():
            _push_start(own, pairs)

        @pl.when(i == 0)
        def _():
            dk_scr[...] = jnp.zeros_like(dk_scr)
            dv_scr[...] = jnp.zeros_like(dv_scr)
            for hh in range(HG):
                for jb in range(nkb):
                    kt_scr[hh, jb] = _bf(
                        k_ref[jb * KB:(jb + 1) * KB, hh * SB_HD:(hh + 1) * SB_HD].astype(F32).T)

        dqt_scr[...] = jnp.zeros_like(dqt_scr)
        later = _bf((_iota2(KB, KB, 1) > _iota2(KB, KB, 0)).astype(F32))
        earlier = _bf((_iota2(KB, KB, 1) < _iota2(KB, KB, 0)).astype(F32))
        dob = _bf(do_ref[...])
        jp0 = (i * B) // KB

        def strict_mask():
            return (jp0 * KB + _iota2(KB, WQ, 0)) < (i * B + (_iota2(KB, WQ, 1) & (B - 1)))

        def heads(fn):
            return [fn(slice(hh * SB_HD, (hh + 1) * SB_HD)) for hh in range(HG)]

        def pass1(jp, cb, masked):
            off = pl.multiple_of(jp * KB, KB)
            z = jnp.concatenate(heads(lambda cs: _dot_nt(k_ref[pl.ds(off, KB), cs], q_ref[:, cs])), axis=1) * scale
            da = jnp.concatenate(heads(lambda cs: _dot_nt(v_ref[pl.ds(off, KB), cs], dob[:, cs])), axis=1)
            lsz, l1m = _sb_logs(z)
            if masked:
                strict = strict_mask()
                l1m = jnp.where(strict, l1m, 0.0)
            a = jnp.exp(lsz + cb + _tri2_left(later, l1m))
            if masked:
                a = jnp.where(strict, a, 0.0)
            g_scr[jp] = a * da
            beta_scr[jp] = jnp.exp(lsz)
            ab = _bf(a)
            for hh in range(HG):
                cs = slice(hh * SB_HD, (hh + 1) * SB_HD)
                dv_scr[pl.ds(off, KB), cs] += _dot(ab[:, hh * B:(hh + 1) * B], dob[:, cs])
            return cb + jnp.sum(l1m, axis=0, keepdims=True)

        zero = jnp.zeros((1, WQ), F32)
        cb = pass1(jp0, zero, True)

        def live(state):
            jj, _, dead = state
            return (jj <= jp0) & jnp.logical_not(dead)

        def step(state):
            jj, cr, _ = state
            cr = pass1(jp0 - jj, cr, False)
            return jj + 1, cr, jnp.max(cr) < SB_DEAD

        n_done, _, _ = lax.while_loop(live, step, (jnp.int32(1), cb, jnp.max(cb) < SB_DEAD))
        jp_first = jp0 - (n_done - 1)

        def pass2(jp, cg, masked):
            off = pl.multiple_of(jp * KB, KB)
            g = g_scr[jp]
            beta = beta_scr[jp]
            dz = g * (1.0 - beta) - beta * (cg + _tri2_left(earlier, g))
            if masked:
                dz = jnp.where(strict_mask(), dz, 0.0)
            dzb = _bf(dz * scale)
            for hh in range(HG):
                cs = slice(hh * SB_HD, (hh + 1) * SB_HD)
                dk_scr[pl.ds(off, KB), cs] += _dot(dzb[:, hh * B:(hh + 1) * B], q_ref[:, cs])
                dqt_scr[hh] += _dot(kt_scr[hh, jp], dzb[:, hh * B:(hh + 1) * B])
            return cg + jnp.sum(g, axis=0, keepdims=True)

        cg = lax.fori_loop(jp_first, jp0, lambda jp, cr: pass2(jp, cr, False), zero)
        pass2(jp0, cg, True)
        for hh in range(HG):
            dq_ref[:, hh * SB_HD:(hh + 1) * SB_HD] = _bf(dqt_scr[hh].T)

        @pl.when(i == nb - 1)
        def _():
            dk_ref[...] = _bf(dk_scr[...])
            dv_ref[...] = _bf(dv_scr[...])

        @pl.when((pl.program_id(0) == n_h - 1) & (i == nb - 1))
        def _():
            _push_wait(own, pairs)

    return pl.pallas_call(
        body, name="sb_bwd",
        grid=(n_h, nb),
        in_specs=[pl.BlockSpec((None, B, W), lambda h, i: (1, i, h)),
                  pl.BlockSpec((None, T, W), lambda h, i: (2, 0, h)),
                  pl.BlockSpec((None, T, W), lambda h, i: (3, 0, h)),
                  pl.BlockSpec((B, W), lambda h, i: (i, h)),
                  _ANY],
        out_specs=(pl.BlockSpec((B, W), lambda h, i: (i, h)),
                   pl.BlockSpec((T, W), lambda h, i: (0, h)),
                   pl.BlockSpec((T, W), lambda h, i: (0, h)),
                   _ANY),
        out_shape=(jax.ShapeDtypeStruct((T, 1024), BF16),
                   jax.ShapeDtypeStruct((T, 1024), BF16),
                   jax.ShapeDtypeStruct((T, 1024), BF16),
                   jax.ShapeDtypeStruct(g_p.shape, g_p.dtype)),
        scratch_shapes=[pltpu.VMEM((T, W), F32), pltpu.VMEM((T, W), F32),
                        pltpu.VMEM((HG, nkb, SB_HD, KB), BF16),
                        pltpu.VMEM((nkb, KB, WQ), F32), pltpu.VMEM((nkb, KB, WQ), F32),
                        pltpu.VMEM((HG, SB_HD, B), F32)] + _PUSH_SEMS,
        compiler_params=_cparams(("arbitrary", "arbitrary")),
    )(projb, projb, projb, do_sb, g_p)


def _mid_call(o_gla, o_sb, projf, x, target, wpa, wpb, wo, gla_g, b_gate, final_g):
    T, D = x.shape
    tm = min(TBLK, T)

    def body(og_ref, ggate_ref, osb_ref, sgate_ref, ma_ref, mb_ref, x_ref, tgt_ref,
             wpa_ref, wpb_ref, wo_ref, glag_ref, bg_ref, fg_ref,
             dx2_ref, dogla_ref, dosb_ref, dggate_ref, dsgate_ref, dm_ref,
             mt_ref, ogt_ref, obt_ref, dx2b_ref, dya_ref, dyb_ref,
             dfg_ref, dbg_ref, dglag_ref, loss_ref):
        @pl.when(pl.program_id(0) == 0)
        def _():
            dfg_ref[...] = jnp.zeros_like(dfg_ref)
            dbg_ref[...] = jnp.zeros_like(dbg_ref)
            dglag_ref[...] = jnp.zeros_like(dglag_ref)
            loss_ref[...] = jnp.zeros_like(loss_ref)

        glag = glag_ref[...]
        ggate = ggate_ref[...]
        sg = _sigmoid(ggate)
        silu_g = ggate * sg
        ohat, rinv, nrm = [], [], []
        for hh in range(GLA_HEADS):
            oh = og_ref[:, hh * GLA_HV:(hh + 1) * GLA_HV]
            r = lax.rsqrt(jnp.mean(oh * oh, axis=-1, keepdims=True) + EPS)
            ohat.append(oh * r)
            rinv.append(r)
            nrm.append(ohat[-1] * glag)
        n_all = jnp.concatenate(nrm, axis=1)
        og = n_all * silu_g
        ogb = _bf(og)
        ya = _dot(ogb, wpa_ref[...])
        sgate = sgate_ref[...]
        ss = _sigmoid(sgate)
        silu_s = sgate * ss
        osb = osb_ref[...]
        ob = osb * silu_s
        obb = _bf(ob)
        yb = _dot(obb, wpb_ref[...])
        ga = _sigmoid(ma_ref[...] + bg_ref[:, :D])
        gb = _sigmoid(mb_ref[...] + bg_ref[:, D:])
        merged = ga * ya + gb * yb
        mgb = _bf(merged)
        x2 = x_ref[...] + _dot(mgb, wo_ref[...])
        r2 = lax.rsqrt(jnp.mean(x2 * x2, axis=-1, keepdims=True) + EPS)
        xh2 = x2 * r2
        fg = fg_ref[...]
        err = xh2 * fg - tgt_ref[...]
        loss_ref[...] += jnp.broadcast_to(
            0.5 * jnp.sum(jnp.mean(err * err, axis=-1, keepdims=True), axis=0, keepdims=True), (1, 128))
        dy = err * (1.0 / D)
        dfg_ref[...] += jnp.sum(dy * xh2, axis=0, keepdims=True)
        dxh = dy * fg
        dx2 = r2 * (dxh - xh2 * jnp.mean(dxh * xh2, axis=-1, keepdims=True))
        dx2_ref[...] = dx2
        dx2b = _bf(dx2)
        dx2b_ref[...] = dx2b
        dmerged = _dot_nt(dx2b, wo_ref[...])
        dya = dmerged * ga
        dyb = dmerged * gb
        dma = dmerged * ya * ga * (1.0 - ga)
        dmb = dmerged * yb * gb * (1.0 - gb)
        dm_ref[:, :D] = _bf(dma)
        dm_ref[:, D:] = _bf(dmb)
        dbg_ref[:, :D] += jnp.sum(dma, axis=0, keepdims=True)
        dbg_ref[:, D:] += jnp.sum(dmb, axis=0, keepdims=True)
        dyab = _bf(dya)
        dybb = _bf(dyb)
        dya_ref[...] = dyab
        dyb_ref[...] = dybb
        dog = _dot_nt(dyab, wpa_ref[...])
        dob = _dot_nt(dybb, wpb_ref[...])
        dosb_ref[...] = dob * silu_s
        dsgate_ref[...] = _bf(dob * osb * (ss * (1.0 + sgate * (1.0 - ss))))
        dn = dog * silu_g
        dggate_ref[...] = _bf(dog * n_all * (sg * (1.0 + ggate * (1.0 - sg))))
        dglag = jnp.zeros((1, GLA_HV), F32)
        for hh in range(GLA_HEADS):
            dnh = dn[:, hh * GLA_HV:(hh + 1) * GLA_HV]
            dglag = dglag + jnp.sum(dnh * ohat[hh], axis=0, keepdims=True)
            dohat = dnh * glag
            dogla_ref[:, hh * GLA_HV:(hh + 1) * GLA_HV] = rinv[hh] * (
                dohat - ohat[hh] * jnp.mean(dohat * ohat[hh], axis=-1, keepdims=True))
        dglag_ref[...] += dglag
        mt_ref[...] = _bf(merged.T)
        ogt_ref[...] = _bf(og.T)
        obt_ref[...] = _bf(ob.T)

    row = lambda i: (i, 0)
    const = lambda i: (0, 0)
    tile = pl.BlockSpec((tm, D), row)
    tile_t = pl.BlockSpec((None, D, tm), lambda i: (i, 0, 0))
    wspec = pl.BlockSpec((D, D), const)
    return pl.pallas_call(
        body, name="mid",
        grid=(T // tm,),
        in_specs=[tile,
                  pl.BlockSpec((None, tm, D), lambda i: (1, i, 0)),
                  tile,
                  pl.BlockSpec((None, tm, D), lambda i: (2, i, 0)),
                  pl.BlockSpec((None, tm, D), lambda i: (3, i, 0)),
                  pl.BlockSpec((None, tm, D), lambda i: (4, i, 0)),
                  tile, tile, wspec, wspec, wspec,
                  pl.BlockSpec((1, GLA_HV), const),
                  pl.BlockSpec((1, 2 * D), const),
                  pl.BlockSpec((1, D), const)],
        out_specs=(tile, tile, tile, tile, tile,
                   pl.BlockSpec((tm, 2 * D), row),
                   tile_t, tile_t, tile_t, tile, tile, tile,
                   pl.BlockSpec((1, D), const),
                   pl.BlockSpec((1, 2 * D), const),
                   pl.BlockSpec((1, GLA_HV), const),
                   pl.BlockSpec((1, 128), const)),
        out_shape=(jax.ShapeDtypeStruct((T, D), F32),
                   jax.ShapeDtypeStruct((T, D), F32),
                   jax.ShapeDtypeStruct((T, D), F32),
                   jax.ShapeDtypeStruct((T, D), BF16),
                   jax.ShapeDtypeStruct((T, D), BF16),
                   jax.ShapeDtypeStruct((T, 2 * D), BF16),
                   jax.ShapeDtypeStruct((T // tm, D, tm), BF16),
                   jax.ShapeDtypeStruct((T // tm, D, tm), BF16),
                   jax.ShapeDtypeStruct((T // tm, D, tm), BF16),
                   jax.ShapeDtypeStruct((T, D), BF16),
                   jax.ShapeDtypeStruct((T, D), BF16),
                   jax.ShapeDtypeStruct((T, D), BF16),
                   jax.ShapeDtypeStruct((1, D), F32),
                   jax.ShapeDtypeStruct((1, 2 * D), F32),
                   jax.ShapeDtypeStruct((1, GLA_HV), F32),
                   jax.ShapeDtypeStruct((1, 128), F32)),
        compiler_params=_cparams(("arbitrary",)),
    )(o_gla, projf, o_sb, projf, projf, projf, x, target, wpa, wpb, wo, gla_g, b_gate, final_g)


def _dh_call(pieces, dmlog, drank, wt, wr, x, dx2, norm_g, s_in):
    T, D = x.shape
    tm = min(256, T)
    npc = len(pieces)
    n_main = N_GROUPS * 1024
    n_i = T // tm

    def body(*refs):
        pcs = refs[:npc]
        (dm_ref, dr_ref, w_hbm, wr_ref, x_ref, dx2_ref, g_ref, sin_ref,
         gx_ref, dg_ref, rin_ref, w_scr, sems, send_sems, recv_sems, loc_sem) = refs[npc:]
        own, pairs = _chip_copies(sin_ref, rin_ref, send_sems, recv_sems, loc_sem)

        @pl.when(pl.program_id(0) == 0)
        def _():
            _push_start(own, pairs)
            lo = pltpu.make_async_copy(w_hbm.at[pl.ds(0, RANK_COL)], w_scr.at[pl.ds(0, RANK_COL)], sems.at[0])
            hi = pltpu.make_async_copy(w_hbm.at[pl.ds(RANK_COL + GLA_RANK, n_main - RANK_COL)],
                                       w_scr.at[pl.ds(RANK_COL, n_main - RANK_COL)], sems.at[1])
            lo.start()
            hi.start()
            dg_ref[...] = jnp.zeros_like(dg_ref)
            lo.wait()
            hi.wait()

        def w_group(g):
            return w_scr[g * 1024:(g + 1) * 1024, :]

        dr = dr_ref[...]
        dh = _dot(dr, wr_ref[...])
        for g in range(npc):
            dh = dh + _dot(pcs[g][...], w_group(g))
        dh = dh + _dot(dm_ref[:, :D], w_group(npc))
        dh = dh + _dot(dm_ref[:, D:], w_group(npc + 1))
        xv = x_ref[...]
        r = lax.rsqrt(jnp.mean(xv * xv, axis=-1, keepdims=True) + EPS)
        xhat = xv * r
        g = g_ref[...]
        dg_ref[...] += jnp.sum(dh * xhat, axis=0, keepdims=True)
        dxhat = dh * g
        gx_ref[...] = r * (dxhat - xhat * jnp.mean(dxhat * xhat, axis=-1, keepdims=True)) + dx2_ref[...]

        @pl.when(pl.program_id(0) == n_i - 1)
        def _():
            _push_wait(own, pairs)

    row = lambda i: (i, 0)
    const = lambda i: (0, 0)
    tile = pl.BlockSpec((tm, D), row)
    return pl.pallas_call(
        body, name="dh",
        grid=(n_i,),
        in_specs=[tile] * npc + [
            pl.BlockSpec((tm, 2 * D), row),
            pl.BlockSpec((tm, 128), row),
            _ANY,
            pl.BlockSpec((128, D), const),
            tile, tile,
            pl.BlockSpec((1, D), const),
            _ANY],
        out_specs=(tile, pl.BlockSpec((1, D), const), _ANY),
        out_shape=(jax.ShapeDtypeStruct((T, D), F32),
                   jax.ShapeDtypeStruct((1, D), F32),
                   jax.ShapeDtypeStruct(s_in.shape, s_in.dtype)),
        scratch_shapes=[pltpu.VMEM((n_main, D), BF16), pltpu.SemaphoreType.DMA((2,))] + _CHIP_SEMS,
        compiler_params=_cparams(("arbitrary",)),
    )(*pieces, dmlog, drank, wt, wr, x, dx2, norm_g, s_in)


def _wgrad_rank_call(ht, drank):
    n_tb, D, tb = ht.shape

    def body(ht_ref, dr_ref, o_ref):
        @pl.when(pl.program_id(0) == 0)
        def _():
            o_ref[...] = jnp.zeros_like(o_ref)

        o_ref[...] += _dot(ht_ref[...], dr_ref[...])

    return pl.pallas_call(
        body, name="wgrad_rank",
        grid=(n_tb,),
        in_specs=[pl.BlockSpec((None, D, tb), lambda i: (i, 0, 0)),
                  pl.BlockSpec((tb, 128), lambda i: (i, 0))],
        out_specs=pl.BlockSpec((D, 128), lambda i: (0, 0)),
        out_shape=jax.ShapeDtypeStruct((D, 128), F32),
        compiler_params=_cparams(("arbitrary",)),
    )(ht, drank)


def _wgrad_call(lhs_list, lhs_of_group, rhs_list, rhs_of_group, n_transposed, name):
    n_groups = len(rhs_of_group)
    n_tb, D, tb = lhs_list[0].shape
    T = n_tb * tb
    per = min(2, n_tb)
    tk = per * tb
    nk = T // tk
    nl = len(lhs_list)

    def body(*refs):
        lhs = refs[:nl]
        rhs = refs[nl:nl + n_groups]
        out_ref, acc = refs[nl + n_groups:]
        g = pl.program_id(0)
        i = pl.program_id(1)

        @pl.when(i == 0)
        def _():
            acc[...] = jnp.zeros_like(acc)

        for p in range(n_groups):
            @pl.when(g == p)
            def _(p=p):
                lref = lhs[lhs_of_group[p]]
                part = _dot(lref[0], rhs[p][0:tb, :])
                for b in range(1, per):
                    part = part + _dot(lref[b], rhs[p][b * tb:(b + 1) * tb, :])
                acc[...] += part

        @pl.when((i == nk - 1) & (g < n_transposed))
        def _():
            out_ref[...] = _bf(acc[...].T)

        @pl.when((i == nk - 1) & (g >= n_transposed))
        def _():
            out_ref[...] = _bf(acc[...])

    def lhs_spec(a):
        groups = [g for g in range(n_groups) if lhs_of_group[g] == a]
        lo, hi = min(groups), max(groups)
        assert groups == list(range(lo, hi + 1))
        return pl.BlockSpec((per, D, tb), lambda g, i: (jnp.where((g >= lo) & (g <= hi), i, 0), 0, 0))

    def rhs_spec(p):
        cb = rhs_of_group[p][1]
        return pl.BlockSpec((tk, 1024), lambda g, i: (jnp.where(g == p, i, 0), cb))

    return pl.pallas_call(
        body, name=name,
        grid=(n_groups, nk),
        in_specs=[lhs_spec(a) for a in range(nl)] + [rhs_spec(p) for p in range(n_groups)],
        out_specs=pl.BlockSpec((None, D, 1024), lambda g, i: (g, 0, 0)),
        out_shape=jax.ShapeDtypeStruct((n_groups, D, 1024), BF16),
        scratch_shapes=[pltpu.VMEM((D, 1024), F32)],
        compiler_params=_cparams(("arbitrary", "arbitrary")),
    )(*lhs_list, *[rhs_list[rhs_of_group[p][0]] for p in range(n_groups)])


def _adamw_math(parts, w, m, v):
    g = parts[0].astype(F32)
    for p in parts[1:]:
        g = g + p.astype(F32)
    mm = ADAM_B1 * m + (1.0 - ADAM_B1) * g
    vv = ADAM_B2 * v + (1.0 - ADAM_B2) * (g * g)
    m_hat = mm / (1.0 - ADAM_B1 ** ADAM_STEP)
    v_hat = vv / (1.0 - ADAM_B2 ** ADAM_STEP)
    return g, -ADAM_LR * (m_hat / (jnp.sqrt(v_hat) + ADAM_EPS) + ADAM_WD * w), mm, vv


def _part_order(n_parts):
    return [n_parts - 1] + list(range(n_parts - 1))


def _adamw_call(parts, w, m, v, name):
    R, C = w.shape
    n_parts = parts.shape[0]
    (tr, tc), grid, idx = _tiling_2d(R, C)

    def body(p_ref, w_ref, m_ref, v_ref, g_ref, d_ref, nm_ref, nv_ref):
        g_ref[...], d_ref[...], nm_ref[...], nv_ref[...] = _adamw_math(
            [p_ref[k] for k in _part_order(n_parts)], w_ref[...], m_ref[...], v_ref[...])

    blk = pl.BlockSpec((tr, tc), idx)
    sds = jax.ShapeDtypeStruct((R, C), F32)
    return pl.pallas_call(
        body, name=name,
        grid=grid,
        in_specs=[pl.BlockSpec((n_parts, tr, tc), lambda i: (0,) + idx(i)), blk, blk, blk],
        out_specs=(blk, blk, blk, blk),
        out_shape=(sds, sds, sds, sds),
        compiler_params=_cparams(("arbitrary",)),
    )(parts, w, m, v)


def _adamw_rows_call(parts, ws, ms, vs, name, gathered):
    n = len(ws)
    R, C = ws[0].shape
    n_parts = parts.shape[0]

    def body(*refs):
        p_ref = refs[0]
        w_refs, m_refs, v_refs = refs[1:1 + n], refs[1 + n:1 + 2 * n], refs[1 + 2 * n:1 + 3 * n]
        src_ref = refs[1 + 3 * n]
        outs = refs[2 + 3 * n:2 + 7 * n]
        dst_ref, send_sems, recv_sems, loc_sem = refs[2 + 7 * n:]
        own, pairs = _push_copies(src_ref, dst_ref, send_sems, recv_sems, loc_sem, scatter=False)
        k_now = pl.program_id(0)

        @pl.when(k_now == 0)
        def _():
            _push_start(own, pairs)

        for k in range(n):
            @pl.when(k_now == k)
            def _(k=k):
                res = _adamw_math([p_ref[j] for j in _part_order(n_parts)],
                                  w_refs[k][...], m_refs[k][...], v_refs[k][...])
                for o_ref, val in zip(outs[4 * k:4 * k + 4], res):
                    o_ref[...] = val

        @pl.when(k_now == n - 1)
        def _():
            _push_wait(own, pairs)

    whole = pl.BlockSpec((R, C), lambda k: (0, 0))
    sds = jax.ShapeDtypeStruct((R, C), F32)
    res = pl.pallas_call(
        body, name=name,
        grid=(n,),
        in_specs=[pl.BlockSpec((n_parts, R, C), lambda k: (0, k, 0))] + [whole] * (3 * n) + [_ANY],
        out_specs=tuple([whole] * (4 * n) + [_ANY]),
        out_shape=tuple([sds] * (4 * n) + [jax.ShapeDtypeStruct((N_DEV,) + gathered.shape, gathered.dtype)]),
        scratch_shapes=_PUSH_SEMS,
        compiler_params=_cparams(("arbitrary",)),
    )(parts, *ws, *ms, *vs, gathered)
    return [res[4 * k:4 * k + 4] for k in range(n)], res[4 * n]


def _adamw_lanes_call(parts, offsets, ws, ms, vs, name):
    n = len(ws)
    n_parts = parts.shape[0]

    def body(*refs):
        p_ref = refs[0]
        w_refs, m_refs, v_refs = refs[1:1 + n], refs[1 + n:1 + 2 * n], refs[1 + 2 * n:1 + 3 * n]
        outs = refs[1 + 3 * n:]
        for k in range(n):
            lanes = slice(offsets[k], offsets[k] + ws[k].shape[1])
            res = _adamw_math([p_ref[j, :, lanes] for j in _part_order(n_parts)],
                              w_refs[k][...], m_refs[k][...], v_refs[k][...])
            for o_ref, val in zip(outs[4 * k:4 * k + 4], res):
                o_ref[...] = val

    res = pl.pallas_call(
        body, name=name,
        out_shape=tuple(jax.ShapeDtypeStruct(ws[k].shape, F32) for k in range(n) for _ in range(4)),
        compiler_params=_cparams(),
    )(parts, *ws, *ms, *vs)
    return [res[4 * k:4 * k + 4] for k in range(n)]


def _local_step(x, target, wt, wr, wdec, bdec, wp_shard, norm_g, gla_g, b_gate, final_g):
    D = x.shape[1]
    half = wp_shard.shape[1] // 2
    projf, projb, rank, ht, wp_lo = _proj_call(x, norm_g, wt, wr, wp_shard[:, :half])
    o_gla, st_all, la = _gla_fwd_call(projf, projb, rank, wdec, bdec)
    o_sb, wp_hi = _sb_fwd_call(projb, wp_shard[:, half:])
    wp_full = jnp.concatenate([wp_lo, wp_hi], axis=2).transpose(1, 0, 2, 3).reshape(3, D, D)
    (dx2, do_gla, do_sb, dggate, dsgate, dmlog, mt, ogt, obt, dx2b, dya, dyb,
     dfinal_g, db_gate, dgla_g, loss) = _mid_call(o_gla, o_sb, projf, x, target, wp_full[0], wp_full[1],
                                                 wp_full[2], gla_g, b_gate, final_g)
    dw_p = _wgrad_call([ogt, obt, mt], [0, 1, 2], [dya, dyb, dx2b], [(0, 0), (1, 0), (2, 0)], 0, "wgrad_p")
    g_p = dw_p.reshape(3, N_DEV, D // N_DEV, D).transpose(1, 0, 2, 3).reshape(N_DEV, 3 * (D // N_DEV), D)
    dqk, dgv, drank, dwdec, dbdec = _gla_bwd_call(projf, projb, la, do_gla, st_all, rank, wdec)
    dsq, dsk, dsv, r_p = _sb_bwd_call(projb, do_sb, g_p)
    pieces = [dqk, dgv, dggate, dsq, dsk, dsv, dsgate]
    rhs_of_group = [(g, 0) for g in range(7)] + [(7, 0), (7, 1)]
    dw_in = _wgrad_call([ht], [0] * N_GROUPS, pieces + [dmlog], rhs_of_group, N_GROUPS, "wgrad_in")
    dwr = _wgrad_rank_call(ht, drank)
    g_in = _parts_by_device(dw_in.reshape(N_GROUPS * 1024, D), dwr[:, :GLA_RANK].T.astype(BF16))
    c_idx = lax.axis_index("c").astype(jnp.int32).reshape(1)
    (p_in,) = _pair_exchange([g_in], "pair_g")
    s_in = _pair_add_call(g_in, p_in, c_idx, "pair_add_in")
    grad_x, dnorm_g, r_in = _dh_call(pieces, dmlog, drank, wt, wr, x, dx2, norm_g, s_in)
    small = jnp.concatenate([
        dnorm_g.reshape(-1), dbdec.reshape(-1), dgla_g.reshape(-1), db_gate.reshape(-1), dfinal_g.reshape(-1),
        loss.reshape(-1), dwdec[:GLA_RANK].reshape(-1)]).reshape(1, _SM_LEN)
    return grad_x, r_in, r_p, small


def _parts_by_device(dmain, drank):
    def part_for(p):
        lo, hi = p * SHARD_COLS, (p + 1) * SHARD_COLS
        pieces = []
        if lo < RANK_COL:
            pieces.append(dmain[lo:min(hi, RANK_COL)])
        if lo < RANK_COL + GLA_RANK and hi > RANK_COL:
            pieces.append(drank[max(lo, RANK_COL) - RANK_COL:min(hi, RANK_COL + GLA_RANK) - RANK_COL])
        if hi > RANK_COL + GLA_RANK:
            pieces.append(dmain[max(lo, RANK_COL + GLA_RANK) - GLA_RANK:hi - GLA_RANK])
        return pieces[0] if len(pieces) == 1 else jnp.concatenate(pieces, axis=0)

    return jnp.stack([part_for(p) for p in range(N_DEV)])


_SM_NORM = 0
_SM_BDEC = _SM_NORM + D_MODEL
_SM_GLAG = _SM_BDEC + GLA_DK
_SM_BGATE = _SM_GLAG + GLA_HV
_SM_FINAL = _SM_BGATE + 2 * D_MODEL
_SM_REPL = _SM_FINAL + D_MODEL
_SM_LOSS = _SM_REPL
_SM_WDEC = _SM_LOSS + 128
_SM_LEN = _SM_WDEC + GLA_RANK * GLA_DK


def kernel(x, norm_g, w_in, w_dec_up, b_dec, gla_norm_g, w_pa, w_pb, b_gate, w_o, final_g, loss_target, m_norm_g, m_w_in, m_w_dec_up, m_b_dec, m_gla_norm_g, m_w_pa, m_w_pb, m_b_gate, m_w_o, m_final_g, v_norm_g, v_w_in, v_w_dec_up, v_b_dec, v_gla_norm_g, v_w_pa, v_w_pb, v_b_gate, v_w_o, v_final_g):
    D = D_MODEL
    me = 4 * lax.axis_index("x") + 2 * lax.axis_index("y") + lax.axis_index("c")

    wp_shard = jnp.stack([w_pa, w_pb, w_o]).astype(BF16)
    win_all, wdec_all = _all_gather([w_in.T.astype(BF16), w_dec_up], "gather_w")
    wt = win_all.reshape(IN_COLS, D)
    wr = jnp.pad(wt[RANK_COL:RANK_COL + GLA_RANK], ((0, 128 - GLA_RANK), (0, 0)))
    wdec_full = wdec_all.transpose(1, 0, 2).reshape(GLA_RANK, GLA_DK)
    wdec = jnp.pad(wdec_full, ((0, 128 - GLA_RANK), (0, 0)))

    grad_x, r_in, r_p, small = _local_step(
        x[0], loss_target[0], wt, wr, wdec, b_dec.reshape(1, -1), wp_shard,
        norm_g.reshape(1, -1), gla_norm_g.reshape(1, -1), b_gate.reshape(1, -1), final_g.reshape(1, -1))

    gw_in, d_in, nm_in, nv_in = (a.T for a in _adamw_call(r_in, w_in.T, m_w_in.T, v_w_in.T, "adamw_in"))
    ((g_pa, d_pa, nm_pa, nv_pa), (g_pb, d_pb, nm_pb, nv_pb), (g_o, d_o, nm_o, nv_o)), r_small = _adamw_rows_call(
        r_p, [w_pa, w_pb, w_o], [m_w_pa, m_w_pb, m_w_o], [v_w_pa, v_w_pb, v_w_o], "adamw_p", small)

    def row(a):
        return a.reshape(1, -1)

    rep = _adamw_lanes_call(
        r_small, [_SM_NORM, _SM_BDEC, _SM_GLAG, _SM_BGATE, _SM_FINAL],
        [row(a) for a in (norm_g, b_dec, gla_norm_g, b_gate, final_g)],
        [row(a) for a in (m_norm_g, m_b_dec, m_gla_norm_g, m_b_gate, m_final_g)],
        [row(a) for a in (v_norm_g, v_b_dec, v_gla_norm_g, v_b_gate, v_final_g)], "adamw_rep")
    ((g_norm, d_norm, nm_norm, nv_norm), (g_bdec, d_bdec, nm_bdec, nv_bdec), (g_glag, d_glag, nm_glag, nv_glag),
     (g_bgate, d_bgate, nm_bgate, nv_bgate), (g_final, d_final, nm_final, nv_final)) = [
        tuple(a.reshape(-1) for a in quad) for quad in rep]

    wdec_parts = r_small[:, 0, _SM_WDEC:].reshape(N_DEV, GLA_RANK, GLA_DK)
    cols = GLA_DK // N_DEV
    wdec_mine = lax.dynamic_slice_in_dim(wdec_parts, me * cols, cols, axis=2)
    g_wdec, d_wdec, nm_wdec, nv_wdec = _adamw_call(wdec_mine, w_dec_up, m_w_dec_up, v_w_dec_up, "adamw_dec")

    loss_total = jnp.sum(r_small[:, 0, _SM_LOSS])

    return (loss_total, grad_x[None],
            g_norm, gw_in, g_wdec, g_bdec, g_glag, g_pa, g_pb, g_bgate, g_o, g_final,
            d_norm, d_in, d_wdec, d_bdec, d_glag, d_pa, d_pb, d_bgate, d_o, d_final,
            nm_norm, nm_in, nm_wdec, nm_bdec, nm_glag, nm_pa, nm_pb, nm_bgate, nm_o, nm_final,
            nv_norm, nv_in, nv_wdec, nv_bdec, nv_glag, nv_pa, nv_pb, nv_bgate, nv_o, nv_final)
```

```python
import math

import jax
import jax.numpy as jnp
from jax import lax
from jax.experimental import pallas as pl
from jax.experimental.pallas import tpu as pltpu

F32 = jnp.float32
BF16 = jnp.bfloat16

N_DEV = 8
D_MODEL = 1024
GLA_HEADS = 4
GLA_HK = 128
GLA_HV = 256
GLA_DK = 512
GLA_RANK = 16
GLA_TAU = 16.0
GLA_CHUNK = 64
SB_HEADS = 8
SB_HD = 128
EPS = 1e-6
N_GROUPS = 9
RANK_COL = 3072
IN_COLS = 9232
SHARD_COLS = IN_COLS // N_DEV

ADAM_LR = 0.001
ADAM_B1 = 0.9
ADAM_B2 = 0.999
ADAM_EPS = 1e-08
ADAM_WD = 0.01
ADAM_STEP = 10

VMEM_LIMIT = 56 * 1024 * 1024
TBLK = 256


def _cparams(sem=None):
    return pltpu.CompilerParams(dimension_semantics=sem, vmem_limit_bytes=VMEM_LIMIT)


def _tiling_2d(rows, cols):
    if rows * cols <= 128 * 1024:
        return (rows, cols), (1,), lambda i: (0, 0)
    if rows % 128 == 0:
        return (128, cols), (rows // 128,), lambda i: (i, 0)
    tc = 256 if cols % 256 == 0 else cols
    return (rows, tc), (cols // tc,), lambda i: (0, i)


def _dot(a, b):
    return jnp.dot(a, b, preferred_element_type=F32)


def _dot_nt(a, b):
    return lax.dot_general(a, b, (((1,), (1,)), ((), ())), preferred_element_type=F32)


def _dot_tn(a, b):
    return lax.dot_general(a, b, (((0,), (0,)), ((), ())), preferred_element_type=F32)


def _bf(x):
    return x.astype(BF16)


def _split3(x):
    hi = x.astype(BF16)
    r = x - hi.astype(F32)
    mid = r.astype(BF16)
    lo = (r - mid.astype(F32)).astype(BF16)
    return hi, mid, lo


def _tri_left(tri, x):
    hi, mid, lo = _split3(x)
    return _dot(tri, hi) + _dot(tri, mid) + _dot(tri, lo)


def _split2(x):
    hi = lax.bitcast_convert_type(lax.bitcast_convert_type(x, jnp.uint32) & jnp.uint32(0xFFFF0000), F32)
    return hi.astype(BF16), (x - hi).astype(BF16)


def _tri2_left(tri, x):
    hi, lo = _split2(x)
    return _dot(tri, hi) + _dot(tri, lo)


def _tri2_right(x, tri):
    hi, lo = _split2(x)
    return _dot(hi, tri) + _dot(lo, tri)


def _iota2(n, m, dim):
    return lax.broadcasted_iota(jnp.int32, (n, m), dim)


def _sigmoid(x):
    return 1.0 / (1.0 + jnp.exp(-x))


def _softplus_neg_abs(z):
    return jnp.log(1.0 + jnp.exp(-jnp.abs(z)))


_ANY = pl.BlockSpec(memory_space=pl.ANY)


def _mesh_pos():
    return lax.axis_index("x"), lax.axis_index("y"), lax.axis_index("c")


def _other_chips(x, y):
    return [(1 - x, y), (x, 1 - y), (1 - x, 1 - y)]


def _rcopy(src, dst, send_sem, recv_sem, to):
    return pltpu.make_async_remote_copy(src_ref=src, dst_ref=dst, send_sem=send_sem, recv_sem=recv_sem,
                                        device_id=to, device_id_type=pl.DeviceIdType.MESH)


def _push_copies(src_ref, dst_ref, send_sems, recv_sems, loc_sem, scatter):
    x, y, c = _mesh_pos()
    me = 4 * x + 2 * y + c
    own = pltpu.make_async_copy(src_ref.at[me] if scatter else src_ref, dst_ref.at[me], loc_sem)
    pairs = []
    for k in range(1, N_DEV):
        px = 1 - x if k & 4 else x
        py = 1 - y if k & 2 else y
        pc = 1 - c if k & 1 else c
        pid = 4 * px + 2 * py + pc
        src = src_ref.at[pid] if scatter else src_ref
        send = _rcopy(src, dst_ref.at[me], send_sems.at[k - 1], recv_sems.at[k - 1], (px, py, pc))
        recv = _rcopy(src, dst_ref.at[pid], send_sems.at[k - 1], recv_sems.at[k - 1], (px, py, pc))
        pairs.append((send, recv))
    return own, pairs


def _push_start(own, pairs):
    own.start()
    for send, _ in pairs:
        send.start()


def _push_wait(own, pairs):
    for _, recv in pairs:
        recv.wait_recv()
    for send, _ in pairs:
        send.wait_send()
    own.wait()


_PUSH_SEMS = [pltpu.SemaphoreType.DMA((N_DEV - 1,)), pltpu.SemaphoreType.DMA((N_DEV - 1,)),
              pltpu.SemaphoreType.DMA]


def _chip_copies(src_ref, dst_ref, send_sems, recv_sems, loc_sem):
    x, y, c = _mesh_pos()
    own = pltpu.make_async_copy(src_ref.at[2 * x + y], dst_ref.at[3], loc_sem)
    pairs = []
    for j, (px, py) in enumerate(_other_chips(x, y)):
        cp = _rcopy(src_ref.at[2 * px + py], dst_ref.at[j], send_sems.at[j], recv_sems.at[j], (px, py, c))
        pairs.append((cp, cp))
    return own, pairs


_CHIP_SEMS = [pltpu.SemaphoreType.DMA((3,)), pltpu.SemaphoreType.DMA((3,)), pltpu.SemaphoreType.DMA]


def _all_gather(arrs, name):
    n = len(arrs)

    def body(*refs):
        ins = refs[:n]
        outs = refs[n:2 * n]
        send_sems, recv_sems, loc_sems = refs[2 * n:]
        x, y, c = _mesh_pos()
        sib = (x, y, 1 - c)
        chips = _other_chips(x, y)

        def place(a, px, py, pc):
            return outs[a].at[4 * px + 2 * py + pc]

        def copy(a, k, block, to, src=None):
            dst = place(a, *block)
            return _rcopy(dst if src is None else src, dst, send_sems.at[a, k], recv_sems.at[a, k], to)

        mine = [pltpu.make_async_copy(ins[a], place(a, x, y, c), loc_sems.at[a]) for a in range(n)]
        for cp in mine:
            cp.start()
        first = [copy(a, 0, (x, y, c), sib, src=ins[a]) for a in range(n)]
        for j, chip in enumerate(chips):
            first += [copy(a, 1 + j, (x, y, c), (*chip, c), src=ins[a]) for a in range(n)]
        for cp in first:
            cp.start()
        passed = []
        for j, chip in enumerate(chips):
            for a in range(n):
                copy(a, 1 + j, (*chip, c), (x, y, c)).wait_recv()
                fwd = copy(a, 4 + j, (*chip, c), sib)
                fwd.start()
                passed.append(fwd)
        for a in range(n):
            copy(a, 0, sib, (x, y, c)).wait_recv()
        for j, chip in enumerate(chips):
            for a in range(n):
                copy(a, 4 + j, (*chip, 1 - c), (x, y, c)).wait_recv()
        for cp in first + passed:
            cp.wait_send()
        for cp in mine:
            cp.wait()

    return pl.pallas_call(
        body, name=name,
        out_shape=tuple(jax.ShapeDtypeStruct((N_DEV,) + a.shape, a.dtype) for a in arrs),
        in_specs=[_ANY] * n,
        out_specs=tuple([_ANY] * n),
        scratch_shapes=[pltpu.SemaphoreType.DMA((n, 7)), pltpu.SemaphoreType.DMA((n, 7)),
                        pltpu.SemaphoreType.DMA((n,))],
    )(*arrs)


def _pair_exchange(arrs, name):
    n = len(arrs)

    def body(*refs):
        ins = refs[:n]
        outs = refs[n:2 * n]
        send_sems, recv_sems = refs[2 * n:]
        x, y, c = _mesh_pos()
        copies = []
        for a in range(n):
            for q in range(4):
                cp = _rcopy(ins[a].at[2 * q + (1 - c)], outs[a].at[q], send_sems.at[a, q], recv_sems.at[a, q],
                            (x, y, 1 - c))
                cp.start()
                copies.append(cp)
        for cp in copies:
            cp.wait_recv()
        for cp in copies:
            cp.wait_send()

    return pl.pallas_call(
        body, name=name,
        out_shape=tuple(jax.ShapeDtypeStruct((4,) + a.shape[1:], a.dtype) for a in arrs),
        in_specs=[_ANY] * n,
        out_specs=tuple([_ANY] * n),
        scratch_shapes=[pltpu.SemaphoreType.DMA((n, 4)), pltpu.SemaphoreType.DMA((n, 4))],
    )(*arrs)


def _pair_add_call(parts, recv, c_idx, name):
    _, R, C = parts.shape
    (tr, tc), (steps,), idx = _tiling_2d(R, C)

    def body(c_ref, p_ref, r_ref, o_ref):
        o_ref[...] = (p_ref[...].astype(F32) + r_ref[...].astype(F32)).astype(o_ref.dtype)

    return pl.pallas_call(
        body, name=name,
        grid_spec=pltpu.PrefetchScalarGridSpec(
            num_scalar_prefetch=1,
            grid=(4, steps),
            in_specs=[pl.BlockSpec((None, tr, tc), lambda q, i, c_ref: (2 * q + c_ref[0],) + idx(i)),
                      pl.BlockSpec((None, tr, tc), lambda q, i, c_ref: (q,) + idx(i))],
            out_specs=pl.BlockSpec((None, tr, tc), lambda q, i, c_ref: (q,) + idx(i))),
        out_shape=jax.ShapeDtypeStruct((4, R, C), parts.dtype),
        compiler_params=_cparams(("arbitrary", "arbitrary")),
    )(c_idx, parts, recv)


def _group_row(g):
    return GLA_RANK * (g * (1024 // GLA_RANK) + (g >= RANK_COL // 1024))


def _proj_call(x, norm_g, wt, wr, wp_part):
    T, D = x.shape
    tm = min(1024, T)
    assert tm % TBLK == 0
    n_i = T // tm

    def f_slot(j):
        return ((j >= 2).astype(jnp.int32) + (j >= 6).astype(jnp.int32)
                + (j >= 7).astype(jnp.int32) + (j >= 8).astype(jnp.int32))

    def b_slot(j):
        return (j >= 3).astype(jnp.int32) + (j >= 4).astype(jnp.int32) + (j >= 5).astype(jnp.int32)

    def body(x_ref, g_ref, w_ref, wr_ref, wp_ref, pf_ref, pb_ref, rank_ref, ht_ref, wpall_ref,
             h_scr, send_sems, recv_sems, loc_sem):
        i = pl.program_id(0)
        j = pl.program_id(1)
        own, pairs = _push_copies(wp_ref, wpall_ref, send_sems, recv_sems, loc_sem, scatter=False)

        @pl.when((i == 0) & (j == 0))
        def _():
            _push_start(own, pairs)

        @pl.when(j == 0)
        def _():
            xv = x_ref[...]
            r = lax.rsqrt(jnp.mean(xv * xv, axis=-1, keepdims=True) + EPS)
            h = (xv * r) * g_ref[...]
            hb = _bf(h)
            h_scr[...] = hb
            for b in range(tm // TBLK):
                ht_ref[b] = _bf(h[b * TBLK:(b + 1) * TBLK].T)
            rank_ref[...] = _dot_nt(hb, wr_ref[...])

        is_b = (j == 1) | ((j >= 3) & (j <= 5))

        @pl.when(is_b)
        def _():
            pb_ref[...] = _bf(_dot_nt(h_scr[...], w_ref[...]))

        @pl.when(jnp.logical_not(is_b))
        def _():
            pf_ref[...] = _dot_nt(h_scr[...], w_ref[...])

        @pl.when((i == n_i - 1) & (j == N_GROUPS - 1))
        def _():
            _push_wait(own, pairs)

    return pl.pallas_call(
        body, name="proj",
        grid=(n_i, N_GROUPS),
        in_specs=[pl.BlockSpec((tm, D), lambda i, j: (i, 0)),
                  pl.BlockSpec((1, D), lambda i, j: (0, 0)),
                  pl.BlockSpec((pl.Element(1024), pl.Element(D)), lambda i, j: (_group_row(j), 0)),
                  pl.BlockSpec((128, D), lambda i, j: (0, 0)),
                  _ANY],
        out_specs=(pl.BlockSpec((None, tm, 1024), lambda i, j: (f_slot(j), i, 0)),
                   pl.BlockSpec((None, tm, 1024), lambda i, j: (b_slot(j), i, 0)),
                   pl.BlockSpec((tm, 128), lambda i, j: (i, 0)),
                   pl.BlockSpec((tm // TBLK, D, TBLK), lambda i, j: (i, 0, 0)),
                   _ANY),
        out_shape=(jax.ShapeDtypeStruct((5, T, 1024), F32),
                   jax.ShapeDtypeStruct((4, T, 1024), BF16),
                   jax.ShapeDtypeStruct((T, 128), F32),
                   jax.ShapeDtypeStruct((T // TBLK, D, TBLK), BF16),
                   jax.ShapeDtypeStruct((N_DEV,) + wp_part.shape, wp_part.dtype)),
        scratch_shapes=[pltpu.VMEM((tm, D), BF16)] + _PUSH_SEMS,
        compiler_params=_cparams(("arbitrary", "arbitrary")),
    )(x, norm_g, wt, wr, wp_part)


GLA_STEP_CHUNKS = 2


def _gla_same_chunk(rows):
    return (_iota2(rows, rows, 0) & -GLA_CHUNK) == (_iota2(rows, rows, 1) & -GLA_CHUNK)


def _gla_chunk_terms(la, q, k, n_c):
    C = GLA_CHUNK
    rows = n_c * C
    low = _gla_same_chunk(rows) & (_iota2(rows, rows, 0) >= _iota2(rows, rows, 1))
    b = _tri_left(_bf(low.astype(F32)), la)
    bl = [b[(c + 1) * C - 1:(c + 1) * C, :] for c in range(n_c)]
    bl_rows = jnp.concatenate([jnp.broadcast_to(bl[c], (C, b.shape[1])) for c in range(n_c)], axis=0)
    eb = jnp.exp(b)
    enb = jnp.exp(-b)
    ebl_b = jnp.exp(bl_rows - b)
    scale = GLA_HK ** -0.5
    qe = q * eb * scale
    ke = k * enb
    kd = k * ebl_b
    return bl, eb, enb, ebl_b, qe, ke, kd


def _gla_fwd_call(projf, projb, rank, wdec, bdec):
    T = projf.shape[1]
    C = GLA_CHUNK
    n_chunks = T // C
    n_c = GLA_STEP_CHUNKS
    R = n_c * C
    assert n_chunks % n_c == 0

    def body(qk_ref, v_ref, rank_ref, wd_ref, bd_ref, o_ref, st_ref, la_ref, st_scr):
        @pl.when(pl.program_id(0) == 0)
        def _():
            st_scr[...] = jnp.zeros_like(st_scr)

        dec = _dot(_bf(rank_ref[...]), _bf(wd_ref[...])) + bd_ref[...]
        la = (jnp.minimum(dec, 0.0) - _softplus_neg_abs(dec)) / GLA_TAU
        la_ref[...] = la
        mask = _gla_same_chunk(R) & (_iota2(R, R, 0) >= _iota2(R, R, 1))
        bl, _, _, _, qe, ke, kd = _gla_chunk_terms(la, qk_ref[:, :GLA_DK], qk_ref[:, GLA_DK:], n_c)
        qeb, keb, kdb = _bf(qe), _bf(ke), _bf(kd)
        ebl = [jnp.exp(bl[c]) for c in range(n_c)]
        heads = range(GLA_HEADS)
        ks = [slice(hh * GLA_HK, (hh + 1) * GLA_HK) for hh in heads]
        vs = [slice(hh * GLA_HV, (hh + 1) * GLA_HV) for hh in heads]
        rs = [slice(c * C, (c + 1) * C) for c in range(n_c)]
        p = [_bf(jnp.where(mask, _dot_nt(qeb[:, ks[hh]], keb[:, ks[hh]]), 0.0)) for hh in heads]
        upd = [[_dot_tn(v_ref[rs[c], vs[hh]], kdb[rs[c], ks[hh]]) for hh in heads] for c in range(n_c)]
        intra = [_dot(p[hh], v_ref[:, vs[hh]]) for hh in heads]
        st = [st_scr[hh] for hh in heads]
        for c in range(n_c):
            inter = [_dot_nt(qeb[rs[c], ks[hh]], _bf(st[hh])) for hh in heads]
            for hh in heads:
                st_ref[c, hh] = st[hh]
                o_ref[rs[c], vs[hh]] = intra[hh][rs[c]] + inter[hh]
            st = [st[hh] * ebl[c][:, ks[hh]] + upd[c][hh] for hh in heads]
        for hh in heads:
            st_scr[hh] = st[hh]

    return pl.pallas_call(
        body, name="gla_fwd",
        grid=(n_chunks // n_c,),
        in_specs=[pl.BlockSpec((None, R, 1024), lambda n: (0, n, 0)),
                  pl.BlockSpec((None, R, 1024), lambda n: (0, n, 0)),
                  pl.BlockSpec((R, 128), lambda n: (n, 0)),
                  pl.BlockSpec((128, GLA_DK), lambda n: (0, 0)),
                  pl.BlockSpec((1, GLA_DK), lambda n: (0, 0))],
        out_specs=(pl.BlockSpec((R, 1024), lambda n: (n, 0)),
                   pl.BlockSpec((n_c, GLA_HEADS, GLA_HV, GLA_HK), lambda n: (n, 0, 0, 0)),
                   pl.BlockSpec((R, GLA_DK), lambda n: (n, 0))),
        out_shape=(jax.ShapeDtypeStruct((T, 1024), F32),
                   jax.ShapeDtypeStruct((n_chunks, GLA_HEADS, GLA_HV, GLA_HK), F32),
                   jax.ShapeDtypeStruct((T, GLA_DK), F32)),
        scratch_shapes=[pltpu.VMEM((GLA_HEADS, GLA_HV, GLA_HK), F32)],
        compiler_params=_cparams(("arbitrary",)),
    )(projf, projb, rank, wdec, bdec)


def _gla_bwd_call(projf, projb, la, do_gla, st_all, rank, wdec):
    T = projf.shape[1]
    C = GLA_CHUNK
    n_chunks = T // C
    n_c = GLA_STEP_CHUNKS
    R = n_c * C
    assert n_chunks % n_c == 0
    last = n_chunks // n_c - 1

    def body(qk_ref, v_ref, la_ref, do_ref, st_ref, rank_ref, wd_ref,
             dqk_ref, dv_ref, drank_ref, dwd_ref, dbd_ref, dst_scr):
        @pl.when(pl.program_id(0) == 0)
        def _():
            dst_scr[...] = jnp.zeros_like(dst_scr)
            dwd_ref[...] = jnp.zeros_like(dwd_ref)
            dbd_ref[...] = jnp.zeros_like(dbd_ref)

        same = _gla_same_chunk(R)
        mask = same & (_iota2(R, R, 0) >= _iota2(R, R, 1))
        upp = _bf((same & (_iota2(R, R, 0) <= _iota2(R, R, 1))).astype(F32))
        scale = GLA_HK ** -0.5
        la = la_ref[...]
        bl, eb, enb, ebl_b, qe, ke, kd = _gla_chunk_terms(la, qk_ref[:, :GLA_DK], qk_ref[:, GLA_DK:], n_c)
        qeb, keb, kdb = _bf(qe), _bf(ke), _bf(kd)
        ebl = [jnp.exp(bl[c]) for c in range(n_c)]
        heads = range(GLA_HEADS)
        ks = [slice(hh * GLA_HK, (hh + 1) * GLA_HK) for hh in heads]
        vs = [slice(hh * GLA_HV, (hh + 1) * GLA_HV) for hh in heads]
        rs = [slice(c * C, (c + 1) * C) for c in range(n_c)]
        v = [v_ref[:, vs[hh]] for hh in heads]
        do = [_bf(do_ref[:, vs[hh]]) for hh in heads]
        p = [_bf(jnp.where(mask, _dot_nt(qeb[:, ks[hh]], keb[:, ks[hh]]), 0.0)) for hh in heads]
        dp = [_bf(jnp.where(mask, _dot_nt(do[hh], v[hh]), 0.0)) for hh in heads]
        dst_intra = [[_dot_tn(do[hh][rs[c]], qeb[rs[c], ks[hh]]) for hh in heads] for c in range(n_c)]
        dqe_inter = [[_dot(do[hh][rs[c]], _bf(st_ref[c, hh])) for hh in heads] for c in range(n_c)]
        dv_intra = [_dot_tn(p[hh], do[hh]) for hh in heads]
        dqe_intra = [_dot(dp[hh], keb[:, ks[hh]]) for hh in heads]
        dke = jnp.concatenate([_dot_tn(dp[hh], qeb[:, ks[hh]]) for hh in heads], axis=1)
        dstn = [dst_scr[hh] for hh in heads]
        dkd_c, dv_inter, debl = [None] * n_c, [None] * n_c, [None] * n_c
        for c in reversed(range(n_c)):
            dstnb = [_bf(dstn[hh]) for hh in heads]
            dkd_c[c] = jnp.concatenate([_dot(v[hh][rs[c]], dstnb[hh]) for hh in heads], axis=1)
            dv_inter[c] = [_dot_nt(kdb[rs[c], ks[hh]], dstnb[hh]) for hh in heads]
            debl[c] = jnp.concatenate(
                [jnp.sum(dstn[hh] * st_ref[c, hh], axis=0, keepdims=True) for hh in heads], axis=1)
            dstn = [dst_intra[c][hh] + dstn[hh] * ebl[c][:, ks[hh]] for hh in heads]
        for hh in heads:
            dst_scr[hh] = dstn[hh]
            dv_ref[:, vs[hh]] = _bf(dv_intra[hh] + jnp.concatenate([dv_inter[c][hh] for c in range(n_c)], axis=0))
        dqe = jnp.concatenate(
            [dqe_intra[hh] + jnp.concatenate([dqe_inter[c][hh] for c in range(n_c)], axis=0) for hh in heads], axis=1)
        dkd = jnp.concatenate(dkd_c, axis=0)
        dkd_kd = dkd * kd
        db = dqe * qe - dke * ke - dkd_kd
        dbl = jnp.concatenate(
            [jnp.broadcast_to(jnp.sum(dkd_kd[rs[c]], axis=0, keepdims=True) + ebl[c] * debl[c], (C, GLA_DK))
             for c in range(n_c)], axis=0)
        dla = _tri_left(upp, db) + dbl
        dqk_ref[:, :GLA_DK] = _bf(dqe * eb * scale)
        dqk_ref[:, GLA_DK:] = _bf(dke * enb + dkd * ebl_b)
        ddec = dla * (1.0 / GLA_TAU) * (1.0 - jnp.exp(GLA_TAU * la))
        ddecb = _bf(ddec)
        drank_ref[...] = _bf(_dot_nt(ddecb, _bf(wd_ref[...])))
        dwd_ref[...] += _dot_tn(_bf(rank_ref[...]), ddecb)
        dbd_ref[...] += jnp.sum(ddec, axis=0, keepdims=True)

    return pl.pallas_call(
        body, name="gla_bwd",
        grid=(n_chunks // n_c,),
        in_specs=[pl.BlockSpec((None, R, 1024), lambda n: (0, last - n, 0)),
                  pl.BlockSpec((None, R, 1024), lambda n: (0, last - n, 0)),
                  pl.BlockSpec((R, GLA_DK), lambda n: (last - n, 0)),
                  pl.BlockSpec((R, 1024), lambda n: (last - n, 0)),
                  pl.BlockSpec((n_c, GLA_HEADS, GLA_HV, GLA_HK), lambda n: (last - n, 0, 0, 0)),
                  pl.BlockSpec((R, 128), lambda n: (last - n, 0)),
                  pl.BlockSpec((128, GLA_DK), lambda n: (0, 0))],
        out_specs=(pl.BlockSpec((R, 1024), lambda n: (last - n, 0)),
                   pl.BlockSpec((R, 1024), lambda n: (last - n, 0)),
                   pl.BlockSpec((R, 128), lambda n: (last - n, 0)),
                   pl.BlockSpec((128, GLA_DK), lambda n: (0, 0)),
                   pl.BlockSpec((1, GLA_DK), lambda n: (0, 0))),
        out_shape=(jax.ShapeDtypeStruct((T, 1024), BF16),
                   jax.ShapeDtypeStruct((T, 1024), BF16),
                   jax.ShapeDtypeStruct((T, 128), BF16),
                   jax.ShapeDtypeStruct((128, GLA_DK), F32),
                   jax.ShapeDtypeStruct((1, GLA_DK), F32)),
        scratch_shapes=[pltpu.VMEM((GLA_HEADS, GLA_HV, GLA_HK), F32)],
        compiler_params=_cparams(("arbitrary",)),
    )(projf, projb, la, do_gla, st_all, rank, wdec)


def _sb_logs(z):
    lsz = jnp.minimum(z, 0.0) - _softplus_neg_abs(z)
    return lsz, lsz - z


SB_HG_FWD = 8
SB_HG_BWD = 4
SB_QUERIES = 256
SB_KEYS = 256
SB_DEAD = -105.0


def _sb_fwd_call(projb, wp_shard):
    T = projb.shape[1]
    B = min(SB_QUERIES, T)
    HG = SB_HG_FWD
    W = HG * SB_HD
    scale = 1.0 / math.sqrt(SB_HD)
    KB = min(SB_KEYS, T)
    n_h, n_i = SB_HEADS // HG, T // B

    def body(q_ref, k_ref, v_ref, wp_ref, o_ref, wpall_ref, cb_scr, send_sems, recv_sems, loc_sem):
        i = pl.program_id(1)
        own, pairs = _push_copies(wp_ref, wpall_ref, send_sems, recv_sems, loc_sem, scatter=False)

        @pl.when((pl.program_id(0) == 0) & (i == 0))
        def _():
            _push_start(own, pairs)

        rows = HG * B
        after = (_iota2(KB, KB, 0) > _iota2(KB, KB, 1)).astype(F32)
        tri = _bf(jnp.concatenate([after, jnp.ones((KB, KB), F32)], axis=1))
        o_ref[...] = jnp.zeros_like(o_ref)
        cb_scr[...] = jnp.zeros_like(cb_scr)

        def block(jp, masked):
            off = pl.multiple_of(jp * KB, KB)
            z = jnp.concatenate(
                [_dot_nt(q_ref[:, hh * SB_HD:(hh + 1) * SB_HD], k_ref[pl.ds(off, KB), hh * SB_HD:(hh + 1) * SB_HD])
                 for hh in range(HG)], axis=0) * scale
            lsz, l1m = _sb_logs(z)
            if masked:
                strict = (jp * KB + _iota2(rows, KB, 1)) < (i * B + (_iota2(rows, KB, 0) & (B - 1)))
                l1m = jnp.where(strict, l1m, 0.0)
            r = _tri2_right(l1m, tri)
            cb = cb_scr[...]
            a = jnp.exp(lsz + cb + r[:, :KB])
            if masked:
                a = jnp.where(strict, a, 0.0)
            cb_scr[...] = cb + r[:, KB:]
            ab = _bf(a)
            for hh in range(HG):
                cs = slice(hh * SB_HD, (hh + 1) * SB_HD)
                o_ref[:, cs] += _dot(ab[hh * B:(hh + 1) * B, :], v_ref[pl.ds(off, KB), cs])

        jp0 = (i * B) // KB
        block(jp0, True)

        def live(state):
            jj, dead = state
            return (jj <= jp0) & jnp.logical_not(dead)

        def step(state):
            jj, _ = state
            block(jp0 - jj, False)
            return jj + 1, jnp.max(cb_scr[:, :SB_HD]) < SB_DEAD

        lax.while_loop(live, step, (jnp.int32(1), jnp.max(cb_scr[:, :SB_HD]) < SB_DEAD))

        @pl.when((pl.program_id(0) == n_h - 1) & (i == n_i - 1))
        def _():
            _push_wait(own, pairs)

    return pl.pallas_call(
        body, name="sb_fwd",
        grid=(n_h, n_i),
        in_specs=[pl.BlockSpec((None, B, W), lambda h, i: (1, i, h)),
                  pl.BlockSpec((None, T, W), lambda h, i: (2, 0, h)),
                  pl.BlockSpec((None, T, W), lambda h, i: (3, 0, h)),
                  _ANY],
        out_specs=(pl.BlockSpec((B, W), lambda h, i: (i, h)), _ANY),
        out_shape=(jax.ShapeDtypeStruct((T, 1024), F32),
                   jax.ShapeDtypeStruct((N_DEV,) + wp_shard.shape, wp_shard.dtype)),
        scratch_shapes=[pltpu.VMEM((HG * B, KB), F32)] + _PUSH_SEMS,
        compiler_params=_cparams(("arbitrary", "arbitrary")),
    )(projb, projb, projb, wp_shard)


def _sb_bwd_call(projb, do_sb, g_p):
    T = projb.shape[1]
    B = min(SB_QUERIES, T)
    nb = T // B
    HG = SB_HG_BWD
    W = HG * SB_HD
    WQ = HG * B
    KB = min(SB_KEYS, T)
    nkb = T // KB
    n_h = SB_HEADS // HG
    scale = 1.0 / math.sqrt(SB_HD)

    def body(q_ref, k_ref, v_ref, do_ref, gp_ref, dq_ref, dk_ref, dv_ref, rp_ref,
             dk_scr, dv_scr, kt_scr, beta_scr, g_scr, dqt_scr, send_sems, recv_sems, loc_sem):
        i = pl.program_id(1)
        own, pairs = _push_copies(gp_ref, rp_ref, send_sems, recv_sems, loc_sem, scatter=True)

        @pl.when((pl.program_id(0) == 0) & (i == 0))
        def _():
            _push_start(own, pairs)

        @pl.when(i == 0)
        def _():
            dk_scr[...] = jnp.zeros_like(dk_scr)
            dv_scr[...] = jnp.zeros_like(dv_scr)
            for hh in range(HG):
                for jb in range(nkb):
                    kt_scr[hh, jb] = _bf(
                        k_ref[jb * KB:(jb + 1) * KB, hh * SB_HD:(hh + 1) * SB_HD].astype(F32).T)

        dqt_scr[...] = jnp.zeros_like(dqt_scr)
        later = _bf((_iota2(KB, KB, 1) > _iota2(KB, KB, 0)).astype(F32))
        earlier = _bf((_iota2(KB, KB, 1) < _iota2(KB, KB, 0)).astype(F32))
        dob = _bf(do_ref[...])
        jp0 = (i * B) // KB

        def strict_mask():
            return (jp0 * KB + _iota2(KB, WQ, 0)) < (i * B + (_iota2(KB, WQ, 1) & (B - 1)))

        def heads(fn):
            return [fn(slice(hh * SB_HD, (hh + 1) * SB_HD)) for hh in range(HG)]

        def pass1(jp, cb, masked):
            off = pl.multiple_of(jp * KB, KB)
            z = jnp.concatenate(heads(lambda cs: _dot_nt(k_ref[pl.ds(off, KB), cs], q_ref[:, cs])), axis=1) * scale
            da = jnp.concatenate(heads(lambda cs: _dot_nt(v_ref[pl.ds(off, KB), cs], dob[:, cs])), axis=1)
            lsz, l1m = _sb_logs(z)
            if masked:
                strict = strict_mask()
                l1m = jnp.where(strict, l1m, 0.0)
            a = jnp.exp(lsz + cb + _tri2_left(later, l1m))
            if masked:
                a = jnp.where(strict, a, 0.0)
            g_scr[jp] = a * da
            beta_scr[jp] = jnp.exp(lsz)
            ab = _bf(a)
            for hh in range(HG):
                cs = slice(hh * SB_HD, (hh + 1) * SB_HD)
                dv_scr[pl.ds(off, KB), cs] += _dot(ab[:, hh * B:(hh + 1) * B], dob[:, cs])
            return cb + jnp.sum(l1m, axis=0, keepdims=True)

        zero = jnp.zeros((1, WQ), F32)
        cb = pass1(jp0, zero, True)

        def live(state):
            jj, _, dead = state
            return (jj <= jp0) & jnp.logical_not(dead)

        def step(state):
            jj, cr, _ = state
            cr = pass1(jp0 - jj, cr, False)
            return jj + 1, cr, jnp.max(cr) < SB_DEAD

        n_done, _, _ = lax.while_loop(live, step, (jnp.int32(1), cb, jnp.max(cb) < SB_DEAD))
        jp_first = jp0 - (n_done - 1)

        def pass2(jp, cg, masked):
            off = pl.multiple_of(jp * KB, KB)
            g = g_scr[jp]
            beta = beta_scr[jp]
            dz = g * (1.0 - beta) - beta * (cg + _tri2_left(earlier, g))
            if masked:
                dz = jnp.where(strict_mask(), dz, 0.0)
            dzb = _bf(dz * scale)
            for hh in range(HG):
                cs = slice(hh * SB_HD, (hh + 1) * SB_HD)
                dk_scr[pl.ds(off, KB), cs] += _dot(dzb[:, hh * B:(hh + 1) * B], q_ref[:, cs])
                dqt_scr[hh] += _dot(kt_scr[hh, jp], dzb[:, hh * B:(hh + 1) * B])
            return cg + jnp.sum(g, axis=0, keepdims=True)

        cg = lax.fori_loop(jp_first, jp0, lambda jp, cr: pass2(jp, cr, False), zero)
        pass2(jp0, cg, True)
        for hh in range(HG):
            dq_ref[:, hh * SB_HD:(hh + 1) * SB_HD] = _bf(dqt_scr[hh].T)

        @pl.when(i == nb - 1)
        def _():
            dk_ref[...] = _bf(dk_scr[...])
            dv_ref[...] = _bf(dv_scr[...])

        @pl.when((pl.program_id(0) == n_h - 1) & (i == nb - 1))
        def _():
            _push_wait(own, pairs)

    return pl.pallas_call(
        body, name="sb_bwd",
        grid=(n_h, nb),
        in_specs=[pl.BlockSpec((None, B, W), lambda h, i: (1, i, h)),
                  pl.BlockSpec((None, T, W), lambda h, i: (2, 0, h)),
                  pl.BlockSpec((None, T, W), lambda h, i: (3, 0, h)),
                  pl.BlockSpec((B, W), lambda h, i: (i, h)),
                  _ANY],
        out_specs=(pl.BlockSpec((B, W), lambda h, i: (i, h)),
                   pl.BlockSpec((T, W), lambda h, i: (0, h)),
                   pl.BlockSpec((T, W), lambda h, i: (0, h)),
                   _ANY),
        out_shape=(jax.ShapeDtypeStruct((T, 1024), BF16),
                   jax.ShapeDtypeStruct((T, 1024), BF16),
                   jax.ShapeDtypeStruct((T, 1024), BF16),
                   jax.ShapeDtypeStruct(g_p.shape, g_p.dtype)),
        scratch_shapes=[pltpu.VMEM((T, W), F32), pltpu.VMEM((T, W), F32),
                        pltpu.VMEM((HG, nkb, SB_HD, KB), BF16),
                        pltpu.VMEM((nkb, KB, WQ), F32), pltpu.VMEM((nkb, KB, WQ), F32),
                        pltpu.VMEM((HG, SB_HD, B), F32)] + _PUSH_SEMS,
        compiler_params=_cparams(("arbitrary", "arbitrary")),
    )(projb, projb, projb, do_sb, g_p)


def _mid_call(o_gla, o_sb, projf, x, target, wpa, wpb, wo, gla_g, b_gate, final_g):
    T, D = x.shape
    tm = min(TBLK, T)

    def body(og_ref, ggate_ref, osb_ref, sgate_ref, ma_ref, mb_ref, x_ref, tgt_ref,
             wpa_ref, wpb_ref, wo_ref, glag_ref, bg_ref, fg_ref,
             dx2_ref, dogla_ref, dosb_ref, dggate_ref, dsgate_ref, dm_ref,
             mt_ref, ogt_ref, obt_ref, dx2b_ref, dya_ref, dyb_ref,
             dfg_ref, dbg_ref, dglag_ref, loss_ref):
        @pl.when(pl.program_id(0) == 0)
        def _():
            dfg_ref[...] = jnp.zeros_like(dfg_ref)
            dbg_ref[...] = jnp.zeros_like(dbg_ref)
            dglag_ref[...] = jnp.zeros_like(dglag_ref)
            loss_ref[...] = jnp.zeros_like(loss_ref)

        glag = glag_ref[...]
        ggate = ggate_ref[...]
        sg = _sigmoid(ggate)
        silu_g = ggate * sg
        ohat, rinv, nrm = [], [], []
        for hh in range(GLA_HEADS):
            oh = og_ref[:, hh * GLA_HV:(hh + 1) * GLA_HV]
            r = lax.rsqrt(jnp.mean(oh * oh, axis=-1, keepdims=True) + EPS)
            ohat.append(oh * r)
            rinv.append(r)
            nrm.append(ohat[-1] * glag)
        n_all = jnp.concatenate(nrm, axis=1)
        og = n_all * silu_g
        ogb = _bf(og)
        ya = _dot(ogb, wpa_ref[...])
        sgate = sgate_ref[...]
        ss = _sigmoid(sgate)
        silu_s = sgate * ss
        osb = osb_ref[...]
        ob = osb * silu_s
        obb = _bf(ob)
        yb = _dot(obb, wpb_ref[...])
        ga = _sigmoid(ma_ref[...] + bg_ref[:, :D])
        gb = _sigmoid(mb_ref[...] + bg_ref[:, D:])
        merged = ga * ya + gb * yb
        mgb = _bf(merged)
        x2 = x_ref[...] + _dot(mgb, wo_ref[...])
        r2 = lax.rsqrt(jnp.mean(x2 * x2, axis=-1, keepdims=True) + EPS)
        xh2 = x2 * r2
        fg = fg_ref[...]
        err = xh2 * fg - tgt_ref[...]
        loss_ref[...] += jnp.broadcast_to(
            0.5 * jnp.sum(jnp.mean(err * err, axis=-1, keepdims=True), axis=0, keepdims=True), (1, 128))
        dy = err * (1.0 / D)
        dfg_ref[...] += jnp.sum(dy * xh2, axis=0, keepdims=True)
        dxh = dy * fg
        dx2 = r2 * (dxh - xh2 * jnp.mean(dxh * xh2, axis=-1, keepdims=True))
        dx2_ref[...] = dx2
        dx2b = _bf(dx2)
        dx2b_ref[...] = dx2b
        dmerged = _dot_nt(dx2b, wo_ref[...])
        dya = dmerged * ga
        dyb = dmerged * gb
        dma = dmerged * ya * ga * (1.0 - ga)
        dmb = dmerged * yb * gb * (1.0 - gb)
        dm_ref[:, :D] = _bf(dma)
        dm_ref[:, D:] = _bf(dmb)
        dbg_ref[:, :D] += jnp.sum(dma, axis=0, keepdims=True)
        dbg_ref[:, D:] += jnp.sum(dmb, axis=0, keepdims=True)
        dyab = _bf(dya)
        dybb = _bf(dyb)
        dya_ref[...] = dyab
        dyb_ref[...] = dybb
        dog = _dot_nt(dyab, wpa_ref[...])
        dob = _dot_nt(dybb, wpb_ref[...])
        dosb_ref[...] = dob * silu_s
        dsgate_ref[...] = _bf(dob * osb * (ss * (1.0 + sgate * (1.0 - ss))))
        dn = dog * silu_g
        dggate_ref[...] = _bf(dog * n_all * (sg * (1.0 + ggate * (1.0 - sg))))
        dglag = jnp.zeros((1, GLA_HV), F32)
        for hh in range(GLA_HEADS):
            dnh = dn[:, hh * GLA_HV:(hh + 1) * GLA_HV]
            dglag = dglag + jnp.sum(dnh * ohat[hh], axis=0, keepdims=True)
            dohat = dnh * glag
            dogla_ref[:, hh * GLA_HV:(hh + 1) * GLA_HV] = rinv[hh] * (
                dohat - ohat[hh] * jnp.mean(dohat * ohat[hh], axis=-1, keepdims=True))
        dglag_ref[...] += dglag
        mt_ref[...] = _bf(merged.T)
        ogt_ref[...] = _bf(og.T)
        obt_ref[...] = _bf(ob.T)

    row = lambda i: (i, 0)
    const = lambda i: (0, 0)
    tile = pl.BlockSpec((tm, D), row)
    tile_t = pl.BlockSpec((None, D, tm), lambda i: (i, 0, 0))
    wspec = pl.BlockSpec((D, D), const)
    return pl.pallas_call(
        body, name="mid",
        grid=(T // tm,),
        in_specs=[tile,
                  pl.BlockSpec((None, tm, D), lambda i: (1, i, 0)),
                  tile,
                  pl.BlockSpec((None, tm, D), lambda i: (2, i, 0)),
                  pl.BlockSpec((None, tm, D), lambda i: (3, i, 0)),
                  pl.BlockSpec((None, tm, D), lambda i: (4, i, 0)),
                  tile, tile, wspec, wspec, wspec,
                  pl.BlockSpec((1, GLA_HV), const),
                  pl.BlockSpec((1, 2 * D), const),
                  pl.BlockSpec((1, D), const)],
        out_specs=(tile, tile, tile, tile, tile,
                   pl.BlockSpec((tm, 2 * D), row),
                   tile_t, tile_t, tile_t, tile, tile, tile,
                   pl.BlockSpec((1, D), const),
                   pl.BlockSpec((1, 2 * D), const),
                   pl.BlockSpec((1, GLA_HV), const),
                   pl.BlockSpec((1, 128), const)),
        out_shape=(jax.ShapeDtypeStruct((T, D), F32),
                   jax.ShapeDtypeStruct((T, D), F32),
                   jax.ShapeDtypeStruct((T, D), F32),
                   jax.ShapeDtypeStruct((T, D), BF16),
                   jax.ShapeDtypeStruct((T, D), BF16),
                   jax.ShapeDtypeStruct((T, 2 * D), BF16),
                   jax.ShapeDtypeStruct((T // tm, D, tm), BF16),
                   jax.ShapeDtypeStruct((T // tm, D, tm), BF16),
                   jax.ShapeDtypeStruct((T // tm, D, tm), BF16),
                   jax.ShapeDtypeStruct((T, D), BF16),
                   jax.ShapeDtypeStruct((T, D), BF16),
                   jax.ShapeDtypeStruct((T, D), BF16),
                   jax.ShapeDtypeStruct((1, D), F32),
                   jax.ShapeDtypeStruct((1, 2 * D), F32),
                   jax.ShapeDtypeStruct((1, GLA_HV), F32),
                   jax.ShapeDtypeStruct((1, 128), F32)),
        compiler_params=_cparams(("arbitrary",)),
    )(o_gla, projf, o_sb, projf, projf, projf, x, target, wpa, wpb, wo, gla_g, b_gate, final_g)


def _dh_call(pieces, dmlog, drank, wt, wr, x, dx2, norm_g, s_in):
    T, D = x.shape
    tm = min(256, T)
    npc = len(pieces)
    n_main = N_GROUPS * 1024
    n_i = T // tm

    def body(*refs):
        pcs = refs[:npc]
        (dm_ref, dr_ref, w_hbm, wr_ref, x_ref, dx2_ref, g_ref, sin_ref,
         gx_ref, dg_ref, rin_ref, w_scr, sems, send_sems, recv_sems, loc_sem) = refs[npc:]
        own, pairs = _chip_copies(sin_ref, rin_ref, send_sems, recv_sems, loc_sem)

        @pl.when(pl.program_id(0) == 0)
        def _():
            _push_start(own, pairs)
            lo = pltpu.make_async_copy(w_hbm.at[pl.ds(0, RANK_COL)], w_scr.at[pl.ds(0, RANK_COL)], sems.at[0])
            hi = pltpu.make_async_copy(w_hbm.at[pl.ds(RANK_COL + GLA_RANK, n_main - RANK_COL)],
                                       w_scr.at[pl.ds(RANK_COL, n_main - RANK_COL)], sems.at[1])
            lo.start()
            hi.start()
            dg_ref[...] = jnp.zeros_like(dg_ref)
            lo.wait()
            hi.wait()

        def w_group(g):
            return w_scr[g * 1024:(g + 1) * 1024, :]

        dr = dr_ref[...]
        dh = _dot(dr, wr_ref[...])
        for g in range(npc):
            dh = dh + _dot(pcs[g][...], w_group(g))
        dh = dh + _dot(dm_ref[:, :D], w_group(npc))
        dh = dh + _dot(dm_ref[:, D:], w_group(npc + 1))
        xv = x_ref[...]
        r = lax.rsqrt(jnp.mean(xv * xv, axis=-1, keepdims=True) + EPS)
        xhat = xv * r
        g = g_ref[...]
        dg_ref[...] += jnp.sum(dh * xhat, axis=0, keepdims=True)
        dxhat = dh * g
        gx_ref[...] = r * (dxhat - xhat * jnp.mean(dxhat * xhat, axis=-1, keepdims=True)) + dx2_ref[...]

        @pl.when(pl.program_id(0) == n_i - 1)
        def _():
            _push_wait(own, pairs)

    row = lambda i: (i, 0)
    const = lambda i: (0, 0)
    tile = pl.BlockSpec((tm, D), row)
    return pl.pallas_call(
        body, name="dh",
        grid=(n_i,),
        in_specs=[tile] * npc + [
            pl.BlockSpec((tm, 2 * D), row),
            pl.BlockSpec((tm, 128), row),
            _ANY,
            pl.BlockSpec((128, D), const),
            tile, tile,
            pl.BlockSpec((1, D), const),
            _ANY],
        out_specs=(tile, pl.BlockSpec((1, D), const), _ANY),
        out_shape=(jax.ShapeDtypeStruct((T, D), F32),
                   jax.ShapeDtypeStruct((1, D), F32),
                   jax.ShapeDtypeStruct(s_in.shape, s_in.dtype)),
        scratch_shapes=[pltpu.VMEM((n_main, D), BF16), pltpu.SemaphoreType.DMA((2,))] + _CHIP_SEMS,
        compiler_params=_cparams(("arbitrary",)),
    )(*pieces, dmlog, drank, wt, wr, x, dx2, norm_g, s_in)


def _wgrad_rank_call(ht, drank):
    n_tb, D, tb = ht.shape

    def body(ht_ref, dr_ref, o_ref):
        @pl.when(pl.program_id(0) == 0)
        def _():
            o_ref[...] = jnp.zeros_like(o_ref)

        o_ref[...] += _dot(ht_ref[...], dr_ref[...])

    return pl.pallas_call(
        body, name="wgrad_rank",
        grid=(n_tb,),
        in_specs=[pl.BlockSpec((None, D, tb), lambda i: (i, 0, 0)),
                  pl.BlockSpec((tb, 128), lambda i: (i, 0))],
        out_specs=pl.BlockSpec((D, 128), lambda i: (0, 0)),
        out_shape=jax.ShapeDtypeStruct((D, 128), F32),
        compiler_params=_cparams(("arbitrary",)),
    )(ht, drank)


def _wgrad_call(lhs_list, lhs_of_group, rhs_list, rhs_of_group, n_transposed, name):
    n_groups = len(rhs_of_group)
    n_tb, D, tb = lhs_list[0].shape
    T = n_tb * tb
    per = min(2, n_tb)
    tk = per * tb
    nk = T // tk
    nl = len(lhs_list)

    def body(*refs):
        lhs = refs[:nl]
        rhs = refs[nl:nl + n_groups]
        out_ref, acc = refs[nl + n_groups:]
        g = pl.program_id(0)
        i = pl.program_id(1)

        @pl.when(i == 0)
        def _():
            acc[...] = jnp.zeros_like(acc)

        for p in range(n_groups):
            @pl.when(g == p)
            def _(p=p):
                lref = lhs[lhs_of_group[p]]
                part = _dot(lref[0], rhs[p][0:tb, :])
                for b in range(1, per):
                    part = part + _dot(lref[b], rhs[p][b * tb:(b + 1) * tb, :])
                acc[...] += part

        @pl.when((i == nk - 1) & (g < n_transposed))
        def _():
            out_ref[...] = _bf(acc[...].T)

        @pl.when((i == nk - 1) & (g >= n_transposed))
        def _():
            out_ref[...] = _bf(acc[...])

    def lhs_spec(a):
        groups = [g for g in range(n_groups) if lhs_of_group[g] == a]
        lo, hi = min(groups), max(groups)
        assert groups == list(range(lo, hi + 1))
        return pl.BlockSpec((per, D, tb), lambda g, i: (jnp.where((g >= lo) & (g <= hi), i, 0), 0, 0))

    def rhs_spec(p):
        cb = rhs_of_group[p][1]
        return pl.BlockSpec((tk, 1024), lambda g, i: (jnp.where(g == p, i, 0), cb))

    return pl.pallas_call(
        body, name=name,
        grid=(n_groups, nk),
        in_specs=[lhs_spec(a) for a in range(nl)] + [rhs_spec(p) for p in range(n_groups)],
        out_specs=pl.BlockSpec((None, D, 1024), lambda g, i: (g, 0, 0)),
        out_shape=jax.ShapeDtypeStruct((n_groups, D, 1024), BF16),
        scratch_shapes=[pltpu.VMEM((D, 1024), F32)],
        compiler_params=_cparams(("arbitrary", "arbitrary")),
    )(*lhs_list, *[rhs_list[rhs_of_group[p][0]] for p in range(n_groups)])


def _adamw_math(parts, w, m, v):
    g = parts[0].astype(F32)
    for p in parts[1:]:
        g = g + p.astype(F32)
    mm = ADAM_B1 * m + (1.0 - ADAM_B1) * g
    vv = ADAM_B2 * v + (1.0 - ADAM_B2) * (g * g)
    m_hat = mm / (1.0 - ADAM_B1 ** ADAM_STEP)
    v_hat = vv / (1.0 - ADAM_B2 ** ADAM_STEP)
    return g, -ADAM_LR * (m_hat / (jnp.sqrt(v_hat) + ADAM_EPS) + ADAM_WD * w), mm, vv


def _part_order(n_parts):
    return [n_parts - 1] + list(range(n_parts - 1))


def _adamw_call(parts, w, m, v, name):
    R, C = w.shape
    n_parts = parts.shape[0]
    (tr, tc), grid, idx = _tiling_2d(R, C)

    def body(p_ref, w_ref, m_ref, v_ref, g_ref, d_ref, nm_ref, nv_ref):
        g_ref[...], d_ref[...], nm_ref[...], nv_ref[...] = _adamw_math(
            [p_ref[k] for k in _part_order(n_parts)], w_ref[...], m_ref[...], v_ref[...])

    blk = pl.BlockSpec((tr, tc), idx)
    sds = jax.ShapeDtypeStruct((R, C), F32)
    return pl.pallas_call(
        body, name=name,
        grid=grid,
        in_specs=[pl.BlockSpec((n_parts, tr, tc), lambda i: (0,) + idx(i)), blk, blk, blk],
        out_specs=(blk, blk, blk, blk),
        out_shape=(sds, sds, sds, sds),
        compiler_params=_cparams(("arbitrary",)),
    )(parts, w, m, v)


def _adamw_rows_call(parts, ws, ms, vs, name, gathered):
    n = len(ws)
    R, C = ws[0].shape
    n_parts = parts.shape[0]

    def body(*refs):
        p_ref = refs[0]
        w_refs, m_refs, v_refs = refs[1:1 + n], refs[1 + n:1 + 2 * n], refs[1 + 2 * n:1 + 3 * n]
        src_ref = refs[1 + 3 * n]
        outs = refs[2 + 3 * n:2 + 7 * n]
        dst_ref, send_sems, recv_sems, loc_sem = refs[2 + 7 * n:]
        own, pairs = _push_copies(src_ref, dst_ref, send_sems, recv_sems, loc_sem, scatter=False)
        k_now = pl.program_id(0)

        @pl.when(k_now == 0)
        def _():
            _push_start(own, pairs)

        for k in range(n):
            @pl.when(k_now == k)
            def _(k=k):
                res = _adamw_math([p_ref[j] for j in _part_order(n_parts)],
                                  w_refs[k][...], m_refs[k][...], v_refs[k][...])
                for o_ref, val in zip(outs[4 * k:4 * k + 4], res):
                    o_ref[...] = val

        @pl.when(k_now == n - 1)
        def _():
            _push_wait(own, pairs)

    whole = pl.BlockSpec((R, C), lambda k: (0, 0))
    sds = jax.ShapeDtypeStruct((R, C), F32)
    res = pl.pallas_call(
        body, name=name,
        grid=(n,),
        in_specs=[pl.BlockSpec((n_parts, R, C), lambda k: (0, k, 0))] + [whole] * (3 * n) + [_ANY],
        out_specs=tuple([whole] * (4 * n) + [_ANY]),
        out_shape=tuple([sds] * (4 * n) + [jax.ShapeDtypeStruct((N_DEV,) + gathered.shape, gathered.dtype)]),
        scratch_shapes=_PUSH_SEMS,
        compiler_params=_cparams(("arbitrary",)),
    )(parts, *ws, *ms, *vs, gathered)
    return [res[4 * k:4 * k + 4] for k in range(n)], res[4 * n]


def _adamw_lanes_call(parts, offsets, ws, ms, vs, name):
    n = len(ws)
    n_parts = parts.shape[0]

    def body(*refs):
        p_ref = refs[0]
        w_refs, m_refs, v_refs = refs[1:1 + n], refs[1 + n:1 + 2 * n], refs[1 + 2 * n:1 + 3 * n]
        outs = refs[1 + 3 * n:]
        for k in range(n):
            lanes = slice(offsets[k], offsets[k] + ws[k].shape[1])
            res = _adamw_math([p_ref[j, :, lanes] for j in _part_order(n_parts)],
                              w_refs[k][...], m_refs[k][...], v_refs[k][...])
            for o_ref, val in zip(outs[4 * k:4 * k + 4], res):
                o_ref[...] = val

    res = pl.pallas_call(
        body, name=name,
        out_shape=tuple(jax.ShapeDtypeStruct(ws[k].shape, F32) for k in range(n) for _ in range(4)),
        compiler_params=_cparams(),
    )(parts, *ws, *ms, *vs)
    return [res[4 * k:4 * k + 4] for k in range(n)]


def _local_step(x, target, wt, wr, wdec, bdec, wp_shard, norm_g, gla_g, b_gate, final_g):
    D = x.shape[1]
    half = wp_shard.shape[1] // 2
    projf, projb, rank, ht, wp_lo = _proj_call(x, norm_g, wt, wr, wp_shard[:, :half])
    o_gla, st_all, la = _gla_fwd_call(projf, projb, rank, wdec, bdec)
    o_sb, wp_hi = _sb_fwd_call(projb, wp_shard[:, half:])
    wp_full = jnp.concatenate([wp_lo, wp_hi], axis=2).transpose(1, 0, 2, 3).reshape(3, D, D)
    (dx2, do_gla, do_sb, dggate, dsgate, dmlog, mt, ogt, obt, dx2b, dya, dyb,
     dfinal_g, db_gate, dgla_g, loss) = _mid_call(o_gla, o_sb, projf, x, target, wp_full[0], wp_full[1],
                                                 wp_full[2], gla_g, b_gate, final_g)
    dw_p = _wgrad_call([ogt, obt, mt], [0, 1, 2], [dya, dyb, dx2b], [(0, 0), (1, 0), (2, 0)], 0, "wgrad_p")
    g_p = dw_p.reshape(3, N_DEV, D // N_DEV, D).transpose(1, 0, 2, 3).reshape(N_DEV, 3 * (D // N_DEV), D)
    dqk, dgv, drank, dwdec, dbdec = _gla_bwd_call(projf, projb, la, do_gla, st_all, rank, wdec)
    dsq, dsk, dsv, r_p = _sb_bwd_call(projb, do_sb, g_p)
    pieces = [dqk, dgv, dggate, dsq, dsk, dsv, dsgate]
    rhs_of_group = [(g, 0) for g in range(7)] + [(7, 0), (7, 1)]
    dw_in = _wgrad_call([ht], [0] * N_GROUPS, pieces + [dmlog], rhs_of_group, N_GROUPS, "wgrad_in")
    dwr = _wgrad_rank_call(ht, drank)
    g_in = _parts_by_device(dw_in.reshape(N_GROUPS * 1024, D), dwr[:, :GLA_RANK].T.astype(BF16))
    c_idx = lax.axis_index("c").astype(jnp.int32).reshape(1)
    (p_in,) = _pair_exchange([g_in], "pair_g")
    s_in = _pair_add_call(g_in, p_in, c_idx, "pair_add_in")
    grad_x, dnorm_g, r_in = _dh_call(pieces, dmlog, drank, wt, wr, x, dx2, norm_g, s_in)
    small = jnp.concatenate([
        dnorm_g.reshape(-1), dbdec.reshape(-1), dgla_g.reshape(-1), db_gate.reshape(-1), dfinal_g.reshape(-1),
        loss.reshape(-1), dwdec[:GLA_RANK].reshape(-1)]).reshape(1, _SM_LEN)
    return grad_x, r_in, r_p, small


def _parts_by_device(dmain, drank):
    def part_for(p):
        lo, hi = p * SHARD_COLS, (p + 1) * SHARD_COLS
        pieces = []
        if lo < RANK_COL:
            pieces.append(dmain[lo:min(hi, RANK_COL)])
        if lo < RANK_COL + GLA_RANK and hi > RANK_COL:
            pieces.append(drank[max(lo, RANK_COL) - RANK_COL:min(hi, RANK_COL + GLA_RANK) - RANK_COL])
        if hi > RANK_COL + GLA_RANK:
            pieces.append(dmain[max(lo, RANK_COL + GLA_RANK) - GLA_RANK:hi - GLA_RANK])
        return pieces[0] if len(pieces) == 1 else jnp.concatenate(pieces, axis=0)

    return jnp.stack([part_for(p) for p in range(N_DEV)])


_SM_NORM = 0
_SM_BDEC = _SM_NORM + D_MODEL
_SM_GLAG = _SM_BDEC + GLA_DK
_SM_BGATE = _SM_GLAG + GLA_HV
_SM_FINAL = _SM_BGATE + 2 * D_MODEL
_SM_REPL = _SM_FINAL + D_MODEL
_SM_LOSS = _SM_REPL
_SM_WDEC = _SM_LOSS + 128
_SM_LEN = _SM_WDEC + GLA_RANK * GLA_DK


def kernel(x, norm_g, w_in, w_dec_up, b_dec, gla_norm_g, w_pa, w_pb, b_gate, w_o, final_g, loss_target, m_norm_g, m_w_in, m_w_dec_up, m_b_dec, m_gla_norm_g, m_w_pa, m_w_pb, m_b_gate, m_w_o, m_final_g, v_norm_g, v_w_in, v_w_dec_up, v_b_dec, v_gla_norm_g, v_w_pa, v_w_pb, v_b_gate, v_w_o, v_final_g):
    D = D_MODEL
    me = 4 * lax.axis_index("x") + 2 * lax.axis_index("y") + lax.axis_index("c")

    wp_shard = jnp.stack([w_pa, w_pb, w_o]).astype(BF16)
    win_all, wdec_all = _all_gather([w_in.T.astype(BF16), w_dec_up], "gather_w")
    wt = win_all.reshape(IN_COLS, D)
    wr = jnp.pad(wt[RANK_COL:RANK_COL + GLA_RANK], ((0, 128 - GLA_RANK), (0, 0)))
    wdec_full = wdec_all.transpose(1, 0, 2).reshape(GLA_RANK, GLA_DK)
    wdec = jnp.pad(wdec_full, ((0, 128 - GLA_RANK), (0, 0)))

    grad_x, r_in, r_p, small = _local_step(
        x[0], loss_target[0], wt, wr, wdec, b_dec.reshape(1, -1), wp_shard,
        norm_g.reshape(1, -1), gla_norm_g.reshape(1, -1), b_gate.reshape(1, -1), final_g.reshape(1, -1))

    gw_in, d_in, nm_in, nv_in = (a.T for a in _adamw_call(r_in, w_in.T, m_w_in.T, v_w_in.T, "adamw_in"))
    ((g_pa, d_pa, nm_pa, nv_pa), (g_pb, d_pb, nm_pb, nv_pb), (g_o, d_o, nm_o, nv_o)), r_small = _adamw_rows_call(
        r_p, [w_pa, w_pb, w_o], [m_w_pa, m_w_pb, m_w_o], [v_w_pa, v_w_pb, v_w_o], "adamw_p", small)

    def row(a):
        return a.reshape(1, -1)

    rep = _adamw_lanes_call(
        r_small, [_SM_NORM, _SM_BDEC, _SM_GLAG, _SM_BGATE, _SM_FINAL],
        [row(a) for a in (norm_g, b_dec, gla_norm_g, b_gate, final_g)],
        [row(a) for a in (m_norm_g, m_b_dec, m_gla_norm_g, m_b_gate, m_final_g)],
        [row(a) for a in (v_norm_g, v_b_dec, v_gla_norm_g, v_b_gate, v_final_g)], "adamw_rep")
    ((g_norm, d_norm, nm_norm, nv_norm), (g_bdec, d_bdec, nm_bdec, nv_bdec), (g_glag, d_glag, nm_glag, nv_glag),
     (g_bgate, d_bgate, nm_bgate, nv_bgate), (g_final, d_final, nm_final, nv_final)) = [
        tuple(a.reshape(-1) for a in quad) for quad in rep]

    wdec_parts = r_small[:, 0, _SM_WDEC:].reshape(N_DEV, GLA_RANK, GLA_DK)
    cols = GLA_DK // N_DEV
    wdec_mine = lax.dynamic_slice_in_dim(wdec_parts, me * cols, cols, axis=2)
    g_wdec, d_wdec, nm_wdec, nv_wdec = _adamw_call(wdec_mine, w_dec_up, m_w_dec_up, v_w_dec_up, "adamw_dec")

    loss_total = jnp.sum(r_small[:, 0, _SM_LOSS])

    return (loss_total, grad_x[None],
            g_norm, gw_in, g_wdec, g_bdec, g_glag, g_pa, g_pb, g_bgate, g_o, g_final,
            d_norm, d_in, d_wdec, d_bdec, d_glag, d_pa, d_pb, d_bgate, d_o, d_final,
            nm_norm, nm_in, nm_wdec, nm_bdec, nm_glag, nm_pa, nm_pb, nm_bgate, nm_o, nm_final,
            nv_norm, nv_in, nv_wdec, nv_bdec, nv_glag, nv_pa, nv_pb, nv_bgate, nv_o, nv_final)
```

```python
import math

import jax
import jax.numpy as jnp
from jax import lax
from jax.experimental import pallas as pl
from jax.experimental.pallas import tpu as pltpu

F32 = jnp.float32
BF16 = jnp.bfloat16

N_DEV = 8
D_MODEL = 1024
GLA_HEADS = 4
GLA_HK = 128
GLA_HV = 256
GLA_DK = 512
GLA_RANK = 16
GLA_TAU = 16.0
GLA_CHUNK = 64
SB_HEADS = 8
SB_HD = 128
EPS = 1e-6
N_GROUPS = 9
RANK_COL = 3072
IN_COLS = 9232
SHARD_COLS = IN_COLS // N_DEV

ADAM_LR = 0.001
ADAM_B1 = 0.9
ADAM_B2 = 0.999
ADAM_EPS = 1e-08
ADAM_WD = 0.01
ADAM_STEP = 10

VMEM_LIMIT = 56 * 1024 * 1024
TBLK = 256


def _cparams(sem=None):
    return pltpu.CompilerParams(dimension_semantics=sem, vmem_limit_bytes=VMEM_LIMIT)


def _tiling_2d(rows, cols):
    if rows * cols <= 128 * 1024:
        return (rows, cols), (1,), lambda i: (0, 0)
    if rows % 128 == 0:
        return (128, cols), (rows // 128,), lambda i: (i, 0)
    tc = 256 if cols % 256 == 0 else cols
    return (rows, tc), (cols // tc,), lambda i: (0, i)


def _dot(a, b):
    return jnp.dot(a, b, preferred_element_type=F32)


def _dot_nt(a, b):
    return lax.dot_general(a, b, (((1,), (1,)), ((), ())), preferred_element_type=F32)


def _dot_tn(a, b):
    return lax.dot_general(a, b, (((0,), (0,)), ((), ())), preferred_element_type=F32)


def _bf(x):
    return x.astype(BF16)


def _split3(x):
    hi = x.astype(BF16)
    r = x - hi.astype(F32)
    mid = r.astype(BF16)
    lo = (r - mid.astype(F32)).astype(BF16)
    return hi, mid, lo


def _tri_left(tri, x):
    hi, mid, lo = _split3(x)
    return _dot(tri, hi) + _dot(tri, mid) + _dot(tri, lo)


def _split2(x):
    hi = lax.bitcast_convert_type(lax.bitcast_convert_type(x, jnp.uint32) & jnp.uint32(0xFFFF0000), F32)
    return hi.astype(BF16), (x - hi).astype(BF16)


def _tri2_left(tri, x):
    hi, lo = _split2(x)
    return _dot(tri, hi) + _dot(tri, lo)


def _tri2_right(x, tri):
    hi, lo = _split2(x)
    return _dot(hi, tri) + _dot(lo, tri)


def _iota2(n, m, dim):
    return lax.broadcasted_iota(jnp.int32, (n, m), dim)


def _sigmoid(x):
    return 1.0 / (1.0 + jnp.exp(-x))


def _softplus_neg_abs(z):
    return jnp.log(1.0 + jnp.exp(-jnp.abs(z)))


_ANY = pl.BlockSpec(memory_space=pl.ANY)


def _mesh_pos():
    return lax.axis_index("x"), lax.axis_index("y"), lax.axis_index("c")


def _other_chips(x, y):
    return [(1 - x, y), (x, 1 - y), (1 - x, 1 - y)]


def _rcopy(src, dst, send_sem, recv_sem, to):
    return pltpu.make_async_remote_copy(src_ref=src, dst_ref=dst, send_sem=send_sem, recv_sem=recv_sem,
                                        device_id=to, device_id_type=pl.DeviceIdType.MESH)


def _push_copies(src_ref, dst_ref, send_sems, recv_sems, loc_sem, scatter):
    x, y, c = _mesh_pos()
    me = 4 * x + 2 * y + c
    own = pltpu.make_async_copy(src_ref.at[me] if scatter else src_ref, dst_ref.at[me], loc_sem)
    pairs = []
    for k in range(1, N_DEV):
        px = 1 - x if k & 4 else x
        py = 1 - y if k & 2 else y
        pc = 1 - c if k & 1 else c
        pid = 4 * px + 2 * py + pc
        src = src_ref.at[pid] if scatter else src_ref
        send = _rcopy(src, dst_ref.at[me], send_sems.at[k - 1], recv_sems.at[k - 1], (px, py, pc))
        recv = _rcopy(src, dst_ref.at[pid], send_sems.at[k - 1], recv_sems.at[k - 1], (px, py, pc))
        pairs.append((send, recv))
    return own, pairs


def _push_start(own, pairs):
    own.start()
    for send, _ in pairs:
        send.start()


def _push_wait(own, pairs):
    for _, recv in pairs:
        recv.wait_recv()
    for send, _ in pairs:
        send.wait_send()
    own.wait()


_PUSH_SEMS = [pltpu.SemaphoreType.DMA((N_DEV - 1,)), pltpu.SemaphoreType.DMA((N_DEV - 1,)),
              pltpu.SemaphoreType.DMA]


def _chip_copies(src_ref, dst_ref, send_sems, recv_sems, loc_sem):
    x, y, c = _mesh_pos()
    own = pltpu.make_async_copy(src_ref.at[2 * x + y], dst_ref.at[3], loc_sem)
    pairs = []
    for j, (px, py) in enumerate(_other_chips(x, y)):
        cp = _rcopy(src_ref.at[2 * px + py], dst_ref.at[j], send_sems.at[j], recv_sems.at[j], (px, py, c))
        pairs.append((cp, cp))
    return own, pairs


_CHIP_SEMS = [pltpu.SemaphoreType.DMA((3,)), pltpu.SemaphoreType.DMA((3,)), pltpu.SemaphoreType.DMA]


def _all_gather(arrs, name, row_pieces=None):
    n = len(arrs)
    pieces = [[None] if not row_pieces or not row_pieces[a] else list(row_pieces[a]) for a in range(n)]
    n_pc = max(len(p) for p in pieces)
    units = [(a, i) for a in range(n) for i in range(len(pieces[a]))]

    def body(*refs):
        ins = refs[:n]
        outs = refs[n:2 * n]
        send_sems, recv_sems, loc_sems = refs[2 * n:]
        x, y, c = _mesh_pos()
        sib = (x, y, 1 - c)
        chips = _other_chips(x, y)

        def rows(ref, a, i):
            return ref if pieces[a][i] is None else ref.at[pl.ds(*pieces[a][i])]

        def place(a, i, px, py, pc):
            return rows(outs[a].at[4 * px + 2 * py + pc], a, i)

        def copy(u, k, block, to, own=False):
            a, i = u
            dst = place(a, i, *block)
            return _rcopy(rows(ins[a], a, i) if own else dst, dst, send_sems.at[a, k, i], recv_sems.at[a, k, i], to)

        mine = [pltpu.make_async_copy(ins[a], outs[a].at[4 * x + 2 * y + c], loc_sems.at[a]) for a in range(n)]
        for cp in mine:
            cp.start()
        first = [copy(u, 0, (x, y, c), sib, own=True) for u in units]
        for j, chip in enumerate(chips):
            first += [copy(u, 1 + j, (x, y, c), (*chip, c), own=True) for u in units]
        for cp in first:
            cp.start()
        passed = []
        for u in units:
            for j, chip in enumerate(chips):
                copy(u, 1 + j, (*chip, c), (x, y, c)).wait_recv()
                fwd = copy(u, 4 + j, (*chip, c), sib)
                fwd.start()
                passed.append(fwd)
        for u in units:
            copy(u, 0, sib, (x, y, c)).wait_recv()
        for u in units:
            for j, chip in enumerate(chips):
                copy(u, 4 + j, (*chip, 1 - c), (x, y, c)).wait_recv()
        for cp in first + passed:
            cp.wait_send()
        for cp in mine:
            cp.wait()

    return pl.pallas_call(
        body, name=name,
        out_shape=tuple(jax.ShapeDtypeStruct((N_DEV,) + a.shape, a.dtype) for a in arrs),
        in_specs=[_ANY] * n,
        out_specs=tuple([_ANY] * n),
        scratch_shapes=[pltpu.SemaphoreType.DMA((n, 7, n_pc)), pltpu.SemaphoreType.DMA((n, 7, n_pc)),
                        pltpu.SemaphoreType.DMA((n,))],
    )(*arrs)


def _pair_exchange(arrs, name):
    n = len(arrs)

    def body(*refs):
        ins = refs[:n]
        outs = refs[n:2 * n]
        send_sems, recv_sems = refs[2 * n:]
        x, y, c = _mesh_pos()
        copies = []
        for a in range(n):
            for q in range(4):
                cp = _rcopy(ins[a].at[2 * q + (1 - c)], outs[a].at[q], send_sems.at[a, q], recv_sems.at[a, q],
                            (x, y, 1 - c))
                cp.start()
                copies.append(cp)
        for cp in copies:
            cp.wait_recv()
        for cp in copies:
            cp.wait_send()

    return pl.pallas_call(
        body, name=name,
        out_shape=tuple(jax.ShapeDtypeStruct((4,) + a.shape[1:], a.dtype) for a in arrs),
        in_specs=[_ANY] * n,
        out_specs=tuple([_ANY] * n),
        scratch_shapes=[pltpu.SemaphoreType.DMA((n, 4)), pltpu.SemaphoreType.DMA((n, 4))],
    )(*arrs)


def _pair_add_call(parts, recv, c_idx, name):
    _, R, C = parts.shape
    (tr, tc), (steps,), idx = _tiling_2d(R, C)

    def body(c_ref, p_ref, r_ref, o_ref):
        o_ref[...] = (p_ref[...].astype(F32) + r_ref[...].astype(F32)).astype(o_ref.dtype)

    return pl.pallas_call(
        body, name=name,
        grid_spec=pltpu.PrefetchScalarGridSpec(
            num_scalar_prefetch=1,
            grid=(4, steps),
            in_specs=[pl.BlockSpec((None, tr, tc), lambda q, i, c_ref: (2 * q + c_ref[0],) + idx(i)),
                      pl.BlockSpec((None, tr, tc), lambda q, i, c_ref: (q,) + idx(i))],
            out_specs=pl.BlockSpec((None, tr, tc), lambda q, i, c_ref: (q,) + idx(i))),
        out_shape=jax.ShapeDtypeStruct((4, R, C), parts.dtype),
        compiler_params=_cparams(("arbitrary", "arbitrary")),
    )(c_idx, parts, recv)


def _group_row(g):
    return GLA_RANK * (g * (1024 // GLA_RANK) + (g >= RANK_COL // 1024))


def _proj_call(x, norm_g, wt, wr, wp_part):
    T, D = x.shape
    tm = min(1024, T)
    assert tm % TBLK == 0
    n_i = T // tm

    def f_slot(j):
        return ((j >= 2).astype(jnp.int32) + (j >= 6).astype(jnp.int32)
                + (j >= 7).astype(jnp.int32) + (j >= 8).astype(jnp.int32))

    def b_slot(j):
        return (j >= 3).astype(jnp.int32) + (j >= 4).astype(jnp.int32) + (j >= 5).astype(jnp.int32)

    def body(x_ref, g_ref, w_ref, wr_ref, wp_ref, pf_ref, pb_ref, rank_ref, ht_ref, wpall_ref,
             h_scr, send_sems, recv_sems, loc_sem):
        i = pl.program_id(0)
        j = pl.program_id(1)
        own, pairs = _push_copies(wp_ref, wpall_ref, send_sems, recv_sems, loc_sem, scatter=False)

        @pl.when((i == 0) & (j == 0))
        def _():
            _push_start(own, pairs)

        @pl.when(j == 0)
        def _():
            xv = x_ref[...]
            r = lax.rsqrt(jnp.mean(xv * xv, axis=-1, keepdims=True) + EPS)
            h = (xv * r) * g_ref[...]
            hb = _bf(h)
            h_scr[...] = hb
            for b in range(tm // TBLK):
                ht_ref[b] = _bf(h[b * TBLK:(b + 1) * TBLK].T)
            rank_ref[...] = _dot_nt(hb, wr_ref[...])

        is_b = (j == 1) | ((j >= 3) & (j <= 5))

        @pl.when(is_b)
        def _():
            pb_ref[...] = _bf(_dot_nt(h_scr[...], w_ref[...]))

        @pl.when(jnp.logical_not(is_b))
        def _():
            pf_ref[...] = _dot_nt(h_scr[...], w_ref[...])

        @pl.when((i == n_i - 1) & (j == N_GROUPS - 1))
        def _():
            _push_wait(own, pairs)

    return pl.pallas_call(
        body, name="proj",
        grid=(n_i, N_GROUPS),
        in_specs=[pl.BlockSpec((tm, D), lambda i, j: (i, 0)),
                  pl.BlockSpec((1, D), lambda i, j: (0, 0)),
                  pl.BlockSpec((pl.Element(1024), pl.Element(D)), lambda i, j: (_group_row(j), 0)),
                  pl.BlockSpec((128, D), lambda i, j: (0, 0)),
                  _ANY],
        out_specs=(pl.BlockSpec((None, tm, 1024), lambda i, j: (f_slot(j), i, 0)),
                   pl.BlockSpec((None, tm, 1024), lambda i, j: (b_slot(j), i, 0)),
                   pl.BlockSpec((tm, 128), lambda i, j: (i, 0)),
                   pl.BlockSpec((tm // TBLK, D, TBLK), lambda i, j: (i, 0, 0)),
                   _ANY),
        out_shape=(jax.ShapeDtypeStruct((5, T, 1024), F32),
                   jax.ShapeDtypeStruct((4, T, 1024), BF16),
                   jax.ShapeDtypeStruct((T, 128), F32),
                   jax.ShapeDtypeStruct((T // TBLK, D, TBLK), BF16),
                   jax.ShapeDtypeStruct((N_DEV,) + wp_part.shape, wp_part.dtype)),
        scratch_shapes=[pltpu.VMEM((tm, D), BF16)] + _PUSH_SEMS,
        compiler_params=_cparams(("arbitrary", "arbitrary")),
    )(x, norm_g, wt, wr, wp_part)


GLA_STEP_CHUNKS = 4


def _gla_same_chunk(rows):
    return (_iota2(rows, rows, 0) & -GLA_CHUNK) == (_iota2(rows, rows, 1) & -GLA_CHUNK)


def _gla_chunk_terms(la, q, k, n_c):
    C = GLA_CHUNK
    rows = n_c * C
    low = _gla_same_chunk(rows) & (_iota2(rows, rows, 0) >= _iota2(rows, rows, 1))
    b = _tri_left(_bf(low.astype(F32)), la)
    bl = [b[(c + 1) * C - 1:(c + 1) * C, :] for c in range(n_c)]
    bl_rows = jnp.concatenate([jnp.broadcast_to(bl[c], (C, b.shape[1])) for c in range(n_c)], axis=0)
    eb = jnp.exp(b)
    enb = jnp.exp(-b)
    ebl_b = jnp.exp(bl_rows - b)
    scale = GLA_HK ** -0.5
    qe = q * eb * scale
    ke = k * enb
    kd = k * ebl_b
    return bl, eb, enb, ebl_b, qe, ke, kd


def _gla_fwd_call(projf, projb, rank, wdec, bdec):
    T = projf.shape[1]
    C = GLA_CHUNK
    n_chunks = T // C
    n_c = GLA_STEP_CHUNKS
    R = n_c * C
    assert n_chunks % n_c == 0

    def body(qk_ref, v_ref, rank_ref, wd_ref, bd_ref, o_ref, st_ref, la_ref, st_scr):
        @pl.when(pl.program_id(0) == 0)
        def _():
            st_scr[...] = jnp.zeros_like(st_scr)

        dec = _dot(_bf(rank_ref[...]), _bf(wd_ref[...])) + bd_ref[...]
        la = (jnp.minimum(dec, 0.0) - _softplus_neg_abs(dec)) / GLA_TAU
        la_ref[...] = la
        mask = _gla_same_chunk(R) & (_iota2(R, R, 0) >= _iota2(R, R, 1))
        bl, _, _, _, qe, ke, kd = _gla_chunk_terms(la, qk_ref[:, :GLA_DK], qk_ref[:, GLA_DK:], n_c)
        qeb, keb, kdb = _bf(qe), _bf(ke), _bf(kd)
        ebl = [jnp.exp(bl[c]) for c in range(n_c)]
        heads = range(GLA_HEADS)
        ks = [slice(hh * GLA_HK, (hh + 1) * GLA_HK) for hh in heads]
        vs = [slice(hh * GLA_HV, (hh + 1) * GLA_HV) for hh in heads]
        rs = [slice(c * C, (c + 1) * C) for c in range(n_c)]
        p = [_bf(jnp.where(mask, _dot_nt(qeb[:, ks[hh]], keb[:, ks[hh]]), 0.0)) for hh in heads]
        upd = [[_dot_tn(v_ref[rs[c], vs[hh]], kdb[rs[c], ks[hh]]) for hh in heads] for c in range(n_c)]
        intra = [_dot(p[hh], v_ref[:, vs[hh]]) for hh in heads]
        st = [st_scr[hh] for hh in heads]
        for c in range(n_c):
            inter = [_dot_nt(qeb[rs[c], ks[hh]], _bf(st[hh])) for hh in heads]
            for hh in heads:
                st_ref[c, hh] = st[hh]
                o_ref[rs[c], vs[hh]] = intra[hh][rs[c]] + inter[hh]
            st = [st[hh] * ebl[c][:, ks[hh]] + upd[c][hh] for hh in heads]
        for hh in heads:
            st_scr[hh] = st[hh]

    return pl.pallas_call(
        body, name="gla_fwd",
        grid=(n_chunks // n_c,),
        in_specs=[pl.BlockSpec((None, R, 1024), lambda n: (0, n, 0)),
                  pl.BlockSpec((None, R, 1024), lambda n: (0, n, 0)),
                  pl.BlockSpec((R, 128), lambda n: (n, 0)),
                  pl.BlockSpec((128, GLA_DK), lambda n: (0, 0)),
                  pl.BlockSpec((1, GLA_DK), lambda n: (0, 0))],
        out_specs=(pl.BlockSpec((R, 1024), lambda n: (n, 0)),
                   pl.BlockSpec((n_c, GLA_HEADS, GLA_HV, GLA_HK), lambda n: (n, 0, 0, 0)),
                   pl.BlockSpec((R, GLA_DK), lambda n: (n, 0))),
        out_shape=(jax.ShapeDtypeStruct((T, 1024), F32),
                   jax.ShapeDtypeStruct((n_chunks, GLA_HEADS, GLA_HV, GLA_HK), F32),
                   jax.ShapeDtypeStruct((T, GLA_DK), F32)),
        scratch_shapes=[pltpu.VMEM((GLA_HEADS, GLA_HV, GLA_HK), F32)],
        compiler_params=_cparams(("arbitrary",)),
    )(projf, projb, rank, wdec, bdec)


def _gla_bwd_call(projf, projb, la, do_gla, st_all, rank, wdec):
    T = projf.shape[1]
    C = GLA_CHUNK
    n_chunks = T // C
    n_c = GLA_STEP_CHUNKS
    R = n_c * C
    assert n_chunks % n_c == 0
    last = n_chunks // n_c - 1

    def body(qk_ref, v_ref, la_ref, do_ref, st_ref, rank_ref, wd_ref,
             dqk_ref, dv_ref, drank_ref, dwd_ref, dbd_ref, dst_scr):
        @pl.when(pl.program_id(0) == 0)
        def _():
            dst_scr[...] = jnp.zeros_like(dst_scr)
            dwd_ref[...] = jnp.zeros_like(dwd_ref)
            dbd_ref[...] = jnp.zeros_like(dbd_ref)

        same = _gla_same_chunk(R)
        mask = same & (_iota2(R, R, 0) >= _iota2(R, R, 1))
        upp = _bf((same & (_iota2(R, R, 0) <= _iota2(R, R, 1))).astype(F32))
        scale = GLA_HK ** -0.5
        la = la_ref[...]
        bl, eb, enb, ebl_b, qe, ke, kd = _gla_chunk_terms(la, qk_ref[:, :GLA_DK], qk_ref[:, GLA_DK:], n_c)
        qeb, keb, kdb = _bf(qe), _bf(ke), _bf(kd)
        ebl = [jnp.exp(bl[c]) for c in range(n_c)]
        heads = range(GLA_HEADS)
        ks = [slice(hh * GLA_HK, (hh + 1) * GLA_HK) for hh in heads]
        vs = [slice(hh * GLA_HV, (hh + 1) * GLA_HV) for hh in heads]
        rs = [slice(c * C, (c + 1) * C) for c in range(n_c)]
        v = [v_ref[:, vs[hh]] for hh in heads]
        do = [_bf(do_ref[:, vs[hh]]) for hh in heads]
        p = [_bf(jnp.where(mask, _dot_nt(qeb[:, ks[hh]], keb[:, ks[hh]]), 0.0)) for hh in heads]
        dp = [_bf(jnp.where(mask, _dot_nt(do[hh], v[hh]), 0.0)) for hh in heads]
        dst_intra = [[_dot_tn(do[hh][rs[c]], qeb[rs[c], ks[hh]]) for hh in heads] for c in range(n_c)]
        dqe_inter = [[_dot(do[hh][rs[c]], _bf(st_ref[c, hh])) for hh in heads] for c in range(n_c)]
        dv_intra = [_dot_tn(p[hh], do[hh]) for hh in heads]
        dqe_intra = [_dot(dp[hh], keb[:, ks[hh]]) for hh in heads]
        dke = jnp.concatenate([_dot_tn(dp[hh], qeb[:, ks[hh]]) for hh in heads], axis=1)
        dstn = [dst_scr[hh] for hh in heads]
        dkd_c, dv_inter, debl = [None] * n_c, [None] * n_c, [None] * n_c
        for c in reversed(range(n_c)):
            dstnb = [_bf(dstn[hh]) for hh in heads]
            dkd_c[c] = jnp.concatenate([_dot(v[hh][rs[c]], dstnb[hh]) for hh in heads], axis=1)
            dv_inter[c] = [_dot_nt(kdb[rs[c], ks[hh]], dstnb[hh]) for hh in heads]
            debl[c] = jnp.concatenate(
                [jnp.sum(dstn[hh] * st_ref[c, hh], axis=0, keepdims=True) for hh in heads], axis=1)
            dstn = [dst_intra[c][hh] + dstn[hh] * ebl[c][:, ks[hh]] for hh in heads]
        for hh in heads:
            dst_scr[hh] = dstn[hh]
            dv_ref[:, vs[hh]] = _bf(dv_intra[hh] + jnp.concatenate([dv_inter[c][hh] for c in range(n_c)], axis=0))
        dqe = jnp.concatenate(
            [dqe_intra[hh] + jnp.concatenate([dqe_inter[c][hh] for c in range(n_c)], axis=0) for hh in heads], axis=1)
        dkd = jnp.concatenate(dkd_c, axis=0)
        dkd_kd = dkd * kd
        db = dqe * qe - dke * ke - dkd_kd
        dbl = jnp.concatenate(
            [jnp.broadcast_to(jnp.sum(dkd_kd[rs[c]], axis=0, keepdims=True) + ebl[c] * debl[c], (C, GLA_DK))
             for c in range(n_c)], axis=0)
        dla = _tri_left(upp, db) + dbl
        dqk_ref[:, :GLA_DK] = _bf(dqe * eb * scale)
        dqk_ref[:, GLA_DK:] = _bf(dke * enb + dkd * ebl_b)
        ddec = dla * (1.0 / GLA_TAU) * (1.0 - jnp.exp(GLA_TAU * la))
        ddecb = _bf(ddec)
        drank_ref[...] = _bf(_dot_nt(ddecb, _bf(wd_ref[...])))
        dwd_ref[...] += _dot_tn(_bf(rank_ref[...]), ddecb)
        dbd_ref[...] += jnp.sum(ddec, axis=0, keepdims=True)

    return pl.pallas_call(
        body, name="gla_bwd",
        grid=(n_chunks // n_c,),
        in_specs=[pl.BlockSpec((None, R, 1024), lambda n: (0, last - n, 0)),
                  pl.BlockSpec((None, R, 1024), lambda n: (0, last - n, 0)),
                  pl.BlockSpec((R, GLA_DK), lambda n: (last - n, 0)),
                  pl.BlockSpec((R, 1024), lambda n: (last - n, 0)),
                  pl.BlockSpec((n_c, GLA_HEADS, GLA_HV, GLA_HK), lambda n: (last - n, 0, 0, 0)),
                  pl.BlockSpec((R, 128), lambda n: (last - n, 0)),
                  pl.BlockSpec((128, GLA_DK), lambda n: (0, 0))],
        out_specs=(pl.BlockSpec((R, 1024), lambda n: (last - n, 0)),
                   pl.BlockSpec((R, 1024), lambda n: (last - n, 0)),
                   pl.BlockSpec((R, 128), lambda n: (last - n, 0)),
                   pl.BlockSpec((128, GLA_DK), lambda n: (0, 0)),
                   pl.BlockSpec((1, GLA_DK), lambda n: (0, 0))),
        out_shape=(jax.ShapeDtypeStruct((T, 1024), BF16),
                   jax.ShapeDtypeStruct((T, 1024), BF16),
                   jax.ShapeDtypeStruct((T, 128), BF16),
                   jax.ShapeDtypeStruct((128, GLA_DK), F32),
                   jax.ShapeDtypeStruct((1, GLA_DK), F32)),
        scratch_shapes=[pltpu.VMEM((GLA_HEADS, GLA_HV, GLA_HK), F32)],
        compiler_params=_cparams(("arbitrary",)),
    )(projf, projb, la, do_gla, st_all, rank, wdec)


def _sb_logs(z):
    lsz = jnp.minimum(z, 0.0) - _softplus_neg_abs(z)
    return lsz, lsz - z


SB_HG_FWD = 8
SB_HG_BWD = 4
SB_QUERIES = 256
SB_KEYS = 256
SB_DEAD = -105.0


def _sb_fwd_call(projb, wp_shard):
    T = projb.shape[1]
    B = min(SB_QUERIES, T)
    HG = SB_HG_FWD
    W = HG * SB_HD
    scale = 1.0 / math.sqrt(SB_HD)
    KB = min(SB_KEYS, T)
    n_h, n_i = SB_HEADS // HG, T // B

    def body(q_ref, k_ref, v_ref, wp_ref, o_ref, wpall_ref, cb_scr, send_sems, recv_sems, loc_sem):
        i = pl.program_id(1)
        own, pairs = _push_copies(wp_ref, wpall_ref, send_sems, recv_sems, loc_sem, scatter=False)

        @pl.when((pl.program_id(0) == 0) & (i == 0))
        def _():
            _push_start(own, pairs)

        rows = HG * B
        after = (_iota2(KB, KB, 0) > _iota2(KB, KB, 1)).astype(F32)
        tri = _bf(jnp.concatenate([after, jnp.ones((KB, KB), F32)], axis=1))
        o_ref[...] = jnp.zeros_like(o_ref)
        cb_scr[...] = jnp.zeros_like(cb_scr)

        def block(jp, masked):
            off = pl.multiple_of(jp * KB, KB)
            z = jnp.concatenate(
                [_dot_nt(q_ref[:, hh * SB_HD:(hh + 1) * SB_HD], k_ref[pl.ds(off, KB), hh * SB_HD:(hh + 1) * SB_HD])
                 for hh in range(HG)], axis=0) * scale
            lsz, l1m = _sb_logs(z)
            if masked:
                strict = (jp * KB + _iota2(rows, KB, 1)) < (i * B + (_iota2(rows, KB, 0) & (B - 1)))
                l1m = jnp.where(strict, l1m, 0.0)
            r = _tri2_right(l1m, tri)
            cb = cb_scr[...]
            a = jnp.exp(lsz + cb + r[:, :KB])
            if masked:
                a = jnp.where(strict, a, 0.0)
            cb_scr[...] = cb + r[:, KB:]
            ab = _bf(a)
            for hh in range(HG):
                cs = slice(hh * SB_HD, (hh + 1) * SB_HD)
                o_ref[:, cs] += _dot(ab[hh * B:(hh + 1) * B, :], v_ref[pl.ds(off, KB), cs])

        jp0 = (i * B) // KB
        block(jp0, True)

        def live(state):
            jj, dead = state
            return (jj <= jp0) & jnp.logical_not(dead)

        def step(state):
            jj, _ = state
            block(jp0 - jj, False)
            return jj + 1, jnp.max(cb_scr[:, :SB_HD]) < SB_DEAD

        lax.while_loop(live, step, (jnp.int32(1), jnp.max(cb_scr[:, :SB_HD]) < SB_DEAD))

        @pl.when((pl.program_id(0) == n_h - 1) & (i == n_i - 1))
        def _():
            _push_wait(own, pairs)

    return pl.pallas_call(
        body, name="sb_fwd",
        grid=(n_h, n_i),
        in_specs=[pl.BlockSpec((None, B, W), lambda h, i: (1, i, h)),
                  pl.BlockSpec((None, T, W), lambda h, i: (2, 0, h)),
                  pl.BlockSpec((None, T, W), lambda h, i: (3, 0, h)),
                  _ANY],
        out_specs=(pl.BlockSpec((B, W), lambda h, i: (i, h)), _ANY),
        out_shape=(jax.ShapeDtypeStruct((T, 1024), F32),
                   jax.ShapeDtypeStruct((N_DEV,) + wp_shard.shape, wp_shard.dtype)),
        scratch_shapes=[pltpu.VMEM((HG * B, KB), F32)] + _PUSH_SEMS,
        compiler_params=_cparams(("arbitrary", "arbitrary")),
    )(projb, projb, projb, wp_shard)


def _sb_bwd_call(projb, do_sb, g_p):
    T = projb.shape[1]
    B = min(SB_QUERIES, T)
    nb = T // B
    HG = SB_HG_BWD
    W = HG * SB_HD
    WQ = HG * B
    KB = min(SB_KEYS, T)
    nkb = T // KB
    n_h = SB_HEADS // HG
    scale = 1.0 / math.sqrt(SB_HD)

    def body(q_ref, k_ref, v_ref, do_ref, gp_ref, dq_ref, dk_ref, dv_ref, rp_ref,
             dk_scr, dv_scr, kt_scr, beta_scr, g_scr, dqt_scr, send_sems, recv_sems, loc_sem):
        i = pl.program_id(1)
        own, pairs = _push_copies(gp_ref, rp_ref, send_sems, recv_sems, loc_sem, scatter=True)

        @pl.when((pl.program_id(0) == 0) & (i == 0))
        def _():
            _push_start(own, pairs)

        @pl.when(i == 0)
        def _():
            dk_scr[...] = jnp.zeros_like(dk_scr)
            dv_scr[...] = jnp.zeros_like(dv_scr)
            for hh in range(HG):
                for jb in range(nkb):
                    kt_scr[hh, jb] = _bf(
                        k_ref[jb * KB:(jb + 1) * KB, hh * SB_HD:(hh + 1) * SB_HD].astype(F32).T)

        dqt_scr[...] = jnp.zeros_like(dqt_scr)
        later = _bf((_iota2(KB, KB, 1) > _iota2(KB, KB, 0)).astype(F32))
        earlier = _bf((_iota2(KB, KB, 1) < _iota2(KB, KB, 0)).astype(F32))
        dob = _bf(do_ref[...])
        jp0 = (i * B) // KB

        def strict_mask():
            return (jp0 * KB + _iota2(KB, WQ, 0)) < (i * B + (_iota2(KB, WQ, 1) & (B - 1)))

        def heads(fn):
            return [fn(slice(hh * SB_HD, (hh + 1) * SB_HD)) for hh in range(HG)]

        def pass1(jp, cb, masked):
            off = pl.multiple_of(jp * KB, KB)
            z = jnp.concatenate(heads(lambda cs: _dot_nt(k_ref[pl.ds(off, KB), cs], q_ref[:, cs])), axis=1) * scale
            da = jnp.concatenate(heads(lambda cs: _dot_nt(v_ref[pl.ds(off, KB), cs], dob[:, cs])), axis=1)
            lsz, l1m = _sb_logs(z)
            if masked:
                strict = strict_mask()
                l1m = jnp.where(strict, l1m, 0.0)
            a = jnp.exp(lsz + cb + _tri2_left(later, l1m))
            if masked:
                a = jnp.where(strict, a, 0.0)
            g_scr[jp] = a * da
            beta_scr[jp] = jnp.exp(lsz)
            ab = _bf(a)
            for hh in range(HG):
                cs = slice(hh * SB_HD, (hh + 1) * SB_HD)
                dv_scr[pl.ds(off, KB), cs] += _dot(ab[:, hh * B:(hh + 1) * B], dob[:, cs])
            return cb + jnp.sum(l1m, axis=0, keepdims=True)

        zero = jnp.zeros((1, WQ), F32)
        cb = pass1(jp0, zero, True)

        def live(state):
            jj, _, dead = state
            return (jj <= jp0) & jnp.logical_not(dead)

        def step(state):
            jj, cr, _ = state
            cr = pass1(jp0 - jj, cr, False)
            return jj + 1, cr, jnp.max(cr) < SB_DEAD

        n_done, _, _ = lax.while_loop(live, step, (jnp.int32(1), cb, jnp.max(cb) < SB_DEAD))
        jp_first = jp0 - (n_done - 1)

        def pass2(jp, cg, masked):
            off = pl.multiple_of(jp * KB, KB)
            g = g_scr[jp]
            beta = beta_scr[jp]
            dz = g * (1.0 - beta) - beta * (cg + _tri2_left(earlier, g))
            if masked:
                dz = jnp.where(strict_mask(), dz, 0.0)
            dzb = _bf(dz * scale)
            for hh in range(HG):
                cs = slice(hh * SB_HD, (hh + 1) * SB_HD)
                dk_scr[pl.ds(off, KB), cs] += _dot(dzb[:, hh * B:(hh + 1) * B], q_ref[:, cs])
                dqt_scr[hh] += _dot(kt_scr[hh, jp], dzb[:, hh * B:(hh + 1) * B])
            return cg + jnp.sum(g, axis=0, keepdims=True)

        cg = lax.fori_loop(jp_first, jp0, lambda jp, cr: pass2(jp, cr, False), zero)
        pass2(jp0, cg, True)
        for hh in range(HG):
            dq_ref[:, hh * SB_HD:(hh + 1) * SB_HD] = _bf(dqt_scr[hh].T)

        @pl.when(i == nb - 1)
        def _():
            dk_ref[...] = _bf(dk_scr[...])
            dv_ref[...] = _bf(dv_scr[...])

        @pl.when((pl.program_id(0) == n_h - 1) & (i == nb - 1))
        def _():
            _push_wait(own, pairs)

    return pl.pallas_call(
        body, name="sb_bwd",
        grid=(n_h, nb),
        in_specs=[pl.BlockSpec((None, B, W), lambda h, i: (1, i, h)),
                  pl.BlockSpec((None, T, W), lambda h, i: (2, 0, h)),
                  pl.BlockSpec((None, T, W), lambda h, i: (3, 0, h)),
                  pl.BlockSpec((B, W), lambda h, i: (i, h)),
                  _ANY],
        out_specs=(pl.BlockSpec((B, W), lambda h, i: (i, h)),
                   pl.BlockSpec((T, W), lambda h, i: (0, h)),
                   pl.BlockSpec((T, W), lambda h, i: (0, h)),
                   _ANY),
        out_shape=(jax.ShapeDtypeStruct((T, 1024), BF16),
                   jax.ShapeDtypeStruct((T, 1024), BF16),
                   jax.ShapeDtypeStruct((T, 1024), BF16),
                   jax.ShapeDtypeStruct(g_p.shape, g_p.dtype)),
        scratch_shapes=[pltpu.VMEM((T, W), F32), pltpu.VMEM((T, W), F32),
                        pltpu.VMEM((HG, nkb, SB_HD, KB), BF16),
                        pltpu.VMEM((nkb, KB, WQ), F32), pltpu.VMEM((nkb, KB, WQ), F32),
                        pltpu.VMEM((HG, SB_HD, B), F32)] + _PUSH_SEMS,
        compiler_params=_cparams(("arbitrary", "arbitrary")),
    )(projb, projb, projb, do_sb, g_p)


def _mid_call(o_gla, o_sb, projf, x, target, wpa, wpb, wo, gla_g, b_gate, final_g):
    T, D = x.shape
    tm = min(TBLK, T)

    def body(og_ref, ggate_ref, osb_ref, sgate_ref, ma_ref, mb_ref, x_ref, tgt_ref,
             wpa_ref, wpb_ref, wo_ref, glag_ref, bg_ref, fg_ref,
             dx2_ref, dogla_ref, dosb_ref, dggate_ref, dsgate_ref, dm_ref,
             mt_ref, ogt_ref, obt_ref, dx2b_ref, dya_ref, dyb_ref,
             dfg_ref, dbg_ref, dglag_ref, loss_ref):
        @pl.when(pl.program_id(0) == 0)
        def _():
            dfg_ref[...] = jnp.zeros_like(dfg_ref)
            dbg_ref[...] = jnp.zeros_like(dbg_ref)
            dglag_ref[...] = jnp.zeros_like(dglag_ref)
            loss_ref[...] = jnp.zeros_like(loss_ref)

        glag = glag_ref[...]
        ggate = ggate_ref[...]
        sg = _sigmoid(ggate)
        silu_g = ggate * sg
        ohat, rinv, nrm = [], [], []
        for hh in range(GLA_HEADS):
            oh = og_ref[:, hh * GLA_HV:(hh + 1) * GLA_HV]
            r = lax.rsqrt(jnp.mean(oh * oh, axis=-1, keepdims=True) + EPS)
            ohat.append(oh * r)
            rinv.append(r)
            nrm.append(ohat[-1] * glag)
        n_all = jnp.concatenate(nrm, axis=1)
        og = n_all * silu_g
        ogb = _bf(og)
        ya = _dot(ogb, wpa_ref[...])
        sgate = sgate_ref[...]
        ss = _sigmoid(sgate)
        silu_s = sgate * ss
        osb = osb_ref[...]
        ob = osb * silu_s
        obb = _bf(ob)
        yb = _dot(obb, wpb_ref[...])
        ga = _sigmoid(ma_ref[...] + bg_ref[:, :D])
        gb = _sigmoid(mb_ref[...] + bg_ref[:, D:])
        merged = ga * ya + gb * yb
        mgb = _bf(merged)
        x2 = x_ref[...] + _dot(mgb, wo_ref[...])
        r2 = lax.rsqrt(jnp.mean(x2 * x2, axis=-1, keepdims=True) + EPS)
        xh2 = x2 * r2
        fg = fg_ref[...]
        err = xh2 * fg - tgt_ref[...]
        loss_ref[...] += jnp.broadcast_to(
            0.5 * jnp.sum(jnp.mean(err * err, axis=-1, keepdims=True), axis=0, keepdims=True), (1, 128))
        dy = err * (1.0 / D)
        dfg_ref[...] += jnp.sum(dy * xh2, axis=0, keepdims=True)
        dxh = dy * fg
        dx2 = r2 * (dxh - xh2 * jnp.mean(dxh * xh2, axis=-1, keepdims=True))
        dx2_ref[...] = dx2
        dx2b = _bf(dx2)
        dx2b_ref[...] = dx2b
        dmerged = _dot_nt(dx2b, wo_ref[...])
        dya = dmerged * ga
        dyb = dmerged * gb
        dma = dmerged * ya * ga * (1.0 - ga)
        dmb = dmerged * yb * gb * (1.0 - gb)
        dm_ref[:, :D] = _bf(dma)
        dm_ref[:, D:] = _bf(dmb)
        dbg_ref[:, :D] += jnp.sum(dma, axis=0, keepdims=True)
        dbg_ref[:, D:] += jnp.sum(dmb, axis=0, keepdims=True)
        dyab = _bf(dya)
        dybb = _bf(dyb)
        dya_ref[...] = dyab
        dyb_ref[...] = dybb
        dog = _dot_nt(dyab, wpa_ref[...])
        dob = _dot_nt(dybb, wpb_ref[...])
        dosb_ref[...] = dob * silu_s
        dsgate_ref[...] = _bf(dob * osb * (ss * (1.0 + sgate * (1.0 - ss))))
        dn = dog * silu_g
        dggate_ref[...] = _bf(dog * n_all * (sg * (1.0 + ggate * (1.0 - sg))))
        dglag = jnp.zeros((1, GLA_HV), F32)
        for hh in range(GLA_HEADS):
            dnh = dn[:, hh * GLA_HV:(hh + 1) * GLA_HV]
            dglag = dglag + jnp.sum(dnh * ohat[hh], axis=0, keepdims=True)
            dohat = dnh * glag
            dogla_ref[:, hh * GLA_HV:(hh + 1) * GLA_HV] = rinv[hh] * (
                dohat - ohat[hh] * jnp.mean(dohat * ohat[hh], axis=-1, keepdims=True))
        dglag_ref[...] += dglag
        mt_ref[...] = _bf(merged.T)
        ogt_ref[...] = _bf(og.T)
        obt_ref[...] = _bf(ob.T)

    row = lambda i: (i, 0)
    const = lambda i: (0, 0)
    tile = pl.BlockSpec((tm, D), row)
    tile_t = pl.BlockSpec((None, D, tm), lambda i: (i, 0, 0))
    wspec = pl.BlockSpec((D, D), const)
    return pl.pallas_call(
        body, name="mid",
        grid=(T // tm,),
        in_specs=[tile,
                  pl.BlockSpec((None, tm, D), lambda i: (1, i, 0)),
                  tile,
                  pl.BlockSpec((None, tm, D), lambda i: (2, i, 0)),
                  pl.BlockSpec((None, tm, D), lambda i: (3, i, 0)),
                  pl.BlockSpec((None, tm, D), lambda i: (4, i, 0)),
                  tile, tile, wspec, wspec, wspec,
                  pl.BlockSpec((1, GLA_HV), const),
                  pl.BlockSpec((1, 2 * D), const),
                  pl.BlockSpec((1, D), const)],
        out_specs=(tile, tile, tile, tile, tile,
                   pl.BlockSpec((tm, 2 * D), row),
                   tile_t, tile_t, tile_t, tile, tile, tile,
                   pl.BlockSpec((1, D), const),
                   pl.BlockSpec((1, 2 * D), const),
                   pl.BlockSpec((1, GLA_HV), const),
                   pl.BlockSpec((1, 128), const)),
        out_shape=(jax.ShapeDtypeStruct((T, D), F32),
                   jax.ShapeDtypeStruct((T, D), F32),
                   jax.ShapeDtypeStruct((T, D), F32),
                   jax.ShapeDtypeStruct((T, D), BF16),
                   jax.ShapeDtypeStruct((T, D), BF16),
                   jax.ShapeDtypeStruct((T, 2 * D), BF16),
                   jax.ShapeDtypeStruct((T // tm, D, tm), BF16),
                   jax.ShapeDtypeStruct((T // tm, D, tm), BF16),
                   jax.ShapeDtypeStruct((T // tm, D, tm), BF16),
                   jax.ShapeDtypeStruct((T, D), BF16),
                   jax.ShapeDtypeStruct((T, D), BF16),
                   jax.ShapeDtypeStruct((T, D), BF16),
                   jax.ShapeDtypeStruct((1, D), F32),
                   jax.ShapeDtypeStruct((1, 2 * D), F32),
                   jax.ShapeDtypeStruct((1, GLA_HV), F32),
                   jax.ShapeDtypeStruct((1, 128), F32)),
        compiler_params=_cparams(("arbitrary",)),
    )(o_gla, projf, o_sb, projf, projf, projf, x, target, wpa, wpb, wo, gla_g, b_gate, final_g)


def _dh_call(pieces, dmlog, drank, wt, wr, x, dx2, norm_g, s_in):
    T, D = x.shape
    tm = min(256, T)
    npc = len(pieces)
    n_main = N_GROUPS * 1024
    n_i = T // tm

    def body(*refs):
        pcs = refs[:npc]
        (dm_ref, dr_ref, w_hbm, wr_ref, x_ref, dx2_ref, g_ref, sin_ref,
         gx_ref, dg_ref, rin_ref, w_scr, sems, send_sems, recv_sems, loc_sem) = refs[npc:]
        own, pairs = _chip_copies(sin_ref, rin_ref, send_sems, recv_sems, loc_sem)

        @pl.when(pl.program_id(0) == 0)
        def _():
            _push_start(own, pairs)
            lo = pltpu.make_async_copy(w_hbm.at[pl.ds(0, RANK_COL)], w_scr.at[pl.ds(0, RANK_COL)], sems.at[0])
            hi = pltpu.make_async_copy(w_hbm.at[pl.ds(RANK_COL + GLA_RANK, n_main - RANK_COL)],
                                       w_scr.at[pl.ds(RANK_COL, n_main - RANK_COL)], sems.at[1])
            lo.start()
            hi.start()
            dg_ref[...] = jnp.zeros_like(dg_ref)
            lo.wait()
            hi.wait()

        def w_group(g):
            return w_scr[g * 1024:(g + 1) * 1024, :]

        dr = dr_ref[...]
        dh = _dot(dr, wr_ref[...])
        for g in range(npc):
            dh = dh + _dot(pcs[g][...], w_group(g))
        dh = dh + _dot(dm_ref[:, :D], w_group(npc))
        dh = dh + _dot(dm_ref[:, D:], w_group(npc + 1))
        xv = x_ref[...]
        r = lax.rsqrt(jnp.mean(xv * xv, axis=-1, keepdims=True) + EPS)
        xhat = xv * r
        g = g_ref[...]
        dg_ref[...] += jnp.sum(dh * xhat, axis=0, keepdims=True)
        dxhat = dh * g
        gx_ref[...] = r * (dxhat - xhat * jnp.mean(dxhat * xhat, axis=-1, keepdims=True)) + dx2_ref[...]

        @pl.when(pl.program_id(0) == n_i - 1)
        def _():
            _push_wait(own, pairs)

    row = lambda i: (i, 0)
    const = lambda i: (0, 0)
    tile = pl.BlockSpec((tm, D), row)
    return pl.pallas_call(
        body, name="dh",
        grid=(n_i,),
        in_specs=[tile] * npc + [
            pl.BlockSpec((tm, 2 * D), row),
            pl.BlockSpec((tm, 128), row),
            _ANY,
            pl.BlockSpec((128, D), const),
            tile, tile,
            pl.BlockSpec((1, D), const),
            _ANY],
        out_specs=(tile, pl.BlockSpec((1, D), const), _ANY),
        out_shape=(jax.ShapeDtypeStruct((T, D), F32),
                   jax.ShapeDtypeStruct((1, D), F32),
                   jax.ShapeDtypeStruct(s_in.shape, s_in.dtype)),
        scratch_shapes=[pltpu.VMEM((n_main, D), BF16), pltpu.SemaphoreType.DMA((2,))] + _CHIP_SEMS,
        compiler_params=_cparams(("arbitrary",)),
    )(*pieces, dmlog, drank, wt, wr, x, dx2, norm_g, s_in)


def _wgrad_rank_call(ht, drank):
    n_tb, D, tb = ht.shape

    def body(ht_ref, dr_ref, o_ref):
        @pl.when(pl.program_id(0) == 0)
        def _():
            o_ref[...] = jnp.zeros_like(o_ref)

        o_ref[...] += _dot(ht_ref[...], dr_ref[...])

    return pl.pallas_call(
        body, name="wgrad_rank",
        grid=(n_tb,),
        in_specs=[pl.BlockSpec((None, D, tb), lambda i: (i, 0, 0)),
                  pl.BlockSpec((tb, 128), lambda i: (i, 0))],
        out_specs=pl.BlockSpec((D, 128), lambda i: (0, 0)),
        out_shape=jax.ShapeDtypeStruct((D, 128), F32),
        compiler_params=_cparams(("arbitrary",)),
    )(ht, drank)


def _wgrad_call(lhs_list, lhs_of_group, rhs_list, rhs_of_group, n_transposed, name):
    n_groups = len(rhs_of_group)
    n_tb, D, tb = lhs_list[0].shape
    T = n_tb * tb
    per = min(2, n_tb)
    tk = per * tb
    nk = T // tk
    nl = len(lhs_list)

    def body(*refs):
        lhs = refs[:nl]
        rhs = refs[nl:nl + n_groups]
        out_ref, acc = refs[nl + n_groups:]
        g = pl.program_id(0)
        i = pl.program_id(1)

        @pl.when(i == 0)
        def _():
            acc[...] = jnp.zeros_like(acc)

        for p in range(n_groups):
            @pl.when(g == p)
            def _(p=p):
                lref = lhs[lhs_of_group[p]]
                part = _dot(lref[0], rhs[p][0:tb, :])
                for b in range(1, per):
                    part = part + _dot(lref[b], rhs[p][b * tb:(b + 1) * tb, :])
                acc[...] += part

        @pl.when((i == nk - 1) & (g < n_transposed))
        def _():
            out_ref[...] = _bf(acc[...].T)

        @pl.when((i == nk - 1) & (g >= n_transposed))
        def _():
            out_ref[...] = _bf(acc[...])

    def lhs_spec(a):
        groups = [g for g in range(n_groups) if lhs_of_group[g] == a]
        lo, hi = min(groups), max(groups)
        assert groups == list(range(lo, hi + 1))
        return pl.BlockSpec((per, D, tb), lambda g, i: (jnp.where((g >= lo) & (g <= hi), i, 0), 0, 0))

    def rhs_spec(p):
        cb = rhs_of_group[p][1]
        return pl.BlockSpec((tk, 1024), lambda g, i: (jnp.where(g == p, i, 0), cb))

    return pl.pallas_call(
        body, name=name,
        grid=(n_groups, nk),
        in_specs=[lhs_spec(a) for a in range(nl)] + [rhs_spec(p) for p in range(n_groups)],
        out_specs=pl.BlockSpec((None, D, 1024), lambda g, i: (g, 0, 0)),
        out_shape=jax.ShapeDtypeStruct((n_groups, D, 1024), BF16),
        scratch_shapes=[pltpu.VMEM((D, 1024), F32)],
        compiler_params=_cparams(("arbitrary", "arbitrary")),
    )(*lhs_list, *[rhs_list[rhs_of_group[p][0]] for p in range(n_groups)])


def _adamw_math(parts, w, m, v):
    g = parts[0].astype(F32)
    for p in parts[1:]:
        g = g + p.astype(F32)
    mm = ADAM_B1 * m + (1.0 - ADAM_B1) * g
    vv = ADAM_B2 * v + (1.0 - ADAM_B2) * (g * g)
    m_hat = mm / (1.0 - ADAM_B1 ** ADAM_STEP)
    v_hat = vv / (1.0 - ADAM_B2 ** ADAM_STEP)
    return g, -ADAM_LR * (m_hat / (jnp.sqrt(v_hat) + ADAM_EPS) + ADAM_WD * w), mm, vv


def _part_order(n_parts):
    return [n_parts - 1] + list(range(n_parts - 1))


def _adamw_call(parts, w, m, v, name):
    R, C = w.shape
    n_parts = parts.shape[0]
    (tr, tc), grid, idx = _tiling_2d(R, C)

    def body(p_ref, w_ref, m_ref, v_ref, g_ref, d_ref, nm_ref, nv_ref):
        g_ref[...], d_ref[...], nm_ref[...], nv_ref[...] = _adamw_math(
            [p_ref[k] for k in _part_order(n_parts)], w_ref[...], m_ref[...], v_ref[...])

    blk = pl.BlockSpec((tr, tc), idx)
    sds = jax.ShapeDtypeStruct((R, C), F32)
    return pl.pallas_call(
        body, name=name,
        grid=grid,
        in_specs=[pl.BlockSpec((n_parts, tr, tc), lambda i: (0,) + idx(i)), blk, blk, blk],
        out_specs=(blk, blk, blk, blk),
        out_shape=(sds, sds, sds, sds),
        compiler_params=_cparams(("arbitrary",)),
    )(parts, w, m, v)


def _adamw_rows_call(parts, ws, ms, vs, name, gathered):
    n = len(ws)
    R, C = ws[0].shape
    n_parts = parts.shape[0]

    def body(*refs):
        p_ref = refs[0]
        w_refs, m_refs, v_refs = refs[1:1 + n], refs[1 + n:1 + 2 * n], refs[1 + 2 * n:1 + 3 * n]
        src_ref = refs[1 + 3 * n]
        outs = refs[2 + 3 * n:2 + 7 * n]
        dst_ref, send_sems, recv_sems, loc_sem = refs[2 + 7 * n:]
        own, pairs = _push_copies(src_ref, dst_ref, send_sems, recv_sems, loc_sem, scatter=False)
        k_now = pl.program_id(0)

        @pl.when(k_now == 0)
        def _():
            _push_start(own, pairs)

        for k in range(n):
            @pl.when(k_now == k)
            def _(k=k):
                res = _adamw_math([p_ref[j] for j in _part_order(n_parts)],
                                  w_refs[k][...], m_refs[k][...], v_refs[k][...])
                for o_ref, val in zip(outs[4 * k:4 * k + 4], res):
                    o_ref[...] = val

        @pl.when(k_now == n - 1)
        def _():
            _push_wait(own, pairs)

    whole = pl.BlockSpec((R, C), lambda k: (0, 0))
    sds = jax.ShapeDtypeStruct((R, C), F32)
    res = pl.pallas_call(
        body, name=name,
        grid=(n,),
        in_specs=[pl.BlockSpec((n_parts, R, C), lambda k: (0, k, 0))] + [whole] * (3 * n) + [_ANY],
        out_specs=tuple([whole] * (4 * n) + [_ANY]),
        out_shape=tuple([sds] * (4 * n) + [jax.ShapeDtypeStruct((N_DEV,) + gathered.shape, gathered.dtype)]),
        scratch_shapes=_PUSH_SEMS,
        compiler_params=_cparams(("arbitrary",)),
    )(parts, *ws, *ms, *vs, gathered)
    return [res[4 * k:4 * k + 4] for k in range(n)], res[4 * n]


def _adamw_lanes_call(parts, offsets, ws, ms, vs, name):
    n = len(ws)
    n_parts = parts.shape[0]

    def body(*refs):
        p_ref = refs[0]
        w_refs, m_refs, v_refs = refs[1:1 + n], refs[1 + n:1 + 2 * n], refs[1 + 2 * n:1 + 3 * n]
        outs = refs[1 + 3 * n:]
        for k in range(n):
            lanes = slice(offsets[k], offsets[k] + ws[k].shape[1])
            res = _adamw_math([p_ref[j, :, lanes] for j in _part_order(n_parts)],
                              w_refs[k][...], m_refs[k][...], v_refs[k][...])
            for o_ref, val in zip(outs[4 * k:4 * k + 4], res):
                o_ref[...] = val

    res = pl.pallas_call(
        body, name=name,
        out_shape=tuple(jax.ShapeDtypeStruct(ws[k].shape, F32) for k in range(n) for _ in range(4)),
        compiler_params=_cparams(),
    )(parts, *ws, *ms, *vs)
    return [res[4 * k:4 * k + 4] for k in range(n)]


def _local_step(x, target, wt, wr, wdec, bdec, wp_shard, norm_g, gla_g, b_gate, final_g):
    D = x.shape[1]
    half = wp_shard.shape[1] // 2
    projf, projb, rank, ht, wp_lo = _proj_call(x, norm_g, wt, wr, wp_shard[:, :half])
    o_gla, st_all, la = _gla_fwd_call(projf, projb, rank, wdec, bdec)
    o_sb, wp_hi = _sb_fwd_call(projb, wp_shard[:, half:])
    wp_full = jnp.concatenate([wp_lo, wp_hi], axis=2).transpose(1, 0, 2, 3).reshape(3, D, D)
    (dx2, do_gla, do_sb, dggate, dsgate, dmlog, mt, ogt, obt, dx2b, dya, dyb,
     dfinal_g, db_gate, dgla_g, loss) = _mid_call(o_gla, o_sb, projf, x, target, wp_full[0], wp_full[1],
                                                 wp_full[2], gla_g, b_gate, final_g)
    dw_p = _wgrad_call([ogt, obt, mt], [0, 1, 2], [dya, dyb, dx2b], [(0, 0), (1, 0), (2, 0)], 0, "wgrad_p")
    g_p = dw_p.reshape(3, N_DEV, D // N_DEV, D).transpose(1, 0, 2, 3).reshape(N_DEV, 3 * (D // N_DEV), D)
    dqk, dgv, drank, dwdec, dbdec = _gla_bwd_call(projf, projb, la, do_gla, st_all, rank, wdec)
    dsq, dsk, dsv, r_p = _sb_bwd_call(projb, do_sb, g_p)
    pieces = [dqk, dgv, dggate, dsq, dsk, dsv, dsgate]
    rhs_of_group = [(g, 0) for g in range(7)] + [(7, 0), (7, 1)]
    dw_in = _wgrad_call([ht], [0] * N_GROUPS, pieces + [dmlog], rhs_of_group, N_GROUPS, "wgrad_in")
    dwr = _wgrad_rank_call(ht, drank)
    g_in = _parts_by_device(dw_in.reshape(N_GROUPS * 1024, D), dwr[:, :GLA_RANK].T.astype(BF16))
    c_idx = lax.axis_index("c").astype(jnp.int32).reshape(1)
    (p_in,) = _pair_exchange([g_in], "pair_g")
    s_in = _pair_add_call(g_in, p_in, c_idx, "pair_add_in")
    grad_x, dnorm_g, r_in = _dh_call(pieces, dmlog, drank, wt, wr, x, dx2, norm_g, s_in)
    small = jnp.concatenate([
        dnorm_g.reshape(-1), dbdec.reshape(-1), dgla_g.reshape(-1), db_gate.reshape(-1), dfinal_g.reshape(-1),
        loss.reshape(-1), dwdec[:GLA_RANK].reshape(-1)]).reshape(1, _SM_LEN)
    return grad_x, r_in, r_p, small


def _parts_by_device(dmain, drank):
    def part_for(p):
        lo, hi = p * SHARD_COLS, (p + 1) * SHARD_COLS
        pieces = []
        if lo < RANK_COL:
            pieces.append(dmain[lo:min(hi, RANK_COL)])
        if lo < RANK_COL + GLA_RANK and hi > RANK_COL:
            pieces.append(drank[max(lo, RANK_COL) - RANK_COL:min(hi, RANK_COL + GLA_RANK) - RANK_COL])
        if hi > RANK_COL + GLA_RANK:
            pieces.append(dmain[max(lo, RANK_COL + GLA_RANK) - GLA_RANK:hi - GLA_RANK])
        return pieces[0] if len(pieces) == 1 else jnp.concatenate(pieces, axis=0)

    return jnp.stack([part_for(p) for p in range(N_DEV)])


_SM_NORM = 0
_SM_BDEC = _SM_NORM + D_MODEL
_SM_GLAG = _SM_BDEC + GLA_DK
_SM_BGATE = _SM_GLAG + GLA_HV
_SM_FINAL = _SM_BGATE + 2 * D_MODEL
_SM_REPL = _SM_FINAL + D_MODEL
_SM_LOSS = _SM_REPL
_SM_WDEC = _SM_LOSS + 128
_SM_LEN = _SM_WDEC + GLA_RANK * GLA_DK


def kernel(x, norm_g, w_in, w_dec_up, b_dec, gla_norm_g, w_pa, w_pb, b_gate, w_o, final_g, loss_target, m_norm_g, m_w_in, m_w_dec_up, m_b_dec, m_gla_norm_g, m_w_pa, m_w_pb, m_b_gate, m_w_o, m_final_g, v_norm_g, v_w_in, v_w_dec_up, v_b_dec, v_gla_norm_g, v_w_pa, v_w_pb, v_b_gate, v_w_o, v_final_g):
    D = D_MODEL
    me = 4 * lax.axis_index("x") + 2 * lax.axis_index("y") + lax.axis_index("c")

    wp_shard = jnp.stack([w_pa, w_pb, w_o]).astype(BF16)
    n_first = (SHARD_COLS // 2) // 16 * 16
    win_all, wdec_all = _all_gather([w_in.T.astype(BF16), w_dec_up], "gather_w",
                                    row_pieces=[[(0, n_first), (n_first, SHARD_COLS - n_first)], None])
    wt = win_all.reshape(IN_COLS, D)
    wr = jnp.pad(wt[RANK_COL:RANK_COL + GLA_RANK], ((0, 128 - GLA_RANK), (0, 0)))
    wdec_full = wdec_all.transpose(1, 0, 2).reshape(GLA_RANK, GLA_DK)
    wdec = jnp.pad(wdec_full, ((0, 128 - GLA_RANK), (0, 0)))

    grad_x, r_in, r_p, small = _local_step(
        x[0], loss_target[0], wt, wr, wdec, b_dec.reshape(1, -1), wp_shard,
        norm_g.reshape(1, -1), gla_norm_g.reshape(1, -1), b_gate.reshape(1, -1), final_g.reshape(1, -1))

    gw_in, d_in, nm_in, nv_in = (a.T for a in _adamw_call(r_in, w_in.T, m_w_in.T, v_w_in.T, "adamw_in"))
    ((g_pa, d_pa, nm_pa, nv_pa), (g_pb, d_pb, nm_pb, nv_pb), (g_o, d_o, nm_o, nv_o)), r_small = _adamw_rows_call(
        r_p, [w_pa, w_pb, w_o], [m_w_pa, m_w_pb, m_w_o], [v_w_pa, v_w_pb, v_w_o], "adamw_p", small)

    def row(a):
        return a.reshape(1, -1)

    rep = _adamw_lanes_call(
        r_small, [_SM_NORM, _SM_BDEC, _SM_GLAG, _SM_BGATE, _SM_FINAL],
        [row(a) for a in (norm_g, b_dec, gla_norm_g, b_gate, final_g)],
        [row(a) for a in (m_norm_g, m_b_dec, m_gla_norm_g, m_b_gate, m_final_g)],
        [row(a) for a in (v_norm_g, v_b_dec, v_gla_norm_g, v_b_gate, v_final_g)], "adamw_rep")
    ((g_norm, d_norm, nm_norm, nv_norm), (g_bdec, d_bdec, nm_bdec, nv_bdec), (g_glag, d_glag, nm_glag, nv_glag),
     (g_bgate, d_bgate, nm_bgate, nv_bgate), (g_final, d_final, nm_final, nv_final)) = [
        tuple(a.reshape(-1) for a in quad) for quad in rep]

    wdec_parts = r_small[:, 0, _SM_WDEC:].reshape(N_DEV, GLA_RANK, GLA_DK)
    cols = GLA_DK // N_DEV
    wdec_mine = lax.dynamic_slice_in_dim(wdec_parts, me * cols, cols, axis=2)
    g_wdec, d_wdec, nm_wdec, nv_wdec = _adamw_call(wdec_mine, w_dec_up, m_w_dec_up, v_w_dec_up, "adamw_dec")

    loss_total = jnp.sum(r_small[:, 0, _SM_LOSS])

    return (loss_total, grad_x[None],
            g_norm, gw_in, g_wdec, g_bdec, g_glag, g_pa, g_pb, g_bgate, g_o, g_final,
            d_norm, d_in, d_wdec, d_bdec, d_glag, d_pa, d_pb, d_bgate, d_o, d_final,
            nm_norm, nm_in, nm_wdec, nm_bdec, nm_glag, nm_pa, nm_pb, nm_bgate, nm_o, nm_final,
            nv_norm, nv_in, nv_wdec, nv_bdec, nv_glag, nv_pa, nv_pb, nv_bgate, nv_o, nv_final)
```

```python
import math

import jax
import jax.numpy as jnp
from jax import lax
from jax.experimental import pallas as pl
from jax.experimental.pallas import tpu as pltpu

F32 = jnp.float32
BF16 = jnp.bfloat16

N_DEV = 8
D_MODEL = 1024
GLA_HEADS = 4
GLA_HK = 128
GLA_HV = 256
GLA_DK = 512
GLA_RANK = 16
GLA_TAU = 16.0
GLA_CHUNK = 64
SB_HEADS = 8
SB_HD = 128
EPS = 1e-6
N_GROUPS = 9
RANK_COL = 3072
IN_COLS = 9232
SHARD_COLS = IN_COLS // N_DEV

ADAM_LR = 0.001
ADAM_B1 = 0.9
ADAM_B2 = 0.999
ADAM_EPS = 1e-08
ADAM_WD = 0.01
ADAM_STEP = 10

VMEM_LIMIT = 56 * 1024 * 1024
TBLK = 256


def _cparams(sem=None):
    return pltpu.CompilerParams(dimension_semantics=sem, vmem_limit_bytes=VMEM_LIMIT)


def _tiling_2d(rows, cols):
    if rows * cols <= 128 * 1024:
        return (rows, cols), (1,), lambda i: (0, 0)
    if rows % 128 == 0:
        return (128, cols), (rows // 128,), lambda i: (i, 0)
    tc = 256 if cols % 256 == 0 else cols
    return (rows, tc), (cols // tc,), lambda i: (0, i)


def _dot(a, b):
    return jnp.dot(a, b, preferred_element_type=F32)


def _dot_nt(a, b):
    return lax.dot_general(a, b, (((1,), (1,)), ((), ())), preferred_element_type=F32)


def _dot_tn(a, b):
    return lax.dot_general(a, b, (((0,), (0,)), ((), ())), preferred_element_type=F32)


def _bf(x):
    return x.astype(BF16)


def _split3(x):
    hi = x.astype(BF16)
    r = x - hi.astype(F32)
    mid = r.astype(BF16)
    lo = (r - mid.astype(F32)).astype(BF16)
    return hi, mid, lo


def _tri_left(tri, x):
    hi, mid, lo = _split3(x)
    return _dot(tri, hi) + _dot(tri, mid) + _dot(tri, lo)


def _split2(x):
    hi = lax.bitcast_convert_type(lax.bitcast_convert_type(x, jnp.uint32) & jnp.uint32(0xFFFF0000), F32)
    return hi.astype(BF16), (x - hi).astype(BF16)


def _tri2_left(tri, x):
    hi, lo = _split2(x)
    return _dot(tri, hi) + _dot(tri, lo)


def _tri2_right(x, tri):
    hi, lo = _split2(x)
    return _dot(hi, tri) + _dot(lo, tri)


def _iota2(n, m, dim):
    return lax.broadcasted_iota(jnp.int32, (n, m), dim)


def _sigmoid(x):
    return 1.0 / (1.0 + jnp.exp(-x))


def _softplus_neg_abs(z):
    return jnp.log(1.0 + jnp.exp(-jnp.abs(z)))


_ANY = pl.BlockSpec(memory_space=pl.ANY)


def _mesh_pos():
    return lax.axis_index("x"), lax.axis_index("y"), lax.axis_index("c")


def _other_chips(x, y):
    return [(1 - x, y), (x, 1 - y), (1 - x, 1 - y)]


def _rcopy(src, dst, send_sem, recv_sem, to):
    return pltpu.make_async_remote_copy(src_ref=src, dst_ref=dst, send_sem=send_sem, recv_sem=recv_sem,
                                        device_id=to, device_id_type=pl.DeviceIdType.MESH)


def _push_copies(src_ref, dst_ref, send_sems, recv_sems, loc_sem, scatter):
    x, y, c = _mesh_pos()
    me = 4 * x + 2 * y + c
    own = pltpu.make_async_copy(src_ref.at[me] if scatter else src_ref, dst_ref.at[me], loc_sem)
    pairs = []
    for k in range(1, N_DEV):
        px = 1 - x if k & 4 else x
        py = 1 - y if k & 2 else y
        pc = 1 - c if k & 1 else c
        pid = 4 * px + 2 * py + pc
        src = src_ref.at[pid] if scatter else src_ref
        send = _rcopy(src, dst_ref.at[me], send_sems.at[k - 1], recv_sems.at[k - 1], (px, py, pc))
        recv = _rcopy(src, dst_ref.at[pid], send_sems.at[k - 1], recv_sems.at[k - 1], (px, py, pc))
        pairs.append((send, recv))
    return own, pairs


def _push_start(own, pairs):
    own.start()
    for send, _ in pairs:
        send.start()


def _push_wait(own, pairs):
    for _, recv in pairs:
        recv.wait_recv()
    for send, _ in pairs:
        send.wait_send()
    own.wait()


_PUSH_SEMS = [pltpu.SemaphoreType.DMA((N_DEV - 1,)), pltpu.SemaphoreType.DMA((N_DEV - 1,)),
              pltpu.SemaphoreType.DMA]


def _chip_copies(src_ref, dst_ref, send_sems, recv_sems, loc_sem):
    x, y, c = _mesh_pos()
    own = pltpu.make_async_copy(src_ref.at[2 * x + y], dst_ref.at[3], loc_sem)
    pairs = []
    for j, (px, py) in enumerate(_other_chips(x, y)):
        cp = _rcopy(src_ref.at[2 * px + py], dst_ref.at[j], send_sems.at[j], recv_sems.at[j], (px, py, c))
        pairs.append((cp, cp))
    return own, pairs


_CHIP_SEMS = [pltpu.SemaphoreType.DMA((3,)), pltpu.SemaphoreType.DMA((3,)), pltpu.SemaphoreType.DMA]


def _all_gather(arrs, name, row_pieces=None):
    n = len(arrs)
    pieces = [[None] if not row_pieces or not row_pieces[a] else list(row_pieces[a]) for a in range(n)]
    n_pc = max(len(p) for p in pieces)
    units = [(a, i) for a in range(n) for i in range(len(pieces[a]))]

    def body(*refs):
        ins = refs[:n]
        outs = refs[n:2 * n]
        send_sems, recv_sems, loc_sems = refs[2 * n:]
        x, y, c = _mesh_pos()
        sib = (x, y, 1 - c)
        chips = _other_chips(x, y)

        def rows(ref, a, i):
            return ref if pieces[a][i] is None else ref.at[pl.ds(*pieces[a][i])]

        def place(a, i, px, py, pc):
            return rows(outs[a].at[4 * px + 2 * py + pc], a, i)

        def copy(u, k, block, to, own=False):
            a, i = u
            dst = place(a, i, *block)
            return _rcopy(rows(ins[a], a, i) if own else dst, dst, send_sems.at[a, k, i], recv_sems.at[a, k, i], to)

        mine = [pltpu.make_async_copy(ins[a], outs[a].at[4 * x + 2 * y + c], loc_sems.at[a]) for a in range(n)]
        for cp in mine:
            cp.start()
        first = [copy(u, 0, (x, y, c), sib, own=True) for u in units]
        for j, chip in enumerate(chips):
            first += [copy(u, 1 + j, (x, y, c), (*chip, c), own=True) for u in units]
        for cp in first:
            cp.start()
        passed = []
        for u in units:
            for j, chip in enumerate(chips):
                copy(u, 1 + j, (*chip, c), (x, y, c)).wait_recv()
                fwd = copy(u, 4 + j, (*chip, c), sib)
                fwd.start()
                passed.append(fwd)
        for u in units:
            copy(u, 0, sib, (x, y, c)).wait_recv()
        for u in units:
            for j, chip in enumerate(chips):
                copy(u, 4 + j, (*chip, 1 - c), (x, y, c)).wait_recv()
        for cp in first + passed:
            cp.wait_send()
        for cp in mine:
            cp.wait()

    return pl.pallas_call(
        body, name=name,
        out_shape=tuple(jax.ShapeDtypeStruct((N_DEV,) + a.shape, a.dtype) for a in arrs),
        in_specs=[_ANY] * n,
        out_specs=tuple([_ANY] * n),
        scratch_shapes=[pltpu.SemaphoreType.DMA((n, 7, n_pc)), pltpu.SemaphoreType.DMA((n, 7, n_pc)),
                        pltpu.SemaphoreType.DMA((n,))],
    )(*arrs)


def _pair_exchange(arrs, name):
    n = len(arrs)

    def body(*refs):
        ins = refs[:n]
        outs = refs[n:2 * n]
        send_sems, recv_sems = refs[2 * n:]
        x, y, c = _mesh_pos()
        copies = []
        for a in range(n):
            for q in range(4):
                cp = _rcopy(ins[a].at[2 * q + (1 - c)], outs[a].at[q], send_sems.at[a, q], recv_sems.at[a, q],
                            (x, y, 1 - c))
                cp.start()
                copies.append(cp)
        for cp in copies:
            cp.wait_recv()
        for cp in copies:
            cp.wait_send()

    return pl.pallas_call(
        body, name=name,
        out_shape=tuple(jax.ShapeDtypeStruct((4,) + a.shape[1:], a.dtype) for a in arrs),
        in_specs=[_ANY] * n,
        out_specs=tuple([_ANY] * n),
        scratch_shapes=[pltpu.SemaphoreType.DMA((n, 4)), pltpu.SemaphoreType.DMA((n, 4))],
    )(*arrs)


def _pair_add_call(parts, recv, c_idx, name):
    _, R, C = parts.shape
    (tr, tc), (steps,), idx = _tiling_2d(R, C)

    def body(c_ref, p_ref, r_ref, o_ref):
        o_ref[...] = (p_ref[...].astype(F32) + r_ref[...].astype(F32)).astype(o_ref.dtype)

    return pl.pallas_call(
        body, name=name,
        grid_spec=pltpu.PrefetchScalarGridSpec(
            num_scalar_prefetch=1,
            grid=(4, steps),
            in_specs=[pl.BlockSpec((None, tr, tc), lambda q, i, c_ref: (2 * q + c_ref[0],) + idx(i)),
                      pl.BlockSpec((None, tr, tc), lambda q, i, c_ref: (q,) + idx(i))],
            out_specs=pl.BlockSpec((None, tr, tc), lambda q, i, c_ref: (q,) + idx(i))),
        out_shape=jax.ShapeDtypeStruct((4, R, C), parts.dtype),
        compiler_params=_cparams(("arbitrary", "arbitrary")),
    )(c_idx, parts, recv)


def _group_row(g):
    return GLA_RANK * (g * (1024 // GLA_RANK) + (g >= RANK_COL // 1024))


def _proj_call(x, norm_g, wt, wr, wp_part):
    T, D = x.shape
    tm = min(1024, T)
    assert tm % TBLK == 0
    n_i = T // tm

    def f_slot(j):
        return ((j >= 2).astype(jnp.int32) + (j >= 6).astype(jnp.int32)
                + (j >= 7).astype(jnp.int32) + (j >= 8).astype(jnp.int32))

    def b_slot(j):
        return (j >= 3).astype(jnp.int32) + (j >= 4).astype(jnp.int32) + (j >= 5).astype(jnp.int32)

    def body(x_ref, g_ref, w_ref, wr_ref, wp_ref, pf_ref, pb_ref, rank_ref, ht_ref, wpall_ref,
             h_scr, send_sems, recv_sems, loc_sem):
        i = pl.program_id(0)
        j = pl.program_id(1)
        own, pairs = _push_copies(wp_ref, wpall_ref, send_sems, recv_sems, loc_sem, scatter=False)

        @pl.when((i == 0) & (j == 0))
        def _():
            _push_start(own, pairs)

        @pl.when(j == 0)
        def _():
            xv = x_ref[...]
            r = lax.rsqrt(jnp.mean(xv * xv, axis=-1, keepdims=True) + EPS)
            h = (xv * r) * g_ref[...]
            hb = _bf(h)
            h_scr[...] = hb
            for b in range(tm // TBLK):
                ht_ref[b] = _bf(h[b * TBLK:(b + 1) * TBLK].T)
            rank_ref[...] = _dot_nt(hb, wr_ref[...])

        is_b = (j == 1) | ((j >= 3) & (j <= 5))

        @pl.when(is_b)
        def _():
            pb_ref[...] = _bf(_dot_nt(h_scr[...], w_ref[...]))

        @pl.when(jnp.logical_not(is_b))
        def _():
            pf_ref[...] = _dot_nt(h_scr[...], w_ref[...])

        @pl.when((i == n_i - 1) & (j == N_GROUPS - 1))
        def _():
            _push_wait(own, pairs)

    return pl.pallas_call(
        body, name="proj",
        grid=(n_i, N_GROUPS),
        in_specs=[pl.BlockSpec((tm, D), lambda i, j: (i, 0)),
                  pl.BlockSpec((1, D), lambda i, j: (0, 0)),
                  pl.BlockSpec((pl.Element(1024), pl.Element(D)), lambda i, j: (_group_row(j), 0)),
                  pl.BlockSpec((128, D), lambda i, j: (0, 0)),
                  _ANY],
        out_specs=(pl.BlockSpec((None, tm, 1024), lambda i, j: (f_slot(j), i, 0)),
                   pl.BlockSpec((None, tm, 1024), lambda i, j: (b_slot(j), i, 0)),
                   pl.BlockSpec((tm, 128), lambda i, j: (i, 0)),
                   pl.BlockSpec((tm // TBLK, D, TBLK), lambda i, j: (i, 0, 0)),
                   _ANY),
        out_shape=(jax.ShapeDtypeStruct((5, T, 1024), F32),
                   jax.ShapeDtypeStruct((4, T, 1024), BF16),
                   jax.ShapeDtypeStruct((T, 128), F32),
                   jax.ShapeDtypeStruct((T // TBLK, D, TBLK), BF16),
                   jax.ShapeDtypeStruct((N_DEV,) + wp_part.shape, wp_part.dtype)),
        scratch_shapes=[pltpu.VMEM((tm, D), BF16)] + _PUSH_SEMS,
        compiler_params=_cparams(("arbitrary", "arbitrary")),
    )(x, norm_g, wt, wr, wp_part)


GLA_STEP_CHUNKS = 4


def _gla_same_chunk(rows):
    return (_iota2(rows, rows, 0) & -GLA_CHUNK) == (_iota2(rows, rows, 1) & -GLA_CHUNK)


def _gla_chunk_terms(la, q, k, n_c):
    C = GLA_CHUNK
    rows = n_c * C
    low = _gla_same_chunk(rows) & (_iota2(rows, rows, 0) >= _iota2(rows, rows, 1))
    b = _tri_left(_bf(low.astype(F32)), la)
    bl = [b[(c + 1) * C - 1:(c + 1) * C, :] for c in range(n_c)]
    bl_rows = jnp.concatenate([jnp.broadcast_to(bl[c], (C, b.shape[1])) for c in range(n_c)], axis=0)
    eb = jnp.exp(b)
    enb = jnp.exp(-b)
    ebl_b = jnp.exp(bl_rows - b)
    scale = GLA_HK ** -0.5
    qe = q * eb * scale
    ke = k * enb
    kd = k * ebl_b
    return bl, eb, enb, ebl_b, qe, ke, kd


def _gla_fwd_call(projf, projb, rank, wdec, bdec):
    T = projf.shape[1]
    C = GLA_CHUNK
    n_chunks = T // C
    n_c = GLA_STEP_CHUNKS
    R = n_c * C
    assert n_chunks % n_c == 0

    def body(qk_ref, v_ref, rank_ref, wd_ref, bd_ref, o_ref, st_ref, la_ref, st_scr):
        @pl.when(pl.program_id(0) == 0)
        def _():
            st_scr[...] = jnp.zeros_like(st_scr)

        dec = _dot(_bf(rank_ref[...]), _bf(wd_ref[...])) + bd_ref[...]
        la = (jnp.minimum(dec, 0.0) - _softplus_neg_abs(dec)) / GLA_TAU
        la_ref[...] = la
        mask = _gla_same_chunk(R) & (_iota2(R, R, 0) >= _iota2(R, R, 1))
        bl, _, _, _, qe, ke, kd = _gla_chunk_terms(la, qk_ref[:, :GLA_DK], qk_ref[:, GLA_DK:], n_c)
        qeb, keb, kdb = _bf(qe), _bf(ke), _bf(kd)
        ebl = [jnp.exp(bl[c]) for c in range(n_c)]
        heads = range(GLA_HEADS)
        ks = [slice(hh * GLA_HK, (hh + 1) * GLA_HK) for hh in heads]
        vs = [slice(hh * GLA_HV, (hh + 1) * GLA_HV) for hh in heads]
        rs = [slice(c * C, (c + 1) * C) for c in range(n_c)]
        p = [_bf(jnp.where(mask, _dot_nt(qeb[:, ks[hh]], keb[:, ks[hh]]), 0.0)) for hh in heads]
        upd = [[_dot_tn(v_ref[rs[c], vs[hh]], kdb[rs[c], ks[hh]]) for hh in heads] for c in range(n_c)]
        intra = [_dot(p[hh], v_ref[:, vs[hh]]) for hh in heads]
        st = [st_scr[hh] for hh in heads]
        for c in range(n_c):
            inter = [_dot_nt(qeb[rs[c], ks[hh]], _bf(st[hh])) for hh in heads]
            for hh in heads:
                st_ref[c, hh] = st[hh]
                o_ref[rs[c], vs[hh]] = intra[hh][rs[c]] + inter[hh]
            st = [st[hh] * ebl[c][:, ks[hh]] + upd[c][hh] for hh in heads]
        for hh in heads:
            st_scr[hh] = st[hh]

    return pl.pallas_call(
        body, name="gla_fwd",
        grid=(n_chunks // n_c,),
        in_specs=[pl.BlockSpec((None, R, 1024), lambda n: (0, n, 0)),
                  pl.BlockSpec((None, R, 1024), lambda n: (0, n, 0)),
                  pl.BlockSpec((R, 128), lambda n: (n, 0)),
                  pl.BlockSpec((128, GLA_DK), lambda n: (0, 0)),
                  pl.BlockSpec((1, GLA_DK), lambda n: (0, 0))],
        out_specs=(pl.BlockSpec((R, 1024), lambda n: (n, 0)),
                   pl.BlockSpec((n_c, GLA_HEADS, GLA_HV, GLA_HK), lambda n: (n, 0, 0, 0)),
                   pl.BlockSpec((R, GLA_DK), lambda n: (n, 0))),
        out_shape=(jax.ShapeDtypeStruct((T, 1024), F32),
                   jax.ShapeDtypeStruct((n_chunks, GLA_HEADS, GLA_HV, GLA_HK), F32),
                   jax.ShapeDtypeStruct((T, GLA_DK), F32)),
        scratch_shapes=[pltpu.VMEM((GLA_HEADS, GLA_HV, GLA_HK), F32)],
        compiler_params=_cparams(("arbitrary",)),
    )(projf, projb, rank, wdec, bdec)


def _gla_bwd_call(projf, projb, la, do_gla, st_all, rank, wdec):
    T = projf.shape[1]
    C = GLA_CHUNK
    n_chunks = T // C
    n_c = GLA_STEP_CHUNKS
    R = n_c * C
    assert n_chunks % n_c == 0
    last = n_chunks // n_c - 1

    def body(qk_ref, v_ref, la_ref, do_ref, st_ref, rank_ref, wd_ref,
             dqk_ref, dv_ref, drank_ref, dwd_ref, dbd_ref, dst_scr):
        @pl.when(pl.program_id(0) == 0)
        def _():
            dst_scr[...] = jnp.zeros_like(dst_scr)
            dwd_ref[...] = jnp.zeros_like(dwd_ref)
            dbd_ref[...] = jnp.zeros_like(dbd_ref)

        same = _gla_same_chunk(R)
        mask = same & (_iota2(R, R, 0) >= _iota2(R, R, 1))
        upp = _bf((same & (_iota2(R, R, 0) <= _iota2(R, R, 1))).astype(F32))
        scale = GLA_HK ** -0.5
        la = la_ref[...]
        bl, eb, enb, ebl_b, qe, ke, kd = _gla_chunk_terms(la, qk_ref[:, :GLA_DK], qk_ref[:, GLA_DK:], n_c)
        qeb, keb, kdb = _bf(qe), _bf(ke), _bf(kd)
        ebl = [jnp.exp(bl[c]) for c in range(n_c)]
        heads = range(GLA_HEADS)
        ks = [slice(hh * GLA_HK, (hh + 1) * GLA_HK) for hh in heads]
        vs = [slice(hh * GLA_HV, (hh + 1) * GLA_HV) for hh in heads]
        rs = [slice(c * C, (c + 1) * C) for c in range(n_c)]
        v = [v_ref[:, vs[hh]] for hh in heads]
        do = [_bf(do_ref[:, vs[hh]]) for hh in heads]
        p = [_bf(jnp.where(mask, _dot_nt(qeb[:, ks[hh]], keb[:, ks[hh]]), 0.0)) for hh in heads]
        dp = [_bf(jnp.where(mask, _dot_nt(do[hh], v[hh]), 0.0)) for hh in heads]
        dst_intra = [[_dot_tn(do[hh][rs[c]], qeb[rs[c], ks[hh]]) for hh in heads] for c in range(n_c)]
        dqe_inter = [[_dot(do[hh][rs[c]], _bf(st_ref[c, hh])) for hh in heads] for c in range(n_c)]
        dv_intra = [_dot_tn(p[hh], do[hh]) for hh in heads]
        dqe_intra = [_dot(dp[hh], keb[:, ks[hh]]) for hh in heads]
        dke = jnp.concatenate([_dot_tn(dp[hh], qeb[:, ks[hh]]) for hh in heads], axis=1)
        dstn = [dst_scr[hh] for hh in heads]
        dkd_c, dv_inter, debl = [None] * n_c, [None] * n_c, [None] * n_c
        for c in reversed(range(n_c)):
            dstnb = [_bf(dstn[hh]) for hh in heads]
            dkd_c[c] = jnp.concatenate([_dot(v[hh][rs[c]], dstnb[hh]) for hh in heads], axis=1)
            dv_inter[c] = [_dot_nt(kdb[rs[c], ks[hh]], dstnb[hh]) for hh in heads]
            debl[c] = jnp.concatenate(
                [jnp.sum(dstn[hh] * st_ref[c, hh], axis=0, keepdims=True) for hh in heads], axis=1)
            dstn = [dst_intra[c][hh] + dstn[hh] * ebl[c][:, ks[hh]] for hh in heads]
        for hh in heads:
            dst_scr[hh] = dstn[hh]
            dv_ref[:, vs[hh]] = _bf(dv_intra[hh] + jnp.concatenate([dv_inter[c][hh] for c in range(n_c)], axis=0))
        dqe = jnp.concatenate(
            [dqe_intra[hh] + jnp.concatenate([dqe_inter[c][hh] for c in range(n_c)], axis=0) for hh in heads], axis=1)
        dkd = jnp.concatenate(dkd_c, axis=0)
        dkd_kd = dkd * kd
        db = dqe * qe - dke * ke - dkd_kd
        dbl = jnp.concatenate(
            [jnp.broadcast_to(jnp.sum(dkd_kd[rs[c]], axis=0, keepdims=True) + ebl[c] * debl[c], (C, GLA_DK))
             for c in range(n_c)], axis=0)
        dla = _tri_left(upp, db) + dbl
        dqk_ref[:, :GLA_DK] = _bf(dqe * eb * scale)
        dqk_ref[:, GLA_DK:] = _bf(dke * enb + dkd * ebl_b)
        ddec = dla * (1.0 / GLA_TAU) * (1.0 - jnp.exp(GLA_TAU * la))
        ddecb = _bf(ddec)
        drank_ref[...] = _bf(_dot_nt(ddecb, _bf(wd_ref[...])))
        dwd_ref[...] += _dot_tn(_bf(rank_ref[...]), ddecb)
        dbd_ref[...] += jnp.sum(ddec, axis=0, keepdims=True)

    return pl.pallas_call(
        body, name="gla_bwd",
        grid=(n_chunks // n_c,),
        in_specs=[pl.BlockSpec((None, R, 1024), lambda n: (0, last - n, 0)),
                  pl.BlockSpec((None, R, 1024), lambda n: (0, last - n, 0)),
                  pl.BlockSpec((R, GLA_DK), lambda n: (last - n, 0)),
                  pl.BlockSpec((R, 1024), lambda n: (last - n, 0)),
                  pl.BlockSpec((n_c, GLA_HEADS, GLA_HV, GLA_HK), lambda n: (last - n, 0, 0, 0)),
                  pl.BlockSpec((R, 128), lambda n: (last - n, 0)),
                  pl.BlockSpec((128, GLA_DK), lambda n: (0, 0))],
        out_specs=(pl.BlockSpec((R, 1024), lambda n: (last - n, 0)),
                   pl.BlockSpec((R, 1024), lambda n: (last - n, 0)),
                   pl.BlockSpec((R, 128), lambda n: (last - n, 0)),
                   pl.BlockSpec((128, GLA_DK), lambda n: (0, 0)),
                   pl.BlockSpec((1, GLA_DK), lambda n: (0, 0))),
        out_shape=(jax.ShapeDtypeStruct((T, 1024), BF16),
                   jax.ShapeDtypeStruct((T, 1024), BF16),
                   jax.ShapeDtypeStruct((T, 128), BF16),
                   jax.ShapeDtypeStruct((128, GLA_DK), F32),
                   jax.ShapeDtypeStruct((1, GLA_DK), F32)),
        scratch_shapes=[pltpu.VMEM((GLA_HEADS, GLA_HV, GLA_HK), F32)],
        compiler_params=_cparams(("arbitrary",)),
    )(projf, projb, la, do_gla, st_all, rank, wdec)


def _sb_logs(z):
    lsz = jnp.minimum(z, 0.0) - _softplus_neg_abs(z)
    return lsz, lsz - z


SB_HG_FWD = 8
SB_HG_BWD = 4
SB_QUERIES = 256
SB_KEYS = 256


def _sb_fwd_call(projb, wp_shard):
    T = projb.shape[1]
    B = min(SB_QUERIES, T)
    HG = SB_HG_FWD
    W = HG * SB_HD
    scale = 1.0 / math.sqrt(SB_HD)
    KB = min(SB_KEYS, T)
    n_h, n_i = SB_HEADS // HG, T // B

    def body(q_ref, k_ref, v_ref, wp_ref, o_ref, wpall_ref, cb_scr, send_sems, recv_sems, loc_sem):
        i = pl.program_id(1)
        own, pairs = _push_copies(wp_ref, wpall_ref, send_sems, recv_sems, loc_sem, scatter=False)

        @pl.when((pl.program_id(0) == 0) & (i == 0))
        def _():
            _push_start(own, pairs)

        rows = HG * B
        after = (_iota2(KB, KB, 0) > _iota2(KB, KB, 1)).astype(F32)
        tri = _bf(jnp.concatenate([after, jnp.ones((KB, KB), F32)], axis=1))
        o_ref[...] = jnp.zeros_like(o_ref)
        cb_scr[...] = jnp.zeros_like(cb_scr)

        def block(jp, masked):
            off = pl.multiple_of(jp * KB, KB)
            z = jnp.concatenate(
                [_dot_nt(q_ref[:, hh * SB_HD:(hh + 1) * SB_HD], k_ref[pl.ds(off, KB), hh * SB_HD:(hh + 1) * SB_HD])
                 for hh in range(HG)], axis=0) * scale
            lsz, l1m = _sb_logs(z)
            if masked:
                strict = (jp * KB + _iota2(rows, KB, 1)) < (i * B + (_iota2(rows, KB, 0) & (B - 1)))
                l1m = jnp.where(strict, l1m, 0.0)
            r = _tri2_right(l1m, tri)
            cb = cb_scr[...]
            a = jnp.exp(lsz + cb + r[:, :KB])
            if masked:
                a = jnp.where(strict, a, 0.0)
            cb_scr[...] = cb + r[:, KB:]
            ab = _bf(a)
            for hh in range(HG):
                cs = slice(hh * SB_HD, (hh + 1) * SB_HD)
                o_ref[:, cs] += _dot(ab[hh * B:(hh + 1) * B, :], v_ref[pl.ds(off, KB), cs])

        jp0 = (i * B) // KB
        block(jp0, True)

        def step(jj, carry):
            block(jp0 - jj, False)
            return carry

        lax.fori_loop(1, jp0 + 1, step, 0)

        @pl.when((pl.program_id(0) == n_h - 1) & (i == n_i - 1))
        def _():
            _push_wait(own, pairs)

    return pl.pallas_call(
        body, name="sb_fwd",
        grid=(n_h, n_i),
        in_specs=[pl.BlockSpec((None, B, W), lambda h, i: (1, i, h)),
                  pl.BlockSpec((None, T, W), lambda h, i: (2, 0, h)),
                  pl.BlockSpec((None, T, W), lambda h, i: (3, 0, h)),
                  _ANY],
        out_specs=(pl.BlockSpec((B, W), lambda h, i: (i, h)), _ANY),
        out_shape=(jax.ShapeDtypeStruct((T, 1024), F32),
                   jax.ShapeDtypeStruct((N_DEV,) + wp_shard.shape, wp_shard.dtype)),
        scratch_shapes=[pltpu.VMEM((HG * B, KB), F32)] + _PUSH_SEMS,
        compiler_params=_cparams(("arbitrary", "arbitrary")),
    )(projb, projb, projb, wp_shard)


def _sb_bwd_call(projb, do_sb, g_p):
    T = projb.shape[1]
    B = min(SB_QUERIES, T)
    nb = T // B
    HG = SB_HG_BWD
    W = HG * SB_HD
    WQ = HG * B
    KB = min(SB_KEYS, T)
    nkb = T // KB
    n_h = SB_HEADS // HG
    scale = 1.0 / math.sqrt(SB_HD)

    def body(q_ref, k_ref, v_ref, do_ref, gp_ref, dq_ref, dk_ref, dv_ref, rp_ref,
             dk_scr, dv_scr, kt_scr, beta_scr, g_scr, dqt_scr, send_sems, recv_sems, loc_sem):
        i = pl.program_id(1)
        own, pairs = _push_copies(gp_ref, rp_ref, send_sems, recv_sems, loc_sem, scatter=True)

        @pl.when((pl.program_id(0) == 0) & (i == 0))
        def _():
            _push_start(own, pairs)

        @pl.when(i == 0)
        def _():
            dk_scr[...] = jnp.zeros_like(dk_scr)
            dv_scr[...] = jnp.zeros_like(dv_scr)
            for hh in range(HG):
                for jb in range(nkb):
                    kt_scr[hh, jb] = _bf(
                        k_ref[jb * KB:(jb + 1) * KB, hh * SB_HD:(hh + 1) * SB_HD].astype(F32).T)

        dqt_scr[...] = jnp.zeros_like(dqt_scr)
        later = _bf((_iota2(KB, KB, 1) > _iota2(KB, KB, 0)).astype(F32))
        earlier = _bf((_iota2(KB, KB, 1) < _iota2(KB, KB, 0)).astype(F32))
        dob = _bf(do_ref[...])
        jp0 = (i * B) // KB

        def strict_mask():
            return (jp0 * KB + _iota2(KB, WQ, 0)) < (i * B + (_iota2(KB, WQ, 1) & (B - 1)))

        def heads(fn):
            return [fn(slice(hh * SB_HD, (hh + 1) * SB_HD)) for hh in range(HG)]

        def pass1(jp, cb, masked):
            off = pl.multiple_of(jp * KB, KB)
            z = jnp.concatenate(heads(lambda cs: _dot_nt(k_ref[pl.ds(off, KB), cs], q_ref[:, cs])), axis=1) * scale
            da = jnp.concatenate(heads(lambda cs: _dot_nt(v_ref[pl.ds(off, KB), cs], dob[:, cs])), axis=1)
            lsz, l1m = _sb_logs(z)
            if masked:
                strict = strict_mask()
                l1m = jnp.where(strict, l1m, 0.0)
            a = jnp.exp(lsz + cb + _tri2_left(later, l1m))
            if masked:
                a = jnp.where(strict, a, 0.0)
            g_scr[jp] = a * da
            beta_scr[jp] = jnp.exp(lsz)
            ab = _bf(a)
            for hh in range(HG):
                cs = slice(hh * SB_HD, (hh + 1) * SB_HD)
                dv_scr[pl.ds(off, KB), cs] += _dot(ab[:, hh * B:(hh + 1) * B], dob[:, cs])
            return cb + jnp.sum(l1m, axis=0, keepdims=True)

        zero = jnp.zeros((1, WQ), F32)
        cb = pass1(jp0, zero, True)

        lax.fori_loop(1, jp0 + 1, lambda jj, cr: pass1(jp0 - jj, cr, False), cb)
        jp_first = 0

        def pass2(jp, cg, masked):
            off = pl.multiple_of(jp * KB, KB)
            g = g_scr[jp]
            beta = beta_scr[jp]
            dz = g * (1.0 - beta) - beta * (cg + _tri2_left(earlier, g))
            if masked:
                dz = jnp.where(strict_mask(), dz, 0.0)
            dzb = _bf(dz * scale)
            for hh in range(HG):
                cs = slice(hh * SB_HD, (hh + 1) * SB_HD)
                dk_scr[pl.ds(off, KB), cs] += _dot(dzb[:, hh * B:(hh + 1) * B], q_ref[:, cs])
                dqt_scr[hh] += _dot(kt_scr[hh, jp], dzb[:, hh * B:(hh + 1) * B])
            return cg + jnp.sum(g, axis=0, keepdims=True)

        cg = lax.fori_loop(jp_first, jp0, lambda jp, cr: pass2(jp, cr, False), zero)
        pass2(jp0, cg, True)
        for hh in range(HG):
            dq_ref[:, hh * SB_HD:(hh + 1) * SB_HD] = _bf(dqt_scr[hh].T)

        @pl.when(i == nb - 1)
        def _():
            dk_ref[...] = _bf(dk_scr[...])
            dv_ref[...] = _bf(dv_scr[...])

        @pl.when((pl.program_id(0) == n_h - 1) & (i == nb - 1))
        def _():
            _push_wait(own, pairs)

    return pl.pallas_call(
        body, name="sb_bwd",
        grid=(n_h, nb),
        in_specs=[pl.BlockSpec((None, B, W), lambda h, i: (1, i, h)),
                  pl.BlockSpec((None, T, W), lambda h, i: (2, 0, h)),
                  pl.BlockSpec((None, T, W), lambda h, i: (3, 0, h)),
                  pl.BlockSpec((B, W), lambda h, i: (i, h)),
                  _ANY],
        out_specs=(pl.BlockSpec((B, W), lambda h, i: (i, h)),
                   pl.BlockSpec((T, W), lambda h, i: (0, h)),
                   pl.BlockSpec((T, W), lambda h, i: (0, h)),
                   _ANY),
        out_shape=(jax.ShapeDtypeStruct((T, 1024), BF16),
                   jax.ShapeDtypeStruct((T, 1024), BF16),
                   jax.ShapeDtypeStruct((T, 1024), BF16),
                   jax.ShapeDtypeStruct(g_p.shape, g_p.dtype)),
        scratch_shapes=[pltpu.VMEM((T, W), F32), pltpu.VMEM((T, W), F32),
                        pltpu.VMEM((HG, nkb, SB_HD, KB), BF16),
                        pltpu.VMEM((nkb, KB, WQ), F32), pltpu.VMEM((nkb, KB, WQ), F32),
                        pltpu.VMEM((HG, SB_HD, B), F32)] + _PUSH_SEMS,
        compiler_params=_cparams(("arbitrary", "arbitrary")),
    )(projb, projb, projb, do_sb, g_p)


def _mid_call(o_gla, o_sb, projf, x, target, wpa, wpb, wo, gla_g, b_gate, final_g):
    T, D = x.shape
    tm = min(TBLK, T)

    def body(og_ref, ggate_ref, osb_ref, sgate_ref, ma_ref, mb_ref, x_ref, tgt_ref,
             wpa_ref, wpb_ref, wo_ref, glag_ref, bg_ref, fg_ref,
             dx2_ref, dogla_ref, dosb_ref, dggate_ref, dsgate_ref, dm_ref,
             mt_ref, ogt_ref, obt_ref, dx2b_ref, dya_ref, dyb_ref,
             dfg_ref, dbg_ref, dglag_ref, loss_ref):
        @pl.when(pl.program_id(0) == 0)
        def _():
            dfg_ref[...] = jnp.zeros_like(dfg_ref)
            dbg_ref[...] = jnp.zeros_like(dbg_ref)
            dglag_ref[...] = jnp.zeros_like(dglag_ref)
            loss_ref[...] = jnp.zeros_like(loss_ref)

        glag = glag_ref[...]
        ggate = ggate_ref[...]
        sg = _sigmoid(ggate)
        silu_g = ggate * sg
        ohat, rinv, nrm = [], [], []
        for hh in range(GLA_HEADS):
            oh = og_ref[:, hh * GLA_HV:(hh + 1) * GLA_HV]
            r = lax.rsqrt(jnp.mean(oh * oh, axis=-1, keepdims=True) + EPS)
            ohat.append(oh * r)
            rinv.append(r)
            nrm.append(ohat[-1] * glag)
        n_all = jnp.concatenate(nrm, axis=1)
        og = n_all * silu_g
        ogb = _bf(og)
        ya = _dot(ogb, wpa_ref[...])
        sgate = sgate_ref[...]
        ss = _sigmoid(sgate)
        silu_s = sgate * ss
        osb = osb_ref[...]
        ob = osb * silu_s
        obb = _bf(ob)
        yb = _dot(obb, wpb_ref[...])
        ga = _sigmoid(ma_ref[...] + bg_ref[:, :D])
        gb = _sigmoid(mb_ref[...] + bg_ref[:, D:])
        merged = ga * ya + gb * yb
        mgb = _bf(merged)
        x2 = x_ref[...] + _dot(mgb, wo_ref[...])
        r2 = lax.rsqrt(jnp.mean(x2 * x2, axis=-1, keepdims=True) + EPS)
        xh2 = x2 * r2
        fg = fg_ref[...]
        err = xh2 * fg - tgt_ref[...]
        loss_ref[...] += jnp.broadcast_to(
            0.5 * jnp.sum(jnp.mean(err * err, axis=-1, keepdims=True), axis=0, keepdims=True), (1, 128))
        dy = err * (1.0 / D)
        dfg_ref[...] += jnp.sum(dy * xh2, axis=0, keepdims=True)
        dxh = dy * fg
        dx2 = r2 * (dxh - xh2 * jnp.mean(dxh * xh2, axis=-1, keepdims=True))
        dx2_ref[...] = dx2
        dx2b = _bf(dx2)
        dx2b_ref[...] = dx2b
        dmerged = _dot_nt(dx2b, wo_ref[...])
        dya = dmerged * ga
        dyb = dmerged * gb
        dma = dmerged * ya * ga * (1.0 - ga)
        dmb = dmerged * yb * gb * (1.0 - gb)
        dm_ref[:, :D] = _bf(dma)
        dm_ref[:, D:] = _bf(dmb)
        dbg_ref[:, :D] += jnp.sum(dma, axis=0, keepdims=True)
        dbg_ref[:, D:] += jnp.sum(dmb, axis=0, keepdims=True)
        dyab = _bf(dya)
        dybb = _bf(dyb)
        dya_ref[...] = dyab
        dyb_ref[...] = dybb
        dog = _dot_nt(dyab, wpa_ref[...])
        dob = _dot_nt(dybb, wpb_ref[...])
        dosb_ref[...] = dob * silu_s
        dsgate_ref[...] = _bf(dob * osb * (ss * (1.0 + sgate * (1.0 - ss))))
        dn = dog * silu_g
        dggate_ref[...] = _bf(dog * n_all * (sg * (1.0 + ggate * (1.0 - sg))))
        dglag = jnp.zeros((1, GLA_HV), F32)
        for hh in range(GLA_HEADS):
            dnh = dn[:, hh * GLA_HV:(hh + 1) * GLA_HV]
            dglag = dglag + jnp.sum(dnh * ohat[hh], axis=0, keepdims=True)
            dohat = dnh * glag
            dogla_ref[:, hh * GLA_HV:(hh + 1) * GLA_HV] = rinv[hh] * (
                dohat - ohat[hh] * jnp.mean(dohat * ohat[hh], axis=-1, keepdims=True))
        dglag_ref[...] += dglag
        mt_ref[...] = _bf(merged.T)
        ogt_ref[...] = _bf(og.T)
        obt_ref[...] = _bf(ob.T)

    row = lambda i: (i, 0)
    const = lambda i: (0, 0)
    tile = pl.BlockSpec((tm, D), row)
    tile_t = pl.BlockSpec((None, D, tm), lambda i: (i, 0, 0))
    wspec = pl.BlockSpec((D, D), const)
    return pl.pallas_call(
        body, name="mid",
        grid=(T // tm,),
        in_specs=[tile,
                  pl.BlockSpec((None, tm, D), lambda i: (1, i, 0)),
                  tile,
                  pl.BlockSpec((None, tm, D), lambda i: (2, i, 0)),
                  pl.BlockSpec((None, tm, D), lambda i: (3, i, 0)),
                  pl.BlockSpec((None, tm, D), lambda i: (4, i, 0)),
                  tile, tile, wspec, wspec, wspec,
                  pl.BlockSpec((1, GLA_HV), const),
                  pl.BlockSpec((1, 2 * D), const),
                  pl.BlockSpec((1, D), const)],
        out_specs=(tile, tile, tile, tile, tile,
                   pl.BlockSpec((tm, 2 * D), row),
                   tile_t, tile_t, tile_t, tile, tile, tile,
                   pl.BlockSpec((1, D), const),
                   pl.BlockSpec((1, 2 * D), const),
                   pl.BlockSpec((1, GLA_HV), const),
                   pl.BlockSpec((1, 128), const)),
        out_shape=(jax.ShapeDtypeStruct((T, D), F32),
                   jax.ShapeDtypeStruct((T, D), F32),
                   jax.ShapeDtypeStruct((T, D), F32),
                   jax.ShapeDtypeStruct((T, D), BF16),
                   jax.ShapeDtypeStruct((T, D), BF16),
                   jax.ShapeDtypeStruct((T, 2 * D), BF16),
                   jax.ShapeDtypeStruct((T // tm, D, tm), BF16),
                   jax.ShapeDtypeStruct((T // tm, D, tm), BF16),
                   jax.ShapeDtypeStruct((T // tm, D, tm), BF16),
                   jax.ShapeDtypeStruct((T, D), BF16),
                   jax.ShapeDtypeStruct((T, D), BF16),
                   jax.ShapeDtypeStruct((T, D), BF16),
                   jax.ShapeDtypeStruct((1, D), F32),
                   jax.ShapeDtypeStruct((1, 2 * D), F32),
                   jax.ShapeDtypeStruct((1, GLA_HV), F32),
                   jax.ShapeDtypeStruct((1, 128), F32)),
        compiler_params=_cparams(("arbitrary",)),
    )(o_gla, projf, o_sb, projf, projf, projf, x, target, wpa, wpb, wo, gla_g, b_gate, final_g)


def _dh_call(pieces, dmlog, drank, wt, wr, x, dx2, norm_g, s_in):
    T, D = x.shape
    tm = min(256, T)
    npc = len(pieces)
    n_main = N_GROUPS * 1024
    n_i = T // tm

    def body(*refs):
        pcs = refs[:npc]
        (dm_ref, dr_ref, w_hbm, wr_ref, x_ref, dx2_ref, g_ref, sin_ref,
         gx_ref, dg_ref, rin_ref, w_scr, sems, send_sems, recv_sems, loc_sem) = refs[npc:]
        own, pairs = _chip_copies(sin_ref, rin_ref, send_sems, recv_sems, loc_sem)

        @pl.when(pl.program_id(0) == 0)
        def _():
            _push_start(own, pairs)
            lo = pltpu.make_async_copy(w_hbm.at[pl.ds(0, RANK_COL)], w_scr.at[pl.ds(0, RANK_COL)], sems.at[0])
            hi = pltpu.make_async_copy(w_hbm.at[pl.ds(RANK_COL + GLA_RANK, n_main - RANK_COL)],
                                       w_scr.at[pl.ds(RANK_COL, n_main - RANK_COL)], sems.at[1])
            lo.start()
            hi.start()
            dg_ref[...] = jnp.zeros_like(dg_ref)
            lo.wait()
            hi.wait()

        def w_group(g):
            return w_scr[g * 1024:(g + 1) * 1024, :]

        dr = dr_ref[...]
        dh = _dot(dr, wr_ref[...])
        for g in range(npc):
            dh = dh + _dot(pcs[g][...], w_group(g))
        dh = dh + _dot(dm_ref[:, :D], w_group(npc))
        dh = dh + _dot(dm_ref[:, D:], w_group(npc + 1))
        xv = x_ref[...]
        r = lax.rsqrt(jnp.mean(xv * xv, axis=-1, keepdims=True) + EPS)
        xhat = xv * r
        g = g_ref[...]
        dg_ref[...] += jnp.sum(dh * xhat, axis=0, keepdims=True)
        dxhat = dh * g
        gx_ref[...] = r * (dxhat - xhat * jnp.mean(dxhat * xhat, axis=-1, keepdims=True)) + dx2_ref[...]

        @pl.when(pl.program_id(0) == n_i - 1)
        def _():
            _push_wait(own, pairs)

    row = lambda i: (i, 0)
    const = lambda i: (0, 0)
    tile = pl.BlockSpec((tm, D), row)
    return pl.pallas_call(
        body, name="dh",
        grid=(n_i,),
        in_specs=[tile] * npc + [
            pl.BlockSpec((tm, 2 * D), row),
            pl.BlockSpec((tm, 128), row),
            _ANY,
            pl.BlockSpec((128, D), const),
            tile, tile,
            pl.BlockSpec((1, D), const),
            _ANY],
        out_specs=(tile, pl.BlockSpec((1, D), const), _ANY),
        out_shape=(jax.ShapeDtypeStruct((T, D), F32),
                   jax.ShapeDtypeStruct((1, D), F32),
                   jax.ShapeDtypeStruct(s_in.shape, s_in.dtype)),
        scratch_shapes=[pltpu.VMEM((n_main, D), BF16), pltpu.SemaphoreType.DMA((2,))] + _CHIP_SEMS,
        compiler_params=_cparams(("arbitrary",)),
    )(*pieces, dmlog, drank, wt, wr, x, dx2, norm_g, s_in)


def _wgrad_rank_call(ht, drank):
    n_tb, D, tb = ht.shape

    def body(ht_ref, dr_ref, o_ref):
        @pl.when(pl.program_id(0) == 0)
        def _():
            o_ref[...] = jnp.zeros_like(o_ref)

        o_ref[...] += _dot(ht_ref[...], dr_ref[...])

    return pl.pallas_call(
        body, name="wgrad_rank",
        grid=(n_tb,),
        in_specs=[pl.BlockSpec((None, D, tb), lambda i: (i, 0, 0)),
                  pl.BlockSpec((tb, 128), lambda i: (i, 0))],
        out_specs=pl.BlockSpec((D, 128), lambda i: (0, 0)),
        out_shape=jax.ShapeDtypeStruct((D, 128), F32),
        compiler_params=_cparams(("arbitrary",)),
    )(ht, drank)


def _wgrad_call(lhs_list, lhs_of_group, rhs_list, rhs_of_group, n_transposed, name):
    n_groups = len(rhs_of_group)
    n_tb, D, tb = lhs_list[0].shape
    T = n_tb * tb
    per = min(2, n_tb)
    tk = per * tb
    nk = T // tk
    nl = len(lhs_list)

    def body(*refs):
        lhs = refs[:nl]
        rhs = refs[nl:nl + n_groups]
        out_ref, acc = refs[nl + n_groups:]
        g = pl.program_id(0)
        i = pl.program_id(1)

        @pl.when(i == 0)
        def _():
            acc[...] = jnp.zeros_like(acc)

        for p in range(n_groups):
            @pl.when(g == p)
            def _(p=p):
                lref = lhs[lhs_of_group[p]]
                part = _dot(lref[0], rhs[p][0:tb, :])
                for b in range(1, per):
                    part = part + _dot(lref[b], rhs[p][b * tb:(b + 1) * tb, :])
                acc[...] += part

        @pl.when((i == nk - 1) & (g < n_transposed))
        def _():
            out_ref[...] = _bf(acc[...].T)

        @pl.when((i == nk - 1) & (g >= n_transposed))
        def _():
            out_ref[...] = _bf(acc[...])

    def lhs_spec(a):
        groups = [g for g in range(n_groups) if lhs_of_group[g] == a]
        lo, hi = min(groups), max(groups)
        assert groups == list(range(lo, hi + 1))
        return pl.BlockSpec((per, D, tb), lambda g, i: (jnp.where((g >= lo) & (g <= hi), i, 0), 0, 0))

    def rhs_spec(p):
        cb = rhs_of_group[p][1]
        return pl.BlockSpec((tk, 1024), lambda g, i: (jnp.where(g == p, i, 0), cb))

    return pl.pallas_call(
        body, name=name,
        grid=(n_groups, nk),
        in_specs=[lhs_spec(a) for a in range(nl)] + [rhs_spec(p) for p in range(n_groups)],
        out_specs=pl.BlockSpec((None, D, 1024), lambda g, i: (g, 0, 0)),
        out_shape=jax.ShapeDtypeStruct((n_groups, D, 1024), BF16),
        scratch_shapes=[pltpu.VMEM((D, 1024), F32)],
        compiler_params=_cparams(("arbitrary", "arbitrary")),
    )(*lhs_list, *[rhs_list[rhs_of_group[p][0]] for p in range(n_groups)])


def _adamw_math(parts, w, m, v):
    g = parts[0].astype(F32)
    for p in parts[1:]:
        g = g + p.astype(F32)
    mm = ADAM_B1 * m + (1.0 - ADAM_B1) * g
    vv = ADAM_B2 * v + (1.0 - ADAM_B2) * (g * g)
    m_hat = mm / (1.0 - ADAM_B1 ** ADAM_STEP)
    v_hat = vv / (1.0 - ADAM_B2 ** ADAM_STEP)
    return g, -ADAM_LR * (m_hat / (jnp.sqrt(v_hat) + ADAM_EPS) + ADAM_WD * w), mm, vv


def _part_order(n_parts):
    return [n_parts - 1] + list(range(n_parts - 1))


def _adamw_call(parts, w, m, v, name):
    R, C = w.shape
    n_parts = parts.shape[0]
    (tr, tc), grid, idx = _tiling_2d(R, C)

    def body(p_ref, w_ref, m_ref, v_ref, g_ref, d_ref, nm_ref, nv_ref):
        g_ref[...], d_ref[...], nm_ref[...], nv_ref[...] = _adamw_math(
            [p_ref[k] for k in _part_order(n_parts)], w_ref[...], m_ref[...], v_ref[...])

    blk = pl.BlockSpec((tr, tc), idx)
    sds = jax.ShapeDtypeStruct((R, C), F32)
    return pl.pallas_call(
        body, name=name,
        grid=grid,
        in_specs=[pl.BlockSpec((n_parts, tr, tc), lambda i: (0,) + idx(i)), blk, blk, blk],
        out_specs=(blk, blk, blk, blk),
        out_shape=(sds, sds, sds, sds),
        compiler_params=_cparams(("arbitrary",)),
    )(parts, w, m, v)


def _adamw_rows_call(parts, ws, ms, vs, name, gathered):
    n = len(ws)
    R, C = ws[0].shape
    n_parts = parts.shape[0]

    def body(*refs):
        p_ref = refs[0]
        w_refs, m_refs, v_refs = refs[1:1 + n], refs[1 + n:1 + 2 * n], refs[1 + 2 * n:1 + 3 * n]
        src_ref = refs[1 + 3 * n]
        outs = refs[2 + 3 * n:2 + 7 * n]
        dst_ref, send_sems, recv_sems, loc_sem = refs[2 + 7 * n:]
        own, pairs = _push_copies(src_ref, dst_ref, send_sems, recv_sems, loc_sem, scatter=False)
        k_now = pl.program_id(0)

        @pl.when(k_now == 0)
        def _():
            _push_start(own, pairs)

        for k in range(n):
            @pl.when(k_now == k)
            def _(k=k):
                res = _adamw_math([p_ref[j] for j in _part_order(n_parts)],
                                  w_refs[k][...], m_refs[k][...], v_refs[k][...])
                for o_ref, val in zip(outs[4 * k:4 * k + 4], res):
                    o_ref[...] = val

        @pl.when(k_now == n - 1)
        def _():
            _push_wait(own, pairs)

    whole = pl.BlockSpec((R, C), lambda k: (0, 0))
    sds = jax.ShapeDtypeStruct((R, C), F32)
    res = pl.pallas_call(
        body, name=name,
        grid=(n,),
        in_specs=[pl.BlockSpec((n_parts, R, C), lambda k: (0, k, 0))] + [whole] * (3 * n) + [_ANY],
        out_specs=tuple([whole] * (4 * n) + [_ANY]),
        out_shape=tuple([sds] * (4 * n) + [jax.ShapeDtypeStruct((N_DEV,) + gathered.shape, gathered.dtype)]),
        scratch_shapes=_PUSH_SEMS,
        compiler_params=_cparams(("arbitrary",)),
    )(parts, *ws, *ms, *vs, gathered)
    return [res[4 * k:4 * k + 4] for k in range(n)], res[4 * n]


def _adamw_lanes_call(parts, offsets, ws, ms, vs, name):
    n = len(ws)
    n_parts = parts.shape[0]

    def body(*refs):
        p_ref = refs[0]
        w_refs, m_refs, v_refs = refs[1:1 + n], refs[1 + n:1 + 2 * n], refs[1 + 2 * n:1 + 3 * n]
        outs = refs[1 + 3 * n:]
        for k in range(n):
            lanes = slice(offsets[k], offsets[k] + ws[k].shape[1])
            res = _adamw_math([p_ref[j, :, lanes] for j in _part_order(n_parts)],
                              w_refs[k][...], m_refs[k][...], v_refs[k][...])
            for o_ref, val in zip(outs[4 * k:4 * k + 4], res):
                o_ref[...] = val

    res = pl.pallas_call(
        body, name=name,
        out_shape=tuple(jax.ShapeDtypeStruct(ws[k].shape, F32) for k in range(n) for _ in range(4)),
        compiler_params=_cparams(),
    )(parts, *ws, *ms, *vs)
    return [res[4 * k:4 * k + 4] for k in range(n)]


def _local_step(x, target, wt, wr, wdec, bdec, wp_shard, norm_g, gla_g, b_gate, final_g):
    D = x.shape[1]
    half = wp_shard.shape[1] // 2
    projf, projb, rank, ht, wp_lo = _proj_call(x, norm_g, wt, wr, wp_shard[:, :half])
    o_gla, st_all, la = _gla_fwd_call(projf, projb, rank, wdec, bdec)
    o_sb, wp_hi = _sb_fwd_call(projb, wp_shard[:, half:])
    wp_full = jnp.concatenate([wp_lo, wp_hi], axis=2).transpose(1, 0, 2, 3).reshape(3, D, D)
    (dx2, do_gla, do_sb, dggate, dsgate, dmlog, mt, ogt, obt, dx2b, dya, dyb,
     dfinal_g, db_gate, dgla_g, loss) = _mid_call(o_gla, o_sb, projf, x, target, wp_full[0], wp_full[1],
                                                 wp_full[2], gla_g, b_gate, final_g)
    dw_p = _wgrad_call([ogt, obt, mt], [0, 1, 2], [dya, dyb, dx2b], [(0, 0), (1, 0), (2, 0)], 0, "wgrad_p")
    g_p = dw_p.reshape(3, N_DEV, D // N_DEV, D).transpose(1, 0, 2, 3).reshape(N_DEV, 3 * (D // N_DEV), D)
    dqk, dgv, drank, dwdec, dbdec = _gla_bwd_call(projf, projb, la, do_gla, st_all, rank, wdec)
    dsq, dsk, dsv, r_p = _sb_bwd_call(projb, do_sb, g_p)
    pieces = [dqk, dgv, dggate, dsq, dsk, dsv, dsgate]
    rhs_of_group = [(g, 0) for g in range(7)] + [(7, 0), (7, 1)]
    dw_in = _wgrad_call([ht], [0] * N_GROUPS, pieces + [dmlog], rhs_of_group, N_GROUPS, "wgrad_in")
    dwr = _wgrad_rank_call(ht, drank)
    g_in = _parts_by_device(dw_in.reshape(N_GROUPS * 1024, D), dwr[:, :GLA_RANK].T.astype(BF16))
    c_idx = lax.axis_index("c").astype(jnp.int32).reshape(1)
    (p_in,) = _pair_exchange([g_in], "pair_g")
    s_in = _pair_add_call(g_in, p_in, c_idx, "pair_add_in")
    grad_x, dnorm_g, r_in = _dh_call(pieces, dmlog, drank, wt, wr, x, dx2, norm_g, s_in)
    small = jnp.concatenate([
        dnorm_g.reshape(-1), dbdec.reshape(-1), dgla_g.reshape(-1), db_gate.reshape(-1), dfinal_g.reshape(-1),
        loss.reshape(-1), dwdec[:GLA_RANK].reshape(-1)]).reshape(1, _SM_LEN)
    return grad_x, r_in, r_p, small


def _parts_by_device(dmain, drank):
    def part_for(p):
        lo, hi = p * SHARD_COLS, (p + 1) * SHARD_COLS
        pieces = []
        if lo < RANK_COL:
            pieces.append(dmain[lo:min(hi, RANK_COL)])
        if lo < RANK_COL + GLA_RANK and hi > RANK_COL:
            pieces.append(drank[max(lo, RANK_COL) - RANK_COL:min(hi, RANK_COL + GLA_RANK) - RANK_COL])
        if hi > RANK_COL + GLA_RANK:
            pieces.append(dmain[max(lo, RANK_COL + GLA_RANK) - GLA_RANK:hi - GLA_RANK])
        return pieces[0] if len(pieces) == 1 else jnp.concatenate(pieces, axis=0)

    return jnp.stack([part_for(p) for p in range(N_DEV)])


_SM_NORM = 0
_SM_BDEC = _SM_NORM + D_MODEL
_SM_GLAG = _SM_BDEC + GLA_DK
_SM_BGATE = _SM_GLAG + GLA_HV
_SM_FINAL = _SM_BGATE + 2 * D_MODEL
_SM_REPL = _SM_FINAL + D_MODEL
_SM_LOSS = _SM_REPL
_SM_WDEC = _SM_LOSS + 128
_SM_LEN = _SM_WDEC + GLA_RANK * GLA_DK


def kernel(x, norm_g, w_in, w_dec_up, b_dec, gla_norm_g, w_pa, w_pb, b_gate, w_o, final_g, loss_target, m_norm_g, m_w_in, m_w_dec_up, m_b_dec, m_gla_norm_g, m_w_pa, m_w_pb, m_b_gate, m_w_o, m_final_g, v_norm_g, v_w_in, v_w_dec_up, v_b_dec, v_gla_norm_g, v_w_pa, v_w_pb, v_b_gate, v_w_o, v_final_g):
    D = D_MODEL
    me = 4 * lax.axis_index("x") + 2 * lax.axis_index("y") + lax.axis_index("c")

    wp_shard = jnp.stack([w_pa, w_pb, w_o]).astype(BF16)
    n_first = (SHARD_COLS // 2) // 16 * 16
    win_all, wdec_all = _all_gather([w_in.T.astype(BF16), w_dec_up], "gather_w",
                                    row_pieces=[[(0, n_first), (n_first, SHARD_COLS - n_first)], None])
    wt = win_all.reshape(IN_COLS, D)
    wr = jnp.pad(wt[RANK_COL:RANK_COL + GLA_RANK], ((0, 128 - GLA_RANK), (0, 0)))
    wdec_full = wdec_all.transpose(1, 0, 2).reshape(GLA_RANK, GLA_DK)
    wdec = jnp.pad(wdec_full, ((0, 128 - GLA_RANK), (0, 0)))

    grad_x, r_in, r_p, small = _local_step(
        x[0], loss_target[0], wt, wr, wdec, b_dec.reshape(1, -1), wp_shard,
        norm_g.reshape(1, -1), gla_norm_g.reshape(1, -1), b_gate.reshape(1, -1), final_g.reshape(1, -1))

    gw_in, d_in, nm_in, nv_in = (a.T for a in _adamw_call(r_in, w_in.T, m_w_in.T, v_w_in.T, "adamw_in"))
    ((g_pa, d_pa, nm_pa, nv_pa), (g_pb, d_pb, nm_pb, nv_pb), (g_o, d_o, nm_o, nv_o)), r_small = _adamw_rows_call(
        r_p, [w_pa, w_pb, w_o], [m_w_pa, m_w_pb, m_w_o], [v_w_pa, v_w_pb, v_w_o], "adamw_p", small)

    def row(a):
        return a.reshape(1, -1)

    rep = _adamw_lanes_call(
        r_small, [_SM_NORM, _SM_BDEC, _SM_GLAG, _SM_BGATE, _SM_FINAL],
        [row(a) for a in (norm_g, b_dec, gla_norm_g, b_gate, final_g)],
        [row(a) for a in (m_norm_g, m_b_dec, m_gla_norm_g, m_b_gate, m_final_g)],
        [row(a) for a in (v_norm_g, v_b_dec, v_gla_norm_g, v_b_gate, v_final_g)], "adamw_rep")
    ((g_norm, d_norm, nm_norm, nv_norm), (g_bdec, d_bdec, nm_bdec, nv_bdec), (g_glag, d_glag, nm_glag, nv_glag),
     (g_bgate, d_bgate, nm_bgate, nv_bgate), (g_final, d_final, nm_final, nv_final)) = [
        tuple(a.reshape(-1) for a in quad) for quad in rep]

    wdec_parts = r_small[:, 0, _SM_WDEC:].reshape(N_DEV, GLA_RANK, GLA_DK)
    cols = GLA_DK // N_DEV
    wdec_mine = lax.dynamic_slice_in_dim(wdec_parts, me * cols, cols, axis=2)
    g_wdec, d_wdec, nm_wdec, nv_wdec = _adamw_call(wdec_mine, w_dec_up, m_w_dec_up, v_w_dec_up, "adamw_dec")

    loss_total = jnp.sum(r_small[:, 0, _SM_LOSS])

    return (loss_total, grad_x[None],
            g_norm, gw_in, g_wdec, g_bdec, g_glag, g_pa, g_pb, g_bgate, g_o, g_final,
            d_norm, d_in, d_wdec, d_bdec, d_glag, d_pa, d_pb, d_bgate, d_o, d_final,
            nm_norm, nm_in, nm_wdec, nm_bdec, nm_glag, nm_pa, nm_pb, nm_bgate, nm_o, nm_final,
            nv_norm, nv_in, nv_wdec, nv_bdec, nv_glag, nv_pa, nv_pb, nv_bgate, nv_o, nv_final)
```

```python
import math

import jax
import jax.numpy as jnp
from jax import lax
from jax.experimental import pallas as pl
from jax.experimental.pallas import tpu as pltpu

F32 = jnp.float32
BF16 = jnp.bfloat16

N_DEV = 8
D_MODEL = 1024
GLA_HEADS = 4
GLA_HK = 128
GLA_HV = 256
GLA_DK = 512
GLA_RANK = 16
GLA_TAU = 16.0
GLA_CHUNK = 64
SB_HEADS = 8
SB_HD = 128
EPS = 1e-6
N_GROUPS = 9
RANK_COL = 3072
IN_COLS = 9232
SHARD_COLS = IN_COLS // N_DEV

ADAM_LR = 0.001
ADAM_B1 = 0.9
ADAM_B2 = 0.999
ADAM_EPS = 1e-08
ADAM_WD = 0.01
ADAM_STEP = 10

VMEM_LIMIT = 56 * 1024 * 1024
TBLK = 256


def _cparams(sem=None):
    return pltpu.CompilerParams(dimension_semantics=sem, vmem_limit_bytes=VMEM_LIMIT)


def _tiling_2d(rows, cols, band_cols):
    if rows * cols <= 128 * 1024:
        return (rows, cols), (1,), lambda i: (0, 0)
    if rows % 128 == 0:
        return (128, cols), (rows // 128,), lambda i: (i, 0)
    tc = band_cols if cols % band_cols == 0 else cols
    return (rows, tc), (cols // tc,), lambda i: (0, i)


def _dot(a, b):
    return jnp.dot(a, b, preferred_element_type=F32)


def _dot_nt(a, b):
    return lax.dot_general(a, b, (((1,), (1,)), ((), ())), preferred_element_type=F32)


def _dot_tn(a, b):
    return lax.dot_general(a, b, (((0,), (0,)), ((), ())), preferred_element_type=F32)


def _bf(x):
    return x.astype(BF16)


def _split3(x):
    hi = x.astype(BF16)
    r = x - hi.astype(F32)
    mid = r.astype(BF16)
    lo = (r - mid.astype(F32)).astype(BF16)
    return hi, mid, lo


def _tri_left(tri, x):
    hi, mid, lo = _split3(x)
    return _dot(tri, hi) + _dot(tri, mid) + _dot(tri, lo)


def _split2(x):
    hi = lax.bitcast_convert_type(lax.bitcast_convert_type(x, jnp.uint32) & jnp.uint32(0xFFFF0000), F32)
    return hi.astype(BF16), (x - hi).astype(BF16)


def _tri2_left(tri, x):
    hi, lo = _split2(x)
    return _dot(tri, hi) + _dot(tri, lo)


def _tri2_right(x, tri):
    hi, lo = _split2(x)
    return _dot(hi, tri) + _dot(lo, tri)


def _iota2(n, m, dim):
    return lax.broadcasted_iota(jnp.int32, (n, m), dim)


def _sigmoid(x):
    return 1.0 / (1.0 + jnp.exp(-x))


def _softplus_neg_abs(z):
    return jnp.log(1.0 + jnp.exp(-jnp.abs(z)))


_ANY = pl.BlockSpec(memory_space=pl.ANY)


def _mesh_pos():
    return lax.axis_index("x"), lax.axis_index("y"), lax.axis_index("c")


def _other_chips(x, y):
    return [(1 - x, y), (x, 1 - y), (1 - x, 1 - y)]


def _rcopy(src, dst, send_sem, recv_sem, to):
    return pltpu.make_async_remote_copy(src_ref=src, dst_ref=dst, send_sem=send_sem, recv_sem=recv_sem,
                                        device_id=to, device_id_type=pl.DeviceIdType.MESH)


def _push_copies(src_ref, dst_ref, send_sems, recv_sems, loc_sem, scatter):
    x, y, c = _mesh_pos()
    me = 4 * x + 2 * y + c
    own = pltpu.make_async_copy(src_ref.at[me] if scatter else src_ref, dst_ref.at[me], loc_sem)
    pairs = []
    for k in range(1, N_DEV):
        px = 1 - x if k & 4 else x
        py = 1 - y if k & 2 else y
        pc = 1 - c if k & 1 else c
        pid = 4 * px + 2 * py + pc
        src = src_ref.at[pid] if scatter else src_ref
        send = _rcopy(src, dst_ref.at[me], send_sems.at[k - 1], recv_sems.at[k - 1], (px, py, pc))
        recv = _rcopy(src, dst_ref.at[pid], send_sems.at[k - 1], recv_sems.at[k - 1], (px, py, pc))
        pairs.append((send, recv))
    return own, pairs


def _push_start(own, pairs):
    own.start()
    for send, _ in pairs:
        send.start()


def _push_wait(own, pairs):
    for _, recv in pairs:
        recv.wait_recv()
    for send, _ in pairs:
        send.wait_send()
    own.wait()


_PUSH_SEMS = [pltpu.SemaphoreType.DMA((N_DEV - 1,)), pltpu.SemaphoreType.DMA((N_DEV - 1,)),
              pltpu.SemaphoreType.DMA]


def _chip_copies(src_ref, dst_ref, send_sems, recv_sems, loc_sem):
    x, y, c = _mesh_pos()
    own = pltpu.make_async_copy(src_ref.at[2 * x + y], dst_ref.at[3], loc_sem)
    pairs = []
    for j, (px, py) in enumerate(_other_chips(x, y)):
        cp = _rcopy(src_ref.at[2 * px + py], dst_ref.at[j], send_sems.at[j], recv_sems.at[j], (px, py, c))
        pairs.append((cp, cp))
    return own, pairs


_CHIP_SEMS = [pltpu.SemaphoreType.DMA((3,)), pltpu.SemaphoreType.DMA((3,)), pltpu.SemaphoreType.DMA]


def _all_gather(arrs, name, row_pieces=None):
    n = len(arrs)
    pieces = [[None] if not row_pieces or not row_pieces[a] else list(row_pieces[a]) for a in range(n)]
    n_pc = max(len(p) for p in pieces)
    units = [(a, i) for a in range(n) for i in range(len(pieces[a]))]

    def body(*refs):
        ins = refs[:n]
        outs = refs[n:2 * n]
        send_sems, recv_sems, loc_sems = refs[2 * n:]
        x, y, c = _mesh_pos()
        sib = (x, y, 1 - c)
        chips = _other_chips(x, y)

        def rows(ref, a, i):
            return ref if pieces[a][i] is None else ref.at[pl.ds(*pieces[a][i])]

        def place(a, i, px, py, pc):
            return rows(outs[a].at[4 * px + 2 * py + pc], a, i)

        def copy(u, k, block, to, own=False):
            a, i = u
            dst = place(a, i, *block)
            return _rcopy(rows(ins[a], a, i) if own else dst, dst, send_sems.at[a, k, i], recv_sems.at[a, k, i], to)

        mine = [pltpu.make_async_copy(ins[a], outs[a].at[4 * x + 2 * y + c], loc_sems.at[a]) for a in range(n)]
        for cp in mine:
            cp.start()
        first = [copy(u, 0, (x, y, c), sib, own=True) for u in units]
        for j, chip in enumerate(chips):
            first += [copy(u, 1 + j, (x, y, c), (*chip, c), own=True) for u in units]
        for cp in first:
            cp.start()
        passed = []
        for u in units:
            for j, chip in enumerate(chips):
                copy(u, 1 + j, (*chip, c), (x, y, c)).wait_recv()
                fwd = copy(u, 4 + j, (*chip, c), sib)
                fwd.start()
                passed.append(fwd)
        for u in units:
            copy(u, 0, sib, (x, y, c)).wait_recv()
        for u in units:
            for j, chip in enumerate(chips):
                copy(u, 4 + j, (*chip, 1 - c), (x, y, c)).wait_recv()
        for cp in first + passed:
            cp.wait_send()
        for cp in mine:
            cp.wait()

    return pl.pallas_call(
        body, name=name,
        out_shape=tuple(jax.ShapeDtypeStruct((N_DEV,) + a.shape, a.dtype) for a in arrs),
        in_specs=[_ANY] * n,
        out_specs=tuple([_ANY] * n),
        scratch_shapes=[pltpu.SemaphoreType.DMA((n, 7, n_pc)), pltpu.SemaphoreType.DMA((n, 7, n_pc)),
                        pltpu.SemaphoreType.DMA((n,))],
    )(*arrs)


def _pair_exchange(arrs, name):
    n = len(arrs)

    def body(*refs):
        ins = refs[:n]
        outs = refs[n:2 * n]
        send_sems, recv_sems = refs[2 * n:]
        x, y, c = _mesh_pos()
        copies = []
        for a in range(n):
            for q in range(4):
                cp = _rcopy(ins[a].at[2 * q + (1 - c)], outs[a].at[q], send_sems.at[a, q], recv_sems.at[a, q],
                            (x, y, 1 - c))
                cp.start()
                copies.append(cp)
        for cp in copies:
            cp.wait_recv()
        for cp in copies:
            cp.wait_send()

    return pl.pallas_call(
        body, name=name,
        out_shape=tuple(jax.ShapeDtypeStruct((4,) + a.shape[1:], a.dtype) for a in arrs),
        in_specs=[_ANY] * n,
        out_specs=tuple([_ANY] * n),
        scratch_shapes=[pltpu.SemaphoreType.DMA((n, 4)), pltpu.SemaphoreType.DMA((n, 4))],
    )(*arrs)


def _pair_add_call(parts, recv, c_idx, name):
    _, R, C = parts.shape
    (tr, tc), (steps,), idx = _tiling_2d(R, C, 1024)

    def body(c_ref, p_ref, r_ref, o_ref):
        o_ref[...] = (p_ref[...].astype(F32) + r_ref[...].astype(F32)).astype(o_ref.dtype)

    return pl.pallas_call(
        body, name=name,
        grid_spec=pltpu.PrefetchScalarGridSpec(
            num_scalar_prefetch=1,
            grid=(4, steps),
            in_specs=[pl.BlockSpec((None, tr, tc), lambda q, i, c_ref: (2 * q + c_ref[0],) + idx(i)),
                      pl.BlockSpec((None, tr, tc), lambda q, i, c_ref: (q,) + idx(i))],
            out_specs=pl.BlockSpec((None, tr, tc), lambda q, i, c_ref: (q,) + idx(i))),
        out_shape=jax.ShapeDtypeStruct((4, R, C), parts.dtype),
        compiler_params=_cparams(("arbitrary", "arbitrary")),
    )(c_idx, parts, recv)


def _group_row(g):
    return GLA_RANK * (g * (1024 // GLA_RANK) + (g >= RANK_COL // 1024))


def _proj_call(x, norm_g, wt, wr, wp_part):
    T, D = x.shape
    tm = min(1024, T)
    assert tm % TBLK == 0
    n_i = T // tm

    def f_slot(j):
        return ((j >= 2).astype(jnp.int32) + (j >= 6).astype(jnp.int32)
                + (j >= 7).astype(jnp.int32) + (j >= 8).astype(jnp.int32))

    def b_slot(j):
        return (j >= 3).astype(jnp.int32) + (j >= 4).astype(jnp.int32) + (j >= 5).astype(jnp.int32)

    def body(x_ref, g_ref, w_ref, wr_ref, wp_ref, pf_ref, pb_ref, rank_ref, ht_ref, wpall_ref,
             h_scr, send_sems, recv_sems, loc_sem):
        i = pl.program_id(0)
        j = pl.program_id(1)
        own, pairs = _push_copies(wp_ref, wpall_ref, send_sems, recv_sems, loc_sem, scatter=False)

        @pl.when((i == 0) & (j == 0))
        def _():
            _push_start(own, pairs)

        @pl.when(j == 0)
        def _():
            xv = x_ref[...]
            r = lax.rsqrt(jnp.mean(xv * xv, axis=-1, keepdims=True) + EPS)
            h = (xv * r) * g_ref[...]
            hb = _bf(h)
            h_scr[...] = hb
            for b in range(tm // TBLK):
                ht_ref[b] = _bf(h[b * TBLK:(b + 1) * TBLK].T)
            rank_ref[...] = _dot_nt(hb, wr_ref[...])

        is_b = (j == 1) | ((j >= 3) & (j <= 5))

        @pl.when(is_b)
        def _():
            pb_ref[...] = _bf(_dot_nt(h_scr[...], w_ref[...]))

        @pl.when(jnp.logical_not(is_b))
        def _():
            pf_ref[...] = _dot_nt(h_scr[...], w_ref[...])

        @pl.when((i == n_i - 1) & (j == N_GROUPS - 1))
        def _():
            _push_wait(own, pairs)

    return pl.pallas_call(
        body, name="proj",
        grid=(n_i, N_GROUPS),
        in_specs=[pl.BlockSpec((tm, D), lambda i, j: (i, 0)),
                  pl.BlockSpec((1, D), lambda i, j: (0, 0)),
                  pl.BlockSpec((pl.Element(1024), pl.Element(D)), lambda i, j: (_group_row(j), 0)),
                  pl.BlockSpec((128, D), lambda i, j: (0, 0)),
                  _ANY],
        out_specs=(pl.BlockSpec((None, tm, 1024), lambda i, j: (f_slot(j), i, 0)),
                   pl.BlockSpec((None, tm, 1024), lambda i, j: (b_slot(j), i, 0)),
                   pl.BlockSpec((tm, 128), lambda i, j: (i, 0)),
                   pl.BlockSpec((tm // TBLK, D, TBLK), lambda i, j: (i, 0, 0)),
                   _ANY),
        out_shape=(jax.ShapeDtypeStruct((5, T, 1024), F32),
                   jax.ShapeDtypeStruct((4, T, 1024), BF16),
                   jax.ShapeDtypeStruct((T, 128), F32),
                   jax.ShapeDtypeStruct((T // TBLK, D, TBLK), BF16),
                   jax.ShapeDtypeStruct((N_DEV,) + wp_part.shape, wp_part.dtype)),
        scratch_shapes=[pltpu.VMEM((tm, D), BF16)] + _PUSH_SEMS,
        compiler_params=_cparams(("arbitrary", "arbitrary")),
    )(x, norm_g, wt, wr, wp_part)


GLA_STEP_CHUNKS = 4


def _gla_same_chunk(rows):
    return (_iota2(rows, rows, 0) & -GLA_CHUNK) == (_iota2(rows, rows, 1) & -GLA_CHUNK)


def _gla_chunk_terms(la, q, k, n_c):
    C = GLA_CHUNK
    rows = n_c * C
    low = _gla_same_chunk(rows) & (_iota2(rows, rows, 0) >= _iota2(rows, rows, 1))
    b = _tri_left(_bf(low.astype(F32)), la)
    bl = [b[(c + 1) * C - 1:(c + 1) * C, :] for c in range(n_c)]
    bl_rows = jnp.concatenate([jnp.broadcast_to(bl[c], (C, b.shape[1])) for c in range(n_c)], axis=0)
    eb = jnp.exp(b)
    enb = jnp.exp(-b)
    ebl_b = jnp.exp(bl_rows - b)
    scale = GLA_HK ** -0.5
    qe = q * eb * scale
    ke = k * enb
    kd = k * ebl_b
    return bl, eb, enb, ebl_b, qe, ke, kd


def _gla_fwd_call(projf, projb, rank, wdec, bdec):
    T = projf.shape[1]
    C = GLA_CHUNK
    n_chunks = T // C
    n_c = GLA_STEP_CHUNKS
    R = n_c * C
    assert n_chunks % n_c == 0

    def body(qk_ref, v_ref, rank_ref, wd_ref, bd_ref, o_ref, st_ref, la_ref, st_scr):
        @pl.when(pl.program_id(0) == 0)
        def _():
            st_scr[...] = jnp.zeros_like(st_scr)

        dec = _dot(_bf(rank_ref[...]), _bf(wd_ref[...])) + bd_ref[...]
        la = (jnp.minimum(dec, 0.0) - _softplus_neg_abs(dec)) / GLA_TAU
        la_ref[...] = la
        mask = _gla_same_chunk(R) & (_iota2(R, R, 0) >= _iota2(R, R, 1))
        bl, _, _, _, qe, ke, kd = _gla_chunk_terms(la, qk_ref[:, :GLA_DK], qk_ref[:, GLA_DK:], n_c)
        qeb, keb, kdb = _bf(qe), _bf(ke), _bf(kd)
        ebl = [jnp.exp(bl[c]) for c in range(n_c)]
        heads = range(GLA_HEADS)
        ks = [slice(hh * GLA_HK, (hh + 1) * GLA_HK) for hh in heads]
        vs = [slice(hh * GLA_HV, (hh + 1) * GLA_HV) for hh in heads]
        rs = [slice(c * C, (c + 1) * C) for c in range(n_c)]
        p = [_bf(jnp.where(mask, _dot_nt(qeb[:, ks[hh]], keb[:, ks[hh]]), 0.0)) for hh in heads]
        upd = [[_dot_tn(v_ref[rs[c], vs[hh]], kdb[rs[c], ks[hh]]) for hh in heads] for c in range(n_c)]
        intra = [_dot(p[hh], v_ref[:, vs[hh]]) for hh in heads]
        st = [st_scr[hh] for hh in heads]
        for c in range(n_c):
            inter = [_dot_nt(qeb[rs[c], ks[hh]], _bf(st[hh])) for hh in heads]
            for hh in heads:
                st_ref[c, hh] = st[hh]
                o_ref[rs[c], vs[hh]] = intra[hh][rs[c]] + inter[hh]
            st = [st[hh] * ebl[c][:, ks[hh]] + upd[c][hh] for hh in heads]
        for hh in heads:
            st_scr[hh] = st[hh]

    return pl.pallas_call(
        body, name="gla_fwd",
        grid=(n_chunks // n_c,),
        in_specs=[pl.BlockSpec((None, R, 1024), lambda n: (0, n, 0)),
                  pl.BlockSpec((None, R, 1024), lambda n: (0, n, 0)),
                  pl.BlockSpec((R, 128), lambda n: (n, 0)),
                  pl.BlockSpec((128, GLA_DK), lambda n: (0, 0)),
                  pl.BlockSpec((1, GLA_DK), lambda n: (0, 0))],
        out_specs=(pl.BlockSpec((R, 1024), lambda n: (n, 0)),
                   pl.BlockSpec((n_c, GLA_HEADS, GLA_HV, GLA_HK), lambda n: (n, 0, 0, 0)),
                   pl.BlockSpec((R, GLA_DK), lambda n: (n, 0))),
        out_shape=(jax.ShapeDtypeStruct((T, 1024), F32),
                   jax.ShapeDtypeStruct((n_chunks, GLA_HEADS, GLA_HV, GLA_HK), F32),
                   jax.ShapeDtypeStruct((T, GLA_DK), F32)),
        scratch_shapes=[pltpu.VMEM((GLA_HEADS, GLA_HV, GLA_HK), F32)],
        compiler_params=_cparams(("arbitrary",)),
    )(projf, projb, rank, wdec, bdec)


def _gla_bwd_call(projf, projb, la, do_gla, st_all, rank, wdec):
    T = projf.shape[1]
    C = GLA_CHUNK
    n_chunks = T // C
    n_c = GLA_STEP_CHUNKS
    R = n_c * C
    assert n_chunks % n_c == 0
    last = n_chunks // n_c - 1

    def body(qk_ref, v_ref, la_ref, do_ref, st_ref, rank_ref, wd_ref,
             dqk_ref, dv_ref, drank_ref, dwd_ref, dbd_ref, dst_scr):
        @pl.when(pl.program_id(0) == 0)
        def _():
            dst_scr[...] = jnp.zeros_like(dst_scr)
            dwd_ref[...] = jnp.zeros_like(dwd_ref)
            dbd_ref[...] = jnp.zeros_like(dbd_ref)

        same = _gla_same_chunk(R)
        mask = same & (_iota2(R, R, 0) >= _iota2(R, R, 1))
        upp = _bf((same & (_iota2(R, R, 0) <= _iota2(R, R, 1))).astype(F32))
        scale = GLA_HK ** -0.5
        la = la_ref[...]
        bl, eb, enb, ebl_b, qe, ke, kd = _gla_chunk_terms(la, qk_ref[:, :GLA_DK], qk_ref[:, GLA_DK:], n_c)
        qeb, keb, kdb = _bf(qe), _bf(ke), _bf(kd)
        ebl = [jnp.exp(bl[c]) for c in range(n_c)]
        heads = range(GLA_HEADS)
        ks = [slice(hh * GLA_HK, (hh + 1) * GLA_HK) for hh in heads]
        vs = [slice(hh * GLA_HV, (hh + 1) * GLA_HV) for hh in heads]
        rs = [slice(c * C, (c + 1) * C) for c in range(n_c)]
        v = [v_ref[:, vs[hh]] for hh in heads]
        do = [_bf(do_ref[:, vs[hh]]) for hh in heads]
        p = [_bf(jnp.where(mask, _dot_nt(qeb[:, ks[hh]], keb[:, ks[hh]]), 0.0)) for hh in heads]
        dp = [_bf(jnp.where(mask, _dot_nt(do[hh], v[hh]), 0.0)) for hh in heads]
        dst_intra = [[_dot_tn(do[hh][rs[c]], qeb[rs[c], ks[hh]]) for hh in heads] for c in range(n_c)]
        dqe_inter = [[_dot(do[hh][rs[c]], _bf(st_ref[c, hh])) for hh in heads] for c in range(n_c)]
        dv_intra = [_dot_tn(p[hh], do[hh]) for hh in heads]
        dqe_intra = [_dot(dp[hh], keb[:, ks[hh]]) for hh in heads]
        dke = jnp.concatenate([_dot_tn(dp[hh], qeb[:, ks[hh]]) for hh in heads], axis=1)
        dstn = [dst_scr[hh] for hh in heads]
        dkd_c, dv_inter, debl = [None] * n_c, [None] * n_c, [None] * n_c
        for c in reversed(range(n_c)):
            dstnb = [_bf(dstn[hh]) for hh in heads]
            dkd_c[c] = jnp.concatenate([_dot(v[hh][rs[c]], dstnb[hh]) for hh in heads], axis=1)
            dv_inter[c] = [_dot_nt(kdb[rs[c], ks[hh]], dstnb[hh]) for hh in heads]
            debl[c] = jnp.concatenate(
                [jnp.sum(dstn[hh] * st_ref[c, hh], axis=0, keepdims=True) for hh in heads], axis=1)
            dstn = [dst_intra[c][hh] + dstn[hh] * ebl[c][:, ks[hh]] for hh in heads]
        for hh in heads:
            dst_scr[hh] = dstn[hh]
            dv_ref[:, vs[hh]] = _bf(dv_intra[hh] + jnp.concatenate([dv_inter[c][hh] for c in range(n_c)], axis=0))
        dqe = jnp.concatenate(
            [dqe_intra[hh] + jnp.concatenate([dqe_inter[c][hh] for c in range(n_c)], axis=0) for hh in heads], axis=1)
        dkd = jnp.concatenate(dkd_c, axis=0)
        dkd_kd = dkd * kd
        db = dqe * qe - dke * ke - dkd_kd
        dbl = jnp.concatenate(
            [jnp.broadcast_to(jnp.sum(dkd_kd[rs[c]], axis=0, keepdims=True) + ebl[c] * debl[c], (C, GLA_DK))
             for c in range(n_c)], axis=0)
        dla = _tri_left(upp, db) + dbl
        dqk_ref[:, :GLA_DK] = _bf(dqe * eb * scale)
        dqk_ref[:, GLA_DK:] = _bf(dke * enb + dkd * ebl_b)
        ddec = dla * (1.0 / GLA_TAU) * (1.0 - jnp.exp(GLA_TAU * la))
        ddecb = _bf(ddec)
        drank_ref[...] = _bf(_dot_nt(ddecb, _bf(wd_ref[...])))
        dwd_ref[...] += _dot_tn(_bf(rank_ref[...]), ddecb)
        dbd_ref[...] += jnp.sum(ddec, axis=0, keepdims=True)

    return pl.pallas_call(
        body, name="gla_bwd",
        grid=(n_chunks // n_c,),
        in_specs=[pl.BlockSpec((None, R, 1024), lambda n: (0, last - n, 0)),
                  pl.BlockSpec((None, R, 1024), lambda n: (0, last - n, 0)),
                  pl.BlockSpec((R, GLA_DK), lambda n: (last - n, 0)),
                  pl.BlockSpec((R, 1024), lambda n: (last - n, 0)),
                  pl.BlockSpec((n_c, GLA_HEADS, GLA_HV, GLA_HK), lambda n: (last - n, 0, 0, 0)),
                  pl.BlockSpec((R, 128), lambda n: (last - n, 0)),
                  pl.BlockSpec((128, GLA_DK), lambda n: (0, 0))],
        out_specs=(pl.BlockSpec((R, 1024), lambda n: (last - n, 0)),
                   pl.BlockSpec((R, 1024), lambda n: (last - n, 0)),
                   pl.BlockSpec((R, 128), lambda n: (last - n, 0)),
                   pl.BlockSpec((128, GLA_DK), lambda n: (0, 0)),
                   pl.BlockSpec((1, GLA_DK), lambda n: (0, 0))),
        out_shape=(jax.ShapeDtypeStruct((T, 1024), BF16),
                   jax.ShapeDtypeStruct((T, 1024), BF16),
                   jax.ShapeDtypeStruct((T, 128), BF16),
                   jax.ShapeDtypeStruct((128, GLA_DK), F32),
                   jax.ShapeDtypeStruct((1, GLA_DK), F32)),
        scratch_shapes=[pltpu.VMEM((GLA_HEADS, GLA_HV, GLA_HK), F32)],
        compiler_params=_cparams(("arbitrary",)),
    )(projf, projb, la, do_gla, st_all, rank, wdec)


def _sb_logs(z):
    lsz = jnp.minimum(z, 0.0) - _softplus_neg_abs(z)
    return lsz, lsz - z


SB_HG_FWD = 8
SB_HG_BWD = 4
SB_QUERIES = 256
SB_KEYS = 256
SB_DEAD = -105.0


def _sb_fwd_call(projb, wp_shard):
    T = projb.shape[1]
    B = min(SB_QUERIES, T)
    HG = SB_HG_FWD
    W = HG * SB_HD
    scale = 1.0 / math.sqrt(SB_HD)
    KB = min(SB_KEYS, T)
    n_h, n_i = SB_HEADS // HG, T // B

    def body(q_ref, k_ref, v_ref, wp_ref, o_ref, wpall_ref, cb_scr, send_sems, recv_sems, loc_sem):
        i = pl.program_id(1)
        own, pairs = _push_copies(wp_ref, wpall_ref, send_sems, recv_sems, loc_sem, scatter=False)

        @pl.when((pl.program_id(0) == 0) & (i == 0))
        def _():
            _push_start(own, pairs)

        rows = HG * B
        after = (_iota2(KB, KB, 0) > _iota2(KB, KB, 1)).astype(F32)
        tri = _bf(jnp.concatenate([after, jnp.ones((KB, KB), F32)], axis=1))
        o_ref[...] = jnp.zeros_like(o_ref)
        cb_scr[...] = jnp.zeros_like(cb_scr)

        def block(jp, masked):
            off = pl.multiple_of(jp * KB, KB)
            z = jnp.concatenate(
                [_dot_nt(q_ref[:, hh * SB_HD:(hh + 1) * SB_HD], k_ref[pl.ds(off, KB), hh * SB_HD:(hh + 1) * SB_HD])
                 for hh in range(HG)], axis=0) * scale
            lsz, l1m = _sb_logs(z)
            if masked:
                strict = (jp * KB + _iota2(rows, KB, 1)) < (i * B + (_iota2(rows, KB, 0) & (B - 1)))
                l1m = jnp.where(strict, l1m, 0.0)
            r = _tri2_right(l1m, tri)
            cb = cb_scr[...]
            a = jnp.exp(lsz + cb + r[:, :KB])
            if masked:
                a = jnp.where(strict, a, 0.0)
            cb_scr[...] = cb + r[:, KB:]
            ab = _bf(a)
            for hh in range(HG):
                cs = slice(hh * SB_HD, (hh + 1) * SB_HD)
                o_ref[:, cs] += _dot(ab[hh * B:(hh + 1) * B, :], v_ref[pl.ds(off, KB), cs])

        jp0 = (i * B) // KB
        block(jp0, True)

        def live(state):
            jj, dead = state
            return (jj <= jp0) & jnp.logical_not(dead)

        def step(state):
            jj, _ = state
            block(jp0 - jj, False)
            return jj + 1, jnp.max(cb_scr[:, :SB_HD]) < SB_DEAD

        lax.while_loop(live, step, (jnp.int32(1), jnp.max(cb_scr[:, :SB_HD]) < SB_DEAD))

        @pl.when((pl.program_id(0) == n_h - 1) & (i == n_i - 1))
        def _():
            _push_wait(own, pairs)

    return pl.pallas_call(
        body, name="sb_fwd",
        grid=(n_h, n_i),
        in_specs=[pl.BlockSpec((None, B, W), lambda h, i: (1, i, h)),
                  pl.BlockSpec((None, T, W), lambda h, i: (2, 0, h)),
                  pl.BlockSpec((None, T, W), lambda h, i: (3, 0, h)),
                  _ANY],
        out_specs=(pl.BlockSpec((B, W), lambda h, i: (i, h)), _ANY),
        out_shape=(jax.ShapeDtypeStruct((T, 1024), F32),
                   jax.ShapeDtypeStruct((N_DEV,) + wp_shard.shape, wp_shard.dtype)),
        scratch_shapes=[pltpu.VMEM((HG * B, KB), F32)] + _PUSH_SEMS,
        compiler_params=_cparams(("arbitrary", "arbitrary")),
    )(projb, projb, projb, wp_shard)


def _sb_bwd_call(projb, do_sb, g_p):
    T = projb.shape[1]
    B = min(SB_QUERIES, T)
    nb = T // B
    HG = SB_HG_BWD
    W = HG * SB_HD
    WQ = HG * B
    KB = min(SB_KEYS, T)
    nkb = T // KB
    n_h = SB_HEADS // HG
    scale = 1.0 / math.sqrt(SB_HD)

    def body(q_ref, k_ref, v_ref, do_ref, gp_ref, dq_ref, dk_ref, dv_ref, rp_ref,
             dk_scr, dv_scr, kt_scr, beta_scr, g_scr, dqt_scr, send_sems, recv_sems, loc_sem):
        i = pl.program_id(1)
        own, pairs = _push_copies(gp_ref, rp_ref, send_sems, recv_sems, loc_sem, scatter=True)

        @pl.when((pl.program_id(0) == 0) & (i == 0))
        def _():
            _push_start(own, pairs)

        @pl.when(i == 0)
        def _():
            dk_scr[...] = jnp.zeros_like(dk_scr)
            dv_scr[...] = jnp.zeros_like(dv_scr)
            for hh in range(HG):
                for jb in range(nkb):
                    kt_scr[hh, jb] = _bf(
                        k_ref[jb * KB:(jb + 1) * KB, hh * SB_HD:(hh + 1) * SB_HD].astype(F32).T)

        dqt_scr[...] = jnp.zeros_like(dqt_scr)
        later = _bf((_iota2(KB, KB, 1) > _iota2(KB, KB, 0)).astype(F32))
        earlier = _bf((_iota2(KB, KB, 1) < _iota2(KB, KB, 0)).astype(F32))
        dob = _bf(do_ref[...])
        jp0 = (i * B) // KB

        def strict_mask():
            return (jp0 * KB + _iota2(KB, WQ, 0)) < (i * B + (_iota2(KB, WQ, 1) & (B - 1)))

        def heads(fn):
            return [fn(slice(hh * SB_HD, (hh + 1) * SB_HD)) for hh in range(HG)]

        def pass1(jp, cb, masked):
            off = pl.multiple_of(jp * KB, KB)
            z = jnp.concatenate(heads(lambda cs: _dot_nt(k_ref[pl.ds(off, KB), cs], q_ref[:, cs])), axis=1) * scale
            da = jnp.concatenate(heads(lambda cs: _dot_nt(v_ref[pl.ds(off, KB), cs], dob[:, cs])), axis=1)
            lsz, l1m = _sb_logs(z)
            if masked:
                strict = strict_mask()
                l1m = jnp.where(strict, l1m, 0.0)
            a = jnp.exp(lsz + cb + _tri2_left(later, l1m))
            if masked:
                a = jnp.where(strict, a, 0.0)
            g_scr[jp] = a * da
            beta_scr[jp] = jnp.exp(lsz)
            ab = _bf(a)
            for hh in range(HG):
                cs = slice(hh * SB_HD, (hh + 1) * SB_HD)
                dv_scr[pl.ds(off, KB), cs] += _dot(ab[:, hh * B:(hh + 1) * B], dob[:, cs])
            return cb + jnp.sum(l1m, axis=0, keepdims=True)

        zero = jnp.zeros((1, WQ), F32)
        cb = pass1(jp0, zero, True)

        def live(state):
            jj, _, dead = state
            return (jj <= jp0) & jnp.logical_not(dead)

        def step(state):
            jj, cr, _ = state
            cr = pass1(jp0 - jj, cr, False)
            return jj + 1, cr, jnp.max(cr) < SB_DEAD

        n_done, _, _ = lax.while_loop(live, step, (jnp.int32(1), cb, jnp.max(cb) < SB_DEAD))
        jp_first = jp0 - (n_done - 1)

        def pass2(jp, cg, masked):
            off = pl.multiple_of(jp * KB, KB)
            g = g_scr[jp]
            beta = beta_scr[jp]
            dz = g * (1.0 - beta) - beta * (cg + _tri2_left(earlier, g))
            if masked:
                dz = jnp.where(strict_mask(), dz, 0.0)
            dzb = _bf(dz * scale)
            for hh in range(HG):
                cs = slice(hh * SB_HD, (hh + 1) * SB_HD)
                dk_scr[pl.ds(off, KB), cs] += _dot(dzb[:, hh * B:(hh + 1) * B], q_ref[:, cs])
                dqt_scr[hh] += _dot(kt_scr[hh, jp], dzb[:, hh * B:(hh + 1) * B])
            return cg + jnp.sum(g, axis=0, keepdims=True)

        cg = lax.fori_loop(jp_first, jp0, lambda jp, cr: pass2(jp, cr, False), zero)
        pass2(jp0, cg, True)
        for hh in range(HG):
            dq_ref[:, hh * SB_HD:(hh + 1) * SB_HD] = _bf(dqt_scr[hh].T)

        @pl.when(i == nb - 1)
        def _():
            dk_ref[...] = _bf(dk_scr[...])
            dv_ref[...] = _bf(dv_scr[...])

        @pl.when((pl.program_id(0) == n_h - 1) & (i == nb - 1))
        def _():
            _push_wait(own, pairs)

    return pl.pallas_call(
        body, name="sb_bwd",
        grid=(n_h, nb),
        in_specs=[pl.BlockSpec((None, B, W), lambda h, i: (1, i, h)),
                  pl.BlockSpec((None, T, W), lambda h, i: (2, 0, h)),
                  pl.BlockSpec((None, T, W), lambda h, i: (3, 0, h)),
                  pl.BlockSpec((B, W), lambda h, i: (i, h)),
                  _ANY],
        out_specs=(pl.BlockSpec((B, W), lambda h, i: (i, h)),
                   pl.BlockSpec((T, W), lambda h, i: (0, h)),
                   pl.BlockSpec((T, W), lambda h, i: (0, h)),
                   _ANY),
        out_shape=(jax.ShapeDtypeStruct((T, 1024), BF16),
                   jax.ShapeDtypeStruct((T, 1024), BF16),
                   jax.ShapeDtypeStruct((T, 1024), BF16),
                   jax.ShapeDtypeStruct(g_p.shape, g_p.dtype)),
        scratch_shapes=[pltpu.VMEM((T, W), F32), pltpu.VMEM((T, W), F32),
                        pltpu.VMEM((HG, nkb, SB_HD, KB), BF16),
                        pltpu.VMEM((nkb, KB, WQ), F32), pltpu.VMEM((nkb, KB, WQ), F32),
                        pltpu.VMEM((HG, SB_HD, B), F32)] + _PUSH_SEMS,
        compiler_params=_cparams(("arbitrary", "arbitrary")),
    )(projb, projb, projb, do_sb, g_p)


def _mid_call(o_gla, o_sb, projf, x, target, wpa, wpb, wo, gla_g, b_gate, final_g):
    T, D = x.shape
    tm = min(TBLK, T)

    def body(og_ref, ggate_ref, osb_ref, sgate_ref, ma_ref, mb_ref, x_ref, tgt_ref,
             wpa_ref, wpb_ref, wo_ref, glag_ref, bg_ref, fg_ref,
             dx2_ref, dogla_ref, dosb_ref, dggate_ref, dsgate_ref, dm_ref,
             mt_ref, ogt_ref, obt_ref, dx2b_ref, dya_ref, dyb_ref,
             dfg_ref, dbg_ref, dglag_ref, loss_ref):
        @pl.when(pl.program_id(0) == 0)
        def _():
            dfg_ref[...] = jnp.zeros_like(dfg_ref)
            dbg_ref[...] = jnp.zeros_like(dbg_ref)
            dglag_ref[...] = jnp.zeros_like(dglag_ref)
            loss_ref[...] = jnp.zeros_like(loss_ref)

        glag = glag_ref[...]
        ggate = ggate_ref[...]
        sg = _sigmoid(ggate)
        silu_g = ggate * sg
        ohat, rinv, nrm = [], [], []
        for hh in range(GLA_HEADS):
            oh = og_ref[:, hh * GLA_HV:(hh + 1) * GLA_HV]
            r = lax.rsqrt(jnp.mean(oh * oh, axis=-1, keepdims=True) + EPS)
            ohat.append(oh * r)
            rinv.append(r)
            nrm.append(ohat[-1] * glag)
        n_all = jnp.concatenate(nrm, axis=1)
        og = n_all * silu_g
        ogb = _bf(og)
        ya = _dot(ogb, wpa_ref[...])
        sgate = sgate_ref[...]
        ss = _sigmoid(sgate)
        silu_s = sgate * ss
        osb = osb_ref[...]
        ob = osb * silu_s
        obb = _bf(ob)
        yb = _dot(obb, wpb_ref[...])
        ga = _sigmoid(ma_ref[...] + bg_ref[:, :D])
        gb = _sigmoid(mb_ref[...] + bg_ref[:, D:])
        merged = ga * ya + gb * yb
        mgb = _bf(merged)
        x2 = x_ref[...] + _dot(mgb, wo_ref[...])
        r2 = lax.rsqrt(jnp.mean(x2 * x2, axis=-1, keepdims=True) + EPS)
        xh2 = x2 * r2
        fg = fg_ref[...]
        err = xh2 * fg - tgt_ref[...]
        loss_ref[...] += jnp.broadcast_to(
            0.5 * jnp.sum(jnp.mean(err * err, axis=-1, keepdims=True), axis=0, keepdims=True), (1, 128))
        dy = err * (1.0 / D)
        dfg_ref[...] += jnp.sum(dy * xh2, axis=0, keepdims=True)
        dxh = dy * fg
        dx2 = r2 * (dxh - xh2 * jnp.mean(dxh * xh2, axis=-1, keepdims=True))
        dx2_ref[...] = dx2
        dx2b = _bf(dx2)
        dx2b_ref[...] = dx2b
        dmerged = _dot_nt(dx2b, wo_ref[...])
        dya = dmerged * ga
        dyb = dmerged * gb
        dma = dmerged * ya * ga * (1.0 - ga)
        dmb = dmerged * yb * gb * (1.0 - gb)
        dm_ref[:, :D] = _bf(dma)
        dm_ref[:, D:] = _bf(dmb)
        dbg_ref[:, :D] += jnp.sum(dma, axis=0, keepdims=True)
        dbg_ref[:, D:] += jnp.sum(dmb, axis=0, keepdims=True)
        dyab = _bf(dya)
        dybb = _bf(dyb)
        dya_ref[...] = dyab
        dyb_ref[...] = dybb
        dog = _dot_nt(dyab, wpa_ref[...])
        dob = _dot_nt(dybb, wpb_ref[...])
        dosb_ref[...] = dob * silu_s
        dsgate_ref[...] = _bf(dob * osb * (ss * (1.0 + sgate * (1.0 - ss))))
        dn = dog * silu_g
        dggate_ref[...] = _bf(dog * n_all * (sg * (1.0 + ggate * (1.0 - sg))))
        dglag = jnp.zeros((1, GLA_HV), F32)
        for hh in range(GLA_HEADS):
            dnh = dn[:, hh * GLA_HV:(hh + 1) * GLA_HV]
            dglag = dglag + jnp.sum(dnh * ohat[hh], axis=0, keepdims=True)
            dohat = dnh * glag
            dogla_ref[:, hh * GLA_HV:(hh + 1) * GLA_HV] = rinv[hh] * (
                dohat - ohat[hh] * jnp.mean(dohat * ohat[hh], axis=-1, keepdims=True))
        dglag_ref[...] += dglag
        mt_ref[...] = _bf(merged.T)
        ogt_ref[...] = _bf(og.T)
        obt_ref[...] = _bf(ob.T)

    row = lambda i: (i, 0)
    const = lambda i: (0, 0)
    tile = pl.BlockSpec((tm, D), row)
    tile_t = pl.BlockSpec((None, D, tm), lambda i: (i, 0, 0))
    wspec = pl.BlockSpec((D, D), const)
    return pl.pallas_call(
        body, name="mid",
        grid=(T // tm,),
        in_specs=[tile,
                  pl.BlockSpec((None, tm, D), lambda i: (1, i, 0)),
                  tile,
                  pl.BlockSpec((None, tm, D), lambda i: (2, i, 0)),
                  pl.BlockSpec((None, tm, D), lambda i: (3, i, 0)),
                  pl.BlockSpec((None, tm, D), lambda i: (4, i, 0)),
                  tile, tile, wspec, wspec, wspec,
                  pl.BlockSpec((1, GLA_HV), const),
                  pl.BlockSpec((1, 2 * D), const),
                  pl.BlockSpec((1, D), const)],
        out_specs=(tile, tile, tile, tile, tile,
                   pl.BlockSpec((tm, 2 * D), row),
                   tile_t, tile_t, tile_t, tile, tile, tile,
                   pl.BlockSpec((1, D), const),
                   pl.BlockSpec((1, 2 * D), const),
                   pl.BlockSpec((1, GLA_HV), const),
                   pl.BlockSpec((1, 128), const)),
        out_shape=(jax.ShapeDtypeStruct((T, D), F32),
                   jax.ShapeDtypeStruct((T, D), F32),
                   jax.ShapeDtypeStruct((T, D), F32),
                   jax.ShapeDtypeStruct((T, D), BF16),
                   jax.ShapeDtypeStruct((T, D), BF16),
                   jax.ShapeDtypeStruct((T, 2 * D), BF16),
                   jax.ShapeDtypeStruct((T // tm, D, tm), BF16),
                   jax.ShapeDtypeStruct((T // tm, D, tm), BF16),
                   jax.ShapeDtypeStruct((T // tm, D, tm), BF16),
                   jax.ShapeDtypeStruct((T, D), BF16),
                   jax.ShapeDtypeStruct((T, D), BF16),
                   jax.ShapeDtypeStruct((T, D), BF16),
                   jax.ShapeDtypeStruct((1, D), F32),
                   jax.ShapeDtypeStruct((1, 2 * D), F32),
                   jax.ShapeDtypeStruct((1, GLA_HV), F32),
                   jax.ShapeDtypeStruct((1, 128), F32)),
        compiler_params=_cparams(("arbitrary",)),
    )(o_gla, projf, o_sb, projf, projf, projf, x, target, wpa, wpb, wo, gla_g, b_gate, final_g)


def _dh_call(pieces, dmlog, drank, wt, wr, x, dx2, norm_g, s_in):
    T, D = x.shape
    tm = min(256, T)
    npc = len(pieces)
    n_main = N_GROUPS * 1024
    n_i = T // tm

    def body(*refs):
        pcs = refs[:npc]
        (dm_ref, dr_ref, w_hbm, wr_ref, x_ref, dx2_ref, g_ref, sin_ref,
         gx_ref, dg_ref, rin_ref, w_scr, sems, send_sems, recv_sems, loc_sem) = refs[npc:]
        own, pairs = _chip_copies(sin_ref, rin_ref, send_sems, recv_sems, loc_sem)

        @pl.when(pl.program_id(0) == 0)
        def _():
            _push_start(own, pairs)
            lo = pltpu.make_async_copy(w_hbm.at[pl.ds(0, RANK_COL)], w_scr.at[pl.ds(0, RANK_COL)], sems.at[0])
            hi = pltpu.make_async_copy(w_hbm.at[pl.ds(RANK_COL + GLA_RANK, n_main - RANK_COL)],
                                       w_scr.at[pl.ds(RANK_COL, n_main - RANK_COL)], sems.at[1])
            lo.start()
            hi.start()
            dg_ref[...] = jnp.zeros_like(dg_ref)
            lo.wait()
            hi.wait()

        def w_group(g):
            return w_scr[g * 1024:(g + 1) * 1024, :]

        dr = dr_ref[...]
        dh = _dot(dr, wr_ref[...])
        for g in range(npc):
            dh = dh + _dot(pcs[g][...], w_group(g))
        dh = dh + _dot(dm_ref[:, :D], w_group(npc))
        dh = dh + _dot(dm_ref[:, D:], w_group(npc + 1))
        xv = x_ref[...]
        r = lax.rsqrt(jnp.mean(xv * xv, axis=-1, keepdims=True) + EPS)
        xhat = xv * r
        g = g_ref[...]
        dg_ref[...] += jnp.sum(dh * xhat, axis=0, keepdims=True)
        dxhat = dh * g
        gx_ref[...] = r * (dxhat - xhat * jnp.mean(dxhat * xhat, axis=-1, keepdims=True)) + dx2_ref[...]

        @pl.when(pl.program_id(0) == n_i - 1)
        def _():
            _push_wait(own, pairs)

    row = lambda i: (i, 0)
    const = lambda i: (0, 0)
    tile = pl.BlockSpec((tm, D), row)
    return pl.pallas_call(
        body, name="dh",
        grid=(n_i,),
        in_specs=[tile] * npc + [
            pl.BlockSpec((tm, 2 * D), row),
            pl.BlockSpec((tm, 128), row),
            _ANY,
            pl.BlockSpec((128, D), const),
            tile, tile,
            pl.BlockSpec((1, D), const),
            _ANY],
        out_specs=(tile, pl.BlockSpec((1, D), const), _ANY),
        out_shape=(jax.ShapeDtypeStruct((T, D), F32),
                   jax.ShapeDtypeStruct((1, D), F32),
                   jax.ShapeDtypeStruct(s_in.shape, s_in.dtype)),
        scratch_shapes=[pltpu.VMEM((n_main, D), BF16), pltpu.SemaphoreType.DMA((2,))] + _CHIP_SEMS,
        compiler_params=_cparams(("arbitrary",)),
    )(*pieces, dmlog, drank, wt, wr, x, dx2, norm_g, s_in)


def _wgrad_rank_call(ht, drank):
    n_tb, D, tb = ht.shape

    def body(ht_ref, dr_ref, o_ref):
        @pl.when(pl.program_id(0) == 0)
        def _():
            o_ref[...] = jnp.zeros_like(o_ref)

        o_ref[...] += _dot(ht_ref[...], dr_ref[...])

    return pl.pallas_call(
        body, name="wgrad_rank",
        grid=(n_tb,),
        in_specs=[pl.BlockSpec((None, D, tb), lambda i: (i, 0, 0)),
                  pl.BlockSpec((tb, 128), lambda i: (i, 0))],
        out_specs=pl.BlockSpec((D, 128), lambda i: (0, 0)),
        out_shape=jax.ShapeDtypeStruct((D, 128), F32),
        compiler_params=_cparams(("arbitrary",)),
    )(ht, drank)


def _wgrad_call(lhs_list, lhs_of_group, rhs_list, rhs_of_group, n_transposed, name):
    n_groups = len(rhs_of_group)
    n_tb, D, tb = lhs_list[0].shape
    T = n_tb * tb
    per = min(4, n_tb)
    tk = per * tb
    nk = T // tk
    nl = len(lhs_list)

    def body(*refs):
        lhs = refs[:nl]
        rhs = refs[nl:nl + n_groups]
        out_ref, acc = refs[nl + n_groups:]
        g = pl.program_id(0)
        i = pl.program_id(1)

        @pl.when(i == 0)
        def _():
            acc[...] = jnp.zeros_like(acc)

        for p in range(n_groups):
            @pl.when(g == p)
            def _(p=p):
                lref = lhs[lhs_of_group[p]]
                part = _dot(lref[0], rhs[p][0:tb, :])
                for b in range(1, per):
                    part = part + _dot(lref[b], rhs[p][b * tb:(b + 1) * tb, :])
                acc[...] += part

        @pl.when((i == nk - 1) & (g < n_transposed))
        def _():
            out_ref[...] = _bf(acc[...].T)

        @pl.when((i == nk - 1) & (g >= n_transposed))
        def _():
            out_ref[...] = _bf(acc[...])

    def lhs_spec(a):
        groups = [g for g in range(n_groups) if lhs_of_group[g] == a]
        lo, hi = min(groups), max(groups)
        assert groups == list(range(lo, hi + 1))
        return pl.BlockSpec((per, D, tb), lambda g, i: (jnp.where((g >= lo) & (g <= hi), i, 0), 0, 0))

    def rhs_spec(p):
        cb = rhs_of_group[p][1]
        return pl.BlockSpec((tk, 1024), lambda g, i: (jnp.where(g == p, i, 0), cb))

    return pl.pallas_call(
        body, name=name,
        grid=(n_groups, nk),
        in_specs=[lhs_spec(a) for a in range(nl)] + [rhs_spec(p) for p in range(n_groups)],
        out_specs=pl.BlockSpec((None, D, 1024), lambda g, i: (g, 0, 0)),
        out_shape=jax.ShapeDtypeStruct((n_groups, D, 1024), BF16),
        scratch_shapes=[pltpu.VMEM((D, 1024), F32)],
        compiler_params=_cparams(("arbitrary", "arbitrary")),
    )(*lhs_list, *[rhs_list[rhs_of_group[p][0]] for p in range(n_groups)])


def _adamw_math(parts, w, m, v):
    g = parts[0].astype(F32)
    for p in parts[1:]:
        g = g + p.astype(F32)
    mm = ADAM_B1 * m + (1.0 - ADAM_B1) * g
    vv = ADAM_B2 * v + (1.0 - ADAM_B2) * (g * g)
    m_hat = mm / (1.0 - ADAM_B1 ** ADAM_STEP)
    v_hat = vv / (1.0 - ADAM_B2 ** ADAM_STEP)
    return g, -ADAM_LR * (m_hat / (jnp.sqrt(v_hat) + ADAM_EPS) + ADAM_WD * w), mm, vv


def _part_order(n_parts):
    return [n_parts - 1] + list(range(n_parts - 1))


def _adamw_call(parts, w, m, v, name):
    R, C = w.shape
    n_parts = parts.shape[0]
    (tr, tc), grid, idx = _tiling_2d(R, C, 512)

    def body(p_ref, w_ref, m_ref, v_ref, g_ref, d_ref, nm_ref, nv_ref):
        g_ref[...], d_ref[...], nm_ref[...], nv_ref[...] = _adamw_math(
            [p_ref[k] for k in _part_order(n_parts)], w_ref[...], m_ref[...], v_ref[...])

    blk = pl.BlockSpec((tr, tc), idx)
    sds = jax.ShapeDtypeStruct((R, C), F32)
    return pl.pallas_call(
        body, name=name,
        grid=grid,
        in_specs=[pl.BlockSpec((n_parts, tr, tc), lambda i: (0,) + idx(i)), blk, blk, blk],
        out_specs=(blk, blk, blk, blk),
        out_shape=(sds, sds, sds, sds),
        compiler_params=_cparams(("arbitrary",)),
    )(parts, w, m, v)


def _adamw_rows_call(parts, ws, ms, vs, name, gathered):
    n = len(ws)
    R, C = ws[0].shape
    n_parts = parts.shape[0]

    def body(*refs):
        p_ref = refs[0]
        w_refs, m_refs, v_refs = refs[1:1 + n], refs[1 + n:1 + 2 * n], refs[1 + 2 * n:1 + 3 * n]
        src_ref = refs[1 + 3 * n]
        outs = refs[2 + 3 * n:2 + 7 * n]
        dst_ref, send_sems, recv_sems, loc_sem = refs[2 + 7 * n:]
        own, pairs = _push_copies(src_ref, dst_ref, send_sems, recv_sems, loc_sem, scatter=False)
        k_now = pl.program_id(0)

        @pl.when(k_now == 0)
        def _():
            _push_start(own, pairs)

        for k in range(n):
            @pl.when(k_now == k)
            def _(k=k):
                res = _adamw_math([p_ref[j] for j in _part_order(n_parts)],
                                  w_refs[k][...], m_refs[k][...], v_refs[k][...])
                for o_ref, val in zip(outs[4 * k:4 * k + 4], res):
                    o_ref[...] = val

        @pl.when(k_now == n - 1)
        def _():
            _push_wait(own, pairs)

    whole = pl.BlockSpec((R, C), lambda k: (0, 0))
    sds = jax.ShapeDtypeStruct((R, C), F32)
    res = pl.pallas_call(
        body, name=name,
        grid=(n,),
        in_specs=[pl.BlockSpec((n_parts, R, C), lambda k: (0, k, 0))] + [whole] * (3 * n) + [_ANY],
        out_specs=tuple([whole] * (4 * n) + [_ANY]),
        out_shape=tuple([sds] * (4 * n) + [jax.ShapeDtypeStruct((N_DEV,) + gathered.shape, gathered.dtype)]),
        scratch_shapes=_PUSH_SEMS,
        compiler_params=_cparams(("arbitrary",)),
    )(parts, *ws, *ms, *vs, gathered)
    return [res[4 * k:4 * k + 4] for k in range(n)], res[4 * n]


def _adamw_lanes_call(parts, offsets, ws, ms, vs, name):
    n = len(ws)
    n_parts = parts.shape[0]

    def body(*refs):
        p_ref = refs[0]
        w_refs, m_refs, v_refs = refs[1:1 + n], refs[1 + n:1 + 2 * n], refs[1 + 2 * n:1 + 3 * n]
        outs = refs[1 + 3 * n:]
        for k in range(n):
            lanes = slice(offsets[k], offsets[k] + ws[k].shape[1])
            res = _adamw_math([p_ref[j, :, lanes] for j in _part_order(n_parts)],
                              w_refs[k][...], m_refs[k][...], v_refs[k][...])
            for o_ref, val in zip(outs[4 * k:4 * k + 4], res):
                o_ref[...] = val

    res = pl.pallas_call(
        body, name=name,
        out_shape=tuple(jax.ShapeDtypeStruct(ws[k].shape, F32) for k in range(n) for _ in range(4)),
        compiler_params=_cparams(),
    )(parts, *ws, *ms, *vs)
    return [res[4 * k:4 * k + 4] for k in range(n)]


def _local_step(x, target, wt, wr, wdec, bdec, wp_shard, norm_g, gla_g, b_gate, final_g):
    D = x.shape[1]
    half = wp_shard.shape[1] // 2
    projf, projb, rank, ht, wp_lo = _proj_call(x, norm_g, wt, wr, wp_shard[:, :half])
    o_gla, st_all, la = _gla_fwd_call(projf, projb, rank, wdec, bdec)
    o_sb, wp_hi = _sb_fwd_call(projb, wp_shard[:, half:])
    wp_full = jnp.concatenate([wp_lo, wp_hi], axis=2).transpose(1, 0, 2, 3).reshape(3, D, D)
    (dx2, do_gla, do_sb, dggate, dsgate, dmlog, mt, ogt, obt, dx2b, dya, dyb,
     dfinal_g, db_gate, dgla_g, loss) = _mid_call(o_gla, o_sb, projf, x, target, wp_full[0], wp_full[1],
                                                 wp_full[2], gla_g, b_gate, final_g)
    dw_p = _wgrad_call([ogt, obt, mt], [0, 1, 2], [dya, dyb, dx2b], [(0, 0), (1, 0), (2, 0)], 0, "wgrad_p")
    g_p = dw_p.reshape(3, N_DEV, D // N_DEV, D).transpose(1, 0, 2, 3).reshape(N_DEV, 3 * (D // N_DEV), D)
    dqk, dgv, drank, dwdec, dbdec = _gla_bwd_call(projf, projb, la, do_gla, st_all, rank, wdec)
    dsq, dsk, dsv, r_p = _sb_bwd_call(projb, do_sb, g_p)
    pieces = [dqk, dgv, dggate, dsq, dsk, dsv, dsgate]
    rhs_of_group = [(g, 0) for g in range(7)] + [(7, 0), (7, 1)]
    dw_in = _wgrad_call([ht], [0] * N_GROUPS, pieces + [dmlog], rhs_of_group, N_GROUPS, "wgrad_in")
    dwr = _wgrad_rank_call(ht, drank)
    g_in = _parts_by_device(dw_in.reshape(N_GROUPS * 1024, D), dwr[:, :GLA_RANK].T.astype(BF16))
    c_idx = lax.axis_index("c").astype(jnp.int32).reshape(1)
    (p_in,) = _pair_exchange([g_in], "pair_g")
    s_in = _pair_add_call(g_in, p_in, c_idx, "pair_add_in")
    grad_x, dnorm_g, r_in = _dh_call(pieces, dmlog, drank, wt, wr, x, dx2, norm_g, s_in)
    small = jnp.concatenate([
        dnorm_g.reshape(-1), dbdec.reshape(-1), dgla_g.reshape(-1), db_gate.reshape(-1), dfinal_g.reshape(-1),
        loss.reshape(-1), dwdec[:GLA_RANK].reshape(-1)]).reshape(1, _SM_LEN)
    return grad_x, r_in, r_p, small


def _parts_by_device(dmain, drank):
    def part_for(p):
        lo, hi = p * SHARD_COLS, (p + 1) * SHARD_COLS
        pieces = []
        if lo < RANK_COL:
            pieces.append(dmain[lo:min(hi, RANK_COL)])
        if lo < RANK_COL + GLA_RANK and hi > RANK_COL:
            pieces.append(drank[max(lo, RANK_COL) - RANK_COL:min(hi, RANK_COL + GLA_RANK) - RANK_COL])
        if hi > RANK_COL + GLA_RANK:
            pieces.append(dmain[max(lo, RANK_COL + GLA_RANK) - GLA_RANK:hi - GLA_RANK])
        return pieces[0] if len(pieces) == 1 else jnp.concatenate(pieces, axis=0)

    return jnp.stack([part_for(p) for p in range(N_DEV)])


_SM_NORM = 0
_SM_BDEC = _SM_NORM + D_MODEL
_SM_GLAG = _SM_BDEC + GLA_DK
_SM_BGATE = _SM_GLAG + GLA_HV
_SM_FINAL = _SM_BGATE + 2 * D_MODEL
_SM_REPL = _SM_FINAL + D_MODEL
_SM_LOSS = _SM_REPL
_SM_WDEC = _SM_LOSS + 128
_SM_LEN = _SM_WDEC + GLA_RANK * GLA_DK


def kernel(x, norm_g, w_in, w_dec_up, b_dec, gla_norm_g, w_pa, w_pb, b_gate, w_o, final_g, loss_target, m_norm_g, m_w_in, m_w_dec_up, m_b_dec, m_gla_norm_g, m_w_pa, m_w_pb, m_b_gate, m_w_o, m_final_g, v_norm_g, v_w_in, v_w_dec_up, v_b_dec, v_gla_norm_g, v_w_pa, v_w_pb, v_b_gate, v_w_o, v_final_g):
    D = D_MODEL
    me = 4 * lax.axis_index("x") + 2 * lax.axis_index("y") + lax.axis_index("c")

    wp_shard = jnp.stack([w_pa, w_pb, w_o]).astype(BF16)
    n_first = (SHARD_COLS // 2) // 16 * 16
    win_all, wdec_all = _all_gather([w_in.T.astype(BF16), w_dec_up], "gather_w",
                                    row_pieces=[[(0, n_first), (n_first, SHARD_COLS - n_first)], None])
    wt = win_all.reshape(IN_COLS, D)
    wr = jnp.pad(wt[RANK_COL:RANK_COL + GLA_RANK], ((0, 128 - GLA_RANK), (0, 0)))
    wdec_full = wdec_all.transpose(1, 0, 2).reshape(GLA_RANK, GLA_DK)
    wdec = jnp.pad(wdec_full, ((0, 128 - GLA_RANK), (0, 0)))

    grad_x, r_in, r_p, small = _local_step(
        x[0], loss_target[0], wt, wr, wdec, b_dec.reshape(1, -1), wp_shard,
        norm_g.reshape(1, -1), gla_norm_g.reshape(1, -1), b_gate.reshape(1, -1), final_g.reshape(1, -1))

    gw_in, d_in, nm_in, nv_in = (a.T for a in _adamw_call(r_in, w_in.T, m_w_in.T, v_w_in.T, "adamw_in"))
    ((g_pa, d_pa, nm_pa, nv_pa), (g_pb, d_pb, nm_pb, nv_pb), (g_o, d_o, nm_o, nv_o)), r_small = _adamw_rows_call(
        r_p, [w_pa, w_pb, w_o], [m_w_pa, m_w_pb, m_w_o], [v_w_pa, v_w_pb, v_w_o], "adamw_p", small)

    def row(a):
        return a.reshape(1, -1)

    rep = _adamw_lanes_call(
        r_small, [_SM_NORM, _SM_BDEC, _SM_GLAG, _SM_BGATE, _SM_FINAL],
        [row(a) for a in (norm_g, b_dec, gla_norm_g, b_gate, final_g)],
        [row(a) for a in (m_norm_g, m_b_dec, m_gla_norm_g, m_b_gate, m_final_g)],
        [row(a) for a in (v_norm_g, v_b_dec, v_gla_norm_g, v_b_gate, v_final_g)], "adamw_rep")
    ((g_norm, d_norm, nm_norm, nv_norm), (g_bdec, d_bdec, nm_bdec, nv_bdec), (g_glag, d_glag, nm_glag, nv_glag),
     (g_bgate, d_bgate, nm_bgate, nv_bgate), (g_final, d_final, nm_final, nv_final)) = [
        tuple(a.reshape(-1) for a in quad) for quad in rep]

    wdec_parts = r_small[:, 0, _SM_WDEC:].reshape(N_DEV, GLA_RANK, GLA_DK)
    cols = GLA_DK // N_DEV
    wdec_mine = lax.dynamic_slice_in_dim(wdec_parts, me * cols, cols, axis=2)
    g_wdec, d_wdec, nm_wdec, nv_wdec = _adamw_call(wdec_mine, w_dec_up, m_w_dec_up, v_w_dec_up, "adamw_dec")

    loss_total = jnp.sum(r_small[:, 0, _SM_LOSS])

    return (loss_total, grad_x[None],
            g_norm, gw_in, g_wdec, g_bdec, g_glag, g_pa, g_pb, g_bgate, g_o, g_final,
            d_norm, d_in, d_wdec, d_bdec, d_glag, d_pa, d_pb, d_bgate, d_o, d_final,
            nm_norm, nm_in, nm_wdec, nm_bdec, nm_glag, nm_pa, nm_pb, nm_bgate, nm_o, nm_final,
            nv_norm, nv_in, nv_wdec, nv_bdec, nv_glag, nv_pa, nv_pb, nv_bgate, nv_o, nv_final)
```

```python
import math

import jax
import jax.numpy as jnp
from jax import lax
from jax.experimental import pallas as pl
from jax.experimental.pallas import tpu as pltpu

F32 = jnp.float32
BF16 = jnp.bfloat16

N_DEV = 8
D_MODEL = 1024
GLA_HEADS = 4
GLA_HK = 128
GLA_HV = 256
GLA_DK = 512
GLA_RANK = 16
GLA_TAU = 16.0
GLA_CHUNK = 64
SB_HEADS = 8
SB_HD = 128
EPS = 1e-6
N_GROUPS = 9
RANK_COL = 3072
IN_COLS = 9232
SHARD_COLS = IN_COLS // N_DEV

ADAM_LR = 0.001
ADAM_B1 = 0.9
ADAM_B2 = 0.999
ADAM_EPS = 1e-08
ADAM_WD = 0.01
ADAM_STEP = 10

VMEM_LIMIT = 56 * 1024 * 1024
TBLK = 256


def _cparams(sem=None):
    return pltpu.CompilerParams(dimension_semantics=sem, vmem_limit_bytes=VMEM_LIMIT)


def _tiling_2d(rows, cols, band_cols):
    if rows * cols <= 128 * 1024:
        return (rows, cols), (1,), lambda i: (0, 0)
    if rows % 128 == 0:
        return (128, cols), (rows // 128,), lambda i: (i, 0)
    tc = band_cols if cols % band_cols == 0 else cols
    return (rows, tc), (cols // tc,), lambda i: (0, i)


def _dot(a, b):
    return jnp.dot(a, b, preferred_element_type=F32)


def _dot_nt(a, b):
    return lax.dot_general(a, b, (((1,), (1,)), ((), ())), preferred_element_type=F32)


def _dot_tn(a, b):
    return lax.dot_general(a, b, (((0,), (0,)), ((), ())), preferred_element_type=F32)


def _bf(x):
    return x.astype(BF16)


def _split3(x):
    hi = x.astype(BF16)
    r = x - hi.astype(F32)
    mid = r.astype(BF16)
    lo = (r - mid.astype(F32)).astype(BF16)
    return hi, mid, lo


def _tri_left(tri, x):
    hi, mid, lo = _split3(x)
    return _dot(tri, hi) + _dot(tri, mid) + _dot(tri, lo)


def _split2(x):
    hi = lax.bitcast_convert_type(lax.bitcast_convert_type(x, jnp.uint32) & jnp.uint32(0xFFFF0000), F32)
    return hi.astype(BF16), (x - hi).astype(BF16)


def _tri2_left(tri, x):
    hi, lo = _split2(x)
    return _dot(tri, hi) + _dot(tri, lo)


def _tri2_right(x, tri):
    hi, lo = _split2(x)
    return _dot(hi, tri) + _dot(lo, tri)


def _iota2(n, m, dim):
    return lax.broadcasted_iota(jnp.int32, (n, m), dim)


def _sigmoid(x):
    return 1.0 / (1.0 + jnp.exp(-x))


def _softplus_neg_abs(z):
    return jnp.log(1.0 + jnp.exp(-jnp.abs(z)))


_ANY = pl.BlockSpec(memory_space=pl.ANY)


def _mesh_pos():
    return lax.axis_index("x"), lax.axis_index("y"), lax.axis_index("c")


def _other_chips(x, y):
    return [(1 - x, y), (x, 1 - y), (1 - x, 1 - y)]


def _rcopy(src, dst, send_sem, recv_sem, to):
    return pltpu.make_async_remote_copy(src_ref=src, dst_ref=dst, send_sem=send_sem, recv_sem=recv_sem,
                                        device_id=to, device_id_type=pl.DeviceIdType.MESH)


def _push_copies(src_ref, dst_ref, send_sems, recv_sems, loc_sem, scatter):
    x, y, c = _mesh_pos()
    me = 4 * x + 2 * y + c
    own = pltpu.make_async_copy(src_ref.at[me] if scatter else src_ref, dst_ref.at[me], loc_sem)
    pairs = []
    for k in range(1, N_DEV):
        px = 1 - x if k & 4 else x
        py = 1 - y if k & 2 else y
        pc = 1 - c if k & 1 else c
        pid = 4 * px + 2 * py + pc
        src = src_ref.at[pid] if scatter else src_ref
        send = _rcopy(src, dst_ref.at[me], send_sems.at[k - 1], recv_sems.at[k - 1], (px, py, pc))
        recv = _rcopy(src, dst_ref.at[pid], send_sems.at[k - 1], recv_sems.at[k - 1], (px, py, pc))
        pairs.append((send, recv))
    return own, pairs


def _push_start(own, pairs):
    own.start()
    for send, _ in pairs:
        send.start()


def _push_wait(own, pairs):
    for _, recv in pairs:
        recv.wait_recv()
    for send, _ in pairs:
        send.wait_send()
    own.wait()


_PUSH_SEMS = [pltpu.SemaphoreType.DMA((N_DEV - 1,)), pltpu.SemaphoreType.DMA((N_DEV - 1,)),
              pltpu.SemaphoreType.DMA]


def _chip_copies(src_ref, dst_ref, send_sems, recv_sems, loc_sem):
    x, y, c = _mesh_pos()
    own = pltpu.make_async_copy(src_ref.at[2 * x + y], dst_ref.at[3], loc_sem)
    pairs = []
    for j, (px, py) in enumerate(_other_chips(x, y)):
        cp = _rcopy(src_ref.at[2 * px + py], dst_ref.at[j], send_sems.at[j], recv_sems.at[j], (px, py, c))
        pairs.append((cp, cp))
    return own, pairs


_CHIP_SEMS = [pltpu.SemaphoreType.DMA((3,)), pltpu.SemaphoreType.DMA((3,)), pltpu.SemaphoreType.DMA]


def _all_gather(arrs, name, row_pieces=None):
    n = len(arrs)
    pieces = [[None] if not row_pieces or not row_pieces[a] else list(row_pieces[a]) for a in range(n)]
    n_pc = max(len(p) for p in pieces)
    units = [(a, i) for a in range(n) for i in range(len(pieces[a]))]

    def body(*refs):
        ins = refs[:n]
        outs = refs[n:2 * n]
        send_sems, recv_sems, loc_sems = refs[2 * n:]
        x, y, c = _mesh_pos()
        sib = (x, y, 1 - c)
        chips = _other_chips(x, y)

        def rows(ref, a, i):
            return ref if pieces[a][i] is None else ref.at[pl.ds(*pieces[a][i])]

        def place(a, i, px, py, pc):
            return rows(outs[a].at[4 * px + 2 * py + pc], a, i)

        def copy(u, k, block, to, own=False):
            a, i = u
            dst = place(a, i, *block)
            return _rcopy(rows(ins[a], a, i) if own else dst, dst, send_sems.at[a, k, i], recv_sems.at[a, k, i], to)

        mine = [pltpu.make_async_copy(ins[a], outs[a].at[4 * x + 2 * y + c], loc_sems.at[a]) for a in range(n)]
        for cp in mine:
            cp.start()
        first = [copy(u, 0, (x, y, c), sib, own=True) for u in units]
        for j, chip in enumerate(chips):
            first += [copy(u, 1 + j, (x, y, c), (*chip, c), own=True) for u in units]
        for cp in first:
            cp.start()
        passed = []
        for u in units:
            for j, chip in enumerate(chips):
                copy(u, 1 + j, (*chip, c), (x, y, c)).wait_recv()
                fwd = copy(u, 4 + j, (*chip, c), sib)
                fwd.start()
                passed.append(fwd)
        for u in units:
            copy(u, 0, sib, (x, y, c)).wait_recv()
        for u in units:
            for j, chip in enumerate(chips):
                copy(u, 4 + j, (*chip, 1 - c), (x, y, c)).wait_recv()
        for cp in first + passed:
            cp.wait_send()
        for cp in mine:
            cp.wait()

    return pl.pallas_call(
        body, name=name,
        out_shape=tuple(jax.ShapeDtypeStruct((N_DEV,) + a.shape, a.dtype) for a in arrs),
        in_specs=[_ANY] * n,
        out_specs=tuple([_ANY] * n),
        scratch_shapes=[pltpu.SemaphoreType.DMA((n, 7, n_pc)), pltpu.SemaphoreType.DMA((n, 7, n_pc)),
                        pltpu.SemaphoreType.DMA((n,))],
    )(*arrs)


def _pair_exchange(arrs, name):
    n = len(arrs)

    def body(*refs):
        ins = refs[:n]
        outs = refs[n:2 * n]
        send_sems, recv_sems = refs[2 * n:]
        x, y, c = _mesh_pos()
        copies = []
        for a in range(n):
            for q in range(4):
                cp = _rcopy(ins[a].at[2 * q + (1 - c)], outs[a].at[q], send_sems.at[a, q], recv_sems.at[a, q],
                            (x, y, 1 - c))
                cp.start()
                copies.append(cp)
        for cp in copies:
            cp.wait_recv()
        for cp in copies:
            cp.wait_send()

    return pl.pallas_call(
        body, name=name,
        out_shape=tuple(jax.ShapeDtypeStruct((4,) + a.shape[1:], a.dtype) for a in arrs),
        in_specs=[_ANY] * n,
        out_specs=tuple([_ANY] * n),
        scratch_shapes=[pltpu.SemaphoreType.DMA((n, 4)), pltpu.SemaphoreType.DMA((n, 4))],
    )(*arrs)


def _pair_add_call(parts, recv, c_idx, name):
    _, R, C = parts.shape
    (tr, tc), (steps,), idx = _tiling_2d(R, C, 1024)

    def body(c_ref, p_ref, r_ref, o_ref):
        o_ref[...] = (p_ref[...].astype(F32) + r_ref[...].astype(F32)).astype(o_ref.dtype)

    return pl.pallas_call(
        body, name=name,
        grid_spec=pltpu.PrefetchScalarGridSpec(
            num_scalar_prefetch=1,
            grid=(4, steps),
            in_specs=[pl.BlockSpec((None, tr, tc), lambda q, i, c_ref: (2 * q + c_ref[0],) + idx(i)),
                      pl.BlockSpec((None, tr, tc), lambda q, i, c_ref: (q,) + idx(i))],
            out_specs=pl.BlockSpec((None, tr, tc), lambda q, i, c_ref: (q,) + idx(i))),
        out_shape=jax.ShapeDtypeStruct((4, R, C), parts.dtype),
        compiler_params=_cparams(("arbitrary", "arbitrary")),
    )(c_idx, parts, recv)


def _flatten_blocks_call(blocks):
    n, R, C = blocks.shape
    tc = C // 2

    def body(in_ref, out_ref):
        for p in range(n):
            out_ref[p * R:(p + 1) * R, :] = in_ref[p]

    return pl.pallas_call(
        body, name="flatten_w",
        grid=(C // tc,),
        in_specs=[pl.BlockSpec((n, R, tc), lambda i: (0, 0, i))],
        out_specs=pl.BlockSpec((n * R, tc), lambda i: (0, i)),
        out_shape=jax.ShapeDtypeStruct((n * R, C), blocks.dtype),
        compiler_params=_cparams(("arbitrary",)),
    )(blocks)


def _parts_by_device_call(dmain, drank):
    D = dmain.shape[1]
    tc = D // 2

    def body(dm_ref, dr_ref, out_ref):
        for p in range(N_DEV):
            lo, hi = p * SHARD_COLS, (p + 1) * SHARD_COLS
            at = 0
            for src, a, b in ((dm_ref, lo, min(hi, RANK_COL)),
                              (dr_ref, max(lo, RANK_COL) - RANK_COL, min(hi, RANK_COL + GLA_RANK) - RANK_COL),
                              (dm_ref, max(lo, RANK_COL + GLA_RANK) - GLA_RANK, hi - GLA_RANK)):
                if b > a:
                    out_ref[p, at:at + (b - a), :] = src[a:b, :]
                    at += b - a

    return pl.pallas_call(
        body, name="parts_by_device",
        grid=(D // tc,),
        in_specs=[pl.BlockSpec((dmain.shape[0], tc), lambda i: (0, i)),
                  pl.BlockSpec((GLA_RANK, tc), lambda i: (0, i))],
        out_specs=pl.BlockSpec((N_DEV, SHARD_COLS, tc), lambda i: (0, 0, i)),
        out_shape=jax.ShapeDtypeStruct((N_DEV, SHARD_COLS, D), dmain.dtype),
        compiler_params=_cparams(("arbitrary",)),
    )(dmain, drank)


def _group_row(g):
    return GLA_RANK * (g * (1024 // GLA_RANK) + (g >= RANK_COL // 1024))


def _proj_call(x, norm_g, wt, wr, wp_part):
    T, D = x.shape
    tm = min(1024, T)
    assert tm % TBLK == 0
    n_i = T // tm

    def f_slot(j):
        return ((j >= 2).astype(jnp.int32) + (j >= 6).astype(jnp.int32)
                + (j >= 7).astype(jnp.int32) + (j >= 8).astype(jnp.int32))

    def b_slot(j):
        return (j >= 3).astype(jnp.int32) + (j >= 4).astype(jnp.int32) + (j >= 5).astype(jnp.int32)

    def body(x_ref, g_ref, w_ref, wr_ref, wp_ref, pf_ref, pb_ref, rank_ref, ht_ref, wpall_ref,
             h_scr, send_sems, recv_sems, loc_sem):
        i = pl.program_id(0)
        j = pl.program_id(1)
        own, pairs = _push_copies(wp_ref, wpall_ref, send_sems, recv_sems, loc_sem, scatter=False)

        @pl.when((i == 0) & (j == 0))
        def _():
            _push_start(own, pairs)

        @pl.when(j == 0)
        def _():
            xv = x_ref[...]
            r = lax.rsqrt(jnp.mean(xv * xv, axis=-1, keepdims=True) + EPS)
            h = (xv * r) * g_ref[...]
            hb = _bf(h)
            h_scr[...] = hb
            for b in range(tm // TBLK):
                ht_ref[b] = _bf(h[b * TBLK:(b + 1) * TBLK].T)
            rank_ref[...] = _dot_nt(hb, wr_ref[...])

        is_b = (j == 1) | ((j >= 3) & (j <= 5))

        @pl.when(is_b)
        def _():
            pb_ref[...] = _bf(_dot_nt(h_scr[...], w_ref[...]))

        @pl.when(jnp.logical_not(is_b))
        def _():
            pf_ref[...] = _dot_nt(h_scr[...], w_ref[...])

        @pl.when((i == n_i - 1) & (j == N_GROUPS - 1))
        def _():
            _push_wait(own, pairs)

    return pl.pallas_call(
        body, name="proj",
        grid=(n_i, N_GROUPS),
        in_specs=[pl.BlockSpec((tm, D), lambda i, j: (i, 0)),
                  pl.BlockSpec((1, D), lambda i, j: (0, 0)),
                  pl.BlockSpec((pl.Element(1024), pl.Element(D)), lambda i, j: (_group_row(j), 0)),
                  pl.BlockSpec((128, D), lambda i, j: (0, 0)),
                  _ANY],
        out_specs=(pl.BlockSpec((None, tm, 1024), lambda i, j: (f_slot(j), i, 0)),
                   pl.BlockSpec((None, tm, 1024), lambda i, j: (b_slot(j), i, 0)),
                   pl.BlockSpec((tm, 128), lambda i, j: (i, 0)),
                   pl.BlockSpec((tm // TBLK, D, TBLK), lambda i, j: (i, 0, 0)),
                   _ANY),
        out_shape=(jax.ShapeDtypeStruct((5, T, 1024), F32),
                   jax.ShapeDtypeStruct((4, T, 1024), BF16),
                   jax.ShapeDtypeStruct((T, 128), F32),
                   jax.ShapeDtypeStruct((T // TBLK, D, TBLK), BF16),
                   jax.ShapeDtypeStruct((N_DEV,) + wp_part.shape, wp_part.dtype)),
        scratch_shapes=[pltpu.VMEM((tm, D), BF16)] + _PUSH_SEMS,
        compiler_params=_cparams(("arbitrary", "arbitrary")),
    )(x, norm_g, wt, wr, wp_part)


GLA_STEP_CHUNKS = 4


def _gla_same_chunk(rows):
    return (_iota2(rows, rows, 0) & -GLA_CHUNK) == (_iota2(rows, rows, 1) & -GLA_CHUNK)


def _gla_chunk_terms(la, q, k, n_c):
    C = GLA_CHUNK
    rows = n_c * C
    low = _gla_same_chunk(rows) & (_iota2(rows, rows, 0) >= _iota2(rows, rows, 1))
    b = _tri_left(_bf(low.astype(F32)), la)
    bl = [b[(c + 1) * C - 1:(c + 1) * C, :] for c in range(n_c)]
    bl_rows = jnp.concatenate([jnp.broadcast_to(bl[c], (C, b.shape[1])) for c in range(n_c)], axis=0)
    eb = jnp.exp(b)
    enb = jnp.exp(-b)
    ebl_b = jnp.exp(bl_rows - b)
    scale = GLA_HK ** -0.5
    qe = q * eb * scale
    ke = k * enb
    kd = k * ebl_b
    return bl, eb, enb, ebl_b, qe, ke, kd


def _gla_fwd_call(projf, projb, rank, wdec, bdec):
    T = projf.shape[1]
    C = GLA_CHUNK
    n_chunks = T // C
    n_c = GLA_STEP_CHUNKS
    R = n_c * C
    assert n_chunks % n_c == 0

    def body(qk_ref, v_ref, rank_ref, wd_ref, bd_ref, o_ref, st_ref, la_ref, st_scr):
        @pl.when(pl.program_id(0) == 0)
        def _():
            st_scr[...] = jnp.zeros_like(st_scr)

        dec = _dot(_bf(rank_ref[...]), _bf(wd_ref[...])) + bd_ref[...]
        la = (jnp.minimum(dec, 0.0) - _softplus_neg_abs(dec)) / GLA_TAU
        la_ref[...] = la
        mask = _gla_same_chunk(R) & (_iota2(R, R, 0) >= _iota2(R, R, 1))
        bl, _, _, _, qe, ke, kd = _gla_chunk_terms(la, qk_ref[:, :GLA_DK], qk_ref[:, GLA_DK:], n_c)
        qeb, keb, kdb = _bf(qe), _bf(ke), _bf(kd)
        ebl = [jnp.exp(bl[c]) for c in range(n_c)]
        heads = range(GLA_HEADS)
        ks = [slice(hh * GLA_HK, (hh + 1) * GLA_HK) for hh in heads]
        vs = [slice(hh * GLA_HV, (hh + 1) * GLA_HV) for hh in heads]
        rs = [slice(c * C, (c + 1) * C) for c in range(n_c)]
        p = [_bf(jnp.where(mask, _dot_nt(qeb[:, ks[hh]], keb[:, ks[hh]]), 0.0)) for hh in heads]
        upd = [[_dot_tn(v_ref[rs[c], vs[hh]], kdb[rs[c], ks[hh]]) for hh in heads] for c in range(n_c)]
        intra = [_dot(p[hh], v_ref[:, vs[hh]]) for hh in heads]
        st = [st_scr[hh] for hh in heads]
        for c in range(n_c):
            inter = [_dot_nt(qeb[rs[c], ks[hh]], _bf(st[hh])) for hh in heads]
            for hh in heads:
                st_ref[c, hh] = st[hh]
                o_ref[rs[c], vs[hh]] = intra[hh][rs[c]] + inter[hh]
            st = [st[hh] * ebl[c][:, ks[hh]] + upd[c][hh] for hh in heads]
        for hh in heads:
            st_scr[hh] = st[hh]

    return pl.pallas_call(
        body, name="gla_fwd",
        grid=(n_chunks // n_c,),
        in_specs=[pl.BlockSpec((None, R, 1024), lambda n: (0, n, 0)),
                  pl.BlockSpec((None, R, 1024), lambda n: (0, n, 0)),
                  pl.BlockSpec((R, 128), lambda n: (n, 0)),
                  pl.BlockSpec((128, GLA_DK), lambda n: (0, 0)),
                  pl.BlockSpec((1, GLA_DK), lambda n: (0, 0))],
        out_specs=(pl.BlockSpec((R, 1024), lambda n: (n, 0)),
                   pl.BlockSpec((n_c, GLA_HEADS, GLA_HV, GLA_HK), lambda n: (n, 0, 0, 0)),
                   pl.BlockSpec((R, GLA_DK), lambda n: (n, 0))),
        out_shape=(jax.ShapeDtypeStruct((T, 1024), F32),
                   jax.ShapeDtypeStruct((n_chunks, GLA_HEADS, GLA_HV, GLA_HK), F32),
                   jax.ShapeDtypeStruct((T, GLA_DK), F32)),
        scratch_shapes=[pltpu.VMEM((GLA_HEADS, GLA_HV, GLA_HK), F32)],
        compiler_params=_cparams(("arbitrary",)),
    )(projf, projb, rank, wdec, bdec)


def _gla_bwd_call(projf, projb, la, do_gla, st_all, rank, wdec):
    T = projf.shape[1]
    C = GLA_CHUNK
    n_chunks = T // C
    n_c = GLA_STEP_CHUNKS
    R = n_c * C
    assert n_chunks % n_c == 0
    last = n_chunks // n_c - 1

    def body(qk_ref, v_ref, la_ref, do_ref, st_ref, rank_ref, wd_ref,
             dqk_ref, dv_ref, drank_ref, dwd_ref, dbd_ref, dst_scr):
        @pl.when(pl.program_id(0) == 0)
        def _():
            dst_scr[...] = jnp.zeros_like(dst_scr)
            dwd_ref[...] = jnp.zeros_like(dwd_ref)
            dbd_ref[...] = jnp.zeros_like(dbd_ref)

        same = _gla_same_chunk(R)
        mask = same & (_iota2(R, R, 0) >= _iota2(R, R, 1))
        upp = _bf((same & (_iota2(R, R, 0) <= _iota2(R, R, 1))).astype(F32))
        scale = GLA_HK ** -0.5
        la = la_ref[...]
        bl, eb, enb, ebl_b, qe, ke, kd = _gla_chunk_terms(la, qk_ref[:, :GLA_DK], qk_ref[:, GLA_DK:], n_c)
        qeb, keb, kdb = _bf(qe), _bf(ke), _bf(kd)
        ebl = [jnp.exp(bl[c]) for c in range(n_c)]
        heads = range(GLA_HEADS)
        ks = [slice(hh * GLA_HK, (hh + 1) * GLA_HK) for hh in heads]
        vs = [slice(hh * GLA_HV, (hh + 1) * GLA_HV) for hh in heads]
        rs = [slice(c * C, (c + 1) * C) for c in range(n_c)]
        v = [v_ref[:, vs[hh]] for hh in heads]
        do = [_bf(do_ref[:, vs[hh]]) for hh in heads]
        p = [_bf(jnp.where(mask, _dot_nt(qeb[:, ks[hh]], keb[:, ks[hh]]), 0.0)) for hh in heads]
        dp = [_bf(jnp.where(mask, _dot_nt(do[hh], v[hh]), 0.0)) for hh in heads]
        dst_intra = [[_dot_tn(do[hh][rs[c]], qeb[rs[c], ks[hh]]) for hh in heads] for c in range(n_c)]
        dqe_inter = [[_dot(do[hh][rs[c]], _bf(st_ref[c, hh])) for hh in heads] for c in range(n_c)]
        dv_intra = [_dot_tn(p[hh], do[hh]) for hh in heads]
        dqe_intra = [_dot(dp[hh], keb[:, ks[hh]]) for hh in heads]
        dke = jnp.concatenate([_dot_tn(dp[hh], qeb[:, ks[hh]]) for hh in heads], axis=1)
        dstn = [dst_scr[hh] for hh in heads]
        dkd_c, dv_inter, debl = [None] * n_c, [None] * n_c, [None] * n_c
        for c in reversed(range(n_c)):
            dstnb = [_bf(dstn[hh]) for hh in heads]
            dkd_c[c] = jnp.concatenate([_dot(v[hh][rs[c]], dstnb[hh]) for hh in heads], axis=1)
            dv_inter[c] = [_dot_nt(kdb[rs[c], ks[hh]], dstnb[hh]) for hh in heads]
            debl[c] = jnp.concatenate(
                [jnp.sum(dstn[hh] * st_ref[c, hh], axis=0, keepdims=True) for hh in heads], axis=1)
            dstn = [dst_intra[c][hh] + dstn[hh] * ebl[c][:, ks[hh]] for hh in heads]
        for hh in heads:
            dst_scr[hh] = dstn[hh]
            dv_ref[:, vs[hh]] = _bf(dv_intra[hh] + jnp.concatenate([dv_inter[c][hh] for c in range(n_c)], axis=0))
        dqe = jnp.concatenate(
            [dqe_intra[hh] + jnp.concatenate([dqe_inter[c][hh] for c in range(n_c)], axis=0) for hh in heads], axis=1)
        dkd = jnp.concatenate(dkd_c, axis=0)
        dkd_kd = dkd * kd
        db = dqe * qe - dke * ke - dkd_kd
        dbl = jnp.concatenate(
            [jnp.broadcast_to(jnp.sum(dkd_kd[rs[c]], axis=0, keepdims=True) + ebl[c] * debl[c], (C, GLA_DK))
             for c in range(n_c)], axis=0)
        dla = _tri_left(upp, db) + dbl
        dqk_ref[:, :GLA_DK] = _bf(dqe * eb * scale)
        dqk_ref[:, GLA_DK:] = _bf(dke * enb + dkd * ebl_b)
        ddec = dla * (1.0 / GLA_TAU) * (1.0 - jnp.exp(GLA_TAU * la))
        ddecb = _bf(ddec)
        drank_ref[...] = _bf(_dot_nt(ddecb, _bf(wd_ref[...])))
        dwd_ref[...] += _dot_tn(_bf(rank_ref[...]), ddecb)
        dbd_ref[...] += jnp.sum(ddec, axis=0, keepdims=True)

    return pl.pallas_call(
        body, name="gla_bwd",
        grid=(n_chunks // n_c,),
        in_specs=[pl.BlockSpec((None, R, 1024), lambda n: (0, last - n, 0)),
                  pl.BlockSpec((None, R, 1024), lambda n: (0, last - n, 0)),
                  pl.BlockSpec((R, GLA_DK), lambda n: (last - n, 0)),
                  pl.BlockSpec((R, 1024), lambda n: (last - n, 0)),
                  pl.BlockSpec((n_c, GLA_HEADS, GLA_HV, GLA_HK), lambda n: (last - n, 0, 0, 0)),
                  pl.BlockSpec((R, 128), lambda n: (last - n, 0)),
                  pl.BlockSpec((128, GLA_DK), lambda n: (0, 0))],
        out_specs=(pl.BlockSpec((R, 1024), lambda n: (last - n, 0)),
                   pl.BlockSpec((R, 1024), lambda n: (last - n, 0)),
                   pl.BlockSpec((R, 128), lambda n: (last - n, 0)),
                   pl.BlockSpec((128, GLA_DK), lambda n: (0, 0)),
                   pl.BlockSpec((1, GLA_DK), lambda n: (0, 0))),
        out_shape=(jax.ShapeDtypeStruct((T, 1024), BF16),
                   jax.ShapeDtypeStruct((T, 1024), BF16),
                   jax.ShapeDtypeStruct((T, 128), BF16),
                   jax.ShapeDtypeStruct((128, GLA_DK), F32),
                   jax.ShapeDtypeStruct((1, GLA_DK), F32)),
        scratch_shapes=[pltpu.VMEM((GLA_HEADS, GLA_HV, GLA_HK), F32)],
        compiler_params=_cparams(("arbitrary",)),
    )(projf, projb, la, do_gla, st_all, rank, wdec)


def _sb_logs(z):
    lsz = jnp.minimum(z, 0.0) - _softplus_neg_abs(z)
    return lsz, lsz - z


SB_HG_FWD = 8
SB_HG_BWD = 4
SB_QUERIES = 256
SB_KEYS = 256
SB_DEAD = -105.0


def _sb_fwd_call(projb, wp_shard):
    T = projb.shape[1]
    B = min(SB_QUERIES, T)
    HG = SB_HG_FWD
    W = HG * SB_HD
    scale = 1.0 / math.sqrt(SB_HD)
    KB = min(SB_KEYS, T)
    n_h, n_i = SB_HEADS // HG, T // B

    def body(q_ref, k_ref, v_ref, wp_ref, o_ref, wpall_ref, cb_scr, send_sems, recv_sems, loc_sem):
        i = pl.program_id(1)
        own, pairs = _push_copies(wp_ref, wpall_ref, send_sems, recv_sems, loc_sem, scatter=False)

        @pl.when((pl.program_id(0) == 0) & (i == 0))
        def _():
            _push_start(own, pairs)

        rows = HG * B
        after = (_iota2(KB, KB, 0) > _iota2(KB, KB, 1)).astype(F32)
        tri = _bf(jnp.concatenate([after, jnp.ones((KB, KB), F32)], axis=1))
        o_ref[...] = jnp.zeros_like(o_ref)
        cb_scr[...] = jnp.zeros_like(cb_scr)

        def block(jp, masked):
            off = pl.multiple_of(jp * KB, KB)
            z = jnp.concatenate(
                [_dot_nt(q_ref[:, hh * SB_HD:(hh + 1) * SB_HD], k_ref[pl.ds(off, KB), hh * SB_HD:(hh + 1) * SB_HD])
                 for hh in range(HG)], axis=0) * scale
            lsz, l1m = _sb_logs(z)
            if masked:
                strict = (jp * KB + _iota2(rows, KB, 1)) < (i * B + (_iota2(rows, KB, 0) & (B - 1)))
                l1m = jnp.where(strict, l1m, 0.0)
            r = _tri2_right(l1m, tri)
            cb = cb_scr[...]
            a = jnp.exp(lsz + cb + r[:, :KB])
            if masked:
                a = jnp.where(strict, a, 0.0)
            cb_scr[...] = cb + r[:, KB:]
            ab = _bf(a)
            for hh in range(HG):
                cs = slice(hh * SB_HD, (hh + 1) * SB_HD)
                o_ref[:, cs] += _dot(ab[hh * B:(hh + 1) * B, :], v_ref[pl.ds(off, KB), cs])

        jp0 = (i * B) // KB
        block(jp0, True)

        def live(state):
            jj, dead = state
            return (jj <= jp0) & jnp.logical_not(dead)

        def step(state):
            jj, _ = state
            block(jp0 - jj, False)
            return jj + 1, jnp.max(cb_scr[:, :SB_HD]) < SB_DEAD

        lax.while_loop(live, step, (jnp.int32(1), jnp.max(cb_scr[:, :SB_HD]) < SB_DEAD))

        @pl.when((pl.program_id(0) == n_h - 1) & (i == n_i - 1))
        def _():
            _push_wait(own, pairs)

    return pl.pallas_call(
        body, name="sb_fwd",
        grid=(n_h, n_i),
        in_specs=[pl.BlockSpec((None, B, W), lambda h, i: (1, i, h)),
                  pl.BlockSpec((None, T, W), lambda h, i: (2, 0, h)),
                  pl.BlockSpec((None, T, W), lambda h, i: (3, 0, h)),
                  _ANY],
        out_specs=(pl.BlockSpec((B, W), lambda h, i: (i, h)), _ANY),
        out_shape=(jax.ShapeDtypeStruct((T, 1024), F32),
                   jax.ShapeDtypeStruct((N_DEV,) + wp_shard.shape, wp_shard.dtype)),
        scratch_shapes=[pltpu.VMEM((HG * B, KB), F32)] + _PUSH_SEMS,
        compiler_params=_cparams(("arbitrary", "arbitrary")),
    )(projb, projb, projb, wp_shard)


def _sb_bwd_call(projb, do_sb, g_p):
    T = projb.shape[1]
    B = min(SB_QUERIES, T)
    nb = T // B
    HG = SB_HG_BWD
    W = HG * SB_HD
    WQ = HG * B
    KB = min(SB_KEYS, T)
    nkb = T // KB
    n_h = SB_HEADS // HG
    scale = 1.0 / math.sqrt(SB_HD)

    def body(q_ref, k_ref, v_ref, do_ref, gp_ref, dq_ref, dk_ref, dv_ref, rp_ref,
             dk_scr, dv_scr, kt_scr, beta_scr, g_scr, dqt_scr, send_sems, recv_sems, loc_sem):
        i = pl.program_id(1)
        own, pairs = _push_copies(gp_ref, rp_ref, send_sems, recv_sems, loc_sem, scatter=True)

        @pl.when((pl.program_id(0) == 0) & (i == 0))
        def _():
            _push_start(own, pairs)

        @pl.when(i == 0)
        def _():
            dk_scr[...] = jnp.zeros_like(dk_scr)
            dv_scr[...] = jnp.zeros_like(dv_scr)
            for hh in range(HG):
                for jb in range(nkb):
                    kt_scr[hh, jb] = _bf(
                        k_ref[jb * KB:(jb + 1) * KB, hh * SB_HD:(hh + 1) * SB_HD].astype(F32).T)

        dqt_scr[...] = jnp.zeros_like(dqt_scr)
        later = _bf((_iota2(KB, KB, 1) > _iota2(KB, KB, 0)).astype(F32))
        earlier = _bf((_iota2(KB, KB, 1) < _iota2(KB, KB, 0)).astype(F32))
        dob = _bf(do_ref[...])
        jp0 = (i * B) // KB

        def strict_mask():
            return (jp0 * KB + _iota2(KB, WQ, 0)) < (i * B + (_iota2(KB, WQ, 1) & (B - 1)))

        def heads(fn):
            return [fn(slice(hh * SB_HD, (hh + 1) * SB_HD)) for hh in range(HG)]

        def pass1(jp, cb, masked):
            off = pl.multiple_of(jp * KB, KB)
            z = jnp.concatenate(heads(lambda cs: _dot_nt(k_ref[pl.ds(off, KB), cs], q_ref[:, cs])), axis=1) * scale
            da = jnp.concatenate(heads(lambda cs: _dot_nt(v_ref[pl.ds(off, KB), cs], dob[:, cs])), axis=1)
            lsz, l1m = _sb_logs(z)
            if masked:
                strict = strict_mask()
                l1m = jnp.where(strict, l1m, 0.0)
            a = jnp.exp(lsz + cb + _tri2_left(later, l1m))
            if masked:
                a = jnp.where(strict, a, 0.0)
            g_scr[jp] = a * da
            beta_scr[jp] = jnp.exp(lsz)
            ab = _bf(a)
            for hh in range(HG):
                cs = slice(hh * SB_HD, (hh + 1) * SB_HD)
                dv_scr[pl.ds(off, KB), cs] += _dot(ab[:, hh * B:(hh + 1) * B], dob[:, cs])
            return cb + jnp.sum(l1m, axis=0, keepdims=True)

        zero = jnp.zeros((1, WQ), F32)
        cb = pass1(jp0, zero, True)

        def live(state):
            jj, _, dead = state
            return (jj <= jp0) & jnp.logical_not(dead)

        def step(state):
            jj, cr, _ = state
            cr = pass1(jp0 - jj, cr, False)
            return jj + 1, cr, jnp.max(cr) < SB_DEAD

        n_done, _, _ = lax.while_loop(live, step, (jnp.int32(1), cb, jnp.max(cb) < SB_DEAD))
        jp_first = jp0 - (n_done - 1)

        def pass2(jp, cg, masked):
            off = pl.multiple_of(jp * KB, KB)
            g = g_scr[jp]
            beta = beta_scr[jp]
            dz = g * (1.0 - beta) - beta * (cg + _tri2_left(earlier, g))
            if masked:
                dz = jnp.where(strict_mask(), dz, 0.0)
            dzb = _bf(dz * scale)
            for hh in range(HG):
                cs = slice(hh * SB_HD, (hh + 1) * SB_HD)
                dk_scr[pl.ds(off, KB), cs] += _dot(dzb[:, hh * B:(hh + 1) * B], q_ref[:, cs])
                dqt_scr[hh] += _dot(kt_scr[hh, jp], dzb[:, hh * B:(hh + 1) * B])
            return cg + jnp.sum(g, axis=0, keepdims=True)

        cg = lax.fori_loop(jp_first, jp0, lambda jp, cr: pass2(jp, cr, False), zero)
        pass2(jp0, cg, True)
        for hh in range(HG):
            dq_ref[:, hh * SB_HD:(hh + 1) * SB_HD] = _bf(dqt_scr[hh].T)

        @pl.when(i == nb - 1)
        def _():
            dk_ref[...] = _bf(dk_scr[...])
            dv_ref[...] = _bf(dv_scr[...])

        @pl.when((pl.program_id(0) == n_h - 1) & (i == nb - 1))
        def _():
            _push_wait(own, pairs)

    return pl.pallas_call(
        body, name="sb_bwd",
        grid=(n_h, nb),
        in_specs=[pl.BlockSpec((None, B, W), lambda h, i: (1, i, h)),
                  pl.BlockSpec((None, T, W), lambda h, i: (2, 0, h)),
                  pl.BlockSpec((None, T, W), lambda h, i: (3, 0, h)),
                  pl.BlockSpec((B, W), lambda h, i: (i, h)),
                  _ANY],
        out_specs=(pl.BlockSpec((B, W), lambda h, i: (i, h)),
                   pl.BlockSpec((T, W), lambda h, i: (0, h)),
                   pl.BlockSpec((T, W), lambda h, i: (0, h)),
                   _ANY),
        out_shape=(jax.ShapeDtypeStruct((T, 1024), BF16),
                   jax.ShapeDtypeStruct((T, 1024), BF16),
                   jax.ShapeDtypeStruct((T, 1024), BF16),
                   jax.ShapeDtypeStruct(g_p.shape, g_p.dtype)),
        scratch_shapes=[pltpu.VMEM((T, W), F32), pltpu.VMEM((T, W), F32),
                        pltpu.VMEM((HG, nkb, SB_HD, KB), BF16),
                        pltpu.VMEM((nkb, KB, WQ), F32), pltpu.VMEM((nkb, KB, WQ), F32),
                        pltpu.VMEM((HG, SB_HD, B), F32)] + _PUSH_SEMS,
        compiler_params=_cparams(("arbitrary", "arbitrary")),
    )(projb, projb, projb, do_sb, g_p)


def _mid_call(o_gla, o_sb, projf, x, target, wpa, wpb, wo, gla_g, b_gate, final_g):
    T, D = x.shape
    tm = min(TBLK, T)

    def body(og_ref, ggate_ref, osb_ref, sgate_ref, ma_ref, mb_ref, x_ref, tgt_ref,
             wpa_ref, wpb_ref, wo_ref, glag_ref, bg_ref, fg_ref,
             dx2_ref, dogla_ref, dosb_ref, dggate_ref, dsgate_ref, dm_ref,
             mt_ref, ogt_ref, obt_ref, dx2b_ref, dya_ref, dyb_ref,
             dfg_ref, dbg_ref, dglag_ref, loss_ref):
        @pl.when(pl.program_id(0) == 0)
        def _():
            dfg_ref[...] = jnp.zeros_like(dfg_ref)
            dbg_ref[...] = jnp.zeros_like(dbg_ref)
            dglag_ref[...] = jnp.zeros_like(dglag_ref)
            loss_ref[...] = jnp.zeros_like(loss_ref)

        glag = glag_ref[...]
        ggate = ggate_ref[...]
        sg = _sigmoid(ggate)
        silu_g = ggate * sg
        ohat, rinv, nrm = [], [], []
        for hh in range(GLA_HEADS):
            oh = og_ref[:, hh * GLA_HV:(hh + 1) * GLA_HV]
            r = lax.rsqrt(jnp.mean(oh * oh, axis=-1, keepdims=True) + EPS)
            ohat.append(oh * r)
            rinv.append(r)
            nrm.append(ohat[-1] * glag)
        n_all = jnp.concatenate(nrm, axis=1)
        og = n_all * silu_g
        ogb = _bf(og)
        ya = _dot(ogb, wpa_ref[...])
        sgate = sgate_ref[...]
        ss = _sigmoid(sgate)
        silu_s = sgate * ss
        osb = osb_ref[...]
        ob = osb * silu_s
        obb = _bf(ob)
        yb = _dot(obb, wpb_ref[...])
        ga = _sigmoid(ma_ref[...] + bg_ref[:, :D])
        gb = _sigmoid(mb_ref[...] + bg_ref[:, D:])
        merged = ga * ya + gb * yb
        mgb = _bf(merged)
        x2 = x_ref[...] + _dot(mgb, wo_ref[...])
        r2 = lax.rsqrt(jnp.mean(x2 * x2, axis=-1, keepdims=True) + EPS)
        xh2 = x2 * r2
        fg = fg_ref[...]
        err = xh2 * fg - tgt_ref[...]
        loss_ref[...] += jnp.broadcast_to(
            0.5 * jnp.sum(jnp.mean(err * err, axis=-1, keepdims=True), axis=0, keepdims=True), (1, 128))
        dy = err * (1.0 / D)
        dfg_ref[...] += jnp.sum(dy * xh2, axis=0, keepdims=True)
        dxh = dy * fg
        dx2 = r2 * (dxh - xh2 * jnp.mean(dxh * xh2, axis=-1, keepdims=True))
        dx2_ref[...] = dx2
        dx2b = _bf(dx2)
        dx2b_ref[...] = dx2b
        dmerged = _dot_nt(dx2b, wo_ref[...])
        dya = dmerged * ga
        dyb = dmerged * gb
        dma = dmerged * ya * ga * (1.0 - ga)
        dmb = dmerged * yb * gb * (1.0 - gb)
        dm_ref[:, :D] = _bf(dma)
        dm_ref[:, D:] = _bf(dmb)
        dbg_ref[:, :D] += jnp.sum(dma, axis=0, keepdims=True)
        dbg_ref[:, D:] += jnp.sum(dmb, axis=0, keepdims=True)
        dyab = _bf(dya)
        dybb = _bf(dyb)
        dya_ref[...] = dyab
        dyb_ref[...] = dybb
        dog = _dot_nt(dyab, wpa_ref[...])
        dob = _dot_nt(dybb, wpb_ref[...])
        dosb_ref[...] = dob * silu_s
        dsgate_ref[...] = _bf(dob * osb * (ss * (1.0 + sgate * (1.0 - ss))))
        dn = dog * silu_g
        dggate_ref[...] = _bf(dog * n_all * (sg * (1.0 + ggate * (1.0 - sg))))
        dglag = jnp.zeros((1, GLA_HV), F32)
        for hh in range(GLA_HEADS):
            dnh = dn[:, hh * GLA_HV:(hh + 1) * GLA_HV]
            dglag = dglag + jnp.sum(dnh * ohat[hh], axis=0, keepdims=True)
            dohat = dnh * glag
            dogla_ref[:, hh * GLA_HV:(hh + 1) * GLA_HV] = rinv[hh] * (
                dohat - ohat[hh] * jnp.mean(dohat * ohat[hh], axis=-1, keepdims=True))
        dglag_ref[...] += dglag
        mt_ref[...] = _bf(merged.T)
        ogt_ref[...] = _bf(og.T)
        obt_ref[...] = _bf(ob.T)

    row = lambda i: (i, 0)
    const = lambda i: (0, 0)
    tile = pl.BlockSpec((tm, D), row)
    tile_t = pl.BlockSpec((None, D, tm), lambda i: (i, 0, 0))
    wspec = pl.BlockSpec((D, D), const)
    return pl.pallas_call(
        body, name="mid",
        grid=(T // tm,),
        in_specs=[tile,
                  pl.BlockSpec((None, tm, D), lambda i: (1, i, 0)),
                  tile,
                  pl.BlockSpec((None, tm, D), lambda i: (2, i, 0)),
                  pl.BlockSpec((None, tm, D), lambda i: (3, i, 0)),
                  pl.BlockSpec((None, tm, D), lambda i: (4, i, 0)),
                  tile, tile, wspec, wspec, wspec,
                  pl.BlockSpec((1, GLA_HV), const),
                  pl.BlockSpec((1, 2 * D), const),
                  pl.BlockSpec((1, D), const)],
        out_specs=(tile, tile, tile, tile, tile,
                   pl.BlockSpec((tm, 2 * D), row),
                   tile_t, tile_t, tile_t, tile, tile, tile,
                   pl.BlockSpec((1, D), const),
                   pl.BlockSpec((1, 2 * D), const),
                   pl.BlockSpec((1, GLA_HV), const),
                   pl.BlockSpec((1, 128), const)),
        out_shape=(jax.ShapeDtypeStruct((T, D), F32),
                   jax.ShapeDtypeStruct((T, D), F32),
                   jax.ShapeDtypeStruct((T, D), F32),
                   jax.ShapeDtypeStruct((T, D), BF16),
                   jax.ShapeDtypeStruct((T, D), BF16),
                   jax.ShapeDtypeStruct((T, 2 * D), BF16),
                   jax.ShapeDtypeStruct((T // tm, D, tm), BF16),
                   jax.ShapeDtypeStruct((T // tm, D, tm), BF16),
                   jax.ShapeDtypeStruct((T // tm, D, tm), BF16),
                   jax.ShapeDtypeStruct((T, D), BF16),
                   jax.ShapeDtypeStruct((T, D), BF16),
                   jax.ShapeDtypeStruct((T, D), BF16),
                   jax.ShapeDtypeStruct((1, D), F32),
                   jax.ShapeDtypeStruct((1, 2 * D), F32),
                   jax.ShapeDtypeStruct((1, GLA_HV), F32),
                   jax.ShapeDtypeStruct((1, 128), F32)),
        compiler_params=_cparams(("arbitrary",)),
    )(o_gla, projf, o_sb, projf, projf, projf, x, target, wpa, wpb, wo, gla_g, b_gate, final_g)


def _dh_call(pieces, dmlog, drank, wt, wr, x, dx2, norm_g, s_in):
    T, D = x.shape
    tm = min(256, T)
    npc = len(pieces)
    n_main = N_GROUPS * 1024
    n_i = T // tm

    def body(*refs):
        pcs = refs[:npc]
        (dm_ref, dr_ref, w_hbm, wr_ref, x_ref, dx2_ref, g_ref, sin_ref,
         gx_ref, dg_ref, rin_ref, w_scr, sems, send_sems, recv_sems, loc_sem) = refs[npc:]
        own, pairs = _chip_copies(sin_ref, rin_ref, send_sems, recv_sems, loc_sem)

        @pl.when(pl.program_id(0) == 0)
        def _():
            _push_start(own, pairs)
            lo = pltpu.make_async_copy(w_hbm.at[pl.ds(0, RANK_COL)], w_scr.at[pl.ds(0, RANK_COL)], sems.at[0])
            hi = pltpu.make_async_copy(w_hbm.at[pl.ds(RANK_COL + GLA_RANK, n_main - RANK_COL)],
                                       w_scr.at[pl.ds(RANK_COL, n_main - RANK_COL)], sems.at[1])
            lo.start()
            hi.start()
            dg_ref[...] = jnp.zeros_like(dg_ref)
            lo.wait()
            hi.wait()

        def w_group(g):
            return w_scr[g * 1024:(g + 1) * 1024, :]

        dr = dr_ref[...]
        dh = _dot(dr, wr_ref[...])
        for g in range(npc):
            dh = dh + _dot(pcs[g][...], w_group(g))
        dh = dh + _dot(dm_ref[:, :D], w_group(npc))
        dh = dh + _dot(dm_ref[:, D:], w_group(npc + 1))
        xv = x_ref[...]
        r = lax.rsqrt(jnp.mean(xv * xv, axis=-1, keepdims=True) + EPS)
        xhat = xv * r
        g = g_ref[...]
        dg_ref[...] += jnp.sum(dh * xhat, axis=0, keepdims=True)
        dxhat = dh * g
        gx_ref[...] = r * (dxhat - xhat * jnp.mean(dxhat * xhat, axis=-1, keepdims=True)) + dx2_ref[...]

        @pl.when(pl.program_id(0) == n_i - 1)
        def _():
            _push_wait(own, pairs)

    row = lambda i: (i, 0)
    const = lambda i: (0, 0)
    tile = pl.BlockSpec((tm, D), row)
    return pl.pallas_call(
        body, name="dh",
        grid=(n_i,),
        in_specs=[tile] * npc + [
            pl.BlockSpec((tm, 2 * D), row),
            pl.BlockSpec((tm, 128), row),
            _ANY,
            pl.BlockSpec((128, D), const),
            tile, tile,
            pl.BlockSpec((1, D), const),
            _ANY],
        out_specs=(tile, pl.BlockSpec((1, D), const), _ANY),
        out_shape=(jax.ShapeDtypeStruct((T, D), F32),
                   jax.ShapeDtypeStruct((1, D), F32),
                   jax.ShapeDtypeStruct(s_in.shape, s_in.dtype)),
        scratch_shapes=[pltpu.VMEM((n_main, D), BF16), pltpu.SemaphoreType.DMA((2,))] + _CHIP_SEMS,
        compiler_params=_cparams(("arbitrary",)),
    )(*pieces, dmlog, drank, wt, wr, x, dx2, norm_g, s_in)


def _wgrad_rank_call(ht, drank):
    n_tb, D, tb = ht.shape

    def body(ht_ref, dr_ref, o_ref):
        @pl.when(pl.program_id(0) == 0)
        def _():
            o_ref[...] = jnp.zeros_like(o_ref)

        o_ref[...] += _dot(ht_ref[...], dr_ref[...])

    return pl.pallas_call(
        body, name="wgrad_rank",
        grid=(n_tb,),
        in_specs=[pl.BlockSpec((None, D, tb), lambda i: (i, 0, 0)),
                  pl.BlockSpec((tb, 128), lambda i: (i, 0))],
        out_specs=pl.BlockSpec((D, 128), lambda i: (0, 0)),
        out_shape=jax.ShapeDtypeStruct((D, 128), F32),
        compiler_params=_cparams(("arbitrary",)),
    )(ht, drank)


def _wgrad_call(lhs_list, lhs_of_group, rhs_list, rhs_of_group, n_transposed, name):
    n_groups = len(rhs_of_group)
    n_tb, D, tb = lhs_list[0].shape
    T = n_tb * tb
    per = min(4, n_tb)
    tk = per * tb
    nk = T // tk
    nl = len(lhs_list)

    def body(*refs):
        lhs = refs[:nl]
        rhs = refs[nl:nl + n_groups]
        out_ref, acc = refs[nl + n_groups:]
        g = pl.program_id(0)
        i = pl.program_id(1)

        @pl.when(i == 0)
        def _():
            acc[...] = jnp.zeros_like(acc)

        for p in range(n_groups):
            @pl.when(g == p)
            def _(p=p):
                lref = lhs[lhs_of_group[p]]
                part = _dot(lref[0], rhs[p][0:tb, :])
                for b in range(1, per):
                    part = part + _dot(lref[b], rhs[p][b * tb:(b + 1) * tb, :])
                acc[...] += part

        @pl.when((i == nk - 1) & (g < n_transposed))
        def _():
            out_ref[...] = _bf(acc[...].T)

        @pl.when((i == nk - 1) & (g >= n_transposed))
        def _():
            out_ref[...] = _bf(acc[...])

    def lhs_spec(a):
        groups = [g for g in range(n_groups) if lhs_of_group[g] == a]
        lo, hi = min(groups), max(groups)
        assert groups == list(range(lo, hi + 1))
        return pl.BlockSpec((per, D, tb), lambda g, i: (jnp.where((g >= lo) & (g <= hi), i, 0), 0, 0))

    def rhs_spec(p):
        cb = rhs_of_group[p][1]
        return pl.BlockSpec((tk, 1024), lambda g, i: (jnp.where(g == p, i, 0), cb))

    return pl.pallas_call(
        body, name=name,
        grid=(n_groups, nk),
        in_specs=[lhs_spec(a) for a in range(nl)] + [rhs_spec(p) for p in range(n_groups)],
        out_specs=pl.BlockSpec((None, D, 1024), lambda g, i: (g, 0, 0)),
        out_shape=jax.ShapeDtypeStruct((n_groups, D, 1024), BF16),
        scratch_shapes=[pltpu.VMEM((D, 1024), F32)],
        compiler_params=_cparams(("arbitrary", "arbitrary")),
    )(*lhs_list, *[rhs_list[rhs_of_group[p][0]] for p in range(n_groups)])


def _adamw_math(parts, w, m, v):
    g = parts[0].astype(F32)
    for p in parts[1:]:
        g = g + p.astype(F32)
    mm = ADAM_B1 * m + (1.0 - ADAM_B1) * g
    vv = ADAM_B2 * v + (1.0 - ADAM_B2) * (g * g)
    m_hat = mm / (1.0 - ADAM_B1 ** ADAM_STEP)
    v_hat = vv / (1.0 - ADAM_B2 ** ADAM_STEP)
    return g, -ADAM_LR * (m_hat / (jnp.sqrt(v_hat) + ADAM_EPS) + ADAM_WD * w), mm, vv


def _part_order(n_parts):
    return [n_parts - 1] + list(range(n_parts - 1))


def _adamw_call(parts, w, m, v, name):
    R, C = w.shape
    n_parts = parts.shape[0]
    (tr, tc), grid, idx = _tiling_2d(R, C, 512)

    def body(p_ref, w_ref, m_ref, v_ref, g_ref, d_ref, nm_ref, nv_ref):
        g_ref[...], d_ref[...], nm_ref[...], nv_ref[...] = _adamw_math(
            [p_ref[k] for k in _part_order(n_parts)], w_ref[...], m_ref[...], v_ref[...])

    blk = pl.BlockSpec((tr, tc), idx)
    sds = jax.ShapeDtypeStruct((R, C), F32)
    return pl.pallas_call(
        body, name=name,
        grid=grid,
        in_specs=[pl.BlockSpec((n_parts, tr, tc), lambda i: (0,) + idx(i)), blk, blk, blk],
        out_specs=(blk, blk, blk, blk),
        out_shape=(sds, sds, sds, sds),
        compiler_params=_cparams(("arbitrary",)),
    )(parts, w, m, v)


def _adamw_rows_call(parts, ws, ms, vs, name, gathered):
    n = len(ws)
    R, C = ws[0].shape
    n_parts = parts.shape[0]

    def body(*refs):
        p_ref = refs[0]
        w_refs, m_refs, v_refs = refs[1:1 + n], refs[1 + n:1 + 2 * n], refs[1 + 2 * n:1 + 3 * n]
        src_ref = refs[1 + 3 * n]
        outs = refs[2 + 3 * n:2 + 7 * n]
        dst_ref, send_sems, recv_sems, loc_sem = refs[2 + 7 * n:]
        own, pairs = _push_copies(src_ref, dst_ref, send_sems, recv_sems, loc_sem, scatter=False)
        k_now = pl.program_id(0)

        @pl.when(k_now == 0)
        def _():
            _push_start(own, pairs)

        for k in range(n):
            @pl.when(k_now == k)
            def _(k=k):
                res = _adamw_math([p_ref[j] for j in _part_order(n_parts)],
                                  w_refs[k][...], m_refs[k][...], v_refs[k][...])
                for o_ref, val in zip(outs[4 * k:4 * k + 4], res):
                    o_ref[...] = val

        @pl.when(k_now == n - 1)
        def _():
            _push_wait(own, pairs)

    whole = pl.BlockSpec((R, C), lambda k: (0, 0))
    sds = jax.ShapeDtypeStruct((R, C), F32)
    res = pl.pallas_call(
        body, name=name,
        grid=(n,),
        in_specs=[pl.BlockSpec((n_parts, R, C), lambda k: (0, k, 0))] + [whole] * (3 * n) + [_ANY],
        out_specs=tuple([whole] * (4 * n) + [_ANY]),
        out_shape=tuple([sds] * (4 * n) + [jax.ShapeDtypeStruct((N_DEV,) + gathered.shape, gathered.dtype)]),
        scratch_shapes=_PUSH_SEMS,
        compiler_params=_cparams(("arbitrary",)),
    )(parts, *ws, *ms, *vs, gathered)
    return [res[4 * k:4 * k + 4] for k in range(n)], res[4 * n]


def _adamw_lanes_call(parts, offsets, ws, ms, vs, name):
    n = len(ws)
    n_parts = parts.shape[0]

    def body(*refs):
        p_ref = refs[0]
        w_refs, m_refs, v_refs = refs[1:1 + n], refs[1 + n:1 + 2 * n], refs[1 + 2 * n:1 + 3 * n]
        outs = refs[1 + 3 * n:]
        for k in range(n):
            lanes = slice(offsets[k], offsets[k] + ws[k].shape[1])
            res = _adamw_math([p_ref[j, :, lanes] for j in _part_order(n_parts)],
                              w_refs[k][...], m_refs[k][...], v_refs[k][...])
            for o_ref, val in zip(outs[4 * k:4 * k + 4], res):
                o_ref[...] = val

    res = pl.pallas_call(
        body, name=name,
        out_shape=tuple(jax.ShapeDtypeStruct(ws[k].shape, F32) for k in range(n) for _ in range(4)),
        compiler_params=_cparams(),
    )(parts, *ws, *ms, *vs)
    return [res[4 * k:4 * k + 4] for k in range(n)]


def _local_step(x, target, wt, wr, wdec, bdec, wp_shard, norm_g, gla_g, b_gate, final_g):
    D = x.shape[1]
    half = wp_shard.shape[1] // 2
    projf, projb, rank, ht, wp_lo = _proj_call(x, norm_g, wt, wr, wp_shard[:, :half])
    o_gla, st_all, la = _gla_fwd_call(projf, projb, rank, wdec, bdec)
    o_sb, wp_hi = _sb_fwd_call(projb, wp_shard[:, half:])
    wp_full = jnp.concatenate([wp_lo, wp_hi], axis=2).transpose(1, 0, 2, 3).reshape(3, D, D)
    (dx2, do_gla, do_sb, dggate, dsgate, dmlog, mt, ogt, obt, dx2b, dya, dyb,
     dfinal_g, db_gate, dgla_g, loss) = _mid_call(o_gla, o_sb, projf, x, target, wp_full[0], wp_full[1],
                                                 wp_full[2], gla_g, b_gate, final_g)
    dw_p = _wgrad_call([ogt, obt, mt], [0, 1, 2], [dya, dyb, dx2b], [(0, 0), (1, 0), (2, 0)], 0, "wgrad_p")
    g_p = dw_p.reshape(3, N_DEV, D // N_DEV, D).transpose(1, 0, 2, 3).reshape(N_DEV, 3 * (D // N_DEV), D)
    dqk, dgv, drank, dwdec, dbdec = _gla_bwd_call(projf, projb, la, do_gla, st_all, rank, wdec)
    dsq, dsk, dsv, r_p = _sb_bwd_call(projb, do_sb, g_p)
    pieces = [dqk, dgv, dggate, dsq, dsk, dsv, dsgate]
    rhs_of_group = [(g, 0) for g in range(7)] + [(7, 0), (7, 1)]
    dw_in = _wgrad_call([ht], [0] * N_GROUPS, pieces + [dmlog], rhs_of_group, N_GROUPS, "wgrad_in")
    dwr = _wgrad_rank_call(ht, drank)
    g_in = _parts_by_device_call(dw_in.reshape(N_GROUPS * 1024, D), dwr[:, :GLA_RANK].T.astype(BF16))
    c_idx = lax.axis_index("c").astype(jnp.int32).reshape(1)
    (p_in,) = _pair_exchange([g_in], "pair_g")
    s_in = _pair_add_call(g_in, p_in, c_idx, "pair_add_in")
    grad_x, dnorm_g, r_in = _dh_call(pieces, dmlog, drank, wt, wr, x, dx2, norm_g, s_in)
    small = jnp.concatenate([
        dnorm_g.reshape(-1), dbdec.reshape(-1), dgla_g.reshape(-1), db_gate.reshape(-1), dfinal_g.reshape(-1),
        loss.reshape(-1), dwdec[:GLA_RANK].reshape(-1)]).reshape(1, _SM_LEN)
    return grad_x, r_in, r_p, small


_SM_NORM = 0
_SM_BDEC = _SM_NORM + D_MODEL
_SM_GLAG = _SM_BDEC + GLA_DK
_SM_BGATE = _SM_GLAG + GLA_HV
_SM_FINAL = _SM_BGATE + 2 * D_MODEL
_SM_REPL = _SM_FINAL + D_MODEL
_SM_LOSS = _SM_REPL
_SM_WDEC = _SM_LOSS + 128
_SM_LEN = _SM_WDEC + GLA_RANK * GLA_DK


def kernel(x, norm_g, w_in, w_dec_up, b_dec, gla_norm_g, w_pa, w_pb, b_gate, w_o, final_g, loss_target, m_norm_g, m_w_in, m_w_dec_up, m_b_dec, m_gla_norm_g, m_w_pa, m_w_pb, m_b_gate, m_w_o, m_final_g, v_norm_g, v_w_in, v_w_dec_up, v_b_dec, v_gla_norm_g, v_w_pa, v_w_pb, v_b_gate, v_w_o, v_final_g):
    D = D_MODEL
    me = 4 * lax.axis_index("x") + 2 * lax.axis_index("y") + lax.axis_index("c")

    wp_shard = jnp.stack([w_pa, w_pb, w_o]).astype(BF16)
    n_first = (SHARD_COLS // 2) // 16 * 16
    win_all, wdec_all = _all_gather([w_in.T.astype(BF16), w_dec_up], "gather_w",
                                    row_pieces=[[(0, n_first), (n_first, SHARD_COLS - n_first)], None])
    wt = _flatten_blocks_call(win_all)
    wr = jnp.pad(wt[RANK_COL:RANK_COL + GLA_RANK], ((0, 128 - GLA_RANK), (0, 0)))
    wdec_full = wdec_all.transpose(1, 0, 2).reshape(GLA_RANK, GLA_DK)
    wdec = jnp.pad(wdec_full, ((0, 128 - GLA_RANK), (0, 0)))

    grad_x, r_in, r_p, small = _local_step(
        x[0], loss_target[0], wt, wr, wdec, b_dec.reshape(1, -1), wp_shard,
        norm_g.reshape(1, -1), gla_norm_g.reshape(1, -1), b_gate.reshape(1, -1), final_g.reshape(1, -1))

    gw_in, d_in, nm_in, nv_in = (a.T for a in _adamw_call(r_in, w_in.T, m_w_in.T, v_w_in.T, "adamw_in"))
    ((g_pa, d_pa, nm_pa, nv_pa), (g_pb, d_pb, nm_pb, nv_pb), (g_o, d_o, nm_o, nv_o)), r_small = _adamw_rows_call(
        r_p, [w_pa, w_pb, w_o], [m_w_pa, m_w_pb, m_w_o], [v_w_pa, v_w_pb, v_w_o], "adamw_p", small)

    def row(a):
        return a.reshape(1, -1)

    rep = _adamw_lanes_call(
        r_small, [_SM_NORM, _SM_BDEC, _SM_GLAG, _SM_BGATE, _SM_FINAL],
        [row(a) for a in (norm_g, b_dec, gla_norm_g, b_gate, final_g)],
        [row(a) for a in (m_norm_g, m_b_dec, m_gla_norm_g, m_b_gate, m_final_g)],
        [row(a) for a in (v_norm_g, v_b_dec, v_gla_norm_g, v_b_gate, v_final_g)], "adamw_rep")
    ((g_norm, d_norm, nm_norm, nv_norm), (g_bdec, d_bdec, nm_bdec, nv_bdec), (g_glag, d_glag, nm_glag, nv_glag),
     (g_bgate, d_bgate, nm_bgate, nv_bgate), (g_final, d_final, nm_final, nv_final)) = [
        tuple(a.reshape(-1) for a in quad) for quad in rep]

    wdec_parts = r_small[:, 0, _SM_WDEC:].reshape(N_DEV, GLA_RANK, GLA_DK)
    cols = GLA_DK // N_DEV
    wdec_mine = lax.dynamic_slice_in_dim(wdec_parts, me * cols, cols, axis=2)
    g_wdec, d_wdec, nm_wdec, nv_wdec = _adamw_call(wdec_mine, w_dec_up, m_w_dec_up, v_w_dec_up, "adamw_dec")

    loss_total = jnp.sum(r_small[:, 0, _SM_LOSS])

    return (loss_total, grad_x[None],
            g_norm, gw_in, g_wdec, g_bdec, g_glag, g_pa, g_pb, g_bgate, g_o, g_final,
            d_norm, d_in, d_wdec, d_bdec, d_glag, d_pa, d_pb, d_bgate, d_o, d_final,
            nm_norm, nm_in, nm_wdec, nm_bdec, nm_glag, nm_pa, nm_pb, nm_bgate, nm_o, nm_final,
            nv_norm, nv_in, nv_wdec, nv_bdec, nv_glag, nv_pa, nv_pb, nv_bgate, nv_o, nv_final)
```

```python
import math

import jax
import jax.numpy as jnp
from jax import lax
from jax.experimental import pallas as pl
from jax.experimental.pallas import tpu as pltpu

F32 = jnp.float32
BF16 = jnp.bfloat16

N_DEV = 8
D_MODEL = 1024
GLA_HEADS = 4
GLA_HK = 128
GLA_HV = 256
GLA_DK = 512
GLA_RANK = 16
GLA_TAU = 16.0
GLA_CHUNK = 64
SB_HEADS = 8
SB_HD = 128
EPS = 1e-6
N_GROUPS = 9
RANK_COL = 3072
IN_COLS = 9232
SHARD_COLS = IN_COLS // N_DEV

ADAM_LR = 0.001
ADAM_B1 = 0.9
ADAM_B2 = 0.999
ADAM_EPS = 1e-08
ADAM_WD = 0.01
ADAM_STEP = 10

VMEM_LIMIT = 56 * 1024 * 1024
TBLK = 256


def _cparams(sem=None):
    return pltpu.CompilerParams(dimension_semantics=sem, vmem_limit_bytes=VMEM_LIMIT)


def _tiling_2d(rows, cols, band_cols):
    if rows * cols <= 128 * 1024:
        return (rows, cols), (1,), lambda i: (0, 0)
    if rows % 128 == 0:
        return (128, cols), (rows // 128,), lambda i: (i, 0)
    tc = band_cols if cols % band_cols == 0 else cols
    return (rows, tc), (cols // tc,), lambda i: (0, i)


def _dot(a, b):
    return jnp.dot(a, b, preferred_element_type=F32)


def _dot_nt(a, b):
    return lax.dot_general(a, b, (((1,), (1,)), ((), ())), preferred_element_type=F32)


def _dot_tn(a, b):
    return lax.dot_general(a, b, (((0,), (0,)), ((), ())), preferred_element_type=F32)


def _bf(x):
    return x.astype(BF16)


def _split3(x):
    hi = x.astype(BF16)
    r = x - hi.astype(F32)
    mid = r.astype(BF16)
    lo = (r - mid.astype(F32)).astype(BF16)
    return hi, mid, lo


def _tri_left(tri, x):
    hi, mid, lo = _split3(x)
    return _dot(tri, hi) + _dot(tri, mid) + _dot(tri, lo)


def _split2(x):
    hi = lax.bitcast_convert_type(lax.bitcast_convert_type(x, jnp.uint32) & jnp.uint32(0xFFFF0000), F32)
    return hi.astype(BF16), (x - hi).astype(BF16)


def _tri2_left(tri, x):
    hi, lo = _split2(x)
    return _dot(tri, hi) + _dot(tri, lo)


def _tri2_right(x, tri):
    hi, lo = _split2(x)
    return _dot(hi, tri) + _dot(lo, tri)


def _iota2(n, m, dim):
    return lax.broadcasted_iota(jnp.int32, (n, m), dim)


def _sigmoid(x):
    return 1.0 / (1.0 + jnp.exp(-x))


def _softplus_neg_abs(z):
    return jnp.log(1.0 + jnp.exp(-jnp.abs(z)))


_ANY = pl.BlockSpec(memory_space=pl.ANY)


def _mesh_pos():
    return lax.axis_index("x"), lax.axis_index("y"), lax.axis_index("c")


def _other_chips(x, y):
    return [(1 - x, y), (x, 1 - y), (1 - x, 1 - y)]


def _rcopy(src, dst, send_sem, recv_sem, to):
    return pltpu.make_async_remote_copy(src_ref=src, dst_ref=dst, send_sem=send_sem, recv_sem=recv_sem,
                                        device_id=to, device_id_type=pl.DeviceIdType.MESH)


def _push_copies(src_ref, dst_ref, send_sems, recv_sems, loc_sem, scatter):
    x, y, c = _mesh_pos()
    me = 4 * x + 2 * y + c
    own = pltpu.make_async_copy(src_ref.at[me] if scatter else src_ref, dst_ref.at[me], loc_sem)
    pairs = []
    for k in range(1, N_DEV):
        px = 1 - x if k & 4 else x
        py = 1 - y if k & 2 else y
        pc = 1 - c if k & 1 else c
        pid = 4 * px + 2 * py + pc
        src = src_ref.at[pid] if scatter else src_ref
        send = _rcopy(src, dst_ref.at[me], send_sems.at[k - 1], recv_sems.at[k - 1], (px, py, pc))
        recv = _rcopy(src, dst_ref.at[pid], send_sems.at[k - 1], recv_sems.at[k - 1], (px, py, pc))
        pairs.append((send, recv))
    return own, pairs


def _push_start(own, pairs):
    own.start()
    for send, _ in pairs:
        send.start()


def _push_wait(own, pairs):
    for _, recv in pairs:
        recv.wait_recv()
    for send, _ in pairs:
        send.wait_send()
    own.wait()


_PUSH_SEMS = [pltpu.SemaphoreType.DMA((N_DEV - 1,)), pltpu.SemaphoreType.DMA((N_DEV - 1,)),
              pltpu.SemaphoreType.DMA]


def _chip_copies(src_ref, dst_ref, send_sems, recv_sems, loc_sem):
    x, y, c = _mesh_pos()
    own = pltpu.make_async_copy(src_ref.at[2 * x + y], dst_ref.at[3], loc_sem)
    pairs = []
    for j, (px, py) in enumerate(_other_chips(x, y)):
        cp = _rcopy(src_ref.at[2 * px + py], dst_ref.at[j], send_sems.at[j], recv_sems.at[j], (px, py, c))
        pairs.append((cp, cp))
    return own, pairs


_CHIP_SEMS = [pltpu.SemaphoreType.DMA((3,)), pltpu.SemaphoreType.DMA((3,)), pltpu.SemaphoreType.DMA]


def _all_gather(arrs, name, row_pieces=None):
    n = len(arrs)
    pieces = [[None] if not row_pieces or not row_pieces[a] else list(row_pieces[a]) for a in range(n)]
    assert all(len(p) in (1, 2) for p in pieces)
    units = [(a, i) for a in range(n) for i in range(len(pieces[a]))]

    def body(*refs):
        ins = refs[:n]
        outs = refs[n:2 * n]
        send_sems, recv_sems, loc_sems = refs[2 * n:]
        x, y, c = _mesh_pos()
        me, sib = (x, y, c), (x, y, 1 - c)
        xn, yn, dg = [(px, py, c) for px, py in _other_chips(x, y)]

        def rows(ref, a, i):
            return ref if pieces[a][i] is None else ref.at[pl.ds(*pieces[a][i])]

        def copy(u, k, block, to, own=False):
            a, i = u
            px, py, pc = block
            dst = rows(outs[a].at[4 * px + 2 * py + pc], a, i)
            return _rcopy(rows(ins[a], a, i) if own else dst, dst, send_sems.at[a, k, i], recv_sems.at[a, k, i], to)

        started = []

        def start(cp):
            cp.start()
            started.append(cp)

        def landed_then_pass_on(u, k, block):
            copy(u, k, block, me).wait_recv()
            start(copy(u, 3 + k, block, sib))

        mine = [pltpu.make_async_copy(ins[a], outs[a].at[4 * x + 2 * y + c], loc_sems.at[a]) for a in range(n)]
        for cp in mine:
            cp.start()
        for u in units:
            start(copy(u, 0, me, sib, own=True))
        for a in range(n):
            if len(pieces[a]) == 2:
                for i, to, k in ((0, xn, 1), (1, yn, 2), (1, xn, 1), (0, yn, 2)):
                    start(copy((a, i), k, me, to, own=True))
            else:
                for to, k in ((xn, 1), (yn, 2), (dg, 3)):
                    start(copy((a, 0), k, me, to, own=True))
        for a in range(n):
            if len(pieces[a]) == 2:
                landed_then_pass_on((a, 0), 1, xn)
                start(copy((a, 0), 3, xn, yn))
                landed_then_pass_on((a, 1), 2, yn)
                start(copy((a, 1), 3, yn, xn))
                landed_then_pass_on((a, 1), 1, xn)
                landed_then_pass_on((a, 0), 2, yn)
                landed_then_pass_on((a, 0), 3, dg)
                landed_then_pass_on((a, 1), 3, dg)
            else:
                for block, k in ((xn, 1), (yn, 2), (dg, 3)):
                    landed_then_pass_on((a, 0), k, block)
        for u in units:
            copy(u, 0, sib, me).wait_recv()
            for k, (px, py, _) in ((4, xn), (5, yn), (6, dg)):
                copy(u, k, (px, py, 1 - c), me).wait_recv()
        for cp in started:
            cp.wait_send()
        for cp in mine:
            cp.wait()

    n_pc = max(len(p) for p in pieces)

    return pl.pallas_call(
        body, name=name,
        out_shape=tuple(jax.ShapeDtypeStruct((N_DEV,) + a.shape, a.dtype) for a in arrs),
        in_specs=[_ANY] * n,
        out_specs=tuple([_ANY] * n),
        scratch_shapes=[pltpu.SemaphoreType.DMA((n, 7, n_pc)), pltpu.SemaphoreType.DMA((n, 7, n_pc)),
                        pltpu.SemaphoreType.DMA((n,))],
    )(*arrs)


def _pair_exchange(arrs, name):
    n = len(arrs)

    def body(*refs):
        ins = refs[:n]
        outs = refs[n:2 * n]
        send_sems, recv_sems = refs[2 * n:]
        x, y, c = _mesh_pos()
        copies = []
        for a in range(n):
            for q in range(4):
                cp = _rcopy(ins[a].at[2 * q + (1 - c)], outs[a].at[q], send_sems.at[a, q], recv_sems.at[a, q],
                            (x, y, 1 - c))
                cp.start()
                copies.append(cp)
        for cp in copies:
            cp.wait_recv()
        for cp in copies:
            cp.wait_send()

    return pl.pallas_call(
        body, name=name,
        out_shape=tuple(jax.ShapeDtypeStruct((4,) + a.shape[1:], a.dtype) for a in arrs),
        in_specs=[_ANY] * n,
        out_specs=tuple([_ANY] * n),
        scratch_shapes=[pltpu.SemaphoreType.DMA((n, 4)), pltpu.SemaphoreType.DMA((n, 4))],
    )(*arrs)


def _pair_add_call(parts, recv, c_idx, name):
    _, R, C = parts.shape
    (tr, tc), (steps,), idx = _tiling_2d(R, C, 1024)

    def body(c_ref, p_ref, r_ref, o_ref):
        o_ref[...] = (p_ref[...].astype(F32) + r_ref[...].astype(F32)).astype(o_ref.dtype)

    return pl.pallas_call(
        body, name=name,
        grid_spec=pltpu.PrefetchScalarGridSpec(
            num_scalar_prefetch=1,
            grid=(4, steps),
            in_specs=[pl.BlockSpec((None, tr, tc), lambda q, i, c_ref: (2 * q + c_ref[0],) + idx(i)),
                      pl.BlockSpec((None, tr, tc), lambda q, i, c_ref: (q,) + idx(i))],
            out_specs=pl.BlockSpec((None, tr, tc), lambda q, i, c_ref: (q,) + idx(i))),
        out_shape=jax.ShapeDtypeStruct((4, R, C), parts.dtype),
        compiler_params=_cparams(("arbitrary", "arbitrary")),
    )(c_idx, parts, recv)


def _flatten_blocks_call(blocks):
    n, R, C = blocks.shape
    tc = C // 2

    def body(in_ref, out_ref):
        for p in range(n):
            out_ref[p * R:(p + 1) * R, :] = in_ref[p]

    return pl.pallas_call(
        body, name="flatten_w",
        grid=(C // tc,),
        in_specs=[pl.BlockSpec((n, R, tc), lambda i: (0, 0, i))],
        out_specs=pl.BlockSpec((n * R, tc), lambda i: (0, i)),
        out_shape=jax.ShapeDtypeStruct((n * R, C), blocks.dtype),
        compiler_params=_cparams(("arbitrary",)),
    )(blocks)


def _parts_by_device_call(dmain, drank):
    D = dmain.shape[1]
    tc = D // 2

    def body(dm_ref, dr_ref, out_ref):
        for p in range(N_DEV):
            lo, hi = p * SHARD_COLS, (p + 1) * SHARD_COLS
            at = 0
            for src, a, b in ((dm_ref, lo, min(hi, RANK_COL)),
                              (dr_ref, max(lo, RANK_COL) - RANK_COL, min(hi, RANK_COL + GLA_RANK) - RANK_COL),
                              (dm_ref, max(lo, RANK_COL + GLA_RANK) - GLA_RANK, hi - GLA_RANK)):
                if b > a:
                    out_ref[p, at:at + (b - a), :] = src[a:b, :]
                    at += b - a

    return pl.pallas_call(
        body, name="parts_by_device",
        grid=(D // tc,),
        in_specs=[pl.BlockSpec((dmain.shape[0], tc), lambda i: (0, i)),
                  pl.BlockSpec((GLA_RANK, tc), lambda i: (0, i))],
        out_specs=pl.BlockSpec((N_DEV, SHARD_COLS, tc), lambda i: (0, 0, i)),
        out_shape=jax.ShapeDtypeStruct((N_DEV, SHARD_COLS, D), dmain.dtype),
        compiler_params=_cparams(("arbitrary",)),
    )(dmain, drank)


def _group_row(g):
    return GLA_RANK * (g * (1024 // GLA_RANK) + (g >= RANK_COL // 1024))


def _proj_call(x, norm_g, wt, wr, wp_part):
    T, D = x.shape
    tm = min(1024, T)
    assert tm % TBLK == 0
    n_i = T // tm

    def f_slot(j):
        return ((j >= 2).astype(jnp.int32) + (j >= 6).astype(jnp.int32)
                + (j >= 7).astype(jnp.int32) + (j >= 8).astype(jnp.int32))

    def b_slot(j):
        return (j >= 3).astype(jnp.int32) + (j >= 4).astype(jnp.int32) + (j >= 5).astype(jnp.int32)

    def body(x_ref, g_ref, w_ref, wr_ref, wp_ref, pf_ref, pb_ref, rank_ref, ht_ref, wpall_ref,
             h_scr, send_sems, recv_sems, loc_sem):
        i = pl.program_id(0)
        j = pl.program_id(1)
        own, pairs = _push_copies(wp_ref, wpall_ref, send_sems, recv_sems, loc_sem, scatter=False)

        @pl.when((i == 0) & (j == 0))
        def _():
            _push_start(own, pairs)

        @pl.when(j == 0)
        def _():
            xv = x_ref[...]
            r = lax.rsqrt(jnp.mean(xv * xv, axis=-1, keepdims=True) + EPS)
            h = (xv * r) * g_ref[...]
            hb = _bf(h)
            h_scr[...] = hb
            for b in range(tm // TBLK):
                ht_ref[b] = _bf(h[b * TBLK:(b + 1) * TBLK].T)
            rank_ref[...] = _dot_nt(hb, wr_ref[...])

        is_b = (j == 1) | ((j >= 3) & (j <= 5))

        @pl.when(is_b)
        def _():
            pb_ref[...] = _bf(_dot_nt(h_scr[...], w_ref[...]))

        @pl.when(jnp.logical_not(is_b))
        def _():
            pf_ref[...] = _dot_nt(h_scr[...], w_ref[...])

        @pl.when((i == n_i - 1) & (j == N_GROUPS - 1))
        def _():
            _push_wait(own, pairs)

    return pl.pallas_call(
        body, name="proj",
        grid=(n_i, N_GROUPS),
        in_specs=[pl.BlockSpec((tm, D), lambda i, j: (i, 0)),
                  pl.BlockSpec((1, D), lambda i, j: (0, 0)),
                  pl.BlockSpec((pl.Element(1024), pl.Element(D)), lambda i, j: (_group_row(j), 0)),
                  pl.BlockSpec((128, D), lambda i, j: (0, 0)),
                  _ANY],
        out_specs=(pl.BlockSpec((None, tm, 1024), lambda i, j: (f_slot(j), i, 0)),
                   pl.BlockSpec((None, tm, 1024), lambda i, j: (b_slot(j), i, 0)),
                   pl.BlockSpec((tm, 128), lambda i, j: (i, 0)),
                   pl.BlockSpec((tm // TBLK, D, TBLK), lambda i, j: (i, 0, 0)),
                   _ANY),
        out_shape=(jax.ShapeDtypeStruct((5, T, 1024), F32),
                   jax.ShapeDtypeStruct((4, T, 1024), BF16),
                   jax.ShapeDtypeStruct((T, 128), F32),
                   jax.ShapeDtypeStruct((T // TBLK, D, TBLK), BF16),
                   jax.ShapeDtypeStruct((N_DEV,) + wp_part.shape, wp_part.dtype)),
        scratch_shapes=[pltpu.VMEM((tm, D), BF16)] + _PUSH_SEMS,
        compiler_params=_cparams(("arbitrary", "arbitrary")),
    )(x, norm_g, wt, wr, wp_part)


GLA_STEP_CHUNKS = 4


def _gla_same_chunk(rows):
    return (_iota2(rows, rows, 0) & -GLA_CHUNK) == (_iota2(rows, rows, 1) & -GLA_CHUNK)


def _gla_chunk_terms(la, q, k, n_c):
    C = GLA_CHUNK
    rows = n_c * C
    low = _gla_same_chunk(rows) & (_iota2(rows, rows, 0) >= _iota2(rows, rows, 1))
    b = _tri_left(_bf(low.astype(F32)), la)
    bl = [b[(c + 1) * C - 1:(c + 1) * C, :] for c in range(n_c)]
    bl_rows = jnp.concatenate([jnp.broadcast_to(bl[c], (C, b.shape[1])) for c in range(n_c)], axis=0)
    eb = jnp.exp(b)
    enb = jnp.exp(-b)
    ebl_b = jnp.exp(bl_rows - b)
    scale = GLA_HK ** -0.5
    qe = q * eb * scale
    ke = k * enb
    kd = k * ebl_b
    return bl, eb, enb, ebl_b, qe, ke, kd


def _gla_fwd_call(projf, projb, rank, wdec, bdec):
    T = projf.shape[1]
    C = GLA_CHUNK
    n_chunks = T // C
    n_c = GLA_STEP_CHUNKS
    R = n_c * C
    assert n_chunks % n_c == 0

    def body(qk_ref, v_ref, rank_ref, wd_ref, bd_ref, o_ref, st_ref, la_ref, st_scr):
        @pl.when(pl.program_id(0) == 0)
        def _():
            st_scr[...] = jnp.zeros_like(st_scr)

        dec = _dot(_bf(rank_ref[...]), _bf(wd_ref[...])) + bd_ref[...]
        la = (jnp.minimum(dec, 0.0) - _softplus_neg_abs(dec)) / GLA_TAU
        la_ref[...] = la
        mask = _gla_same_chunk(R) & (_iota2(R, R, 0) >= _iota2(R, R, 1))
        bl, _, _, _, qe, ke, kd = _gla_chunk_terms(la, qk_ref[:, :GLA_DK], qk_ref[:, GLA_DK:], n_c)
        qeb, keb, kdb = _bf(qe), _bf(ke), _bf(kd)
        ebl = [jnp.exp(bl[c]) for c in range(n_c)]
        heads = range(GLA_HEADS)
        ks = [slice(hh * GLA_HK, (hh + 1) * GLA_HK) for hh in heads]
        vs = [slice(hh * GLA_HV, (hh + 1) * GLA_HV) for hh in heads]
        rs = [slice(c * C, (c + 1) * C) for c in range(n_c)]
        p = [_bf(jnp.where(mask, _dot_nt(qeb[:, ks[hh]], keb[:, ks[hh]]), 0.0)) for hh in heads]
        upd = [[_dot_tn(v_ref[rs[c], vs[hh]], kdb[rs[c], ks[hh]]) for hh in heads] for c in range(n_c)]
        intra = [_dot(p[hh], v_ref[:, vs[hh]]) for hh in heads]
        st = [st_scr[hh] for hh in heads]
        for c in range(n_c):
            inter = [_dot_nt(qeb[rs[c], ks[hh]], _bf(st[hh])) for hh in heads]
            for hh in heads:
                st_ref[c, hh] = st[hh]
                o_ref[rs[c], vs[hh]] = intra[hh][rs[c]] + inter[hh]
            st = [st[hh] * ebl[c][:, ks[hh]] + upd[c][hh] for hh in heads]
        for hh in heads:
            st_scr[hh] = st[hh]

    return pl.pallas_call(
        body, name="gla_fwd",
        grid=(n_chunks // n_c,),
        in_specs=[pl.BlockSpec((None, R, 1024), lambda n: (0, n, 0)),
                  pl.BlockSpec((None, R, 1024), lambda n: (0, n, 0)),
                  pl.BlockSpec((R, 128), lambda n: (n, 0)),
                  pl.BlockSpec((128, GLA_DK), lambda n: (0, 0)),
                  pl.BlockSpec((1, GLA_DK), lambda n: (0, 0))],
        out_specs=(pl.BlockSpec((R, 1024), lambda n: (n, 0)),
                   pl.BlockSpec((n_c, GLA_HEADS, GLA_HV, GLA_HK), lambda n: (n, 0, 0, 0)),
                   pl.BlockSpec((R, GLA_DK), lambda n: (n, 0))),
        out_shape=(jax.ShapeDtypeStruct((T, 1024), F32),
                   jax.ShapeDtypeStruct((n_chunks, GLA_HEADS, GLA_HV, GLA_HK), F32),
                   jax.ShapeDtypeStruct((T, GLA_DK), F32)),
        scratch_shapes=[pltpu.VMEM((GLA_HEADS, GLA_HV, GLA_HK), F32)],
        compiler_params=_cparams(("arbitrary",)),
    )(projf, projb, rank, wdec, bdec)


def _gla_bwd_call(projf, projb, la, do_gla, st_all, rank, wdec):
    T = projf.shape[1]
    C = GLA_CHUNK
    n_chunks = T // C
    n_c = GLA_STEP_CHUNKS
    R = n_c * C
    assert n_chunks % n_c == 0
    last = n_chunks // n_c - 1

    def body(qk_ref, v_ref, la_ref, do_ref, st_ref, rank_ref, wd_ref,
             dqk_ref, dv_ref, drank_ref, dwd_ref, dbd_ref, dst_scr):
        @pl.when(pl.program_id(0) == 0)
        def _():
            dst_scr[...] = jnp.zeros_like(dst_scr)
            dwd_ref[...] = jnp.zeros_like(dwd_ref)
            dbd_ref[...] = jnp.zeros_like(dbd_ref)

        same = _gla_same_chunk(R)
        mask = same & (_iota2(R, R, 0) >= _iota2(R, R, 1))
        upp = _bf((same & (_iota2(R, R, 0) <= _iota2(R, R, 1))).astype(F32))
        scale = GLA_HK ** -0.5
        la = la_ref[...]
        bl, eb, enb, ebl_b, qe, ke, kd = _gla_chunk_terms(la, qk_ref[:, :GLA_DK], qk_ref[:, GLA_DK:], n_c)
        qeb, keb, kdb = _bf(qe), _bf(ke), _bf(kd)
        ebl = [jnp.exp(bl[c]) for c in range(n_c)]
        heads = range(GLA_HEADS)
        ks = [slice(hh * GLA_HK, (hh + 1) * GLA_HK) for hh in heads]
        vs = [slice(hh * GLA_HV, (hh + 1) * GLA_HV) for hh in heads]
        rs = [slice(c * C, (c + 1) * C) for c in range(n_c)]
        v = [v_ref[:, vs[hh]] for hh in heads]
        do = [_bf(do_ref[:, vs[hh]]) for hh in heads]
        p = [_bf(jnp.where(mask, _dot_nt(qeb[:, ks[hh]], keb[:, ks[hh]]), 0.0)) for hh in heads]
        dp = [_bf(jnp.where(mask, _dot_nt(do[hh], v[hh]), 0.0)) for hh in heads]
        dst_intra = [[_dot_tn(do[hh][rs[c]], qeb[rs[c], ks[hh]]) for hh in heads] for c in range(n_c)]
        dqe_inter = [[_dot(do[hh][rs[c]], _bf(st_ref[c, hh])) for hh in heads] for c in range(n_c)]
        dv_intra = [_dot_tn(p[hh], do[hh]) for hh in heads]
        dqe_intra = [_dot(dp[hh], keb[:, ks[hh]]) for hh in heads]
        dke = jnp.concatenate([_dot_tn(dp[hh], qeb[:, ks[hh]]) for hh in heads], axis=1)
        dstn = [dst_scr[hh] for hh in heads]
        dkd_c, dv_inter, debl = [None] * n_c, [None] * n_c, [None] * n_c
        for c in reversed(range(n_c)):
            dstnb = [_bf(dstn[hh]) for hh in heads]
            dkd_c[c] = jnp.concatenate([_dot(v[hh][rs[c]], dstnb[hh]) for hh in heads], axis=1)
            dv_inter[c] = [_dot_nt(kdb[rs[c], ks[hh]], dstnb[hh]) for hh in heads]
            debl[c] = jnp.concatenate(
                [jnp.sum(dstn[hh] * st_ref[c, hh], axis=0, keepdims=True) for hh in heads], axis=1)
            dstn = [dst_intra[c][hh] + dstn[hh] * ebl[c][:, ks[hh]] for hh in heads]
        for hh in heads:
            dst_scr[hh] = dstn[hh]
            dv_ref[:, vs[hh]] = _bf(dv_intra[hh] + jnp.concatenate([dv_inter[c][hh] for c in range(n_c)], axis=0))
        dqe = jnp.concatenate(
            [dqe_intra[hh] + jnp.concatenate([dqe_inter[c][hh] for c in range(n_c)], axis=0) for hh in heads], axis=1)
        dkd = jnp.concatenate(dkd_c, axis=0)
        dkd_kd = dkd * kd
        db = dqe * qe - dke * ke - dkd_kd
        dbl = jnp.concatenate(
            [jnp.broadcast_to(jnp.sum(dkd_kd[rs[c]], axis=0, keepdims=True) + ebl[c] * debl[c], (C, GLA_DK))
             for c in range(n_c)], axis=0)
        dla = _tri_left(upp, db) + dbl
        dqk_ref[:, :GLA_DK] = _bf(dqe * eb * scale)
        dqk_ref[:, GLA_DK:] = _bf(dke * enb + dkd * ebl_b)
        ddec = dla * (1.0 / GLA_TAU) * (1.0 - jnp.exp(GLA_TAU * la))
        ddecb = _bf(ddec)
        drank_ref[...] = _bf(_dot_nt(ddecb, _bf(wd_ref[...])))
        dwd_ref[...] += _dot_tn(_bf(rank_ref[...]), ddecb)
        dbd_ref[...] += jnp.sum(ddec, axis=0, keepdims=True)

    return pl.pallas_call(
        body, name="gla_bwd",
        grid=(n_chunks // n_c,),
        in_specs=[pl.BlockSpec((None, R, 1024), lambda n: (0, last - n, 0)),
                  pl.BlockSpec((None, R, 1024), lambda n: (0, last - n, 0)),
                  pl.BlockSpec((R, GLA_DK), lambda n: (last - n, 0)),
                  pl.BlockSpec((R, 1024), lambda n: (last - n, 0)),
                  pl.BlockSpec((n_c, GLA_HEADS, GLA_HV, GLA_HK), lambda n: (last - n, 0, 0, 0)),
                  pl.BlockSpec((R, 128), lambda n: (last - n, 0)),
                  pl.BlockSpec((128, GLA_DK), lambda n: (0, 0))],
        out_specs=(pl.BlockSpec((R, 1024), lambda n: (last - n, 0)),
                   pl.BlockSpec((R, 1024), lambda n: (last - n, 0)),
                   pl.BlockSpec((R, 128), lambda n: (last - n, 0)),
                   pl.BlockSpec((128, GLA_DK), lambda n: (0, 0)),
                   pl.BlockSpec((1, GLA_DK), lambda n: (0, 0))),
        out_shape=(jax.ShapeDtypeStruct((T, 1024), BF16),
                   jax.ShapeDtypeStruct((T, 1024), BF16),
                   jax.ShapeDtypeStruct((T, 128), BF16),
                   jax.ShapeDtypeStruct((128, GLA_DK), F32),
                   jax.ShapeDtypeStruct((1, GLA_DK), F32)),
        scratch_shapes=[pltpu.VMEM((GLA_HEADS, GLA_HV, GLA_HK), F32)],
        compiler_params=_cparams(("arbitrary",)),
    )(projf, projb, la, do_gla, st_all, rank, wdec)


def _sb_logs(z):
    lsz = jnp.minimum(z, 0.0) - _softplus_neg_abs(z)
    return lsz, lsz - z


SB_HG_FWD = 8
SB_HG_BWD = 4
SB_QUERIES = 256
SB_KEYS = 256
SB_DEAD = -105.0


def _sb_fwd_call(projb, wp_shard):
    T = projb.shape[1]
    B = min(SB_QUERIES, T)
    HG = SB_HG_FWD
    W = HG * SB_HD
    scale = 1.0 / math.sqrt(SB_HD)
    KB = min(SB_KEYS, T)
    n_h, n_i = SB_HEADS // HG, T // B

    def body(q_ref, k_ref, v_ref, wp_ref, o_ref, wpall_ref, cb_scr, send_sems, recv_sems, loc_sem):
        i = pl.program_id(1)
        own, pairs = _push_copies(wp_ref, wpall_ref, send_sems, recv_sems, loc_sem, scatter=False)

        @pl.when((pl.program_id(0) == 0) & (i == 0))
        def _():
            _push_start(own, pairs)

        rows = HG * B
        after = (_iota2(KB, KB, 0) > _iota2(KB, KB, 1)).astype(F32)
        tri = _bf(jnp.concatenate([after, jnp.ones((KB, KB), F32)], axis=1))
        o_ref[...] = jnp.zeros_like(o_ref)
        cb_scr[...] = jnp.zeros_like(cb_scr)

        def block(jp, masked):
            off = pl.multiple_of(jp * KB, KB)
            z = jnp.concatenate(
                [_dot_nt(q_ref[:, hh * SB_HD:(hh + 1) * SB_HD], k_ref[pl.ds(off, KB), hh * SB_HD:(hh + 1) * SB_HD])
                 for hh in range(HG)], axis=0) * scale
            lsz, l1m = _sb_logs(z)
            if masked:
                strict = (jp * KB + _iota2(rows, KB, 1)) < (i * B + (_iota2(rows, KB, 0) & (B - 1)))
                l1m = jnp.where(strict, l1m, 0.0)
            r = _tri2_right(l1m, tri)
            cb = cb_scr[...]
            a = jnp.exp(lsz + cb + r[:, :KB])
            if masked:
                a = jnp.where(strict, a, 0.0)
            cb_scr[...] = cb + r[:, KB:]
            ab = _bf(a)
            for hh in range(HG):
                cs = slice(hh * SB_HD, (hh + 1) * SB_HD)
                o_ref[:, cs] += _dot(ab[hh * B:(hh + 1) * B, :], v_ref[pl.ds(off, KB), cs])

        jp0 = (i * B) // KB
        block(jp0, True)

        def live(state):
            jj, dead = state
            return (jj <= jp0) & jnp.logical_not(dead)

        def step(state):
            jj, _ = state
            block(jp0 - jj, False)
            return jj + 1, jnp.max(cb_scr[:, :SB_HD]) < SB_DEAD

        lax.while_loop(live, step, (jnp.int32(1), jnp.max(cb_scr[:, :SB_HD]) < SB_DEAD))

        @pl.when((pl.program_id(0) == n_h - 1) & (i == n_i - 1))
        def _():
            _push_wait(own, pairs)

    return pl.pallas_call(
        body, name="sb_fwd",
        grid=(n_h, n_i),
        in_specs=[pl.BlockSpec((None, B, W), lambda h, i: (1, i, h)),
                  pl.BlockSpec((None, T, W), lambda h, i: (2, 0, h)),
                  pl.BlockSpec((None, T, W), lambda h, i: (3, 0, h)),
                  _ANY],
        out_specs=(pl.BlockSpec((B, W), lambda h, i: (i, h)), _ANY),
        out_shape=(jax.ShapeDtypeStruct((T, 1024), F32),
                   jax.ShapeDtypeStruct((N_DEV,) + wp_shard.shape, wp_shard.dtype)),
        scratch_shapes=[pltpu.VMEM((HG * B, KB), F32)] + _PUSH_SEMS,
        compiler_params=_cparams(("arbitrary", "arbitrary")),
    )(projb, projb, projb, wp_shard)


def _sb_bwd_call(projb, do_sb, g_p):
    T = projb.shape[1]
    B = min(SB_QUERIES, T)
    nb = T // B
    HG = SB_HG_BWD
    W = HG * SB_HD
    WQ = HG * B
    KB = min(SB_KEYS, T)
    nkb = T // KB
    n_h = SB_HEADS // HG
    scale = 1.0 / math.sqrt(SB_HD)

    def body(q_ref, k_ref, v_ref, do_ref, gp_ref, dq_ref, dk_ref, dv_ref, rp_ref,
             dk_scr, dv_scr, kt_scr, beta_scr, g_scr, dqt_scr, send_sems, recv_sems, loc_sem):
        i = pl.program_id(1)
        own, pairs = _push_copies(gp_ref, rp_ref, send_sems, recv_sems, loc_sem, scatter=True)

        @pl.when((pl.program_id(0) == 0) & (i == 0))
        def _():
            _push_start(own, pairs)

        @pl.when(i == 0)
        def _():
            dk_scr[...] = jnp.zeros_like(dk_scr)
            dv_scr[...] = jnp.zeros_like(dv_scr)
            for hh in range(HG):
                for jb in range(nkb):
                    kt_scr[hh, jb] = _bf(
                        k_ref[jb * KB:(jb + 1) * KB, hh * SB_HD:(hh + 1) * SB_HD].astype(F32).T)

        dqt_scr[...] = jnp.zeros_like(dqt_scr)
        later = _bf((_iota2(KB, KB, 1) > _iota2(KB, KB, 0)).astype(F32))
        earlier = _bf((_iota2(KB, KB, 1) < _iota2(KB, KB, 0)).astype(F32))
        dob = _bf(do_ref[...])
        jp0 = (i * B) // KB

        def strict_mask():
            return (jp0 * KB + _iota2(KB, WQ, 0)) < (i * B + (_iota2(KB, WQ, 1) & (B - 1)))

        def heads(fn):
            return [fn(slice(hh * SB_HD, (hh + 1) * SB_HD)) for hh in range(HG)]

        def pass1(jp, cb, masked):
            off = pl.multiple_of(jp * KB, KB)
            z = jnp.concatenate(heads(lambda cs: _dot_nt(k_ref[pl.ds(off, KB), cs], q_ref[:, cs])), axis=1) * scale
            da = jnp.concatenate(heads(lambda cs: _dot_nt(v_ref[pl.ds(off, KB), cs], dob[:, cs])), axis=1)
            lsz, l1m = _sb_logs(z)
            if masked:
                strict = strict_mask()
                l1m = jnp.where(strict, l1m, 0.0)
            a = jnp.exp(lsz + cb + _tri2_left(later, l1m))
            if masked:
                a = jnp.where(strict, a, 0.0)
            g_scr[jp] = a * da
            beta_scr[jp] = jnp.exp(lsz)
            ab = _bf(a)
            for hh in range(HG):
                cs = slice(hh * SB_HD, (hh + 1) * SB_HD)
                dv_scr[pl.ds(off, KB), cs] += _dot(ab[:, hh * B:(hh + 1) * B], dob[:, cs])
            return cb + jnp.sum(l1m, axis=0, keepdims=True)

        zero = jnp.zeros((1, WQ), F32)
        cb = pass1(jp0, zero, True)

        def live(state):
            jj, _, dead = state
            return (jj <= jp0) & jnp.logical_not(dead)

        def step(state):
            jj, cr, _ = state
            cr = pass1(jp0 - jj, cr, False)
            return jj + 1, cr, jnp.max(cr) < SB_DEAD

        n_done, _, _ = lax.while_loop(live, step, (jnp.int32(1), cb, jnp.max(cb) < SB_DEAD))
        jp_first = jp0 - (n_done - 1)

        def pass2(jp, cg, masked):
            off = pl.multiple_of(jp * KB, KB)
            g = g_scr[jp]
            beta = beta_scr[jp]
            dz = g * (1.0 - beta) - beta * (cg + _tri2_left(earlier, g))
            if masked:
                dz = jnp.where(strict_mask(), dz, 0.0)
            dzb = _bf(dz * scale)
            for hh in range(HG):
                cs = slice(hh * SB_HD, (hh + 1) * SB_HD)
                dk_scr[pl.ds(off, KB), cs] += _dot(dzb[:, hh * B:(hh + 1) * B], q_ref[:, cs])
                dqt_scr[hh] += _dot(kt_scr[hh, jp], dzb[:, hh * B:(hh + 1) * B])
            return cg + jnp.sum(g, axis=0, keepdims=True)

        cg = lax.fori_loop(jp_first, jp0, lambda jp, cr: pass2(jp, cr, False), zero)
        pass2(jp0, cg, True)
        for hh in range(HG):
            dq_ref[:, hh * SB_HD:(hh + 1) * SB_HD] = _bf(dqt_scr[hh].T)

        @pl.when(i == nb - 1)
        def _():
            dk_ref[...] = _bf(dk_scr[...])
            dv_ref[...] = _bf(dv_scr[...])

        @pl.when((pl.program_id(0) == n_h - 1) & (i == nb - 1))
        def _():
            _push_wait(own, pairs)

    return pl.pallas_call(
        body, name="sb_bwd",
        grid=(n_h, nb),
        in_specs=[pl.BlockSpec((None, B, W), lambda h, i: (1, i, h)),
                  pl.BlockSpec((None, T, W), lambda h, i: (2, 0, h)),
                  pl.BlockSpec((None, T, W), lambda h, i: (3, 0, h)),
                  pl.BlockSpec((B, W), lambda h, i: (i, h)),
                  _ANY],
        out_specs=(pl.BlockSpec((B, W), lambda h, i: (i, h)),
                   pl.BlockSpec((T, W), lambda h, i: (0, h)),
                   pl.BlockSpec((T, W), lambda h, i: (0, h)),
                   _ANY),
        out_shape=(jax.ShapeDtypeStruct((T, 1024), BF16),
                   jax.ShapeDtypeStruct((T, 1024), BF16),
                   jax.ShapeDtypeStruct((T, 1024), BF16),
                   jax.ShapeDtypeStruct(g_p.shape, g_p.dtype)),
        scratch_shapes=[pltpu.VMEM((T, W), F32), pltpu.VMEM((T, W), F32),
                        pltpu.VMEM((HG, nkb, SB_HD, KB), BF16),
                        pltpu.VMEM((nkb, KB, WQ), F32), pltpu.VMEM((nkb, KB, WQ), F32),
                        pltpu.VMEM((HG, SB_HD, B), F32)] + _PUSH_SEMS,
        compiler_params=_cparams(("arbitrary", "arbitrary")),
    )(projb, projb, projb, do_sb, g_p)


def _mid_call(o_gla, o_sb, projf, x, target, wpa, wpb, wo, gla_g, b_gate, final_g):
    T, D = x.shape
    tm = min(TBLK, T)

    def body(og_ref, ggate_ref, osb_ref, sgate_ref, ma_ref, mb_ref, x_ref, tgt_ref,
             wpa_ref, wpb_ref, wo_ref, glag_ref, bg_ref, fg_ref,
             dx2_ref, dogla_ref, dosb_ref, dggate_ref, dsgate_ref, dm_ref,
             mt_ref, ogt_ref, obt_ref, dx2b_ref, dya_ref, dyb_ref,
             dfg_ref, dbg_ref, dglag_ref, loss_ref):
        @pl.when(pl.program_id(0) == 0)
        def _():
            dfg_ref[...] = jnp.zeros_like(dfg_ref)
            dbg_ref[...] = jnp.zeros_like(dbg_ref)
            dglag_ref[...] = jnp.zeros_like(dglag_ref)
            loss_ref[...] = jnp.zeros_like(loss_ref)

        glag = glag_ref[...]
        ggate = ggate_ref[...]
        sg = _sigmoid(ggate)
        silu_g = ggate * sg
        ohat, rinv, nrm = [], [], []
        for hh in range(GLA_HEADS):
            oh = og_ref[:, hh * GLA_HV:(hh + 1) * GLA_HV]
            r = lax.rsqrt(jnp.mean(oh * oh, axis=-1, keepdims=True) + EPS)
            ohat.append(oh * r)
            rinv.append(r)
            nrm.append(ohat[-1] * glag)
        n_all = jnp.concatenate(nrm, axis=1)
        og = n_all * silu_g
        ogb = _bf(og)
        ya = _dot(ogb, wpa_ref[...])
        sgate = sgate_ref[...]
        ss = _sigmoid(sgate)
        silu_s = sgate * ss
        osb = osb_ref[...]
        ob = osb * silu_s
        obb = _bf(ob)
        yb = _dot(obb, wpb_ref[...])
        ga = _sigmoid(ma_ref[...] + bg_ref[:, :D])
        gb = _sigmoid(mb_ref[...] + bg_ref[:, D:])
        merged = ga * ya + gb * yb
        mgb = _bf(merged)
        x2 = x_ref[...] + _dot(mgb, wo_ref[...])
        r2 = lax.rsqrt(jnp.mean(x2 * x2, axis=-1, keepdims=True) + EPS)
        xh2 = x2 * r2
        fg = fg_ref[...]
        err = xh2 * fg - tgt_ref[...]
        loss_ref[...] += jnp.broadcast_to(
            0.5 * jnp.sum(jnp.mean(err * err, axis=-1, keepdims=True), axis=0, keepdims=True), (1, 128))
        dy = err * (1.0 / D)
        dfg_ref[...] += jnp.sum(dy * xh2, axis=0, keepdims=True)
        dxh = dy * fg
        dx2 = r2 * (dxh - xh2 * jnp.mean(dxh * xh2, axis=-1, keepdims=True))
        dx2_ref[...] = dx2
        dx2b = _bf(dx2)
        dx2b_ref[...] = dx2b
        dmerged = _dot_nt(dx2b, wo_ref[...])
        dya = dmerged * ga
        dyb = dmerged * gb
        dma = dmerged * ya * ga * (1.0 - ga)
        dmb = dmerged * yb * gb * (1.0 - gb)
        dm_ref[:, :D] = _bf(dma)
        dm_ref[:, D:] = _bf(dmb)
        dbg_ref[:, :D] += jnp.sum(dma, axis=0, keepdims=True)
        dbg_ref[:, D:] += jnp.sum(dmb, axis=0, keepdims=True)
        dyab = _bf(dya)
        dybb = _bf(dyb)
        dya_ref[...] = dyab
        dyb_ref[...] = dybb
        dog = _dot_nt(dyab, wpa_ref[...])
        dob = _dot_nt(dybb, wpb_ref[...])
        dosb_ref[...] = dob * silu_s
        dsgate_ref[...] = _bf(dob * osb * (ss * (1.0 + sgate * (1.0 - ss))))
        dn = dog * silu_g
        dggate_ref[...] = _bf(dog * n_all * (sg * (1.0 + ggate * (1.0 - sg))))
        dglag = jnp.zeros((1, GLA_HV), F32)
        for hh in range(GLA_HEADS):
            dnh = dn[:, hh * GLA_HV:(hh + 1) * GLA_HV]
            dglag = dglag + jnp.sum(dnh * ohat[hh], axis=0, keepdims=True)
            dohat = dnh * glag
            dogla_ref[:, hh * GLA_HV:(hh + 1) * GLA_HV] = rinv[hh] * (
                dohat - ohat[hh] * jnp.mean(dohat * ohat[hh], axis=-1, keepdims=True))
        dglag_ref[...] += dglag
        mt_ref[...] = _bf(merged.T)
        ogt_ref[...] = _bf(og.T)
        obt_ref[...] = _bf(ob.T)

    row = lambda i: (i, 0)
    const = lambda i: (0, 0)
    tile = pl.BlockSpec((tm, D), row)
    tile_t = pl.BlockSpec((None, D, tm), lambda i: (i, 0, 0))
    wspec = pl.BlockSpec((D, D), const)
    return pl.pallas_call(
        body, name="mid",
        grid=(T // tm,),
        in_specs=[tile,
                  pl.BlockSpec((None, tm, D), lambda i: (1, i, 0)),
                  tile,
                  pl.BlockSpec((None, tm, D), lambda i: (2, i, 0)),
                  pl.BlockSpec((None, tm, D), lambda i: (3, i, 0)),
                  pl.BlockSpec((None, tm, D), lambda i: (4, i, 0)),
                  tile, tile, wspec, wspec, wspec,
                  pl.BlockSpec((1, GLA_HV), const),
                  pl.BlockSpec((1, 2 * D), const),
                  pl.BlockSpec((1, D), const)],
        out_specs=(tile, tile, tile, tile, tile,
                   pl.BlockSpec((tm, 2 * D), row),
                   tile_t, tile_t, tile_t, tile, tile, tile,
                   pl.BlockSpec((1, D), const),
                   pl.BlockSpec((1, 2 * D), const),
                   pl.BlockSpec((1, GLA_HV), const),
                   pl.BlockSpec((1, 128), const)),
        out_shape=(jax.ShapeDtypeStruct((T, D), F32),
                   jax.ShapeDtypeStruct((T, D), F32),
                   jax.ShapeDtypeStruct((T, D), F32),
                   jax.ShapeDtypeStruct((T, D), BF16),
                   jax.ShapeDtypeStruct((T, D), BF16),
                   jax.ShapeDtypeStruct((T, 2 * D), BF16),
                   jax.ShapeDtypeStruct((T // tm, D, tm), BF16),
                   jax.ShapeDtypeStruct((T // tm, D, tm), BF16),
                   jax.ShapeDtypeStruct((T // tm, D, tm), BF16),
                   jax.ShapeDtypeStruct((T, D), BF16),
                   jax.ShapeDtypeStruct((T, D), BF16),
                   jax.ShapeDtypeStruct((T, D), BF16),
                   jax.ShapeDtypeStruct((1, D), F32),
                   jax.ShapeDtypeStruct((1, 2 * D), F32),
                   jax.ShapeDtypeStruct((1, GLA_HV), F32),
                   jax.ShapeDtypeStruct((1, 128), F32)),
        compiler_params=_cparams(("arbitrary",)),
    )(o_gla, projf, o_sb, projf, projf, projf, x, target, wpa, wpb, wo, gla_g, b_gate, final_g)


def _dh_call(pieces, dmlog, drank, wt, wr, x, dx2, norm_g, s_in):
    T, D = x.shape
    tm = min(256, T)
    npc = len(pieces)
    n_main = N_GROUPS * 1024
    n_i = T // tm

    def body(*refs):
        pcs = refs[:npc]
        (dm_ref, dr_ref, w_hbm, wr_ref, x_ref, dx2_ref, g_ref, sin_ref,
         gx_ref, dg_ref, rin_ref, w_scr, sems, send_sems, recv_sems, loc_sem) = refs[npc:]
        own, pairs = _chip_copies(sin_ref, rin_ref, send_sems, recv_sems, loc_sem)

        @pl.when(pl.program_id(0) == 0)
        def _():
            _push_start(own, pairs)
            lo = pltpu.make_async_copy(w_hbm.at[pl.ds(0, RANK_COL)], w_scr.at[pl.ds(0, RANK_COL)], sems.at[0])
            hi = pltpu.make_async_copy(w_hbm.at[pl.ds(RANK_COL + GLA_RANK, n_main - RANK_COL)],
                                       w_scr.at[pl.ds(RANK_COL, n_main - RANK_COL)], sems.at[1])
            lo.start()
            hi.start()
            dg_ref[...] = jnp.zeros_like(dg_ref)
            lo.wait()
            hi.wait()

        def w_group(g):
            return w_scr[g * 1024:(g + 1) * 1024, :]

        dr = dr_ref[...]
        dh = _dot(dr, wr_ref[...])
        for g in range(npc):
            dh = dh + _dot(pcs[g][...], w_group(g))
        dh = dh + _dot(dm_ref[:, :D], w_group(npc))
        dh = dh + _dot(dm_ref[:, D:], w_group(npc + 1))
        xv = x_ref[...]
        r = lax.rsqrt(jnp.mean(xv * xv, axis=-1, keepdims=True) + EPS)
        xhat = xv * r
        g = g_ref[...]
        dg_ref[...] += jnp.sum(dh * xhat, axis=0, keepdims=True)
        dxhat = dh * g
        gx_ref[...] = r * (dxhat - xhat * jnp.mean(dxhat * xhat, axis=-1, keepdims=True)) + dx2_ref[...]

        @pl.when(pl.program_id(0) == n_i - 1)
        def _():
            _push_wait(own, pairs)

    row = lambda i: (i, 0)
    const = lambda i: (0, 0)
    tile = pl.BlockSpec((tm, D), row)
    return pl.pallas_call(
        body, name="dh",
        grid=(n_i,),
        in_specs=[tile] * npc + [
            pl.BlockSpec((tm, 2 * D), row),
            pl.BlockSpec((tm, 128), row),
            _ANY,
            pl.BlockSpec((128, D), const),
            tile, tile,
            pl.BlockSpec((1, D), const),
            _ANY],
        out_specs=(tile, pl.BlockSpec((1, D), const), _ANY),
        out_shape=(jax.ShapeDtypeStruct((T, D), F32),
                   jax.ShapeDtypeStruct((1, D), F32),
                   jax.ShapeDtypeStruct(s_in.shape, s_in.dtype)),
        scratch_shapes=[pltpu.VMEM((n_main, D), BF16), pltpu.SemaphoreType.DMA((2,))] + _CHIP_SEMS,
        compiler_params=_cparams(("arbitrary",)),
    )(*pieces, dmlog, drank, wt, wr, x, dx2, norm_g, s_in)


def _wgrad_rank_call(ht, drank):
    n_tb, D, tb = ht.shape

    def body(ht_ref, dr_ref, o_ref):
        @pl.when(pl.program_id(0) == 0)
        def _():
            o_ref[...] = jnp.zeros_like(o_ref)

        o_ref[...] += _dot(ht_ref[...], dr_ref[...])

    return pl.pallas_call(
        body, name="wgrad_rank",
        grid=(n_tb,),
        in_specs=[pl.BlockSpec((None, D, tb), lambda i: (i, 0, 0)),
                  pl.BlockSpec((tb, 128), lambda i: (i, 0))],
        out_specs=pl.BlockSpec((D, 128), lambda i: (0, 0)),
        out_shape=jax.ShapeDtypeStruct((D, 128), F32),
        compiler_params=_cparams(("arbitrary",)),
    )(ht, drank)


def _wgrad_call(lhs_list, lhs_of_group, rhs_list, rhs_of_group, n_transposed, name):
    n_groups = len(rhs_of_group)
    n_tb, D, tb = lhs_list[0].shape
    T = n_tb * tb
    per = min(4, n_tb)
    tk = per * tb
    nk = T // tk
    nl = len(lhs_list)

    def body(*refs):
        lhs = refs[:nl]
        rhs = refs[nl:nl + n_groups]
        out_ref, acc = refs[nl + n_groups:]
        g = pl.program_id(0)
        i = pl.program_id(1)

        @pl.when(i == 0)
        def _():
            acc[...] = jnp.zeros_like(acc)

        for p in range(n_groups):
            @pl.when(g == p)
            def _(p=p):
                lref = lhs[lhs_of_group[p]]
                part = _dot(lref[0], rhs[p][0:tb, :])
                for b in range(1, per):
                    part = part + _dot(lref[b], rhs[p][b * tb:(b + 1) * tb, :])
                acc[...] += part

        @pl.when((i == nk - 1) & (g < n_transposed))
        def _():
            out_ref[...] = _bf(acc[...].T)

        @pl.when((i == nk - 1) & (g >= n_transposed))
        def _():
            out_ref[...] = _bf(acc[...])

    def lhs_spec(a):
        groups = [g for g in range(n_groups) if lhs_of_group[g] == a]
        lo, hi = min(groups), max(groups)
        assert groups == list(range(lo, hi + 1))
        return pl.BlockSpec((per, D, tb), lambda g, i: (jnp.where((g >= lo) & (g <= hi), i, 0), 0, 0))

    def rhs_spec(p):
        cb = rhs_of_group[p][1]
        return pl.BlockSpec((tk, 1024), lambda g, i: (jnp.where(g == p, i, 0), cb))

    return pl.pallas_call(
        body, name=name,
        grid=(n_groups, nk),
        in_specs=[lhs_spec(a) for a in range(nl)] + [rhs_spec(p) for p in range(n_groups)],
        out_specs=pl.BlockSpec((None, D, 1024), lambda g, i: (g, 0, 0)),
        out_shape=jax.ShapeDtypeStruct((n_groups, D, 1024), BF16),
        scratch_shapes=[pltpu.VMEM((D, 1024), F32)],
        compiler_params=_cparams(("arbitrary", "arbitrary")),
    )(*lhs_list, *[rhs_list[rhs_of_group[p][0]] for p in range(n_groups)])


def _adamw_math(parts, w, m, v):
    g = parts[0].astype(F32)
    for p in parts[1:]:
        g = g + p.astype(F32)
    mm = ADAM_B1 * m + (1.0 - ADAM_B1) * g
    vv = ADAM_B2 * v + (1.0 - ADAM_B2) * (g * g)
    m_hat = mm / (1.0 - ADAM_B1 ** ADAM_STEP)
    v_hat = vv / (1.0 - ADAM_B2 ** ADAM_STEP)
    return g, -ADAM_LR * (m_hat / (jnp.sqrt(v_hat) + ADAM_EPS) + ADAM_WD * w), mm, vv


def _part_order(n_parts):
    return [n_parts - 1] + list(range(n_parts - 1))


def _adamw_call(parts, w, m, v, name):
    R, C = w.shape
    n_parts = parts.shape[0]
    (tr, tc), grid, idx = _tiling_2d(R, C, 512)

    def body(p_ref, w_ref, m_ref, v_ref, g_ref, d_ref, nm_ref, nv_ref):
        g_ref[...], d_ref[...], nm_ref[...], nv_ref[...] = _adamw_math(
            [p_ref[k] for k in _part_order(n_parts)], w_ref[...], m_ref[...], v_ref[...])

    blk = pl.BlockSpec((tr, tc), idx)
    sds = jax.ShapeDtypeStruct((R, C), F32)
    return pl.pallas_call(
        body, name=name,
        grid=grid,
        in_specs=[pl.BlockSpec((n_parts, tr, tc), lambda i: (0,) + idx(i)), blk, blk, blk],
        out_specs=(blk, blk, blk, blk),
        out_shape=(sds, sds, sds, sds),
        compiler_params=_cparams(("arbitrary",)),
    )(parts, w, m, v)


def _adamw_rows_call(parts, ws, ms, vs, name, gathered):
    n = len(ws)
    R, C = ws[0].shape
    n_parts = parts.shape[0]

    def body(*refs):
        p_ref = refs[0]
        w_refs, m_refs, v_refs = refs[1:1 + n], refs[1 + n:1 + 2 * n], refs[1 + 2 * n:1 + 3 * n]
        src_ref = refs[1 + 3 * n]
        outs = refs[2 + 3 * n:2 + 7 * n]
        dst_ref, send_sems, recv_sems, loc_sem = refs[2 + 7 * n:]
        own, pairs = _push_copies(src_ref, dst_ref, send_sems, recv_sems, loc_sem, scatter=False)
        k_now = pl.program_id(0)

        @pl.when(k_now == 0)
        def _():
            _push_start(own, pairs)

        for k in range(n):
            @pl.when(k_now == k)
            def _(k=k):
                res = _adamw_math([p_ref[j] for j in _part_order(n_parts)],
                                  w_refs[k][...], m_refs[k][...], v_refs[k][...])
                for o_ref, val in zip(outs[4 * k:4 * k + 4], res):
                    o_ref[...] = val

        @pl.when(k_now == n - 1)
        def _():
            _push_wait(own, pairs)

    whole = pl.BlockSpec((R, C), lambda k: (0, 0))
    sds = jax.ShapeDtypeStruct((R, C), F32)
    res = pl.pallas_call(
        body, name=name,
        grid=(n,),
        in_specs=[pl.BlockSpec((n_parts, R, C), lambda k: (0, k, 0))] + [whole] * (3 * n) + [_ANY],
        out_specs=tuple([whole] * (4 * n) + [_ANY]),
        out_shape=tuple([sds] * (4 * n) + [jax.ShapeDtypeStruct((N_DEV,) + gathered.shape, gathered.dtype)]),
        scratch_shapes=_PUSH_SEMS,
        compiler_params=_cparams(("arbitrary",)),
    )(parts, *ws, *ms, *vs, gathered)
    return [res[4 * k:4 * k + 4] for k in range(n)], res[4 * n]


def _adamw_lanes_call(parts, offsets, ws, ms, vs, name):
    n = len(ws)
    n_parts = parts.shape[0]

    def body(*refs):
        p_ref = refs[0]
        w_refs, m_refs, v_refs = refs[1:1 + n], refs[1 + n:1 + 2 * n], refs[1 + 2 * n:1 + 3 * n]
        outs = refs[1 + 3 * n:]
        for k in range(n):
            lanes = slice(offsets[k], offsets[k] + ws[k].shape[1])
            res = _adamw_math([p_ref[j, :, lanes] for j in _part_order(n_parts)],
                              w_refs[k][...], m_refs[k][...], v_refs[k][...])
            for o_ref, val in zip(outs[4 * k:4 * k + 4], res):
                o_ref[...] = val

    res = pl.pallas_call(
        body, name=name,
        out_shape=tuple(jax.ShapeDtypeStruct(ws[k].shape, F32) for k in range(n) for _ in range(4)),
        compiler_params=_cparams(),
    )(parts, *ws, *ms, *vs)
    return [res[4 * k:4 * k + 4] for k in range(n)]


def _local_step(x, target, wt, wr, wdec, bdec, wp_shard, norm_g, gla_g, b_gate, final_g):
    D = x.shape[1]
    half = wp_shard.shape[1] // 2
    projf, projb, rank, ht, wp_lo = _proj_call(x, norm_g, wt, wr, wp_shard[:, :half])
    o_gla, st_all, la = _gla_fwd_call(projf, projb, rank, wdec, bdec)
    o_sb, wp_hi = _sb_fwd_call(projb, wp_shard[:, half:])
    wp_full = jnp.concatenate([wp_lo, wp_hi], axis=2).transpose(1, 0, 2, 3).reshape(3, D, D)
    (dx2, do_gla, do_sb, dggate, dsgate, dmlog, mt, ogt, obt, dx2b, dya, dyb,
     dfinal_g, db_gate, dgla_g, loss) = _mid_call(o_gla, o_sb, projf, x, target, wp_full[0], wp_full[1],
                                                 wp_full[2], gla_g, b_gate, final_g)
    dw_p = _wgrad_call([ogt, obt, mt], [0, 1, 2], [dya, dyb, dx2b], [(0, 0), (1, 0), (2, 0)], 0, "wgrad_p")
    g_p = dw_p.reshape(3, N_DEV, D // N_DEV, D).transpose(1, 0, 2, 3).reshape(N_DEV, 3 * (D // N_DEV), D)
    dqk, dgv, drank, dwdec, dbdec = _gla_bwd_call(projf, projb, la, do_gla, st_all, rank, wdec)
    dsq, dsk, dsv, r_p = _sb_bwd_call(projb, do_sb, g_p)
    pieces = [dqk, dgv, dggate, dsq, dsk, dsv, dsgate]
    rhs_of_group = [(g, 0) for g in range(7)] + [(7, 0), (7, 1)]
    dw_in = _wgrad_call([ht], [0] * N_GROUPS, pieces + [dmlog], rhs_of_group, N_GROUPS, "wgrad_in")
    dwr = _wgrad_rank_call(ht, drank)
    g_in = _parts_by_device_call(dw_in.reshape(N_GROUPS * 1024, D), dwr[:, :GLA_RANK].T.astype(BF16))
    c_idx = lax.axis_index("c").astype(jnp.int32).reshape(1)
    (p_in,) = _pair_exchange([g_in], "pair_g")
    s_in = _pair_add_call(g_in, p_in, c_idx, "pair_add_in")
    grad_x, dnorm_g, r_in = _dh_call(pieces, dmlog, drank, wt, wr, x, dx2, norm_g, s_in)
    small = jnp.concatenate([
        dnorm_g.reshape(-1), dbdec.reshape(-1), dgla_g.reshape(-1), db_gate.reshape(-1), dfinal_g.reshape(-1),
        loss.reshape(-1), dwdec[:GLA_RANK].reshape(-1)]).reshape(1, _SM_LEN)
    return grad_x, r_in, r_p, small


_SM_NORM = 0
_SM_BDEC = _SM_NORM + D_MODEL
_SM_GLAG = _SM_BDEC + GLA_DK
_SM_BGATE = _SM_GLAG + GLA_HV
_SM_FINAL = _SM_BGATE + 2 * D_MODEL
_SM_REPL = _SM_FINAL + D_MODEL
_SM_LOSS = _SM_REPL
_SM_WDEC = _SM_LOSS + 128
_SM_LEN = _SM_WDEC + GLA_RANK * GLA_DK


def kernel(x, norm_g, w_in, w_dec_up, b_dec, gla_norm_g, w_pa, w_pb, b_gate, w_o, final_g, loss_target, m_norm_g, m_w_in, m_w_dec_up, m_b_dec, m_gla_norm_g, m_w_pa, m_w_pb, m_b_gate, m_w_o, m_final_g, v_norm_g, v_w_in, v_w_dec_up, v_b_dec, v_gla_norm_g, v_w_pa, v_w_pb, v_b_gate, v_w_o, v_final_g):
    D = D_MODEL
    me = 4 * lax.axis_index("x") + 2 * lax.axis_index("y") + lax.axis_index("c")

    wp_shard = jnp.stack([w_pa, w_pb, w_o]).astype(BF16)
    n_first = (SHARD_COLS // 2) // 16 * 16
    win_all, wdec_all = _all_gather([w_in.T.astype(BF16), w_dec_up], "gather_w",
                                    row_pieces=[[(0, n_first), (n_first, SHARD_COLS - n_first)], None])
    wt = _flatten_blocks_call(win_all)
    wr = jnp.pad(wt[RANK_COL:RANK_COL + GLA_RANK], ((0, 128 - GLA_RANK), (0, 0)))
    wdec_full = wdec_all.transpose(1, 0, 2).reshape(GLA_RANK, GLA_DK)
    wdec = jnp.pad(wdec_full, ((0, 128 - GLA_RANK), (0, 0)))

    grad_x, r_in, r_p, small = _local_step(
        x[0], loss_target[0], wt, wr, wdec, b_dec.reshape(1, -1), wp_shard,
        norm_g.reshape(1, -1), gla_norm_g.reshape(1, -1), b_gate.reshape(1, -1), final_g.reshape(1, -1))

    gw_in, d_in, nm_in, nv_in = (a.T for a in _adamw_call(r_in, w_in.T, m_w_in.T, v_w_in.T, "adamw_in"))
    ((g_pa, d_pa, nm_pa, nv_pa), (g_pb, d_pb, nm_pb, nv_pb), (g_o, d_o, nm_o, nv_o)), r_small = _adamw_rows_call(
        r_p, [w_pa, w_pb, w_o], [m_w_pa, m_w_pb, m_w_o], [v_w_pa, v_w_pb, v_w_o], "adamw_p", small)

    def row(a):
        return a.reshape(1, -1)

    rep = _adamw_lanes_call(
        r_small, [_SM_NORM, _SM_BDEC, _SM_GLAG, _SM_BGATE, _SM_FINAL],
        [row(a) for a in (norm_g, b_dec, gla_norm_g, b_gate, final_g)],
        [row(a) for a in (m_norm_g, m_b_dec, m_gla_norm_g, m_b_gate, m_final_g)],
        [row(a) for a in (v_norm_g, v_b_dec, v_gla_norm_g, v_b_gate, v_final_g)], "adamw_rep")
    ((g_norm, d_norm, nm_norm, nv_norm), (g_bdec, d_bdec, nm_bdec, nv_bdec), (g_glag, d_glag, nm_glag, nv_glag),
     (g_bgate, d_bgate, nm_bgate, nv_bgate), (g_final, d_final, nm_final, nv_final)) = [
        tuple(a.reshape(-1) for a in quad) for quad in rep]

    wdec_parts = r_small[:, 0, _SM_WDEC:].reshape(N_DEV, GLA_RANK, GLA_DK)
    cols = GLA_DK // N_DEV
    wdec_mine = lax.dynamic_slice_in_dim(wdec_parts, me * cols, cols, axis=2)
    g_wdec, d_wdec, nm_wdec, nv_wdec = _adamw_call(wdec_mine, w_dec_up, m_w_dec_up, v_w_dec_up, "adamw_dec")

    loss_total = jnp.sum(r_small[:, 0, _SM_LOSS])

    return (loss_total, grad_x[None],
            g_norm, gw_in, g_wdec, g_bdec, g_glag, g_pa, g_pb, g_bgate, g_o, g_final,
            d_norm, d_in, d_wdec, d_bdec, d_glag, d_pa, d_pb, d_bgate, d_o, d_final,
            nm_norm, nm_in, nm_wdec, nm_bdec, nm_glag, nm_pa, nm_pb, nm_bgate, nm_o, nm_final,
            nv_norm, nv_in, nv_wdec, nv_bdec, nv_glag, nv_pa, nv_pb, nv_bgate, nv_o, nv_final)
```

```python
import math

import jax
import jax.numpy as jnp
from jax import lax
from jax.experimental import pallas as pl
from jax.experimental.pallas import tpu as pltpu

F32 = jnp.float32
BF16 = jnp.bfloat16

N_DEV = 8
D_MODEL = 1024
GLA_HEADS = 4
GLA_HK = 128
GLA_HV = 256
GLA_DK = 512
GLA_RANK = 16
GLA_TAU = 16.0
GLA_CHUNK = 64
SB_HEADS = 8
SB_HD = 128
EPS = 1e-6
N_GROUPS = 9
RANK_COL = 3072
IN_COLS = 9232
SHARD_COLS = IN_COLS // N_DEV

ADAM_LR = 0.001
ADAM_B1 = 0.9
ADAM_B2 = 0.999
ADAM_EPS = 1e-08
ADAM_WD = 0.01
ADAM_STEP = 10

VMEM_LIMIT = 56 * 1024 * 1024
TBLK = 256


def _cparams(sem=None):
    return pltpu.CompilerParams(dimension_semantics=sem, vmem_limit_bytes=VMEM_LIMIT)


def _tiling_2d(rows, cols, band_cols):
    if rows * cols <= 128 * 1024:
        return (rows, cols), (1,), lambda i: (0, 0)
    if rows % 128 == 0:
        return (128, cols), (rows // 128,), lambda i: (i, 0)
    tc = band_cols if cols % band_cols == 0 else cols
    return (rows, tc), (cols // tc,), lambda i: (0, i)


def _dot(a, b):
    return jnp.dot(a, b, preferred_element_type=F32)


def _dot_nt(a, b):
    return lax.dot_general(a, b, (((1,), (1,)), ((), ())), preferred_element_type=F32)


def _dot_tn(a, b):
    return lax.dot_general(a, b, (((0,), (0,)), ((), ())), preferred_element_type=F32)


def _bf(x):
    return x.astype(BF16)


def _split3(x):
    hi = x.astype(BF16)
    r = x - hi.astype(F32)
    mid = r.astype(BF16)
    lo = (r - mid.astype(F32)).astype(BF16)
    return hi, mid, lo


def _tri_left(tri, x):
    hi, mid, lo = _split3(x)
    return _dot(tri, hi) + _dot(tri, mid) + _dot(tri, lo)


def _split2(x):
    hi = lax.bitcast_convert_type(lax.bitcast_convert_type(x, jnp.uint32) & jnp.uint32(0xFFFF0000), F32)
    return hi.astype(BF16), (x - hi).astype(BF16)


def _tri2_left(tri, x):
    hi, lo = _split2(x)
    return _dot(tri, hi) + _dot(tri, lo)


def _tri2_right(x, tri):
    hi, lo = _split2(x)
    return _dot(hi, tri) + _dot(lo, tri)


def _iota2(n, m, dim):
    return lax.broadcasted_iota(jnp.int32, (n, m), dim)


def _sigmoid(x):
    return 1.0 / (1.0 + jnp.exp(-x))


def _softplus_neg_abs(z):
    return jnp.log(1.0 + jnp.exp(-jnp.abs(z)))


_ANY = pl.BlockSpec(memory_space=pl.ANY)


def _mesh_pos():
    return lax.axis_index("x"), lax.axis_index("y"), lax.axis_index("c")


def _other_chips(x, y):
    return [(1 - x, y), (x, 1 - y), (1 - x, 1 - y)]


def _rcopy(src, dst, send_sem, recv_sem, to):
    return pltpu.make_async_remote_copy(src_ref=src, dst_ref=dst, send_sem=send_sem, recv_sem=recv_sem,
                                        device_id=to, device_id_type=pl.DeviceIdType.MESH)


def _push_copies(src_ref, dst_ref, send_sems, recv_sems, loc_sem, scatter):
    x, y, c = _mesh_pos()
    me = 4 * x + 2 * y + c
    own = pltpu.make_async_copy(src_ref.at[me] if scatter else src_ref, dst_ref.at[me], loc_sem)
    pairs = []
    for k in range(1, N_DEV):
        px = 1 - x if k & 4 else x
        py = 1 - y if k & 2 else y
        pc = 1 - c if k & 1 else c
        pid = 4 * px + 2 * py + pc
        src = src_ref.at[pid] if scatter else src_ref
        send = _rcopy(src, dst_ref.at[me], send_sems.at[k - 1], recv_sems.at[k - 1], (px, py, pc))
        recv = _rcopy(src, dst_ref.at[pid], send_sems.at[k - 1], recv_sems.at[k - 1], (px, py, pc))
        pairs.append((send, recv))
    return own, pairs


def _push_start(own, pairs):
    own.start()
    for send, _ in pairs:
        send.start()


def _push_wait(own, pairs):
    for _, recv in pairs:
        recv.wait_recv()
    for send, _ in pairs:
        send.wait_send()
    own.wait()


_PUSH_SEMS = [pltpu.SemaphoreType.DMA((N_DEV - 1,)), pltpu.SemaphoreType.DMA((N_DEV - 1,)),
              pltpu.SemaphoreType.DMA]


def _neighbour_copies(own_ref, to_x_ref, to_y_ref, dst_ref, send_sems, recv_sems, loc_sem):
    x, y, c = _mesh_pos()
    (xx, xy), (yx, yy), _ = _other_chips(x, y)
    own = pltpu.make_async_copy(own_ref, dst_ref.at[2], loc_sem)
    to_x = _rcopy(to_x_ref, dst_ref.at[0], send_sems.at[0], recv_sems.at[0], (xx, xy, c))
    to_y = _rcopy(to_y_ref, dst_ref.at[1], send_sems.at[1], recv_sems.at[1], (yx, yy, c))
    return own, [(to_x, to_x), (to_y, to_y)]


_NEIGHBOUR_SEMS = [pltpu.SemaphoreType.DMA((2,)), pltpu.SemaphoreType.DMA((2,)), pltpu.SemaphoreType.DMA]


def _half_rows(rows):
    return (rows // 2) // 16 * 16


def _diag_relay_call(s_in):
    _, R, C = s_in.shape
    n0 = _half_rows(R)

    def body(s_ref, out_ref, send_sems, recv_sems):
        x, y, c = _mesh_pos()
        (xx, xy), (yx, yy), (dx, dy) = _other_chips(x, y)
        mine = s_ref.at[2 * dx + dy]
        to_x = _rcopy(mine.at[pl.ds(0, n0)], out_ref.at[pl.ds(0, n0)], send_sems.at[0], recv_sems.at[0],
                      (xx, xy, c))
        to_y = _rcopy(mine.at[pl.ds(n0, R - n0)], out_ref.at[pl.ds(n0, R - n0)], send_sems.at[1],
                      recv_sems.at[1], (yx, yy, c))
        to_x.start()
        to_y.start()
        to_x.wait_recv()
        to_y.wait_recv()
        to_x.wait_send()
        to_y.wait_send()

    return pl.pallas_call(
        body, name="diag_relay",
        out_shape=jax.ShapeDtypeStruct((R, C), s_in.dtype),
        in_specs=[_ANY], out_specs=_ANY,
        scratch_shapes=[pltpu.SemaphoreType.DMA((2,)), pltpu.SemaphoreType.DMA((2,))],
    )(s_in)


def _relay_add_call(s_in, relayed, q_xy):
    _, R, C = s_in.shape
    n0 = _half_rows(R)
    tc = C // 2

    def body(q_ref, sx_ref, sy_ref, rel_ref, ox_ref, oy_ref):
        ox_ref[0:n0, :] = sx_ref[0:n0, :]
        ox_ref[n0:R, :] = (sx_ref[n0:R, :].astype(F32) + rel_ref[n0:R, :].astype(F32)).astype(ox_ref.dtype)
        oy_ref[0:n0, :] = (sy_ref[0:n0, :].astype(F32) + rel_ref[0:n0, :].astype(F32)).astype(oy_ref.dtype)
        oy_ref[n0:R, :] = sy_ref[n0:R, :]

    band = pl.BlockSpec((R, tc), lambda j, q_ref: (0, j))
    return pl.pallas_call(
        body, name="relay_add",
        grid_spec=pltpu.PrefetchScalarGridSpec(
            num_scalar_prefetch=1,
            grid=(C // tc,),
            in_specs=[pl.BlockSpec((None, R, tc), lambda j, q_ref: (q_ref[0], 0, j)),
                      pl.BlockSpec((None, R, tc), lambda j, q_ref: (q_ref[1], 0, j)),
                      band],
            out_specs=(band, band)),
        out_shape=(jax.ShapeDtypeStruct((R, C), s_in.dtype), jax.ShapeDtypeStruct((R, C), s_in.dtype)),
        compiler_params=_cparams(("arbitrary",)),
    )(q_xy, s_in, s_in, relayed)


def _all_gather(arrs, name, row_pieces=None):
    n = len(arrs)
    pieces = [[None] if not row_pieces or not row_pieces[a] else list(row_pieces[a]) for a in range(n)]
    assert all(len(p) in (1, 2) for p in pieces)
    units = [(a, i) for a in range(n) for i in range(len(pieces[a]))]

    def body(*refs):
        ins = refs[:n]
        outs = refs[n:2 * n]
        send_sems, recv_sems, loc_sems = refs[2 * n:]
        x, y, c = _mesh_pos()
        me, sib = (x, y, c), (x, y, 1 - c)
        xn, yn, dg = [(px, py, c) for px, py in _other_chips(x, y)]

        def rows(ref, a, i):
            return ref if pieces[a][i] is None else ref.at[pl.ds(*pieces[a][i])]

        def copy(u, k, block, to, own=False):
            a, i = u
            px, py, pc = block
            dst = rows(outs[a].at[4 * px + 2 * py + pc], a, i)
            return _rcopy(rows(ins[a], a, i) if own else dst, dst, send_sems.at[a, k, i], recv_sems.at[a, k, i], to)

        started = []

        def start(cp):
            cp.start()
            started.append(cp)

        def landed_then_pass_on(u, k, block):
            copy(u, k, block, me).wait_recv()
            start(copy(u, 3 + k, block, sib))

        mine = [pltpu.make_async_copy(ins[a], outs[a].at[4 * x + 2 * y + c], loc_sems.at[a]) for a in range(n)]
        for cp in mine:
            cp.start()
        for u in units:
            start(copy(u, 0, me, sib, own=True))
        for a in range(n):
            if len(pieces[a]) == 2:
                for i, to, k in ((0, xn, 1), (1, yn, 2), (1, xn, 1), (0, yn, 2)):
                    start(copy((a, i), k, me, to, own=True))
            else:
                for to, k in ((xn, 1), (yn, 2), (dg, 3)):
                    start(copy((a, 0), k, me, to, own=True))
        for a in range(n):
            if len(pieces[a]) == 2:
                landed_then_pass_on((a, 0), 1, xn)
                start(copy((a, 0), 3, xn, yn))
                landed_then_pass_on((a, 1), 2, yn)
                start(copy((a, 1), 3, yn, xn))
                landed_then_pass_on((a, 1), 1, xn)
                landed_then_pass_on((a, 0), 2, yn)
                landed_then_pass_on((a, 0), 3, dg)
                landed_then_pass_on((a, 1), 3, dg)
            else:
                for block, k in ((xn, 1), (yn, 2), (dg, 3)):
                    landed_then_pass_on((a, 0), k, block)
        for u in units:
            copy(u, 0, sib, me).wait_recv()
            for k, (px, py, _) in ((4, xn), (5, yn), (6, dg)):
                copy(u, k, (px, py, 1 - c), me).wait_recv()
        for cp in started:
            cp.wait_send()
        for cp in mine:
            cp.wait()

    n_pc = max(len(p) for p in pieces)

    return pl.pallas_call(
        body, name=name,
        out_shape=tuple(jax.ShapeDtypeStruct((N_DEV,) + a.shape, a.dtype) for a in arrs),
        in_specs=[_ANY] * n,
        out_specs=tuple([_ANY] * n),
        scratch_shapes=[pltpu.SemaphoreType.DMA((n, 7, n_pc)), pltpu.SemaphoreType.DMA((n, 7, n_pc)),
                        pltpu.SemaphoreType.DMA((n,))],
    )(*arrs)


def _pair_exchange(arrs, name):
    n = len(arrs)

    def body(*refs):
        ins = refs[:n]
        outs = refs[n:2 * n]
        send_sems, recv_sems = refs[2 * n:]
        x, y, c = _mesh_pos()
        copies = []
        for a in range(n):
            for q in range(4):
                cp = _rcopy(ins[a].at[2 * q + (1 - c)], outs[a].at[q], send_sems.at[a, q], recv_sems.at[a, q],
                            (x, y, 1 - c))
                cp.start()
                copies.append(cp)
        for cp in copies:
            cp.wait_recv()
        for cp in copies:
            cp.wait_send()

    return pl.pallas_call(
        body, name=name,
        out_shape=tuple(jax.ShapeDtypeStruct((4,) + a.shape[1:], a.dtype) for a in arrs),
        in_specs=[_ANY] * n,
        out_specs=tuple([_ANY] * n),
        scratch_shapes=[pltpu.SemaphoreType.DMA((n, 4)), pltpu.SemaphoreType.DMA((n, 4))],
    )(*arrs)


def _pair_add_call(parts, recv, c_idx, name):
    _, R, C = parts.shape
    (tr, tc), (steps,), idx = _tiling_2d(R, C, 1024)

    def body(c_ref, p_ref, r_ref, o_ref):
        o_ref[...] = (p_ref[...].astype(F32) + r_ref[...].astype(F32)).astype(o_ref.dtype)

    return pl.pallas_call(
        body, name=name,
        grid_spec=pltpu.PrefetchScalarGridSpec(
            num_scalar_prefetch=1,
            grid=(4, steps),
            in_specs=[pl.BlockSpec((None, tr, tc), lambda q, i, c_ref: (2 * q + c_ref[0],) + idx(i)),
                      pl.BlockSpec((None, tr, tc), lambda q, i, c_ref: (q,) + idx(i))],
            out_specs=pl.BlockSpec((None, tr, tc), lambda q, i, c_ref: (q,) + idx(i))),
        out_shape=jax.ShapeDtypeStruct((4, R, C), parts.dtype),
        compiler_params=_cparams(("arbitrary", "arbitrary")),
    )(c_idx, parts, recv)


def _flatten_blocks_call(blocks):
    n, R, C = blocks.shape
    tc = C // 2

    def body(in_ref, out_ref):
        for p in range(n):
            out_ref[p * R:(p + 1) * R, :] = in_ref[p]

    return pl.pallas_call(
        body, name="flatten_w",
        grid=(C // tc,),
        in_specs=[pl.BlockSpec((n, R, tc), lambda i: (0, 0, i))],
        out_specs=pl.BlockSpec((n * R, tc), lambda i: (0, i)),
        out_shape=jax.ShapeDtypeStruct((n * R, C), blocks.dtype),
        compiler_params=_cparams(("arbitrary",)),
    )(blocks)


def _parts_by_device_call(dmain, drank):
    D = dmain.shape[1]
    tc = D // 2

    def body(dm_ref, dr_ref, out_ref):
        for p in range(N_DEV):
            lo, hi = p * SHARD_COLS, (p + 1) * SHARD_COLS
            at = 0
            for src, a, b in ((dm_ref, lo, min(hi, RANK_COL)),
                              (dr_ref, max(lo, RANK_COL) - RANK_COL, min(hi, RANK_COL + GLA_RANK) - RANK_COL),
                              (dm_ref, max(lo, RANK_COL + GLA_RANK) - GLA_RANK, hi - GLA_RANK)):
                if b > a:
                    out_ref[p, at:at + (b - a), :] = src[a:b, :]
                    at += b - a

    return pl.pallas_call(
        body, name="parts_by_device",
        grid=(D // tc,),
        in_specs=[pl.BlockSpec((dmain.shape[0], tc), lambda i: (0, i)),
                  pl.BlockSpec((GLA_RANK, tc), lambda i: (0, i))],
        out_specs=pl.BlockSpec((N_DEV, SHARD_COLS, tc), lambda i: (0, 0, i)),
        out_shape=jax.ShapeDtypeStruct((N_DEV, SHARD_COLS, D), dmain.dtype),
        compiler_params=_cparams(("arbitrary",)),
    )(dmain, drank)


def _group_row(g):
    return GLA_RANK * (g * (1024 // GLA_RANK) + (g >= RANK_COL // 1024))


def _proj_call(x, norm_g, wt, wr, wp_part):
    T, D = x.shape
    tm = min(1024, T)
    assert tm % TBLK == 0
    n_i = T // tm

    def f_slot(j):
        return ((j >= 2).astype(jnp.int32) + (j >= 6).astype(jnp.int32)
                + (j >= 7).astype(jnp.int32) + (j >= 8).astype(jnp.int32))

    def b_slot(j):
        return (j >= 3).astype(jnp.int32) + (j >= 4).astype(jnp.int32) + (j >= 5).astype(jnp.int32)

    def body(x_ref, g_ref, w_ref, wr_ref, wp_ref, pf_ref, pb_ref, rank_ref, ht_ref, wpall_ref,
             h_scr, send_sems, recv_sems, loc_sem):
        i = pl.program_id(0)
        j = pl.program_id(1)
        own, pairs = _push_copies(wp_ref, wpall_ref, send_sems, recv_sems, loc_sem, scatter=False)

        @pl.when((i == 0) & (j == 0))
        def _():
            _push_start(own, pairs)

        @pl.when(j == 0)
        def _():
            xv = x_ref[...]
            r = lax.rsqrt(jnp.mean(xv * xv, axis=-1, keepdims=True) + EPS)
            h = (xv * r) * g_ref[...]
            hb = _bf(h)
            h_scr[...] = hb
            for b in range(tm // TBLK):
                ht_ref[b] = _bf(h[b * TBLK:(b + 1) * TBLK].T)
            rank_ref[...] = _dot_nt(hb, wr_ref[...])

        is_b = (j == 1) | ((j >= 3) & (j <= 5))

        @pl.when(is_b)
        def _():
            pb_ref[...] = _bf(_dot_nt(h_scr[...], w_ref[...]))

        @pl.when(jnp.logical_not(is_b))
        def _():
            pf_ref[...] = _dot_nt(h_scr[...], w_ref[...])

        @pl.when((i == n_i - 1) & (j == N_GROUPS - 1))
        def _():
            _push_wait(own, pairs)

    return pl.pallas_call(
        body, name="proj",
        grid=(n_i, N_GROUPS),
        in_specs=[pl.BlockSpec((tm, D), lambda i, j: (i, 0)),
                  pl.BlockSpec((1, D), lambda i, j: (0, 0)),
                  pl.BlockSpec((pl.Element(1024), pl.Element(D)), lambda i, j: (_group_row(j), 0)),
                  pl.BlockSpec((128, D), lambda i, j: (0, 0)),
                  _ANY],
        out_specs=(pl.BlockSpec((None, tm, 1024), lambda i, j: (f_slot(j), i, 0)),
                   pl.BlockSpec((None, tm, 1024), lambda i, j: (b_slot(j), i, 0)),
                   pl.BlockSpec((tm, 128), lambda i, j: (i, 0)),
                   pl.BlockSpec((tm // TBLK, D, TBLK), lambda i, j: (i, 0, 0)),
                   _ANY),
        out_shape=(jax.ShapeDtypeStruct((5, T, 1024), F32),
                   jax.ShapeDtypeStruct((4, T, 1024), BF16),
                   jax.ShapeDtypeStruct((T, 128), F32),
                   jax.ShapeDtypeStruct((T // TBLK, D, TBLK), BF16),
                   jax.ShapeDtypeStruct((N_DEV,) + wp_part.shape, wp_part.dtype)),
        scratch_shapes=[pltpu.VMEM((tm, D), BF16)] + _PUSH_SEMS,
        compiler_params=_cparams(("arbitrary", "arbitrary")),
    )(x, norm_g, wt, wr, wp_part)


GLA_STEP_CHUNKS = 4


def _gla_same_chunk(rows):
    return (_iota2(rows, rows, 0) & -GLA_CHUNK) == (_iota2(rows, rows, 1) & -GLA_CHUNK)


def _gla_chunk_terms(la, q, k, n_c):
    C = GLA_CHUNK
    rows = n_c * C
    low = _gla_same_chunk(rows) & (_iota2(rows, rows, 0) >= _iota2(rows, rows, 1))
    b = _tri_left(_bf(low.astype(F32)), la)
    bl = [b[(c + 1) * C - 1:(c + 1) * C, :] for c in range(n_c)]
    bl_rows = jnp.concatenate([jnp.broadcast_to(bl[c], (C, b.shape[1])) for c in range(n_c)], axis=0)
    eb = jnp.exp(b)
    enb = jnp.exp(-b)
    ebl_b = jnp.exp(bl_rows - b)
    scale = GLA_HK ** -0.5
    qe = q * eb * scale
    ke = k * enb
    kd = k * ebl_b
    return bl, eb, enb, ebl_b, qe, ke, kd


def _gla_fwd_call(projf, projb, rank, wdec, bdec):
    T = projf.shape[1]
    C = GLA_CHUNK
    n_chunks = T // C
    n_c = GLA_STEP_CHUNKS
    R = n_c * C
    assert n_chunks % n_c == 0

    def body(qk_ref, v_ref, rank_ref, wd_ref, bd_ref, o_ref, st_ref, la_ref, st_scr):
        @pl.when(pl.program_id(0) == 0)
        def _():
            st_scr[...] = jnp.zeros_like(st_scr)

        dec = _dot(_bf(rank_ref[...]), _bf(wd_ref[...])) + bd_ref[...]
        la = (jnp.minimum(dec, 0.0) - _softplus_neg_abs(dec)) / GLA_TAU
        la_ref[...] = la
        mask = _gla_same_chunk(R) & (_iota2(R, R, 0) >= _iota2(R, R, 1))
        bl, _, _, _, qe, ke, kd = _gla_chunk_terms(la, qk_ref[:, :GLA_DK], qk_ref[:, GLA_DK:], n_c)
        qeb, keb, kdb = _bf(qe), _bf(ke), _bf(kd)
        ebl = [jnp.exp(bl[c]) for c in range(n_c)]
        heads = range(GLA_HEADS)
        ks = [slice(hh * GLA_HK, (hh + 1) * GLA_HK) for hh in heads]
        vs = [slice(hh * GLA_HV, (hh + 1) * GLA_HV) for hh in heads]
        rs = [slice(c * C, (c + 1) * C) for c in range(n_c)]
        p = [_bf(jnp.where(mask, _dot_nt(qeb[:, ks[hh]], keb[:, ks[hh]]), 0.0)) for hh in heads]
        upd = [[_dot_tn(v_ref[rs[c], vs[hh]], kdb[rs[c], ks[hh]]) for hh in heads] for c in range(n_c)]
        intra = [_dot(p[hh], v_ref[:, vs[hh]]) for hh in heads]
        st = [st_scr[hh] for hh in heads]
        for c in range(n_c):
            inter = [_dot_nt(qeb[rs[c], ks[hh]], _bf(st[hh])) for hh in heads]
            for hh in heads:
                st_ref[c, hh] = st[hh]
                o_ref[rs[c], vs[hh]] = intra[hh][rs[c]] + inter[hh]
            st = [st[hh] * ebl[c][:, ks[hh]] + upd[c][hh] for hh in heads]
        for hh in heads:
            st_scr[hh] = st[hh]

    return pl.pallas_call(
        body, name="gla_fwd",
        grid=(n_chunks // n_c,),
        in_specs=[pl.BlockSpec((None, R, 1024), lambda n: (0, n, 0)),
                  pl.BlockSpec((None, R, 1024), lambda n: (0, n, 0)),
                  pl.BlockSpec((R, 128), lambda n: (n, 0)),
                  pl.BlockSpec((128, GLA_DK), lambda n: (0, 0)),
                  pl.BlockSpec((1, GLA_DK), lambda n: (0, 0))],
        out_specs=(pl.BlockSpec((R, 1024), lambda n: (n, 0)),
                   pl.BlockSpec((n_c, GLA_HEADS, GLA_HV, GLA_HK), lambda n: (n, 0, 0, 0)),
                   pl.BlockSpec((R, GLA_DK), lambda n: (n, 0))),
        out_shape=(jax.ShapeDtypeStruct((T, 1024), F32),
                   jax.ShapeDtypeStruct((n_chunks, GLA_HEADS, GLA_HV, GLA_HK), F32),
                   jax.ShapeDtypeStruct((T, GLA_DK), F32)),
        scratch_shapes=[pltpu.VMEM((GLA_HEADS, GLA_HV, GLA_HK), F32)],
        compiler_params=_cparams(("arbitrary",)),
    )(projf, projb, rank, wdec, bdec)


def _gla_bwd_call(projf, projb, la, do_gla, st_all, rank, wdec):
    T = projf.shape[1]
    C = GLA_CHUNK
    n_chunks = T // C
    n_c = GLA_STEP_CHUNKS
    R = n_c * C
    assert n_chunks % n_c == 0
    last = n_chunks // n_c - 1

    def body(qk_ref, v_ref, la_ref, do_ref, st_ref, rank_ref, wd_ref,
             dqk_ref, dv_ref, drank_ref, dwd_ref, dbd_ref, dst_scr):
        @pl.when(pl.program_id(0) == 0)
        def _():
            dst_scr[...] = jnp.zeros_like(dst_scr)
            dwd_ref[...] = jnp.zeros_like(dwd_ref)
            dbd_ref[...] = jnp.zeros_like(dbd_ref)

        same = _gla_same_chunk(R)
        mask = same & (_iota2(R, R, 0) >= _iota2(R, R, 1))
        upp = _bf((same & (_iota2(R, R, 0) <= _iota2(R, R, 1))).astype(F32))
        scale = GLA_HK ** -0.5
        la = la_ref[...]
        bl, eb, enb, ebl_b, qe, ke, kd = _gla_chunk_terms(la, qk_ref[:, :GLA_DK], qk_ref[:, GLA_DK:], n_c)
        qeb, keb, kdb = _bf(qe), _bf(ke), _bf(kd)
        ebl = [jnp.exp(bl[c]) for c in range(n_c)]
        heads = range(GLA_HEADS)
        ks = [slice(hh * GLA_HK, (hh + 1) * GLA_HK) for hh in heads]
        vs = [slice(hh * GLA_HV, (hh + 1) * GLA_HV) for hh in heads]
        rs = [slice(c * C, (c + 1) * C) for c in range(n_c)]
        v = [v_ref[:, vs[hh]] for hh in heads]
        do = [_bf(do_ref[:, vs[hh]]) for hh in heads]
        p = [_bf(jnp.where(mask, _dot_nt(qeb[:, ks[hh]], keb[:, ks[hh]]), 0.0)) for hh in heads]
        dp = [_bf(jnp.where(mask, _dot_nt(do[hh], v[hh]), 0.0)) for hh in heads]
        dst_intra = [[_dot_tn(do[hh][rs[c]], qeb[rs[c], ks[hh]]) for hh in heads] for c in range(n_c)]
        dqe_inter = [[_dot(do[hh][rs[c]], _bf(st_ref[c, hh])) for hh in heads] for c in range(n_c)]
        dv_intra = [_dot_tn(p[hh], do[hh]) for hh in heads]
        dqe_intra = [_dot(dp[hh], keb[:, ks[hh]]) for hh in heads]
        dke = jnp.concatenate([_dot_tn(dp[hh], qeb[:, ks[hh]]) for hh in heads], axis=1)
        dstn = [dst_scr[hh] for hh in heads]
        dkd_c, dv_inter, debl = [None] * n_c, [None] * n_c, [None] * n_c
        for c in reversed(range(n_c)):
            dstnb = [_bf(dstn[hh]) for hh in heads]
            dkd_c[c] = jnp.concatenate([_dot(v[hh][rs[c]], dstnb[hh]) for hh in heads], axis=1)
            dv_inter[c] = [_dot_nt(kdb[rs[c], ks[hh]], dstnb[hh]) for hh in heads]
            debl[c] = jnp.concatenate(
                [jnp.sum(dstn[hh] * st_ref[c, hh], axis=0, keepdims=True) for hh in heads], axis=1)
            dstn = [dst_intra[c][hh] + dstn[hh] * ebl[c][:, ks[hh]] for hh in heads]
        for hh in heads:
            dst_scr[hh] = dstn[hh]
            dv_ref[:, vs[hh]] = _bf(dv_intra[hh] + jnp.concatenate([dv_inter[c][hh] for c in range(n_c)], axis=0))
        dqe = jnp.concatenate(
            [dqe_intra[hh] + jnp.concatenate([dqe_inter[c][hh] for c in range(n_c)], axis=0) for hh in heads], axis=1)
        dkd = jnp.concatenate(dkd_c, axis=0)
        dkd_kd = dkd * kd
        db = dqe * qe - dke * ke - dkd_kd
        dbl = jnp.concatenate(
            [jnp.broadcast_to(jnp.sum(dkd_kd[rs[c]], axis=0, keepdims=True) + ebl[c] * debl[c], (C, GLA_DK))
             for c in range(n_c)], axis=0)
        dla = _tri_left(upp, db) + dbl
        dqk_ref[:, :GLA_DK] = _bf(dqe * eb * scale)
        dqk_ref[:, GLA_DK:] = _bf(dke * enb + dkd * ebl_b)
        ddec = dla * (1.0 / GLA_TAU) * (1.0 - jnp.exp(GLA_TAU * la))
        ddecb = _bf(ddec)
        drank_ref[...] = _bf(_dot_nt(ddecb, _bf(wd_ref[...])))
        dwd_ref[...] += _dot_tn(_bf(rank_ref[...]), ddecb)
        dbd_ref[...] += jnp.sum(ddec, axis=0, keepdims=True)

    return pl.pallas_call(
        body, name="gla_bwd",
        grid=(n_chunks // n_c,),
        in_specs=[pl.BlockSpec((None, R, 1024), lambda n: (0, last - n, 0)),
                  pl.BlockSpec((None, R, 1024), lambda n: (0, last - n, 0)),
                  pl.BlockSpec((R, GLA_DK), lambda n: (last - n, 0)),
                  pl.BlockSpec((R, 1024), lambda n: (last - n, 0)),
                  pl.BlockSpec((n_c, GLA_HEADS, GLA_HV, GLA_HK), lambda n: (last - n, 0, 0, 0)),
                  pl.BlockSpec((R, 128), lambda n: (last - n, 0)),
                  pl.BlockSpec((128, GLA_DK), lambda n: (0, 0))],
        out_specs=(pl.BlockSpec((R, 1024), lambda n: (last - n, 0)),
                   pl.BlockSpec((R, 1024), lambda n: (last - n, 0)),
                   pl.BlockSpec((R, 128), lambda n: (last - n, 0)),
                   pl.BlockSpec((128, GLA_DK), lambda n: (0, 0)),
                   pl.BlockSpec((1, GLA_DK), lambda n: (0, 0))),
        out_shape=(jax.ShapeDtypeStruct((T, 1024), BF16),
                   jax.ShapeDtypeStruct((T, 1024), BF16),
                   jax.ShapeDtypeStruct((T, 128), BF16),
                   jax.ShapeDtypeStruct((128, GLA_DK), F32),
                   jax.ShapeDtypeStruct((1, GLA_DK), F32)),
        scratch_shapes=[pltpu.VMEM((GLA_HEADS, GLA_HV, GLA_HK), F32)],
        compiler_params=_cparams(("arbitrary",)),
    )(projf, projb, la, do_gla, st_all, rank, wdec)


def _sb_logs(z):
    lsz = jnp.minimum(z, 0.0) - _softplus_neg_abs(z)
    return lsz, lsz - z


SB_HG_FWD = 8
SB_HG_BWD = 4
SB_QUERIES = 256
SB_KEYS = 256
SB_DEAD = -105.0


def _sb_fwd_call(projb, wp_shard):
    T = projb.shape[1]
    B = min(SB_QUERIES, T)
    HG = SB_HG_FWD
    W = HG * SB_HD
    scale = 1.0 / math.sqrt(SB_HD)
    KB = min(SB_KEYS, T)
    n_h, n_i = SB_HEADS // HG, T // B

    def body(q_ref, k_ref, v_ref, wp_ref, o_ref, wpall_ref, cb_scr, send_sems, recv_sems, loc_sem):
        i = pl.program_id(1)
        own, pairs = _push_copies(wp_ref, wpall_ref, send_sems, recv_sems, loc_sem, scatter=False)

        @pl.when((pl.program_id(0) == 0) & (i == 0))
        def _():
            _push_start(own, pairs)

        rows = HG * B
        after = (_iota2(KB, KB, 0) > _iota2(KB, KB, 1)).astype(F32)
        tri = _bf(jnp.concatenate([after, jnp.ones((KB, KB), F32)], axis=1))
        o_ref[...] = jnp.zeros_like(o_ref)
        cb_scr[...] = jnp.zeros_like(cb_scr)

        def block(jp, masked):
            off = pl.multiple_of(jp * KB, KB)
            z = jnp.concatenate(
                [_dot_nt(q_ref[:, hh * SB_HD:(hh + 1) * SB_HD], k_ref[pl.ds(off, KB), hh * SB_HD:(hh + 1) * SB_HD])
                 for hh in range(HG)], axis=0) * scale
            lsz, l1m = _sb_logs(z)
            if masked:
                strict = (jp * KB + _iota2(rows, KB, 1)) < (i * B + (_iota2(rows, KB, 0) & (B - 1)))
                l1m = jnp.where(strict, l1m, 0.0)
            r = _tri2_right(l1m, tri)
            cb = cb_scr[...]
            a = jnp.exp(lsz + cb + r[:, :KB])
            if masked:
                a = jnp.where(strict, a, 0.0)
            cb_scr[...] = cb + r[:, KB:]
            ab = _bf(a)
            for hh in range(HG):
                cs = slice(hh * SB_HD, (hh + 1) * SB_HD)
                o_ref[:, cs] += _dot(ab[hh * B:(hh + 1) * B, :], v_ref[pl.ds(off, KB), cs])

        jp0 = (i * B) // KB
        block(jp0, True)

        def live(state):
            jj, dead = state
            return (jj <= jp0) & jnp.logical_not(dead)

        def step(state):
            jj, _ = state
            block(jp0 - jj, False)
            return jj + 1, jnp.max(cb_scr[:, :SB_HD]) < SB_DEAD

        lax.while_loop(live, step, (jnp.int32(1), jnp.max(cb_scr[:, :SB_HD]) < SB_DEAD))

        @pl.when((pl.program_id(0) == n_h - 1) & (i == n_i - 1))
        def _():
            _push_wait(own, pairs)

    return pl.pallas_call(
        body, name="sb_fwd",
        grid=(n_h, n_i),
        in_specs=[pl.BlockSpec((None, B, W), lambda h, i: (1, i, h)),
                  pl.BlockSpec((None, T, W), lambda h, i: (2, 0, h)),
                  pl.BlockSpec((None, T, W), lambda h, i: (3, 0, h)),
                  _ANY],
        out_specs=(pl.BlockSpec((B, W), lambda h, i: (i, h)), _ANY),
        out_shape=(jax.ShapeDtypeStruct((T, 1024), F32),
                   jax.ShapeDtypeStruct((N_DEV,) + wp_shard.shape, wp_shard.dtype)),
        scratch_shapes=[pltpu.VMEM((HG * B, KB), F32)] + _PUSH_SEMS,
        compiler_params=_cparams(("arbitrary", "arbitrary")),
    )(projb, projb, projb, wp_shard)


def _sb_bwd_call(projb, do_sb, g_p):
    T = projb.shape[1]
    B = min(SB_QUERIES, T)
    nb = T // B
    HG = SB_HG_BWD
    W = HG * SB_HD
    WQ = HG * B
    KB = min(SB_KEYS, T)
    nkb = T // KB
    n_h = SB_HEADS // HG
    scale = 1.0 / math.sqrt(SB_HD)

    def body(q_ref, k_ref, v_ref, do_ref, gp_ref, dq_ref, dk_ref, dv_ref, rp_ref,
             dk_scr, dv_scr, kt_scr, beta_scr, g_scr, dqt_scr, send_sems, recv_sems, loc_sem):
        i = pl.program_id(1)
        own, pairs = _push_copies(gp_ref, rp_ref, send_sems, recv_sems, loc_sem, scatter=True)

        @pl.when((pl.program_id(0) == 0) & (i == 0))
        def _():
            _push_start(own, pairs)

        @pl.when(i == 0)
        def _():
            dk_scr[...] = jnp.zeros_like(dk_scr)
            dv_scr[...] = jnp.zeros_like(dv_scr)
            for hh in range(HG):
                for jb in range(nkb):
                    kt_scr[hh, jb] = _bf(
                        k_ref[jb * KB:(jb + 1) * KB, hh * SB_HD:(hh + 1) * SB_HD].astype(F32).T)

        dqt_scr[...] = jnp.zeros_like(dqt_scr)
        later = _bf((_iota2(KB, KB, 1) > _iota2(KB, KB, 0)).astype(F32))
        earlier = _bf((_iota2(KB, KB, 1) < _iota2(KB, KB, 0)).astype(F32))
        dob = _bf(do_ref[...])
        jp0 = (i * B) // KB

        def strict_mask():
            return (jp0 * KB + _iota2(KB, WQ, 0)) < (i * B + (_iota2(KB, WQ, 1) & (B - 1)))

        def heads(fn):
            return [fn(slice(hh * SB_HD, (hh + 1) * SB_HD)) for hh in range(HG)]

        def pass1(jp, cb, masked):
            off = pl.multiple_of(jp * KB, KB)
            z = jnp.concatenate(heads(lambda cs: _dot_nt(k_ref[pl.ds(off, KB), cs], q_ref[:, cs])), axis=1) * scale
            da = jnp.concatenate(heads(lambda cs: _dot_nt(v_ref[pl.ds(off, KB), cs], dob[:, cs])), axis=1)
            lsz, l1m = _sb_logs(z)
            if masked:
                strict = strict_mask()
                l1m = jnp.where(strict, l1m, 0.0)
            a = jnp.exp(lsz + cb + _tri2_left(later, l1m))
            if masked:
                a = jnp.where(strict, a, 0.0)
            g_scr[jp] = a * da
            beta_scr[jp] = jnp.exp(lsz)
            ab = _bf(a)
            for hh in range(HG):
                cs = slice(hh * SB_HD, (hh + 1) * SB_HD)
                dv_scr[pl.ds(off, KB), cs] += _dot(ab[:, hh * B:(hh + 1) * B], dob[:, cs])
            return cb + jnp.sum(l1m, axis=0, keepdims=True)

        zero = jnp.zeros((1, WQ), F32)
        cb = pass1(jp0, zero, True)

        def live(state):
            jj, _, dead = state
            return (jj <= jp0) & jnp.logical_not(dead)

        def step(state):
            jj, cr, _ = state
            cr = pass1(jp0 - jj, cr, False)
            return jj + 1, cr, jnp.max(cr) < SB_DEAD

        n_done, _, _ = lax.while_loop(live, step, (jnp.int32(1), cb, jnp.max(cb) < SB_DEAD))
        jp_first = jp0 - (n_done - 1)

        def pass2(jp, cg, masked):
            off = pl.multiple_of(jp * KB, KB)
            g = g_scr[jp]
            beta = beta_scr[jp]
            dz = g * (1.0 - beta) - beta * (cg + _tri2_left(earlier, g))
            if masked:
                dz = jnp.where(strict_mask(), dz, 0.0)
            dzb = _bf(dz * scale)
            for hh in range(HG):
                cs = slice(hh * SB_HD, (hh + 1) * SB_HD)
                dk_scr[pl.ds(off, KB), cs] += _dot(dzb[:, hh * B:(hh + 1) * B], q_ref[:, cs])
                dqt_scr[hh] += _dot(kt_scr[hh, jp], dzb[:, hh * B:(hh + 1) * B])
            return cg + jnp.sum(g, axis=0, keepdims=True)

        cg = lax.fori_loop(jp_first, jp0, lambda jp, cr: pass2(jp, cr, False), zero)
        pass2(jp0, cg, True)
        for hh in range(HG):
            dq_ref[:, hh * SB_HD:(hh + 1) * SB_HD] = _bf(dqt_scr[hh].T)

        @pl.when(i == nb - 1)
        def _():
            dk_ref[...] = _bf(dk_scr[...])
            dv_ref[...] = _bf(dv_scr[...])

        @pl.when((pl.program_id(0) == n_h - 1) & (i == nb - 1))
        def _():
            _push_wait(own, pairs)

    return pl.pallas_call(
        body, name="sb_bwd",
        grid=(n_h, nb),
        in_specs=[pl.BlockSpec((None, B, W), lambda h, i: (1, i, h)),
                  pl.BlockSpec((None, T, W), lambda h, i: (2, 0, h)),
                  pl.BlockSpec((None, T, W), lambda h, i: (3, 0, h)),
                  pl.BlockSpec((B, W), lambda h, i: (i, h)),
                  _ANY],
        out_specs=(pl.BlockSpec((B, W), lambda h, i: (i, h)),
                   pl.BlockSpec((T, W), lambda h, i: (0, h)),
                   pl.BlockSpec((T, W), lambda h, i: (0, h)),
                   _ANY),
        out_shape=(jax.ShapeDtypeStruct((T, 1024), BF16),
                   jax.ShapeDtypeStruct((T, 1024), BF16),
                   jax.ShapeDtypeStruct((T, 1024), BF16),
                   jax.ShapeDtypeStruct(g_p.shape, g_p.dtype)),
        scratch_shapes=[pltpu.VMEM((T, W), F32), pltpu.VMEM((T, W), F32),
                        pltpu.VMEM((HG, nkb, SB_HD, KB), BF16),
                        pltpu.VMEM((nkb, KB, WQ), F32), pltpu.VMEM((nkb, KB, WQ), F32),
                        pltpu.VMEM((HG, SB_HD, B), F32)] + _PUSH_SEMS,
        compiler_params=_cparams(("arbitrary", "arbitrary")),
    )(projb, projb, projb, do_sb, g_p)


def _mid_call(o_gla, o_sb, projf, x, target, wpa, wpb, wo, gla_g, b_gate, final_g):
    T, D = x.shape
    tm = min(TBLK, T)

    def body(og_ref, ggate_ref, osb_ref, sgate_ref, ma_ref, mb_ref, x_ref, tgt_ref,
             wpa_ref, wpb_ref, wo_ref, glag_ref, bg_ref, fg_ref,
             dx2_ref, dogla_ref, dosb_ref, dggate_ref, dsgate_ref, dm_ref,
             mt_ref, ogt_ref, obt_ref, dx2b_ref, dya_ref, dyb_ref,
             dfg_ref, dbg_ref, dglag_ref, loss_ref):
        @pl.when(pl.program_id(0) == 0)
        def _():
            dfg_ref[...] = jnp.zeros_like(dfg_ref)
            dbg_ref[...] = jnp.zeros_like(dbg_ref)
            dglag_ref[...] = jnp.zeros_like(dglag_ref)
            loss_ref[...] = jnp.zeros_like(loss_ref)

        glag = glag_ref[...]
        ggate = ggate_ref[...]
        sg = _sigmoid(ggate)
        silu_g = ggate * sg
        ohat, rinv, nrm = [], [], []
        for hh in range(GLA_HEADS):
            oh = og_ref[:, hh * GLA_HV:(hh + 1) * GLA_HV]
            r = lax.rsqrt(jnp.mean(oh * oh, axis=-1, keepdims=True) + EPS)
            ohat.append(oh * r)
            rinv.append(r)
            nrm.append(ohat[-1] * glag)
        n_all = jnp.concatenate(nrm, axis=1)
        og = n_all * silu_g
        ogb = _bf(og)
        ya = _dot(ogb, wpa_ref[...])
        sgate = sgate_ref[...]
        ss = _sigmoid(sgate)
        silu_s = sgate * ss
        osb = osb_ref[...]
        ob = osb * silu_s
        obb = _bf(ob)
        yb = _dot(obb, wpb_ref[...])
        ga = _sigmoid(ma_ref[...] + bg_ref[:, :D])
        gb = _sigmoid(mb_ref[...] + bg_ref[:, D:])
        merged = ga * ya + gb * yb
        mgb = _bf(merged)
        x2 = x_ref[...] + _dot(mgb, wo_ref[...])
        r2 = lax.rsqrt(jnp.mean(x2 * x2, axis=-1, keepdims=True) + EPS)
        xh2 = x2 * r2
        fg = fg_ref[...]
        err = xh2 * fg - tgt_ref[...]
        loss_ref[...] += jnp.broadcast_to(
            0.5 * jnp.sum(jnp.mean(err * err, axis=-1, keepdims=True), axis=0, keepdims=True), (1, 128))
        dy = err * (1.0 / D)
        dfg_ref[...] += jnp.sum(dy * xh2, axis=0, keepdims=True)
        dxh = dy * fg
        dx2 = r2 * (dxh - xh2 * jnp.mean(dxh * xh2, axis=-1, keepdims=True))
        dx2_ref[...] = dx2
        dx2b = _bf(dx2)
        dx2b_ref[...] = dx2b
        dmerged = _dot_nt(dx2b, wo_ref[...])
        dya = dmerged * ga
        dyb = dmerged * gb
        dma = dmerged * ya * ga * (1.0 - ga)
        dmb = dmerged * yb * gb * (1.0 - gb)
        dm_ref[:, :D] = _bf(dma)
        dm_ref[:, D:] = _bf(dmb)
        dbg_ref[:, :D] += jnp.sum(dma, axis=0, keepdims=True)
        dbg_ref[:, D:] += jnp.sum(dmb, axis=0, keepdims=True)
        dyab = _bf(dya)
        dybb = _bf(dyb)
        dya_ref[...] = dyab
        dyb_ref[...] = dybb
        dog = _dot_nt(dyab, wpa_ref[...])
        dob = _dot_nt(dybb, wpb_ref[...])
        dosb_ref[...] = dob * silu_s
        dsgate_ref[...] = _bf(dob * osb * (ss * (1.0 + sgate * (1.0 - ss))))
        dn = dog * silu_g
        dggate_ref[...] = _bf(dog * n_all * (sg * (1.0 + ggate * (1.0 - sg))))
        dglag = jnp.zeros((1, GLA_HV), F32)
        for hh in range(GLA_HEADS):
            dnh = dn[:, hh * GLA_HV:(hh + 1) * GLA_HV]
            dglag = dglag + jnp.sum(dnh * ohat[hh], axis=0, keepdims=True)
            dohat = dnh * glag
            dogla_ref[:, hh * GLA_HV:(hh + 1) * GLA_HV] = rinv[hh] * (
                dohat - ohat[hh] * jnp.mean(dohat * ohat[hh], axis=-1, keepdims=True))
        dglag_ref[...] += dglag
        mt_ref[...] = _bf(merged.T)
        ogt_ref[...] = _bf(og.T)
        obt_ref[...] = _bf(ob.T)

    row = lambda i: (i, 0)
    const = lambda i: (0, 0)
    tile = pl.BlockSpec((tm, D), row)
    tile_t = pl.BlockSpec((None, D, tm), lambda i: (i, 0, 0))
    wspec = pl.BlockSpec((D, D), const)
    return pl.pallas_call(
        body, name="mid",
        grid=(T // tm,),
        in_specs=[tile,
                  pl.BlockSpec((None, tm, D), lambda i: (1, i, 0)),
                  tile,
                  pl.BlockSpec((None, tm, D), lambda i: (2, i, 0)),
                  pl.BlockSpec((None, tm, D), lambda i: (3, i, 0)),
                  pl.BlockSpec((None, tm, D), lambda i: (4, i, 0)),
                  tile, tile, wspec, wspec, wspec,
                  pl.BlockSpec((1, GLA_HV), const),
                  pl.BlockSpec((1, 2 * D), const),
                  pl.BlockSpec((1, D), const)],
        out_specs=(tile, tile, tile, tile, tile,
                   pl.BlockSpec((tm, 2 * D), row),
                   tile_t, tile_t, tile_t, tile, tile, tile,
                   pl.BlockSpec((1, D), const),
                   pl.BlockSpec((1, 2 * D), const),
                   pl.BlockSpec((1, GLA_HV), const),
                   pl.BlockSpec((1, 128), const)),
        out_shape=(jax.ShapeDtypeStruct((T, D), F32),
                   jax.ShapeDtypeStruct((T, D), F32),
                   jax.ShapeDtypeStruct((T, D), F32),
                   jax.ShapeDtypeStruct((T, D), BF16),
                   jax.ShapeDtypeStruct((T, D), BF16),
                   jax.ShapeDtypeStruct((T, 2 * D), BF16),
                   jax.ShapeDtypeStruct((T // tm, D, tm), BF16),
                   jax.ShapeDtypeStruct((T // tm, D, tm), BF16),
                   jax.ShapeDtypeStruct((T // tm, D, tm), BF16),
                   jax.ShapeDtypeStruct((T, D), BF16),
                   jax.ShapeDtypeStruct((T, D), BF16),
                   jax.ShapeDtypeStruct((T, D), BF16),
                   jax.ShapeDtypeStruct((1, D), F32),
                   jax.ShapeDtypeStruct((1, 2 * D), F32),
                   jax.ShapeDtypeStruct((1, GLA_HV), F32),
                   jax.ShapeDtypeStruct((1, 128), F32)),
        compiler_params=_cparams(("arbitrary",)),
    )(o_gla, projf, o_sb, projf, projf, projf, x, target, wpa, wpb, wo, gla_g, b_gate, final_g)


def _dh_call(pieces, dmlog, drank, wt, wr, x, dx2, norm_g, s_in, s_to_x, s_to_y):
    T, D = x.shape
    tm = min(256, T)
    npc = len(pieces)
    n_main = N_GROUPS * 1024
    n_i = T // tm

    def body(*refs):
        pcs = refs[:npc]
        (dm_ref, dr_ref, w_hbm, wr_ref, x_ref, dx2_ref, g_ref, sin_ref, sx_ref, sy_ref,
         gx_ref, dg_ref, rin_ref, w_scr, sems, send_sems, recv_sems, loc_sem) = refs[npc:]
        cx, cy, _ = _mesh_pos()
        own, pairs = _neighbour_copies(sin_ref.at[2 * cx + cy], sx_ref, sy_ref, rin_ref,
                                       send_sems, recv_sems, loc_sem)

        @pl.when(pl.program_id(0) == 0)
        def _():
            _push_start(own, pairs)
            lo = pltpu.make_async_copy(w_hbm.at[pl.ds(0, RANK_COL)], w_scr.at[pl.ds(0, RANK_COL)], sems.at[0])
            hi = pltpu.make_async_copy(w_hbm.at[pl.ds(RANK_COL + GLA_RANK, n_main - RANK_COL)],
                                       w_scr.at[pl.ds(RANK_COL, n_main - RANK_COL)], sems.at[1])
            lo.start()
            hi.start()
            dg_ref[...] = jnp.zeros_like(dg_ref)
            lo.wait()
            hi.wait()

        def w_group(g):
            return w_scr[g * 1024:(g + 1) * 1024, :]

        dr = dr_ref[...]
        dh = _dot(dr, wr_ref[...])
        for g in range(npc):
            dh = dh + _dot(pcs[g][...], w_group(g))
        dh = dh + _dot(dm_ref[:, :D], w_group(npc))
        dh = dh + _dot(dm_ref[:, D:], w_group(npc + 1))
        xv = x_ref[...]
        r = lax.rsqrt(jnp.mean(xv * xv, axis=-1, keepdims=True) + EPS)
        xhat = xv * r
        g = g_ref[...]
        dg_ref[...] += jnp.sum(dh * xhat, axis=0, keepdims=True)
        dxhat = dh * g
        gx_ref[...] = r * (dxhat - xhat * jnp.mean(dxhat * xhat, axis=-1, keepdims=True)) + dx2_ref[...]

        @pl.when(pl.program_id(0) == n_i - 1)
        def _():
            _push_wait(own, pairs)

    row = lambda i: (i, 0)
    const = lambda i: (0, 0)
    tile = pl.BlockSpec((tm, D), row)
    return pl.pallas_call(
        body, name="dh",
        grid=(n_i,),
        in_specs=[tile] * npc + [
            pl.BlockSpec((tm, 2 * D), row),
            pl.BlockSpec((tm, 128), row),
            _ANY,
            pl.BlockSpec((128, D), const),
            tile, tile,
            pl.BlockSpec((1, D), const),
            _ANY, _ANY, _ANY],
        out_specs=(tile, pl.BlockSpec((1, D), const), _ANY),
        out_shape=(jax.ShapeDtypeStruct((T, D), F32),
                   jax.ShapeDtypeStruct((1, D), F32),
                   jax.ShapeDtypeStruct((3,) + s_to_x.shape, s_in.dtype)),
        scratch_shapes=[pltpu.VMEM((n_main, D), BF16), pltpu.SemaphoreType.DMA((2,))] + _NEIGHBOUR_SEMS,
        compiler_params=_cparams(("arbitrary",)),
    )(*pieces, dmlog, drank, wt, wr, x, dx2, norm_g, s_in, s_to_x, s_to_y)


def _wgrad_rank_call(ht, drank):
    n_tb, D, tb = ht.shape

    def body(ht_ref, dr_ref, o_ref):
        @pl.when(pl.program_id(0) == 0)
        def _():
            o_ref[...] = jnp.zeros_like(o_ref)

        o_ref[...] += _dot(ht_ref[...], dr_ref[...])

    return pl.pallas_call(
        body, name="wgrad_rank",
        grid=(n_tb,),
        in_specs=[pl.BlockSpec((None, D, tb), lambda i: (i, 0, 0)),
                  pl.BlockSpec((tb, 128), lambda i: (i, 0))],
        out_specs=pl.BlockSpec((D, 128), lambda i: (0, 0)),
        out_shape=jax.ShapeDtypeStruct((D, 128), F32),
        compiler_params=_cparams(("arbitrary",)),
    )(ht, drank)


def _wgrad_call(lhs_list, lhs_of_group, rhs_list, rhs_of_group, n_transposed, name):
    n_groups = len(rhs_of_group)
    n_tb, D, tb = lhs_list[0].shape
    T = n_tb * tb
    per = min(4, n_tb)
    tk = per * tb
    nk = T // tk
    nl = len(lhs_list)

    def body(*refs):
        lhs = refs[:nl]
        rhs = refs[nl:nl + n_groups]
        out_ref, acc = refs[nl + n_groups:]
        g = pl.program_id(0)
        i = pl.program_id(1)

        @pl.when(i == 0)
        def _():
            acc[...] = jnp.zeros_like(acc)

        for p in range(n_groups):
            @pl.when(g == p)
            def _(p=p):
                lref = lhs[lhs_of_group[p]]
                part = _dot(lref[0], rhs[p][0:tb, :])
                for b in range(1, per):
                    part = part + _dot(lref[b], rhs[p][b * tb:(b + 1) * tb, :])
                acc[...] += part

        @pl.when((i == nk - 1) & (g < n_transposed))
        def _():
            out_ref[...] = _bf(acc[...].T)

        @pl.when((i == nk - 1) & (g >= n_transposed))
        def _():
            out_ref[...] = _bf(acc[...])

    def lhs_spec(a):
        groups = [g for g in range(n_groups) if lhs_of_group[g] == a]
        lo, hi = min(groups), max(groups)
        assert groups == list(range(lo, hi + 1))
        return pl.BlockSpec((per, D, tb), lambda g, i: (jnp.where((g >= lo) & (g <= hi), i, 0), 0, 0))

    def rhs_spec(p):
        cb = rhs_of_group[p][1]
        return pl.BlockSpec((tk, 1024), lambda g, i: (jnp.where(g == p, i, 0), cb))

    return pl.pallas_call(
        body, name=name,
        grid=(n_groups, nk),
        in_specs=[lhs_spec(a) for a in range(nl)] + [rhs_spec(p) for p in range(n_groups)],
        out_specs=pl.BlockSpec((None, D, 1024), lambda g, i: (g, 0, 0)),
        out_shape=jax.ShapeDtypeStruct((n_groups, D, 1024), BF16),
        scratch_shapes=[pltpu.VMEM((D, 1024), F32)],
        compiler_params=_cparams(("arbitrary", "arbitrary")),
    )(*lhs_list, *[rhs_list[rhs_of_group[p][0]] for p in range(n_groups)])


def _adamw_math(parts, w, m, v):
    g = parts[0].astype(F32)
    for p in parts[1:]:
        g = g + p.astype(F32)
    mm = ADAM_B1 * m + (1.0 - ADAM_B1) * g
    vv = ADAM_B2 * v + (1.0 - ADAM_B2) * (g * g)
    m_hat = mm / (1.0 - ADAM_B1 ** ADAM_STEP)
    v_hat = vv / (1.0 - ADAM_B2 ** ADAM_STEP)
    return g, -ADAM_LR * (m_hat / (jnp.sqrt(v_hat) + ADAM_EPS) + ADAM_WD * w), mm, vv


def _part_order(n_parts):
    return [n_parts - 1] + list(range(n_parts - 1))


def _adamw_call(parts, w, m, v, name):
    R, C = w.shape
    n_parts = parts.shape[0]
    (tr, tc), grid, idx = _tiling_2d(R, C, 512)

    def body(p_ref, w_ref, m_ref, v_ref, g_ref, d_ref, nm_ref, nv_ref):
        g_ref[...], d_ref[...], nm_ref[...], nv_ref[...] = _adamw_math(
            [p_ref[k] for k in _part_order(n_parts)], w_ref[...], m_ref[...], v_ref[...])

    blk = pl.BlockSpec((tr, tc), idx)
    sds = jax.ShapeDtypeStruct((R, C), F32)
    return pl.pallas_call(
        body, name=name,
        grid=grid,
        in_specs=[pl.BlockSpec((n_parts, tr, tc), lambda i: (0,) + idx(i)), blk, blk, blk],
        out_specs=(blk, blk, blk, blk),
        out_shape=(sds, sds, sds, sds),
        compiler_params=_cparams(("arbitrary",)),
    )(parts, w, m, v)


def _adamw_rows_call(parts, ws, ms, vs, name, gathered):
    n = len(ws)
    R, C = ws[0].shape
    n_parts = parts.shape[0]

    def body(*refs):
        p_ref = refs[0]
        w_refs, m_refs, v_refs = refs[1:1 + n], refs[1 + n:1 + 2 * n], refs[1 + 2 * n:1 + 3 * n]
        src_ref = refs[1 + 3 * n]
        outs = refs[2 + 3 * n:2 + 7 * n]
        dst_ref, send_sems, recv_sems, loc_sem = refs[2 + 7 * n:]
        own, pairs = _push_copies(src_ref, dst_ref, send_sems, recv_sems, loc_sem, scatter=False)
        k_now = pl.program_id(0)

        @pl.when(k_now == 0)
        def _():
            _push_start(own, pairs)

        for k in range(n):
            @pl.when(k_now == k)
            def _(k=k):
                res = _adamw_math([p_ref[j] for j in _part_order(n_parts)],
                                  w_refs[k][...], m_refs[k][...], v_refs[k][...])
                for o_ref, val in zip(outs[4 * k:4 * k + 4], res):
                    o_ref[...] = val

        @pl.when(k_now == n - 1)
        def _():
            _push_wait(own, pairs)

    whole = pl.BlockSpec((R, C), lambda k: (0, 0))
    sds = jax.ShapeDtypeStruct((R, C), F32)
    res = pl.pallas_call(
        body, name=name,
        grid=(n,),
        in_specs=[pl.BlockSpec((n_parts, R, C), lambda k: (0, k, 0))] + [whole] * (3 * n) + [_ANY],
        out_specs=tuple([whole] * (4 * n) + [_ANY]),
        out_shape=tuple([sds] * (4 * n) + [jax.ShapeDtypeStruct((N_DEV,) + gathered.shape, gathered.dtype)]),
        scratch_shapes=_PUSH_SEMS,
        compiler_params=_cparams(("arbitrary",)),
    )(parts, *ws, *ms, *vs, gathered)
    return [res[4 * k:4 * k + 4] for k in range(n)], res[4 * n]


def _adamw_lanes_call(parts, offsets, ws, ms, vs, name):
    n = len(ws)
    n_parts = parts.shape[0]

    def body(*refs):
        p_ref = refs[0]
        w_refs, m_refs, v_refs = refs[1:1 + n], refs[1 + n:1 + 2 * n], refs[1 + 2 * n:1 + 3 * n]
        outs = refs[1 + 3 * n:]
        for k in range(n):
            lanes = slice(offsets[k], offsets[k] + ws[k].shape[1])
            res = _adamw_math([p_ref[j, :, lanes] for j in _part_order(n_parts)],
                              w_refs[k][...], m_refs[k][...], v_refs[k][...])
            for o_ref, val in zip(outs[4 * k:4 * k + 4], res):
                o_ref[...] = val

    res = pl.pallas_call(
        body, name=name,
        out_shape=tuple(jax.ShapeDtypeStruct(ws[k].shape, F32) for k in range(n) for _ in range(4)),
        compiler_params=_cparams(),
    )(parts, *ws, *ms, *vs)
    return [res[4 * k:4 * k + 4] for k in range(n)]


def _local_step(x, target, wt, wr, wdec, bdec, wp_shard, norm_g, gla_g, b_gate, final_g):
    D = x.shape[1]
    half = wp_shard.shape[1] // 2
    projf, projb, rank, ht, wp_lo = _proj_call(x, norm_g, wt, wr, wp_shard[:, :half])
    o_gla, st_all, la = _gla_fwd_call(projf, projb, rank, wdec, bdec)
    o_sb, wp_hi = _sb_fwd_call(projb, wp_shard[:, half:])
    wp_full = jnp.concatenate([wp_lo, wp_hi], axis=2).transpose(1, 0, 2, 3).reshape(3, D, D)
    (dx2, do_gla, do_sb, dggate, dsgate, dmlog, mt, ogt, obt, dx2b, dya, dyb,
     dfinal_g, db_gate, dgla_g, loss) = _mid_call(o_gla, o_sb, projf, x, target, wp_full[0], wp_full[1],
                                                 wp_full[2], gla_g, b_gate, final_g)
    dw_p = _wgrad_call([ogt, obt, mt], [0, 1, 2], [dya, dyb, dx2b], [(0, 0), (1, 0), (2, 0)], 0, "wgrad_p")
    g_p = dw_p.reshape(3, N_DEV, D // N_DEV, D).transpose(1, 0, 2, 3).reshape(N_DEV, 3 * (D // N_DEV), D)
    dqk, dgv, drank, dwdec, dbdec = _gla_bwd_call(projf, projb, la, do_gla, st_all, rank, wdec)
    dsq, dsk, dsv, r_p = _sb_bwd_call(projb, do_sb, g_p)
    pieces = [dqk, dgv, dggate, dsq, dsk, dsv, dsgate]
    rhs_of_group = [(g, 0) for g in range(7)] + [(7, 0), (7, 1)]
    dw_in = _wgrad_call([ht], [0] * N_GROUPS, pieces + [dmlog], rhs_of_group, N_GROUPS, "wgrad_in")
    dwr = _wgrad_rank_call(ht, drank)
    g_in = _parts_by_device_call(dw_in.reshape(N_GROUPS * 1024, D), dwr[:, :GLA_RANK].T.astype(BF16))
    cx, cy, cc = (lax.axis_index(a).astype(jnp.int32) for a in ("x", "y", "c"))
    (p_in,) = _pair_exchange([g_in], "pair_g")
    s_in = _pair_add_call(g_in, p_in, cc.reshape(1), "pair_add_in")
    relayed = _diag_relay_call(s_in)
    s_to_x, s_to_y = _relay_add_call(s_in, relayed, jnp.stack([2 * (1 - cx) + cy, 2 * cx + (1 - cy)]))
    grad_x, dnorm_g, r_in = _dh_call(pieces, dmlog, drank, wt, wr, x, dx2, norm_g, s_in, s_to_x, s_to_y)
    small = jnp.concatenate([
        dnorm_g.reshape(-1), dbdec.reshape(-1), dgla_g.reshape(-1), db_gate.reshape(-1), dfinal_g.reshape(-1),
        loss.reshape(-1), dwdec[:GLA_RANK].reshape(-1)]).reshape(1, _SM_LEN)
    return grad_x, r_in, r_p, small


_SM_NORM = 0
_SM_BDEC = _SM_NORM + D_MODEL
_SM_GLAG = _SM_BDEC + GLA_DK
_SM_BGATE = _SM_GLAG + GLA_HV
_SM_FINAL = _SM_BGATE + 2 * D_MODEL
_SM_REPL = _SM_FINAL + D_MODEL
_SM_LOSS = _SM_REPL
_SM_WDEC = _SM_LOSS + 128
_SM_LEN = _SM_WDEC + GLA_RANK * GLA_DK


def kernel(x, norm_g, w_in, w_dec_up, b_dec, gla_norm_g, w_pa, w_pb, b_gate, w_o, final_g, loss_target, m_norm_g, m_w_in, m_w_dec_up, m_b_dec, m_gla_norm_g, m_w_pa, m_w_pb, m_b_gate, m_w_o, m_final_g, v_norm_g, v_w_in, v_w_dec_up, v_b_dec, v_gla_norm_g, v_w_pa, v_w_pb, v_b_gate, v_w_o, v_final_g):
    D = D_MODEL
    me = 4 * lax.axis_index("x") + 2 * lax.axis_index("y") + lax.axis_index("c")

    wp_shard = jnp.stack([w_pa, w_pb, w_o]).astype(BF16)
    n_first = _half_rows(SHARD_COLS)
    win_all, wdec_all = _all_gather([w_in.T.astype(BF16), w_dec_up], "gather_w",
                                    row_pieces=[[(0, n_first), (n_first, SHARD_COLS - n_first)], None])
    wt = _flatten_blocks_call(win_all)
    wr = jnp.pad(wt[RANK_COL:RANK_COL + GLA_RANK], ((0, 128 - GLA_RANK), (0, 0)))
    wdec_full = wdec_all.transpose(1, 0, 2).reshape(GLA_RANK, GLA_DK)
    wdec = jnp.pad(wdec_full, ((0, 128 - GLA_RANK), (0, 0)))

    grad_x, r_in, r_p, small = _local_step(
        x[0], loss_target[0], wt, wr, wdec, b_dec.reshape(1, -1), wp_shard,
        norm_g.reshape(1, -1), gla_norm_g.reshape(1, -1), b_gate.reshape(1, -1), final_g.reshape(1, -1))

    gw_in, d_in, nm_in, nv_in = (a.T for a in _adamw_call(r_in, w_in.T, m_w_in.T, v_w_in.T, "adamw_in"))
    ((g_pa, d_pa, nm_pa, nv_pa), (g_pb, d_pb, nm_pb, nv_pb), (g_o, d_o, nm_o, nv_o)), r_small = _adamw_rows_call(
        r_p, [w_pa, w_pb, w_o], [m_w_pa, m_w_pb, m_w_o], [v_w_pa, v_w_pb, v_w_o], "adamw_p", small)

    def row(a):
        return a.reshape(1, -1)

    rep = _adamw_lanes_call(
        r_small, [_SM_NORM, _SM_BDEC, _SM_GLAG, _SM_BGATE, _SM_FINAL],
        [row(a) for a in (norm_g, b_dec, gla_norm_g, b_gate, final_g)],
        [row(a) for a in (m_norm_g, m_b_dec, m_gla_norm_g, m_b_gate, m_final_g)],
        [row(a) for a in (v_norm_g, v_b_dec, v_gla_norm_g, v_b_gate, v_final_g)], "adamw_rep")
    ((g_norm, d_norm, nm_norm, nv_norm), (g_bdec, d_bdec, nm_bdec, nv_bdec), (g_glag, d_glag, nm_glag, nv_glag),
     (g_bgate, d_bgate, nm_bgate, nv_bgate), (g_final, d_final, nm_final, nv_final)) = [
        tuple(a.reshape(-1) for a in quad) for quad in rep]

    wdec_parts = r_small[:, 0, _SM_WDEC:].reshape(N_DEV, GLA_RANK, GLA_DK)
    cols = GLA_DK // N_DEV
    wdec_mine = lax.dynamic_slice_in_dim(wdec_parts, me * cols, cols, axis=2)
    g_wdec, d_wdec, nm_wdec, nv_wdec = _adamw_call(wdec_mine, w_dec_up, m_w_dec_up, v_w_dec_up, "adamw_dec")

    loss_total = jnp.sum(r_small[:, 0, _SM_LOSS])

    return (loss_total, grad_x[None],
            g_norm, gw_in, g_wdec, g_bdec, g_glag, g_pa, g_pb, g_bgate, g_o, g_final,
            d_norm, d_in, d_wdec, d_bdec, d_glag, d_pa, d_pb, d_bgate, d_o, d_final,
            nm_norm, nm_in, nm_wdec, nm_bdec, nm_glag, nm_pa, nm_pb, nm_bgate, nm_o, nm_final,
            nv_norm, nv_in, nv_wdec, nv_bdec, nv_glag, nv_pa, nv_pb, nv_bgate, nv_o, nv_final)
```

```python
import math

import jax
import jax.numpy as jnp
from jax import lax
from jax.experimental import pallas as pl
from jax.experimental.pallas import tpu as pltpu

F32 = jnp.float32
BF16 = jnp.bfloat16

N_DEV = 8
D_MODEL = 1024
GLA_HEADS = 4
GLA_HK = 128
GLA_HV = 256
GLA_DK = 512
GLA_RANK = 16
GLA_TAU = 16.0
GLA_CHUNK = 64
SB_HEADS = 8
SB_HD = 128
EPS = 1e-6
N_GROUPS = 9
RANK_COL = 3072
IN_COLS = 9232
SHARD_COLS = IN_COLS // N_DEV

ADAM_LR = 0.001
ADAM_B1 = 0.9
ADAM_B2 = 0.999
ADAM_EPS = 1e-08
ADAM_WD = 0.01
ADAM_STEP = 10

VMEM_LIMIT = 56 * 1024 * 1024
TBLK = 256


def _cparams(sem=None):
    return pltpu.CompilerParams(dimension_semantics=sem, vmem_limit_bytes=VMEM_LIMIT)


def _tiling_2d(rows, cols, band_cols):
    if rows * cols <= 128 * 1024:
        return (rows, cols), (1,), lambda i: (0, 0)
    if rows % 128 == 0:
        return (128, cols), (rows // 128,), lambda i: (i, 0)
    tc = band_cols if cols % band_cols == 0 else cols
    return (rows, tc), (cols // tc,), lambda i: (0, i)


def _dot(a, b):
    return jnp.dot(a, b, preferred_element_type=F32)


def _dot_nt(a, b):
    return lax.dot_general(a, b, (((1,), (1,)), ((), ())), preferred_element_type=F32)


def _dot_tn(a, b):
    return lax.dot_general(a, b, (((0,), (0,)), ((), ())), preferred_element_type=F32)


def _bf(x):
    return x.astype(BF16)


def _split3(x):
    hi = x.astype(BF16)
    r = x - hi.astype(F32)
    mid = r.astype(BF16)
    lo = (r - mid.astype(F32)).astype(BF16)
    return hi, mid, lo


def _tri_left(tri, x):
    hi, mid, lo = _split3(x)
    return _dot(tri, hi) + _dot(tri, mid) + _dot(tri, lo)


def _split2(x):
    hi = lax.bitcast_convert_type(lax.bitcast_convert_type(x, jnp.uint32) & jnp.uint32(0xFFFF0000), F32)
    return hi.astype(BF16), (x - hi).astype(BF16)


def _tri2_left(tri, x):
    hi, lo = _split2(x)
    return _dot(tri, hi) + _dot(tri, lo)


def _tri2_right(x, tri):
    hi, lo = _split2(x)
    return _dot(hi, tri) + _dot(lo, tri)


def _iota2(n, m, dim):
    return lax.broadcasted_iota(jnp.int32, (n, m), dim)


def _sigmoid(x):
    return 1.0 / (1.0 + jnp.exp(-x))


def _softplus_neg_abs(z):
    return jnp.log(1.0 + jnp.exp(-jnp.abs(z)))


_ANY = pl.BlockSpec(memory_space=pl.ANY)


def _mesh_pos():
    return lax.axis_index("x"), lax.axis_index("y"), lax.axis_index("c")


def _other_chips(x, y):
    return [(1 - x, y), (x, 1 - y), (1 - x, 1 - y)]


def _rcopy(src, dst, send_sem, recv_sem, to):
    return pltpu.make_async_remote_copy(src_ref=src, dst_ref=dst, send_sem=send_sem, recv_sem=recv_sem,
                                        device_id=to, device_id_type=pl.DeviceIdType.MESH)


def _push_copies(src_ref, dst_ref, send_sems, recv_sems, loc_sem, scatter):
    x, y, c = _mesh_pos()
    me = 4 * x + 2 * y + c
    own = pltpu.make_async_copy(src_ref.at[me] if scatter else src_ref, dst_ref.at[me], loc_sem)
    pairs = []
    for k in range(1, N_DEV):
        px = 1 - x if k & 4 else x
        py = 1 - y if k & 2 else y
        pc = 1 - c if k & 1 else c
        pid = 4 * px + 2 * py + pc
        src = src_ref.at[pid] if scatter else src_ref
        send = _rcopy(src, dst_ref.at[me], send_sems.at[k - 1], recv_sems.at[k - 1], (px, py, pc))
        recv = _rcopy(src, dst_ref.at[pid], send_sems.at[k - 1], recv_sems.at[k - 1], (px, py, pc))
        pairs.append((send, recv))
    return own, pairs


def _push_start(own, pairs):
    own.start()
    for send, _ in pairs:
        send.start()


def _push_wait(own, pairs):
    for _, recv in pairs:
        recv.wait_recv()
    for send, _ in pairs:
        send.wait_send()
    own.wait()


_PUSH_SEMS = [pltpu.SemaphoreType.DMA((N_DEV - 1,)), pltpu.SemaphoreType.DMA((N_DEV - 1,)),
              pltpu.SemaphoreType.DMA]


def _half_rows(rows):
    return (rows // 2) // 16 * 16


_ADD_ROWS = 128


def _chip_reduce_steps(src_ref, dst_ref, relay_ref, to_x, to_y, relayed, load_sems, send_sems, recv_sems, loc_sem):
    _, R, C = src_ref.shape
    n0 = _half_rows(R)
    x, y, c = _mesh_pos()
    (xx, xy), (yx, yy), (dx, dy) = _other_chips(x, y)
    to_diag = src_ref.at[2 * dx + dy]
    relay_x = _rcopy(to_diag.at[pl.ds(0, n0)], relay_ref.at[pl.ds(0, n0)], send_sems.at[0], recv_sems.at[0],
                     (xx, xy, c))
    relay_y = _rcopy(to_diag.at[pl.ds(n0, R - n0)], relay_ref.at[pl.ds(n0, R - n0)], send_sems.at[1],
                     recv_sems.at[1], (yx, yy, c))
    load_x = pltpu.make_async_copy(src_ref.at[2 * xx + xy], to_x, load_sems.at[0])
    load_y = pltpu.make_async_copy(src_ref.at[2 * yx + yy], to_y, load_sems.at[1])
    load_relayed = pltpu.make_async_copy(relay_ref, relayed, load_sems.at[2])
    own = pltpu.make_async_copy(src_ref.at[2 * x + y], dst_ref.at[2], loc_sem)
    push_x = _rcopy(to_x, dst_ref.at[0], send_sems.at[2], recv_sems.at[2], (xx, xy, c))
    push_y = _rcopy(to_y, dst_ref.at[1], send_sems.at[3], recv_sems.at[3], (yx, yy, c))

    def start():
        for cp in (relay_x, relay_y, own, load_x, load_y):
            cp.start()

    def add_rows(acc_ref, lo, hi):
        for r0 in range(lo, hi, _ADD_ROWS):
            rows = slice(r0, min(r0 + _ADD_ROWS, hi))
            acc_ref[rows, :] = (acc_ref[rows, :].astype(F32) + relayed[rows, :].astype(F32)).astype(acc_ref.dtype)

    def forward():
        relay_x.wait_recv()
        relay_y.wait_recv()
        load_relayed.start()
        load_x.wait()
        load_y.wait()
        load_relayed.wait()
        add_rows(to_x, n0, R)
        add_rows(to_y, 0, n0)
        push_x.start()
        push_y.start()

    def finish():
        for cp in (push_x, push_y):
            cp.wait_recv()
        for cp in (relay_x, relay_y, push_x, push_y):
            cp.wait_send()
        own.wait()

    return start, forward, finish


def _chip_reduce_scratch(rows, cols, dtype):
    return [pltpu.VMEM((rows, cols), dtype)] * 3 + [
        pltpu.SemaphoreType.DMA((3,)), pltpu.SemaphoreType.DMA((4,)), pltpu.SemaphoreType.DMA((4,)),
        pltpu.SemaphoreType.DMA]


def _all_gather(arrs, name, row_pieces=None):
    n = len(arrs)
    pieces = [[None] if not row_pieces or not row_pieces[a] else list(row_pieces[a]) for a in range(n)]
    assert all(len(p) in (1, 2) for p in pieces)
    units = [(a, i) for a in range(n) for i in range(len(pieces[a]))]

    def body(*refs):
        ins = refs[:n]
        outs = refs[n:2 * n]
        send_sems, recv_sems, loc_sems = refs[2 * n:]
        x, y, c = _mesh_pos()
        me, sib = (x, y, c), (x, y, 1 - c)
        xn, yn, dg = [(px, py, c) for px, py in _other_chips(x, y)]

        def rows(ref, a, i):
            return ref if pieces[a][i] is None else ref.at[pl.ds(*pieces[a][i])]

        def copy(u, k, block, to, own=False):
            a, i = u
            px, py, pc = block
            dst = rows(outs[a].at[4 * px + 2 * py + pc], a, i)
            return _rcopy(rows(ins[a], a, i) if own else dst, dst, send_sems.at[a, k, i], recv_sems.at[a, k, i], to)

        started = []

        def start(cp):
            cp.start()
            started.append(cp)

        def landed_then_pass_on(u, k, block):
            copy(u, k, block, me).wait_recv()
            start(copy(u, 3 + k, block, sib))

        mine = [pltpu.make_async_copy(ins[a], outs[a].at[4 * x + 2 * y + c], loc_sems.at[a]) for a in range(n)]
        for cp in mine:
            cp.start()
        for u in units:
            start(copy(u, 0, me, sib, own=True))
        for a in range(n):
            if len(pieces[a]) == 2:
                for i, to, k in ((0, xn, 1), (1, yn, 2), (1, xn, 1), (0, yn, 2)):
                    start(copy((a, i), k, me, to, own=True))
            else:
                for to, k in ((xn, 1), (yn, 2), (dg, 3)):
                    start(copy((a, 0), k, me, to, own=True))
        for a in range(n):
            if len(pieces[a]) == 2:
                landed_then_pass_on((a, 0), 1, xn)
                start(copy((a, 0), 3, xn, yn))
                landed_then_pass_on((a, 1), 2, yn)
                start(copy((a, 1), 3, yn, xn))
                landed_then_pass_on((a, 1), 1, xn)
                landed_then_pass_on((a, 0), 2, yn)
                landed_then_pass_on((a, 0), 3, dg)
                landed_then_pass_on((a, 1), 3, dg)
            else:
                for block, k in ((xn, 1), (yn, 2), (dg, 3)):
                    landed_then_pass_on((a, 0), k, block)
        for u in units:
            copy(u, 0, sib, me).wait_recv()
            for k, (px, py, _) in ((4, xn), (5, yn), (6, dg)):
                copy(u, k, (px, py, 1 - c), me).wait_recv()
        for cp in started:
            cp.wait_send()
        for cp in mine:
            cp.wait()

    n_pc = max(len(p) for p in pieces)

    return pl.pallas_call(
        body, name=name,
        out_shape=tuple(jax.ShapeDtypeStruct((N_DEV,) + a.shape, a.dtype) for a in arrs),
        in_specs=[_ANY] * n,
        out_specs=tuple([_ANY] * n),
        scratch_shapes=[pltpu.SemaphoreType.DMA((n, 7, n_pc)), pltpu.SemaphoreType.DMA((n, 7, n_pc)),
                        pltpu.SemaphoreType.DMA((n,))],
    )(*arrs)


def _pair_exchange(arrs, name):
    n = len(arrs)

    def body(*refs):
        ins = refs[:n]
        outs = refs[n:2 * n]
        send_sems, recv_sems = refs[2 * n:]
        x, y, c = _mesh_pos()
        copies = []
        for a in range(n):
            for q in range(4):
                cp = _rcopy(ins[a].at[2 * q + (1 - c)], outs[a].at[q], send_sems.at[a, q], recv_sems.at[a, q],
                            (x, y, 1 - c))
                cp.start()
                copies.append(cp)
        for cp in copies:
            cp.wait_recv()
        for cp in copies:
            cp.wait_send()

    return pl.pallas_call(
        body, name=name,
        out_shape=tuple(jax.ShapeDtypeStruct((4,) + a.shape[1:], a.dtype) for a in arrs),
        in_specs=[_ANY] * n,
        out_specs=tuple([_ANY] * n),
        scratch_shapes=[pltpu.SemaphoreType.DMA((n, 4)), pltpu.SemaphoreType.DMA((n, 4))],
    )(*arrs)


def _pair_add_call(parts, recv, c_idx, name):
    _, R, C = parts.shape
    (tr, tc), (steps,), idx = _tiling_2d(R, C, 1024)

    def body(c_ref, p_ref, r_ref, o_ref):
        o_ref[...] = (p_ref[...].astype(F32) + r_ref[...].astype(F32)).astype(o_ref.dtype)

    return pl.pallas_call(
        body, name=name,
        grid_spec=pltpu.PrefetchScalarGridSpec(
            num_scalar_prefetch=1,
            grid=(4, steps),
            in_specs=[pl.BlockSpec((None, tr, tc), lambda q, i, c_ref: (2 * q + c_ref[0],) + idx(i)),
                      pl.BlockSpec((None, tr, tc), lambda q, i, c_ref: (q,) + idx(i))],
            out_specs=pl.BlockSpec((None, tr, tc), lambda q, i, c_ref: (q,) + idx(i))),
        out_shape=jax.ShapeDtypeStruct((4, R, C), parts.dtype),
        compiler_params=_cparams(("arbitrary", "arbitrary")),
    )(c_idx, parts, recv)


def _flatten_blocks_call(blocks):
    n, R, C = blocks.shape
    tc = C // 2

    def body(in_ref, out_ref):
        for p in range(n):
            out_ref[p * R:(p + 1) * R, :] = in_ref[p]

    return pl.pallas_call(
        body, name="flatten_w",
        grid=(C // tc,),
        in_specs=[pl.BlockSpec((n, R, tc), lambda i: (0, 0, i))],
        out_specs=pl.BlockSpec((n * R, tc), lambda i: (0, i)),
        out_shape=jax.ShapeDtypeStruct((n * R, C), blocks.dtype),
        compiler_params=_cparams(("arbitrary",)),
    )(blocks)


def _parts_by_device_call(dmain, drank):
    D = dmain.shape[1]
    tc = D // 2

    def body(dm_ref, dr_ref, out_ref):
        for p in range(N_DEV):
            lo, hi = p * SHARD_COLS, (p + 1) * SHARD_COLS
            at = 0
            for src, a, b in ((dm_ref, lo, min(hi, RANK_COL)),
                              (dr_ref, max(lo, RANK_COL) - RANK_COL, min(hi, RANK_COL + GLA_RANK) - RANK_COL),
                              (dm_ref, max(lo, RANK_COL + GLA_RANK) - GLA_RANK, hi - GLA_RANK)):
                if b > a:
                    out_ref[p, at:at + (b - a), :] = src[a:b, :]
                    at += b - a

    return pl.pallas_call(
        body, name="parts_by_device",
        grid=(D // tc,),
        in_specs=[pl.BlockSpec((dmain.shape[0], tc), lambda i: (0, i)),
                  pl.BlockSpec((GLA_RANK, tc), lambda i: (0, i))],
        out_specs=pl.BlockSpec((N_DEV, SHARD_COLS, tc), lambda i: (0, 0, i)),
        out_shape=jax.ShapeDtypeStruct((N_DEV, SHARD_COLS, D), dmain.dtype),
        compiler_params=_cparams(("arbitrary",)),
    )(dmain, drank)


def _group_row(g):
    return GLA_RANK * (g * (1024 // GLA_RANK) + (g >= RANK_COL // 1024))


def _proj_call(x, norm_g, wt, wr, wp_part):
    T, D = x.shape
    tm = min(1024, T)
    assert tm % TBLK == 0
    n_i = T // tm

    def f_slot(j):
        return ((j >= 2).astype(jnp.int32) + (j >= 6).astype(jnp.int32)
                + (j >= 7).astype(jnp.int32) + (j >= 8).astype(jnp.int32))

    def b_slot(j):
        return (j >= 3).astype(jnp.int32) + (j >= 4).astype(jnp.int32) + (j >= 5).astype(jnp.int32)

    def body(x_ref, g_ref, w_ref, wr_ref, wp_ref, pf_ref, pb_ref, rank_ref, ht_ref, wpall_ref,
             h_scr, send_sems, recv_sems, loc_sem):
        i = pl.program_id(0)
        j = pl.program_id(1)
        own, pairs = _push_copies(wp_ref, wpall_ref, send_sems, recv_sems, loc_sem, scatter=False)

        @pl.when((i == 0) & (j == 0))
        def _():
            _push_start(own, pairs)

        @pl.when(j == 0)
        def _():
            xv = x_ref[...]
            r = lax.rsqrt(jnp.mean(xv * xv, axis=-1, keepdims=True) + EPS)
            h = (xv * r) * g_ref[...]
            hb = _bf(h)
            h_scr[...] = hb
            for b in range(tm // TBLK):
                ht_ref[b] = _bf(h[b * TBLK:(b + 1) * TBLK].T)
            rank_ref[...] = _dot_nt(hb, wr_ref[...])

        is_b = (j == 1) | ((j >= 3) & (j <= 5))

        @pl.when(is_b)
        def _():
            pb_ref[...] = _bf(_dot_nt(h_scr[...], w_ref[...]))

        @pl.when(jnp.logical_not(is_b))
        def _():
            pf_ref[...] = _dot_nt(h_scr[...], w_ref[...])

        @pl.when((i == n_i - 1) & (j == N_GROUPS - 1))
        def _():
            _push_wait(own, pairs)

    return pl.pallas_call(
        body, name="proj",
        grid=(n_i, N_GROUPS),
        in_specs=[pl.BlockSpec((tm, D), lambda i, j: (i, 0)),
                  pl.BlockSpec((1, D), lambda i, j: (0, 0)),
                  pl.BlockSpec((pl.Element(1024), pl.Element(D)), lambda i, j: (_group_row(j), 0)),
                  pl.BlockSpec((128, D), lambda i, j: (0, 0)),
                  _ANY],
        out_specs=(pl.BlockSpec((None, tm, 1024), lambda i, j: (f_slot(j), i, 0)),
                   pl.BlockSpec((None, tm, 1024), lambda i, j: (b_slot(j), i, 0)),
                   pl.BlockSpec((tm, 128), lambda i, j: (i, 0)),
                   pl.BlockSpec((tm // TBLK, D, TBLK), lambda i, j: (i, 0, 0)),
                   _ANY),
        out_shape=(jax.ShapeDtypeStruct((5, T, 1024), F32),
                   jax.ShapeDtypeStruct((4, T, 1024), BF16),
                   jax.ShapeDtypeStruct((T, 128), F32),
                   jax.ShapeDtypeStruct((T // TBLK, D, TBLK), BF16),
                   jax.ShapeDtypeStruct((N_DEV,) + wp_part.shape, wp_part.dtype)),
        scratch_shapes=[pltpu.VMEM((tm, D), BF16)] + _PUSH_SEMS,
        compiler_params=_cparams(("arbitrary", "arbitrary")),
    )(x, norm_g, wt, wr, wp_part)


GLA_STEP_CHUNKS = 4


def _gla_same_chunk(rows):
    return (_iota2(rows, rows, 0) & -GLA_CHUNK) == (_iota2(rows, rows, 1) & -GLA_CHUNK)


def _gla_chunk_terms(la, q, k, n_c):
    C = GLA_CHUNK
    rows = n_c * C
    low = _gla_same_chunk(rows) & (_iota2(rows, rows, 0) >= _iota2(rows, rows, 1))
    b = _tri_left(_bf(low.astype(F32)), la)
    bl = [b[(c + 1) * C - 1:(c + 1) * C, :] for c in range(n_c)]
    bl_rows = jnp.concatenate([jnp.broadcast_to(bl[c], (C, b.shape[1])) for c in range(n_c)], axis=0)
    eb = jnp.exp(b)
    enb = jnp.exp(-b)
    ebl_b = jnp.exp(bl_rows - b)
    scale = GLA_HK ** -0.5
    qe = q * eb * scale
    ke = k * enb
    kd = k * ebl_b
    return bl, eb, enb, ebl_b, qe, ke, kd


def _gla_fwd_call(projf, projb, rank, wdec, bdec):
    T = projf.shape[1]
    C = GLA_CHUNK
    n_chunks = T // C
    n_c = GLA_STEP_CHUNKS
    R = n_c * C
    assert n_chunks % n_c == 0

    def body(qk_ref, v_ref, rank_ref, wd_ref, bd_ref, o_ref, st_ref, la_ref, st_scr):
        @pl.when(pl.program_id(0) == 0)
        def _():
            st_scr[...] = jnp.zeros_like(st_scr)

        dec = _dot(_bf(rank_ref[...]), _bf(wd_ref[...])) + bd_ref[...]
        la = (jnp.minimum(dec, 0.0) - _softplus_neg_abs(dec)) / GLA_TAU
        la_ref[...] = la
        mask = _gla_same_chunk(R) & (_iota2(R, R, 0) >= _iota2(R, R, 1))
        bl, _, _, _, qe, ke, kd = _gla_chunk_terms(la, qk_ref[:, :GLA_DK], qk_ref[:, GLA_DK:], n_c)
        qeb, keb, kdb = _bf(qe), _bf(ke), _bf(kd)
        ebl = [jnp.exp(bl[c]) for c in range(n_c)]
        heads = range(GLA_HEADS)
        ks = [slice(hh * GLA_HK, (hh + 1) * GLA_HK) for hh in heads]
        vs = [slice(hh * GLA_HV, (hh + 1) * GLA_HV) for hh in heads]
        rs = [slice(c * C, (c + 1) * C) for c in range(n_c)]
        p = [_bf(jnp.where(mask, _dot_nt(qeb[:, ks[hh]], keb[:, ks[hh]]), 0.0)) for hh in heads]
        upd = [[_dot_tn(v_ref[rs[c], vs[hh]], kdb[rs[c], ks[hh]]) for hh in heads] for c in range(n_c)]
        intra = [_dot(p[hh], v_ref[:, vs[hh]]) for hh in heads]
        st = [st_scr[hh] for hh in heads]
        for c in range(n_c):
            inter = [_dot_nt(qeb[rs[c], ks[hh]], _bf(st[hh])) for hh in heads]
            for hh in heads:
                st_ref[c, hh] = st[hh]
                o_ref[rs[c], vs[hh]] = intra[hh][rs[c]] + inter[hh]
            st = [st[hh] * ebl[c][:, ks[hh]] + upd[c][hh] for hh in heads]
        for hh in heads:
            st_scr[hh] = st[hh]

    return pl.pallas_call(
        body, name="gla_fwd",
        grid=(n_chunks // n_c,),
        in_specs=[pl.BlockSpec((None, R, 1024), lambda n: (0, n, 0)),
                  pl.BlockSpec((None, R, 1024), lambda n: (0, n, 0)),
                  pl.BlockSpec((R, 128), lambda n: (n, 0)),
                  pl.BlockSpec((128, GLA_DK), lambda n: (0, 0)),
                  pl.BlockSpec((1, GLA_DK), lambda n: (0, 0))],
        out_specs=(pl.BlockSpec((R, 1024), lambda n: (n, 0)),
                   pl.BlockSpec((n_c, GLA_HEADS, GLA_HV, GLA_HK), lambda n: (n, 0, 0, 0)),
                   pl.BlockSpec((R, GLA_DK), lambda n: (n, 0))),
        out_shape=(jax.ShapeDtypeStruct((T, 1024), F32),
                   jax.ShapeDtypeStruct((n_chunks, GLA_HEADS, GLA_HV, GLA_HK), F32),
                   jax.ShapeDtypeStruct((T, GLA_DK), F32)),
        scratch_shapes=[pltpu.VMEM((GLA_HEADS, GLA_HV, GLA_HK), F32)],
        compiler_params=_cparams(("arbitrary",)),
    )(projf, projb, rank, wdec, bdec)


def _gla_bwd_call(projf, projb, la, do_gla, st_all, rank, wdec):
    T = projf.shape[1]
    C = GLA_CHUNK
    n_chunks = T // C
    n_c = GLA_STEP_CHUNKS
    R = n_c * C
    assert n_chunks % n_c == 0
    last = n_chunks // n_c - 1

    def body(qk_ref, v_ref, la_ref, do_ref, st_ref, rank_ref, wd_ref,
             dqk_ref, dv_ref, drank_ref, dwd_ref, dbd_ref, dst_scr):
        @pl.when(pl.program_id(0) == 0)
        def _():
            dst_scr[...] = jnp.zeros_like(dst_scr)
            dwd_ref[...] = jnp.zeros_like(dwd_ref)
            dbd_ref[...] = jnp.zeros_like(dbd_ref)

        same = _gla_same_chunk(R)
        mask = same & (_iota2(R, R, 0) >= _iota2(R, R, 1))
        upp = _bf((same & (_iota2(R, R, 0) <= _iota2(R, R, 1))).astype(F32))
        scale = GLA_HK ** -0.5
        la = la_ref[...]
        bl, eb, enb, ebl_b, qe, ke, kd = _gla_chunk_terms(la, qk_ref[:, :GLA_DK], qk_ref[:, GLA_DK:], n_c)
        qeb, keb, kdb = _bf(qe), _bf(ke), _bf(kd)
        ebl = [jnp.exp(bl[c]) for c in range(n_c)]
        heads = range(GLA_HEADS)
        ks = [slice(hh * GLA_HK, (hh + 1) * GLA_HK) for hh in heads]
        vs = [slice(hh * GLA_HV, (hh + 1) * GLA_HV) for hh in heads]
        rs = [slice(c * C, (c + 1) * C) for c in range(n_c)]
        v = [v_ref[:, vs[hh]] for hh in heads]
        do = [_bf(do_ref[:, vs[hh]]) for hh in heads]
        p = [_bf(jnp.where(mask, _dot_nt(qeb[:, ks[hh]], keb[:, ks[hh]]), 0.0)) for hh in heads]
        dp = [_bf(jnp.where(mask, _dot_nt(do[hh], v[hh]), 0.0)) for hh in heads]
        dst_intra = [[_dot_tn(do[hh][rs[c]], qeb[rs[c], ks[hh]]) for hh in heads] for c in range(n_c)]
        dqe_inter = [[_dot(do[hh][rs[c]], _bf(st_ref[c, hh])) for hh in heads] for c in range(n_c)]
        dv_intra = [_dot_tn(p[hh], do[hh]) for hh in heads]
        dqe_intra = [_dot(dp[hh], keb[:, ks[hh]]) for hh in heads]
        dke = jnp.concatenate([_dot_tn(dp[hh], qeb[:, ks[hh]]) for hh in heads], axis=1)
        dstn = [dst_scr[hh] for hh in heads]
        dkd_c, dv_inter, debl = [None] * n_c, [None] * n_c, [None] * n_c
        for c in reversed(range(n_c)):
            dstnb = [_bf(dstn[hh]) for hh in heads]
            dkd_c[c] = jnp.concatenate([_dot(v[hh][rs[c]], dstnb[hh]) for hh in heads], axis=1)
            dv_inter[c] = [_dot_nt(kdb[rs[c], ks[hh]], dstnb[hh]) for hh in heads]
            debl[c] = jnp.concatenate(
                [jnp.sum(dstn[hh] * st_ref[c, hh], axis=0, keepdims=True) for hh in heads], axis=1)
            dstn = [dst_intra[c][hh] + dstn[hh] * ebl[c][:, ks[hh]] for hh in heads]
        for hh in heads:
            dst_scr[hh] = dstn[hh]
            dv_ref[:, vs[hh]] = _bf(dv_intra[hh] + jnp.concatenate([dv_inter[c][hh] for c in range(n_c)], axis=0))
        dqe = jnp.concatenate(
            [dqe_intra[hh] + jnp.concatenate([dqe_inter[c][hh] for c in range(n_c)], axis=0) for hh in heads], axis=1)
        dkd = jnp.concatenate(dkd_c, axis=0)
        dkd_kd = dkd * kd
        db = dqe * qe - dke * ke - dkd_kd
        dbl = jnp.concatenate(
            [jnp.broadcast_to(jnp.sum(dkd_kd[rs[c]], axis=0, keepdims=True) + ebl[c] * debl[c], (C, GLA_DK))
             for c in range(n_c)], axis=0)
        dla = _tri_left(upp, db) + dbl
        dqk_ref[:, :GLA_DK] = _bf(dqe * eb * scale)
        dqk_ref[:, GLA_DK:] = _bf(dke * enb + dkd * ebl_b)
        ddec = dla * (1.0 / GLA_TAU) * (1.0 - jnp.exp(GLA_TAU * la))
        ddecb = _bf(ddec)
        drank_ref[...] = _bf(_dot_nt(ddecb, _bf(wd_ref[...])))
        dwd_ref[...] += _dot_tn(_bf(rank_ref[...]), ddecb)
        dbd_ref[...] += jnp.sum(ddec, axis=0, keepdims=True)

    return pl.pallas_call(
        body, name="gla_bwd",
        grid=(n_chunks // n_c,),
        in_specs=[pl.BlockSpec((None, R, 1024), lambda n: (0, last - n, 0)),
                  pl.BlockSpec((None, R, 1024), lambda n: (0, last - n, 0)),
                  pl.BlockSpec((R, GLA_DK), lambda n: (last - n, 0)),
                  pl.BlockSpec((R, 1024), lambda n: (last - n, 0)),
                  pl.BlockSpec((n_c, GLA_HEADS, GLA_HV, GLA_HK), lambda n: (last - n, 0, 0, 0)),
                  pl.BlockSpec((R, 128), lambda n: (last - n, 0)),
                  pl.BlockSpec((128, GLA_DK), lambda n: (0, 0))],
        out_specs=(pl.BlockSpec((R, 1024), lambda n: (last - n, 0)),
                   pl.BlockSpec((R, 1024), lambda n: (last - n, 0)),
                   pl.BlockSpec((R, 128), lambda n: (last - n, 0)),
                   pl.BlockSpec((128, GLA_DK), lambda n: (0, 0)),
                   pl.BlockSpec((1, GLA_DK), lambda n: (0, 0))),
        out_shape=(jax.ShapeDtypeStruct((T, 1024), BF16),
                   jax.ShapeDtypeStruct((T, 1024), BF16),
                   jax.ShapeDtypeStruct((T, 128), BF16),
                   jax.ShapeDtypeStruct((128, GLA_DK), F32),
                   jax.ShapeDtypeStruct((1, GLA_DK), F32)),
        scratch_shapes=[pltpu.VMEM((GLA_HEADS, GLA_HV, GLA_HK), F32)],
        compiler_params=_cparams(("arbitrary",)),
    )(projf, projb, la, do_gla, st_all, rank, wdec)


def _sb_logs(z):
    lsz = jnp.minimum(z, 0.0) - _softplus_neg_abs(z)
    return lsz, lsz - z


SB_HG_FWD = 8
SB_HG_BWD = 4
SB_QUERIES = 256
SB_KEYS = 256
SB_DEAD = -105.0


def _sb_fwd_call(projb, wp_shard):
    T = projb.shape[1]
    B = min(SB_QUERIES, T)
    HG = SB_HG_FWD
    W = HG * SB_HD
    scale = 1.0 / math.sqrt(SB_HD)
    KB = min(SB_KEYS, T)
    n_h, n_i = SB_HEADS // HG, T // B

    def body(q_ref, k_ref, v_ref, wp_ref, o_ref, wpall_ref, cb_scr, send_sems, recv_sems, loc_sem):
        i = pl.program_id(1)
        own, pairs = _push_copies(wp_ref, wpall_ref, send_sems, recv_sems, loc_sem, scatter=False)

        @pl.when((pl.program_id(0) == 0) & (i == 0))
        def _():
            _push_start(own, pairs)

        rows = HG * B
        after = (_iota2(KB, KB, 0) > _iota2(KB, KB, 1)).astype(F32)
        tri = _bf(jnp.concatenate([after, jnp.ones((KB, KB), F32)], axis=1))
        o_ref[...] = jnp.zeros_like(o_ref)
        cb_scr[...] = jnp.zeros_like(cb_scr)

        def block(jp, masked):
            off = pl.multiple_of(jp * KB, KB)
            z = jnp.concatenate(
                [_dot_nt(q_ref[:, hh * SB_HD:(hh + 1) * SB_HD], k_ref[pl.ds(off, KB), hh * SB_HD:(hh + 1) * SB_HD])
                 for hh in range(HG)], axis=0) * scale
            lsz, l1m = _sb_logs(z)
            if masked:
                strict = (jp * KB + _iota2(rows, KB, 1)) < (i * B + (_iota2(rows, KB, 0) & (B - 1)))
                l1m = jnp.where(strict, l1m, 0.0)
            r = _tri2_right(l1m, tri)
            cb = cb_scr[...]
            a = jnp.exp(lsz + cb + r[:, :KB])
            if masked:
                a = jnp.where(strict, a, 0.0)
            cb_scr[...] = cb + r[:, KB:]
            ab = _bf(a)
            for hh in range(HG):
                cs = slice(hh * SB_HD, (hh + 1) * SB_HD)
                o_ref[:, cs] += _dot(ab[hh * B:(hh + 1) * B, :], v_ref[pl.ds(off, KB), cs])

        jp0 = (i * B) // KB
        block(jp0, True)

        def live(state):
            jj, dead = state
            return (jj <= jp0) & jnp.logical_not(dead)

        def step(state):
            jj, _ = state
            block(jp0 - jj, False)
            return jj + 1, jnp.max(cb_scr[:, :SB_HD]) < SB_DEAD

        lax.while_loop(live, step, (jnp.int32(1), jnp.max(cb_scr[:, :SB_HD]) < SB_DEAD))

        @pl.when((pl.program_id(0) == n_h - 1) & (i == n_i - 1))
        def _():
            _push_wait(own, pairs)

    return pl.pallas_call(
        body, name="sb_fwd",
        grid=(n_h, n_i),
        in_specs=[pl.BlockSpec((None, B, W), lambda h, i: (1, i, h)),
                  pl.BlockSpec((None, T, W), lambda h, i: (2, 0, h)),
                  pl.BlockSpec((None, T, W), lambda h, i: (3, 0, h)),
                  _ANY],
        out_specs=(pl.BlockSpec((B, W), lambda h, i: (i, h)), _ANY),
        out_shape=(jax.ShapeDtypeStruct((T, 1024), F32),
                   jax.ShapeDtypeStruct((N_DEV,) + wp_shard.shape, wp_shard.dtype)),
        scratch_shapes=[pltpu.VMEM((HG * B, KB), F32)] + _PUSH_SEMS,
        compiler_params=_cparams(("arbitrary", "arbitrary")),
    )(projb, projb, projb, wp_shard)


def _sb_bwd_call(projb, do_sb, g_p):
    T = projb.shape[1]
    B = min(SB_QUERIES, T)
    nb = T // B
    HG = SB_HG_BWD
    W = HG * SB_HD
    WQ = HG * B
    KB = min(SB_KEYS, T)
    nkb = T // KB
    n_h = SB_HEADS // HG
    scale = 1.0 / math.sqrt(SB_HD)

    def body(q_ref, k_ref, v_ref, do_ref, gp_ref, dq_ref, dk_ref, dv_ref, rp_ref,
             dk_scr, dv_scr, kt_scr, beta_scr, g_scr, dqt_scr, send_sems, recv_sems, loc_sem):
        i = pl.program_id(1)
        own, pairs = _push_copies(gp_ref, rp_ref, send_sems, recv_sems, loc_sem, scatter=True)

        @pl.when((pl.program_id(0) == 0) & (i == 0))
        def _():
            _push_start(own, pairs)

        @pl.when(i == 0)
        def _():
            dk_scr[...] = jnp.zeros_like(dk_scr)
            dv_scr[...] = jnp.zeros_like(dv_scr)
            for hh in range(HG):
                for jb in range(nkb):
                    kt_scr[hh, jb] = _bf(
                        k_ref[jb * KB:(jb + 1) * KB, hh * SB_HD:(hh + 1) * SB_HD].astype(F32).T)

        dqt_scr[...] = jnp.zeros_like(dqt_scr)
        later = _bf((_iota2(KB, KB, 1) > _iota2(KB, KB, 0)).astype(F32))
        earlier = _bf((_iota2(KB, KB, 1) < _iota2(KB, KB, 0)).astype(F32))
        dob = _bf(do_ref[...])
        jp0 = (i * B) // KB

        def strict_mask():
            return (jp0 * KB + _iota2(KB, WQ, 0)) < (i * B + (_iota2(KB, WQ, 1) & (B - 1)))

        def heads(fn):
            return [fn(slice(hh * SB_HD, (hh + 1) * SB_HD)) for hh in range(HG)]

        def pass1(jp, cb, masked):
            off = pl.multiple_of(jp * KB, KB)
            z = jnp.concatenate(heads(lambda cs: _dot_nt(k_ref[pl.ds(off, KB), cs], q_ref[:, cs])), axis=1) * scale
            da = jnp.concatenate(heads(lambda cs: _dot_nt(v_ref[pl.ds(off, KB), cs], dob[:, cs])), axis=1)
            lsz, l1m = _sb_logs(z)
            if masked:
                strict = strict_mask()
                l1m = jnp.where(strict, l1m, 0.0)
            a = jnp.exp(lsz + cb + _tri2_left(later, l1m))
            if masked:
                a = jnp.where(strict, a, 0.0)
            g_scr[jp] = a * da
            beta_scr[jp] = jnp.exp(lsz)
            ab = _bf(a)
            for hh in range(HG):
                cs = slice(hh * SB_HD, (hh + 1) * SB_HD)
                dv_scr[pl.ds(off, KB), cs] += _dot(ab[:, hh * B:(hh + 1) * B], dob[:, cs])
            return cb + jnp.sum(l1m, axis=0, keepdims=True)

        zero = jnp.zeros((1, WQ), F32)
        cb = pass1(jp0, zero, True)

        def live(state):
            jj, _, dead = state
            return (jj <= jp0) & jnp.logical_not(dead)

        def step(state):
            jj, cr, _ = state
            cr = pass1(jp0 - jj, cr, False)
            return jj + 1, cr, jnp.max(cr) < SB_DEAD

        n_done, _, _ = lax.while_loop(live, step, (jnp.int32(1), cb, jnp.max(cb) < SB_DEAD))
        jp_first = jp0 - (n_done - 1)

        def pass2(jp, cg, masked):
            off = pl.multiple_of(jp * KB, KB)
            g = g_scr[jp]
            beta = beta_scr[jp]
            dz = g * (1.0 - beta) - beta * (cg + _tri2_left(earlier, g))
            if masked:
                dz = jnp.where(strict_mask(), dz, 0.0)
            dzb = _bf(dz * scale)
            for hh in range(HG):
                cs = slice(hh * SB_HD, (hh + 1) * SB_HD)
                dk_scr[pl.ds(off, KB), cs] += _dot(dzb[:, hh * B:(hh + 1) * B], q_ref[:, cs])
                dqt_scr[hh] += _dot(kt_scr[hh, jp], dzb[:, hh * B:(hh + 1) * B])
            return cg + jnp.sum(g, axis=0, keepdims=True)

        cg = lax.fori_loop(jp_first, jp0, lambda jp, cr: pass2(jp, cr, False), zero)
        pass2(jp0, cg, True)
        for hh in range(HG):
            dq_ref[:, hh * SB_HD:(hh + 1) * SB_HD] = _bf(dqt_scr[hh].T)

        @pl.when(i == nb - 1)
        def _():
            dk_ref[...] = _bf(dk_scr[...])
            dv_ref[...] = _bf(dv_scr[...])

        @pl.when((pl.program_id(0) == n_h - 1) & (i == nb - 1))
        def _():
            _push_wait(own, pairs)

    return pl.pallas_call(
        body, name="sb_bwd",
        grid=(n_h, nb),
        in_specs=[pl.BlockSpec((None, B, W), lambda h, i: (1, i, h)),
                  pl.BlockSpec((None, T, W), lambda h, i: (2, 0, h)),
                  pl.BlockSpec((None, T, W), lambda h, i: (3, 0, h)),
                  pl.BlockSpec((B, W), lambda h, i: (i, h)),
                  _ANY],
        out_specs=(pl.BlockSpec((B, W), lambda h, i: (i, h)),
                   pl.BlockSpec((T, W), lambda h, i: (0, h)),
                   pl.BlockSpec((T, W), lambda h, i: (0, h)),
                   _ANY),
        out_shape=(jax.ShapeDtypeStruct((T, 1024), BF16),
                   jax.ShapeDtypeStruct((T, 1024), BF16),
                   jax.ShapeDtypeStruct((T, 1024), BF16),
                   jax.ShapeDtypeStruct(g_p.shape, g_p.dtype)),
        scratch_shapes=[pltpu.VMEM((T, W), F32), pltpu.VMEM((T, W), F32),
                        pltpu.VMEM((HG, nkb, SB_HD, KB), BF16),
                        pltpu.VMEM((nkb, KB, WQ), F32), pltpu.VMEM((nkb, KB, WQ), F32),
                        pltpu.VMEM((HG, SB_HD, B), F32)] + _PUSH_SEMS,
        compiler_params=_cparams(("arbitrary", "arbitrary")),
    )(projb, projb, projb, do_sb, g_p)


def _mid_call(o_gla, o_sb, projf, x, target, wpa, wpb, wo, gla_g, b_gate, final_g):
    T, D = x.shape
    tm = min(TBLK, T)

    def body(og_ref, ggate_ref, osb_ref, sgate_ref, ma_ref, mb_ref, x_ref, tgt_ref,
             wpa_ref, wpb_ref, wo_ref, glag_ref, bg_ref, fg_ref,
             dx2_ref, dogla_ref, dosb_ref, dggate_ref, dsgate_ref, dm_ref,
             mt_ref, ogt_ref, obt_ref, dx2b_ref, dya_ref, dyb_ref,
             dfg_ref, dbg_ref, dglag_ref, loss_ref):
        @pl.when(pl.program_id(0) == 0)
        def _():
            dfg_ref[...] = jnp.zeros_like(dfg_ref)
            dbg_ref[...] = jnp.zeros_like(dbg_ref)
            dglag_ref[...] = jnp.zeros_like(dglag_ref)
            loss_ref[...] = jnp.zeros_like(loss_ref)

        glag = glag_ref[...]
        ggate = ggate_ref[...]
        sg = _sigmoid(ggate)
        silu_g = ggate * sg
        ohat, rinv, nrm = [], [], []
        for hh in range(GLA_HEADS):
            oh = og_ref[:, hh * GLA_HV:(hh + 1) * GLA_HV]
            r = lax.rsqrt(jnp.mean(oh * oh, axis=-1, keepdims=True) + EPS)
            ohat.append(oh * r)
            rinv.append(r)
            nrm.append(ohat[-1] * glag)
        n_all = jnp.concatenate(nrm, axis=1)
        og = n_all * silu_g
        ogb = _bf(og)
        ya = _dot(ogb, wpa_ref[...])
        sgate = sgate_ref[...]
        ss = _sigmoid(sgate)
        silu_s = sgate * ss
        osb = osb_ref[...]
        ob = osb * silu_s
        obb = _bf(ob)
        yb = _dot(obb, wpb_ref[...])
        ga = _sigmoid(ma_ref[...] + bg_ref[:, :D])
        gb = _sigmoid(mb_ref[...] + bg_ref[:, D:])
        merged = ga * ya + gb * yb
        mgb = _bf(merged)
        x2 = x_ref[...] + _dot(mgb, wo_ref[...])
        r2 = lax.rsqrt(jnp.mean(x2 * x2, axis=-1, keepdims=True) + EPS)
        xh2 = x2 * r2
        fg = fg_ref[...]
        err = xh2 * fg - tgt_ref[...]
        loss_ref[...] += jnp.broadcast_to(
            0.5 * jnp.sum(jnp.mean(err * err, axis=-1, keepdims=True), axis=0, keepdims=True), (1, 128))
        dy = err * (1.0 / D)
        dfg_ref[...] += jnp.sum(dy * xh2, axis=0, keepdims=True)
        dxh = dy * fg
        dx2 = r2 * (dxh - xh2 * jnp.mean(dxh * xh2, axis=-1, keepdims=True))
        dx2_ref[...] = dx2
        dx2b = _bf(dx2)
        dx2b_ref[...] = dx2b
        dmerged = _dot_nt(dx2b, wo_ref[...])
        dya = dmerged * ga
        dyb = dmerged * gb
        dma = dmerged * ya * ga * (1.0 - ga)
        dmb = dmerged * yb * gb * (1.0 - gb)
        dm_ref[:, :D] = _bf(dma)
        dm_ref[:, D:] = _bf(dmb)
        dbg_ref[:, :D] += jnp.sum(dma, axis=0, keepdims=True)
        dbg_ref[:, D:] += jnp.sum(dmb, axis=0, keepdims=True)
        dyab = _bf(dya)
        dybb = _bf(dyb)
        dya_ref[...] = dyab
        dyb_ref[...] = dybb
        dog = _dot_nt(dyab, wpa_ref[...])
        dob = _dot_nt(dybb, wpb_ref[...])
        dosb_ref[...] = dob * silu_s
        dsgate_ref[...] = _bf(dob * osb * (ss * (1.0 + sgate * (1.0 - ss))))
        dn = dog * silu_g
        dggate_ref[...] = _bf(dog * n_all * (sg * (1.0 + ggate * (1.0 - sg))))
        dglag = jnp.zeros((1, GLA_HV), F32)
        for hh in range(GLA_HEADS):
            dnh = dn[:, hh * GLA_HV:(hh + 1) * GLA_HV]
            dglag = dglag + jnp.sum(dnh * ohat[hh], axis=0, keepdims=True)
            dohat = dnh * glag
            dogla_ref[:, hh * GLA_HV:(hh + 1) * GLA_HV] = rinv[hh] * (
                dohat - ohat[hh] * jnp.mean(dohat * ohat[hh], axis=-1, keepdims=True))
        dglag_ref[...] += dglag
        mt_ref[...] = _bf(merged.T)
        ogt_ref[...] = _bf(og.T)
        obt_ref[...] = _bf(ob.T)

    row = lambda i: (i, 0)
    const = lambda i: (0, 0)
    tile = pl.BlockSpec((tm, D), row)
    tile_t = pl.BlockSpec((None, D, tm), lambda i: (i, 0, 0))
    wspec = pl.BlockSpec((D, D), const)
    return pl.pallas_call(
        body, name="mid",
        grid=(T // tm,),
        in_specs=[tile,
                  pl.BlockSpec((None, tm, D), lambda i: (1, i, 0)),
                  tile,
                  pl.BlockSpec((None, tm, D), lambda i: (2, i, 0)),
                  pl.BlockSpec((None, tm, D), lambda i: (3, i, 0)),
                  pl.BlockSpec((None, tm, D), lambda i: (4, i, 0)),
                  tile, tile, wspec, wspec, wspec,
                  pl.BlockSpec((1, GLA_HV), const),
                  pl.BlockSpec((1, 2 * D), const),
                  pl.BlockSpec((1, D), const)],
        out_specs=(tile, tile, tile, tile, tile,
                   pl.BlockSpec((tm, 2 * D), row),
                   tile_t, tile_t, tile_t, tile, tile, tile,
                   pl.BlockSpec((1, D), const),
                   pl.BlockSpec((1, 2 * D), const),
                   pl.BlockSpec((1, GLA_HV), const),
                   pl.BlockSpec((1, 128), const)),
        out_shape=(jax.ShapeDtypeStruct((T, D), F32),
                   jax.ShapeDtypeStruct((T, D), F32),
                   jax.ShapeDtypeStruct((T, D), F32),
                   jax.ShapeDtypeStruct((T, D), BF16),
                   jax.ShapeDtypeStruct((T, D), BF16),
                   jax.ShapeDtypeStruct((T, 2 * D), BF16),
                   jax.ShapeDtypeStruct((T // tm, D, tm), BF16),
                   jax.ShapeDtypeStruct((T // tm, D, tm), BF16),
                   jax.ShapeDtypeStruct((T // tm, D, tm), BF16),
                   jax.ShapeDtypeStruct((T, D), BF16),
                   jax.ShapeDtypeStruct((T, D), BF16),
                   jax.ShapeDtypeStruct((T, D), BF16),
                   jax.ShapeDtypeStruct((1, D), F32),
                   jax.ShapeDtypeStruct((1, 2 * D), F32),
                   jax.ShapeDtypeStruct((1, GLA_HV), F32),
                   jax.ShapeDtypeStruct((1, 128), F32)),
        compiler_params=_cparams(("arbitrary",)),
    )(o_gla, projf, o_sb, projf, projf, projf, x, target, wpa, wpb, wo, gla_g, b_gate, final_g)


def _dh_call(pieces, dmlog, drank, wt, wr, x, dx2, norm_g, s_in):
    T, D = x.shape
    tm = min(256, T)
    npc = len(pieces)
    n_main = N_GROUPS * 1024
    n_i = T // tm
    i_forward = 3 * n_i // 8

    def body(*refs):
        pcs = refs[:npc]
        (dm_ref, dr_ref, w_hbm, wr_ref, x_ref, dx2_ref, g_ref, sin_ref,
         gx_ref, dg_ref, rin_ref, relay_ref, w_scr, sems, *exchange_scratch) = refs[npc:]
        start, forward, finish = _chip_reduce_steps(sin_ref, rin_ref, relay_ref, *exchange_scratch)

        @pl.when(pl.program_id(0) == 0)
        def _():
            start()
            lo = pltpu.make_async_copy(w_hbm.at[pl.ds(0, RANK_COL)], w_scr.at[pl.ds(0, RANK_COL)], sems.at[0])
            hi = pltpu.make_async_copy(w_hbm.at[pl.ds(RANK_COL + GLA_RANK, n_main - RANK_COL)],
                                       w_scr.at[pl.ds(RANK_COL, n_main - RANK_COL)], sems.at[1])
            lo.start()
            hi.start()
            dg_ref[...] = jnp.zeros_like(dg_ref)
            lo.wait()
            hi.wait()

        @pl.when(pl.program_id(0) == i_forward)
        def _():
            forward()

        def w_group(g):
            return w_scr[g * 1024:(g + 1) * 1024, :]

        dr = dr_ref[...]
        dh = _dot(dr, wr_ref[...])
        for g in range(npc):
            dh = dh + _dot(pcs[g][...], w_group(g))
        dh = dh + _dot(dm_ref[:, :D], w_group(npc))
        dh = dh + _dot(dm_ref[:, D:], w_group(npc + 1))
        xv = x_ref[...]
        r = lax.rsqrt(jnp.mean(xv * xv, axis=-1, keepdims=True) + EPS)
        xhat = xv * r
        g = g_ref[...]
        dg_ref[...] += jnp.sum(dh * xhat, axis=0, keepdims=True)
        dxhat = dh * g
        gx_ref[...] = r * (dxhat - xhat * jnp.mean(dxhat * xhat, axis=-1, keepdims=True)) + dx2_ref[...]

        @pl.when(pl.program_id(0) == n_i - 1)
        def _():
            finish()

    row = lambda i: (i, 0)
    const = lambda i: (0, 0)
    tile = pl.BlockSpec((tm, D), row)
    part = s_in.shape[1:]
    return pl.pallas_call(
        body, name="dh",
        grid=(n_i,),
        in_specs=[tile] * npc + [
            pl.BlockSpec((tm, 2 * D), row),
            pl.BlockSpec((tm, 128), row),
            _ANY,
            pl.BlockSpec((128, D), const),
            tile, tile,
            pl.BlockSpec((1, D), const),
            _ANY],
        out_specs=(tile, pl.BlockSpec((1, D), const), _ANY, _ANY),
        out_shape=(jax.ShapeDtypeStruct((T, D), F32),
                   jax.ShapeDtypeStruct((1, D), F32),
                   jax.ShapeDtypeStruct((3,) + part, s_in.dtype),
                   jax.ShapeDtypeStruct(part, s_in.dtype)),
        scratch_shapes=[pltpu.VMEM((n_main, D), BF16), pltpu.SemaphoreType.DMA((2,))]
        + _chip_reduce_scratch(*part, s_in.dtype),
        compiler_params=_cparams(("arbitrary",)),
    )(*pieces, dmlog, drank, wt, wr, x, dx2, norm_g, s_in)


def _wgrad_rank_call(ht, drank):
    n_tb, D, tb = ht.shape

    def body(ht_ref, dr_ref, o_ref):
        @pl.when(pl.program_id(0) == 0)
        def _():
            o_ref[...] = jnp.zeros_like(o_ref)

        o_ref[...] += _dot(ht_ref[...], dr_ref[...])

    return pl.pallas_call(
        body, name="wgrad_rank",
        grid=(n_tb,),
        in_specs=[pl.BlockSpec((None, D, tb), lambda i: (i, 0, 0)),
                  pl.BlockSpec((tb, 128), lambda i: (i, 0))],
        out_specs=pl.BlockSpec((D, 128), lambda i: (0, 0)),
        out_shape=jax.ShapeDtypeStruct((D, 128), F32),
        compiler_params=_cparams(("arbitrary",)),
    )(ht, drank)


def _wgrad_call(lhs_list, lhs_of_group, rhs_list, rhs_of_group, n_transposed, name):
    n_groups = len(rhs_of_group)
    n_tb, D, tb = lhs_list[0].shape
    T = n_tb * tb
    per = min(4, n_tb)
    tk = per * tb
    nk = T // tk
    nl = len(lhs_list)

    def body(*refs):
        lhs = refs[:nl]
        rhs = refs[nl:nl + n_groups]
        out_ref, acc = refs[nl + n_groups:]
        g = pl.program_id(0)
        i = pl.program_id(1)

        @pl.when(i == 0)
        def _():
            acc[...] = jnp.zeros_like(acc)

        for p in range(n_groups):
            @pl.when(g == p)
            def _(p=p):
                lref = lhs[lhs_of_group[p]]
                part = _dot(lref[0], rhs[p][0:tb, :])
                for b in range(1, per):
                    part = part + _dot(lref[b], rhs[p][b * tb:(b + 1) * tb, :])
                acc[...] += part

        @pl.when((i == nk - 1) & (g < n_transposed))
        def _():
            out_ref[...] = _bf(acc[...].T)

        @pl.when((i == nk - 1) & (g >= n_transposed))
        def _():
            out_ref[...] = _bf(acc[...])

    def lhs_spec(a):
        groups = [g for g in range(n_groups) if lhs_of_group[g] == a]
        lo, hi = min(groups), max(groups)
        assert groups == list(range(lo, hi + 1))
        return pl.BlockSpec((per, D, tb), lambda g, i: (jnp.where((g >= lo) & (g <= hi), i, 0), 0, 0))

    def rhs_spec(p):
        cb = rhs_of_group[p][1]
        return pl.BlockSpec((tk, 1024), lambda g, i: (jnp.where(g == p, i, 0), cb))

    return pl.pallas_call(
        body, name=name,
        grid=(n_groups, nk),
        in_specs=[lhs_spec(a) for a in range(nl)] + [rhs_spec(p) for p in range(n_groups)],
        out_specs=pl.BlockSpec((None, D, 1024), lambda g, i: (g, 0, 0)),
        out_shape=jax.ShapeDtypeStruct((n_groups, D, 1024), BF16),
        scratch_shapes=[pltpu.VMEM((D, 1024), F32)],
        compiler_params=_cparams(("arbitrary", "arbitrary")),
    )(*lhs_list, *[rhs_list[rhs_of_group[p][0]] for p in range(n_groups)])


def _adamw_math(parts, w, m, v):
    g = parts[0].astype(F32)
    for p in parts[1:]:
        g = g + p.astype(F32)
    mm = ADAM_B1 * m + (1.0 - ADAM_B1) * g
    vv = ADAM_B2 * v + (1.0 - ADAM_B2) * (g * g)
    m_hat = mm / (1.0 - ADAM_B1 ** ADAM_STEP)
    v_hat = vv / (1.0 - ADAM_B2 ** ADAM_STEP)
    return g, -ADAM_LR * (m_hat / (jnp.sqrt(v_hat) + ADAM_EPS) + ADAM_WD * w), mm, vv


def _part_order(n_parts):
    return [n_parts - 1] + list(range(n_parts - 1))


def _adamw_call(parts, w, m, v, name):
    R, C = w.shape
    n_parts = parts.shape[0]
    (tr, tc), grid, idx = _tiling_2d(R, C, 512)

    def body(p_ref, w_ref, m_ref, v_ref, g_ref, d_ref, nm_ref, nv_ref):
        g_ref[...], d_ref[...], nm_ref[...], nv_ref[...] = _adamw_math(
            [p_ref[k] for k in _part_order(n_parts)], w_ref[...], m_ref[...], v_ref[...])

    blk = pl.BlockSpec((tr, tc), idx)
    sds = jax.ShapeDtypeStruct((R, C), F32)
    return pl.pallas_call(
        body, name=name,
        grid=grid,
        in_specs=[pl.BlockSpec((n_parts, tr, tc), lambda i: (0,) + idx(i)), blk, blk, blk],
        out_specs=(blk, blk, blk, blk),
        out_shape=(sds, sds, sds, sds),
        compiler_params=_cparams(("arbitrary",)),
    )(parts, w, m, v)


def _adamw_rows_call(parts, ws, ms, vs, name, gathered):
    n = len(ws)
    R, C = ws[0].shape
    n_parts = parts.shape[0]

    def body(*refs):
        p_ref = refs[0]
        w_refs, m_refs, v_refs = refs[1:1 + n], refs[1 + n:1 + 2 * n], refs[1 + 2 * n:1 + 3 * n]
        src_ref = refs[1 + 3 * n]
        outs = refs[2 + 3 * n:2 + 7 * n]
        dst_ref, send_sems, recv_sems, loc_sem = refs[2 + 7 * n:]
        own, pairs = _push_copies(src_ref, dst_ref, send_sems, recv_sems, loc_sem, scatter=False)
        k_now = pl.program_id(0)

        @pl.when(k_now == 0)
        def _():
            _push_start(own, pairs)

        for k in range(n):
            @pl.when(k_now == k)
            def _(k=k):
                res = _adamw_math([p_ref[j] for j in _part_order(n_parts)],
                                  w_refs[k][...], m_refs[k][...], v_refs[k][...])
                for o_ref, val in zip(outs[4 * k:4 * k + 4], res):
                    o_ref[...] = val

        @pl.when(k_now == n - 1)
        def _():
            _push_wait(own, pairs)

    whole = pl.BlockSpec((R, C), lambda k: (0, 0))
    sds = jax.ShapeDtypeStruct((R, C), F32)
    res = pl.pallas_call(
        body, name=name,
        grid=(n,),
        in_specs=[pl.BlockSpec((n_parts, R, C), lambda k: (0, k, 0))] + [whole] * (3 * n) + [_ANY],
        out_specs=tuple([whole] * (4 * n) + [_ANY]),
        out_shape=tuple([sds] * (4 * n) + [jax.ShapeDtypeStruct((N_DEV,) + gathered.shape, gathered.dtype)]),
        scratch_shapes=_PUSH_SEMS,
        compiler_params=_cparams(("arbitrary",)),
    )(parts, *ws, *ms, *vs, gathered)
    return [res[4 * k:4 * k + 4] for k in range(n)], res[4 * n]


def _adamw_lanes_call(parts, offsets, ws, ms, vs, name):
    n = len(ws)
    n_parts = parts.shape[0]

    def body(*refs):
        p_ref = refs[0]
        w_refs, m_refs, v_refs = refs[1:1 + n], refs[1 + n:1 + 2 * n], refs[1 + 2 * n:1 + 3 * n]
        outs = refs[1 + 3 * n:]
        for k in range(n):
            lanes = slice(offsets[k], offsets[k] + ws[k].shape[1])
            res = _adamw_math([p_ref[j, :, lanes] for j in _part_order(n_parts)],
                              w_refs[k][...], m_refs[k][...], v_refs[k][...])
            for o_ref, val in zip(outs[4 * k:4 * k + 4], res):
                o_ref[...] = val

    res = pl.pallas_call(
        body, name=name,
        out_shape=tuple(jax.ShapeDtypeStruct(ws[k].shape, F32) for k in range(n) for _ in range(4)),
        compiler_params=_cparams(),
    )(parts, *ws, *ms, *vs)
    return [res[4 * k:4 * k + 4] for k in range(n)]


def _local_step(x, target, wt, wr, wdec, bdec, wp_shard, norm_g, gla_g, b_gate, final_g):
    D = x.shape[1]
    half = wp_shard.shape[1] // 2
    projf, projb, rank, ht, wp_lo = _proj_call(x, norm_g, wt, wr, wp_shard[:, :half])
    o_gla, st_all, la = _gla_fwd_call(projf, projb, rank, wdec, bdec)
    o_sb, wp_hi = _sb_fwd_call(projb, wp_shard[:, half:])
    wp_full = jnp.concatenate([wp_lo, wp_hi], axis=2).transpose(1, 0, 2, 3).reshape(3, D, D)
    (dx2, do_gla, do_sb, dggate, dsgate, dmlog, mt, ogt, obt, dx2b, dya, dyb,
     dfinal_g, db_gate, dgla_g, loss) = _mid_call(o_gla, o_sb, projf, x, target, wp_full[0], wp_full[1],
                                                 wp_full[2], gla_g, b_gate, final_g)
    dw_p = _wgrad_call([ogt, obt, mt], [0, 1, 2], [dya, dyb, dx2b], [(0, 0), (1, 0), (2, 0)], 0, "wgrad_p")
    g_p = dw_p.reshape(3, N_DEV, D // N_DEV, D).transpose(1, 0, 2, 3).reshape(N_DEV, 3 * (D // N_DEV), D)
    dqk, dgv, drank, dwdec, dbdec = _gla_bwd_call(projf, projb, la, do_gla, st_all, rank, wdec)
    dsq, dsk, dsv, r_p = _sb_bwd_call(projb, do_sb, g_p)
    pieces = [dqk, dgv, dggate, dsq, dsk, dsv, dsgate]
    rhs_of_group = [(g, 0) for g in range(7)] + [(7, 0), (7, 1)]
    dw_in = _wgrad_call([ht], [0] * N_GROUPS, pieces + [dmlog], rhs_of_group, N_GROUPS, "wgrad_in")
    dwr = _wgrad_rank_call(ht, drank)
    g_in = _parts_by_device_call(dw_in.reshape(N_GROUPS * 1024, D), dwr[:, :GLA_RANK].T.astype(BF16))
    c_idx = lax.axis_index("c").astype(jnp.int32).reshape(1)
    (p_in,) = _pair_exchange([g_in], "pair_g")
    s_in = _pair_add_call(g_in, p_in, c_idx, "pair_add_in")
    grad_x, dnorm_g, r_in, _ = _dh_call(pieces, dmlog, drank, wt, wr, x, dx2, norm_g, s_in)
    small = jnp.concatenate([
        dnorm_g.reshape(-1), dbdec.reshape(-1), dgla_g.reshape(-1), db_gate.reshape(-1), dfinal_g.reshape(-1),
        loss.reshape(-1), dwdec[:GLA_RANK].reshape(-1)]).reshape(1, _SM_LEN)
    return grad_x, r_in, r_p, small


_SM_NORM = 0
_SM_BDEC = _SM_NORM + D_MODEL
_SM_GLAG = _SM_BDEC + GLA_DK
_SM_BGATE = _SM_GLAG + GLA_HV
_SM_FINAL = _SM_BGATE + 2 * D_MODEL
_SM_REPL = _SM_FINAL + D_MODEL
_SM_LOSS = _SM_REPL
_SM_WDEC = _SM_LOSS + 128
_SM_LEN = _SM_WDEC + GLA_RANK * GLA_DK


def kernel(x, norm_g, w_in, w_dec_up, b_dec, gla_norm_g, w_pa, w_pb, b_gate, w_o, final_g, loss_target, m_norm_g, m_w_in, m_w_dec_up, m_b_dec, m_gla_norm_g, m_w_pa, m_w_pb, m_b_gate, m_w_o, m_final_g, v_norm_g, v_w_in, v_w_dec_up, v_b_dec, v_gla_norm_g, v_w_pa, v_w_pb, v_b_gate, v_w_o, v_final_g):
    D = D_MODEL
    me = 4 * lax.axis_index("x") + 2 * lax.axis_index("y") + lax.axis_index("c")

    wp_shard = jnp.stack([w_pa, w_pb, w_o]).astype(BF16)
    n_first = _half_rows(SHARD_COLS)
    win_all, wdec_all = _all_gather([w_in.T.astype(BF16), w_dec_up], "gather_w",
                                    row_pieces=[[(0, n_first), (n_first, SHARD_COLS - n_first)], None])
    wt = _flatten_blocks_call(win_all)
    wr = jnp.pad(wt[RANK_COL:RANK_COL + GLA_RANK], ((0, 128 - GLA_RANK), (0, 0)))
    wdec_full = wdec_all.transpose(1, 0, 2).reshape(GLA_RANK, GLA_DK)
    wdec = jnp.pad(wdec_full, ((0, 128 - GLA_RANK), (0, 0)))

    grad_x, r_in, r_p, small = _local_step(
        x[0], loss_target[0], wt, wr, wdec, b_dec.reshape(1, -1), wp_shard,
        norm_g.reshape(1, -1), gla_norm_g.reshape(1, -1), b_gate.reshape(1, -1), final_g.reshape(1, -1))

    gw_in, d_in, nm_in, nv_in = (a.T for a in _adamw_call(r_in, w_in.T, m_w_in.T, v_w_in.T, "adamw_in"))
    ((g_pa, d_pa, nm_pa, nv_pa), (g_pb, d_pb, nm_pb, nv_pb), (g_o, d_o, nm_o, nv_o)), r_small = _adamw_rows_call(
        r_p, [w_pa, w_pb, w_o], [m_w_pa, m_w_pb, m_w_o], [v_w_pa, v_w_pb, v_w_o], "adamw_p", small)

    def row(a):
        return a.reshape(1, -1)

    rep = _adamw_lanes_call(
        r_small, [_SM_NORM, _SM_BDEC, _SM_GLAG, _SM_BGATE, _SM_FINAL],
        [row(a) for a in (norm_g, b_dec, gla_norm_g, b_gate, final_g)],
        [row(a) for a in (m_norm_g, m_b_dec, m_gla_norm_g, m_b_gate, m_final_g)],
        [row(a) for a in (v_norm_g, v_b_dec, v_gla_norm_g, v_b_gate, v_final_g)], "adamw_rep")
    ((g_norm, d_norm, nm_norm, nv_norm), (g_bdec, d_bdec, nm_bdec, nv_bdec), (g_glag, d_glag, nm_glag, nv_glag),
     (g_bgate, d_bgate, nm_bgate, nv_bgate), (g_final, d_final, nm_final, nv_final)) = [
        tuple(a.reshape(-1) for a in quad) for quad in rep]

    wdec_parts = r_small[:, 0, _SM_WDEC:].reshape(N_DEV, GLA_RANK, GLA_DK)
    cols = GLA_DK // N_DEV
    wdec_mine = lax.dynamic_slice_in_dim(wdec_parts, me * cols, cols, axis=2)
    g_wdec, d_wdec, nm_wdec, nv_wdec = _adamw_call(wdec_mine, w_dec_up, m_w_dec_up, v_w_dec_up, "adamw_dec")

    loss_total = jnp.sum(r_small[:, 0, _SM_LOSS])

    return (loss_total, grad_x[None],
            g_norm, gw_in, g_wdec, g_bdec, g_glag, g_pa, g_pb, g_bgate, g_o, g_final,
            d_norm, d_in, d_wdec, d_bdec, d_glag, d_pa, d_pb, d_bgate, d_o, d_final,
            nm_norm, nm_in, nm_wdec, nm_bdec, nm_glag, nm_pa, nm_pb, nm_bgate, nm_o, nm_final,
            nv_norm, nv_in, nv_wdec, nv_bdec, nv_glag, nv_pa, nv_pb, nv_bgate, nv_o, nv_final)
```

```python
import math

import jax
import jax.numpy as jnp
from jax import lax
from jax.experimental import pallas as pl
from jax.experimental.pallas import tpu as pltpu

F32 = jnp.float32
BF16 = jnp.bfloat16

N_DEV = 8
D_MODEL = 1024
GLA_HEADS = 4
GLA_HK = 128
GLA_HV = 256
GLA_DK = 512
GLA_RANK = 16
GLA_TAU = 16.0
GLA_CHUNK = 64
SB_HEADS = 8
SB_HD = 128
EPS = 1e-6
N_GROUPS = 9
RANK_COL = 3072
IN_COLS = 9232
SHARD_COLS = IN_COLS // N_DEV

ADAM_LR = 0.001
ADAM_B1 = 0.9
ADAM_B2 = 0.999
ADAM_EPS = 1e-08
ADAM_WD = 0.01
ADAM_STEP = 10

VMEM_LIMIT = 56 * 1024 * 1024
TBLK = 256


def _cparams(sem=None):
    return pltpu.CompilerParams(dimension_semantics=sem, vmem_limit_bytes=VMEM_LIMIT)


def _tiling_2d(rows, cols, band_cols):
    if rows * cols <= 128 * 1024:
        return (rows, cols), (1,), lambda i: (0, 0)
    if rows % 128 == 0:
        return (128, cols), (rows // 128,), lambda i: (i, 0)
    tc = band_cols if cols % band_cols == 0 else cols
    return (rows, tc), (cols // tc,), lambda i: (0, i)


def _dot(a, b):
    return jnp.dot(a, b, preferred_element_type=F32)


def _dot_nt(a, b):
    return lax.dot_general(a, b, (((1,), (1,)), ((), ())), preferred_element_type=F32)


def _dot_tn(a, b):
    return lax.dot_general(a, b, (((0,), (0,)), ((), ())), preferred_element_type=F32)


def _bf(x):
    return x.astype(BF16)


def _split3(x):
    hi = x.astype(BF16)
    r = x - hi.astype(F32)
    mid = r.astype(BF16)
    lo = (r - mid.astype(F32)).astype(BF16)
    return hi, mid, lo


def _tri_left(tri, x):
    hi, mid, lo = _split3(x)
    return _dot(tri, hi) + _dot(tri, mid) + _dot(tri, lo)


def _split2(x):
    hi = lax.bitcast_convert_type(lax.bitcast_convert_type(x, jnp.uint32) & jnp.uint32(0xFFFF0000), F32)
    return hi.astype(BF16), (x - hi).astype(BF16)


def _tri2_left(tri, x):
    hi, lo = _split2(x)
    return _dot(tri, hi) + _dot(tri, lo)


def _tri2_right(x, tri):
    hi, lo = _split2(x)
    return _dot(hi, tri) + _dot(lo, tri)


def _iota2(n, m, dim):
    return lax.broadcasted_iota(jnp.int32, (n, m), dim)


def _sigmoid(x):
    return 1.0 / (1.0 + jnp.exp(-x))


def _softplus_neg_abs(z):
    return jnp.log(1.0 + jnp.exp(-jnp.abs(z)))


_ANY = pl.BlockSpec(memory_space=pl.ANY)


def _mesh_pos():
    return lax.axis_index("x"), lax.axis_index("y"), lax.axis_index("c")


def _other_chips(x, y):
    return [(1 - x, y), (x, 1 - y), (1 - x, 1 - y)]


def _rcopy(src, dst, send_sem, recv_sem, to):
    return pltpu.make_async_remote_copy(src_ref=src, dst_ref=dst, send_sem=send_sem, recv_sem=recv_sem,
                                        device_id=to, device_id_type=pl.DeviceIdType.MESH)


def _push_copies(src_ref, dst_ref, send_sems, recv_sems, loc_sem, scatter):
    x, y, c = _mesh_pos()
    me = 4 * x + 2 * y + c
    own = pltpu.make_async_copy(src_ref.at[me] if scatter else src_ref, dst_ref.at[me], loc_sem)
    pairs = []
    for k in range(1, N_DEV):
        px = 1 - x if k & 4 else x
        py = 1 - y if k & 2 else y
        pc = 1 - c if k & 1 else c
        pid = 4 * px + 2 * py + pc
        src = src_ref.at[pid] if scatter else src_ref
        send = _rcopy(src, dst_ref.at[me], send_sems.at[k - 1], recv_sems.at[k - 1], (px, py, pc))
        recv = _rcopy(src, dst_ref.at[pid], send_sems.at[k - 1], recv_sems.at[k - 1], (px, py, pc))
        pairs.append((send, recv))
    return own, pairs


def _push_start(own, pairs):
    own.start()
    for send, _ in pairs:
        send.start()


def _push_wait(own, pairs):
    for _, recv in pairs:
        recv.wait_recv()
    for send, _ in pairs:
        send.wait_send()
    own.wait()


_PUSH_SEMS = [pltpu.SemaphoreType.DMA((N_DEV - 1,)), pltpu.SemaphoreType.DMA((N_DEV - 1,)),
              pltpu.SemaphoreType.DMA]


def _half_rows(rows):
    return (rows // 2) // 16 * 16


_ADD_ROWS = 128


def _relay_copies(to_diag, relayed_ref, send_sems, recv_sems):
    R = to_diag.shape[0]
    n0 = _half_rows(R)
    x, y, c = _mesh_pos()
    (xx, xy), (yx, yy), _ = _other_chips(x, y)
    return (_rcopy(to_diag.at[pl.ds(0, n0)], relayed_ref.at[pl.ds(0, n0)], send_sems.at[0], recv_sems.at[0],
                   (xx, xy, c)),
            _rcopy(to_diag.at[pl.ds(n0, R - n0)], relayed_ref.at[pl.ds(n0, R - n0)], send_sems.at[1],
                   recv_sems.at[1], (yx, yy, c)))


def _chip_reduce_steps(src_ref, relayed_ref, dst_ref, to_x, to_y, relayed, load_sems, send_sems, recv_sems, loc_sem):
    _, R, C = src_ref.shape
    n0 = _half_rows(R)
    x, y, c = _mesh_pos()
    (xx, xy), (yx, yy), _ = _other_chips(x, y)
    loads = (pltpu.make_async_copy(src_ref.at[2 * xx + xy], to_x, load_sems.at[0]),
             pltpu.make_async_copy(src_ref.at[2 * yx + yy], to_y, load_sems.at[1]),
             pltpu.make_async_copy(relayed_ref, relayed, load_sems.at[2]))
    own = pltpu.make_async_copy(src_ref.at[2 * x + y], dst_ref.at[2], loc_sem)
    pushes = (_rcopy(to_x, dst_ref.at[0], send_sems.at[0], recv_sems.at[0], (xx, xy, c)),
              _rcopy(to_y, dst_ref.at[1], send_sems.at[1], recv_sems.at[1], (yx, yy, c)))

    def add_rows(acc_ref, lo, hi):
        for r0 in range(lo, hi, _ADD_ROWS):
            rows = slice(r0, min(r0 + _ADD_ROWS, hi))
            acc_ref[rows, :] = (acc_ref[rows, :].astype(F32) + relayed[rows, :].astype(F32)).astype(acc_ref.dtype)

    def start():
        own.start()
        for cp in loads:
            cp.start()
        for cp in loads:
            cp.wait()
        add_rows(to_x, n0, R)
        add_rows(to_y, 0, n0)
        for cp in pushes:
            cp.start()

    def finish():
        for cp in pushes:
            cp.wait_recv()
        for cp in pushes:
            cp.wait_send()
        own.wait()

    return start, finish


def _chip_reduce_scratch(rows, cols, dtype):
    return [pltpu.VMEM((rows, cols), dtype)] * 3 + [
        pltpu.SemaphoreType.DMA((3,)), pltpu.SemaphoreType.DMA((2,)), pltpu.SemaphoreType.DMA((2,)),
        pltpu.SemaphoreType.DMA]


def _all_gather(arrs, name, row_pieces=None):
    n = len(arrs)
    pieces = [[None] if not row_pieces or not row_pieces[a] else list(row_pieces[a]) for a in range(n)]
    assert all(len(p) in (1, 2) for p in pieces)
    units = [(a, i) for a in range(n) for i in range(len(pieces[a]))]

    def body(*refs):
        ins = refs[:n]
        outs = refs[n:2 * n]
        send_sems, recv_sems, loc_sems = refs[2 * n:]
        x, y, c = _mesh_pos()
        me, sib = (x, y, c), (x, y, 1 - c)
        xn, yn, dg = [(px, py, c) for px, py in _other_chips(x, y)]

        def rows(ref, a, i):
            return ref if pieces[a][i] is None else ref.at[pl.ds(*pieces[a][i])]

        def copy(u, k, block, to, own=False):
            a, i = u
            px, py, pc = block
            dst = rows(outs[a].at[4 * px + 2 * py + pc], a, i)
            return _rcopy(rows(ins[a], a, i) if own else dst, dst, send_sems.at[a, k, i], recv_sems.at[a, k, i], to)

        started = []

        def start(cp):
            cp.start()
            started.append(cp)

        def landed_then_pass_on(u, k, block):
            copy(u, k, block, me).wait_recv()
            start(copy(u, 3 + k, block, sib))

        mine = [pltpu.make_async_copy(ins[a], outs[a].at[4 * x + 2 * y + c], loc_sems.at[a]) for a in range(n)]
        for cp in mine:
            cp.start()
        for u in units:
            start(copy(u, 0, me, sib, own=True))
        for a in range(n):
            if len(pieces[a]) == 2:
                for i, to, k in ((0, xn, 1), (1, yn, 2), (1, xn, 1), (0, yn, 2)):
                    start(copy((a, i), k, me, to, own=True))
            else:
                for to, k in ((xn, 1), (yn, 2), (dg, 3)):
                    start(copy((a, 0), k, me, to, own=True))
        for a in range(n):
            if len(pieces[a]) == 2:
                landed_then_pass_on((a, 0), 1, xn)
                start(copy((a, 0), 3, xn, yn))
                landed_then_pass_on((a, 1), 2, yn)
                start(copy((a, 1), 3, yn, xn))
                landed_then_pass_on((a, 1), 1, xn)
                landed_then_pass_on((a, 0), 2, yn)
                landed_then_pass_on((a, 0), 3, dg)
                landed_then_pass_on((a, 1), 3, dg)
            else:
                for block, k in ((xn, 1), (yn, 2), (dg, 3)):
                    landed_then_pass_on((a, 0), k, block)
        for u in units:
            copy(u, 0, sib, me).wait_recv()
            for k, (px, py, _) in ((4, xn), (5, yn), (6, dg)):
                copy(u, k, (px, py, 1 - c), me).wait_recv()
        for cp in started:
            cp.wait_send()
        for cp in mine:
            cp.wait()

    n_pc = max(len(p) for p in pieces)

    return pl.pallas_call(
        body, name=name,
        out_shape=tuple(jax.ShapeDtypeStruct((N_DEV,) + a.shape, a.dtype) for a in arrs),
        in_specs=[_ANY] * n,
        out_specs=tuple([_ANY] * n),
        scratch_shapes=[pltpu.SemaphoreType.DMA((n, 7, n_pc)), pltpu.SemaphoreType.DMA((n, 7, n_pc)),
                        pltpu.SemaphoreType.DMA((n,))],
    )(*arrs)


def _pair_exchange(arrs, name):
    n = len(arrs)

    def body(*refs):
        ins = refs[:n]
        outs = refs[n:2 * n]
        send_sems, recv_sems = refs[2 * n:]
        x, y, c = _mesh_pos()
        copies = []
        for a in range(n):
            for q in range(4):
                cp = _rcopy(ins[a].at[2 * q + (1 - c)], outs[a].at[q], send_sems.at[a, q], recv_sems.at[a, q],
                            (x, y, 1 - c))
                cp.start()
                copies.append(cp)
        for cp in copies:
            cp.wait_recv()
        for cp in copies:
            cp.wait_send()

    return pl.pallas_call(
        body, name=name,
        out_shape=tuple(jax.ShapeDtypeStruct((4,) + a.shape[1:], a.dtype) for a in arrs),
        in_specs=[_ANY] * n,
        out_specs=tuple([_ANY] * n),
        scratch_shapes=[pltpu.SemaphoreType.DMA((n, 4)), pltpu.SemaphoreType.DMA((n, 4))],
    )(*arrs)


def _pair_add_call(parts, recv, name):
    _, R, C = parts.shape
    x, y, c = (lax.axis_index(a).astype(jnp.int32) for a in ("x", "y", "c"))
    chips = [2 * px + py for px, py in reversed(_other_chips(x, y))] + [2 * x + y]
    n_chips = len(chips)

    def body(sc_ref, p_ref, r_ref, s_ref, relayed_ref, to_diag, send_sems, recv_sems):
        j = pl.program_id(0)
        s_ref[...] = (p_ref[...].astype(F32) + r_ref[...].astype(F32)).astype(s_ref.dtype)
        relays = _relay_copies(to_diag, relayed_ref, send_sems, recv_sems)

        @pl.when(j == 0)
        def _():
            to_diag[...] = s_ref[...]
            for cp in relays:
                cp.start()

        @pl.when(j == n_chips - 1)
        def _():
            for cp in relays:
                cp.wait_recv()
            for cp in relays:
                cp.wait_send()

    def chip(j, sc_ref):
        return (sc_ref[1 + j], 0, 0)

    return pl.pallas_call(
        body, name=name,
        grid_spec=pltpu.PrefetchScalarGridSpec(
            num_scalar_prefetch=1,
            grid=(n_chips,),
            in_specs=[pl.BlockSpec((None, R, C), lambda j, sc_ref: (2 * sc_ref[1 + j] + sc_ref[0], 0, 0)),
                      pl.BlockSpec((None, R, C), chip)],
            out_specs=(pl.BlockSpec((None, R, C), chip), _ANY),
            scratch_shapes=[pltpu.VMEM((R, C), parts.dtype), pltpu.SemaphoreType.DMA((2,)),
                            pltpu.SemaphoreType.DMA((2,))]),
        out_shape=(jax.ShapeDtypeStruct((n_chips, R, C), parts.dtype),
                   jax.ShapeDtypeStruct((R, C), parts.dtype)),
        compiler_params=_cparams(("arbitrary",)),
    )(jnp.stack([c] + chips), parts, recv)


def _flatten_blocks_call(blocks):
    n, R, C = blocks.shape
    tc = C // 2

    def body(in_ref, out_ref):
        for p in range(n):
            out_ref[p * R:(p + 1) * R, :] = in_ref[p]

    return pl.pallas_call(
        body, name="flatten_w",
        grid=(C // tc,),
        in_specs=[pl.BlockSpec((n, R, tc), lambda i: (0, 0, i))],
        out_specs=pl.BlockSpec((n * R, tc), lambda i: (0, i)),
        out_shape=jax.ShapeDtypeStruct((n * R, C), blocks.dtype),
        compiler_params=_cparams(("arbitrary",)),
    )(blocks)


def _parts_by_device_call(dmain, drank):
    D = dmain.shape[1]
    tc = D // 2

    def body(dm_ref, dr_ref, out_ref):
        for p in range(N_DEV):
            lo, hi = p * SHARD_COLS, (p + 1) * SHARD_COLS
            at = 0
            for src, a, b in ((dm_ref, lo, min(hi, RANK_COL)),
                              (dr_ref, max(lo, RANK_COL) - RANK_COL, min(hi, RANK_COL + GLA_RANK) - RANK_COL),
                              (dm_ref, max(lo, RANK_COL + GLA_RANK) - GLA_RANK, hi - GLA_RANK)):
                if b > a:
                    out_ref[p, at:at + (b - a), :] = src[a:b, :]
                    at += b - a

    return pl.pallas_call(
        body, name="parts_by_device",
        grid=(D // tc,),
        in_specs=[pl.BlockSpec((dmain.shape[0], tc), lambda i: (0, i)),
                  pl.BlockSpec((GLA_RANK, tc), lambda i: (0, i))],
        out_specs=pl.BlockSpec((N_DEV, SHARD_COLS, tc), lambda i: (0, 0, i)),
        out_shape=jax.ShapeDtypeStruct((N_DEV, SHARD_COLS, D), dmain.dtype),
        compiler_params=_cparams(("arbitrary",)),
    )(dmain, drank)


def _group_row(g):
    return GLA_RANK * (g * (1024 // GLA_RANK) + (g >= RANK_COL // 1024))


def _proj_call(x, norm_g, wt, wr, wp_part):
    T, D = x.shape
    tm = min(1024, T)
    assert tm % TBLK == 0
    n_i = T // tm

    def f_slot(j):
        return ((j >= 2).astype(jnp.int32) + (j >= 6).astype(jnp.int32)
                + (j >= 7).astype(jnp.int32) + (j >= 8).astype(jnp.int32))

    def b_slot(j):
        return (j >= 3).astype(jnp.int32) + (j >= 4).astype(jnp.int32) + (j >= 5).astype(jnp.int32)

    def body(x_ref, g_ref, w_ref, wr_ref, wp_ref, pf_ref, pb_ref, rank_ref, ht_ref, wpall_ref,
             h_scr, send_sems, recv_sems, loc_sem):
        i = pl.program_id(0)
        j = pl.program_id(1)
        own, pairs = _push_copies(wp_ref, wpall_ref, send_sems, recv_sems, loc_sem, scatter=False)

        @pl.when((i == 0) & (j == 0))
        def _():
            _push_start(own, pairs)

        @pl.when(j == 0)
        def _():
            xv = x_ref[...]
            r = lax.rsqrt(jnp.mean(xv * xv, axis=-1, keepdims=True) + EPS)
            h = (xv * r) * g_ref[...]
            hb = _bf(h)
            h_scr[...] = hb
            for b in range(tm // TBLK):
                ht_ref[b] = _bf(h[b * TBLK:(b + 1) * TBLK].T)
            rank_ref[...] = _dot_nt(hb, wr_ref[...])

        is_b = (j == 1) | ((j >= 3) & (j <= 5))

        @pl.when(is_b)
        def _():
            pb_ref[...] = _bf(_dot_nt(h_scr[...], w_ref[...]))

        @pl.when(jnp.logical_not(is_b))
        def _():
            pf_ref[...] = _dot_nt(h_scr[...], w_ref[...])

        @pl.when((i == n_i - 1) & (j == N_GROUPS - 1))
        def _():
            _push_wait(own, pairs)

    return pl.pallas_call(
        body, name="proj",
        grid=(n_i, N_GROUPS),
        in_specs=[pl.BlockSpec((tm, D), lambda i, j: (i, 0)),
                  pl.BlockSpec((1, D), lambda i, j: (0, 0)),
                  pl.BlockSpec((pl.Element(1024), pl.Element(D)), lambda i, j: (_group_row(j), 0)),
                  pl.BlockSpec((128, D), lambda i, j: (0, 0)),
                  _ANY],
        out_specs=(pl.BlockSpec((None, tm, 1024), lambda i, j: (f_slot(j), i, 0)),
                   pl.BlockSpec((None, tm, 1024), lambda i, j: (b_slot(j), i, 0)),
                   pl.BlockSpec((tm, 128), lambda i, j: (i, 0)),
                   pl.BlockSpec((tm // TBLK, D, TBLK), lambda i, j: (i, 0, 0)),
                   _ANY),
        out_shape=(jax.ShapeDtypeStruct((5, T, 1024), F32),
                   jax.ShapeDtypeStruct((4, T, 1024), BF16),
                   jax.ShapeDtypeStruct((T, 128), F32),
                   jax.ShapeDtypeStruct((T // TBLK, D, TBLK), BF16),
                   jax.ShapeDtypeStruct((N_DEV,) + wp_part.shape, wp_part.dtype)),
        scratch_shapes=[pltpu.VMEM((tm, D), BF16)] + _PUSH_SEMS,
        compiler_params=_cparams(("arbitrary", "arbitrary")),
    )(x, norm_g, wt, wr, wp_part)


GLA_STEP_CHUNKS = 4


def _gla_same_chunk(rows):
    return (_iota2(rows, rows, 0) & -GLA_CHUNK) == (_iota2(rows, rows, 1) & -GLA_CHUNK)


def _gla_chunk_terms(la, q, k, n_c):
    C = GLA_CHUNK
    rows = n_c * C
    low = _gla_same_chunk(rows) & (_iota2(rows, rows, 0) >= _iota2(rows, rows, 1))
    b = _tri_left(_bf(low.astype(F32)), la)
    bl = [b[(c + 1) * C - 1:(c + 1) * C, :] for c in range(n_c)]
    bl_rows = jnp.concatenate([jnp.broadcast_to(bl[c], (C, b.shape[1])) for c in range(n_c)], axis=0)
    eb = jnp.exp(b)
    enb = jnp.exp(-b)
    ebl_b = jnp.exp(bl_rows - b)
    scale = GLA_HK ** -0.5
    qe = q * eb * scale
    ke = k * enb
    kd = k * ebl_b
    return bl, eb, enb, ebl_b, qe, ke, kd


def _gla_fwd_call(projf, projb, rank, wdec, bdec):
    T = projf.shape[1]
    C = GLA_CHUNK
    n_chunks = T // C
    n_c = GLA_STEP_CHUNKS
    R = n_c * C
    assert n_chunks % n_c == 0

    def body(qk_ref, v_ref, rank_ref, wd_ref, bd_ref, o_ref, st_ref, la_ref, st_scr):
        @pl.when(pl.program_id(0) == 0)
        def _():
            st_scr[...] = jnp.zeros_like(st_scr)

        dec = _dot(_bf(rank_ref[...]), _bf(wd_ref[...])) + bd_ref[...]
        la = (jnp.minimum(dec, 0.0) - _softplus_neg_abs(dec)) / GLA_TAU
        la_ref[...] = la
        mask = _gla_same_chunk(R) & (_iota2(R, R, 0) >= _iota2(R, R, 1))
        bl, _, _, _, qe, ke, kd = _gla_chunk_terms(la, qk_ref[:, :GLA_DK], qk_ref[:, GLA_DK:], n_c)
        qeb, keb, kdb = _bf(qe), _bf(ke), _bf(kd)
        ebl = [jnp.exp(bl[c]) for c in range(n_c)]
        heads = range(GLA_HEADS)
        ks = [slice(hh * GLA_HK, (hh + 1) * GLA_HK) for hh in heads]
        vs = [slice(hh * GLA_HV, (hh + 1) * GLA_HV) for hh in heads]
        rs = [slice(c * C, (c + 1) * C) for c in range(n_c)]
        p = [_bf(jnp.where(mask, _dot_nt(qeb[:, ks[hh]], keb[:, ks[hh]]), 0.0)) for hh in heads]
        upd = [[_dot_tn(v_ref[rs[c], vs[hh]], kdb[rs[c], ks[hh]]) for hh in heads] for c in range(n_c)]
        intra = [_dot(p[hh], v_ref[:, vs[hh]]) for hh in heads]
        st = [st_scr[hh] for hh in heads]
        for c in range(n_c):
            inter = [_dot_nt(qeb[rs[c], ks[hh]], _bf(st[hh])) for hh in heads]
            for hh in heads:
                st_ref[c, hh] = st[hh]
                o_ref[rs[c], vs[hh]] = intra[hh][rs[c]] + inter[hh]
            st = [st[hh] * ebl[c][:, ks[hh]] + upd[c][hh] for hh in heads]
        for hh in heads:
            st_scr[hh] = st[hh]

    return pl.pallas_call(
        body, name="gla_fwd",
        grid=(n_chunks // n_c,),
        in_specs=[pl.BlockSpec((None, R, 1024), lambda n: (0, n, 0)),
                  pl.BlockSpec((None, R, 1024), lambda n: (0, n, 0)),
                  pl.BlockSpec((R, 128), lambda n: (n, 0)),
                  pl.BlockSpec((128, GLA_DK), lambda n: (0, 0)),
                  pl.BlockSpec((1, GLA_DK), lambda n: (0, 0))],
        out_specs=(pl.BlockSpec((R, 1024), lambda n: (n, 0)),
                   pl.BlockSpec((n_c, GLA_HEADS, GLA_HV, GLA_HK), lambda n: (n, 0, 0, 0)),
                   pl.BlockSpec((R, GLA_DK), lambda n: (n, 0))),
        out_shape=(jax.ShapeDtypeStruct((T, 1024), F32),
                   jax.ShapeDtypeStruct((n_chunks, GLA_HEADS, GLA_HV, GLA_HK), F32),
                   jax.ShapeDtypeStruct((T, GLA_DK), F32)),
        scratch_shapes=[pltpu.VMEM((GLA_HEADS, GLA_HV, GLA_HK), F32)],
        compiler_params=_cparams(("arbitrary",)),
    )(projf, projb, rank, wdec, bdec)


def _gla_bwd_call(projf, projb, la, do_gla, st_all, rank, wdec):
    T = projf.shape[1]
    C = GLA_CHUNK
    n_chunks = T // C
    n_c = GLA_STEP_CHUNKS
    R = n_c * C
    assert n_chunks % n_c == 0
    last = n_chunks // n_c - 1

    def body(qk_ref, v_ref, la_ref, do_ref, st_ref, rank_ref, wd_ref,
             dqk_ref, dv_ref, drank_ref, dwd_ref, dbd_ref, dst_scr):
        @pl.when(pl.program_id(0) == 0)
        def _():
            dst_scr[...] = jnp.zeros_like(dst_scr)
            dwd_ref[...] = jnp.zeros_like(dwd_ref)
            dbd_ref[...] = jnp.zeros_like(dbd_ref)

        same = _gla_same_chunk(R)
        mask = same & (_iota2(R, R, 0) >= _iota2(R, R, 1))
        upp = _bf((same & (_iota2(R, R, 0) <= _iota2(R, R, 1))).astype(F32))
        scale = GLA_HK ** -0.5
        la = la_ref[...]
        bl, eb, enb, ebl_b, qe, ke, kd = _gla_chunk_terms(la, qk_ref[:, :GLA_DK], qk_ref[:, GLA_DK:], n_c)
        qeb, keb, kdb = _bf(qe), _bf(ke), _bf(kd)
        ebl = [jnp.exp(bl[c]) for c in range(n_c)]
        heads = range(GLA_HEADS)
        ks = [slice(hh * GLA_HK, (hh + 1) * GLA_HK) for hh in heads]
        vs = [slice(hh * GLA_HV, (hh + 1) * GLA_HV) for hh in heads]
        rs = [slice(c * C, (c + 1) * C) for c in range(n_c)]
        v = [v_ref[:, vs[hh]] for hh in heads]
        do = [_bf(do_ref[:, vs[hh]]) for hh in heads]
        p = [_bf(jnp.where(mask, _dot_nt(qeb[:, ks[hh]], keb[:, ks[hh]]), 0.0)) for hh in heads]
        dp = [_bf(jnp.where(mask, _dot_nt(do[hh], v[hh]), 0.0)) for hh in heads]
        dst_intra = [[_dot_tn(do[hh][rs[c]], qeb[rs[c], ks[hh]]) for hh in heads] for c in range(n_c)]
        dqe_inter = [[_dot(do[hh][rs[c]], _bf(st_ref[c, hh])) for hh in heads] for c in range(n_c)]
        dv_intra = [_dot_tn(p[hh], do[hh]) for hh in heads]
        dqe_intra = [_dot(dp[hh], keb[:, ks[hh]]) for hh in heads]
        dke = jnp.concatenate([_dot_tn(dp[hh], qeb[:, ks[hh]]) for hh in heads], axis=1)
        dstn = [dst_scr[hh] for hh in heads]
        dkd_c, dv_inter, debl = [None] * n_c, [None] * n_c, [None] * n_c
        for c in reversed(range(n_c)):
            dstnb = [_bf(dstn[hh]) for hh in heads]
            dkd_c[c] = jnp.concatenate([_dot(v[hh][rs[c]], dstnb[hh]) for hh in heads], axis=1)
            dv_inter[c] = [_dot_nt(kdb[rs[c], ks[hh]], dstnb[hh]) for hh in heads]
            debl[c] = jnp.concatenate(
                [jnp.sum(dstn[hh] * st_ref[c, hh], axis=0, keepdims=True) for hh in heads], axis=1)
            dstn = [dst_intra[c][hh] + dstn[hh] * ebl[c][:, ks[hh]] for hh in heads]
        for hh in heads:
            dst_scr[hh] = dstn[hh]
            dv_ref[:, vs[hh]] = _bf(dv_intra[hh] + jnp.concatenate([dv_inter[c][hh] for c in range(n_c)], axis=0))
        dqe = jnp.concatenate(
            [dqe_intra[hh] + jnp.concatenate([dqe_inter[c][hh] for c in range(n_c)], axis=0) for hh in heads], axis=1)
        dkd = jnp.concatenate(dkd_c, axis=0)
        dkd_kd = dkd * kd
        db = dqe * qe - dke * ke - dkd_kd
        dbl = jnp.concatenate(
            [jnp.broadcast_to(jnp.sum(dkd_kd[rs[c]], axis=0, keepdims=True) + ebl[c] * debl[c], (C, GLA_DK))
             for c in range(n_c)], axis=0)
        dla = _tri_left(upp, db) + dbl
        dqk_ref[:, :GLA_DK] = _bf(dqe * eb * scale)
        dqk_ref[:, GLA_DK:] = _bf(dke * enb + dkd * ebl_b)
        ddec = dla * (1.0 / GLA_TAU) * (1.0 - jnp.exp(GLA_TAU * la))
        ddecb = _bf(ddec)
        drank_ref[...] = _bf(_dot_nt(ddecb, _bf(wd_ref[...])))
        dwd_ref[...] += _dot_tn(_bf(rank_ref[...]), ddecb)
        dbd_ref[...] += jnp.sum(ddec, axis=0, keepdims=True)

    return pl.pallas_call(
        body, name="gla_bwd",
        grid=(n_chunks // n_c,),
        in_specs=[pl.BlockSpec((None, R, 1024), lambda n: (0, last - n, 0)),
                  pl.BlockSpec((None, R, 1024), lambda n: (0, last - n, 0)),
                  pl.BlockSpec((R, GLA_DK), lambda n: (last - n, 0)),
                  pl.BlockSpec((R, 1024), lambda n: (last - n, 0)),
                  pl.BlockSpec((n_c, GLA_HEADS, GLA_HV, GLA_HK), lambda n: (last - n, 0, 0, 0)),
                  pl.BlockSpec((R, 128), lambda n: (last - n, 0)),
                  pl.BlockSpec((128, GLA_DK), lambda n: (0, 0))],
        out_specs=(pl.BlockSpec((R, 1024), lambda n: (last - n, 0)),
                   pl.BlockSpec((R, 1024), lambda n: (last - n, 0)),
                   pl.BlockSpec((R, 128), lambda n: (last - n, 0)),
                   pl.BlockSpec((128, GLA_DK), lambda n: (0, 0)),
                   pl.BlockSpec((1, GLA_DK), lambda n: (0, 0))),
        out_shape=(jax.ShapeDtypeStruct((T, 1024), BF16),
                   jax.ShapeDtypeStruct((T, 1024), BF16),
                   jax.ShapeDtypeStruct((T, 128), BF16),
                   jax.ShapeDtypeStruct((128, GLA_DK), F32),
                   jax.ShapeDtypeStruct((1, GLA_DK), F32)),
        scratch_shapes=[pltpu.VMEM((GLA_HEADS, GLA_HV, GLA_HK), F32)],
        compiler_params=_cparams(("arbitrary",)),
    )(projf, projb, la, do_gla, st_all, rank, wdec)


def _sb_logs(z):
    lsz = jnp.minimum(z, 0.0) - _softplus_neg_abs(z)
    return lsz, lsz - z


SB_HG_FWD = 8
SB_HG_BWD = 4
SB_QUERIES = 256
SB_KEYS = 256
SB_DEAD = -105.0


def _sb_fwd_call(projb, wp_shard):
    T = projb.shape[1]
    B = min(SB_QUERIES, T)
    HG = SB_HG_FWD
    W = HG * SB_HD
    scale = 1.0 / math.sqrt(SB_HD)
    KB = min(SB_KEYS, T)
    n_h, n_i = SB_HEADS // HG, T // B

    def body(q_ref, k_ref, v_ref, wp_ref, o_ref, wpall_ref, cb_scr, send_sems, recv_sems, loc_sem):
        i = pl.program_id(1)
        own, pairs = _push_copies(wp_ref, wpall_ref, send_sems, recv_sems, loc_sem, scatter=False)

        @pl.when((pl.program_id(0) == 0) & (i == 0))
        def _():
            _push_start(own, pairs)

        rows = HG * B
        after = (_iota2(KB, KB, 0) > _iota2(KB, KB, 1)).astype(F32)
        tri = _bf(jnp.concatenate([after, jnp.ones((KB, KB), F32)], axis=1))
        o_ref[...] = jnp.zeros_like(o_ref)
        cb_scr[...] = jnp.zeros_like(cb_scr)

        def block(jp, masked):
            off = pl.multiple_of(jp * KB, KB)
            z = jnp.concatenate(
                [_dot_nt(q_ref[:, hh * SB_HD:(hh + 1) * SB_HD], k_ref[pl.ds(off, KB), hh * SB_HD:(hh + 1) * SB_HD])
                 for hh in range(HG)], axis=0) * scale
            lsz, l1m = _sb_logs(z)
            if masked:
                strict = (jp * KB + _iota2(rows, KB, 1)) < (i * B + (_iota2(rows, KB, 0) & (B - 1)))
                l1m = jnp.where(strict, l1m, 0.0)
            r = _tri2_right(l1m, tri)
            cb = cb_scr[...]
            a = jnp.exp(lsz + cb + r[:, :KB])
            if masked:
                a = jnp.where(strict, a, 0.0)
            cb_scr[...] = cb + r[:, KB:]
            ab = _bf(a)
            for hh in range(HG):
                cs = slice(hh * SB_HD, (hh + 1) * SB_HD)
                o_ref[:, cs] += _dot(ab[hh * B:(hh + 1) * B, :], v_ref[pl.ds(off, KB), cs])

        jp0 = (i * B) // KB
        block(jp0, True)

        def live(state):
            jj, dead = state
            return (jj <= jp0) & jnp.logical_not(dead)

        def step(state):
            jj, _ = state
            block(jp0 - jj, False)
            return jj + 1, jnp.max(cb_scr[:, :SB_HD]) < SB_DEAD

        lax.while_loop(live, step, (jnp.int32(1), jnp.max(cb_scr[:, :SB_HD]) < SB_DEAD))

        @pl.when((pl.program_id(0) == n_h - 1) & (i == n_i - 1))
        def _():
            _push_wait(own, pairs)

    return pl.pallas_call(
        body, name="sb_fwd",
        grid=(n_h, n_i),
        in_specs=[pl.BlockSpec((None, B, W), lambda h, i: (1, i, h)),
                  pl.BlockSpec((None, T, W), lambda h, i: (2, 0, h)),
                  pl.BlockSpec((None, T, W), lambda h, i: (3, 0, h)),
                  _ANY],
        out_specs=(pl.BlockSpec((B, W), lambda h, i: (i, h)), _ANY),
        out_shape=(jax.ShapeDtypeStruct((T, 1024), F32),
                   jax.ShapeDtypeStruct((N_DEV,) + wp_shard.shape, wp_shard.dtype)),
        scratch_shapes=[pltpu.VMEM((HG * B, KB), F32)] + _PUSH_SEMS,
        compiler_params=_cparams(("arbitrary", "arbitrary")),
    )(projb, projb, projb, wp_shard)


def _sb_bwd_call(projb, do_sb, g_p):
    T = projb.shape[1]
    B = min(SB_QUERIES, T)
    nb = T // B
    HG = SB_HG_BWD
    W = HG * SB_HD
    WQ = HG * B
    KB = min(SB_KEYS, T)
    nkb = T // KB
    n_h = SB_HEADS // HG
    scale = 1.0 / math.sqrt(SB_HD)

    def body(q_ref, k_ref, v_ref, do_ref, gp_ref, dq_ref, dk_ref, dv_ref, rp_ref,
             dk_scr, dv_scr, kt_scr, beta_scr, g_scr, dqt_scr, send_sems, recv_sems, loc_sem):
        i = pl.program_id(1)
        own, pairs = _push_copies(gp_ref, rp_ref, send_sems, recv_sems, loc_sem, scatter=True)

        @pl.when((pl.program_id(0) == 0) & (i == 0))
        def _():
            _push_start(own, pairs)

        @pl.when(i == 0)
        def _():
            dk_scr[...] = jnp.zeros_like(dk_scr)
            dv_scr[...] = jnp.zeros_like(dv_scr)
            for hh in range(HG):
                for jb in range(nkb):
                    kt_scr[hh, jb] = _bf(
                        k_ref[jb * KB:(jb + 1) * KB, hh * SB_HD:(hh + 1) * SB_HD].astype(F32).T)

        dqt_scr[...] = jnp.zeros_like(dqt_scr)
        later = _bf((_iota2(KB, KB, 1) > _iota2(KB, KB, 0)).astype(F32))
        earlier = _bf((_iota2(KB, KB, 1) < _iota2(KB, KB, 0)).astype(F32))
        dob = _bf(do_ref[...])
        jp0 = (i * B) // KB

        def strict_mask():
            return (jp0 * KB + _iota2(KB, WQ, 0)) < (i * B + (_iota2(KB, WQ, 1) & (B - 1)))

        def heads(fn):
            return [fn(slice(hh * SB_HD, (hh + 1) * SB_HD)) for hh in range(HG)]

        def pass1(jp, cb, masked):
            off = pl.multiple_of(jp * KB, KB)
            z = jnp.concatenate(heads(lambda cs: _dot_nt(k_ref[pl.ds(off, KB), cs], q_ref[:, cs])), axis=1) * scale
            da = jnp.concatenate(heads(lambda cs: _dot_nt(v_ref[pl.ds(off, KB), cs], dob[:, cs])), axis=1)
            lsz, l1m = _sb_logs(z)
            if masked:
                strict = strict_mask()
                l1m = jnp.where(strict, l1m, 0.0)
            a = jnp.exp(lsz + cb + _tri2_left(later, l1m))
            if masked:
                a = jnp.where(strict, a, 0.0)
            g_scr[jp] = a * da
            beta_scr[jp] = jnp.exp(lsz)
            ab = _bf(a)
            for hh in range(HG):
                cs = slice(hh * SB_HD, (hh + 1) * SB_HD)
                dv_scr[pl.ds(off, KB), cs] += _dot(ab[:, hh * B:(hh + 1) * B], dob[:, cs])
            return cb + jnp.sum(l1m, axis=0, keepdims=True)

        zero = jnp.zeros((1, WQ), F32)
        cb = pass1(jp0, zero, True)

        def live(state):
            jj, _, dead = state
            return (jj <= jp0) & jnp.logical_not(dead)

        def step(state):
            jj, cr, _ = state
            cr = pass1(jp0 - jj, cr, False)
            return jj + 1, cr, jnp.max(cr) < SB_DEAD

        n_done, _, _ = lax.while_loop(live, step, (jnp.int32(1), cb, jnp.max(cb) < SB_DEAD))
        jp_first = jp0 - (n_done - 1)

        def pass2(jp, cg, masked):
            off = pl.multiple_of(jp * KB, KB)
            g = g_scr[jp]
            beta = beta_scr[jp]
            dz = g * (1.0 - beta) - beta * (cg + _tri2_left(earlier, g))
            if masked:
                dz = jnp.where(strict_mask(), dz, 0.0)
            dzb = _bf(dz * scale)
            for hh in range(HG):
                cs = slice(hh * SB_HD, (hh + 1) * SB_HD)
                dk_scr[pl.ds(off, KB), cs] += _dot(dzb[:, hh * B:(hh + 1) * B], q_ref[:, cs])
                dqt_scr[hh] += _dot(kt_scr[hh, jp], dzb[:, hh * B:(hh + 1) * B])
            return cg + jnp.sum(g, axis=0, keepdims=True)

        cg = lax.fori_loop(jp_first, jp0, lambda jp, cr: pass2(jp, cr, False), zero)
        pass2(jp0, cg, True)
        for hh in range(HG):
            dq_ref[:, hh * SB_HD:(hh + 1) * SB_HD] = _bf(dqt_scr[hh].T)

        @pl.when(i == nb - 1)
        def _():
            dk_ref[...] = _bf(dk_scr[...])
            dv_ref[...] = _bf(dv_scr[...])

        @pl.when((pl.program_id(0) == n_h - 1) & (i == nb - 1))
        def _():
            _push_wait(own, pairs)

    return pl.pallas_call(
        body, name="sb_bwd",
        grid=(n_h, nb),
        in_specs=[pl.BlockSpec((None, B, W), lambda h, i: (1, i, h)),
                  pl.BlockSpec((None, T, W), lambda h, i: (2, 0, h)),
                  pl.BlockSpec((None, T, W), lambda h, i: (3, 0, h)),
                  pl.BlockSpec((B, W), lambda h, i: (i, h)),
                  _ANY],
        out_specs=(pl.BlockSpec((B, W), lambda h, i: (i, h)),
                   pl.BlockSpec((T, W), lambda h, i: (0, h)),
                   pl.BlockSpec((T, W), lambda h, i: (0, h)),
                   _ANY),
        out_shape=(jax.ShapeDtypeStruct((T, 1024), BF16),
                   jax.ShapeDtypeStruct((T, 1024), BF16),
                   jax.ShapeDtypeStruct((T, 1024), BF16),
                   jax.ShapeDtypeStruct(g_p.shape, g_p.dtype)),
        scratch_shapes=[pltpu.VMEM((T, W), F32), pltpu.VMEM((T, W), F32),
                        pltpu.VMEM((HG, nkb, SB_HD, KB), BF16),
                        pltpu.VMEM((nkb, KB, WQ), F32), pltpu.VMEM((nkb, KB, WQ), F32),
                        pltpu.VMEM((HG, SB_HD, B), F32)] + _PUSH_SEMS,
        compiler_params=_cparams(("arbitrary", "arbitrary")),
    )(projb, projb, projb, do_sb, g_p)


def _mid_call(o_gla, o_sb, projf, x, target, wpa, wpb, wo, gla_g, b_gate, final_g):
    T, D = x.shape
    tm = min(TBLK, T)

    def body(og_ref, ggate_ref, osb_ref, sgate_ref, ma_ref, mb_ref, x_ref, tgt_ref,
             wpa_ref, wpb_ref, wo_ref, glag_ref, bg_ref, fg_ref,
             dx2_ref, dogla_ref, dosb_ref, dggate_ref, dsgate_ref, dm_ref,
             mt_ref, ogt_ref, obt_ref, dx2b_ref, dya_ref, dyb_ref,
             dfg_ref, dbg_ref, dglag_ref, loss_ref):
        @pl.when(pl.program_id(0) == 0)
        def _():
            dfg_ref[...] = jnp.zeros_like(dfg_ref)
            dbg_ref[...] = jnp.zeros_like(dbg_ref)
            dglag_ref[...] = jnp.zeros_like(dglag_ref)
            loss_ref[...] = jnp.zeros_like(loss_ref)

        glag = glag_ref[...]
        ggate = ggate_ref[...]
        sg = _sigmoid(ggate)
        silu_g = ggate * sg
        ohat, rinv, nrm = [], [], []
        for hh in range(GLA_HEADS):
            oh = og_ref[:, hh * GLA_HV:(hh + 1) * GLA_HV]
            r = lax.rsqrt(jnp.mean(oh * oh, axis=-1, keepdims=True) + EPS)
            ohat.append(oh * r)
            rinv.append(r)
            nrm.append(ohat[-1] * glag)
        n_all = jnp.concatenate(nrm, axis=1)
        og = n_all * silu_g
        ogb = _bf(og)
        ya = _dot(ogb, wpa_ref[...])
        sgate = sgate_ref[...]
        ss = _sigmoid(sgate)
        silu_s = sgate * ss
        osb = osb_ref[...]
        ob = osb * silu_s
        obb = _bf(ob)
        yb = _dot(obb, wpb_ref[...])
        ga = _sigmoid(ma_ref[...] + bg_ref[:, :D])
        gb = _sigmoid(mb_ref[...] + bg_ref[:, D:])
        merged = ga * ya + gb * yb
        mgb = _bf(merged)
        x2 = x_ref[...] + _dot(mgb, wo_ref[...])
        r2 = lax.rsqrt(jnp.mean(x2 * x2, axis=-1, keepdims=True) + EPS)
        xh2 = x2 * r2
        fg = fg_ref[...]
        err = xh2 * fg - tgt_ref[...]
        loss_ref[...] += jnp.broadcast_to(
            0.5 * jnp.sum(jnp.mean(err * err, axis=-1, keepdims=True), axis=0, keepdims=True), (1, 128))
        dy = err * (1.0 / D)
        dfg_ref[...] += jnp.sum(dy * xh2, axis=0, keepdims=True)
        dxh = dy * fg
        dx2 = r2 * (dxh - xh2 * jnp.mean(dxh * xh2, axis=-1, keepdims=True))
        dx2_ref[...] = dx2
        dx2b = _bf(dx2)
        dx2b_ref[...] = dx2b
        dmerged = _dot_nt(dx2b, wo_ref[...])
        dya = dmerged * ga
        dyb = dmerged * gb
        dma = dmerged * ya * ga * (1.0 - ga)
        dmb = dmerged * yb * gb * (1.0 - gb)
        dm_ref[:, :D] = _bf(dma)
        dm_ref[:, D:] = _bf(dmb)
        dbg_ref[:, :D] += jnp.sum(dma, axis=0, keepdims=True)
        dbg_ref[:, D:] += jnp.sum(dmb, axis=0, keepdims=True)
        dyab = _bf(dya)
        dybb = _bf(dyb)
        dya_ref[...] = dyab
        dyb_ref[...] = dybb
        dog = _dot_nt(dyab, wpa_ref[...])
        dob = _dot_nt(dybb, wpb_ref[...])
        dosb_ref[...] = dob * silu_s
        dsgate_ref[...] = _bf(dob * osb * (ss * (1.0 + sgate * (1.0 - ss))))
        dn = dog * silu_g
        dggate_ref[...] = _bf(dog * n_all * (sg * (1.0 + ggate * (1.0 - sg))))
        dglag = jnp.zeros((1, GLA_HV), F32)
        for hh in range(GLA_HEADS):
            dnh = dn[:, hh * GLA_HV:(hh + 1) * GLA_HV]
            dglag = dglag + jnp.sum(dnh * ohat[hh], axis=0, keepdims=True)
            dohat = dnh * glag
            dogla_ref[:, hh * GLA_HV:(hh + 1) * GLA_HV] = rinv[hh] * (
                dohat - ohat[hh] * jnp.mean(dohat * ohat[hh], axis=-1, keepdims=True))
        dglag_ref[...] += dglag
        mt_ref[...] = _bf(merged.T)
        ogt_ref[...] = _bf(og.T)
        obt_ref[...] = _bf(ob.T)

    row = lambda i: (i, 0)
    const = lambda i: (0, 0)
    tile = pl.BlockSpec((tm, D), row)
    tile_t = pl.BlockSpec((None, D, tm), lambda i: (i, 0, 0))
    wspec = pl.BlockSpec((D, D), const)
    return pl.pallas_call(
        body, name="mid",
        grid=(T // tm,),
        in_specs=[tile,
                  pl.BlockSpec((None, tm, D), lambda i: (1, i, 0)),
                  tile,
                  pl.BlockSpec((None, tm, D), lambda i: (2, i, 0)),
                  pl.BlockSpec((None, tm, D), lambda i: (3, i, 0)),
                  pl.BlockSpec((None, tm, D), lambda i: (4, i, 0)),
                  tile, tile, wspec, wspec, wspec,
                  pl.BlockSpec((1, GLA_HV), const),
                  pl.BlockSpec((1, 2 * D), const),
                  pl.BlockSpec((1, D), const)],
        out_specs=(tile, tile, tile, tile, tile,
                   pl.BlockSpec((tm, 2 * D), row),
                   tile_t, tile_t, tile_t, tile, tile, tile,
                   pl.BlockSpec((1, D), const),
                   pl.BlockSpec((1, 2 * D), const),
                   pl.BlockSpec((1, GLA_HV), const),
                   pl.BlockSpec((1, 128), const)),
        out_shape=(jax.ShapeDtypeStruct((T, D), F32),
                   jax.ShapeDtypeStruct((T, D), F32),
                   jax.ShapeDtypeStruct((T, D), F32),
                   jax.ShapeDtypeStruct((T, D), BF16),
                   jax.ShapeDtypeStruct((T, D), BF16),
                   jax.ShapeDtypeStruct((T, 2 * D), BF16),
                   jax.ShapeDtypeStruct((T // tm, D, tm), BF16),
                   jax.ShapeDtypeStruct((T // tm, D, tm), BF16),
                   jax.ShapeDtypeStruct((T // tm, D, tm), BF16),
                   jax.ShapeDtypeStruct((T, D), BF16),
                   jax.ShapeDtypeStruct((T, D), BF16),
                   jax.ShapeDtypeStruct((T, D), BF16),
                   jax.ShapeDtypeStruct((1, D), F32),
                   jax.ShapeDtypeStruct((1, 2 * D), F32),
                   jax.ShapeDtypeStruct((1, GLA_HV), F32),
                   jax.ShapeDtypeStruct((1, 128), F32)),
        compiler_params=_cparams(("arbitrary",)),
    )(o_gla, projf, o_sb, projf, projf, projf, x, target, wpa, wpb, wo, gla_g, b_gate, final_g)


def _dh_call(pieces, dmlog, drank, wt, wr, x, dx2, norm_g, s_in, relayed):
    T, D = x.shape
    tm = min(256, T)
    npc = len(pieces)
    n_main = N_GROUPS * 1024
    n_i = T // tm

    def body(*refs):
        pcs = refs[:npc]
        (dm_ref, dr_ref, w_hbm, wr_ref, x_ref, dx2_ref, g_ref, sin_ref, relayed_ref,
         gx_ref, dg_ref, rin_ref, w_scr, sems, *exchange_scratch) = refs[npc:]
        start, finish = _chip_reduce_steps(sin_ref, relayed_ref, rin_ref, *exchange_scratch)

        @pl.when(pl.program_id(0) == 0)
        def _():
            start()
            lo = pltpu.make_async_copy(w_hbm.at[pl.ds(0, RANK_COL)], w_scr.at[pl.ds(0, RANK_COL)], sems.at[0])
            hi = pltpu.make_async_copy(w_hbm.at[pl.ds(RANK_COL + GLA_RANK, n_main - RANK_COL)],
                                       w_scr.at[pl.ds(RANK_COL, n_main - RANK_COL)], sems.at[1])
            lo.start()
            hi.start()
            dg_ref[...] = jnp.zeros_like(dg_ref)
            lo.wait()
            hi.wait()

        def w_group(g):
            return w_scr[g * 1024:(g + 1) * 1024, :]

        dr = dr_ref[...]
        dh = _dot(dr, wr_ref[...])
        for g in range(npc):
            dh = dh + _dot(pcs[g][...], w_group(g))
        dh = dh + _dot(dm_ref[:, :D], w_group(npc))
        dh = dh + _dot(dm_ref[:, D:], w_group(npc + 1))
        xv = x_ref[...]
        r = lax.rsqrt(jnp.mean(xv * xv, axis=-1, keepdims=True) + EPS)
        xhat = xv * r
        g = g_ref[...]
        dg_ref[...] += jnp.sum(dh * xhat, axis=0, keepdims=True)
        dxhat = dh * g
        gx_ref[...] = r * (dxhat - xhat * jnp.mean(dxhat * xhat, axis=-1, keepdims=True)) + dx2_ref[...]

        @pl.when(pl.program_id(0) == n_i - 1)
        def _():
            finish()

    row = lambda i: (i, 0)
    const = lambda i: (0, 0)
    tile = pl.BlockSpec((tm, D), row)
    part = s_in.shape[1:]
    return pl.pallas_call(
        body, name="dh",
        grid=(n_i,),
        in_specs=[tile] * npc + [
            pl.BlockSpec((tm, 2 * D), row),
            pl.BlockSpec((tm, 128), row),
            _ANY,
            pl.BlockSpec((128, D), const),
            tile, tile,
            pl.BlockSpec((1, D), const),
            _ANY, _ANY],
        out_specs=(tile, pl.BlockSpec((1, D), const), _ANY),
        out_shape=(jax.ShapeDtypeStruct((T, D), F32),
                   jax.ShapeDtypeStruct((1, D), F32),
                   jax.ShapeDtypeStruct((3,) + part, s_in.dtype)),
        scratch_shapes=[pltpu.VMEM((n_main, D), BF16), pltpu.SemaphoreType.DMA((2,))]
        + _chip_reduce_scratch(*part, s_in.dtype),
        compiler_params=_cparams(("arbitrary",)),
    )(*pieces, dmlog, drank, wt, wr, x, dx2, norm_g, s_in, relayed)


def _wgrad_rank_call(ht, drank):
    n_tb, D, tb = ht.shape

    def body(ht_ref, dr_ref, o_ref):
        @pl.when(pl.program_id(0) == 0)
        def _():
            o_ref[...] = jnp.zeros_like(o_ref)

        o_ref[...] += _dot(ht_ref[...], dr_ref[...])

    return pl.pallas_call(
        body, name="wgrad_rank",
        grid=(n_tb,),
        in_specs=[pl.BlockSpec((None, D, tb), lambda i: (i, 0, 0)),
                  pl.BlockSpec((tb, 128), lambda i: (i, 0))],
        out_specs=pl.BlockSpec((D, 128), lambda i: (0, 0)),
        out_shape=jax.ShapeDtypeStruct((D, 128), F32),
        compiler_params=_cparams(("arbitrary",)),
    )(ht, drank)


def _wgrad_call(lhs_list, lhs_of_group, rhs_list, rhs_of_group, n_transposed, name):
    n_groups = len(rhs_of_group)
    n_tb, D, tb = lhs_list[0].shape
    T = n_tb * tb
    per = min(4, n_tb)
    tk = per * tb
    nk = T // tk
    nl = len(lhs_list)

    def body(*refs):
        lhs = refs[:nl]
        rhs = refs[nl:nl + n_groups]
        out_ref, acc = refs[nl + n_groups:]
        g = pl.program_id(0)
        i = pl.program_id(1)

        @pl.when(i == 0)
        def _():
            acc[...] = jnp.zeros_like(acc)

        for p in range(n_groups):
            @pl.when(g == p)
            def _(p=p):
                lref = lhs[lhs_of_group[p]]
                part = _dot(lref[0], rhs[p][0:tb, :])
                for b in range(1, per):
                    part = part + _dot(lref[b], rhs[p][b * tb:(b + 1) * tb, :])
                acc[...] += part

        @pl.when((i == nk - 1) & (g < n_transposed))
        def _():
            out_ref[...] = _bf(acc[...].T)

        @pl.when((i == nk - 1) & (g >= n_transposed))
        def _():
            out_ref[...] = _bf(acc[...])

    def lhs_spec(a):
        groups = [g for g in range(n_groups) if lhs_of_group[g] == a]
        lo, hi = min(groups), max(groups)
        assert groups == list(range(lo, hi + 1))
        return pl.BlockSpec((per, D, tb), lambda g, i: (jnp.where((g >= lo) & (g <= hi), i, 0), 0, 0))

    def rhs_spec(p):
        cb = rhs_of_group[p][1]
        return pl.BlockSpec((tk, 1024), lambda g, i: (jnp.where(g == p, i, 0), cb))

    return pl.pallas_call(
        body, name=name,
        grid=(n_groups, nk),
        in_specs=[lhs_spec(a) for a in range(nl)] + [rhs_spec(p) for p in range(n_groups)],
        out_specs=pl.BlockSpec((None, D, 1024), lambda g, i: (g, 0, 0)),
        out_shape=jax.ShapeDtypeStruct((n_groups, D, 1024), BF16),
        scratch_shapes=[pltpu.VMEM((D, 1024), F32)],
        compiler_params=_cparams(("arbitrary", "arbitrary")),
    )(*lhs_list, *[rhs_list[rhs_of_group[p][0]] for p in range(n_groups)])


def _adamw_math(parts, w, m, v):
    g = parts[0].astype(F32)
    for p in parts[1:]:
        g = g + p.astype(F32)
    mm = ADAM_B1 * m + (1.0 - ADAM_B1) * g
    vv = ADAM_B2 * v + (1.0 - ADAM_B2) * (g * g)
    m_hat = mm / (1.0 - ADAM_B1 ** ADAM_STEP)
    v_hat = vv / (1.0 - ADAM_B2 ** ADAM_STEP)
    return g, -ADAM_LR * (m_hat / (jnp.sqrt(v_hat) + ADAM_EPS) + ADAM_WD * w), mm, vv


def _part_order(n_parts):
    return [n_parts - 1] + list(range(n_parts - 1))


def _adamw_call(parts, w, m, v, name):
    R, C = w.shape
    n_parts = parts.shape[0]
    (tr, tc), grid, idx = _tiling_2d(R, C, 512)

    def body(p_ref, w_ref, m_ref, v_ref, g_ref, d_ref, nm_ref, nv_ref):
        g_ref[...], d_ref[...], nm_ref[...], nv_ref[...] = _adamw_math(
            [p_ref[k] for k in _part_order(n_parts)], w_ref[...], m_ref[...], v_ref[...])

    blk = pl.BlockSpec((tr, tc), idx)
    sds = jax.ShapeDtypeStruct((R, C), F32)
    return pl.pallas_call(
        body, name=name,
        grid=grid,
        in_specs=[pl.BlockSpec((n_parts, tr, tc), lambda i: (0,) + idx(i)), blk, blk, blk],
        out_specs=(blk, blk, blk, blk),
        out_shape=(sds, sds, sds, sds),
        compiler_params=_cparams(("arbitrary",)),
    )(parts, w, m, v)


def _adamw_rows_call(parts, ws, ms, vs, name, gathered):
    n = len(ws)
    R, C = ws[0].shape
    n_parts = parts.shape[0]

    def body(*refs):
        p_ref = refs[0]
        w_refs, m_refs, v_refs = refs[1:1 + n], refs[1 + n:1 + 2 * n], refs[1 + 2 * n:1 + 3 * n]
        src_ref = refs[1 + 3 * n]
        outs = refs[2 + 3 * n:2 + 7 * n]
        dst_ref, send_sems, recv_sems, loc_sem = refs[2 + 7 * n:]
        own, pairs = _push_copies(src_ref, dst_ref, send_sems, recv_sems, loc_sem, scatter=False)
        k_now = pl.program_id(0)

        @pl.when(k_now == 0)
        def _():
            _push_start(own, pairs)

        for k in range(n):
            @pl.when(k_now == k)
            def _(k=k):
                res = _adamw_math([p_ref[j] for j in _part_order(n_parts)],
                                  w_refs[k][...], m_refs[k][...], v_refs[k][...])
                for o_ref, val in zip(outs[4 * k:4 * k + 4], res):
                    o_ref[...] = val

        @pl.when(k_now == n - 1)
        def _():
            _push_wait(own, pairs)

    whole = pl.BlockSpec((R, C), lambda k: (0, 0))
    sds = jax.ShapeDtypeStruct((R, C), F32)
    res = pl.pallas_call(
        body, name=name,
        grid=(n,),
        in_specs=[pl.BlockSpec((n_parts, R, C), lambda k: (0, k, 0))] + [whole] * (3 * n) + [_ANY],
        out_specs=tuple([whole] * (4 * n) + [_ANY]),
        out_shape=tuple([sds] * (4 * n) + [jax.ShapeDtypeStruct((N_DEV,) + gathered.shape, gathered.dtype)]),
        scratch_shapes=_PUSH_SEMS,
        compiler_params=_cparams(("arbitrary",)),
    )(parts, *ws, *ms, *vs, gathered)
    return [res[4 * k:4 * k + 4] for k in range(n)], res[4 * n]


def _adamw_lanes_call(parts, offsets, ws, ms, vs, name):
    n = len(ws)
    n_parts = parts.shape[0]

    def body(*refs):
        p_ref = refs[0]
        w_refs, m_refs, v_refs = refs[1:1 + n], refs[1 + n:1 + 2 * n], refs[1 + 2 * n:1 + 3 * n]
        outs = refs[1 + 3 * n:]
        for k in range(n):
            lanes = slice(offsets[k], offsets[k] + ws[k].shape[1])
            res = _adamw_math([p_ref[j, :, lanes] for j in _part_order(n_parts)],
                              w_refs[k][...], m_refs[k][...], v_refs[k][...])
            for o_ref, val in zip(outs[4 * k:4 * k + 4], res):
                o_ref[...] = val

    res = pl.pallas_call(
        body, name=name,
        out_shape=tuple(jax.ShapeDtypeStruct(ws[k].shape, F32) for k in range(n) for _ in range(4)),
        compiler_params=_cparams(),
    )(parts, *ws, *ms, *vs)
    return [res[4 * k:4 * k + 4] for k in range(n)]


def _local_step(x, target, wt, wr, wdec, bdec, wp_shard, norm_g, gla_g, b_gate, final_g):
    D = x.shape[1]
    half = wp_shard.shape[1] // 2
    projf, projb, rank, ht, wp_lo = _proj_call(x, norm_g, wt, wr, wp_shard[:, :half])
    o_gla, st_all, la = _gla_fwd_call(projf, projb, rank, wdec, bdec)
    o_sb, wp_hi = _sb_fwd_call(projb, wp_shard[:, half:])
    wp_full = jnp.concatenate([wp_lo, wp_hi], axis=2).transpose(1, 0, 2, 3).reshape(3, D, D)
    (dx2, do_gla, do_sb, dggate, dsgate, dmlog, mt, ogt, obt, dx2b, dya, dyb,
     dfinal_g, db_gate, dgla_g, loss) = _mid_call(o_gla, o_sb, projf, x, target, wp_full[0], wp_full[1],
                                                 wp_full[2], gla_g, b_gate, final_g)
    dw_p = _wgrad_call([ogt, obt, mt], [0, 1, 2], [dya, dyb, dx2b], [(0, 0), (1, 0), (2, 0)], 0, "wgrad_p")
    g_p = dw_p.reshape(3, N_DEV, D // N_DEV, D).transpose(1, 0, 2, 3).reshape(N_DEV, 3 * (D // N_DEV), D)
    dqk, dgv, drank, dwdec, dbdec = _gla_bwd_call(projf, projb, la, do_gla, st_all, rank, wdec)
    dsq, dsk, dsv, r_p = _sb_bwd_call(projb, do_sb, g_p)
    pieces = [dqk, dgv, dggate, dsq, dsk, dsv, dsgate]
    rhs_of_group = [(g, 0) for g in range(7)] + [(7, 0), (7, 1)]
    dw_in = _wgrad_call([ht], [0] * N_GROUPS, pieces + [dmlog], rhs_of_group, N_GROUPS, "wgrad_in")
    dwr = _wgrad_rank_call(ht, drank)
    g_in = _parts_by_device_call(dw_in.reshape(N_GROUPS * 1024, D), dwr[:, :GLA_RANK].T.astype(BF16))
    (p_in,) = _pair_exchange([g_in], "pair_g")
    s_in, relayed = _pair_add_call(g_in, p_in, "pair_add_in")
    grad_x, dnorm_g, r_in = _dh_call(pieces, dmlog, drank, wt, wr, x, dx2, norm_g, s_in, relayed)
    small = jnp.concatenate([
        dnorm_g.reshape(-1), dbdec.reshape(-1), dgla_g.reshape(-1), db_gate.reshape(-1), dfinal_g.reshape(-1),
        loss.reshape(-1), dwdec[:GLA_RANK].reshape(-1)]).reshape(1, _SM_LEN)
    return grad_x, r_in, r_p, small


_SM_NORM = 0
_SM_BDEC = _SM_NORM + D_MODEL
_SM_GLAG = _SM_BDEC + GLA_DK
_SM_BGATE = _SM_GLAG + GLA_HV
_SM_FINAL = _SM_BGATE + 2 * D_MODEL
_SM_REPL = _SM_FINAL + D_MODEL
_SM_LOSS = _SM_REPL
_SM_WDEC = _SM_LOSS + 128
_SM_LEN = _SM_WDEC + GLA_RANK * GLA_DK


def kernel(x, norm_g, w_in, w_dec_up, b_dec, gla_norm_g, w_pa, w_pb, b_gate, w_o, final_g, loss_target, m_norm_g, m_w_in, m_w_dec_up, m_b_dec, m_gla_norm_g, m_w_pa, m_w_pb, m_b_gate, m_w_o, m_final_g, v_norm_g, v_w_in, v_w_dec_up, v_b_dec, v_gla_norm_g, v_w_pa, v_w_pb, v_b_gate, v_w_o, v_final_g):
    D = D_MODEL
    me = 4 * lax.axis_index("x") + 2 * lax.axis_index("y") + lax.axis_index("c")

    wp_shard = jnp.stack([w_pa, w_pb, w_o]).astype(BF16)
    n_first = _half_rows(SHARD_COLS)
    win_all, wdec_all = _all_gather([w_in.T.astype(BF16), w_dec_up], "gather_w",
                                    row_pieces=[[(0, n_first), (n_first, SHARD_COLS - n_first)], None])
    wt = _flatten_blocks_call(win_all)
    wr = jnp.pad(wt[RANK_COL:RANK_COL + GLA_RANK], ((0, 128 - GLA_RANK), (0, 0)))
    wdec_full = wdec_all.transpose(1, 0, 2).reshape(GLA_RANK, GLA_DK)
    wdec = jnp.pad(wdec_full, ((0, 128 - GLA_RANK), (0, 0)))

    grad_x, r_in, r_p, small = _local_step(
        x[0], loss_target[0], wt, wr, wdec, b_dec.reshape(1, -1), wp_shard,
        norm_g.reshape(1, -1), gla_norm_g.reshape(1, -1), b_gate.reshape(1, -1), final_g.reshape(1, -1))

    gw_in, d_in, nm_in, nv_in = (a.T for a in _adamw_call(r_in, w_in.T, m_w_in.T, v_w_in.T, "adamw_in"))
    ((g_pa, d_pa, nm_pa, nv_pa), (g_pb, d_pb, nm_pb, nv_pb), (g_o, d_o, nm_o, nv_o)), r_small = _adamw_rows_call(
        r_p, [w_pa, w_pb, w_o], [m_w_pa, m_w_pb, m_w_o], [v_w_pa, v_w_pb, v_w_o], "adamw_p", small)

    def row(a):
        return a.reshape(1, -1)

    rep = _adamw_lanes_call(
        r_small, [_SM_NORM, _SM_BDEC, _SM_GLAG, _SM_BGATE, _SM_FINAL],
        [row(a) for a in (norm_g, b_dec, gla_norm_g, b_gate, final_g)],
        [row(a) for a in (m_norm_g, m_b_dec, m_gla_norm_g, m_b_gate, m_final_g)],
        [row(a) for a in (v_norm_g, v_b_dec, v_gla_norm_g, v_b_gate, v_final_g)], "adamw_rep")
    ((g_norm, d_norm, nm_norm, nv_norm), (g_bdec, d_bdec, nm_bdec, nv_bdec), (g_glag, d_glag, nm_glag, nv_glag),
     (g_bgate, d_bgate, nm_bgate, nv_bgate), (g_final, d_final, nm_final, nv_final)) = [
        tuple(a.reshape(-1) for a in quad) for quad in rep]

    wdec_parts = r_small[:, 0, _SM_WDEC:].reshape(N_DEV, GLA_RANK, GLA_DK)
    cols = GLA_DK // N_DEV
    wdec_mine = lax.dynamic_slice_in_dim(wdec_parts, me * cols, cols, axis=2)
    g_wdec, d_wdec, nm_wdec, nv_wdec = _adamw_call(wdec_mine, w_dec_up, m_w_dec_up, v_w_dec_up, "adamw_dec")

    loss_total = jnp.sum(r_small[:, 0, _SM_LOSS])

    return (loss_total, grad_x[None],
            g_norm, gw_in, g_wdec, g_bdec, g_glag, g_pa, g_pb, g_bgate, g_o, g_final,
            d_norm, d_in, d_wdec, d_bdec, d_glag, d_pa, d_pb, d_bgate, d_o, d_final,
            nm_norm, nm_in, nm_wdec, nm_bdec, nm_glag, nm_pa, nm_pb, nm_bgate, nm_o, nm_final,
            nv_norm, nv_in, nv_wdec, nv_bdec, nv_glag, nv_pa, nv_pb, nv_bgate, nv_o, nv_final)
```

```python
import math

import jax
import jax.numpy as jnp
from jax import lax
from jax.experimental import pallas as pl
from jax.experimental.pallas import tpu as pltpu

F32 = jnp.float32
BF16 = jnp.bfloat16

N_DEV = 8
D_MODEL = 1024
GLA_HEADS = 4
GLA_HK = 128
GLA_HV = 256
GLA_DK = 512
GLA_RANK = 16
GLA_TAU = 16.0
GLA_CHUNK = 64
SB_HEADS = 8
SB_HD = 128
EPS = 1e-6
N_GROUPS = 9
RANK_COL = 3072
IN_COLS = 9232
SHARD_COLS = IN_COLS // N_DEV

ADAM_LR = 0.001
ADAM_B1 = 0.9
ADAM_B2 = 0.999
ADAM_EPS = 1e-08
ADAM_WD = 0.01
ADAM_STEP = 10

VMEM_LIMIT = 56 * 1024 * 1024
TBLK = 256


def _cparams(sem=None):
    return pltpu.CompilerParams(dimension_semantics=sem, vmem_limit_bytes=VMEM_LIMIT)


def _tiling_2d(rows, cols, band_cols):
    if rows * cols <= 128 * 1024:
        return (rows, cols), (1,), lambda i: (0, 0)
    if rows % 128 == 0:
        return (128, cols), (rows // 128,), lambda i: (i, 0)
    tc = band_cols if cols % band_cols == 0 else cols
    return (rows, tc), (cols // tc,), lambda i: (0, i)


def _dot(a, b):
    return jnp.dot(a, b, preferred_element_type=F32)


def _dot_nt(a, b):
    return lax.dot_general(a, b, (((1,), (1,)), ((), ())), preferred_element_type=F32)


def _dot_tn(a, b):
    return lax.dot_general(a, b, (((0,), (0,)), ((), ())), preferred_element_type=F32)


def _bf(x):
    return x.astype(BF16)


def _split3(x):
    hi = x.astype(BF16)
    r = x - hi.astype(F32)
    mid = r.astype(BF16)
    lo = (r - mid.astype(F32)).astype(BF16)
    return hi, mid, lo


def _tri_left(tri, x):
    hi, mid, lo = _split3(x)
    return _dot(tri, hi) + _dot(tri, mid) + _dot(tri, lo)


def _split2(x):
    hi = lax.bitcast_convert_type(lax.bitcast_convert_type(x, jnp.uint32) & jnp.uint32(0xFFFF0000), F32)
    return hi.astype(BF16), (x - hi).astype(BF16)


def _tri2_left(tri, x):
    hi, lo = _split2(x)
    return _dot(tri, hi) + _dot(tri, lo)


def _tri2_right(x, tri):
    hi, lo = _split2(x)
    return _dot(hi, tri) + _dot(lo, tri)


def _iota2(n, m, dim):
    return lax.broadcasted_iota(jnp.int32, (n, m), dim)


def _sigmoid(x):
    return 1.0 / (1.0 + jnp.exp(-x))


def _softplus_neg_abs(z):
    return jnp.log(1.0 + jnp.exp(-jnp.abs(z)))


_ANY = pl.BlockSpec(memory_space=pl.ANY)


def _mesh_pos():
    return lax.axis_index("x"), lax.axis_index("y"), lax.axis_index("c")


def _other_chips(x, y):
    return [(1 - x, y), (x, 1 - y), (1 - x, 1 - y)]


def _rcopy(src, dst, send_sem, recv_sem, to):
    return pltpu.make_async_remote_copy(src_ref=src, dst_ref=dst, send_sem=send_sem, recv_sem=recv_sem,
                                        device_id=to, device_id_type=pl.DeviceIdType.MESH)


def _push_copies(src_ref, dst_ref, send_sems, recv_sems, loc_sem, scatter):
    x, y, c = _mesh_pos()
    me = 4 * x + 2 * y + c
    own = pltpu.make_async_copy(src_ref.at[me] if scatter else src_ref, dst_ref.at[me], loc_sem)
    pairs = []
    for k in range(1, N_DEV):
        px = 1 - x if k & 4 else x
        py = 1 - y if k & 2 else y
        pc = 1 - c if k & 1 else c
        pid = 4 * px + 2 * py + pc
        src = src_ref.at[pid] if scatter else src_ref
        send = _rcopy(src, dst_ref.at[me], send_sems.at[k - 1], recv_sems.at[k - 1], (px, py, pc))
        recv = _rcopy(src, dst_ref.at[pid], send_sems.at[k - 1], recv_sems.at[k - 1], (px, py, pc))
        pairs.append((send, recv))
    return own, pairs


def _push_start(own, pairs):
    own.start()
    for send, _ in pairs:
        send.start()


def _push_wait(own, pairs):
    for _, recv in pairs:
        recv.wait_recv()
    for send, _ in pairs:
        send.wait_send()
    own.wait()


_PUSH_SEMS = [pltpu.SemaphoreType.DMA((N_DEV - 1,)), pltpu.SemaphoreType.DMA((N_DEV - 1,)),
              pltpu.SemaphoreType.DMA]


def _half_rows(rows):
    return (rows // 2) // 16 * 16


_ADD_ROWS = 128


def _chip_reduce_steps(src_ref, dst_ref, relayed_ref, sum_x, sum_y, rel_x, rel_y, load_sems, send_sems, recv_sems,
                       loc_sem):
    _, R, C = src_ref.shape
    n0 = _half_rows(R)
    lo, hi = pl.ds(0, n0), pl.ds(n0, R - n0)
    x, y, c = _mesh_pos()
    (xx, xy), (yx, yy), (dx, dy) = _other_chips(x, y)
    to_diag, to_x, to_y = src_ref.at[2 * dx + dy], src_ref.at[2 * xx + xy], src_ref.at[2 * yx + yy]
    x_nb, y_nb = (xx, xy, c), (yx, yy, c)
    relays = (_rcopy(to_diag.at[lo], relayed_ref.at[lo], send_sems.at[0], recv_sems.at[0], x_nb),
              _rcopy(to_diag.at[hi], relayed_ref.at[hi], send_sems.at[1], recv_sems.at[1], y_nb))
    plain = (_rcopy(to_x.at[lo], dst_ref.at[0, lo], send_sems.at[2], recv_sems.at[2], x_nb),
             _rcopy(to_y.at[hi], dst_ref.at[1, hi], send_sems.at[3], recv_sems.at[3], y_nb))
    summed = (_rcopy(sum_x, dst_ref.at[0, hi], send_sems.at[4], recv_sems.at[4], x_nb),
              _rcopy(sum_y, dst_ref.at[1, lo], send_sems.at[5], recv_sems.at[5], y_nb))
    load_mine = (pltpu.make_async_copy(to_x.at[hi], sum_x, load_sems.at[0]),
                 pltpu.make_async_copy(to_y.at[lo], sum_y, load_sems.at[1]))
    load_relayed = (pltpu.make_async_copy(relayed_ref.at[hi], rel_x, load_sems.at[2]),
                    pltpu.make_async_copy(relayed_ref.at[lo], rel_y, load_sems.at[3]))
    own = pltpu.make_async_copy(src_ref.at[2 * x + y], dst_ref.at[2], loc_sem)

    def start():
        for cp in relays + plain + (own,) + load_mine:
            cp.start()

    def add(acc_ref, rel_ref):
        for r0 in range(0, acc_ref.shape[0], _ADD_ROWS):
            rows = slice(r0, min(r0 + _ADD_ROWS, acc_ref.shape[0]))
            acc_ref[rows, :] = (acc_ref[rows, :].astype(F32) + rel_ref[rows, :].astype(F32)).astype(acc_ref.dtype)

    def forward():
        for cp in relays:
            cp.wait_recv()
        for cp in load_relayed:
            cp.start()
        for cp in load_mine + load_relayed:
            cp.wait()
        add(sum_x, rel_x)
        add(sum_y, rel_y)
        for cp in summed:
            cp.start()

    def finish():
        for cp in plain + summed:
            cp.wait_recv()
        for cp in relays + plain + summed:
            cp.wait_send()
        own.wait()

    return start, forward, finish


def _chip_reduce_scratch(rows, cols, dtype):
    n0 = _half_rows(rows)
    return [pltpu.VMEM((rows - n0, cols), dtype), pltpu.VMEM((n0, cols), dtype)] * 2 + [
        pltpu.SemaphoreType.DMA((4,)), pltpu.SemaphoreType.DMA((6,)), pltpu.SemaphoreType.DMA((6,)),
        pltpu.SemaphoreType.DMA]


def _all_gather(arrs, name, row_pieces=None):
    n = len(arrs)
    pieces = [[None] if not row_pieces or not row_pieces[a] else list(row_pieces[a]) for a in range(n)]
    assert all(len(p) in (1, 2) for p in pieces)
    units = [(a, i) for a in range(n) for i in range(len(pieces[a]))]

    def body(*refs):
        ins = refs[:n]
        outs = refs[n:2 * n]
        send_sems, recv_sems, loc_sems = refs[2 * n:]
        x, y, c = _mesh_pos()
        me, sib = (x, y, c), (x, y, 1 - c)
        xn, yn, dg = [(px, py, c) for px, py in _other_chips(x, y)]

        def rows(ref, a, i):
            return ref if pieces[a][i] is None else ref.at[pl.ds(*pieces[a][i])]

        def copy(u, k, block, to, own=False):
            a, i = u
            px, py, pc = block
            dst = rows(outs[a].at[4 * px + 2 * py + pc], a, i)
            return _rcopy(rows(ins[a], a, i) if own else dst, dst, send_sems.at[a, k, i], recv_sems.at[a, k, i], to)

        started = []

        def start(cp):
            cp.start()
            started.append(cp)

        def landed_then_pass_on(u, k, block):
            copy(u, k, block, me).wait_recv()
            start(copy(u, 3 + k, block, sib))

        mine = [pltpu.make_async_copy(ins[a], outs[a].at[4 * x + 2 * y + c], loc_sems.at[a]) for a in range(n)]
        for cp in mine:
            cp.start()
        for u in units:
            start(copy(u, 0, me, sib, own=True))
        for a in range(n):
            if len(pieces[a]) == 2:
                for i, to, k in ((0, xn, 1), (1, yn, 2), (1, xn, 1), (0, yn, 2)):
                    start(copy((a, i), k, me, to, own=True))
            else:
                for to, k in ((xn, 1), (yn, 2), (dg, 3)):
                    start(copy((a, 0), k, me, to, own=True))
        for a in range(n):
            if len(pieces[a]) == 2:
                landed_then_pass_on((a, 0), 1, xn)
                start(copy((a, 0), 3, xn, yn))
                landed_then_pass_on((a, 1), 2, yn)
                start(copy((a, 1), 3, yn, xn))
                landed_then_pass_on((a, 1), 1, xn)
                landed_then_pass_on((a, 0), 2, yn)
                landed_then_pass_on((a, 0), 3, dg)
                landed_then_pass_on((a, 1), 3, dg)
            else:
                for block, k in ((xn, 1), (yn, 2), (dg, 3)):
                    landed_then_pass_on((a, 0), k, block)
        for u in units:
            copy(u, 0, sib, me).wait_recv()
            for k, (px, py, _) in ((4, xn), (5, yn), (6, dg)):
                copy(u, k, (px, py, 1 - c), me).wait_recv()
        for cp in started:
            cp.wait_send()
        for cp in mine:
            cp.wait()

    n_pc = max(len(p) for p in pieces)

    return pl.pallas_call(
        body, name=name,
        out_shape=tuple(jax.ShapeDtypeStruct((N_DEV,) + a.shape, a.dtype) for a in arrs),
        in_specs=[_ANY] * n,
        out_specs=tuple([_ANY] * n),
        scratch_shapes=[pltpu.SemaphoreType.DMA((n, 7, n_pc)), pltpu.SemaphoreType.DMA((n, 7, n_pc)),
                        pltpu.SemaphoreType.DMA((n,))],
    )(*arrs)


def _pair_exchange(arrs, name):
    n = len(arrs)

    def body(*refs):
        ins = refs[:n]
        outs = refs[n:2 * n]
        send_sems, recv_sems = refs[2 * n:]
        x, y, c = _mesh_pos()
        copies = []
        for a in range(n):
            for q in range(4):
                cp = _rcopy(ins[a].at[2 * q + (1 - c)], outs[a].at[q], send_sems.at[a, q], recv_sems.at[a, q],
                            (x, y, 1 - c))
                cp.start()
                copies.append(cp)
        for cp in copies:
            cp.wait_recv()
        for cp in copies:
            cp.wait_send()

    return pl.pallas_call(
        body, name=name,
        out_shape=tuple(jax.ShapeDtypeStruct((4,) + a.shape[1:], a.dtype) for a in arrs),
        in_specs=[_ANY] * n,
        out_specs=tuple([_ANY] * n),
        scratch_shapes=[pltpu.SemaphoreType.DMA((n, 4)), pltpu.SemaphoreType.DMA((n, 4))],
    )(*arrs)


def _pair_add_call(parts, recv, c_idx, name):
    _, R, C = parts.shape
    (tr, tc), (steps,), idx = _tiling_2d(R, C, 1024)

    def body(c_ref, p_ref, r_ref, o_ref):
        o_ref[...] = (p_ref[...].astype(F32) + r_ref[...].astype(F32)).astype(o_ref.dtype)

    return pl.pallas_call(
        body, name=name,
        grid_spec=pltpu.PrefetchScalarGridSpec(
            num_scalar_prefetch=1,
            grid=(4, steps),
            in_specs=[pl.BlockSpec((None, tr, tc), lambda q, i, c_ref: (2 * q + c_ref[0],) + idx(i)),
                      pl.BlockSpec((None, tr, tc), lambda q, i, c_ref: (q,) + idx(i))],
            out_specs=pl.BlockSpec((None, tr, tc), lambda q, i, c_ref: (q,) + idx(i))),
        out_shape=jax.ShapeDtypeStruct((4, R, C), parts.dtype),
        compiler_params=_cparams(("arbitrary", "arbitrary")),
    )(c_idx, parts, recv)


def _flatten_blocks_call(blocks):
    n, R, C = blocks.shape
    tc = C // 2

    def body(in_ref, out_ref):
        for p in range(n):
            out_ref[p * R:(p + 1) * R, :] = in_ref[p]

    return pl.pallas_call(
        body, name="flatten_w",
        grid=(C // tc,),
        in_specs=[pl.BlockSpec((n, R, tc), lambda i: (0, 0, i))],
        out_specs=pl.BlockSpec((n * R, tc), lambda i: (0, i)),
        out_shape=jax.ShapeDtypeStruct((n * R, C), blocks.dtype),
        compiler_params=_cparams(("arbitrary",)),
    )(blocks)


def _parts_by_device_call(dmain, drank):
    D = dmain.shape[1]
    tc = D // 2

    def body(dm_ref, dr_ref, out_ref):
        for p in range(N_DEV):
            lo, hi = p * SHARD_COLS, (p + 1) * SHARD_COLS
            at = 0
            for src, a, b in ((dm_ref, lo, min(hi, RANK_COL)),
                              (dr_ref, max(lo, RANK_COL) - RANK_COL, min(hi, RANK_COL + GLA_RANK) - RANK_COL),
                              (dm_ref, max(lo, RANK_COL + GLA_RANK) - GLA_RANK, hi - GLA_RANK)):
                if b > a:
                    out_ref[p, at:at + (b - a), :] = src[a:b, :]
                    at += b - a

    return pl.pallas_call(
        body, name="parts_by_device",
        grid=(D // tc,),
        in_specs=[pl.BlockSpec((dmain.shape[0], tc), lambda i: (0, i)),
                  pl.BlockSpec((GLA_RANK, tc), lambda i: (0, i))],
        out_specs=pl.BlockSpec((N_DEV, SHARD_COLS, tc), lambda i: (0, 0, i)),
        out_shape=jax.ShapeDtypeStruct((N_DEV, SHARD_COLS, D), dmain.dtype),
        compiler_params=_cparams(("arbitrary",)),
    )(dmain, drank)


def _group_row(g):
    return GLA_RANK * (g * (1024 // GLA_RANK) + (g >= RANK_COL // 1024))


def _proj_call(x, norm_g, wt, wr, wp_part):
    T, D = x.shape
    tm = min(1024, T)
    assert tm % TBLK == 0
    n_i = T // tm

    def f_slot(j):
        return ((j >= 2).astype(jnp.int32) + (j >= 6).astype(jnp.int32)
                + (j >= 7).astype(jnp.int32) + (j >= 8).astype(jnp.int32))

    def b_slot(j):
        return (j >= 3).astype(jnp.int32) + (j >= 4).astype(jnp.int32) + (j >= 5).astype(jnp.int32)

    def body(x_ref, g_ref, w_ref, wr_ref, wp_ref, pf_ref, pb_ref, rank_ref, ht_ref, wpall_ref,
             h_scr, send_sems, recv_sems, loc_sem):
        i = pl.program_id(0)
        j = pl.program_id(1)
        own, pairs = _push_copies(wp_ref, wpall_ref, send_sems, recv_sems, loc_sem, scatter=False)

        @pl.when((i == 0) & (j == 0))
        def _():
            _push_start(own, pairs)

        @pl.when(j == 0)
        def _():
            xv = x_ref[...]
            r = lax.rsqrt(jnp.mean(xv * xv, axis=-1, keepdims=True) + EPS)
            h = (xv * r) * g_ref[...]
            hb = _bf(h)
            h_scr[...] = hb
            for b in range(tm // TBLK):
                ht_ref[b] = _bf(h[b * TBLK:(b + 1) * TBLK].T)
            rank_ref[...] = _dot_nt(hb, wr_ref[...])

        is_b = (j == 1) | ((j >= 3) & (j <= 5))

        @pl.when(is_b)
        def _():
            pb_ref[...] = _bf(_dot_nt(h_scr[...], w_ref[...]))

        @pl.when(jnp.logical_not(is_b))
        def _():
            pf_ref[...] = _dot_nt(h_scr[...], w_ref[...])

        @pl.when((i == n_i - 1) & (j == N_GROUPS - 1))
        def _():
            _push_wait(own, pairs)

    return pl.pallas_call(
        body, name="proj",
        grid=(n_i, N_GROUPS),
        in_specs=[pl.BlockSpec((tm, D), lambda i, j: (i, 0)),
                  pl.BlockSpec((1, D), lambda i, j: (0, 0)),
                  pl.BlockSpec((pl.Element(1024), pl.Element(D)), lambda i, j: (_group_row(j), 0)),
                  pl.BlockSpec((128, D), lambda i, j: (0, 0)),
                  _ANY],
        out_specs=(pl.BlockSpec((None, tm, 1024), lambda i, j: (f_slot(j), i, 0)),
                   pl.BlockSpec((None, tm, 1024), lambda i, j: (b_slot(j), i, 0)),
                   pl.BlockSpec((tm, 128), lambda i, j: (i, 0)),
                   pl.BlockSpec((tm // TBLK, D, TBLK), lambda i, j: (i, 0, 0)),
                   _ANY),
        out_shape=(jax.ShapeDtypeStruct((5, T, 1024), F32),
                   jax.ShapeDtypeStruct((4, T, 1024), BF16),
                   jax.ShapeDtypeStruct((T, 128), F32),
                   jax.ShapeDtypeStruct((T // TBLK, D, TBLK), BF16),
                   jax.ShapeDtypeStruct((N_DEV,) + wp_part.shape, wp_part.dtype)),
        scratch_shapes=[pltpu.VMEM((tm, D), BF16)] + _PUSH_SEMS,
        compiler_params=_cparams(("arbitrary", "arbitrary")),
    )(x, norm_g, wt, wr, wp_part)


GLA_STEP_CHUNKS = 4


def _gla_same_chunk(rows):
    return (_iota2(rows, rows, 0) & -GLA_CHUNK) == (_iota2(rows, rows, 1) & -GLA_CHUNK)


def _gla_chunk_terms(la, q, k, n_c):
    C = GLA_CHUNK
    rows = n_c * C
    low = _gla_same_chunk(rows) & (_iota2(rows, rows, 0) >= _iota2(rows, rows, 1))
    b = _tri_left(_bf(low.astype(F32)), la)
    bl = [b[(c + 1) * C - 1:(c + 1) * C, :] for c in range(n_c)]
    bl_rows = jnp.concatenate([jnp.broadcast_to(bl[c], (C, b.shape[1])) for c in range(n_c)], axis=0)
    eb = jnp.exp(b)
    enb = jnp.exp(-b)
    ebl_b = jnp.exp(bl_rows - b)
    scale = GLA_HK ** -0.5
    qe = q * eb * scale
    ke = k * enb
    kd = k * ebl_b
    return bl, eb, enb, ebl_b, qe, ke, kd


def _gla_fwd_call(projf, projb, rank, wdec, bdec):
    T = projf.shape[1]
    C = GLA_CHUNK
    n_chunks = T // C
    n_c = GLA_STEP_CHUNKS
    R = n_c * C
    assert n_chunks % n_c == 0

    def body(qk_ref, v_ref, rank_ref, wd_ref, bd_ref, o_ref, st_ref, la_ref, st_scr):
        @pl.when(pl.program_id(0) == 0)
        def _():
            st_scr[...] = jnp.zeros_like(st_scr)

        dec = _dot(_bf(rank_ref[...]), _bf(wd_ref[...])) + bd_ref[...]
        la = (jnp.minimum(dec, 0.0) - _softplus_neg_abs(dec)) / GLA_TAU
        la_ref[...] = la
        mask = _gla_same_chunk(R) & (_iota2(R, R, 0) >= _iota2(R, R, 1))
        bl, _, _, _, qe, ke, kd = _gla_chunk_terms(la, qk_ref[:, :GLA_DK], qk_ref[:, GLA_DK:], n_c)
        qeb, keb, kdb = _bf(qe), _bf(ke), _bf(kd)
        ebl = [jnp.exp(bl[c]) for c in range(n_c)]
        heads = range(GLA_HEADS)
        ks = [slice(hh * GLA_HK, (hh + 1) * GLA_HK) for hh in heads]
        vs = [slice(hh * GLA_HV, (hh + 1) * GLA_HV) for hh in heads]
        rs = [slice(c * C, (c + 1) * C) for c in range(n_c)]
        p = [_bf(jnp.where(mask, _dot_nt(qeb[:, ks[hh]], keb[:, ks[hh]]), 0.0)) for hh in heads]
        upd = [[_dot_tn(v_ref[rs[c], vs[hh]], kdb[rs[c], ks[hh]]) for hh in heads] for c in range(n_c)]
        intra = [_dot(p[hh], v_ref[:, vs[hh]]) for hh in heads]
        st = [st_scr[hh] for hh in heads]
        for c in range(n_c):
            inter = [_dot_nt(qeb[rs[c], ks[hh]], _bf(st[hh])) for hh in heads]
            for hh in heads:
                st_ref[c, hh] = st[hh]
                o_ref[rs[c], vs[hh]] = intra[hh][rs[c]] + inter[hh]
            st = [st[hh] * ebl[c][:, ks[hh]] + upd[c][hh] for hh in heads]
        for hh in heads:
            st_scr[hh] = st[hh]

    return pl.pallas_call(
        body, name="gla_fwd",
        grid=(n_chunks // n_c,),
        in_specs=[pl.BlockSpec((None, R, 1024), lambda n: (0, n, 0)),
                  pl.BlockSpec((None, R, 1024), lambda n: (0, n, 0)),
                  pl.BlockSpec((R, 128), lambda n: (n, 0)),
                  pl.BlockSpec((128, GLA_DK), lambda n: (0, 0)),
                  pl.BlockSpec((1, GLA_DK), lambda n: (0, 0))],
        out_specs=(pl.BlockSpec((R, 1024), lambda n: (n, 0)),
                   pl.BlockSpec((n_c, GLA_HEADS, GLA_HV, GLA_HK), lambda n: (n, 0, 0, 0)),
                   pl.BlockSpec((R, GLA_DK), lambda n: (n, 0))),
        out_shape=(jax.ShapeDtypeStruct((T, 1024), F32),
                   jax.ShapeDtypeStruct((n_chunks, GLA_HEADS, GLA_HV, GLA_HK), F32),
                   jax.ShapeDtypeStruct((T, GLA_DK), F32)),
        scratch_shapes=[pltpu.VMEM((GLA_HEADS, GLA_HV, GLA_HK), F32)],
        compiler_params=_cparams(("arbitrary",)),
    )(projf, projb, rank, wdec, bdec)


def _gla_bwd_call(projf, projb, la, do_gla, st_all, rank, wdec):
    T = projf.shape[1]
    C = GLA_CHUNK
    n_chunks = T // C
    n_c = GLA_STEP_CHUNKS
    R = n_c * C
    assert n_chunks % n_c == 0
    last = n_chunks // n_c - 1

    def body(qk_ref, v_ref, la_ref, do_ref, st_ref, rank_ref, wd_ref,
             dqk_ref, dv_ref, drank_ref, dwd_ref, dbd_ref, dst_scr):
        @pl.when(pl.program_id(0) == 0)
        def _():
            dst_scr[...] = jnp.zeros_like(dst_scr)
            dwd_ref[...] = jnp.zeros_like(dwd_ref)
            dbd_ref[...] = jnp.zeros_like(dbd_ref)

        same = _gla_same_chunk(R)
        mask = same & (_iota2(R, R, 0) >= _iota2(R, R, 1))
        upp = _bf((same & (_iota2(R, R, 0) <= _iota2(R, R, 1))).astype(F32))
        scale = GLA_HK ** -0.5
        la = la_ref[...]
        bl, eb, enb, ebl_b, qe, ke, kd = _gla_chunk_terms(la, qk_ref[:, :GLA_DK], qk_ref[:, GLA_DK:], n_c)
        qeb, keb, kdb = _bf(qe), _bf(ke), _bf(kd)
        ebl = [jnp.exp(bl[c]) for c in range(n_c)]
        heads = range(GLA_HEADS)
        ks = [slice(hh * GLA_HK, (hh + 1) * GLA_HK) for hh in heads]
        vs = [slice(hh * GLA_HV, (hh + 1) * GLA_HV) for hh in heads]
        rs = [slice(c * C, (c + 1) * C) for c in range(n_c)]
        v = [v_ref[:, vs[hh]] for hh in heads]
        do = [_bf(do_ref[:, vs[hh]]) for hh in heads]
        p = [_bf(jnp.where(mask, _dot_nt(qeb[:, ks[hh]], keb[:, ks[hh]]), 0.0)) for hh in heads]
        dp = [_bf(jnp.where(mask, _dot_nt(do[hh], v[hh]), 0.0)) for hh in heads]
        dst_intra = [[_dot_tn(do[hh][rs[c]], qeb[rs[c], ks[hh]]) for hh in heads] for c in range(n_c)]
        dqe_inter = [[_dot(do[hh][rs[c]], _bf(st_ref[c, hh])) for hh in heads] for c in range(n_c)]
        dv_intra = [_dot_tn(p[hh], do[hh]) for hh in heads]
        dqe_intra = [_dot(dp[hh], keb[:, ks[hh]]) for hh in heads]
        dke = jnp.concatenate([_dot_tn(dp[hh], qeb[:, ks[hh]]) for hh in heads], axis=1)
        dstn = [dst_scr[hh] for hh in heads]
        dkd_c, dv_inter, debl = [None] * n_c, [None] * n_c, [None] * n_c
        for c in reversed(range(n_c)):
            dstnb = [_bf(dstn[hh]) for hh in heads]
            dkd_c[c] = jnp.concatenate([_dot(v[hh][rs[c]], dstnb[hh]) for hh in heads], axis=1)
            dv_inter[c] = [_dot_nt(kdb[rs[c], ks[hh]], dstnb[hh]) for hh in heads]
            debl[c] = jnp.concatenate(
                [jnp.sum(dstn[hh] * st_ref[c, hh], axis=0, keepdims=True) for hh in heads], axis=1)
            dstn = [dst_intra[c][hh] + dstn[hh] * ebl[c][:, ks[hh]] for hh in heads]
        for hh in heads:
            dst_scr[hh] = dstn[hh]
            dv_ref[:, vs[hh]] = _bf(dv_intra[hh] + jnp.concatenate([dv_inter[c][hh] for c in range(n_c)], axis=0))
        dqe = jnp.concatenate(
            [dqe_intra[hh] + jnp.concatenate([dqe_inter[c][hh] for c in range(n_c)], axis=0) for hh in heads], axis=1)
        dkd = jnp.concatenate(dkd_c, axis=0)
        dkd_kd = dkd * kd
        db = dqe * qe - dke * ke - dkd_kd
        dbl = jnp.concatenate(
            [jnp.broadcast_to(jnp.sum(dkd_kd[rs[c]], axis=0, keepdims=True) + ebl[c] * debl[c], (C, GLA_DK))
             for c in range(n_c)], axis=0)
        dla = _tri_left(upp, db) + dbl
        dqk_ref[:, :GLA_DK] = _bf(dqe * eb * scale)
        dqk_ref[:, GLA_DK:] = _bf(dke * enb + dkd * ebl_b)
        ddec = dla * (1.0 / GLA_TAU) * (1.0 - jnp.exp(GLA_TAU * la))
        ddecb = _bf(ddec)
        drank_ref[...] = _bf(_dot_nt(ddecb, _bf(wd_ref[...])))
        dwd_ref[...] += _dot_tn(_bf(rank_ref[...]), ddecb)
        dbd_ref[...] += jnp.sum(ddec, axis=0, keepdims=True)

    return pl.pallas_call(
        body, name="gla_bwd",
        grid=(n_chunks // n_c,),
        in_specs=[pl.BlockSpec((None, R, 1024), lambda n: (0, last - n, 0)),
                  pl.BlockSpec((None, R, 1024), lambda n: (0, last - n, 0)),
                  pl.BlockSpec((R, GLA_DK), lambda n: (last - n, 0)),
                  pl.BlockSpec((R, 1024), lambda n: (last - n, 0)),
                  pl.BlockSpec((n_c, GLA_HEADS, GLA_HV, GLA_HK), lambda n: (last - n, 0, 0, 0)),
                  pl.BlockSpec((R, 128), lambda n: (last - n, 0)),
                  pl.BlockSpec((128, GLA_DK), lambda n: (0, 0))],
        out_specs=(pl.BlockSpec((R, 1024), lambda n: (last - n, 0)),
                   pl.BlockSpec((R, 1024), lambda n: (last - n, 0)),
                   pl.BlockSpec((R, 128), lambda n: (last - n, 0)),
                   pl.BlockSpec((128, GLA_DK), lambda n: (0, 0)),
                   pl.BlockSpec((1, GLA_DK), lambda n: (0, 0))),
        out_shape=(jax.ShapeDtypeStruct((T, 1024), BF16),
                   jax.ShapeDtypeStruct((T, 1024), BF16),
                   jax.ShapeDtypeStruct((T, 128), BF16),
                   jax.ShapeDtypeStruct((128, GLA_DK), F32),
                   jax.ShapeDtypeStruct((1, GLA_DK), F32)),
        scratch_shapes=[pltpu.VMEM((GLA_HEADS, GLA_HV, GLA_HK), F32)],
        compiler_params=_cparams(("arbitrary",)),
    )(projf, projb, la, do_gla, st_all, rank, wdec)


def _sb_logs(z):
    lsz = jnp.minimum(z, 0.0) - _softplus_neg_abs(z)
    return lsz, lsz - z


SB_HG_FWD = 8
SB_HG_BWD = 4
SB_QUERIES = 256
SB_KEYS = 256
SB_DEAD = -105.0


def _sb_fwd_call(projb, wp_shard):
    T = projb.shape[1]
    B = min(SB_QUERIES, T)
    HG = SB_HG_FWD
    W = HG * SB_HD
    scale = 1.0 / math.sqrt(SB_HD)
    KB = min(SB_KEYS, T)
    n_h, n_i = SB_HEADS // HG, T // B

    def body(q_ref, k_ref, v_ref, wp_ref, o_ref, wpall_ref, cb_scr, send_sems, recv_sems, loc_sem):
        i = pl.program_id(1)
        own, pairs = _push_copies(wp_ref, wpall_ref, send_sems, recv_sems, loc_sem, scatter=False)

        @pl.when((pl.program_id(0) == 0) & (i == 0))
        def _():
            _push_start(own, pairs)

        rows = HG * B
        after = (_iota2(KB, KB, 0) > _iota2(KB, KB, 1)).astype(F32)
        tri = _bf(jnp.concatenate([after, jnp.ones((KB, KB), F32)], axis=1))
        o_ref[...] = jnp.zeros_like(o_ref)
        cb_scr[...] = jnp.zeros_like(cb_scr)

        def block(jp, masked):
            off = pl.multiple_of(jp * KB, KB)
            z = jnp.concatenate(
                [_dot_nt(q_ref[:, hh * SB_HD:(hh + 1) * SB_HD], k_ref[pl.ds(off, KB), hh * SB_HD:(hh + 1) * SB_HD])
                 for hh in range(HG)], axis=0) * scale
            lsz, l1m = _sb_logs(z)
            if masked:
                strict = (jp * KB + _iota2(rows, KB, 1)) < (i * B + (_iota2(rows, KB, 0) & (B - 1)))
                l1m = jnp.where(strict, l1m, 0.0)
            r = _tri2_right(l1m, tri)
            cb = cb_scr[...]
            a = jnp.exp(lsz + cb + r[:, :KB])
            if masked:
                a = jnp.where(strict, a, 0.0)
            cb_scr[...] = cb + r[:, KB:]
            ab = _bf(a)
            for hh in range(HG):
                cs = slice(hh * SB_HD, (hh + 1) * SB_HD)
                o_ref[:, cs] += _dot(ab[hh * B:(hh + 1) * B, :], v_ref[pl.ds(off, KB), cs])

        jp0 = (i * B) // KB
        block(jp0, True)

        def live(state):
            jj, dead = state
            return (jj <= jp0) & jnp.logical_not(dead)

        def step(state):
            jj, _ = state
            block(jp0 - jj, False)
            return jj + 1, jnp.max(cb_scr[:, :SB_HD]) < SB_DEAD

        lax.while_loop(live, step, (jnp.int32(1), jnp.max(cb_scr[:, :SB_HD]) < SB_DEAD))

        @pl.when((pl.program_id(0) == n_h - 1) & (i == n_i - 1))
        def _():
            _push_wait(own, pairs)

    return pl.pallas_call(
        body, name="sb_fwd",
        grid=(n_h, n_i),
        in_specs=[pl.BlockSpec((None, B, W), lambda h, i: (1, i, h)),
                  pl.BlockSpec((None, T, W), lambda h, i: (2, 0, h)),
                  pl.BlockSpec((None, T, W), lambda h, i: (3, 0, h)),
                  _ANY],
        out_specs=(pl.BlockSpec((B, W), lambda h, i: (i, h)), _ANY),
        out_shape=(jax.ShapeDtypeStruct((T, 1024), F32),
                   jax.ShapeDtypeStruct((N_DEV,) + wp_shard.shape, wp_shard.dtype)),
        scratch_shapes=[pltpu.VMEM((HG * B, KB), F32)] + _PUSH_SEMS,
        compiler_params=_cparams(("arbitrary", "arbitrary")),
    )(projb, projb, projb, wp_shard)


def _sb_bwd_call(projb, do_sb, g_p):
    T = projb.shape[1]
    B = min(SB_QUERIES, T)
    nb = T // B
    HG = SB_HG_BWD
    W = HG * SB_HD
    WQ = HG * B
    KB = min(SB_KEYS, T)
    nkb = T // KB
    n_h = SB_HEADS // HG
    scale = 1.0 / math.sqrt(SB_HD)

    def body(q_ref, k_ref, v_ref, do_ref, gp_ref, dq_ref, dk_ref, dv_ref, rp_ref,
             dk_scr, dv_scr, kt_scr, beta_scr, g_scr, dqt_scr, send_sems, recv_sems, loc_sem):
        i = pl.program_id(1)
        own, pairs = _push_copies(gp_ref, rp_ref, send_sems, recv_sems, loc_sem, scatter=True)

        @pl.when((pl.program_id(0) == 0) & (i == 0))
        def _():
            _push_start(own, pairs)

        @pl.when(i == 0)
        def _():
            dk_scr[...] = jnp.zeros_like(dk_scr)
            dv_scr[...] = jnp.zeros_like(dv_scr)
            for hh in range(HG):
                for jb in range(nkb):
                    kt_scr[hh, jb] = _bf(
                        k_ref[jb * KB:(jb + 1) * KB, hh * SB_HD:(hh + 1) * SB_HD].astype(F32).T)

        dqt_scr[...] = jnp.zeros_like(dqt_scr)
        later = _bf((_iota2(KB, KB, 1) > _iota2(KB, KB, 0)).astype(F32))
        earlier = _bf((_iota2(KB, KB, 1) < _iota2(KB, KB, 0)).astype(F32))
        dob = _bf(do_ref[...])
        jp0 = (i * B) // KB

        def strict_mask():
            return (jp0 * KB + _iota2(KB, WQ, 0)) < (i * B + (_iota2(KB, WQ, 1) & (B - 1)))

        def heads(fn):
            return [fn(slice(hh * SB_HD, (hh + 1) * SB_HD)) for hh in range(HG)]

        def pass1(jp, cb, masked):
            off = pl.multiple_of(jp * KB, KB)
            z = jnp.concatenate(heads(lambda cs: _dot_nt(k_ref[pl.ds(off, KB), cs], q_ref[:, cs])), axis=1) * scale
            da = jnp.concatenate(heads(lambda cs: _dot_nt(v_ref[pl.ds(off, KB), cs], dob[:, cs])), axis=1)
            lsz, l1m = _sb_logs(z)
            if masked:
                strict = strict_mask()
                l1m = jnp.where(strict, l1m, 0.0)
            a = jnp.exp(lsz + cb + _tri2_left(later, l1m))
            if masked:
                a = jnp.where(strict, a, 0.0)
            g_scr[jp] = a * da
            beta_scr[jp] = jnp.exp(lsz)
            ab = _bf(a)
            for hh in range(HG):
                cs = slice(hh * SB_HD, (hh + 1) * SB_HD)
                dv_scr[pl.ds(off, KB), cs] += _dot(ab[:, hh * B:(hh + 1) * B], dob[:, cs])
            return cb + jnp.sum(l1m, axis=0, keepdims=True)

        zero = jnp.zeros((1, WQ), F32)
        cb = pass1(jp0, zero, True)

        def live(state):
            jj, _, dead = state
            return (jj <= jp0) & jnp.logical_not(dead)

        def step(state):
            jj, cr, _ = state
            cr = pass1(jp0 - jj, cr, False)
            return jj + 1, cr, jnp.max(cr) < SB_DEAD

        n_done, _, _ = lax.while_loop(live, step, (jnp.int32(1), cb, jnp.max(cb) < SB_DEAD))
        jp_first = jp0 - (n_done - 1)

        def pass2(jp, cg, masked):
            off = pl.multiple_of(jp * KB, KB)
            g = g_scr[jp]
            beta = beta_scr[jp]
            dz = g * (1.0 - beta) - beta * (cg + _tri2_left(earlier, g))
            if masked:
                dz = jnp.where(strict_mask(), dz, 0.0)
            dzb = _bf(dz * scale)
            for hh in range(HG):
                cs = slice(hh * SB_HD, (hh + 1) * SB_HD)
                dk_scr[pl.ds(off, KB), cs] += _dot(dzb[:, hh * B:(hh + 1) * B], q_ref[:, cs])
                dqt_scr[hh] += _dot(kt_scr[hh, jp], dzb[:, hh * B:(hh + 1) * B])
            return cg + jnp.sum(g, axis=0, keepdims=True)

        cg = lax.fori_loop(jp_first, jp0, lambda jp, cr: pass2(jp, cr, False), zero)
        pass2(jp0, cg, True)
        for hh in range(HG):
            dq_ref[:, hh * SB_HD:(hh + 1) * SB_HD] = _bf(dqt_scr[hh].T)

        @pl.when(i == nb - 1)
        def _():
            dk_ref[...] = _bf(dk_scr[...])
            dv_ref[...] = _bf(dv_scr[...])

        @pl.when((pl.program_id(0) == n_h - 1) & (i == nb - 1))
        def _():
            _push_wait(own, pairs)

    return pl.pallas_call(
        body, name="sb_bwd",
        grid=(n_h, nb),
        in_specs=[pl.BlockSpec((None, B, W), lambda h, i: (1, i, h)),
                  pl.BlockSpec((None, T, W), lambda h, i: (2, 0, h)),
                  pl.BlockSpec((None, T, W), lambda h, i: (3, 0, h)),
                  pl.BlockSpec((B, W), lambda h, i: (i, h)),
                  _ANY],
        out_specs=(pl.BlockSpec((B, W), lambda h, i: (i, h)),
                   pl.BlockSpec((T, W), lambda h, i: (0, h)),
                   pl.BlockSpec((T, W), lambda h, i: (0, h)),
                   _ANY),
        out_shape=(jax.ShapeDtypeStruct((T, 1024), BF16),
                   jax.ShapeDtypeStruct((T, 1024), BF16),
                   jax.ShapeDtypeStruct((T, 1024), BF16),
                   jax.ShapeDtypeStruct(g_p.shape, g_p.dtype)),
        scratch_shapes=[pltpu.VMEM((T, W), F32), pltpu.VMEM((T, W), F32),
                        pltpu.VMEM((HG, nkb, SB_HD, KB), BF16),
                        pltpu.VMEM((nkb, KB, WQ), F32), pltpu.VMEM((nkb, KB, WQ), F32),
                        pltpu.VMEM((HG, SB_HD, B), F32)] + _PUSH_SEMS,
        compiler_params=_cparams(("arbitrary", "arbitrary")),
    )(projb, projb, projb, do_sb, g_p)


def _mid_call(o_gla, o_sb, projf, x, target, wpa, wpb, wo, gla_g, b_gate, final_g):
    T, D = x.shape
    tm = min(TBLK, T)

    def body(og_ref, ggate_ref, osb_ref, sgate_ref, ma_ref, mb_ref, x_ref, tgt_ref,
             wpa_ref, wpb_ref, wo_ref, glag_ref, bg_ref, fg_ref,
             dx2_ref, dogla_ref, dosb_ref, dggate_ref, dsgate_ref, dm_ref,
             mt_ref, ogt_ref, obt_ref, dx2b_ref, dya_ref, dyb_ref,
             dfg_ref, dbg_ref, dglag_ref, loss_ref):
        @pl.when(pl.program_id(0) == 0)
        def _():
            dfg_ref[...] = jnp.zeros_like(dfg_ref)
            dbg_ref[...] = jnp.zeros_like(dbg_ref)
            dglag_ref[...] = jnp.zeros_like(dglag_ref)
            loss_ref[...] = jnp.zeros_like(loss_ref)

        glag = glag_ref[...]
        ggate = ggate_ref[...]
        sg = _sigmoid(ggate)
        silu_g = ggate * sg
        ohat, rinv, nrm = [], [], []
        for hh in range(GLA_HEADS):
            oh = og_ref[:, hh * GLA_HV:(hh + 1) * GLA_HV]
            r = lax.rsqrt(jnp.mean(oh * oh, axis=-1, keepdims=True) + EPS)
            ohat.append(oh * r)
            rinv.append(r)
            nrm.append(ohat[-1] * glag)
        n_all = jnp.concatenate(nrm, axis=1)
        og = n_all * silu_g
        ogb = _bf(og)
        ya = _dot(ogb, wpa_ref[...])
        sgate = sgate_ref[...]
        ss = _sigmoid(sgate)
        silu_s = sgate * ss
        osb = osb_ref[...]
        ob = osb * silu_s
        obb = _bf(ob)
        yb = _dot(obb, wpb_ref[...])
        ga = _sigmoid(ma_ref[...] + bg_ref[:, :D])
        gb = _sigmoid(mb_ref[...] + bg_ref[:, D:])
        merged = ga * ya + gb * yb
        mgb = _bf(merged)
        x2 = x_ref[...] + _dot(mgb, wo_ref[...])
        r2 = lax.rsqrt(jnp.mean(x2 * x2, axis=-1, keepdims=True) + EPS)
        xh2 = x2 * r2
        fg = fg_ref[...]
        err = xh2 * fg - tgt_ref[...]
        loss_ref[...] += jnp.broadcast_to(
            0.5 * jnp.sum(jnp.mean(err * err, axis=-1, keepdims=True), axis=0, keepdims=True), (1, 128))
        dy = err * (1.0 / D)
        dfg_ref[...] += jnp.sum(dy * xh2, axis=0, keepdims=True)
        dxh = dy * fg
        dx2 = r2 * (dxh - xh2 * jnp.mean(dxh * xh2, axis=-1, keepdims=True))
        dx2_ref[...] = dx2
        dx2b = _bf(dx2)
        dx2b_ref[...] = dx2b
        dmerged = _dot_nt(dx2b, wo_ref[...])
        dya = dmerged * ga
        dyb = dmerged * gb
        dma = dmerged * ya * ga * (1.0 - ga)
        dmb = dmerged * yb * gb * (1.0 - gb)
        dm_ref[:, :D] = _bf(dma)
        dm_ref[:, D:] = _bf(dmb)
        dbg_ref[:, :D] += jnp.sum(dma, axis=0, keepdims=True)
        dbg_ref[:, D:] += jnp.sum(dmb, axis=0, keepdims=True)
        dyab = _bf(dya)
        dybb = _bf(dyb)
        dya_ref[...] = dyab
        dyb_ref[...] = dybb
        dog = _dot_nt(dyab, wpa_ref[...])
        dob = _dot_nt(dybb, wpb_ref[...])
        dosb_ref[...] = dob * silu_s
        dsgate_ref[...] = _bf(dob * osb * (ss * (1.0 + sgate * (1.0 - ss))))
        dn = dog * silu_g
        dggate_ref[...] = _bf(dog * n_all * (sg * (1.0 + ggate * (1.0 - sg))))
        dglag = jnp.zeros((1, GLA_HV), F32)
        for hh in range(GLA_HEADS):
            dnh = dn[:, hh * GLA_HV:(hh + 1) * GLA_HV]
            dglag = dglag + jnp.sum(dnh * ohat[hh], axis=0, keepdims=True)
            dohat = dnh * glag
            dogla_ref[:, hh * GLA_HV:(hh + 1) * GLA_HV] = rinv[hh] * (
                dohat - ohat[hh] * jnp.mean(dohat * ohat[hh], axis=-1, keepdims=True))
        dglag_ref[...] += dglag
        mt_ref[...] = _bf(merged.T)
        ogt_ref[...] = _bf(og.T)
        obt_ref[...] = _bf(ob.T)

    row = lambda i: (i, 0)
    const = lambda i: (0, 0)
    tile = pl.BlockSpec((tm, D), row)
    tile_t = pl.BlockSpec((None, D, tm), lambda i: (i, 0, 0))
    wspec = pl.BlockSpec((D, D), const)
    return pl.pallas_call(
        body, name="mid",
        grid=(T // tm,),
        in_specs=[tile,
                  pl.BlockSpec((None, tm, D), lambda i: (1, i, 0)),
                  tile,
                  pl.BlockSpec((None, tm, D), lambda i: (2, i, 0)),
                  pl.BlockSpec((None, tm, D), lambda i: (3, i, 0)),
                  pl.BlockSpec((None, tm, D), lambda i: (4, i, 0)),
                  tile, tile, wspec, wspec, wspec,
                  pl.BlockSpec((1, GLA_HV), const),
                  pl.BlockSpec((1, 2 * D), const),
                  pl.BlockSpec((1, D), const)],
        out_specs=(tile, tile, tile, tile, tile,
                   pl.BlockSpec((tm, 2 * D), row),
                   tile_t, tile_t, tile_t, tile, tile, tile,
                   pl.BlockSpec((1, D), const),
                   pl.BlockSpec((1, 2 * D), const),
                   pl.BlockSpec((1, GLA_HV), const),
                   pl.BlockSpec((1, 128), const)),
        out_shape=(jax.ShapeDtypeStruct((T, D), F32),
                   jax.ShapeDtypeStruct((T, D), F32),
                   jax.ShapeDtypeStruct((T, D), F32),
                   jax.ShapeDtypeStruct((T, D), BF16),
                   jax.ShapeDtypeStruct((T, D), BF16),
                   jax.ShapeDtypeStruct((T, 2 * D), BF16),
                   jax.ShapeDtypeStruct((T // tm, D, tm), BF16),
                   jax.ShapeDtypeStruct((T // tm, D, tm), BF16),
                   jax.ShapeDtypeStruct((T // tm, D, tm), BF16),
                   jax.ShapeDtypeStruct((T, D), BF16),
                   jax.ShapeDtypeStruct((T, D), BF16),
                   jax.ShapeDtypeStruct((T, D), BF16),
                   jax.ShapeDtypeStruct((1, D), F32),
                   jax.ShapeDtypeStruct((1, 2 * D), F32),
                   jax.ShapeDtypeStruct((1, GLA_HV), F32),
                   jax.ShapeDtypeStruct((1, 128), F32)),
        compiler_params=_cparams(("arbitrary",)),
    )(o_gla, projf, o_sb, projf, projf, projf, x, target, wpa, wpb, wo, gla_g, b_gate, final_g)


def _dh_call(pieces, dmlog, drank, wt, wr, x, dx2, norm_g, s_in):
    T, D = x.shape
    tm = min(256, T)
    npc = len(pieces)
    n_main = N_GROUPS * 1024
    n_i = T // tm
    i_forward = 5 * n_i // 8

    def body(*refs):
        pcs = refs[:npc]
        (dm_ref, dr_ref, w_hbm, wr_ref, x_ref, dx2_ref, g_ref, sin_ref,
         gx_ref, dg_ref, rin_ref, relayed_ref, w_scr, sems, *exchange_scratch) = refs[npc:]
        start, forward, finish = _chip_reduce_steps(sin_ref, rin_ref, relayed_ref, *exchange_scratch)

        @pl.when(pl.program_id(0) == 0)
        def _():
            start()
            lo = pltpu.make_async_copy(w_hbm.at[pl.ds(0, RANK_COL)], w_scr.at[pl.ds(0, RANK_COL)], sems.at[0])
            hi = pltpu.make_async_copy(w_hbm.at[pl.ds(RANK_COL + GLA_RANK, n_main - RANK_COL)],
                                       w_scr.at[pl.ds(RANK_COL, n_main - RANK_COL)], sems.at[1])
            lo.start()
            hi.start()
            dg_ref[...] = jnp.zeros_like(dg_ref)
            lo.wait()
            hi.wait()

        @pl.when(pl.program_id(0) == i_forward)
        def _():
            forward()

        def w_group(g):
            return w_scr[g * 1024:(g + 1) * 1024, :]

        dr = dr_ref[...]
        dh = _dot(dr, wr_ref[...])
        for g in range(npc):
            dh = dh + _dot(pcs[g][...], w_group(g))
        dh = dh + _dot(dm_ref[:, :D], w_group(npc))
        dh = dh + _dot(dm_ref[:, D:], w_group(npc + 1))
        xv = x_ref[...]
        r = lax.rsqrt(jnp.mean(xv * xv, axis=-1, keepdims=True) + EPS)
        xhat = xv * r
        g = g_ref[...]
        dg_ref[...] += jnp.sum(dh * xhat, axis=0, keepdims=True)
        dxhat = dh * g
        gx_ref[...] = r * (dxhat - xhat * jnp.mean(dxhat * xhat, axis=-1, keepdims=True)) + dx2_ref[...]

        @pl.when(pl.program_id(0) == n_i - 1)
        def _():
            finish()

    row = lambda i: (i, 0)
    const = lambda i: (0, 0)
    tile = pl.BlockSpec((tm, D), row)
    part = s_in.shape[1:]
    return pl.pallas_call(
        body, name="dh",
        grid=(n_i,),
        in_specs=[tile] * npc + [
            pl.BlockSpec((tm, 2 * D), row),
            pl.BlockSpec((tm, 128), row),
            _ANY,
            pl.BlockSpec((128, D), const),
            tile, tile,
            pl.BlockSpec((1, D), const),
            _ANY],
        out_specs=(tile, pl.BlockSpec((1, D), const), _ANY, _ANY),
        out_shape=(jax.ShapeDtypeStruct((T, D), F32),
                   jax.ShapeDtypeStruct((1, D), F32),
                   jax.ShapeDtypeStruct((3,) + part, s_in.dtype),
                   jax.ShapeDtypeStruct(part, s_in.dtype)),
        scratch_shapes=[pltpu.VMEM((n_main, D), BF16), pltpu.SemaphoreType.DMA((2,))]
        + _chip_reduce_scratch(*part, s_in.dtype),
        compiler_params=_cparams(("arbitrary",)),
    )(*pieces, dmlog, drank, wt, wr, x, dx2, norm_g, s_in)


def _wgrad_rank_call(ht, drank):
    n_tb, D, tb = ht.shape

    def body(ht_ref, dr_ref, o_ref):
        @pl.when(pl.program_id(0) == 0)
        def _():
            o_ref[...] = jnp.zeros_like(o_ref)

        o_ref[...] += _dot(ht_ref[...], dr_ref[...])

    return pl.pallas_call(
        body, name="wgrad_rank",
        grid=(n_tb,),
        in_specs=[pl.BlockSpec((None, D, tb), lambda i: (i, 0, 0)),
                  pl.BlockSpec((tb, 128), lambda i: (i, 0))],
        out_specs=pl.BlockSpec((D, 128), lambda i: (0, 0)),
        out_shape=jax.ShapeDtypeStruct((D, 128), F32),
        compiler_params=_cparams(("arbitrary",)),
    )(ht, drank)


def _wgrad_call(lhs_list, lhs_of_group, rhs_list, rhs_of_group, n_transposed, name):
    n_groups = len(rhs_of_group)
    n_tb, D, tb = lhs_list[0].shape
    T = n_tb * tb
    per = min(4, n_tb)
    tk = per * tb
    nk = T // tk
    nl = len(lhs_list)

    def body(*refs):
        lhs = refs[:nl]
        rhs = refs[nl:nl + n_groups]
        out_ref, acc = refs[nl + n_groups:]
        g = pl.program_id(0)
        i = pl.program_id(1)

        @pl.when(i == 0)
        def _():
            acc[...] = jnp.zeros_like(acc)

        for p in range(n_groups):
            @pl.when(g == p)
            def _(p=p):
                lref = lhs[lhs_of_group[p]]
                part = _dot(lref[0], rhs[p][0:tb, :])
                for b in range(1, per):
                    part = part + _dot(lref[b], rhs[p][b * tb:(b + 1) * tb, :])
                acc[...] += part

        @pl.when((i == nk - 1) & (g < n_transposed))
        def _():
            out_ref[...] = _bf(acc[...].T)

        @pl.when((i == nk - 1) & (g >= n_transposed))
        def _():
            out_ref[...] = _bf(acc[...])

    def lhs_spec(a):
        groups = [g for g in range(n_groups) if lhs_of_group[g] == a]
        lo, hi = min(groups), max(groups)
        assert groups == list(range(lo, hi + 1))
        return pl.BlockSpec((per, D, tb), lambda g, i: (jnp.where((g >= lo) & (g <= hi), i, 0), 0, 0))

    def rhs_spec(p):
        cb = rhs_of_group[p][1]
        return pl.BlockSpec((tk, 1024), lambda g, i: (jnp.where(g == p, i, 0), cb))

    return pl.pallas_call(
        body, name=name,
        grid=(n_groups, nk),
        in_specs=[lhs_spec(a) for a in range(nl)] + [rhs_spec(p) for p in range(n_groups)],
        out_specs=pl.BlockSpec((None, D, 1024), lambda g, i: (g, 0, 0)),
        out_shape=jax.ShapeDtypeStruct((n_groups, D, 1024), BF16),
        scratch_shapes=[pltpu.VMEM((D, 1024), F32)],
        compiler_params=_cparams(("arbitrary", "arbitrary")),
    )(*lhs_list, *[rhs_list[rhs_of_group[p][0]] for p in range(n_groups)])


def _adamw_math(parts, w, m, v):
    g = parts[0].astype(F32)
    for p in parts[1:]:
        g = g + p.astype(F32)
    mm = ADAM_B1 * m + (1.0 - ADAM_B1) * g
    vv = ADAM_B2 * v + (1.0 - ADAM_B2) * (g * g)
    m_hat = mm / (1.0 - ADAM_B1 ** ADAM_STEP)
    v_hat = vv / (1.0 - ADAM_B2 ** ADAM_STEP)
    return g, -ADAM_LR * (m_hat / (jnp.sqrt(v_hat) + ADAM_EPS) + ADAM_WD * w), mm, vv


def _part_order(n_parts):
    return [n_parts - 1] + list(range(n_parts - 1))


def _adamw_call(parts, w, m, v, name):
    R, C = w.shape
    n_parts = parts.shape[0]
    (tr, tc), grid, idx = _tiling_2d(R, C, 512)

    def body(p_ref, w_ref, m_ref, v_ref, g_ref, d_ref, nm_ref, nv_ref):
        g_ref[...], d_ref[...], nm_ref[...], nv_ref[...] = _adamw_math(
            [p_ref[k] for k in _part_order(n_parts)], w_ref[...], m_ref[...], v_ref[...])

    blk = pl.BlockSpec((tr, tc), idx)
    sds = jax.ShapeDtypeStruct((R, C), F32)
    return pl.pallas_call(
        body, name=name,
        grid=grid,
        in_specs=[pl.BlockSpec((n_parts, tr, tc), lambda i: (0,) + idx(i)), blk, blk, blk],
        out_specs=(blk, blk, blk, blk),
        out_shape=(sds, sds, sds, sds),
        compiler_params=_cparams(("arbitrary",)),
    )(parts, w, m, v)


def _adamw_rows_call(parts, ws, ms, vs, name, gathered):
    n = len(ws)
    R, C = ws[0].shape
    n_parts = parts.shape[0]

    def body(*refs):
        p_ref = refs[0]
        w_refs, m_refs, v_refs = refs[1:1 + n], refs[1 + n:1 + 2 * n], refs[1 + 2 * n:1 + 3 * n]
        src_ref = refs[1 + 3 * n]
        outs = refs[2 + 3 * n:2 + 7 * n]
        dst_ref, send_sems, recv_sems, loc_sem = refs[2 + 7 * n:]
        own, pairs = _push_copies(src_ref, dst_ref, send_sems, recv_sems, loc_sem, scatter=False)
        k_now = pl.program_id(0)

        @pl.when(k_now == 0)
        def _():
            _push_start(own, pairs)

        for k in range(n):
            @pl.when(k_now == k)
            def _(k=k):
                res = _adamw_math([p_ref[j] for j in _part_order(n_parts)],
                                  w_refs[k][...], m_refs[k][...], v_refs[k][...])
                for o_ref, val in zip(outs[4 * k:4 * k + 4], res):
                    o_ref[...] = val

        @pl.when(k_now == n - 1)
        def _():
            _push_wait(own, pairs)

    whole = pl.BlockSpec((R, C), lambda k: (0, 0))
    sds = jax.ShapeDtypeStruct((R, C), F32)
    res = pl.pallas_call(
        body, name=name,
        grid=(n,),
        in_specs=[pl.BlockSpec((n_parts, R, C), lambda k: (0, k, 0))] + [whole] * (3 * n) + [_ANY],
        out_specs=tuple([whole] * (4 * n) + [_ANY]),
        out_shape=tuple([sds] * (4 * n) + [jax.ShapeDtypeStruct((N_DEV,) + gathered.shape, gathered.dtype)]),
        scratch_shapes=_PUSH_SEMS,
        compiler_params=_cparams(("arbitrary",)),
    )(parts, *ws, *ms, *vs, gathered)
    return [res[4 * k:4 * k + 4] for k in range(n)], res[4 * n]


def _adamw_lanes_call(parts, offsets, ws, ms, vs, name):
    n = len(ws)
    n_parts = parts.shape[0]

    def body(*refs):
        p_ref = refs[0]
        w_refs, m_refs, v_refs = refs[1:1 + n], refs[1 + n:1 + 2 * n], refs[1 + 2 * n:1 + 3 * n]
        outs = refs[1 + 3 * n:]
        for k in range(n):
            lanes = slice(offsets[k], offsets[k] + ws[k].shape[1])
            res = _adamw_math([p_ref[j, :, lanes] for j in _part_order(n_parts)],
                              w_refs[k][...], m_refs[k][...], v_refs[k][...])
            for o_ref, val in zip(outs[4 * k:4 * k + 4], res):
                o_ref[...] = val

    res = pl.pallas_call(
        body, name=name,
        out_shape=tuple(jax.ShapeDtypeStruct(ws[k].shape, F32) for k in range(n) for _ in range(4)),
        compiler_params=_cparams(),
    )(parts, *ws, *ms, *vs)
    return [res[4 * k:4 * k + 4] for k in range(n)]


def _local_step(x, target, wt, wr, wdec, bdec, wp_shard, norm_g, gla_g, b_gate, final_g):
    D = x.shape[1]
    half = wp_shard.shape[1] // 2
    projf, projb, rank, ht, wp_lo = _proj_call(x, norm_g, wt, wr, wp_shard[:, :half])
    o_gla, st_all, la = _gla_fwd_call(projf, projb, rank, wdec, bdec)
    o_sb, wp_hi = _sb_fwd_call(projb, wp_shard[:, half:])
    wp_full = jnp.concatenate([wp_lo, wp_hi], axis=2).transpose(1, 0, 2, 3).reshape(3, D, D)
    (dx2, do_gla, do_sb, dggate, dsgate, dmlog, mt, ogt, obt, dx2b, dya, dyb,
     dfinal_g, db_gate, dgla_g, loss) = _mid_call(o_gla, o_sb, projf, x, target, wp_full[0], wp_full[1],
                                                 wp_full[2], gla_g, b_gate, final_g)
    dw_p = _wgrad_call([ogt, obt, mt], [0, 1, 2], [dya, dyb, dx2b], [(0, 0), (1, 0), (2, 0)], 0, "wgrad_p")
    g_p = dw_p.reshape(3, N_DEV, D // N_DEV, D).transpose(1, 0, 2, 3).reshape(N_DEV, 3 * (D // N_DEV), D)
    dqk, dgv, drank, dwdec, dbdec = _gla_bwd_call(projf, projb, la, do_gla, st_all, rank, wdec)
    dsq, dsk, dsv, r_p = _sb_bwd_call(projb, do_sb, g_p)
    pieces = [dqk, dgv, dggate, dsq, dsk, dsv, dsgate]
    rhs_of_group = [(g, 0) for g in range(7)] + [(7, 0), (7, 1)]
    dw_in = _wgrad_call([ht], [0] * N_GROUPS, pieces + [dmlog], rhs_of_group, N_GROUPS, "wgrad_in")
    dwr = _wgrad_rank_call(ht, drank)
    g_in = _parts_by_device_call(dw_in.reshape(N_GROUPS * 1024, D), dwr[:, :GLA_RANK].T.astype(BF16))
    c_idx = lax.axis_index("c").astype(jnp.int32).reshape(1)
    (p_in,) = _pair_exchange([g_in], "pair_g")
    s_in = _pair_add_call(g_in, p_in, c_idx, "pair_add_in")
    grad_x, dnorm_g, r_in, _ = _dh_call(pieces, dmlog, drank, wt, wr, x, dx2, norm_g, s_in)
    small = jnp.concatenate([
        dnorm_g.reshape(-1), dbdec.reshape(-1), dgla_g.reshape(-1), db_gate.reshape(-1), dfinal_g.reshape(-1),
        loss.reshape(-1), dwdec[:GLA_RANK].reshape(-1)]).reshape(1, _SM_LEN)
    return grad_x, r_in, r_p, small


_SM_NORM = 0
_SM_BDEC = _SM_NORM + D_MODEL
_SM_GLAG = _SM_BDEC + GLA_DK
_SM_BGATE = _SM_GLAG + GLA_HV
_SM_FINAL = _SM_BGATE + 2 * D_MODEL
_SM_REPL = _SM_FINAL + D_MODEL
_SM_LOSS = _SM_REPL
_SM_WDEC = _SM_LOSS + 128
_SM_LEN = _SM_WDEC + GLA_RANK * GLA_DK


def kernel(x, norm_g, w_in, w_dec_up, b_dec, gla_norm_g, w_pa, w_pb, b_gate, w_o, final_g, loss_target, m_norm_g, m_w_in, m_w_dec_up, m_b_dec, m_gla_norm_g, m_w_pa, m_w_pb, m_b_gate, m_w_o, m_final_g, v_norm_g, v_w_in, v_w_dec_up, v_b_dec, v_gla_norm_g, v_w_pa, v_w_pb, v_b_gate, v_w_o, v_final_g):
    D = D_MODEL
    me = 4 * lax.axis_index("x") + 2 * lax.axis_index("y") + lax.axis_index("c")

    wp_shard = jnp.stack([w_pa, w_pb, w_o]).astype(BF16)
    n_first = _half_rows(SHARD_COLS)
    win_all, wdec_all = _all_gather([w_in.T.astype(BF16), w_dec_up], "gather_w",
                                    row_pieces=[[(0, n_first), (n_first, SHARD_COLS - n_first)], None])
    wt = _flatten_blocks_call(win_all)
    wr = jnp.pad(wt[RANK_COL:RANK_COL + GLA_RANK], ((0, 128 - GLA_RANK), (0, 0)))
    wdec_full = wdec_all.transpose(1, 0, 2).reshape(GLA_RANK, GLA_DK)
    wdec = jnp.pad(wdec_full, ((0, 128 - GLA_RANK), (0, 0)))

    grad_x, r_in, r_p, small = _local_step(
        x[0], loss_target[0], wt, wr, wdec, b_dec.reshape(1, -1), wp_shard,
        norm_g.reshape(1, -1), gla_norm_g.reshape(1, -1), b_gate.reshape(1, -1), final_g.reshape(1, -1))

    gw_in, d_in, nm_in, nv_in = (a.T for a in _adamw_call(r_in, w_in.T, m_w_in.T, v_w_in.T, "adamw_in"))
    ((g_pa, d_pa, nm_pa, nv_pa), (g_pb, d_pb, nm_pb, nv_pb), (g_o, d_o, nm_o, nv_o)), r_small = _adamw_rows_call(
        r_p, [w_pa, w_pb, w_o], [m_w_pa, m_w_pb, m_w_o], [v_w_pa, v_w_pb, v_w_o], "adamw_p", small)

    def row(a):
        return a.reshape(1, -1)

    rep = _adamw_lanes_call(
        r_small, [_SM_NORM, _SM_BDEC, _SM_GLAG, _SM_BGATE, _SM_FINAL],
        [row(a) for a in (norm_g, b_dec, gla_norm_g, b_gate, final_g)],
        [row(a) for a in (m_norm_g, m_b_dec, m_gla_norm_g, m_b_gate, m_final_g)],
        [row(a) for a in (v_norm_g, v_b_dec, v_gla_norm_g, v_b_gate, v_final_g)], "adamw_rep")
    ((g_norm, d_norm, nm_norm, nv_norm), (g_bdec, d_bdec, nm_bdec, nv_bdec), (g_glag, d_glag, nm_glag, nv_glag),
     (g_bgate, d_bgate, nm_bgate, nv_bgate), (g_final, d_final, nm_final, nv_final)) = [
        tuple(a.reshape(-1) for a in quad) for quad in rep]

    wdec_parts = r_small[:, 0, _SM_WDEC:].reshape(N_DEV, GLA_RANK, GLA_DK)
    cols = GLA_DK // N_DEV
    wdec_mine = lax.dynamic_slice_in_dim(wdec_parts, me * cols, cols, axis=2)
    g_wdec, d_wdec, nm_wdec, nv_wdec = _adamw_call(wdec_mine, w_dec_up, m_w_dec_up, v_w_dec_up, "adamw_dec")

    loss_total = jnp.sum(r_small[:, 0, _SM_LOSS])

    return (loss_total, grad_x[None],
            g_norm, gw_in, g_wdec, g_bdec, g_glag, g_pa, g_pb, g_bgate, g_o, g_final,
            d_norm, d_in, d_wdec, d_bdec, d_glag, d_pa, d_pb, d_bgate, d_o, d_final,
            nm_norm, nm_in, nm_wdec, nm_bdec, nm_glag, nm_pa, nm_pb, nm_bgate, nm_o, nm_final,
            nv_norm, nv_in, nv_wdec, nv_bdec, nv_glag, nv_pa, nv_pb, nv_bgate, nv_o, nv_final)
```

```python
import math

import jax
import jax.numpy as jnp
from jax import lax
from jax.experimental import pallas as pl
from jax.experimental.pallas import tpu as pltpu

F32 = jnp.float32
BF16 = jnp.bfloat16

N_DEV = 8
D_MODEL = 1024
GLA_HEADS = 4
GLA_HK = 128
GLA_HV = 256
GLA_DK = 512
GLA_RANK = 16
GLA_TAU = 16.0
GLA_CHUNK = 64
SB_HEADS = 8
SB_HD = 128
EPS = 1e-6
N_GROUPS = 9
RANK_COL = 3072
IN_COLS = 9232
SHARD_COLS = IN_COLS // N_DEV

ADAM_LR = 0.001
ADAM_B1 = 0.9
ADAM_B2 = 0.999
ADAM_EPS = 1e-08
ADAM_WD = 0.01
ADAM_STEP = 10

VMEM_LIMIT = 56 * 1024 * 1024
TBLK = 256


def _cparams(sem=None):
    return pltpu.CompilerParams(dimension_semantics=sem, vmem_limit_bytes=VMEM_LIMIT)


def _tiling_2d(rows, cols, band_cols):
    if rows * cols <= 128 * 1024:
        return (rows, cols), (1,), lambda i: (0, 0)
    if rows % 128 == 0:
        return (128, cols), (rows // 128,), lambda i: (i, 0)
    tc = band_cols if cols % band_cols == 0 else cols
    return (rows, tc), (cols // tc,), lambda i: (0, i)


def _dot(a, b):
    return jnp.dot(a, b, preferred_element_type=F32)


def _dot_nt(a, b):
    return lax.dot_general(a, b, (((1,), (1,)), ((), ())), preferred_element_type=F32)


def _dot_tn(a, b):
    return lax.dot_general(a, b, (((0,), (0,)), ((), ())), preferred_element_type=F32)


def _bf(x):
    return x.astype(BF16)


def _split3(x):
    hi = x.astype(BF16)
    r = x - hi.astype(F32)
    mid = r.astype(BF16)
    lo = (r - mid.astype(F32)).astype(BF16)
    return hi, mid, lo


def _tri_left(tri, x):
    hi, mid, lo = _split3(x)
    return _dot(tri, hi) + _dot(tri, mid) + _dot(tri, lo)


def _split2(x):
    hi = lax.bitcast_convert_type(lax.bitcast_convert_type(x, jnp.uint32) & jnp.uint32(0xFFFF0000), F32)
    return hi.astype(BF16), (x - hi).astype(BF16)


def _tri2_left(tri, x):
    hi, lo = _split2(x)
    return _dot(tri, hi) + _dot(tri, lo)


def _tri2_right(x, tri):
    hi, lo = _split2(x)
    return _dot(hi, tri) + _dot(lo, tri)


def _iota2(n, m, dim):
    return lax.broadcasted_iota(jnp.int32, (n, m), dim)


def _sigmoid(x):
    return 1.0 / (1.0 + jnp.exp(-x))


def _softplus_neg_abs(z):
    return jnp.log(1.0 + jnp.exp(-jnp.abs(z)))


_ANY = pl.BlockSpec(memory_space=pl.ANY)


def _mesh_pos():
    return lax.axis_index("x"), lax.axis_index("y"), lax.axis_index("c")


def _other_chips(x, y):
    return [(1 - x, y), (x, 1 - y), (1 - x, 1 - y)]


def _rcopy(src, dst, send_sem, recv_sem, to):
    return pltpu.make_async_remote_copy(src_ref=src, dst_ref=dst, send_sem=send_sem, recv_sem=recv_sem,
                                        device_id=to, device_id_type=pl.DeviceIdType.MESH)


def _push_copies(src_ref, dst_ref, send_sems, recv_sems, loc_sem, scatter):
    x, y, c = _mesh_pos()
    me = 4 * x + 2 * y + c
    own = pltpu.make_async_copy(src_ref.at[me] if scatter else src_ref, dst_ref.at[me], loc_sem)
    pairs = []
    for k in range(1, N_DEV):
        px = 1 - x if k & 4 else x
        py = 1 - y if k & 2 else y
        pc = 1 - c if k & 1 else c
        pid = 4 * px + 2 * py + pc
        src = src_ref.at[pid] if scatter else src_ref
        send = _rcopy(src, dst_ref.at[me], send_sems.at[k - 1], recv_sems.at[k - 1], (px, py, pc))
        recv = _rcopy(src, dst_ref.at[pid], send_sems.at[k - 1], recv_sems.at[k - 1], (px, py, pc))
        pairs.append((send, recv))
    return own, pairs


def _push_start(own, pairs):
    own.start()
    for send, _ in pairs:
        send.start()


def _push_wait(own, pairs):
    for _, recv in pairs:
        recv.wait_recv()
    for send, _ in pairs:
        send.wait_send()
    own.wait()


_PUSH_SEMS = [pltpu.SemaphoreType.DMA((N_DEV - 1,)), pltpu.SemaphoreType.DMA((N_DEV - 1,)),
              pltpu.SemaphoreType.DMA]


def _half_rows(rows):
    return (rows // 2) // 16 * 16


_ADD_ROWS = 128


def _chip_reduce_steps(src_ref, dst_ref, relayed_ref, sum_x, sum_y, rel_x, rel_y, load_sems, send_sems, recv_sems,
                       loc_sem):
    _, R, C = src_ref.shape
    n0 = _half_rows(R)
    lo, hi = pl.ds(0, n0), pl.ds(n0, R - n0)
    x, y, c = _mesh_pos()
    (xx, xy), (yx, yy), (dx, dy) = _other_chips(x, y)
    to_diag, to_x, to_y = src_ref.at[2 * dx + dy], src_ref.at[2 * xx + xy], src_ref.at[2 * yx + yy]
    x_nb, y_nb = (xx, xy, c), (yx, yy, c)
    relays = (_rcopy(to_diag.at[lo], relayed_ref.at[lo], send_sems.at[0], recv_sems.at[0], x_nb),
              _rcopy(to_diag.at[hi], relayed_ref.at[hi], send_sems.at[1], recv_sems.at[1], y_nb))
    plain = (_rcopy(to_x.at[lo], dst_ref.at[0, lo], send_sems.at[2], recv_sems.at[2], x_nb),
             _rcopy(to_y.at[hi], dst_ref.at[1, hi], send_sems.at[3], recv_sems.at[3], y_nb))
    summed = (_rcopy(sum_x, dst_ref.at[0, hi], send_sems.at[4], recv_sems.at[4], x_nb),
              _rcopy(sum_y, dst_ref.at[1, lo], send_sems.at[5], recv_sems.at[5], y_nb))
    load_mine = (pltpu.make_async_copy(to_x.at[hi], sum_x, load_sems.at[0]),
                 pltpu.make_async_copy(to_y.at[lo], sum_y, load_sems.at[1]))
    load_relayed = (pltpu.make_async_copy(relayed_ref.at[hi], rel_x, load_sems.at[2]),
                    pltpu.make_async_copy(relayed_ref.at[lo], rel_y, load_sems.at[3]))
    own = pltpu.make_async_copy(src_ref.at[2 * x + y], dst_ref.at[2], loc_sem)

    def start():
        for cp in relays + plain + (own,) + load_mine:
            cp.start()

    def add(acc_ref, rel_ref):
        for r0 in range(0, acc_ref.shape[0], _ADD_ROWS):
            rows = slice(r0, min(r0 + _ADD_ROWS, acc_ref.shape[0]))
            acc_ref[rows, :] = (acc_ref[rows, :].astype(F32) + rel_ref[rows, :].astype(F32)).astype(acc_ref.dtype)

    def forward():
        for cp in relays:
            cp.wait_recv()
        for cp in load_relayed:
            cp.start()
        for cp in load_mine + load_relayed:
            cp.wait()
        add(sum_x, rel_x)
        add(sum_y, rel_y)
        for cp in summed:
            cp.start()

    def finish():
        for cp in plain + summed:
            cp.wait_recv()
        for cp in relays + plain + summed:
            cp.wait_send()
        own.wait()

    return start, forward, finish


def _chip_reduce_scratch(rows, cols, dtype):
    n0 = _half_rows(rows)
    return [pltpu.VMEM((rows - n0, cols), dtype), pltpu.VMEM((n0, cols), dtype)] * 2 + [
        pltpu.SemaphoreType.DMA((4,)), pltpu.SemaphoreType.DMA((6,)), pltpu.SemaphoreType.DMA((6,)),
        pltpu.SemaphoreType.DMA]


def _all_gather(arrs, name, row_pieces=None):
    n = len(arrs)
    pieces = [[None] if not row_pieces or not row_pieces[a] else list(row_pieces[a]) for a in range(n)]
    assert all(len(p) in (1, 2) for p in pieces)
    units = [(a, i) for a in range(n) for i in range(len(pieces[a]))]

    def body(*refs):
        ins = refs[:n]
        outs = refs[n:2 * n]
        send_sems, recv_sems, loc_sems = refs[2 * n:]
        x, y, c = _mesh_pos()
        me, sib = (x, y, c), (x, y, 1 - c)
        xn, yn, dg = [(px, py, c) for px, py in _other_chips(x, y)]

        def rows(ref, a, i):
            return ref if pieces[a][i] is None else ref.at[pl.ds(*pieces[a][i])]

        def copy(u, k, block, to, own=False):
            a, i = u
            px, py, pc = block
            dst = rows(outs[a].at[4 * px + 2 * py + pc], a, i)
            return _rcopy(rows(ins[a], a, i) if own else dst, dst, send_sems.at[a, k, i], recv_sems.at[a, k, i], to)

        started = []

        def start(cp):
            cp.start()
            started.append(cp)

        def landed_then_pass_on(u, k, block):
            copy(u, k, block, me).wait_recv()
            start(copy(u, 3 + k, block, sib))

        mine = [pltpu.make_async_copy(ins[a], outs[a].at[4 * x + 2 * y + c], loc_sems.at[a]) for a in range(n)]
        for cp in mine:
            cp.start()
        for u in units:
            start(copy(u, 0, me, sib, own=True))
        for a in range(n):
            if len(pieces[a]) == 2:
                for i, to, k in ((0, xn, 1), (1, yn, 2), (1, xn, 1), (0, yn, 2)):
                    start(copy((a, i), k, me, to, own=True))
            else:
                for to, k in ((xn, 1), (yn, 2), (dg, 3)):
                    start(copy((a, 0), k, me, to, own=True))
        for a in range(n):
            if len(pieces[a]) == 2:
                landed_then_pass_on((a, 0), 1, xn)
                start(copy((a, 0), 3, xn, yn))
                landed_then_pass_on((a, 1), 2, yn)
                start(copy((a, 1), 3, yn, xn))
                landed_then_pass_on((a, 1), 1, xn)
                landed_then_pass_on((a, 0), 2, yn)
                landed_then_pass_on((a, 0), 3, dg)
                landed_then_pass_on((a, 1), 3, dg)
            else:
                for block, k in ((xn, 1), (yn, 2), (dg, 3)):
                    landed_then_pass_on((a, 0), k, block)
        for u in units:
            copy(u, 0, sib, me).wait_recv()
            for k, (px, py, _) in ((4, xn), (5, yn), (6, dg)):
                copy(u, k, (px, py, 1 - c), me).wait_recv()
        for cp in started:
            cp.wait_send()
        for cp in mine:
            cp.wait()

    n_pc = max(len(p) for p in pieces)

    return pl.pallas_call(
        body, name=name,
        out_shape=tuple(jax.ShapeDtypeStruct((N_DEV,) + a.shape, a.dtype) for a in arrs),
        in_specs=[_ANY] * n,
        out_specs=tuple([_ANY] * n),
        scratch_shapes=[pltpu.SemaphoreType.DMA((n, 7, n_pc)), pltpu.SemaphoreType.DMA((n, 7, n_pc)),
                        pltpu.SemaphoreType.DMA((n,))],
    )(*arrs)


def _pair_add_call(mine, recv, name):
    n, R, C = mine.shape
    (tr, tc), (steps,), idx = _tiling_2d(R, C, 1024)

    def body(p_ref, r_ref, o_ref):
        o_ref[...] = (p_ref[...].astype(F32) + r_ref[...].astype(F32)).astype(o_ref.dtype)

    blk = pl.BlockSpec((None, tr, tc), lambda q, i: (q,) + idx(i))
    return pl.pallas_call(
        body, name=name,
        grid=(n, steps),
        in_specs=[blk, blk],
        out_specs=blk,
        out_shape=jax.ShapeDtypeStruct(mine.shape, mine.dtype),
        compiler_params=_cparams(("arbitrary", "arbitrary")),
    )(mine, recv)


def _flatten_blocks_call(blocks):
    n, R, C = blocks.shape
    tc = C // 2

    def body(in_ref, out_ref):
        for p in range(n):
            out_ref[p * R:(p + 1) * R, :] = in_ref[p]

    return pl.pallas_call(
        body, name="flatten_w",
        grid=(C // tc,),
        in_specs=[pl.BlockSpec((n, R, tc), lambda i: (0, 0, i))],
        out_specs=pl.BlockSpec((n * R, tc), lambda i: (0, i)),
        out_shape=jax.ShapeDtypeStruct((n * R, C), blocks.dtype),
        compiler_params=_cparams(("arbitrary",)),
    )(blocks)


_PARTS_BANDS = 4


def _parts_pair_call(dmain, drank):
    D = dmain.shape[1]
    tc = D // _PARTS_BANDS

    def body(dm_ref, dr_ref, mine_ref, recv_ref, laid, loc_sems, send_sems, recv_sems):
        x, y, c = _mesh_pos()

        def copies(k):
            cols = pl.ds(k * tc, tc)
            stay = [pltpu.make_async_copy(laid.at[k % 2, 2 * q + c], mine_ref.at[q, pl.ds(0, SHARD_COLS), cols],
                                          loc_sems.at[k, q]) for q in range(4)]
            go = [_rcopy(laid.at[k % 2, 2 * q + (1 - c)], recv_ref.at[q, pl.ds(0, SHARD_COLS), cols],
                         send_sems.at[k, q], recv_sems.at[k, q], (x, y, 1 - c)) for q in range(4)]
            return stay, go

        def wait_sent(k):
            stay, go = copies(k)
            for cp in stay:
                cp.wait()
            for cp in go:
                cp.wait_send()

        for k in range(_PARTS_BANDS):
            @pl.when(pl.program_id(0) == k)
            def _(k=k):
                if k >= 2:
                    wait_sent(k - 2)
                for p in range(N_DEV):
                    lo, hi = p * SHARD_COLS, (p + 1) * SHARD_COLS
                    at = 0
                    for src, a, b in ((dm_ref, lo, min(hi, RANK_COL)),
                                      (dr_ref, max(lo, RANK_COL) - RANK_COL,
                                       min(hi, RANK_COL + GLA_RANK) - RANK_COL),
                                      (dm_ref, max(lo, RANK_COL + GLA_RANK) - GLA_RANK, hi - GLA_RANK)):
                        if b > a:
                            laid[k % 2, p, at:at + (b - a), :] = src[a:b, :]
                            at += b - a
                stay, go = copies(k)
                for cp in stay + go:
                    cp.start()
                if k == _PARTS_BANDS - 1:
                    for k_open in range(max(0, k - 1), k + 1):
                        wait_sent(k_open)
                    for k_any in range(_PARTS_BANDS):
                        for cp in copies(k_any)[1]:
                            cp.wait_recv()

    sds = jax.ShapeDtypeStruct((4, SHARD_COLS, D), dmain.dtype)
    sems = pltpu.SemaphoreType.DMA((_PARTS_BANDS, 4))
    return pl.pallas_call(
        body, name="parts_pair",
        grid=(_PARTS_BANDS,),
        in_specs=[pl.BlockSpec((dmain.shape[0], tc), lambda i: (0, i)),
                  pl.BlockSpec((GLA_RANK, tc), lambda i: (0, i))],
        out_specs=(_ANY, _ANY),
        out_shape=(sds, sds),
        scratch_shapes=[pltpu.VMEM((2, N_DEV, SHARD_COLS, tc), dmain.dtype), sems, sems, sems],
        compiler_params=_cparams(("arbitrary",)),
    )(dmain, drank)


def _group_row(g):
    return GLA_RANK * (g * (1024 // GLA_RANK) + (g >= RANK_COL // 1024))


def _proj_call(x, norm_g, wt, wr, wp_part):
    T, D = x.shape
    tm = min(1024, T)
    assert tm % TBLK == 0
    n_i = T // tm

    def f_slot(j):
        return ((j >= 2).astype(jnp.int32) + (j >= 6).astype(jnp.int32)
                + (j >= 7).astype(jnp.int32) + (j >= 8).astype(jnp.int32))

    def b_slot(j):
        return (j >= 3).astype(jnp.int32) + (j >= 4).astype(jnp.int32) + (j >= 5).astype(jnp.int32)

    def body(x_ref, g_ref, w_ref, wr_ref, wp_ref, pf_ref, pb_ref, rank_ref, ht_ref, wpall_ref,
             h_scr, send_sems, recv_sems, loc_sem):
        i = pl.program_id(0)
        j = pl.program_id(1)
        own, pairs = _push_copies(wp_ref, wpall_ref, send_sems, recv_sems, loc_sem, scatter=False)

        @pl.when((i == 0) & (j == 0))
        def _():
            _push_start(own, pairs)

        @pl.when(j == 0)
        def _():
            xv = x_ref[...]
            r = lax.rsqrt(jnp.mean(xv * xv, axis=-1, keepdims=True) + EPS)
            h = (xv * r) * g_ref[...]
            hb = _bf(h)
            h_scr[...] = hb
            for b in range(tm // TBLK):
                ht_ref[b] = _bf(h[b * TBLK:(b + 1) * TBLK].T)
            rank_ref[...] = _dot_nt(hb, wr_ref[...])

        is_b = (j == 1) | ((j >= 3) & (j <= 5))

        @pl.when(is_b)
        def _():
            pb_ref[...] = _bf(_dot_nt(h_scr[...], w_ref[...]))

        @pl.when(jnp.logical_not(is_b))
        def _():
            pf_ref[...] = _dot_nt(h_scr[...], w_ref[...])

        @pl.when((i == n_i - 1) & (j == N_GROUPS - 1))
        def _():
            _push_wait(own, pairs)

    return pl.pallas_call(
        body, name="proj",
        grid=(n_i, N_GROUPS),
        in_specs=[pl.BlockSpec((tm, D), lambda i, j: (i, 0)),
                  pl.BlockSpec((1, D), lambda i, j: (0, 0)),
                  pl.BlockSpec((pl.Element(1024), pl.Element(D)), lambda i, j: (_group_row(j), 0)),
                  pl.BlockSpec((128, D), lambda i, j: (0, 0)),
                  _ANY],
        out_specs=(pl.BlockSpec((None, tm, 1024), lambda i, j: (f_slot(j), i, 0)),
                   pl.BlockSpec((None, tm, 1024), lambda i, j: (b_slot(j), i, 0)),
                   pl.BlockSpec((tm, 128), lambda i, j: (i, 0)),
                   pl.BlockSpec((tm // TBLK, D, TBLK), lambda i, j: (i, 0, 0)),
                   _ANY),
        out_shape=(jax.ShapeDtypeStruct((5, T, 1024), F32),
                   jax.ShapeDtypeStruct((4, T, 1024), BF16),
                   jax.ShapeDtypeStruct((T, 128), F32),
                   jax.ShapeDtypeStruct((T // TBLK, D, TBLK), BF16),
                   jax.ShapeDtypeStruct((N_DEV,) + wp_part.shape, wp_part.dtype)),
        scratch_shapes=[pltpu.VMEM((tm, D), BF16)] + _PUSH_SEMS,
        compiler_params=_cparams(("arbitrary", "arbitrary")),
    )(x, norm_g, wt, wr, wp_part)


GLA_STEP_CHUNKS = 4


def _gla_same_chunk(rows):
    return (_iota2(rows, rows, 0) & -GLA_CHUNK) == (_iota2(rows, rows, 1) & -GLA_CHUNK)


def _gla_chunk_terms(la, q, k, n_c):
    C = GLA_CHUNK
    rows = n_c * C
    low = _gla_same_chunk(rows) & (_iota2(rows, rows, 0) >= _iota2(rows, rows, 1))
    b = _tri_left(_bf(low.astype(F32)), la)
    bl = [b[(c + 1) * C - 1:(c + 1) * C, :] for c in range(n_c)]
    bl_rows = jnp.concatenate([jnp.broadcast_to(bl[c], (C, b.shape[1])) for c in range(n_c)], axis=0)
    eb = jnp.exp(b)
    enb = jnp.exp(-b)
    ebl_b = jnp.exp(bl_rows - b)
    scale = GLA_HK ** -0.5
    qe = q * eb * scale
    ke = k * enb
    kd = k * ebl_b
    return bl, eb, enb, ebl_b, qe, ke, kd


def _gla_fwd_call(projf, projb, rank, wdec, bdec):
    T = projf.shape[1]
    C = GLA_CHUNK
    n_chunks = T // C
    n_c = GLA_STEP_CHUNKS
    R = n_c * C
    assert n_chunks % n_c == 0

    def body(qk_ref, v_ref, rank_ref, wd_ref, bd_ref, o_ref, st_ref, la_ref, st_scr):
        @pl.when(pl.program_id(0) == 0)
        def _():
            st_scr[...] = jnp.zeros_like(st_scr)

        dec = _dot(_bf(rank_ref[...]), _bf(wd_ref[...])) + bd_ref[...]
        la = (jnp.minimum(dec, 0.0) - _softplus_neg_abs(dec)) / GLA_TAU
        la_ref[...] = la
        mask = _gla_same_chunk(R) & (_iota2(R, R, 0) >= _iota2(R, R, 1))
        bl, _, _, _, qe, ke, kd = _gla_chunk_terms(la, qk_ref[:, :GLA_DK], qk_ref[:, GLA_DK:], n_c)
        qeb, keb, kdb = _bf(qe), _bf(ke), _bf(kd)
        ebl = [jnp.exp(bl[c]) for c in range(n_c)]
        heads = range(GLA_HEADS)
        ks = [slice(hh * GLA_HK, (hh + 1) * GLA_HK) for hh in heads]
        vs = [slice(hh * GLA_HV, (hh + 1) * GLA_HV) for hh in heads]
        rs = [slice(c * C, (c + 1) * C) for c in range(n_c)]
        p = [_bf(jnp.where(mask, _dot_nt(qeb[:, ks[hh]], keb[:, ks[hh]]), 0.0)) for hh in heads]
        upd = [[_dot_tn(v_ref[rs[c], vs[hh]], kdb[rs[c], ks[hh]]) for hh in heads] for c in range(n_c)]
        intra = [_dot(p[hh], v_ref[:, vs[hh]]) for hh in heads]
        st = [st_scr[hh] for hh in heads]
        for c in range(n_c):
            inter = [_dot_nt(qeb[rs[c], ks[hh]], _bf(st[hh])) for hh in heads]
            for hh in heads:
                st_ref[c, hh] = st[hh]
                o_ref[rs[c], vs[hh]] = intra[hh][rs[c]] + inter[hh]
            st = [st[hh] * ebl[c][:, ks[hh]] + upd[c][hh] for hh in heads]
        for hh in heads:
            st_scr[hh] = st[hh]

    return pl.pallas_call(
        body, name="gla_fwd",
        grid=(n_chunks // n_c,),
        in_specs=[pl.BlockSpec((None, R, 1024), lambda n: (0, n, 0)),
                  pl.BlockSpec((None, R, 1024), lambda n: (0, n, 0)),
                  pl.BlockSpec((R, 128), lambda n: (n, 0)),
                  pl.BlockSpec((128, GLA_DK), lambda n: (0, 0)),
                  pl.BlockSpec((1, GLA_DK), lambda n: (0, 0))],
        out_specs=(pl.BlockSpec((R, 1024), lambda n: (n, 0)),
                   pl.BlockSpec((n_c, GLA_HEADS, GLA_HV, GLA_HK), lambda n: (n, 0, 0, 0)),
                   pl.BlockSpec((R, GLA_DK), lambda n: (n, 0))),
        out_shape=(jax.ShapeDtypeStruct((T, 1024), F32),
                   jax.ShapeDtypeStruct((n_chunks, GLA_HEADS, GLA_HV, GLA_HK), F32),
                   jax.ShapeDtypeStruct((T, GLA_DK), F32)),
        scratch_shapes=[pltpu.VMEM((GLA_HEADS, GLA_HV, GLA_HK), F32)],
        compiler_params=_cparams(("arbitrary",)),
    )(projf, projb, rank, wdec, bdec)


def _gla_bwd_call(projf, projb, la, do_gla, st_all, rank, wdec):
    T = projf.shape[1]
    C = GLA_CHUNK
    n_chunks = T // C
    n_c = GLA_STEP_CHUNKS
    R = n_c * C
    assert n_chunks % n_c == 0
    last = n_chunks // n_c - 1

    def body(qk_ref, v_ref, la_ref, do_ref, st_ref, rank_ref, wd_ref,
             dqk_ref, dv_ref, drank_ref, dwd_ref, dbd_ref, dst_scr):
        @pl.when(pl.program_id(0) == 0)
        def _():
            dst_scr[...] = jnp.zeros_like(dst_scr)
            dwd_ref[...] = jnp.zeros_like(dwd_ref)
            dbd_ref[...] = jnp.zeros_like(dbd_ref)

        same = _gla_same_chunk(R)
        mask = same & (_iota2(R, R, 0) >= _iota2(R, R, 1))
        upp = _bf((same & (_iota2(R, R, 0) <= _iota2(R, R, 1))).astype(F32))
        scale = GLA_HK ** -0.5
        la = la_ref[...]
        bl, eb, enb, ebl_b, qe, ke, kd = _gla_chunk_terms(la, qk_ref[:, :GLA_DK], qk_ref[:, GLA_DK:], n_c)
        qeb, keb, kdb = _bf(qe), _bf(ke), _bf(kd)
        ebl = [jnp.exp(bl[c]) for c in range(n_c)]
        heads = range(GLA_HEADS)
        ks = [slice(hh * GLA_HK, (hh + 1) * GLA_HK) for hh in heads]
        vs = [slice(hh * GLA_HV, (hh + 1) * GLA_HV) for hh in heads]
        rs = [slice(c * C, (c + 1) * C) for c in range(n_c)]
        v = [v_ref[:, vs[hh]] for hh in heads]
        do = [_bf(do_ref[:, vs[hh]]) for hh in heads]
        p = [_bf(jnp.where(mask, _dot_nt(qeb[:, ks[hh]], keb[:, ks[hh]]), 0.0)) for hh in heads]
        dp = [_bf(jnp.where(mask, _dot_nt(do[hh], v[hh]), 0.0)) for hh in heads]
        dst_intra = [[_dot_tn(do[hh][rs[c]], qeb[rs[c], ks[hh]]) for hh in heads] for c in range(n_c)]
        dqe_inter = [[_dot(do[hh][rs[c]], _bf(st_ref[c, hh])) for hh in heads] for c in range(n_c)]
        dv_intra = [_dot_tn(p[hh], do[hh]) for hh in heads]
        dqe_intra = [_dot(dp[hh], keb[:, ks[hh]]) for hh in heads]
        dke = jnp.concatenate([_dot_tn(dp[hh], qeb[:, ks[hh]]) for hh in heads], axis=1)
        dstn = [dst_scr[hh] for hh in heads]
        dkd_c, dv_inter, debl = [None] * n_c, [None] * n_c, [None] * n_c
        for c in reversed(range(n_c)):
            dstnb = [_bf(dstn[hh]) for hh in heads]
            dkd_c[c] = jnp.concatenate([_dot(v[hh][rs[c]], dstnb[hh]) for hh in heads], axis=1)
            dv_inter[c] = [_dot_nt(kdb[rs[c], ks[hh]], dstnb[hh]) for hh in heads]
            debl[c] = jnp.concatenate(
                [jnp.sum(dstn[hh] * st_ref[c, hh], axis=0, keepdims=True) for hh in heads], axis=1)
            dstn = [dst_intra[c][hh] + dstn[hh] * ebl[c][:, ks[hh]] for hh in heads]
        for hh in heads:
            dst_scr[hh] = dstn[hh]
            dv_ref[:, vs[hh]] = _bf(dv_intra[hh] + jnp.concatenate([dv_inter[c][hh] for c in range(n_c)], axis=0))
        dqe = jnp.concatenate(
            [dqe_intra[hh] + jnp.concatenate([dqe_inter[c][hh] for c in range(n_c)], axis=0) for hh in heads], axis=1)
        dkd = jnp.concatenate(dkd_c, axis=0)
        dkd_kd = dkd * kd
        db = dqe * qe - dke * ke - dkd_kd
        dbl = jnp.concatenate(
            [jnp.broadcast_to(jnp.sum(dkd_kd[rs[c]], axis=0, keepdims=True) + ebl[c] * debl[c], (C, GLA_DK))
             for c in range(n_c)], axis=0)
        dla = _tri_left(upp, db) + dbl
        dqk_ref[:, :GLA_DK] = _bf(dqe * eb * scale)
        dqk_ref[:, GLA_DK:] = _bf(dke * enb + dkd * ebl_b)
        ddec = dla * (1.0 / GLA_TAU) * (1.0 - jnp.exp(GLA_TAU * la))
        ddecb = _bf(ddec)
        drank_ref[...] = _bf(_dot_nt(ddecb, _bf(wd_ref[...])))
        dwd_ref[...] += _dot_tn(_bf(rank_ref[...]), ddecb)
        dbd_ref[...] += jnp.sum(ddec, axis=0, keepdims=True)

    return pl.pallas_call(
        body, name="gla_bwd",
        grid=(n_chunks // n_c,),
        in_specs=[pl.BlockSpec((None, R, 1024), lambda n: (0, last - n, 0)),
                  pl.BlockSpec((None, R, 1024), lambda n: (0, last - n, 0)),
                  pl.BlockSpec((R, GLA_DK), lambda n: (last - n, 0)),
                  pl.BlockSpec((R, 1024), lambda n: (last - n, 0)),
                  pl.BlockSpec((n_c, GLA_HEADS, GLA_HV, GLA_HK), lambda n: (last - n, 0, 0, 0)),
                  pl.BlockSpec((R, 128), lambda n: (last - n, 0)),
                  pl.BlockSpec((128, GLA_DK), lambda n: (0, 0))],
        out_specs=(pl.BlockSpec((R, 1024), lambda n: (last - n, 0)),
                   pl.BlockSpec((R, 1024), lambda n: (last - n, 0)),
                   pl.BlockSpec((R, 128), lambda n: (last - n, 0)),
                   pl.BlockSpec((128, GLA_DK), lambda n: (0, 0)),
                   pl.BlockSpec((1, GLA_DK), lambda n: (0, 0))),
        out_shape=(jax.ShapeDtypeStruct((T, 1024), BF16),
                   jax.ShapeDtypeStruct((T, 1024), BF16),
                   jax.ShapeDtypeStruct((T, 128), BF16),
                   jax.ShapeDtypeStruct((128, GLA_DK), F32),
                   jax.ShapeDtypeStruct((1, GLA_DK), F32)),
        scratch_shapes=[pltpu.VMEM((GLA_HEADS, GLA_HV, GLA_HK), F32)],
        compiler_params=_cparams(("arbitrary",)),
    )(projf, projb, la, do_gla, st_all, rank, wdec)


def _sb_logs(z):
    lsz = jnp.minimum(z, 0.0) - _softplus_neg_abs(z)
    return lsz, lsz - z


SB_HG_FWD = 8
SB_HG_BWD = 4
SB_QUERIES = 256
SB_KEYS = 256
SB_DEAD = -105.0


def _sb_fwd_call(projb, wp_shard):
    T = projb.shape[1]
    B = min(SB_QUERIES, T)
    HG = SB_HG_FWD
    W = HG * SB_HD
    scale = 1.0 / math.sqrt(SB_HD)
    KB = min(SB_KEYS, T)
    n_h, n_i = SB_HEADS // HG, T // B

    def body(q_ref, k_ref, v_ref, wp_ref, o_ref, wpall_ref, cb_scr, send_sems, recv_sems, loc_sem):
        i = pl.program_id(1)
        own, pairs = _push_copies(wp_ref, wpall_ref, send_sems, recv_sems, loc_sem, scatter=False)

        @pl.when((pl.program_id(0) == 0) & (i == 0))
        def _():
            _push_start(own, pairs)

        rows = HG * B
        after = (_iota2(KB, KB, 0) > _iota2(KB, KB, 1)).astype(F32)
        tri = _bf(jnp.concatenate([after, jnp.ones((KB, KB), F32)], axis=1))
        o_ref[...] = jnp.zeros_like(o_ref)
        cb_scr[...] = jnp.zeros_like(cb_scr)

        def block(jp, masked):
            off = pl.multiple_of(jp * KB, KB)
            z = jnp.concatenate(
                [_dot_nt(q_ref[:, hh * SB_HD:(hh + 1) * SB_HD], k_ref[pl.ds(off, KB), hh * SB_HD:(hh + 1) * SB_HD])
                 for hh in range(HG)], axis=0) * scale
            lsz, l1m = _sb_logs(z)
            if masked:
                strict = (jp * KB + _iota2(rows, KB, 1)) < (i * B + (_iota2(rows, KB, 0) & (B - 1)))
                l1m = jnp.where(strict, l1m, 0.0)
            r = _tri2_right(l1m, tri)
            cb = cb_scr[...]
            a = jnp.exp(lsz + cb + r[:, :KB])
            if masked:
                a = jnp.where(strict, a, 0.0)
            cb_scr[...] = cb + r[:, KB:]
            ab = _bf(a)
            for hh in range(HG):
                cs = slice(hh * SB_HD, (hh + 1) * SB_HD)
                o_ref[:, cs] += _dot(ab[hh * B:(hh + 1) * B, :], v_ref[pl.ds(off, KB), cs])

        jp0 = (i * B) // KB
        block(jp0, True)

        def live(state):
            jj, dead = state
            return (jj <= jp0) & jnp.logical_not(dead)

        def step(state):
            jj, _ = state
            block(jp0 - jj, False)
            return jj + 1, jnp.max(cb_scr[:, :SB_HD]) < SB_DEAD

        lax.while_loop(live, step, (jnp.int32(1), jnp.max(cb_scr[:, :SB_HD]) < SB_DEAD))

        @pl.when((pl.program_id(0) == n_h - 1) & (i == n_i - 1))
        def _():
            _push_wait(own, pairs)

    return pl.pallas_call(
        body, name="sb_fwd",
        grid=(n_h, n_i),
        in_specs=[pl.BlockSpec((None, B, W), lambda h, i: (1, i, h)),
                  pl.BlockSpec((None, T, W), lambda h, i: (2, 0, h)),
                  pl.BlockSpec((None, T, W), lambda h, i: (3, 0, h)),
                  _ANY],
        out_specs=(pl.BlockSpec((B, W), lambda h, i: (i, h)), _ANY),
        out_shape=(jax.ShapeDtypeStruct((T, 1024), F32),
                   jax.ShapeDtypeStruct((N_DEV,) + wp_shard.shape, wp_shard.dtype)),
        scratch_shapes=[pltpu.VMEM((HG * B, KB), F32)] + _PUSH_SEMS,
        compiler_params=_cparams(("arbitrary", "arbitrary")),
    )(projb, projb, projb, wp_shard)


def _sb_bwd_call(projb, do_sb, g_p):
    T = projb.shape[1]
    B = min(SB_QUERIES, T)
    nb = T // B
    HG = SB_HG_BWD
    W = HG * SB_HD
    WQ = HG * B
    KB = min(SB_KEYS, T)
    nkb = T // KB
    n_h = SB_HEADS // HG
    scale = 1.0 / math.sqrt(SB_HD)

    def body(q_ref, k_ref, v_ref, do_ref, gp_ref, dq_ref, dk_ref, dv_ref, rp_ref,
             dk_scr, dv_scr, kt_scr, beta_scr, g_scr, dqt_scr, send_sems, recv_sems, loc_sem):
        i = pl.program_id(1)
        own, pairs = _push_copies(gp_ref, rp_ref, send_sems, recv_sems, loc_sem, scatter=True)

        @pl.when((pl.program_id(0) == 0) & (i == 0))
        def _():
            _push_start(own, pairs)

        @pl.when(i == 0)
        def _():
            dk_scr[...] = jnp.zeros_like(dk_scr)
            dv_scr[...] = jnp.zeros_like(dv_scr)
            for hh in range(HG):
                for jb in range(nkb):
                    kt_scr[hh, jb] = _bf(
                        k_ref[jb * KB:(jb + 1) * KB, hh * SB_HD:(hh + 1) * SB_HD].astype(F32).T)

        dqt_scr[...] = jnp.zeros_like(dqt_scr)
        later = _bf((_iota2(KB, KB, 1) > _iota2(KB, KB, 0)).astype(F32))
        earlier = _bf((_iota2(KB, KB, 1) < _iota2(KB, KB, 0)).astype(F32))
        dob = _bf(do_ref[...])
        jp0 = (i * B) // KB

        def strict_mask():
            return (jp0 * KB + _iota2(KB, WQ, 0)) < (i * B + (_iota2(KB, WQ, 1) & (B - 1)))

        def heads(fn):
            return [fn(slice(hh * SB_HD, (hh + 1) * SB_HD)) for hh in range(HG)]

        def pass1(jp, cb, masked):
            off = pl.multiple_of(jp * KB, KB)
            z = jnp.concatenate(heads(lambda cs: _dot_nt(k_ref[pl.ds(off, KB), cs], q_ref[:, cs])), axis=1) * scale
            da = jnp.concatenate(heads(lambda cs: _dot_nt(v_ref[pl.ds(off, KB), cs], dob[:, cs])), axis=1)
            lsz, l1m = _sb_logs(z)
            if masked:
                strict = strict_mask()
                l1m = jnp.where(strict, l1m, 0.0)
            a = jnp.exp(lsz + cb + _tri2_left(later, l1m))
            if masked:
                a = jnp.where(strict, a, 0.0)
            g_scr[jp] = a * da
            beta_scr[jp] = jnp.exp(lsz)
            ab = _bf(a)
            for hh in range(HG):
                cs = slice(hh * SB_HD, (hh + 1) * SB_HD)
                dv_scr[pl.ds(off, KB), cs] += _dot(ab[:, hh * B:(hh + 1) * B], dob[:, cs])
            return cb + jnp.sum(l1m, axis=0, keepdims=True)

        zero = jnp.zeros((1, WQ), F32)
        cb = pass1(jp0, zero, True)

        def live(state):
            jj, _, dead = state
            return (jj <= jp0) & jnp.logical_not(dead)

        def step(state):
            jj, cr, _ = state
            cr = pass1(jp0 - jj, cr, False)
            return jj + 1, cr, jnp.max(cr) < SB_DEAD

        n_done, _, _ = lax.while_loop(live, step, (jnp.int32(1), cb, jnp.max(cb) < SB_DEAD))
        jp_first = jp0 - (n_done - 1)

        def pass2(jp, cg, masked):
            off = pl.multiple_of(jp * KB, KB)
            g = g_scr[jp]
            beta = beta_scr[jp]
            dz = g * (1.0 - beta) - beta * (cg + _tri2_left(earlier, g))
            if masked:
                dz = jnp.where(strict_mask(), dz, 0.0)
            dzb = _bf(dz * scale)
            for hh in range(HG):
                cs = slice(hh * SB_HD, (hh + 1) * SB_HD)
                dk_scr[pl.ds(off, KB), cs] += _dot(dzb[:, hh * B:(hh + 1) * B], q_ref[:, cs])
                dqt_scr[hh] += _dot(kt_scr[hh, jp], dzb[:, hh * B:(hh + 1) * B])
            return cg + jnp.sum(g, axis=0, keepdims=True)

        cg = lax.fori_loop(jp_first, jp0, lambda jp, cr: pass2(jp, cr, False), zero)
        pass2(jp0, cg, True)
        for hh in range(HG):
            dq_ref[:, hh * SB_HD:(hh + 1) * SB_HD] = _bf(dqt_scr[hh].T)

        @pl.when(i == nb - 1)
        def _():
            dk_ref[...] = _bf(dk_scr[...])
            dv_ref[...] = _bf(dv_scr[...])

        @pl.when((pl.program_id(0) == n_h - 1) & (i == nb - 1))
        def _():
            _push_wait(own, pairs)

    return pl.pallas_call(
        body, name="sb_bwd",
        grid=(n_h, nb),
        in_specs=[pl.BlockSpec((None, B, W), lambda h, i: (1, i, h)),
                  pl.BlockSpec((None, T, W), lambda h, i: (2, 0, h)),
                  pl.BlockSpec((None, T, W), lambda h, i: (3, 0, h)),
                  pl.BlockSpec((B, W), lambda h, i: (i, h)),
                  _ANY],
        out_specs=(pl.BlockSpec((B, W), lambda h, i: (i, h)),
                   pl.BlockSpec((T, W), lambda h, i: (0, h)),
                   pl.BlockSpec((T, W), lambda h, i: (0, h)),
                   _ANY),
        out_shape=(jax.ShapeDtypeStruct((T, 1024), BF16),
                   jax.ShapeDtypeStruct((T, 1024), BF16),
                   jax.ShapeDtypeStruct((T, 1024), BF16),
                   jax.ShapeDtypeStruct(g_p.shape, g_p.dtype)),
        scratch_shapes=[pltpu.VMEM((T, W), F32), pltpu.VMEM((T, W), F32),
                        pltpu.VMEM((HG, nkb, SB_HD, KB), BF16),
                        pltpu.VMEM((nkb, KB, WQ), F32), pltpu.VMEM((nkb, KB, WQ), F32),
                        pltpu.VMEM((HG, SB_HD, B), F32)] + _PUSH_SEMS,
        compiler_params=_cparams(("arbitrary", "arbitrary")),
    )(projb, projb, projb, do_sb, g_p)


def _mid_call(o_gla, o_sb, projf, x, target, wpa, wpb, wo, gla_g, b_gate, final_g):
    T, D = x.shape
    tm = min(TBLK, T)

    def body(og_ref, ggate_ref, osb_ref, sgate_ref, ma_ref, mb_ref, x_ref, tgt_ref,
             wpa_ref, wpb_ref, wo_ref, glag_ref, bg_ref, fg_ref,
             dx2_ref, dogla_ref, dosb_ref, dggate_ref, dsgate_ref, dm_ref,
             mt_ref, ogt_ref, obt_ref, dx2b_ref, dya_ref, dyb_ref,
             dfg_ref, dbg_ref, dglag_ref, loss_ref):
        @pl.when(pl.program_id(0) == 0)
        def _():
            dfg_ref[...] = jnp.zeros_like(dfg_ref)
            dbg_ref[...] = jnp.zeros_like(dbg_ref)
            dglag_ref[...] = jnp.zeros_like(dglag_ref)
            loss_ref[...] = jnp.zeros_like(loss_ref)

        glag = glag_ref[...]
        ggate = ggate_ref[...]
        sg = _sigmoid(ggate)
        silu_g = ggate * sg
        ohat, rinv, nrm = [], [], []
        for hh in range(GLA_HEADS):
            oh = og_ref[:, hh * GLA_HV:(hh + 1) * GLA_HV]
            r = lax.rsqrt(jnp.mean(oh * oh, axis=-1, keepdims=True) + EPS)
            ohat.append(oh * r)
            rinv.append(r)
            nrm.append(ohat[-1] * glag)
        n_all = jnp.concatenate(nrm, axis=1)
        og = n_all * silu_g
        ogb = _bf(og)
        ya = _dot(ogb, wpa_ref[...])
        sgate = sgate_ref[...]
        ss = _sigmoid(sgate)
        silu_s = sgate * ss
        osb = osb_ref[...]
        ob = osb * silu_s
        obb = _bf(ob)
        yb = _dot(obb, wpb_ref[...])
        ga = _sigmoid(ma_ref[...] + bg_ref[:, :D])
        gb = _sigmoid(mb_ref[...] + bg_ref[:, D:])
        merged = ga * ya + gb * yb
        mgb = _bf(merged)
        x2 = x_ref[...] + _dot(mgb, wo_ref[...])
        r2 = lax.rsqrt(jnp.mean(x2 * x2, axis=-1, keepdims=True) + EPS)
        xh2 = x2 * r2
        fg = fg_ref[...]
        err = xh2 * fg - tgt_ref[...]
        loss_ref[...] += jnp.broadcast_to(
            0.5 * jnp.sum(jnp.mean(err * err, axis=-1, keepdims=True), axis=0, keepdims=True), (1, 128))
        dy = err * (1.0 / D)
        dfg_ref[...] += jnp.sum(dy * xh2, axis=0, keepdims=True)
        dxh = dy * fg
        dx2 = r2 * (dxh - xh2 * jnp.mean(dxh * xh2, axis=-1, keepdims=True))
        dx2_ref[...] = dx2
        dx2b = _bf(dx2)
        dx2b_ref[...] = dx2b
        dmerged = _dot_nt(dx2b, wo_ref[...])
        dya = dmerged * ga
        dyb = dmerged * gb
        dma = dmerged * ya * ga * (1.0 - ga)
        dmb = dmerged * yb * gb * (1.0 - gb)
        dm_ref[:, :D] = _bf(dma)
        dm_ref[:, D:] = _bf(dmb)
        dbg_ref[:, :D] += jnp.sum(dma, axis=0, keepdims=True)
        dbg_ref[:, D:] += jnp.sum(dmb, axis=0, keepdims=True)
        dyab = _bf(dya)
        dybb = _bf(dyb)
        dya_ref[...] = dyab
        dyb_ref[...] = dybb
        dog = _dot_nt(dyab, wpa_ref[...])
        dob = _dot_nt(dybb, wpb_ref[...])
        dosb_ref[...] = dob * silu_s
        dsgate_ref[...] = _bf(dob * osb * (ss * (1.0 + sgate * (1.0 - ss))))
        dn = dog * silu_g
        dggate_ref[...] = _bf(dog * n_all * (sg * (1.0 + ggate * (1.0 - sg))))
        dglag = jnp.zeros((1, GLA_HV), F32)
        for hh in range(GLA_HEADS):
            dnh = dn[:, hh * GLA_HV:(hh + 1) * GLA_HV]
            dglag = dglag + jnp.sum(dnh * ohat[hh], axis=0, keepdims=True)
            dohat = dnh * glag
            dogla_ref[:, hh * GLA_HV:(hh + 1) * GLA_HV] = rinv[hh] * (
                dohat - ohat[hh] * jnp.mean(dohat * ohat[hh], axis=-1, keepdims=True))
        dglag_ref[...] += dglag
        mt_ref[...] = _bf(merged.T)
        ogt_ref[...] = _bf(og.T)
        obt_ref[...] = _bf(ob.T)

    row = lambda i: (i, 0)
    const = lambda i: (0, 0)
    tile = pl.BlockSpec((tm, D), row)
    tile_t = pl.BlockSpec((None, D, tm), lambda i: (i, 0, 0))
    wspec = pl.BlockSpec((D, D), const)
    return pl.pallas_call(
        body, name="mid",
        grid=(T // tm,),
        in_specs=[tile,
                  pl.BlockSpec((None, tm, D), lambda i: (1, i, 0)),
                  tile,
                  pl.BlockSpec((None, tm, D), lambda i: (2, i, 0)),
                  pl.BlockSpec((None, tm, D), lambda i: (3, i, 0)),
                  pl.BlockSpec((None, tm, D), lambda i: (4, i, 0)),
                  tile, tile, wspec, wspec, wspec,
                  pl.BlockSpec((1, GLA_HV), const),
                  pl.BlockSpec((1, 2 * D), const),
                  pl.BlockSpec((1, D), const)],
        out_specs=(tile, tile, tile, tile, tile,
                   pl.BlockSpec((tm, 2 * D), row),
                   tile_t, tile_t, tile_t, tile, tile, tile,
                   pl.BlockSpec((1, D), const),
                   pl.BlockSpec((1, 2 * D), const),
                   pl.BlockSpec((1, GLA_HV), const),
                   pl.BlockSpec((1, 128), const)),
        out_shape=(jax.ShapeDtypeStruct((T, D), F32),
                   jax.ShapeDtypeStruct((T, D), F32),
                   jax.ShapeDtypeStruct((T, D), F32),
                   jax.ShapeDtypeStruct((T, D), BF16),
                   jax.ShapeDtypeStruct((T, D), BF16),
                   jax.ShapeDtypeStruct((T, 2 * D), BF16),
                   jax.ShapeDtypeStruct((T // tm, D, tm), BF16),
                   jax.ShapeDtypeStruct((T // tm, D, tm), BF16),
                   jax.ShapeDtypeStruct((T // tm, D, tm), BF16),
                   jax.ShapeDtypeStruct((T, D), BF16),
                   jax.ShapeDtypeStruct((T, D), BF16),
                   jax.ShapeDtypeStruct((T, D), BF16),
                   jax.ShapeDtypeStruct((1, D), F32),
                   jax.ShapeDtypeStruct((1, 2 * D), F32),
                   jax.ShapeDtypeStruct((1, GLA_HV), F32),
                   jax.ShapeDtypeStruct((1, 128), F32)),
        compiler_params=_cparams(("arbitrary",)),
    )(o_gla, projf, o_sb, projf, projf, projf, x, target, wpa, wpb, wo, gla_g, b_gate, final_g)


def _dh_call(pieces, dmlog, drank, wt, wr, x, dx2, norm_g, s_in):
    T, D = x.shape
    tm = min(256, T)
    npc = len(pieces)
    n_main = N_GROUPS * 1024
    n_i = T // tm
    i_forward = 5 * n_i // 8

    def body(*refs):
        pcs = refs[:npc]
        (dm_ref, dr_ref, w_hbm, wr_ref, x_ref, dx2_ref, g_ref, sin_ref,
         gx_ref, dg_ref, rin_ref, relayed_ref, w_scr, sems, *exchange_scratch) = refs[npc:]
        start, forward, finish = _chip_reduce_steps(sin_ref, rin_ref, relayed_ref, *exchange_scratch)

        @pl.when(pl.program_id(0) == 0)
        def _():
            start()
            lo = pltpu.make_async_copy(w_hbm.at[pl.ds(0, RANK_COL)], w_scr.at[pl.ds(0, RANK_COL)], sems.at[0])
            hi = pltpu.make_async_copy(w_hbm.at[pl.ds(RANK_COL + GLA_RANK, n_main - RANK_COL)],
                                       w_scr.at[pl.ds(RANK_COL, n_main - RANK_COL)], sems.at[1])
            lo.start()
            hi.start()
            dg_ref[...] = jnp.zeros_like(dg_ref)
            lo.wait()
            hi.wait()

        @pl.when(pl.program_id(0) == i_forward)
        def _():
            forward()

        def w_group(g):
            return w_scr[g * 1024:(g + 1) * 1024, :]

        dr = dr_ref[...]
        dh = _dot(dr, wr_ref[...])
        for g in range(npc):
            dh = dh + _dot(pcs[g][...], w_group(g))
        dh = dh + _dot(dm_ref[:, :D], w_group(npc))
        dh = dh + _dot(dm_ref[:, D:], w_group(npc + 1))
        xv = x_ref[...]
        r = lax.rsqrt(jnp.mean(xv * xv, axis=-1, keepdims=True) + EPS)
        xhat = xv * r
        g = g_ref[...]
        dg_ref[...] += jnp.sum(dh * xhat, axis=0, keepdims=True)
        dxhat = dh * g
        gx_ref[...] = r * (dxhat - xhat * jnp.mean(dxhat * xhat, axis=-1, keepdims=True)) + dx2_ref[...]

        @pl.when(pl.program_id(0) == n_i - 1)
        def _():
            finish()

    row = lambda i: (i, 0)
    const = lambda i: (0, 0)
    tile = pl.BlockSpec((tm, D), row)
    part = s_in.shape[1:]
    return pl.pallas_call(
        body, name="dh",
        grid=(n_i,),
        in_specs=[tile] * npc + [
            pl.BlockSpec((tm, 2 * D), row),
            pl.BlockSpec((tm, 128), row),
            _ANY,
            pl.BlockSpec((128, D), const),
            tile, tile,
            pl.BlockSpec((1, D), const),
            _ANY],
        out_specs=(tile, pl.BlockSpec((1, D), const), _ANY, _ANY),
        out_shape=(jax.ShapeDtypeStruct((T, D), F32),
                   jax.ShapeDtypeStruct((1, D), F32),
                   jax.ShapeDtypeStruct((3,) + part, s_in.dtype),
                   jax.ShapeDtypeStruct(part, s_in.dtype)),
        scratch_shapes=[pltpu.VMEM((n_main, D), BF16), pltpu.SemaphoreType.DMA((2,))]
        + _chip_reduce_scratch(*part, s_in.dtype),
        compiler_params=_cparams(("arbitrary",)),
    )(*pieces, dmlog, drank, wt, wr, x, dx2, norm_g, s_in)


def _wgrad_rank_call(ht, drank):
    n_tb, D, tb = ht.shape

    def body(ht_ref, dr_ref, o_ref):
        @pl.when(pl.program_id(0) == 0)
        def _():
            o_ref[...] = jnp.zeros_like(o_ref)

        o_ref[...] += _dot(ht_ref[...], dr_ref[...])

    return pl.pallas_call(
        body, name="wgrad_rank",
        grid=(n_tb,),
        in_specs=[pl.BlockSpec((None, D, tb), lambda i: (i, 0, 0)),
                  pl.BlockSpec((tb, 128), lambda i: (i, 0))],
        out_specs=pl.BlockSpec((D, 128), lambda i: (0, 0)),
        out_shape=jax.ShapeDtypeStruct((D, 128), F32),
        compiler_params=_cparams(("arbitrary",)),
    )(ht, drank)


def _wgrad_call(lhs_list, lhs_of_group, rhs_list, rhs_of_group, n_transposed, name):
    n_groups = len(rhs_of_group)
    n_tb, D, tb = lhs_list[0].shape
    T = n_tb * tb
    per = min(4, n_tb)
    tk = per * tb
    nk = T // tk
    nl = len(lhs_list)

    def body(*refs):
        lhs = refs[:nl]
        rhs = refs[nl:nl + n_groups]
        out_ref, acc = refs[nl + n_groups:]
        g = pl.program_id(0)
        i = pl.program_id(1)

        @pl.when(i == 0)
        def _():
            acc[...] = jnp.zeros_like(acc)

        for p in range(n_groups):
            @pl.when(g == p)
            def _(p=p):
                lref = lhs[lhs_of_group[p]]
                part = _dot(lref[0], rhs[p][0:tb, :])
                for b in range(1, per):
                    part = part + _dot(lref[b], rhs[p][b * tb:(b + 1) * tb, :])
                acc[...] += part

        @pl.when((i == nk - 1) & (g < n_transposed))
        def _():
            out_ref[...] = _bf(acc[...].T)

        @pl.when((i == nk - 1) & (g >= n_transposed))
        def _():
            out_ref[...] = _bf(acc[...])

    def lhs_spec(a):
        groups = [g for g in range(n_groups) if lhs_of_group[g] == a]
        lo, hi = min(groups), max(groups)
        assert groups == list(range(lo, hi + 1))
        return pl.BlockSpec((per, D, tb), lambda g, i: (jnp.where((g >= lo) & (g <= hi), i, 0), 0, 0))

    def rhs_spec(p):
        cb = rhs_of_group[p][1]
        return pl.BlockSpec((tk, 1024), lambda g, i: (jnp.where(g == p, i, 0), cb))

    return pl.pallas_call(
        body, name=name,
        grid=(n_groups, nk),
        in_specs=[lhs_spec(a) for a in range(nl)] + [rhs_spec(p) for p in range(n_groups)],
        out_specs=pl.BlockSpec((None, D, 1024), lambda g, i: (g, 0, 0)),
        out_shape=jax.ShapeDtypeStruct((n_groups, D, 1024), BF16),
        scratch_shapes=[pltpu.VMEM((D, 1024), F32)],
        compiler_params=_cparams(("arbitrary", "arbitrary")),
    )(*lhs_list, *[rhs_list[rhs_of_group[p][0]] for p in range(n_groups)])


def _adamw_math(parts, w, m, v):
    g = parts[0].astype(F32)
    for p in parts[1:]:
        g = g + p.astype(F32)
    mm = ADAM_B1 * m + (1.0 - ADAM_B1) * g
    vv = ADAM_B2 * v + (1.0 - ADAM_B2) * (g * g)
    m_hat = mm / (1.0 - ADAM_B1 ** ADAM_STEP)
    v_hat = vv / (1.0 - ADAM_B2 ** ADAM_STEP)
    return g, -ADAM_LR * (m_hat / (jnp.sqrt(v_hat) + ADAM_EPS) + ADAM_WD * w), mm, vv


def _part_order(n_parts):
    return [n_parts - 1] + list(range(n_parts - 1))


def _adamw_call(parts, w, m, v, name):
    R, C = w.shape
    n_parts = parts.shape[0]
    (tr, tc), grid, idx = _tiling_2d(R, C, 512)

    def body(p_ref, w_ref, m_ref, v_ref, g_ref, d_ref, nm_ref, nv_ref):
        g_ref[...], d_ref[...], nm_ref[...], nv_ref[...] = _adamw_math(
            [p_ref[k] for k in _part_order(n_parts)], w_ref[...], m_ref[...], v_ref[...])

    blk = pl.BlockSpec((tr, tc), idx)
    sds = jax.ShapeDtypeStruct((R, C), F32)
    return pl.pallas_call(
        body, name=name,
        grid=grid,
        in_specs=[pl.BlockSpec((n_parts, tr, tc), lambda i: (0,) + idx(i)), blk, blk, blk],
        out_specs=(blk, blk, blk, blk),
        out_shape=(sds, sds, sds, sds),
        compiler_params=_cparams(("arbitrary",)),
    )(parts, w, m, v)


def _adamw_rows_call(parts, ws, ms, vs, name, gathered):
    n = len(ws)
    R, C = ws[0].shape
    n_parts = parts.shape[0]

    def body(*refs):
        p_ref = refs[0]
        w_refs, m_refs, v_refs = refs[1:1 + n], refs[1 + n:1 + 2 * n], refs[1 + 2 * n:1 + 3 * n]
        src_ref = refs[1 + 3 * n]
        outs = refs[2 + 3 * n:2 + 7 * n]
        dst_ref, send_sems, recv_sems, loc_sem = refs[2 + 7 * n:]
        own, pairs = _push_copies(src_ref, dst_ref, send_sems, recv_sems, loc_sem, scatter=False)
        k_now = pl.program_id(0)

        @pl.when(k_now == 0)
        def _():
            _push_start(own, pairs)

        for k in range(n):
            @pl.when(k_now == k)
            def _(k=k):
                res = _adamw_math([p_ref[j] for j in _part_order(n_parts)],
                                  w_refs[k][...], m_refs[k][...], v_refs[k][...])
                for o_ref, val in zip(outs[4 * k:4 * k + 4], res):
                    o_ref[...] = val

        @pl.when(k_now == n - 1)
        def _():
            _push_wait(own, pairs)

    whole = pl.BlockSpec((R, C), lambda k: (0, 0))
    sds = jax.ShapeDtypeStruct((R, C), F32)
    res = pl.pallas_call(
        body, name=name,
        grid=(n,),
        in_specs=[pl.BlockSpec((n_parts, R, C), lambda k: (0, k, 0))] + [whole] * (3 * n) + [_ANY],
        out_specs=tuple([whole] * (4 * n) + [_ANY]),
        out_shape=tuple([sds] * (4 * n) + [jax.ShapeDtypeStruct((N_DEV,) + gathered.shape, gathered.dtype)]),
        scratch_shapes=_PUSH_SEMS,
        compiler_params=_cparams(("arbitrary",)),
    )(parts, *ws, *ms, *vs, gathered)
    return [res[4 * k:4 * k + 4] for k in range(n)], res[4 * n]


def _adamw_lanes_call(parts, offsets, ws, ms, vs, name):
    n = len(ws)
    n_parts = parts.shape[0]

    def body(*refs):
        p_ref = refs[0]
        w_refs, m_refs, v_refs = refs[1:1 + n], refs[1 + n:1 + 2 * n], refs[1 + 2 * n:1 + 3 * n]
        outs = refs[1 + 3 * n:]
        for k in range(n):
            lanes = slice(offsets[k], offsets[k] + ws[k].shape[1])
            res = _adamw_math([p_ref[j, :, lanes] for j in _part_order(n_parts)],
                              w_refs[k][...], m_refs[k][...], v_refs[k][...])
            for o_ref, val in zip(outs[4 * k:4 * k + 4], res):
                o_ref[...] = val

    res = pl.pallas_call(
        body, name=name,
        out_shape=tuple(jax.ShapeDtypeStruct(ws[k].shape, F32) for k in range(n) for _ in range(4)),
        compiler_params=_cparams(),
    )(parts, *ws, *ms, *vs)
    return [res[4 * k:4 * k + 4] for k in range(n)]


def _local_step(x, target, wt, wr, wdec, bdec, wp_shard, norm_g, gla_g, b_gate, final_g):
    D = x.shape[1]
    half = wp_shard.shape[1] // 2
    projf, projb, rank, ht, wp_lo = _proj_call(x, norm_g, wt, wr, wp_shard[:, :half])
    o_gla, st_all, la = _gla_fwd_call(projf, projb, rank, wdec, bdec)
    o_sb, wp_hi = _sb_fwd_call(projb, wp_shard[:, half:])
    wp_full = jnp.concatenate([wp_lo, wp_hi], axis=2).transpose(1, 0, 2, 3).reshape(3, D, D)
    (dx2, do_gla, do_sb, dggate, dsgate, dmlog, mt, ogt, obt, dx2b, dya, dyb,
     dfinal_g, db_gate, dgla_g, loss) = _mid_call(o_gla, o_sb, projf, x, target, wp_full[0], wp_full[1],
                                                 wp_full[2], gla_g, b_gate, final_g)
    dw_p = _wgrad_call([ogt, obt, mt], [0, 1, 2], [dya, dyb, dx2b], [(0, 0), (1, 0), (2, 0)], 0, "wgrad_p")
    g_p = dw_p.reshape(3, N_DEV, D // N_DEV, D).transpose(1, 0, 2, 3).reshape(N_DEV, 3 * (D // N_DEV), D)
    dqk, dgv, drank, dwdec, dbdec = _gla_bwd_call(projf, projb, la, do_gla, st_all, rank, wdec)
    dsq, dsk, dsv, r_p = _sb_bwd_call(projb, do_sb, g_p)
    pieces = [dqk, dgv, dggate, dsq, dsk, dsv, dsgate]
    rhs_of_group = [(g, 0) for g in range(7)] + [(7, 0), (7, 1)]
    dw_in = _wgrad_call([ht], [0] * N_GROUPS, pieces + [dmlog], rhs_of_group, N_GROUPS, "wgrad_in")
    dwr = _wgrad_rank_call(ht, drank)
    g_in, p_in = _parts_pair_call(dw_in.reshape(N_GROUPS * 1024, D), dwr[:, :GLA_RANK].T.astype(BF16))
    s_in = _pair_add_call(g_in, p_in, "pair_add_in")
    grad_x, dnorm_g, r_in, _ = _dh_call(pieces, dmlog, drank, wt, wr, x, dx2, norm_g, s_in)
    small = jnp.concatenate([
        dnorm_g.reshape(-1), dbdec.reshape(-1), dgla_g.reshape(-1), db_gate.reshape(-1), dfinal_g.reshape(-1),
        loss.reshape(-1), dwdec[:GLA_RANK].reshape(-1)]).reshape(1, _SM_LEN)
    return grad_x, r_in, r_p, small


_SM_NORM = 0
_SM_BDEC = _SM_NORM + D_MODEL
_SM_GLAG = _SM_BDEC + GLA_DK
_SM_BGATE = _SM_GLAG + GLA_HV
_SM_FINAL = _SM_BGATE + 2 * D_MODEL
_SM_REPL = _SM_FINAL + D_MODEL
_SM_LOSS = _SM_REPL
_SM_WDEC = _SM_LOSS + 128
_SM_LEN = _SM_WDEC + GLA_RANK * GLA_DK


def kernel(x, norm_g, w_in, w_dec_up, b_dec, gla_norm_g, w_pa, w_pb, b_gate, w_o, final_g, loss_target, m_norm_g, m_w_in, m_w_dec_up, m_b_dec, m_gla_norm_g, m_w_pa, m_w_pb, m_b_gate, m_w_o, m_final_g, v_norm_g, v_w_in, v_w_dec_up, v_b_dec, v_gla_norm_g, v_w_pa, v_w_pb, v_b_gate, v_w_o, v_final_g):
    D = D_MODEL
    me = 4 * lax.axis_index("x") + 2 * lax.axis_index("y") + lax.axis_index("c")

    wp_shard = jnp.stack([w_pa, w_pb, w_o]).astype(BF16)
    n_first = _half_rows(SHARD_COLS)
    win_all, wdec_all = _all_gather([w_in.T.astype(BF16), w_dec_up], "gather_w",
                                    row_pieces=[[(0, n_first), (n_first, SHARD_COLS - n_first)], None])
    wt = _flatten_blocks_call(win_all)
    wr = jnp.pad(wt[RANK_COL:RANK_COL + GLA_RANK], ((0, 128 - GLA_RANK), (0, 0)))
    wdec_full = wdec_all.transpose(1, 0, 2).reshape(GLA_RANK, GLA_DK)
    wdec = jnp.pad(wdec_full, ((0, 128 - GLA_RANK), (0, 0)))

    grad_x, r_in, r_p, small = _local_step(
        x[0], loss_target[0], wt, wr, wdec, b_dec.reshape(1, -1), wp_shard,
        norm_g.reshape(1, -1), gla_norm_g.reshape(1, -1), b_gate.reshape(1, -1), final_g.reshape(1, -1))

    gw_in, d_in, nm_in, nv_in = (a.T for a in _adamw_call(r_in, w_in.T, m_w_in.T, v_w_in.T, "adamw_in"))
    ((g_pa, d_pa, nm_pa, nv_pa), (g_pb, d_pb, nm_pb, nv_pb), (g_o, d_o, nm_o, nv_o)), r_small = _adamw_rows_call(
        r_p, [w_pa, w_pb, w_o], [m_w_pa, m_w_pb, m_w_o], [v_w_pa, v_w_pb, v_w_o], "adamw_p", small)

    def row(a):
        return a.reshape(1, -1)

    rep = _adamw_lanes_call(
        r_small, [_SM_NORM, _SM_BDEC, _SM_GLAG, _SM_BGATE, _SM_FINAL],
        [row(a) for a in (norm_g, b_dec, gla_norm_g, b_gate, final_g)],
        [row(a) for a in (m_norm_g, m_b_dec, m_gla_norm_g, m_b_gate, m_final_g)],
        [row(a) for a in (v_norm_g, v_b_dec, v_gla_norm_g, v_b_gate, v_final_g)], "adamw_rep")
    ((g_norm, d_norm, nm_norm, nv_norm), (g_bdec, d_bdec, nm_bdec, nv_bdec), (g_glag, d_glag, nm_glag, nv_glag),
     (g_bgate, d_bgate, nm_bgate, nv_bgate), (g_final, d_final, nm_final, nv_final)) = [
        tuple(a.reshape(-1) for a in quad) for quad in rep]

    wdec_parts = r_small[:, 0, _SM_WDEC:].reshape(N_DEV, GLA_RANK, GLA_DK)
    cols = GLA_DK // N_DEV
    wdec_mine = lax.dynamic_slice_in_dim(wdec_parts, me * cols, cols, axis=2)
    g_wdec, d_wdec, nm_wdec, nv_wdec = _adamw_call(wdec_mine, w_dec_up, m_w_dec_up, v_w_dec_up, "adamw_dec")

    loss_total = jnp.sum(r_small[:, 0, _SM_LOSS])

    return (loss_total, grad_x[None],
            g_norm, gw_in, g_wdec, g_bdec, g_glag, g_pa, g_pb, g_bgate, g_o, g_final,
            d_norm, d_in, d_wdec, d_bdec, d_glag, d_pa, d_pb, d_bgate, d_o, d_final,
            nm_norm, nm_in, nm_wdec, nm_bdec, nm_glag, nm_pa, nm_pb, nm_bgate, nm_o, nm_final,
            nv_norm, nv_in, nv_wdec, nv_bdec, nv_glag, nv_pa, nv_pb, nv_bgate, nv_o, nv_final)
```

```python
import math

import jax
import jax.numpy as jnp
from jax import lax
from jax.experimental import pallas as pl
from jax.experimental.pallas import tpu as pltpu

F32 = jnp.float32
BF16 = jnp.bfloat16

N_DEV = 8
D_MODEL = 1024
GLA_HEADS = 4
GLA_HK = 128
GLA_HV = 256
GLA_DK = 512
GLA_RANK = 16
GLA_TAU = 16.0
GLA_CHUNK = 64
SB_HEADS = 8
SB_HD = 128
EPS = 1e-6
N_GROUPS = 9
RANK_COL = 3072
IN_COLS = 9232
SHARD_COLS = IN_COLS // N_DEV

ADAM_LR = 0.001
ADAM_B1 = 0.9
ADAM_B2 = 0.999
ADAM_EPS = 1e-08
ADAM_WD = 0.01
ADAM_STEP = 10

VMEM_LIMIT = 56 * 1024 * 1024
TBLK = 256


def _cparams(sem=None):
    return pltpu.CompilerParams(dimension_semantics=sem, vmem_limit_bytes=VMEM_LIMIT)


def _tiling_2d(rows, cols, band_cols):
    if rows * cols <= 128 * 1024:
        return (rows, cols), (1,), lambda i: (0, 0)
    if rows % 128 == 0:
        return (128, cols), (rows // 128,), lambda i: (i, 0)
    tc = band_cols if cols % band_cols == 0 else cols
    return (rows, tc), (cols // tc,), lambda i: (0, i)


def _dot(a, b):
    return jnp.dot(a, b, preferred_element_type=F32)


def _dot_nt(a, b):
    return lax.dot_general(a, b, (((1,), (1,)), ((), ())), preferred_element_type=F32)


def _dot_tn(a, b):
    return lax.dot_general(a, b, (((0,), (0,)), ((), ())), preferred_element_type=F32)


def _bf(x):
    return x.astype(BF16)


def _split3(x):
    hi = x.astype(BF16)
    r = x - hi.astype(F32)
    mid = r.astype(BF16)
    lo = (r - mid.astype(F32)).astype(BF16)
    return hi, mid, lo


def _tri_left(tri, x):
    hi, mid, lo = _split3(x)
    return _dot(tri, hi) + _dot(tri, mid) + _dot(tri, lo)


def _split2(x):
    hi = lax.bitcast_convert_type(lax.bitcast_convert_type(x, jnp.uint32) & jnp.uint32(0xFFFF0000), F32)
    return hi.astype(BF16), (x - hi).astype(BF16)


def _tri2_left(tri, x):
    hi, lo = _split2(x)
    return _dot(tri, hi) + _dot(tri, lo)


def _tri2_right(x, tri):
    hi, lo = _split2(x)
    return _dot(hi, tri) + _dot(lo, tri)


def _iota2(n, m, dim):
    return lax.broadcasted_iota(jnp.int32, (n, m), dim)


def _sigmoid(x):
    return 1.0 / (1.0 + jnp.exp(-x))


def _softplus_neg_abs(z):
    return jnp.log(1.0 + jnp.exp(-jnp.abs(z)))


_ANY = pl.BlockSpec(memory_space=pl.ANY)


def _mesh_pos():
    return lax.axis_index("x"), lax.axis_index("y"), lax.axis_index("c")


def _other_chips(x, y):
    return [(1 - x, y), (x, 1 - y), (1 - x, 1 - y)]


def _rcopy(src, dst, send_sem, recv_sem, to):
    return pltpu.make_async_remote_copy(src_ref=src, dst_ref=dst, send_sem=send_sem, recv_sem=recv_sem,
                                        device_id=to, device_id_type=pl.DeviceIdType.MESH)


def _push_copies(src_ref, dst_ref, send_sems, recv_sems, loc_sem, scatter):
    x, y, c = _mesh_pos()
    me = 4 * x + 2 * y + c
    own = pltpu.make_async_copy(src_ref.at[me] if scatter else src_ref, dst_ref.at[me], loc_sem)
    pairs = []
    for k in range(1, N_DEV):
        px = 1 - x if k & 4 else x
        py = 1 - y if k & 2 else y
        pc = 1 - c if k & 1 else c
        pid = 4 * px + 2 * py + pc
        src = src_ref.at[pid] if scatter else src_ref
        send = _rcopy(src, dst_ref.at[me], send_sems.at[k - 1], recv_sems.at[k - 1], (px, py, pc))
        recv = _rcopy(src, dst_ref.at[pid], send_sems.at[k - 1], recv_sems.at[k - 1], (px, py, pc))
        pairs.append((send, recv))
    return own, pairs


def _push_start(own, pairs):
    own.start()
    for send, _ in pairs:
        send.start()


def _push_wait(own, pairs):
    for _, recv in pairs:
        recv.wait_recv()
    for send, _ in pairs:
        send.wait_send()
    own.wait()


_PUSH_SEMS = [pltpu.SemaphoreType.DMA((N_DEV - 1,)), pltpu.SemaphoreType.DMA((N_DEV - 1,)),
              pltpu.SemaphoreType.DMA]


def _half_rows(rows):
    return (rows // 2) // 16 * 16


_ADD_ROWS = 128


def _chip_reduce_steps(src_ref, dst_ref, relayed_ref, sum_x, sum_y, rel_x, rel_y, load_sems, send_sems, recv_sems,
                       loc_sem):
    _, R, C = src_ref.shape
    n0 = _half_rows(R)
    lo, hi = pl.ds(0, n0), pl.ds(n0, R - n0)
    x, y, c = _mesh_pos()
    (xx, xy), (yx, yy), (dx, dy) = _other_chips(x, y)
    to_diag, to_x, to_y = src_ref.at[2 * dx + dy], src_ref.at[2 * xx + xy], src_ref.at[2 * yx + yy]
    x_nb, y_nb = (xx, xy, c), (yx, yy, c)
    relays = (_rcopy(to_diag.at[lo], relayed_ref.at[lo], send_sems.at[0], recv_sems.at[0], x_nb),
              _rcopy(to_diag.at[hi], relayed_ref.at[hi], send_sems.at[1], recv_sems.at[1], y_nb))
    plain = (_rcopy(to_x.at[lo], dst_ref.at[0, lo], send_sems.at[2], recv_sems.at[2], x_nb),
             _rcopy(to_y.at[hi], dst_ref.at[1, hi], send_sems.at[3], recv_sems.at[3], y_nb))
    summed = (_rcopy(sum_x, dst_ref.at[0, hi], send_sems.at[4], recv_sems.at[4], x_nb),
              _rcopy(sum_y, dst_ref.at[1, lo], send_sems.at[5], recv_sems.at[5], y_nb))
    load_mine = (pltpu.make_async_copy(to_x.at[hi], sum_x, load_sems.at[0]),
                 pltpu.make_async_copy(to_y.at[lo], sum_y, load_sems.at[1]))
    load_relayed = (pltpu.make_async_copy(relayed_ref.at[hi], rel_x, load_sems.at[2]),
                    pltpu.make_async_copy(relayed_ref.at[lo], rel_y, load_sems.at[3]))
    own = pltpu.make_async_copy(src_ref.at[2 * x + y], dst_ref.at[2], loc_sem)

    def start():
        for cp in relays + plain + (own,) + load_mine:
            cp.start()

    def add(acc_ref, rel_ref):
        for r0 in range(0, acc_ref.shape[0], _ADD_ROWS):
            rows = slice(r0, min(r0 + _ADD_ROWS, acc_ref.shape[0]))
            acc_ref[rows, :] = (acc_ref[rows, :].astype(F32) + rel_ref[rows, :].astype(F32)).astype(acc_ref.dtype)

    def forward():
        for cp in relays:
            cp.wait_recv()
        for cp in load_relayed:
            cp.start()
        for cp in load_mine + load_relayed:
            cp.wait()
        add(sum_x, rel_x)
        add(sum_y, rel_y)
        for cp in summed:
            cp.start()

    def finish():
        for cp in plain + summed:
            cp.wait_recv()
        for cp in relays + plain + summed:
            cp.wait_send()
        own.wait()

    return start, forward, finish


def _chip_reduce_scratch(rows, cols, dtype):
    n0 = _half_rows(rows)
    return [pltpu.VMEM((rows - n0, cols), dtype), pltpu.VMEM((n0, cols), dtype)] * 2 + [
        pltpu.SemaphoreType.DMA((4,)), pltpu.SemaphoreType.DMA((6,)), pltpu.SemaphoreType.DMA((6,)),
        pltpu.SemaphoreType.DMA]


def _all_gather(arrs, name, row_pieces=None):
    n = len(arrs)
    pieces = [[None] if not row_pieces or not row_pieces[a] else list(row_pieces[a]) for a in range(n)]
    assert all(len(p) in (1, 2) for p in pieces)
    units = [(a, i) for a in range(n) for i in range(len(pieces[a]))]

    def body(*refs):
        ins = refs[:n]
        outs = refs[n:2 * n]
        send_sems, recv_sems, loc_sems = refs[2 * n:]
        x, y, c = _mesh_pos()
        me, sib = (x, y, c), (x, y, 1 - c)
        xn, yn, dg = [(px, py, c) for px, py in _other_chips(x, y)]

        def rows(ref, a, i):
            return ref if pieces[a][i] is None else ref.at[pl.ds(*pieces[a][i])]

        def copy(u, k, block, to, own=False):
            a, i = u
            px, py, pc = block
            dst = rows(outs[a].at[4 * px + 2 * py + pc], a, i)
            return _rcopy(rows(ins[a], a, i) if own else dst, dst, send_sems.at[a, k, i], recv_sems.at[a, k, i], to)

        started = []

        def start(cp):
            cp.start()
            started.append(cp)

        def landed_then_pass_on(u, k, block):
            copy(u, k, block, me).wait_recv()
            start(copy(u, 3 + k, block, sib))

        mine = [pltpu.make_async_copy(ins[a], outs[a].at[4 * x + 2 * y + c], loc_sems.at[a]) for a in range(n)]
        for cp in mine:
            cp.start()
        for u in units:
            start(copy(u, 0, me, sib, own=True))
        for a in range(n):
            if len(pieces[a]) == 2:
                for i, to, k in ((0, xn, 1), (1, yn, 2), (1, xn, 1), (0, yn, 2)):
                    start(copy((a, i), k, me, to, own=True))
            else:
                for to, k in ((xn, 1), (yn, 2), (dg, 3)):
                    start(copy((a, 0), k, me, to, own=True))
        for a in range(n):
            if len(pieces[a]) == 2:
                landed_then_pass_on((a, 0), 1, xn)
                start(copy((a, 0), 3, xn, yn))
                landed_then_pass_on((a, 1), 2, yn)
                start(copy((a, 1), 3, yn, xn))
                landed_then_pass_on((a, 1), 1, xn)
                landed_then_pass_on((a, 0), 2, yn)
                landed_then_pass_on((a, 0), 3, dg)
                landed_then_pass_on((a, 1), 3, dg)
            else:
                for block, k in ((xn, 1), (yn, 2), (dg, 3)):
                    landed_then_pass_on((a, 0), k, block)
        for u in units:
            copy(u, 0, sib, me).wait_recv()
            for k, (px, py, _) in ((4, xn), (5, yn), (6, dg)):
                copy(u, k, (px, py, 1 - c), me).wait_recv()
        for cp in started:
            cp.wait_send()
        for cp in mine:
            cp.wait()

    n_pc = max(len(p) for p in pieces)

    return pl.pallas_call(
        body, name=name,
        out_shape=tuple(jax.ShapeDtypeStruct((N_DEV,) + a.shape, a.dtype) for a in arrs),
        in_specs=[_ANY] * n,
        out_specs=tuple([_ANY] * n),
        scratch_shapes=[pltpu.SemaphoreType.DMA((n, 7, n_pc)), pltpu.SemaphoreType.DMA((n, 7, n_pc)),
                        pltpu.SemaphoreType.DMA((n,))],
    )(*arrs)


def _pair_add_call(mine, recv, name):
    n, R, C = mine.shape
    (tr, tc), (steps,), idx = _tiling_2d(R, C, 1024)

    def body(p_ref, r_ref, o_ref):
        o_ref[...] = (p_ref[...].astype(F32) + r_ref[...].astype(F32)).astype(o_ref.dtype)

    blk = pl.BlockSpec((None, tr, tc), lambda q, i: (q,) + idx(i))
    return pl.pallas_call(
        body, name=name,
        grid=(n, steps),
        in_specs=[blk, blk],
        out_specs=blk,
        out_shape=jax.ShapeDtypeStruct(mine.shape, mine.dtype),
        compiler_params=_cparams(("arbitrary", "arbitrary")),
    )(mine, recv)


def _flatten_blocks_call(blocks):
    n, R, C = blocks.shape
    tc = C // 2

    def body(in_ref, out_ref):
        for p in range(n):
            out_ref[p * R:(p + 1) * R, :] = in_ref[p]

    return pl.pallas_call(
        body, name="flatten_w",
        grid=(C // tc,),
        in_specs=[pl.BlockSpec((n, R, tc), lambda i: (0, 0, i))],
        out_specs=pl.BlockSpec((n * R, tc), lambda i: (0, i)),
        out_shape=jax.ShapeDtypeStruct((n * R, C), blocks.dtype),
        compiler_params=_cparams(("arbitrary",)),
    )(blocks)


_PARTS_BANDS = 8


def _parts_pair_call(dmain, drank):
    D = dmain.shape[1]
    tc = D // _PARTS_BANDS

    def body(dm_ref, dr_ref, mine_ref, recv_ref, laid, loc_sems, send_sems, recv_sems):
        x, y, c = _mesh_pos()

        def copies(k):
            cols = pl.ds(k * tc, tc)
            stay = [pltpu.make_async_copy(laid.at[k % 2, 2 * q + c], mine_ref.at[q, pl.ds(0, SHARD_COLS), cols],
                                          loc_sems.at[k, q]) for q in range(4)]
            go = [_rcopy(laid.at[k % 2, 2 * q + (1 - c)], recv_ref.at[q, pl.ds(0, SHARD_COLS), cols],
                         send_sems.at[k, q], recv_sems.at[k, q], (x, y, 1 - c)) for q in range(4)]
            return stay, go

        def wait_sent(k):
            stay, go = copies(k)
            for cp in stay:
                cp.wait()
            for cp in go:
                cp.wait_send()

        for k in range(_PARTS_BANDS):
            @pl.when(pl.program_id(0) == k)
            def _(k=k):
                if k >= 2:
                    wait_sent(k - 2)
                for p in range(N_DEV):
                    lo, hi = p * SHARD_COLS, (p + 1) * SHARD_COLS
                    at = 0
                    for src, a, b in ((dm_ref, lo, min(hi, RANK_COL)),
                                      (dr_ref, max(lo, RANK_COL) - RANK_COL,
                                       min(hi, RANK_COL + GLA_RANK) - RANK_COL),
                                      (dm_ref, max(lo, RANK_COL + GLA_RANK) - GLA_RANK, hi - GLA_RANK)):
                        if b > a:
                            laid[k % 2, p, at:at + (b - a), :] = src[a:b, :]
                            at += b - a
                stay, go = copies(k)
                for cp in stay + go:
                    cp.start()
                if k == _PARTS_BANDS - 1:
                    for k_open in range(max(0, k - 1), k + 1):
                        wait_sent(k_open)
                    for k_any in range(_PARTS_BANDS):
                        for cp in copies(k_any)[1]:
                            cp.wait_recv()

    sds = jax.ShapeDtypeStruct((4, SHARD_COLS, D), dmain.dtype)
    sems = pltpu.SemaphoreType.DMA((_PARTS_BANDS, 4))
    return pl.pallas_call(
        body, name="parts_pair",
        grid=(_PARTS_BANDS,),
        in_specs=[pl.BlockSpec((dmain.shape[0], tc), lambda i: (0, i)),
                  pl.BlockSpec((GLA_RANK, tc), lambda i: (0, i))],
        out_specs=(_ANY, _ANY),
        out_shape=(sds, sds),
        scratch_shapes=[pltpu.VMEM((2, N_DEV, SHARD_COLS, tc), dmain.dtype), sems, sems, sems],
        compiler_params=_cparams(("arbitrary",)),
    )(dmain, drank)


def _group_row(g):
    return GLA_RANK * (g * (1024 // GLA_RANK) + (g >= RANK_COL // 1024))


def _proj_call(x, norm_g, wt, wr, wp_part):
    T, D = x.shape
    tm = min(1024, T)
    assert tm % TBLK == 0
    n_i = T // tm

    def f_slot(j):
        return ((j >= 2).astype(jnp.int32) + (j >= 6).astype(jnp.int32)
                + (j >= 7).astype(jnp.int32) + (j >= 8).astype(jnp.int32))

    def b_slot(j):
        return (j >= 3).astype(jnp.int32) + (j >= 4).astype(jnp.int32) + (j >= 5).astype(jnp.int32)

    def body(x_ref, g_ref, w_ref, wr_ref, wp_ref, pf_ref, pb_ref, rank_ref, ht_ref, wpall_ref,
             h_scr, send_sems, recv_sems, loc_sem):
        i = pl.program_id(0)
        j = pl.program_id(1)
        own, pairs = _push_copies(wp_ref, wpall_ref, send_sems, recv_sems, loc_sem, scatter=False)

        @pl.when((i == 0) & (j == 0))
        def _():
            _push_start(own, pairs)

        @pl.when(j == 0)
        def _():
            xv = x_ref[...]
            r = lax.rsqrt(jnp.mean(xv * xv, axis=-1, keepdims=True) + EPS)
            h = (xv * r) * g_ref[...]
            hb = _bf(h)
            h_scr[...] = hb
            for b in range(tm // TBLK):
                ht_ref[b] = _bf(h[b * TBLK:(b + 1) * TBLK].T)
            rank_ref[...] = _dot_nt(hb, wr_ref[...])

        is_b = (j == 1) | ((j >= 3) & (j <= 5))

        @pl.when(is_b)
        def _():
            pb_ref[...] = _bf(_dot_nt(h_scr[...], w_ref[...]))

        @pl.when(jnp.logical_not(is_b))
        def _():
            pf_ref[...] = _dot_nt(h_scr[...], w_ref[...])

        @pl.when((i == n_i - 1) & (j == N_GROUPS - 1))
        def _():
            _push_wait(own, pairs)

    return pl.pallas_call(
        body, name="proj",
        grid=(n_i, N_GROUPS),
        in_specs=[pl.BlockSpec((tm, D), lambda i, j: (i, 0)),
                  pl.BlockSpec((1, D), lambda i, j: (0, 0)),
                  pl.BlockSpec((pl.Element(1024), pl.Element(D)), lambda i, j: (_group_row(j), 0)),
                  pl.BlockSpec((128, D), lambda i, j: (0, 0)),
                  _ANY],
        out_specs=(pl.BlockSpec((None, tm, 1024), lambda i, j: (f_slot(j), i, 0)),
                   pl.BlockSpec((None, tm, 1024), lambda i, j: (b_slot(j), i, 0)),
                   pl.BlockSpec((tm, 128), lambda i, j: (i, 0)),
                   pl.BlockSpec((tm // TBLK, D, TBLK), lambda i, j: (i, 0, 0)),
                   _ANY),
        out_shape=(jax.ShapeDtypeStruct((5, T, 1024), F32),
                   jax.ShapeDtypeStruct((4, T, 1024), BF16),
                   jax.ShapeDtypeStruct((T, 128), F32),
                   jax.ShapeDtypeStruct((T // TBLK, D, TBLK), BF16),
                   jax.ShapeDtypeStruct((N_DEV,) + wp_part.shape, wp_part.dtype)),
        scratch_shapes=[pltpu.VMEM((tm, D), BF16)] + _PUSH_SEMS,
        compiler_params=_cparams(("arbitrary", "arbitrary")),
    )(x, norm_g, wt, wr, wp_part)


GLA_STEP_CHUNKS = 4


def _gla_same_chunk(rows):
    return (_iota2(rows, rows, 0) & -GLA_CHUNK) == (_iota2(rows, rows, 1) & -GLA_CHUNK)


def _gla_chunk_terms(la, q, k, n_c):
    C = GLA_CHUNK
    rows = n_c * C
    low = _gla_same_chunk(rows) & (_iota2(rows, rows, 0) >= _iota2(rows, rows, 1))
    b = _tri_left(_bf(low.astype(F32)), la)
    bl = [b[(c + 1) * C - 1:(c + 1) * C, :] for c in range(n_c)]
    bl_rows = jnp.concatenate([jnp.broadcast_to(bl[c], (C, b.shape[1])) for c in range(n_c)], axis=0)
    eb = jnp.exp(b)
    enb = jnp.exp(-b)
    ebl_b = jnp.exp(bl_rows - b)
    scale = GLA_HK ** -0.5
    qe = q * eb * scale
    ke = k * enb
    kd = k * ebl_b
    return bl, eb, enb, ebl_b, qe, ke, kd


def _gla_fwd_call(projf, projb, rank, wdec, bdec):
    T = projf.shape[1]
    C = GLA_CHUNK
    n_chunks = T // C
    n_c = GLA_STEP_CHUNKS
    R = n_c * C
    assert n_chunks % n_c == 0

    def body(qk_ref, v_ref, rank_ref, wd_ref, bd_ref, o_ref, st_ref, la_ref, st_scr):
        @pl.when(pl.program_id(0) == 0)
        def _():
            st_scr[...] = jnp.zeros_like(st_scr)

        dec = _dot(_bf(rank_ref[...]), _bf(wd_ref[...])) + bd_ref[...]
        la = (jnp.minimum(dec, 0.0) - _softplus_neg_abs(dec)) / GLA_TAU
        la_ref[...] = la
        mask = _gla_same_chunk(R) & (_iota2(R, R, 0) >= _iota2(R, R, 1))
        bl, _, _, _, qe, ke, kd = _gla_chunk_terms(la, qk_ref[:, :GLA_DK], qk_ref[:, GLA_DK:], n_c)
        qeb, keb, kdb = _bf(qe), _bf(ke), _bf(kd)
        ebl = [jnp.exp(bl[c]) for c in range(n_c)]
        heads = range(GLA_HEADS)
        ks = [slice(hh * GLA_HK, (hh + 1) * GLA_HK) for hh in heads]
        vs = [slice(hh * GLA_HV, (hh + 1) * GLA_HV) for hh in heads]
        rs = [slice(c * C, (c + 1) * C) for c in range(n_c)]
        p = [_bf(jnp.where(mask, _dot_nt(qeb[:, ks[hh]], keb[:, ks[hh]]), 0.0)) for hh in heads]
        upd = [[_dot_tn(v_ref[rs[c], vs[hh]], kdb[rs[c], ks[hh]]) for hh in heads] for c in range(n_c)]
        intra = [_dot(p[hh], v_ref[:, vs[hh]]) for hh in heads]
        st = [st_scr[hh] for hh in heads]
        for c in range(n_c):
            inter = [_dot_nt(qeb[rs[c], ks[hh]], _bf(st[hh])) for hh in heads]
            for hh in heads:
                st_ref[c, hh] = st[hh]
                o_ref[rs[c], vs[hh]] = intra[hh][rs[c]] + inter[hh]
            st = [st[hh] * ebl[c][:, ks[hh]] + upd[c][hh] for hh in heads]
        for hh in heads:
            st_scr[hh] = st[hh]

    return pl.pallas_call(
        body, name="gla_fwd",
        grid=(n_chunks // n_c,),
        in_specs=[pl.BlockSpec((None, R, 1024), lambda n: (0, n, 0)),
                  pl.BlockSpec((None, R, 1024), lambda n: (0, n, 0)),
                  pl.BlockSpec((R, 128), lambda n: (n, 0)),
                  pl.BlockSpec((128, GLA_DK), lambda n: (0, 0)),
                  pl.BlockSpec((1, GLA_DK), lambda n: (0, 0))],
        out_specs=(pl.BlockSpec((R, 1024), lambda n: (n, 0)),
                   pl.BlockSpec((n_c, GLA_HEADS, GLA_HV, GLA_HK), lambda n: (n, 0, 0, 0)),
                   pl.BlockSpec((R, GLA_DK), lambda n: (n, 0))),
        out_shape=(jax.ShapeDtypeStruct((T, 1024), F32),
                   jax.ShapeDtypeStruct((n_chunks, GLA_HEADS, GLA_HV, GLA_HK), F32),
                   jax.ShapeDtypeStruct((T, GLA_DK), F32)),
        scratch_shapes=[pltpu.VMEM((GLA_HEADS, GLA_HV, GLA_HK), F32)],
        compiler_params=_cparams(("arbitrary",)),
    )(projf, projb, rank, wdec, bdec)


def _gla_bwd_call(projf, projb, la, do_gla, st_all, rank, wdec):
    T = projf.shape[1]
    C = GLA_CHUNK
    n_chunks = T // C
    n_c = GLA_STEP_CHUNKS
    R = n_c * C
    assert n_chunks % n_c == 0
    last = n_chunks // n_c - 1

    def body(qk_ref, v_ref, la_ref, do_ref, st_ref, rank_ref, wd_ref,
             dqk_ref, dv_ref, drank_ref, dwd_ref, dbd_ref, dst_scr):
        @pl.when(pl.program_id(0) == 0)
        def _():
            dst_scr[...] = jnp.zeros_like(dst_scr)
            dwd_ref[...] = jnp.zeros_like(dwd_ref)
            dbd_ref[...] = jnp.zeros_like(dbd_ref)

        same = _gla_same_chunk(R)
        mask = same & (_iota2(R, R, 0) >= _iota2(R, R, 1))
        upp = _bf((same & (_iota2(R, R, 0) <= _iota2(R, R, 1))).astype(F32))
        scale = GLA_HK ** -0.5
        la = la_ref[...]
        bl, eb, enb, ebl_b, qe, ke, kd = _gla_chunk_terms(la, qk_ref[:, :GLA_DK], qk_ref[:, GLA_DK:], n_c)
        qeb, keb, kdb = _bf(qe), _bf(ke), _bf(kd)
        ebl = [jnp.exp(bl[c]) for c in range(n_c)]
        heads = range(GLA_HEADS)
        ks = [slice(hh * GLA_HK, (hh + 1) * GLA_HK) for hh in heads]
        vs = [slice(hh * GLA_HV, (hh + 1) * GLA_HV) for hh in heads]
        rs = [slice(c * C, (c + 1) * C) for c in range(n_c)]
        v = [v_ref[:, vs[hh]] for hh in heads]
        do = [_bf(do_ref[:, vs[hh]]) for hh in heads]
        p = [_bf(jnp.where(mask, _dot_nt(qeb[:, ks[hh]], keb[:, ks[hh]]), 0.0)) for hh in heads]
        dp = [_bf(jnp.where(mask, _dot_nt(do[hh], v[hh]), 0.0)) for hh in heads]
        dst_intra = [[_dot_tn(do[hh][rs[c]], qeb[rs[c], ks[hh]]) for hh in heads] for c in range(n_c)]
        dqe_inter = [[_dot(do[hh][rs[c]], _bf(st_ref[c, hh])) for hh in heads] for c in range(n_c)]
        dv_intra = [_dot_tn(p[hh], do[hh]) for hh in heads]
        dqe_intra = [_dot(dp[hh], keb[:, ks[hh]]) for hh in heads]
        dke = jnp.concatenate([_dot_tn(dp[hh], qeb[:, ks[hh]]) for hh in heads], axis=1)
        dstn = [dst_scr[hh] for hh in heads]
        dkd_c, dv_inter, debl = [None] * n_c, [None] * n_c, [None] * n_c
        for c in reversed(range(n_c)):
            dstnb = [_bf(dstn[hh]) for hh in heads]
            dkd_c[c] = jnp.concatenate([_dot(v[hh][rs[c]], dstnb[hh]) for hh in heads], axis=1)
            dv_inter[c] = [_dot_nt(kdb[rs[c], ks[hh]], dstnb[hh]) for hh in heads]
            debl[c] = jnp.concatenate(
                [jnp.sum(dstn[hh] * st_ref[c, hh], axis=0, keepdims=True) for hh in heads], axis=1)
            dstn = [dst_intra[c][hh] + dstn[hh] * ebl[c][:, ks[hh]] for hh in heads]
        for hh in heads:
            dst_scr[hh] = dstn[hh]
            dv_ref[:, vs[hh]] = _bf(dv_intra[hh] + jnp.concatenate([dv_inter[c][hh] for c in range(n_c)], axis=0))
        dqe = jnp.concatenate(
            [dqe_intra[hh] + jnp.concatenate([dqe_inter[c][hh] for c in range(n_c)], axis=0) for hh in heads], axis=1)
        dkd = jnp.concatenate(dkd_c, axis=0)
        dkd_kd = dkd * kd
        db = dqe * qe - dke * ke - dkd_kd
        dbl = jnp.concatenate(
            [jnp.broadcast_to(jnp.sum(dkd_kd[rs[c]], axis=0, keepdims=True) + ebl[c] * debl[c], (C, GLA_DK))
             for c in range(n_c)], axis=0)
        dla = _tri_left(upp, db) + dbl
        dqk_ref[:, :GLA_DK] = _bf(dqe * eb * scale)
        dqk_ref[:, GLA_DK:] = _bf(dke * enb + dkd * ebl_b)
        ddec = dla * (1.0 / GLA_TAU) * (1.0 - jnp.exp(GLA_TAU * la))
        ddecb = _bf(ddec)
        drank_ref[...] = _bf(_dot_nt(ddecb, _bf(wd_ref[...])))
        dwd_ref[...] += _dot_tn(_bf(rank_ref[...]), ddecb)
        dbd_ref[...] += jnp.sum(ddec, axis=0, keepdims=True)

    return pl.pallas_call(
        body, name="gla_bwd",
        grid=(n_chunks // n_c,),
        in_specs=[pl.BlockSpec((None, R, 1024), lambda n: (0, last - n, 0)),
                  pl.BlockSpec((None, R, 1024), lambda n: (0, last - n, 0)),
                  pl.BlockSpec((R, GLA_DK), lambda n: (last - n, 0)),
                  pl.BlockSpec((R, 1024), lambda n: (last - n, 0)),
                  pl.BlockSpec((n_c, GLA_HEADS, GLA_HV, GLA_HK), lambda n: (last - n, 0, 0, 0)),
                  pl.BlockSpec((R, 128), lambda n: (last - n, 0)),
                  pl.BlockSpec((128, GLA_DK), lambda n: (0, 0))],
        out_specs=(pl.BlockSpec((R, 1024), lambda n: (last - n, 0)),
                   pl.BlockSpec((R, 1024), lambda n: (last - n, 0)),
                   pl.BlockSpec((R, 128), lambda n: (last - n, 0)),
                   pl.BlockSpec((128, GLA_DK), lambda n: (0, 0)),
                   pl.BlockSpec((1, GLA_DK), lambda n: (0, 0))),
        out_shape=(jax.ShapeDtypeStruct((T, 1024), BF16),
                   jax.ShapeDtypeStruct((T, 1024), BF16),
                   jax.ShapeDtypeStruct((T, 128), BF16),
                   jax.ShapeDtypeStruct((128, GLA_DK), F32),
                   jax.ShapeDtypeStruct((1, GLA_DK), F32)),
        scratch_shapes=[pltpu.VMEM((GLA_HEADS, GLA_HV, GLA_HK), F32)],
        compiler_params=_cparams(("arbitrary",)),
    )(projf, projb, la, do_gla, st_all, rank, wdec)


def _sb_logs(z):
    lsz = jnp.minimum(z, 0.0) - _softplus_neg_abs(z)
    return lsz, lsz - z


SB_HG_FWD = 8
SB_HG_BWD = 4
SB_QUERIES = 256
SB_KEYS = 256
SB_DEAD = -105.0


def _sb_fwd_call(projb, wp_shard):
    T = projb.shape[1]
    B = min(SB_QUERIES, T)
    HG = SB_HG_FWD
    W = HG * SB_HD
    scale = 1.0 / math.sqrt(SB_HD)
    KB = min(SB_KEYS, T)
    n_h, n_i = SB_HEADS // HG, T // B

    def body(q_ref, k_ref, v_ref, wp_ref, o_ref, wpall_ref, cb_scr, send_sems, recv_sems, loc_sem):
        i = pl.program_id(1)
        own, pairs = _push_copies(wp_ref, wpall_ref, send_sems, recv_sems, loc_sem, scatter=False)

        @pl.when((pl.program_id(0) == 0) & (i == 0))
        def _():
            _push_start(own, pairs)

        rows = HG * B
        after = (_iota2(KB, KB, 0) > _iota2(KB, KB, 1)).astype(F32)
        tri = _bf(jnp.concatenate([after, jnp.ones((KB, KB), F32)], axis=1))
        o_ref[...] = jnp.zeros_like(o_ref)
        cb_scr[...] = jnp.zeros_like(cb_scr)

        def block(jp, masked):
            off = pl.multiple_of(jp * KB, KB)
            z = jnp.concatenate(
                [_dot_nt(q_ref[:, hh * SB_HD:(hh + 1) * SB_HD], k_ref[pl.ds(off, KB), hh * SB_HD:(hh + 1) * SB_HD])
                 for hh in range(HG)], axis=0) * scale
            lsz, l1m = _sb_logs(z)
            if masked:
                strict = (jp * KB + _iota2(rows, KB, 1)) < (i * B + (_iota2(rows, KB, 0) & (B - 1)))
                l1m = jnp.where(strict, l1m, 0.0)
            r = _tri2_right(l1m, tri)
            cb = cb_scr[...]
            a = jnp.exp(lsz + cb + r[:, :KB])
            if masked:
                a = jnp.where(strict, a, 0.0)
            cb_scr[...] = cb + r[:, KB:]
            ab = _bf(a)
            for hh in range(HG):
                cs = slice(hh * SB_HD, (hh + 1) * SB_HD)
                o_ref[:, cs] += _dot(ab[hh * B:(hh + 1) * B, :], v_ref[pl.ds(off, KB), cs])

        jp0 = (i * B) // KB
        block(jp0, True)

        def live(state):
            jj, dead = state
            return (jj <= jp0) & jnp.logical_not(dead)

        def step(state):
            jj, _ = state
            block(jp0 - jj, False)
            return jj + 1, jnp.max(cb_scr[:, :SB_HD]) < SB_DEAD

        lax.while_loop(live, step, (jnp.int32(1), jnp.max(cb_scr[:, :SB_HD]) < SB_DEAD))

        @pl.when((pl.program_id(0) == n_h - 1) & (i == n_i - 1))
        def _():
            _push_wait(own, pairs)

    return pl.pallas_call(
        body, name="sb_fwd",
        grid=(n_h, n_i),
        in_specs=[pl.BlockSpec((None, B, W), lambda h, i: (1, i, h)),
                  pl.BlockSpec((None, T, W), lambda h, i: (2, 0, h)),
                  pl.BlockSpec((None, T, W), lambda h, i: (3, 0, h)),
                  _ANY],
        out_specs=(pl.BlockSpec((B, W), lambda h, i: (i, h)), _ANY),
        out_shape=(jax.ShapeDtypeStruct((T, 1024), F32),
                   jax.ShapeDtypeStruct((N_DEV,) + wp_shard.shape, wp_shard.dtype)),
        scratch_shapes=[pltpu.VMEM((HG * B, KB), F32)] + _PUSH_SEMS,
        compiler_params=_cparams(("arbitrary", "arbitrary")),
    )(projb, projb, projb, wp_shard)


def _sb_bwd_call(projb, do_sb, g_p):
    T = projb.shape[1]
    B = min(SB_QUERIES, T)
    nb = T // B
    HG = SB_HG_BWD
    W = HG * SB_HD
    WQ = HG * B
    KB = min(SB_KEYS, T)
    nkb = T // KB
    n_h = SB_HEADS // HG
    scale = 1.0 / math.sqrt(SB_HD)

    def body(q_ref, k_ref, v_ref, do_ref, gp_ref, dq_ref, dk_ref, dv_ref, rp_ref,
             dk_scr, dv_scr, kt_scr, beta_scr, g_scr, dqt_scr, send_sems, recv_sems, loc_sem):
        i = pl.program_id(1)
        own, pairs = _push_copies(gp_ref, rp_ref, send_sems, recv_sems, loc_sem, scatter=True)

        @pl.when((pl.program_id(0) == 0) & (i == 0))
        def _():
            _push_start(own, pairs)

        @pl.when(i == 0)
        def _():
            dk_scr[...] = jnp.zeros_like(dk_scr)
            dv_scr[...] = jnp.zeros_like(dv_scr)
            for hh in range(HG):
                for jb in range(nkb):
                    kt_scr[hh, jb] = _bf(
                        k_ref[jb * KB:(jb + 1) * KB, hh * SB_HD:(hh + 1) * SB_HD].astype(F32).T)

        dqt_scr[...] = jnp.zeros_like(dqt_scr)
        later = _bf((_iota2(KB, KB, 1) > _iota2(KB, KB, 0)).astype(F32))
        earlier = _bf((_iota2(KB, KB, 1) < _iota2(KB, KB, 0)).astype(F32))
        dob = _bf(do_ref[...])
        jp0 = (i * B) // KB

        def strict_mask():
            return (jp0 * KB + _iota2(KB, WQ, 0)) < (i * B + (_iota2(KB, WQ, 1) & (B - 1)))

        def heads(fn):
            return [fn(slice(hh * SB_HD, (hh + 1) * SB_HD)) for hh in range(HG)]

        def pass1(jp, cb, masked):
            off = pl.multiple_of(jp * KB, KB)
            z = jnp.concatenate(heads(lambda cs: _dot_nt(k_ref[pl.ds(off, KB), cs], q_ref[:, cs])), axis=1) * scale
            da = jnp.concatenate(heads(lambda cs: _dot_nt(v_ref[pl.ds(off, KB), cs], dob[:, cs])), axis=1)
            lsz, l1m = _sb_logs(z)
            if masked:
                strict = strict_mask()
                l1m = jnp.where(strict, l1m, 0.0)
            a = jnp.exp(lsz + cb + _tri2_left(later, l1m))
            if masked:
                a = jnp.where(strict, a, 0.0)
            g_scr[jp] = a * da
            beta_scr[jp] = jnp.exp(lsz)
            ab = _bf(a)
            for hh in range(HG):
                cs = slice(hh * SB_HD, (hh + 1) * SB_HD)
                dv_scr[pl.ds(off, KB), cs] += _dot(ab[:, hh * B:(hh + 1) * B], dob[:, cs])
            return cb + jnp.sum(l1m, axis=0, keepdims=True)

        zero = jnp.zeros((1, WQ), F32)
        cb = pass1(jp0, zero, True)

        def live(state):
            jj, _, dead = state
            return (jj <= jp0) & jnp.logical_not(dead)

        def step(state):
            jj, cr, _ = state
            cr = pass1(jp0 - jj, cr, False)
            return jj + 1, cr, jnp.max(cr) < SB_DEAD

        n_done, _, _ = lax.while_loop(live, step, (jnp.int32(1), cb, jnp.max(cb) < SB_DEAD))
        jp_first = jp0 - (n_done - 1)

        def pass2(jp, cg, masked):
            off = pl.multiple_of(jp * KB, KB)
            g = g_scr[jp]
            beta = beta_scr[jp]
            dz = g * (1.0 - beta) - beta * (cg + _tri2_left(earlier, g))
            if masked:
                dz = jnp.where(strict_mask(), dz, 0.0)
            dzb = _bf(dz * scale)
            for hh in range(HG):
                cs = slice(hh * SB_HD, (hh + 1) * SB_HD)
                dk_scr[pl.ds(off, KB), cs] += _dot(dzb[:, hh * B:(hh + 1) * B], q_ref[:, cs])
                dqt_scr[hh] += _dot(kt_scr[hh, jp], dzb[:, hh * B:(hh + 1) * B])
            return cg + jnp.sum(g, axis=0, keepdims=True)

        cg = lax.fori_loop(jp_first, jp0, lambda jp, cr: pass2(jp, cr, False), zero)
        pass2(jp0, cg, True)
        for hh in range(HG):
            dq_ref[:, hh * SB_HD:(hh + 1) * SB_HD] = _bf(dqt_scr[hh].T)

        @pl.when(i == nb - 1)
        def _():
            dk_ref[...] = _bf(dk_scr[...])
            dv_ref[...] = _bf(dv_scr[...])

        @pl.when((pl.program_id(0) == n_h - 1) & (i == nb - 1))
        def _():
            _push_wait(own, pairs)

    return pl.pallas_call(
        body, name="sb_bwd",
        grid=(n_h, nb),
        in_specs=[pl.BlockSpec((None, B, W), lambda h, i: (1, i, h)),
                  pl.BlockSpec((None, T, W), lambda h, i: (2, 0, h)),
                  pl.BlockSpec((None, T, W), lambda h, i: (3, 0, h)),
                  pl.BlockSpec((B, W), lambda h, i: (i, h)),
                  _ANY],
        out_specs=(pl.BlockSpec((B, W), lambda h, i: (i, h)),
                   pl.BlockSpec((T, W), lambda h, i: (0, h)),
                   pl.BlockSpec((T, W), lambda h, i: (0, h)),
                   _ANY),
        out_shape=(jax.ShapeDtypeStruct((T, 1024), BF16),
                   jax.ShapeDtypeStruct((T, 1024), BF16),
                   jax.ShapeDtypeStruct((T, 1024), BF16),
                   jax.ShapeDtypeStruct(g_p.shape, g_p.dtype)),
        scratch_shapes=[pltpu.VMEM((T, W), F32), pltpu.VMEM((T, W), F32),
                        pltpu.VMEM((HG, nkb, SB_HD, KB), BF16),
                        pltpu.VMEM((nkb, KB, WQ), F32), pltpu.VMEM((nkb, KB, WQ), F32),
                        pltpu.VMEM((HG, SB_HD, B), F32)] + _PUSH_SEMS,
        compiler_params=_cparams(("arbitrary", "arbitrary")),
    )(projb, projb, projb, do_sb, g_p)


def _mid_call(o_gla, o_sb, projf, x, target, wpa, wpb, wo, gla_g, b_gate, final_g):
    T, D = x.shape
    tm = min(TBLK, T)

    def body(og_ref, ggate_ref, osb_ref, sgate_ref, ma_ref, mb_ref, x_ref, tgt_ref,
             wpa_ref, wpb_ref, wo_ref, glag_ref, bg_ref, fg_ref,
             dx2_ref, dogla_ref, dosb_ref, dggate_ref, dsgate_ref, dm_ref,
             mt_ref, ogt_ref, obt_ref, dx2b_ref, dya_ref, dyb_ref,
             dfg_ref, dbg_ref, dglag_ref, loss_ref):
        @pl.when(pl.program_id(0) == 0)
        def _():
            dfg_ref[...] = jnp.zeros_like(dfg_ref)
            dbg_ref[...] = jnp.zeros_like(dbg_ref)
            dglag_ref[...] = jnp.zeros_like(dglag_ref)
            loss_ref[...] = jnp.zeros_like(loss_ref)

        glag = glag_ref[...]
        ggate = ggate_ref[...]
        sg = _sigmoid(ggate)
        silu_g = ggate * sg
        ohat, rinv, nrm = [], [], []
        for hh in range(GLA_HEADS):
            oh = og_ref[:, hh * GLA_HV:(hh + 1) * GLA_HV]
            r = lax.rsqrt(jnp.mean(oh * oh, axis=-1, keepdims=True) + EPS)
            ohat.append(oh * r)
            rinv.append(r)
            nrm.append(ohat[-1] * glag)
        n_all = jnp.concatenate(nrm, axis=1)
        og = n_all * silu_g
        ogb = _bf(og)
        ya = _dot(ogb, wpa_ref[...])
        sgate = sgate_ref[...]
        ss = _sigmoid(sgate)
        silu_s = sgate * ss
        osb = osb_ref[...]
        ob = osb * silu_s
        obb = _bf(ob)
        yb = _dot(obb, wpb_ref[...])
        ga = _sigmoid(ma_ref[...] + bg_ref[:, :D])
        gb = _sigmoid(mb_ref[...] + bg_ref[:, D:])
        merged = ga * ya + gb * yb
        mgb = _bf(merged)
        x2 = x_ref[...] + _dot(mgb, wo_ref[...])
        r2 = lax.rsqrt(jnp.mean(x2 * x2, axis=-1, keepdims=True) + EPS)
        xh2 = x2 * r2
        fg = fg_ref[...]
        err = xh2 * fg - tgt_ref[...]
        loss_ref[...] += jnp.broadcast_to(
            0.5 * jnp.sum(jnp.mean(err * err, axis=-1, keepdims=True), axis=0, keepdims=True), (1, 128))
        dy = err * (1.0 / D)
        dfg_ref[...] += jnp.sum(dy * xh2, axis=0, keepdims=True)
        dxh = dy * fg
        dx2 = r2 * (dxh - xh2 * jnp.mean(dxh * xh2, axis=-1, keepdims=True))
        dx2_ref[...] = dx2
        dx2b = _bf(dx2)
        dx2b_ref[...] = dx2b
        dmerged = _dot_nt(dx2b, wo_ref[...])
        dya = dmerged * ga
        dyb = dmerged * gb
        dma = dmerged * ya * ga * (1.0 - ga)
        dmb = dmerged * yb * gb * (1.0 - gb)
        dm_ref[:, :D] = _bf(dma)
        dm_ref[:, D:] = _bf(dmb)
        dbg_ref[:, :D] += jnp.sum(dma, axis=0, keepdims=True)
        dbg_ref[:, D:] += jnp.sum(dmb, axis=0, keepdims=True)
        dyab = _bf(dya)
        dybb = _bf(dyb)
        dya_ref[...] = dyab
        dyb_ref[...] = dybb
        dog = _dot_nt(dyab, wpa_ref[...])
        dob = _dot_nt(dybb, wpb_ref[...])
        dosb_ref[...] = dob * silu_s
        dsgate_ref[...] = _bf(dob * osb * (ss * (1.0 + sgate * (1.0 - ss))))
        dn = dog * silu_g
        dggate_ref[...] = _bf(dog * n_all * (sg * (1.0 + ggate * (1.0 - sg))))
        dglag = jnp.zeros((1, GLA_HV), F32)
        for hh in range(GLA_HEADS):
            dnh = dn[:, hh * GLA_HV:(hh + 1) * GLA_HV]
            dglag = dglag + jnp.sum(dnh * ohat[hh], axis=0, keepdims=True)
            dohat = dnh * glag
            dogla_ref[:, hh * GLA_HV:(hh + 1) * GLA_HV] = rinv[hh] * (
                dohat - ohat[hh] * jnp.mean(dohat * ohat[hh], axis=-1, keepdims=True))
        dglag_ref[...] += dglag
        mt_ref[...] = _bf(merged.T)
        ogt_ref[...] = _bf(og.T)
        obt_ref[...] = _bf(ob.T)

    row = lambda i: (i, 0)
    const = lambda i: (0, 0)
    tile = pl.BlockSpec((tm, D), row)
    tile_t = pl.BlockSpec((None, D, tm), lambda i: (i, 0, 0))
    wspec = pl.BlockSpec((D, D), const)
    return pl.pallas_call(
        body, name="mid",
        grid=(T // tm,),
        in_specs=[tile,
                  pl.BlockSpec((None, tm, D), lambda i: (1, i, 0)),
                  tile,
                  pl.BlockSpec((None, tm, D), lambda i: (2, i, 0)),
                  pl.BlockSpec((None, tm, D), lambda i: (3, i, 0)),
                  pl.BlockSpec((None, tm, D), lambda i: (4, i, 0)),
                  tile, tile, wspec, wspec, wspec,
                  pl.BlockSpec((1, GLA_HV), const),
                  pl.BlockSpec((1, 2 * D), const),
                  pl.BlockSpec((1, D), const)],
        out_specs=(tile, tile, tile, tile, tile,
                   pl.BlockSpec((tm, 2 * D), row),
                   tile_t, tile_t, tile_t, tile, tile, tile,
                   pl.BlockSpec((1, D), const),
                   pl.BlockSpec((1, 2 * D), const),
                   pl.BlockSpec((1, GLA_HV), const),
                   pl.BlockSpec((1, 128), const)),
        out_shape=(jax.ShapeDtypeStruct((T, D), F32),
                   jax.ShapeDtypeStruct((T, D), F32),
                   jax.ShapeDtypeStruct((T, D), F32),
                   jax.ShapeDtypeStruct((T, D), BF16),
                   jax.ShapeDtypeStruct((T, D), BF16),
                   jax.ShapeDtypeStruct((T, 2 * D), BF16),
                   jax.ShapeDtypeStruct((T // tm, D, tm), BF16),
                   jax.ShapeDtypeStruct((T // tm, D, tm), BF16),
                   jax.ShapeDtypeStruct((T // tm, D, tm), BF16),
                   jax.ShapeDtypeStruct((T, D), BF16),
                   jax.ShapeDtypeStruct((T, D), BF16),
                   jax.ShapeDtypeStruct((T, D), BF16),
                   jax.ShapeDtypeStruct((1, D), F32),
                   jax.ShapeDtypeStruct((1, 2 * D), F32),
                   jax.ShapeDtypeStruct((1, GLA_HV), F32),
                   jax.ShapeDtypeStruct((1, 128), F32)),
        compiler_params=_cparams(("arbitrary",)),
    )(o_gla, projf, o_sb, projf, projf, projf, x, target, wpa, wpb, wo, gla_g, b_gate, final_g)


def _dh_call(pieces, dmlog, drank, wt, wr, x, dx2, norm_g, s_in, small):
    T, D = x.shape
    tm = min(256, T)
    npc = len(pieces)
    n_main = N_GROUPS * 1024
    n_i = T // tm
    i_forward = 5 * n_i // 8

    def body(*refs):
        pcs = refs[:npc]
        (dm_ref, dr_ref, w_hbm, wr_ref, x_ref, dx2_ref, g_ref, sin_ref, small_ref,
         gx_ref, rin_ref, relayed_ref, rsmall_ref,
         w_scr, sems, dg_ref, small_mine, small_send, small_recv, small_loc, *exchange_scratch) = refs[npc:]
        start, forward, finish = _chip_reduce_steps(sin_ref, rin_ref, relayed_ref, *exchange_scratch)

        @pl.when(pl.program_id(0) == 0)
        def _():
            start()
            lo = pltpu.make_async_copy(w_hbm.at[pl.ds(0, RANK_COL)], w_scr.at[pl.ds(0, RANK_COL)], sems.at[0])
            hi = pltpu.make_async_copy(w_hbm.at[pl.ds(RANK_COL + GLA_RANK, n_main - RANK_COL)],
                                       w_scr.at[pl.ds(RANK_COL, n_main - RANK_COL)], sems.at[1])
            lo.start()
            hi.start()
            dg_ref[...] = jnp.zeros_like(dg_ref)
            lo.wait()
            hi.wait()

        @pl.when(pl.program_id(0) == i_forward)
        def _():
            forward()

        def w_group(g):
            return w_scr[g * 1024:(g + 1) * 1024, :]

        dr = dr_ref[...]
        dh = _dot(dr, wr_ref[...])
        for g in range(npc):
            dh = dh + _dot(pcs[g][...], w_group(g))
        dh = dh + _dot(dm_ref[:, :D], w_group(npc))
        dh = dh + _dot(dm_ref[:, D:], w_group(npc + 1))
        xv = x_ref[...]
        r = lax.rsqrt(jnp.mean(xv * xv, axis=-1, keepdims=True) + EPS)
        xhat = xv * r
        g = g_ref[...]
        dg_ref[...] += jnp.sum(dh * xhat, axis=0, keepdims=True)
        dxhat = dh * g
        gx_ref[...] = r * (dxhat - xhat * jnp.mean(dxhat * xhat, axis=-1, keepdims=True)) + dx2_ref[...]

        @pl.when(pl.program_id(0) == n_i - 1)
        def _():
            small_mine[...] = small_ref[...]
            small_mine[:, _SM_NORM:_SM_NORM + D] = dg_ref[...]
            own, pairs = _push_copies(small_mine, rsmall_ref, small_send, small_recv, small_loc, scatter=False)
            _push_start(own, pairs)
            finish()
            _push_wait(own, pairs)

    row = lambda i: (i, 0)
    const = lambda i: (0, 0)
    tile = pl.BlockSpec((tm, D), row)
    part = s_in.shape[1:]
    return pl.pallas_call(
        body, name="dh",
        grid=(n_i,),
        in_specs=[tile] * npc + [
            pl.BlockSpec((tm, 2 * D), row),
            pl.BlockSpec((tm, 128), row),
            _ANY,
            pl.BlockSpec((128, D), const),
            tile, tile,
            pl.BlockSpec((1, D), const),
            _ANY,
            pl.BlockSpec(small.shape, const)],
        out_specs=(tile, _ANY, _ANY, _ANY),
        out_shape=(jax.ShapeDtypeStruct((T, D), F32),
                   jax.ShapeDtypeStruct((3,) + part, s_in.dtype),
                   jax.ShapeDtypeStruct(part, s_in.dtype),
                   jax.ShapeDtypeStruct((N_DEV,) + small.shape, small.dtype)),
        scratch_shapes=[pltpu.VMEM((n_main, D), BF16), pltpu.SemaphoreType.DMA((2,)),
                        pltpu.VMEM((1, D), F32), pltpu.VMEM(small.shape, small.dtype)]
        + _PUSH_SEMS + _chip_reduce_scratch(*part, s_in.dtype),
        compiler_params=_cparams(("arbitrary",)),
    )(*pieces, dmlog, drank, wt, wr, x, dx2, norm_g, s_in, small)


def _wgrad_rank_call(ht, drank):
    n_tb, D, tb = ht.shape

    def body(ht_ref, dr_ref, o_ref):
        @pl.when(pl.program_id(0) == 0)
        def _():
            o_ref[...] = jnp.zeros_like(o_ref)

        o_ref[...] += _dot(ht_ref[...], dr_ref[...])

    return pl.pallas_call(
        body, name="wgrad_rank",
        grid=(n_tb,),
        in_specs=[pl.BlockSpec((None, D, tb), lambda i: (i, 0, 0)),
                  pl.BlockSpec((tb, 128), lambda i: (i, 0))],
        out_specs=pl.BlockSpec((D, 128), lambda i: (0, 0)),
        out_shape=jax.ShapeDtypeStruct((D, 128), F32),
        compiler_params=_cparams(("arbitrary",)),
    )(ht, drank)


def _wgrad_call(lhs_list, lhs_of_group, rhs_list, rhs_of_group, n_transposed, name):
    n_groups = len(rhs_of_group)
    n_tb, D, tb = lhs_list[0].shape
    T = n_tb * tb
    per = min(4, n_tb)
    tk = per * tb
    nk = T // tk
    nl = len(lhs_list)

    def body(*refs):
        lhs = refs[:nl]
        rhs = refs[nl:nl + n_groups]
        out_ref, acc = refs[nl + n_groups:]
        g = pl.program_id(0)
        i = pl.program_id(1)

        @pl.when(i == 0)
        def _():
            acc[...] = jnp.zeros_like(acc)

        for p in range(n_groups):
            @pl.when(g == p)
            def _(p=p):
                lref = lhs[lhs_of_group[p]]
                part = _dot(lref[0], rhs[p][0:tb, :])
                for b in range(1, per):
                    part = part + _dot(lref[b], rhs[p][b * tb:(b + 1) * tb, :])
                acc[...] += part

        @pl.when((i == nk - 1) & (g < n_transposed))
        def _():
            out_ref[...] = _bf(acc[...].T)

        @pl.when((i == nk - 1) & (g >= n_transposed))
        def _():
            out_ref[...] = _bf(acc[...])

    def lhs_spec(a):
        groups = [g for g in range(n_groups) if lhs_of_group[g] == a]
        lo, hi = min(groups), max(groups)
        assert groups == list(range(lo, hi + 1))
        return pl.BlockSpec((per, D, tb), lambda g, i: (jnp.where((g >= lo) & (g <= hi), i, 0), 0, 0))

    def rhs_spec(p):
        cb = rhs_of_group[p][1]
        return pl.BlockSpec((tk, 1024), lambda g, i: (jnp.where(g == p, i, 0), cb))

    return pl.pallas_call(
        body, name=name,
        grid=(n_groups, nk),
        in_specs=[lhs_spec(a) for a in range(nl)] + [rhs_spec(p) for p in range(n_groups)],
        out_specs=pl.BlockSpec((None, D, 1024), lambda g, i: (g, 0, 0)),
        out_shape=jax.ShapeDtypeStruct((n_groups, D, 1024), BF16),
        scratch_shapes=[pltpu.VMEM((D, 1024), F32)],
        compiler_params=_cparams(("arbitrary", "arbitrary")),
    )(*lhs_list, *[rhs_list[rhs_of_group[p][0]] for p in range(n_groups)])


def _adamw_math(parts, w, m, v):
    g = parts[0].astype(F32)
    for p in parts[1:]:
        g = g + p.astype(F32)
    mm = ADAM_B1 * m + (1.0 - ADAM_B1) * g
    vv = ADAM_B2 * v + (1.0 - ADAM_B2) * (g * g)
    m_hat = mm / (1.0 - ADAM_B1 ** ADAM_STEP)
    v_hat = vv / (1.0 - ADAM_B2 ** ADAM_STEP)
    return g, -ADAM_LR * (m_hat / (jnp.sqrt(v_hat) + ADAM_EPS) + ADAM_WD * w), mm, vv


def _part_order(n_parts):
    return [n_parts - 1] + list(range(n_parts - 1))


def _adamw_call(parts, w, m, v, name):
    R, C = w.shape
    n_parts = parts.shape[0]
    (tr, tc), grid, idx = _tiling_2d(R, C, 512)

    def body(p_ref, w_ref, m_ref, v_ref, g_ref, d_ref, nm_ref, nv_ref):
        g_ref[...], d_ref[...], nm_ref[...], nv_ref[...] = _adamw_math(
            [p_ref[k] for k in _part_order(n_parts)], w_ref[...], m_ref[...], v_ref[...])

    blk = pl.BlockSpec((tr, tc), idx)
    sds = jax.ShapeDtypeStruct((R, C), F32)
    return pl.pallas_call(
        body, name=name,
        grid=grid,
        in_specs=[pl.BlockSpec((n_parts, tr, tc), lambda i: (0,) + idx(i)), blk, blk, blk],
        out_specs=(blk, blk, blk, blk),
        out_shape=(sds, sds, sds, sds),
        compiler_params=_cparams(("arbitrary",)),
    )(parts, w, m, v)


def _adamw_rows_call(parts, ws, ms, vs, name):
    n = len(ws)
    R, C = ws[0].shape
    n_parts = parts.shape[0]

    def body(*refs):
        p_ref = refs[0]
        w_refs, m_refs, v_refs = refs[1:1 + n], refs[1 + n:1 + 2 * n], refs[1 + 2 * n:1 + 3 * n]
        outs = refs[1 + 3 * n:]
        for k in range(n):
            @pl.when(pl.program_id(0) == k)
            def _(k=k):
                res = _adamw_math([p_ref[j] for j in _part_order(n_parts)],
                                  w_refs[k][...], m_refs[k][...], v_refs[k][...])
                for o_ref, val in zip(outs[4 * k:4 * k + 4], res):
                    o_ref[...] = val

    whole = pl.BlockSpec((R, C), lambda k: (0, 0))
    sds = jax.ShapeDtypeStruct((R, C), F32)
    res = pl.pallas_call(
        body, name=name,
        grid=(n,),
        in_specs=[pl.BlockSpec((n_parts, R, C), lambda k: (0, k, 0))] + [whole] * (3 * n),
        out_specs=tuple([whole] * (4 * n)),
        out_shape=tuple([sds] * (4 * n)),
        compiler_params=_cparams(("arbitrary",)),
    )(parts, *ws, *ms, *vs)
    return [res[4 * k:4 * k + 4] for k in range(n)]


def _adamw_lanes_call(parts, offsets, ws, ms, vs, name):
    n = len(ws)
    n_parts = parts.shape[0]

    def body(*refs):
        p_ref = refs[0]
        w_refs, m_refs, v_refs = refs[1:1 + n], refs[1 + n:1 + 2 * n], refs[1 + 2 * n:1 + 3 * n]
        outs = refs[1 + 3 * n:]
        for k in range(n):
            lanes = slice(offsets[k], offsets[k] + ws[k].shape[1])
            res = _adamw_math([p_ref[j, :, lanes] for j in _part_order(n_parts)],
                              w_refs[k][...], m_refs[k][...], v_refs[k][...])
            for o_ref, val in zip(outs[4 * k:4 * k + 4], res):
                o_ref[...] = val

    res = pl.pallas_call(
        body, name=name,
        out_shape=tuple(jax.ShapeDtypeStruct(ws[k].shape, F32) for k in range(n) for _ in range(4)),
        compiler_params=_cparams(),
    )(parts, *ws, *ms, *vs)
    return [res[4 * k:4 * k + 4] for k in range(n)]


def _local_step(x, target, wt, wr, wdec, bdec, wp_shard, norm_g, gla_g, b_gate, final_g):
    D = x.shape[1]
    half = wp_shard.shape[1] // 2
    projf, projb, rank, ht, wp_lo = _proj_call(x, norm_g, wt, wr, wp_shard[:, :half])
    o_gla, st_all, la = _gla_fwd_call(projf, projb, rank, wdec, bdec)
    o_sb, wp_hi = _sb_fwd_call(projb, wp_shard[:, half:])
    wp_full = jnp.concatenate([wp_lo, wp_hi], axis=2).transpose(1, 0, 2, 3).reshape(3, D, D)
    (dx2, do_gla, do_sb, dggate, dsgate, dmlog, mt, ogt, obt, dx2b, dya, dyb,
     dfinal_g, db_gate, dgla_g, loss) = _mid_call(o_gla, o_sb, projf, x, target, wp_full[0], wp_full[1],
                                                 wp_full[2], gla_g, b_gate, final_g)
    dw_p = _wgrad_call([ogt, obt, mt], [0, 1, 2], [dya, dyb, dx2b], [(0, 0), (1, 0), (2, 0)], 0, "wgrad_p")
    g_p = dw_p.reshape(3, N_DEV, D // N_DEV, D).transpose(1, 0, 2, 3).reshape(N_DEV, 3 * (D // N_DEV), D)
    dqk, dgv, drank, dwdec, dbdec = _gla_bwd_call(projf, projb, la, do_gla, st_all, rank, wdec)
    dsq, dsk, dsv, r_p = _sb_bwd_call(projb, do_sb, g_p)
    pieces = [dqk, dgv, dggate, dsq, dsk, dsv, dsgate]
    rhs_of_group = [(g, 0) for g in range(7)] + [(7, 0), (7, 1)]
    dw_in = _wgrad_call([ht], [0] * N_GROUPS, pieces + [dmlog], rhs_of_group, N_GROUPS, "wgrad_in")
    dwr = _wgrad_rank_call(ht, drank)
    g_in, p_in = _parts_pair_call(dw_in.reshape(N_GROUPS * 1024, D), dwr[:, :GLA_RANK].T.astype(BF16))
    s_in = _pair_add_call(g_in, p_in, "pair_add_in")
    small = jnp.concatenate([
        jnp.zeros((D,), F32), dbdec.reshape(-1), dgla_g.reshape(-1), db_gate.reshape(-1), dfinal_g.reshape(-1),
        loss.reshape(-1), dwdec[:GLA_RANK].reshape(-1)]).reshape(1, _SM_LEN)
    grad_x, r_in, _, r_small = _dh_call(pieces, dmlog, drank, wt, wr, x, dx2, norm_g, s_in, small)
    return grad_x, r_in, r_p, r_small


_SM_NORM = 0
_SM_BDEC = _SM_NORM + D_MODEL
_SM_GLAG = _SM_BDEC + GLA_DK
_SM_BGATE = _SM_GLAG + GLA_HV
_SM_FINAL = _SM_BGATE + 2 * D_MODEL
_SM_REPL = _SM_FINAL + D_MODEL
_SM_LOSS = _SM_REPL
_SM_WDEC = _SM_LOSS + 128
_SM_LEN = _SM_WDEC + GLA_RANK * GLA_DK


def kernel(x, norm_g, w_in, w_dec_up, b_dec, gla_norm_g, w_pa, w_pb, b_gate, w_o, final_g, loss_target, m_norm_g, m_w_in, m_w_dec_up, m_b_dec, m_gla_norm_g, m_w_pa, m_w_pb, m_b_gate, m_w_o, m_final_g, v_norm_g, v_w_in, v_w_dec_up, v_b_dec, v_gla_norm_g, v_w_pa, v_w_pb, v_b_gate, v_w_o, v_final_g):
    D = D_MODEL
    me = 4 * lax.axis_index("x") + 2 * lax.axis_index("y") + lax.axis_index("c")

    wp_shard = jnp.stack([w_pa, w_pb, w_o]).astype(BF16)
    n_first = _half_rows(SHARD_COLS)
    win_all, wdec_all = _all_gather([w_in.T.astype(BF16), w_dec_up], "gather_w",
                                    row_pieces=[[(0, n_first), (n_first, SHARD_COLS - n_first)], None])
    wt = _flatten_blocks_call(win_all)
    wr = jnp.pad(wt[RANK_COL:RANK_COL + GLA_RANK], ((0, 128 - GLA_RANK), (0, 0)))
    wdec_full = wdec_all.transpose(1, 0, 2).reshape(GLA_RANK, GLA_DK)
    wdec = jnp.pad(wdec_full, ((0, 128 - GLA_RANK), (0, 0)))

    grad_x, r_in, r_p, r_small = _local_step(
        x[0], loss_target[0], wt, wr, wdec, b_dec.reshape(1, -1), wp_shard,
        norm_g.reshape(1, -1), gla_norm_g.reshape(1, -1), b_gate.reshape(1, -1), final_g.reshape(1, -1))

    gw_in, d_in, nm_in, nv_in = (a.T for a in _adamw_call(r_in, w_in.T, m_w_in.T, v_w_in.T, "adamw_in"))
    (g_pa, d_pa, nm_pa, nv_pa), (g_pb, d_pb, nm_pb, nv_pb), (g_o, d_o, nm_o, nv_o) = _adamw_rows_call(
        r_p, [w_pa, w_pb, w_o], [m_w_pa, m_w_pb, m_w_o], [v_w_pa, v_w_pb, v_w_o], "adamw_p")

    def row(a):
        return a.reshape(1, -1)

    rep = _adamw_lanes_call(
        r_small, [_SM_NORM, _SM_BDEC, _SM_GLAG, _SM_BGATE, _SM_FINAL],
        [row(a) for a in (norm_g, b_dec, gla_norm_g, b_gate, final_g)],
        [row(a) for a in (m_norm_g, m_b_dec, m_gla_norm_g, m_b_gate, m_final_g)],
        [row(a) for a in (v_norm_g, v_b_dec, v_gla_norm_g, v_b_gate, v_final_g)], "adamw_rep")
    ((g_norm, d_norm, nm_norm, nv_norm), (g_bdec, d_bdec, nm_bdec, nv_bdec), (g_glag, d_glag, nm_glag, nv_glag),
     (g_bgate, d_bgate, nm_bgate, nv_bgate), (g_final, d_final, nm_final, nv_final)) = [
        tuple(a.reshape(-1) for a in quad) for quad in rep]

    wdec_parts = r_small[:, 0, _SM_WDEC:].reshape(N_DEV, GLA_RANK, GLA_DK)
    cols = GLA_DK // N_DEV
    wdec_mine = lax.dynamic_slice_in_dim(wdec_parts, me * cols, cols, axis=2)
    g_wdec, d_wdec, nm_wdec, nv_wdec = _adamw_call(wdec_mine, w_dec_up, m_w_dec_up, v_w_dec_up, "adamw_dec")

    loss_total = jnp.sum(r_small[:, 0, _SM_LOSS])

    return (loss_total, grad_x[None],
            g_norm, gw_in, g_wdec, g_bdec, g_glag, g_pa, g_pb, g_bgate, g_o, g_final,
            d_norm, d_in, d_wdec, d_bdec, d_glag, d_pa, d_pb, d_bgate, d_o, d_final,
            nm_norm, nm_in, nm_wdec, nm_bdec, nm_glag, nm_pa, nm_pb, nm_bgate, nm_o, nm_final,
            nv_norm, nv_in, nv_wdec, nv_bdec, nv_glag, nv_pa, nv_pb, nv_bgate, nv_o, nv_final)
```

```python
import math

import jax
import jax.numpy as jnp
from jax import lax
from jax.experimental import pallas as pl
from jax.experimental.pallas import tpu as pltpu

F32 = jnp.float32
BF16 = jnp.bfloat16

N_DEV = 8
D_MODEL = 1024
GLA_HEADS = 4
GLA_HK = 128
GLA_HV = 256
GLA_DK = 512
GLA_RANK = 16
GLA_TAU = 16.0
GLA_CHUNK = 64
SB_HEADS = 8
SB_HD = 128
EPS = 1e-6
N_GROUPS = 9
RANK_COL = 3072
IN_COLS = 9232
SHARD_COLS = IN_COLS // N_DEV

ADAM_LR = 0.001
ADAM_B1 = 0.9
ADAM_B2 = 0.999
ADAM_EPS = 1e-08
ADAM_WD = 0.01
ADAM_STEP = 10

VMEM_LIMIT = 56 * 1024 * 1024
TBLK = 256


def _cparams(sem=None):
    return pltpu.CompilerParams(dimension_semantics=sem, vmem_limit_bytes=VMEM_LIMIT)


def _tiling_2d(rows, cols, band_cols):
    if rows * cols <= 128 * 1024:
        return (rows, cols), (1,), lambda i: (0, 0)
    if rows % 128 == 0:
        return (128, cols), (rows // 128,), lambda i: (i, 0)
    tc = band_cols if cols % band_cols == 0 else cols
    return (rows, tc), (cols // tc,), lambda i: (0, i)


def _dot(a, b):
    return jnp.dot(a, b, preferred_element_type=F32)


def _dot_nt(a, b):
    return lax.dot_general(a, b, (((1,), (1,)), ((), ())), preferred_element_type=F32)


def _dot_tn(a, b):
    return lax.dot_general(a, b, (((0,), (0,)), ((), ())), preferred_element_type=F32)


def _bf(x):
    return x.astype(BF16)


def _split3(x):
    hi = x.astype(BF16)
    r = x - hi.astype(F32)
    mid = r.astype(BF16)
    lo = (r - mid.astype(F32)).astype(BF16)
    return hi, mid, lo


def _tri_left(tri, x):
    hi, mid, lo = _split3(x)
    return _dot(tri, hi) + _dot(tri, mid) + _dot(tri, lo)


def _split2(x):
    hi = lax.bitcast_convert_type(lax.bitcast_convert_type(x, jnp.uint32) & jnp.uint32(0xFFFF0000), F32)
    return hi.astype(BF16), (x - hi).astype(BF16)


def _tri2_left(tri, x):
    hi, lo = _split2(x)
    return _dot(tri, hi) + _dot(tri, lo)


def _tri2_right(x, tri):
    hi, lo = _split2(x)
    return _dot(hi, tri) + _dot(lo, tri)


def _iota2(n, m, dim):
    return lax.broadcasted_iota(jnp.int32, (n, m), dim)


def _sigmoid(x):
    return 1.0 / (1.0 + jnp.exp(-x))


def _softplus_neg_abs(z):
    return jnp.log(1.0 + jnp.exp(-jnp.abs(z)))


_ANY = pl.BlockSpec(memory_space=pl.ANY)


def _mesh_pos():
    return lax.axis_index("x"), lax.axis_index("y"), lax.axis_index("c")


def _other_chips(x, y):
    return [(1 - x, y), (x, 1 - y), (1 - x, 1 - y)]


def _rcopy(src, dst, send_sem, recv_sem, to):
    return pltpu.make_async_remote_copy(src_ref=src, dst_ref=dst, send_sem=send_sem, recv_sem=recv_sem,
                                        device_id=to, device_id_type=pl.DeviceIdType.MESH)


def _push_copies(src_ref, dst_ref, send_sems, recv_sems, loc_sem, scatter):
    x, y, c = _mesh_pos()
    me = 4 * x + 2 * y + c
    own = pltpu.make_async_copy(src_ref.at[me] if scatter else src_ref, dst_ref.at[me], loc_sem)
    pairs = []
    for k in range(1, N_DEV):
        px = 1 - x if k & 4 else x
        py = 1 - y if k & 2 else y
        pc = 1 - c if k & 1 else c
        pid = 4 * px + 2 * py + pc
        src = src_ref.at[pid] if scatter else src_ref
        send = _rcopy(src, dst_ref.at[me], send_sems.at[k - 1], recv_sems.at[k - 1], (px, py, pc))
        recv = _rcopy(src, dst_ref.at[pid], send_sems.at[k - 1], recv_sems.at[k - 1], (px, py, pc))
        pairs.append((send, recv))
    return own, pairs


def _push_start(own, pairs):
    own.start()
    for send, _ in pairs:
        send.start()


def _push_wait(own, pairs):
    for _, recv in pairs:
        recv.wait_recv()
    for send, _ in pairs:
        send.wait_send()
    own.wait()


_PUSH_SEMS = [pltpu.SemaphoreType.DMA((N_DEV - 1,)), pltpu.SemaphoreType.DMA((N_DEV - 1,)),
              pltpu.SemaphoreType.DMA]


def _half_rows(rows):
    return (rows // 2) // 16 * 16


_ADD_ROWS = 128


def _chip_reduce_steps(src_ref, dst_ref, relayed_ref, sum_x, sum_y, rel_x, rel_y, load_sems, send_sems, recv_sems,
                       loc_sem):
    _, R, C = src_ref.shape
    n0 = _half_rows(R)
    lo, hi = pl.ds(0, n0), pl.ds(n0, R - n0)
    x, y, c = _mesh_pos()
    (xx, xy), (yx, yy), (dx, dy) = _other_chips(x, y)
    to_diag, to_x, to_y = src_ref.at[2 * dx + dy], src_ref.at[2 * xx + xy], src_ref.at[2 * yx + yy]
    x_nb, y_nb = (xx, xy, c), (yx, yy, c)
    relays = (_rcopy(to_diag.at[lo], relayed_ref.at[lo], send_sems.at[0], recv_sems.at[0], x_nb),
              _rcopy(to_diag.at[hi], relayed_ref.at[hi], send_sems.at[1], recv_sems.at[1], y_nb))
    plain = (_rcopy(to_x.at[lo], dst_ref.at[0, lo], send_sems.at[2], recv_sems.at[2], x_nb),
             _rcopy(to_y.at[hi], dst_ref.at[1, hi], send_sems.at[3], recv_sems.at[3], y_nb))
    summed = (_rcopy(sum_x, dst_ref.at[0, hi], send_sems.at[4], recv_sems.at[4], x_nb),
              _rcopy(sum_y, dst_ref.at[1, lo], send_sems.at[5], recv_sems.at[5], y_nb))
    load_mine = (pltpu.make_async_copy(to_x.at[hi], sum_x, load_sems.at[0]),
                 pltpu.make_async_copy(to_y.at[lo], sum_y, load_sems.at[1]))
    load_relayed = (pltpu.make_async_copy(relayed_ref.at[hi], rel_x, load_sems.at[2]),
                    pltpu.make_async_copy(relayed_ref.at[lo], rel_y, load_sems.at[3]))
    own = pltpu.make_async_copy(src_ref.at[2 * x + y], dst_ref.at[2], loc_sem)

    def start():
        for cp in relays + plain + (own,) + load_mine:
            cp.start()

    def add(acc_ref, rel_ref):
        for r0 in range(0, acc_ref.shape[0], _ADD_ROWS):
            rows = slice(r0, min(r0 + _ADD_ROWS, acc_ref.shape[0]))
            acc_ref[rows, :] = (acc_ref[rows, :].astype(F32) + rel_ref[rows, :].astype(F32)).astype(acc_ref.dtype)

    def forward():
        for cp in relays:
            cp.wait_recv()
        for cp in load_relayed:
            cp.start()
        for cp in load_mine + load_relayed:
            cp.wait()
        add(sum_x, rel_x)
        add(sum_y, rel_y)
        for cp in summed:
            cp.start()

    def finish():
        for cp in plain + summed:
            cp.wait_recv()
        for cp in relays + plain + summed:
            cp.wait_send()
        own.wait()

    return start, forward, finish


def _chip_reduce_scratch(rows, cols, dtype):
    n0 = _half_rows(rows)
    return [pltpu.VMEM((rows - n0, cols), dtype), pltpu.VMEM((n0, cols), dtype)] * 2 + [
        pltpu.SemaphoreType.DMA((4,)), pltpu.SemaphoreType.DMA((6,)), pltpu.SemaphoreType.DMA((6,)),
        pltpu.SemaphoreType.DMA]


def _all_gather(arrs, name, row_pieces=None):
    n = len(arrs)
    pieces = [[None] if not row_pieces or not row_pieces[a] else list(row_pieces[a]) for a in range(n)]
    assert all(len(p) in (1, 2) for p in pieces)
    units = [(a, i) for a in range(n) for i in range(len(pieces[a]))]

    def body(*refs):
        ins = refs[:n]
        outs = refs[n:2 * n]
        send_sems, recv_sems, loc_sems = refs[2 * n:]
        x, y, c = _mesh_pos()
        me, sib = (x, y, c), (x, y, 1 - c)
        xn, yn, dg = [(px, py, c) for px, py in _other_chips(x, y)]

        def rows(ref, a, i):
            return ref if pieces[a][i] is None else ref.at[pl.ds(*pieces[a][i])]

        def copy(u, k, block, to, own=False):
            a, i = u
            px, py, pc = block
            dst = rows(outs[a].at[4 * px + 2 * py + pc], a, i)
            return _rcopy(rows(ins[a], a, i) if own else dst, dst, send_sems.at[a, k, i], recv_sems.at[a, k, i], to)

        started = []

        def start(cp):
            cp.start()
            started.append(cp)

        def landed_then_pass_on(u, k, block):
            copy(u, k, block, me).wait_recv()
            start(copy(u, 3 + k, block, sib))

        mine = [pltpu.make_async_copy(ins[a], outs[a].at[4 * x + 2 * y + c], loc_sems.at[a]) for a in range(n)]
        for cp in mine:
            cp.start()
        for u in units:
            start(copy(u, 0, me, sib, own=True))
        for a in range(n):
            if len(pieces[a]) == 2:
                for i, to, k in ((0, xn, 1), (1, yn, 2), (1, xn, 1), (0, yn, 2)):
                    start(copy((a, i), k, me, to, own=True))
            else:
                for to, k in ((xn, 1), (yn, 2), (dg, 3)):
                    start(copy((a, 0), k, me, to, own=True))
        for a in range(n):
            if len(pieces[a]) == 2:
                landed_then_pass_on((a, 0), 1, xn)
                start(copy((a, 0), 3, xn, yn))
                landed_then_pass_on((a, 1), 2, yn)
                start(copy((a, 1), 3, yn, xn))
                landed_then_pass_on((a, 1), 1, xn)
                landed_then_pass_on((a, 0), 2, yn)
                landed_then_pass_on((a, 0), 3, dg)
                landed_then_pass_on((a, 1), 3, dg)
            else:
                for block, k in ((xn, 1), (yn, 2), (dg, 3)):
                    landed_then_pass_on((a, 0), k, block)
        for u in units:
            copy(u, 0, sib, me).wait_recv()
            for k, (px, py, _) in ((4, xn), (5, yn), (6, dg)):
                copy(u, k, (px, py, 1 - c), me).wait_recv()
        for cp in started:
            cp.wait_send()
        for cp in mine:
            cp.wait()

    n_pc = max(len(p) for p in pieces)

    return pl.pallas_call(
        body, name=name,
        out_shape=tuple(jax.ShapeDtypeStruct((N_DEV,) + a.shape, a.dtype) for a in arrs),
        in_specs=[_ANY] * n,
        out_specs=tuple([_ANY] * n),
        scratch_shapes=[pltpu.SemaphoreType.DMA((n, 7, n_pc)), pltpu.SemaphoreType.DMA((n, 7, n_pc)),
                        pltpu.SemaphoreType.DMA((n,))],
    )(*arrs)


def _flatten_blocks_call(blocks):
    n, R, C = blocks.shape
    tc = C // 2

    def body(in_ref, out_ref):
        for p in range(n):
            out_ref[p * R:(p + 1) * R, :] = in_ref[p]

    return pl.pallas_call(
        body, name="flatten_w",
        grid=(C // tc,),
        in_specs=[pl.BlockSpec((n, R, tc), lambda i: (0, 0, i))],
        out_specs=pl.BlockSpec((n * R, tc), lambda i: (0, i)),
        out_shape=jax.ShapeDtypeStruct((n * R, C), blocks.dtype),
        compiler_params=_cparams(("arbitrary",)),
    )(blocks)


_PARTS_BANDS = 8


def _pair_sum_call(dmain, drank):
    D = dmain.shape[1]
    n = _PARTS_BANDS
    tc = D // n

    def body(dm_ref, dr_ref, sum_ref, laid, got, send_sems, recv_sems):
        x, y, c = _mesh_pos()

        def pushes(k):
            return [_rcopy(laid.at[k % 2, 2 * q + (1 - c)], got.at[k, q], send_sems.at[k, q], recv_sems.at[k, q],
                           (x, y, 1 - c)) for q in range(4)]

        def lay_out(k):
            for p in range(N_DEV):
                lo, hi = p * SHARD_COLS, (p + 1) * SHARD_COLS
                at = 0
                for src, a, b in ((dm_ref, lo, min(hi, RANK_COL)),
                                  (dr_ref, max(lo, RANK_COL) - RANK_COL, min(hi, RANK_COL + GLA_RANK) - RANK_COL),
                                  (dm_ref, max(lo, RANK_COL + GLA_RANK) - GLA_RANK, hi - GLA_RANK)):
                    if b > a:
                        laid[k % 2, p, at:at + (b - a), :] = src[a:b, :]
                        at += b - a

        for k in range(n + 1):
            @pl.when(pl.program_id(0) == k)
            def _(k=k):
                if k < n:
                    if k >= 2:
                        for cp in pushes(k - 2):
                            cp.wait_send()
                    lay_out(k)
                    for cp in pushes(k):
                        cp.start()
                if k >= 1:
                    for cp in pushes(k - 1):
                        cp.wait_recv()
                    for q in range(4):
                        sum_ref[q] = (laid[(k - 1) % 2, 2 * q + c].astype(F32)
                                      + got[k - 1, q].astype(F32)).astype(sum_ref.dtype)
                if k == n:
                    for k_open in range(max(0, n - 2), n):
                        for cp in pushes(k_open):
                            cp.wait_send()

    sems = pltpu.SemaphoreType.DMA((n, 4))
    return pl.pallas_call(
        body, name="pair_sum",
        grid=(n + 1,),
        in_specs=[pl.BlockSpec((dmain.shape[0], tc), lambda k: (0, jnp.minimum(k, n - 1))),
                  pl.BlockSpec((GLA_RANK, tc), lambda k: (0, jnp.minimum(k, n - 1)))],
        out_specs=pl.BlockSpec((4, SHARD_COLS, tc), lambda k: (0, 0, jnp.maximum(k - 1, 0))),
        out_shape=jax.ShapeDtypeStruct((4, SHARD_COLS, D), dmain.dtype),
        scratch_shapes=[pltpu.VMEM((2, N_DEV, SHARD_COLS, tc), dmain.dtype),
                        pltpu.VMEM((n, 4, SHARD_COLS, tc), dmain.dtype), sems, sems],
        compiler_params=_cparams(("arbitrary",)),
    )(dmain, drank)


def _group_row(g):
    return GLA_RANK * (g * (1024 // GLA_RANK) + (g >= RANK_COL // 1024))


def _proj_call(x, norm_g, wt, wr, wp_part):
    T, D = x.shape
    tm = min(1024, T)
    assert tm % TBLK == 0
    n_i = T // tm

    def f_slot(j):
        return ((j >= 2).astype(jnp.int32) + (j >= 6).astype(jnp.int32)
                + (j >= 7).astype(jnp.int32) + (j >= 8).astype(jnp.int32))

    def b_slot(j):
        return (j >= 3).astype(jnp.int32) + (j >= 4).astype(jnp.int32) + (j >= 5).astype(jnp.int32)

    def body(x_ref, g_ref, w_ref, wr_ref, wp_ref, pf_ref, pb_ref, rank_ref, ht_ref, wpall_ref,
             h_scr, send_sems, recv_sems, loc_sem):
        i = pl.program_id(0)
        j = pl.program_id(1)
        own, pairs = _push_copies(wp_ref, wpall_ref, send_sems, recv_sems, loc_sem, scatter=False)

        @pl.when((i == 0) & (j == 0))
        def _():
            _push_start(own, pairs)

        @pl.when(j == 0)
        def _():
            xv = x_ref[...]
            r = lax.rsqrt(jnp.mean(xv * xv, axis=-1, keepdims=True) + EPS)
            h = (xv * r) * g_ref[...]
            hb = _bf(h)
            h_scr[...] = hb
            for b in range(tm // TBLK):
                ht_ref[b] = _bf(h[b * TBLK:(b + 1) * TBLK].T)
            rank_ref[...] = _dot_nt(hb, wr_ref[...])

        is_b = (j == 1) | ((j >= 3) & (j <= 5))

        @pl.when(is_b)
        def _():
            pb_ref[...] = _bf(_dot_nt(h_scr[...], w_ref[...]))

        @pl.when(jnp.logical_not(is_b))
        def _():
            pf_ref[...] = _dot_nt(h_scr[...], w_ref[...])

        @pl.when((i == n_i - 1) & (j == N_GROUPS - 1))
        def _():
            _push_wait(own, pairs)

    return pl.pallas_call(
        body, name="proj",
        grid=(n_i, N_GROUPS),
        in_specs=[pl.BlockSpec((tm, D), lambda i, j: (i, 0)),
                  pl.BlockSpec((1, D), lambda i, j: (0, 0)),
                  pl.BlockSpec((pl.Element(1024), pl.Element(D)), lambda i, j: (_group_row(j), 0)),
                  pl.BlockSpec((128, D), lambda i, j: (0, 0)),
                  _ANY],
        out_specs=(pl.BlockSpec((None, tm, 1024), lambda i, j: (f_slot(j), i, 0)),
                   pl.BlockSpec((None, tm, 1024), lambda i, j: (b_slot(j), i, 0)),
                   pl.BlockSpec((tm, 128), lambda i, j: (i, 0)),
                   pl.BlockSpec((tm // TBLK, D, TBLK), lambda i, j: (i, 0, 0)),
                   _ANY),
        out_shape=(jax.ShapeDtypeStruct((5, T, 1024), F32),
                   jax.ShapeDtypeStruct((4, T, 1024), BF16),
                   jax.ShapeDtypeStruct((T, 128), F32),
                   jax.ShapeDtypeStruct((T // TBLK, D, TBLK), BF16),
                   jax.ShapeDtypeStruct((N_DEV,) + wp_part.shape, wp_part.dtype)),
        scratch_shapes=[pltpu.VMEM((tm, D), BF16)] + _PUSH_SEMS,
        compiler_params=_cparams(("arbitrary", "arbitrary")),
    )(x, norm_g, wt, wr, wp_part)


GLA_STEP_CHUNKS = 4


def _gla_same_chunk(rows):
    return (_iota2(rows, rows, 0) & -GLA_CHUNK) == (_iota2(rows, rows, 1) & -GLA_CHUNK)


def _gla_chunk_terms(la, q, k, n_c):
    C = GLA_CHUNK
    rows = n_c * C
    low = _gla_same_chunk(rows) & (_iota2(rows, rows, 0) >= _iota2(rows, rows, 1))
    b = _tri_left(_bf(low.astype(F32)), la)
    bl = [b[(c + 1) * C - 1:(c + 1) * C, :] for c in range(n_c)]
    bl_rows = jnp.concatenate([jnp.broadcast_to(bl[c], (C, b.shape[1])) for c in range(n_c)], axis=0)
    eb = jnp.exp(b)
    enb = jnp.exp(-b)
    ebl_b = jnp.exp(bl_rows - b)
    scale = GLA_HK ** -0.5
    qe = q * eb * scale
    ke = k * enb
    kd = k * ebl_b
    return bl, eb, enb, ebl_b, qe, ke, kd


def _gla_fwd_call(projf, projb, rank, wdec, bdec):
    T = projf.shape[1]
    C = GLA_CHUNK
    n_chunks = T // C
    n_c = GLA_STEP_CHUNKS
    R = n_c * C
    assert n_chunks % n_c == 0

    def body(qk_ref, v_ref, rank_ref, wd_ref, bd_ref, o_ref, st_ref, la_ref, st_scr):
        @pl.when(pl.program_id(0) == 0)
        def _():
            st_scr[...] = jnp.zeros_like(st_scr)

        dec = _dot(_bf(rank_ref[...]), _bf(wd_ref[...])) + bd_ref[...]
        la = (jnp.minimum(dec, 0.0) - _softplus_neg_abs(dec)) / GLA_TAU
        la_ref[...] = la
        mask = _gla_same_chunk(R) & (_iota2(R, R, 0) >= _iota2(R, R, 1))
        bl, _, _, _, qe, ke, kd = _gla_chunk_terms(la, qk_ref[:, :GLA_DK], qk_ref[:, GLA_DK:], n_c)
        qeb, keb, kdb = _bf(qe), _bf(ke), _bf(kd)
        ebl = [jnp.exp(bl[c]) for c in range(n_c)]
        heads = range(GLA_HEADS)
        ks = [slice(hh * GLA_HK, (hh + 1) * GLA_HK) for hh in heads]
        vs = [slice(hh * GLA_HV, (hh + 1) * GLA_HV) for hh in heads]
        rs = [slice(c * C, (c + 1) * C) for c in range(n_c)]
        p = [_bf(jnp.where(mask, _dot_nt(qeb[:, ks[hh]], keb[:, ks[hh]]), 0.0)) for hh in heads]
        upd = [[_dot_tn(v_ref[rs[c], vs[hh]], kdb[rs[c], ks[hh]]) for hh in heads] for c in range(n_c)]
        intra = [_dot(p[hh], v_ref[:, vs[hh]]) for hh in heads]
        st = [st_scr[hh] for hh in heads]
        for c in range(n_c):
            inter = [_dot_nt(qeb[rs[c], ks[hh]], _bf(st[hh])) for hh in heads]
            for hh in heads:
                st_ref[c, hh] = st[hh]
                o_ref[rs[c], vs[hh]] = intra[hh][rs[c]] + inter[hh]
            st = [st[hh] * ebl[c][:, ks[hh]] + upd[c][hh] for hh in heads]
        for hh in heads:
            st_scr[hh] = st[hh]

    return pl.pallas_call(
        body, name="gla_fwd",
        grid=(n_chunks // n_c,),
        in_specs=[pl.BlockSpec((None, R, 1024), lambda n: (0, n, 0)),
                  pl.BlockSpec((None, R, 1024), lambda n: (0, n, 0)),
                  pl.BlockSpec((R, 128), lambda n: (n, 0)),
                  pl.BlockSpec((128, GLA_DK), lambda n: (0, 0)),
                  pl.BlockSpec((1, GLA_DK), lambda n: (0, 0))],
        out_specs=(pl.BlockSpec((R, 1024), lambda n: (n, 0)),
                   pl.BlockSpec((n_c, GLA_HEADS, GLA_HV, GLA_HK), lambda n: (n, 0, 0, 0)),
                   pl.BlockSpec((R, GLA_DK), lambda n: (n, 0))),
        out_shape=(jax.ShapeDtypeStruct((T, 1024), F32),
                   jax.ShapeDtypeStruct((n_chunks, GLA_HEADS, GLA_HV, GLA_HK), F32),
                   jax.ShapeDtypeStruct((T, GLA_DK), F32)),
        scratch_shapes=[pltpu.VMEM((GLA_HEADS, GLA_HV, GLA_HK), F32)],
        compiler_params=_cparams(("arbitrary",)),
    )(projf, projb, rank, wdec, bdec)


def _gla_bwd_call(projf, projb, la, do_gla, st_all, rank, wdec):
    T = projf.shape[1]
    C = GLA_CHUNK
    n_chunks = T // C
    n_c = GLA_STEP_CHUNKS
    R = n_c * C
    assert n_chunks % n_c == 0
    last = n_chunks // n_c - 1

    def body(qk_ref, v_ref, la_ref, do_ref, st_ref, rank_ref, wd_ref,
             dqk_ref, dv_ref, drank_ref, dwd_ref, dbd_ref, dst_scr):
        @pl.when(pl.program_id(0) == 0)
        def _():
            dst_scr[...] = jnp.zeros_like(dst_scr)
            dwd_ref[...] = jnp.zeros_like(dwd_ref)
            dbd_ref[...] = jnp.zeros_like(dbd_ref)

        same = _gla_same_chunk(R)
        mask = same & (_iota2(R, R, 0) >= _iota2(R, R, 1))
        upp = _bf((same & (_iota2(R, R, 0) <= _iota2(R, R, 1))).astype(F32))
        scale = GLA_HK ** -0.5
        la = la_ref[...]
        bl, eb, enb, ebl_b, qe, ke, kd = _gla_chunk_terms(la, qk_ref[:, :GLA_DK], qk_ref[:, GLA_DK:], n_c)
        qeb, keb, kdb = _bf(qe), _bf(ke), _bf(kd)
        ebl = [jnp.exp(bl[c]) for c in range(n_c)]
        heads = range(GLA_HEADS)
        ks = [slice(hh * GLA_HK, (hh + 1) * GLA_HK) for hh in heads]
        vs = [slice(hh * GLA_HV, (hh + 1) * GLA_HV) for hh in heads]
        rs = [slice(c * C, (c + 1) * C) for c in range(n_c)]
        v = [v_ref[:, vs[hh]] for hh in heads]
        do = [_bf(do_ref[:, vs[hh]]) for hh in heads]
        p = [_bf(jnp.where(mask, _dot_nt(qeb[:, ks[hh]], keb[:, ks[hh]]), 0.0)) for hh in heads]
        dp = [_bf(jnp.where(mask, _dot_nt(do[hh], v[hh]), 0.0)) for hh in heads]
        dst_intra = [[_dot_tn(do[hh][rs[c]], qeb[rs[c], ks[hh]]) for hh in heads] for c in range(n_c)]
        dqe_inter = [[_dot(do[hh][rs[c]], _bf(st_ref[c, hh])) for hh in heads] for c in range(n_c)]
        dv_intra = [_dot_tn(p[hh], do[hh]) for hh in heads]
        dqe_intra = [_dot(dp[hh], keb[:, ks[hh]]) for hh in heads]
        dke = jnp.concatenate([_dot_tn(dp[hh], qeb[:, ks[hh]]) for hh in heads], axis=1)
        dstn = [dst_scr[hh] for hh in heads]
        dkd_c, dv_inter, debl = [None] * n_c, [None] * n_c, [None] * n_c
        for c in reversed(range(n_c)):
            dstnb = [_bf(dstn[hh]) for hh in heads]
            dkd_c[c] = jnp.concatenate([_dot(v[hh][rs[c]], dstnb[hh]) for hh in heads], axis=1)
            dv_inter[c] = [_dot_nt(kdb[rs[c], ks[hh]], dstnb[hh]) for hh in heads]
            debl[c] = jnp.concatenate(
                [jnp.sum(dstn[hh] * st_ref[c, hh], axis=0, keepdims=True) for hh in heads], axis=1)
            dstn = [dst_intra[c][hh] + dstn[hh] * ebl[c][:, ks[hh]] for hh in heads]
        for hh in heads:
            dst_scr[hh] = dstn[hh]
            dv_ref[:, vs[hh]] = _bf(dv_intra[hh] + jnp.concatenate([dv_inter[c][hh] for c in range(n_c)], axis=0))
        dqe = jnp.concatenate(
            [dqe_intra[hh] + jnp.concatenate([dqe_inter[c][hh] for c in range(n_c)], axis=0) for hh in heads], axis=1)
        dkd = jnp.concatenate(dkd_c, axis=0)
        dkd_kd = dkd * kd
        db = dqe * qe - dke * ke - dkd_kd
        dbl = jnp.concatenate(
            [jnp.broadcast_to(jnp.sum(dkd_kd[rs[c]], axis=0, keepdims=True) + ebl[c] * debl[c], (C, GLA_DK))
             for c in range(n_c)], axis=0)
        dla = _tri_left(upp, db) + dbl
        dqk_ref[:, :GLA_DK] = _bf(dqe * eb * scale)
        dqk_ref[:, GLA_DK:] = _bf(dke * enb + dkd * ebl_b)
        ddec = dla * (1.0 / GLA_TAU) * (1.0 - jnp.exp(GLA_TAU * la))
        ddecb = _bf(ddec)
        drank_ref[...] = _bf(_dot_nt(ddecb, _bf(wd_ref[...])))
        dwd_ref[...] += _dot_tn(_bf(rank_ref[...]), ddecb)
        dbd_ref[...] += jnp.sum(ddec, axis=0, keepdims=True)

    return pl.pallas_call(
        body, name="gla_bwd",
        grid=(n_chunks // n_c,),
        in_specs=[pl.BlockSpec((None, R, 1024), lambda n: (0, last - n, 0)),
                  pl.BlockSpec((None, R, 1024), lambda n: (0, last - n, 0)),
                  pl.BlockSpec((R, GLA_DK), lambda n: (last - n, 0)),
                  pl.BlockSpec((R, 1024), lambda n: (last - n, 0)),
                  pl.BlockSpec((n_c, GLA_HEADS, GLA_HV, GLA_HK), lambda n: (last - n, 0, 0, 0)),
                  pl.BlockSpec((R, 128), lambda n: (last - n, 0)),
                  pl.BlockSpec((128, GLA_DK), lambda n: (0, 0))],
        out_specs=(pl.BlockSpec((R, 1024), lambda n: (last - n, 0)),
                   pl.BlockSpec((R, 1024), lambda n: (last - n, 0)),
                   pl.BlockSpec((R, 128), lambda n: (last - n, 0)),
                   pl.BlockSpec((128, GLA_DK), lambda n: (0, 0)),
                   pl.BlockSpec((1, GLA_DK), lambda n: (0, 0))),
        out_shape=(jax.ShapeDtypeStruct((T, 1024), BF16),
                   jax.ShapeDtypeStruct((T, 1024), BF16),
                   jax.ShapeDtypeStruct((T, 128), BF16),
                   jax.ShapeDtypeStruct((128, GLA_DK), F32),
                   jax.ShapeDtypeStruct((1, GLA_DK), F32)),
        scratch_shapes=[pltpu.VMEM((GLA_HEADS, GLA_HV, GLA_HK), F32)],
        compiler_params=_cparams(("arbitrary",)),
    )(projf, projb, la, do_gla, st_all, rank, wdec)


def _sb_logs(z):
    lsz = jnp.minimum(z, 0.0) - _softplus_neg_abs(z)
    return lsz, lsz - z


SB_HG_FWD = 8
SB_HG_BWD = 4
SB_QUERIES = 256
SB_KEYS = 256
SB_DEAD = -105.0


def _sb_fwd_call(projb, wp_shard):
    T = projb.shape[1]
    B = min(SB_QUERIES, T)
    HG = SB_HG_FWD
    W = HG * SB_HD
    scale = 1.0 / math.sqrt(SB_HD)
    KB = min(SB_KEYS, T)
    n_h, n_i = SB_HEADS // HG, T // B

    def body(q_ref, k_ref, v_ref, wp_ref, o_ref, wpall_ref, cb_scr, send_sems, recv_sems, loc_sem):
        i = pl.program_id(1)
        own, pairs = _push_copies(wp_ref, wpall_ref, send_sems, recv_sems, loc_sem, scatter=False)

        @pl.when((pl.program_id(0) == 0) & (i == 0))
        def _():
            _push_start(own, pairs)

        rows = HG * B
        after = (_iota2(KB, KB, 0) > _iota2(KB, KB, 1)).astype(F32)
        tri = _bf(jnp.concatenate([after, jnp.ones((KB, KB), F32)], axis=1))
        o_ref[...] = jnp.zeros_like(o_ref)
        cb_scr[...] = jnp.zeros_like(cb_scr)

        def block(jp, masked):
            off = pl.multiple_of(jp * KB, KB)
            z = jnp.concatenate(
                [_dot_nt(q_ref[:, hh * SB_HD:(hh + 1) * SB_HD], k_ref[pl.ds(off, KB), hh * SB_HD:(hh + 1) * SB_HD])
                 for hh in range(HG)], axis=0) * scale
            lsz, l1m = _sb_logs(z)
            if masked:
                strict = (jp * KB + _iota2(rows, KB, 1)) < (i * B + (_iota2(rows, KB, 0) & (B - 1)))
                l1m = jnp.where(strict, l1m, 0.0)
            r = _tri2_right(l1m, tri)
            cb = cb_scr[...]
            a = jnp.exp(lsz + cb + r[:, :KB])
            if masked:
                a = jnp.where(strict, a, 0.0)
            cb_scr[...] = cb + r[:, KB:]
            ab = _bf(a)
            for hh in range(HG):
                cs = slice(hh * SB_HD, (hh + 1) * SB_HD)
                o_ref[:, cs] += _dot(ab[hh * B:(hh + 1) * B, :], v_ref[pl.ds(off, KB), cs])

        jp0 = (i * B) // KB
        block(jp0, True)

        def live(state):
            jj, dead = state
            return (jj <= jp0) & jnp.logical_not(dead)

        def step(state):
            jj, _ = state
            block(jp0 - jj, False)
            return jj + 1, jnp.max(cb_scr[:, :SB_HD]) < SB_DEAD

        lax.while_loop(live, step, (jnp.int32(1), jnp.max(cb_scr[:, :SB_HD]) < SB_DEAD))

        @pl.when((pl.program_id(0) == n_h - 1) & (i == n_i - 1))
        def _():
            _push_wait(own, pairs)

    return pl.pallas_call(
        body, name="sb_fwd",
        grid=(n_h, n_i),
        in_specs=[pl.BlockSpec((None, B, W), lambda h, i: (1, i, h)),
                  pl.BlockSpec((None, T, W), lambda h, i: (2, 0, h)),
                  pl.BlockSpec((None, T, W), lambda h, i: (3, 0, h)),
                  _ANY],
        out_specs=(pl.BlockSpec((B, W), lambda h, i: (i, h)), _ANY),
        out_shape=(jax.ShapeDtypeStruct((T, 1024), F32),
                   jax.ShapeDtypeStruct((N_DEV,) + wp_shard.shape, wp_shard.dtype)),
        scratch_shapes=[pltpu.VMEM((HG * B, KB), F32)] + _PUSH_SEMS,
        compiler_params=_cparams(("arbitrary", "arbitrary")),
    )(projb, projb, projb, wp_shard)


def _sb_bwd_call(projb, do_sb, g_p):
    T = projb.shape[1]
    B = min(SB_QUERIES, T)
    nb = T // B
    HG = SB_HG_BWD
    W = HG * SB_HD
    WQ = HG * B
    KB = min(SB_KEYS, T)
    nkb = T // KB
    n_h = SB_HEADS // HG
    scale = 1.0 / math.sqrt(SB_HD)

    def body(q_ref, k_ref, v_ref, do_ref, gp_ref, dq_ref, dk_ref, dv_ref, rp_ref,
             dk_scr, dv_scr, kt_scr, beta_scr, g_scr, dqt_scr, send_sems, recv_sems, loc_sem):
        i = pl.program_id(1)
        own, pairs = _push_copies(gp_ref, rp_ref, send_sems, recv_sems, loc_sem, scatter=True)

        @pl.when((pl.program_id(0) == 0) & (i == 0))
        def _():
            _push_start(own, pairs)

        @pl.when(i == 0)
        def _():
            dk_scr[...] = jnp.zeros_like(dk_scr)
            dv_scr[...] = jnp.zeros_like(dv_scr)
            for hh in range(HG):
                for jb in range(nkb):
                    kt_scr[hh, jb] = _bf(
                        k_ref[jb * KB:(jb + 1) * KB, hh * SB_HD:(hh + 1) * SB_HD].astype(F32).T)

        dqt_scr[...] = jnp.zeros_like(dqt_scr)
        later = _bf((_iota2(KB, KB, 1) > _iota2(KB, KB, 0)).astype(F32))
        earlier = _bf((_iota2(KB, KB, 1) < _iota2(KB, KB, 0)).astype(F32))
        dob = _bf(do_ref[...])
        jp0 = (i * B) // KB

        def strict_mask():
            return (jp0 * KB + _iota2(KB, WQ, 0)) < (i * B + (_iota2(KB, WQ, 1) & (B - 1)))

        def heads(fn):
            return [fn(slice(hh * SB_HD, (hh + 1) * SB_HD)) for hh in range(HG)]

        def pass1(jp, cb, masked):
            off = pl.multiple_of(jp * KB, KB)
            z = jnp.concatenate(heads(lambda cs: _dot_nt(k_ref[pl.ds(off, KB), cs], q_ref[:, cs])), axis=1) * scale
            da = jnp.concatenate(heads(lambda cs: _dot_nt(v_ref[pl.ds(off, KB), cs], dob[:, cs])), axis=1)
            lsz, l1m = _sb_logs(z)
            if masked:
                strict = strict_mask()
                l1m = jnp.where(strict, l1m, 0.0)
            a = jnp.exp(lsz + cb + _tri2_left(later, l1m))
            if masked:
                a = jnp.where(strict, a, 0.0)
            g_scr[jp] = a * da
            beta_scr[jp] = jnp.exp(lsz)
            ab = _bf(a)
            for hh in range(HG):
                cs = slice(hh * SB_HD, (hh + 1) * SB_HD)
                dv_scr[pl.ds(off, KB), cs] += _dot(ab[:, hh * B:(hh + 1) * B], dob[:, cs])
            return cb + jnp.sum(l1m, axis=0, keepdims=True)

        zero = jnp.zeros((1, WQ), F32)
        cb = pass1(jp0, zero, True)

        def live(state):
            jj, _, dead = state
            return (jj <= jp0) & jnp.logical_not(dead)

        def step(state):
            jj, cr, _ = state
            cr = pass1(jp0 - jj, cr, False)
            return jj + 1, cr, jnp.max(cr) < SB_DEAD

        n_done, _, _ = lax.while_loop(live, step, (jnp.int32(1), cb, jnp.max(cb) < SB_DEAD))
        jp_first = jp0 - (n_done - 1)

        def pass2(jp, cg, masked):
            off = pl.multiple_of(jp * KB, KB)
            g = g_scr[jp]
            beta = beta_scr[jp]
            dz = g * (1.0 - beta) - beta * (cg + _tri2_left(earlier, g))
            if masked:
                dz = jnp.where(strict_mask(), dz, 0.0)
            dzb = _bf(dz * scale)
            for hh in range(HG):
                cs = slice(hh * SB_HD, (hh + 1) * SB_HD)
                dk_scr[pl.ds(off, KB), cs] += _dot(dzb[:, hh * B:(hh + 1) * B], q_ref[:, cs])
                dqt_scr[hh] += _dot(kt_scr[hh, jp], dzb[:, hh * B:(hh + 1) * B])
            return cg + jnp.sum(g, axis=0, keepdims=True)

        cg = lax.fori_loop(jp_first, jp0, lambda jp, cr: pass2(jp, cr, False), zero)
        pass2(jp0, cg, True)
        for hh in range(HG):
            dq_ref[:, hh * SB_HD:(hh + 1) * SB_HD] = _bf(dqt_scr[hh].T)

        @pl.when(i == nb - 1)
        def _():
            dk_ref[...] = _bf(dk_scr[...])
            dv_ref[...] = _bf(dv_scr[...])

        @pl.when((pl.program_id(0) == n_h - 1) & (i == nb - 1))
        def _():
            _push_wait(own, pairs)

    return pl.pallas_call(
        body, name="sb_bwd",
        grid=(n_h, nb),
        in_specs=[pl.BlockSpec((None, B, W), lambda h, i: (1, i, h)),
                  pl.BlockSpec((None, T, W), lambda h, i: (2, 0, h)),
                  pl.BlockSpec((None, T, W), lambda h, i: (3, 0, h)),
                  pl.BlockSpec((B, W), lambda h, i: (i, h)),
                  _ANY],
        out_specs=(pl.BlockSpec((B, W), lambda h, i: (i, h)),
                   pl.BlockSpec((T, W), lambda h, i: (0, h)),
                   pl.BlockSpec((T, W), lambda h, i: (0, h)),
                   _ANY),
        out_shape=(jax.ShapeDtypeStruct((T, 1024), BF16),
                   jax.ShapeDtypeStruct((T, 1024), BF16),
                   jax.ShapeDtypeStruct((T, 1024), BF16),
                   jax.ShapeDtypeStruct(g_p.shape, g_p.dtype)),
        scratch_shapes=[pltpu.VMEM((T, W), F32), pltpu.VMEM((T, W), F32),
                        pltpu.VMEM((HG, nkb, SB_HD, KB), BF16),
                        pltpu.VMEM((nkb, KB, WQ), F32), pltpu.VMEM((nkb, KB, WQ), F32),
                        pltpu.VMEM((HG, SB_HD, B), F32)] + _PUSH_SEMS,
        compiler_params=_cparams(("arbitrary", "arbitrary")),
    )(projb, projb, projb, do_sb, g_p)


def _mid_call(o_gla, o_sb, projf, x, target, wpa, wpb, wo, gla_g, b_gate, final_g):
    T, D = x.shape
    tm = min(TBLK, T)

    def body(og_ref, ggate_ref, osb_ref, sgate_ref, ma_ref, mb_ref, x_ref, tgt_ref,
             wpa_ref, wpb_ref, wo_ref, glag_ref, bg_ref, fg_ref,
             dx2_ref, dogla_ref, dosb_ref, dggate_ref, dsgate_ref, dm_ref,
             mt_ref, ogt_ref, obt_ref, dx2b_ref, dya_ref, dyb_ref,
             dfg_ref, dbg_ref, dglag_ref, loss_ref):
        @pl.when(pl.program_id(0) == 0)
        def _():
            dfg_ref[...] = jnp.zeros_like(dfg_ref)
            dbg_ref[...] = jnp.zeros_like(dbg_ref)
            dglag_ref[...] = jnp.zeros_like(dglag_ref)
            loss_ref[...] = jnp.zeros_like(loss_ref)

        glag = glag_ref[...]
        ggate = ggate_ref[...]
        sg = _sigmoid(ggate)
        silu_g = ggate * sg
        ohat, rinv, nrm = [], [], []
        for hh in range(GLA_HEADS):
            oh = og_ref[:, hh * GLA_HV:(hh + 1) * GLA_HV]
            r = lax.rsqrt(jnp.mean(oh * oh, axis=-1, keepdims=True) + EPS)
            ohat.append(oh * r)
            rinv.append(r)
            nrm.append(ohat[-1] * glag)
        n_all = jnp.concatenate(nrm, axis=1)
        og = n_all * silu_g
        ogb = _bf(og)
        ya = _dot(ogb, wpa_ref[...])
        sgate = sgate_ref[...]
        ss = _sigmoid(sgate)
        silu_s = sgate * ss
        osb = osb_ref[...]
        ob = osb * silu_s
        obb = _bf(ob)
        yb = _dot(obb, wpb_ref[...])
        ga = _sigmoid(ma_ref[...] + bg_ref[:, :D])
        gb = _sigmoid(mb_ref[...] + bg_ref[:, D:])
        merged = ga * ya + gb * yb
        mgb = _bf(merged)
        x2 = x_ref[...] + _dot(mgb, wo_ref[...])
        r2 = lax.rsqrt(jnp.mean(x2 * x2, axis=-1, keepdims=True) + EPS)
        xh2 = x2 * r2
        fg = fg_ref[...]
        err = xh2 * fg - tgt_ref[...]
        loss_ref[...] += jnp.broadcast_to(
            0.5 * jnp.sum(jnp.mean(err * err, axis=-1, keepdims=True), axis=0, keepdims=True), (1, 128))
        dy = err * (1.0 / D)
        dfg_ref[...] += jnp.sum(dy * xh2, axis=0, keepdims=True)
        dxh = dy * fg
        dx2 = r2 * (dxh - xh2 * jnp.mean(dxh * xh2, axis=-1, keepdims=True))
        dx2_ref[...] = dx2
        dx2b = _bf(dx2)
        dx2b_ref[...] = dx2b
        dmerged = _dot_nt(dx2b, wo_ref[...])
        dya = dmerged * ga
        dyb = dmerged * gb
        dma = dmerged * ya * ga * (1.0 - ga)
        dmb = dmerged * yb * gb * (1.0 - gb)
        dm_ref[:, :D] = _bf(dma)
        dm_ref[:, D:] = _bf(dmb)
        dbg_ref[:, :D] += jnp.sum(dma, axis=0, keepdims=True)
        dbg_ref[:, D:] += jnp.sum(dmb, axis=0, keepdims=True)
        dyab = _bf(dya)
        dybb = _bf(dyb)
        dya_ref[...] = dyab
        dyb_ref[...] = dybb
        dog = _dot_nt(dyab, wpa_ref[...])
        dob = _dot_nt(dybb, wpb_ref[...])
        dosb_ref[...] = dob * silu_s
        dsgate_ref[...] = _bf(dob * osb * (ss * (1.0 + sgate * (1.0 - ss))))
        dn = dog * silu_g
        dggate_ref[...] = _bf(dog * n_all * (sg * (1.0 + ggate * (1.0 - sg))))
        dglag = jnp.zeros((1, GLA_HV), F32)
        for hh in range(GLA_HEADS):
            dnh = dn[:, hh * GLA_HV:(hh + 1) * GLA_HV]
            dglag = dglag + jnp.sum(dnh * ohat[hh], axis=0, keepdims=True)
            dohat = dnh * glag
            dogla_ref[:, hh * GLA_HV:(hh + 1) * GLA_HV] = rinv[hh] * (
                dohat - ohat[hh] * jnp.mean(dohat * ohat[hh], axis=-1, keepdims=True))
        dglag_ref[...] += dglag
        mt_ref[...] = _bf(merged.T)
        ogt_ref[...] = _bf(og.T)
        obt_ref[...] = _bf(ob.T)

    row = lambda i: (i, 0)
    const = lambda i: (0, 0)
    tile = pl.BlockSpec((tm, D), row)
    tile_t = pl.BlockSpec((None, D, tm), lambda i: (i, 0, 0))
    wspec = pl.BlockSpec((D, D), const)
    return pl.pallas_call(
        body, name="mid",
        grid=(T // tm,),
        in_specs=[tile,
                  pl.BlockSpec((None, tm, D), lambda i: (1, i, 0)),
                  tile,
                  pl.BlockSpec((None, tm, D), lambda i: (2, i, 0)),
                  pl.BlockSpec((None, tm, D), lambda i: (3, i, 0)),
                  pl.BlockSpec((None, tm, D), lambda i: (4, i, 0)),
                  tile, tile, wspec, wspec, wspec,
                  pl.BlockSpec((1, GLA_HV), const),
                  pl.BlockSpec((1, 2 * D), const),
                  pl.BlockSpec((1, D), const)],
        out_specs=(tile, tile, tile, tile, tile,
                   pl.BlockSpec((tm, 2 * D), row),
                   tile_t, tile_t, tile_t, tile, tile, tile,
                   pl.BlockSpec((1, D), const),
                   pl.BlockSpec((1, 2 * D), const),
                   pl.BlockSpec((1, GLA_HV), const),
                   pl.BlockSpec((1, 128), const)),
        out_shape=(jax.ShapeDtypeStruct((T, D), F32),
                   jax.ShapeDtypeStruct((T, D), F32),
                   jax.ShapeDtypeStruct((T, D), F32),
                   jax.ShapeDtypeStruct((T, D), BF16),
                   jax.ShapeDtypeStruct((T, D), BF16),
                   jax.ShapeDtypeStruct((T, 2 * D), BF16),
                   jax.ShapeDtypeStruct((T // tm, D, tm), BF16),
                   jax.ShapeDtypeStruct((T // tm, D, tm), BF16),
                   jax.ShapeDtypeStruct((T // tm, D, tm), BF16),
                   jax.ShapeDtypeStruct((T, D), BF16),
                   jax.ShapeDtypeStruct((T, D), BF16),
                   jax.ShapeDtypeStruct((T, D), BF16),
                   jax.ShapeDtypeStruct((1, D), F32),
                   jax.ShapeDtypeStruct((1, 2 * D), F32),
                   jax.ShapeDtypeStruct((1, GLA_HV), F32),
                   jax.ShapeDtypeStruct((1, 128), F32)),
        compiler_params=_cparams(("arbitrary",)),
    )(o_gla, projf, o_sb, projf, projf, projf, x, target, wpa, wpb, wo, gla_g, b_gate, final_g)


def _dh_call(pieces, dmlog, drank, wt, wr, x, dx2, norm_g, s_in, small):
    T, D = x.shape
    tm = min(256, T)
    npc = len(pieces)
    n_main = N_GROUPS * 1024
    n_i = T // tm
    i_forward = 5 * n_i // 8

    def body(*refs):
        pcs = refs[:npc]
        (dm_ref, dr_ref, w_hbm, wr_ref, x_ref, dx2_ref, g_ref, sin_ref, small_ref,
         gx_ref, rin_ref, relayed_ref, rsmall_ref,
         w_scr, sems, dg_ref, small_mine, small_send, small_recv, small_loc, *exchange_scratch) = refs[npc:]
        start, forward, finish = _chip_reduce_steps(sin_ref, rin_ref, relayed_ref, *exchange_scratch)

        @pl.when(pl.program_id(0) == 0)
        def _():
            start()
            lo = pltpu.make_async_copy(w_hbm.at[pl.ds(0, RANK_COL)], w_scr.at[pl.ds(0, RANK_COL)], sems.at[0])
            hi = pltpu.make_async_copy(w_hbm.at[pl.ds(RANK_COL + GLA_RANK, n_main - RANK_COL)],
                                       w_scr.at[pl.ds(RANK_COL, n_main - RANK_COL)], sems.at[1])
            lo.start()
            hi.start()
            dg_ref[...] = jnp.zeros_like(dg_ref)
            lo.wait()
            hi.wait()

        @pl.when(pl.program_id(0) == i_forward)
        def _():
            forward()

        def w_group(g):
            return w_scr[g * 1024:(g + 1) * 1024, :]

        dr = dr_ref[...]
        dh = _dot(dr, wr_ref[...])
        for g in range(npc):
            dh = dh + _dot(pcs[g][...], w_group(g))
        dh = dh + _dot(dm_ref[:, :D], w_group(npc))
        dh = dh + _dot(dm_ref[:, D:], w_group(npc + 1))
        xv = x_ref[...]
        r = lax.rsqrt(jnp.mean(xv * xv, axis=-1, keepdims=True) + EPS)
        xhat = xv * r
        g = g_ref[...]
        dg_ref[...] += jnp.sum(dh * xhat, axis=0, keepdims=True)
        dxhat = dh * g
        gx_ref[...] = r * (dxhat - xhat * jnp.mean(dxhat * xhat, axis=-1, keepdims=True)) + dx2_ref[...]

        @pl.when(pl.program_id(0) == n_i - 1)
        def _():
            small_mine[...] = small_ref[...]
            small_mine[:, _SM_NORM:_SM_NORM + D] = dg_ref[...]
            own, pairs = _push_copies(small_mine, rsmall_ref, small_send, small_recv, small_loc, scatter=False)
            _push_start(own, pairs)
            finish()
            _push_wait(own, pairs)

    row = lambda i: (i, 0)
    const = lambda i: (0, 0)
    tile = pl.BlockSpec((tm, D), row)
    part = s_in.shape[1:]
    return pl.pallas_call(
        body, name="dh",
        grid=(n_i,),
        in_specs=[tile] * npc + [
            pl.BlockSpec((tm, 2 * D), row),
            pl.BlockSpec((tm, 128), row),
            _ANY,
            pl.BlockSpec((128, D), const),
            tile, tile,
            pl.BlockSpec((1, D), const),
            _ANY,
            pl.BlockSpec(small.shape, const)],
        out_specs=(tile, _ANY, _ANY, _ANY),
        out_shape=(jax.ShapeDtypeStruct((T, D), F32),
                   jax.ShapeDtypeStruct((3,) + part, s_in.dtype),
                   jax.ShapeDtypeStruct(part, s_in.dtype),
                   jax.ShapeDtypeStruct((N_DEV,) + small.shape, small.dtype)),
        scratch_shapes=[pltpu.VMEM((n_main, D), BF16), pltpu.SemaphoreType.DMA((2,)),
                        pltpu.VMEM((1, D), F32), pltpu.VMEM(small.shape, small.dtype)]
        + _PUSH_SEMS + _chip_reduce_scratch(*part, s_in.dtype),
        compiler_params=_cparams(("arbitrary",)),
    )(*pieces, dmlog, drank, wt, wr, x, dx2, norm_g, s_in, small)


def _wgrad_rank_call(ht, drank):
    n_tb, D, tb = ht.shape

    def body(ht_ref, dr_ref, o_ref):
        @pl.when(pl.program_id(0) == 0)
        def _():
            o_ref[...] = jnp.zeros_like(o_ref)

        o_ref[...] += _dot(ht_ref[...], dr_ref[...])

    return pl.pallas_call(
        body, name="wgrad_rank",
        grid=(n_tb,),
        in_specs=[pl.BlockSpec((None, D, tb), lambda i: (i, 0, 0)),
                  pl.BlockSpec((tb, 128), lambda i: (i, 0))],
        out_specs=pl.BlockSpec((D, 128), lambda i: (0, 0)),
        out_shape=jax.ShapeDtypeStruct((D, 128), F32),
        compiler_params=_cparams(("arbitrary",)),
    )(ht, drank)


def _wgrad_call(lhs_list, lhs_of_group, rhs_list, rhs_of_group, n_transposed, name):
    n_groups = len(rhs_of_group)
    n_tb, D, tb = lhs_list[0].shape
    T = n_tb * tb
    per = min(4, n_tb)
    tk = per * tb
    nk = T // tk
    nl = len(lhs_list)

    def body(*refs):
        lhs = refs[:nl]
        rhs = refs[nl:nl + n_groups]
        out_ref, acc = refs[nl + n_groups:]
        g = pl.program_id(0)
        i = pl.program_id(1)

        @pl.when(i == 0)
        def _():
            acc[...] = jnp.zeros_like(acc)

        for p in range(n_groups):
            @pl.when(g == p)
            def _(p=p):
                lref = lhs[lhs_of_group[p]]
                part = _dot(lref[0], rhs[p][0:tb, :])
                for b in range(1, per):
                    part = part + _dot(lref[b], rhs[p][b * tb:(b + 1) * tb, :])
                acc[...] += part

        @pl.when((i == nk - 1) & (g < n_transposed))
        def _():
            out_ref[...] = _bf(acc[...].T)

        @pl.when((i == nk - 1) & (g >= n_transposed))
        def _():
            out_ref[...] = _bf(acc[...])

    def lhs_spec(a):
        groups = [g for g in range(n_groups) if lhs_of_group[g] == a]
        lo, hi = min(groups), max(groups)
        assert groups == list(range(lo, hi + 1))
        return pl.BlockSpec((per, D, tb), lambda g, i: (jnp.where((g >= lo) & (g <= hi), i, 0), 0, 0))

    def rhs_spec(p):
        cb = rhs_of_group[p][1]
        return pl.BlockSpec((tk, 1024), lambda g, i: (jnp.where(g == p, i, 0), cb))

    return pl.pallas_call(
        body, name=name,
        grid=(n_groups, nk),
        in_specs=[lhs_spec(a) for a in range(nl)] + [rhs_spec(p) for p in range(n_groups)],
        out_specs=pl.BlockSpec((None, D, 1024), lambda g, i: (g, 0, 0)),
        out_shape=jax.ShapeDtypeStruct((n_groups, D, 1024), BF16),
        scratch_shapes=[pltpu.VMEM((D, 1024), F32)],
        compiler_params=_cparams(("arbitrary", "arbitrary")),
    )(*lhs_list, *[rhs_list[rhs_of_group[p][0]] for p in range(n_groups)])


def _adamw_math(parts, w, m, v):
    g = parts[0].astype(F32)
    for p in parts[1:]:
        g = g + p.astype(F32)
    mm = ADAM_B1 * m + (1.0 - ADAM_B1) * g
    vv = ADAM_B2 * v + (1.0 - ADAM_B2) * (g * g)
    m_hat = mm / (1.0 - ADAM_B1 ** ADAM_STEP)
    v_hat = vv / (1.0 - ADAM_B2 ** ADAM_STEP)
    return g, -ADAM_LR * (m_hat / (jnp.sqrt(v_hat) + ADAM_EPS) + ADAM_WD * w), mm, vv


def _part_order(n_parts):
    return [n_parts - 1] + list(range(n_parts - 1))


def _adamw_call(parts, w, m, v, name):
    R, C = w.shape
    n_parts = parts.shape[0]
    (tr, tc), grid, idx = _tiling_2d(R, C, 512)

    def body(p_ref, w_ref, m_ref, v_ref, g_ref, d_ref, nm_ref, nv_ref):
        g_ref[...], d_ref[...], nm_ref[...], nv_ref[...] = _adamw_math(
            [p_ref[k] for k in _part_order(n_parts)], w_ref[...], m_ref[...], v_ref[...])

    blk = pl.BlockSpec((tr, tc), idx)
    sds = jax.ShapeDtypeStruct((R, C), F32)
    return pl.pallas_call(
        body, name=name,
        grid=grid,
        in_specs=[pl.BlockSpec((n_parts, tr, tc), lambda i: (0,) + idx(i)), blk, blk, blk],
        out_specs=(blk, blk, blk, blk),
        out_shape=(sds, sds, sds, sds),
        compiler_params=_cparams(("arbitrary",)),
    )(parts, w, m, v)


def _adamw_rows_call(parts, ws, ms, vs, name):
    n = len(ws)
    R, C = ws[0].shape
    n_parts = parts.shape[0]

    def body(*refs):
        p_ref = refs[0]
        w_refs, m_refs, v_refs = refs[1:1 + n], refs[1 + n:1 + 2 * n], refs[1 + 2 * n:1 + 3 * n]
        outs = refs[1 + 3 * n:]
        for k in range(n):
            @pl.when(pl.program_id(0) == k)
            def _(k=k):
                res = _adamw_math([p_ref[j] for j in _part_order(n_parts)],
                                  w_refs[k][...], m_refs[k][...], v_refs[k][...])
                for o_ref, val in zip(outs[4 * k:4 * k + 4], res):
                    o_ref[...] = val

    whole = pl.BlockSpec((R, C), lambda k: (0, 0))
    sds = jax.ShapeDtypeStruct((R, C), F32)
    res = pl.pallas_call(
        body, name=name,
        grid=(n,),
        in_specs=[pl.BlockSpec((n_parts, R, C), lambda k: (0, k, 0))] + [whole] * (3 * n),
        out_specs=tuple([whole] * (4 * n)),
        out_shape=tuple([sds] * (4 * n)),
        compiler_params=_cparams(("arbitrary",)),
    )(parts, *ws, *ms, *vs)
    return [res[4 * k:4 * k + 4] for k in range(n)]


def _adamw_lanes_call(parts, offsets, ws, ms, vs, name):
    n = len(ws)
    n_parts = parts.shape[0]

    def body(*refs):
        p_ref = refs[0]
        w_refs, m_refs, v_refs = refs[1:1 + n], refs[1 + n:1 + 2 * n], refs[1 + 2 * n:1 + 3 * n]
        outs = refs[1 + 3 * n:]
        for k in range(n):
            lanes = slice(offsets[k], offsets[k] + ws[k].shape[1])
            res = _adamw_math([p_ref[j, :, lanes] for j in _part_order(n_parts)],
                              w_refs[k][...], m_refs[k][...], v_refs[k][...])
            for o_ref, val in zip(outs[4 * k:4 * k + 4], res):
                o_ref[...] = val

    res = pl.pallas_call(
        body, name=name,
        out_shape=tuple(jax.ShapeDtypeStruct(ws[k].shape, F32) for k in range(n) for _ in range(4)),
        compiler_params=_cparams(),
    )(parts, *ws, *ms, *vs)
    return [res[4 * k:4 * k + 4] for k in range(n)]


def _local_step(x, target, wt, wr, wdec, bdec, wp_shard, norm_g, gla_g, b_gate, final_g):
    D = x.shape[1]
    half = wp_shard.shape[1] // 2
    projf, projb, rank, ht, wp_lo = _proj_call(x, norm_g, wt, wr, wp_shard[:, :half])
    o_gla, st_all, la = _gla_fwd_call(projf, projb, rank, wdec, bdec)
    o_sb, wp_hi = _sb_fwd_call(projb, wp_shard[:, half:])
    wp_full = jnp.concatenate([wp_lo, wp_hi], axis=2).transpose(1, 0, 2, 3).reshape(3, D, D)
    (dx2, do_gla, do_sb, dggate, dsgate, dmlog, mt, ogt, obt, dx2b, dya, dyb,
     dfinal_g, db_gate, dgla_g, loss) = _mid_call(o_gla, o_sb, projf, x, target, wp_full[0], wp_full[1],
                                                 wp_full[2], gla_g, b_gate, final_g)
    dw_p = _wgrad_call([ogt, obt, mt], [0, 1, 2], [dya, dyb, dx2b], [(0, 0), (1, 0), (2, 0)], 0, "wgrad_p")
    g_p = dw_p.reshape(3, N_DEV, D // N_DEV, D).transpose(1, 0, 2, 3).reshape(N_DEV, 3 * (D // N_DEV), D)
    dqk, dgv, drank, dwdec, dbdec = _gla_bwd_call(projf, projb, la, do_gla, st_all, rank, wdec)
    dsq, dsk, dsv, r_p = _sb_bwd_call(projb, do_sb, g_p)
    pieces = [dqk, dgv, dggate, dsq, dsk, dsv, dsgate]
    rhs_of_group = [(g, 0) for g in range(7)] + [(7, 0), (7, 1)]
    dw_in = _wgrad_call([ht], [0] * N_GROUPS, pieces + [dmlog], rhs_of_group, N_GROUPS, "wgrad_in")
    dwr = _wgrad_rank_call(ht, drank)
    s_in = _pair_sum_call(dw_in.reshape(N_GROUPS * 1024, D), dwr[:, :GLA_RANK].T.astype(BF16))
    small = jnp.concatenate([
        jnp.zeros((D,), F32), dbdec.reshape(-1), dgla_g.reshape(-1), db_gate.reshape(-1), dfinal_g.reshape(-1),
        loss.reshape(-1), dwdec[:GLA_RANK].reshape(-1)]).reshape(1, _SM_LEN)
    grad_x, r_in, _, r_small = _dh_call(pieces, dmlog, drank, wt, wr, x, dx2, norm_g, s_in, small)
    return grad_x, r_in, r_p, r_small


_SM_NORM = 0
_SM_BDEC = _SM_NORM + D_MODEL
_SM_GLAG = _SM_BDEC + GLA_DK
_SM_BGATE = _SM_GLAG + GLA_HV
_SM_FINAL = _SM_BGATE + 2 * D_MODEL
_SM_REPL = _SM_FINAL + D_MODEL
_SM_LOSS = _SM_REPL
_SM_WDEC = _SM_LOSS + 128
_SM_LEN = _SM_WDEC + GLA_RANK * GLA_DK


def kernel(x, norm_g, w_in, w_dec_up, b_dec, gla_norm_g, w_pa, w_pb, b_gate, w_o, final_g, loss_target, m_norm_g, m_w_in, m_w_dec_up, m_b_dec, m_gla_norm_g, m_w_pa, m_w_pb, m_b_gate, m_w_o, m_final_g, v_norm_g, v_w_in, v_w_dec_up, v_b_dec, v_gla_norm_g, v_w_pa, v_w_pb, v_b_gate, v_w_o, v_final_g):
    D = D_MODEL
    me = 4 * lax.axis_index("x") + 2 * lax.axis_index("y") + lax.axis_index("c")

    wp_shard = jnp.stack([w_pa, w_pb, w_o]).astype(BF16)
    n_first = _half_rows(SHARD_COLS)
    win_all, wdec_all = _all_gather([w_in.T.astype(BF16), w_dec_up], "gather_w",
                                    row_pieces=[[(0, n_first), (n_first, SHARD_COLS - n_first)], None])
    wt = _flatten_blocks_call(win_all)
    wr = jnp.pad(wt[RANK_COL:RANK_COL + GLA_RANK], ((0, 128 - GLA_RANK), (0, 0)))
    wdec_full = wdec_all.transpose(1, 0, 2).reshape(GLA_RANK, GLA_DK)
    wdec = jnp.pad(wdec_full, ((0, 128 - GLA_RANK), (0, 0)))

    grad_x, r_in, r_p, r_small = _local_step(
        x[0], loss_target[0], wt, wr, wdec, b_dec.reshape(1, -1), wp_shard,
        norm_g.reshape(1, -1), gla_norm_g.reshape(1, -1), b_gate.reshape(1, -1), final_g.reshape(1, -1))

    gw_in, d_in, nm_in, nv_in = (a.T for a in _adamw_call(r_in, w_in.T, m_w_in.T, v_w_in.T, "adamw_in"))
    (g_pa, d_pa, nm_pa, nv_pa), (g_pb, d_pb, nm_pb, nv_pb), (g_o, d_o, nm_o, nv_o) = _adamw_rows_call(
        r_p, [w_pa, w_pb, w_o], [m_w_pa, m_w_pb, m_w_o], [v_w_pa, v_w_pb, v_w_o], "adamw_p")

    def row(a):
        return a.reshape(1, -1)

    rep = _adamw_lanes_call(
        r_small, [_SM_NORM, _SM_BDEC, _SM_GLAG, _SM_BGATE, _SM_FINAL],
        [row(a) for a in (norm_g, b_dec, gla_norm_g, b_gate, final_g)],
        [row(a) for a in (m_norm_g, m_b_dec, m_gla_norm_g, m_b_gate, m_final_g)],
        [row(a) for a in (v_norm_g, v_b_dec, v_gla_norm_g, v_b_gate, v_final_g)], "adamw_rep")
    ((g_norm, d_norm, nm_norm, nv_norm), (g_bdec, d_bdec, nm_bdec, nv_bdec), (g_glag, d_glag, nm_glag, nv_glag),
     (g_bgate, d_bgate, nm_bgate, nv_bgate), (g_final, d_final, nm_final, nv_final)) = [
        tuple(a.reshape(-1) for a in quad) for quad in rep]

    wdec_parts = r_small[:, 0, _SM_WDEC:].reshape(N_DEV, GLA_RANK, GLA_DK)
    cols = GLA_DK // N_DEV
    wdec_mine = lax.dynamic_slice_in_dim(wdec_parts, me * cols, cols, axis=2)
    g_wdec, d_wdec, nm_wdec, nv_wdec = _adamw_call(wdec_mine, w_dec_up, m_w_dec_up, v_w_dec_up, "adamw_dec")

    loss_total = jnp.sum(r_small[:, 0, _SM_LOSS])

    return (loss_total, grad_x[None],
            g_norm, gw_in, g_wdec, g_bdec, g_glag, g_pa, g_pb, g_bgate, g_o, g_final,
            d_norm, d_in, d_wdec, d_bdec, d_glag, d_pa, d_pb, d_bgate, d_o, d_final,
            nm_norm, nm_in, nm_wdec, nm_bdec, nm_glag, nm_pa, nm_pb, nm_bgate, nm_o, nm_final,
            nv_norm, nv_in, nv_wdec, nv_bdec, nv_glag, nv_pa, nv_pb, nv_bgate, nv_o, nv_final)
```

```python
import math

import jax
import jax.numpy as jnp
from jax import lax
from jax.experimental import pallas as pl
from jax.experimental.pallas import tpu as pltpu

F32 = jnp.float32
BF16 = jnp.bfloat16

N_DEV = 8
D_MODEL = 1024
GLA_HEADS = 4
GLA_HK = 128
GLA_HV = 256
GLA_DK = 512
GLA_RANK = 16
GLA_TAU = 16.0
GLA_CHUNK = 64
SB_HEADS = 8
SB_HD = 128
EPS = 1e-6
N_GROUPS = 9
RANK_COL = 3072
IN_COLS = 9232
SHARD_COLS = IN_COLS // N_DEV

ADAM_LR = 0.001
ADAM_B1 = 0.9
ADAM_B2 = 0.999
ADAM_EPS = 1e-08
ADAM_WD = 0.01
ADAM_STEP = 10

VMEM_LIMIT = 56 * 1024 * 1024
TBLK = 256


def _cparams(sem=None):
    return pltpu.CompilerParams(dimension_semantics=sem, vmem_limit_bytes=VMEM_LIMIT)


def _tiling_2d(rows, cols, band_cols):
    if rows * cols <= 128 * 1024:
        return (rows, cols), (1,), lambda i: (0, 0)
    if rows % 128 == 0:
        return (128, cols), (rows // 128,), lambda i: (i, 0)
    tc = band_cols if cols % band_cols == 0 else cols
    return (rows, tc), (cols // tc,), lambda i: (0, i)


def _dot(a, b):
    return jnp.dot(a, b, preferred_element_type=F32)


def _dot_nt(a, b):
    return lax.dot_general(a, b, (((1,), (1,)), ((), ())), preferred_element_type=F32)


def _dot_tn(a, b):
    return lax.dot_general(a, b, (((0,), (0,)), ((), ())), preferred_element_type=F32)


def _bf(x):
    return x.astype(BF16)


def _split3(x):
    hi = x.astype(BF16)
    r = x - hi.astype(F32)
    mid = r.astype(BF16)
    lo = (r - mid.astype(F32)).astype(BF16)
    return hi, mid, lo


def _tri_left(tri, x):
    hi, mid, lo = _split3(x)
    return _dot(tri, hi) + _dot(tri, mid) + _dot(tri, lo)


def _split2(x):
    hi = lax.bitcast_convert_type(lax.bitcast_convert_type(x, jnp.uint32) & jnp.uint32(0xFFFF0000), F32)
    return hi.astype(BF16), (x - hi).astype(BF16)


def _tri2_left(tri, x):
    hi, lo = _split2(x)
    return _dot(tri, hi) + _dot(tri, lo)


def _tri2_right(x, tri):
    hi, lo = _split2(x)
    return _dot(hi, tri) + _dot(lo, tri)


def _iota2(n, m, dim):
    return lax.broadcasted_iota(jnp.int32, (n, m), dim)


def _sigmoid(x):
    return 1.0 / (1.0 + jnp.exp(-x))


def _softplus_neg_abs(z):
    return jnp.log(1.0 + jnp.exp(-jnp.abs(z)))


_ANY = pl.BlockSpec(memory_space=pl.ANY)


def _mesh_pos():
    return lax.axis_index("x"), lax.axis_index("y"), lax.axis_index("c")


def _other_chips(x, y):
    return [(1 - x, y), (x, 1 - y), (1 - x, 1 - y)]


def _rcopy(src, dst, send_sem, recv_sem, to):
    return pltpu.make_async_remote_copy(src_ref=src, dst_ref=dst, send_sem=send_sem, recv_sem=recv_sem,
                                        device_id=to, device_id_type=pl.DeviceIdType.MESH)


def _push_copies(src_ref, dst_ref, send_sems, recv_sems, loc_sem, scatter):
    x, y, c = _mesh_pos()
    me = 4 * x + 2 * y + c
    own = pltpu.make_async_copy(src_ref.at[me] if scatter else src_ref, dst_ref.at[me], loc_sem)
    pairs = []
    for k in range(1, N_DEV):
        px = 1 - x if k & 4 else x
        py = 1 - y if k & 2 else y
        pc = 1 - c if k & 1 else c
        pid = 4 * px + 2 * py + pc
        src = src_ref.at[pid] if scatter else src_ref
        send = _rcopy(src, dst_ref.at[me], send_sems.at[k - 1], recv_sems.at[k - 1], (px, py, pc))
        recv = _rcopy(src, dst_ref.at[pid], send_sems.at[k - 1], recv_sems.at[k - 1], (px, py, pc))
        pairs.append((send, recv))
    return own, pairs


def _push_start(own, pairs):
    own.start()
    for send, _ in pairs:
        send.start()


def _push_wait(own, pairs):
    for _, recv in pairs:
        recv.wait_recv()
    for send, _ in pairs:
        send.wait_send()
    own.wait()


_PUSH_SEMS = [pltpu.SemaphoreType.DMA((N_DEV - 1,)), pltpu.SemaphoreType.DMA((N_DEV - 1,)),
              pltpu.SemaphoreType.DMA]


def _half_rows(rows):
    return (rows // 2) // 16 * 16


_ADD_ROWS = 128


def _chip_reduce_steps(src_ref, dst_ref, relayed_ref, sum_x, sum_y, rel_x, rel_y, load_sems, send_sems, recv_sems,
                       loc_sem):
    _, R, C = src_ref.shape
    n0 = _half_rows(R)
    lo, hi = pl.ds(0, n0), pl.ds(n0, R - n0)
    x, y, c = _mesh_pos()
    (xx, xy), (yx, yy), (dx, dy) = _other_chips(x, y)
    to_diag, to_x, to_y = src_ref.at[2 * dx + dy], src_ref.at[2 * xx + xy], src_ref.at[2 * yx + yy]
    x_nb, y_nb = (xx, xy, c), (yx, yy, c)
    relays = (_rcopy(to_diag.at[lo], relayed_ref.at[lo], send_sems.at[0], recv_sems.at[0], x_nb),
              _rcopy(to_diag.at[hi], relayed_ref.at[hi], send_sems.at[1], recv_sems.at[1], y_nb))
    plain = (_rcopy(to_x.at[lo], dst_ref.at[0, lo], send_sems.at[2], recv_sems.at[2], x_nb),
             _rcopy(to_y.at[hi], dst_ref.at[1, hi], send_sems.at[3], recv_sems.at[3], y_nb))
    summed = (_rcopy(sum_x, dst_ref.at[0, hi], send_sems.at[4], recv_sems.at[4], x_nb),
              _rcopy(sum_y, dst_ref.at[1, lo], send_sems.at[5], recv_sems.at[5], y_nb))
    load_mine = (pltpu.make_async_copy(to_x.at[hi], sum_x, load_sems.at[0]),
                 pltpu.make_async_copy(to_y.at[lo], sum_y, load_sems.at[1]))
    load_relayed = (pltpu.make_async_copy(relayed_ref.at[hi], rel_x, load_sems.at[2]),
                    pltpu.make_async_copy(relayed_ref.at[lo], rel_y, load_sems.at[3]))
    own = pltpu.make_async_copy(src_ref.at[2 * x + y], dst_ref.at[2], loc_sem)

    def start():
        for cp in relays + plain + (own,) + load_mine:
            cp.start()

    def add(acc_ref, rel_ref):
        for r0 in range(0, acc_ref.shape[0], _ADD_ROWS):
            rows = slice(r0, min(r0 + _ADD_ROWS, acc_ref.shape[0]))
            acc_ref[rows, :] = (acc_ref[rows, :].astype(F32) + rel_ref[rows, :].astype(F32)).astype(acc_ref.dtype)

    def forward():
        for cp in relays:
            cp.wait_recv()
        for cp in load_relayed:
            cp.start()
        for cp in load_mine + load_relayed:
            cp.wait()
        add(sum_x, rel_x)
        add(sum_y, rel_y)
        for cp in summed:
            cp.start()

    def finish():
        for cp in plain + summed:
            cp.wait_recv()
        for cp in relays + plain + summed:
            cp.wait_send()
        own.wait()

    return start, forward, finish


def _chip_reduce_scratch(rows, cols, dtype):
    n0 = _half_rows(rows)
    return [pltpu.VMEM((rows - n0, cols), dtype), pltpu.VMEM((n0, cols), dtype)] * 2 + [
        pltpu.SemaphoreType.DMA((4,)), pltpu.SemaphoreType.DMA((6,)), pltpu.SemaphoreType.DMA((6,)),
        pltpu.SemaphoreType.DMA]


def _all_gather(arrs, name, row_pieces=None):
    n = len(arrs)
    pieces = [[None] if not row_pieces or not row_pieces[a] else list(row_pieces[a]) for a in range(n)]
    assert all(len(p) in (1, 2) for p in pieces)
    units = [(a, i) for a in range(n) for i in range(len(pieces[a]))]

    def body(*refs):
        ins = refs[:n]
        outs = refs[n:2 * n]
        send_sems, recv_sems, loc_sems = refs[2 * n:]
        x, y, c = _mesh_pos()
        me, sib = (x, y, c), (x, y, 1 - c)
        xn, yn, dg = [(px, py, c) for px, py in _other_chips(x, y)]

        def rows(ref, a, i):
            return ref if pieces[a][i] is None else ref.at[pl.ds(*pieces[a][i])]

        def copy(u, k, block, to, own=False):
            a, i = u
            px, py, pc = block
            dst = rows(outs[a].at[4 * px + 2 * py + pc], a, i)
            return _rcopy(rows(ins[a], a, i) if own else dst, dst, send_sems.at[a, k, i], recv_sems.at[a, k, i], to)

        started = []

        def start(cp):
            cp.start()
            started.append(cp)

        def landed_then_pass_on(u, k, block):
            copy(u, k, block, me).wait_recv()
            start(copy(u, 3 + k, block, sib))

        mine = [pltpu.make_async_copy(ins[a], outs[a].at[4 * x + 2 * y + c], loc_sems.at[a]) for a in range(n)]
        for cp in mine:
            cp.start()
        for u in units:
            start(copy(u, 0, me, sib, own=True))
        for a in range(n):
            if len(pieces[a]) == 2:
                for i, to, k in ((0, xn, 1), (1, yn, 2), (1, xn, 1), (0, yn, 2)):
                    start(copy((a, i), k, me, to, own=True))
            else:
                for to, k in ((xn, 1), (yn, 2), (dg, 3)):
                    start(copy((a, 0), k, me, to, own=True))
        for a in range(n):
            if len(pieces[a]) == 2:
                landed_then_pass_on((a, 0), 1, xn)
                start(copy((a, 0), 3, xn, yn))
                landed_then_pass_on((a, 1), 2, yn)
                start(copy((a, 1), 3, yn, xn))
                landed_then_pass_on((a, 1), 1, xn)
                landed_then_pass_on((a, 0), 2, yn)
                landed_then_pass_on((a, 0), 3, dg)
                landed_then_pass_on((a, 1), 3, dg)
            else:
                for block, k in ((xn, 1), (yn, 2), (dg, 3)):
                    landed_then_pass_on((a, 0), k, block)
        for u in units:
            copy(u, 0, sib, me).wait_recv()
            for k, (px, py, _) in ((4, xn), (5, yn), (6, dg)):
                copy(u, k, (px, py, 1 - c), me).wait_recv()
        for cp in started:
            cp.wait_send()
        for cp in mine:
            cp.wait()

    n_pc = max(len(p) for p in pieces)

    return pl.pallas_call(
        body, name=name,
        out_shape=tuple(jax.ShapeDtypeStruct((N_DEV,) + a.shape, a.dtype) for a in arrs),
        in_specs=[_ANY] * n,
        out_specs=tuple([_ANY] * n),
        scratch_shapes=[pltpu.SemaphoreType.DMA((n, 7, n_pc)), pltpu.SemaphoreType.DMA((n, 7, n_pc)),
                        pltpu.SemaphoreType.DMA((n,))],
    )(*arrs)


def _flatten_blocks_call(blocks):
    n, R, C = blocks.shape
    tc = C // 2

    def body(in_ref, out_ref):
        for p in range(n):
            out_ref[p * R:(p + 1) * R, :] = in_ref[p]

    return pl.pallas_call(
        body, name="flatten_w",
        grid=(C // tc,),
        in_specs=[pl.BlockSpec((n, R, tc), lambda i: (0, 0, i))],
        out_specs=pl.BlockSpec((n * R, tc), lambda i: (0, i)),
        out_shape=jax.ShapeDtypeStruct((n * R, C), blocks.dtype),
        compiler_params=_cparams(("arbitrary",)),
    )(blocks)


_PARTS_BANDS = 8


def _pair_sum_call(dmain, drank):
    D = dmain.shape[1]
    n = _PARTS_BANDS
    tc = D // n

    def body(dm_ref, dr_ref, sum_ref, laid, got, send_sems, recv_sems):
        x, y, c = _mesh_pos()

        def pushes(k):
            return [_rcopy(laid.at[k % 2, 2 * q + (1 - c)], got.at[k, q], send_sems.at[k, q], recv_sems.at[k, q],
                           (x, y, 1 - c)) for q in range(4)]

        def lay_out(k):
            for p in range(N_DEV):
                lo, hi = p * SHARD_COLS, (p + 1) * SHARD_COLS
                at = 0
                for src, a, b in ((dm_ref, lo, min(hi, RANK_COL)),
                                  (dr_ref, max(lo, RANK_COL) - RANK_COL, min(hi, RANK_COL + GLA_RANK) - RANK_COL),
                                  (dm_ref, max(lo, RANK_COL + GLA_RANK) - GLA_RANK, hi - GLA_RANK)):
                    if b > a:
                        laid[k % 2, p, at:at + (b - a), :] = src[a:b, :]
                        at += b - a

        for k in range(n + 1):
            @pl.when(pl.program_id(0) == k)
            def _(k=k):
                if k < n:
                    if k >= 2:
                        for cp in pushes(k - 2):
                            cp.wait_send()
                    lay_out(k)
                    for cp in pushes(k):
                        cp.start()
                if k >= 1:
                    for cp in pushes(k - 1):
                        cp.wait_recv()
                    for q in range(4):
                        sum_ref[q] = (laid[(k - 1) % 2, 2 * q + c].astype(F32)
                                      + got[k - 1, q].astype(F32)).astype(sum_ref.dtype)
                if k == n:
                    for k_open in range(max(0, n - 2), n):
                        for cp in pushes(k_open):
                            cp.wait_send()

    sems = pltpu.SemaphoreType.DMA((n, 4))
    return pl.pallas_call(
        body, name="pair_sum",
        grid=(n + 1,),
        in_specs=[pl.BlockSpec((dmain.shape[0], tc), lambda k: (0, jnp.minimum(k, n - 1))),
                  pl.BlockSpec((GLA_RANK, tc), lambda k: (0, jnp.minimum(k, n - 1)))],
        out_specs=pl.BlockSpec((4, SHARD_COLS, tc), lambda k: (0, 0, jnp.maximum(k - 1, 0))),
        out_shape=jax.ShapeDtypeStruct((4, SHARD_COLS, D), dmain.dtype),
        scratch_shapes=[pltpu.VMEM((2, N_DEV, SHARD_COLS, tc), dmain.dtype),
                        pltpu.VMEM((n, 4, SHARD_COLS, tc), dmain.dtype), sems, sems],
        compiler_params=_cparams(("arbitrary",)),
    )(dmain, drank)


def _group_row(g):
    return GLA_RANK * (g * (1024 // GLA_RANK) + (g >= RANK_COL // 1024))


def _proj_call(x, norm_g, wt, wr, wp_part):
    T, D = x.shape
    tm = min(1024, T)
    assert tm % TBLK == 0
    n_i = T // tm

    def f_slot(j):
        return ((j >= 2).astype(jnp.int32) + (j >= 6).astype(jnp.int32)
                + (j >= 7).astype(jnp.int32) + (j >= 8).astype(jnp.int32))

    def b_slot(j):
        return (j >= 3).astype(jnp.int32) + (j >= 4).astype(jnp.int32) + (j >= 5).astype(jnp.int32)

    def body(x_ref, g_ref, w_ref, wr_ref, wp_ref, pf_ref, pb_ref, rank_ref, ht_ref, wpall_ref,
             h_scr, send_sems, recv_sems, loc_sem):
        i = pl.program_id(0)
        j = pl.program_id(1)
        own, pairs = _push_copies(wp_ref, wpall_ref, send_sems, recv_sems, loc_sem, scatter=False)

        @pl.when((i == 0) & (j == 0))
        def _():
            _push_start(own, pairs)

        @pl.when(j == 0)
        def _():
            xv = x_ref[...]
            r = lax.rsqrt(jnp.mean(xv * xv, axis=-1, keepdims=True) + EPS)
            h = (xv * r) * g_ref[...]
            hb = _bf(h)
            h_scr[...] = hb
            for b in range(tm // TBLK):
                ht_ref[b] = _bf(h[b * TBLK:(b + 1) * TBLK].T)
            rank_ref[...] = _dot_nt(hb, wr_ref[...])

        is_b = (j == 1) | ((j >= 3) & (j <= 5))

        @pl.when(is_b)
        def _():
            pb_ref[...] = _bf(_dot_nt(h_scr[...], w_ref[...]))

        @pl.when(jnp.logical_not(is_b))
        def _():
            pf_ref[...] = _dot_nt(h_scr[...], w_ref[...])

        @pl.when((i == n_i - 1) & (j == N_GROUPS - 1))
        def _():
            _push_wait(own, pairs)

    return pl.pallas_call(
        body, name="proj",
        grid=(n_i, N_GROUPS),
        in_specs=[pl.BlockSpec((tm, D), lambda i, j: (i, 0)),
                  pl.BlockSpec((1, D), lambda i, j: (0, 0)),
                  pl.BlockSpec((pl.Element(1024), pl.Element(D)), lambda i, j: (_group_row(j), 0)),
                  pl.BlockSpec((128, D), lambda i, j: (0, 0)),
                  _ANY],
        out_specs=(pl.BlockSpec((None, tm, 1024), lambda i, j: (f_slot(j), i, 0)),
                   pl.BlockSpec((None, tm, 1024), lambda i, j: (b_slot(j), i, 0)),
                   pl.BlockSpec((tm, 128), lambda i, j: (i, 0)),
                   pl.BlockSpec((tm // TBLK, D, TBLK), lambda i, j: (i, 0, 0)),
                   _ANY),
        out_shape=(jax.ShapeDtypeStruct((5, T, 1024), F32),
                   jax.ShapeDtypeStruct((4, T, 1024), BF16),
                   jax.ShapeDtypeStruct((T, 128), F32),
                   jax.ShapeDtypeStruct((T // TBLK, D, TBLK), BF16),
                   jax.ShapeDtypeStruct((N_DEV,) + wp_part.shape, wp_part.dtype)),
        scratch_shapes=[pltpu.VMEM((tm, D), BF16)] + _PUSH_SEMS,
        compiler_params=_cparams(("arbitrary", "arbitrary")),
    )(x, norm_g, wt, wr, wp_part)


GLA_STEP_CHUNKS = 4


def _gla_same_chunk(rows):
    return (_iota2(rows, rows, 0) & -GLA_CHUNK) == (_iota2(rows, rows, 1) & -GLA_CHUNK)


def _gla_chunk_terms(la, q, k, n_c):
    C = GLA_CHUNK
    rows = n_c * C
    low = _gla_same_chunk(rows) & (_iota2(rows, rows, 0) >= _iota2(rows, rows, 1))
    b = _tri_left(_bf(low.astype(F32)), la)
    bl = [b[(c + 1) * C - 1:(c + 1) * C, :] for c in range(n_c)]
    bl_rows = jnp.concatenate([jnp.broadcast_to(bl[c], (C, b.shape[1])) for c in range(n_c)], axis=0)
    eb = jnp.exp(b)
    enb = jnp.exp(-b)
    ebl_b = jnp.exp(bl_rows - b)
    scale = GLA_HK ** -0.5
    qe = q * eb * scale
    ke = k * enb
    kd = k * ebl_b
    return bl, eb, enb, ebl_b, qe, ke, kd


def _gla_fwd_call(projf, projb, rank, wdec, bdec):
    T = projf.shape[1]
    C = GLA_CHUNK
    n_chunks = T // C
    n_c = GLA_STEP_CHUNKS
    R = n_c * C
    assert n_chunks % n_c == 0

    def body(qk_ref, v_ref, rank_ref, wd_ref, bd_ref, o_ref, st_ref, la_ref, st_scr):
        @pl.when(pl.program_id(0) == 0)
        def _():
            st_scr[...] = jnp.zeros_like(st_scr)

        dec = _dot(_bf(rank_ref[...]), _bf(wd_ref[...])) + bd_ref[...]
        la = (jnp.minimum(dec, 0.0) - _softplus_neg_abs(dec)) / GLA_TAU
        la_ref[...] = la
        mask = _gla_same_chunk(R) & (_iota2(R, R, 0) >= _iota2(R, R, 1))
        bl, _, _, _, qe, ke, kd = _gla_chunk_terms(la, qk_ref[:, :GLA_DK], qk_ref[:, GLA_DK:], n_c)
        qeb, keb, kdb = _bf(qe), _bf(ke), _bf(kd)
        ebl = [jnp.exp(bl[c]) for c in range(n_c)]
        heads = range(GLA_HEADS)
        ks = [slice(hh * GLA_HK, (hh + 1) * GLA_HK) for hh in heads]
        vs = [slice(hh * GLA_HV, (hh + 1) * GLA_HV) for hh in heads]
        rs = [slice(c * C, (c + 1) * C) for c in range(n_c)]
        p = [_bf(jnp.where(mask, _dot_nt(qeb[:, ks[hh]], keb[:, ks[hh]]), 0.0)) for hh in heads]
        upd = [[_dot_tn(v_ref[rs[c], vs[hh]], kdb[rs[c], ks[hh]]) for hh in heads] for c in range(n_c)]
        intra = [_dot(p[hh], v_ref[:, vs[hh]]) for hh in heads]
        st = [st_scr[hh] for hh in heads]
        for c in range(n_c):
            inter = [_dot_nt(qeb[rs[c], ks[hh]], _bf(st[hh])) for hh in heads]
            for hh in heads:
                st_ref[c, hh] = st[hh]
                o_ref[rs[c], vs[hh]] = intra[hh][rs[c]] + inter[hh]
            st = [st[hh] * ebl[c][:, ks[hh]] + upd[c][hh] for hh in heads]
        for hh in heads:
            st_scr[hh] = st[hh]

    return pl.pallas_call(
        body, name="gla_fwd",
        grid=(n_chunks // n_c,),
        in_specs=[pl.BlockSpec((None, R, 1024), lambda n: (0, n, 0)),
                  pl.BlockSpec((None, R, 1024), lambda n: (0, n, 0)),
                  pl.BlockSpec((R, 128), lambda n: (n, 0)),
                  pl.BlockSpec((128, GLA_DK), lambda n: (0, 0)),
                  pl.BlockSpec((1, GLA_DK), lambda n: (0, 0))],
        out_specs=(pl.BlockSpec((R, 1024), lambda n: (n, 0)),
                   pl.BlockSpec((n_c, GLA_HEADS, GLA_HV, GLA_HK), lambda n: (n, 0, 0, 0)),
                   pl.BlockSpec((R, GLA_DK), lambda n: (n, 0))),
        out_shape=(jax.ShapeDtypeStruct((T, 1024), F32),
                   jax.ShapeDtypeStruct((n_chunks, GLA_HEADS, GLA_HV, GLA_HK), F32),
                   jax.ShapeDtypeStruct((T, GLA_DK), F32)),
        scratch_shapes=[pltpu.VMEM((GLA_HEADS, GLA_HV, GLA_HK), F32)],
        compiler_params=_cparams(("arbitrary",)),
    )(projf, projb, rank, wdec, bdec)


def _gla_bwd_call(projf, projb, la, do_gla, st_all, rank, wdec):
    T = projf.shape[1]
    C = GLA_CHUNK
    n_chunks = T // C
    n_c = GLA_STEP_CHUNKS
    R = n_c * C
    assert n_chunks % n_c == 0
    last = n_chunks // n_c - 1

    def body(qk_ref, v_ref, la_ref, do_ref, st_ref, rank_ref, wd_ref,
             dqk_ref, dv_ref, drank_ref, dwd_ref, dbd_ref, dst_scr):
        @pl.when(pl.program_id(0) == 0)
        def _():
            dst_scr[...] = jnp.zeros_like(dst_scr)
            dwd_ref[...] = jnp.zeros_like(dwd_ref)
            dbd_ref[...] = jnp.zeros_like(dbd_ref)

        same = _gla_same_chunk(R)
        mask = same & (_iota2(R, R, 0) >= _iota2(R, R, 1))
        upp = _bf((same & (_iota2(R, R, 0) <= _iota2(R, R, 1))).astype(F32))
        scale = GLA_HK ** -0.5
        la = la_ref[...]
        bl, eb, enb, ebl_b, qe, ke, kd = _gla_chunk_terms(la, qk_ref[:, :GLA_DK], qk_ref[:, GLA_DK:], n_c)
        qeb, keb, kdb = _bf(qe), _bf(ke), _bf(kd)
        ebl = [jnp.exp(bl[c]) for c in range(n_c)]
        heads = range(GLA_HEADS)
        ks = [slice(hh * GLA_HK, (hh + 1) * GLA_HK) for hh in heads]
        vs = [slice(hh * GLA_HV, (hh + 1) * GLA_HV) for hh in heads]
        rs = [slice(c * C, (c + 1) * C) for c in range(n_c)]
        v = [v_ref[:, vs[hh]] for hh in heads]
        do = [_bf(do_ref[:, vs[hh]]) for hh in heads]
        p = [_bf(jnp.where(mask, _dot_nt(qeb[:, ks[hh]], keb[:, ks[hh]]), 0.0)) for hh in heads]
        dp = [_bf(jnp.where(mask, _dot_nt(do[hh], v[hh]), 0.0)) for hh in heads]
        dst_intra = [[_dot_tn(do[hh][rs[c]], qeb[rs[c], ks[hh]]) for hh in heads] for c in range(n_c)]
        dqe_inter = [[_dot(do[hh][rs[c]], _bf(st_ref[c, hh])) for hh in heads] for c in range(n_c)]
        dv_intra = [_dot_tn(p[hh], do[hh]) for hh in heads]
        dqe_intra = [_dot(dp[hh], keb[:, ks[hh]]) for hh in heads]
        dke = jnp.concatenate([_dot_tn(dp[hh], qeb[:, ks[hh]]) for hh in heads], axis=1)
        dstn = [dst_scr[hh] for hh in heads]
        dkd_c, dv_inter, debl = [None] * n_c, [None] * n_c, [None] * n_c
        for c in reversed(range(n_c)):
            dstnb = [_bf(dstn[hh]) for hh in heads]
            dkd_c[c] = jnp.concatenate([_dot(v[hh][rs[c]], dstnb[hh]) for hh in heads], axis=1)
            dv_inter[c] = [_dot_nt(kdb[rs[c], ks[hh]], dstnb[hh]) for hh in heads]
            debl[c] = jnp.concatenate(
                [jnp.sum(dstn[hh] * st_ref[c, hh], axis=0, keepdims=True) for hh in heads], axis=1)
            dstn = [dst_intra[c][hh] + dstn[hh] * ebl[c][:, ks[hh]] for hh in heads]
        for hh in heads:
            dst_scr[hh] = dstn[hh]
            dv_ref[:, vs[hh]] = _bf(dv_intra[hh] + jnp.concatenate([dv_inter[c][hh] for c in range(n_c)], axis=0))
        dqe = jnp.concatenate(
            [dqe_intra[hh] + jnp.concatenate([dqe_inter[c][hh] for c in range(n_c)], axis=0) for hh in heads], axis=1)
        dkd = jnp.concatenate(dkd_c, axis=0)
        dkd_kd = dkd * kd
        db = dqe * qe - dke * ke - dkd_kd
        dbl = jnp.concatenate(
            [jnp.broadcast_to(jnp.sum(dkd_kd[rs[c]], axis=0, keepdims=True) + ebl[c] * debl[c], (C, GLA_DK))
             for c in range(n_c)], axis=0)
        dla = _tri_left(upp, db) + dbl
        dqk_ref[:, :GLA_DK] = _bf(dqe * eb * scale)
        dqk_ref[:, GLA_DK:] = _bf(dke * enb + dkd * ebl_b)
        ddec = dla * (1.0 / GLA_TAU) * (1.0 - jnp.exp(GLA_TAU * la))
        ddecb = _bf(ddec)
        drank_ref[...] = _bf(_dot_nt(ddecb, _bf(wd_ref[...])))
        dwd_ref[...] += _dot_tn(_bf(rank_ref[...]), ddecb)
        dbd_ref[...] += jnp.sum(ddec, axis=0, keepdims=True)

    return pl.pallas_call(
        body, name="gla_bwd",
        grid=(n_chunks // n_c,),
        in_specs=[pl.BlockSpec((None, R, 1024), lambda n: (0, last - n, 0)),
                  pl.BlockSpec((None, R, 1024), lambda n: (0, last - n, 0)),
                  pl.BlockSpec((R, GLA_DK), lambda n: (last - n, 0)),
                  pl.BlockSpec((R, 1024), lambda n: (last - n, 0)),
                  pl.BlockSpec((n_c, GLA_HEADS, GLA_HV, GLA_HK), lambda n: (last - n, 0, 0, 0)),
                  pl.BlockSpec((R, 128), lambda n: (last - n, 0)),
                  pl.BlockSpec((128, GLA_DK), lambda n: (0, 0))],
        out_specs=(pl.BlockSpec((R, 1024), lambda n: (last - n, 0)),
                   pl.BlockSpec((R, 1024), lambda n: (last - n, 0)),
                   pl.BlockSpec((R, 128), lambda n: (last - n, 0)),
                   pl.BlockSpec((128, GLA_DK), lambda n: (0, 0)),
                   pl.BlockSpec((1, GLA_DK), lambda n: (0, 0))),
        out_shape=(jax.ShapeDtypeStruct((T, 1024), BF16),
                   jax.ShapeDtypeStruct((T, 1024), BF16),
                   jax.ShapeDtypeStruct((T, 128), BF16),
                   jax.ShapeDtypeStruct((128, GLA_DK), F32),
                   jax.ShapeDtypeStruct((1, GLA_DK), F32)),
        scratch_shapes=[pltpu.VMEM((GLA_HEADS, GLA_HV, GLA_HK), F32)],
        compiler_params=_cparams(("arbitrary",)),
    )(projf, projb, la, do_gla, st_all, rank, wdec)


def _sb_logs(z):
    lsz = jnp.minimum(z, 0.0) - _softplus_neg_abs(z)
    return lsz, lsz - z


SB_HG_FWD = 8
SB_HG_BWD = 4
SB_QUERIES = 256
SB_KEYS = 256
SB_DEAD = -105.0


def _sb_fwd_call(projb, wp_shard):
    T = projb.shape[1]
    B = min(SB_QUERIES, T)
    HG = SB_HG_FWD
    W = HG * SB_HD
    scale = 1.0 / math.sqrt(SB_HD)
    KB = min(SB_KEYS, T)
    n_h, n_i = SB_HEADS // HG, T // B

    def body(q_ref, k_ref, v_ref, wp_ref, o_ref, wpall_ref, cb_scr, send_sems, recv_sems, loc_sem):
        i = pl.program_id(1)
        own, pairs = _push_copies(wp_ref, wpall_ref, send_sems, recv_sems, loc_sem, scatter=False)

        @pl.when((pl.program_id(0) == 0) & (i == 0))
        def _():
            _push_start(own, pairs)

        rows = HG * B
        after = (_iota2(KB, KB, 0) > _iota2(KB, KB, 1)).astype(F32)
        tri = _bf(jnp.concatenate([after, jnp.ones((KB, KB), F32)], axis=1))
        o_ref[...] = jnp.zeros_like(o_ref)
        cb_scr[...] = jnp.zeros_like(cb_scr)

        def block(jp, masked):
            off = pl.multiple_of(jp * KB, KB)
            z = jnp.concatenate(
                [_dot_nt(q_ref[:, hh * SB_HD:(hh + 1) * SB_HD], k_ref[pl.ds(off, KB), hh * SB_HD:(hh + 1) * SB_HD])
                 for hh in range(HG)], axis=0) * scale
            lsz, l1m = _sb_logs(z)
            if masked:
                strict = (jp * KB + _iota2(rows, KB, 1)) < (i * B + (_iota2(rows, KB, 0) & (B - 1)))
                l1m = jnp.where(strict, l1m, 0.0)
            r = _tri2_right(l1m, tri)
            cb = cb_scr[...]
            a = jnp.exp(lsz + cb + r[:, :KB])
            if masked:
                a = jnp.where(strict, a, 0.0)
            cb_scr[...] = cb + r[:, KB:]
            ab = _bf(a)
            for hh in range(HG):
                cs = slice(hh * SB_HD, (hh + 1) * SB_HD)
                o_ref[:, cs] += _dot(ab[hh * B:(hh + 1) * B, :], v_ref[pl.ds(off, KB), cs])

        jp0 = (i * B) // KB
        block(jp0, True)

        def live(state):
            jj, dead = state
            return (jj <= jp0) & jnp.logical_not(dead)

        def step(state):
            jj, _ = state
            block(jp0 - jj, False)
            return jj + 1, jnp.max(cb_scr[:, :SB_HD]) < SB_DEAD

        lax.while_loop(live, step, (jnp.int32(1), jnp.max(cb_scr[:, :SB_HD]) < SB_DEAD))

        @pl.when((pl.program_id(0) == n_h - 1) & (i == n_i - 1))
        def _():
            _push_wait(own, pairs)

    return pl.pallas_call(
        body, name="sb_fwd",
        grid=(n_h, n_i),
        in_specs=[pl.BlockSpec((None, B, W), lambda h, i: (1, i, h)),
                  pl.BlockSpec((None, T, W), lambda h, i: (2, 0, h)),
                  pl.BlockSpec((None, T, W), lambda h, i: (3, 0, h)),
                  _ANY],
        out_specs=(pl.BlockSpec((B, W), lambda h, i: (i, h)), _ANY),
        out_shape=(jax.ShapeDtypeStruct((T, 1024), F32),
                   jax.ShapeDtypeStruct((N_DEV,) + wp_shard.shape, wp_shard.dtype)),
        scratch_shapes=[pltpu.VMEM((HG * B, KB), F32)] + _PUSH_SEMS,
        compiler_params=_cparams(("arbitrary", "arbitrary")),
    )(projb, projb, projb, wp_shard)


def _sb_bwd_call(projb, do_sb, g_p):
    T = projb.shape[1]
    B = min(SB_QUERIES, T)
    nb = T // B
    HG = SB_HG_BWD
    W = HG * SB_HD
    WQ = HG * B
    KB = min(SB_KEYS, T)
    nkb = T // KB
    n_h = SB_HEADS // HG
    scale = 1.0 / math.sqrt(SB_HD)

    def body(q_ref, k_ref, v_ref, do_ref, gp_ref, dq_ref, dk_ref, dv_ref, rp_ref,
             dk_scr, dv_scr, kt_scr, beta_scr, g_scr, dqt_scr, send_sems, recv_sems, loc_sem):
        i = pl.program_id(1)
        own, pairs = _push_copies(gp_ref, rp_ref, send_sems, recv_sems, loc_sem, scatter=True)

        @pl.when((pl.program_id(0) == 0) & (i == 0))
        def _():
            _push_start(own, pairs)

        @pl.when(i == 0)
        def _():
            dk_scr[...] = jnp.zeros_like(dk_scr)
            dv_scr[...] = jnp.zeros_like(dv_scr)
            for hh in range(HG):
                for jb in range(nkb):
                    kt_scr[hh, jb] = _bf(
                        k_ref[jb * KB:(jb + 1) * KB, hh * SB_HD:(hh + 1) * SB_HD].astype(F32).T)

        dqt_scr[...] = jnp.zeros_like(dqt_scr)
        later = _bf((_iota2(KB, KB, 1) > _iota2(KB, KB, 0)).astype(F32))
        earlier = _bf((_iota2(KB, KB, 1) < _iota2(KB, KB, 0)).astype(F32))
        dob = _bf(do_ref[...])
        jp0 = (i * B) // KB

        def strict_mask():
            return (jp0 * KB + _iota2(KB, WQ, 0)) < (i * B + (_iota2(KB, WQ, 1) & (B - 1)))

        def heads(fn):
            return [fn(slice(hh * SB_HD, (hh + 1) * SB_HD)) for hh in range(HG)]

        def pass1(jp, cb, masked):
            off = pl.multiple_of(jp * KB, KB)
            z = jnp.concatenate(heads(lambda cs: _dot_nt(k_ref[pl.ds(off, KB), cs], q_ref[:, cs])), axis=1) * scale
            da = jnp.concatenate(heads(lambda cs: _dot_nt(v_ref[pl.ds(off, KB), cs], dob[:, cs])), axis=1)
            lsz, l1m = _sb_logs(z)
            if masked:
                strict = strict_mask()
                l1m = jnp.where(strict, l1m, 0.0)
            a = jnp.exp(lsz + cb + _tri2_left(later, l1m))
            if masked:
                a = jnp.where(strict, a, 0.0)
            g_scr[jp] = a * da
            beta_scr[jp] = jnp.exp(lsz)
            ab = _bf(a)
            for hh in range(HG):
                cs = slice(hh * SB_HD, (hh + 1) * SB_HD)
                dv_scr[pl.ds(off, KB), cs] += _dot(ab[:, hh * B:(hh + 1) * B], dob[:, cs])
            return cb + jnp.sum(l1m, axis=0, keepdims=True)

        zero = jnp.zeros((1, WQ), F32)
        cb = pass1(jp0, zero, True)

        def live(state):
            jj, _, dead = state
            return (jj <= jp0) & jnp.logical_not(dead)

        def step(state):
            jj, cr, _ = state
            cr = pass1(jp0 - jj, cr, False)
            return jj + 1, cr, jnp.max(cr) < SB_DEAD

        n_done, _, _ = lax.while_loop(live, step, (jnp.int32(1), cb, jnp.max(cb) < SB_DEAD))
        jp_first = jp0 - (n_done - 1)

        def pass2(jp, cg, masked):
            off = pl.multiple_of(jp * KB, KB)
            g = g_scr[jp]
            beta = beta_scr[jp]
            dz = g * (1.0 - beta) - beta * (cg + _tri2_left(earlier, g))
            if masked:
                dz = jnp.where(strict_mask(), dz, 0.0)
            dzb = _bf(dz * scale)
            for hh in range(HG):
                cs = slice(hh * SB_HD, (hh + 1) * SB_HD)
                dk_scr[pl.ds(off, KB), cs] += _dot(dzb[:, hh * B:(hh + 1) * B], q_ref[:, cs])
                dqt_scr[hh] += _dot(kt_scr[hh, jp], dzb[:, hh * B:(hh + 1) * B])
            return cg + jnp.sum(g, axis=0, keepdims=True)

        cg = lax.fori_loop(jp_first, jp0, lambda jp, cr: pass2(jp, cr, False), zero)
        pass2(jp0, cg, True)
        for hh in range(HG):
            dq_ref[:, hh * SB_HD:(hh + 1) * SB_HD] = _bf(dqt_scr[hh].T)

        @pl.when(i == nb - 1)
        def _():
            dk_ref[...] = _bf(dk_scr[...])
            dv_ref[...] = _bf(dv_scr[...])

        @pl.when((pl.program_id(0) == n_h - 1) & (i == nb - 1))
        def _():
            _push_wait(own, pairs)

    return pl.pallas_call(
        body, name="sb_bwd",
        grid=(n_h, nb),
        in_specs=[pl.BlockSpec((None, B, W), lambda h, i: (1, i, h)),
                  pl.BlockSpec((None, T, W), lambda h, i: (2, 0, h)),
                  pl.BlockSpec((None, T, W), lambda h, i: (3, 0, h)),
                  pl.BlockSpec((B, W), lambda h, i: (i, h)),
                  _ANY],
        out_specs=(pl.BlockSpec((B, W), lambda h, i: (i, h)),
                   pl.BlockSpec((T, W), lambda h, i: (0, h)),
                   pl.BlockSpec((T, W), lambda h, i: (0, h)),
                   _ANY),
        out_shape=(jax.ShapeDtypeStruct((T, 1024), BF16),
                   jax.ShapeDtypeStruct((T, 1024), BF16),
                   jax.ShapeDtypeStruct((T, 1024), BF16),
                   jax.ShapeDtypeStruct(g_p.shape, g_p.dtype)),
        scratch_shapes=[pltpu.VMEM((T, W), F32), pltpu.VMEM((T, W), F32),
                        pltpu.VMEM((HG, nkb, SB_HD, KB), BF16),
                        pltpu.VMEM((nkb, KB, WQ), F32), pltpu.VMEM((nkb, KB, WQ), F32),
                        pltpu.VMEM((HG, SB_HD, B), F32)] + _PUSH_SEMS,
        compiler_params=_cparams(("arbitrary", "arbitrary")),
    )(projb, projb, projb, do_sb, g_p)


def _mid_call(o_gla, o_sb, projf, x, target, wpa, wpb, wo, gla_g, b_gate, final_g):
    T, D = x.shape
    tm = min(TBLK, T)

    def body(og_ref, ggate_ref, osb_ref, sgate_ref, ma_ref, mb_ref, x_ref, tgt_ref,
             wpa_ref, wpb_ref, wo_ref, glag_ref, bg_ref, fg_ref,
             dx2_ref, dogla_ref, dosb_ref, dggate_ref, dsgate_ref, dm_ref,
             mt_ref, ogt_ref, obt_ref, dx2b_ref, dya_ref, dyb_ref,
             dfg_ref, dbg_ref, dglag_ref, loss_ref):
        @pl.when(pl.program_id(0) == 0)
        def _():
            dfg_ref[...] = jnp.zeros_like(dfg_ref)
            dbg_ref[...] = jnp.zeros_like(dbg_ref)
            dglag_ref[...] = jnp.zeros_like(dglag_ref)
            loss_ref[...] = jnp.zeros_like(loss_ref)

        glag = glag_ref[...]
        ggate = ggate_ref[...]
        sg = _sigmoid(ggate)
        silu_g = ggate * sg
        ohat, rinv, nrm = [], [], []
        for hh in range(GLA_HEADS):
            oh = og_ref[:, hh * GLA_HV:(hh + 1) * GLA_HV]
            r = lax.rsqrt(jnp.mean(oh * oh, axis=-1, keepdims=True) + EPS)
            ohat.append(oh * r)
            rinv.append(r)
            nrm.append(ohat[-1] * glag)
        n_all = jnp.concatenate(nrm, axis=1)
        og = n_all * silu_g
        ogb = _bf(og)
        ya = _dot(ogb, wpa_ref[...])
        sgate = sgate_ref[...]
        ss = _sigmoid(sgate)
        silu_s = sgate * ss
        osb = osb_ref[...]
        ob = osb * silu_s
        obb = _bf(ob)
        yb = _dot(obb, wpb_ref[...])
        ga = _sigmoid(ma_ref[...] + bg_ref[:, :D])
        gb = _sigmoid(mb_ref[...] + bg_ref[:, D:])
        merged = ga * ya + gb * yb
        mgb = _bf(merged)
        x2 = x_ref[...] + _dot(mgb, wo_ref[...])
        r2 = lax.rsqrt(jnp.mean(x2 * x2, axis=-1, keepdims=True) + EPS)
        xh2 = x2 * r2
        fg = fg_ref[...]
        err = xh2 * fg - tgt_ref[...]
        loss_ref[...] += jnp.broadcast_to(
            0.5 * jnp.sum(jnp.mean(err * err, axis=-1, keepdims=True), axis=0, keepdims=True), (1, 128))
        dy = err * (1.0 / D)
        dfg_ref[...] += jnp.sum(dy * xh2, axis=0, keepdims=True)
        dxh = dy * fg
        dx2 = r2 * (dxh - xh2 * jnp.mean(dxh * xh2, axis=-1, keepdims=True))
        dx2_ref[...] = dx2
        dx2b = _bf(dx2)
        dx2b_ref[...] = dx2b
        dmerged = _dot_nt(dx2b, wo_ref[...])
        dya = dmerged * ga
        dyb = dmerged * gb
        dma = dmerged * ya * ga * (1.0 - ga)
        dmb = dmerged * yb * gb * (1.0 - gb)
        dm_ref[:, :D] = _bf(dma)
        dm_ref[:, D:] = _bf(dmb)
        dbg_ref[:, :D] += jnp.sum(dma, axis=0, keepdims=True)
        dbg_ref[:, D:] += jnp.sum(dmb, axis=0, keepdims=True)
        dyab = _bf(dya)
        dybb = _bf(dyb)
        dya_ref[...] = dyab
        dyb_ref[...] = dybb
        dog = _dot_nt(dyab, wpa_ref[...])
        dob = _dot_nt(dybb, wpb_ref[...])
        dosb_ref[...] = dob * silu_s
        dsgate_ref[...] = _bf(dob * osb * (ss * (1.0 + sgate * (1.0 - ss))))
        dn = dog * silu_g
        dggate_ref[...] = _bf(dog * n_all * (sg * (1.0 + ggate * (1.0 - sg))))
        dglag = jnp.zeros((1, GLA_HV), F32)
        for hh in range(GLA_HEADS):
            dnh = dn[:, hh * GLA_HV:(hh + 1) * GLA_HV]
            dglag = dglag + jnp.sum(dnh * ohat[hh], axis=0, keepdims=True)
            dohat = dnh * glag
            dogla_ref[:, hh * GLA_HV:(hh + 1) * GLA_HV] = rinv[hh] * (
                dohat - ohat[hh] * jnp.mean(dohat * ohat[hh], axis=-1, keepdims=True))
        dglag_ref[...] += dglag
        mt_ref[...] = _bf(merged.T)
        ogt_ref[...] = _bf(og.T)
        obt_ref[...] = _bf(ob.T)

    row = lambda i: (i, 0)
    const = lambda i: (0, 0)
    tile = pl.BlockSpec((tm, D), row)
    tile_t = pl.BlockSpec((None, D, tm), lambda i: (i, 0, 0))
    wspec = pl.BlockSpec((D, D), const)
    return pl.pallas_call(
        body, name="mid",
        grid=(T // tm,),
        in_specs=[tile,
                  pl.BlockSpec((None, tm, D), lambda i: (1, i, 0)),
                  tile,
                  pl.BlockSpec((None, tm, D), lambda i: (2, i, 0)),
                  pl.BlockSpec((None, tm, D), lambda i: (3, i, 0)),
                  pl.BlockSpec((None, tm, D), lambda i: (4, i, 0)),
                  tile, tile, wspec, wspec, wspec,
                  pl.BlockSpec((1, GLA_HV), const),
                  pl.BlockSpec((1, 2 * D), const),
                  pl.BlockSpec((1, D), const)],
        out_specs=(tile, tile, tile, tile, tile,
                   pl.BlockSpec((tm, 2 * D), row),
                   tile_t, tile_t, tile_t, tile, tile, tile,
                   pl.BlockSpec((1, D), const),
                   pl.BlockSpec((1, 2 * D), const),
                   pl.BlockSpec((1, GLA_HV), const),
                   pl.BlockSpec((1, 128), const)),
        out_shape=(jax.ShapeDtypeStruct((T, D), F32),
                   jax.ShapeDtypeStruct((T, D), F32),
                   jax.ShapeDtypeStruct((T, D), F32),
                   jax.ShapeDtypeStruct((T, D), BF16),
                   jax.ShapeDtypeStruct((T, D), BF16),
                   jax.ShapeDtypeStruct((T, 2 * D), BF16),
                   jax.ShapeDtypeStruct((T // tm, D, tm), BF16),
                   jax.ShapeDtypeStruct((T // tm, D, tm), BF16),
                   jax.ShapeDtypeStruct((T // tm, D, tm), BF16),
                   jax.ShapeDtypeStruct((T, D), BF16),
                   jax.ShapeDtypeStruct((T, D), BF16),
                   jax.ShapeDtypeStruct((T, D), BF16),
                   jax.ShapeDtypeStruct((1, D), F32),
                   jax.ShapeDtypeStruct((1, 2 * D), F32),
                   jax.ShapeDtypeStruct((1, GLA_HV), F32),
                   jax.ShapeDtypeStruct((1, 128), F32)),
        compiler_params=_cparams(("arbitrary",)),
    )(o_gla, projf, o_sb, projf, projf, projf, x, target, wpa, wpb, wo, gla_g, b_gate, final_g)


def _dh_call(pieces, dmlog, drank, wt, wr, x, dx2, norm_g, s_in, small):
    T, D = x.shape
    tm = min(256, T)
    npc = len(pieces)
    n_main = N_GROUPS * 1024
    n_i = T // tm
    i_forward = 5 * n_i // 8

    def body(*refs):
        pcs = refs[:npc]
        (dm_ref, dr_ref, w_hbm, wr_ref, x_ref, dx2_ref, g_ref, sin_ref, small_ref,
         gx_ref, rin_ref, relayed_ref, rsmall_ref,
         w_scr, sems, dg_ref, small_mine, small_send, small_recv, small_loc, *exchange_scratch) = refs[npc:]
        start, forward, finish = _chip_reduce_steps(sin_ref, rin_ref, relayed_ref, *exchange_scratch)

        @pl.when(pl.program_id(0) == 0)
        def _():
            start()
            lo = pltpu.make_async_copy(w_hbm.at[pl.ds(0, RANK_COL)], w_scr.at[pl.ds(0, RANK_COL)], sems.at[0])
            hi = pltpu.make_async_copy(w_hbm.at[pl.ds(RANK_COL + GLA_RANK, n_main - RANK_COL)],
                                       w_scr.at[pl.ds(RANK_COL, n_main - RANK_COL)], sems.at[1])
            lo.start()
            hi.start()
            dg_ref[...] = jnp.zeros_like(dg_ref)
            lo.wait()
            hi.wait()

        @pl.when(pl.program_id(0) == i_forward)
        def _():
            forward()

        def w_group(g):
            return w_scr[g * 1024:(g + 1) * 1024, :]

        dr = dr_ref[...]
        dh = _dot(dr, wr_ref[...])
        for g in range(npc):
            dh = dh + _dot(pcs[g][...], w_group(g))
        dh = dh + _dot(dm_ref[:, :D], w_group(npc))
        dh = dh + _dot(dm_ref[:, D:], w_group(npc + 1))
        xv = x_ref[...]
        r = lax.rsqrt(jnp.mean(xv * xv, axis=-1, keepdims=True) + EPS)
        xhat = xv * r
        g = g_ref[...]
        dg_ref[...] += jnp.sum(dh * xhat, axis=0, keepdims=True)
        dxhat = dh * g
        gx_ref[...] = r * (dxhat - xhat * jnp.mean(dxhat * xhat, axis=-1, keepdims=True)) + dx2_ref[...]

        @pl.when(pl.program_id(0) == n_i - 1)
        def _():
            small_mine[...] = small_ref[...]
            small_mine[:, _SM_NORM:_SM_NORM + D] = dg_ref[...]
            own, pairs = _push_copies(small_mine, rsmall_ref, small_send, small_recv, small_loc, scatter=False)
            _push_start(own, pairs)
            finish()
            _push_wait(own, pairs)

    row = lambda i: (i, 0)
    const = lambda i: (0, 0)
    tile = pl.BlockSpec((tm, D), row)
    part = s_in.shape[1:]
    return pl.pallas_call(
        body, name="dh",
        grid=(n_i,),
        in_specs=[tile] * npc + [
            pl.BlockSpec((tm, 2 * D), row),
            pl.BlockSpec((tm, 128), row),
            _ANY,
            pl.BlockSpec((128, D), const),
            tile, tile,
            pl.BlockSpec((1, D), const),
            _ANY,
            pl.BlockSpec(small.shape, const)],
        out_specs=(tile, _ANY, _ANY, _ANY),
        out_shape=(jax.ShapeDtypeStruct((T, D), F32),
                   jax.ShapeDtypeStruct((3,) + part, s_in.dtype),
                   jax.ShapeDtypeStruct(part, s_in.dtype),
                   jax.ShapeDtypeStruct((N_DEV,) + small.shape, small.dtype)),
        scratch_shapes=[pltpu.VMEM((n_main, D), BF16), pltpu.SemaphoreType.DMA((2,)),
                        pltpu.VMEM((1, D), F32), pltpu.VMEM(small.shape, small.dtype)]
        + _PUSH_SEMS + _chip_reduce_scratch(*part, s_in.dtype),
        compiler_params=_cparams(("arbitrary",)),
    )(*pieces, dmlog, drank, wt, wr, x, dx2, norm_g, s_in, small)


def _wgrad_call(lhs_list, lhs_of_group, rhs_list, rhs_of_group, n_transposed, name, narrow=None):
    n_groups = len(rhs_of_group)
    n_tb, D, tb = lhs_list[0].shape
    T = n_tb * tb
    per = min(4, n_tb)
    tk = per * tb
    nk = T // tk
    nl = len(lhs_list)
    extra = [] if narrow is None else [narrow]

    def body(*refs):
        lhs = refs[:nl]
        rhs = refs[nl:nl + n_groups]
        rest = refs[nl + n_groups:]
        g = pl.program_id(0)
        i = pl.program_id(1)
        if narrow is None:
            out_ref, acc = rest
        else:
            narrow_ref, out_ref, narrow_out, acc, narrow_acc = rest

            @pl.when((g == 0) & (i == 0))
            def _():
                narrow_acc[...] = jnp.zeros_like(narrow_acc)

            @pl.when(g == 0)
            def _():
                lref = lhs[lhs_of_group[0]]
                for b in range(per):
                    narrow_acc[...] += _dot(lref[b], narrow_ref[b * tb:(b + 1) * tb, :])

            @pl.when((g == 0) & (i == nk - 1))
            def _():
                narrow_out[...] = _bf(narrow_acc[...].T)

        @pl.when(i == 0)
        def _():
            acc[...] = jnp.zeros_like(acc)

        for p in range(n_groups):
            @pl.when(g == p)
            def _(p=p):
                lref = lhs[lhs_of_group[p]]
                part = _dot(lref[0], rhs[p][0:tb, :])
                for b in range(1, per):
                    part = part + _dot(lref[b], rhs[p][b * tb:(b + 1) * tb, :])
                acc[...] += part

        @pl.when((i == nk - 1) & (g < n_transposed))
        def _():
            out_ref[...] = _bf(acc[...].T)

        @pl.when((i == nk - 1) & (g >= n_transposed))
        def _():
            out_ref[...] = _bf(acc[...])

    def lhs_spec(a):
        groups = [g for g in range(n_groups) if lhs_of_group[g] == a]
        lo, hi = min(groups), max(groups)
        assert groups == list(range(lo, hi + 1))
        return pl.BlockSpec((per, D, tb), lambda g, i: (jnp.where((g >= lo) & (g <= hi), i, 0), 0, 0))

    def rhs_spec(p):
        cb = rhs_of_group[p][1]
        return pl.BlockSpec((tk, 1024), lambda g, i: (jnp.where(g == p, i, 0), cb))

    res = pl.pallas_call(
        body, name=name,
        grid=(n_groups, nk),
        in_specs=[lhs_spec(a) for a in range(nl)] + [rhs_spec(p) for p in range(n_groups)]
        + [pl.BlockSpec((tk, 128), lambda g, i: (jnp.where(g == 0, i, 0), 0)) for _ in extra],
        out_specs=[pl.BlockSpec((None, D, 1024), lambda g, i: (g, 0, 0))]
        + [pl.BlockSpec((128, D), lambda g, i: (0, 0)) for _ in extra],
        out_shape=[jax.ShapeDtypeStruct((n_groups, D, 1024), BF16)]
        + [jax.ShapeDtypeStruct((128, D), BF16) for _ in extra],
        scratch_shapes=[pltpu.VMEM((D, 1024), F32)] + [pltpu.VMEM((D, 128), F32) for _ in extra],
        compiler_params=_cparams(("arbitrary", "arbitrary")),
    )(*lhs_list, *[rhs_list[rhs_of_group[p][0]] for p in range(n_groups)], *extra)
    return res[0] if narrow is None else res


def _adamw_math(parts, w, m, v):
    g = parts[0].astype(F32)
    for p in parts[1:]:
        g = g + p.astype(F32)
    mm = ADAM_B1 * m + (1.0 - ADAM_B1) * g
    vv = ADAM_B2 * v + (1.0 - ADAM_B2) * (g * g)
    m_hat = mm / (1.0 - ADAM_B1 ** ADAM_STEP)
    v_hat = vv / (1.0 - ADAM_B2 ** ADAM_STEP)
    return g, -ADAM_LR * (m_hat / (jnp.sqrt(v_hat) + ADAM_EPS) + ADAM_WD * w), mm, vv


def _part_order(n_parts):
    return [n_parts - 1] + list(range(n_parts - 1))


def _adamw_call(parts, w, m, v, name):
    R, C = w.shape
    n_parts = parts.shape[0]
    (tr, tc), grid, idx = _tiling_2d(R, C, 512)

    def body(p_ref, w_ref, m_ref, v_ref, g_ref, d_ref, nm_ref, nv_ref):
        g_ref[...], d_ref[...], nm_ref[...], nv_ref[...] = _adamw_math(
            [p_ref[k] for k in _part_order(n_parts)], w_ref[...], m_ref[...], v_ref[...])

    blk = pl.BlockSpec((tr, tc), idx)
    sds = jax.ShapeDtypeStruct((R, C), F32)
    return pl.pallas_call(
        body, name=name,
        grid=grid,
        in_specs=[pl.BlockSpec((n_parts, tr, tc), lambda i: (0,) + idx(i)), blk, blk, blk],
        out_specs=(blk, blk, blk, blk),
        out_shape=(sds, sds, sds, sds),
        compiler_params=_cparams(("arbitrary",)),
    )(parts, w, m, v)


def _adamw_rows_call(parts, ws, ms, vs, name):
    n = len(ws)
    R, C = ws[0].shape
    n_parts = parts.shape[0]

    def body(*refs):
        p_ref = refs[0]
        w_refs, m_refs, v_refs = refs[1:1 + n], refs[1 + n:1 + 2 * n], refs[1 + 2 * n:1 + 3 * n]
        outs = refs[1 + 3 * n:]
        for k in range(n):
            @pl.when(pl.program_id(0) == k)
            def _(k=k):
                res = _adamw_math([p_ref[j] for j in _part_order(n_parts)],
                                  w_refs[k][...], m_refs[k][...], v_refs[k][...])
                for o_ref, val in zip(outs[4 * k:4 * k + 4], res):
                    o_ref[...] = val

    whole = pl.BlockSpec((R, C), lambda k: (0, 0))
    sds = jax.ShapeDtypeStruct((R, C), F32)
    res = pl.pallas_call(
        body, name=name,
        grid=(n,),
        in_specs=[pl.BlockSpec((n_parts, R, C), lambda k: (0, k, 0))] + [whole] * (3 * n),
        out_specs=tuple([whole] * (4 * n)),
        out_shape=tuple([sds] * (4 * n)),
        compiler_params=_cparams(("arbitrary",)),
    )(parts, *ws, *ms, *vs)
    return [res[4 * k:4 * k + 4] for k in range(n)]


def _adamw_lanes_call(parts, offsets, ws, ms, vs, name):
    n = len(ws)
    n_parts = parts.shape[0]

    def body(*refs):
        p_ref = refs[0]
        w_refs, m_refs, v_refs = refs[1:1 + n], refs[1 + n:1 + 2 * n], refs[1 + 2 * n:1 + 3 * n]
        outs = refs[1 + 3 * n:]
        for k in range(n):
            lanes = slice(offsets[k], offsets[k] + ws[k].shape[1])
            res = _adamw_math([p_ref[j, :, lanes] for j in _part_order(n_parts)],
                              w_refs[k][...], m_refs[k][...], v_refs[k][...])
            for o_ref, val in zip(outs[4 * k:4 * k + 4], res):
                o_ref[...] = val

    res = pl.pallas_call(
        body, name=name,
        out_shape=tuple(jax.ShapeDtypeStruct(ws[k].shape, F32) for k in range(n) for _ in range(4)),
        compiler_params=_cparams(),
    )(parts, *ws, *ms, *vs)
    return [res[4 * k:4 * k + 4] for k in range(n)]


def _local_step(x, target, wt, wr, wdec, bdec, wp_shard, norm_g, gla_g, b_gate, final_g):
    D = x.shape[1]
    half = wp_shard.shape[1] // 2
    projf, projb, rank, ht, wp_lo = _proj_call(x, norm_g, wt, wr, wp_shard[:, :half])
    o_gla, st_all, la = _gla_fwd_call(projf, projb, rank, wdec, bdec)
    o_sb, wp_hi = _sb_fwd_call(projb, wp_shard[:, half:])
    wp_full = jnp.concatenate([wp_lo, wp_hi], axis=2).transpose(1, 0, 2, 3).reshape(3, D, D)
    (dx2, do_gla, do_sb, dggate, dsgate, dmlog, mt, ogt, obt, dx2b, dya, dyb,
     dfinal_g, db_gate, dgla_g, loss) = _mid_call(o_gla, o_sb, projf, x, target, wp_full[0], wp_full[1],
                                                 wp_full[2], gla_g, b_gate, final_g)
    dw_p = _wgrad_call([ogt, obt, mt], [0, 1, 2], [dya, dyb, dx2b], [(0, 0), (1, 0), (2, 0)], 0, "wgrad_p")
    g_p = dw_p.reshape(3, N_DEV, D // N_DEV, D).transpose(1, 0, 2, 3).reshape(N_DEV, 3 * (D // N_DEV), D)
    dqk, dgv, drank, dwdec, dbdec = _gla_bwd_call(projf, projb, la, do_gla, st_all, rank, wdec)
    dsq, dsk, dsv, r_p = _sb_bwd_call(projb, do_sb, g_p)
    pieces = [dqk, dgv, dggate, dsq, dsk, dsv, dsgate]
    rhs_of_group = [(g, 0) for g in range(7)] + [(7, 0), (7, 1)]
    dw_in, dwr = _wgrad_call([ht], [0] * N_GROUPS, pieces + [dmlog], rhs_of_group, N_GROUPS, "wgrad_in",
                             narrow=drank)
    s_in = _pair_sum_call(dw_in.reshape(N_GROUPS * 1024, D), dwr[:GLA_RANK])
    small = jnp.concatenate([
        jnp.zeros((D,), F32), dbdec.reshape(-1), dgla_g.reshape(-1), db_gate.reshape(-1), dfinal_g.reshape(-1),
        loss.reshape(-1), dwdec[:GLA_RANK].reshape(-1)]).reshape(1, _SM_LEN)
    grad_x, r_in, _, r_small = _dh_call(pieces, dmlog, drank, wt, wr, x, dx2, norm_g, s_in, small)
    return grad_x, r_in, r_p, r_small


_SM_NORM = 0
_SM_BDEC = _SM_NORM + D_MODEL
_SM_GLAG = _SM_BDEC + GLA_DK
_SM_BGATE = _SM_GLAG + GLA_HV
_SM_FINAL = _SM_BGATE + 2 * D_MODEL
_SM_REPL = _SM_FINAL + D_MODEL
_SM_LOSS = _SM_REPL
_SM_WDEC = _SM_LOSS + 128
_SM_LEN = _SM_WDEC + GLA_RANK * GLA_DK


def kernel(x, norm_g, w_in, w_dec_up, b_dec, gla_norm_g, w_pa, w_pb, b_gate, w_o, final_g, loss_target, m_norm_g, m_w_in, m_w_dec_up, m_b_dec, m_gla_norm_g, m_w_pa, m_w_pb, m_b_gate, m_w_o, m_final_g, v_norm_g, v_w_in, v_w_dec_up, v_b_dec, v_gla_norm_g, v_w_pa, v_w_pb, v_b_gate, v_w_o, v_final_g):
    D = D_MODEL
    me = 4 * lax.axis_index("x") + 2 * lax.axis_index("y") + lax.axis_index("c")

    wp_shard = jnp.stack([w_pa, w_pb, w_o]).astype(BF16)
    n_first = _half_rows(SHARD_COLS)
    win_all, wdec_all = _all_gather([w_in.T.astype(BF16), w_dec_up], "gather_w",
                                    row_pieces=[[(0, n_first), (n_first, SHARD_COLS - n_first)], None])
    wt = _flatten_blocks_call(win_all)
    wr = jnp.pad(wt[RANK_COL:RANK_COL + GLA_RANK], ((0, 128 - GLA_RANK), (0, 0)))
    wdec_full = wdec_all.transpose(1, 0, 2).reshape(GLA_RANK, GLA_DK)
    wdec = jnp.pad(wdec_full, ((0, 128 - GLA_RANK), (0, 0)))

    grad_x, r_in, r_p, r_small = _local_step(
        x[0], loss_target[0], wt, wr, wdec, b_dec.reshape(1, -1), wp_shard,
        norm_g.reshape(1, -1), gla_norm_g.reshape(1, -1), b_gate.reshape(1, -1), final_g.reshape(1, -1))

    gw_in, d_in, nm_in, nv_in = (a.T for a in _adamw_call(r_in, w_in.T, m_w_in.T, v_w_in.T, "adamw_in"))
    (g_pa, d_pa, nm_pa, nv_pa), (g_pb, d_pb, nm_pb, nv_pb), (g_o, d_o, nm_o, nv_o) = _adamw_rows_call(
        r_p, [w_pa, w_pb, w_o], [m_w_pa, m_w_pb, m_w_o], [v_w_pa, v_w_pb, v_w_o], "adamw_p")

    def row(a):
        return a.reshape(1, -1)

    rep = _adamw_lanes_call(
        r_small, [_SM_NORM, _SM_BDEC, _SM_GLAG, _SM_BGATE, _SM_FINAL],
        [row(a) for a in (norm_g, b_dec, gla_norm_g, b_gate, final_g)],
        [row(a) for a in (m_norm_g, m_b_dec, m_gla_norm_g, m_b_gate, m_final_g)],
        [row(a) for a in (v_norm_g, v_b_dec, v_gla_norm_g, v_b_gate, v_final_g)], "adamw_rep")
    ((g_norm, d_norm, nm_norm, nv_norm), (g_bdec, d_bdec, nm_bdec, nv_bdec), (g_glag, d_glag, nm_glag, nv_glag),
     (g_bgate, d_bgate, nm_bgate, nv_bgate), (g_final, d_final, nm_final, nv_final)) = [
        tuple(a.reshape(-1) for a in quad) for quad in rep]

    wdec_parts = r_small[:, 0, _SM_WDEC:].reshape(N_DEV, GLA_RANK, GLA_DK)
    cols = GLA_DK // N_DEV
    wdec_mine = lax.dynamic_slice_in_dim(wdec_parts, me * cols, cols, axis=2)
    g_wdec, d_wdec, nm_wdec, nv_wdec = _adamw_call(wdec_mine, w_dec_up, m_w_dec_up, v_w_dec_up, "adamw_dec")

    loss_total = jnp.sum(r_small[:, 0, _SM_LOSS])

    return (loss_total, grad_x[None],
            g_norm, gw_in, g_wdec, g_bdec, g_glag, g_pa, g_pb, g_bgate, g_o, g_final,
            d_norm, d_in, d_wdec, d_bdec, d_glag, d_pa, d_pb, d_bgate, d_o, d_final,
            nm_norm, nm_in, nm_wdec, nm_bdec, nm_glag, nm_pa, nm_pb, nm_bgate, nm_o, nm_final,
            nv_norm, nv_in, nv_wdec, nv_bdec, nv_glag, nv_pa, nv_pb, nv_bgate, nv_o, nv_final)
```

```python
import math

import jax
import jax.numpy as jnp
from jax import lax
from jax.experimental import pallas as pl
from jax.experimental.pallas import tpu as pltpu

F32 = jnp.float32
BF16 = jnp.bfloat16

N_DEV = 8
D_MODEL = 1024
GLA_HEADS = 4
GLA_HK = 128
GLA_HV = 256
GLA_DK = 512
GLA_RANK = 16
GLA_TAU = 16.0
GLA_CHUNK = 64
SB_HEADS = 8
SB_HD = 128
EPS = 1e-6
N_GROUPS = 9
RANK_COL = 3072
IN_COLS = 9232
SHARD_COLS = IN_COLS // N_DEV

ADAM_LR = 0.001
ADAM_B1 = 0.9
ADAM_B2 = 0.999
ADAM_EPS = 1e-08
ADAM_WD = 0.01
ADAM_STEP = 10

VMEM_LIMIT = 56 * 1024 * 1024
TBLK = 256


def _cparams(sem=None):
    return pltpu.CompilerParams(dimension_semantics=sem, vmem_limit_bytes=VMEM_LIMIT)


def _tiling_2d(rows, cols, band_cols):
    if rows * cols <= 128 * 1024:
        return (rows, cols), (1,), lambda i: (0, 0)
    if rows % 128 == 0:
        return (128, cols), (rows // 128,), lambda i: (i, 0)
    tc = band_cols if cols % band_cols == 0 else cols
    return (rows, tc), (cols // tc,), lambda i: (0, i)


def _dot(a, b):
    return jnp.dot(a, b, preferred_element_type=F32)


def _dot_nt(a, b):
    return lax.dot_general(a, b, (((1,), (1,)), ((), ())), preferred_element_type=F32)


def _dot_tn(a, b):
    return lax.dot_general(a, b, (((0,), (0,)), ((), ())), preferred_element_type=F32)


def _bf(x):
    return x.astype(BF16)


def _split3(x):
    hi = x.astype(BF16)
    r = x - hi.astype(F32)
    mid = r.astype(BF16)
    lo = (r - mid.astype(F32)).astype(BF16)
    return hi, mid, lo


def _tri_left(tri, x):
    hi, mid, lo = _split3(x)
    return _dot(tri, hi) + _dot(tri, mid) + _dot(tri, lo)


def _split2(x):
    hi = lax.bitcast_convert_type(lax.bitcast_convert_type(x, jnp.uint32) & jnp.uint32(0xFFFF0000), F32)
    return hi.astype(BF16), (x - hi).astype(BF16)


def _tri2_left(tri, x):
    hi, lo = _split2(x)
    return _dot(tri, hi) + _dot(tri, lo)


def _tri2_right(x, tri):
    hi, lo = _split2(x)
    return _dot(hi, tri) + _dot(lo, tri)


def _iota2(n, m, dim):
    return lax.broadcasted_iota(jnp.int32, (n, m), dim)


def _sigmoid(x):
    return 1.0 / (1.0 + jnp.exp(-x))


def _softplus_neg_abs(z):
    return jnp.log(1.0 + jnp.exp(-jnp.abs(z)))


_ANY = pl.BlockSpec(memory_space=pl.ANY)


def _mesh_pos():
    return lax.axis_index("x"), lax.axis_index("y"), lax.axis_index("c")


def _other_chips(x, y):
    return [(1 - x, y), (x, 1 - y), (1 - x, 1 - y)]


def _rcopy(src, dst, send_sem, recv_sem, to):
    return pltpu.make_async_remote_copy(src_ref=src, dst_ref=dst, send_sem=send_sem, recv_sem=recv_sem,
                                        device_id=to, device_id_type=pl.DeviceIdType.MESH)


def _push_copies(src_ref, dst_ref, send_sems, recv_sems, loc_sem, scatter):
    x, y, c = _mesh_pos()
    me = 4 * x + 2 * y + c
    own = pltpu.make_async_copy(src_ref.at[me] if scatter else src_ref, dst_ref.at[me], loc_sem)
    pairs = []
    for k in range(1, N_DEV):
        px = 1 - x if k & 4 else x
        py = 1 - y if k & 2 else y
        pc = 1 - c if k & 1 else c
        pid = 4 * px + 2 * py + pc
        src = src_ref.at[pid] if scatter else src_ref
        send = _rcopy(src, dst_ref.at[me], send_sems.at[k - 1], recv_sems.at[k - 1], (px, py, pc))
        recv = _rcopy(src, dst_ref.at[pid], send_sems.at[k - 1], recv_sems.at[k - 1], (px, py, pc))
        pairs.append((send, recv))
    return own, pairs


def _push_start(own, pairs):
    own.start()
    for send, _ in pairs:
        send.start()


def _push_wait(own, pairs):
    for _, recv in pairs:
        recv.wait_recv()
    for send, _ in pairs:
        send.wait_send()
    own.wait()


_PUSH_SEMS = [pltpu.SemaphoreType.DMA((N_DEV - 1,)), pltpu.SemaphoreType.DMA((N_DEV - 1,)),
              pltpu.SemaphoreType.DMA]


def _half_rows(rows):
    return (rows // 2) // 16 * 16


_ADD_ROWS = 128


def _chip_reduce_steps(src_ref, dst_ref, relayed_ref, sum_x, sum_y, rel_x, rel_y, load_sems, send_sems, recv_sems,
                       loc_sem):
    _, R, C = src_ref.shape
    n0 = _half_rows(R)
    lo, hi = pl.ds(0, n0), pl.ds(n0, R - n0)
    x, y, c = _mesh_pos()
    (xx, xy), (yx, yy), (dx, dy) = _other_chips(x, y)
    to_diag, to_x, to_y = src_ref.at[2 * dx + dy], src_ref.at[2 * xx + xy], src_ref.at[2 * yx + yy]
    x_nb, y_nb = (xx, xy, c), (yx, yy, c)
    relays = (_rcopy(to_diag.at[lo], relayed_ref.at[lo], send_sems.at[0], recv_sems.at[0], x_nb),
              _rcopy(to_diag.at[hi], relayed_ref.at[hi], send_sems.at[1], recv_sems.at[1], y_nb))
    plain = (_rcopy(to_x.at[lo], dst_ref.at[0, lo], send_sems.at[2], recv_sems.at[2], x_nb),
             _rcopy(to_y.at[hi], dst_ref.at[1, hi], send_sems.at[3], recv_sems.at[3], y_nb))
    summed = (_rcopy(sum_x, dst_ref.at[0, hi], send_sems.at[4], recv_sems.at[4], x_nb),
              _rcopy(sum_y, dst_ref.at[1, lo], send_sems.at[5], recv_sems.at[5], y_nb))
    load_mine = (pltpu.make_async_copy(to_x.at[hi], sum_x, load_sems.at[0]),
                 pltpu.make_async_copy(to_y.at[lo], sum_y, load_sems.at[1]))
    load_relayed = (pltpu.make_async_copy(relayed_ref.at[hi], rel_x, load_sems.at[2]),
                    pltpu.make_async_copy(relayed_ref.at[lo], rel_y, load_sems.at[3]))
    own = pltpu.make_async_copy(src_ref.at[2 * x + y], dst_ref.at[2], loc_sem)

    def start():
        for cp in relays + plain + (own,) + load_mine:
            cp.start()

    def add(acc_ref, rel_ref):
        for r0 in range(0, acc_ref.shape[0], _ADD_ROWS):
            rows = slice(r0, min(r0 + _ADD_ROWS, acc_ref.shape[0]))
            acc_ref[rows, :] = (acc_ref[rows, :].astype(F32) + rel_ref[rows, :].astype(F32)).astype(acc_ref.dtype)

    def forward():
        for cp in relays:
            cp.wait_recv()
        for cp in load_relayed:
            cp.start()
        for cp in load_mine + load_relayed:
            cp.wait()
        add(sum_x, rel_x)
        add(sum_y, rel_y)
        for cp in summed:
            cp.start()

    def finish():
        for cp in plain + summed:
            cp.wait_recv()
        for cp in relays + plain + summed:
            cp.wait_send()
        own.wait()

    return start, forward, finish


def _chip_reduce_scratch(rows, cols, dtype):
    n0 = _half_rows(rows)
    return [pltpu.VMEM((rows - n0, cols), dtype), pltpu.VMEM((n0, cols), dtype)] * 2 + [
        pltpu.SemaphoreType.DMA((4,)), pltpu.SemaphoreType.DMA((6,)), pltpu.SemaphoreType.DMA((6,)),
        pltpu.SemaphoreType.DMA]


def _all_gather(arrs, name, row_pieces=None):
    n = len(arrs)
    pieces = [[None] if not row_pieces or not row_pieces[a] else list(row_pieces[a]) for a in range(n)]
    assert all(len(p) in (1, 2) for p in pieces)
    units = [(a, i) for a in range(n) for i in range(len(pieces[a]))]

    def body(*refs):
        ins = refs[:n]
        outs = refs[n:2 * n]
        send_sems, recv_sems, loc_sems = refs[2 * n:]
        x, y, c = _mesh_pos()
        me, sib = (x, y, c), (x, y, 1 - c)
        xn, yn, dg = [(px, py, c) for px, py in _other_chips(x, y)]

        def rows(ref, a, i):
            return ref if pieces[a][i] is None else ref.at[pl.ds(*pieces[a][i])]

        def copy(u, k, block, to, own=False):
            a, i = u
            px, py, pc = block
            dst = rows(outs[a].at[4 * px + 2 * py + pc], a, i)
            return _rcopy(rows(ins[a], a, i) if own else dst, dst, send_sems.at[a, k, i], recv_sems.at[a, k, i], to)

        started = []

        def start(cp):
            cp.start()
            started.append(cp)

        def landed_then_pass_on(u, k, block):
            copy(u, k, block, me).wait_recv()
            start(copy(u, 3 + k, block, sib))

        mine = [pltpu.make_async_copy(ins[a], outs[a].at[4 * x + 2 * y + c], loc_sems.at[a]) for a in range(n)]
        for cp in mine:
            cp.start()
        for u in units:
            start(copy(u, 0, me, sib, own=True))
        for a in range(n):
            if len(pieces[a]) == 2:
                for i, to, k in ((0, xn, 1), (1, yn, 2), (1, xn, 1), (0, yn, 2)):
                    start(copy((a, i), k, me, to, own=True))
            else:
                for to, k in ((xn, 1), (yn, 2), (dg, 3)):
                    start(copy((a, 0), k, me, to, own=True))
        for a in range(n):
            if len(pieces[a]) == 2:
                landed_then_pass_on((a, 0), 1, xn)
                start(copy((a, 0), 3, xn, yn))
                landed_then_pass_on((a, 1), 2, yn)
                start(copy((a, 1), 3, yn, xn))
                landed_then_pass_on((a, 1), 1, xn)
                landed_then_pass_on((a, 0), 2, yn)
                landed_then_pass_on((a, 0), 3, dg)
                landed_then_pass_on((a, 1), 3, dg)
            else:
                for block, k in ((xn, 1), (yn, 2), (dg, 3)):
                    landed_then_pass_on((a, 0), k, block)
        for u in units:
            copy(u, 0, sib, me).wait_recv()
            for k, (px, py, _) in ((4, xn), (5, yn), (6, dg)):
                copy(u, k, (px, py, 1 - c), me).wait_recv()
        for cp in started:
            cp.wait_send()
        for cp in mine:
            cp.wait()

    n_pc = max(len(p) for p in pieces)

    return pl.pallas_call(
        body, name=name,
        out_shape=tuple(jax.ShapeDtypeStruct((N_DEV,) + a.shape, a.dtype) for a in arrs),
        in_specs=[_ANY] * n,
        out_specs=tuple([_ANY] * n),
        scratch_shapes=[pltpu.SemaphoreType.DMA((n, 7, n_pc)), pltpu.SemaphoreType.DMA((n, 7, n_pc)),
                        pltpu.SemaphoreType.DMA((n,))],
    )(*arrs)


def _flatten_blocks_call(blocks):
    n, R, C = blocks.shape
    tc = C // 2

    def body(in_ref, out_ref):
        for p in range(n):
            out_ref[p * R:(p + 1) * R, :] = in_ref[p]

    return pl.pallas_call(
        body, name="flatten_w",
        grid=(C // tc,),
        in_specs=[pl.BlockSpec((n, R, tc), lambda i: (0, 0, i))],
        out_specs=pl.BlockSpec((n * R, tc), lambda i: (0, i)),
        out_shape=jax.ShapeDtypeStruct((n * R, C), blocks.dtype),
        compiler_params=_cparams(("arbitrary",)),
    )(blocks)


_PARTS_BANDS = 8


def _pair_sum_call(dmain, drank):
    D = dmain.shape[1]
    n = _PARTS_BANDS
    tc = D // n

    def body(dm_ref, dr_ref, sum_ref, laid, got, send_sems, recv_sems):
        x, y, c = _mesh_pos()

        def pushes(k):
            return [_rcopy(laid.at[k % 2, 2 * q + (1 - c)], got.at[k, q], send_sems.at[k, q], recv_sems.at[k, q],
                           (x, y, 1 - c)) for q in range(4)]

        def lay_out(k):
            for p in range(N_DEV):
                lo, hi = p * SHARD_COLS, (p + 1) * SHARD_COLS
                at = 0
                for src, a, b in ((dm_ref, lo, min(hi, RANK_COL)),
                                  (dr_ref, max(lo, RANK_COL) - RANK_COL, min(hi, RANK_COL + GLA_RANK) - RANK_COL),
                                  (dm_ref, max(lo, RANK_COL + GLA_RANK) - GLA_RANK, hi - GLA_RANK)):
                    if b > a:
                        laid[k % 2, p, at:at + (b - a), :] = src[a:b, :]
                        at += b - a

        for k in range(n + 1):
            @pl.when(pl.program_id(0) == k)
            def _(k=k):
                if k < n:
                    if k >= 2:
                        for cp in pushes(k - 2):
                            cp.wait_send()
                    lay_out(k)
                    for cp in pushes(k):
                        cp.start()
                if k >= 1:
                    for cp in pushes(k - 1):
                        cp.wait_recv()
                    for q in range(4):
                        sum_ref[q] = (laid[(k - 1) % 2, 2 * q + c].astype(F32)
                                      + got[k - 1, q].astype(F32)).astype(sum_ref.dtype)
                if k == n:
                    for k_open in range(max(0, n - 2), n):
                        for cp in pushes(k_open):
                            cp.wait_send()

    sems = pltpu.SemaphoreType.DMA((n, 4))
    return pl.pallas_call(
        body, name="pair_sum",
        grid=(n + 1,),
        in_specs=[pl.BlockSpec((dmain.shape[0], tc), lambda k: (0, jnp.minimum(k, n - 1))),
                  pl.BlockSpec((GLA_RANK, tc), lambda k: (0, jnp.minimum(k, n - 1)))],
        out_specs=pl.BlockSpec((4, SHARD_COLS, tc), lambda k: (0, 0, jnp.maximum(k - 1, 0))),
        out_shape=jax.ShapeDtypeStruct((4, SHARD_COLS, D), dmain.dtype),
        scratch_shapes=[pltpu.VMEM((2, N_DEV, SHARD_COLS, tc), dmain.dtype),
                        pltpu.VMEM((n, 4, SHARD_COLS, tc), dmain.dtype), sems, sems],
        compiler_params=_cparams(("arbitrary",)),
    )(dmain, drank)


def _group_row(g):
    return GLA_RANK * (g * (1024 // GLA_RANK) + (g >= RANK_COL // 1024))


def _proj_call(x, norm_g, wt, wr, wp_part):
    T, D = x.shape
    tm = min(1024, T)
    assert tm % TBLK == 0
    n_i = T // tm

    def f_slot(j):
        return ((j >= 2).astype(jnp.int32) + (j >= 6).astype(jnp.int32)
                + (j >= 7).astype(jnp.int32) + (j >= 8).astype(jnp.int32))

    def b_slot(j):
        return (j >= 3).astype(jnp.int32) + (j >= 4).astype(jnp.int32) + (j >= 5).astype(jnp.int32)

    def body(x_ref, g_ref, w_ref, wr_ref, wp_ref, pf_ref, pb_ref, rank_ref, ht_ref, wpall_ref,
             h_scr, send_sems, recv_sems, loc_sem):
        i = pl.program_id(0)
        j = pl.program_id(1)
        own, pairs = _push_copies(wp_ref, wpall_ref, send_sems, recv_sems, loc_sem, scatter=False)

        @pl.when((i == 0) & (j == 0))
        def _():
            _push_start(own, pairs)

        @pl.when(j == 0)
        def _():
            xv = x_ref[...]
            r = lax.rsqrt(jnp.mean(xv * xv, axis=-1, keepdims=True) + EPS)
            h = (xv * r) * g_ref[...]
            hb = _bf(h)
            h_scr[...] = hb
            for b in range(tm // TBLK):
                ht_ref[b] = _bf(h[b * TBLK:(b + 1) * TBLK].T)
            rank_ref[...] = _dot_nt(hb, wr_ref[...])

        is_b = (j == 1) | ((j >= 3) & (j <= 5))

        @pl.when(is_b)
        def _():
            pb_ref[...] = _bf(_dot_nt(h_scr[...], w_ref[...]))

        @pl.when(jnp.logical_not(is_b))
        def _():
            pf_ref[...] = _dot_nt(h_scr[...], w_ref[...])

        @pl.when((i == n_i - 1) & (j == N_GROUPS - 1))
        def _():
            _push_wait(own, pairs)

    return pl.pallas_call(
        body, name="proj",
        grid=(n_i, N_GROUPS),
        in_specs=[pl.BlockSpec((tm, D), lambda i, j: (i, 0)),
                  pl.BlockSpec((1, D), lambda i, j: (0, 0)),
                  pl.BlockSpec((pl.Element(1024), pl.Element(D)), lambda i, j: (_group_row(j), 0)),
                  pl.BlockSpec((128, D), lambda i, j: (0, 0)),
                  _ANY],
        out_specs=(pl.BlockSpec((None, tm, 1024), lambda i, j: (f_slot(j), i, 0)),
                   pl.BlockSpec((None, tm, 1024), lambda i, j: (b_slot(j), i, 0)),
                   pl.BlockSpec((tm, 128), lambda i, j: (i, 0)),
                   pl.BlockSpec((tm // TBLK, D, TBLK), lambda i, j: (i, 0, 0)),
                   _ANY),
        out_shape=(jax.ShapeDtypeStruct((5, T, 1024), F32),
                   jax.ShapeDtypeStruct((4, T, 1024), BF16),
                   jax.ShapeDtypeStruct((T, 128), F32),
                   jax.ShapeDtypeStruct((T // TBLK, D, TBLK), BF16),
                   jax.ShapeDtypeStruct((N_DEV,) + wp_part.shape, wp_part.dtype)),
        scratch_shapes=[pltpu.VMEM((tm, D), BF16)] + _PUSH_SEMS,
        compiler_params=_cparams(("arbitrary", "arbitrary")),
    )(x, norm_g, wt, wr, wp_part)


GLA_STEP_CHUNKS = 4


def _gla_same_chunk(rows):
    return (_iota2(rows, rows, 0) & -GLA_CHUNK) == (_iota2(rows, rows, 1) & -GLA_CHUNK)


def _gla_chunk_terms(la, q, k, n_c):
    C = GLA_CHUNK
    rows = n_c * C
    low = _gla_same_chunk(rows) & (_iota2(rows, rows, 0) >= _iota2(rows, rows, 1))
    b = _tri_left(_bf(low.astype(F32)), la)
    bl = [b[(c + 1) * C - 1:(c + 1) * C, :] for c in range(n_c)]
    bl_rows = jnp.concatenate([jnp.broadcast_to(bl[c], (C, b.shape[1])) for c in range(n_c)], axis=0)
    eb = jnp.exp(b)
    enb = jnp.exp(-b)
    ebl_b = jnp.exp(bl_rows - b)
    scale = GLA_HK ** -0.5
    qe = q * eb * scale
    ke = k * enb
    kd = k * ebl_b
    return bl, eb, enb, ebl_b, qe, ke, kd


def _gla_fwd_call(projf, projb, rank, wdec, bdec):
    T = projf.shape[1]
    C = GLA_CHUNK
    n_chunks = T // C
    n_c = GLA_STEP_CHUNKS
    R = n_c * C
    assert n_chunks % n_c == 0

    def body(qk_ref, v_ref, rank_ref, wd_ref, bd_ref, o_ref, st_ref, la_ref, st_scr):
        @pl.when(pl.program_id(0) == 0)
        def _():
            st_scr[...] = jnp.zeros_like(st_scr)

        dec = _dot(_bf(rank_ref[...]), _bf(wd_ref[...])) + bd_ref[...]
        la = (jnp.minimum(dec, 0.0) - _softplus_neg_abs(dec)) / GLA_TAU
        la_ref[...] = la
        mask = _gla_same_chunk(R) & (_iota2(R, R, 0) >= _iota2(R, R, 1))
        bl, _, _, _, qe, ke, kd = _gla_chunk_terms(la, qk_ref[:, :GLA_DK], qk_ref[:, GLA_DK:], n_c)
        qeb, keb, kdb = _bf(qe), _bf(ke), _bf(kd)
        ebl = [jnp.exp(bl[c]) for c in range(n_c)]
        heads = range(GLA_HEADS)
        ks = [slice(hh * GLA_HK, (hh + 1) * GLA_HK) for hh in heads]
        vs = [slice(hh * GLA_HV, (hh + 1) * GLA_HV) for hh in heads]
        rs = [slice(c * C, (c + 1) * C) for c in range(n_c)]
        p = [_bf(jnp.where(mask, _dot_nt(qeb[:, ks[hh]], keb[:, ks[hh]]), 0.0)) for hh in heads]
        upd = [[_dot_tn(v_ref[rs[c], vs[hh]], kdb[rs[c], ks[hh]]) for hh in heads] for c in range(n_c)]
        intra = [_dot(p[hh], v_ref[:, vs[hh]]) for hh in heads]
        st = [st_scr[hh] for hh in heads]
        for c in range(n_c):
            inter = [_dot_nt(qeb[rs[c], ks[hh]], _bf(st[hh])) for hh in heads]
            for hh in heads:
                st_ref[c, hh] = st[hh]
                o_ref[rs[c], vs[hh]] = intra[hh][rs[c]] + inter[hh]
            st = [st[hh] * ebl[c][:, ks[hh]] + upd[c][hh] for hh in heads]
        for hh in heads:
            st_scr[hh] = st[hh]

    return pl.pallas_call(
        body, name="gla_fwd",
        grid=(n_chunks // n_c,),
        in_specs=[pl.BlockSpec((None, R, 1024), lambda n: (0, n, 0)),
                  pl.BlockSpec((None, R, 1024), lambda n: (0, n, 0)),
                  pl.BlockSpec((R, 128), lambda n: (n, 0)),
                  pl.BlockSpec((128, GLA_DK), lambda n: (0, 0)),
                  pl.BlockSpec((1, GLA_DK), lambda n: (0, 0))],
        out_specs=(pl.BlockSpec((R, 1024), lambda n: (n, 0)),
                   pl.BlockSpec((n_c, GLA_HEADS, GLA_HV, GLA_HK), lambda n: (n, 0, 0, 0)),
                   pl.BlockSpec((R, GLA_DK), lambda n: (n, 0))),
        out_shape=(jax.ShapeDtypeStruct((T, 1024), F32),
                   jax.ShapeDtypeStruct((n_chunks, GLA_HEADS, GLA_HV, GLA_HK), F32),
                   jax.ShapeDtypeStruct((T, GLA_DK), F32)),
        scratch_shapes=[pltpu.VMEM((GLA_HEADS, GLA_HV, GLA_HK), F32)],
        compiler_params=_cparams(("arbitrary",)),
    )(projf, projb, rank, wdec, bdec)


def _gla_bwd_call(projf, projb, la, do_gla, st_all, rank, wdec):
    T = projf.shape[1]
    C = GLA_CHUNK
    n_chunks = T // C
    n_c = GLA_STEP_CHUNKS
    R = n_c * C
    assert n_chunks % n_c == 0
    last = n_chunks // n_c - 1

    def body(qk_ref, v_ref, la_ref, do_ref, st_ref, rank_ref, wd_ref,
             dqk_ref, dv_ref, drank_ref, dwd_ref, dbd_ref, dst_scr):
        @pl.when(pl.program_id(0) == 0)
        def _():
            dst_scr[...] = jnp.zeros_like(dst_scr)
            dwd_ref[...] = jnp.zeros_like(dwd_ref)
            dbd_ref[...] = jnp.zeros_like(dbd_ref)

        same = _gla_same_chunk(R)
        mask = same & (_iota2(R, R, 0) >= _iota2(R, R, 1))
        upp = _bf((same & (_iota2(R, R, 0) <= _iota2(R, R, 1))).astype(F32))
        scale = GLA_HK ** -0.5
        la = la_ref[...]
        bl, eb, enb, ebl_b, qe, ke, kd = _gla_chunk_terms(la, qk_ref[:, :GLA_DK], qk_ref[:, GLA_DK:], n_c)
        qeb, keb, kdb = _bf(qe), _bf(ke), _bf(kd)
        ebl = [jnp.exp(bl[c]) for c in range(n_c)]
        heads = range(GLA_HEADS)
        ks = [slice(hh * GLA_HK, (hh + 1) * GLA_HK) for hh in heads]
        vs = [slice(hh * GLA_HV, (hh + 1) * GLA_HV) for hh in heads]
        rs = [slice(c * C, (c + 1) * C) for c in range(n_c)]
        v = [v_ref[:, vs[hh]] for hh in heads]
        do = [_bf(do_ref[:, vs[hh]]) for hh in heads]
        p = [_bf(jnp.where(mask, _dot_nt(qeb[:, ks[hh]], keb[:, ks[hh]]), 0.0)) for hh in heads]
        dp = [_bf(jnp.where(mask, _dot_nt(do[hh], v[hh]), 0.0)) for hh in heads]
        dst_intra = [[_dot_tn(do[hh][rs[c]], qeb[rs[c], ks[hh]]) for hh in heads] for c in range(n_c)]
        dqe_inter = [[_dot(do[hh][rs[c]], _bf(st_ref[c, hh])) for hh in heads] for c in range(n_c)]
        dv_intra = [_dot_tn(p[hh], do[hh]) for hh in heads]
        dqe_intra = [_dot(dp[hh], keb[:, ks[hh]]) for hh in heads]
        dke = jnp.concatenate([_dot_tn(dp[hh], qeb[:, ks[hh]]) for hh in heads], axis=1)
        dstn = [dst_scr[hh] for hh in heads]
        dkd_c, dv_inter, debl = [None] * n_c, [None] * n_c, [None] * n_c
        for c in reversed(range(n_c)):
            dstnb = [_bf(dstn[hh]) for hh in heads]
            dkd_c[c] = jnp.concatenate([_dot(v[hh][rs[c]], dstnb[hh]) for hh in heads], axis=1)
            dv_inter[c] = [_dot_nt(kdb[rs[c], ks[hh]], dstnb[hh]) for hh in heads]
            debl[c] = jnp.concatenate(
                [jnp.sum(dstn[hh] * st_ref[c, hh], axis=0, keepdims=True) for hh in heads], axis=1)
            dstn = [dst_intra[c][hh] + dstn[hh] * ebl[c][:, ks[hh]] for hh in heads]
        for hh in heads:
            dst_scr[hh] = dstn[hh]
            dv_ref[:, vs[hh]] = _bf(dv_intra[hh] + jnp.concatenate([dv_inter[c][hh] for c in range(n_c)], axis=0))
        dqe = jnp.concatenate(
            [dqe_intra[hh] + jnp.concatenate([dqe_inter[c][hh] for c in range(n_c)], axis=0) for hh in heads], axis=1)
        dkd = jnp.concatenate(dkd_c, axis=0)
        dkd_kd = dkd * kd
        db = dqe * qe - dke * ke - dkd_kd
        dbl = jnp.concatenate(
            [jnp.broadcast_to(jnp.sum(dkd_kd[rs[c]], axis=0, keepdims=True) + ebl[c] * debl[c], (C, GLA_DK))
             for c in range(n_c)], axis=0)
        dla = _tri_left(upp, db) + dbl
        dqk_ref[:, :GLA_DK] = _bf(dqe * eb * scale)
        dqk_ref[:, GLA_DK:] = _bf(dke * enb + dkd * ebl_b)
        ddec = dla * (1.0 / GLA_TAU) * (1.0 - jnp.exp(GLA_TAU * la))
        ddecb = _bf(ddec)
        drank_ref[...] = _bf(_dot_nt(ddecb, _bf(wd_ref[...])))
        dwd_ref[...] += _dot_tn(_bf(rank_ref[...]), ddecb)
        dbd_ref[...] += jnp.sum(ddec, axis=0, keepdims=True)

    return pl.pallas_call(
        body, name="gla_bwd",
        grid=(n_chunks // n_c,),
        in_specs=[pl.BlockSpec((None, R, 1024), lambda n: (0, last - n, 0)),
                  pl.BlockSpec((None, R, 1024), lambda n: (0, last - n, 0)),
                  pl.BlockSpec((R, GLA_DK), lambda n: (last - n, 0)),
                  pl.BlockSpec((R, 1024), lambda n: (last - n, 0)),
                  pl.BlockSpec((n_c, GLA_HEADS, GLA_HV, GLA_HK), lambda n: (last - n, 0, 0, 0)),
                  pl.BlockSpec((R, 128), lambda n: (last - n, 0)),
                  pl.BlockSpec((128, GLA_DK), lambda n: (0, 0))],
        out_specs=(pl.BlockSpec((R, 1024), lambda n: (last - n, 0)),
                   pl.BlockSpec((R, 1024), lambda n: (last - n, 0)),
                   pl.BlockSpec((R, 128), lambda n: (last - n, 0)),
                   pl.BlockSpec((128, GLA_DK), lambda n: (0, 0)),
                   pl.BlockSpec((1, GLA_DK), lambda n: (0, 0))),
        out_shape=(jax.ShapeDtypeStruct((T, 1024), BF16),
                   jax.ShapeDtypeStruct((T, 1024), BF16),
                   jax.ShapeDtypeStruct((T, 128), BF16),
                   jax.ShapeDtypeStruct((128, GLA_DK), F32),
                   jax.ShapeDtypeStruct((1, GLA_DK), F32)),
        scratch_shapes=[pltpu.VMEM((GLA_HEADS, GLA_HV, GLA_HK), F32)],
        compiler_params=_cparams(("arbitrary",)),
    )(projf, projb, la, do_gla, st_all, rank, wdec)


def _sb_logs(z):
    lsz = jnp.minimum(z, 0.0) - _softplus_neg_abs(z)
    return lsz, lsz - z


SB_HG_FWD = 8
SB_HG_BWD = 4
SB_QUERIES = 256
SB_KEYS = 256
SB_DEAD = -105.0


def _sb_fwd_call(projb, wp_shard):
    T = projb.shape[1]
    B = min(SB_QUERIES, T)
    HG = SB_HG_FWD
    W = HG * SB_HD
    scale = 1.0 / math.sqrt(SB_HD)
    KB = min(SB_KEYS, T)
    n_h, n_i = SB_HEADS // HG, T // B

    def body(q_ref, k_ref, v_ref, wp_ref, o_ref, wpall_ref, cb_scr, send_sems, recv_sems, loc_sem):
        i = pl.program_id(1)
        own, pairs = _push_copies(wp_ref, wpall_ref, send_sems, recv_sems, loc_sem, scatter=False)

        @pl.when((pl.program_id(0) == 0) & (i == 0))
        def _():
            _push_start(own, pairs)

        rows = HG * B
        after = (_iota2(KB, KB, 0) > _iota2(KB, KB, 1)).astype(F32)
        tri = _bf(jnp.concatenate([after, jnp.ones((KB, KB), F32)], axis=1))
        o_ref[...] = jnp.zeros_like(o_ref)
        cb_scr[...] = jnp.zeros_like(cb_scr)

        def block(jp, masked):
            off = pl.multiple_of(jp * KB, KB)
            z = jnp.concatenate(
                [_dot_nt(q_ref[:, hh * SB_HD:(hh + 1) * SB_HD], k_ref[pl.ds(off, KB), hh * SB_HD:(hh + 1) * SB_HD])
                 for hh in range(HG)], axis=0) * scale
            lsz, l1m = _sb_logs(z)
            if masked:
                strict = (jp * KB + _iota2(rows, KB, 1)) < (i * B + (_iota2(rows, KB, 0) & (B - 1)))
                l1m = jnp.where(strict, l1m, 0.0)
            r = _tri2_right(l1m, tri)
            cb = cb_scr[...]
            a = jnp.exp(lsz + cb + r[:, :KB])
            if masked:
                a = jnp.where(strict, a, 0.0)
            cb_scr[...] = cb + r[:, KB:]
            ab = _bf(a)
            for hh in range(HG):
                cs = slice(hh * SB_HD, (hh + 1) * SB_HD)
                o_ref[:, cs] += _dot(ab[hh * B:(hh + 1) * B, :], v_ref[pl.ds(off, KB), cs])

        jp0 = (i * B) // KB
        block(jp0, True)

        def live(state):
            jj, dead = state
            return (jj <= jp0) & jnp.logical_not(dead)

        def step(state):
            jj, _ = state
            block(jp0 - jj, False)
            return jj + 1, jnp.max(cb_scr[:, :SB_HD]) < SB_DEAD

        lax.while_loop(live, step, (jnp.int32(1), jnp.max(cb_scr[:, :SB_HD]) < SB_DEAD))

        @pl.when((pl.program_id(0) == n_h - 1) & (i == n_i - 1))
        def _():
            _push_wait(own, pairs)

    return pl.pallas_call(
        body, name="sb_fwd",
        grid=(n_h, n_i),
        in_specs=[pl.BlockSpec((None, B, W), lambda h, i: (1, i, h)),
                  pl.BlockSpec((None, T, W), lambda h, i: (2, 0, h)),
                  pl.BlockSpec((None, T, W), lambda h, i: (3, 0, h)),
                  _ANY],
        out_specs=(pl.BlockSpec((B, W), lambda h, i: (i, h)), _ANY),
        out_shape=(jax.ShapeDtypeStruct((T, 1024), F32),
                   jax.ShapeDtypeStruct((N_DEV,) + wp_shard.shape, wp_shard.dtype)),
        scratch_shapes=[pltpu.VMEM((HG * B, KB), F32)] + _PUSH_SEMS,
        compiler_params=_cparams(("arbitrary", "arbitrary")),
    )(projb, projb, projb, wp_shard)


def _sb_bwd_call(projb, do_sb, g_p):
    T = projb.shape[1]
    B = min(SB_QUERIES, T)
    nb = T // B
    HG = SB_HG_BWD
    W = HG * SB_HD
    WQ = HG * B
    KB = min(SB_KEYS, T)
    nkb = T // KB
    n_h = SB_HEADS // HG
    scale = 1.0 / math.sqrt(SB_HD)

    def body(q_ref, k_ref, v_ref, do_ref, gp_ref, dq_ref, dk_ref, dv_ref, rp_ref,
             dk_scr, dv_scr, kt_scr, beta_scr, g_scr, dqt_scr, send_sems, recv_sems, loc_sem):
        i = pl.program_id(1)
        own, pairs = _push_copies(gp_ref, rp_ref, send_sems, recv_sems, loc_sem, scatter=True)

        @pl.when((pl.program_id(0) == 0) & (i == 0))
        def _():
            _push_start(own, pairs)

        @pl.when(i == 0)
        def _():
            dk_scr[...] = jnp.zeros_like(dk_scr)
            dv_scr[...] = jnp.zeros_like(dv_scr)
            for hh in range(HG):
                for jb in range(nkb):
                    kt_scr[hh, jb] = _bf(
                        k_ref[jb * KB:(jb + 1) * KB, hh * SB_HD:(hh + 1) * SB_HD].astype(F32).T)

        dqt_scr[...] = jnp.zeros_like(dqt_scr)
        later = _bf((_iota2(KB, KB, 1) > _iota2(KB, KB, 0)).astype(F32))
        earlier = _bf((_iota2(KB, KB, 1) < _iota2(KB, KB, 0)).astype(F32))
        dob = _bf(do_ref[...])
        jp0 = (i * B) // KB

        def strict_mask():
            return (jp0 * KB + _iota2(KB, WQ, 0)) < (i * B + (_iota2(KB, WQ, 1) & (B - 1)))

        def heads(fn):
            return [fn(slice(hh * SB_HD, (hh + 1) * SB_HD)) for hh in range(HG)]

        def pass1(jp, cb, masked):
            off = pl.multiple_of(jp * KB, KB)
            z = jnp.concatenate(heads(lambda cs: _dot_nt(k_ref[pl.ds(off, KB), cs], q_ref[:, cs])), axis=1) * scale
            da = jnp.concatenate(heads(lambda cs: _dot_nt(v_ref[pl.ds(off, KB), cs], dob[:, cs])), axis=1)
            lsz, l1m = _sb_logs(z)
            if masked:
                strict = strict_mask()
                l1m = jnp.where(strict, l1m, 0.0)
            a = jnp.exp(lsz + cb + _tri2_left(later, l1m))
            if masked:
                a = jnp.where(strict, a, 0.0)
            g_scr[jp] = a * da
            beta_scr[jp] = jnp.exp(lsz)
            ab = _bf(a)
            for hh in range(HG):
                cs = slice(hh * SB_HD, (hh + 1) * SB_HD)
                dv_scr[pl.ds(off, KB), cs] += _dot(ab[:, hh * B:(hh + 1) * B], dob[:, cs])
            return cb + jnp.sum(l1m, axis=0, keepdims=True)

        zero = jnp.zeros((1, WQ), F32)
        cb = pass1(jp0, zero, True)

        def live(state):
            jj, _, dead = state
            return (jj <= jp0) & jnp.logical_not(dead)

        def step(state):
            jj, cr, _ = state
            cr = pass1(jp0 - jj, cr, False)
            return jj + 1, cr, jnp.max(cr) < SB_DEAD

        n_done, _, _ = lax.while_loop(live, step, (jnp.int32(1), cb, jnp.max(cb) < SB_DEAD))
        jp_first = jp0 - (n_done - 1)

        def pass2(jp, cg, masked):
            off = pl.multiple_of(jp * KB, KB)
            g = g_scr[jp]
            beta = beta_scr[jp]
            dz = g * (1.0 - beta) - beta * (cg + _tri2_left(earlier, g))
            if masked:
                dz = jnp.where(strict_mask(), dz, 0.0)
            dzb = _bf(dz * scale)
            for hh in range(HG):
                cs = slice(hh * SB_HD, (hh + 1) * SB_HD)
                dk_scr[pl.ds(off, KB), cs] += _dot(dzb[:, hh * B:(hh + 1) * B], q_ref[:, cs])
                dqt_scr[hh] += _dot(kt_scr[hh, jp], dzb[:, hh * B:(hh + 1) * B])
            return cg + jnp.sum(g, axis=0, keepdims=True)

        cg = lax.fori_loop(jp_first, jp0, lambda jp, cr: pass2(jp, cr, False), zero)
        pass2(jp0, cg, True)
        for hh in range(HG):
            dq_ref[:, hh * SB_HD:(hh + 1) * SB_HD] = _bf(dqt_scr[hh].T)

        @pl.when(i == nb - 1)
        def _():
            dk_ref[...] = _bf(dk_scr[...])
            dv_ref[...] = _bf(dv_scr[...])

        @pl.when((pl.program_id(0) == n_h - 1) & (i == nb - 1))
        def _():
            _push_wait(own, pairs)

    return pl.pallas_call(
        body, name="sb_bwd",
        grid=(n_h, nb),
        in_specs=[pl.BlockSpec((None, B, W), lambda h, i: (1, i, h)),
                  pl.BlockSpec((None, T, W), lambda h, i: (2, 0, h)),
                  pl.BlockSpec((None, T, W), lambda h, i: (3, 0, h)),
                  pl.BlockSpec((B, W), lambda h, i: (i, h)),
                  _ANY],
        out_specs=(pl.BlockSpec((B, W), lambda h, i: (i, h)),
                   pl.BlockSpec((T, W), lambda h, i: (0, h)),
                   pl.BlockSpec((T, W), lambda h, i: (0, h)),
                   _ANY),
        out_shape=(jax.ShapeDtypeStruct((T, 1024), BF16),
                   jax.ShapeDtypeStruct((T, 1024), BF16),
                   jax.ShapeDtypeStruct((T, 1024), BF16),
                   jax.ShapeDtypeStruct(g_p.shape, g_p.dtype)),
        scratch_shapes=[pltpu.VMEM((T, W), F32), pltpu.VMEM((T, W), F32),
                        pltpu.VMEM((HG, nkb, SB_HD, KB), BF16),
                        pltpu.VMEM((nkb, KB, WQ), F32), pltpu.VMEM((nkb, KB, WQ), F32),
                        pltpu.VMEM((HG, SB_HD, B), F32)] + _PUSH_SEMS,
        compiler_params=_cparams(("arbitrary", "arbitrary")),
    )(projb, projb, projb, do_sb, g_p)


def _mid_call(o_gla, o_sb, projf, x, target, wpa, wpb, wo, gla_g, b_gate, final_g):
    T, D = x.shape
    tm = min(TBLK, T)

    def body(og_ref, ggate_ref, osb_ref, sgate_ref, ma_ref, mb_ref, x_ref, tgt_ref,
             wpa_ref, wpb_ref, wo_ref, glag_ref, bg_ref, fg_ref,
             dx2_ref, dogla_ref, dosb_ref, dggate_ref, dsgate_ref, dm_ref,
             mt_ref, ogt_ref, obt_ref, dx2b_ref, dya_ref, dyb_ref,
             dfg_ref, dbg_ref, dglag_ref, loss_ref):
        @pl.when(pl.program_id(0) == 0)
        def _():
            dfg_ref[...] = jnp.zeros_like(dfg_ref)
            dbg_ref[...] = jnp.zeros_like(dbg_ref)
            dglag_ref[...] = jnp.zeros_like(dglag_ref)
            loss_ref[...] = jnp.zeros_like(loss_ref)

        glag = glag_ref[...]
        ggate = ggate_ref[...]
        sg = _sigmoid(ggate)
        silu_g = ggate * sg
        ohat, rinv, nrm = [], [], []
        for hh in range(GLA_HEADS):
            oh = og_ref[:, hh * GLA_HV:(hh + 1) * GLA_HV]
            r = lax.rsqrt(jnp.mean(oh * oh, axis=-1, keepdims=True) + EPS)
            ohat.append(oh * r)
            rinv.append(r)
            nrm.append(ohat[-1] * glag)
        n_all = jnp.concatenate(nrm, axis=1)
        og = n_all * silu_g
        ogb = _bf(og)
        ya = _dot(ogb, wpa_ref[...])
        sgate = sgate_ref[...]
        ss = _sigmoid(sgate)
        silu_s = sgate * ss
        osb = osb_ref[...]
        ob = osb * silu_s
        obb = _bf(ob)
        yb = _dot(obb, wpb_ref[...])
        ga = _sigmoid(ma_ref[...] + bg_ref[:, :D])
        gb = _sigmoid(mb_ref[...] + bg_ref[:, D:])
        merged = ga * ya + gb * yb
        mgb = _bf(merged)
        x2 = x_ref[...] + _dot(mgb, wo_ref[...])
        r2 = lax.rsqrt(jnp.mean(x2 * x2, axis=-1, keepdims=True) + EPS)
        xh2 = x2 * r2
        fg = fg_ref[...]
        err = xh2 * fg - tgt_ref[...]
        loss_ref[...] += jnp.broadcast_to(
            0.5 * jnp.sum(jnp.mean(err * err, axis=-1, keepdims=True), axis=0, keepdims=True), (1, 128))
        dy = err * (1.0 / D)
        dfg_ref[...] += jnp.sum(dy * xh2, axis=0, keepdims=True)
        dxh = dy * fg
        dx2 = r2 * (dxh - xh2 * jnp.mean(dxh * xh2, axis=-1, keepdims=True))
        dx2_ref[...] = dx2
        dx2b = _bf(dx2)
        dx2b_ref[...] = dx2b
        dmerged = _dot_nt(dx2b, wo_ref[...])
        dya = dmerged * ga
        dyb = dmerged * gb
        dma = dmerged * ya * ga * (1.0 - ga)
        dmb = dmerged * yb * gb * (1.0 - gb)
        dm_ref[:, :D] = _bf(dma)
        dm_ref[:, D:] = _bf(dmb)
        dbg_ref[:, :D] += jnp.sum(dma, axis=0, keepdims=True)
        dbg_ref[:, D:] += jnp.sum(dmb, axis=0, keepdims=True)
        dyab = _bf(dya)
        dybb = _bf(dyb)
        dya_ref[...] = dyab
        dyb_ref[...] = dybb
        dog = _dot_nt(dyab, wpa_ref[...])
        dob = _dot_nt(dybb, wpb_ref[...])
        dosb_ref[...] = dob * silu_s
        dsgate_ref[...] = _bf(dob * osb * (ss * (1.0 + sgate * (1.0 - ss))))
        dn = dog * silu_g
        dggate_ref[...] = _bf(dog * n_all * (sg * (1.0 + ggate * (1.0 - sg))))
        dglag = jnp.zeros((1, GLA_HV), F32)
        for hh in range(GLA_HEADS):
            dnh = dn[:, hh * GLA_HV:(hh + 1) * GLA_HV]
            dglag = dglag + jnp.sum(dnh * ohat[hh], axis=0, keepdims=True)
            dohat = dnh * glag
            dogla_ref[:, hh * GLA_HV:(hh + 1) * GLA_HV] = rinv[hh] * (
                dohat - ohat[hh] * jnp.mean(dohat * ohat[hh], axis=-1, keepdims=True))
        dglag_ref[...] += dglag
        mt_ref[...] = _bf(merged.T)
        ogt_ref[...] = _bf(og.T)
        obt_ref[...] = _bf(ob.T)

    row = lambda i: (i, 0)
    const = lambda i: (0, 0)
    tile = pl.BlockSpec((tm, D), row)
    tile_t = pl.BlockSpec((None, D, tm), lambda i: (i, 0, 0))
    wspec = pl.BlockSpec((D, D), const)
    return pl.pallas_call(
        body, name="mid",
        grid=(T // tm,),
        in_specs=[tile,
                  pl.BlockSpec((None, tm, D), lambda i: (1, i, 0)),
                  tile,
                  pl.BlockSpec((None, tm, D), lambda i: (2, i, 0)),
                  pl.BlockSpec((None, tm, D), lambda i: (3, i, 0)),
                  pl.BlockSpec((None, tm, D), lambda i: (4, i, 0)),
                  tile, tile, wspec, wspec, wspec,
                  pl.BlockSpec((1, GLA_HV), const),
                  pl.BlockSpec((1, 2 * D), const),
                  pl.BlockSpec((1, D), const)],
        out_specs=(tile, tile, tile, tile, tile,
                   pl.BlockSpec((tm, 2 * D), row),
                   tile_t, tile_t, tile_t, tile, tile, tile,
                   pl.BlockSpec((1, D), const),
                   pl.BlockSpec((1, 2 * D), const),
                   pl.BlockSpec((1, GLA_HV), const),
                   pl.BlockSpec((1, 128), const)),
        out_shape=(jax.ShapeDtypeStruct((T, D), F32),
                   jax.ShapeDtypeStruct((T, D), F32),
                   jax.ShapeDtypeStruct((T, D), F32),
                   jax.ShapeDtypeStruct((T, D), BF16),
                   jax.ShapeDtypeStruct((T, D), BF16),
                   jax.ShapeDtypeStruct((T, 2 * D), BF16),
                   jax.ShapeDtypeStruct((T // tm, D, tm), BF16),
                   jax.ShapeDtypeStruct((T // tm, D, tm), BF16),
                   jax.ShapeDtypeStruct((T // tm, D, tm), BF16),
                   jax.ShapeDtypeStruct((T, D), BF16),
                   jax.ShapeDtypeStruct((T, D), BF16),
                   jax.ShapeDtypeStruct((T, D), BF16),
                   jax.ShapeDtypeStruct((1, D), F32),
                   jax.ShapeDtypeStruct((1, 2 * D), F32),
                   jax.ShapeDtypeStruct((1, GLA_HV), F32),
                   jax.ShapeDtypeStruct((1, 128), F32)),
        compiler_params=_cparams(("arbitrary",)),
    )(o_gla, projf, o_sb, projf, projf, projf, x, target, wpa, wpb, wo, gla_g, b_gate, final_g)


def _dh_call(pieces, dmlog, drank, wt, wr, x, dx2, norm_g, s_in, small):
    T, D = x.shape
    tm = min(256, T)
    npc = len(pieces)
    n_main = N_GROUPS * 1024
    n_i = T // tm
    i_forward = 5 * n_i // 8

    def body(*refs):
        pcs = refs[:npc]
        (dm_ref, dr_ref, w_hbm, wr_ref, x_ref, dx2_ref, g_ref, sin_ref, small_ref,
         gx_ref, rin_ref, relayed_ref, rsmall_ref,
         w_scr, sems, dg_ref, small_mine, small_send, small_recv, small_loc, *exchange_scratch) = refs[npc:]
        start, forward, finish = _chip_reduce_steps(sin_ref, rin_ref, relayed_ref, *exchange_scratch)

        @pl.when(pl.program_id(0) == 0)
        def _():
            start()
            lo = pltpu.make_async_copy(w_hbm.at[pl.ds(0, RANK_COL)], w_scr.at[pl.ds(0, RANK_COL)], sems.at[0])
            hi = pltpu.make_async_copy(w_hbm.at[pl.ds(RANK_COL + GLA_RANK, n_main - RANK_COL)],
                                       w_scr.at[pl.ds(RANK_COL, n_main - RANK_COL)], sems.at[1])
            lo.start()
            hi.start()
            dg_ref[...] = jnp.zeros_like(dg_ref)
            lo.wait()
            hi.wait()

        @pl.when(pl.program_id(0) == i_forward)
        def _():
            forward()

        def w_group(g):
            return w_scr[g * 1024:(g + 1) * 1024, :]

        dr = dr_ref[...]
        dh = _dot(dr, wr_ref[...])
        for g in range(npc):
            dh = dh + _dot(pcs[g][...], w_group(g))
        dh = dh + _dot(dm_ref[:, :D], w_group(npc))
        dh = dh + _dot(dm_ref[:, D:], w_group(npc + 1))
        xv = x_ref[...]
        r = lax.rsqrt(jnp.mean(xv * xv, axis=-1, keepdims=True) + EPS)
        xhat = xv * r
        g = g_ref[...]
        dg_ref[...] += jnp.sum(dh * xhat, axis=0, keepdims=True)
        dxhat = dh * g
        gx_ref[...] = r * (dxhat - xhat * jnp.mean(dxhat * xhat, axis=-1, keepdims=True)) + dx2_ref[...]

        @pl.when(pl.program_id(0) == n_i - 1)
        def _():
            small_mine[...] = small_ref[...]
            small_mine[:, _SM_NORM:_SM_NORM + D] = dg_ref[...]
            own, pairs = _push_copies(small_mine, rsmall_ref, small_send, small_recv, small_loc, scatter=False)
            _push_start(own, pairs)
            finish()
            _push_wait(own, pairs)

    row = lambda i: (i, 0)
    const = lambda i: (0, 0)
    tile = pl.BlockSpec((tm, D), row)
    part = s_in.shape[1:]
    return pl.pallas_call(
        body, name="dh",
        grid=(n_i,),
        in_specs=[tile] * npc + [
            pl.BlockSpec((tm, 2 * D), row),
            pl.BlockSpec((tm, 128), row),
            _ANY,
            pl.BlockSpec((128, D), const),
            tile, tile,
            pl.BlockSpec((1, D), const),
            _ANY,
            pl.BlockSpec(small.shape, const)],
        out_specs=(tile, _ANY, _ANY, _ANY),
        out_shape=(jax.ShapeDtypeStruct((T, D), F32),
                   jax.ShapeDtypeStruct((3,) + part, s_in.dtype),
                   jax.ShapeDtypeStruct(part, s_in.dtype),
                   jax.ShapeDtypeStruct((N_DEV,) + small.shape, small.dtype)),
        scratch_shapes=[pltpu.VMEM((n_main, D), BF16), pltpu.SemaphoreType.DMA((2,)),
                        pltpu.VMEM((1, D), F32), pltpu.VMEM(small.shape, small.dtype)]
        + _PUSH_SEMS + _chip_reduce_scratch(*part, s_in.dtype),
        compiler_params=_cparams(("arbitrary",)),
    )(*pieces, dmlog, drank, wt, wr, x, dx2, norm_g, s_in, small)


def _wgrad_call(lhs_list, lhs_of_group, rhs_list, rhs_of_group, n_transposed, name, narrow=None):
    n_groups = len(rhs_of_group)
    n_tb, D, tb = lhs_list[0].shape
    T = n_tb * tb
    per = min(4, n_tb)
    tk = per * tb
    nk = T // tk
    nl = len(lhs_list)
    extra = [] if narrow is None else [narrow]

    def tokens_side_by_side(lref):
        return jnp.concatenate([lref[b] for b in range(per)], axis=1)

    def body(*refs):
        lhs = refs[:nl]
        rhs = refs[nl:nl + n_groups]
        rest = refs[nl + n_groups:]
        g = pl.program_id(0)
        i = pl.program_id(1)
        if narrow is None:
            out_ref, acc = rest
        else:
            narrow_ref, out_ref, narrow_out, acc, narrow_acc = rest

            @pl.when((g == 0) & (i == 0))
            def _():
                narrow_acc[...] = jnp.zeros_like(narrow_acc)

            @pl.when(g == 0)
            def _():
                narrow_acc[...] += _dot(tokens_side_by_side(lhs[lhs_of_group[0]]), narrow_ref[...])

            @pl.when((g == 0) & (i == nk - 1))
            def _():
                narrow_out[...] = _bf(narrow_acc[...].T)

        @pl.when(i == 0)
        def _():
            acc[...] = jnp.zeros_like(acc)

        for p in range(n_groups):
            @pl.when(g == p)
            def _(p=p):
                acc[...] += _dot(tokens_side_by_side(lhs[lhs_of_group[p]]), rhs[p][...])

        @pl.when((i == nk - 1) & (g < n_transposed))
        def _():
            out_ref[...] = _bf(acc[...].T)

        @pl.when((i == nk - 1) & (g >= n_transposed))
        def _():
            out_ref[...] = _bf(acc[...])

    def lhs_spec(a):
        groups = [g for g in range(n_groups) if lhs_of_group[g] == a]
        lo, hi = min(groups), max(groups)
        assert groups == list(range(lo, hi + 1))
        return pl.BlockSpec((per, D, tb), lambda g, i: (jnp.where((g >= lo) & (g <= hi), i, 0), 0, 0))

    def rhs_spec(p):
        cb = rhs_of_group[p][1]
        return pl.BlockSpec((tk, 1024), lambda g, i: (jnp.where(g == p, i, 0), cb))

    res = pl.pallas_call(
        body, name=name,
        grid=(n_groups, nk),
        in_specs=[lhs_spec(a) for a in range(nl)] + [rhs_spec(p) for p in range(n_groups)]
        + [pl.BlockSpec((tk, 128), lambda g, i: (jnp.where(g == 0, i, 0), 0)) for _ in extra],
        out_specs=[pl.BlockSpec((None, D, 1024), lambda g, i: (g, 0, 0))]
        + [pl.BlockSpec((128, D), lambda g, i: (0, 0)) for _ in extra],
        out_shape=[jax.ShapeDtypeStruct((n_groups, D, 1024), BF16)]
        + [jax.ShapeDtypeStruct((128, D), BF16) for _ in extra],
        scratch_shapes=[pltpu.VMEM((D, 1024), F32)] + [pltpu.VMEM((D, 128), F32) for _ in extra],
        compiler_params=_cparams(("arbitrary", "arbitrary")),
    )(*lhs_list, *[rhs_list[rhs_of_group[p][0]] for p in range(n_groups)], *extra)
    return res[0] if narrow is None else res


def _adamw_math(parts, w, m, v):
    g = parts[0].astype(F32)
    for p in parts[1:]:
        g = g + p.astype(F32)
    mm = ADAM_B1 * m + (1.0 - ADAM_B1) * g
    vv = ADAM_B2 * v + (1.0 - ADAM_B2) * (g * g)
    m_hat = mm / (1.0 - ADAM_B1 ** ADAM_STEP)
    v_hat = vv / (1.0 - ADAM_B2 ** ADAM_STEP)
    return g, -ADAM_LR * (m_hat / (jnp.sqrt(v_hat) + ADAM_EPS) + ADAM_WD * w), mm, vv


def _part_order(n_parts):
    return [n_parts - 1] + list(range(n_parts - 1))


def _adamw_call(parts, w, m, v, name):
    R, C = w.shape
    n_parts = parts.shape[0]
    (tr, tc), grid, idx = _tiling_2d(R, C, 512)

    def body(p_ref, w_ref, m_ref, v_ref, g_ref, d_ref, nm_ref, nv_ref):
        g_ref[...], d_ref[...], nm_ref[...], nv_ref[...] = _adamw_math(
            [p_ref[k] for k in _part_order(n_parts)], w_ref[...], m_ref[...], v_ref[...])

    blk = pl.BlockSpec((tr, tc), idx)
    sds = jax.ShapeDtypeStruct((R, C), F32)
    return pl.pallas_call(
        body, name=name,
        grid=grid,
        in_specs=[pl.BlockSpec((n_parts, tr, tc), lambda i: (0,) + idx(i)), blk, blk, blk],
        out_specs=(blk, blk, blk, blk),
        out_shape=(sds, sds, sds, sds),
        compiler_params=_cparams(("arbitrary",)),
    )(parts, w, m, v)


def _adamw_rows_call(parts, ws, ms, vs, name):
    n = len(ws)
    R, C = ws[0].shape
    n_parts = parts.shape[0]

    def body(*refs):
        p_ref = refs[0]
        w_refs, m_refs, v_refs = refs[1:1 + n], refs[1 + n:1 + 2 * n], refs[1 + 2 * n:1 + 3 * n]
        outs = refs[1 + 3 * n:]
        for k in range(n):
            @pl.when(pl.program_id(0) == k)
            def _(k=k):
                res = _adamw_math([p_ref[j] for j in _part_order(n_parts)],
                                  w_refs[k][...], m_refs[k][...], v_refs[k][...])
                for o_ref, val in zip(outs[4 * k:4 * k + 4], res):
                    o_ref[...] = val

    whole = pl.BlockSpec((R, C), lambda k: (0, 0))
    sds = jax.ShapeDtypeStruct((R, C), F32)
    res = pl.pallas_call(
        body, name=name,
        grid=(n,),
        in_specs=[pl.BlockSpec((n_parts, R, C), lambda k: (0, k, 0))] + [whole] * (3 * n),
        out_specs=tuple([whole] * (4 * n)),
        out_shape=tuple([sds] * (4 * n)),
        compiler_params=_cparams(("arbitrary",)),
    )(parts, *ws, *ms, *vs)
    return [res[4 * k:4 * k + 4] for k in range(n)]


def _adamw_lanes_call(parts, offsets, ws, ms, vs, name):
    n = len(ws)
    n_parts = parts.shape[0]

    def body(*refs):
        p_ref = refs[0]
        w_refs, m_refs, v_refs = refs[1:1 + n], refs[1 + n:1 + 2 * n], refs[1 + 2 * n:1 + 3 * n]
        outs = refs[1 + 3 * n:]
        for k in range(n):
            lanes = slice(offsets[k], offsets[k] + ws[k].shape[1])
            res = _adamw_math([p_ref[j, :, lanes] for j in _part_order(n_parts)],
                              w_refs[k][...], m_refs[k][...], v_refs[k][...])
            for o_ref, val in zip(outs[4 * k:4 * k + 4], res):
                o_ref[...] = val

    res = pl.pallas_call(
        body, name=name,
        out_shape=tuple(jax.ShapeDtypeStruct(ws[k].shape, F32) for k in range(n) for _ in range(4)),
        compiler_params=_cparams(),
    )(parts, *ws, *ms, *vs)
    return [res[4 * k:4 * k + 4] for k in range(n)]


def _local_step(x, target, wt, wr, wdec, bdec, wp_shard, norm_g, gla_g, b_gate, final_g):
    D = x.shape[1]
    half = wp_shard.shape[1] // 2
    projf, projb, rank, ht, wp_lo = _proj_call(x, norm_g, wt, wr, wp_shard[:, :half])
    o_gla, st_all, la = _gla_fwd_call(projf, projb, rank, wdec, bdec)
    o_sb, wp_hi = _sb_fwd_call(projb, wp_shard[:, half:])
    wp_full = jnp.concatenate([wp_lo, wp_hi], axis=2).transpose(1, 0, 2, 3).reshape(3, D, D)
    (dx2, do_gla, do_sb, dggate, dsgate, dmlog, mt, ogt, obt, dx2b, dya, dyb,
     dfinal_g, db_gate, dgla_g, loss) = _mid_call(o_gla, o_sb, projf, x, target, wp_full[0], wp_full[1],
                                                 wp_full[2], gla_g, b_gate, final_g)
    dw_p = _wgrad_call([ogt, obt, mt], [0, 1, 2], [dya, dyb, dx2b], [(0, 0), (1, 0), (2, 0)], 0, "wgrad_p")
    g_p = dw_p.reshape(3, N_DEV, D // N_DEV, D).transpose(1, 0, 2, 3).reshape(N_DEV, 3 * (D // N_DEV), D)
    dqk, dgv, drank, dwdec, dbdec = _gla_bwd_call(projf, projb, la, do_gla, st_all, rank, wdec)
    dsq, dsk, dsv, r_p = _sb_bwd_call(projb, do_sb, g_p)
    pieces = [dqk, dgv, dggate, dsq, dsk, dsv, dsgate]
    rhs_of_group = [(g, 0) for g in range(7)] + [(7, 0), (7, 1)]
    dw_in, dwr = _wgrad_call([ht], [0] * N_GROUPS, pieces + [dmlog], rhs_of_group, N_GROUPS, "wgrad_in",
                             narrow=drank)
    s_in = _pair_sum_call(dw_in.reshape(N_GROUPS * 1024, D), dwr)
    small = jnp.concatenate([
        jnp.zeros((D,), F32), dbdec.reshape(-1), dgla_g.reshape(-1), db_gate.reshape(-1), dfinal_g.reshape(-1),
        loss.reshape(-1), dwdec[:GLA_RANK].reshape(-1)]).reshape(1, _SM_LEN)
    grad_x, r_in, _, r_small = _dh_call(pieces, dmlog, drank, wt, wr, x, dx2, norm_g, s_in, small)
    return grad_x, r_in, r_p, r_small


_SM_NORM = 0
_SM_BDEC = _SM_NORM + D_MODEL
_SM_GLAG = _SM_BDEC + GLA_DK
_SM_BGATE = _SM_GLAG + GLA_HV
_SM_FINAL = _SM_BGATE + 2 * D_MODEL
_SM_REPL = _SM_FINAL + D_MODEL
_SM_LOSS = _SM_REPL
_SM_WDEC = _SM_LOSS + 128
_SM_LEN = _SM_WDEC + GLA_RANK * GLA_DK


def kernel(x, norm_g, w_in, w_dec_up, b_dec, gla_norm_g, w_pa, w_pb, b_gate, w_o, final_g, loss_target, m_norm_g, m_w_in, m_w_dec_up, m_b_dec, m_gla_norm_g, m_w_pa, m_w_pb, m_b_gate, m_w_o, m_final_g, v_norm_g, v_w_in, v_w_dec_up, v_b_dec, v_gla_norm_g, v_w_pa, v_w_pb, v_b_gate, v_w_o, v_final_g):
    D = D_MODEL
    me = 4 * lax.axis_index("x") + 2 * lax.axis_index("y") + lax.axis_index("c")

    wp_shard = jnp.stack([w_pa, w_pb, w_o]).astype(BF16)
    n_first = _half_rows(SHARD_COLS)
    win_all, wdec_all = _all_gather([w_in.T.astype(BF16), w_dec_up], "gather_w",
                                    row_pieces=[[(0, n_first), (n_first, SHARD_COLS - n_first)], None])
    wt = _flatten_blocks_call(win_all)
    wr = jnp.pad(wt[RANK_COL:RANK_COL + GLA_RANK], ((0, 128 - GLA_RANK), (0, 0)))
    wdec_full = wdec_all.transpose(1, 0, 2).reshape(GLA_RANK, GLA_DK)
    wdec = jnp.pad(wdec_full, ((0, 128 - GLA_RANK), (0, 0)))

    grad_x, r_in, r_p, r_small = _local_step(
        x[0], loss_target[0], wt, wr, wdec, b_dec.reshape(1, -1), wp_shard,
        norm_g.reshape(1, -1), gla_norm_g.reshape(1, -1), b_gate.reshape(1, -1), final_g.reshape(1, -1))

    gw_in, d_in, nm_in, nv_in = (a.T for a in _adamw_call(r_in, w_in.T, m_w_in.T, v_w_in.T, "adamw_in"))
    (g_pa, d_pa, nm_pa, nv_pa), (g_pb, d_pb, nm_pb, nv_pb), (g_o, d_o, nm_o, nv_o) = _adamw_rows_call(
        r_p, [w_pa, w_pb, w_o], [m_w_pa, m_w_pb, m_w_o], [v_w_pa, v_w_pb, v_w_o], "adamw_p")

    def row(a):
        return a.reshape(1, -1)

    rep = _adamw_lanes_call(
        r_small, [_SM_NORM, _SM_BDEC, _SM_GLAG, _SM_BGATE, _SM_FINAL],
        [row(a) for a in (norm_g, b_dec, gla_norm_g, b_gate, final_g)],
        [row(a) for a in (m_norm_g, m_b_dec, m_gla_norm_g, m_b_gate, m_final_g)],
        [row(a) for a in (v_norm_g, v_b_dec, v_gla_norm_g, v_b_gate, v_final_g)], "adamw_rep")
    ((g_norm, d_norm, nm_norm, nv_norm), (g_bdec, d_bdec, nm_bdec, nv_bdec), (g_glag, d_glag, nm_glag, nv_glag),
     (g_bgate, d_bgate, nm_bgate, nv_bgate), (g_final, d_final, nm_final, nv_final)) = [
        tuple(a.reshape(-1) for a in quad) for quad in rep]

    wdec_parts = r_small[:, 0, _SM_WDEC:].reshape(N_DEV, GLA_RANK, GLA_DK)
    cols = GLA_DK // N_DEV
    wdec_mine = lax.dynamic_slice_in_dim(wdec_parts, me * cols, cols, axis=2)
    g_wdec, d_wdec, nm_wdec, nv_wdec = _adamw_call(wdec_mine, w_dec_up, m_w_dec_up, v_w_dec_up, "adamw_dec")

    loss_total = jnp.sum(r_small[:, 0, _SM_LOSS])

    return (loss_total, grad_x[None],
            g_norm, gw_in, g_wdec, g_bdec, g_glag, g_pa, g_pb, g_bgate, g_o, g_final,
            d_norm, d_in, d_wdec, d_bdec, d_glag, d_pa, d_pb, d_bgate, d_o, d_final,
            nm_norm, nm_in, nm_wdec, nm_bdec, nm_glag, nm_pa, nm_pb, nm_bgate, nm_o, nm_final,
            nv_norm, nv_in, nv_wdec, nv_bdec, nv_glag, nv_pa, nv_pb, nv_bgate, nv_o, nv_final)
```

```python
import math

import jax
import jax.numpy as jnp
from jax import lax
from jax.experimental import pallas as pl
from jax.experimental.pallas import tpu as pltpu

F32 = jnp.float32
BF16 = jnp.bfloat16

N_DEV = 8
D_MODEL = 1024
GLA_HEADS = 4
GLA_HK = 128
GLA_HV = 256
GLA_DK = 512
GLA_RANK = 16
GLA_TAU = 16.0
GLA_CHUNK = 64
SB_HEADS = 8
SB_HD = 128
EPS = 1e-6
N_GROUPS = 9
RANK_COL = 3072
IN_COLS = 9232
SHARD_COLS = IN_COLS // N_DEV

ADAM_LR = 0.001
ADAM_B1 = 0.9
ADAM_B2 = 0.999
ADAM_EPS = 1e-08
ADAM_WD = 0.01
ADAM_STEP = 10

VMEM_LIMIT = 56 * 1024 * 1024
TBLK = 256


def _cparams(sem=None):
    return pltpu.CompilerParams(dimension_semantics=sem, vmem_limit_bytes=VMEM_LIMIT)


def _tiling_2d(rows, cols, band_cols):
    if rows * cols <= 128 * 1024:
        return (rows, cols), (1,), lambda i: (0, 0)
    if rows % 128 == 0:
        return (128, cols), (rows // 128,), lambda i: (i, 0)
    tc = band_cols if cols % band_cols == 0 else cols
    return (rows, tc), (cols // tc,), lambda i: (0, i)


def _dot(a, b):
    return jnp.dot(a, b, preferred_element_type=F32)


def _dot_nt(a, b):
    return lax.dot_general(a, b, (((1,), (1,)), ((), ())), preferred_element_type=F32)


def _dot_tn(a, b):
    return lax.dot_general(a, b, (((0,), (0,)), ((), ())), preferred_element_type=F32)


def _bf(x):
    return x.astype(BF16)


def _split3(x):
    hi = x.astype(BF16)
    r = x - hi.astype(F32)
    mid = r.astype(BF16)
    lo = (r - mid.astype(F32)).astype(BF16)
    return hi, mid, lo


def _tri_left(tri, x):
    hi, mid, lo = _split3(x)
    return _dot(tri, hi) + _dot(tri, mid) + _dot(tri, lo)


def _split2(x):
    hi = lax.bitcast_convert_type(lax.bitcast_convert_type(x, jnp.uint32) & jnp.uint32(0xFFFF0000), F32)
    return hi.astype(BF16), (x - hi).astype(BF16)


def _tri2_left(tri, x):
    hi, lo = _split2(x)
    return _dot(tri, hi) + _dot(tri, lo)


def _tri2_right(x, tri):
    hi, lo = _split2(x)
    return _dot(hi, tri) + _dot(lo, tri)


def _iota2(n, m, dim):
    return lax.broadcasted_iota(jnp.int32, (n, m), dim)


def _sigmoid(x):
    return 1.0 / (1.0 + jnp.exp(-x))


def _softplus_neg_abs(z):
    return jnp.log(1.0 + jnp.exp(-jnp.abs(z)))


_ANY = pl.BlockSpec(memory_space=pl.ANY)


def _mesh_pos():
    return lax.axis_index("x"), lax.axis_index("y"), lax.axis_index("c")


def _other_chips(x, y):
    return [(1 - x, y), (x, 1 - y), (1 - x, 1 - y)]


def _rcopy(src, dst, send_sem, recv_sem, to):
    return pltpu.make_async_remote_copy(src_ref=src, dst_ref=dst, send_sem=send_sem, recv_sem=recv_sem,
                                        device_id=to, device_id_type=pl.DeviceIdType.MESH)


def _push_copies(src_ref, dst_ref, send_sems, recv_sems, loc_sem, scatter):
    x, y, c = _mesh_pos()
    me = 4 * x + 2 * y + c
    own = pltpu.make_async_copy(src_ref.at[me] if scatter else src_ref, dst_ref.at[me], loc_sem)
    pairs = []
    for k in range(1, N_DEV):
        px = 1 - x if k & 4 else x
        py = 1 - y if k & 2 else y
        pc = 1 - c if k & 1 else c
        pid = 4 * px + 2 * py + pc
        src = src_ref.at[pid] if scatter else src_ref
        send = _rcopy(src, dst_ref.at[me], send_sems.at[k - 1], recv_sems.at[k - 1], (px, py, pc))
        recv = _rcopy(src, dst_ref.at[pid], send_sems.at[k - 1], recv_sems.at[k - 1], (px, py, pc))
        pairs.append((send, recv))
    return own, pairs


def _push_start(own, pairs):
    own.start()
    for send, _ in pairs:
        send.start()


def _push_wait(own, pairs):
    for _, recv in pairs:
        recv.wait_recv()
    for send, _ in pairs:
        send.wait_send()
    own.wait()


_PUSH_SEMS = [pltpu.SemaphoreType.DMA((N_DEV - 1,)), pltpu.SemaphoreType.DMA((N_DEV - 1,)),
              pltpu.SemaphoreType.DMA]


def _half_rows(rows):
    return (rows // 2) // 16 * 16


_ADD_ROWS = 128


def _chip_reduce_steps(src_ref, dst_ref, relayed_ref, sum_x, sum_y, rel_x, rel_y, load_sems, send_sems, recv_sems,
                       loc_sem):
    _, R, C = src_ref.shape
    n0 = _half_rows(R)
    lo, hi = pl.ds(0, n0), pl.ds(n0, R - n0)
    x, y, c = _mesh_pos()
    (xx, xy), (yx, yy), (dx, dy) = _other_chips(x, y)
    to_diag, to_x, to_y = src_ref.at[2 * dx + dy], src_ref.at[2 * xx + xy], src_ref.at[2 * yx + yy]
    x_nb, y_nb = (xx, xy, c), (yx, yy, c)
    relays = (_rcopy(to_diag.at[lo], relayed_ref.at[lo], send_sems.at[0], recv_sems.at[0], x_nb),
              _rcopy(to_diag.at[hi], relayed_ref.at[hi], send_sems.at[1], recv_sems.at[1], y_nb))
    plain = (_rcopy(to_x.at[lo], dst_ref.at[0, lo], send_sems.at[2], recv_sems.at[2], x_nb),
             _rcopy(to_y.at[hi], dst_ref.at[1, hi], send_sems.at[3], recv_sems.at[3], y_nb))
    summed = (_rcopy(sum_x, dst_ref.at[0, hi], send_sems.at[4], recv_sems.at[4], x_nb),
              _rcopy(sum_y, dst_ref.at[1, lo], send_sems.at[5], recv_sems.at[5], y_nb))
    load_mine = (pltpu.make_async_copy(to_x.at[hi], sum_x, load_sems.at[0]),
                 pltpu.make_async_copy(to_y.at[lo], sum_y, load_sems.at[1]))
    load_relayed = (pltpu.make_async_copy(relayed_ref.at[hi], rel_x, load_sems.at[2]),
                    pltpu.make_async_copy(relayed_ref.at[lo], rel_y, load_sems.at[3]))
    own = pltpu.make_async_copy(src_ref.at[2 * x + y], dst_ref.at[2], loc_sem)

    def start():
        for cp in relays + plain + (own,) + load_mine:
            cp.start()

    def add(acc_ref, rel_ref):
        for r0 in range(0, acc_ref.shape[0], _ADD_ROWS):
            rows = slice(r0, min(r0 + _ADD_ROWS, acc_ref.shape[0]))
            acc_ref[rows, :] = (acc_ref[rows, :].astype(F32) + rel_ref[rows, :].astype(F32)).astype(acc_ref.dtype)

    def forward():
        for cp in relays:
            cp.wait_recv()
        for cp in load_relayed:
            cp.start()
        for cp in load_mine + load_relayed:
            cp.wait()
        add(sum_x, rel_x)
        add(sum_y, rel_y)
        for cp in summed:
            cp.start()

    def finish():
        for cp in plain + summed:
            cp.wait_recv()
        for cp in relays + plain + summed:
            cp.wait_send()
        own.wait()

    return start, forward, finish


def _chip_reduce_scratch(rows, cols, dtype):
    n0 = _half_rows(rows)
    return [pltpu.VMEM((rows - n0, cols), dtype), pltpu.VMEM((n0, cols), dtype)] * 2 + [
        pltpu.SemaphoreType.DMA((4,)), pltpu.SemaphoreType.DMA((6,)), pltpu.SemaphoreType.DMA((6,)),
        pltpu.SemaphoreType.DMA]


def _all_gather(arrs, name, row_pieces=None):
    n = len(arrs)
    pieces = [[None] if not row_pieces or not row_pieces[a] else list(row_pieces[a]) for a in range(n)]
    assert all(len(p) in (1, 2) for p in pieces)
    units = [(a, i) for a in range(n) for i in range(len(pieces[a]))]

    def body(*refs):
        ins = refs[:n]
        outs = refs[n:2 * n]
        send_sems, recv_sems, loc_sems = refs[2 * n:]
        x, y, c = _mesh_pos()
        me, sib = (x, y, c), (x, y, 1 - c)
        xn, yn, dg = [(px, py, c) for px, py in _other_chips(x, y)]

        def rows(ref, a, i):
            return ref if pieces[a][i] is None else ref.at[pl.ds(*pieces[a][i])]

        def copy(u, k, block, to, own=False):
            a, i = u
            px, py, pc = block
            dst = rows(outs[a].at[4 * px + 2 * py + pc], a, i)
            return _rcopy(rows(ins[a], a, i) if own else dst, dst, send_sems.at[a, k, i], recv_sems.at[a, k, i], to)

        started = []

        def start(cp):
            cp.start()
            started.append(cp)

        def landed_then_pass_on(u, k, block):
            copy(u, k, block, me).wait_recv()
            start(copy(u, 3 + k, block, sib))

        mine = [pltpu.make_async_copy(ins[a], outs[a].at[4 * x + 2 * y + c], loc_sems.at[a]) for a in range(n)]
        for cp in mine:
            cp.start()
        for u in units:
            start(copy(u, 0, me, sib, own=True))
        for a in range(n):
            if len(pieces[a]) == 2:
                for i, to, k in ((0, xn, 1), (1, yn, 2), (1, xn, 1), (0, yn, 2)):
                    start(copy((a, i), k, me, to, own=True))
            else:
                for to, k in ((xn, 1), (yn, 2), (dg, 3)):
                    start(copy((a, 0), k, me, to, own=True))
        for a in range(n):
            if len(pieces[a]) == 2:
                landed_then_pass_on((a, 0), 1, xn)
                start(copy((a, 0), 3, xn, yn))
                landed_then_pass_on((a, 1), 2, yn)
                start(copy((a, 1), 3, yn, xn))
                landed_then_pass_on((a, 1), 1, xn)
                landed_then_pass_on((a, 0), 2, yn)
                landed_then_pass_on((a, 0), 3, dg)
                landed_then_pass_on((a, 1), 3, dg)
            else:
                for block, k in ((xn, 1), (yn, 2), (dg, 3)):
                    landed_then_pass_on((a, 0), k, block)
        for u in units:
            copy(u, 0, sib, me).wait_recv()
            for k, (px, py, _) in ((4, xn), (5, yn), (6, dg)):
                copy(u, k, (px, py, 1 - c), me).wait_recv()
        for cp in started:
            cp.wait_send()
        for cp in mine:
            cp.wait()

    n_pc = max(len(p) for p in pieces)

    return pl.pallas_call(
        body, name=name,
        out_shape=tuple(jax.ShapeDtypeStruct((N_DEV,) + a.shape, a.dtype) for a in arrs),
        in_specs=[_ANY] * n,
        out_specs=tuple([_ANY] * n),
        scratch_shapes=[pltpu.SemaphoreType.DMA((n, 7, n_pc)), pltpu.SemaphoreType.DMA((n, 7, n_pc)),
                        pltpu.SemaphoreType.DMA((n,))],
    )(*arrs)


def _flatten_blocks_call(blocks):
    n, R, C = blocks.shape
    tc = C // 2

    def body(in_ref, out_ref):
        for p in range(n):
            out_ref[p * R:(p + 1) * R, :] = in_ref[p]

    return pl.pallas_call(
        body, name="flatten_w",
        grid=(C // tc,),
        in_specs=[pl.BlockSpec((n, R, tc), lambda i: (0, 0, i))],
        out_specs=pl.BlockSpec((n * R, tc), lambda i: (0, i)),
        out_shape=jax.ShapeDtypeStruct((n * R, C), blocks.dtype),
        compiler_params=_cparams(("arbitrary",)),
    )(blocks)


_PARTS_BANDS = 8


def _pair_sum_call(dmain, drank):
    D = dmain.shape[1]
    n = _PARTS_BANDS
    tc = D // n

    def pieces(p):
        lo, hi = p * SHARD_COLS, (p + 1) * SHARD_COLS
        at, found = 0, []
        for is_rank, a, b in ((False, lo, min(hi, RANK_COL)),
                              (True, max(lo, RANK_COL) - RANK_COL, min(hi, RANK_COL + GLA_RANK) - RANK_COL),
                              (False, max(lo, RANK_COL + GLA_RANK) - GLA_RANK, hi - GLA_RANK)):
            if b > a:
                found.append((is_rank, a, at, b - a))
                at += b - a
        return found

    def body(dm_ref, dr_ref, dm_prev, dr_prev, sum_ref, laid, got, send_sems, recv_sems):
        x, y, c = _mesh_pos()

        def pushes(k):
            return [_rcopy(laid.at[k % 2, q], got.at[k, q], send_sems.at[k, q], recv_sems.at[k, q], (x, y, 1 - c))
                    for q in range(4)]

        for k in range(n + 1):
            for mine in range(2):
                @pl.when((pl.program_id(0) == k) & (c == mine))
                def _(k=k, mine=mine):
                    if k < n:
                        if k >= 2:
                            for cp in pushes(k - 2):
                                cp.wait_send()
                        for q in range(4):
                            for is_rank, a, at, rows in pieces(2 * q + 1 - mine):
                                src = dr_ref if is_rank else dm_ref
                                laid[k % 2, q, at:at + rows, :] = src[a:a + rows, :]
                        for cp in pushes(k):
                            cp.start()
                    if k >= 1:
                        for cp in pushes(k - 1):
                            cp.wait_recv()
                        for q in range(4):
                            for is_rank, a, at, rows in pieces(2 * q + mine):
                                src = dr_prev if is_rank else dm_prev
                                sum_ref[q, at:at + rows, :] = (
                                    src[a:a + rows, :].astype(F32)
                                    + got[k - 1, q, at:at + rows, :].astype(F32)).astype(sum_ref.dtype)
                    if k == n:
                        for k_open in range(max(0, n - 2), n):
                            for cp in pushes(k_open):
                                cp.wait_send()

    def band(k):
        return (0, jnp.minimum(k, n - 1))

    def band_before(k):
        return (0, jnp.maximum(k - 1, 0))

    sems = pltpu.SemaphoreType.DMA((n, 4))
    return pl.pallas_call(
        body, name="pair_sum",
        grid=(n + 1,),
        in_specs=[pl.BlockSpec((dmain.shape[0], tc), band), pl.BlockSpec((GLA_RANK, tc), band),
                  pl.BlockSpec((dmain.shape[0], tc), band_before), pl.BlockSpec((GLA_RANK, tc), band_before)],
        out_specs=pl.BlockSpec((4, SHARD_COLS, tc), lambda k: (0,) + band_before(k)),
        out_shape=jax.ShapeDtypeStruct((4, SHARD_COLS, D), dmain.dtype),
        scratch_shapes=[pltpu.VMEM((2, 4, SHARD_COLS, tc), dmain.dtype),
                        pltpu.VMEM((n, 4, SHARD_COLS, tc), dmain.dtype), sems, sems],
        compiler_params=_cparams(("arbitrary",)),
    )(dmain, drank, dmain, drank)


def _group_row(g):
    return GLA_RANK * (g * (1024 // GLA_RANK) + (g >= RANK_COL // 1024))


def _proj_call(x, norm_g, wt, wr, wp_part):
    T, D = x.shape
    tm = min(1024, T)
    assert tm % TBLK == 0
    n_i = T // tm

    def f_slot(j):
        return ((j >= 2).astype(jnp.int32) + (j >= 6).astype(jnp.int32)
                + (j >= 7).astype(jnp.int32) + (j >= 8).astype(jnp.int32))

    def b_slot(j):
        return (j >= 3).astype(jnp.int32) + (j >= 4).astype(jnp.int32) + (j >= 5).astype(jnp.int32)

    def body(x_ref, g_ref, w_ref, wr_ref, wp_ref, pf_ref, pb_ref, rank_ref, ht_ref, wpall_ref,
             h_scr, send_sems, recv_sems, loc_sem):
        i = pl.program_id(0)
        j = pl.program_id(1)
        own, pairs = _push_copies(wp_ref, wpall_ref, send_sems, recv_sems, loc_sem, scatter=False)

        @pl.when((i == 0) & (j == 0))
        def _():
            _push_start(own, pairs)

        @pl.when(j == 0)
        def _():
            xv = x_ref[...]
            r = lax.rsqrt(jnp.mean(xv * xv, axis=-1, keepdims=True) + EPS)
            h = (xv * r) * g_ref[...]
            hb = _bf(h)
            h_scr[...] = hb
            for b in range(tm // TBLK):
                ht_ref[b] = _bf(h[b * TBLK:(b + 1) * TBLK].T)
            rank_ref[...] = _dot_nt(hb, wr_ref[...])

        is_b = (j == 1) | ((j >= 3) & (j <= 5))

        @pl.when(is_b)
        def _():
            pb_ref[...] = _bf(_dot_nt(h_scr[...], w_ref[...]))

        @pl.when(jnp.logical_not(is_b))
        def _():
            pf_ref[...] = _dot_nt(h_scr[...], w_ref[...])

        @pl.when((i == n_i - 1) & (j == N_GROUPS - 1))
        def _():
            _push_wait(own, pairs)

    return pl.pallas_call(
        body, name="proj",
        grid=(n_i, N_GROUPS),
        in_specs=[pl.BlockSpec((tm, D), lambda i, j: (i, 0)),
                  pl.BlockSpec((1, D), lambda i, j: (0, 0)),
                  pl.BlockSpec((pl.Element(1024), pl.Element(D)), lambda i, j: (_group_row(j), 0)),
                  pl.BlockSpec((128, D), lambda i, j: (0, 0)),
                  _ANY],
        out_specs=(pl.BlockSpec((None, tm, 1024), lambda i, j: (f_slot(j), i, 0)),
                   pl.BlockSpec((None, tm, 1024), lambda i, j: (b_slot(j), i, 0)),
                   pl.BlockSpec((tm, 128), lambda i, j: (i, 0)),
                   pl.BlockSpec((tm // TBLK, D, TBLK), lambda i, j: (i, 0, 0)),
                   _ANY),
        out_shape=(jax.ShapeDtypeStruct((5, T, 1024), F32),
                   jax.ShapeDtypeStruct((4, T, 1024), BF16),
                   jax.ShapeDtypeStruct((T, 128), F32),
                   jax.ShapeDtypeStruct((T // TBLK, D, TBLK), BF16),
                   jax.ShapeDtypeStruct((N_DEV,) + wp_part.shape, wp_part.dtype)),
        scratch_shapes=[pltpu.VMEM((tm, D), BF16)] + _PUSH_SEMS,
        compiler_params=_cparams(("arbitrary", "arbitrary")),
    )(x, norm_g, wt, wr, wp_part)


GLA_STEP_CHUNKS = 4


def _gla_same_chunk(rows):
    return (_iota2(rows, rows, 0) & -GLA_CHUNK) == (_iota2(rows, rows, 1) & -GLA_CHUNK)


def _gla_chunk_terms(la, q, k, n_c):
    C = GLA_CHUNK
    rows = n_c * C
    low = _gla_same_chunk(rows) & (_iota2(rows, rows, 0) >= _iota2(rows, rows, 1))
    b = _tri_left(_bf(low.astype(F32)), la)
    bl = [b[(c + 1) * C - 1:(c + 1) * C, :] for c in range(n_c)]
    bl_rows = jnp.concatenate([jnp.broadcast_to(bl[c], (C, b.shape[1])) for c in range(n_c)], axis=0)
    eb = jnp.exp(b)
    enb = jnp.exp(-b)
    ebl_b = jnp.exp(bl_rows - b)
    scale = GLA_HK ** -0.5
    qe = q * eb * scale
    ke = k * enb
    kd = k * ebl_b
    return bl, eb, enb, ebl_b, qe, ke, kd


def _gla_fwd_call(projf, projb, rank, wdec, bdec):
    T = projf.shape[1]
    C = GLA_CHUNK
    n_chunks = T // C
    n_c = GLA_STEP_CHUNKS
    R = n_c * C
    assert n_chunks % n_c == 0

    def body(qk_ref, v_ref, rank_ref, wd_ref, bd_ref, o_ref, st_ref, la_ref, st_scr):
        @pl.when(pl.program_id(0) == 0)
        def _():
            st_scr[...] = jnp.zeros_like(st_scr)

        dec = _dot(_bf(rank_ref[...]), _bf(wd_ref[...])) + bd_ref[...]
        la = (jnp.minimum(dec, 0.0) - _softplus_neg_abs(dec)) / GLA_TAU
        la_ref[...] = la
        mask = _gla_same_chunk(R) & (_iota2(R, R, 0) >= _iota2(R, R, 1))
        bl, _, _, _, qe, ke, kd = _gla_chunk_terms(la, qk_ref[:, :GLA_DK], qk_ref[:, GLA_DK:], n_c)
        qeb, keb, kdb = _bf(qe), _bf(ke), _bf(kd)
        ebl = [jnp.exp(bl[c]) for c in range(n_c)]
        heads = range(GLA_HEADS)
        ks = [slice(hh * GLA_HK, (hh + 1) * GLA_HK) for hh in heads]
        vs = [slice(hh * GLA_HV, (hh + 1) * GLA_HV) for hh in heads]
        rs = [slice(c * C, (c + 1) * C) for c in range(n_c)]
        p = [_bf(jnp.where(mask, _dot_nt(qeb[:, ks[hh]], keb[:, ks[hh]]), 0.0)) for hh in heads]
        upd = [[_dot_tn(v_ref[rs[c], vs[hh]], kdb[rs[c], ks[hh]]) for hh in heads] for c in range(n_c)]
        intra = [_dot(p[hh], v_ref[:, vs[hh]]) for hh in heads]
        st = [st_scr[hh] for hh in heads]
        for c in range(n_c):
            inter = [_dot_nt(qeb[rs[c], ks[hh]], _bf(st[hh])) for hh in heads]
            for hh in heads:
                st_ref[c, hh] = st[hh]
                o_ref[rs[c], vs[hh]] = intra[hh][rs[c]] + inter[hh]
            st = [st[hh] * ebl[c][:, ks[hh]] + upd[c][hh] for hh in heads]
        for hh in heads:
            st_scr[hh] = st[hh]

    return pl.pallas_call(
        body, name="gla_fwd",
        grid=(n_chunks // n_c,),
        in_specs=[pl.BlockSpec((None, R, 1024), lambda n: (0, n, 0)),
                  pl.BlockSpec((None, R, 1024), lambda n: (0, n, 0)),
                  pl.BlockSpec((R, 128), lambda n: (n, 0)),
                  pl.BlockSpec((128, GLA_DK), lambda n: (0, 0)),
                  pl.BlockSpec((1, GLA_DK), lambda n: (0, 0))],
        out_specs=(pl.BlockSpec((R, 1024), lambda n: (n, 0)),
                   pl.BlockSpec((n_c, GLA_HEADS, GLA_HV, GLA_HK), lambda n: (n, 0, 0, 0)),
                   pl.BlockSpec((R, GLA_DK), lambda n: (n, 0))),
        out_shape=(jax.ShapeDtypeStruct((T, 1024), F32),
                   jax.ShapeDtypeStruct((n_chunks, GLA_HEADS, GLA_HV, GLA_HK), F32),
                   jax.ShapeDtypeStruct((T, GLA_DK), F32)),
        scratch_shapes=[pltpu.VMEM((GLA_HEADS, GLA_HV, GLA_HK), F32)],
        compiler_params=_cparams(("arbitrary",)),
    )(projf, projb, rank, wdec, bdec)


def _gla_bwd_call(projf, projb, la, do_gla, st_all, rank, wdec):
    T = projf.shape[1]
    C = GLA_CHUNK
    n_chunks = T // C
    n_c = GLA_STEP_CHUNKS
    R = n_c * C
    assert n_chunks % n_c == 0
    last = n_chunks // n_c - 1

    def body(qk_ref, v_ref, la_ref, do_ref, st_ref, rank_ref, wd_ref,
             dqk_ref, dv_ref, drank_ref, dwd_ref, dbd_ref, dst_scr):
        @pl.when(pl.program_id(0) == 0)
        def _():
            dst_scr[...] = jnp.zeros_like(dst_scr)
            dwd_ref[...] = jnp.zeros_like(dwd_ref)
            dbd_ref[...] = jnp.zeros_like(dbd_ref)

        same = _gla_same_chunk(R)
        mask = same & (_iota2(R, R, 0) >= _iota2(R, R, 1))
        upp = _bf((same & (_iota2(R, R, 0) <= _iota2(R, R, 1))).astype(F32))
        scale = GLA_HK ** -0.5
        la = la_ref[...]
        bl, eb, enb, ebl_b, qe, ke, kd = _gla_chunk_terms(la, qk_ref[:, :GLA_DK], qk_ref[:, GLA_DK:], n_c)
        qeb, keb, kdb = _bf(qe), _bf(ke), _bf(kd)
        ebl = [jnp.exp(bl[c]) for c in range(n_c)]
        heads = range(GLA_HEADS)
        ks = [slice(hh * GLA_HK, (hh + 1) * GLA_HK) for hh in heads]
        vs = [slice(hh * GLA_HV, (hh + 1) * GLA_HV) for hh in heads]
        rs = [slice(c * C, (c + 1) * C) for c in range(n_c)]
        v = [v_ref[:, vs[hh]] for hh in heads]
        do = [_bf(do_ref[:, vs[hh]]) for hh in heads]
        p = [_bf(jnp.where(mask, _dot_nt(qeb[:, ks[hh]], keb[:, ks[hh]]), 0.0)) for hh in heads]
        dp = [_bf(jnp.where(mask, _dot_nt(do[hh], v[hh]), 0.0)) for hh in heads]
        dst_intra = [[_dot_tn(do[hh][rs[c]], qeb[rs[c], ks[hh]]) for hh in heads] for c in range(n_c)]
        dqe_inter = [[_dot(do[hh][rs[c]], _bf(st_ref[c, hh])) for hh in heads] for c in range(n_c)]
        dv_intra = [_dot_tn(p[hh], do[hh]) for hh in heads]
        dqe_intra = [_dot(dp[hh], keb[:, ks[hh]]) for hh in heads]
        dke = jnp.concatenate([_dot_tn(dp[hh], qeb[:, ks[hh]]) for hh in heads], axis=1)
        dstn = [dst_scr[hh] for hh in heads]
        dkd_c, dv_inter, debl = [None] * n_c, [None] * n_c, [None] * n_c
        for c in reversed(range(n_c)):
            dstnb = [_bf(dstn[hh]) for hh in heads]
            dkd_c[c] = jnp.concatenate([_dot(v[hh][rs[c]], dstnb[hh]) for hh in heads], axis=1)
            dv_inter[c] = [_dot_nt(kdb[rs[c], ks[hh]], dstnb[hh]) for hh in heads]
            debl[c] = jnp.concatenate(
                [jnp.sum(dstn[hh] * st_ref[c, hh], axis=0, keepdims=True) for hh in heads], axis=1)
            dstn = [dst_intra[c][hh] + dstn[hh] * ebl[c][:, ks[hh]] for hh in heads]
        for hh in heads:
            dst_scr[hh] = dstn[hh]
            dv_ref[:, vs[hh]] = _bf(dv_intra[hh] + jnp.concatenate([dv_inter[c][hh] for c in range(n_c)], axis=0))
        dqe = jnp.concatenate(
            [dqe_intra[hh] + jnp.concatenate([dqe_inter[c][hh] for c in range(n_c)], axis=0) for hh in heads], axis=1)
        dkd = jnp.concatenate(dkd_c, axis=0)
        dkd_kd = dkd * kd
        db = dqe * qe - dke * ke - dkd_kd
        dbl = jnp.concatenate(
            [jnp.broadcast_to(jnp.sum(dkd_kd[rs[c]], axis=0, keepdims=True) + ebl[c] * debl[c], (C, GLA_DK))
             for c in range(n_c)], axis=0)
        dla = _tri_left(upp, db) + dbl
        dqk_ref[:, :GLA_DK] = _bf(dqe * eb * scale)
        dqk_ref[:, GLA_DK:] = _bf(dke * enb + dkd * ebl_b)
        ddec = dla * (1.0 / GLA_TAU) * (1.0 - jnp.exp(GLA_TAU * la))
        ddecb = _bf(ddec)
        drank_ref[...] = _bf(_dot_nt(ddecb, _bf(wd_ref[...])))
        dwd_ref[...] += _dot_tn(_bf(rank_ref[...]), ddecb)
        dbd_ref[...] += jnp.sum(ddec, axis=0, keepdims=True)

    return pl.pallas_call(
        body, name="gla_bwd",
        grid=(n_chunks // n_c,),
        in_specs=[pl.BlockSpec((None, R, 1024), lambda n: (0, last - n, 0)),
                  pl.BlockSpec((None, R, 1024), lambda n: (0, last - n, 0)),
                  pl.BlockSpec((R, GLA_DK), lambda n: (last - n, 0)),
                  pl.BlockSpec((R, 1024), lambda n: (last - n, 0)),
                  pl.BlockSpec((n_c, GLA_HEADS, GLA_HV, GLA_HK), lambda n: (last - n, 0, 0, 0)),
                  pl.BlockSpec((R, 128), lambda n: (last - n, 0)),
                  pl.BlockSpec((128, GLA_DK), lambda n: (0, 0))],
        out_specs=(pl.BlockSpec((R, 1024), lambda n: (last - n, 0)),
                   pl.BlockSpec((R, 1024), lambda n: (last - n, 0)),
                   pl.BlockSpec((R, 128), lambda n: (last - n, 0)),
                   pl.BlockSpec((128, GLA_DK), lambda n: (0, 0)),
                   pl.BlockSpec((1, GLA_DK), lambda n: (0, 0))),
        out_shape=(jax.ShapeDtypeStruct((T, 1024), BF16),
                   jax.ShapeDtypeStruct((T, 1024), BF16),
                   jax.ShapeDtypeStruct((T, 128), BF16),
                   jax.ShapeDtypeStruct((128, GLA_DK), F32),
                   jax.ShapeDtypeStruct((1, GLA_DK), F32)),
        scratch_shapes=[pltpu.VMEM((GLA_HEADS, GLA_HV, GLA_HK), F32)],
        compiler_params=_cparams(("arbitrary",)),
    )(projf, projb, la, do_gla, st_all, rank, wdec)


def _sb_logs(z):
    lsz = jnp.minimum(z, 0.0) - _softplus_neg_abs(z)
    return lsz, lsz - z


SB_HG_FWD = 8
SB_HG_BWD = 4
SB_QUERIES = 256
SB_KEYS = 256
SB_DEAD = -105.0


def _sb_fwd_call(projb, wp_shard):
    T = projb.shape[1]
    B = min(SB_QUERIES, T)
    HG = SB_HG_FWD
    W = HG * SB_HD
    scale = 1.0 / math.sqrt(SB_HD)
    KB = min(SB_KEYS, T)
    n_h, n_i = SB_HEADS // HG, T // B

    def body(q_ref, k_ref, v_ref, wp_ref, o_ref, wpall_ref, cb_scr, send_sems, recv_sems, loc_sem):
        i = pl.program_id(1)
        own, pairs = _push_copies(wp_ref, wpall_ref, send_sems, recv_sems, loc_sem, scatter=False)

        @pl.when((pl.program_id(0) == 0) & (i == 0))
        def _():
            _push_start(own, pairs)

        rows = HG * B
        after = (_iota2(KB, KB, 0) > _iota2(KB, KB, 1)).astype(F32)
        tri = _bf(jnp.concatenate([after, jnp.ones((KB, KB), F32)], axis=1))
        o_ref[...] = jnp.zeros_like(o_ref)
        cb_scr[...] = jnp.zeros_like(cb_scr)

        def block(jp, masked):
            off = pl.multiple_of(jp * KB, KB)
            z = jnp.concatenate(
                [_dot_nt(q_ref[:, hh * SB_HD:(hh + 1) * SB_HD], k_ref[pl.ds(off, KB), hh * SB_HD:(hh + 1) * SB_HD])
                 for hh in range(HG)], axis=0) * scale
            lsz, l1m = _sb_logs(z)
            if masked:
                strict = (jp * KB + _iota2(rows, KB, 1)) < (i * B + (_iota2(rows, KB, 0) & (B - 1)))
                l1m = jnp.where(strict, l1m, 0.0)
            r = _tri2_right(l1m, tri)
            cb = cb_scr[...]
            a = jnp.exp(lsz + cb + r[:, :KB])
            if masked:
                a = jnp.where(strict, a, 0.0)
            cb_scr[...] = cb + r[:, KB:]
            ab = _bf(a)
            for hh in range(HG):
                cs = slice(hh * SB_HD, (hh + 1) * SB_HD)
                o_ref[:, cs] += _dot(ab[hh * B:(hh + 1) * B, :], v_ref[pl.ds(off, KB), cs])

        jp0 = (i * B) // KB
        block(jp0, True)

        def live(state):
            jj, dead = state
            return (jj <= jp0) & jnp.logical_not(dead)

        def step(state):
            jj, _ = state
            block(jp0 - jj, False)
            return jj + 1, jnp.max(cb_scr[:, :SB_HD]) < SB_DEAD

        lax.while_loop(live, step, (jnp.int32(1), jnp.max(cb_scr[:, :SB_HD]) < SB_DEAD))

        @pl.when((pl.program_id(0) == n_h - 1) & (i == n_i - 1))
        def _():
            _push_wait(own, pairs)

    return pl.pallas_call(
        body, name="sb_fwd",
        grid=(n_h, n_i),
        in_specs=[pl.BlockSpec((None, B, W), lambda h, i: (1, i, h)),
                  pl.BlockSpec((None, T, W), lambda h, i: (2, 0, h)),
                  pl.BlockSpec((None, T, W), lambda h, i: (3, 0, h)),
                  _ANY],
        out_specs=(pl.BlockSpec((B, W), lambda h, i: (i, h)), _ANY),
        out_shape=(jax.ShapeDtypeStruct((T, 1024), F32),
                   jax.ShapeDtypeStruct((N_DEV,) + wp_shard.shape, wp_shard.dtype)),
        scratch_shapes=[pltpu.VMEM((HG * B, KB), F32)] + _PUSH_SEMS,
        compiler_params=_cparams(("arbitrary", "arbitrary")),
    )(projb, projb, projb, wp_shard)


def _sb_bwd_call(projb, do_sb, g_p):
    T = projb.shape[1]
    B = min(SB_QUERIES, T)
    nb = T // B
    HG = SB_HG_BWD
    W = HG * SB_HD
    WQ = HG * B
    KB = min(SB_KEYS, T)
    nkb = T // KB
    n_h = SB_HEADS // HG
    scale = 1.0 / math.sqrt(SB_HD)

    def body(q_ref, k_ref, v_ref, do_ref, gp_ref, dq_ref, dk_ref, dv_ref, rp_ref,
             dk_scr, dv_scr, kt_scr, beta_scr, g_scr, dqt_scr, send_sems, recv_sems, loc_sem):
        i = pl.program_id(1)
        own, pairs = _push_copies(gp_ref, rp_ref, send_sems, recv_sems, loc_sem, scatter=True)

        @pl.when((pl.program_id(0) == 0) & (i == 0))
        def _():
            _push_start(own, pairs)

        @pl.when(i == 0)
        def _():
            dk_scr[...] = jnp.zeros_like(dk_scr)
            dv_scr[...] = jnp.zeros_like(dv_scr)
            for hh in range(HG):
                for jb in range(nkb):
                    kt_scr[hh, jb] = _bf(
                        k_ref[jb * KB:(jb + 1) * KB, hh * SB_HD:(hh + 1) * SB_HD].astype(F32).T)

        dqt_scr[...] = jnp.zeros_like(dqt_scr)
        later = _bf((_iota2(KB, KB, 1) > _iota2(KB, KB, 0)).astype(F32))
        earlier = _bf((_iota2(KB, KB, 1) < _iota2(KB, KB, 0)).astype(F32))
        dob = _bf(do_ref[...])
        jp0 = (i * B) // KB

        def strict_mask():
            return (jp0 * KB + _iota2(KB, WQ, 0)) < (i * B + (_iota2(KB, WQ, 1) & (B - 1)))

        def heads(fn):
            return [fn(slice(hh * SB_HD, (hh + 1) * SB_HD)) for hh in range(HG)]

        def pass1(jp, cb, masked):
            off = pl.multiple_of(jp * KB, KB)
            z = jnp.concatenate(heads(lambda cs: _dot_nt(k_ref[pl.ds(off, KB), cs], q_ref[:, cs])), axis=1) * scale
            da = jnp.concatenate(heads(lambda cs: _dot_nt(v_ref[pl.ds(off, KB), cs], dob[:, cs])), axis=1)
            lsz, l1m = _sb_logs(z)
            if masked:
                strict = strict_mask()
                l1m = jnp.where(strict, l1m, 0.0)
            a = jnp.exp(lsz + cb + _tri2_left(later, l1m))
            if masked:
                a = jnp.where(strict, a, 0.0)
            g_scr[jp] = a * da
            beta_scr[jp] = jnp.exp(lsz)
            ab = _bf(a)
            for hh in range(HG):
                cs = slice(hh * SB_HD, (hh + 1) * SB_HD)
                dv_scr[pl.ds(off, KB), cs] += _dot(ab[:, hh * B:(hh + 1) * B], dob[:, cs])
            return cb + jnp.sum(l1m, axis=0, keepdims=True)

        zero = jnp.zeros((1, WQ), F32)
        cb = pass1(jp0, zero, True)

        def live(state):
            jj, _, dead = state
            return (jj <= jp0) & jnp.logical_not(dead)

        def step(state):
            jj, cr, _ = state
            cr = pass1(jp0 - jj, cr, False)
            return jj + 1, cr, jnp.max(cr) < SB_DEAD

        n_done, _, _ = lax.while_loop(live, step, (jnp.int32(1), cb, jnp.max(cb) < SB_DEAD))
        jp_first = jp0 - (n_done - 1)

        def pass2(jp, cg, masked):
            off = pl.multiple_of(jp * KB, KB)
            g = g_scr[jp]
            beta = beta_scr[jp]
            dz = g * (1.0 - beta) - beta * (cg + _tri2_left(earlier, g))
            if masked:
                dz = jnp.where(strict_mask(), dz, 0.0)
            dzb = _bf(dz * scale)
            for hh in range(HG):
                cs = slice(hh * SB_HD, (hh + 1) * SB_HD)
                dk_scr[pl.ds(off, KB), cs] += _dot(dzb[:, hh * B:(hh + 1) * B], q_ref[:, cs])
                dqt_scr[hh] += _dot(kt_scr[hh, jp], dzb[:, hh * B:(hh + 1) * B])
            return cg + jnp.sum(g, axis=0, keepdims=True)

        cg = lax.fori_loop(jp_first, jp0, lambda jp, cr: pass2(jp, cr, False), zero)
        pass2(jp0, cg, True)
        for hh in range(HG):
            dq_ref[:, hh * SB_HD:(hh + 1) * SB_HD] = _bf(dqt_scr[hh].T)

        @pl.when(i == nb - 1)
        def _():
            dk_ref[...] = _bf(dk_scr[...])
            dv_ref[...] = _bf(dv_scr[...])

        @pl.when((pl.program_id(0) == n_h - 1) & (i == nb - 1))
        def _():
            _push_wait(own, pairs)

    return pl.pallas_call(
        body, name="sb_bwd",
        grid=(n_h, nb),
        in_specs=[pl.BlockSpec((None, B, W), lambda h, i: (1, i, h)),
                  pl.BlockSpec((None, T, W), lambda h, i: (2, 0, h)),
                  pl.BlockSpec((None, T, W), lambda h, i: (3, 0, h)),
                  pl.BlockSpec((B, W), lambda h, i: (i, h)),
                  _ANY],
        out_specs=(pl.BlockSpec((B, W), lambda h, i: (i, h)),
                   pl.BlockSpec((T, W), lambda h, i: (0, h)),
                   pl.BlockSpec((T, W), lambda h, i: (0, h)),
                   _ANY),
        out_shape=(jax.ShapeDtypeStruct((T, 1024), BF16),
                   jax.ShapeDtypeStruct((T, 1024), BF16),
                   jax.ShapeDtypeStruct((T, 1024), BF16),
                   jax.ShapeDtypeStruct(g_p.shape, g_p.dtype)),
        scratch_shapes=[pltpu.VMEM((T, W), F32), pltpu.VMEM((T, W), F32),
                        pltpu.VMEM((HG, nkb, SB_HD, KB), BF16),
                        pltpu.VMEM((nkb, KB, WQ), F32), pltpu.VMEM((nkb, KB, WQ), F32),
                        pltpu.VMEM((HG, SB_HD, B), F32)] + _PUSH_SEMS,
        compiler_params=_cparams(("arbitrary", "arbitrary")),
    )(projb, projb, projb, do_sb, g_p)


def _mid_call(o_gla, o_sb, projf, x, target, wpa, wpb, wo, gla_g, b_gate, final_g):
    T, D = x.shape
    tm = min(TBLK, T)

    def body(og_ref, ggate_ref, osb_ref, sgate_ref, ma_ref, mb_ref, x_ref, tgt_ref,
             wpa_ref, wpb_ref, wo_ref, glag_ref, bg_ref, fg_ref,
             dx2_ref, dogla_ref, dosb_ref, dggate_ref, dsgate_ref, dm_ref,
             mt_ref, ogt_ref, obt_ref, dx2b_ref, dya_ref, dyb_ref,
             dfg_ref, dbg_ref, dglag_ref, loss_ref):
        @pl.when(pl.program_id(0) == 0)
        def _():
            dfg_ref[...] = jnp.zeros_like(dfg_ref)
            dbg_ref[...] = jnp.zeros_like(dbg_ref)
            dglag_ref[...] = jnp.zeros_like(dglag_ref)
            loss_ref[...] = jnp.zeros_like(loss_ref)

        glag = glag_ref[...]
        ggate = ggate_ref[...]
        sg = _sigmoid(ggate)
        silu_g = ggate * sg
        ohat, rinv, nrm = [], [], []
        for hh in range(GLA_HEADS):
            oh = og_ref[:, hh * GLA_HV:(hh + 1) * GLA_HV]
            r = lax.rsqrt(jnp.mean(oh * oh, axis=-1, keepdims=True) + EPS)
            ohat.append(oh * r)
            rinv.append(r)
            nrm.append(ohat[-1] * glag)
        n_all = jnp.concatenate(nrm, axis=1)
        og = n_all * silu_g
        ogb = _bf(og)
        ya = _dot(ogb, wpa_ref[...])
        sgate = sgate_ref[...]
        ss = _sigmoid(sgate)
        silu_s = sgate * ss
        osb = osb_ref[...]
        ob = osb * silu_s
        obb = _bf(ob)
        yb = _dot(obb, wpb_ref[...])
        ga = _sigmoid(ma_ref[...] + bg_ref[:, :D])
        gb = _sigmoid(mb_ref[...] + bg_ref[:, D:])
        merged = ga * ya + gb * yb
        mgb = _bf(merged)
        x2 = x_ref[...] + _dot(mgb, wo_ref[...])
        r2 = lax.rsqrt(jnp.mean(x2 * x2, axis=-1, keepdims=True) + EPS)
        xh2 = x2 * r2
        fg = fg_ref[...]
        err = xh2 * fg - tgt_ref[...]
        loss_ref[...] += jnp.broadcast_to(
            0.5 * jnp.sum(jnp.mean(err * err, axis=-1, keepdims=True), axis=0, keepdims=True), (1, 128))
        dy = err * (1.0 / D)
        dfg_ref[...] += jnp.sum(dy * xh2, axis=0, keepdims=True)
        dxh = dy * fg
        dx2 = r2 * (dxh - xh2 * jnp.mean(dxh * xh2, axis=-1, keepdims=True))
        dx2_ref[...] = dx2
        dx2b = _bf(dx2)
        dx2b_ref[...] = dx2b
        dmerged = _dot_nt(dx2b, wo_ref[...])
        dya = dmerged * ga
        dyb = dmerged * gb
        dma = dmerged * ya * ga * (1.0 - ga)
        dmb = dmerged * yb * gb * (1.0 - gb)
        dm_ref[:, :D] = _bf(dma)
        dm_ref[:, D:] = _bf(dmb)
        dbg_ref[:, :D] += jnp.sum(dma, axis=0, keepdims=True)
        dbg_ref[:, D:] += jnp.sum(dmb, axis=0, keepdims=True)
        dyab = _bf(dya)
        dybb = _bf(dyb)
        dya_ref[...] = dyab
        dyb_ref[...] = dybb
        dog = _dot_nt(dyab, wpa_ref[...])
        dob = _dot_nt(dybb, wpb_ref[...])
        dosb_ref[...] = dob * silu_s
        dsgate_ref[...] = _bf(dob * osb * (ss * (1.0 + sgate * (1.0 - ss))))
        dn = dog * silu_g
        dggate_ref[...] = _bf(dog * n_all * (sg * (1.0 + ggate * (1.0 - sg))))
        dglag = jnp.zeros((1, GLA_HV), F32)
        for hh in range(GLA_HEADS):
            dnh = dn[:, hh * GLA_HV:(hh + 1) * GLA_HV]
            dglag = dglag + jnp.sum(dnh * ohat[hh], axis=0, keepdims=True)
            dohat = dnh * glag
            dogla_ref[:, hh * GLA_HV:(hh + 1) * GLA_HV] = rinv[hh] * (
                dohat - ohat[hh] * jnp.mean(dohat * ohat[hh], axis=-1, keepdims=True))
        dglag_ref[...] += dglag
        mt_ref[...] = _bf(merged.T)
        ogt_ref[...] = _bf(og.T)
        obt_ref[...] = _bf(ob.T)

    row = lambda i: (i, 0)
    const = lambda i: (0, 0)
    tile = pl.BlockSpec((tm, D), row)
    tile_t = pl.BlockSpec((None, D, tm), lambda i: (i, 0, 0))
    wspec = pl.BlockSpec((D, D), const)
    return pl.pallas_call(
        body, name="mid",
        grid=(T // tm,),
        in_specs=[tile,
                  pl.BlockSpec((None, tm, D), lambda i: (1, i, 0)),
                  tile,
                  pl.BlockSpec((None, tm, D), lambda i: (2, i, 0)),
                  pl.BlockSpec((None, tm, D), lambda i: (3, i, 0)),
                  pl.BlockSpec((None, tm, D), lambda i: (4, i, 0)),
                  tile, tile, wspec, wspec, wspec,
                  pl.BlockSpec((1, GLA_HV), const),
                  pl.BlockSpec((1, 2 * D), const),
                  pl.BlockSpec((1, D), const)],
        out_specs=(tile, tile, tile, tile, tile,
                   pl.BlockSpec((tm, 2 * D), row),
                   tile_t, tile_t, tile_t, tile, tile, tile,
                   pl.BlockSpec((1, D), const),
                   pl.BlockSpec((1, 2 * D), const),
                   pl.BlockSpec((1, GLA_HV), const),
                   pl.BlockSpec((1, 128), const)),
        out_shape=(jax.ShapeDtypeStruct((T, D), F32),
                   jax.ShapeDtypeStruct((T, D), F32),
                   jax.ShapeDtypeStruct((T, D), F32),
                   jax.ShapeDtypeStruct((T, D), BF16),
                   jax.ShapeDtypeStruct((T, D), BF16),
                   jax.ShapeDtypeStruct((T, 2 * D), BF16),
                   jax.ShapeDtypeStruct((T // tm, D, tm), BF16),
                   jax.ShapeDtypeStruct((T // tm, D, tm), BF16),
                   jax.ShapeDtypeStruct((T // tm, D, tm), BF16),
                   jax.ShapeDtypeStruct((T, D), BF16),
                   jax.ShapeDtypeStruct((T, D), BF16),
                   jax.ShapeDtypeStruct((T, D), BF16),
                   jax.ShapeDtypeStruct((1, D), F32),
                   jax.ShapeDtypeStruct((1, 2 * D), F32),
                   jax.ShapeDtypeStruct((1, GLA_HV), F32),
                   jax.ShapeDtypeStruct((1, 128), F32)),
        compiler_params=_cparams(("arbitrary",)),
    )(o_gla, projf, o_sb, projf, projf, projf, x, target, wpa, wpb, wo, gla_g, b_gate, final_g)


def _dh_call(pieces, dmlog, drank, wt, wr, x, dx2, norm_g, s_in, small):
    T, D = x.shape
    tm = min(256, T)
    npc = len(pieces)
    n_main = N_GROUPS * 1024
    n_i = T // tm
    i_forward = 5 * n_i // 8

    def body(*refs):
        pcs = refs[:npc]
        (dm_ref, dr_ref, w_hbm, wr_ref, x_ref, dx2_ref, g_ref, sin_ref, small_ref,
         gx_ref, rin_ref, relayed_ref, rsmall_ref,
         w_scr, sems, dg_ref, small_mine, small_send, small_recv, small_loc, *exchange_scratch) = refs[npc:]
        start, forward, finish = _chip_reduce_steps(sin_ref, rin_ref, relayed_ref, *exchange_scratch)

        @pl.when(pl.program_id(0) == 0)
        def _():
            start()
            lo = pltpu.make_async_copy(w_hbm.at[pl.ds(0, RANK_COL)], w_scr.at[pl.ds(0, RANK_COL)], sems.at[0])
            hi = pltpu.make_async_copy(w_hbm.at[pl.ds(RANK_COL + GLA_RANK, n_main - RANK_COL)],
                                       w_scr.at[pl.ds(RANK_COL, n_main - RANK_COL)], sems.at[1])
            lo.start()
            hi.start()
            dg_ref[...] = jnp.zeros_like(dg_ref)
            lo.wait()
            hi.wait()

        @pl.when(pl.program_id(0) == i_forward)
        def _():
            forward()

        def w_group(g):
            return w_scr[g * 1024:(g + 1) * 1024, :]

        dr = dr_ref[...]
        dh = _dot(dr, wr_ref[...])
        for g in range(npc):
            dh = dh + _dot(pcs[g][...], w_group(g))
        dh = dh + _dot(dm_ref[:, :D], w_group(npc))
        dh = dh + _dot(dm_ref[:, D:], w_group(npc + 1))
        xv = x_ref[...]
        r = lax.rsqrt(jnp.mean(xv * xv, axis=-1, keepdims=True) + EPS)
        xhat = xv * r
        g = g_ref[...]
        dg_ref[...] += jnp.sum(dh * xhat, axis=0, keepdims=True)
        dxhat = dh * g
        gx_ref[...] = r * (dxhat - xhat * jnp.mean(dxhat * xhat, axis=-1, keepdims=True)) + dx2_ref[...]

        @pl.when(pl.program_id(0) == n_i - 1)
        def _():
            small_mine[...] = small_ref[...]
            small_mine[:, _SM_NORM:_SM_NORM + D] = dg_ref[...]
            own, pairs = _push_copies(small_mine, rsmall_ref, small_send, small_recv, small_loc, scatter=False)
            _push_start(own, pairs)
            finish()
            _push_wait(own, pairs)

    row = lambda i: (i, 0)
    const = lambda i: (0, 0)
    tile = pl.BlockSpec((tm, D), row)
    part = s_in.shape[1:]
    return pl.pallas_call(
        body, name="dh",
        grid=(n_i,),
        in_specs=[tile] * npc + [
            pl.BlockSpec((tm, 2 * D), row),
            pl.BlockSpec((tm, 128), row),
            _ANY,
            pl.BlockSpec((128, D), const),
            tile, tile,
            pl.BlockSpec((1, D), const),
            _ANY,
            pl.BlockSpec(small.shape, const)],
        out_specs=(tile, _ANY, _ANY, _ANY),
        out_shape=(jax.ShapeDtypeStruct((T, D), F32),
                   jax.ShapeDtypeStruct((3,) + part, s_in.dtype),
                   jax.ShapeDtypeStruct(part, s_in.dtype),
                   jax.ShapeDtypeStruct((N_DEV,) + small.shape, small.dtype)),
        scratch_shapes=[pltpu.VMEM((n_main, D), BF16), pltpu.SemaphoreType.DMA((2,)),
                        pltpu.VMEM((1, D), F32), pltpu.VMEM(small.shape, small.dtype)]
        + _PUSH_SEMS + _chip_reduce_scratch(*part, s_in.dtype),
        compiler_params=_cparams(("arbitrary",)),
    )(*pieces, dmlog, drank, wt, wr, x, dx2, norm_g, s_in, small)


def _wgrad_call(lhs_list, lhs_of_group, rhs_list, rhs_of_group, n_transposed, name, narrow=None):
    n_groups = len(rhs_of_group)
    n_tb, D, tb = lhs_list[0].shape
    T = n_tb * tb
    per = min(4, n_tb)
    tk = per * tb
    nk = T // tk
    nl = len(lhs_list)
    extra = [] if narrow is None else [narrow]

    def tokens_side_by_side(lref):
        return jnp.concatenate([lref[b] for b in range(per)], axis=1)

    def body(*refs):
        lhs = refs[:nl]
        rhs = refs[nl:nl + n_groups]
        rest = refs[nl + n_groups:]
        g = pl.program_id(0)
        i = pl.program_id(1)
        if narrow is None:
            out_ref, acc = rest
        else:
            narrow_ref, out_ref, narrow_out, acc, narrow_acc = rest

            @pl.when((g == 0) & (i == 0))
            def _():
                narrow_acc[...] = jnp.zeros_like(narrow_acc)

            @pl.when(g == 0)
            def _():
                narrow_acc[...] += _dot(tokens_side_by_side(lhs[lhs_of_group[0]]), narrow_ref[...])

            @pl.when((g == 0) & (i == nk - 1))
            def _():
                narrow_out[...] = _bf(narrow_acc[...].T)

        @pl.when(i == 0)
        def _():
            acc[...] = jnp.zeros_like(acc)

        for p in range(n_groups):
            @pl.when(g == p)
            def _(p=p):
                acc[...] += _dot(tokens_side_by_side(lhs[lhs_of_group[p]]), rhs[p][...])

        @pl.when((i == nk - 1) & (g < n_transposed))
        def _():
            out_ref[...] = _bf(acc[...].T)

        @pl.when((i == nk - 1) & (g >= n_transposed))
        def _():
            out_ref[...] = _bf(acc[...])

    def lhs_spec(a):
        groups = [g for g in range(n_groups) if lhs_of_group[g] == a]
        lo, hi = min(groups), max(groups)
        assert groups == list(range(lo, hi + 1))
        return pl.BlockSpec((per, D, tb), lambda g, i: (jnp.where((g >= lo) & (g <= hi), i, 0), 0, 0))

    def rhs_spec(p):
        cb = rhs_of_group[p][1]
        return pl.BlockSpec((tk, 1024), lambda g, i: (jnp.where(g == p, i, 0), cb))

    res = pl.pallas_call(
        body, name=name,
        grid=(n_groups, nk),
        in_specs=[lhs_spec(a) for a in range(nl)] + [rhs_spec(p) for p in range(n_groups)]
        + [pl.BlockSpec((tk, 128), lambda g, i: (jnp.where(g == 0, i, 0), 0)) for _ in extra],
        out_specs=[pl.BlockSpec((None, D, 1024), lambda g, i: (g, 0, 0))]
        + [pl.BlockSpec((128, D), lambda g, i: (0, 0)) for _ in extra],
        out_shape=[jax.ShapeDtypeStruct((n_groups, D, 1024), BF16)]
        + [jax.ShapeDtypeStruct((128, D), BF16) for _ in extra],
        scratch_shapes=[pltpu.VMEM((D, 1024), F32)] + [pltpu.VMEM((D, 128), F32) for _ in extra],
        compiler_params=_cparams(("arbitrary", "arbitrary")),
    )(*lhs_list, *[rhs_list[rhs_of_group[p][0]] for p in range(n_groups)], *extra)
    return res[0] if narrow is None else res


def _adamw_math(parts, w, m, v):
    g = parts[0].astype(F32)
    for p in parts[1:]:
        g = g + p.astype(F32)
    mm = ADAM_B1 * m + (1.0 - ADAM_B1) * g
    vv = ADAM_B2 * v + (1.0 - ADAM_B2) * (g * g)
    m_hat = mm / (1.0 - ADAM_B1 ** ADAM_STEP)
    v_hat = vv / (1.0 - ADAM_B2 ** ADAM_STEP)
    return g, -ADAM_LR * (m_hat / (jnp.sqrt(v_hat) + ADAM_EPS) + ADAM_WD * w), mm, vv


def _part_order(n_parts):
    return [n_parts - 1] + list(range(n_parts - 1))


def _adamw_call(parts, w, m, v, name):
    R, C = w.shape
    n_parts = parts.shape[0]
    (tr, tc), grid, idx = _tiling_2d(R, C, 512)

    def body(p_ref, w_ref, m_ref, v_ref, g_ref, d_ref, nm_ref, nv_ref):
        g_ref[...], d_ref[...], nm_ref[...], nv_ref[...] = _adamw_math(
            [p_ref[k] for k in _part_order(n_parts)], w_ref[...], m_ref[...], v_ref[...])

    blk = pl.BlockSpec((tr, tc), idx)
    sds = jax.ShapeDtypeStruct((R, C), F32)
    return pl.pallas_call(
        body, name=name,
        grid=grid,
        in_specs=[pl.BlockSpec((n_parts, tr, tc), lambda i: (0,) + idx(i)), blk, blk, blk],
        out_specs=(blk, blk, blk, blk),
        out_shape=(sds, sds, sds, sds),
        compiler_params=_cparams(("arbitrary",)),
    )(parts, w, m, v)


def _adamw_rows_call(parts, ws, ms, vs, name):
    n = len(ws)
    R, C = ws[0].shape
    n_parts = parts.shape[0]

    def body(*refs):
        p_ref = refs[0]
        w_refs, m_refs, v_refs = refs[1:1 + n], refs[1 + n:1 + 2 * n], refs[1 + 2 * n:1 + 3 * n]
        outs = refs[1 + 3 * n:]
        for k in range(n):
            @pl.when(pl.program_id(0) == k)
            def _(k=k):
                res = _adamw_math([p_ref[j] for j in _part_order(n_parts)],
                                  w_refs[k][...], m_refs[k][...], v_refs[k][...])
                for o_ref, val in zip(outs[4 * k:4 * k + 4], res):
                    o_ref[...] = val

    whole = pl.BlockSpec((R, C), lambda k: (0, 0))
    sds = jax.ShapeDtypeStruct((R, C), F32)
    res = pl.pallas_call(
        body, name=name,
        grid=(n,),
        in_specs=[pl.BlockSpec((n_parts, R, C), lambda k: (0, k, 0))] + [whole] * (3 * n),
        out_specs=tuple([whole] * (4 * n)),
        out_shape=tuple([sds] * (4 * n)),
        compiler_params=_cparams(("arbitrary",)),
    )(parts, *ws, *ms, *vs)
    return [res[4 * k:4 * k + 4] for k in range(n)]


def _adamw_lanes_call(parts, offsets, ws, ms, vs, name):
    n = len(ws)
    n_parts = parts.shape[0]

    def body(*refs):
        p_ref = refs[0]
        w_refs, m_refs, v_refs = refs[1:1 + n], refs[1 + n:1 + 2 * n], refs[1 + 2 * n:1 + 3 * n]
        outs = refs[1 + 3 * n:]
        for k in range(n):
            lanes = slice(offsets[k], offsets[k] + ws[k].shape[1])
            res = _adamw_math([p_ref[j, :, lanes] for j in _part_order(n_parts)],
                              w_refs[k][...], m_refs[k][...], v_refs[k][...])
            for o_ref, val in zip(outs[4 * k:4 * k + 4], res):
                o_ref[...] = val

    res = pl.pallas_call(
        body, name=name,
        out_shape=tuple(jax.ShapeDtypeStruct(ws[k].shape, F32) for k in range(n) for _ in range(4)),
        compiler_params=_cparams(),
    )(parts, *ws, *ms, *vs)
    return [res[4 * k:4 * k + 4] for k in range(n)]


def _local_step(x, target, wt, wr, wdec, bdec, wp_shard, norm_g, gla_g, b_gate, final_g):
    D = x.shape[1]
    half = wp_shard.shape[1] // 2
    projf, projb, rank, ht, wp_lo = _proj_call(x, norm_g, wt, wr, wp_shard[:, :half])
    o_gla, st_all, la = _gla_fwd_call(projf, projb, rank, wdec, bdec)
    o_sb, wp_hi = _sb_fwd_call(projb, wp_shard[:, half:])
    wp_full = jnp.concatenate([wp_lo, wp_hi], axis=2).transpose(1, 0, 2, 3).reshape(3, D, D)
    (dx2, do_gla, do_sb, dggate, dsgate, dmlog, mt, ogt, obt, dx2b, dya, dyb,
     dfinal_g, db_gate, dgla_g, loss) = _mid_call(o_gla, o_sb, projf, x, target, wp_full[0], wp_full[1],
                                                 wp_full[2], gla_g, b_gate, final_g)
    dw_p = _wgrad_call([ogt, obt, mt], [0, 1, 2], [dya, dyb, dx2b], [(0, 0), (1, 0), (2, 0)], 0, "wgrad_p")
    g_p = dw_p.reshape(3, N_DEV, D // N_DEV, D).transpose(1, 0, 2, 3).reshape(N_DEV, 3 * (D // N_DEV), D)
    dqk, dgv, drank, dwdec, dbdec = _gla_bwd_call(projf, projb, la, do_gla, st_all, rank, wdec)
    dsq, dsk, dsv, r_p = _sb_bwd_call(projb, do_sb, g_p)
    pieces = [dqk, dgv, dggate, dsq, dsk, dsv, dsgate]
    rhs_of_group = [(g, 0) for g in range(7)] + [(7, 0), (7, 1)]
    dw_in, dwr = _wgrad_call([ht], [0] * N_GROUPS, pieces + [dmlog], rhs_of_group, N_GROUPS, "wgrad_in",
                             narrow=drank)
    s_in = _pair_sum_call(dw_in.reshape(N_GROUPS * 1024, D), dwr)
    small = jnp.concatenate([
        jnp.zeros((D,), F32), dbdec.reshape(-1), dgla_g.reshape(-1), db_gate.reshape(-1), dfinal_g.reshape(-1),
        loss.reshape(-1), dwdec[:GLA_RANK].reshape(-1)]).reshape(1, _SM_LEN)
    grad_x, r_in, _, r_small = _dh_call(pieces, dmlog, drank, wt, wr, x, dx2, norm_g, s_in, small)
    return grad_x, r_in, r_p, r_small


_SM_NORM = 0
_SM_BDEC = _SM_NORM + D_MODEL
_SM_GLAG = _SM_BDEC + GLA_DK
_SM_BGATE = _SM_GLAG + GLA_HV
_SM_FINAL = _SM_BGATE + 2 * D_MODEL
_SM_REPL = _SM_FINAL + D_MODEL
_SM_LOSS = _SM_REPL
_SM_WDEC = _SM_LOSS + 128
_SM_LEN = _SM_WDEC + GLA_RANK * GLA_DK


def kernel(x, norm_g, w_in, w_dec_up, b_dec, gla_norm_g, w_pa, w_pb, b_gate, w_o, final_g, loss_target, m_norm_g, m_w_in, m_w_dec_up, m_b_dec, m_gla_norm_g, m_w_pa, m_w_pb, m_b_gate, m_w_o, m_final_g, v_norm_g, v_w_in, v_w_dec_up, v_b_dec, v_gla_norm_g, v_w_pa, v_w_pb, v_b_gate, v_w_o, v_final_g):
    D = D_MODEL
    me = 4 * lax.axis_index("x") + 2 * lax.axis_index("y") + lax.axis_index("c")

    wp_shard = jnp.stack([w_pa, w_pb, w_o]).astype(BF16)
    n_first = _half_rows(SHARD_COLS)
    win_all, wdec_all = _all_gather([w_in.T.astype(BF16), w_dec_up], "gather_w",
                                    row_pieces=[[(0, n_first), (n_first, SHARD_COLS - n_first)], None])
    wt = _flatten_blocks_call(win_all)
    wr = jnp.pad(wt[RANK_COL:RANK_COL + GLA_RANK], ((0, 128 - GLA_RANK), (0, 0)))
    wdec_full = wdec_all.transpose(1, 0, 2).reshape(GLA_RANK, GLA_DK)
    wdec = jnp.pad(wdec_full, ((0, 128 - GLA_RANK), (0, 0)))

    grad_x, r_in, r_p, r_small = _local_step(
        x[0], loss_target[0], wt, wr, wdec, b_dec.reshape(1, -1), wp_shard,
        norm_g.reshape(1, -1), gla_norm_g.reshape(1, -1), b_gate.reshape(1, -1), final_g.reshape(1, -1))

    gw_in, d_in, nm_in, nv_in = (a.T for a in _adamw_call(r_in, w_in.T, m_w_in.T, v_w_in.T, "adamw_in"))
    (g_pa, d_pa, nm_pa, nv_pa), (g_pb, d_pb, nm_pb, nv_pb), (g_o, d_o, nm_o, nv_o) = _adamw_rows_call(
        r_p, [w_pa, w_pb, w_o], [m_w_pa, m_w_pb, m_w_o], [v_w_pa, v_w_pb, v_w_o], "adamw_p")

    def row(a):
        return a.reshape(1, -1)

    rep = _adamw_lanes_call(
        r_small, [_SM_NORM, _SM_BDEC, _SM_GLAG, _SM_BGATE, _SM_FINAL],
        [row(a) for a in (norm_g, b_dec, gla_norm_g, b_gate, final_g)],
        [row(a) for a in (m_norm_g, m_b_dec, m_gla_norm_g, m_b_gate, m_final_g)],
        [row(a) for a in (v_norm_g, v_b_dec, v_gla_norm_g, v_b_gate, v_final_g)], "adamw_rep")
    ((g_norm, d_norm, nm_norm, nv_norm), (g_bdec, d_bdec, nm_bdec, nv_bdec), (g_glag, d_glag, nm_glag, nv_glag),
     (g_bgate, d_bgate, nm_bgate, nv_bgate), (g_final, d_final, nm_final, nv_final)) = [
        tuple(a.reshape(-1) for a in quad) for quad in rep]

    wdec_parts = r_small[:, 0, _SM_WDEC:].reshape(N_DEV, GLA_RANK, GLA_DK)
    cols = GLA_DK // N_DEV
    wdec_mine = lax.dynamic_slice_in_dim(wdec_parts, me * cols, cols, axis=2)
    g_wdec, d_wdec, nm_wdec, nv_wdec = _adamw_call(wdec_mine, w_dec_up, m_w_dec_up, v_w_dec_up, "adamw_dec")

    loss_total = jnp.sum(r_small[:, 0, _SM_LOSS])

    return (loss_total, grad_x[None],
            g_norm, gw_in, g_wdec, g_bdec, g_glag, g_pa, g_pb, g_bgate, g_o, g_final,
            d_norm, d_in, d_wdec, d_bdec, d_glag, d_pa, d_pb, d_bgate, d_o, d_final,
            nm_norm, nm_in, nm_wdec, nm_bdec, nm_glag, nm_pa, nm_pb, nm_bgate, nm_o, nm_final,
            nv_norm, nv_in, nv_wdec, nv_bdec, nv_glag, nv_pa, nv_pb, nv_bgate, nv_o, nv_final)
```

```python
import math

import jax
import jax.numpy as jnp
from jax import lax
from jax.experimental import pallas as pl
from jax.experimental.pallas import tpu as pltpu

F32 = jnp.float32
BF16 = jnp.bfloat16

N_DEV = 8
D_MODEL = 1024
GLA_HEADS = 4
GLA_HK = 128
GLA_HV = 256
GLA_DK = 512
GLA_RANK = 16
GLA_TAU = 16.0
GLA_CHUNK = 64
SB_HEADS = 8
SB_HD = 128
EPS = 1e-6
N_GROUPS = 9
RANK_COL = 3072
IN_COLS = 9232
SHARD_COLS = IN_COLS // N_DEV

ADAM_LR = 0.001
ADAM_B1 = 0.9
ADAM_B2 = 0.999
ADAM_EPS = 1e-08
ADAM_WD = 0.01
ADAM_STEP = 10

VMEM_LIMIT = 56 * 1024 * 1024
TBLK = 256


def _cparams(sem=None):
    return pltpu.CompilerParams(dimension_semantics=sem, vmem_limit_bytes=VMEM_LIMIT)


def _tiling_2d(rows, cols, band_cols):
    if rows * cols <= 128 * 1024:
        return (rows, cols), (1,), lambda i: (0, 0)
    if rows % 128 == 0:
        return (128, cols), (rows // 128,), lambda i: (i, 0)
    tc = band_cols if cols % band_cols == 0 else cols
    return (rows, tc), (cols // tc,), lambda i: (0, i)


def _dot(a, b):
    return jnp.dot(a, b, preferred_element_type=F32)


def _dot_nt(a, b):
    return lax.dot_general(a, b, (((1,), (1,)), ((), ())), preferred_element_type=F32)


def _dot_tn(a, b):
    return lax.dot_general(a, b, (((0,), (0,)), ((), ())), preferred_element_type=F32)


def _bf(x):
    return x.astype(BF16)


def _split3(x):
    hi = x.astype(BF16)
    r = x - hi.astype(F32)
    mid = r.astype(BF16)
    lo = (r - mid.astype(F32)).astype(BF16)
    return hi, mid, lo


def _tri_left(tri, x):
    hi, mid, lo = _split3(x)
    return _dot(tri, hi) + _dot(tri, mid) + _dot(tri, lo)


def _split2(x):
    hi = lax.bitcast_convert_type(lax.bitcast_convert_type(x, jnp.uint32) & jnp.uint32(0xFFFF0000), F32)
    return hi.astype(BF16), (x - hi).astype(BF16)


def _tri2_left(tri, x):
    hi, lo = _split2(x)
    return _dot(tri, hi) + _dot(tri, lo)


def _tri2_right(x, tri):
    hi, lo = _split2(x)
    return _dot(hi, tri) + _dot(lo, tri)


def _iota2(n, m, dim):
    return lax.broadcasted_iota(jnp.int32, (n, m), dim)


def _sigmoid(x):
    return 1.0 / (1.0 + jnp.exp(-x))


def _softplus_neg_abs(z):
    return jnp.log(1.0 + jnp.exp(-jnp.abs(z)))


_ANY = pl.BlockSpec(memory_space=pl.ANY)


def _mesh_pos():
    return lax.axis_index("x"), lax.axis_index("y"), lax.axis_index("c")


def _other_chips(x, y):
    return [(1 - x, y), (x, 1 - y), (1 - x, 1 - y)]


def _rcopy(src, dst, send_sem, recv_sem, to):
    return pltpu.make_async_remote_copy(src_ref=src, dst_ref=dst, send_sem=send_sem, recv_sem=recv_sem,
                                        device_id=to, device_id_type=pl.DeviceIdType.MESH)


def _push_copies(src_ref, dst_ref, send_sems, recv_sems, loc_sem, scatter):
    x, y, c = _mesh_pos()
    me = 4 * x + 2 * y + c
    own = pltpu.make_async_copy(src_ref.at[me] if scatter else src_ref, dst_ref.at[me], loc_sem)
    pairs = []
    for k in range(1, N_DEV):
        px = 1 - x if k & 4 else x
        py = 1 - y if k & 2 else y
        pc = 1 - c if k & 1 else c
        pid = 4 * px + 2 * py + pc
        src = src_ref.at[pid] if scatter else src_ref
        send = _rcopy(src, dst_ref.at[me], send_sems.at[k - 1], recv_sems.at[k - 1], (px, py, pc))
        recv = _rcopy(src, dst_ref.at[pid], send_sems.at[k - 1], recv_sems.at[k - 1], (px, py, pc))
        pairs.append((send, recv))
    return own, pairs


def _push_start(own, pairs):
    own.start()
    for send, _ in pairs:
        send.start()


def _push_wait(own, pairs):
    for _, recv in pairs:
        recv.wait_recv()
    for send, _ in pairs:
        send.wait_send()
    own.wait()


_PUSH_SEMS = [pltpu.SemaphoreType.DMA((N_DEV - 1,)), pltpu.SemaphoreType.DMA((N_DEV - 1,)),
              pltpu.SemaphoreType.DMA]


def _half_rows(rows):
    return (rows // 2) // 16 * 16


_ADD_ROWS = 128


def _chip_reduce_steps(src_ref, dst_ref, relayed_ref, sum_x, sum_y, rel_x, rel_y, load_sems, send_sems, recv_sems,
                       loc_sem):
    _, R, C = src_ref.shape
    n0 = _half_rows(R)
    lo, hi = pl.ds(0, n0), pl.ds(n0, R - n0)
    x, y, c = _mesh_pos()
    (xx, xy), (yx, yy), (dx, dy) = _other_chips(x, y)
    to_diag, to_x, to_y = src_ref.at[2 * dx + dy], src_ref.at[2 * xx + xy], src_ref.at[2 * yx + yy]
    x_nb, y_nb = (xx, xy, c), (yx, yy, c)
    relays = (_rcopy(to_diag.at[lo], relayed_ref.at[lo], send_sems.at[0], recv_sems.at[0], x_nb),
              _rcopy(to_diag.at[hi], relayed_ref.at[hi], send_sems.at[1], recv_sems.at[1], y_nb))
    plain = (_rcopy(to_x.at[lo], dst_ref.at[0, lo], send_sems.at[2], recv_sems.at[2], x_nb),
             _rcopy(to_y.at[hi], dst_ref.at[1, hi], send_sems.at[3], recv_sems.at[3], y_nb))
    summed = (_rcopy(sum_x, dst_ref.at[0, hi], send_sems.at[4], recv_sems.at[4], x_nb),
              _rcopy(sum_y, dst_ref.at[1, lo], send_sems.at[5], recv_sems.at[5], y_nb))
    load_mine = (pltpu.make_async_copy(to_x.at[hi], sum_x, load_sems.at[0]),
                 pltpu.make_async_copy(to_y.at[lo], sum_y, load_sems.at[1]))
    load_relayed = (pltpu.make_async_copy(relayed_ref.at[hi], rel_x, load_sems.at[2]),
                    pltpu.make_async_copy(relayed_ref.at[lo], rel_y, load_sems.at[3]))
    own = pltpu.make_async_copy(src_ref.at[2 * x + y], dst_ref.at[2], loc_sem)

    def start():
        for cp in relays + plain + (own,) + load_mine:
            cp.start()

    def add(acc_ref, rel_ref):
        for r0 in range(0, acc_ref.shape[0], _ADD_ROWS):
            rows = slice(r0, min(r0 + _ADD_ROWS, acc_ref.shape[0]))
            acc_ref[rows, :] = (acc_ref[rows, :].astype(F32) + rel_ref[rows, :].astype(F32)).astype(acc_ref.dtype)

    def forward():
        for cp in relays:
            cp.wait_recv()
        for cp in load_relayed:
            cp.start()
        for cp in load_mine + load_relayed:
            cp.wait()
        add(sum_x, rel_x)
        add(sum_y, rel_y)
        for cp in summed:
            cp.start()

    def finish():
        for cp in plain + summed:
            cp.wait_recv()
        for cp in relays + plain + summed:
            cp.wait_send()
        own.wait()

    return start, forward, finish


def _chip_reduce_scratch(rows, cols, dtype):
    n0 = _half_rows(rows)
    return [pltpu.VMEM((rows - n0, cols), dtype), pltpu.VMEM((n0, cols), dtype)] * 2 + [
        pltpu.SemaphoreType.DMA((4,)), pltpu.SemaphoreType.DMA((6,)), pltpu.SemaphoreType.DMA((6,)),
        pltpu.SemaphoreType.DMA]


def _all_gather(arrs, name, row_pieces=None):
    n = len(arrs)
    pieces = [[None] if not row_pieces or not row_pieces[a] else list(row_pieces[a]) for a in range(n)]
    assert all(len(p) in (1, 2) for p in pieces)
    units = [(a, i) for a in range(n) for i in range(len(pieces[a]))]

    def body(*refs):
        ins = refs[:n]
        outs = refs[n:2 * n]
        send_sems, recv_sems, loc_sems = refs[2 * n:]
        x, y, c = _mesh_pos()
        me, sib = (x, y, c), (x, y, 1 - c)
        xn, yn, dg = [(px, py, c) for px, py in _other_chips(x, y)]

        def rows(ref, a, i):
            return ref if pieces[a][i] is None else ref.at[pl.ds(*pieces[a][i])]

        def copy(u, k, block, to, own=False):
            a, i = u
            px, py, pc = block
            dst = rows(outs[a].at[4 * px + 2 * py + pc], a, i)
            return _rcopy(rows(ins[a], a, i) if own else dst, dst, send_sems.at[a, k, i], recv_sems.at[a, k, i], to)

        started = []

        def start(cp):
            cp.start()
            started.append(cp)

        def landed_then_pass_on(u, k, block):
            copy(u, k, block, me).wait_recv()
            start(copy(u, 3 + k, block, sib))

        mine = [pltpu.make_async_copy(ins[a], outs[a].at[4 * x + 2 * y + c], loc_sems.at[a]) for a in range(n)]
        for cp in mine:
            cp.start()
        for u in units:
            start(copy(u, 0, me, sib, own=True))
        for a in range(n):
            if len(pieces[a]) == 2:
                for i, to, k in ((0, xn, 1), (1, yn, 2), (1, xn, 1), (0, yn, 2)):
                    start(copy((a, i), k, me, to, own=True))
            else:
                for to, k in ((xn, 1), (yn, 2), (dg, 3)):
                    start(copy((a, 0), k, me, to, own=True))
        for a in range(n):
            if len(pieces[a]) == 2:
                landed_then_pass_on((a, 0), 1, xn)
                start(copy((a, 0), 3, xn, yn))
                landed_then_pass_on((a, 1), 2, yn)
                start(copy((a, 1), 3, yn, xn))
                landed_then_pass_on((a, 1), 1, xn)
                landed_then_pass_on((a, 0), 2, yn)
                landed_then_pass_on((a, 0), 3, dg)
                landed_then_pass_on((a, 1), 3, dg)
            else:
                for block, k in ((xn, 1), (yn, 2), (dg, 3)):
                    landed_then_pass_on((a, 0), k, block)
        for u in units:
            copy(u, 0, sib, me).wait_recv()
            for k, (px, py, _) in ((4, xn), (5, yn), (6, dg)):
                copy(u, k, (px, py, 1 - c), me).wait_recv()
        for cp in started:
            cp.wait_send()
        for cp in mine:
            cp.wait()

    n_pc = max(len(p) for p in pieces)

    return pl.pallas_call(
        body, name=name,
        out_shape=tuple(jax.ShapeDtypeStruct((N_DEV,) + a.shape, a.dtype) for a in arrs),
        in_specs=[_ANY] * n,
        out_specs=tuple([_ANY] * n),
        scratch_shapes=[pltpu.SemaphoreType.DMA((n, 7, n_pc)), pltpu.SemaphoreType.DMA((n, 7, n_pc)),
                        pltpu.SemaphoreType.DMA((n,))],
    )(*arrs)


_PARTS_BANDS = 8


def _flatten_blocks_call(blocks):
    n, R, C = blocks.shape
    tc = C // _PARTS_BANDS

    def body(in_ref, out_ref):
        for p in range(n):
            out_ref[p * R:(p + 1) * R, :] = in_ref[p]

    return pl.pallas_call(
        body, name="flatten_w",
        grid=(C // tc,),
        in_specs=[pl.BlockSpec((n, R, tc), lambda i: (0, 0, i))],
        out_specs=pl.BlockSpec((n * R, tc), lambda i: (0, i)),
        out_shape=jax.ShapeDtypeStruct((n * R, C), blocks.dtype),
        compiler_params=_cparams(("arbitrary",)),
    )(blocks)


def _pair_sum_call(dmain, drank):
    D = dmain.shape[1]
    n = _PARTS_BANDS
    tc = D // n

    def body(dm_ref, dr_ref, sum_ref, laid, got, send_sems, recv_sems):
        x, y, c = _mesh_pos()

        def pushes(k):
            return [_rcopy(laid.at[k % 2, 2 * q + (1 - c)], got.at[k, q], send_sems.at[k, q], recv_sems.at[k, q],
                           (x, y, 1 - c)) for q in range(4)]

        def lay_out(k):
            for p in range(N_DEV):
                lo, hi = p * SHARD_COLS, (p + 1) * SHARD_COLS
                at = 0
                for src, a, b in ((dm_ref, lo, min(hi, RANK_COL)),
                                  (dr_ref, max(lo, RANK_COL) - RANK_COL, min(hi, RANK_COL + GLA_RANK) - RANK_COL),
                                  (dm_ref, max(lo, RANK_COL + GLA_RANK) - GLA_RANK, hi - GLA_RANK)):
                    if b > a:
                        laid[k % 2, p, at:at + (b - a), :] = src[a:b, :]
                        at += b - a

        for k in range(n + 1):
            @pl.when(pl.program_id(0) == k)
            def _(k=k):
                if k < n:
                    if k >= 2:
                        for cp in pushes(k - 2):
                            cp.wait_send()
                    lay_out(k)
                    for cp in pushes(k):
                        cp.start()
                if k >= 1:
                    for cp in pushes(k - 1):
                        cp.wait_recv()
                    for q in range(4):
                        sum_ref[q] = (laid[(k - 1) % 2, 2 * q + c].astype(F32)
                                      + got[k - 1, q].astype(F32)).astype(sum_ref.dtype)
                if k == n:
                    for k_open in range(max(0, n - 2), n):
                        for cp in pushes(k_open):
                            cp.wait_send()

    sems = pltpu.SemaphoreType.DMA((n, 4))
    return pl.pallas_call(
        body, name="pair_sum",
        grid=(n + 1,),
        in_specs=[pl.BlockSpec((dmain.shape[0], tc), lambda k: (0, jnp.minimum(k, n - 1))),
                  pl.BlockSpec((GLA_RANK, tc), lambda k: (0, jnp.minimum(k, n - 1)))],
        out_specs=pl.BlockSpec((4, SHARD_COLS, tc), lambda k: (0, 0, jnp.maximum(k - 1, 0))),
        out_shape=jax.ShapeDtypeStruct((4, SHARD_COLS, D), dmain.dtype),
        scratch_shapes=[pltpu.VMEM((2, N_DEV, SHARD_COLS, tc), dmain.dtype),
                        pltpu.VMEM((n, 4, SHARD_COLS, tc), dmain.dtype), sems, sems],
        compiler_params=_cparams(("arbitrary",)),
    )(dmain, drank)


def _group_row(g):
    return GLA_RANK * (g * (1024 // GLA_RANK) + (g >= RANK_COL // 1024))


def _proj_call(x, norm_g, wt, wr, wp_part):
    T, D = x.shape
    tm = min(1024, T)
    assert tm % TBLK == 0
    n_i = T // tm

    def f_slot(j):
        return ((j >= 2).astype(jnp.int32) + (j >= 6).astype(jnp.int32)
                + (j >= 7).astype(jnp.int32) + (j >= 8).astype(jnp.int32))

    def b_slot(j):
        return (j >= 3).astype(jnp.int32) + (j >= 4).astype(jnp.int32) + (j >= 5).astype(jnp.int32)

    def body(x_ref, g_ref, w_ref, wr_ref, wp_ref, pf_ref, pb_ref, rank_ref, ht_ref, wpall_ref,
             h_scr, send_sems, recv_sems, loc_sem):
        i = pl.program_id(0)
        j = pl.program_id(1)
        own, pairs = _push_copies(wp_ref, wpall_ref, send_sems, recv_sems, loc_sem, scatter=False)

        @pl.when((i == 0) & (j == 0))
        def _():
            _push_start(own, pairs)

        @pl.when(j == 0)
        def _():
            xv = x_ref[...]
            r = lax.rsqrt(jnp.mean(xv * xv, axis=-1, keepdims=True) + EPS)
            h = (xv * r) * g_ref[...]
            hb = _bf(h)
            h_scr[...] = hb
            for b in range(tm // TBLK):
                ht_ref[b] = _bf(h[b * TBLK:(b + 1) * TBLK].T)
            rank_ref[...] = _dot_nt(hb, wr_ref[...])

        is_b = (j == 1) | ((j >= 3) & (j <= 5))

        @pl.when(is_b)
        def _():
            pb_ref[...] = _bf(_dot_nt(h_scr[...], w_ref[...]))

        @pl.when(jnp.logical_not(is_b))
        def _():
            pf_ref[...] = _dot_nt(h_scr[...], w_ref[...])

        @pl.when((i == n_i - 1) & (j == N_GROUPS - 1))
        def _():
            _push_wait(own, pairs)

    return pl.pallas_call(
        body, name="proj",
        grid=(n_i, N_GROUPS),
        in_specs=[pl.BlockSpec((tm, D), lambda i, j: (i, 0)),
                  pl.BlockSpec((1, D), lambda i, j: (0, 0)),
                  pl.BlockSpec((pl.Element(1024), pl.Element(D)), lambda i, j: (_group_row(j), 0)),
                  pl.BlockSpec((128, D), lambda i, j: (0, 0)),
                  _ANY],
        out_specs=(pl.BlockSpec((None, tm, 1024), lambda i, j: (f_slot(j), i, 0)),
                   pl.BlockSpec((None, tm, 1024), lambda i, j: (b_slot(j), i, 0)),
                   pl.BlockSpec((tm, 128), lambda i, j: (i, 0)),
                   pl.BlockSpec((tm // TBLK, D, TBLK), lambda i, j: (i, 0, 0)),
                   _ANY),
        out_shape=(jax.ShapeDtypeStruct((5, T, 1024), F32),
                   jax.ShapeDtypeStruct((4, T, 1024), BF16),
                   jax.ShapeDtypeStruct((T, 128), F32),
                   jax.ShapeDtypeStruct((T // TBLK, D, TBLK), BF16),
                   jax.ShapeDtypeStruct((N_DEV,) + wp_part.shape, wp_part.dtype)),
        scratch_shapes=[pltpu.VMEM((tm, D), BF16)] + _PUSH_SEMS,
        compiler_params=_cparams(("arbitrary", "arbitrary")),
    )(x, norm_g, wt, wr, wp_part)


GLA_STEP_CHUNKS = 4


def _gla_same_chunk(rows):
    return (_iota2(rows, rows, 0) & -GLA_CHUNK) == (_iota2(rows, rows, 1) & -GLA_CHUNK)


def _gla_chunk_terms(la, q, k, n_c):
    C = GLA_CHUNK
    rows = n_c * C
    low = _gla_same_chunk(rows) & (_iota2(rows, rows, 0) >= _iota2(rows, rows, 1))
    b = _tri_left(_bf(low.astype(F32)), la)
    bl = [b[(c + 1) * C - 1:(c + 1) * C, :] for c in range(n_c)]
    bl_rows = jnp.concatenate([jnp.broadcast_to(bl[c], (C, b.shape[1])) for c in range(n_c)], axis=0)
    eb = jnp.exp(b)
    enb = jnp.exp(-b)
    ebl_b = jnp.exp(bl_rows - b)
    scale = GLA_HK ** -0.5
    qe = q * eb * scale
    ke = k * enb
    kd = k * ebl_b
    return bl, eb, enb, ebl_b, qe, ke, kd


def _gla_fwd_call(projf, projb, rank, wdec, bdec):
    T = projf.shape[1]
    C = GLA_CHUNK
    n_chunks = T // C
    n_c = GLA_STEP_CHUNKS
    R = n_c * C
    assert n_chunks % n_c == 0

    def body(qk_ref, v_ref, rank_ref, wd_ref, bd_ref, o_ref, st_ref, la_ref, st_scr):
        @pl.when(pl.program_id(0) == 0)
        def _():
            st_scr[...] = jnp.zeros_like(st_scr)

        dec = _dot(_bf(rank_ref[...]), _bf(wd_ref[...])) + bd_ref[...]
        la = (jnp.minimum(dec, 0.0) - _softplus_neg_abs(dec)) / GLA_TAU
        la_ref[...] = la
        mask = _gla_same_chunk(R) & (_iota2(R, R, 0) >= _iota2(R, R, 1))
        bl, _, _, _, qe, ke, kd = _gla_chunk_terms(la, qk_ref[:, :GLA_DK], qk_ref[:, GLA_DK:], n_c)
        qeb, keb, kdb = _bf(qe), _bf(ke), _bf(kd)
        ebl = [jnp.exp(bl[c]) for c in range(n_c)]
        heads = range(GLA_HEADS)
        ks = [slice(hh * GLA_HK, (hh + 1) * GLA_HK) for hh in heads]
        vs = [slice(hh * GLA_HV, (hh + 1) * GLA_HV) for hh in heads]
        rs = [slice(c * C, (c + 1) * C) for c in range(n_c)]
        p = [_bf(jnp.where(mask, _dot_nt(qeb[:, ks[hh]], keb[:, ks[hh]]), 0.0)) for hh in heads]
        upd = [[_dot_tn(v_ref[rs[c], vs[hh]], kdb[rs[c], ks[hh]]) for hh in heads] for c in range(n_c)]
        intra = [_dot(p[hh], v_ref[:, vs[hh]]) for hh in heads]
        st = [st_scr[hh] for hh in heads]
        for c in range(n_c):
            inter = [_dot_nt(qeb[rs[c], ks[hh]], _bf(st[hh])) for hh in heads]
            for hh in heads:
                st_ref[c, hh] = st[hh]
                o_ref[rs[c], vs[hh]] = intra[hh][rs[c]] + inter[hh]
            st = [st[hh] * ebl[c][:, ks[hh]] + upd[c][hh] for hh in heads]
        for hh in heads:
            st_scr[hh] = st[hh]

    return pl.pallas_call(
        body, name="gla_fwd",
        grid=(n_chunks // n_c,),
        in_specs=[pl.BlockSpec((None, R, 1024), lambda n: (0, n, 0)),
                  pl.BlockSpec((None, R, 1024), lambda n: (0, n, 0)),
                  pl.BlockSpec((R, 128), lambda n: (n, 0)),
                  pl.BlockSpec((128, GLA_DK), lambda n: (0, 0)),
                  pl.BlockSpec((1, GLA_DK), lambda n: (0, 0))],
        out_specs=(pl.BlockSpec((R, 1024), lambda n: (n, 0)),
                   pl.BlockSpec((n_c, GLA_HEADS, GLA_HV, GLA_HK), lambda n: (n, 0, 0, 0)),
                   pl.BlockSpec((R, GLA_DK), lambda n: (n, 0))),
        out_shape=(jax.ShapeDtypeStruct((T, 1024), F32),
                   jax.ShapeDtypeStruct((n_chunks, GLA_HEADS, GLA_HV, GLA_HK), F32),
                   jax.ShapeDtypeStruct((T, GLA_DK), F32)),
        scratch_shapes=[pltpu.VMEM((GLA_HEADS, GLA_HV, GLA_HK), F32)],
        compiler_params=_cparams(("arbitrary",)),
    )(projf, projb, rank, wdec, bdec)


def _gla_bwd_call(projf, projb, la, do_gla, st_all, rank, wdec):
    T = projf.shape[1]
    C = GLA_CHUNK
    n_chunks = T // C
    n_c = GLA_STEP_CHUNKS
    R = n_c * C
    assert n_chunks % n_c == 0
    last = n_chunks // n_c - 1

    def body(qk_ref, v_ref, la_ref, do_ref, st_ref, rank_ref, wd_ref,
             dqk_ref, dv_ref, drank_ref, dwd_ref, dbd_ref, dst_scr):
        @pl.when(pl.program_id(0) == 0)
        def _():
            dst_scr[...] = jnp.zeros_like(dst_scr)
            dwd_ref[...] = jnp.zeros_like(dwd_ref)
            dbd_ref[...] = jnp.zeros_like(dbd_ref)

        same = _gla_same_chunk(R)
        mask = same & (_iota2(R, R, 0) >= _iota2(R, R, 1))
        upp = _bf((same & (_iota2(R, R, 0) <= _iota2(R, R, 1))).astype(F32))
        scale = GLA_HK ** -0.5
        la = la_ref[...]
        bl, eb, enb, ebl_b, qe, ke, kd = _gla_chunk_terms(la, qk_ref[:, :GLA_DK], qk_ref[:, GLA_DK:], n_c)
        qeb, keb, kdb = _bf(qe), _bf(ke), _bf(kd)
        ebl = [jnp.exp(bl[c]) for c in range(n_c)]
        heads = range(GLA_HEADS)
        ks = [slice(hh * GLA_HK, (hh + 1) * GLA_HK) for hh in heads]
        vs = [slice(hh * GLA_HV, (hh + 1) * GLA_HV) for hh in heads]
        rs = [slice(c * C, (c + 1) * C) for c in range(n_c)]
        v = [v_ref[:, vs[hh]] for hh in heads]
        do = [_bf(do_ref[:, vs[hh]]) for hh in heads]
        p = [_bf(jnp.where(mask, _dot_nt(qeb[:, ks[hh]], keb[:, ks[hh]]), 0.0)) for hh in heads]
        dp = [_bf(jnp.where(mask, _dot_nt(do[hh], v[hh]), 0.0)) for hh in heads]
        dst_intra = [[_dot_tn(do[hh][rs[c]], qeb[rs[c], ks[hh]]) for hh in heads] for c in range(n_c)]
        dqe_inter = [[_dot(do[hh][rs[c]], _bf(st_ref[c, hh])) for hh in heads] for c in range(n_c)]
        dv_intra = [_dot_tn(p[hh], do[hh]) for hh in heads]
        dqe_intra = [_dot(dp[hh], keb[:, ks[hh]]) for hh in heads]
        dke = jnp.concatenate([_dot_tn(dp[hh], qeb[:, ks[hh]]) for hh in heads], axis=1)
        dstn = [dst_scr[hh] for hh in heads]
        dkd_c, dv_inter, debl = [None] * n_c, [None] * n_c, [None] * n_c
        for c in reversed(range(n_c)):
            dstnb = [_bf(dstn[hh]) for hh in heads]
            dkd_c[c] = jnp.concatenate([_dot(v[hh][rs[c]], dstnb[hh]) for hh in heads], axis=1)
            dv_inter[c] = [_dot_nt(kdb[rs[c], ks[hh]], dstnb[hh]) for hh in heads]
            debl[c] = jnp.concatenate(
                [jnp.sum(dstn[hh] * st_ref[c, hh], axis=0, keepdims=True) for hh in heads], axis=1)
            dstn = [dst_intra[c][hh] + dstn[hh] * ebl[c][:, ks[hh]] for hh in heads]
        for hh in heads:
            dst_scr[hh] = dstn[hh]
            dv_ref[:, vs[hh]] = _bf(dv_intra[hh] + jnp.concatenate([dv_inter[c][hh] for c in range(n_c)], axis=0))
        dqe = jnp.concatenate(
            [dqe_intra[hh] + jnp.concatenate([dqe_inter[c][hh] for c in range(n_c)], axis=0) for hh in heads], axis=1)
        dkd = jnp.concatenate(dkd_c, axis=0)
        dkd_kd = dkd * kd
        db = dqe * qe - dke * ke - dkd_kd
        dbl = jnp.concatenate(
            [jnp.broadcast_to(jnp.sum(dkd_kd[rs[c]], axis=0, keepdims=True) + ebl[c] * debl[c], (C, GLA_DK))
             for c in range(n_c)], axis=0)
        dla = _tri_left(upp, db) + dbl
        dqk_ref[:, :GLA_DK] = _bf(dqe * eb * scale)
        dqk_ref[:, GLA_DK:] = _bf(dke * enb + dkd * ebl_b)
        ddec = dla * (1.0 / GLA_TAU) * (1.0 - jnp.exp(GLA_TAU * la))
        ddecb = _bf(ddec)
        drank_ref[...] = _bf(_dot_nt(ddecb, _bf(wd_ref[...])))
        dwd_ref[...] += _dot_tn(_bf(rank_ref[...]), ddecb)
        dbd_ref[...] += jnp.sum(ddec, axis=0, keepdims=True)

    return pl.pallas_call(
        body, name="gla_bwd",
        grid=(n_chunks // n_c,),
        in_specs=[pl.BlockSpec((None, R, 1024), lambda n: (0, last - n, 0)),
                  pl.BlockSpec((None, R, 1024), lambda n: (0, last - n, 0)),
                  pl.BlockSpec((R, GLA_DK), lambda n: (last - n, 0)),
                  pl.BlockSpec((R, 1024), lambda n: (last - n, 0)),
                  pl.BlockSpec((n_c, GLA_HEADS, GLA_HV, GLA_HK), lambda n: (last - n, 0, 0, 0)),
                  pl.BlockSpec((R, 128), lambda n: (last - n, 0)),
                  pl.BlockSpec((128, GLA_DK), lambda n: (0, 0))],
        out_specs=(pl.BlockSpec((R, 1024), lambda n: (last - n, 0)),
                   pl.BlockSpec((R, 1024), lambda n: (last - n, 0)),
                   pl.BlockSpec((R, 128), lambda n: (last - n, 0)),
                   pl.BlockSpec((128, GLA_DK), lambda n: (0, 0)),
                   pl.BlockSpec((1, GLA_DK), lambda n: (0, 0))),
        out_shape=(jax.ShapeDtypeStruct((T, 1024), BF16),
                   jax.ShapeDtypeStruct((T, 1024), BF16),
                   jax.ShapeDtypeStruct((T, 128), BF16),
                   jax.ShapeDtypeStruct((128, GLA_DK), F32),
                   jax.ShapeDtypeStruct((1, GLA_DK), F32)),
        scratch_shapes=[pltpu.VMEM((GLA_HEADS, GLA_HV, GLA_HK), F32)],
        compiler_params=_cparams(("arbitrary",)),
    )(projf, projb, la, do_gla, st_all, rank, wdec)


def _sb_logs(z):
    lsz = jnp.minimum(z, 0.0) - _softplus_neg_abs(z)
    return lsz, lsz - z


SB_HG_FWD = 8
SB_HG_BWD = 4
SB_QUERIES = 256
SB_KEYS = 256
SB_DEAD = -105.0


def _sb_fwd_call(projb, wp_shard):
    T = projb.shape[1]
    B = min(SB_QUERIES, T)
    HG = SB_HG_FWD
    W = HG * SB_HD
    scale = 1.0 / math.sqrt(SB_HD)
    KB = min(SB_KEYS, T)
    n_h, n_i = SB_HEADS // HG, T // B

    def body(q_ref, k_ref, v_ref, wp_ref, o_ref, wpall_ref, cb_scr, send_sems, recv_sems, loc_sem):
        i = pl.program_id(1)
        own, pairs = _push_copies(wp_ref, wpall_ref, send_sems, recv_sems, loc_sem, scatter=False)

        @pl.when((pl.program_id(0) == 0) & (i == 0))
        def _():
            _push_start(own, pairs)

        rows = HG * B
        after = (_iota2(KB, KB, 0) > _iota2(KB, KB, 1)).astype(F32)
        tri = _bf(jnp.concatenate([after, jnp.ones((KB, KB), F32)], axis=1))
        o_ref[...] = jnp.zeros_like(o_ref)
        cb_scr[...] = jnp.zeros_like(cb_scr)

        def block(jp, masked):
            off = pl.multiple_of(jp * KB, KB)
            z = jnp.concatenate(
                [_dot_nt(q_ref[:, hh * SB_HD:(hh + 1) * SB_HD], k_ref[pl.ds(off, KB), hh * SB_HD:(hh + 1) * SB_HD])
                 for hh in range(HG)], axis=0) * scale
            lsz, l1m = _sb_logs(z)
            if masked:
                strict = (jp * KB + _iota2(rows, KB, 1)) < (i * B + (_iota2(rows, KB, 0) & (B - 1)))
                l1m = jnp.where(strict, l1m, 0.0)
            r = _tri2_right(l1m, tri)
            cb = cb_scr[...]
            a = jnp.exp(lsz + cb + r[:, :KB])
            if masked:
                a = jnp.where(strict, a, 0.0)
            cb_scr[...] = cb + r[:, KB:]
            ab = _bf(a)
            for hh in range(HG):
                cs = slice(hh * SB_HD, (hh + 1) * SB_HD)
                o_ref[:, cs] += _dot(ab[hh * B:(hh + 1) * B, :], v_ref[pl.ds(off, KB), cs])

        jp0 = (i * B) // KB
        block(jp0, True)

        def live(state):
            jj, dead = state
            return (jj <= jp0) & jnp.logical_not(dead)

        def step(state):
            jj, _ = state
            block(jp0 - jj, False)
            return jj + 1, jnp.max(cb_scr[:, :SB_HD]) < SB_DEAD

        lax.while_loop(live, step, (jnp.int32(1), jnp.max(cb_scr[:, :SB_HD]) < SB_DEAD))

        @pl.when((pl.program_id(0) == n_h - 1) & (i == n_i - 1))
        def _():
            _push_wait(own, pairs)

    return pl.pallas_call(
        body, name="sb_fwd",
        grid=(n_h, n_i),
        in_specs=[pl.BlockSpec((None, B, W), lambda h, i: (1, i, h)),
                  pl.BlockSpec((None, T, W), lambda h, i: (2, 0, h)),
                  pl.BlockSpec((None, T, W), lambda h, i: (3, 0, h)),
                  _ANY],
        out_specs=(pl.BlockSpec((B, W), lambda h, i: (i, h)), _ANY),
        out_shape=(jax.ShapeDtypeStruct((T, 1024), F32),
                   jax.ShapeDtypeStruct((N_DEV,) + wp_shard.shape, wp_shard.dtype)),
        scratch_shapes=[pltpu.VMEM((HG * B, KB), F32)] + _PUSH_SEMS,
        compiler_params=_cparams(("arbitrary", "arbitrary")),
    )(projb, projb, projb, wp_shard)


def _sb_bwd_call(projb, do_sb, g_p):
    T = projb.shape[1]
    B = min(SB_QUERIES, T)
    nb = T // B
    HG = SB_HG_BWD
    W = HG * SB_HD
    WQ = HG * B
    KB = min(SB_KEYS, T)
    nkb = T // KB
    n_h = SB_HEADS // HG
    scale = 1.0 / math.sqrt(SB_HD)

    def body(q_ref, k_ref, v_ref, do_ref, gp_ref, dq_ref, dk_ref, dv_ref, rp_ref,
             dk_scr, dv_scr, kt_scr, beta_scr, g_scr, dqt_scr, send_sems, recv_sems, loc_sem):
        i = pl.program_id(1)
        own, pairs = _push_copies(gp_ref, rp_ref, send_sems, recv_sems, loc_sem, scatter=True)

        @pl.when((pl.program_id(0) == 0) & (i == 0))
        def _():
            _push_start(own, pairs)

        @pl.when(i == 0)
        def _():
            dk_scr[...] = jnp.zeros_like(dk_scr)
            dv_scr[...] = jnp.zeros_like(dv_scr)
            for hh in range(HG):
                for jb in range(nkb):
                    kt_scr[hh, jb] = _bf(
                        k_ref[jb * KB:(jb + 1) * KB, hh * SB_HD:(hh + 1) * SB_HD].astype(F32).T)

        dqt_scr[...] = jnp.zeros_like(dqt_scr)
        later = _bf((_iota2(KB, KB, 1) > _iota2(KB, KB, 0)).astype(F32))
        earlier = _bf((_iota2(KB, KB, 1) < _iota2(KB, KB, 0)).astype(F32))
        dob = _bf(do_ref[...])
        jp0 = (i * B) // KB

        def strict_mask():
            return (jp0 * KB + _iota2(KB, WQ, 0)) < (i * B + (_iota2(KB, WQ, 1) & (B - 1)))

        def heads(fn):
            return [fn(slice(hh * SB_HD, (hh + 1) * SB_HD)) for hh in range(HG)]

        def pass1(jp, cb, masked):
            off = pl.multiple_of(jp * KB, KB)
            z = jnp.concatenate(heads(lambda cs: _dot_nt(k_ref[pl.ds(off, KB), cs], q_ref[:, cs])), axis=1) * scale
            da = jnp.concatenate(heads(lambda cs: _dot_nt(v_ref[pl.ds(off, KB), cs], dob[:, cs])), axis=1)
            lsz, l1m = _sb_logs(z)
            if masked:
                strict = strict_mask()
                l1m = jnp.where(strict, l1m, 0.0)
            a = jnp.exp(lsz + cb + _tri2_left(later, l1m))
            if masked:
                a = jnp.where(strict, a, 0.0)
            g_scr[jp] = a * da
            beta_scr[jp] = jnp.exp(lsz)
            ab = _bf(a)
            for hh in range(HG):
                cs = slice(hh * SB_HD, (hh + 1) * SB_HD)
                dv_scr[pl.ds(off, KB), cs] += _dot(ab[:, hh * B:(hh + 1) * B], dob[:, cs])
            return cb + jnp.sum(l1m, axis=0, keepdims=True)

        zero = jnp.zeros((1, WQ), F32)
        cb = pass1(jp0, zero, True)

        def live(state):
            jj, _, dead = state
            return (jj <= jp0) & jnp.logical_not(dead)

        def step(state):
            jj, cr, _ = state
            cr = pass1(jp0 - jj, cr, False)
            return jj + 1, cr, jnp.max(cr) < SB_DEAD

        n_done, _, _ = lax.while_loop(live, step, (jnp.int32(1), cb, jnp.max(cb) < SB_DEAD))
        jp_first = jp0 - (n_done - 1)

        def pass2(jp, cg, masked):
            off = pl.multiple_of(jp * KB, KB)
            g = g_scr[jp]
            beta = beta_scr[jp]
            dz = g * (1.0 - beta) - beta * (cg + _tri2_left(earlier, g))
            if masked:
                dz = jnp.where(strict_mask(), dz, 0.0)
            dzb = _bf(dz * scale)
            for hh in range(HG):
                cs = slice(hh * SB_HD, (hh + 1) * SB_HD)
                dk_scr[pl.ds(off, KB), cs] += _dot(dzb[:, hh * B:(hh + 1) * B], q_ref[:, cs])
                dqt_scr[hh] += _dot(kt_scr[hh, jp], dzb[:, hh * B:(hh + 1) * B])
            return cg + jnp.sum(g, axis=0, keepdims=True)

        cg = lax.fori_loop(jp_first, jp0, lambda jp, cr: pass2(jp, cr, False), zero)
        pass2(jp0, cg, True)
        for hh in range(HG):
            dq_ref[:, hh * SB_HD:(hh + 1) * SB_HD] = _bf(dqt_scr[hh].T)

        @pl.when(i == nb - 1)
        def _():
            dk_ref[...] = _bf(dk_scr[...])
            dv_ref[...] = _bf(dv_scr[...])

        @pl.when((pl.program_id(0) == n_h - 1) & (i == nb - 1))
        def _():
            _push_wait(own, pairs)

    return pl.pallas_call(
        body, name="sb_bwd",
        grid=(n_h, nb),
        in_specs=[pl.BlockSpec((None, B, W), lambda h, i: (1, i, h)),
                  pl.BlockSpec((None, T, W), lambda h, i: (2, 0, h)),
                  pl.BlockSpec((None, T, W), lambda h, i: (3, 0, h)),
                  pl.BlockSpec((B, W), lambda h, i: (i, h)),
                  _ANY],
        out_specs=(pl.BlockSpec((B, W), lambda h, i: (i, h)),
                   pl.BlockSpec((T, W), lambda h, i: (0, h)),
                   pl.BlockSpec((T, W), lambda h, i: (0, h)),
                   _ANY),
        out_shape=(jax.ShapeDtypeStruct((T, 1024), BF16),
                   jax.ShapeDtypeStruct((T, 1024), BF16),
                   jax.ShapeDtypeStruct((T, 1024), BF16),
                   jax.ShapeDtypeStruct(g_p.shape, g_p.dtype)),
        scratch_shapes=[pltpu.VMEM((T, W), F32), pltpu.VMEM((T, W), F32),
                        pltpu.VMEM((HG, nkb, SB_HD, KB), BF16),
                        pltpu.VMEM((nkb, KB, WQ), F32), pltpu.VMEM((nkb, KB, WQ), F32),
                        pltpu.VMEM((HG, SB_HD, B), F32)] + _PUSH_SEMS,
        compiler_params=_cparams(("arbitrary", "arbitrary")),
    )(projb, projb, projb, do_sb, g_p)


def _mid_call(o_gla, o_sb, projf, x, target, wpa, wpb, wo, gla_g, b_gate, final_g):
    T, D = x.shape
    tm = min(TBLK, T)

    def body(og_ref, ggate_ref, osb_ref, sgate_ref, ma_ref, mb_ref, x_ref, tgt_ref,
             wpa_ref, wpb_ref, wo_ref, glag_ref, bg_ref, fg_ref,
             dx2_ref, dogla_ref, dosb_ref, dggate_ref, dsgate_ref, dm_ref,
             mt_ref, ogt_ref, obt_ref, dx2b_ref, dya_ref, dyb_ref,
             dfg_ref, dbg_ref, dglag_ref, loss_ref):
        @pl.when(pl.program_id(0) == 0)
        def _():
            dfg_ref[...] = jnp.zeros_like(dfg_ref)
            dbg_ref[...] = jnp.zeros_like(dbg_ref)
            dglag_ref[...] = jnp.zeros_like(dglag_ref)
            loss_ref[...] = jnp.zeros_like(loss_ref)

        glag = glag_ref[...]
        ggate = ggate_ref[...]
        sg = _sigmoid(ggate)
        silu_g = ggate * sg
        ohat, rinv, nrm = [], [], []
        for hh in range(GLA_HEADS):
            oh = og_ref[:, hh * GLA_HV:(hh + 1) * GLA_HV]
            r = lax.rsqrt(jnp.mean(oh * oh, axis=-1, keepdims=True) + EPS)
            ohat.append(oh * r)
            rinv.append(r)
            nrm.append(ohat[-1] * glag)
        n_all = jnp.concatenate(nrm, axis=1)
        og = n_all * silu_g
        ogb = _bf(og)
        ya = _dot(ogb, wpa_ref[...])
        sgate = sgate_ref[...]
        ss = _sigmoid(sgate)
        silu_s = sgate * ss
        osb = osb_ref[...]
        ob = osb * silu_s
        obb = _bf(ob)
        yb = _dot(obb, wpb_ref[...])
        ga = _sigmoid(ma_ref[...] + bg_ref[:, :D])
        gb = _sigmoid(mb_ref[...] + bg_ref[:, D:])
        merged = ga * ya + gb * yb
        mgb = _bf(merged)
        x2 = x_ref[...] + _dot(mgb, wo_ref[...])
        r2 = lax.rsqrt(jnp.mean(x2 * x2, axis=-1, keepdims=True) + EPS)
        xh2 = x2 * r2
        fg = fg_ref[...]
        err = xh2 * fg - tgt_ref[...]
        loss_ref[...] += jnp.broadcast_to(
            0.5 * jnp.sum(jnp.mean(err * err, axis=-1, keepdims=True), axis=0, keepdims=True), (1, 128))
        dy = err * (1.0 / D)
        dfg_ref[...] += jnp.sum(dy * xh2, axis=0, keepdims=True)
        dxh = dy * fg
        dx2 = r2 * (dxh - xh2 * jnp.mean(dxh * xh2, axis=-1, keepdims=True))
        dx2_ref[...] = dx2
        dx2b = _bf(dx2)
        dx2b_ref[...] = dx2b
        dmerged = _dot_nt(dx2b, wo_ref[...])
        dya = dmerged * ga
        dyb = dmerged * gb
        dma = dmerged * ya * ga * (1.0 - ga)
        dmb = dmerged * yb * gb * (1.0 - gb)
        dm_ref[:, :D] = _bf(dma)
        dm_ref[:, D:] = _bf(dmb)
        dbg_ref[:, :D] += jnp.sum(dma, axis=0, keepdims=True)
        dbg_ref[:, D:] += jnp.sum(dmb, axis=0, keepdims=True)
        dyab = _bf(dya)
        dybb = _bf(dyb)
        dya_ref[...] = dyab
        dyb_ref[...] = dybb
        dog = _dot_nt(dyab, wpa_ref[...])
        dob = _dot_nt(dybb, wpb_ref[...])
        dosb_ref[...] = dob * silu_s
        dsgate_ref[...] = _bf(dob * osb * (ss * (1.0 + sgate * (1.0 - ss))))
        dn = dog * silu_g
        dggate_ref[...] = _bf(dog * n_all * (sg * (1.0 + ggate * (1.0 - sg))))
        dglag = jnp.zeros((1, GLA_HV), F32)
        for hh in range(GLA_HEADS):
            dnh = dn[:, hh * GLA_HV:(hh + 1) * GLA_HV]
            dglag = dglag + jnp.sum(dnh * ohat[hh], axis=0, keepdims=True)
            dohat = dnh * glag
            dogla_ref[:, hh * GLA_HV:(hh + 1) * GLA_HV] = rinv[hh] * (
                dohat - ohat[hh] * jnp.mean(dohat * ohat[hh], axis=-1, keepdims=True))
        dglag_ref[...] += dglag
        mt_ref[...] = _bf(merged.T)
        ogt_ref[...] = _bf(og.T)
        obt_ref[...] = _bf(ob.T)

    row = lambda i: (i, 0)
    const = lambda i: (0, 0)
    tile = pl.BlockSpec((tm, D), row)
    tile_t = pl.BlockSpec((None, D, tm), lambda i: (i, 0, 0))
    wspec = pl.BlockSpec((D, D), const)
    return pl.pallas_call(
        body, name="mid",
        grid=(T // tm,),
        in_specs=[tile,
                  pl.BlockSpec((None, tm, D), lambda i: (1, i, 0)),
                  tile,
                  pl.BlockSpec((None, tm, D), lambda i: (2, i, 0)),
                  pl.BlockSpec((None, tm, D), lambda i: (3, i, 0)),
                  pl.BlockSpec((None, tm, D), lambda i: (4, i, 0)),
                  tile, tile, wspec, wspec, wspec,
                  pl.BlockSpec((1, GLA_HV), const),
                  pl.BlockSpec((1, 2 * D), const),
                  pl.BlockSpec((1, D), const)],
        out_specs=(tile, tile, tile, tile, tile,
                   pl.BlockSpec((tm, 2 * D), row),
                   tile_t, tile_t, tile_t, tile, tile, tile,
                   pl.BlockSpec((1, D), const),
                   pl.BlockSpec((1, 2 * D), const),
                   pl.BlockSpec((1, GLA_HV), const),
                   pl.BlockSpec((1, 128), const)),
        out_shape=(jax.ShapeDtypeStruct((T, D), F32),
                   jax.ShapeDtypeStruct((T, D), F32),
                   jax.ShapeDtypeStruct((T, D), F32),
                   jax.ShapeDtypeStruct((T, D), BF16),
                   jax.ShapeDtypeStruct((T, D), BF16),
                   jax.ShapeDtypeStruct((T, 2 * D), BF16),
                   jax.ShapeDtypeStruct((T // tm, D, tm), BF16),
                   jax.ShapeDtypeStruct((T // tm, D, tm), BF16),
                   jax.ShapeDtypeStruct((T // tm, D, tm), BF16),
                   jax.ShapeDtypeStruct((T, D), BF16),
                   jax.ShapeDtypeStruct((T, D), BF16),
                   jax.ShapeDtypeStruct((T, D), BF16),
                   jax.ShapeDtypeStruct((1, D), F32),
                   jax.ShapeDtypeStruct((1, 2 * D), F32),
                   jax.ShapeDtypeStruct((1, GLA_HV), F32),
                   jax.ShapeDtypeStruct((1, 128), F32)),
        compiler_params=_cparams(("arbitrary",)),
    )(o_gla, projf, o_sb, projf, projf, projf, x, target, wpa, wpb, wo, gla_g, b_gate, final_g)


def _dh_call(pieces, dmlog, drank, wt, wr, x, dx2, norm_g, s_in, small):
    T, D = x.shape
    tm = min(256, T)
    npc = len(pieces)
    n_main = N_GROUPS * 1024
    n_i = T // tm
    i_forward = 5 * n_i // 8

    def body(*refs):
        pcs = refs[:npc]
        (dm_ref, dr_ref, w_hbm, wr_ref, x_ref, dx2_ref, g_ref, sin_ref, small_ref,
         gx_ref, rin_ref, relayed_ref, rsmall_ref,
         w_scr, sems, dg_ref, small_mine, small_send, small_recv, small_loc, *exchange_scratch) = refs[npc:]
        start, forward, finish = _chip_reduce_steps(sin_ref, rin_ref, relayed_ref, *exchange_scratch)

        @pl.when(pl.program_id(0) == 0)
        def _():
            start()
            lo = pltpu.make_async_copy(w_hbm.at[pl.ds(0, RANK_COL)], w_scr.at[pl.ds(0, RANK_COL)], sems.at[0])
            hi = pltpu.make_async_copy(w_hbm.at[pl.ds(RANK_COL + GLA_RANK, n_main - RANK_COL)],
                                       w_scr.at[pl.ds(RANK_COL, n_main - RANK_COL)], sems.at[1])
            lo.start()
            hi.start()
            dg_ref[...] = jnp.zeros_like(dg_ref)
            lo.wait()
            hi.wait()

        @pl.when(pl.program_id(0) == i_forward)
        def _():
            forward()

        def w_group(g):
            return w_scr[g * 1024:(g + 1) * 1024, :]

        dr = dr_ref[...]
        dh = _dot(dr, wr_ref[...])
        for g in range(npc):
            dh = dh + _dot(pcs[g][...], w_group(g))
        dh = dh + _dot(dm_ref[:, :D], w_group(npc))
        dh = dh + _dot(dm_ref[:, D:], w_group(npc + 1))
        xv = x_ref[...]
        r = lax.rsqrt(jnp.mean(xv * xv, axis=-1, keepdims=True) + EPS)
        xhat = xv * r
        g = g_ref[...]
        dg_ref[...] += jnp.sum(dh * xhat, axis=0, keepdims=True)
        dxhat = dh * g
        gx_ref[...] = r * (dxhat - xhat * jnp.mean(dxhat * xhat, axis=-1, keepdims=True)) + dx2_ref[...]

        @pl.when(pl.program_id(0) == n_i - 1)
        def _():
            small_mine[...] = small_ref[...]
            small_mine[:, _SM_NORM:_SM_NORM + D] = dg_ref[...]
            own, pairs = _push_copies(small_mine, rsmall_ref, small_send, small_recv, small_loc, scatter=False)
            _push_start(own, pairs)
            finish()
            _push_wait(own, pairs)

    row = lambda i: (i, 0)
    const = lambda i: (0, 0)
    tile = pl.BlockSpec((tm, D), row)
    part = s_in.shape[1:]
    return pl.pallas_call(
        body, name="dh",
        grid=(n_i,),
        in_specs=[tile] * npc + [
            pl.BlockSpec((tm, 2 * D), row),
            pl.BlockSpec((tm, 128), row),
            _ANY,
            pl.BlockSpec((128, D), const),
            tile, tile,
            pl.BlockSpec((1, D), const),
            _ANY,
            pl.BlockSpec(small.shape, const)],
        out_specs=(tile, _ANY, _ANY, _ANY),
        out_shape=(jax.ShapeDtypeStruct((T, D), F32),
                   jax.ShapeDtypeStruct((3,) + part, s_in.dtype),
                   jax.ShapeDtypeStruct(part, s_in.dtype),
                   jax.ShapeDtypeStruct((N_DEV,) + small.shape, small.dtype)),
        scratch_shapes=[pltpu.VMEM((n_main, D), BF16), pltpu.SemaphoreType.DMA((2,)),
                        pltpu.VMEM((1, D), F32), pltpu.VMEM(small.shape, small.dtype)]
        + _PUSH_SEMS + _chip_reduce_scratch(*part, s_in.dtype),
        compiler_params=_cparams(("arbitrary",)),
    )(*pieces, dmlog, drank, wt, wr, x, dx2, norm_g, s_in, small)


def _wgrad_call(lhs_list, lhs_of_group, rhs_list, rhs_of_group, n_transposed, name, narrow=None):
    n_groups = len(rhs_of_group)
    n_tb, D, tb = lhs_list[0].shape
    T = n_tb * tb
    per = min(4, n_tb)
    tk = per * tb
    nk = T // tk
    nl = len(lhs_list)
    extra = [] if narrow is None else [narrow]

    def tokens_side_by_side(lref):
        return jnp.concatenate([lref[b] for b in range(per)], axis=1)

    def body(*refs):
        lhs = refs[:nl]
        rhs = refs[nl:nl + n_groups]
        rest = refs[nl + n_groups:]
        g = pl.program_id(0)
        i = pl.program_id(1)
        if narrow is None:
            out_ref, acc = rest
        else:
            narrow_ref, out_ref, narrow_out, acc, narrow_acc = rest

            @pl.when((g == 0) & (i == 0))
            def _():
                narrow_acc[...] = jnp.zeros_like(narrow_acc)

            @pl.when(g == 0)
            def _():
                narrow_acc[...] += _dot(tokens_side_by_side(lhs[lhs_of_group[0]]), narrow_ref[...])

            @pl.when((g == 0) & (i == nk - 1))
            def _():
                narrow_out[...] = _bf(narrow_acc[...].T)

        @pl.when(i == 0)
        def _():
            acc[...] = jnp.zeros_like(acc)

        for p in range(n_groups):
            @pl.when(g == p)
            def _(p=p):
                acc[...] += _dot(tokens_side_by_side(lhs[lhs_of_group[p]]), rhs[p][...])

        @pl.when((i == nk - 1) & (g < n_transposed))
        def _():
            out_ref[...] = _bf(acc[...].T)

        @pl.when((i == nk - 1) & (g >= n_transposed))
        def _():
            out_ref[...] = _bf(acc[...])

    def lhs_spec(a):
        groups = [g for g in range(n_groups) if lhs_of_group[g] == a]
        lo, hi = min(groups), max(groups)
        assert groups == list(range(lo, hi + 1))
        return pl.BlockSpec((per, D, tb), lambda g, i: (jnp.where((g >= lo) & (g <= hi), i, 0), 0, 0))

    def rhs_spec(p):
        cb = rhs_of_group[p][1]
        return pl.BlockSpec((tk, 1024), lambda g, i: (jnp.where(g == p, i, 0), cb))

    res = pl.pallas_call(
        body, name=name,
        grid=(n_groups, nk),
        in_specs=[lhs_spec(a) for a in range(nl)] + [rhs_spec(p) for p in range(n_groups)]
        + [pl.BlockSpec((tk, 128), lambda g, i: (jnp.where(g == 0, i, 0), 0)) for _ in extra],
        out_specs=[pl.BlockSpec((None, D, 1024), lambda g, i: (g, 0, 0))]
        + [pl.BlockSpec((128, D), lambda g, i: (0, 0)) for _ in extra],
        out_shape=[jax.ShapeDtypeStruct((n_groups, D, 1024), BF16)]
        + [jax.ShapeDtypeStruct((128, D), BF16) for _ in extra],
        scratch_shapes=[pltpu.VMEM((D, 1024), F32)] + [pltpu.VMEM((D, 128), F32) for _ in extra],
        compiler_params=_cparams(("arbitrary", "arbitrary")),
    )(*lhs_list, *[rhs_list[rhs_of_group[p][0]] for p in range(n_groups)], *extra)
    return res[0] if narrow is None else res


def _adamw_math(parts, w, m, v):
    g = parts[0].astype(F32)
    for p in parts[1:]:
        g = g + p.astype(F32)
    mm = ADAM_B1 * m + (1.0 - ADAM_B1) * g
    vv = ADAM_B2 * v + (1.0 - ADAM_B2) * (g * g)
    m_hat = mm / (1.0 - ADAM_B1 ** ADAM_STEP)
    v_hat = vv / (1.0 - ADAM_B2 ** ADAM_STEP)
    return g, -ADAM_LR * (m_hat / (jnp.sqrt(v_hat) + ADAM_EPS) + ADAM_WD * w), mm, vv


def _part_order(n_parts):
    return [n_parts - 1] + list(range(n_parts - 1))


def _adamw_call(parts, w, m, v, name):
    R, C = w.shape
    n_parts = parts.shape[0]
    (tr, tc), grid, idx = _tiling_2d(R, C, 512)

    def body(p_ref, w_ref, m_ref, v_ref, g_ref, d_ref, nm_ref, nv_ref):
        g_ref[...], d_ref[...], nm_ref[...], nv_ref[...] = _adamw_math(
            [p_ref[k] for k in _part_order(n_parts)], w_ref[...], m_ref[...], v_ref[...])

    blk = pl.BlockSpec((tr, tc), idx)
    sds = jax.ShapeDtypeStruct((R, C), F32)
    return pl.pallas_call(
        body, name=name,
        grid=grid,
        in_specs=[pl.BlockSpec((n_parts, tr, tc), lambda i: (0,) + idx(i)), blk, blk, blk],
        out_specs=(blk, blk, blk, blk),
        out_shape=(sds, sds, sds, sds),
        compiler_params=_cparams(("arbitrary",)),
    )(parts, w, m, v)


def _adamw_rows_call(parts, ws, ms, vs, name):
    n = len(ws)
    R, C = ws[0].shape
    n_parts = parts.shape[0]

    def body(*refs):
        p_ref = refs[0]
        w_refs, m_refs, v_refs = refs[1:1 + n], refs[1 + n:1 + 2 * n], refs[1 + 2 * n:1 + 3 * n]
        outs = refs[1 + 3 * n:]
        for k in range(n):
            @pl.when(pl.program_id(0) == k)
            def _(k=k):
                res = _adamw_math([p_ref[j] for j in _part_order(n_parts)],
                                  w_refs[k][...], m_refs[k][...], v_refs[k][...])
                for o_ref, val in zip(outs[4 * k:4 * k + 4], res):
                    o_ref[...] = val

    whole = pl.BlockSpec((R, C), lambda k: (0, 0))
    sds = jax.ShapeDtypeStruct((R, C), F32)
    res = pl.pallas_call(
        body, name=name,
        grid=(n,),
        in_specs=[pl.BlockSpec((n_parts, R, C), lambda k: (0, k, 0))] + [whole] * (3 * n),
        out_specs=tuple([whole] * (4 * n)),
        out_shape=tuple([sds] * (4 * n)),
        compiler_params=_cparams(("arbitrary",)),
    )(parts, *ws, *ms, *vs)
    return [res[4 * k:4 * k + 4] for k in range(n)]


def _adamw_lanes_call(parts, offsets, ws, ms, vs, name):
    n = len(ws)
    n_parts = parts.shape[0]

    def body(*refs):
        p_ref = refs[0]
        w_refs, m_refs, v_refs = refs[1:1 + n], refs[1 + n:1 + 2 * n], refs[1 + 2 * n:1 + 3 * n]
        outs = refs[1 + 3 * n:]
        for k in range(n):
            lanes = slice(offsets[k], offsets[k] + ws[k].shape[1])
            res = _adamw_math([p_ref[j, :, lanes] for j in _part_order(n_parts)],
                              w_refs[k][...], m_refs[k][...], v_refs[k][...])
            for o_ref, val in zip(outs[4 * k:4 * k + 4], res):
                o_ref[...] = val

    res = pl.pallas_call(
        body, name=name,
        out_shape=tuple(jax.ShapeDtypeStruct(ws[k].shape, F32) for k in range(n) for _ in range(4)),
        compiler_params=_cparams(),
    )(parts, *ws, *ms, *vs)
    return [res[4 * k:4 * k + 4] for k in range(n)]


def _local_step(x, target, wt, wr, wdec, bdec, wp_shard, norm_g, gla_g, b_gate, final_g):
    D = x.shape[1]
    half = wp_shard.shape[1] // 2
    projf, projb, rank, ht, wp_lo = _proj_call(x, norm_g, wt, wr, wp_shard[:, :half])
    o_gla, st_all, la = _gla_fwd_call(projf, projb, rank, wdec, bdec)
    o_sb, wp_hi = _sb_fwd_call(projb, wp_shard[:, half:])
    wp_full = jnp.concatenate([wp_lo, wp_hi], axis=2).transpose(1, 0, 2, 3).reshape(3, D, D)
    (dx2, do_gla, do_sb, dggate, dsgate, dmlog, mt, ogt, obt, dx2b, dya, dyb,
     dfinal_g, db_gate, dgla_g, loss) = _mid_call(o_gla, o_sb, projf, x, target, wp_full[0], wp_full[1],
                                                 wp_full[2], gla_g, b_gate, final_g)
    dw_p = _wgrad_call([ogt, obt, mt], [0, 1, 2], [dya, dyb, dx2b], [(0, 0), (1, 0), (2, 0)], 0, "wgrad_p")
    g_p = dw_p.reshape(3, N_DEV, D // N_DEV, D).transpose(1, 0, 2, 3).reshape(N_DEV, 3 * (D // N_DEV), D)
    dqk, dgv, drank, dwdec, dbdec = _gla_bwd_call(projf, projb, la, do_gla, st_all, rank, wdec)
    dsq, dsk, dsv, r_p = _sb_bwd_call(projb, do_sb, g_p)
    pieces = [dqk, dgv, dggate, dsq, dsk, dsv, dsgate]
    rhs_of_group = [(g, 0) for g in range(7)] + [(7, 0), (7, 1)]
    dw_in, dwr = _wgrad_call([ht], [0] * N_GROUPS, pieces + [dmlog], rhs_of_group, N_GROUPS, "wgrad_in",
                             narrow=drank)
    s_in = _pair_sum_call(dw_in.reshape(N_GROUPS * 1024, D), dwr)
    small = jnp.concatenate([
        jnp.zeros((D,), F32), dbdec.reshape(-1), dgla_g.reshape(-1), db_gate.reshape(-1), dfinal_g.reshape(-1),
        loss.reshape(-1), dwdec[:GLA_RANK].reshape(-1)]).reshape(1, _SM_LEN)
    grad_x, r_in, _, r_small = _dh_call(pieces, dmlog, drank, wt, wr, x, dx2, norm_g, s_in, small)
    return grad_x, r_in, r_p, r_small


_SM_NORM = 0
_SM_BDEC = _SM_NORM + D_MODEL
_SM_GLAG = _SM_BDEC + GLA_DK
_SM_BGATE = _SM_GLAG + GLA_HV
_SM_FINAL = _SM_BGATE + 2 * D_MODEL
_SM_REPL = _SM_FINAL + D_MODEL
_SM_LOSS = _SM_REPL
_SM_WDEC = _SM_LOSS + 128
_SM_LEN = _SM_WDEC + GLA_RANK * GLA_DK


def kernel(x, norm_g, w_in, w_dec_up, b_dec, gla_norm_g, w_pa, w_pb, b_gate, w_o, final_g, loss_target, m_norm_g, m_w_in, m_w_dec_up, m_b_dec, m_gla_norm_g, m_w_pa, m_w_pb, m_b_gate, m_w_o, m_final_g, v_norm_g, v_w_in, v_w_dec_up, v_b_dec, v_gla_norm_g, v_w_pa, v_w_pb, v_b_gate, v_w_o, v_final_g):
    D = D_MODEL
    me = 4 * lax.axis_index("x") + 2 * lax.axis_index("y") + lax.axis_index("c")

    wp_shard = jnp.stack([w_pa, w_pb, w_o]).astype(BF16)
    n_first = _half_rows(SHARD_COLS)
    win_all, wdec_all = _all_gather([w_in.T.astype(BF16), w_dec_up], "gather_w",
                                    row_pieces=[[(0, n_first), (n_first, SHARD_COLS - n_first)], None])
    wt = _flatten_blocks_call(win_all)
    wr = jnp.pad(wt[RANK_COL:RANK_COL + GLA_RANK], ((0, 128 - GLA_RANK), (0, 0)))
    wdec_full = wdec_all.transpose(1, 0, 2).reshape(GLA_RANK, GLA_DK)
    wdec = jnp.pad(wdec_full, ((0, 128 - GLA_RANK), (0, 0)))

    grad_x, r_in, r_p, r_small = _local_step(
        x[0], loss_target[0], wt, wr, wdec, b_dec.reshape(1, -1), wp_shard,
        norm_g.reshape(1, -1), gla_norm_g.reshape(1, -1), b_gate.reshape(1, -1), final_g.reshape(1, -1))

    gw_in, d_in, nm_in, nv_in = (a.T for a in _adamw_call(r_in, w_in.T, m_w_in.T, v_w_in.T, "adamw_in"))
    (g_pa, d_pa, nm_pa, nv_pa), (g_pb, d_pb, nm_pb, nv_pb), (g_o, d_o, nm_o, nv_o) = _adamw_rows_call(
        r_p, [w_pa, w_pb, w_o], [m_w_pa, m_w_pb, m_w_o], [v_w_pa, v_w_pb, v_w_o], "adamw_p")

    def row(a):
        return a.reshape(1, -1)

    rep = _adamw_lanes_call(
        r_small, [_SM_NORM, _SM_BDEC, _SM_GLAG, _SM_BGATE, _SM_FINAL],
        [row(a) for a in (norm_g, b_dec, gla_norm_g, b_gate, final_g)],
        [row(a) for a in (m_norm_g, m_b_dec, m_gla_norm_g, m_b_gate, m_final_g)],
        [row(a) for a in (v_norm_g, v_b_dec, v_gla_norm_g, v_b_gate, v_final_g)], "adamw_rep")
    ((g_norm, d_norm, nm_norm, nv_norm), (g_bdec, d_bdec, nm_bdec, nv_bdec), (g_glag, d_glag, nm_glag, nv_glag),
     (g_bgate, d_bgate, nm_bgate, nv_bgate), (g_final, d_final, nm_final, nv_final)) = [
        tuple(a.reshape(-1) for a in quad) for quad in rep]

    wdec_parts = r_small[:, 0, _SM_WDEC:].reshape(N_DEV, GLA_RANK, GLA_DK)
    cols = GLA_DK // N_DEV
    wdec_mine = lax.dynamic_slice_in_dim(wdec_parts, me * cols, cols, axis=2)
    g_wdec, d_wdec, nm_wdec, nv_wdec = _adamw_call(wdec_mine, w_dec_up, m_w_dec_up, v_w_dec_up, "adamw_dec")

    loss_total = jnp.sum(r_small[:, 0, _SM_LOSS])

    return (loss_total, grad_x[None],
            g_norm, gw_in, g_wdec, g_bdec, g_glag, g_pa, g_pb, g_bgate, g_o, g_final,
            d_norm, d_in, d_wdec, d_bdec, d_glag, d_pa, d_pb, d_bgate, d_o, d_final,
            nm_norm, nm_in, nm_wdec, nm_bdec, nm_glag, nm_pa, nm_pb, nm_bgate, nm_o, nm_final,
            nv_norm, nv_in, nv_wdec, nv_bdec, nv_glag, nv_pa, nv_pb, nv_bgate, nv_o, nv_final)
```

```python
import math

import jax
import jax.numpy as jnp
from jax import lax
from jax.experimental import pallas as pl
from jax.experimental.pallas import tpu as pltpu

F32 = jnp.float32
BF16 = jnp.bfloat16

N_DEV = 8
D_MODEL = 1024
GLA_HEADS = 4
GLA_HK = 128
GLA_HV = 256
GLA_DK = 512
GLA_RANK = 16
GLA_TAU = 16.0
GLA_CHUNK = 64
SB_HEADS = 8
SB_HD = 128
EPS = 1e-6
N_GROUPS = 9
RANK_COL = 3072
IN_COLS = 9232
SHARD_COLS = IN_COLS // N_DEV

ADAM_LR = 0.001
ADAM_B1 = 0.9
ADAM_B2 = 0.999
ADAM_EPS = 1e-08
ADAM_WD = 0.01
ADAM_STEP = 10

VMEM_LIMIT = 56 * 1024 * 1024
TBLK = 256


def _cparams(sem=None):
    return pltpu.CompilerParams(dimension_semantics=sem, vmem_limit_bytes=VMEM_LIMIT)


def _tiling_2d(rows, cols, band_cols):
    if rows * cols <= 128 * 1024:
        return (rows, cols), (1,), lambda i: (0, 0)
    if rows % 128 == 0:
        return (128, cols), (rows // 128,), lambda i: (i, 0)
    tc = band_cols if cols % band_cols == 0 else cols
    return (rows, tc), (cols // tc,), lambda i: (0, i)


def _dot(a, b):
    return jnp.dot(a, b, preferred_element_type=F32)


def _dot_nt(a, b):
    return lax.dot_general(a, b, (((1,), (1,)), ((), ())), preferred_element_type=F32)


def _dot_tn(a, b):
    return lax.dot_general(a, b, (((0,), (0,)), ((), ())), preferred_element_type=F32)


def _bf(x):
    return x.astype(BF16)


def _split3(x):
    hi = x.astype(BF16)
    r = x - hi.astype(F32)
    mid = r.astype(BF16)
    lo = (r - mid.astype(F32)).astype(BF16)
    return hi, mid, lo


def _tri_left(tri, x):
    hi, mid, lo = _split3(x)
    return _dot(tri, hi) + _dot(tri, mid) + _dot(tri, lo)


def _split2(x):
    hi = lax.bitcast_convert_type(lax.bitcast_convert_type(x, jnp.uint32) & jnp.uint32(0xFFFF0000), F32)
    return hi.astype(BF16), (x - hi).astype(BF16)


def _tri2_left(tri, x):
    hi, lo = _split2(x)
    return _dot(tri, hi) + _dot(tri, lo)


def _tri2_right(x, tri):
    hi, lo = _split2(x)
    return _dot(hi, tri) + _dot(lo, tri)


def _iota2(n, m, dim):
    return lax.broadcasted_iota(jnp.int32, (n, m), dim)


def _sigmoid(x):
    return 1.0 / (1.0 + jnp.exp(-x))


def _softplus_neg_abs(z):
    return jnp.log(1.0 + jnp.exp(-jnp.abs(z)))


_ANY = pl.BlockSpec(memory_space=pl.ANY)


def _mesh_pos():
    return lax.axis_index("x"), lax.axis_index("y"), lax.axis_index("c")


def _other_chips(x, y):
    return [(1 - x, y), (x, 1 - y), (1 - x, 1 - y)]


def _rcopy(src, dst, send_sem, recv_sem, to):
    return pltpu.make_async_remote_copy(src_ref=src, dst_ref=dst, send_sem=send_sem, recv_sem=recv_sem,
                                        device_id=to, device_id_type=pl.DeviceIdType.MESH)


def _push_copies(src_ref, dst_ref, send_sems, recv_sems, loc_sem, scatter):
    x, y, c = _mesh_pos()
    me = 4 * x + 2 * y + c
    own = pltpu.make_async_copy(src_ref.at[me] if scatter else src_ref, dst_ref.at[me], loc_sem)
    pairs = []
    for k in range(1, N_DEV):
        px = 1 - x if k & 4 else x
        py = 1 - y if k & 2 else y
        pc = 1 - c if k & 1 else c
        pid = 4 * px + 2 * py + pc
        src = src_ref.at[pid] if scatter else src_ref
        send = _rcopy(src, dst_ref.at[me], send_sems.at[k - 1], recv_sems.at[k - 1], (px, py, pc))
        recv = _rcopy(src, dst_ref.at[pid], send_sems.at[k - 1], recv_sems.at[k - 1], (px, py, pc))
        pairs.append((send, recv))
    return own, pairs


def _push_start(own, pairs):
    own.start()
    for send, _ in pairs:
        send.start()


def _push_wait(own, pairs):
    for _, recv in pairs:
        recv.wait_recv()
    for send, _ in pairs:
        send.wait_send()
    own.wait()


_PUSH_SEMS = [pltpu.SemaphoreType.DMA((N_DEV - 1,)), pltpu.SemaphoreType.DMA((N_DEV - 1,)),
              pltpu.SemaphoreType.DMA]


def _half_rows(rows):
    return (rows // 2) // 16 * 16


_ADD_ROWS = 128


def _chip_reduce_steps(src_ref, dst_ref, relayed_ref, sum_x, sum_y, rel_x, rel_y, load_sems, send_sems, recv_sems,
                       loc_sem):
    _, R, C = src_ref.shape
    n0 = _half_rows(R)
    lo, hi = pl.ds(0, n0), pl.ds(n0, R - n0)
    x, y, c = _mesh_pos()
    (xx, xy), (yx, yy), (dx, dy) = _other_chips(x, y)
    to_diag, to_x, to_y = src_ref.at[2 * dx + dy], src_ref.at[2 * xx + xy], src_ref.at[2 * yx + yy]
    x_nb, y_nb = (xx, xy, c), (yx, yy, c)
    relays = (_rcopy(to_diag.at[lo], relayed_ref.at[lo], send_sems.at[0], recv_sems.at[0], x_nb),
              _rcopy(to_diag.at[hi], relayed_ref.at[hi], send_sems.at[1], recv_sems.at[1], y_nb))
    plain = (_rcopy(to_x.at[lo], dst_ref.at[0, lo], send_sems.at[2], recv_sems.at[2], x_nb),
             _rcopy(to_y.at[hi], dst_ref.at[1, hi], send_sems.at[3], recv_sems.at[3], y_nb))
    summed = (_rcopy(sum_x, dst_ref.at[0, hi], send_sems.at[4], recv_sems.at[4], x_nb),
              _rcopy(sum_y, dst_ref.at[1, lo], send_sems.at[5], recv_sems.at[5], y_nb))
    load_mine = (pltpu.make_async_copy(to_x.at[hi], sum_x, load_sems.at[0]),
                 pltpu.make_async_copy(to_y.at[lo], sum_y, load_sems.at[1]))
    load_relayed = (pltpu.make_async_copy(relayed_ref.at[hi], rel_x, load_sems.at[2]),
                    pltpu.make_async_copy(relayed_ref.at[lo], rel_y, load_sems.at[3]))
    own = pltpu.make_async_copy(src_ref.at[2 * x + y], dst_ref.at[2], loc_sem)

    def start():
        for cp in relays + plain + (own,) + load_mine:
            cp.start()

    def add(acc_ref, rel_ref):
        for r0 in range(0, acc_ref.shape[0], _ADD_ROWS):
            rows = slice(r0, min(r0 + _ADD_ROWS, acc_ref.shape[0]))
            acc_ref[rows, :] = (acc_ref[rows, :].astype(F32) + rel_ref[rows, :].astype(F32)).astype(acc_ref.dtype)

    def forward():
        for cp in relays:
            cp.wait_recv()
        for cp in load_relayed:
            cp.start()
        for cp in load_mine + load_relayed:
            cp.wait()
        add(sum_x, rel_x)
        add(sum_y, rel_y)
        for cp in summed:
            cp.start()

    def finish():
        for cp in plain + summed:
            cp.wait_recv()
        for cp in relays + plain + summed:
            cp.wait_send()
        own.wait()

    return start, forward, finish


def _chip_reduce_scratch(rows, cols, dtype):
    n0 = _half_rows(rows)
    return [pltpu.VMEM((rows - n0, cols), dtype), pltpu.VMEM((n0, cols), dtype)] * 2 + [
        pltpu.SemaphoreType.DMA((4,)), pltpu.SemaphoreType.DMA((6,)), pltpu.SemaphoreType.DMA((6,)),
        pltpu.SemaphoreType.DMA]


def _all_gather(arrs, name, row_pieces=None):
    n = len(arrs)
    pieces = [[None] if not row_pieces or not row_pieces[a] else list(row_pieces[a]) for a in range(n)]
    assert all(len(p) in (1, 2) for p in pieces)
    units = [(a, i) for a in range(n) for i in range(len(pieces[a]))]

    def body(*refs):
        ins = refs[:n]
        outs = refs[n:2 * n]
        send_sems, recv_sems, loc_sems = refs[2 * n:]
        x, y, c = _mesh_pos()
        me, sib = (x, y, c), (x, y, 1 - c)
        xn, yn, dg = [(px, py, c) for px, py in _other_chips(x, y)]

        def rows(ref, a, i):
            return ref if pieces[a][i] is None else ref.at[pl.ds(*pieces[a][i])]

        def copy(u, k, block, to, own=False):
            a, i = u
            px, py, pc = block
            dst = rows(outs[a].at[4 * px + 2 * py + pc], a, i)
            return _rcopy(rows(ins[a], a, i) if own else dst, dst, send_sems.at[a, k, i], recv_sems.at[a, k, i], to)

        started = []

        def start(cp):
            cp.start()
            started.append(cp)

        def landed_then_pass_on(u, k, block):
            copy(u, k, block, me).wait_recv()
            start(copy(u, 3 + k, block, sib))

        mine = [pltpu.make_async_copy(ins[a], outs[a].at[4 * x + 2 * y + c], loc_sems.at[a]) for a in range(n)]
        for cp in mine:
            cp.start()
        for u in units:
            start(copy(u, 0, me, sib, own=True))
        for a in range(n):
            if len(pieces[a]) == 2:
                for i, to, k in ((0, xn, 1), (1, yn, 2), (1, xn, 1), (0, yn, 2)):
                    start(copy((a, i), k, me, to, own=True))
            else:
                for to, k in ((xn, 1), (yn, 2), (dg, 3)):
                    start(copy((a, 0), k, me, to, own=True))
        for a in range(n):
            if len(pieces[a]) == 2:
                landed_then_pass_on((a, 0), 1, xn)
                start(copy((a, 0), 3, xn, yn))
                landed_then_pass_on((a, 1), 2, yn)
                start(copy((a, 1), 3, yn, xn))
                landed_then_pass_on((a, 1), 1, xn)
                landed_then_pass_on((a, 0), 2, yn)
                landed_then_pass_on((a, 0), 3, dg)
                landed_then_pass_on((a, 1), 3, dg)
            else:
                for block, k in ((xn, 1), (yn, 2), (dg, 3)):
                    landed_then_pass_on((a, 0), k, block)
        for u in units:
            copy(u, 0, sib, me).wait_recv()
            for k, (px, py, _) in ((4, xn), (5, yn), (6, dg)):
                copy(u, k, (px, py, 1 - c), me).wait_recv()
        for cp in started:
            cp.wait_send()
        for cp in mine:
            cp.wait()

    n_pc = max(len(p) for p in pieces)

    return pl.pallas_call(
        body, name=name,
        out_shape=tuple(jax.ShapeDtypeStruct((N_DEV,) + a.shape, a.dtype) for a in arrs),
        in_specs=[_ANY] * n,
        out_specs=tuple([_ANY] * n),
        scratch_shapes=[pltpu.SemaphoreType.DMA((n, 7, n_pc)), pltpu.SemaphoreType.DMA((n, 7, n_pc)),
                        pltpu.SemaphoreType.DMA((n,))],
    )(*arrs)


def _flatten_blocks_call(blocks):
    n, R, C = blocks.shape
    tc = C // 2

    def body(in_ref, out_ref):
        for p in range(n):
            out_ref[p * R:(p + 1) * R, :] = in_ref[p]

    return pl.pallas_call(
        body, name="flatten_w",
        grid=(C // tc,),
        in_specs=[pl.BlockSpec((n, R, tc), lambda i: (0, 0, i))],
        out_specs=pl.BlockSpec((n * R, tc), lambda i: (0, i)),
        out_shape=jax.ShapeDtypeStruct((n * R, C), blocks.dtype),
        compiler_params=_cparams(("arbitrary",)),
    )(blocks)


_PARTS_BANDS = 4


def _pair_sum_call(dmain, drank):
    D = dmain.shape[1]
    n = _PARTS_BANDS
    tc = D // n

    def body(dm_ref, dr_ref, sum_ref, laid, got, send_sems, recv_sems):
        x, y, c = _mesh_pos()

        def pushes(k):
            return [_rcopy(laid.at[k % 2, 2 * q + (1 - c)], got.at[k, q], send_sems.at[k, q], recv_sems.at[k, q],
                           (x, y, 1 - c)) for q in range(4)]

        def lay_out(k):
            for p in range(N_DEV):
                lo, hi = p * SHARD_COLS, (p + 1) * SHARD_COLS
                at = 0
                for src, a, b in ((dm_ref, lo, min(hi, RANK_COL)),
                                  (dr_ref, max(lo, RANK_COL) - RANK_COL, min(hi, RANK_COL + GLA_RANK) - RANK_COL),
                                  (dm_ref, max(lo, RANK_COL + GLA_RANK) - GLA_RANK, hi - GLA_RANK)):
                    if b > a:
                        laid[k % 2, p, at:at + (b - a), :] = src[a:b, :]
                        at += b - a

        for k in range(n + 1):
            @pl.when(pl.program_id(0) == k)
            def _(k=k):
                if k < n:
                    if k >= 2:
                        for cp in pushes(k - 2):
                            cp.wait_send()
                    lay_out(k)
                    for cp in pushes(k):
                        cp.start()
                if k >= 1:
                    for cp in pushes(k - 1):
                        cp.wait_recv()
                    for q in range(4):
                        sum_ref[q] = (laid[(k - 1) % 2, 2 * q + c].astype(F32)
                                      + got[k - 1, q].astype(F32)).astype(sum_ref.dtype)
                if k == n:
                    for k_open in range(max(0, n - 2), n):
                        for cp in pushes(k_open):
                            cp.wait_send()

    sems = pltpu.SemaphoreType.DMA((n, 4))
    return pl.pallas_call(
        body, name="pair_sum",
        grid=(n + 1,),
        in_specs=[pl.BlockSpec((dmain.shape[0], tc), lambda k: (0, jnp.minimum(k, n - 1))),
                  pl.BlockSpec((GLA_RANK, tc), lambda k: (0, jnp.minimum(k, n - 1)))],
        out_specs=pl.BlockSpec((4, SHARD_COLS, tc), lambda k: (0, 0, jnp.maximum(k - 1, 0))),
        out_shape=jax.ShapeDtypeStruct((4, SHARD_COLS, D), dmain.dtype),
        scratch_shapes=[pltpu.VMEM((2, N_DEV, SHARD_COLS, tc), dmain.dtype),
                        pltpu.VMEM((n, 4, SHARD_COLS, tc), dmain.dtype), sems, sems],
        compiler_params=_cparams(("arbitrary",)),
    )(dmain, drank)


def _group_row(g):
    return GLA_RANK * (g * (1024 // GLA_RANK) + (g >= RANK_COL // 1024))


def _proj_call(x, norm_g, wt, wr, wp_part):
    T, D = x.shape
    tm = min(1024, T)
    assert tm % TBLK == 0
    n_i = T // tm

    def f_slot(j):
        return ((j >= 2).astype(jnp.int32) + (j >= 6).astype(jnp.int32)
                + (j >= 7).astype(jnp.int32) + (j >= 8).astype(jnp.int32))

    def b_slot(j):
        return (j >= 3).astype(jnp.int32) + (j >= 4).astype(jnp.int32) + (j >= 5).astype(jnp.int32)

    def body(x_ref, g_ref, w_ref, wr_ref, wp_ref, pf_ref, pb_ref, rank_ref, ht_ref, wpall_ref,
             h_scr, send_sems, recv_sems, loc_sem):
        i = pl.program_id(0)
        j = pl.program_id(1)
        own, pairs = _push_copies(wp_ref, wpall_ref, send_sems, recv_sems, loc_sem, scatter=False)

        @pl.when((i == 0) & (j == 0))
        def _():
            _push_start(own, pairs)

        @pl.when(j == 0)
        def _():
            xv = x_ref[...]
            r = lax.rsqrt(jnp.mean(xv * xv, axis=-1, keepdims=True) + EPS)
            h = (xv * r) * g_ref[...]
            hb = _bf(h)
            h_scr[...] = hb
            for b in range(tm // TBLK):
                ht_ref[b] = _bf(h[b * TBLK:(b + 1) * TBLK].T)
            rank_ref[...] = _dot_nt(hb, wr_ref[...])

        is_b = (j == 1) | ((j >= 3) & (j <= 5))

        @pl.when(is_b)
        def _():
            pb_ref[...] = _bf(_dot_nt(h_scr[...], w_ref[...]))

        @pl.when(jnp.logical_not(is_b))
        def _():
            pf_ref[...] = _dot_nt(h_scr[...], w_ref[...])

        @pl.when((i == n_i - 1) & (j == N_GROUPS - 1))
        def _():
            _push_wait(own, pairs)

    return pl.pallas_call(
        body, name="proj",
        grid=(n_i, N_GROUPS),
        in_specs=[pl.BlockSpec((tm, D), lambda i, j: (i, 0)),
                  pl.BlockSpec((1, D), lambda i, j: (0, 0)),
                  pl.BlockSpec((pl.Element(1024), pl.Element(D)), lambda i, j: (_group_row(j), 0)),
                  pl.BlockSpec((128, D), lambda i, j: (0, 0)),
                  _ANY],
        out_specs=(pl.BlockSpec((None, tm, 1024), lambda i, j: (f_slot(j), i, 0)),
                   pl.BlockSpec((None, tm, 1024), lambda i, j: (b_slot(j), i, 0)),
                   pl.BlockSpec((tm, 128), lambda i, j: (i, 0)),
                   pl.BlockSpec((tm // TBLK, D, TBLK), lambda i, j: (i, 0, 0)),
                   _ANY),
        out_shape=(jax.ShapeDtypeStruct((5, T, 1024), F32),
                   jax.ShapeDtypeStruct((4, T, 1024), BF16),
                   jax.ShapeDtypeStruct((T, 128), F32),
                   jax.ShapeDtypeStruct((T // TBLK, D, TBLK), BF16),
                   jax.ShapeDtypeStruct((N_DEV,) + wp_part.shape, wp_part.dtype)),
        scratch_shapes=[pltpu.VMEM((tm, D), BF16)] + _PUSH_SEMS,
        compiler_params=_cparams(("arbitrary", "arbitrary")),
    )(x, norm_g, wt, wr, wp_part)


GLA_STEP_CHUNKS = 4


def _gla_same_chunk(rows):
    return (_iota2(rows, rows, 0) & -GLA_CHUNK) == (_iota2(rows, rows, 1) & -GLA_CHUNK)


def _gla_chunk_terms(la, q, k, n_c):
    C = GLA_CHUNK
    rows = n_c * C
    low = _gla_same_chunk(rows) & (_iota2(rows, rows, 0) >= _iota2(rows, rows, 1))
    b = _tri_left(_bf(low.astype(F32)), la)
    bl = [b[(c + 1) * C - 1:(c + 1) * C, :] for c in range(n_c)]
    bl_rows = jnp.concatenate([jnp.broadcast_to(bl[c], (C, b.shape[1])) for c in range(n_c)], axis=0)
    eb = jnp.exp(b)
    enb = jnp.exp(-b)
    ebl_b = jnp.exp(bl_rows - b)
    scale = GLA_HK ** -0.5
    qe = q * eb * scale
    ke = k * enb
    kd = k * ebl_b
    return bl, eb, enb, ebl_b, qe, ke, kd


def _gla_fwd_call(projf, projb, rank, wdec, bdec):
    T = projf.shape[1]
    C = GLA_CHUNK
    n_chunks = T // C
    n_c = GLA_STEP_CHUNKS
    R = n_c * C
    assert n_chunks % n_c == 0

    def body(qk_ref, v_ref, rank_ref, wd_ref, bd_ref, o_ref, st_ref, la_ref, st_scr):
        @pl.when(pl.program_id(0) == 0)
        def _():
            st_scr[...] = jnp.zeros_like(st_scr)

        dec = _dot(_bf(rank_ref[...]), _bf(wd_ref[...])) + bd_ref[...]
        la = (jnp.minimum(dec, 0.0) - _softplus_neg_abs(dec)) / GLA_TAU
        la_ref[...] = la
        mask = _gla_same_chunk(R) & (_iota2(R, R, 0) >= _iota2(R, R, 1))
        bl, _, _, _, qe, ke, kd = _gla_chunk_terms(la, qk_ref[:, :GLA_DK], qk_ref[:, GLA_DK:], n_c)
        qeb, keb, kdb = _bf(qe), _bf(ke), _bf(kd)
        ebl = [jnp.exp(bl[c]) for c in range(n_c)]
        heads = range(GLA_HEADS)
        ks = [slice(hh * GLA_HK, (hh + 1) * GLA_HK) for hh in heads]
        vs = [slice(hh * GLA_HV, (hh + 1) * GLA_HV) for hh in heads]
        rs = [slice(c * C, (c + 1) * C) for c in range(n_c)]
        p = [_bf(jnp.where(mask, _dot_nt(qeb[:, ks[hh]], keb[:, ks[hh]]), 0.0)) for hh in heads]
        upd = [[_dot_tn(v_ref[rs[c], vs[hh]], kdb[rs[c], ks[hh]]) for hh in heads] for c in range(n_c)]
        intra = [_dot(p[hh], v_ref[:, vs[hh]]) for hh in heads]
        st = [st_scr[hh] for hh in heads]
        for c in range(n_c):
            inter = [_dot_nt(qeb[rs[c], ks[hh]], _bf(st[hh])) for hh in heads]
            for hh in heads:
                st_ref[c, hh] = st[hh]
                o_ref[rs[c], vs[hh]] = intra[hh][rs[c]] + inter[hh]
            st = [st[hh] * ebl[c][:, ks[hh]] + upd[c][hh] for hh in heads]
        for hh in heads:
            st_scr[hh] = st[hh]

    return pl.pallas_call(
        body, name="gla_fwd",
        grid=(n_chunks // n_c,),
        in_specs=[pl.BlockSpec((None, R, 1024), lambda n: (0, n, 0)),
                  pl.BlockSpec((None, R, 1024), lambda n: (0, n, 0)),
                  pl.BlockSpec((R, 128), lambda n: (n, 0)),
                  pl.BlockSpec((128, GLA_DK), lambda n: (0, 0)),
                  pl.BlockSpec((1, GLA_DK), lambda n: (0, 0))],
        out_specs=(pl.BlockSpec((R, 1024), lambda n: (n, 0)),
                   pl.BlockSpec((n_c, GLA_HEADS, GLA_HV, GLA_HK), lambda n: (n, 0, 0, 0)),
                   pl.BlockSpec((R, GLA_DK), lambda n: (n, 0))),
        out_shape=(jax.ShapeDtypeStruct((T, 1024), F32),
                   jax.ShapeDtypeStruct((n_chunks, GLA_HEADS, GLA_HV, GLA_HK), F32),
                   jax.ShapeDtypeStruct((T, GLA_DK), F32)),
        scratch_shapes=[pltpu.VMEM((GLA_HEADS, GLA_HV, GLA_HK), F32)],
        compiler_params=_cparams(("arbitrary",)),
    )(projf, projb, rank, wdec, bdec)


def _gla_bwd_call(projf, projb, la, do_gla, st_all, rank, wdec):
    T = projf.shape[1]
    C = GLA_CHUNK
    n_chunks = T // C
    n_c = GLA_STEP_CHUNKS
    R = n_c * C
    assert n_chunks % n_c == 0
    last = n_chunks // n_c - 1

    def body(qk_ref, v_ref, la_ref, do_ref, st_ref, rank_ref, wd_ref,
             dqk_ref, dv_ref, drank_ref, dwd_ref, dbd_ref, dst_scr):
        @pl.when(pl.program_id(0) == 0)
        def _():
            dst_scr[...] = jnp.zeros_like(dst_scr)
            dwd_ref[...] = jnp.zeros_like(dwd_ref)
            dbd_ref[...] = jnp.zeros_like(dbd_ref)

        same = _gla_same_chunk(R)
        mask = same & (_iota2(R, R, 0) >= _iota2(R, R, 1))
        upp = _bf((same & (_iota2(R, R, 0) <= _iota2(R, R, 1))).astype(F32))
        scale = GLA_HK ** -0.5
        la = la_ref[...]
        bl, eb, enb, ebl_b, qe, ke, kd = _gla_chunk_terms(la, qk_ref[:, :GLA_DK], qk_ref[:, GLA_DK:], n_c)
        qeb, keb, kdb = _bf(qe), _bf(ke), _bf(kd)
        ebl = [jnp.exp(bl[c]) for c in range(n_c)]
        heads = range(GLA_HEADS)
        ks = [slice(hh * GLA_HK, (hh + 1) * GLA_HK) for hh in heads]
        vs = [slice(hh * GLA_HV, (hh + 1) * GLA_HV) for hh in heads]
        rs = [slice(c * C, (c + 1) * C) for c in range(n_c)]
        v = [v_ref[:, vs[hh]] for hh in heads]
        do = [_bf(do_ref[:, vs[hh]]) for hh in heads]
        p = [_bf(jnp.where(mask, _dot_nt(qeb[:, ks[hh]], keb[:, ks[hh]]), 0.0)) for hh in heads]
        dp = [_bf(jnp.where(mask, _dot_nt(do[hh], v[hh]), 0.0)) for hh in heads]
        dst_intra = [[_dot_tn(do[hh][rs[c]], qeb[rs[c], ks[hh]]) for hh in heads] for c in range(n_c)]
        dqe_inter = [[_dot(do[hh][rs[c]], _bf(st_ref[c, hh])) for hh in heads] for c in range(n_c)]
        dv_intra = [_dot_tn(p[hh], do[hh]) for hh in heads]
        dqe_intra = [_dot(dp[hh], keb[:, ks[hh]]) for hh in heads]
        dke = jnp.concatenate([_dot_tn(dp[hh], qeb[:, ks[hh]]) for hh in heads], axis=1)
        dstn = [dst_scr[hh] for hh in heads]
        dkd_c, dv_inter, debl = [None] * n_c, [None] * n_c, [None] * n_c
        for c in reversed(range(n_c)):
            dstnb = [_bf(dstn[hh]) for hh in heads]
            dkd_c[c] = jnp.concatenate([_dot(v[hh][rs[c]], dstnb[hh]) for hh in heads], axis=1)
            dv_inter[c] = [_dot_nt(kdb[rs[c], ks[hh]], dstnb[hh]) for hh in heads]
            debl[c] = jnp.concatenate(
                [jnp.sum(dstn[hh] * st_ref[c, hh], axis=0, keepdims=True) for hh in heads], axis=1)
            dstn = [dst_intra[c][hh] + dstn[hh] * ebl[c][:, ks[hh]] for hh in heads]
        for hh in heads:
            dst_scr[hh] = dstn[hh]
            dv_ref[:, vs[hh]] = _bf(dv_intra[hh] + jnp.concatenate([dv_inter[c][hh] for c in range(n_c)], axis=0))
        dqe = jnp.concatenate(
            [dqe_intra[hh] + jnp.concatenate([dqe_inter[c][hh] for c in range(n_c)], axis=0) for hh in heads], axis=1)
        dkd = jnp.concatenate(dkd_c, axis=0)
        dkd_kd = dkd * kd
        db = dqe * qe - dke * ke - dkd_kd
        dbl = jnp.concatenate(
            [jnp.broadcast_to(jnp.sum(dkd_kd[rs[c]], axis=0, keepdims=True) + ebl[c] * debl[c], (C, GLA_DK))
             for c in range(n_c)], axis=0)
        dla = _tri_left(upp, db) + dbl
        dqk_ref[:, :GLA_DK] = _bf(dqe * eb * scale)
        dqk_ref[:, GLA_DK:] = _bf(dke * enb + dkd * ebl_b)
        ddec = dla * (1.0 / GLA_TAU) * (1.0 - jnp.exp(GLA_TAU * la))
        ddecb = _bf(ddec)
        drank_ref[...] = _bf(_dot_nt(ddecb, _bf(wd_ref[...])))
        dwd_ref[...] += _dot_tn(_bf(rank_ref[...]), ddecb)
        dbd_ref[...] += jnp.sum(ddec, axis=0, keepdims=True)

    return pl.pallas_call(
        body, name="gla_bwd",
        grid=(n_chunks // n_c,),
        in_specs=[pl.BlockSpec((None, R, 1024), lambda n: (0, last - n, 0)),
                  pl.BlockSpec((None, R, 1024), lambda n: (0, last - n, 0)),
                  pl.BlockSpec((R, GLA_DK), lambda n: (last - n, 0)),
                  pl.BlockSpec((R, 1024), lambda n: (last - n, 0)),
                  pl.BlockSpec((n_c, GLA_HEADS, GLA_HV, GLA_HK), lambda n: (last - n, 0, 0, 0)),
                  pl.BlockSpec((R, 128), lambda n: (last - n, 0)),
                  pl.BlockSpec((128, GLA_DK), lambda n: (0, 0))],
        out_specs=(pl.BlockSpec((R, 1024), lambda n: (last - n, 0)),
                   pl.BlockSpec((R, 1024), lambda n: (last - n, 0)),
                   pl.BlockSpec((R, 128), lambda n: (last - n, 0)),
                   pl.BlockSpec((128, GLA_DK), lambda n: (0, 0)),
                   pl.BlockSpec((1, GLA_DK), lambda n: (0, 0))),
        out_shape=(jax.ShapeDtypeStruct((T, 1024), BF16),
                   jax.ShapeDtypeStruct((T, 1024), BF16),
                   jax.ShapeDtypeStruct((T, 128), BF16),
                   jax.ShapeDtypeStruct((128, GLA_DK), F32),
                   jax.ShapeDtypeStruct((1, GLA_DK), F32)),
        scratch_shapes=[pltpu.VMEM((GLA_HEADS, GLA_HV, GLA_HK), F32)],
        compiler_params=_cparams(("arbitrary",)),
    )(projf, projb, la, do_gla, st_all, rank, wdec)


def _sb_logs(z):
    lsz = jnp.minimum(z, 0.0) - _softplus_neg_abs(z)
    return lsz, lsz - z


SB_HG_FWD = 8
SB_HG_BWD = 4
SB_QUERIES = 256
SB_KEYS = 256
SB_DEAD = -105.0


def _sb_fwd_call(projb, wp_shard):
    T = projb.shape[1]
    B = min(SB_QUERIES, T)
    HG = SB_HG_FWD
    W = HG * SB_HD
    scale = 1.0 / math.sqrt(SB_HD)
    KB = min(SB_KEYS, T)
    n_h, n_i = SB_HEADS // HG, T // B

    def body(q_ref, k_ref, v_ref, wp_ref, o_ref, wpall_ref, cb_scr, send_sems, recv_sems, loc_sem):
        i = pl.program_id(1)
        own, pairs = _push_copies(wp_ref, wpall_ref, send_sems, recv_sems, loc_sem, scatter=False)

        @pl.when((pl.program_id(0) == 0) & (i == 0))
        def _():
            _push_start(own, pairs)

        rows = HG * B
        after = (_iota2(KB, KB, 0) > _iota2(KB, KB, 1)).astype(F32)
        tri = _bf(jnp.concatenate([after, jnp.ones((KB, KB), F32)], axis=1))
        o_ref[...] = jnp.zeros_like(o_ref)
        cb_scr[...] = jnp.zeros_like(cb_scr)

        def block(jp, masked):
            off = pl.multiple_of(jp * KB, KB)
            z = jnp.concatenate(
                [_dot_nt(q_ref[:, hh * SB_HD:(hh + 1) * SB_HD], k_ref[pl.ds(off, KB), hh * SB_HD:(hh + 1) * SB_HD])
                 for hh in range(HG)], axis=0) * scale
            lsz, l1m = _sb_logs(z)
            if masked:
                strict = (jp * KB + _iota2(rows, KB, 1)) < (i * B + (_iota2(rows, KB, 0) & (B - 1)))
                l1m = jnp.where(strict, l1m, 0.0)
            r = _tri2_right(l1m, tri)
            cb = cb_scr[...]
            a = jnp.exp(lsz + cb + r[:, :KB])
            if masked:
                a = jnp.where(strict, a, 0.0)
            cb_scr[...] = cb + r[:, KB:]
            ab = _bf(a)
            for hh in range(HG):
                cs = slice(hh * SB_HD, (hh + 1) * SB_HD)
                o_ref[:, cs] += _dot(ab[hh * B:(hh + 1) * B, :], v_ref[pl.ds(off, KB), cs])

        jp0 = (i * B) // KB
        block(jp0, True)

        def live(state):
            jj, dead = state
            return (jj <= jp0) & jnp.logical_not(dead)

        def step(state):
            jj, _ = state
            block(jp0 - jj, False)
            return jj + 1, jnp.max(cb_scr[:, :SB_HD]) < SB_DEAD

        lax.while_loop(live, step, (jnp.int32(1), jnp.max(cb_scr[:, :SB_HD]) < SB_DEAD))

        @pl.when((pl.program_id(0) == n_h - 1) & (i == n_i - 1))
        def _():
            _push_wait(own, pairs)

    return pl.pallas_call(
        body, name="sb_fwd",
        grid=(n_h, n_i),
        in_specs=[pl.BlockSpec((None, B, W), lambda h, i: (1, i, h)),
                  pl.BlockSpec((None, T, W), lambda h, i: (2, 0, h)),
                  pl.BlockSpec((None, T, W), lambda h, i: (3, 0, h)),
                  _ANY],
        out_specs=(pl.BlockSpec((B, W), lambda h, i: (i, h)), _ANY),
        out_shape=(jax.ShapeDtypeStruct((T, 1024), F32),
                   jax.ShapeDtypeStruct((N_DEV,) + wp_shard.shape, wp_shard.dtype)),
        scratch_shapes=[pltpu.VMEM((HG * B, KB), F32)] + _PUSH_SEMS,
        compiler_params=_cparams(("arbitrary", "arbitrary")),
    )(projb, projb, projb, wp_shard)


def _sb_bwd_call(projb, do_sb, g_p):
    T = projb.shape[1]
    B = min(SB_QUERIES, T)
    nb = T // B
    HG = SB_HG_BWD
    W = HG * SB_HD
    WQ = HG * B
    KB = min(SB_KEYS, T)
    nkb = T // KB
    n_h = SB_HEADS // HG
    scale = 1.0 / math.sqrt(SB_HD)

    def body(q_ref, k_ref, v_ref, do_ref, gp_ref, dq_ref, dk_ref, dv_ref, rp_ref,
             dk_scr, dv_scr, kt_scr, beta_scr, g_scr, dqt_scr, send_sems, recv_sems, loc_sem):
        i = pl.program_id(1)
        own, pairs = _push_copies(gp_ref, rp_ref, send_sems, recv_sems, loc_sem, scatter=True)

        @pl.when((pl.program_id(0) == 0) & (i == 0))
        def _():
            _push_start(own, pairs)

        @pl.when(i == 0)
        def _():
            dk_scr[...] = jnp.zeros_like(dk_scr)
            dv_scr[...] = jnp.zeros_like(dv_scr)
            for hh in range(HG):
                for jb in range(nkb):
                    kt_scr[hh, jb] = _bf(
                        k_ref[jb * KB:(jb + 1) * KB, hh * SB_HD:(hh + 1) * SB_HD].astype(F32).T)

        dqt_scr[...] = jnp.zeros_like(dqt_scr)
        later = _bf((_iota2(KB, KB, 1) > _iota2(KB, KB, 0)).astype(F32))
        earlier = _bf((_iota2(KB, KB, 1) < _iota2(KB, KB, 0)).astype(F32))
        dob = _bf(do_ref[...])
        jp0 = (i * B) // KB

        def strict_mask():
            return (jp0 * KB + _iota2(KB, WQ, 0)) < (i * B + (_iota2(KB, WQ, 1) & (B - 1)))

        def heads(fn):
            return [fn(slice(hh * SB_HD, (hh + 1) * SB_HD)) for hh in range(HG)]

        def pass1(jp, cb, masked):
            off = pl.multiple_of(jp * KB, KB)
            z = jnp.concatenate(heads(lambda cs: _dot_nt(k_ref[pl.ds(off, KB), cs], q_ref[:, cs])), axis=1) * scale
            da = jnp.concatenate(heads(lambda cs: _dot_nt(v_ref[pl.ds(off, KB), cs], dob[:, cs])), axis=1)
            lsz, l1m = _sb_logs(z)
            if masked:
                strict = strict_mask()
                l1m = jnp.where(strict, l1m, 0.0)
            a = jnp.exp(lsz + cb + _tri2_left(later, l1m))
            if masked:
                a = jnp.where(strict, a, 0.0)
            g_scr[jp] = a * da
            beta_scr[jp] = jnp.exp(lsz)
            ab = _bf(a)
            for hh in range(HG):
                cs = slice(hh * SB_HD, (hh + 1) * SB_HD)
                dv_scr[pl.ds(off, KB), cs] += _dot(ab[:, hh * B:(hh + 1) * B], dob[:, cs])
            return cb + jnp.sum(l1m, axis=0, keepdims=True)

        zero = jnp.zeros((1, WQ), F32)
        cb = pass1(jp0, zero, True)

        def live(state):
            jj, _, dead = state
            return (jj <= jp0) & jnp.logical_not(dead)

        def step(state):
            jj, cr, _ = state
            cr = pass1(jp0 - jj, cr, False)
            return jj + 1, cr, jnp.max(cr) < SB_DEAD

        n_done, _, _ = lax.while_loop(live, step, (jnp.int32(1), cb, jnp.max(cb) < SB_DEAD))
        jp_first = jp0 - (n_done - 1)

        def pass2(jp, cg, masked):
            off = pl.multiple_of(jp * KB, KB)
            g = g_scr[jp]
            beta = beta_scr[jp]
            dz = g * (1.0 - beta) - beta * (cg + _tri2_left(earlier, g))
            if masked:
                dz = jnp.where(strict_mask(), dz, 0.0)
            dzb = _bf(dz * scale)
            for hh in range(HG):
                cs = slice(hh * SB_HD, (hh + 1) * SB_HD)
                dk_scr[pl.ds(off, KB), cs] += _dot(dzb[:, hh * B:(hh + 1) * B], q_ref[:, cs])
                dqt_scr[hh] += _dot(kt_scr[hh, jp], dzb[:, hh * B:(hh + 1) * B])
            return cg + jnp.sum(g, axis=0, keepdims=True)

        cg = lax.fori_loop(jp_first, jp0, lambda jp, cr: pass2(jp, cr, False), zero)
        pass2(jp0, cg, True)
        for hh in range(HG):
            dq_ref[:, hh * SB_HD:(hh + 1) * SB_HD] = _bf(dqt_scr[hh].T)

        @pl.when(i == nb - 1)
        def _():
            dk_ref[...] = _bf(dk_scr[...])
            dv_ref[...] = _bf(dv_scr[...])

        @pl.when((pl.program_id(0) == n_h - 1) & (i == nb - 1))
        def _():
            _push_wait(own, pairs)

    return pl.pallas_call(
        body, name="sb_bwd",
        grid=(n_h, nb),
        in_specs=[pl.BlockSpec((None, B, W), lambda h, i: (1, i, h)),
                  pl.BlockSpec((None, T, W), lambda h, i: (2, 0, h)),
                  pl.BlockSpec((None, T, W), lambda h, i: (3, 0, h)),
                  pl.BlockSpec((B, W), lambda h, i: (i, h)),
                  _ANY],
        out_specs=(pl.BlockSpec((B, W), lambda h, i: (i, h)),
                   pl.BlockSpec((T, W), lambda h, i: (0, h)),
                   pl.BlockSpec((T, W), lambda h, i: (0, h)),
                   _ANY),
        out_shape=(jax.ShapeDtypeStruct((T, 1024), BF16),
                   jax.ShapeDtypeStruct((T, 1024), BF16),
                   jax.ShapeDtypeStruct((T, 1024), BF16),
                   jax.ShapeDtypeStruct(g_p.shape, g_p.dtype)),
        scratch_shapes=[pltpu.VMEM((T, W), F32), pltpu.VMEM((T, W), F32),
                        pltpu.VMEM((HG, nkb, SB_HD, KB), BF16),
                        pltpu.VMEM((nkb, KB, WQ), F32), pltpu.VMEM((nkb, KB, WQ), F32),
                        pltpu.VMEM((HG, SB_HD, B), F32)] + _PUSH_SEMS,
        compiler_params=_cparams(("arbitrary", "arbitrary")),
    )(projb, projb, projb, do_sb, g_p)


def _mid_call(o_gla, o_sb, projf, x, target, wpa, wpb, wo, gla_g, b_gate, final_g):
    T, D = x.shape
    tm = min(TBLK, T)

    def body(og_ref, ggate_ref, osb_ref, sgate_ref, ma_ref, mb_ref, x_ref, tgt_ref,
             wpa_ref, wpb_ref, wo_ref, glag_ref, bg_ref, fg_ref,
             dx2_ref, dogla_ref, dosb_ref, dggate_ref, dsgate_ref, dm_ref,
             mt_ref, ogt_ref, obt_ref, dx2b_ref, dya_ref, dyb_ref,
             dfg_ref, dbg_ref, dglag_ref, loss_ref):
        @pl.when(pl.program_id(0) == 0)
        def _():
            dfg_ref[...] = jnp.zeros_like(dfg_ref)
            dbg_ref[...] = jnp.zeros_like(dbg_ref)
            dglag_ref[...] = jnp.zeros_like(dglag_ref)
            loss_ref[...] = jnp.zeros_like(loss_ref)

        glag = glag_ref[...]
        ggate = ggate_ref[...]
        sg = _sigmoid(ggate)
        silu_g = ggate * sg
        ohat, rinv, nrm = [], [], []
        for hh in range(GLA_HEADS):
            oh = og_ref[:, hh * GLA_HV:(hh + 1) * GLA_HV]
            r = lax.rsqrt(jnp.mean(oh * oh, axis=-1, keepdims=True) + EPS)
            ohat.append(oh * r)
            rinv.append(r)
            nrm.append(ohat[-1] * glag)
        n_all = jnp.concatenate(nrm, axis=1)
        og = n_all * silu_g
        ogb = _bf(og)
        ya = _dot(ogb, wpa_ref[...])
        sgate = sgate_ref[...]
        ss = _sigmoid(sgate)
        silu_s = sgate * ss
        osb = osb_ref[...]
        ob = osb * silu_s
        obb = _bf(ob)
        yb = _dot(obb, wpb_ref[...])
        ga = _sigmoid(ma_ref[...] + bg_ref[:, :D])
        gb = _sigmoid(mb_ref[...] + bg_ref[:, D:])
        merged = ga * ya + gb * yb
        mgb = _bf(merged)
        x2 = x_ref[...] + _dot(mgb, wo_ref[...])
        r2 = lax.rsqrt(jnp.mean(x2 * x2, axis=-1, keepdims=True) + EPS)
        xh2 = x2 * r2
        fg = fg_ref[...]
        err = xh2 * fg - tgt_ref[...]
        loss_ref[...] += jnp.broadcast_to(
            0.5 * jnp.sum(jnp.mean(err * err, axis=-1, keepdims=True), axis=0, keepdims=True), (1, 128))
        dy = err * (1.0 / D)
        dfg_ref[...] += jnp.sum(dy * xh2, axis=0, keepdims=True)
        dxh = dy * fg
        dx2 = r2 * (dxh - xh2 * jnp.mean(dxh * xh2, axis=-1, keepdims=True))
        dx2_ref[...] = dx2
        dx2b = _bf(dx2)
        dx2b_ref[...] = dx2b
        dmerged = _dot_nt(dx2b, wo_ref[...])
        dya = dmerged * ga
        dyb = dmerged * gb
        dma = dmerged * ya * ga * (1.0 - ga)
        dmb = dmerged * yb * gb * (1.0 - gb)
        dm_ref[:, :D] = _bf(dma)
        dm_ref[:, D:] = _bf(dmb)
        dbg_ref[:, :D] += jnp.sum(dma, axis=0, keepdims=True)
        dbg_ref[:, D:] += jnp.sum(dmb, axis=0, keepdims=True)
        dyab = _bf(dya)
        dybb = _bf(dyb)
        dya_ref[...] = dyab
        dyb_ref[...] = dybb
        dog = _dot_nt(dyab, wpa_ref[...])
        dob = _dot_nt(dybb, wpb_ref[...])
        dosb_ref[...] = dob * silu_s
        dsgate_ref[...] = _bf(dob * osb * (ss * (1.0 + sgate * (1.0 - ss))))
        dn = dog * silu_g
        dggate_ref[...] = _bf(dog * n_all * (sg * (1.0 + ggate * (1.0 - sg))))
        dglag = jnp.zeros((1, GLA_HV), F32)
        for hh in range(GLA_HEADS):
            dnh = dn[:, hh * GLA_HV:(hh + 1) * GLA_HV]
            dglag = dglag + jnp.sum(dnh * ohat[hh], axis=0, keepdims=True)
            dohat = dnh * glag
            dogla_ref[:, hh * GLA_HV:(hh + 1) * GLA_HV] = rinv[hh] * (
                dohat - ohat[hh] * jnp.mean(dohat * ohat[hh], axis=-1, keepdims=True))
        dglag_ref[...] += dglag
        mt_ref[...] = _bf(merged.T)
        ogt_ref[...] = _bf(og.T)
        obt_ref[...] = _bf(ob.T)

    row = lambda i: (i, 0)
    const = lambda i: (0, 0)
    tile = pl.BlockSpec((tm, D), row)
    tile_t = pl.BlockSpec((None, D, tm), lambda i: (i, 0, 0))
    wspec = pl.BlockSpec((D, D), const)
    return pl.pallas_call(
        body, name="mid",
        grid=(T // tm,),
        in_specs=[tile,
                  pl.BlockSpec((None, tm, D), lambda i: (1, i, 0)),
                  tile,
                  pl.BlockSpec((None, tm, D), lambda i: (2, i, 0)),
                  pl.BlockSpec((None, tm, D), lambda i: (3, i, 0)),
                  pl.BlockSpec((None, tm, D), lambda i: (4, i, 0)),
                  tile, tile, wspec, wspec, wspec,
                  pl.BlockSpec((1, GLA_HV), const),
                  pl.BlockSpec((1, 2 * D), const),
                  pl.BlockSpec((1, D), const)],
        out_specs=(tile, tile, tile, tile, tile,
                   pl.BlockSpec((tm, 2 * D), row),
                   tile_t, tile_t, tile_t, tile, tile, tile,
                   pl.BlockSpec((1, D), const),
                   pl.BlockSpec((1, 2 * D), const),
                   pl.BlockSpec((1, GLA_HV), const),
                   pl.BlockSpec((1, 128), const)),
        out_shape=(jax.ShapeDtypeStruct((T, D), F32),
                   jax.ShapeDtypeStruct((T, D), F32),
                   jax.ShapeDtypeStruct((T, D), F32),
                   jax.ShapeDtypeStruct((T, D), BF16),
                   jax.ShapeDtypeStruct((T, D), BF16),
                   jax.ShapeDtypeStruct((T, 2 * D), BF16),
                   jax.ShapeDtypeStruct((T // tm, D, tm), BF16),
                   jax.ShapeDtypeStruct((T // tm, D, tm), BF16),
                   jax.ShapeDtypeStruct((T // tm, D, tm), BF16),
                   jax.ShapeDtypeStruct((T, D), BF16),
                   jax.ShapeDtypeStruct((T, D), BF16),
                   jax.ShapeDtypeStruct((T, D), BF16),
                   jax.ShapeDtypeStruct((1, D), F32),
                   jax.ShapeDtypeStruct((1, 2 * D), F32),
                   jax.ShapeDtypeStruct((1, GLA_HV), F32),
                   jax.ShapeDtypeStruct((1, 128), F32)),
        compiler_params=_cparams(("arbitrary",)),
    )(o_gla, projf, o_sb, projf, projf, projf, x, target, wpa, wpb, wo, gla_g, b_gate, final_g)


def _dh_call(pieces, dmlog, drank, wt, wr, x, dx2, norm_g, s_in, small):
    T, D = x.shape
    tm = min(256, T)
    npc = len(pieces)
    n_main = N_GROUPS * 1024
    n_i = T // tm
    i_forward = 5 * n_i // 8

    def body(*refs):
        pcs = refs[:npc]
        (dm_ref, dr_ref, w_hbm, wr_ref, x_ref, dx2_ref, g_ref, sin_ref, small_ref,
         gx_ref, rin_ref, relayed_ref, rsmall_ref,
         w_scr, sems, dg_ref, small_mine, small_send, small_recv, small_loc, *exchange_scratch) = refs[npc:]
        start, forward, finish = _chip_reduce_steps(sin_ref, rin_ref, relayed_ref, *exchange_scratch)

        @pl.when(pl.program_id(0) == 0)
        def _():
            start()
            lo = pltpu.make_async_copy(w_hbm.at[pl.ds(0, RANK_COL)], w_scr.at[pl.ds(0, RANK_COL)], sems.at[0])
            hi = pltpu.make_async_copy(w_hbm.at[pl.ds(RANK_COL + GLA_RANK, n_main - RANK_COL)],
                                       w_scr.at[pl.ds(RANK_COL, n_main - RANK_COL)], sems.at[1])
            lo.start()
            hi.start()
            dg_ref[...] = jnp.zeros_like(dg_ref)
            lo.wait()
            hi.wait()

        @pl.when(pl.program_id(0) == i_forward)
        def _():
            forward()

        def w_group(g):
            return w_scr[g * 1024:(g + 1) * 1024, :]

        dr = dr_ref[...]
        dh = _dot(dr, wr_ref[...])
        for g in range(npc):
            dh = dh + _dot(pcs[g][...], w_group(g))
        dh = dh + _dot(dm_ref[:, :D], w_group(npc))
        dh = dh + _dot(dm_ref[:, D:], w_group(npc + 1))
        xv = x_ref[...]
        r = lax.rsqrt(jnp.mean(xv * xv, axis=-1, keepdims=True) + EPS)
        xhat = xv * r
        g = g_ref[...]
        dg_ref[...] += jnp.sum(dh * xhat, axis=0, keepdims=True)
        dxhat = dh * g
        gx_ref[...] = r * (dxhat - xhat * jnp.mean(dxhat * xhat, axis=-1, keepdims=True)) + dx2_ref[...]

        @pl.when(pl.program_id(0) == n_i - 1)
        def _():
            small_mine[...] = small_ref[...]
            small_mine[:, _SM_NORM:_SM_NORM + D] = dg_ref[...]
            own, pairs = _push_copies(small_mine, rsmall_ref, small_send, small_recv, small_loc, scatter=False)
            _push_start(own, pairs)
            finish()
            _push_wait(own, pairs)

    row = lambda i: (i, 0)
    const = lambda i: (0, 0)
    tile = pl.BlockSpec((tm, D), row)
    part = s_in.shape[1:]
    return pl.pallas_call(
        body, name="dh",
        grid=(n_i,),
        in_specs=[tile] * npc + [
            pl.BlockSpec((tm, 2 * D), row),
            pl.BlockSpec((tm, 128), row),
            _ANY,
            pl.BlockSpec((128, D), const),
            tile, tile,
            pl.BlockSpec((1, D), const),
            _ANY,
            pl.BlockSpec(small.shape, const)],
        out_specs=(tile, _ANY, _ANY, _ANY),
        out_shape=(jax.ShapeDtypeStruct((T, D), F32),
                   jax.ShapeDtypeStruct((3,) + part, s_in.dtype),
                   jax.ShapeDtypeStruct(part, s_in.dtype),
                   jax.ShapeDtypeStruct((N_DEV,) + small.shape, small.dtype)),
        scratch_shapes=[pltpu.VMEM((n_main, D), BF16), pltpu.SemaphoreType.DMA((2,)),
                        pltpu.VMEM((1, D), F32), pltpu.VMEM(small.shape, small.dtype)]
        + _PUSH_SEMS + _chip_reduce_scratch(*part, s_in.dtype),
        compiler_params=_cparams(("arbitrary",)),
    )(*pieces, dmlog, drank, wt, wr, x, dx2, norm_g, s_in, small)


def _wgrad_call(lhs_list, lhs_of_group, rhs_list, rhs_of_group, n_transposed, name, narrow=None):
    n_groups = len(rhs_of_group)
    n_tb, D, tb = lhs_list[0].shape
    T = n_tb * tb
    per = min(4, n_tb)
    tk = per * tb
    nk = T // tk
    nl = len(lhs_list)
    extra = [] if narrow is None else [narrow]

    def tokens_side_by_side(lref):
        return jnp.concatenate([lref[b] for b in range(per)], axis=1)

    def body(*refs):
        lhs = refs[:nl]
        rhs = refs[nl:nl + n_groups]
        rest = refs[nl + n_groups:]
        g = pl.program_id(0)
        i = pl.program_id(1)
        if narrow is None:
            out_ref, acc = rest
        else:
            narrow_ref, out_ref, narrow_out, acc, narrow_acc = rest

            @pl.when((g == 0) & (i == 0))
            def _():
                narrow_acc[...] = jnp.zeros_like(narrow_acc)

            @pl.when(g == 0)
            def _():
                narrow_acc[...] += _dot(tokens_side_by_side(lhs[lhs_of_group[0]]), narrow_ref[...])

            @pl.when((g == 0) & (i == nk - 1))
            def _():
                narrow_out[...] = _bf(narrow_acc[...].T)

        @pl.when(i == 0)
        def _():
            acc[...] = jnp.zeros_like(acc)

        for p in range(n_groups):
            @pl.when(g == p)
            def _(p=p):
                acc[...] += _dot(tokens_side_by_side(lhs[lhs_of_group[p]]), rhs[p][...])

        @pl.when((i == nk - 1) & (g < n_transposed))
        def _():
            out_ref[...] = _bf(acc[...].T)

        @pl.when((i == nk - 1) & (g >= n_transposed))
        def _():
            out_ref[...] = _bf(acc[...])

    def lhs_spec(a):
        groups = [g for g in range(n_groups) if lhs_of_group[g] == a]
        lo, hi = min(groups), max(groups)
        assert groups == list(range(lo, hi + 1))
        return pl.BlockSpec((per, D, tb), lambda g, i: (jnp.where((g >= lo) & (g <= hi), i, 0), 0, 0))

    def rhs_spec(p):
        cb = rhs_of_group[p][1]
        return pl.BlockSpec((tk, 1024), lambda g, i: (jnp.where(g == p, i, 0), cb))

    res = pl.pallas_call(
        body, name=name,
        grid=(n_groups, nk),
        in_specs=[lhs_spec(a) for a in range(nl)] + [rhs_spec(p) for p in range(n_groups)]
        + [pl.BlockSpec((tk, 128), lambda g, i: (jnp.where(g == 0, i, 0), 0)) for _ in extra],
        out_specs=[pl.BlockSpec((None, D, 1024), lambda g, i: (g, 0, 0))]
        + [pl.BlockSpec((128, D), lambda g, i: (0, 0)) for _ in extra],
        out_shape=[jax.ShapeDtypeStruct((n_groups, D, 1024), BF16)]
        + [jax.ShapeDtypeStruct((128, D), BF16) for _ in extra],
        scratch_shapes=[pltpu.VMEM((D, 1024), F32)] + [pltpu.VMEM((D, 128), F32) for _ in extra],
        compiler_params=_cparams(("arbitrary", "arbitrary")),
    )(*lhs_list, *[rhs_list[rhs_of_group[p][0]] for p in range(n_groups)], *extra)
    return res[0] if narrow is None else res


def _adamw_math(parts, w, m, v):
    g = parts[0].astype(F32)
    for p in parts[1:]:
        g = g + p.astype(F32)
    mm = ADAM_B1 * m + (1.0 - ADAM_B1) * g
    vv = ADAM_B2 * v + (1.0 - ADAM_B2) * (g * g)
    m_hat = mm / (1.0 - ADAM_B1 ** ADAM_STEP)
    v_hat = vv / (1.0 - ADAM_B2 ** ADAM_STEP)
    return g, -ADAM_LR * (m_hat / (jnp.sqrt(v_hat) + ADAM_EPS) + ADAM_WD * w), mm, vv


def _part_order(n_parts):
    return [n_parts - 1] + list(range(n_parts - 1))


def _adamw_call(parts, w, m, v, name):
    R, C = w.shape
    n_parts = parts.shape[0]
    (tr, tc), grid, idx = _tiling_2d(R, C, 512)

    def body(p_ref, w_ref, m_ref, v_ref, g_ref, d_ref, nm_ref, nv_ref):
        g_ref[...], d_ref[...], nm_ref[...], nv_ref[...] = _adamw_math(
            [p_ref[k] for k in _part_order(n_parts)], w_ref[...], m_ref[...], v_ref[...])

    blk = pl.BlockSpec((tr, tc), idx)
    sds = jax.ShapeDtypeStruct((R, C), F32)
    return pl.pallas_call(
        body, name=name,
        grid=grid,
        in_specs=[pl.BlockSpec((n_parts, tr, tc), lambda i: (0,) + idx(i)), blk, blk, blk],
        out_specs=(blk, blk, blk, blk),
        out_shape=(sds, sds, sds, sds),
        compiler_params=_cparams(("arbitrary",)),
    )(parts, w, m, v)


def _adamw_rows_call(parts, ws, ms, vs, name):
    n = len(ws)
    R, C = ws[0].shape
    n_parts = parts.shape[0]

    def body(*refs):
        p_ref = refs[0]
        w_refs, m_refs, v_refs = refs[1:1 + n], refs[1 + n:1 + 2 * n], refs[1 + 2 * n:1 + 3 * n]
        outs = refs[1 + 3 * n:]
        for k in range(n):
            @pl.when(pl.program_id(0) == k)
            def _(k=k):
                res = _adamw_math([p_ref[j] for j in _part_order(n_parts)],
                                  w_refs[k][...], m_refs[k][...], v_refs[k][...])
                for o_ref, val in zip(outs[4 * k:4 * k + 4], res):
                    o_ref[...] = val

    whole = pl.BlockSpec((R, C), lambda k: (0, 0))
    sds = jax.ShapeDtypeStruct((R, C), F32)
    res = pl.pallas_call(
        body, name=name,
        grid=(n,),
        in_specs=[pl.BlockSpec((n_parts, R, C), lambda k: (0, k, 0))] + [whole] * (3 * n),
        out_specs=tuple([whole] * (4 * n)),
        out_shape=tuple([sds] * (4 * n)),
        compiler_params=_cparams(("arbitrary",)),
    )(parts, *ws, *ms, *vs)
    return [res[4 * k:4 * k + 4] for k in range(n)]


def _adamw_lanes_call(parts, offsets, ws, ms, vs, name):
    n = len(ws)
    n_parts = parts.shape[0]

    def body(*refs):
        p_ref = refs[0]
        w_refs, m_refs, v_refs = refs[1:1 + n], refs[1 + n:1 + 2 * n], refs[1 + 2 * n:1 + 3 * n]
        outs = refs[1 + 3 * n:]
        for k in range(n):
            lanes = slice(offsets[k], offsets[k] + ws[k].shape[1])
            res = _adamw_math([p_ref[j, :, lanes] for j in _part_order(n_parts)],
                              w_refs[k][...], m_refs[k][...], v_refs[k][...])
            for o_ref, val in zip(outs[4 * k:4 * k + 4], res):
                o_ref[...] = val

    res = pl.pallas_call(
        body, name=name,
        out_shape=tuple(jax.ShapeDtypeStruct(ws[k].shape, F32) for k in range(n) for _ in range(4)),
        compiler_params=_cparams(),
    )(parts, *ws, *ms, *vs)
    return [res[4 * k:4 * k + 4] for k in range(n)]


def _local_step(x, target, wt, wr, wdec, bdec, wp_shard, norm_g, gla_g, b_gate, final_g):
    D = x.shape[1]
    half = wp_shard.shape[1] // 2
    projf, projb, rank, ht, wp_lo = _proj_call(x, norm_g, wt, wr, wp_shard[:, :half])
    o_gla, st_all, la = _gla_fwd_call(projf, projb, rank, wdec, bdec)
    o_sb, wp_hi = _sb_fwd_call(projb, wp_shard[:, half:])
    wp_full = jnp.concatenate([wp_lo, wp_hi], axis=2).transpose(1, 0, 2, 3).reshape(3, D, D)
    (dx2, do_gla, do_sb, dggate, dsgate, dmlog, mt, ogt, obt, dx2b, dya, dyb,
     dfinal_g, db_gate, dgla_g, loss) = _mid_call(o_gla, o_sb, projf, x, target, wp_full[0], wp_full[1],
                                                 wp_full[2], gla_g, b_gate, final_g)
    dw_p = _wgrad_call([ogt, obt, mt], [0, 1, 2], [dya, dyb, dx2b], [(0, 0), (1, 0), (2, 0)], 0, "wgrad_p")
    g_p = dw_p.reshape(3, N_DEV, D // N_DEV, D).transpose(1, 0, 2, 3).reshape(N_DEV, 3 * (D // N_DEV), D)
    dqk, dgv, drank, dwdec, dbdec = _gla_bwd_call(projf, projb, la, do_gla, st_all, rank, wdec)
    dsq, dsk, dsv, r_p = _sb_bwd_call(projb, do_sb, g_p)
    pieces = [dqk, dgv, dggate, dsq, dsk, dsv, dsgate]
    rhs_of_group = [(g, 0) for g in range(7)] + [(7, 0), (7, 1)]
    dw_in, dwr = _wgrad_call([ht], [0] * N_GROUPS, pieces + [dmlog], rhs_of_group, N_GROUPS, "wgrad_in",
                             narrow=drank)
    s_in = _pair_sum_call(dw_in.reshape(N_GROUPS * 1024, D), dwr)
    small = jnp.concatenate([
        jnp.zeros((D,), F32), dbdec.reshape(-1), dgla_g.reshape(-1), db_gate.reshape(-1), dfinal_g.reshape(-1),
        loss.reshape(-1), dwdec[:GLA_RANK].reshape(-1)]).reshape(1, _SM_LEN)
    grad_x, r_in, _, r_small = _dh_call(pieces, dmlog, drank, wt, wr, x, dx2, norm_g, s_in, small)
    return grad_x, r_in, r_p, r_small


_SM_NORM = 0
_SM_BDEC = _SM_NORM + D_MODEL
_SM_GLAG = _SM_BDEC + GLA_DK
_SM_BGATE = _SM_GLAG + GLA_HV
_SM_FINAL = _SM_BGATE + 2 * D_MODEL
_SM_REPL = _SM_FINAL + D_MODEL
_SM_LOSS = _SM_REPL
_SM_WDEC = _SM_LOSS + 128
_SM_LEN = _SM_WDEC + GLA_RANK * GLA_DK


def kernel(x, norm_g, w_in, w_dec_up, b_dec, gla_norm_g, w_pa, w_pb, b_gate, w_o, final_g, loss_target, m_norm_g, m_w_in, m_w_dec_up, m_b_dec, m_gla_norm_g, m_w_pa, m_w_pb, m_b_gate, m_w_o, m_final_g, v_norm_g, v_w_in, v_w_dec_up, v_b_dec, v_gla_norm_g, v_w_pa, v_w_pb, v_b_gate, v_w_o, v_final_g):
    D = D_MODEL
    me = 4 * lax.axis_index("x") + 2 * lax.axis_index("y") + lax.axis_index("c")

    wp_shard = jnp.stack([w_pa, w_pb, w_o]).astype(BF16)
    n_first = _half_rows(SHARD_COLS)
    win_all, wdec_all = _all_gather([w_in.T.astype(BF16), w_dec_up], "gather_w",
                                    row_pieces=[[(0, n_first), (n_first, SHARD_COLS - n_first)], None])
    wt = _flatten_blocks_call(win_all)
    wr = jnp.pad(wt[RANK_COL:RANK_COL + GLA_RANK], ((0, 128 - GLA_RANK), (0, 0)))
    wdec_full = wdec_all.transpose(1, 0, 2).reshape(GLA_RANK, GLA_DK)
    wdec = jnp.pad(wdec_full, ((0, 128 - GLA_RANK), (0, 0)))

    grad_x, r_in, r_p, r_small = _local_step(
        x[0], loss_target[0], wt, wr, wdec, b_dec.reshape(1, -1), wp_shard,
        norm_g.reshape(1, -1), gla_norm_g.reshape(1, -1), b_gate.reshape(1, -1), final_g.reshape(1, -1))

    gw_in, d_in, nm_in, nv_in = (a.T for a in _adamw_call(r_in, w_in.T, m_w_in.T, v_w_in.T, "adamw_in"))
    (g_pa, d_pa, nm_pa, nv_pa), (g_pb, d_pb, nm_pb, nv_pb), (g_o, d_o, nm_o, nv_o) = _adamw_rows_call(
        r_p, [w_pa, w_pb, w_o], [m_w_pa, m_w_pb, m_w_o], [v_w_pa, v_w_pb, v_w_o], "adamw_p")

    def row(a):
        return a.reshape(1, -1)

    rep = _adamw_lanes_call(
        r_small, [_SM_NORM, _SM_BDEC, _SM_GLAG, _SM_BGATE, _SM_FINAL],
        [row(a) for a in (norm_g, b_dec, gla_norm_g, b_gate, final_g)],
        [row(a) for a in (m_norm_g, m_b_dec, m_gla_norm_g, m_b_gate, m_final_g)],
        [row(a) for a in (v_norm_g, v_b_dec, v_gla_norm_g, v_b_gate, v_final_g)], "adamw_rep")
    ((g_norm, d_norm, nm_norm, nv_norm), (g_bdec, d_bdec, nm_bdec, nv_bdec), (g_glag, d_glag, nm_glag, nv_glag),
     (g_bgate, d_bgate, nm_bgate, nv_bgate), (g_final, d_final, nm_final, nv_final)) = [
        tuple(a.reshape(-1) for a in quad) for quad in rep]

    wdec_parts = r_small[:, 0, _SM_WDEC:].reshape(N_DEV, GLA_RANK, GLA_DK)
    cols = GLA_DK // N_DEV
    wdec_mine = lax.dynamic_slice_in_dim(wdec_parts, me * cols, cols, axis=2)
    g_wdec, d_wdec, nm_wdec, nv_wdec = _adamw_call(wdec_mine, w_dec_up, m_w_dec_up, v_w_dec_up, "adamw_dec")

    loss_total = jnp.sum(r_small[:, 0, _SM_LOSS])

    return (loss_total, grad_x[None],
            g_norm, gw_in, g_wdec, g_bdec, g_glag, g_pa, g_pb, g_bgate, g_o, g_final,
            d_norm, d_in, d_wdec, d_bdec, d_glag, d_pa, d_pb, d_bgate, d_o, d_final,
            nm_norm, nm_in, nm_wdec, nm_bdec, nm_glag, nm_pa, nm_pb, nm_bgate, nm_o, nm_final,
            nv_norm, nv_in, nv_wdec, nv_bdec, nv_glag, nv_pa, nv_pb, nv_bgate, nv_o, nv_final)
```

```python
import math

import jax
import jax.numpy as jnp
from jax import lax
from jax.experimental import pallas as pl
from jax.experimental.pallas import tpu as pltpu

F32 = jnp.float32
BF16 = jnp.bfloat16

N_DEV = 8
D_MODEL = 1024
GLA_HEADS = 4
GLA_HK = 128
GLA_HV = 256
GLA_DK = 512
GLA_RANK = 16
GLA_TAU = 16.0
GLA_CHUNK = 64
SB_HEADS = 8
SB_HD = 128
EPS = 1e-6
N_GROUPS = 9
RANK_COL = 3072
IN_COLS = 9232
SHARD_COLS = IN_COLS // N_DEV

ADAM_LR = 0.001
ADAM_B1 = 0.9
ADAM_B2 = 0.999
ADAM_EPS = 1e-08
ADAM_WD = 0.01
ADAM_STEP = 10

VMEM_LIMIT = 56 * 1024 * 1024
TBLK = 256


def _cparams(sem=None):
    return pltpu.CompilerParams(dimension_semantics=sem, vmem_limit_bytes=VMEM_LIMIT)


def _tiling_2d(rows, cols, band_cols):
    if rows * cols <= 128 * 1024:
        return (rows, cols), (1,), lambda i: (0, 0)
    if rows % 128 == 0:
        return (128, cols), (rows // 128,), lambda i: (i, 0)
    tc = band_cols if cols % band_cols == 0 else cols
    return (rows, tc), (cols // tc,), lambda i: (0, i)


def _dot(a, b):
    return jnp.dot(a, b, preferred_element_type=F32)


def _dot_nt(a, b):
    return lax.dot_general(a, b, (((1,), (1,)), ((), ())), preferred_element_type=F32)


def _dot_tn(a, b):
    return lax.dot_general(a, b, (((0,), (0,)), ((), ())), preferred_element_type=F32)


def _bf(x):
    return x.astype(BF16)


def _split3(x):
    hi = x.astype(BF16)
    r = x - hi.astype(F32)
    mid = r.astype(BF16)
    lo = (r - mid.astype(F32)).astype(BF16)
    return hi, mid, lo


def _tri_left(tri, x):
    hi, mid, lo = _split3(x)
    return _dot(tri, hi) + _dot(tri, mid) + _dot(tri, lo)


def _split2(x):
    hi = lax.bitcast_convert_type(lax.bitcast_convert_type(x, jnp.uint32) & jnp.uint32(0xFFFF0000), F32)
    return hi.astype(BF16), (x - hi).astype(BF16)


def _tri2_left(tri, x):
    hi, lo = _split2(x)
    return _dot(tri, hi) + _dot(tri, lo)


def _tri2_right(x, tri):
    hi, lo = _split2(x)
    return _dot(hi, tri) + _dot(lo, tri)


def _iota2(n, m, dim):
    return lax.broadcasted_iota(jnp.int32, (n, m), dim)


def _sigmoid(x):
    return 1.0 / (1.0 + jnp.exp(-x))


def _softplus_neg_abs(z):
    return jnp.log(1.0 + jnp.exp(-jnp.abs(z)))


_ANY = pl.BlockSpec(memory_space=pl.ANY)


def _mesh_pos():
    return lax.axis_index("x"), lax.axis_index("y"), lax.axis_index("c")


def _other_chips(x, y):
    return [(1 - x, y), (x, 1 - y), (1 - x, 1 - y)]


def _rcopy(src, dst, send_sem, recv_sem, to):
    return pltpu.make_async_remote_copy(src_ref=src, dst_ref=dst, send_sem=send_sem, recv_sem=recv_sem,
                                        device_id=to, device_id_type=pl.DeviceIdType.MESH)


def _push_copies(src_ref, dst_ref, send_sems, recv_sems, loc_sem, scatter):
    x, y, c = _mesh_pos()
    me = 4 * x + 2 * y + c
    own = pltpu.make_async_copy(src_ref.at[me] if scatter else src_ref, dst_ref.at[me], loc_sem)
    pairs = []
    for k in range(1, N_DEV):
        px = 1 - x if k & 4 else x
        py = 1 - y if k & 2 else y
        pc = 1 - c if k & 1 else c
        pid = 4 * px + 2 * py + pc
        src = src_ref.at[pid] if scatter else src_ref
        send = _rcopy(src, dst_ref.at[me], send_sems.at[k - 1], recv_sems.at[k - 1], (px, py, pc))
        recv = _rcopy(src, dst_ref.at[pid], send_sems.at[k - 1], recv_sems.at[k - 1], (px, py, pc))
        pairs.append((send, recv))
    return own, pairs


def _push_start(own, pairs):
    own.start()
    for send, _ in pairs:
        send.start()


def _push_wait(own, pairs):
    for _, recv in pairs:
        recv.wait_recv()
    for send, _ in pairs:
        send.wait_send()
    own.wait()


_PUSH_SEMS = [pltpu.SemaphoreType.DMA((N_DEV - 1,)), pltpu.SemaphoreType.DMA((N_DEV - 1,)),
              pltpu.SemaphoreType.DMA]


def _half_rows(rows):
    return (rows // 2) // 16 * 16


_ADD_ROWS = 128


def _chip_reduce_steps(src_ref, dst_ref, relayed_ref, sum_x, sum_y, rel_x, rel_y, load_sems, send_sems, recv_sems,
                       loc_sem):
    _, R, C = src_ref.shape
    n0 = _half_rows(R)
    lo, hi = pl.ds(0, n0), pl.ds(n0, R - n0)
    x, y, c = _mesh_pos()
    (xx, xy), (yx, yy), (dx, dy) = _other_chips(x, y)
    to_diag, to_x, to_y = src_ref.at[2 * dx + dy], src_ref.at[2 * xx + xy], src_ref.at[2 * yx + yy]
    x_nb, y_nb = (xx, xy, c), (yx, yy, c)
    relays = (_rcopy(to_diag.at[lo], relayed_ref.at[lo], send_sems.at[0], recv_sems.at[0], x_nb),
              _rcopy(to_diag.at[hi], relayed_ref.at[hi], send_sems.at[1], recv_sems.at[1], y_nb))
    plain = (_rcopy(to_x.at[lo], dst_ref.at[0, lo], send_sems.at[2], recv_sems.at[2], x_nb),
             _rcopy(to_y.at[hi], dst_ref.at[1, hi], send_sems.at[3], recv_sems.at[3], y_nb))
    summed = (_rcopy(sum_x, dst_ref.at[0, hi], send_sems.at[4], recv_sems.at[4], x_nb),
              _rcopy(sum_y, dst_ref.at[1, lo], send_sems.at[5], recv_sems.at[5], y_nb))
    load_mine = (pltpu.make_async_copy(to_x.at[hi], sum_x, load_sems.at[0]),
                 pltpu.make_async_copy(to_y.at[lo], sum_y, load_sems.at[1]))
    load_relayed = (pltpu.make_async_copy(relayed_ref.at[hi], rel_x, load_sems.at[2]),
                    pltpu.make_async_copy(relayed_ref.at[lo], rel_y, load_sems.at[3]))
    own = pltpu.make_async_copy(src_ref.at[2 * x + y], dst_ref.at[2], loc_sem)

    def start():
        for cp in relays + (own,) + load_mine:
            cp.start()

    def follow():
        for cp in plain:
            cp.start()

    def add(acc_ref, rel_ref):
        for r0 in range(0, acc_ref.shape[0], _ADD_ROWS):
            rows = slice(r0, min(r0 + _ADD_ROWS, acc_ref.shape[0]))
            acc_ref[rows, :] = (acc_ref[rows, :].astype(F32) + rel_ref[rows, :].astype(F32)).astype(acc_ref.dtype)

    def forward():
        for cp in relays:
            cp.wait_recv()
        for cp in load_relayed:
            cp.start()
        for cp in load_mine + load_relayed:
            cp.wait()
        add(sum_x, rel_x)
        add(sum_y, rel_y)
        for cp in summed:
            cp.start()

    def finish():
        for cp in plain + summed:
            cp.wait_recv()
        for cp in relays + plain + summed:
            cp.wait_send()
        own.wait()

    return start, follow, forward, finish


def _chip_reduce_scratch(rows, cols, dtype):
    n0 = _half_rows(rows)
    return [pltpu.VMEM((rows - n0, cols), dtype), pltpu.VMEM((n0, cols), dtype)] * 2 + [
        pltpu.SemaphoreType.DMA((4,)), pltpu.SemaphoreType.DMA((6,)), pltpu.SemaphoreType.DMA((6,)),
        pltpu.SemaphoreType.DMA]


def _all_gather(arrs, name, row_pieces=None):
    n = len(arrs)
    pieces = [[None] if not row_pieces or not row_pieces[a] else list(row_pieces[a]) for a in range(n)]
    assert all(len(p) in (1, 2) for p in pieces)
    units = [(a, i) for a in range(n) for i in range(len(pieces[a]))]

    def body(*refs):
        ins = refs[:n]
        outs = refs[n:2 * n]
        send_sems, recv_sems, loc_sems = refs[2 * n:]
        x, y, c = _mesh_pos()
        me, sib = (x, y, c), (x, y, 1 - c)
        xn, yn, dg = [(px, py, c) for px, py in _other_chips(x, y)]

        def rows(ref, a, i):
            return ref if pieces[a][i] is None else ref.at[pl.ds(*pieces[a][i])]

        def copy(u, k, block, to, own=False):
            a, i = u
            px, py, pc = block
            dst = rows(outs[a].at[4 * px + 2 * py + pc], a, i)
            return _rcopy(rows(ins[a], a, i) if own else dst, dst, send_sems.at[a, k, i], recv_sems.at[a, k, i], to)

        started = []

        def start(cp):
            cp.start()
            started.append(cp)

        def landed_then_pass_on(u, k, block):
            copy(u, k, block, me).wait_recv()
            start(copy(u, 3 + k, block, sib))

        mine = [pltpu.make_async_copy(ins[a], outs[a].at[4 * x + 2 * y + c], loc_sems.at[a]) for a in range(n)]
        for cp in mine:
            cp.start()
        for u in units:
            start(copy(u, 0, me, sib, own=True))
        for a in range(n):
            if len(pieces[a]) == 2:
                for i, to, k in ((0, xn, 1), (1, yn, 2), (1, xn, 1), (0, yn, 2)):
                    start(copy((a, i), k, me, to, own=True))
            else:
                for to, k in ((xn, 1), (yn, 2), (dg, 3)):
                    start(copy((a, 0), k, me, to, own=True))
        for a in range(n):
            if len(pieces[a]) == 2:
                landed_then_pass_on((a, 0), 1, xn)
                start(copy((a, 0), 3, xn, yn))
                landed_then_pass_on((a, 1), 2, yn)
                start(copy((a, 1), 3, yn, xn))
                landed_then_pass_on((a, 1), 1, xn)
                landed_then_pass_on((a, 0), 2, yn)
                landed_then_pass_on((a, 0), 3, dg)
                landed_then_pass_on((a, 1), 3, dg)
            else:
                for block, k in ((xn, 1), (yn, 2), (dg, 3)):
                    landed_then_pass_on((a, 0), k, block)
        for u in units:
            copy(u, 0, sib, me).wait_recv()
            for k, (px, py, _) in ((4, xn), (5, yn), (6, dg)):
                copy(u, k, (px, py, 1 - c), me).wait_recv()
        for cp in started:
            cp.wait_send()
        for cp in mine:
            cp.wait()

    n_pc = max(len(p) for p in pieces)

    return pl.pallas_call(
        body, name=name,
        out_shape=tuple(jax.ShapeDtypeStruct((N_DEV,) + a.shape, a.dtype) for a in arrs),
        in_specs=[_ANY] * n,
        out_specs=tuple([_ANY] * n),
        scratch_shapes=[pltpu.SemaphoreType.DMA((n, 7, n_pc)), pltpu.SemaphoreType.DMA((n, 7, n_pc)),
                        pltpu.SemaphoreType.DMA((n,))],
    )(*arrs)


def _flatten_blocks_call(blocks):
    n, R, C = blocks.shape
    tc = C // 2

    def body(in_ref, out_ref):
        for p in range(n):
            out_ref[p * R:(p + 1) * R, :] = in_ref[p]

    return pl.pallas_call(
        body, name="flatten_w",
        grid=(C // tc,),
        in_specs=[pl.BlockSpec((n, R, tc), lambda i: (0, 0, i))],
        out_specs=pl.BlockSpec((n * R, tc), lambda i: (0, i)),
        out_shape=jax.ShapeDtypeStruct((n * R, C), blocks.dtype),
        compiler_params=_cparams(("arbitrary",)),
    )(blocks)


_PARTS_BANDS = 4


def _pair_sum_call(dmain, drank):
    D = dmain.shape[1]
    n = _PARTS_BANDS
    tc = D // n

    def body(dm_ref, dr_ref, sum_ref, laid, got, send_sems, recv_sems):
        x, y, c = _mesh_pos()

        def pushes(k):
            return [_rcopy(laid.at[k % 2, 2 * q + (1 - c)], got.at[k, q], send_sems.at[k, q], recv_sems.at[k, q],
                           (x, y, 1 - c)) for q in range(4)]

        def lay_out(k):
            for p in range(N_DEV):
                lo, hi = p * SHARD_COLS, (p + 1) * SHARD_COLS
                at = 0
                for src, a, b in ((dm_ref, lo, min(hi, RANK_COL)),
                                  (dr_ref, max(lo, RANK_COL) - RANK_COL, min(hi, RANK_COL + GLA_RANK) - RANK_COL),
                                  (dm_ref, max(lo, RANK_COL + GLA_RANK) - GLA_RANK, hi - GLA_RANK)):
                    if b > a:
                        laid[k % 2, p, at:at + (b - a), :] = src[a:b, :]
                        at += b - a

        for k in range(n + 1):
            @pl.when(pl.program_id(0) == k)
            def _(k=k):
                if k < n:
                    if k >= 2:
                        for cp in pushes(k - 2):
                            cp.wait_send()
                    lay_out(k)
                    for cp in pushes(k):
                        cp.start()
                if k >= 1:
                    for cp in pushes(k - 1):
                        cp.wait_recv()
                    for q in range(4):
                        sum_ref[q] = (laid[(k - 1) % 2, 2 * q + c].astype(F32)
                                      + got[k - 1, q].astype(F32)).astype(sum_ref.dtype)
                if k == n:
                    for k_open in range(max(0, n - 2), n):
                        for cp in pushes(k_open):
                            cp.wait_send()

    sems = pltpu.SemaphoreType.DMA((n, 4))
    return pl.pallas_call(
        body, name="pair_sum",
        grid=(n + 1,),
        in_specs=[pl.BlockSpec((dmain.shape[0], tc), lambda k: (0, jnp.minimum(k, n - 1))),
                  pl.BlockSpec((GLA_RANK, tc), lambda k: (0, jnp.minimum(k, n - 1)))],
        out_specs=pl.BlockSpec((4, SHARD_COLS, tc), lambda k: (0, 0, jnp.maximum(k - 1, 0))),
        out_shape=jax.ShapeDtypeStruct((4, SHARD_COLS, D), dmain.dtype),
        scratch_shapes=[pltpu.VMEM((2, N_DEV, SHARD_COLS, tc), dmain.dtype),
                        pltpu.VMEM((n, 4, SHARD_COLS, tc), dmain.dtype), sems, sems],
        compiler_params=_cparams(("arbitrary",)),
    )(dmain, drank)


def _group_row(g):
    return GLA_RANK * (g * (1024 // GLA_RANK) + (g >= RANK_COL // 1024))


def _proj_call(x, norm_g, wt, wr, wp_part):
    T, D = x.shape
    tm = min(1024, T)
    assert tm % TBLK == 0
    n_i = T // tm

    def f_slot(j):
        return ((j >= 2).astype(jnp.int32) + (j >= 6).astype(jnp.int32)
                + (j >= 7).astype(jnp.int32) + (j >= 8).astype(jnp.int32))

    def b_slot(j):
        return (j >= 3).astype(jnp.int32) + (j >= 4).astype(jnp.int32) + (j >= 5).astype(jnp.int32)

    def body(x_ref, g_ref, w_ref, wr_ref, wp_ref, pf_ref, pb_ref, rank_ref, ht_ref, wpall_ref,
             h_scr, send_sems, recv_sems, loc_sem):
        i = pl.program_id(0)
        j = pl.program_id(1)
        own, pairs = _push_copies(wp_ref, wpall_ref, send_sems, recv_sems, loc_sem, scatter=False)

        @pl.when((i == 0) & (j == 0))
        def _():
            _push_start(own, pairs)

        @pl.when(j == 0)
        def _():
            xv = x_ref[...]
            r = lax.rsqrt(jnp.mean(xv * xv, axis=-1, keepdims=True) + EPS)
            h = (xv * r) * g_ref[...]
            hb = _bf(h)
            h_scr[...] = hb
            for b in range(tm // TBLK):
                ht_ref[b] = _bf(h[b * TBLK:(b + 1) * TBLK].T)
            rank_ref[...] = _dot_nt(hb, wr_ref[...])

        is_b = (j == 1) | ((j >= 3) & (j <= 5))

        @pl.when(is_b)
        def _():
            pb_ref[...] = _bf(_dot_nt(h_scr[...], w_ref[...]))

        @pl.when(jnp.logical_not(is_b))
        def _():
            pf_ref[...] = _dot_nt(h_scr[...], w_ref[...])

        @pl.when((i == n_i - 1) & (j == N_GROUPS - 1))
        def _():
            _push_wait(own, pairs)

    return pl.pallas_call(
        body, name="proj",
        grid=(n_i, N_GROUPS),
        in_specs=[pl.BlockSpec((tm, D), lambda i, j: (i, 0)),
                  pl.BlockSpec((1, D), lambda i, j: (0, 0)),
                  pl.BlockSpec((pl.Element(1024), pl.Element(D)), lambda i, j: (_group_row(j), 0)),
                  pl.BlockSpec((128, D), lambda i, j: (0, 0)),
                  _ANY],
        out_specs=(pl.BlockSpec((None, tm, 1024), lambda i, j: (f_slot(j), i, 0)),
                   pl.BlockSpec((None, tm, 1024), lambda i, j: (b_slot(j), i, 0)),
                   pl.BlockSpec((tm, 128), lambda i, j: (i, 0)),
                   pl.BlockSpec((tm // TBLK, D, TBLK), lambda i, j: (i, 0, 0)),
                   _ANY),
        out_shape=(jax.ShapeDtypeStruct((5, T, 1024), F32),
                   jax.ShapeDtypeStruct((4, T, 1024), BF16),
                   jax.ShapeDtypeStruct((T, 128), F32),
                   jax.ShapeDtypeStruct((T // TBLK, D, TBLK), BF16),
                   jax.ShapeDtypeStruct((N_DEV,) + wp_part.shape, wp_part.dtype)),
        scratch_shapes=[pltpu.VMEM((tm, D), BF16)] + _PUSH_SEMS,
        compiler_params=_cparams(("arbitrary", "arbitrary")),
    )(x, norm_g, wt, wr, wp_part)


GLA_STEP_CHUNKS = 4


def _gla_same_chunk(rows):
    return (_iota2(rows, rows, 0) & -GLA_CHUNK) == (_iota2(rows, rows, 1) & -GLA_CHUNK)


def _gla_chunk_terms(la, q, k, n_c):
    C = GLA_CHUNK
    rows = n_c * C
    low = _gla_same_chunk(rows) & (_iota2(rows, rows, 0) >= _iota2(rows, rows, 1))
    b = _tri_left(_bf(low.astype(F32)), la)
    bl = [b[(c + 1) * C - 1:(c + 1) * C, :] for c in range(n_c)]
    bl_rows = jnp.concatenate([jnp.broadcast_to(bl[c], (C, b.shape[1])) for c in range(n_c)], axis=0)
    eb = jnp.exp(b)
    enb = jnp.exp(-b)
    ebl_b = jnp.exp(bl_rows - b)
    scale = GLA_HK ** -0.5
    qe = q * eb * scale
    ke = k * enb
    kd = k * ebl_b
    return bl, eb, enb, ebl_b, qe, ke, kd


def _gla_fwd_call(projf, projb, rank, wdec, bdec):
    T = projf.shape[1]
    C = GLA_CHUNK
    n_chunks = T // C
    n_c = GLA_STEP_CHUNKS
    R = n_c * C
    assert n_chunks % n_c == 0

    def body(qk_ref, v_ref, rank_ref, wd_ref, bd_ref, o_ref, st_ref, la_ref, st_scr):
        @pl.when(pl.program_id(0) == 0)
        def _():
            st_scr[...] = jnp.zeros_like(st_scr)

        dec = _dot(_bf(rank_ref[...]), _bf(wd_ref[...])) + bd_ref[...]
        la = (jnp.minimum(dec, 0.0) - _softplus_neg_abs(dec)) / GLA_TAU
        la_ref[...] = la
        mask = _gla_same_chunk(R) & (_iota2(R, R, 0) >= _iota2(R, R, 1))
        bl, _, _, _, qe, ke, kd = _gla_chunk_terms(la, qk_ref[:, :GLA_DK], qk_ref[:, GLA_DK:], n_c)
        qeb, keb, kdb = _bf(qe), _bf(ke), _bf(kd)
        ebl = [jnp.exp(bl[c]) for c in range(n_c)]
        heads = range(GLA_HEADS)
        ks = [slice(hh * GLA_HK, (hh + 1) * GLA_HK) for hh in heads]
        vs = [slice(hh * GLA_HV, (hh + 1) * GLA_HV) for hh in heads]
        rs = [slice(c * C, (c + 1) * C) for c in range(n_c)]
        p = [_bf(jnp.where(mask, _dot_nt(qeb[:, ks[hh]], keb[:, ks[hh]]), 0.0)) for hh in heads]
        upd = [[_dot_tn(v_ref[rs[c], vs[hh]], kdb[rs[c], ks[hh]]) for hh in heads] for c in range(n_c)]
        intra = [_dot(p[hh], v_ref[:, vs[hh]]) for hh in heads]
        st = [st_scr[hh] for hh in heads]
        for c in range(n_c):
            inter = [_dot_nt(qeb[rs[c], ks[hh]], _bf(st[hh])) for hh in heads]
            for hh in heads:
                st_ref[c, hh] = st[hh]
                o_ref[rs[c], vs[hh]] = intra[hh][rs[c]] + inter[hh]
            st = [st[hh] * ebl[c][:, ks[hh]] + upd[c][hh] for hh in heads]
        for hh in heads:
            st_scr[hh] = st[hh]

    return pl.pallas_call(
        body, name="gla_fwd",
        grid=(n_chunks // n_c,),
        in_specs=[pl.BlockSpec((None, R, 1024), lambda n: (0, n, 0)),
                  pl.BlockSpec((None, R, 1024), lambda n: (0, n, 0)),
                  pl.BlockSpec((R, 128), lambda n: (n, 0)),
                  pl.BlockSpec((128, GLA_DK), lambda n: (0, 0)),
                  pl.BlockSpec((1, GLA_DK), lambda n: (0, 0))],
        out_specs=(pl.BlockSpec((R, 1024), lambda n: (n, 0)),
                   pl.BlockSpec((n_c, GLA_HEADS, GLA_HV, GLA_HK), lambda n: (n, 0, 0, 0)),
                   pl.BlockSpec((R, GLA_DK), lambda n: (n, 0))),
        out_shape=(jax.ShapeDtypeStruct((T, 1024), F32),
                   jax.ShapeDtypeStruct((n_chunks, GLA_HEADS, GLA_HV, GLA_HK), F32),
                   jax.ShapeDtypeStruct((T, GLA_DK), F32)),
        scratch_shapes=[pltpu.VMEM((GLA_HEADS, GLA_HV, GLA_HK), F32)],
        compiler_params=_cparams(("arbitrary",)),
    )(projf, projb, rank, wdec, bdec)


def _gla_bwd_call(projf, projb, la, do_gla, st_all, rank, wdec):
    T = projf.shape[1]
    C = GLA_CHUNK
    n_chunks = T // C
    n_c = GLA_STEP_CHUNKS
    R = n_c * C
    assert n_chunks % n_c == 0
    last = n_chunks // n_c - 1

    def body(qk_ref, v_ref, la_ref, do_ref, st_ref, rank_ref, wd_ref,
             dqk_ref, dv_ref, drank_ref, dwd_ref, dbd_ref, dst_scr):
        @pl.when(pl.program_id(0) == 0)
        def _():
            dst_scr[...] = jnp.zeros_like(dst_scr)
            dwd_ref[...] = jnp.zeros_like(dwd_ref)
            dbd_ref[...] = jnp.zeros_like(dbd_ref)

        same = _gla_same_chunk(R)
        mask = same & (_iota2(R, R, 0) >= _iota2(R, R, 1))
        upp = _bf((same & (_iota2(R, R, 0) <= _iota2(R, R, 1))).astype(F32))
        scale = GLA_HK ** -0.5
        la = la_ref[...]
        bl, eb, enb, ebl_b, qe, ke, kd = _gla_chunk_terms(la, qk_ref[:, :GLA_DK], qk_ref[:, GLA_DK:], n_c)
        qeb, keb, kdb = _bf(qe), _bf(ke), _bf(kd)
        ebl = [jnp.exp(bl[c]) for c in range(n_c)]
        heads = range(GLA_HEADS)
        ks = [slice(hh * GLA_HK, (hh + 1) * GLA_HK) for hh in heads]
        vs = [slice(hh * GLA_HV, (hh + 1) * GLA_HV) for hh in heads]
        rs = [slice(c * C, (c + 1) * C) for c in range(n_c)]
        v = [v_ref[:, vs[hh]] for hh in heads]
        do = [_bf(do_ref[:, vs[hh]]) for hh in heads]
        p = [_bf(jnp.where(mask, _dot_nt(qeb[:, ks[hh]], keb[:, ks[hh]]), 0.0)) for hh in heads]
        dp = [_bf(jnp.where(mask, _dot_nt(do[hh], v[hh]), 0.0)) for hh in heads]
        dst_intra = [[_dot_tn(do[hh][rs[c]], qeb[rs[c], ks[hh]]) for hh in heads] for c in range(n_c)]
        dqe_inter = [[_dot(do[hh][rs[c]], _bf(st_ref[c, hh])) for hh in heads] for c in range(n_c)]
        dv_intra = [_dot_tn(p[hh], do[hh]) for hh in heads]
        dqe_intra = [_dot(dp[hh], keb[:, ks[hh]]) for hh in heads]
        dke = jnp.concatenate([_dot_tn(dp[hh], qeb[:, ks[hh]]) for hh in heads], axis=1)
        dstn = [dst_scr[hh] for hh in heads]
        dkd_c, dv_inter, debl = [None] * n_c, [None] * n_c, [None] * n_c
        for c in reversed(range(n_c)):
            dstnb = [_bf(dstn[hh]) for hh in heads]
            dkd_c[c] = jnp.concatenate([_dot(v[hh][rs[c]], dstnb[hh]) for hh in heads], axis=1)
            dv_inter[c] = [_dot_nt(kdb[rs[c], ks[hh]], dstnb[hh]) for hh in heads]
            debl[c] = jnp.concatenate(
                [jnp.sum(dstn[hh] * st_ref[c, hh], axis=0, keepdims=True) for hh in heads], axis=1)
            dstn = [dst_intra[c][hh] + dstn[hh] * ebl[c][:, ks[hh]] for hh in heads]
        for hh in heads:
            dst_scr[hh] = dstn[hh]
            dv_ref[:, vs[hh]] = _bf(dv_intra[hh] + jnp.concatenate([dv_inter[c][hh] for c in range(n_c)], axis=0))
        dqe = jnp.concatenate(
            [dqe_intra[hh] + jnp.concatenate([dqe_inter[c][hh] for c in range(n_c)], axis=0) for hh in heads], axis=1)
        dkd = jnp.concatenate(dkd_c, axis=0)
        dkd_kd = dkd * kd
        db = dqe * qe - dke * ke - dkd_kd
        dbl = jnp.concatenate(
            [jnp.broadcast_to(jnp.sum(dkd_kd[rs[c]], axis=0, keepdims=True) + ebl[c] * debl[c], (C, GLA_DK))
             for c in range(n_c)], axis=0)
        dla = _tri_left(upp, db) + dbl
        dqk_ref[:, :GLA_DK] = _bf(dqe * eb * scale)
        dqk_ref[:, GLA_DK:] = _bf(dke * enb + dkd * ebl_b)
        ddec = dla * (1.0 / GLA_TAU) * (1.0 - jnp.exp(GLA_TAU * la))
        ddecb = _bf(ddec)
        drank_ref[...] = _bf(_dot_nt(ddecb, _bf(wd_ref[...])))
        dwd_ref[...] += _dot_tn(_bf(rank_ref[...]), ddecb)
        dbd_ref[...] += jnp.sum(ddec, axis=0, keepdims=True)

    return pl.pallas_call(
        body, name="gla_bwd",
        grid=(n_chunks // n_c,),
        in_specs=[pl.BlockSpec((None, R, 1024), lambda n: (0, last - n, 0)),
                  pl.BlockSpec((None, R, 1024), lambda n: (0, last - n, 0)),
                  pl.BlockSpec((R, GLA_DK), lambda n: (last - n, 0)),
                  pl.BlockSpec((R, 1024), lambda n: (last - n, 0)),
                  pl.BlockSpec((n_c, GLA_HEADS, GLA_HV, GLA_HK), lambda n: (last - n, 0, 0, 0)),
                  pl.BlockSpec((R, 128), lambda n: (last - n, 0)),
                  pl.BlockSpec((128, GLA_DK), lambda n: (0, 0))],
        out_specs=(pl.BlockSpec((R, 1024), lambda n: (last - n, 0)),
                   pl.BlockSpec((R, 1024), lambda n: (last - n, 0)),
                   pl.BlockSpec((R, 128), lambda n: (last - n, 0)),
                   pl.BlockSpec((128, GLA_DK), lambda n: (0, 0)),
                   pl.BlockSpec((1, GLA_DK), lambda n: (0, 0))),
        out_shape=(jax.ShapeDtypeStruct((T, 1024), BF16),
                   jax.ShapeDtypeStruct((T, 1024), BF16),
                   jax.ShapeDtypeStruct((T, 128), BF16),
                   jax.ShapeDtypeStruct((128, GLA_DK), F32),
                   jax.ShapeDtypeStruct((1, GLA_DK), F32)),
        scratch_shapes=[pltpu.VMEM((GLA_HEADS, GLA_HV, GLA_HK), F32)],
        compiler_params=_cparams(("arbitrary",)),
    )(projf, projb, la, do_gla, st_all, rank, wdec)


def _sb_logs(z):
    lsz = jnp.minimum(z, 0.0) - _softplus_neg_abs(z)
    return lsz, lsz - z


SB_HG_FWD = 8
SB_HG_BWD = 4
SB_QUERIES = 256
SB_KEYS = 256
SB_DEAD = -105.0


def _sb_fwd_call(projb, wp_shard):
    T = projb.shape[1]
    B = min(SB_QUERIES, T)
    HG = SB_HG_FWD
    W = HG * SB_HD
    scale = 1.0 / math.sqrt(SB_HD)
    KB = min(SB_KEYS, T)
    n_h, n_i = SB_HEADS // HG, T // B

    def body(q_ref, k_ref, v_ref, wp_ref, o_ref, wpall_ref, cb_scr, send_sems, recv_sems, loc_sem):
        i = pl.program_id(1)
        own, pairs = _push_copies(wp_ref, wpall_ref, send_sems, recv_sems, loc_sem, scatter=False)

        @pl.when((pl.program_id(0) == 0) & (i == 0))
        def _():
            _push_start(own, pairs)

        rows = HG * B
        after = (_iota2(KB, KB, 0) > _iota2(KB, KB, 1)).astype(F32)
        tri = _bf(jnp.concatenate([after, jnp.ones((KB, KB), F32)], axis=1))
        o_ref[...] = jnp.zeros_like(o_ref)
        cb_scr[...] = jnp.zeros_like(cb_scr)

        def block(jp, masked):
            off = pl.multiple_of(jp * KB, KB)
            z = jnp.concatenate(
                [_dot_nt(q_ref[:, hh * SB_HD:(hh + 1) * SB_HD], k_ref[pl.ds(off, KB), hh * SB_HD:(hh + 1) * SB_HD])
                 for hh in range(HG)], axis=0) * scale
            lsz, l1m = _sb_logs(z)
            if masked:
                strict = (jp * KB + _iota2(rows, KB, 1)) < (i * B + (_iota2(rows, KB, 0) & (B - 1)))
                l1m = jnp.where(strict, l1m, 0.0)
            r = _tri2_right(l1m, tri)
            cb = cb_scr[...]
            a = jnp.exp(lsz + cb + r[:, :KB])
            if masked:
                a = jnp.where(strict, a, 0.0)
            cb_scr[...] = cb + r[:, KB:]
            ab = _bf(a)
            for hh in range(HG):
                cs = slice(hh * SB_HD, (hh + 1) * SB_HD)
                o_ref[:, cs] += _dot(ab[hh * B:(hh + 1) * B, :], v_ref[pl.ds(off, KB), cs])

        jp0 = (i * B) // KB
        block(jp0, True)

        def live(state):
            jj, dead = state
            return (jj <= jp0) & jnp.logical_not(dead)

        def step(state):
            jj, _ = state
            block(jp0 - jj, False)
            return jj + 1, jnp.max(cb_scr[:, :SB_HD]) < SB_DEAD

        lax.while_loop(live, step, (jnp.int32(1), jnp.max(cb_scr[:, :SB_HD]) < SB_DEAD))

        @pl.when((pl.program_id(0) == n_h - 1) & (i == n_i - 1))
        def _():
            _push_wait(own, pairs)

    return pl.pallas_call(
        body, name="sb_fwd",
        grid=(n_h, n_i),
        in_specs=[pl.BlockSpec((None, B, W), lambda h, i: (1, i, h)),
                  pl.BlockSpec((None, T, W), lambda h, i: (2, 0, h)),
                  pl.BlockSpec((None, T, W), lambda h, i: (3, 0, h)),
                  _ANY],
        out_specs=(pl.BlockSpec((B, W), lambda h, i: (i, h)), _ANY),
        out_shape=(jax.ShapeDtypeStruct((T, 1024), F32),
                   jax.ShapeDtypeStruct((N_DEV,) + wp_shard.shape, wp_shard.dtype)),
        scratch_shapes=[pltpu.VMEM((HG * B, KB), F32)] + _PUSH_SEMS,
        compiler_params=_cparams(("arbitrary", "arbitrary")),
    )(projb, projb, projb, wp_shard)


def _sb_bwd_call(projb, do_sb, g_p):
    T = projb.shape[1]
    B = min(SB_QUERIES, T)
    nb = T // B
    HG = SB_HG_BWD
    W = HG * SB_HD
    WQ = HG * B
    KB = min(SB_KEYS, T)
    nkb = T // KB
    n_h = SB_HEADS // HG
    scale = 1.0 / math.sqrt(SB_HD)

    def body(q_ref, k_ref, v_ref, do_ref, gp_ref, dq_ref, dk_ref, dv_ref, rp_ref,
             dk_scr, dv_scr, kt_scr, beta_scr, g_scr, dqt_scr, send_sems, recv_sems, loc_sem):
        i = pl.program_id(1)
        own, pairs = _push_copies(gp_ref, rp_ref, send_sems, recv_sems, loc_sem, scatter=True)

        @pl.when((pl.program_id(0) == 0) & (i == 0))
        def _():
            _push_start(own, pairs)

        @pl.when(i == 0)
        def _():
            dk_scr[...] = jnp.zeros_like(dk_scr)
            dv_scr[...] = jnp.zeros_like(dv_scr)
            for hh in range(HG):
                for jb in range(nkb):
                    kt_scr[hh, jb] = _bf(
                        k_ref[jb * KB:(jb + 1) * KB, hh * SB_HD:(hh + 1) * SB_HD].astype(F32).T)

        dqt_scr[...] = jnp.zeros_like(dqt_scr)
        later = _bf((_iota2(KB, KB, 1) > _iota2(KB, KB, 0)).astype(F32))
        earlier = _bf((_iota2(KB, KB, 1) < _iota2(KB, KB, 0)).astype(F32))
        dob = _bf(do_ref[...])
        jp0 = (i * B) // KB

        def strict_mask():
            return (jp0 * KB + _iota2(KB, WQ, 0)) < (i * B + (_iota2(KB, WQ, 1) & (B - 1)))

        def heads(fn):
            return [fn(slice(hh * SB_HD, (hh + 1) * SB_HD)) for hh in range(HG)]

        def pass1(jp, cb, masked):
            off = pl.multiple_of(jp * KB, KB)
            z = jnp.concatenate(heads(lambda cs: _dot_nt(k_ref[pl.ds(off, KB), cs], q_ref[:, cs])), axis=1) * scale
            da = jnp.concatenate(heads(lambda cs: _dot_nt(v_ref[pl.ds(off, KB), cs], dob[:, cs])), axis=1)
            lsz, l1m = _sb_logs(z)
            if masked:
                strict = strict_mask()
                l1m = jnp.where(strict, l1m, 0.0)
            a = jnp.exp(lsz + cb + _tri2_left(later, l1m))
            if masked:
                a = jnp.where(strict, a, 0.0)
            g_scr[jp] = a * da
            beta_scr[jp] = jnp.exp(lsz)
            ab = _bf(a)
            for hh in range(HG):
                cs = slice(hh * SB_HD, (hh + 1) * SB_HD)
                dv_scr[pl.ds(off, KB), cs] += _dot(ab[:, hh * B:(hh + 1) * B], dob[:, cs])
            return cb + jnp.sum(l1m, axis=0, keepdims=True)

        zero = jnp.zeros((1, WQ), F32)
        cb = pass1(jp0, zero, True)

        def live(state):
            jj, _, dead = state
            return (jj <= jp0) & jnp.logical_not(dead)

        def step(state):
            jj, cr, _ = state
            cr = pass1(jp0 - jj, cr, False)
            return jj + 1, cr, jnp.max(cr) < SB_DEAD

        n_done, _, _ = lax.while_loop(live, step, (jnp.int32(1), cb, jnp.max(cb) < SB_DEAD))
        jp_first = jp0 - (n_done - 1)

        def pass2(jp, cg, masked):
            off = pl.multiple_of(jp * KB, KB)
            g = g_scr[jp]
            beta = beta_scr[jp]
            dz = g * (1.0 - beta) - beta * (cg + _tri2_left(earlier, g))
            if masked:
                dz = jnp.where(strict_mask(), dz, 0.0)
            dzb = _bf(dz * scale)
            for hh in range(HG):
                cs = slice(hh * SB_HD, (hh + 1) * SB_HD)
                dk_scr[pl.ds(off, KB), cs] += _dot(dzb[:, hh * B:(hh + 1) * B], q_ref[:, cs])
                dqt_scr[hh] += _dot(kt_scr[hh, jp], dzb[:, hh * B:(hh + 1) * B])
            return cg + jnp.sum(g, axis=0, keepdims=True)

        cg = lax.fori_loop(jp_first, jp0, lambda jp, cr: pass2(jp, cr, False), zero)
        pass2(jp0, cg, True)
        for hh in range(HG):
            dq_ref[:, hh * SB_HD:(hh + 1) * SB_HD] = _bf(dqt_scr[hh].T)

        @pl.when(i == nb - 1)
        def _():
            dk_ref[...] = _bf(dk_scr[...])
            dv_ref[...] = _bf(dv_scr[...])

        @pl.when((pl.program_id(0) == n_h - 1) & (i == nb - 1))
        def _():
            _push_wait(own, pairs)

    return pl.pallas_call(
        body, name="sb_bwd",
        grid=(n_h, nb),
        in_specs=[pl.BlockSpec((None, B, W), lambda h, i: (1, i, h)),
                  pl.BlockSpec((None, T, W), lambda h, i: (2, 0, h)),
                  pl.BlockSpec((None, T, W), lambda h, i: (3, 0, h)),
                  pl.BlockSpec((B, W), lambda h, i: (i, h)),
                  _ANY],
        out_specs=(pl.BlockSpec((B, W), lambda h, i: (i, h)),
                   pl.BlockSpec((T, W), lambda h, i: (0, h)),
                   pl.BlockSpec((T, W), lambda h, i: (0, h)),
                   _ANY),
        out_shape=(jax.ShapeDtypeStruct((T, 1024), BF16),
                   jax.ShapeDtypeStruct((T, 1024), BF16),
                   jax.ShapeDtypeStruct((T, 1024), BF16),
                   jax.ShapeDtypeStruct(g_p.shape, g_p.dtype)),
        scratch_shapes=[pltpu.VMEM((T, W), F32), pltpu.VMEM((T, W), F32),
                        pltpu.VMEM((HG, nkb, SB_HD, KB), BF16),
                        pltpu.VMEM((nkb, KB, WQ), F32), pltpu.VMEM((nkb, KB, WQ), F32),
                        pltpu.VMEM((HG, SB_HD, B), F32)] + _PUSH_SEMS,
        compiler_params=_cparams(("arbitrary", "arbitrary")),
    )(projb, projb, projb, do_sb, g_p)


def _mid_call(o_gla, o_sb, projf, x, target, wpa, wpb, wo, gla_g, b_gate, final_g):
    T, D = x.shape
    tm = min(TBLK, T)

    def body(og_ref, ggate_ref, osb_ref, sgate_ref, ma_ref, mb_ref, x_ref, tgt_ref,
             wpa_ref, wpb_ref, wo_ref, glag_ref, bg_ref, fg_ref,
             dx2_ref, dogla_ref, dosb_ref, dggate_ref, dsgate_ref, dm_ref,
             mt_ref, ogt_ref, obt_ref, dx2b_ref, dya_ref, dyb_ref,
             dfg_ref, dbg_ref, dglag_ref, loss_ref):
        @pl.when(pl.program_id(0) == 0)
        def _():
            dfg_ref[...] = jnp.zeros_like(dfg_ref)
            dbg_ref[...] = jnp.zeros_like(dbg_ref)
            dglag_ref[...] = jnp.zeros_like(dglag_ref)
            loss_ref[...] = jnp.zeros_like(loss_ref)

        glag = glag_ref[...]
        ggate = ggate_ref[...]
        sg = _sigmoid(ggate)
        silu_g = ggate * sg
        ohat, rinv, nrm = [], [], []
        for hh in range(GLA_HEADS):
            oh = og_ref[:, hh * GLA_HV:(hh + 1) * GLA_HV]
            r = lax.rsqrt(jnp.mean(oh * oh, axis=-1, keepdims=True) + EPS)
            ohat.append(oh * r)
            rinv.append(r)
            nrm.append(ohat[-1] * glag)
        n_all = jnp.concatenate(nrm, axis=1)
        og = n_all * silu_g
        ogb = _bf(og)
        ya = _dot(ogb, wpa_ref[...])
        sgate = sgate_ref[...]
        ss = _sigmoid(sgate)
        silu_s = sgate * ss
        osb = osb_ref[...]
        ob = osb * silu_s
        obb = _bf(ob)
        yb = _dot(obb, wpb_ref[...])
        ga = _sigmoid(ma_ref[...] + bg_ref[:, :D])
        gb = _sigmoid(mb_ref[...] + bg_ref[:, D:])
        merged = ga * ya + gb * yb
        mgb = _bf(merged)
        x2 = x_ref[...] + _dot(mgb, wo_ref[...])
        r2 = lax.rsqrt(jnp.mean(x2 * x2, axis=-1, keepdims=True) + EPS)
        xh2 = x2 * r2
        fg = fg_ref[...]
        err = xh2 * fg - tgt_ref[...]
        loss_ref[...] += jnp.broadcast_to(
            0.5 * jnp.sum(jnp.mean(err * err, axis=-1, keepdims=True), axis=0, keepdims=True), (1, 128))
        dy = err * (1.0 / D)
        dfg_ref[...] += jnp.sum(dy * xh2, axis=0, keepdims=True)
        dxh = dy * fg
        dx2 = r2 * (dxh - xh2 * jnp.mean(dxh * xh2, axis=-1, keepdims=True))
        dx2_ref[...] = dx2
        dx2b = _bf(dx2)
        dx2b_ref[...] = dx2b
        dmerged = _dot_nt(dx2b, wo_ref[...])
        dya = dmerged * ga
        dyb = dmerged * gb
        dma = dmerged * ya * ga * (1.0 - ga)
        dmb = dmerged * yb * gb * (1.0 - gb)
        dm_ref[:, :D] = _bf(dma)
        dm_ref[:, D:] = _bf(dmb)
        dbg_ref[:, :D] += jnp.sum(dma, axis=0, keepdims=True)
        dbg_ref[:, D:] += jnp.sum(dmb, axis=0, keepdims=True)
        dyab = _bf(dya)
        dybb = _bf(dyb)
        dya_ref[...] = dyab
        dyb_ref[...] = dybb
        dog = _dot_nt(dyab, wpa_ref[...])
        dob = _dot_nt(dybb, wpb_ref[...])
        dosb_ref[...] = dob * silu_s
        dsgate_ref[...] = _bf(dob * osb * (ss * (1.0 + sgate * (1.0 - ss))))
        dn = dog * silu_g
        dggate_ref[...] = _bf(dog * n_all * (sg * (1.0 + ggate * (1.0 - sg))))
        dglag = jnp.zeros((1, GLA_HV), F32)
        for hh in range(GLA_HEADS):
            dnh = dn[:, hh * GLA_HV:(hh + 1) * GLA_HV]
            dglag = dglag + jnp.sum(dnh * ohat[hh], axis=0, keepdims=True)
            dohat = dnh * glag
            dogla_ref[:, hh * GLA_HV:(hh + 1) * GLA_HV] = rinv[hh] * (
                dohat - ohat[hh] * jnp.mean(dohat * ohat[hh], axis=-1, keepdims=True))
        dglag_ref[...] += dglag
        mt_ref[...] = _bf(merged.T)
        ogt_ref[...] = _bf(og.T)
        obt_ref[...] = _bf(ob.T)

    row = lambda i: (i, 0)
    const = lambda i: (0, 0)
    tile = pl.BlockSpec((tm, D), row)
    tile_t = pl.BlockSpec((None, D, tm), lambda i: (i, 0, 0))
    wspec = pl.BlockSpec((D, D), const)
    return pl.pallas_call(
        body, name="mid",
        grid=(T // tm,),
        in_specs=[tile,
                  pl.BlockSpec((None, tm, D), lambda i: (1, i, 0)),
                  tile,
                  pl.BlockSpec((None, tm, D), lambda i: (2, i, 0)),
                  pl.BlockSpec((None, tm, D), lambda i: (3, i, 0)),
                  pl.BlockSpec((None, tm, D), lambda i: (4, i, 0)),
                  tile, tile, wspec, wspec, wspec,
                  pl.BlockSpec((1, GLA_HV), const),
                  pl.BlockSpec((1, 2 * D), const),
                  pl.BlockSpec((1, D), const)],
        out_specs=(tile, tile, tile, tile, tile,
                   pl.BlockSpec((tm, 2 * D), row),
                   tile_t, tile_t, tile_t, tile, tile, tile,
                   pl.BlockSpec((1, D), const),
                   pl.BlockSpec((1, 2 * D), const),
                   pl.BlockSpec((1, GLA_HV), const),
                   pl.BlockSpec((1, 128), const)),
        out_shape=(jax.ShapeDtypeStruct((T, D), F32),
                   jax.ShapeDtypeStruct((T, D), F32),
                   jax.ShapeDtypeStruct((T, D), F32),
                   jax.ShapeDtypeStruct((T, D), BF16),
                   jax.ShapeDtypeStruct((T, D), BF16),
                   jax.ShapeDtypeStruct((T, 2 * D), BF16),
                   jax.ShapeDtypeStruct((T // tm, D, tm), BF16),
                   jax.ShapeDtypeStruct((T // tm, D, tm), BF16),
                   jax.ShapeDtypeStruct((T // tm, D, tm), BF16),
                   jax.ShapeDtypeStruct((T, D), BF16),
                   jax.ShapeDtypeStruct((T, D), BF16),
                   jax.ShapeDtypeStruct((T, D), BF16),
                   jax.ShapeDtypeStruct((1, D), F32),
                   jax.ShapeDtypeStruct((1, 2 * D), F32),
                   jax.ShapeDtypeStruct((1, GLA_HV), F32),
                   jax.ShapeDtypeStruct((1, 128), F32)),
        compiler_params=_cparams(("arbitrary",)),
    )(o_gla, projf, o_sb, projf, projf, projf, x, target, wpa, wpb, wo, gla_g, b_gate, final_g)


def _dh_call(pieces, dmlog, drank, wt, wr, x, dx2, norm_g, s_in, small):
    T, D = x.shape
    tm = min(256, T)
    npc = len(pieces)
    n_main = N_GROUPS * 1024
    n_i = T // tm
    i_follow = n_i // 8
    i_forward = 5 * n_i // 8

    def body(*refs):
        pcs = refs[:npc]
        (dm_ref, dr_ref, w_hbm, wr_ref, x_ref, dx2_ref, g_ref, sin_ref, small_ref,
         gx_ref, rin_ref, relayed_ref, rsmall_ref,
         w_scr, sems, dg_ref, small_mine, small_send, small_recv, small_loc, *exchange_scratch) = refs[npc:]
        start, follow, forward, finish = _chip_reduce_steps(sin_ref, rin_ref, relayed_ref, *exchange_scratch)

        @pl.when(pl.program_id(0) == 0)
        def _():
            start()
            lo = pltpu.make_async_copy(w_hbm.at[pl.ds(0, RANK_COL)], w_scr.at[pl.ds(0, RANK_COL)], sems.at[0])
            hi = pltpu.make_async_copy(w_hbm.at[pl.ds(RANK_COL + GLA_RANK, n_main - RANK_COL)],
                                       w_scr.at[pl.ds(RANK_COL, n_main - RANK_COL)], sems.at[1])
            lo.start()
            hi.start()
            dg_ref[...] = jnp.zeros_like(dg_ref)
            lo.wait()
            hi.wait()

        @pl.when(pl.program_id(0) == i_follow)
        def _():
            follow()

        @pl.when(pl.program_id(0) == i_forward)
        def _():
            forward()

        def w_group(g):
            return w_scr[g * 1024:(g + 1) * 1024, :]

        dr = dr_ref[...]
        dh = _dot(dr, wr_ref[...])
        for g in range(npc):
            dh = dh + _dot(pcs[g][...], w_group(g))
        dh = dh + _dot(dm_ref[:, :D], w_group(npc))
        dh = dh + _dot(dm_ref[:, D:], w_group(npc + 1))
        xv = x_ref[...]
        r = lax.rsqrt(jnp.mean(xv * xv, axis=-1, keepdims=True) + EPS)
        xhat = xv * r
        g = g_ref[...]
        dg_ref[...] += jnp.sum(dh * xhat, axis=0, keepdims=True)
        dxhat = dh * g
        gx_ref[...] = r * (dxhat - xhat * jnp.mean(dxhat * xhat, axis=-1, keepdims=True)) + dx2_ref[...]

        @pl.when(pl.program_id(0) == n_i - 1)
        def _():
            small_mine[...] = small_ref[...]
            small_mine[:, _SM_NORM:_SM_NORM + D] = dg_ref[...]
            own, pairs = _push_copies(small_mine, rsmall_ref, small_send, small_recv, small_loc, scatter=False)
            _push_start(own, pairs)
            finish()
            _push_wait(own, pairs)

    row = lambda i: (i, 0)
    const = lambda i: (0, 0)
    tile = pl.BlockSpec((tm, D), row)
    part = s_in.shape[1:]
    return pl.pallas_call(
        body, name="dh",
        grid=(n_i,),
        in_specs=[tile] * npc + [
            pl.BlockSpec((tm, 2 * D), row),
            pl.BlockSpec((tm, 128), row),
            _ANY,
            pl.BlockSpec((128, D), const),
            tile, tile,
            pl.BlockSpec((1, D), const),
            _ANY,
            pl.BlockSpec(small.shape, const)],
        out_specs=(tile, _ANY, _ANY, _ANY),
        out_shape=(jax.ShapeDtypeStruct((T, D), F32),
                   jax.ShapeDtypeStruct((3,) + part, s_in.dtype),
                   jax.ShapeDtypeStruct(part, s_in.dtype),
                   jax.ShapeDtypeStruct((N_DEV,) + small.shape, small.dtype)),
        scratch_shapes=[pltpu.VMEM((n_main, D), BF16), pltpu.SemaphoreType.DMA((2,)),
                        pltpu.VMEM((1, D), F32), pltpu.VMEM(small.shape, small.dtype)]
        + _PUSH_SEMS + _chip_reduce_scratch(*part, s_in.dtype),
        compiler_params=_cparams(("arbitrary",)),
    )(*pieces, dmlog, drank, wt, wr, x, dx2, norm_g, s_in, small)


def _wgrad_call(lhs_list, lhs_of_group, rhs_list, rhs_of_group, n_transposed, name, narrow=None):
    n_groups = len(rhs_of_group)
    n_tb, D, tb = lhs_list[0].shape
    T = n_tb * tb
    per = min(4, n_tb)
    tk = per * tb
    nk = T // tk
    nl = len(lhs_list)
    extra = [] if narrow is None else [narrow]

    def tokens_side_by_side(lref):
        return jnp.concatenate([lref[b] for b in range(per)], axis=1)

    def body(*refs):
        lhs = refs[:nl]
        rhs = refs[nl:nl + n_groups]
        rest = refs[nl + n_groups:]
        g = pl.program_id(0)
        i = pl.program_id(1)
        if narrow is None:
            out_ref, acc = rest
        else:
            narrow_ref, out_ref, narrow_out, acc, narrow_acc = rest

            @pl.when((g == 0) & (i == 0))
            def _():
                narrow_acc[...] = jnp.zeros_like(narrow_acc)

            @pl.when(g == 0)
            def _():
                narrow_acc[...] += _dot(tokens_side_by_side(lhs[lhs_of_group[0]]), narrow_ref[...])

            @pl.when((g == 0) & (i == nk - 1))
            def _():
                narrow_out[...] = _bf(narrow_acc[...].T)

        @pl.when(i == 0)
        def _():
            acc[...] = jnp.zeros_like(acc)

        for p in range(n_groups):
            @pl.when(g == p)
            def _(p=p):
                acc[...] += _dot(tokens_side_by_side(lhs[lhs_of_group[p]]), rhs[p][...])

        @pl.when((i == nk - 1) & (g < n_transposed))
        def _():
            out_ref[...] = _bf(acc[...].T)

        @pl.when((i == nk - 1) & (g >= n_transposed))
        def _():
            out_ref[...] = _bf(acc[...])

    def lhs_spec(a):
        groups = [g for g in range(n_groups) if lhs_of_group[g] == a]
        lo, hi = min(groups), max(groups)
        assert groups == list(range(lo, hi + 1))
        return pl.BlockSpec((per, D, tb), lambda g, i: (jnp.where((g >= lo) & (g <= hi), i, 0), 0, 0))

    def rhs_spec(p):
        cb = rhs_of_group[p][1]
        return pl.BlockSpec((tk, 1024), lambda g, i: (jnp.where(g == p, i, 0), cb))

    res = pl.pallas_call(
        body, name=name,
        grid=(n_groups, nk),
        in_specs=[lhs_spec(a) for a in range(nl)] + [rhs_spec(p) for p in range(n_groups)]
        + [pl.BlockSpec((tk, 128), lambda g, i: (jnp.where(g == 0, i, 0), 0)) for _ in extra],
        out_specs=[pl.BlockSpec((None, D, 1024), lambda g, i: (g, 0, 0))]
        + [pl.BlockSpec((128, D), lambda g, i: (0, 0)) for _ in extra],
        out_shape=[jax.ShapeDtypeStruct((n_groups, D, 1024), BF16)]
        + [jax.ShapeDtypeStruct((128, D), BF16) for _ in extra],
        scratch_shapes=[pltpu.VMEM((D, 1024), F32)] + [pltpu.VMEM((D, 128), F32) for _ in extra],
        compiler_params=_cparams(("arbitrary", "arbitrary")),
    )(*lhs_list, *[rhs_list[rhs_of_group[p][0]] for p in range(n_groups)], *extra)
    return res[0] if narrow is None else res


def _adamw_math(parts, w, m, v):
    g = parts[0].astype(F32)
    for p in parts[1:]:
        g = g + p.astype(F32)
    mm = ADAM_B1 * m + (1.0 - ADAM_B1) * g
    vv = ADAM_B2 * v + (1.0 - ADAM_B2) * (g * g)
    m_hat = mm / (1.0 - ADAM_B1 ** ADAM_STEP)
    v_hat = vv / (1.0 - ADAM_B2 ** ADAM_STEP)
    return g, -ADAM_LR * (m_hat / (jnp.sqrt(v_hat) + ADAM_EPS) + ADAM_WD * w), mm, vv


def _part_order(n_parts):
    return [n_parts - 1] + list(range(n_parts - 1))


def _adamw_call(parts, w, m, v, name):
    R, C = w.shape
    n_parts = parts.shape[0]
    (tr, tc), grid, idx = _tiling_2d(R, C, 512)

    def body(p_ref, w_ref, m_ref, v_ref, g_ref, d_ref, nm_ref, nv_ref):
        g_ref[...], d_ref[...], nm_ref[...], nv_ref[...] = _adamw_math(
            [p_ref[k] for k in _part_order(n_parts)], w_ref[...], m_ref[...], v_ref[...])

    blk = pl.BlockSpec((tr, tc), idx)
    sds = jax.ShapeDtypeStruct((R, C), F32)
    return pl.pallas_call(
        body, name=name,
        grid=grid,
        in_specs=[pl.BlockSpec((n_parts, tr, tc), lambda i: (0,) + idx(i)), blk, blk, blk],
        out_specs=(blk, blk, blk, blk),
        out_shape=(sds, sds, sds, sds),
        compiler_params=_cparams(("arbitrary",)),
    )(parts, w, m, v)


def _adamw_rows_call(parts, ws, ms, vs, name):
    n = len(ws)
    R, C = ws[0].shape
    n_parts = parts.shape[0]

    def body(*refs):
        p_ref = refs[0]
        w_refs, m_refs, v_refs = refs[1:1 + n], refs[1 + n:1 + 2 * n], refs[1 + 2 * n:1 + 3 * n]
        outs = refs[1 + 3 * n:]
        for k in range(n):
            @pl.when(pl.program_id(0) == k)
            def _(k=k):
                res = _adamw_math([p_ref[j] for j in _part_order(n_parts)],
                                  w_refs[k][...], m_refs[k][...], v_refs[k][...])
                for o_ref, val in zip(outs[4 * k:4 * k + 4], res):
                    o_ref[...] = val

    whole = pl.BlockSpec((R, C), lambda k: (0, 0))
    sds = jax.ShapeDtypeStruct((R, C), F32)
    res = pl.pallas_call(
        body, name=name,
        grid=(n,),
        in_specs=[pl.BlockSpec((n_parts, R, C), lambda k: (0, k, 0))] + [whole] * (3 * n),
        out_specs=tuple([whole] * (4 * n)),
        out_shape=tuple([sds] * (4 * n)),
        compiler_params=_cparams(("arbitrary",)),
    )(parts, *ws, *ms, *vs)
    return [res[4 * k:4 * k + 4] for k in range(n)]


def _adamw_lanes_call(parts, offsets, ws, ms, vs, name):
    n = len(ws)
    n_parts = parts.shape[0]

    def body(*refs):
        p_ref = refs[0]
        w_refs, m_refs, v_refs = refs[1:1 + n], refs[1 + n:1 + 2 * n], refs[1 + 2 * n:1 + 3 * n]
        outs = refs[1 + 3 * n:]
        for k in range(n):
            lanes = slice(offsets[k], offsets[k] + ws[k].shape[1])
            res = _adamw_math([p_ref[j, :, lanes] for j in _part_order(n_parts)],
                              w_refs[k][...], m_refs[k][...], v_refs[k][...])
            for o_ref, val in zip(outs[4 * k:4 * k + 4], res):
                o_ref[...] = val

    res = pl.pallas_call(
        body, name=name,
        out_shape=tuple(jax.ShapeDtypeStruct(ws[k].shape, F32) for k in range(n) for _ in range(4)),
        compiler_params=_cparams(),
    )(parts, *ws, *ms, *vs)
    return [res[4 * k:4 * k + 4] for k in range(n)]


def _local_step(x, target, wt, wr, wdec, bdec, wp_shard, norm_g, gla_g, b_gate, final_g):
    D = x.shape[1]
    half = wp_shard.shape[1] // 2
    projf, projb, rank, ht, wp_lo = _proj_call(x, norm_g, wt, wr, wp_shard[:, :half])
    o_gla, st_all, la = _gla_fwd_call(projf, projb, rank, wdec, bdec)
    o_sb, wp_hi = _sb_fwd_call(projb, wp_shard[:, half:])
    wp_full = jnp.concatenate([wp_lo, wp_hi], axis=2).transpose(1, 0, 2, 3).reshape(3, D, D)
    (dx2, do_gla, do_sb, dggate, dsgate, dmlog, mt, ogt, obt, dx2b, dya, dyb,
     dfinal_g, db_gate, dgla_g, loss) = _mid_call(o_gla, o_sb, projf, x, target, wp_full[0], wp_full[1],
                                                 wp_full[2], gla_g, b_gate, final_g)
    dw_p = _wgrad_call([ogt, obt, mt], [0, 1, 2], [dya, dyb, dx2b], [(0, 0), (1, 0), (2, 0)], 0, "wgrad_p")
    g_p = dw_p.reshape(3, N_DEV, D // N_DEV, D).transpose(1, 0, 2, 3).reshape(N_DEV, 3 * (D // N_DEV), D)
    dqk, dgv, drank, dwdec, dbdec = _gla_bwd_call(projf, projb, la, do_gla, st_all, rank, wdec)
    dsq, dsk, dsv, r_p = _sb_bwd_call(projb, do_sb, g_p)
    pieces = [dqk, dgv, dggate, dsq, dsk, dsv, dsgate]
    rhs_of_group = [(g, 0) for g in range(7)] + [(7, 0), (7, 1)]
    dw_in, dwr = _wgrad_call([ht], [0] * N_GROUPS, pieces + [dmlog], rhs_of_group, N_GROUPS, "wgrad_in",
                             narrow=drank)
    s_in = _pair_sum_call(dw_in.reshape(N_GROUPS * 1024, D), dwr)
    small = jnp.concatenate([
        jnp.zeros((D,), F32), dbdec.reshape(-1), dgla_g.reshape(-1), db_gate.reshape(-1), dfinal_g.reshape(-1),
        loss.reshape(-1), dwdec[:GLA_RANK].reshape(-1)]).reshape(1, _SM_LEN)
    grad_x, r_in, _, r_small = _dh_call(pieces, dmlog, drank, wt, wr, x, dx2, norm_g, s_in, small)
    return grad_x, r_in, r_p, r_small


_SM_NORM = 0
_SM_BDEC = _SM_NORM + D_MODEL
_SM_GLAG = _SM_BDEC + GLA_DK
_SM_BGATE = _SM_GLAG + GLA_HV
_SM_FINAL = _SM_BGATE + 2 * D_MODEL
_SM_REPL = _SM_FINAL + D_MODEL
_SM_LOSS = _SM_REPL
_SM_WDEC = _SM_LOSS + 128
_SM_LEN = _SM_WDEC + GLA_RANK * GLA_DK


def kernel(x, norm_g, w_in, w_dec_up, b_dec, gla_norm_g, w_pa, w_pb, b_gate, w_o, final_g, loss_target, m_norm_g, m_w_in, m_w_dec_up, m_b_dec, m_gla_norm_g, m_w_pa, m_w_pb, m_b_gate, m_w_o, m_final_g, v_norm_g, v_w_in, v_w_dec_up, v_b_dec, v_gla_norm_g, v_w_pa, v_w_pb, v_b_gate, v_w_o, v_final_g):
    D = D_MODEL
    me = 4 * lax.axis_index("x") + 2 * lax.axis_index("y") + lax.axis_index("c")

    wp_shard = jnp.stack([w_pa, w_pb, w_o]).astype(BF16)
    n_first = _half_rows(SHARD_COLS)
    win_all, wdec_all = _all_gather([w_in.T.astype(BF16), w_dec_up], "gather_w",
                                    row_pieces=[[(0, n_first), (n_first, SHARD_COLS - n_first)], None])
    wt = _flatten_blocks_call(win_all)
    wr = jnp.pad(wt[RANK_COL:RANK_COL + GLA_RANK], ((0, 128 - GLA_RANK), (0, 0)))
    wdec_full = wdec_all.transpose(1, 0, 2).reshape(GLA_RANK, GLA_DK)
    wdec = jnp.pad(wdec_full, ((0, 128 - GLA_RANK), (0, 0)))

    grad_x, r_in, r_p, r_small = _local_step(
        x[0], loss_target[0], wt, wr, wdec, b_dec.reshape(1, -1), wp_shard,
        norm_g.reshape(1, -1), gla_norm_g.reshape(1, -1), b_gate.reshape(1, -1), final_g.reshape(1, -1))

    gw_in, d_in, nm_in, nv_in = (a.T for a in _adamw_call(r_in, w_in.T, m_w_in.T, v_w_in.T, "adamw_in"))
    (g_pa, d_pa, nm_pa, nv_pa), (g_pb, d_pb, nm_pb, nv_pb), (g_o, d_o, nm_o, nv_o) = _adamw_rows_call(
        r_p, [w_pa, w_pb, w_o], [m_w_pa, m_w_pb, m_w_o], [v_w_pa, v_w_pb, v_w_o], "adamw_p")

    def row(a):
        return a.reshape(1, -1)

    rep = _adamw_lanes_call(
        r_small, [_SM_NORM, _SM_BDEC, _SM_GLAG, _SM_BGATE, _SM_FINAL],
        [row(a) for a in (norm_g, b_dec, gla_norm_g, b_gate, final_g)],
        [row(a) for a in (m_norm_g, m_b_dec, m_gla_norm_g, m_b_gate, m_final_g)],
        [row(a) for a in (v_norm_g, v_b_dec, v_gla_norm_g, v_b_gate, v_final_g)], "adamw_rep")
    ((g_norm, d_norm, nm_norm, nv_norm), (g_bdec, d_bdec, nm_bdec, nv_bdec), (g_glag, d_glag, nm_glag, nv_glag),
     (g_bgate, d_bgate, nm_bgate, nv_bgate), (g_final, d_final, nm_final, nv_final)) = [
        tuple(a.reshape(-1) for a in quad) for quad in rep]

    wdec_parts = r_small[:, 0, _SM_WDEC:].reshape(N_DEV, GLA_RANK, GLA_DK)
    cols = GLA_DK // N_DEV
    wdec_mine = lax.dynamic_slice_in_dim(wdec_parts, me * cols, cols, axis=2)
    g_wdec, d_wdec, nm_wdec, nv_wdec = _adamw_call(wdec_mine, w_dec_up, m_w_dec_up, v_w_dec_up, "adamw_dec")

    loss_total = jnp.sum(r_small[:, 0, _SM_LOSS])

    return (loss_total, grad_x[None],
            g_norm, gw_in, g_wdec, g_bdec, g_glag, g_pa, g_pb, g_bgate, g_o, g_final,
            d_norm, d_in, d_wdec, d_bdec, d_glag, d_pa, d_pb, d_bgate, d_o, d_final,
            nm_norm, nm_in, nm_wdec, nm_bdec, nm_glag, nm_pa, nm_pb, nm_bgate, nm_o, nm_final,
            nv_norm, nv_in, nv_wdec, nv_bdec, nv_glag, nv_pa, nv_pb, nv_bgate, nv_o, nv_final)
```

```python
import math

import jax
import jax.numpy as jnp
from jax import lax
from jax.experimental import pallas as pl
from jax.experimental.pallas import tpu as pltpu

F32 = jnp.float32
BF16 = jnp.bfloat16

N_DEV = 8
D_MODEL = 1024
GLA_HEADS = 4
GLA_HK = 128
GLA_HV = 256
GLA_DK = 512
GLA_RANK = 16
GLA_TAU = 16.0
GLA_CHUNK = 64
SB_HEADS = 8
SB_HD = 128
EPS = 1e-6
N_GROUPS = 9
RANK_COL = 3072
IN_COLS = 9232
SHARD_COLS = IN_COLS // N_DEV

ADAM_LR = 0.001
ADAM_B1 = 0.9
ADAM_B2 = 0.999
ADAM_EPS = 1e-08
ADAM_WD = 0.01
ADAM_STEP = 10

VMEM_LIMIT = 56 * 1024 * 1024
TBLK = 256


def _cparams(sem=None):
    return pltpu.CompilerParams(dimension_semantics=sem, vmem_limit_bytes=VMEM_LIMIT)


def _tiling_2d(rows, cols, band_cols):
    if rows * cols <= 128 * 1024:
        return (rows, cols), (1,), lambda i: (0, 0)
    if rows % 128 == 0:
        return (128, cols), (rows // 128,), lambda i: (i, 0)
    tc = band_cols if cols % band_cols == 0 else cols
    return (rows, tc), (cols // tc,), lambda i: (0, i)


def _dot(a, b):
    return jnp.dot(a, b, preferred_element_type=F32)


def _dot_nt(a, b):
    return lax.dot_general(a, b, (((1,), (1,)), ((), ())), preferred_element_type=F32)


def _dot_tn(a, b):
    return lax.dot_general(a, b, (((0,), (0,)), ((), ())), preferred_element_type=F32)


def _bf(x):
    return x.astype(BF16)


def _split3(x):
    hi = x.astype(BF16)
    r = x - hi.astype(F32)
    mid = r.astype(BF16)
    lo = (r - mid.astype(F32)).astype(BF16)
    return hi, mid, lo


def _tri_left(tri, x):
    hi, mid, lo = _split3(x)
    return _dot(tri, hi) + _dot(tri, mid) + _dot(tri, lo)


def _split2(x):
    hi = lax.bitcast_convert_type(lax.bitcast_convert_type(x, jnp.uint32) & jnp.uint32(0xFFFF0000), F32)
    return hi.astype(BF16), (x - hi).astype(BF16)


def _tri2_left(tri, x):
    hi, lo = _split2(x)
    return _dot(tri, hi) + _dot(tri, lo)


def _tri2_right(x, tri):
    hi, lo = _split2(x)
    return _dot(hi, tri) + _dot(lo, tri)


def _iota2(n, m, dim):
    return lax.broadcasted_iota(jnp.int32, (n, m), dim)


def _sigmoid(x):
    return 1.0 / (1.0 + jnp.exp(-x))


def _softplus_neg_abs(z):
    return jnp.log(1.0 + jnp.exp(-jnp.abs(z)))


_ANY = pl.BlockSpec(memory_space=pl.ANY)


def _mesh_pos():
    return lax.axis_index("x"), lax.axis_index("y"), lax.axis_index("c")


def _other_chips(x, y):
    return [(1 - x, y), (x, 1 - y), (1 - x, 1 - y)]


def _rcopy(src, dst, send_sem, recv_sem, to):
    return pltpu.make_async_remote_copy(src_ref=src, dst_ref=dst, send_sem=send_sem, recv_sem=recv_sem,
                                        device_id=to, device_id_type=pl.DeviceIdType.MESH)


def _push_copies(src_ref, dst_ref, send_sems, recv_sems, loc_sem, scatter):
    x, y, c = _mesh_pos()
    me = 4 * x + 2 * y + c
    own = pltpu.make_async_copy(src_ref.at[me] if scatter else src_ref, dst_ref.at[me], loc_sem)
    pairs = []
    for k in range(1, N_DEV):
        px = 1 - x if k & 4 else x
        py = 1 - y if k & 2 else y
        pc = 1 - c if k & 1 else c
        pid = 4 * px + 2 * py + pc
        src = src_ref.at[pid] if scatter else src_ref
        send = _rcopy(src, dst_ref.at[me], send_sems.at[k - 1], recv_sems.at[k - 1], (px, py, pc))
        recv = _rcopy(src, dst_ref.at[pid], send_sems.at[k - 1], recv_sems.at[k - 1], (px, py, pc))
        pairs.append((send, recv))
    return own, pairs


def _push_start(own, pairs):
    own.start()
    for send, _ in pairs:
        send.start()


def _push_wait(own, pairs):
    for _, recv in pairs:
        recv.wait_recv()
    for send, _ in pairs:
        send.wait_send()
    own.wait()


_PUSH_SEMS = [pltpu.SemaphoreType.DMA((N_DEV - 1,)), pltpu.SemaphoreType.DMA((N_DEV - 1,)),
              pltpu.SemaphoreType.DMA]


def _half_rows(rows):
    return (rows // 2) // 16 * 16


_ADD_ROWS = 128


def _chip_reduce_steps(src_ref, dst_ref, relayed_ref, sum_x, sum_y, rel_x, rel_y, load_sems, send_sems, recv_sems,
                       loc_sem):
    _, R, C = src_ref.shape
    n0 = _half_rows(R)
    lo, hi = pl.ds(0, n0), pl.ds(n0, R - n0)
    x, y, c = _mesh_pos()
    (xx, xy), (yx, yy), (dx, dy) = _other_chips(x, y)
    to_diag, to_x, to_y = src_ref.at[2 * dx + dy], src_ref.at[2 * xx + xy], src_ref.at[2 * yx + yy]
    x_nb, y_nb = (xx, xy, c), (yx, yy, c)
    relays = (_rcopy(to_diag.at[lo], relayed_ref.at[lo], send_sems.at[0], recv_sems.at[0], x_nb),
              _rcopy(to_diag.at[hi], relayed_ref.at[hi], send_sems.at[1], recv_sems.at[1], y_nb))
    plain = (_rcopy(to_x.at[lo], dst_ref.at[0, lo], send_sems.at[2], recv_sems.at[2], x_nb),
             _rcopy(to_y.at[hi], dst_ref.at[1, hi], send_sems.at[3], recv_sems.at[3], y_nb))
    summed = (_rcopy(sum_x, dst_ref.at[0, hi], send_sems.at[4], recv_sems.at[4], x_nb),
              _rcopy(sum_y, dst_ref.at[1, lo], send_sems.at[5], recv_sems.at[5], y_nb))
    load_mine = (pltpu.make_async_copy(to_x.at[hi], sum_x, load_sems.at[0]),
                 pltpu.make_async_copy(to_y.at[lo], sum_y, load_sems.at[1]))
    load_relayed = (pltpu.make_async_copy(relayed_ref.at[hi], rel_x, load_sems.at[2]),
                    pltpu.make_async_copy(relayed_ref.at[lo], rel_y, load_sems.at[3]))
    own = pltpu.make_async_copy(src_ref.at[2 * x + y], dst_ref.at[2], loc_sem)

    def start():
        for cp in relays + plain + (own,) + load_mine:
            cp.start()

    def add(acc_ref, rel_ref):
        for r0 in range(0, acc_ref.shape[0], _ADD_ROWS):
            rows = slice(r0, min(r0 + _ADD_ROWS, acc_ref.shape[0]))
            acc_ref[rows, :] = (acc_ref[rows, :].astype(F32) + rel_ref[rows, :].astype(F32)).astype(acc_ref.dtype)

    def forward():
        for cp in relays:
            cp.wait_recv()
        for cp in load_relayed:
            cp.start()
        for cp in load_mine + load_relayed:
            cp.wait()
        add(sum_x, rel_x)
        add(sum_y, rel_y)
        for cp in summed:
            cp.start()

    def finish():
        for cp in plain + summed:
            cp.wait_recv()
        for cp in relays + plain + summed:
            cp.wait_send()
        own.wait()

    return start, forward, finish


def _chip_reduce_scratch(rows, cols, dtype):
    n0 = _half_rows(rows)
    return [pltpu.VMEM((rows - n0, cols), dtype), pltpu.VMEM((n0, cols), dtype)] * 2 + [
        pltpu.SemaphoreType.DMA((4,)), pltpu.SemaphoreType.DMA((6,)), pltpu.SemaphoreType.DMA((6,)),
        pltpu.SemaphoreType.DMA]


def _all_gather(arrs, name, row_pieces=None):
    n = len(arrs)
    pieces = [[None] if not row_pieces or not row_pieces[a] else list(row_pieces[a]) for a in range(n)]
    assert all(len(p) in (1, 2) for p in pieces)
    units = [(a, i) for a in range(n) for i in range(len(pieces[a]))]

    def body(*refs):
        ins = refs[:n]
        outs = refs[n:2 * n]
        send_sems, recv_sems, loc_sems = refs[2 * n:]
        x, y, c = _mesh_pos()
        me, sib = (x, y, c), (x, y, 1 - c)
        xn, yn, dg = [(px, py, c) for px, py in _other_chips(x, y)]

        def rows(ref, a, i):
            return ref if pieces[a][i] is None else ref.at[pl.ds(*pieces[a][i])]

        def copy(u, k, block, to, own=False):
            a, i = u
            px, py, pc = block
            dst = rows(outs[a].at[4 * px + 2 * py + pc], a, i)
            return _rcopy(rows(ins[a], a, i) if own else dst, dst, send_sems.at[a, k, i], recv_sems.at[a, k, i], to)

        started = []

        def start(cp):
            cp.start()
            started.append(cp)

        def landed_then_pass_on(u, k, block):
            copy(u, k, block, me).wait_recv()
            start(copy(u, 3 + k, block, sib))

        mine = [pltpu.make_async_copy(ins[a], outs[a].at[4 * x + 2 * y + c], loc_sems.at[a]) for a in range(n)]
        for cp in mine:
            cp.start()
        for u in units:
            start(copy(u, 0, me, sib, own=True))
        for a in range(n):
            if len(pieces[a]) == 2:
                for i, to, k in ((0, xn, 1), (1, yn, 2), (1, xn, 1), (0, yn, 2)):
                    start(copy((a, i), k, me, to, own=True))
            else:
                for to, k in ((xn, 1), (yn, 2), (dg, 3)):
                    start(copy((a, 0), k, me, to, own=True))
        for a in range(n):
            if len(pieces[a]) == 2:
                landed_then_pass_on((a, 0), 1, xn)
                start(copy((a, 0), 3, xn, yn))
                landed_then_pass_on((a, 1), 2, yn)
                start(copy((a, 1), 3, yn, xn))
                landed_then_pass_on((a, 1), 1, xn)
                landed_then_pass_on((a, 0), 2, yn)
                landed_then_pass_on((a, 0), 3, dg)
                landed_then_pass_on((a, 1), 3, dg)
            else:
                for block, k in ((xn, 1), (yn, 2), (dg, 3)):
                    landed_then_pass_on((a, 0), k, block)
        for u in units:
            copy(u, 0, sib, me).wait_recv()
            for k, (px, py, _) in ((4, xn), (5, yn), (6, dg)):
                copy(u, k, (px, py, 1 - c), me).wait_recv()
        for cp in started:
            cp.wait_send()
        for cp in mine:
            cp.wait()

    n_pc = max(len(p) for p in pieces)

    return pl.pallas_call(
        body, name=name,
        out_shape=tuple(jax.ShapeDtypeStruct((N_DEV,) + a.shape, a.dtype) for a in arrs),
        in_specs=[_ANY] * n,
        out_specs=tuple([_ANY] * n),
        scratch_shapes=[pltpu.SemaphoreType.DMA((n, 7, n_pc)), pltpu.SemaphoreType.DMA((n, 7, n_pc)),
                        pltpu.SemaphoreType.DMA((n,))],
    )(*arrs)


def _flatten_blocks_call(blocks):
    n, R, C = blocks.shape
    tc = C // 2

    def body(in_ref, out_ref):
        for p in range(n):
            out_ref[p * R:(p + 1) * R, :] = in_ref[p]

    return pl.pallas_call(
        body, name="flatten_w",
        grid=(C // tc,),
        in_specs=[pl.BlockSpec((n, R, tc), lambda i: (0, 0, i))],
        out_specs=pl.BlockSpec((n * R, tc), lambda i: (0, i)),
        out_shape=jax.ShapeDtypeStruct((n * R, C), blocks.dtype),
        compiler_params=_cparams(("arbitrary",)),
    )(blocks)


_PARTS_BANDS = 4


def _pair_sum_call(dmain, drank):
    D = dmain.shape[1]
    n = _PARTS_BANDS
    tc = D // n

    def body(dm_ref, dr_ref, sum_ref, laid, got, send_sems, recv_sems):
        x, y, c = _mesh_pos()

        def pushes(k):
            return [_rcopy(laid.at[k % 2, 2 * q + (1 - c)], got.at[k, q], send_sems.at[k, q], recv_sems.at[k, q],
                           (x, y, 1 - c)) for q in range(4)]

        def lay_out(k):
            for p in range(N_DEV):
                lo, hi = p * SHARD_COLS, (p + 1) * SHARD_COLS
                at = 0
                for src, a, b in ((dm_ref, lo, min(hi, RANK_COL)),
                                  (dr_ref, max(lo, RANK_COL) - RANK_COL, min(hi, RANK_COL + GLA_RANK) - RANK_COL),
                                  (dm_ref, max(lo, RANK_COL + GLA_RANK) - GLA_RANK, hi - GLA_RANK)):
                    if b > a:
                        laid[k % 2, p, at:at + (b - a), :] = src[a:b, :]
                        at += b - a

        for k in range(n + 1):
            @pl.when(pl.program_id(0) == k)
            def _(k=k):
                if k < n:
                    if k >= 2:
                        for cp in pushes(k - 2):
                            cp.wait_send()
                    lay_out(k)
                    for cp in pushes(k):
                        cp.start()
                if k >= 1:
                    for cp in pushes(k - 1):
                        cp.wait_recv()
                    for q in range(4):
                        sum_ref[q] = (laid[(k - 1) % 2, 2 * q + c].astype(F32)
                                      + got[k - 1, q].astype(F32)).astype(sum_ref.dtype)
                if k == n:
                    for k_open in range(max(0, n - 2), n):
                        for cp in pushes(k_open):
                            cp.wait_send()

    sems = pltpu.SemaphoreType.DMA((n, 4))
    return pl.pallas_call(
        body, name="pair_sum",
        grid=(n + 1,),
        in_specs=[pl.BlockSpec((dmain.shape[0], tc), lambda k: (0, jnp.minimum(k, n - 1))),
                  pl.BlockSpec((GLA_RANK, tc), lambda k: (0, jnp.minimum(k, n - 1)))],
        out_specs=pl.BlockSpec((4, SHARD_COLS, tc), lambda k: (0, 0, jnp.maximum(k - 1, 0))),
        out_shape=jax.ShapeDtypeStruct((4, SHARD_COLS, D), dmain.dtype),
        scratch_shapes=[pltpu.VMEM((2, N_DEV, SHARD_COLS, tc), dmain.dtype),
                        pltpu.VMEM((n, 4, SHARD_COLS, tc), dmain.dtype), sems, sems],
        compiler_params=_cparams(("arbitrary",)),
    )(dmain, drank)


def _group_row(g):
    return GLA_RANK * (g * (1024 // GLA_RANK) + (g >= RANK_COL // 1024))


def _proj_call(x, norm_g, wt, wr, wp_part):
    T, D = x.shape
    tm = min(1024, T)
    assert tm % TBLK == 0
    n_i = T // tm

    def f_slot(j):
        return ((j >= 2).astype(jnp.int32) + (j >= 6).astype(jnp.int32)
                + (j >= 7).astype(jnp.int32) + (j >= 8).astype(jnp.int32))

    def b_slot(j):
        return (j >= 3).astype(jnp.int32) + (j >= 4).astype(jnp.int32) + (j >= 5).astype(jnp.int32)

    def body(x_ref, g_ref, w_ref, wr_ref, wp_ref, pf_ref, pb_ref, rank_ref, ht_ref, wpall_ref,
             h_scr, send_sems, recv_sems, loc_sem):
        i = pl.program_id(0)
        j = pl.program_id(1)
        own, pairs = _push_copies(wp_ref, wpall_ref, send_sems, recv_sems, loc_sem, scatter=False)

        @pl.when((i == 0) & (j == 0))
        def _():
            _push_start(own, pairs)

        @pl.when(j == 0)
        def _():
            xv = x_ref[...]
            r = lax.rsqrt(jnp.mean(xv * xv, axis=-1, keepdims=True) + EPS)
            h = (xv * r) * g_ref[...]
            hb = _bf(h)
            h_scr[...] = hb
            for b in range(tm // TBLK):
                ht_ref[b] = _bf(h[b * TBLK:(b + 1) * TBLK].T)
            rank_ref[...] = _dot_nt(hb, wr_ref[...])

        is_b = (j == 1) | ((j >= 3) & (j <= 5))

        @pl.when(is_b)
        def _():
            pb_ref[...] = _bf(_dot_nt(h_scr[...], w_ref[...]))

        @pl.when(jnp.logical_not(is_b))
        def _():
            pf_ref[...] = _dot_nt(h_scr[...], w_ref[...])

        @pl.when((i == n_i - 1) & (j == N_GROUPS - 1))
        def _():
            _push_wait(own, pairs)

    return pl.pallas_call(
        body, name="proj",
        grid=(n_i, N_GROUPS),
        in_specs=[pl.BlockSpec((tm, D), lambda i, j: (i, 0)),
                  pl.BlockSpec((1, D), lambda i, j: (0, 0)),
                  pl.BlockSpec((pl.Element(1024), pl.Element(D)), lambda i, j: (_group_row(j), 0)),
                  pl.BlockSpec((128, D), lambda i, j: (0, 0)),
                  _ANY],
        out_specs=(pl.BlockSpec((None, tm, 1024), lambda i, j: (f_slot(j), i, 0)),
                   pl.BlockSpec((None, tm, 1024), lambda i, j: (b_slot(j), i, 0)),
                   pl.BlockSpec((tm, 128), lambda i, j: (i, 0)),
                   pl.BlockSpec((tm // TBLK, D, TBLK), lambda i, j: (i, 0, 0)),
                   _ANY),
        out_shape=(jax.ShapeDtypeStruct((5, T, 1024), F32),
                   jax.ShapeDtypeStruct((4, T, 1024), BF16),
                   jax.ShapeDtypeStruct((T, 128), F32),
                   jax.ShapeDtypeStruct((T // TBLK, D, TBLK), BF16),
                   jax.ShapeDtypeStruct((N_DEV,) + wp_part.shape, wp_part.dtype)),
        scratch_shapes=[pltpu.VMEM((tm, D), BF16)] + _PUSH_SEMS,
        compiler_params=_cparams(("arbitrary", "arbitrary")),
    )(x, norm_g, wt, wr, wp_part)


GLA_STEP_CHUNKS = 4


def _gla_same_chunk(rows):
    return (_iota2(rows, rows, 0) & -GLA_CHUNK) == (_iota2(rows, rows, 1) & -GLA_CHUNK)


def _gla_chunk_terms(la, q, k, n_c):
    C = GLA_CHUNK
    rows = n_c * C
    low = _gla_same_chunk(rows) & (_iota2(rows, rows, 0) >= _iota2(rows, rows, 1))
    b = _tri_left(_bf(low.astype(F32)), la)
    bl = [b[(c + 1) * C - 1:(c + 1) * C, :] for c in range(n_c)]
    bl_rows = jnp.concatenate([jnp.broadcast_to(bl[c], (C, b.shape[1])) for c in range(n_c)], axis=0)
    eb = jnp.exp(b)
    enb = jnp.exp(-b)
    ebl_b = jnp.exp(bl_rows - b)
    scale = GLA_HK ** -0.5
    qe = q * eb * scale
    ke = k * enb
    kd = k * ebl_b
    return bl, eb, enb, ebl_b, qe, ke, kd


def _gla_fwd_call(projf, projb, rank, wdec, bdec):
    T = projf.shape[1]
    C = GLA_CHUNK
    n_chunks = T // C
    n_c = GLA_STEP_CHUNKS
    R = n_c * C
    assert n_chunks % n_c == 0

    def body(qk_ref, v_ref, rank_ref, wd_ref, bd_ref, o_ref, st_ref, la_ref, st_scr):
        @pl.when(pl.program_id(0) == 0)
        def _():
            st_scr[...] = jnp.zeros_like(st_scr)

        dec = _dot(_bf(rank_ref[...]), _bf(wd_ref[...])) + bd_ref[...]
        la = (jnp.minimum(dec, 0.0) - _softplus_neg_abs(dec)) / GLA_TAU
        la_ref[...] = la
        mask = _gla_same_chunk(R) & (_iota2(R, R, 0) >= _iota2(R, R, 1))
        bl, _, _, _, qe, ke, kd = _gla_chunk_terms(la, qk_ref[:, :GLA_DK], qk_ref[:, GLA_DK:], n_c)
        qeb, keb, kdb = _bf(qe), _bf(ke), _bf(kd)
        ebl = [jnp.exp(bl[c]) for c in range(n_c)]
        heads = range(GLA_HEADS)
        ks = [slice(hh * GLA_HK, (hh + 1) * GLA_HK) for hh in heads]
        vs = [slice(hh * GLA_HV, (hh + 1) * GLA_HV) for hh in heads]
        rs = [slice(c * C, (c + 1) * C) for c in range(n_c)]
        p = [_bf(jnp.where(mask, _dot_nt(qeb[:, ks[hh]], keb[:, ks[hh]]), 0.0)) for hh in heads]
        upd = [[_dot_tn(v_ref[rs[c], vs[hh]], kdb[rs[c], ks[hh]]) for hh in heads] for c in range(n_c)]
        intra = [_dot(p[hh], v_ref[:, vs[hh]]) for hh in heads]
        st = [st_scr[hh] for hh in heads]
        for c in range(n_c):
            inter = [_dot_nt(qeb[rs[c], ks[hh]], _bf(st[hh])) for hh in heads]
            for hh in heads:
                st_ref[c, hh] = st[hh]
                o_ref[rs[c], vs[hh]] = intra[hh][rs[c]] + inter[hh]
            st = [st[hh] * ebl[c][:, ks[hh]] + upd[c][hh] for hh in heads]
        for hh in heads:
            st_scr[hh] = st[hh]

    return pl.pallas_call(
        body, name="gla_fwd",
        grid=(n_chunks // n_c,),
        in_specs=[pl.BlockSpec((None, R, 1024), lambda n: (0, n, 0)),
                  pl.BlockSpec((None, R, 1024), lambda n: (0, n, 0)),
                  pl.BlockSpec((R, 128), lambda n: (n, 0)),
                  pl.BlockSpec((128, GLA_DK), lambda n: (0, 0)),
                  pl.BlockSpec((1, GLA_DK), lambda n: (0, 0))],
        out_specs=(pl.BlockSpec((R, 1024), lambda n: (n, 0)),
                   pl.BlockSpec((n_c, GLA_HEADS, GLA_HV, GLA_HK), lambda n: (n, 0, 0, 0)),
                   pl.BlockSpec((R, GLA_DK), lambda n: (n, 0))),
        out_shape=(jax.ShapeDtypeStruct((T, 1024), F32),
                   jax.ShapeDtypeStruct((n_chunks, GLA_HEADS, GLA_HV, GLA_HK), F32),
                   jax.ShapeDtypeStruct((T, GLA_DK), F32)),
        scratch_shapes=[pltpu.VMEM((GLA_HEADS, GLA_HV, GLA_HK), F32)],
        compiler_params=_cparams(("arbitrary",)),
    )(projf, projb, rank, wdec, bdec)


def _gla_bwd_call(projf, projb, la, do_gla, st_all, rank, wdec):
    T = projf.shape[1]
    C = GLA_CHUNK
    n_chunks = T // C
    n_c = GLA_STEP_CHUNKS
    R = n_c * C
    assert n_chunks % n_c == 0
    last = n_chunks // n_c - 1

    def body(qk_ref, v_ref, la_ref, do_ref, st_ref, rank_ref, wd_ref,
             dqk_ref, dv_ref, drank_ref, dwd_ref, dbd_ref, dst_scr):
        @pl.when(pl.program_id(0) == 0)
        def _():
            dst_scr[...] = jnp.zeros_like(dst_scr)
            dwd_ref[...] = jnp.zeros_like(dwd_ref)
            dbd_ref[...] = jnp.zeros_like(dbd_ref)

        same = _gla_same_chunk(R)
        mask = same & (_iota2(R, R, 0) >= _iota2(R, R, 1))
        upp = _bf((same & (_iota2(R, R, 0) <= _iota2(R, R, 1))).astype(F32))
        scale = GLA_HK ** -0.5
        la = la_ref[...]
        bl, eb, enb, ebl_b, qe, ke, kd = _gla_chunk_terms(la, qk_ref[:, :GLA_DK], qk_ref[:, GLA_DK:], n_c)
        qeb, keb, kdb = _bf(qe), _bf(ke), _bf(kd)
        ebl = [jnp.exp(bl[c]) for c in range(n_c)]
        heads = range(GLA_HEADS)
        ks = [slice(hh * GLA_HK, (hh + 1) * GLA_HK) for hh in heads]
        vs = [slice(hh * GLA_HV, (hh + 1) * GLA_HV) for hh in heads]
        rs = [slice(c * C, (c + 1) * C) for c in range(n_c)]
        v = [v_ref[:, vs[hh]] for hh in heads]
        do = [_bf(do_ref[:, vs[hh]]) for hh in heads]
        p = [_bf(jnp.where(mask, _dot_nt(qeb[:, ks[hh]], keb[:, ks[hh]]), 0.0)) for hh in heads]
        dp = [_bf(jnp.where(mask, _dot_nt(do[hh], v[hh]), 0.0)) for hh in heads]
        dst_intra = [[_dot_tn(do[hh][rs[c]], qeb[rs[c], ks[hh]]) for hh in heads] for c in range(n_c)]
        dqe_inter = [[_dot(do[hh][rs[c]], _bf(st_ref[c, hh])) for hh in heads] for c in range(n_c)]
        dv_intra = [_dot_tn(p[hh], do[hh]) for hh in heads]
        dqe_intra = [_dot(dp[hh], keb[:, ks[hh]]) for hh in heads]
        dke = jnp.concatenate([_dot_tn(dp[hh], qeb[:, ks[hh]]) for hh in heads], axis=1)
        dstn = [dst_scr[hh] for hh in heads]
        dkd_c, dv_inter, debl = [None] * n_c, [None] * n_c, [None] * n_c
        for c in reversed(range(n_c)):
            dstnb = [_bf(dstn[hh]) for hh in heads]
            dkd_c[c] = jnp.concatenate([_dot(v[hh][rs[c]], dstnb[hh]) for hh in heads], axis=1)
            dv_inter[c] = [_dot_nt(kdb[rs[c], ks[hh]], dstnb[hh]) for hh in heads]
            debl[c] = jnp.concatenate(
                [jnp.sum(dstn[hh] * st_ref[c, hh], axis=0, keepdims=True) for hh in heads], axis=1)
            dstn = [dst_intra[c][hh] + dstn[hh] * ebl[c][:, ks[hh]] for hh in heads]
        for hh in heads:
            dst_scr[hh] = dstn[hh]
            dv_ref[:, vs[hh]] = _bf(dv_intra[hh] + jnp.concatenate([dv_inter[c][hh] for c in range(n_c)], axis=0))
        dqe = jnp.concatenate(
            [dqe_intra[hh] + jnp.concatenate([dqe_inter[c][hh] for c in range(n_c)], axis=0) for hh in heads], axis=1)
        dkd = jnp.concatenate(dkd_c, axis=0)
        dkd_kd = dkd * kd
        db = dqe * qe - dke * ke - dkd_kd
        dbl = jnp.concatenate(
            [jnp.broadcast_to(jnp.sum(dkd_kd[rs[c]], axis=0, keepdims=True) + ebl[c] * debl[c], (C, GLA_DK))
             for c in range(n_c)], axis=0)
        dla = _tri_left(upp, db) + dbl
        dqk_ref[:, :GLA_DK] = _bf(dqe * eb * scale)
        dqk_ref[:, GLA_DK:] = _bf(dke * enb + dkd * ebl_b)
        ddec = dla * (1.0 / GLA_TAU) * (1.0 - jnp.exp(GLA_TAU * la))
        ddecb = _bf(ddec)
        drank_ref[...] = _bf(_dot_nt(ddecb, _bf(wd_ref[...])))
        dwd_ref[...] += _dot_tn(_bf(rank_ref[...]), ddecb)
        dbd_ref[...] += jnp.sum(ddec, axis=0, keepdims=True)

    return pl.pallas_call(
        body, name="gla_bwd",
        grid=(n_chunks // n_c,),
        in_specs=[pl.BlockSpec((None, R, 1024), lambda n: (0, last - n, 0)),
                  pl.BlockSpec((None, R, 1024), lambda n: (0, last - n, 0)),
                  pl.BlockSpec((R, GLA_DK), lambda n: (last - n, 0)),
                  pl.BlockSpec((R, 1024), lambda n: (last - n, 0)),
                  pl.BlockSpec((n_c, GLA_HEADS, GLA_HV, GLA_HK), lambda n: (last - n, 0, 0, 0)),
                  pl.BlockSpec((R, 128), lambda n: (last - n, 0)),
                  pl.BlockSpec((128, GLA_DK), lambda n: (0, 0))],
        out_specs=(pl.BlockSpec((R, 1024), lambda n: (last - n, 0)),
                   pl.BlockSpec((R, 1024), lambda n: (last - n, 0)),
                   pl.BlockSpec((R, 128), lambda n: (last - n, 0)),
                   pl.BlockSpec((128, GLA_DK), lambda n: (0, 0)),
                   pl.BlockSpec((1, GLA_DK), lambda n: (0, 0))),
        out_shape=(jax.ShapeDtypeStruct((T, 1024), BF16),
                   jax.ShapeDtypeStruct((T, 1024), BF16),
                   jax.ShapeDtypeStruct((T, 128), BF16),
                   jax.ShapeDtypeStruct((128, GLA_DK), F32),
                   jax.ShapeDtypeStruct((1, GLA_DK), F32)),
        scratch_shapes=[pltpu.VMEM((GLA_HEADS, GLA_HV, GLA_HK), F32)],
        compiler_params=_cparams(("arbitrary",)),
    )(projf, projb, la, do_gla, st_all, rank, wdec)


def _sb_logs(z):
    lsz = jnp.minimum(z, 0.0) - _softplus_neg_abs(z)
    return lsz, lsz - z


SB_HG_FWD = 8
SB_HG_BWD = 4
SB_QUERIES = 256
SB_KEYS = 256
SB_DEAD = -105.0


def _sb_fwd_call(projb, wp_shard):
    T = projb.shape[1]
    B = min(SB_QUERIES, T)
    HG = SB_HG_FWD
    W = HG * SB_HD
    scale = 1.0 / math.sqrt(SB_HD)
    KB = min(SB_KEYS, T)
    n_h, n_i = SB_HEADS // HG, T // B

    def body(q_ref, k_ref, v_ref, wp_ref, o_ref, wpall_ref, cb_scr, send_sems, recv_sems, loc_sem):
        i = pl.program_id(1)
        own, pairs = _push_copies(wp_ref, wpall_ref, send_sems, recv_sems, loc_sem, scatter=False)

        @pl.when((pl.program_id(0) == 0) & (i == 0))
        def _():
            _push_start(own, pairs)

        rows = HG * B
        after = (_iota2(KB, KB, 0) > _iota2(KB, KB, 1)).astype(F32)
        tri = _bf(jnp.concatenate([after, jnp.ones((KB, KB), F32)], axis=1))
        o_ref[...] = jnp.zeros_like(o_ref)
        cb_scr[...] = jnp.zeros_like(cb_scr)

        def block(jp, masked):
            off = pl.multiple_of(jp * KB, KB)
            z = jnp.concatenate(
                [_dot_nt(q_ref[:, hh * SB_HD:(hh + 1) * SB_HD], k_ref[pl.ds(off, KB), hh * SB_HD:(hh + 1) * SB_HD])
                 for hh in range(HG)], axis=0) * scale
            lsz, l1m = _sb_logs(z)
            if masked:
                strict = (jp * KB + _iota2(rows, KB, 1)) < (i * B + (_iota2(rows, KB, 0) & (B - 1)))
                l1m = jnp.where(strict, l1m, 0.0)
            r = _tri2_right(l1m, tri)
            cb = cb_scr[...]
            a = jnp.exp(lsz + cb + r[:, :KB])
            if masked:
                a = jnp.where(strict, a, 0.0)
            cb_scr[...] = cb + r[:, KB:]
            ab = _bf(a)
            for hh in range(HG):
                cs = slice(hh * SB_HD, (hh + 1) * SB_HD)
                o_ref[:, cs] += _dot(ab[hh * B:(hh + 1) * B, :], v_ref[pl.ds(off, KB), cs])

        jp0 = (i * B) // KB
        block(jp0, True)

        def live(state):
            jj, dead = state
            return (jj <= jp0) & jnp.logical_not(dead)

        def step(state):
            jj, _ = state
            block(jp0 - jj, False)
            return jj + 1, jnp.max(cb_scr[:, :SB_HD]) < SB_DEAD

        lax.while_loop(live, step, (jnp.int32(1), jnp.max(cb_scr[:, :SB_HD]) < SB_DEAD))

        @pl.when((pl.program_id(0) == n_h - 1) & (i == n_i - 1))
        def _():
            _push_wait(own, pairs)

    return pl.pallas_call(
        body, name="sb_fwd",
        grid=(n_h, n_i),
        in_specs=[pl.BlockSpec((None, B, W), lambda h, i: (1, i, h)),
                  pl.BlockSpec((None, T, W), lambda h, i: (2, 0, h)),
                  pl.BlockSpec((None, T, W), lambda h, i: (3, 0, h)),
                  _ANY],
        out_specs=(pl.BlockSpec((B, W), lambda h, i: (i, h)), _ANY),
        out_shape=(jax.ShapeDtypeStruct((T, 1024), F32),
                   jax.ShapeDtypeStruct((N_DEV,) + wp_shard.shape, wp_shard.dtype)),
        scratch_shapes=[pltpu.VMEM((HG * B, KB), F32)] + _PUSH_SEMS,
        compiler_params=_cparams(("arbitrary", "arbitrary")),
    )(projb, projb, projb, wp_shard)


def _sb_bwd_call(projb, do_sb, g_p):
    T = projb.shape[1]
    B = min(SB_QUERIES, T)
    nb = T // B
    HG = SB_HG_BWD
    W = HG * SB_HD
    WQ = HG * B
    KB = min(SB_KEYS, T)
    nkb = T // KB
    n_h = SB_HEADS // HG
    scale = 1.0 / math.sqrt(SB_HD)

    def body(q_ref, k_ref, v_ref, do_ref, gp_ref, dq_ref, dk_ref, dv_ref, rp_ref,
             dk_scr, dv_scr, kt_scr, beta_scr, g_scr, dqt_scr, send_sems, recv_sems, loc_sem):
        i = pl.program_id(1)
        own, pairs = _push_copies(gp_ref, rp_ref, send_sems, recv_sems, loc_sem, scatter=True)

        @pl.when((pl.program_id(0) == 0) & (i == 0))
        def _():
            _push_start(own, pairs)

        @pl.when(i == 0)
        def _():
            dk_scr[...] = jnp.zeros_like(dk_scr)
            dv_scr[...] = jnp.zeros_like(dv_scr)
            for hh in range(HG):
                for jb in range(nkb):
                    kt_scr[hh, jb] = _bf(
                        k_ref[jb * KB:(jb + 1) * KB, hh * SB_HD:(hh + 1) * SB_HD].astype(F32).T)

        dqt_scr[...] = jnp.zeros_like(dqt_scr)
        later = _bf((_iota2(KB, KB, 1) > _iota2(KB, KB, 0)).astype(F32))
        earlier = _bf((_iota2(KB, KB, 1) < _iota2(KB, KB, 0)).astype(F32))
        dob = _bf(do_ref[...])
        jp0 = (i * B) // KB

        def strict_mask():
            return (jp0 * KB + _iota2(KB, WQ, 0)) < (i * B + (_iota2(KB, WQ, 1) & (B - 1)))

        def heads(fn):
            return [fn(slice(hh * SB_HD, (hh + 1) * SB_HD)) for hh in range(HG)]

        def pass1(jp, cb, masked):
            off = pl.multiple_of(jp * KB, KB)
            z = jnp.concatenate(heads(lambda cs: _dot_nt(k_ref[pl.ds(off, KB), cs], q_ref[:, cs])), axis=1) * scale
            da = jnp.concatenate(heads(lambda cs: _dot_nt(v_ref[pl.ds(off, KB), cs], dob[:, cs])), axis=1)
            lsz, l1m = _sb_logs(z)
            if masked:
                strict = strict_mask()
                l1m = jnp.where(strict, l1m, 0.0)
            a = jnp.exp(lsz + cb + _tri2_left(later, l1m))
            if masked:
                a = jnp.where(strict, a, 0.0)
            g_scr[jp] = a * da
            beta_scr[jp] = jnp.exp(lsz)
            ab = _bf(a)
            for hh in range(HG):
                cs = slice(hh * SB_HD, (hh + 1) * SB_HD)
                dv_scr[pl.ds(off, KB), cs] += _dot(ab[:, hh * B:(hh + 1) * B], dob[:, cs])
            return cb + jnp.sum(l1m, axis=0, keepdims=True)

        zero = jnp.zeros((1, WQ), F32)
        cb = pass1(jp0, zero, True)

        def live(state):
            jj, _, dead = state
            return (jj <= jp0) & jnp.logical_not(dead)

        def step(state):
            jj, cr, _ = state
            cr = pass1(jp0 - jj, cr, False)
            return jj + 1, cr, jnp.max(cr) < SB_DEAD

        n_done, _, _ = lax.while_loop(live, step, (jnp.int32(1), cb, jnp.max(cb) < SB_DEAD))
        jp_first = jp0 - (n_done - 1)

        def pass2(jp, cg, masked):
            off = pl.multiple_of(jp * KB, KB)
            g = g_scr[jp]
            beta = beta_scr[jp]
            dz = g * (1.0 - beta) - beta * (cg + _tri2_left(earlier, g))
            if masked:
                dz = jnp.where(strict_mask(), dz, 0.0)
            dzb = _bf(dz * scale)
            for hh in range(HG):
                cs = slice(hh * SB_HD, (hh + 1) * SB_HD)
                dk_scr[pl.ds(off, KB), cs] += _dot(dzb[:, hh * B:(hh + 1) * B], q_ref[:, cs])
                dqt_scr[hh] += _dot(kt_scr[hh, jp], dzb[:, hh * B:(hh + 1) * B])
            return cg + jnp.sum(g, axis=0, keepdims=True)

        cg = lax.fori_loop(jp_first, jp0, lambda jp, cr: pass2(jp, cr, False), zero)
        pass2(jp0, cg, True)
        for hh in range(HG):
            dq_ref[:, hh * SB_HD:(hh + 1) * SB_HD] = _bf(dqt_scr[hh].T)

        @pl.when(i == nb - 1)
        def _():
            dk_ref[...] = _bf(dk_scr[...])
            dv_ref[...] = _bf(dv_scr[...])

        @pl.when((pl.program_id(0) == n_h - 1) & (i == nb - 1))
        def _():
            _push_wait(own, pairs)

    return pl.pallas_call(
        body, name="sb_bwd",
        grid=(n_h, nb),
        in_specs=[pl.BlockSpec((None, B, W), lambda h, i: (1, i, h)),
                  pl.BlockSpec((None, T, W), lambda h, i: (2, 0, h)),
                  pl.BlockSpec((None, T, W), lambda h, i: (3, 0, h)),
                  pl.BlockSpec((B, W), lambda h, i: (i, h)),
                  _ANY],
        out_specs=(pl.BlockSpec((B, W), lambda h, i: (i, h)),
                   pl.BlockSpec((T, W), lambda h, i: (0, h)),
                   pl.BlockSpec((T, W), lambda h, i: (0, h)),
                   _ANY),
        out_shape=(jax.ShapeDtypeStruct((T, 1024), BF16),
                   jax.ShapeDtypeStruct((T, 1024), BF16),
                   jax.ShapeDtypeStruct((T, 1024), BF16),
                   jax.ShapeDtypeStruct(g_p.shape, g_p.dtype)),
        scratch_shapes=[pltpu.VMEM((T, W), F32), pltpu.VMEM((T, W), F32),
                        pltpu.VMEM((HG, nkb, SB_HD, KB), BF16),
                        pltpu.VMEM((nkb, KB, WQ), F32), pltpu.VMEM((nkb, KB, WQ), F32),
                        pltpu.VMEM((HG, SB_HD, B), F32)] + _PUSH_SEMS,
        compiler_params=_cparams(("arbitrary", "arbitrary")),
    )(projb, projb, projb, do_sb, g_p)


def _mid_call(o_gla, o_sb, projf, x, target, wpa, wpb, wo, gla_g, b_gate, final_g):
    T, D = x.shape
    tm = min(TBLK, T)

    def body(og_ref, ggate_ref, osb_ref, sgate_ref, ma_ref, mb_ref, x_ref, tgt_ref,
             wpa_ref, wpb_ref, wo_ref, glag_ref, bg_ref, fg_ref,
             dx2_ref, dogla_ref, dosb_ref, dggate_ref, dsgate_ref, dm_ref,
             mt_ref, ogt_ref, obt_ref, dx2b_ref, dya_ref, dyb_ref,
             dfg_ref, dbg_ref, dglag_ref, loss_ref):
        @pl.when(pl.program_id(0) == 0)
        def _():
            dfg_ref[...] = jnp.zeros_like(dfg_ref)
            dbg_ref[...] = jnp.zeros_like(dbg_ref)
            dglag_ref[...] = jnp.zeros_like(dglag_ref)
            loss_ref[...] = jnp.zeros_like(loss_ref)

        glag = glag_ref[...]
        ggate = ggate_ref[...]
        sg = _sigmoid(ggate)
        silu_g = ggate * sg
        ohat, rinv, nrm = [], [], []
        for hh in range(GLA_HEADS):
            oh = og_ref[:, hh * GLA_HV:(hh + 1) * GLA_HV]
            r = lax.rsqrt(jnp.mean(oh * oh, axis=-1, keepdims=True) + EPS)
            ohat.append(oh * r)
            rinv.append(r)
            nrm.append(ohat[-1] * glag)
        n_all = jnp.concatenate(nrm, axis=1)
        og = n_all * silu_g
        ogb = _bf(og)
        ya = _dot(ogb, wpa_ref[...])
        sgate = sgate_ref[...]
        ss = _sigmoid(sgate)
        silu_s = sgate * ss
        osb = osb_ref[...]
        ob = osb * silu_s
        obb = _bf(ob)
        yb = _dot(obb, wpb_ref[...])
        ga = _sigmoid(ma_ref[...] + bg_ref[:, :D])
        gb = _sigmoid(mb_ref[...] + bg_ref[:, D:])
        merged = ga * ya + gb * yb
        mgb = _bf(merged)
        x2 = x_ref[...] + _dot(mgb, wo_ref[...])
        r2 = lax.rsqrt(jnp.mean(x2 * x2, axis=-1, keepdims=True) + EPS)
        xh2 = x2 * r2
        fg = fg_ref[...]
        err = xh2 * fg - tgt_ref[...]
        loss_ref[...] += jnp.broadcast_to(
            0.5 * jnp.sum(jnp.mean(err * err, axis=-1, keepdims=True), axis=0, keepdims=True), (1, 128))
        dy = err * (1.0 / D)
        dfg_ref[...] += jnp.sum(dy * xh2, axis=0, keepdims=True)
        dxh = dy * fg
        dx2 = r2 * (dxh - xh2 * jnp.mean(dxh * xh2, axis=-1, keepdims=True))
        dx2_ref[...] = dx2
        dx2b = _bf(dx2)
        dx2b_ref[...] = dx2b
        dmerged = _dot_nt(dx2b, wo_ref[...])
        dya = dmerged * ga
        dyb = dmerged * gb
        dma = dmerged * ya * ga * (1.0 - ga)
        dmb = dmerged * yb * gb * (1.0 - gb)
        dm_ref[:, :D] = _bf(dma)
        dm_ref[:, D:] = _bf(dmb)
        dbg_ref[:, :D] += jnp.sum(dma, axis=0, keepdims=True)
        dbg_ref[:, D:] += jnp.sum(dmb, axis=0, keepdims=True)
        dyab = _bf(dya)
        dybb = _bf(dyb)
        dya_ref[...] = dyab
        dyb_ref[...] = dybb
        dog = _dot_nt(dyab, wpa_ref[...])
        dob = _dot_nt(dybb, wpb_ref[...])
        dosb_ref[...] = dob * silu_s
        dsgate_ref[...] = _bf(dob * osb * (ss * (1.0 + sgate * (1.0 - ss))))
        dn = dog * silu_g
        dggate_ref[...] = _bf(dog * n_all * (sg * (1.0 + ggate * (1.0 - sg))))
        dglag = jnp.zeros((1, GLA_HV), F32)
        for hh in range(GLA_HEADS):
            dnh = dn[:, hh * GLA_HV:(hh + 1) * GLA_HV]
            dglag = dglag + jnp.sum(dnh * ohat[hh], axis=0, keepdims=True)
            dohat = dnh * glag
            dogla_ref[:, hh * GLA_HV:(hh + 1) * GLA_HV] = rinv[hh] * (
                dohat - ohat[hh] * jnp.mean(dohat * ohat[hh], axis=-1, keepdims=True))
        dglag_ref[...] += dglag
        mt_ref[...] = _bf(merged.T)
        ogt_ref[...] = _bf(og.T)
        obt_ref[...] = _bf(ob.T)

    row = lambda i: (i, 0)
    const = lambda i: (0, 0)
    tile = pl.BlockSpec((tm, D), row)
    tile_t = pl.BlockSpec((None, D, tm), lambda i: (i, 0, 0))
    wspec = pl.BlockSpec((D, D), const)
    return pl.pallas_call(
        body, name="mid",
        grid=(T // tm,),
        in_specs=[tile,
                  pl.BlockSpec((None, tm, D), lambda i: (1, i, 0)),
                  tile,
                  pl.BlockSpec((None, tm, D), lambda i: (2, i, 0)),
                  pl.BlockSpec((None, tm, D), lambda i: (3, i, 0)),
                  pl.BlockSpec((None, tm, D), lambda i: (4, i, 0)),
                  tile, tile, wspec, wspec, wspec,
                  pl.BlockSpec((1, GLA_HV), const),
                  pl.BlockSpec((1, 2 * D), const),
                  pl.BlockSpec((1, D), const)],
        out_specs=(tile, tile, tile, tile, tile,
                   pl.BlockSpec((tm, 2 * D), row),
                   tile_t, tile_t, tile_t, tile, tile, tile,
                   pl.BlockSpec((1, D), const),
                   pl.BlockSpec((1, 2 * D), const),
                   pl.BlockSpec((1, GLA_HV), const),
                   pl.BlockSpec((1, 128), const)),
        out_shape=(jax.ShapeDtypeStruct((T, D), F32),
                   jax.ShapeDtypeStruct((T, D), F32),
                   jax.ShapeDtypeStruct((T, D), F32),
                   jax.ShapeDtypeStruct((T, D), BF16),
                   jax.ShapeDtypeStruct((T, D), BF16),
                   jax.ShapeDtypeStruct((T, 2 * D), BF16),
                   jax.ShapeDtypeStruct((T // tm, D, tm), BF16),
                   jax.ShapeDtypeStruct((T // tm, D, tm), BF16),
                   jax.ShapeDtypeStruct((T // tm, D, tm), BF16),
                   jax.ShapeDtypeStruct((T, D), BF16),
                   jax.ShapeDtypeStruct((T, D), BF16),
                   jax.ShapeDtypeStruct((T, D), BF16),
                   jax.ShapeDtypeStruct((1, D), F32),
                   jax.ShapeDtypeStruct((1, 2 * D), F32),
                   jax.ShapeDtypeStruct((1, GLA_HV), F32),
                   jax.ShapeDtypeStruct((1, 128), F32)),
        compiler_params=_cparams(("arbitrary",)),
    )(o_gla, projf, o_sb, projf, projf, projf, x, target, wpa, wpb, wo, gla_g, b_gate, final_g)


def _dh_call(pieces, dmlog, drank, wt, wr, x, dx2, norm_g, s_in, small):
    T, D = x.shape
    tm = min(256, T)
    npc = len(pieces)
    n_main = N_GROUPS * 1024
    n_i = T // tm
    i_forward = 5 * n_i // 8

    def body(*refs):
        pcs = refs[:npc]
        (dm_ref, dr_ref, w_hbm, wr_ref, x_ref, dx2_ref, g_ref, sin_ref, small_ref,
         gx_ref, rin_ref, relayed_ref, rsmall_ref,
         w_scr, sems, dg_ref, small_mine, small_send, small_recv, small_loc, *exchange_scratch) = refs[npc:]
        start, forward, finish = _chip_reduce_steps(sin_ref, rin_ref, relayed_ref, *exchange_scratch)

        @pl.when(pl.program_id(0) == 0)
        def _():
            start()
            lo = pltpu.make_async_copy(w_hbm.at[pl.ds(0, RANK_COL)], w_scr.at[pl.ds(0, RANK_COL)], sems.at[0])
            hi = pltpu.make_async_copy(w_hbm.at[pl.ds(RANK_COL + GLA_RANK, n_main - RANK_COL)],
                                       w_scr.at[pl.ds(RANK_COL, n_main - RANK_COL)], sems.at[1])
            lo.start()
            hi.start()
            dg_ref[...] = jnp.zeros_like(dg_ref)
            lo.wait()
            hi.wait()

        @pl.when(pl.program_id(0) == i_forward)
        def _():
            forward()

        def w_group(g):
            return w_scr[g * 1024:(g + 1) * 1024, :]

        dr = dr_ref[...]
        dh = _dot(dr, wr_ref[...])
        for g in range(npc):
            dh = dh + _dot(pcs[g][...], w_group(g))
        dh = dh + _dot(dm_ref[:, :D], w_group(npc))
        dh = dh + _dot(dm_ref[:, D:], w_group(npc + 1))
        xv = x_ref[...]
        r = lax.rsqrt(jnp.mean(xv * xv, axis=-1, keepdims=True) + EPS)
        xhat = xv * r
        g = g_ref[...]
        dg_ref[...] += jnp.sum(dh * xhat, axis=0, keepdims=True)
        dxhat = dh * g
        gx_ref[...] = r * (dxhat - xhat * jnp.mean(dxhat * xhat, axis=-1, keepdims=True)) + dx2_ref[...]

        @pl.when(pl.program_id(0) == n_i - 1)
        def _():
            small_mine[...] = small_ref[...]
            small_mine[:, _SM_NORM:_SM_NORM + D] = dg_ref[...]
            own, pairs = _push_copies(small_mine, rsmall_ref, small_send, small_recv, small_loc, scatter=False)
            _push_start(own, pairs)
            finish()
            _push_wait(own, pairs)

    row = lambda i: (i, 0)
    const = lambda i: (0, 0)
    tile = pl.BlockSpec((tm, D), row)
    part = s_in.shape[1:]
    return pl.pallas_call(
        body, name="dh",
        grid=(n_i,),
        in_specs=[tile] * npc + [
            pl.BlockSpec((tm, 2 * D), row),
            pl.BlockSpec((tm, 128), row),
            _ANY,
            pl.BlockSpec((128, D), const),
            tile, tile,
            pl.BlockSpec((1, D), const),
            _ANY,
            pl.BlockSpec(small.shape, const)],
        out_specs=(tile, _ANY, _ANY, _ANY),
        out_shape=(jax.ShapeDtypeStruct((T, D), F32),
                   jax.ShapeDtypeStruct((3,) + part, s_in.dtype),
                   jax.ShapeDtypeStruct(part, s_in.dtype),
                   jax.ShapeDtypeStruct((N_DEV,) + small.shape, small.dtype)),
        scratch_shapes=[pltpu.VMEM((n_main, D), BF16), pltpu.SemaphoreType.DMA((2,)),
                        pltpu.VMEM((1, D), F32), pltpu.VMEM(small.shape, small.dtype)]
        + _PUSH_SEMS + _chip_reduce_scratch(*part, s_in.dtype),
        input_output_aliases={npc + 5: 0},
        compiler_params=_cparams(("arbitrary",)),
    )(*pieces, dmlog, drank, wt, wr, x, dx2, norm_g, s_in, small)


def _wgrad_call(lhs_list, lhs_of_group, rhs_list, rhs_of_group, n_transposed, name, narrow=None):
    n_groups = len(rhs_of_group)
    n_tb, D, tb = lhs_list[0].shape
    T = n_tb * tb
    per = min(4, n_tb)
    tk = per * tb
    nk = T // tk
    nl = len(lhs_list)
    extra = [] if narrow is None else [narrow]

    def tokens_side_by_side(lref):
        return jnp.concatenate([lref[b] for b in range(per)], axis=1)

    def body(*refs):
        lhs = refs[:nl]
        rhs = refs[nl:nl + n_groups]
        rest = refs[nl + n_groups:]
        g = pl.program_id(0)
        i = pl.program_id(1)
        if narrow is None:
            out_ref, acc = rest
        else:
            narrow_ref, out_ref, narrow_out, acc, narrow_acc = rest

            @pl.when((g == 0) & (i == 0))
            def _():
                narrow_acc[...] = jnp.zeros_like(narrow_acc)

            @pl.when(g == 0)
            def _():
                narrow_acc[...] += _dot(tokens_side_by_side(lhs[lhs_of_group[0]]), narrow_ref[...])

            @pl.when((g == 0) & (i == nk - 1))
            def _():
                narrow_out[...] = _bf(narrow_acc[...].T)

        @pl.when(i == 0)
        def _():
            acc[...] = jnp.zeros_like(acc)

        for p in range(n_groups):
            @pl.when(g == p)
            def _(p=p):
                acc[...] += _dot(tokens_side_by_side(lhs[lhs_of_group[p]]), rhs[p][...])

        @pl.when((i == nk - 1) & (g < n_transposed))
        def _():
            out_ref[...] = _bf(acc[...].T)

        @pl.when((i == nk - 1) & (g >= n_transposed))
        def _():
            out_ref[...] = _bf(acc[...])

    def lhs_spec(a):
        groups = [g for g in range(n_groups) if lhs_of_group[g] == a]
        lo, hi = min(groups), max(groups)
        assert groups == list(range(lo, hi + 1))
        return pl.BlockSpec((per, D, tb), lambda g, i: (jnp.where((g >= lo) & (g <= hi), i, 0), 0, 0))

    def rhs_spec(p):
        cb = rhs_of_group[p][1]
        return pl.BlockSpec((tk, 1024), lambda g, i: (jnp.where(g == p, i, 0), cb))

    res = pl.pallas_call(
        body, name=name,
        grid=(n_groups, nk),
        in_specs=[lhs_spec(a) for a in range(nl)] + [rhs_spec(p) for p in range(n_groups)]
        + [pl.BlockSpec((tk, 128), lambda g, i: (jnp.where(g == 0, i, 0), 0)) for _ in extra],
        out_specs=[pl.BlockSpec((None, D, 1024), lambda g, i: (g, 0, 0))]
        + [pl.BlockSpec((128, D), lambda g, i: (0, 0)) for _ in extra],
        out_shape=[jax.ShapeDtypeStruct((n_groups, D, 1024), BF16)]
        + [jax.ShapeDtypeStruct((128, D), BF16) for _ in extra],
        scratch_shapes=[pltpu.VMEM((D, 1024), F32)] + [pltpu.VMEM((D, 128), F32) for _ in extra],
        compiler_params=_cparams(("arbitrary", "arbitrary")),
    )(*lhs_list, *[rhs_list[rhs_of_group[p][0]] for p in range(n_groups)], *extra)
    return res[0] if narrow is None else res


def _adamw_math(parts, w, m, v):
    g = parts[0].astype(F32)
    for p in parts[1:]:
        g = g + p.astype(F32)
    mm = ADAM_B1 * m + (1.0 - ADAM_B1) * g
    vv = ADAM_B2 * v + (1.0 - ADAM_B2) * (g * g)
    m_hat = mm / (1.0 - ADAM_B1 ** ADAM_STEP)
    v_hat = vv / (1.0 - ADAM_B2 ** ADAM_STEP)
    return g, -ADAM_LR * (m_hat / (jnp.sqrt(v_hat) + ADAM_EPS) + ADAM_WD * w), mm, vv


def _part_order(n_parts):
    return [n_parts - 1] + list(range(n_parts - 1))


def _adamw_call(parts, w, m, v, name):
    R, C = w.shape
    n_parts = parts.shape[0]
    (tr, tc), grid, idx = _tiling_2d(R, C, 512)

    def body(p_ref, w_ref, m_ref, v_ref, g_ref, d_ref, nm_ref, nv_ref):
        g_ref[...], d_ref[...], nm_ref[...], nv_ref[...] = _adamw_math(
            [p_ref[k] for k in _part_order(n_parts)], w_ref[...], m_ref[...], v_ref[...])

    blk = pl.BlockSpec((tr, tc), idx)
    sds = jax.ShapeDtypeStruct((R, C), F32)
    return pl.pallas_call(
        body, name=name,
        grid=grid,
        in_specs=[pl.BlockSpec((n_parts, tr, tc), lambda i: (0,) + idx(i)), blk, blk, blk],
        out_specs=(blk, blk, blk, blk),
        out_shape=(sds, sds, sds, sds),
        compiler_params=_cparams(("arbitrary",)),
    )(parts, w, m, v)


def _adamw_rows_call(parts, ws, ms, vs, name):
    n = len(ws)
    R, C = ws[0].shape
    n_parts = parts.shape[0]

    def body(*refs):
        p_ref = refs[0]
        w_refs, m_refs, v_refs = refs[1:1 + n], refs[1 + n:1 + 2 * n], refs[1 + 2 * n:1 + 3 * n]
        outs = refs[1 + 3 * n:]
        for k in range(n):
            @pl.when(pl.program_id(0) == k)
            def _(k=k):
                res = _adamw_math([p_ref[j] for j in _part_order(n_parts)],
                                  w_refs[k][...], m_refs[k][...], v_refs[k][...])
                for o_ref, val in zip(outs[4 * k:4 * k + 4], res):
                    o_ref[...] = val

    whole = pl.BlockSpec((R, C), lambda k: (0, 0))
    sds = jax.ShapeDtypeStruct((R, C), F32)
    res = pl.pallas_call(
        body, name=name,
        grid=(n,),
        in_specs=[pl.BlockSpec((n_parts, R, C), lambda k: (0, k, 0))] + [whole] * (3 * n),
        out_specs=tuple([whole] * (4 * n)),
        out_shape=tuple([sds] * (4 * n)),
        compiler_params=_cparams(("arbitrary",)),
    )(parts, *ws, *ms, *vs)
    return [res[4 * k:4 * k + 4] for k in range(n)]


def _adamw_lanes_call(parts, offsets, ws, ms, vs, name):
    n = len(ws)
    n_parts = parts.shape[0]

    def body(*refs):
        p_ref = refs[0]
        w_refs, m_refs, v_refs = refs[1:1 + n], refs[1 + n:1 + 2 * n], refs[1 + 2 * n:1 + 3 * n]
        outs = refs[1 + 3 * n:]
        for k in range(n):
            lanes = slice(offsets[k], offsets[k] + ws[k].shape[1])
            res = _adamw_math([p_ref[j, :, lanes] for j in _part_order(n_parts)],
                              w_refs[k][...], m_refs[k][...], v_refs[k][...])
            for o_ref, val in zip(outs[4 * k:4 * k + 4], res):
                o_ref[...] = val

    res = pl.pallas_call(
        body, name=name,
        out_shape=tuple(jax.ShapeDtypeStruct(ws[k].shape, F32) for k in range(n) for _ in range(4)),
        compiler_params=_cparams(),
    )(parts, *ws, *ms, *vs)
    return [res[4 * k:4 * k + 4] for k in range(n)]


def _local_step(x, target, wt, wr, wdec, bdec, wp_shard, norm_g, gla_g, b_gate, final_g):
    D = x.shape[1]
    half = wp_shard.shape[1] // 2
    projf, projb, rank, ht, wp_lo = _proj_call(x, norm_g, wt, wr, wp_shard[:, :half])
    o_gla, st_all, la = _gla_fwd_call(projf, projb, rank, wdec, bdec)
    o_sb, wp_hi = _sb_fwd_call(projb, wp_shard[:, half:])
    wp_full = jnp.concatenate([wp_lo, wp_hi], axis=2).transpose(1, 0, 2, 3).reshape(3, D, D)
    (dx2, do_gla, do_sb, dggate, dsgate, dmlog, mt, ogt, obt, dx2b, dya, dyb,
     dfinal_g, db_gate, dgla_g, loss) = _mid_call(o_gla, o_sb, projf, x, target, wp_full[0], wp_full[1],
                                                 wp_full[2], gla_g, b_gate, final_g)
    dw_p = _wgrad_call([ogt, obt, mt], [0, 1, 2], [dya, dyb, dx2b], [(0, 0), (1, 0), (2, 0)], 0, "wgrad_p")
    g_p = dw_p.reshape(3, N_DEV, D // N_DEV, D).transpose(1, 0, 2, 3).reshape(N_DEV, 3 * (D // N_DEV), D)
    dqk, dgv, drank, dwdec, dbdec = _gla_bwd_call(projf, projb, la, do_gla, st_all, rank, wdec)
    dsq, dsk, dsv, r_p = _sb_bwd_call(projb, do_sb, g_p)
    pieces = [dqk, dgv, dggate, dsq, dsk, dsv, dsgate]
    rhs_of_group = [(g, 0) for g in range(7)] + [(7, 0), (7, 1)]
    dw_in, dwr = _wgrad_call([ht], [0] * N_GROUPS, pieces + [dmlog], rhs_of_group, N_GROUPS, "wgrad_in",
                             narrow=drank)
    s_in = _pair_sum_call(dw_in.reshape(N_GROUPS * 1024, D), dwr)
    small = jnp.concatenate([
        jnp.zeros((D,), F32), dbdec.reshape(-1), dgla_g.reshape(-1), db_gate.reshape(-1), dfinal_g.reshape(-1),
        loss.reshape(-1), dwdec[:GLA_RANK].reshape(-1)]).reshape(1, _SM_LEN)
    grad_x, r_in, _, r_small = _dh_call(pieces, dmlog, drank, wt, wr, x, dx2, norm_g, s_in, small)
    return grad_x, r_in, r_p, r_small


_SM_NORM = 0
_SM_BDEC = _SM_NORM + D_MODEL
_SM_GLAG = _SM_BDEC + GLA_DK
_SM_BGATE = _SM_GLAG + GLA_HV
_SM_FINAL = _SM_BGATE + 2 * D_MODEL
_SM_REPL = _SM_FINAL + D_MODEL
_SM_LOSS = _SM_REPL
_SM_WDEC = _SM_LOSS + 128
_SM_LEN = _SM_WDEC + GLA_RANK * GLA_DK


def kernel(x, norm_g, w_in, w_dec_up, b_dec, gla_norm_g, w_pa, w_pb, b_gate, w_o, final_g, loss_target, m_norm_g, m_w_in, m_w_dec_up, m_b_dec, m_gla_norm_g, m_w_pa, m_w_pb, m_b_gate, m_w_o, m_final_g, v_norm_g, v_w_in, v_w_dec_up, v_b_dec, v_gla_norm_g, v_w_pa, v_w_pb, v_b_gate, v_w_o, v_final_g):
    D = D_MODEL
    me = 4 * lax.axis_index("x") + 2 * lax.axis_index("y") + lax.axis_index("c")

    wp_shard = jnp.stack([w_pa, w_pb, w_o]).astype(BF16)
    n_first = _half_rows(SHARD_COLS)
    win_all, wdec_all = _all_gather([w_in.T.astype(BF16), w_dec_up], "gather_w",
                                    row_pieces=[[(0, n_first), (n_first, SHARD_COLS - n_first)], None])
    wt = _flatten_blocks_call(win_all)
    wr = jnp.pad(wt[RANK_COL:RANK_COL + GLA_RANK], ((0, 128 - GLA_RANK), (0, 0)))
    wdec_full = wdec_all.transpose(1, 0, 2).reshape(GLA_RANK, GLA_DK)
    wdec = jnp.pad(wdec_full, ((0, 128 - GLA_RANK), (0, 0)))

    grad_x, r_in, r_p, r_small = _local_step(
        x[0], loss_target[0], wt, wr, wdec, b_dec.reshape(1, -1), wp_shard,
        norm_g.reshape(1, -1), gla_norm_g.reshape(1, -1), b_gate.reshape(1, -1), final_g.reshape(1, -1))

    gw_in, d_in, nm_in, nv_in = (a.T for a in _adamw_call(r_in, w_in.T, m_w_in.T, v_w_in.T, "adamw_in"))
    (g_pa, d_pa, nm_pa, nv_pa), (g_pb, d_pb, nm_pb, nv_pb), (g_o, d_o, nm_o, nv_o) = _adamw_rows_call(
        r_p, [w_pa, w_pb, w_o], [m_w_pa, m_w_pb, m_w_o], [v_w_pa, v_w_pb, v_w_o], "adamw_p")

    def row(a):
        return a.reshape(1, -1)

    rep = _adamw_lanes_call(
        r_small, [_SM_NORM, _SM_BDEC, _SM_GLAG, _SM_BGATE, _SM_FINAL],
        [row(a) for a in (norm_g, b_dec, gla_norm_g, b_gate, final_g)],
        [row(a) for a in (m_norm_g, m_b_dec, m_gla_norm_g, m_b_gate, m_final_g)],
        [row(a) for a in (v_norm_g, v_b_dec, v_gla_norm_g, v_b_gate, v_final_g)], "adamw_rep")
    ((g_norm, d_norm, nm_norm, nv_norm), (g_bdec, d_bdec, nm_bdec, nv_bdec), (g_glag, d_glag, nm_glag, nv_glag),
     (g_bgate, d_bgate, nm_bgate, nv_bgate), (g_final, d_final, nm_final, nv_final)) = [
        tuple(a.reshape(-1) for a in quad) for quad in rep]

    wdec_parts = r_small[:, 0, _SM_WDEC:].reshape(N_DEV, GLA_RANK, GLA_DK)
    cols = GLA_DK // N_DEV
    wdec_mine = lax.dynamic_slice_in_dim(wdec_parts, me * cols, cols, axis=2)
    g_wdec, d_wdec, nm_wdec, nv_wdec = _adamw_call(wdec_mine, w_dec_up, m_w_dec_up, v_w_dec_up, "adamw_dec")

    loss_total = jnp.sum(r_small[:, 0, _SM_LOSS])

    return (loss_total, grad_x[None],
            g_norm, gw_in, g_wdec, g_bdec, g_glag, g_pa, g_pb, g_bgate, g_o, g_final,
            d_norm, d_in, d_wdec, d_bdec, d_glag, d_pa, d_pb, d_bgate, d_o, d_final,
            nm_norm, nm_in, nm_wdec, nm_bdec, nm_glag, nm_pa, nm_pb, nm_bgate, nm_o, nm_final,
            nv_norm, nv_in, nv_wdec, nv_bdec, nv_glag, nv_pa, nv_pb, nv_bgate, nv_o, nv_final)
```

```python
import math

import jax
import jax.numpy as jnp
from jax import lax
from jax.experimental import pallas as pl
from jax.experimental.pallas import tpu as pltpu

F32 = jnp.float32
BF16 = jnp.bfloat16

N_DEV = 8
D_MODEL = 1024
GLA_HEADS = 4
GLA_HK = 128
GLA_HV = 256
GLA_DK = 512
GLA_RANK = 16
GLA_TAU = 16.0
GLA_CHUNK = 64
SB_HEADS = 8
SB_HD = 128
EPS = 1e-6
N_GROUPS = 9
RANK_COL = 3072
IN_COLS = 9232
SHARD_COLS = IN_COLS // N_DEV

ADAM_LR = 0.001
ADAM_B1 = 0.9
ADAM_B2 = 0.999
ADAM_EPS = 1e-08
ADAM_WD = 0.01
ADAM_STEP = 10

VMEM_LIMIT = 56 * 1024 * 1024
TBLK = 256


def _cparams(sem=None):
    return pltpu.CompilerParams(dimension_semantics=sem, vmem_limit_bytes=VMEM_LIMIT)


def _tiling_2d(rows, cols, band_cols):
    if rows * cols <= 128 * 1024:
        return (rows, cols), (1,), lambda i: (0, 0)
    if rows % 128 == 0:
        return (128, cols), (rows // 128,), lambda i: (i, 0)
    tc = band_cols if cols % band_cols == 0 else cols
    return (rows, tc), (cols // tc,), lambda i: (0, i)


def _dot(a, b):
    return jnp.dot(a, b, preferred_element_type=F32)


def _dot_nt(a, b):
    return lax.dot_general(a, b, (((1,), (1,)), ((), ())), preferred_element_type=F32)


def _dot_tn(a, b):
    return lax.dot_general(a, b, (((0,), (0,)), ((), ())), preferred_element_type=F32)


def _bf(x):
    return x.astype(BF16)


def _split3(x):
    hi = x.astype(BF16)
    r = x - hi.astype(F32)
    mid = r.astype(BF16)
    lo = (r - mid.astype(F32)).astype(BF16)
    return hi, mid, lo


def _tri_left(tri, x):
    hi, mid, lo = _split3(x)
    return _dot(tri, hi) + _dot(tri, mid) + _dot(tri, lo)


def _split2(x):
    hi = lax.bitcast_convert_type(lax.bitcast_convert_type(x, jnp.uint32) & jnp.uint32(0xFFFF0000), F32)
    return hi.astype(BF16), (x - hi).astype(BF16)


def _tri2_left(tri, x):
    hi, lo = _split2(x)
    return _dot(tri, hi) + _dot(tri, lo)


def _tri2_right(x, tri):
    hi, lo = _split2(x)
    return _dot(hi, tri) + _dot(lo, tri)


def _iota2(n, m, dim):
    return lax.broadcasted_iota(jnp.int32, (n, m), dim)


def _sigmoid(x):
    return 1.0 / (1.0 + jnp.exp(-x))


def _softplus_neg_abs(z):
    return jnp.log(1.0 + jnp.exp(-jnp.abs(z)))


_ANY = pl.BlockSpec(memory_space=pl.ANY)


def _mesh_pos():
    return lax.axis_index("x"), lax.axis_index("y"), lax.axis_index("c")


def _other_chips(x, y):
    return [(1 - x, y), (x, 1 - y), (1 - x, 1 - y)]


def _rcopy(src, dst, send_sem, recv_sem, to):
    return pltpu.make_async_remote_copy(src_ref=src, dst_ref=dst, send_sem=send_sem, recv_sem=recv_sem,
                                        device_id=to, device_id_type=pl.DeviceIdType.MESH)


def _push_copies(src_ref, dst_ref, send_sems, recv_sems, loc_sem, scatter):
    x, y, c = _mesh_pos()
    me = 4 * x + 2 * y + c
    own = pltpu.make_async_copy(src_ref.at[me] if scatter else src_ref, dst_ref.at[me], loc_sem)
    pairs = []
    for k in range(1, N_DEV):
        px = 1 - x if k & 4 else x
        py = 1 - y if k & 2 else y
        pc = 1 - c if k & 1 else c
        pid = 4 * px + 2 * py + pc
        src = src_ref.at[pid] if scatter else src_ref
        send = _rcopy(src, dst_ref.at[me], send_sems.at[k - 1], recv_sems.at[k - 1], (px, py, pc))
        recv = _rcopy(src, dst_ref.at[pid], send_sems.at[k - 1], recv_sems.at[k - 1], (px, py, pc))
        pairs.append((send, recv))
    return own, pairs


def _push_start(own, pairs):
    own.start()
    for send, _ in pairs:
        send.start()


def _push_wait(own, pairs):
    for _, recv in pairs:
        recv.wait_recv()
    for send, _ in pairs:
        send.wait_send()
    own.wait()


_PUSH_SEMS = [pltpu.SemaphoreType.DMA((N_DEV - 1,)), pltpu.SemaphoreType.DMA((N_DEV - 1,)),
              pltpu.SemaphoreType.DMA]


def _half_rows(rows):
    return (rows // 2) // 16 * 16


_ADD_ROWS = 128


def _chip_reduce_steps(src_ref, dst_ref, relayed_ref, sum_x, sum_y, rel_x, rel_y, load_sems, send_sems, recv_sems,
                       loc_sem):
    _, R, C = src_ref.shape
    n0 = _half_rows(R)
    lo, hi = pl.ds(0, n0), pl.ds(n0, R - n0)
    x, y, c = _mesh_pos()
    (xx, xy), (yx, yy), (dx, dy) = _other_chips(x, y)
    to_diag, to_x, to_y = src_ref.at[2 * dx + dy], src_ref.at[2 * xx + xy], src_ref.at[2 * yx + yy]
    x_nb, y_nb = (xx, xy, c), (yx, yy, c)
    relays = (_rcopy(to_diag.at[lo], relayed_ref.at[lo], send_sems.at[0], recv_sems.at[0], x_nb),
              _rcopy(to_diag.at[hi], relayed_ref.at[hi], send_sems.at[1], recv_sems.at[1], y_nb))
    plain = (_rcopy(to_x.at[lo], dst_ref.at[0, lo], send_sems.at[2], recv_sems.at[2], x_nb),
             _rcopy(to_y.at[hi], dst_ref.at[1, hi], send_sems.at[3], recv_sems.at[3], y_nb))
    summed = (_rcopy(sum_x, dst_ref.at[0, hi], send_sems.at[4], recv_sems.at[4], x_nb),
              _rcopy(sum_y, dst_ref.at[1, lo], send_sems.at[5], recv_sems.at[5], y_nb))
    load_mine = (pltpu.make_async_copy(to_x.at[hi], sum_x, load_sems.at[0]),
                 pltpu.make_async_copy(to_y.at[lo], sum_y, load_sems.at[1]))
    load_relayed = (pltpu.make_async_copy(relayed_ref.at[hi], rel_x, load_sems.at[2]),
                    pltpu.make_async_copy(relayed_ref.at[lo], rel_y, load_sems.at[3]))
    own = pltpu.make_async_copy(src_ref.at[2 * x + y], dst_ref.at[2], loc_sem)

    def start():
        for cp in relays + plain + (own,) + load_mine:
            cp.start()

    def add(acc_ref, rel_ref):
        for r0 in range(0, acc_ref.shape[0], _ADD_ROWS):
            rows = slice(r0, min(r0 + _ADD_ROWS, acc_ref.shape[0]))
            acc_ref[rows, :] = (acc_ref[rows, :].astype(F32) + rel_ref[rows, :].astype(F32)).astype(acc_ref.dtype)

    def forward():
        for cp in relays:
            cp.wait_recv()
        for cp in load_relayed:
            cp.start()
        for cp in load_mine + load_relayed:
            cp.wait()
        add(sum_x, rel_x)
        add(sum_y, rel_y)
        for cp in summed:
            cp.start()

    def finish():
        for cp in plain + summed:
            cp.wait_recv()
        for cp in relays + plain + summed:
            cp.wait_send()
        own.wait()

    return start, forward, finish


def _chip_reduce_scratch(rows, cols, dtype):
    n0 = _half_rows(rows)
    return [pltpu.VMEM((rows - n0, cols), dtype), pltpu.VMEM((n0, cols), dtype)] * 2 + [
        pltpu.SemaphoreType.DMA((4,)), pltpu.SemaphoreType.DMA((6,)), pltpu.SemaphoreType.DMA((6,)),
        pltpu.SemaphoreType.DMA]


def _all_gather(arrs, name, row_pieces=None):
    n = len(arrs)
    pieces = [[None] if not row_pieces or not row_pieces[a] else list(row_pieces[a]) for a in range(n)]
    assert all(len(p) in (1, 2) for p in pieces)
    units = [(a, i) for a in range(n) for i in range(len(pieces[a]))]

    def body(*refs):
        ins = refs[:n]
        outs = refs[n:2 * n]
        send_sems, recv_sems, loc_sems = refs[2 * n:]
        x, y, c = _mesh_pos()
        me, sib = (x, y, c), (x, y, 1 - c)
        xn, yn, dg = [(px, py, c) for px, py in _other_chips(x, y)]

        def rows(ref, a, i):
            return ref if pieces[a][i] is None else ref.at[pl.ds(*pieces[a][i])]

        def copy(u, k, block, to, own=False):
            a, i = u
            px, py, pc = block
            dst = rows(outs[a].at[4 * px + 2 * py + pc], a, i)
            return _rcopy(rows(ins[a], a, i) if own else dst, dst, send_sems.at[a, k, i], recv_sems.at[a, k, i], to)

        started = []

        def start(cp):
            cp.start()
            started.append(cp)

        def landed_then_pass_on(u, k, block):
            copy(u, k, block, me).wait_recv()
            start(copy(u, 3 + k, block, sib))

        mine = [pltpu.make_async_copy(ins[a], outs[a].at[4 * x + 2 * y + c], loc_sems.at[a]) for a in range(n)]
        for cp in mine:
            cp.start()
        for u in units:
            start(copy(u, 0, me, sib, own=True))
        for a in range(n):
            if len(pieces[a]) == 2:
                for i, to, k in ((0, xn, 1), (1, yn, 2), (1, xn, 1), (0, yn, 2)):
                    start(copy((a, i), k, me, to, own=True))
            else:
                for to, k in ((xn, 1), (yn, 2), (dg, 3)):
                    start(copy((a, 0), k, me, to, own=True))
        for a in range(n):
            if len(pieces[a]) == 2:
                landed_then_pass_on((a, 0), 1, xn)
                start(copy((a, 0), 3, xn, yn))
                landed_then_pass_on((a, 1), 2, yn)
                start(copy((a, 1), 3, yn, xn))
                landed_then_pass_on((a, 1), 1, xn)
                landed_then_pass_on((a, 0), 2, yn)
                landed_then_pass_on((a, 0), 3, dg)
                landed_then_pass_on((a, 1), 3, dg)
            else:
                for block, k in ((xn, 1), (yn, 2), (dg, 3)):
                    landed_then_pass_on((a, 0), k, block)
        for u in units:
            copy(u, 0, sib, me).wait_recv()
            for k, (px, py, _) in ((4, xn), (5, yn), (6, dg)):
                copy(u, k, (px, py, 1 - c), me).wait_recv()
        for cp in started:
            cp.wait_send()
        for cp in mine:
            cp.wait()

    n_pc = max(len(p) for p in pieces)

    return pl.pallas_call(
        body, name=name,
        out_shape=tuple(jax.ShapeDtypeStruct((N_DEV,) + a.shape, a.dtype) for a in arrs),
        in_specs=[_ANY] * n,
        out_specs=tuple([_ANY] * n),
        scratch_shapes=[pltpu.SemaphoreType.DMA((n, 7, n_pc)), pltpu.SemaphoreType.DMA((n, 7, n_pc)),
                        pltpu.SemaphoreType.DMA((n,))],
    )(*arrs)


def _flatten_blocks_call(blocks):
    n, R, C = blocks.shape
    tc = C // 2

    def body(in_ref, out_ref):
        for p in range(n):
            out_ref[p * R:(p + 1) * R, :] = in_ref[p]

    return pl.pallas_call(
        body, name="flatten_w",
        grid=(C // tc,),
        in_specs=[pl.BlockSpec((n, R, tc), lambda i: (0, 0, i))],
        out_specs=pl.BlockSpec((n * R, tc), lambda i: (0, i)),
        out_shape=jax.ShapeDtypeStruct((n * R, C), blocks.dtype),
        compiler_params=_cparams(("arbitrary",)),
    )(blocks)


_PARTS_BANDS = 4


def _pair_sum_call(dmain, drank):
    D = dmain.shape[1]
    n = _PARTS_BANDS
    tc = D // n

    def body(dm_ref, dr_ref, sum_ref, laid, got, send_sems, recv_sems):
        x, y, c = _mesh_pos()

        def pushes(k):
            return [_rcopy(laid.at[k % 2, 2 * q + (1 - c)], got.at[k, q], send_sems.at[k, q], recv_sems.at[k, q],
                           (x, y, 1 - c)) for q in range(4)]

        def lay_out(k):
            for p in range(N_DEV):
                lo, hi = p * SHARD_COLS, (p + 1) * SHARD_COLS
                at = 0
                for src, a, b in ((dm_ref, lo, min(hi, RANK_COL)),
                                  (dr_ref, max(lo, RANK_COL) - RANK_COL, min(hi, RANK_COL + GLA_RANK) - RANK_COL),
                                  (dm_ref, max(lo, RANK_COL + GLA_RANK) - GLA_RANK, hi - GLA_RANK)):
                    if b > a:
                        laid[k % 2, p, at:at + (b - a), :] = src[a:b, :]
                        at += b - a

        for k in range(n + 1):
            @pl.when(pl.program_id(0) == k)
            def _(k=k):
                if k < n:
                    if k >= 2:
                        for cp in pushes(k - 2):
                            cp.wait_send()
                    lay_out(k)
                    for cp in pushes(k):
                        cp.start()
                if k >= 1:
                    for cp in pushes(k - 1):
                        cp.wait_recv()
                    for q in range(4):
                        sum_ref[q] = (laid[(k - 1) % 2, 2 * q + c].astype(F32)
                                      + got[k - 1, q].astype(F32)).astype(sum_ref.dtype)
                if k == n:
                    for k_open in range(max(0, n - 2), n):
                        for cp in pushes(k_open):
                            cp.wait_send()

    sems = pltpu.SemaphoreType.DMA((n, 4))
    return pl.pallas_call(
        body, name="pair_sum",
        grid=(n + 1,),
        in_specs=[pl.BlockSpec((dmain.shape[0], tc), lambda k: (0, jnp.minimum(k, n - 1))),
                  pl.BlockSpec((GLA_RANK, tc), lambda k: (0, jnp.minimum(k, n - 1)))],
        out_specs=pl.BlockSpec((4, SHARD_COLS, tc), lambda k: (0, 0, jnp.maximum(k - 1, 0))),
        out_shape=jax.ShapeDtypeStruct((4, SHARD_COLS, D), dmain.dtype),
        scratch_shapes=[pltpu.VMEM((2, N_DEV, SHARD_COLS, tc), dmain.dtype),
                        pltpu.VMEM((n, 4, SHARD_COLS, tc), dmain.dtype), sems, sems],
        compiler_params=_cparams(("arbitrary",)),
    )(dmain, drank)


def _group_row(g):
    return GLA_RANK * (g * (1024 // GLA_RANK) + (g >= RANK_COL // 1024))


def _proj_call(x, norm_g, wt, wr, wp_part):
    T, D = x.shape
    tm = min(1024, T)
    assert tm % TBLK == 0
    n_i = T // tm

    def f_slot(j):
        return ((j >= 2).astype(jnp.int32) + (j >= 6).astype(jnp.int32)
                + (j >= 7).astype(jnp.int32) + (j >= 8).astype(jnp.int32))

    def b_slot(j):
        return (j >= 3).astype(jnp.int32) + (j >= 4).astype(jnp.int32) + (j >= 5).astype(jnp.int32)

    def body(x_ref, g_ref, w_ref, wr_ref, wp_ref, pf_ref, pb_ref, rank_ref, ht_ref, wpall_ref,
             h_scr, send_sems, recv_sems, loc_sem):
        i = pl.program_id(0)
        j = pl.program_id(1)
        own, pairs = _push_copies(wp_ref, wpall_ref, send_sems, recv_sems, loc_sem, scatter=False)

        @pl.when((i == 0) & (j == 0))
        def _():
            _push_start(own, pairs)

        @pl.when(j == 0)
        def _():
            xv = x_ref[...]
            r = lax.rsqrt(jnp.mean(xv * xv, axis=-1, keepdims=True) + EPS)
            h = (xv * r) * g_ref[...]
            hb = _bf(h)
            h_scr[...] = hb
            for b in range(tm // TBLK):
                ht_ref[b] = _bf(h[b * TBLK:(b + 1) * TBLK].T)
            rank_ref[...] = _dot_nt(hb, wr_ref[...])

        is_b = (j == 1) | ((j >= 3) & (j <= 5))

        @pl.when(is_b)
        def _():
            pb_ref[...] = _bf(_dot_nt(h_scr[...], w_ref[...]))

        @pl.when(jnp.logical_not(is_b))
        def _():
            pf_ref[...] = _dot_nt(h_scr[...], w_ref[...])

        @pl.when((i == n_i - 1) & (j == N_GROUPS - 1))
        def _():
            _push_wait(own, pairs)

    return pl.pallas_call(
        body, name="proj",
        grid=(n_i, N_GROUPS),
        in_specs=[pl.BlockSpec((tm, D), lambda i, j: (i, 0)),
                  pl.BlockSpec((1, D), lambda i, j: (0, 0)),
                  pl.BlockSpec((pl.Element(1024), pl.Element(D)), lambda i, j: (_group_row(j), 0)),
                  pl.BlockSpec((128, D), lambda i, j: (0, 0)),
                  _ANY],
        out_specs=(pl.BlockSpec((None, tm, 1024), lambda i, j: (f_slot(j), i, 0)),
                   pl.BlockSpec((None, tm, 1024), lambda i, j: (b_slot(j), i, 0)),
                   pl.BlockSpec((tm, 128), lambda i, j: (i, 0)),
                   pl.BlockSpec((tm // TBLK, D, TBLK), lambda i, j: (i, 0, 0)),
                   _ANY),
        out_shape=(jax.ShapeDtypeStruct((5, T, 1024), F32),
                   jax.ShapeDtypeStruct((4, T, 1024), BF16),
                   jax.ShapeDtypeStruct((T, 128), F32),
                   jax.ShapeDtypeStruct((T // TBLK, D, TBLK), BF16),
                   jax.ShapeDtypeStruct((N_DEV,) + wp_part.shape, wp_part.dtype)),
        scratch_shapes=[pltpu.VMEM((tm, D), BF16)] + _PUSH_SEMS,
        compiler_params=_cparams(("arbitrary", "arbitrary")),
    )(x, norm_g, wt, wr, wp_part)


GLA_STEP_CHUNKS = 4


def _gla_same_chunk(rows):
    return (_iota2(rows, rows, 0) & -GLA_CHUNK) == (_iota2(rows, rows, 1) & -GLA_CHUNK)


def _gla_chunk_terms(la, q, k, n_c):
    C = GLA_CHUNK
    rows = n_c * C
    low = _gla_same_chunk(rows) & (_iota2(rows, rows, 0) >= _iota2(rows, rows, 1))
    b = _tri_left(_bf(low.astype(F32)), la)
    bl = [b[(c + 1) * C - 1:(c + 1) * C, :] for c in range(n_c)]
    bl_rows = jnp.concatenate([jnp.broadcast_to(bl[c], (C, b.shape[1])) for c in range(n_c)], axis=0)
    eb = jnp.exp(b)
    enb = jnp.exp(-b)
    ebl_b = jnp.exp(bl_rows - b)
    scale = GLA_HK ** -0.5
    qe = q * eb * scale
    ke = k * enb
    kd = k * ebl_b
    return bl, eb, enb, ebl_b, qe, ke, kd


def _gla_fwd_call(projf, projb, rank, wdec, bdec):
    T = projf.shape[1]
    C = GLA_CHUNK
    n_chunks = T // C
    n_c = GLA_STEP_CHUNKS
    R = n_c * C
    assert n_chunks % n_c == 0

    def body(qk_ref, v_ref, rank_ref, wd_ref, bd_ref, o_ref, st_ref, la_ref, st_scr):
        @pl.when(pl.program_id(0) == 0)
        def _():
            st_scr[...] = jnp.zeros_like(st_scr)

        dec = _dot(_bf(rank_ref[...]), _bf(wd_ref[...])) + bd_ref[...]
        la = (jnp.minimum(dec, 0.0) - _softplus_neg_abs(dec)) / GLA_TAU
        la_ref[...] = la
        mask = _gla_same_chunk(R) & (_iota2(R, R, 0) >= _iota2(R, R, 1))
        bl, _, _, _, qe, ke, kd = _gla_chunk_terms(la, qk_ref[:, :GLA_DK], qk_ref[:, GLA_DK:], n_c)
        qeb, keb, kdb = _bf(qe), _bf(ke), _bf(kd)
        ebl = [jnp.exp(bl[c]) for c in range(n_c)]
        heads = range(GLA_HEADS)
        ks = [slice(hh * GLA_HK, (hh + 1) * GLA_HK) for hh in heads]
        vs = [slice(hh * GLA_HV, (hh + 1) * GLA_HV) for hh in heads]
        rs = [slice(c * C, (c + 1) * C) for c in range(n_c)]
        p = [_bf(jnp.where(mask, _dot_nt(qeb[:, ks[hh]], keb[:, ks[hh]]), 0.0)) for hh in heads]
        upd = [[_dot_tn(v_ref[rs[c], vs[hh]], kdb[rs[c], ks[hh]]) for hh in heads] for c in range(n_c)]
        intra = [_dot(p[hh], v_ref[:, vs[hh]]) for hh in heads]
        st = [st_scr[hh] for hh in heads]
        for c in range(n_c):
            inter = [_dot_nt(qeb[rs[c], ks[hh]], _bf(st[hh])) for hh in heads]
            for hh in heads:
                st_ref[c, hh] = st[hh]
                o_ref[rs[c], vs[hh]] = intra[hh][rs[c]] + inter[hh]
            st = [st[hh] * ebl[c][:, ks[hh]] + upd[c][hh] for hh in heads]
        for hh in heads:
            st_scr[hh] = st[hh]

    return pl.pallas_call(
        body, name="gla_fwd",
        grid=(n_chunks // n_c,),
        in_specs=[pl.BlockSpec((None, R, 1024), lambda n: (0, n, 0)),
                  pl.BlockSpec((None, R, 1024), lambda n: (0, n, 0)),
                  pl.BlockSpec((R, 128), lambda n: (n, 0)),
                  pl.BlockSpec((128, GLA_DK), lambda n: (0, 0)),
                  pl.BlockSpec((1, GLA_DK), lambda n: (0, 0))],
        out_specs=(pl.BlockSpec((R, 1024), lambda n: (n, 0)),
                   pl.BlockSpec((n_c, GLA_HEADS, GLA_HV, GLA_HK), lambda n: (n, 0, 0, 0)),
                   pl.BlockSpec((R, GLA_DK), lambda n: (n, 0))),
        out_shape=(jax.ShapeDtypeStruct((T, 1024), F32),
                   jax.ShapeDtypeStruct((n_chunks, GLA_HEADS, GLA_HV, GLA_HK), F32),
                   jax.ShapeDtypeStruct((T, GLA_DK), F32)),
        scratch_shapes=[pltpu.VMEM((GLA_HEADS, GLA_HV, GLA_HK), F32)],
        compiler_params=_cparams(("arbitrary",)),
    )(projf, projb, rank, wdec, bdec)


def _gla_bwd_call(projf, projb, la, do_gla, st_all, rank, wdec):
    T = projf.shape[1]
    C = GLA_CHUNK
    n_chunks = T // C
    n_c = GLA_STEP_CHUNKS
    R = n_c * C
    assert n_chunks % n_c == 0
    last = n_chunks // n_c - 1

    def body(qk_ref, v_ref, la_ref, do_ref, st_ref, rank_ref, wd_ref,
             dqk_ref, dv_ref, drank_ref, dwd_ref, dbd_ref, dst_scr):
        @pl.when(pl.program_id(0) == 0)
        def _():
            dst_scr[...] = jnp.zeros_like(dst_scr)
            dwd_ref[...] = jnp.zeros_like(dwd_ref)
            dbd_ref[...] = jnp.zeros_like(dbd_ref)

        same = _gla_same_chunk(R)
        mask = same & (_iota2(R, R, 0) >= _iota2(R, R, 1))
        upp = _bf((same & (_iota2(R, R, 0) <= _iota2(R, R, 1))).astype(F32))
        scale = GLA_HK ** -0.5
        la = la_ref[...]
        bl, eb, enb, ebl_b, qe, ke, kd = _gla_chunk_terms(la, qk_ref[:, :GLA_DK], qk_ref[:, GLA_DK:], n_c)
        qeb, keb, kdb = _bf(qe), _bf(ke), _bf(kd)
        ebl = [jnp.exp(bl[c]) for c in range(n_c)]
        heads = range(GLA_HEADS)
        ks = [slice(hh * GLA_HK, (hh + 1) * GLA_HK) for hh in heads]
        vs = [slice(hh * GLA_HV, (hh + 1) * GLA_HV) for hh in heads]
        rs = [slice(c * C, (c + 1) * C) for c in range(n_c)]
        v = [v_ref[:, vs[hh]] for hh in heads]
        do = [_bf(do_ref[:, vs[hh]]) for hh in heads]
        p = [_bf(jnp.where(mask, _dot_nt(qeb[:, ks[hh]], keb[:, ks[hh]]), 0.0)) for hh in heads]
        dp = [_bf(jnp.where(mask, _dot_nt(do[hh], v[hh]), 0.0)) for hh in heads]
        dst_intra = [[_dot_tn(do[hh][rs[c]], qeb[rs[c], ks[hh]]) for hh in heads] for c in range(n_c)]
        dqe_inter = [[_dot(do[hh][rs[c]], _bf(st_ref[c, hh])) for hh in heads] for c in range(n_c)]
        dv_intra = [_dot_tn(p[hh], do[hh]) for hh in heads]
        dqe_intra = [_dot(dp[hh], keb[:, ks[hh]]) for hh in heads]
        dke = jnp.concatenate([_dot_tn(dp[hh], qeb[:, ks[hh]]) for hh in heads], axis=1)
        dstn = [dst_scr[hh] for hh in heads]
        dkd_c, dv_inter, debl = [None] * n_c, [None] * n_c, [None] * n_c
        for c in reversed(range(n_c)):
            dstnb = [_bf(dstn[hh]) for hh in heads]
            dkd_c[c] = jnp.concatenate([_dot(v[hh][rs[c]], dstnb[hh]) for hh in heads], axis=1)
            dv_inter[c] = [_dot_nt(kdb[rs[c], ks[hh]], dstnb[hh]) for hh in heads]
            debl[c] = jnp.concatenate(
                [jnp.sum(dstn[hh] * st_ref[c, hh], axis=0, keepdims=True) for hh in heads], axis=1)
            dstn = [dst_intra[c][hh] + dstn[hh] * ebl[c][:, ks[hh]] for hh in heads]
        for hh in heads:
            dst_scr[hh] = dstn[hh]
            dv_ref[:, vs[hh]] = _bf(dv_intra[hh] + jnp.concatenate([dv_inter[c][hh] for c in range(n_c)], axis=0))
        dqe = jnp.concatenate(
            [dqe_intra[hh] + jnp.concatenate([dqe_inter[c][hh] for c in range(n_c)], axis=0) for hh in heads], axis=1)
        dkd = jnp.concatenate(dkd_c, axis=0)
        dkd_kd = dkd * kd
        db = dqe * qe - dke * ke - dkd_kd
        dbl = jnp.concatenate(
            [jnp.broadcast_to(jnp.sum(dkd_kd[rs[c]], axis=0, keepdims=True) + ebl[c] * debl[c], (C, GLA_DK))
             for c in range(n_c)], axis=0)
        dla = _tri_left(upp, db) + dbl
        dqk_ref[:, :GLA_DK] = _bf(dqe * eb * scale)
        dqk_ref[:, GLA_DK:] = _bf(dke * enb + dkd * ebl_b)
        ddec = dla * (1.0 / GLA_TAU) * (1.0 - jnp.exp(GLA_TAU * la))
        ddecb = _bf(ddec)
        drank_ref[...] = _bf(_dot_nt(ddecb, _bf(wd_ref[...])))
        dwd_ref[...] += _dot_tn(_bf(rank_ref[...]), ddecb)
        dbd_ref[...] += jnp.sum(ddec, axis=0, keepdims=True)

    return pl.pallas_call(
        body, name="gla_bwd",
        grid=(n_chunks // n_c,),
        in_specs=[pl.BlockSpec((None, R, 1024), lambda n: (0, last - n, 0)),
                  pl.BlockSpec((None, R, 1024), lambda n: (0, last - n, 0)),
                  pl.BlockSpec((R, GLA_DK), lambda n: (last - n, 0)),
                  pl.BlockSpec((R, 1024), lambda n: (last - n, 0)),
                  pl.BlockSpec((n_c, GLA_HEADS, GLA_HV, GLA_HK), lambda n: (last - n, 0, 0, 0)),
                  pl.BlockSpec((R, 128), lambda n: (last - n, 0)),
                  pl.BlockSpec((128, GLA_DK), lambda n: (0, 0))],
        out_specs=(pl.BlockSpec((R, 1024), lambda n: (last - n, 0)),
                   pl.BlockSpec((R, 1024), lambda n: (last - n, 0)),
                   pl.BlockSpec((R, 128), lambda n: (last - n, 0)),
                   pl.BlockSpec((128, GLA_DK), lambda n: (0, 0)),
                   pl.BlockSpec((1, GLA_DK), lambda n: (0, 0))),
        out_shape=(jax.ShapeDtypeStruct((T, 1024), BF16),
                   jax.ShapeDtypeStruct((T, 1024), BF16),
                   jax.ShapeDtypeStruct((T, 128), BF16),
                   jax.ShapeDtypeStruct((128, GLA_DK), F32),
                   jax.ShapeDtypeStruct((1, GLA_DK), F32)),
        scratch_shapes=[pltpu.VMEM((GLA_HEADS, GLA_HV, GLA_HK), F32)],
        compiler_params=_cparams(("arbitrary",)),
    )(projf, projb, la, do_gla, st_all, rank, wdec)


def _sb_logs(z):
    lsz = jnp.minimum(z, 0.0) - _softplus_neg_abs(z)
    return lsz, lsz - z


SB_HG_FWD = 8
SB_HG_BWD = 4
SB_QUERIES = 256
SB_KEYS = 256
SB_DEAD = -105.0


def _sb_fwd_call(projb, wp_shard):
    T = projb.shape[1]
    B = min(SB_QUERIES, T)
    HG = SB_HG_FWD
    W = HG * SB_HD
    scale = 1.0 / math.sqrt(SB_HD)
    KB = min(SB_KEYS, T)
    n_h, n_i = SB_HEADS // HG, T // B

    def body(q_ref, k_ref, v_ref, wp_ref, o_ref, wpall_ref, cb_scr, send_sems, recv_sems, loc_sem):
        i = pl.program_id(1)
        own, pairs = _push_copies(wp_ref, wpall_ref, send_sems, recv_sems, loc_sem, scatter=False)

        @pl.when((pl.program_id(0) == 0) & (i == 0))
        def _():
            _push_start(own, pairs)

        rows = HG * B
        after = (_iota2(KB, KB, 0) > _iota2(KB, KB, 1)).astype(F32)
        tri = _bf(jnp.concatenate([after, jnp.ones((KB, KB), F32)], axis=1))
        o_ref[...] = jnp.zeros_like(o_ref)
        cb_scr[...] = jnp.zeros_like(cb_scr)

        def block(jp, masked):
            off = pl.multiple_of(jp * KB, KB)
            z = jnp.concatenate(
                [_dot_nt(q_ref[:, hh * SB_HD:(hh + 1) * SB_HD], k_ref[pl.ds(off, KB), hh * SB_HD:(hh + 1) * SB_HD])
                 for hh in range(HG)], axis=0) * scale
            lsz, l1m = _sb_logs(z)
            if masked:
                strict = (jp * KB + _iota2(rows, KB, 1)) < (i * B + (_iota2(rows, KB, 0) & (B - 1)))
                l1m = jnp.where(strict, l1m, 0.0)
            r = _tri2_right(l1m, tri)
            cb = cb_scr[...]
            a = jnp.exp(lsz + cb + r[:, :KB])
            if masked:
                a = jnp.where(strict, a, 0.0)
            cb_scr[...] = cb + r[:, KB:]
            ab = _bf(a)
            for hh in range(HG):
                cs = slice(hh * SB_HD, (hh + 1) * SB_HD)
                o_ref[:, cs] += _dot(ab[hh * B:(hh + 1) * B, :], v_ref[pl.ds(off, KB), cs])

        jp0 = (i * B) // KB
        block(jp0, True)

        def live(state):
            jj, dead = state
            return (jj <= jp0) & jnp.logical_not(dead)

        def step(state):
            jj, _ = state
            block(jp0 - jj, False)
            return jj + 1, jnp.max(cb_scr[:, :SB_HD]) < SB_DEAD

        lax.while_loop(live, step, (jnp.int32(1), jnp.max(cb_scr[:, :SB_HD]) < SB_DEAD))

        @pl.when((pl.program_id(0) == n_h - 1) & (i == n_i - 1))
        def _():
            _push_wait(own, pairs)

    return pl.pallas_call(
        body, name="sb_fwd",
        grid=(n_h, n_i),
        in_specs=[pl.BlockSpec((None, B, W), lambda h, i: (1, i, h)),
                  pl.BlockSpec((None, T, W), lambda h, i: (2, 0, h)),
                  pl.BlockSpec((None, T, W), lambda h, i: (3, 0, h)),
                  _ANY],
        out_specs=(pl.BlockSpec((B, W), lambda h, i: (i, h)), _ANY),
        out_shape=(jax.ShapeDtypeStruct((T, 1024), F32),
                   jax.ShapeDtypeStruct((N_DEV,) + wp_shard.shape, wp_shard.dtype)),
        scratch_shapes=[pltpu.VMEM((HG * B, KB), F32)] + _PUSH_SEMS,
        compiler_params=_cparams(("arbitrary", "arbitrary")),
    )(projb, projb, projb, wp_shard)


def _sb_bwd_call(projb, do_sb, g_p):
    T = projb.shape[1]
    B = min(SB_QUERIES, T)
    nb = T // B
    HG = SB_HG_BWD
    W = HG * SB_HD
    WQ = HG * B
    KB = min(SB_KEYS, T)
    nkb = T // KB
    n_h = SB_HEADS // HG
    scale = 1.0 / math.sqrt(SB_HD)

    def body(q_ref, k_ref, v_ref, do_ref, gp_ref, dq_ref, dk_ref, dv_ref, rp_ref,
             dk_scr, dv_scr, kt_scr, beta_scr, g_scr, dqt_scr, send_sems, recv_sems, loc_sem):
        i = pl.program_id(1)
        own, pairs = _push_copies(gp_ref, rp_ref, send_sems, recv_sems, loc_sem, scatter=True)

        @pl.when((pl.program_id(0) == 0) & (i == 0))
        def _():
            _push_start(own, pairs)

        @pl.when(i == 0)
        def _():
            dk_scr[...] = jnp.zeros_like(dk_scr)
            dv_scr[...] = jnp.zeros_like(dv_scr)
            for hh in range(HG):
                for jb in range(nkb):
                    kt_scr[hh, jb] = _bf(
                        k_ref[jb * KB:(jb + 1) * KB, hh * SB_HD:(hh + 1) * SB_HD].astype(F32).T)

        dqt_scr[...] = jnp.zeros_like(dqt_scr)
        later = _bf((_iota2(KB, KB, 1) > _iota2(KB, KB, 0)).astype(F32))
        earlier = _bf((_iota2(KB, KB, 1) < _iota2(KB, KB, 0)).astype(F32))
        dob = _bf(do_ref[...])
        jp0 = (i * B) // KB

        def strict_mask():
            return (jp0 * KB + _iota2(KB, WQ, 0)) < (i * B + (_iota2(KB, WQ, 1) & (B - 1)))

        def heads(fn):
            return [fn(slice(hh * SB_HD, (hh + 1) * SB_HD)) for hh in range(HG)]

        def pass1(jp, cb, masked):
            off = pl.multiple_of(jp * KB, KB)
            z = jnp.concatenate(heads(lambda cs: _dot_nt(k_ref[pl.ds(off, KB), cs], q_ref[:, cs])), axis=1) * scale
            da = jnp.concatenate(heads(lambda cs: _dot_nt(v_ref[pl.ds(off, KB), cs], dob[:, cs])), axis=1)
            lsz, l1m = _sb_logs(z)
            if masked:
                strict = strict_mask()
                l1m = jnp.where(strict, l1m, 0.0)
            a = jnp.exp(lsz + cb + _tri2_left(later, l1m))
            if masked:
                a = jnp.where(strict, a, 0.0)
            g_scr[jp] = a * da
            beta_scr[jp] = jnp.exp(lsz)
            ab = _bf(a)
            for hh in range(HG):
                cs = slice(hh * SB_HD, (hh + 1) * SB_HD)
                dv_scr[pl.ds(off, KB), cs] += _dot(ab[:, hh * B:(hh + 1) * B], dob[:, cs])
            return cb + jnp.sum(l1m, axis=0, keepdims=True)

        zero = jnp.zeros((1, WQ), F32)
        cb = pass1(jp0, zero, True)

        def live(state):
            jj, _, dead = state
            return (jj <= jp0) & jnp.logical_not(dead)

        def step(state):
            jj, cr, _ = state
            cr = pass1(jp0 - jj, cr, False)
            return jj + 1, cr, jnp.max(cr) < SB_DEAD

        n_done, _, _ = lax.while_loop(live, step, (jnp.int32(1), cb, jnp.max(cb) < SB_DEAD))
        jp_first = jp0 - (n_done - 1)

        def pass2(jp, cg, masked):
            off = pl.multiple_of(jp * KB, KB)
            g = g_scr[jp]
            beta = beta_scr[jp]
            dz = g * (1.0 - beta) - beta * (cg + _tri2_left(earlier, g))
            if masked:
                dz = jnp.where(strict_mask(), dz, 0.0)
            dzb = _bf(dz * scale)
            for hh in range(HG):
                cs = slice(hh * SB_HD, (hh + 1) * SB_HD)
                dk_scr[pl.ds(off, KB), cs] += _dot(dzb[:, hh * B:(hh + 1) * B], q_ref[:, cs])
                dqt_scr[hh] += _dot(kt_scr[hh, jp], dzb[:, hh * B:(hh + 1) * B])
            return cg + jnp.sum(g, axis=0, keepdims=True)

        cg = lax.fori_loop(jp_first, jp0, lambda jp, cr: pass2(jp, cr, False), zero)
        pass2(jp0, cg, True)
        for hh in range(HG):
            dq_ref[:, hh * SB_HD:(hh + 1) * SB_HD] = _bf(dqt_scr[hh].T)

        @pl.when(i == nb - 1)
        def _():
            dk_ref[...] = _bf(dk_scr[...])
            dv_ref[...] = _bf(dv_scr[...])

        @pl.when((pl.program_id(0) == n_h - 1) & (i == nb - 1))
        def _():
            _push_wait(own, pairs)

    return pl.pallas_call(
        body, name="sb_bwd",
        grid=(n_h, nb),
        in_specs=[pl.BlockSpec((None, B, W), lambda h, i: (1, i, h)),
                  pl.BlockSpec((None, T, W), lambda h, i: (2, 0, h)),
                  pl.BlockSpec((None, T, W), lambda h, i: (3, 0, h)),
                  pl.BlockSpec((B, W), lambda h, i: (i, h)),
                  _ANY],
        out_specs=(pl.BlockSpec((B, W), lambda h, i: (i, h)),
                   pl.BlockSpec((T, W), lambda h, i: (0, h)),
                   pl.BlockSpec((T, W), lambda h, i: (0, h)),
                   _ANY),
        out_shape=(jax.ShapeDtypeStruct((T, 1024), BF16),
                   jax.ShapeDtypeStruct((T, 1024), BF16),
                   jax.ShapeDtypeStruct((T, 1024), BF16),
                   jax.ShapeDtypeStruct(g_p.shape, g_p.dtype)),
        scratch_shapes=[pltpu.VMEM((T, W), F32), pltpu.VMEM((T, W), F32),
                        pltpu.VMEM((HG, nkb, SB_HD, KB), BF16),
                        pltpu.VMEM((nkb, KB, WQ), F32), pltpu.VMEM((nkb, KB, WQ), F32),
                        pltpu.VMEM((HG, SB_HD, B), F32)] + _PUSH_SEMS,
        compiler_params=_cparams(("arbitrary", "arbitrary")),
    )(projb, projb, projb, do_sb, g_p)


def _mid_call(o_gla, o_sb, projf, x, target, wpa, wpb, wo, gla_g, b_gate, final_g):
    T, D = x.shape
    tm = min(TBLK, T)

    def body(og_ref, ggate_ref, osb_ref, sgate_ref, ma_ref, mb_ref, x_ref, tgt_ref,
             wpa_ref, wpb_ref, wo_ref, glag_ref, bg_ref, fg_ref,
             dx2_ref, dogla_ref, dosb_ref, dggate_ref, dsgate_ref, dm_ref,
             mt_ref, ogt_ref, obt_ref, dx2b_ref, dya_ref, dyb_ref,
             dfg_ref, dbg_ref, dglag_ref, loss_ref):
        @pl.when(pl.program_id(0) == 0)
        def _():
            dfg_ref[...] = jnp.zeros_like(dfg_ref)
            dbg_ref[...] = jnp.zeros_like(dbg_ref)
            dglag_ref[...] = jnp.zeros_like(dglag_ref)
            loss_ref[...] = jnp.zeros_like(loss_ref)

        glag = glag_ref[...]
        ggate = ggate_ref[...]
        sg = _sigmoid(ggate)
        silu_g = ggate * sg
        ohat, rinv, nrm = [], [], []
        for hh in range(GLA_HEADS):
            oh = og_ref[:, hh * GLA_HV:(hh + 1) * GLA_HV]
            r = lax.rsqrt(jnp.mean(oh * oh, axis=-1, keepdims=True) + EPS)
            ohat.append(oh * r)
            rinv.append(r)
            nrm.append(ohat[-1] * glag)
        n_all = jnp.concatenate(nrm, axis=1)
        og = n_all * silu_g
        ogb = _bf(og)
        ya = _dot(ogb, wpa_ref[...])
        sgate = sgate_ref[...]
        ss = _sigmoid(sgate)
        silu_s = sgate * ss
        osb = osb_ref[...]
        ob = osb * silu_s
        obb = _bf(ob)
        yb = _dot(obb, wpb_ref[...])
        ga = _sigmoid(ma_ref[...] + bg_ref[:, :D])
        gb = _sigmoid(mb_ref[...] + bg_ref[:, D:])
        merged = ga * ya + gb * yb
        mgb = _bf(merged)
        x2 = x_ref[...] + _dot(mgb, wo_ref[...])
        r2 = lax.rsqrt(jnp.mean(x2 * x2, axis=-1, keepdims=True) + EPS)
        xh2 = x2 * r2
        fg = fg_ref[...]
        err = xh2 * fg - tgt_ref[...]
        loss_ref[...] += jnp.broadcast_to(
            0.5 * jnp.sum(jnp.mean(err * err, axis=-1, keepdims=True), axis=0, keepdims=True), (1, 128))
        dy = err * (1.0 / D)
        dfg_ref[...] += jnp.sum(dy * xh2, axis=0, keepdims=True)
        dxh = dy * fg
        dx2 = r2 * (dxh - xh2 * jnp.mean(dxh * xh2, axis=-1, keepdims=True))
        dx2_ref[...] = dx2
        dx2b = _bf(dx2)
        dx2b_ref[...] = dx2b
        dmerged = _dot_nt(dx2b, wo_ref[...])
        dya = dmerged * ga
        dyb = dmerged * gb
        dma = dmerged * ya * ga * (1.0 - ga)
        dmb = dmerged * yb * gb * (1.0 - gb)
        dm_ref[:, :D] = _bf(dma)
        dm_ref[:, D:] = _bf(dmb)
        dbg_ref[:, :D] += jnp.sum(dma, axis=0, keepdims=True)
        dbg_ref[:, D:] += jnp.sum(dmb, axis=0, keepdims=True)
        dyab = _bf(dya)
        dybb = _bf(dyb)
        dya_ref[...] = dyab
        dyb_ref[...] = dybb
        dog = _dot_nt(dyab, wpa_ref[...])
        dob = _dot_nt(dybb, wpb_ref[...])
        dosb_ref[...] = dob * silu_s
        dsgate_ref[...] = _bf(dob * osb * (ss * (1.0 + sgate * (1.0 - ss))))
        dn = dog * silu_g
        dggate_ref[...] = _bf(dog * n_all * (sg * (1.0 + ggate * (1.0 - sg))))
        dglag = jnp.zeros((1, GLA_HV), F32)
        for hh in range(GLA_HEADS):
            dnh = dn[:, hh * GLA_HV:(hh + 1) * GLA_HV]
            dglag = dglag + jnp.sum(dnh * ohat[hh], axis=0, keepdims=True)
            dohat = dnh * glag
            dogla_ref[:, hh * GLA_HV:(hh + 1) * GLA_HV] = rinv[hh] * (
                dohat - ohat[hh] * jnp.mean(dohat * ohat[hh], axis=-1, keepdims=True))
        dglag_ref[...] += dglag
        mt_ref[...] = _bf(merged.T)
        ogt_ref[...] = _bf(og.T)
        obt_ref[...] = _bf(ob.T)

    row = lambda i: (i, 0)
    const = lambda i: (0, 0)
    tile = pl.BlockSpec((tm, D), row)
    tile_t = pl.BlockSpec((None, D, tm), lambda i: (i, 0, 0))
    wspec = pl.BlockSpec((D, D), const)
    return pl.pallas_call(
        body, name="mid",
        grid=(T // tm,),
        in_specs=[tile,
                  pl.BlockSpec((None, tm, D), lambda i: (1, i, 0)),
                  tile,
                  pl.BlockSpec((None, tm, D), lambda i: (2, i, 0)),
                  pl.BlockSpec((None, tm, D), lambda i: (3, i, 0)),
                  pl.BlockSpec((None, tm, D), lambda i: (4, i, 0)),
                  tile, tile, wspec, wspec, wspec,
                  pl.BlockSpec((1, GLA_HV), const),
                  pl.BlockSpec((1, 2 * D), const),
                  pl.BlockSpec((1, D), const)],
        out_specs=(tile, tile, tile, tile, tile,
                   pl.BlockSpec((tm, 2 * D), row),
                   tile_t, tile_t, tile_t, tile, tile, tile,
                   pl.BlockSpec((1, D), const),
                   pl.BlockSpec((1, 2 * D), const),
                   pl.BlockSpec((1, GLA_HV), const),
                   pl.BlockSpec((1, 128), const)),
        out_shape=(jax.ShapeDtypeStruct((T, D), F32),
                   jax.ShapeDtypeStruct((T, D), F32),
                   jax.ShapeDtypeStruct((T, D), F32),
                   jax.ShapeDtypeStruct((T, D), BF16),
                   jax.ShapeDtypeStruct((T, D), BF16),
                   jax.ShapeDtypeStruct((T, 2 * D), BF16),
                   jax.ShapeDtypeStruct((T // tm, D, tm), BF16),
                   jax.ShapeDtypeStruct((T // tm, D, tm), BF16),
                   jax.ShapeDtypeStruct((T // tm, D, tm), BF16),
                   jax.ShapeDtypeStruct((T, D), BF16),
                   jax.ShapeDtypeStruct((T, D), BF16),
                   jax.ShapeDtypeStruct((T, D), BF16),
                   jax.ShapeDtypeStruct((1, D), F32),
                   jax.ShapeDtypeStruct((1, 2 * D), F32),
                   jax.ShapeDtypeStruct((1, GLA_HV), F32),
                   jax.ShapeDtypeStruct((1, 128), F32)),
        compiler_params=_cparams(("arbitrary",)),
    )(o_gla, projf, o_sb, projf, projf, projf, x, target, wpa, wpb, wo, gla_g, b_gate, final_g)


def _dh_call(pieces, dmlog, drank, wt, wr, x, dx2, norm_g, s_in, small):
    T, D = x.shape
    tm = min(256, T)
    npc = len(pieces)
    n_main = N_GROUPS * 1024
    n_i = T // tm
    i_forward = 5 * n_i // 8

    def body(*refs):
        pcs = refs[:npc]
        (dm_ref, dr_ref, w_hbm, wr_ref, x_ref, dx2_ref, g_ref, sin_ref, small_ref,
         gx_ref, rin_ref, relayed_ref, rsmall_ref,
         w_scr, sems, dg_ref, small_mine, small_send, small_recv, small_loc, *exchange_scratch) = refs[npc:]
        start, forward, finish = _chip_reduce_steps(sin_ref, rin_ref, relayed_ref, *exchange_scratch)

        def w_copy(g):
            return pltpu.make_async_copy(w_hbm.at[pl.ds(_group_row(g), 1024)], w_scr.at[pl.ds(g * 1024, 1024)],
                                         sems.at[g])

        @pl.when(pl.program_id(0) == 0)
        def _():
            start()
            for g in range(N_GROUPS):
                w_copy(g).start()
            dg_ref[...] = jnp.zeros_like(dg_ref)

        @pl.when(pl.program_id(0) == i_forward)
        def _():
            forward()

        def compute(first_step):
            def w_group(g):
                if first_step:
                    w_copy(g).wait()
                return w_scr[g * 1024:(g + 1) * 1024, :]

            dh = _dot(dr_ref[...], wr_ref[...])
            for g in range(npc):
                dh = dh + _dot(pcs[g][...], w_group(g))
            dh = dh + _dot(dm_ref[:, :D], w_group(npc))
            dh = dh + _dot(dm_ref[:, D:], w_group(npc + 1))
            xv = x_ref[...]
            r = lax.rsqrt(jnp.mean(xv * xv, axis=-1, keepdims=True) + EPS)
            xhat = xv * r
            dg_ref[...] += jnp.sum(dh * xhat, axis=0, keepdims=True)
            dxhat = dh * g_ref[...]
            gx_ref[...] = r * (dxhat - xhat * jnp.mean(dxhat * xhat, axis=-1, keepdims=True)) + dx2_ref[...]

        @pl.when(pl.program_id(0) == 0)
        def _():
            compute(True)

        @pl.when(pl.program_id(0) > 0)
        def _():
            compute(False)

        @pl.when(pl.program_id(0) == n_i - 1)
        def _():
            small_mine[...] = small_ref[...]
            small_mine[:, _SM_NORM:_SM_NORM + D] = dg_ref[...]
            own, pairs = _push_copies(small_mine, rsmall_ref, small_send, small_recv, small_loc, scatter=False)
            _push_start(own, pairs)
            finish()
            _push_wait(own, pairs)

    row = lambda i: (i, 0)
    const = lambda i: (0, 0)
    tile = pl.BlockSpec((tm, D), row)
    part = s_in.shape[1:]
    return pl.pallas_call(
        body, name="dh",
        grid=(n_i,),
        in_specs=[tile] * npc + [
            pl.BlockSpec((tm, 2 * D), row),
            pl.BlockSpec((tm, 128), row),
            _ANY,
            pl.BlockSpec((128, D), const),
            tile, tile,
            pl.BlockSpec((1, D), const),
            _ANY,
            pl.BlockSpec(small.shape, const)],
        out_specs=(tile, _ANY, _ANY, _ANY),
        out_shape=(jax.ShapeDtypeStruct((T, D), F32),
                   jax.ShapeDtypeStruct((3,) + part, s_in.dtype),
                   jax.ShapeDtypeStruct(part, s_in.dtype),
                   jax.ShapeDtypeStruct((N_DEV,) + small.shape, small.dtype)),
        scratch_shapes=[pltpu.VMEM((n_main, D), BF16), pltpu.SemaphoreType.DMA((N_GROUPS,)),
                        pltpu.VMEM((1, D), F32), pltpu.VMEM(small.shape, small.dtype)]
        + _PUSH_SEMS + _chip_reduce_scratch(*part, s_in.dtype),
        compiler_params=_cparams(("arbitrary",)),
    )(*pieces, dmlog, drank, wt, wr, x, dx2, norm_g, s_in, small)


def _wgrad_call(lhs_list, lhs_of_group, rhs_list, rhs_of_group, n_transposed, name, narrow=None):
    n_groups = len(rhs_of_group)
    n_tb, D, tb = lhs_list[0].shape
    T = n_tb * tb
    per = min(4, n_tb)
    tk = per * tb
    nk = T // tk
    nl = len(lhs_list)
    extra = [] if narrow is None else [narrow]

    def tokens_side_by_side(lref):
        return jnp.concatenate([lref[b] for b in range(per)], axis=1)

    def body(*refs):
        lhs = refs[:nl]
        rhs = refs[nl:nl + n_groups]
        rest = refs[nl + n_groups:]
        g = pl.program_id(0)
        i = pl.program_id(1)
        if narrow is None:
            out_ref, acc = rest
        else:
            narrow_ref, out_ref, narrow_out, acc, narrow_acc = rest

            @pl.when((g == 0) & (i == 0))
            def _():
                narrow_acc[...] = jnp.zeros_like(narrow_acc)

            @pl.when(g == 0)
            def _():
                narrow_acc[...] += _dot(tokens_side_by_side(lhs[lhs_of_group[0]]), narrow_ref[...])

            @pl.when((g == 0) & (i == nk - 1))
            def _():
                narrow_out[...] = _bf(narrow_acc[...].T)

        @pl.when(i == 0)
        def _():
            acc[...] = jnp.zeros_like(acc)

        for p in range(n_groups):
            @pl.when(g == p)
            def _(p=p):
                acc[...] += _dot(tokens_side_by_side(lhs[lhs_of_group[p]]), rhs[p][...])

        @pl.when((i == nk - 1) & (g < n_transposed))
        def _():
            out_ref[...] = _bf(acc[...].T)

        @pl.when((i == nk - 1) & (g >= n_transposed))
        def _():
            out_ref[...] = _bf(acc[...])

    def lhs_spec(a):
        groups = [g for g in range(n_groups) if lhs_of_group[g] == a]
        lo, hi = min(groups), max(groups)
        assert groups == list(range(lo, hi + 1))
        return pl.BlockSpec((per, D, tb), lambda g, i: (jnp.where((g >= lo) & (g <= hi), i, 0), 0, 0))

    def rhs_spec(p):
        cb = rhs_of_group[p][1]
        return pl.BlockSpec((tk, 1024), lambda g, i: (jnp.where(g == p, i, 0), cb))

    res = pl.pallas_call(
        body, name=name,
        grid=(n_groups, nk),
        in_specs=[lhs_spec(a) for a in range(nl)] + [rhs_spec(p) for p in range(n_groups)]
        + [pl.BlockSpec((tk, 128), lambda g, i: (jnp.where(g == 0, i, 0), 0)) for _ in extra],
        out_specs=[pl.BlockSpec((None, D, 1024), lambda g, i: (g, 0, 0))]
        + [pl.BlockSpec((128, D), lambda g, i: (0, 0)) for _ in extra],
        out_shape=[jax.ShapeDtypeStruct((n_groups, D, 1024), BF16)]
        + [jax.ShapeDtypeStruct((128, D), BF16) for _ in extra],
        scratch_shapes=[pltpu.VMEM((D, 1024), F32)] + [pltpu.VMEM((D, 128), F32) for _ in extra],
        compiler_params=_cparams(("arbitrary", "arbitrary")),
    )(*lhs_list, *[rhs_list[rhs_of_group[p][0]] for p in range(n_groups)], *extra)
    return res[0] if narrow is None else res


def _adamw_math(parts, w, m, v):
    g = parts[0].astype(F32)
    for p in parts[1:]:
        g = g + p.astype(F32)
    mm = ADAM_B1 * m + (1.0 - ADAM_B1) * g
    vv = ADAM_B2 * v + (1.0 - ADAM_B2) * (g * g)
    m_hat = mm / (1.0 - ADAM_B1 ** ADAM_STEP)
    v_hat = vv / (1.0 - ADAM_B2 ** ADAM_STEP)
    return g, -ADAM_LR * (m_hat / (jnp.sqrt(v_hat) + ADAM_EPS) + ADAM_WD * w), mm, vv


def _part_order(n_parts):
    return [n_parts - 1] + list(range(n_parts - 1))


def _adamw_call(parts, w, m, v, name):
    R, C = w.shape
    n_parts = parts.shape[0]
    (tr, tc), grid, idx = _tiling_2d(R, C, 512)

    def body(p_ref, w_ref, m_ref, v_ref, g_ref, d_ref, nm_ref, nv_ref):
        g_ref[...], d_ref[...], nm_ref[...], nv_ref[...] = _adamw_math(
            [p_ref[k] for k in _part_order(n_parts)], w_ref[...], m_ref[...], v_ref[...])

    blk = pl.BlockSpec((tr, tc), idx)
    sds = jax.ShapeDtypeStruct((R, C), F32)
    return pl.pallas_call(
        body, name=name,
        grid=grid,
        in_specs=[pl.BlockSpec((n_parts, tr, tc), lambda i: (0,) + idx(i)), blk, blk, blk],
        out_specs=(blk, blk, blk, blk),
        out_shape=(sds, sds, sds, sds),
        compiler_params=_cparams(("arbitrary",)),
    )(parts, w, m, v)


def _adamw_rows_call(parts, ws, ms, vs, name):
    n = len(ws)
    R, C = ws[0].shape
    n_parts = parts.shape[0]

    def body(*refs):
        p_ref = refs[0]
        w_refs, m_refs, v_refs = refs[1:1 + n], refs[1 + n:1 + 2 * n], refs[1 + 2 * n:1 + 3 * n]
        outs = refs[1 + 3 * n:]
        for k in range(n):
            @pl.when(pl.program_id(0) == k)
            def _(k=k):
                res = _adamw_math([p_ref[j] for j in _part_order(n_parts)],
                                  w_refs[k][...], m_refs[k][...], v_refs[k][...])
                for o_ref, val in zip(outs[4 * k:4 * k + 4], res):
                    o_ref[...] = val

    whole = pl.BlockSpec((R, C), lambda k: (0, 0))
    sds = jax.ShapeDtypeStruct((R, C), F32)
    res = pl.pallas_call(
        body, name=name,
        grid=(n,),
        in_specs=[pl.BlockSpec((n_parts, R, C), lambda k: (0, k, 0))] + [whole] * (3 * n),
        out_specs=tuple([whole] * (4 * n)),
        out_shape=tuple([sds] * (4 * n)),
        compiler_params=_cparams(("arbitrary",)),
    )(parts, *ws, *ms, *vs)
    return [res[4 * k:4 * k + 4] for k in range(n)]


def _adamw_lanes_call(parts, offsets, ws, ms, vs, name):
    n = len(ws)
    n_parts = parts.shape[0]

    def body(*refs):
        p_ref = refs[0]
        w_refs, m_refs, v_refs = refs[1:1 + n], refs[1 + n:1 + 2 * n], refs[1 + 2 * n:1 + 3 * n]
        outs = refs[1 + 3 * n:]
        for k in range(n):
            lanes = slice(offsets[k], offsets[k] + ws[k].shape[1])
            res = _adamw_math([p_ref[j, :, lanes] for j in _part_order(n_parts)],
                              w_refs[k][...], m_refs[k][...], v_refs[k][...])
            for o_ref, val in zip(outs[4 * k:4 * k + 4], res):
                o_ref[...] = val

    res = pl.pallas_call(
        body, name=name,
        out_shape=tuple(jax.ShapeDtypeStruct(ws[k].shape, F32) for k in range(n) for _ in range(4)),
        compiler_params=_cparams(),
    )(parts, *ws, *ms, *vs)
    return [res[4 * k:4 * k + 4] for k in range(n)]


def _local_step(x, target, wt, wr, wdec, bdec, wp_shard, norm_g, gla_g, b_gate, final_g):
    D = x.shape[1]
    half = wp_shard.shape[1] // 2
    projf, projb, rank, ht, wp_lo = _proj_call(x, norm_g, wt, wr, wp_shard[:, :half])
    o_gla, st_all, la = _gla_fwd_call(projf, projb, rank, wdec, bdec)
    o_sb, wp_hi = _sb_fwd_call(projb, wp_shard[:, half:])
    wp_full = jnp.concatenate([wp_lo, wp_hi], axis=2).transpose(1, 0, 2, 3).reshape(3, D, D)
    (dx2, do_gla, do_sb, dggate, dsgate, dmlog, mt, ogt, obt, dx2b, dya, dyb,
     dfinal_g, db_gate, dgla_g, loss) = _mid_call(o_gla, o_sb, projf, x, target, wp_full[0], wp_full[1],
                                                 wp_full[2], gla_g, b_gate, final_g)
    dw_p = _wgrad_call([ogt, obt, mt], [0, 1, 2], [dya, dyb, dx2b], [(0, 0), (1, 0), (2, 0)], 0, "wgrad_p")
    g_p = dw_p.reshape(3, N_DEV, D // N_DEV, D).transpose(1, 0, 2, 3).reshape(N_DEV, 3 * (D // N_DEV), D)
    dqk, dgv, drank, dwdec, dbdec = _gla_bwd_call(projf, projb, la, do_gla, st_all, rank, wdec)
    dsq, dsk, dsv, r_p = _sb_bwd_call(projb, do_sb, g_p)
    pieces = [dqk, dgv, dggate, dsq, dsk, dsv, dsgate]
    rhs_of_group = [(g, 0) for g in range(7)] + [(7, 0), (7, 1)]
    dw_in, dwr = _wgrad_call([ht], [0] * N_GROUPS, pieces + [dmlog], rhs_of_group, N_GROUPS, "wgrad_in",
                             narrow=drank)
    s_in = _pair_sum_call(dw_in.reshape(N_GROUPS * 1024, D), dwr)
    small = jnp.concatenate([
        jnp.zeros((D,), F32), dbdec.reshape(-1), dgla_g.reshape(-1), db_gate.reshape(-1), dfinal_g.reshape(-1),
        loss.reshape(-1), dwdec[:GLA_RANK].reshape(-1)]).reshape(1, _SM_LEN)
    grad_x, r_in, _, r_small = _dh_call(pieces, dmlog, drank, wt, wr, x, dx2, norm_g, s_in, small)
    return grad_x, r_in, r_p, r_small


_SM_NORM = 0
_SM_BDEC = _SM_NORM + D_MODEL
_SM_GLAG = _SM_BDEC + GLA_DK
_SM_BGATE = _SM_GLAG + GLA_HV
_SM_FINAL = _SM_BGATE + 2 * D_MODEL
_SM_REPL = _SM_FINAL + D_MODEL
_SM_LOSS = _SM_REPL
_SM_WDEC = _SM_LOSS + 128
_SM_LEN = _SM_WDEC + GLA_RANK * GLA_DK


def kernel(x, norm_g, w_in, w_dec_up, b_dec, gla_norm_g, w_pa, w_pb, b_gate, w_o, final_g, loss_target, m_norm_g, m_w_in, m_w_dec_up, m_b_dec, m_gla_norm_g, m_w_pa, m_w_pb, m_b_gate, m_w_o, m_final_g, v_norm_g, v_w_in, v_w_dec_up, v_b_dec, v_gla_norm_g, v_w_pa, v_w_pb, v_b_gate, v_w_o, v_final_g):
    D = D_MODEL
    me = 4 * lax.axis_index("x") + 2 * lax.axis_index("y") + lax.axis_index("c")

    wp_shard = jnp.stack([w_pa, w_pb, w_o]).astype(BF16)
    n_first = _half_rows(SHARD_COLS)
    win_all, wdec_all = _all_gather([w_in.T.astype(BF16), w_dec_up], "gather_w",
                                    row_pieces=[[(0, n_first), (n_first, SHARD_COLS - n_first)], None])
    wt = _flatten_blocks_call(win_all)
    wr = jnp.pad(wt[RANK_COL:RANK_COL + GLA_RANK], ((0, 128 - GLA_RANK), (0, 0)))
    wdec_full = wdec_all.transpose(1, 0, 2).reshape(GLA_RANK, GLA_DK)
    wdec = jnp.pad(wdec_full, ((0, 128 - GLA_RANK), (0, 0)))

    grad_x, r_in, r_p, r_small = _local_step(
        x[0], loss_target[0], wt, wr, wdec, b_dec.reshape(1, -1), wp_shard,
        norm_g.reshape(1, -1), gla_norm_g.reshape(1, -1), b_gate.reshape(1, -1), final_g.reshape(1, -1))

    gw_in, d_in, nm_in, nv_in = (a.T for a in _adamw_call(r_in, w_in.T, m_w_in.T, v_w_in.T, "adamw_in"))
    (g_pa, d_pa, nm_pa, nv_pa), (g_pb, d_pb, nm_pb, nv_pb), (g_o, d_o, nm_o, nv_o) = _adamw_rows_call(
        r_p, [w_pa, w_pb, w_o], [m_w_pa, m_w_pb, m_w_o], [v_w_pa, v_w_pb, v_w_o], "adamw_p")

    def row(a):
        return a.reshape(1, -1)

    rep = _adamw_lanes_call(
        r_small, [_SM_NORM, _SM_BDEC, _SM_GLAG, _SM_BGATE, _SM_FINAL],
        [row(a) for a in (norm_g, b_dec, gla_norm_g, b_gate, final_g)],
        [row(a) for a in (m_norm_g, m_b_dec, m_gla_norm_g, m_b_gate, m_final_g)],
        [row(a) for a in (v_norm_g, v_b_dec, v_gla_norm_g, v_b_gate, v_final_g)], "adamw_rep")
    ((g_norm, d_norm, nm_norm, nv_norm), (g_bdec, d_bdec, nm_bdec, nv_bdec), (g_glag, d_glag, nm_glag, nv_glag),
     (g_bgate, d_bgate, nm_bgate, nv_bgate), (g_final, d_final, nm_final, nv_final)) = [
        tuple(a.reshape(-1) for a in quad) for quad in rep]

    wdec_parts = r_small[:, 0, _SM_WDEC:].reshape(N_DEV, GLA_RANK, GLA_DK)
    cols = GLA_DK // N_DEV
    wdec_mine = lax.dynamic_slice_in_dim(wdec_parts, me * cols, cols, axis=2)
    g_wdec, d_wdec, nm_wdec, nv_wdec = _adamw_call(wdec_mine, w_dec_up, m_w_dec_up, v_w_dec_up, "adamw_dec")

    loss_total = jnp.sum(r_small[:, 0, _SM_LOSS])

    return (loss_total, grad_x[None],
            g_norm, gw_in, g_wdec, g_bdec, g_glag, g_pa, g_pb, g_bgate, g_o, g_final,
            d_norm, d_in, d_wdec, d_bdec, d_glag, d_pa, d_pb, d_bgate, d_o, d_final,
            nm_norm, nm_in, nm_wdec, nm_bdec, nm_glag, nm_pa, nm_pb, nm_bgate, nm_o, nm_final,
            nv_norm, nv_in, nv_wdec, nv_bdec, nv_glag, nv_pa, nv_pb, nv_bgate, nv_o, nv_final)
```

```python
import math

import jax
import jax.numpy as jnp
from jax import lax
from jax.experimental import pallas as pl
from jax.experimental.pallas import tpu as pltpu

F32 = jnp.float32
BF16 = jnp.bfloat16

N_DEV = 8
D_MODEL = 1024
GLA_HEADS = 4
GLA_HK = 128
GLA_HV = 256
GLA_DK = 512
GLA_RANK = 16
GLA_TAU = 16.0
GLA_CHUNK = 64
SB_HEADS = 8
SB_HD = 128
EPS = 1e-6
N_GROUPS = 9
RANK_COL = 3072
IN_COLS = 9232
SHARD_COLS = IN_COLS // N_DEV

ADAM_LR = 0.001
ADAM_B1 = 0.9
ADAM_B2 = 0.999
ADAM_EPS = 1e-08
ADAM_WD = 0.01
ADAM_STEP = 10

VMEM_LIMIT = 56 * 1024 * 1024
TBLK = 256


def _cparams(sem=None):
    return pltpu.CompilerParams(dimension_semantics=sem, vmem_limit_bytes=VMEM_LIMIT)


def _tiling_2d(rows, cols, band_cols):
    if rows * cols <= 128 * 1024:
        return (rows, cols), (1,), lambda i: (0, 0)
    if rows % 128 == 0:
        return (128, cols), (rows // 128,), lambda i: (i, 0)
    tc = band_cols if cols % band_cols == 0 else cols
    return (rows, tc), (cols // tc,), lambda i: (0, i)


def _dot(a, b):
    return jnp.dot(a, b, preferred_element_type=F32)


def _dot_nt(a, b):
    return lax.dot_general(a, b, (((1,), (1,)), ((), ())), preferred_element_type=F32)


def _dot_tn(a, b):
    return lax.dot_general(a, b, (((0,), (0,)), ((), ())), preferred_element_type=F32)


def _bf(x):
    return x.astype(BF16)


def _split3(x):
    hi = x.astype(BF16)
    r = x - hi.astype(F32)
    mid = r.astype(BF16)
    lo = (r - mid.astype(F32)).astype(BF16)
    return hi, mid, lo


def _tri_left(tri, x):
    hi, mid, lo = _split3(x)
    return _dot(tri, hi) + _dot(tri, mid) + _dot(tri, lo)


def _split2(x):
    hi = lax.bitcast_convert_type(lax.bitcast_convert_type(x, jnp.uint32) & jnp.uint32(0xFFFF0000), F32)
    return hi.astype(BF16), (x - hi).astype(BF16)


def _tri2_left(tri, x):
    hi, lo = _split2(x)
    return _dot(tri, hi) + _dot(tri, lo)


def _tri2_right(x, tri):
    hi, lo = _split2(x)
    return _dot(hi, tri) + _dot(lo, tri)


def _iota2(n, m, dim):
    return lax.broadcasted_iota(jnp.int32, (n, m), dim)


def _sigmoid(x):
    return 1.0 / (1.0 + jnp.exp(-x))


def _softplus_neg_abs(z):
    return jnp.log(1.0 + jnp.exp(-jnp.abs(z)))


_ANY = pl.BlockSpec(memory_space=pl.ANY)


def _mesh_pos():
    return lax.axis_index("x"), lax.axis_index("y"), lax.axis_index("c")


def _other_chips(x, y):
    return [(1 - x, y), (x, 1 - y), (1 - x, 1 - y)]


def _rcopy(src, dst, send_sem, recv_sem, to):
    return pltpu.make_async_remote_copy(src_ref=src, dst_ref=dst, send_sem=send_sem, recv_sem=recv_sem,
                                        device_id=to, device_id_type=pl.DeviceIdType.MESH)


def _push_copies(src_ref, dst_ref, send_sems, recv_sems, loc_sem, scatter):
    x, y, c = _mesh_pos()
    me = 4 * x + 2 * y + c
    own = pltpu.make_async_copy(src_ref.at[me] if scatter else src_ref, dst_ref.at[me], loc_sem)
    pairs = []
    for k in range(1, N_DEV):
        px = 1 - x if k & 4 else x
        py = 1 - y if k & 2 else y
        pc = 1 - c if k & 1 else c
        pid = 4 * px + 2 * py + pc
        src = src_ref.at[pid] if scatter else src_ref
        send = _rcopy(src, dst_ref.at[me], send_sems.at[k - 1], recv_sems.at[k - 1], (px, py, pc))
        recv = _rcopy(src, dst_ref.at[pid], send_sems.at[k - 1], recv_sems.at[k - 1], (px, py, pc))
        pairs.append((send, recv))
    return own, pairs


def _push_start(own, pairs):
    own.start()
    for send, _ in pairs:
        send.start()


def _push_wait(own, pairs):
    for _, recv in pairs:
        recv.wait_recv()
    for send, _ in pairs:
        send.wait_send()
    own.wait()


_PUSH_SEMS = [pltpu.SemaphoreType.DMA((N_DEV - 1,)), pltpu.SemaphoreType.DMA((N_DEV - 1,)),
              pltpu.SemaphoreType.DMA]


def _half_rows(rows):
    return (rows // 2) // 16 * 16


_ADD_ROWS = 128


def _chip_reduce_steps(src_ref, dst_ref, relayed_ref, sum_x, sum_y, rel_x, rel_y, load_sems, send_sems, recv_sems,
                       loc_sem):
    _, R, C = src_ref.shape
    n0 = _half_rows(R)
    lo, hi = pl.ds(0, n0), pl.ds(n0, R - n0)
    x, y, c = _mesh_pos()
    (xx, xy), (yx, yy), (dx, dy) = _other_chips(x, y)
    to_diag, to_x, to_y = src_ref.at[2 * dx + dy], src_ref.at[2 * xx + xy], src_ref.at[2 * yx + yy]
    x_nb, y_nb = (xx, xy, c), (yx, yy, c)
    relays = (_rcopy(to_diag.at[lo], relayed_ref.at[lo], send_sems.at[0], recv_sems.at[0], x_nb),
              _rcopy(to_diag.at[hi], relayed_ref.at[hi], send_sems.at[1], recv_sems.at[1], y_nb))
    plain = (_rcopy(to_x.at[lo], dst_ref.at[0, lo], send_sems.at[2], recv_sems.at[2], x_nb),
             _rcopy(to_y.at[hi], dst_ref.at[1, hi], send_sems.at[3], recv_sems.at[3], y_nb))
    summed = (_rcopy(sum_x, dst_ref.at[0, hi], send_sems.at[4], recv_sems.at[4], x_nb),
              _rcopy(sum_y, dst_ref.at[1, lo], send_sems.at[5], recv_sems.at[5], y_nb))
    load_mine = (pltpu.make_async_copy(to_x.at[hi], sum_x, load_sems.at[0]),
                 pltpu.make_async_copy(to_y.at[lo], sum_y, load_sems.at[1]))
    load_relayed = (pltpu.make_async_copy(relayed_ref.at[hi], rel_x, load_sems.at[2]),
                    pltpu.make_async_copy(relayed_ref.at[lo], rel_y, load_sems.at[3]))
    own = pltpu.make_async_copy(src_ref.at[2 * x + y], dst_ref.at[2], loc_sem)

    def start():
        for cp in relays + plain + (own,) + load_mine:
            cp.start()

    def add(acc_ref, rel_ref):
        for r0 in range(0, acc_ref.shape[0], _ADD_ROWS):
            rows = slice(r0, min(r0 + _ADD_ROWS, acc_ref.shape[0]))
            acc_ref[rows, :] = (acc_ref[rows, :].astype(F32) + rel_ref[rows, :].astype(F32)).astype(acc_ref.dtype)

    def forward():
        for cp in relays:
            cp.wait_recv()
        for cp in load_relayed:
            cp.start()
        for cp in load_mine + load_relayed:
            cp.wait()
        add(sum_x, rel_x)
        add(sum_y, rel_y)
        for cp in summed:
            cp.start()

    def finish():
        for cp in plain + summed:
            cp.wait_recv()
        for cp in relays + plain + summed:
            cp.wait_send()
        own.wait()

    return start, forward, finish


def _chip_reduce_scratch(rows, cols, dtype):
    n0 = _half_rows(rows)
    return [pltpu.VMEM((rows - n0, cols), dtype), pltpu.VMEM((n0, cols), dtype)] * 2 + [
        pltpu.SemaphoreType.DMA((4,)), pltpu.SemaphoreType.DMA((6,)), pltpu.SemaphoreType.DMA((6,)),
        pltpu.SemaphoreType.DMA]


def _all_gather(arrs, name, row_pieces=None):
    n = len(arrs)
    pieces = [[None] if not row_pieces or not row_pieces[a] else list(row_pieces[a]) for a in range(n)]
    assert all(len(p) in (1, 2) for p in pieces)
    units = [(a, i) for a in range(n) for i in range(len(pieces[a]))]

    def body(*refs):
        ins = refs[:n]
        outs = refs[n:2 * n]
        send_sems, recv_sems, loc_sems = refs[2 * n:]
        x, y, c = _mesh_pos()
        me, sib = (x, y, c), (x, y, 1 - c)
        xn, yn, dg = [(px, py, c) for px, py in _other_chips(x, y)]

        def rows(ref, a, i):
            return ref if pieces[a][i] is None else ref.at[pl.ds(*pieces[a][i])]

        def copy(u, k, block, to, own=False):
            a, i = u
            px, py, pc = block
            dst = rows(outs[a].at[4 * px + 2 * py + pc], a, i)
            return _rcopy(rows(ins[a], a, i) if own else dst, dst, send_sems.at[a, k, i], recv_sems.at[a, k, i], to)

        started = []

        def start(cp):
            cp.start()
            started.append(cp)

        def landed_then_pass_on(u, k, block):
            copy(u, k, block, me).wait_recv()
            start(copy(u, 3 + k, block, sib))

        mine = [pltpu.make_async_copy(ins[a], outs[a].at[4 * x + 2 * y + c], loc_sems.at[a]) for a in range(n)]
        for cp in mine:
            cp.start()
        for u in units:
            start(copy(u, 0, me, sib, own=True))
        for a in range(n):
            if len(pieces[a]) == 2:
                for i, to, k in ((0, xn, 1), (1, yn, 2), (1, xn, 1), (0, yn, 2)):
                    start(copy((a, i), k, me, to, own=True))
            else:
                for to, k in ((xn, 1), (yn, 2), (dg, 3)):
                    start(copy((a, 0), k, me, to, own=True))
        for a in range(n):
            if len(pieces[a]) == 2:
                landed_then_pass_on((a, 0), 1, xn)
                start(copy((a, 0), 3, xn, yn))
                landed_then_pass_on((a, 1), 2, yn)
                start(copy((a, 1), 3, yn, xn))
                landed_then_pass_on((a, 1), 1, xn)
                landed_then_pass_on((a, 0), 2, yn)
                landed_then_pass_on((a, 0), 3, dg)
                landed_then_pass_on((a, 1), 3, dg)
            else:
                for block, k in ((xn, 1), (yn, 2), (dg, 3)):
                    landed_then_pass_on((a, 0), k, block)
        for u in units:
            copy(u, 0, sib, me).wait_recv()
            for k, (px, py, _) in ((4, xn), (5, yn), (6, dg)):
                copy(u, k, (px, py, 1 - c), me).wait_recv()
        for cp in started:
            cp.wait_send()
        for cp in mine:
            cp.wait()

    n_pc = max(len(p) for p in pieces)

    return pl.pallas_call(
        body, name=name,
        out_shape=tuple(jax.ShapeDtypeStruct((N_DEV,) + a.shape, a.dtype) for a in arrs),
        in_specs=[_ANY] * n,
        out_specs=tuple([_ANY] * n),
        scratch_shapes=[pltpu.SemaphoreType.DMA((n, 7, n_pc)), pltpu.SemaphoreType.DMA((n, 7, n_pc)),
                        pltpu.SemaphoreType.DMA((n,))],
    )(*arrs)


def _flatten_blocks_call(blocks):
    n, R, C = blocks.shape
    tc = C // 2

    def body(in_ref, out_ref):
        for p in range(n):
            out_ref[p * R:(p + 1) * R, :] = in_ref[p]

    return pl.pallas_call(
        body, name="flatten_w",
        grid=(C // tc,),
        in_specs=[pl.BlockSpec((n, R, tc), lambda i: (0, 0, i))],
        out_specs=pl.BlockSpec((n * R, tc), lambda i: (0, i)),
        out_shape=jax.ShapeDtypeStruct((n * R, C), blocks.dtype),
        compiler_params=_cparams(("arbitrary",)),
    )(blocks)


_PARTS_BANDS = 4


def _pair_sum_call(dmain, drank):
    D = dmain.shape[1]
    n = _PARTS_BANDS
    tc = D // n

    def body(dm_ref, dr_ref, sum_ref, laid, got, send_sems, recv_sems):
        x, y, c = _mesh_pos()

        def pushes(k):
            return [_rcopy(laid.at[k % 2, 2 * q + (1 - c)], got.at[k, q], send_sems.at[k, q], recv_sems.at[k, q],
                           (x, y, 1 - c)) for q in range(4)]

        def lay_out(k):
            for p in range(N_DEV):
                lo, hi = p * SHARD_COLS, (p + 1) * SHARD_COLS
                at = 0
                for src, a, b in ((dm_ref, lo, min(hi, RANK_COL)),
                                  (dr_ref, max(lo, RANK_COL) - RANK_COL, min(hi, RANK_COL + GLA_RANK) - RANK_COL),
                                  (dm_ref, max(lo, RANK_COL + GLA_RANK) - GLA_RANK, hi - GLA_RANK)):
                    if b > a:
                        laid[k % 2, p, at:at + (b - a), :] = src[a:b, :]
                        at += b - a

        for k in range(n + 1):
            @pl.when(pl.program_id(0) == k)
            def _(k=k):
                if k < n:
                    if k >= 2:
                        for cp in pushes(k - 2):
                            cp.wait_send()
                    lay_out(k)
                    for cp in pushes(k):
                        cp.start()
                if k >= 1:
                    for cp in pushes(k - 1):
                        cp.wait_recv()
                    for q in range(4):
                        sum_ref[q] = (laid[(k - 1) % 2, 2 * q + c].astype(F32)
                                      + got[k - 1, q].astype(F32)).astype(sum_ref.dtype)
                if k == n:
                    for k_open in range(max(0, n - 2), n):
                        for cp in pushes(k_open):
                            cp.wait_send()

    sems = pltpu.SemaphoreType.DMA((n, 4))
    return pl.pallas_call(
        body, name="pair_sum",
        grid=(n + 1,),
        in_specs=[pl.BlockSpec((dmain.shape[0], tc), lambda k: (0, jnp.minimum(k, n - 1))),
                  pl.BlockSpec((GLA_RANK, tc), lambda k: (0, jnp.minimum(k, n - 1)))],
        out_specs=pl.BlockSpec((4, SHARD_COLS, tc), lambda k: (0, 0, jnp.maximum(k - 1, 0))),
        out_shape=jax.ShapeDtypeStruct((4, SHARD_COLS, D), dmain.dtype),
        scratch_shapes=[pltpu.VMEM((2, N_DEV, SHARD_COLS, tc), dmain.dtype),
                        pltpu.VMEM((n, 4, SHARD_COLS, tc), dmain.dtype), sems, sems],
        compiler_params=_cparams(("arbitrary",)),
    )(dmain, drank)


def _group_row(g):
    return GLA_RANK * (g * (1024 // GLA_RANK) + (g >= RANK_COL // 1024))


def _proj_call(x, norm_g, wt, wr, wp_part):
    T, D = x.shape
    tm = min(1024, T)
    assert tm % TBLK == 0
    n_i = T // tm

    def f_slot(j):
        return ((j >= 2).astype(jnp.int32) + (j >= 6).astype(jnp.int32)
                + (j >= 7).astype(jnp.int32) + (j >= 8).astype(jnp.int32))

    def b_slot(j):
        return (j >= 3).astype(jnp.int32) + (j >= 4).astype(jnp.int32) + (j >= 5).astype(jnp.int32)

    def body(x_ref, g_ref, w_ref, wr_ref, wp_ref, pf_ref, pb_ref, rank_ref, ht_ref, wpall_ref,
             h_scr, send_sems, recv_sems, loc_sem):
        i = pl.program_id(0)
        j = pl.program_id(1)
        own, pairs = _push_copies(wp_ref, wpall_ref, send_sems, recv_sems, loc_sem, scatter=False)

        @pl.when((i == 0) & (j == 0))
        def _():
            _push_start(own, pairs)

        @pl.when(j == 0)
        def _():
            xv = x_ref[...]
            r = lax.rsqrt(jnp.mean(xv * xv, axis=-1, keepdims=True) + EPS)
            h = (xv * r) * g_ref[...]
            hb = _bf(h)
            h_scr[...] = hb
            for b in range(tm // TBLK):
                ht_ref[b] = _bf(h[b * TBLK:(b + 1) * TBLK].T)
            rank_ref[...] = _dot_nt(hb, wr_ref[...])

        is_b = (j == 1) | ((j >= 3) & (j <= 5))

        @pl.when(is_b)
        def _():
            pb_ref[...] = _bf(_dot_nt(h_scr[...], w_ref[...]))

        @pl.when(jnp.logical_not(is_b))
        def _():
            pf_ref[...] = _dot_nt(h_scr[...], w_ref[...])

        @pl.when((i == n_i - 1) & (j == N_GROUPS - 1))
        def _():
            _push_wait(own, pairs)

    return pl.pallas_call(
        body, name="proj",
        grid=(n_i, N_GROUPS),
        in_specs=[pl.BlockSpec((tm, D), lambda i, j: (i, 0)),
                  pl.BlockSpec((1, D), lambda i, j: (0, 0)),
                  pl.BlockSpec((pl.Element(1024), pl.Element(D)), lambda i, j: (_group_row(j), 0)),
                  pl.BlockSpec((128, D), lambda i, j: (0, 0)),
                  _ANY],
        out_specs=(pl.BlockSpec((None, tm, 1024), lambda i, j: (f_slot(j), i, 0)),
                   pl.BlockSpec((None, tm, 1024), lambda i, j: (b_slot(j), i, 0)),
                   pl.BlockSpec((tm, 128), lambda i, j: (i, 0)),
                   pl.BlockSpec((tm // TBLK, D, TBLK), lambda i, j: (i, 0, 0)),
                   _ANY),
        out_shape=(jax.ShapeDtypeStruct((5, T, 1024), F32),
                   jax.ShapeDtypeStruct((4, T, 1024), BF16),
                   jax.ShapeDtypeStruct((T, 128), F32),
                   jax.ShapeDtypeStruct((T // TBLK, D, TBLK), BF16),
                   jax.ShapeDtypeStruct((N_DEV,) + wp_part.shape, wp_part.dtype)),
        scratch_shapes=[pltpu.VMEM((tm, D), BF16)] + _PUSH_SEMS,
        compiler_params=_cparams(("arbitrary", "arbitrary")),
    )(x, norm_g, wt, wr, wp_part)


GLA_STEP_CHUNKS = 4


def _gla_same_chunk(rows):
    return (_iota2(rows, rows, 0) & -GLA_CHUNK) == (_iota2(rows, rows, 1) & -GLA_CHUNK)


def _gla_chunk_terms(la, q, k, n_c):
    C = GLA_CHUNK
    rows = n_c * C
    low = _gla_same_chunk(rows) & (_iota2(rows, rows, 0) >= _iota2(rows, rows, 1))
    b = _tri_left(_bf(low.astype(F32)), la)
    bl = [b[(c + 1) * C - 1:(c + 1) * C, :] for c in range(n_c)]
    bl_rows = jnp.concatenate([jnp.broadcast_to(bl[c], (C, b.shape[1])) for c in range(n_c)], axis=0)
    eb = jnp.exp(b)
    enb = jnp.exp(-b)
    ebl_b = jnp.exp(bl_rows - b)
    scale = GLA_HK ** -0.5
    qe = q * eb * scale
    ke = k * enb
    kd = k * ebl_b
    return bl, eb, enb, ebl_b, qe, ke, kd


def _gla_fwd_call(projf, projb, rank, wdec, bdec):
    T = projf.shape[1]
    C = GLA_CHUNK
    n_chunks = T // C
    n_c = GLA_STEP_CHUNKS
    R = n_c * C
    assert n_chunks % n_c == 0

    def body(qk_ref, v_ref, rank_ref, wd_ref, bd_ref, o_ref, st_ref, la_ref, st_scr):
        @pl.when(pl.program_id(0) == 0)
        def _():
            st_scr[...] = jnp.zeros_like(st_scr)

        dec = _dot(_bf(rank_ref[...]), _bf(wd_ref[...])) + bd_ref[...]
        la = (jnp.minimum(dec, 0.0) - _softplus_neg_abs(dec)) / GLA_TAU
        la_ref[...] = la
        mask = _gla_same_chunk(R) & (_iota2(R, R, 0) >= _iota2(R, R, 1))
        bl, _, _, _, qe, ke, kd = _gla_chunk_terms(la, qk_ref[:, :GLA_DK], qk_ref[:, GLA_DK:], n_c)
        qeb, keb, kdb = _bf(qe), _bf(ke), _bf(kd)
        ebl = [jnp.exp(bl[c]) for c in range(n_c)]
        heads = range(GLA_HEADS)
        ks = [slice(hh * GLA_HK, (hh + 1) * GLA_HK) for hh in heads]
        vs = [slice(hh * GLA_HV, (hh + 1) * GLA_HV) for hh in heads]
        rs = [slice(c * C, (c + 1) * C) for c in range(n_c)]
        p = [_bf(jnp.where(mask, _dot_nt(qeb[:, ks[hh]], keb[:, ks[hh]]), 0.0)) for hh in heads]
        upd = [[_dot_tn(v_ref[rs[c], vs[hh]], kdb[rs[c], ks[hh]]) for hh in heads] for c in range(n_c)]
        intra = [_dot(p[hh], v_ref[:, vs[hh]]) for hh in heads]
        st = [st_scr[hh] for hh in heads]
        for c in range(n_c):
            inter = [_dot_nt(qeb[rs[c], ks[hh]], _bf(st[hh])) for hh in heads]
            for hh in heads:
                st_ref[c, hh] = st[hh]
                o_ref[rs[c], vs[hh]] = intra[hh][rs[c]] + inter[hh]
            st = [st[hh] * ebl[c][:, ks[hh]] + upd[c][hh] for hh in heads]
        for hh in heads:
            st_scr[hh] = st[hh]

    return pl.pallas_call(
        body, name="gla_fwd",
        grid=(n_chunks // n_c,),
        in_specs=[pl.BlockSpec((None, R, 1024), lambda n: (0, n, 0)),
                  pl.BlockSpec((None, R, 1024), lambda n: (0, n, 0)),
                  pl.BlockSpec((R, 128), lambda n: (n, 0)),
                  pl.BlockSpec((128, GLA_DK), lambda n: (0, 0)),
                  pl.BlockSpec((1, GLA_DK), lambda n: (0, 0))],
        out_specs=(pl.BlockSpec((R, 1024), lambda n: (n, 0)),
                   pl.BlockSpec((n_c, GLA_HEADS, GLA_HV, GLA_HK), lambda n: (n, 0, 0, 0)),
                   pl.BlockSpec((R, GLA_DK), lambda n: (n, 0))),
        out_shape=(jax.ShapeDtypeStruct((T, 1024), F32),
                   jax.ShapeDtypeStruct((n_chunks, GLA_HEADS, GLA_HV, GLA_HK), F32),
                   jax.ShapeDtypeStruct((T, GLA_DK), F32)),
        scratch_shapes=[pltpu.VMEM((GLA_HEADS, GLA_HV, GLA_HK), F32)],
        compiler_params=_cparams(("arbitrary",)),
    )(projf, projb, rank, wdec, bdec)


def _gla_bwd_call(projf, projb, la, do_gla, st_all, rank, wdec):
    T = projf.shape[1]
    C = GLA_CHUNK
    n_chunks = T // C
    n_c = GLA_STEP_CHUNKS
    R = n_c * C
    assert n_chunks % n_c == 0
    last = n_chunks // n_c - 1

    def body(qk_ref, v_ref, la_ref, do_ref, st_ref, rank_ref, wd_ref,
             dqk_ref, dv_ref, drank_ref, dwd_ref, dbd_ref, dst_scr):
        @pl.when(pl.program_id(0) == 0)
        def _():
            dst_scr[...] = jnp.zeros_like(dst_scr)
            dwd_ref[...] = jnp.zeros_like(dwd_ref)
            dbd_ref[...] = jnp.zeros_like(dbd_ref)

        same = _gla_same_chunk(R)
        mask = same & (_iota2(R, R, 0) >= _iota2(R, R, 1))
        upp = _bf((same & (_iota2(R, R, 0) <= _iota2(R, R, 1))).astype(F32))
        scale = GLA_HK ** -0.5
        la = la_ref[...]
        bl, eb, enb, ebl_b, qe, ke, kd = _gla_chunk_terms(la, qk_ref[:, :GLA_DK], qk_ref[:, GLA_DK:], n_c)
        qeb, keb, kdb = _bf(qe), _bf(ke), _bf(kd)
        ebl = [jnp.exp(bl[c]) for c in range(n_c)]
        heads = range(GLA_HEADS)
        ks = [slice(hh * GLA_HK, (hh + 1) * GLA_HK) for hh in heads]
        vs = [slice(hh * GLA_HV, (hh + 1) * GLA_HV) for hh in heads]
        rs = [slice(c * C, (c + 1) * C) for c in range(n_c)]
        v = [v_ref[:, vs[hh]] for hh in heads]
        do = [_bf(do_ref[:, vs[hh]]) for hh in heads]
        p = [_bf(jnp.where(mask, _dot_nt(qeb[:, ks[hh]], keb[:, ks[hh]]), 0.0)) for hh in heads]
        dp = [_bf(jnp.where(mask, _dot_nt(do[hh], v[hh]), 0.0)) for hh in heads]
        dst_intra = [[_dot_tn(do[hh][rs[c]], qeb[rs[c], ks[hh]]) for hh in heads] for c in range(n_c)]
        dqe_inter = [[_dot(do[hh][rs[c]], _bf(st_ref[c, hh])) for hh in heads] for c in range(n_c)]
        dv_intra = [_dot_tn(p[hh], do[hh]) for hh in heads]
        dqe_intra = [_dot(dp[hh], keb[:, ks[hh]]) for hh in heads]
        dke = jnp.concatenate([_dot_tn(dp[hh], qeb[:, ks[hh]]) for hh in heads], axis=1)
        dstn = [dst_scr[hh] for hh in heads]
        dkd_c, dv_inter, debl = [None] * n_c, [None] * n_c, [None] * n_c
        for c in reversed(range(n_c)):
            dstnb = [_bf(dstn[hh]) for hh in heads]
            dkd_c[c] = jnp.concatenate([_dot(v[hh][rs[c]], dstnb[hh]) for hh in heads], axis=1)
            dv_inter[c] = [_dot_nt(kdb[rs[c], ks[hh]], dstnb[hh]) for hh in heads]
            debl[c] = jnp.concatenate(
                [jnp.sum(dstn[hh] * st_ref[c, hh], axis=0, keepdims=True) for hh in heads], axis=1)
            dstn = [dst_intra[c][hh] + dstn[hh] * ebl[c][:, ks[hh]] for hh in heads]
        for hh in heads:
            dst_scr[hh] = dstn[hh]
            dv_ref[:, vs[hh]] = _bf(dv_intra[hh] + jnp.concatenate([dv_inter[c][hh] for c in range(n_c)], axis=0))
        dqe = jnp.concatenate(
            [dqe_intra[hh] + jnp.concatenate([dqe_inter[c][hh] for c in range(n_c)], axis=0) for hh in heads], axis=1)
        dkd = jnp.concatenate(dkd_c, axis=0)
        dkd_kd = dkd * kd
        db = dqe * qe - dke * ke - dkd_kd
        dbl = jnp.concatenate(
            [jnp.broadcast_to(jnp.sum(dkd_kd[rs[c]], axis=0, keepdims=True) + ebl[c] * debl[c], (C, GLA_DK))
             for c in range(n_c)], axis=0)
        dla = _tri_left(upp, db) + dbl
        dqk_ref[:, :GLA_DK] = _bf(dqe * eb * scale)
        dqk_ref[:, GLA_DK:] = _bf(dke * enb + dkd * ebl_b)
        ddec = dla * (1.0 / GLA_TAU) * (1.0 - jnp.exp(GLA_TAU * la))
        ddecb = _bf(ddec)
        drank_ref[...] = _bf(_dot_nt(ddecb, _bf(wd_ref[...])))
        dwd_ref[...] += _dot_tn(_bf(rank_ref[...]), ddecb)
        dbd_ref[...] += jnp.sum(ddec, axis=0, keepdims=True)

    return pl.pallas_call(
        body, name="gla_bwd",
        grid=(n_chunks // n_c,),
        in_specs=[pl.BlockSpec((None, R, 1024), lambda n: (0, last - n, 0)),
                  pl.BlockSpec((None, R, 1024), lambda n: (0, last - n, 0)),
                  pl.BlockSpec((R, GLA_DK), lambda n: (last - n, 0)),
                  pl.BlockSpec((R, 1024), lambda n: (last - n, 0)),
                  pl.BlockSpec((n_c, GLA_HEADS, GLA_HV, GLA_HK), lambda n: (last - n, 0, 0, 0)),
                  pl.BlockSpec((R, 128), lambda n: (last - n, 0)),
                  pl.BlockSpec((128, GLA_DK), lambda n: (0, 0))],
        out_specs=(pl.BlockSpec((R, 1024), lambda n: (last - n, 0)),
                   pl.BlockSpec((R, 1024), lambda n: (last - n, 0)),
                   pl.BlockSpec((R, 128), lambda n: (last - n, 0)),
                   pl.BlockSpec((128, GLA_DK), lambda n: (0, 0)),
                   pl.BlockSpec((1, GLA_DK), lambda n: (0, 0))),
        out_shape=(jax.ShapeDtypeStruct((T, 1024), BF16),
                   jax.ShapeDtypeStruct((T, 1024), BF16),
                   jax.ShapeDtypeStruct((T, 128), BF16),
                   jax.ShapeDtypeStruct((128, GLA_DK), F32),
                   jax.ShapeDtypeStruct((1, GLA_DK), F32)),
        scratch_shapes=[pltpu.VMEM((GLA_HEADS, GLA_HV, GLA_HK), F32)],
        compiler_params=_cparams(("arbitrary",)),
    )(projf, projb, la, do_gla, st_all, rank, wdec)


def _sb_logs(z):
    lsz = jnp.minimum(z, 0.0) - _softplus_neg_abs(z)
    return lsz, lsz - z


SB_HG_FWD = 8
SB_HG_BWD = 4
SB_QUERIES = 256
SB_KEYS = 256
SB_DEAD = -105.0


def _sb_fwd_call(projb, wp_shard):
    T = projb.shape[1]
    B = min(SB_QUERIES, T)
    HG = SB_HG_FWD
    W = HG * SB_HD
    scale = 1.0 / math.sqrt(SB_HD)
    KB = min(SB_KEYS, T)
    n_h, n_i = SB_HEADS // HG, T // B

    def body(q_ref, k_ref, v_ref, wp_ref, o_ref, wpall_ref, cb_scr, send_sems, recv_sems, loc_sem):
        i = pl.program_id(1)
        own, pairs = _push_copies(wp_ref, wpall_ref, send_sems, recv_sems, loc_sem, scatter=False)

        @pl.when((pl.program_id(0) == 0) & (i == 0))
        def _():
            _push_start(own, pairs)

        rows = HG * B
        after = (_iota2(KB, KB, 0) > _iota2(KB, KB, 1)).astype(F32)
        tri = _bf(jnp.concatenate([after, jnp.ones((KB, KB), F32)], axis=1))
        o_ref[...] = jnp.zeros_like(o_ref)
        cb_scr[...] = jnp.zeros_like(cb_scr)

        def block(jp, masked):
            off = pl.multiple_of(jp * KB, KB)
            z = jnp.concatenate(
                [_dot_nt(q_ref[:, hh * SB_HD:(hh + 1) * SB_HD], k_ref[pl.ds(off, KB), hh * SB_HD:(hh + 1) * SB_HD])
                 for hh in range(HG)], axis=0) * scale
            lsz, l1m = _sb_logs(z)
            if masked:
                strict = (jp * KB + _iota2(rows, KB, 1)) < (i * B + (_iota2(rows, KB, 0) & (B - 1)))
                l1m = jnp.where(strict, l1m, 0.0)
            r = _tri2_right(l1m, tri)
            cb = cb_scr[...]
            a = jnp.exp(lsz + cb + r[:, :KB])
            if masked:
                a = jnp.where(strict, a, 0.0)
            cb_scr[...] = cb + r[:, KB:]
            ab = _bf(a)
            for hh in range(HG):
                cs = slice(hh * SB_HD, (hh + 1) * SB_HD)
                o_ref[:, cs] += _dot(ab[hh * B:(hh + 1) * B, :], v_ref[pl.ds(off, KB), cs])

        jp0 = (i * B) // KB
        block(jp0, True)

        def live(state):
            jj, dead = state
            return (jj <= jp0) & jnp.logical_not(dead)

        def step(state):
            jj, _ = state
            block(jp0 - jj, False)
            return jj + 1, jnp.max(cb_scr[:, :SB_HD]) < SB_DEAD

        lax.while_loop(live, step, (jnp.int32(1), jnp.max(cb_scr[:, :SB_HD]) < SB_DEAD))

        @pl.when((pl.program_id(0) == n_h - 1) & (i == n_i - 1))
        def _():
            _push_wait(own, pairs)

    return pl.pallas_call(
        body, name="sb_fwd",
        grid=(n_h, n_i),
        in_specs=[pl.BlockSpec((None, B, W), lambda h, i: (1, i, h)),
                  pl.BlockSpec((None, T, W), lambda h, i: (2, 0, h)),
                  pl.BlockSpec((None, T, W), lambda h, i: (3, 0, h)),
                  _ANY],
        out_specs=(pl.BlockSpec((B, W), lambda h, i: (i, h)), _ANY),
        out_shape=(jax.ShapeDtypeStruct((T, 1024), F32),
                   jax.ShapeDtypeStruct((N_DEV,) + wp_shard.shape, wp_shard.dtype)),
        scratch_shapes=[pltpu.VMEM((HG * B, KB), F32)] + _PUSH_SEMS,
        compiler_params=_cparams(("arbitrary", "arbitrary")),
    )(projb, projb, projb, wp_shard)


def _sb_bwd_call(projb, do_sb, g_p):
    T = projb.shape[1]
    B = min(SB_QUERIES, T)
    nb = T // B
    HG = SB_HG_BWD
    W = HG * SB_HD
    WQ = HG * B
    KB = min(SB_KEYS, T)
    nkb = T // KB
    n_h = SB_HEADS // HG
    scale = 1.0 / math.sqrt(SB_HD)

    def body(q_ref, k_ref, v_ref, do_ref, gp_ref, dq_ref, dk_ref, dv_ref, rp_ref,
             dk_scr, dv_scr, kt_scr, beta_scr, g_scr, dqt_scr, send_sems, recv_sems, loc_sem):
        i = pl.program_id(1)
        own, pairs = _push_copies(gp_ref, rp_ref, send_sems, recv_sems, loc_sem, scatter=True)

        @pl.when((pl.program_id(0) == 0) & (i == 0))
        def _():
            _push_start(own, pairs)

        @pl.when(i == 0)
        def _():
            dk_scr[...] = jnp.zeros_like(dk_scr)
            dv_scr[...] = jnp.zeros_like(dv_scr)
            for hh in range(HG):
                for jb in range(nkb):
                    kt_scr[hh, jb] = _bf(
                        k_ref[jb * KB:(jb + 1) * KB, hh * SB_HD:(hh + 1) * SB_HD].astype(F32).T)

        dqt_scr[...] = jnp.zeros_like(dqt_scr)
        later = _bf((_iota2(KB, KB, 1) > _iota2(KB, KB, 0)).astype(F32))
        earlier = _bf((_iota2(KB, KB, 1) < _iota2(KB, KB, 0)).astype(F32))
        dob = _bf(do_ref[...])
        jp0 = (i * B) // KB

        def strict_mask():
            return (jp0 * KB + _iota2(KB, WQ, 0)) < (i * B + (_iota2(KB, WQ, 1) & (B - 1)))

        def heads(fn):
            return [fn(slice(hh * SB_HD, (hh + 1) * SB_HD)) for hh in range(HG)]

        def pass1(jp, cb, masked):
            off = pl.multiple_of(jp * KB, KB)
            z = jnp.concatenate(heads(lambda cs: _dot_nt(k_ref[pl.ds(off, KB), cs], q_ref[:, cs])), axis=1) * scale
            da = jnp.concatenate(heads(lambda cs: _dot_nt(v_ref[pl.ds(off, KB), cs], dob[:, cs])), axis=1)
            lsz, l1m = _sb_logs(z)
            if masked:
                strict = strict_mask()
                l1m = jnp.where(strict, l1m, 0.0)
            a = jnp.exp(lsz + cb + _tri2_left(later, l1m))
            if masked:
                a = jnp.where(strict, a, 0.0)
            g_scr[jp] = a * da
            beta_scr[jp] = jnp.exp(lsz)
            ab = _bf(a)
            for hh in range(HG):
                cs = slice(hh * SB_HD, (hh + 1) * SB_HD)
                dv_scr[pl.ds(off, KB), cs] += _dot(ab[:, hh * B:(hh + 1) * B], dob[:, cs])
            return cb + jnp.sum(l1m, axis=0, keepdims=True)

        zero = jnp.zeros((1, WQ), F32)
        cb = pass1(jp0, zero, True)

        def live(state):
            jj, _, dead = state
            return (jj <= jp0) & jnp.logical_not(dead)

        def step(state):
            jj, cr, _ = state
            cr = pass1(jp0 - jj, cr, False)
            return jj + 1, cr, jnp.max(cr) < SB_DEAD

        n_done, _, _ = lax.while_loop(live, step, (jnp.int32(1), cb, jnp.max(cb) < SB_DEAD))
        jp_first = jp0 - (n_done - 1)

        def pass2(jp, cg, masked):
            off = pl.multiple_of(jp * KB, KB)
            g = g_scr[jp]
            beta = beta_scr[jp]
            dz = g * (1.0 - beta) - beta * (cg + _tri2_left(earlier, g))
            if masked:
                dz = jnp.where(strict_mask(), dz, 0.0)
            dzb = _bf(dz * scale)
            for hh in range(HG):
                cs = slice(hh * SB_HD, (hh + 1) * SB_HD)
                dk_scr[pl.ds(off, KB), cs] += _dot(dzb[:, hh * B:(hh + 1) * B], q_ref[:, cs])
                dqt_scr[hh] += _dot(kt_scr[hh, jp], dzb[:, hh * B:(hh + 1) * B])
            return cg + jnp.sum(g, axis=0, keepdims=True)

        cg = lax.fori_loop(jp_first, jp0, lambda jp, cr: pass2(jp, cr, False), zero)
        pass2(jp0, cg, True)
        for hh in range(HG):
            dq_ref[:, hh * SB_HD:(hh + 1) * SB_HD] = _bf(dqt_scr[hh].T)

        @pl.when(i == nb - 1)
        def _():
            dk_ref[...] = _bf(dk_scr[...])
            dv_ref[...] = _bf(dv_scr[...])

        @pl.when((pl.program_id(0) == n_h - 1) & (i == nb - 1))
        def _():
            _push_wait(own, pairs)

    return pl.pallas_call(
        body, name="sb_bwd",
        grid=(n_h, nb),
        in_specs=[pl.BlockSpec((None, B, W), lambda h, i: (1, i, h)),
                  pl.BlockSpec((None, T, W), lambda h, i: (2, 0, h)),
                  pl.BlockSpec((None, T, W), lambda h, i: (3, 0, h)),
                  pl.BlockSpec((B, W), lambda h, i: (i, h)),
                  _ANY],
        out_specs=(pl.BlockSpec((B, W), lambda h, i: (i, h)),
                   pl.BlockSpec((T, W), lambda h, i: (0, h)),
                   pl.BlockSpec((T, W), lambda h, i: (0, h)),
                   _ANY),
        out_shape=(jax.ShapeDtypeStruct((T, 1024), BF16),
                   jax.ShapeDtypeStruct((T, 1024), BF16),
                   jax.ShapeDtypeStruct((T, 1024), BF16),
                   jax.ShapeDtypeStruct(g_p.shape, g_p.dtype)),
        scratch_shapes=[pltpu.VMEM((T, W), F32), pltpu.VMEM((T, W), F32),
                        pltpu.VMEM((HG, nkb, SB_HD, KB), BF16),
                        pltpu.VMEM((nkb, KB, WQ), F32), pltpu.VMEM((nkb, KB, WQ), F32),
                        pltpu.VMEM((HG, SB_HD, B), F32)] + _PUSH_SEMS,
        compiler_params=_cparams(("arbitrary", "arbitrary")),
    )(projb, projb, projb, do_sb, g_p)


def _mid_call(o_gla, o_sb, projf, x, target, wpa, wpb, wo, gla_g, b_gate, final_g):
    T, D = x.shape
    tm = min(TBLK, T)

    def body(og_ref, ggate_ref, osb_ref, sgate_ref, ma_ref, mb_ref, x_ref, tgt_ref,
             wpa_ref, wpb_ref, wo_ref, glag_ref, bg_ref, fg_ref,
             dx2_ref, dogla_ref, dosb_ref, dggate_ref, dsgate_ref, dm_ref,
             mt_ref, ogt_ref, obt_ref, dx2b_ref, dya_ref, dyb_ref,
             dfg_ref, dbg_ref, dglag_ref, loss_ref):
        @pl.when(pl.program_id(0) == 0)
        def _():
            dfg_ref[...] = jnp.zeros_like(dfg_ref)
            dbg_ref[...] = jnp.zeros_like(dbg_ref)
            dglag_ref[...] = jnp.zeros_like(dglag_ref)
            loss_ref[...] = jnp.zeros_like(loss_ref)

        glag = glag_ref[...]
        ggate = ggate_ref[...]
        sg = _sigmoid(ggate)
        silu_g = ggate * sg
        ohat, rinv, nrm = [], [], []
        for hh in range(GLA_HEADS):
            oh = og_ref[:, hh * GLA_HV:(hh + 1) * GLA_HV]
            r = lax.rsqrt(jnp.mean(oh * oh, axis=-1, keepdims=True) + EPS)
            ohat.append(oh * r)
            rinv.append(r)
            nrm.append(ohat[-1] * glag)
        n_all = jnp.concatenate(nrm, axis=1)
        og = n_all * silu_g
        ogb = _bf(og)
        ya = _dot(ogb, wpa_ref[...])
        sgate = sgate_ref[...]
        ss = _sigmoid(sgate)
        silu_s = sgate * ss
        osb = osb_ref[...]
        ob = osb * silu_s
        obb = _bf(ob)
        yb = _dot(obb, wpb_ref[...])
        ga = _sigmoid(ma_ref[...] + bg_ref[:, :D])
        gb = _sigmoid(mb_ref[...] + bg_ref[:, D:])
        merged = ga * ya + gb * yb
        mgb = _bf(merged)
        x2 = x_ref[...] + _dot(mgb, wo_ref[...])
        r2 = lax.rsqrt(jnp.mean(x2 * x2, axis=-1, keepdims=True) + EPS)
        xh2 = x2 * r2
        fg = fg_ref[...]
        err = xh2 * fg - tgt_ref[...]
        loss_ref[...] += jnp.broadcast_to(
            0.5 * jnp.sum(jnp.mean(err * err, axis=-1, keepdims=True), axis=0, keepdims=True), (1, 128))
        dy = err * (1.0 / D)
        dfg_ref[...] += jnp.sum(dy * xh2, axis=0, keepdims=True)
        dxh = dy * fg
        dx2 = r2 * (dxh - xh2 * jnp.mean(dxh * xh2, axis=-1, keepdims=True))
        dx2_ref[...] = dx2
        dx2b = _bf(dx2)
        dx2b_ref[...] = dx2b
        dmerged = _dot_nt(dx2b, wo_ref[...])
        dya = dmerged * ga
        dyb = dmerged * gb
        dma = dmerged * ya * ga * (1.0 - ga)
        dmb = dmerged * yb * gb * (1.0 - gb)
        dm_ref[:, :D] = _bf(dma)
        dm_ref[:, D:] = _bf(dmb)
        dbg_ref[:, :D] += jnp.sum(dma, axis=0, keepdims=True)
        dbg_ref[:, D:] += jnp.sum(dmb, axis=0, keepdims=True)
        dyab = _bf(dya)
        dybb = _bf(dyb)
        dya_ref[...] = dyab
        dyb_ref[...] = dybb
        dog = _dot_nt(dyab, wpa_ref[...])
        dob = _dot_nt(dybb, wpb_ref[...])
        dosb_ref[...] = dob * silu_s
        dsgate_ref[...] = _bf(dob * osb * (ss * (1.0 + sgate * (1.0 - ss))))
        dn = dog * silu_g
        dggate_ref[...] = _bf(dog * n_all * (sg * (1.0 + ggate * (1.0 - sg))))
        dglag = jnp.zeros((1, GLA_HV), F32)
        for hh in range(GLA_HEADS):
            dnh = dn[:, hh * GLA_HV:(hh + 1) * GLA_HV]
            dglag = dglag + jnp.sum(dnh * ohat[hh], axis=0, keepdims=True)
            dohat = dnh * glag
            dogla_ref[:, hh * GLA_HV:(hh + 1) * GLA_HV] = rinv[hh] * (
                dohat - ohat[hh] * jnp.mean(dohat * ohat[hh], axis=-1, keepdims=True))
        dglag_ref[...] += dglag
        mt_ref[...] = _bf(merged.T)
        ogt_ref[...] = _bf(og.T)
        obt_ref[...] = _bf(ob.T)

    row = lambda i: (i, 0)
    const = lambda i: (0, 0)
    tile = pl.BlockSpec((tm, D), row)
    tile_t = pl.BlockSpec((None, D, tm), lambda i: (i, 0, 0))
    wspec = pl.BlockSpec((D, D), const)
    return pl.pallas_call(
        body, name="mid",
        grid=(T // tm,),
        in_specs=[tile,
                  pl.BlockSpec((None, tm, D), lambda i: (1, i, 0)),
                  tile,
                  pl.BlockSpec((None, tm, D), lambda i: (2, i, 0)),
                  pl.BlockSpec((None, tm, D), lambda i: (3, i, 0)),
                  pl.BlockSpec((None, tm, D), lambda i: (4, i, 0)),
                  tile, tile, wspec, wspec, wspec,
                  pl.BlockSpec((1, GLA_HV), const),
                  pl.BlockSpec((1, 2 * D), const),
                  pl.BlockSpec((1, D), const)],
        out_specs=(tile, tile, tile, tile, tile,
                   pl.BlockSpec((tm, 2 * D), row),
                   tile_t, tile_t, tile_t, tile, tile, tile,
                   pl.BlockSpec((1, D), const),
                   pl.BlockSpec((1, 2 * D), const),
                   pl.BlockSpec((1, GLA_HV), const),
                   pl.BlockSpec((1, 128), const)),
        out_shape=(jax.ShapeDtypeStruct((T, D), F32),
                   jax.ShapeDtypeStruct((T, D), F32),
                   jax.ShapeDtypeStruct((T, D), F32),
                   jax.ShapeDtypeStruct((T, D), BF16),
                   jax.ShapeDtypeStruct((T, D), BF16),
                   jax.ShapeDtypeStruct((T, 2 * D), BF16),
                   jax.ShapeDtypeStruct((T // tm, D, tm), BF16),
                   jax.ShapeDtypeStruct((T // tm, D, tm), BF16),
                   jax.ShapeDtypeStruct((T // tm, D, tm), BF16),
                   jax.ShapeDtypeStruct((T, D), BF16),
                   jax.ShapeDtypeStruct((T, D), BF16),
                   jax.ShapeDtypeStruct((T, D), BF16),
                   jax.ShapeDtypeStruct((1, D), F32),
                   jax.ShapeDtypeStruct((1, 2 * D), F32),
                   jax.ShapeDtypeStruct((1, GLA_HV), F32),
                   jax.ShapeDtypeStruct((1, 128), F32)),
        compiler_params=_cparams(("arbitrary",)),
    )(o_gla, projf, o_sb, projf, projf, projf, x, target, wpa, wpb, wo, gla_g, b_gate, final_g)


def _dh_call(pieces, dmlog, drank, wt, wr, x, dx2, norm_g, s_in, small):
    T, D = x.shape
    tm = min(256, T)
    npc = len(pieces)
    n_main = N_GROUPS * 1024
    n_i = T // tm
    i_forward = 5 * n_i // 8

    def body(*refs):
        pcs = refs[:npc]
        (dm_ref, dr_ref, w_hbm, wr_ref, x_ref, dx2_ref, g_ref, sin_ref, small_ref,
         gx_ref, rin_ref, relayed_ref, rsmall_ref,
         w_scr, sems, dg_ref, small_mine, small_send, small_recv, small_loc, *exchange_scratch) = refs[npc:]
        start, forward, finish = _chip_reduce_steps(sin_ref, rin_ref, relayed_ref, *exchange_scratch)

        @pl.when(pl.program_id(0) == 0)
        def _():
            start()
            lo = pltpu.make_async_copy(w_hbm.at[pl.ds(0, RANK_COL)], w_scr.at[pl.ds(0, RANK_COL)], sems.at[0])
            hi = pltpu.make_async_copy(w_hbm.at[pl.ds(RANK_COL + GLA_RANK, n_main - RANK_COL)],
                                       w_scr.at[pl.ds(RANK_COL, n_main - RANK_COL)], sems.at[1])
            lo.start()
            hi.start()
            dg_ref[...] = jnp.zeros_like(dg_ref)
            lo.wait()
            hi.wait()

        @pl.when(pl.program_id(0) == i_forward)
        def _():
            forward()

        def w_group(g):
            return w_scr[g * 1024:(g + 1) * 1024, :]

        dr = dr_ref[...]
        dh = _dot(dr, wr_ref[...])
        for g in range(npc):
            dh = dh + _dot(pcs[g][...], w_group(g))
        dh = dh + _dot(dm_ref[:, :D], w_group(npc))
        dh = dh + _dot(dm_ref[:, D:], w_group(npc + 1))
        xv = x_ref[...]
        r = lax.rsqrt(jnp.mean(xv * xv, axis=-1, keepdims=True) + EPS)
        xhat = xv * r
        g = g_ref[...]
        dg_ref[...] += jnp.sum(dh * xhat, axis=0, keepdims=True)
        dxhat = dh * g
        gx_ref[...] = r * (dxhat - xhat * jnp.mean(dxhat * xhat, axis=-1, keepdims=True)) + dx2_ref[...]

        @pl.when(pl.program_id(0) == n_i - 1)
        def _():
            small_mine[...] = small_ref[...]
            small_mine[:, _SM_NORM:_SM_NORM + D] = dg_ref[...]
            own, pairs = _push_copies(small_mine, rsmall_ref, small_send, small_recv, small_loc, scatter=False)
            _push_start(own, pairs)
            finish()
            _push_wait(own, pairs)

    row = lambda i: (i, 0)
    const = lambda i: (0, 0)
    tile = pl.BlockSpec((tm, D), row)
    part = s_in.shape[1:]
    return pl.pallas_call(
        body, name="dh",
        grid=(n_i,),
        in_specs=[tile] * npc + [
            pl.BlockSpec((tm, 2 * D), row),
            pl.BlockSpec((tm, 128), row),
            _ANY,
            pl.BlockSpec((128, D), const),
            tile, tile,
            pl.BlockSpec((1, D), const),
            _ANY,
            pl.BlockSpec(small.shape, const)],
        out_specs=(tile, _ANY, _ANY, _ANY),
        out_shape=(jax.ShapeDtypeStruct((T, D), F32),
                   jax.ShapeDtypeStruct((3,) + part, s_in.dtype),
                   jax.ShapeDtypeStruct(part, s_in.dtype),
                   jax.ShapeDtypeStruct((N_DEV,) + small.shape, small.dtype)),
        scratch_shapes=[pltpu.VMEM((n_main, D), BF16), pltpu.SemaphoreType.DMA((2,)),
                        pltpu.VMEM((1, D), F32), pltpu.VMEM(small.shape, small.dtype)]
        + _PUSH_SEMS + _chip_reduce_scratch(*part, s_in.dtype),
        compiler_params=_cparams(("arbitrary",)),
    )(*pieces, dmlog, drank, wt, wr, x, dx2, norm_g, s_in, small)


def _wgrad_call(lhs_list, lhs_of_group, rhs_list, rhs_of_group, n_transposed, name, narrow=None):
    n_groups = len(rhs_of_group)
    n_tb, D, tb = lhs_list[0].shape
    T = n_tb * tb
    per = min(4, n_tb)
    tk = per * tb
    nk = T // tk
    nl = len(lhs_list)
    extra = [] if narrow is None else [narrow]

    def tokens_side_by_side(lref):
        return jnp.concatenate([lref[b] for b in range(per)], axis=1)

    def body(*refs):
        lhs = refs[:nl]
        rhs = refs[nl:nl + n_groups]
        rest = refs[nl + n_groups:]
        g = pl.program_id(0)
        i = pl.program_id(1)
        if narrow is None:
            out_ref, acc = rest
        else:
            narrow_ref, out_ref, narrow_out, acc, narrow_acc = rest

            @pl.when((g == 0) & (i == 0))
            def _():
                narrow_acc[...] = jnp.zeros_like(narrow_acc)

            @pl.when(g == 0)
            def _():
                narrow_acc[...] += _dot(tokens_side_by_side(lhs[lhs_of_group[0]]), narrow_ref[...])

            @pl.when((g == 0) & (i == nk - 1))
            def _():
                narrow_out[...] = _bf(narrow_acc[...].T)

        @pl.when(i == 0)
        def _():
            acc[...] = jnp.zeros_like(acc)

        for p in range(n_groups):
            @pl.when(g == p)
            def _(p=p):
                acc[...] += _dot(tokens_side_by_side(lhs[lhs_of_group[p]]), rhs[p][...])

        @pl.when((i == nk - 1) & (g < n_transposed))
        def _():
            out_ref[...] = _bf(acc[...].T)

        @pl.when((i == nk - 1) & (g >= n_transposed))
        def _():
            out_ref[...] = _bf(acc[...])

    def lhs_spec(a):
        groups = [g for g in range(n_groups) if lhs_of_group[g] == a]
        lo, hi = min(groups), max(groups)
        assert groups == list(range(lo, hi + 1))
        return pl.BlockSpec((per, D, tb), lambda g, i: (jnp.where((g >= lo) & (g <= hi), i, 0), 0, 0))

    def rhs_spec(p):
        cb = rhs_of_group[p][1]
        return pl.BlockSpec((tk, 1024), lambda g, i: (jnp.where(g == p, i, 0), cb))

    res = pl.pallas_call(
        body, name=name,
        grid=(n_groups, nk),
        in_specs=[lhs_spec(a) for a in range(nl)] + [rhs_spec(p) for p in range(n_groups)]
        + [pl.BlockSpec((tk, 128), lambda g, i: (jnp.where(g == 0, i, 0), 0)) for _ in extra],
        out_specs=[pl.BlockSpec((None, D, 1024), lambda g, i: (g, 0, 0))]
        + [pl.BlockSpec((128, D), lambda g, i: (0, 0)) for _ in extra],
        out_shape=[jax.ShapeDtypeStruct((n_groups, D, 1024), BF16)]
        + [jax.ShapeDtypeStruct((128, D), BF16) for _ in extra],
        scratch_shapes=[pltpu.VMEM((D, 1024), F32)] + [pltpu.VMEM((D, 128), F32) for _ in extra],
        compiler_params=_cparams(("arbitrary", "arbitrary")),
    )(*lhs_list, *[rhs_list[rhs_of_group[p][0]] for p in range(n_groups)], *extra)
    return res[0] if narrow is None else res


def _adamw_math(parts, w, m, v):
    g = parts[0].astype(F32)
    for p in parts[1:]:
        g = g + p.astype(F32)
    mm = ADAM_B1 * m + (1.0 - ADAM_B1) * g
    vv = ADAM_B2 * v + (1.0 - ADAM_B2) * (g * g)
    m_hat = mm / (1.0 - ADAM_B1 ** ADAM_STEP)
    v_hat = vv / (1.0 - ADAM_B2 ** ADAM_STEP)
    return g, -ADAM_LR * (m_hat / (jnp.sqrt(v_hat) + ADAM_EPS) + ADAM_WD * w), mm, vv


def _part_order(n_parts):
    return [n_parts - 1] + list(range(n_parts - 1))


def _adamw_call(parts, w, m, v, name):
    R, C = w.shape
    n_parts = parts.shape[0]
    (tr, tc), grid, idx = _tiling_2d(R, C, 512)

    def body(p_ref, w_ref, m_ref, v_ref, g_ref, d_ref, nm_ref, nv_ref):
        g_ref[...], d_ref[...], nm_ref[...], nv_ref[...] = _adamw_math(
            [p_ref[k] for k in _part_order(n_parts)], w_ref[...], m_ref[...], v_ref[...])

    blk = pl.BlockSpec((tr, tc), idx)
    sds = jax.ShapeDtypeStruct((R, C), F32)
    return pl.pallas_call(
        body, name=name,
        grid=grid,
        in_specs=[pl.BlockSpec((n_parts, tr, tc), lambda i: (0,) + idx(i)), blk, blk, blk],
        out_specs=(blk, blk, blk, blk),
        out_shape=(sds, sds, sds, sds),
        compiler_params=_cparams(("arbitrary",)),
    )(parts, w, m, v)


def _adamw_rows_call(parts, ws, ms, vs, name, side_parts, side_jobs):
    n = len(ws)
    R, C = ws[0].shape
    n_parts = parts.shape[0]
    n_sp, n_jobs = len(side_parts), len(side_jobs)
    side_w = [job[2] for job in side_jobs]

    def body(*refs):
        p_ref = refs[0]
        w_refs, m_refs, v_refs = refs[1:1 + n], refs[1 + n:1 + 2 * n], refs[1 + 2 * n:1 + 3 * n]
        sp_refs = refs[1 + 3 * n:1 + 3 * n + n_sp]
        sw_refs = refs[1 + 3 * n + n_sp:1 + 3 * n + n_sp + 3 * n_jobs]
        outs = refs[1 + 3 * n + n_sp + 3 * n_jobs:]
        for k in range(n):
            @pl.when(pl.program_id(0) == k)
            def _(k=k):
                res = _adamw_math([p_ref[j] for j in _part_order(n_parts)],
                                  w_refs[k][...], m_refs[k][...], v_refs[k][...])
                for o_ref, val in zip(outs[4 * k:4 * k + 4], res):
                    o_ref[...] = val

        @pl.when(pl.program_id(0) == 0)
        def _():
            for q, (i, lanes, _, _, _) in enumerate(side_jobs):
                src = sp_refs[i]
                order = _part_order(src.shape[0])
                res = _adamw_math([src[j] if lanes is None else src[j, :, lanes] for j in order],
                                  *[r[...] for r in sw_refs[3 * q:3 * q + 3]])
                for o_ref, val in zip(outs[4 * (n + q):4 * (n + q) + 4], res):
                    o_ref[...] = val

    def resident(a):
        return pl.BlockSpec(a.shape, lambda k, nd=a.ndim: (0,) * nd)

    whole = pl.BlockSpec((R, C), lambda k: (0, 0))
    sds = jax.ShapeDtypeStruct((R, C), F32)
    side_wmv = [a for job in side_jobs for a in job[2:]]
    res = pl.pallas_call(
        body, name=name,
        grid=(n,),
        in_specs=[pl.BlockSpec((n_parts, R, C), lambda k: (0, k, 0))] + [whole] * (3 * n)
        + [resident(a) for a in side_parts] + [resident(a) for a in side_wmv],
        out_specs=tuple([whole] * (4 * n) + [resident(w) for w in side_w for _ in range(4)]),
        out_shape=tuple([sds] * (4 * n) + [jax.ShapeDtypeStruct(w.shape, F32) for w in side_w for _ in range(4)]),
        compiler_params=_cparams(("arbitrary",)),
    )(parts, *ws, *ms, *vs, *side_parts, *side_wmv)
    return [res[4 * k:4 * k + 4] for k in range(n + n_jobs)]


def _local_step(x, target, wt, wr, wdec, bdec, wp_shard, norm_g, gla_g, b_gate, final_g):
    D = x.shape[1]
    half = wp_shard.shape[1] // 2
    projf, projb, rank, ht, wp_lo = _proj_call(x, norm_g, wt, wr, wp_shard[:, :half])
    o_gla, st_all, la = _gla_fwd_call(projf, projb, rank, wdec, bdec)
    o_sb, wp_hi = _sb_fwd_call(projb, wp_shard[:, half:])
    wp_full = jnp.concatenate([wp_lo, wp_hi], axis=2).transpose(1, 0, 2, 3).reshape(3, D, D)
    (dx2, do_gla, do_sb, dggate, dsgate, dmlog, mt, ogt, obt, dx2b, dya, dyb,
     dfinal_g, db_gate, dgla_g, loss) = _mid_call(o_gla, o_sb, projf, x, target, wp_full[0], wp_full[1],
                                                 wp_full[2], gla_g, b_gate, final_g)
    dw_p = _wgrad_call([ogt, obt, mt], [0, 1, 2], [dya, dyb, dx2b], [(0, 0), (1, 0), (2, 0)], 0, "wgrad_p")
    g_p = dw_p.reshape(3, N_DEV, D // N_DEV, D).transpose(1, 0, 2, 3).reshape(N_DEV, 3 * (D // N_DEV), D)
    dqk, dgv, drank, dwdec, dbdec = _gla_bwd_call(projf, projb, la, do_gla, st_all, rank, wdec)
    dsq, dsk, dsv, r_p = _sb_bwd_call(projb, do_sb, g_p)
    pieces = [dqk, dgv, dggate, dsq, dsk, dsv, dsgate]
    rhs_of_group = [(g, 0) for g in range(7)] + [(7, 0), (7, 1)]
    dw_in, dwr = _wgrad_call([ht], [0] * N_GROUPS, pieces + [dmlog], rhs_of_group, N_GROUPS, "wgrad_in",
                             narrow=drank)
    s_in = _pair_sum_call(dw_in.reshape(N_GROUPS * 1024, D), dwr)
    small = jnp.concatenate([
        jnp.zeros((D,), F32), dbdec.reshape(-1), dgla_g.reshape(-1), db_gate.reshape(-1), dfinal_g.reshape(-1),
        loss.reshape(-1), dwdec[:GLA_RANK].reshape(-1)]).reshape(1, _SM_LEN)
    grad_x, r_in, _, r_small = _dh_call(pieces, dmlog, drank, wt, wr, x, dx2, norm_g, s_in, small)
    return grad_x, r_in, r_p, r_small


_SM_NORM = 0
_SM_BDEC = _SM_NORM + D_MODEL
_SM_GLAG = _SM_BDEC + GLA_DK
_SM_BGATE = _SM_GLAG + GLA_HV
_SM_FINAL = _SM_BGATE + 2 * D_MODEL
_SM_REPL = _SM_FINAL + D_MODEL
_SM_LOSS = _SM_REPL
_SM_WDEC = _SM_LOSS + 128
_SM_LEN = _SM_WDEC + GLA_RANK * GLA_DK


def kernel(x, norm_g, w_in, w_dec_up, b_dec, gla_norm_g, w_pa, w_pb, b_gate, w_o, final_g, loss_target, m_norm_g, m_w_in, m_w_dec_up, m_b_dec, m_gla_norm_g, m_w_pa, m_w_pb, m_b_gate, m_w_o, m_final_g, v_norm_g, v_w_in, v_w_dec_up, v_b_dec, v_gla_norm_g, v_w_pa, v_w_pb, v_b_gate, v_w_o, v_final_g):
    D = D_MODEL
    me = 4 * lax.axis_index("x") + 2 * lax.axis_index("y") + lax.axis_index("c")

    wp_shard = jnp.stack([w_pa, w_pb, w_o]).astype(BF16)
    n_first = _half_rows(SHARD_COLS)
    win_all, wdec_all = _all_gather([w_in.T.astype(BF16), w_dec_up], "gather_w",
                                    row_pieces=[[(0, n_first), (n_first, SHARD_COLS - n_first)], None])
    wt = _flatten_blocks_call(win_all)
    wr = jnp.pad(wt[RANK_COL:RANK_COL + GLA_RANK], ((0, 128 - GLA_RANK), (0, 0)))
    wdec_full = wdec_all.transpose(1, 0, 2).reshape(GLA_RANK, GLA_DK)
    wdec = jnp.pad(wdec_full, ((0, 128 - GLA_RANK), (0, 0)))

    grad_x, r_in, r_p, r_small = _local_step(
        x[0], loss_target[0], wt, wr, wdec, b_dec.reshape(1, -1), wp_shard,
        norm_g.reshape(1, -1), gla_norm_g.reshape(1, -1), b_gate.reshape(1, -1), final_g.reshape(1, -1))

    gw_in, d_in, nm_in, nv_in = (a.T for a in _adamw_call(r_in, w_in.T, m_w_in.T, v_w_in.T, "adamw_in"))
    wdec_parts = r_small[:, 0, _SM_WDEC:].reshape(N_DEV, GLA_RANK, GLA_DK)
    cols = GLA_DK // N_DEV
    wdec_mine = lax.dynamic_slice_in_dim(wdec_parts, me * cols, cols, axis=2)

    def row(a):
        return a.reshape(1, -1)

    rep_jobs = [(0, slice(at, at + w.size), row(w), row(m), row(v)) for at, w, m, v in (
        (_SM_NORM, norm_g, m_norm_g, v_norm_g), (_SM_BDEC, b_dec, m_b_dec, v_b_dec),
        (_SM_GLAG, gla_norm_g, m_gla_norm_g, v_gla_norm_g), (_SM_BGATE, b_gate, m_b_gate, v_b_gate),
        (_SM_FINAL, final_g, m_final_g, v_final_g))]
    quads = _adamw_rows_call(
        r_p, [w_pa, w_pb, w_o], [m_w_pa, m_w_pb, m_w_o], [v_w_pa, v_w_pb, v_w_o], "adamw_p",
        [r_small, wdec_mine], rep_jobs + [(1, None, w_dec_up, m_w_dec_up, v_w_dec_up)])
    (g_pa, d_pa, nm_pa, nv_pa), (g_pb, d_pb, nm_pb, nv_pb), (g_o, d_o, nm_o, nv_o) = quads[:3]
    ((g_norm, d_norm, nm_norm, nv_norm), (g_bdec, d_bdec, nm_bdec, nv_bdec), (g_glag, d_glag, nm_glag, nv_glag),
     (g_bgate, d_bgate, nm_bgate, nv_bgate), (g_final, d_final, nm_final, nv_final)) = [
        tuple(a.reshape(-1) for a in quad) for quad in quads[3:8]]
    g_wdec, d_wdec, nm_wdec, nv_wdec = quads[8]

    loss_total = jnp.sum(r_small[:, 0, _SM_LOSS])

    return (loss_total, grad_x[None],
            g_norm, gw_in, g_wdec, g_bdec, g_glag, g_pa, g_pb, g_bgate, g_o, g_final,
            d_norm, d_in, d_wdec, d_bdec, d_glag, d_pa, d_pb, d_bgate, d_o, d_final,
            nm_norm, nm_in, nm_wdec, nm_bdec, nm_glag, nm_pa, nm_pb, nm_bgate, nm_o, nm_final,
            nv_norm, nv_in, nv_wdec, nv_bdec, nv_glag, nv_pa, nv_pb, nv_bgate, nv_o, nv_final)
```

```python
import math

import jax
import jax.numpy as jnp
from jax import lax
from jax.experimental import pallas as pl
from jax.experimental.pallas import tpu as pltpu

F32 = jnp.float32
BF16 = jnp.bfloat16

N_DEV = 8
D_MODEL = 1024
GLA_HEADS = 4
GLA_HK = 128
GLA_HV = 256
GLA_DK = 512
GLA_RANK = 16
GLA_TAU = 16.0
GLA_CHUNK = 64
SB_HEADS = 8
SB_HD = 128
EPS = 1e-6
N_GROUPS = 9
RANK_COL = 3072
IN_COLS = 9232
SHARD_COLS = IN_COLS // N_DEV

ADAM_LR = 0.001
ADAM_B1 = 0.9
ADAM_B2 = 0.999
ADAM_EPS = 1e-08
ADAM_WD = 0.01
ADAM_STEP = 10

VMEM_LIMIT = 56 * 1024 * 1024
TBLK = 256


def _cparams(sem=None):
    return pltpu.CompilerParams(dimension_semantics=sem, vmem_limit_bytes=VMEM_LIMIT)


def _tiling_2d(rows, cols, band_cols):
    if rows * cols <= 128 * 1024:
        return (rows, cols), (1,), lambda i: (0, 0)
    if rows % 128 == 0:
        return (128, cols), (rows // 128,), lambda i: (i, 0)
    tc = band_cols if cols % band_cols == 0 else cols
    return (rows, tc), (cols // tc,), lambda i: (0, i)


def _dot(a, b):
    return jnp.dot(a, b, preferred_element_type=F32)


def _dot_nt(a, b):
    return lax.dot_general(a, b, (((1,), (1,)), ((), ())), preferred_element_type=F32)


def _dot_tn(a, b):
    return lax.dot_general(a, b, (((0,), (0,)), ((), ())), preferred_element_type=F32)


def _bf(x):
    return x.astype(BF16)


def _split3(x):
    hi = x.astype(BF16)
    r = x - hi.astype(F32)
    mid = r.astype(BF16)
    lo = (r - mid.astype(F32)).astype(BF16)
    return hi, mid, lo


def _tri_left(tri, x):
    hi, mid, lo = _split3(x)
    return _dot(tri, hi) + _dot(tri, mid) + _dot(tri, lo)


def _split2(x):
    hi = lax.bitcast_convert_type(lax.bitcast_convert_type(x, jnp.uint32) & jnp.uint32(0xFFFF0000), F32)
    return hi.astype(BF16), (x - hi).astype(BF16)


def _tri2_left(tri, x):
    hi, lo = _split2(x)
    return _dot(tri, hi) + _dot(tri, lo)


def _tri2_right(x, tri):
    hi, lo = _split2(x)
    return _dot(hi, tri) + _dot(lo, tri)


def _iota2(n, m, dim):
    return lax.broadcasted_iota(jnp.int32, (n, m), dim)


def _sigmoid(x):
    return 1.0 / (1.0 + jnp.exp(-x))


def _softplus_neg_abs(z):
    return jnp.log(1.0 + jnp.exp(-jnp.abs(z)))


_ANY = pl.BlockSpec(memory_space=pl.ANY)


def _mesh_pos():
    return lax.axis_index("x"), lax.axis_index("y"), lax.axis_index("c")


def _other_chips(x, y):
    return [(1 - x, y), (x, 1 - y), (1 - x, 1 - y)]


def _rcopy(src, dst, send_sem, recv_sem, to):
    return pltpu.make_async_remote_copy(src_ref=src, dst_ref=dst, send_sem=send_sem, recv_sem=recv_sem,
                                        device_id=to, device_id_type=pl.DeviceIdType.MESH)


def _push_copies(src_ref, dst_ref, send_sems, recv_sems, loc_sem, scatter):
    x, y, c = _mesh_pos()
    me = 4 * x + 2 * y + c
    own = pltpu.make_async_copy(src_ref.at[me] if scatter else src_ref, dst_ref.at[me], loc_sem)
    pairs = []
    for k in range(1, N_DEV):
        px = 1 - x if k & 4 else x
        py = 1 - y if k & 2 else y
        pc = 1 - c if k & 1 else c
        pid = 4 * px + 2 * py + pc
        src = src_ref.at[pid] if scatter else src_ref
        send = _rcopy(src, dst_ref.at[me], send_sems.at[k - 1], recv_sems.at[k - 1], (px, py, pc))
        recv = _rcopy(src, dst_ref.at[pid], send_sems.at[k - 1], recv_sems.at[k - 1], (px, py, pc))
        pairs.append((send, recv))
    return own, pairs


def _push_start(own, pairs):
    own.start()
    for send, _ in pairs:
        send.start()


def _push_wait(own, pairs):
    for _, recv in pairs:
        recv.wait_recv()
    for send, _ in pairs:
        send.wait_send()
    own.wait()


_PUSH_SEMS = [pltpu.SemaphoreType.DMA((N_DEV - 1,)), pltpu.SemaphoreType.DMA((N_DEV - 1,)),
              pltpu.SemaphoreType.DMA]


def _half_rows(rows):
    return (rows // 2) // 16 * 16


_ADD_ROWS = 128


def _chip_reduce_steps(src_ref, dst_ref, relayed_ref, sum_x, sum_y, rel_x, rel_y, load_sems, send_sems, recv_sems,
                       loc_sem):
    _, R, C = src_ref.shape
    n0 = _half_rows(R)
    lo, hi = pl.ds(0, n0), pl.ds(n0, R - n0)
    x, y, c = _mesh_pos()
    (xx, xy), (yx, yy), (dx, dy) = _other_chips(x, y)
    to_diag, to_x, to_y = src_ref.at[2 * dx + dy], src_ref.at[2 * xx + xy], src_ref.at[2 * yx + yy]
    x_nb, y_nb = (xx, xy, c), (yx, yy, c)
    relays = (_rcopy(to_diag.at[lo], relayed_ref.at[lo], send_sems.at[0], recv_sems.at[0], x_nb),
              _rcopy(to_diag.at[hi], relayed_ref.at[hi], send_sems.at[1], recv_sems.at[1], y_nb))
    plain = (_rcopy(to_x.at[lo], dst_ref.at[0, lo], send_sems.at[2], recv_sems.at[2], x_nb),
             _rcopy(to_y.at[hi], dst_ref.at[1, hi], send_sems.at[3], recv_sems.at[3], y_nb))
    summed = (_rcopy(sum_x, dst_ref.at[0, hi], send_sems.at[4], recv_sems.at[4], x_nb),
              _rcopy(sum_y, dst_ref.at[1, lo], send_sems.at[5], recv_sems.at[5], y_nb))
    load_mine = (pltpu.make_async_copy(to_x.at[hi], sum_x, load_sems.at[0]),
                 pltpu.make_async_copy(to_y.at[lo], sum_y, load_sems.at[1]))
    load_relayed = (pltpu.make_async_copy(relayed_ref.at[hi], rel_x, load_sems.at[2]),
                    pltpu.make_async_copy(relayed_ref.at[lo], rel_y, load_sems.at[3]))
    own = pltpu.make_async_copy(src_ref.at[2 * x + y], dst_ref.at[2], loc_sem)

    def start():
        for cp in relays + plain + (own,) + load_mine:
            cp.start()

    def add(acc_ref, rel_ref):
        for r0 in range(0, acc_ref.shape[0], _ADD_ROWS):
            rows = slice(r0, min(r0 + _ADD_ROWS, acc_ref.shape[0]))
            acc_ref[rows, :] = (acc_ref[rows, :].astype(F32) + rel_ref[rows, :].astype(F32)).astype(acc_ref.dtype)

    def forward():
        for cp in relays:
            cp.wait_recv()
        for cp in load_relayed:
            cp.start()
        for cp in load_mine + load_relayed:
            cp.wait()
        add(sum_x, rel_x)
        add(sum_y, rel_y)
        for cp in summed:
            cp.start()

    def finish():
        for cp in plain + summed:
            cp.wait_recv()
        for cp in relays + plain + summed:
            cp.wait_send()
        own.wait()

    return start, forward, finish


def _chip_reduce_scratch(rows, cols, dtype):
    n0 = _half_rows(rows)
    return [pltpu.VMEM((rows - n0, cols), dtype), pltpu.VMEM((n0, cols), dtype)] * 2 + [
        pltpu.SemaphoreType.DMA((4,)), pltpu.SemaphoreType.DMA((6,)), pltpu.SemaphoreType.DMA((6,)),
        pltpu.SemaphoreType.DMA]


def _all_gather(arrs, name, row_pieces=None):
    n = len(arrs)
    pieces = [[None] if not row_pieces or not row_pieces[a] else list(row_pieces[a]) for a in range(n)]
    assert all(len(p) in (1, 2) for p in pieces)
    units = [(a, i) for a in range(n) for i in range(len(pieces[a]))]

    def body(*refs):
        ins = refs[:n]
        outs = refs[n:2 * n]
        send_sems, recv_sems, loc_sems = refs[2 * n:]
        x, y, c = _mesh_pos()
        me, sib = (x, y, c), (x, y, 1 - c)
        xn, yn, dg = [(px, py, c) for px, py in _other_chips(x, y)]

        def rows(ref, a, i):
            return ref if pieces[a][i] is None else ref.at[pl.ds(*pieces[a][i])]

        def copy(u, k, block, to, own=False):
            a, i = u
            px, py, pc = block
            dst = rows(outs[a].at[4 * px + 2 * py + pc], a, i)
            return _rcopy(rows(ins[a], a, i) if own else dst, dst, send_sems.at[a, k, i], recv_sems.at[a, k, i], to)

        started = []

        def start(cp):
            cp.start()
            started.append(cp)

        def landed_then_pass_on(u, k, block):
            copy(u, k, block, me).wait_recv()
            start(copy(u, 3 + k, block, sib))

        mine = [pltpu.make_async_copy(ins[a], outs[a].at[4 * x + 2 * y + c], loc_sems.at[a]) for a in range(n)]
        for cp in mine:
            cp.start()
        for u in units:
            start(copy(u, 0, me, sib, own=True))
        for a in range(n):
            if len(pieces[a]) == 2:
                for i, to, k in ((0, xn, 1), (1, yn, 2), (1, xn, 1), (0, yn, 2)):
                    start(copy((a, i), k, me, to, own=True))
            else:
                for to, k in ((xn, 1), (yn, 2), (dg, 3)):
                    start(copy((a, 0), k, me, to, own=True))
        for a in range(n):
            if len(pieces[a]) == 2:
                landed_then_pass_on((a, 0), 1, xn)
                start(copy((a, 0), 3, xn, yn))
                landed_then_pass_on((a, 1), 2, yn)
                start(copy((a, 1), 3, yn, xn))
                landed_then_pass_on((a, 1), 1, xn)
                landed_then_pass_on((a, 0), 2, yn)
                landed_then_pass_on((a, 0), 3, dg)
                landed_then_pass_on((a, 1), 3, dg)
            else:
                for block, k in ((xn, 1), (yn, 2), (dg, 3)):
                    landed_then_pass_on((a, 0), k, block)
        for u in units:
            copy(u, 0, sib, me).wait_recv()
            for k, (px, py, _) in ((4, xn), (5, yn), (6, dg)):
                copy(u, k, (px, py, 1 - c), me).wait_recv()
        for cp in started:
            cp.wait_send()
        for cp in mine:
            cp.wait()

    n_pc = max(len(p) for p in pieces)

    return pl.pallas_call(
        body, name=name,
        out_shape=tuple(jax.ShapeDtypeStruct((N_DEV,) + a.shape, a.dtype) for a in arrs),
        in_specs=[_ANY] * n,
        out_specs=tuple([_ANY] * n),
        scratch_shapes=[pltpu.SemaphoreType.DMA((n, 7, n_pc)), pltpu.SemaphoreType.DMA((n, 7, n_pc)),
                        pltpu.SemaphoreType.DMA((n,))],
    )(*arrs)


def _flatten_blocks_call(blocks):
    n, R, C = blocks.shape
    tc = C // 2

    def body(in_ref, out_ref):
        for p in range(n):
            out_ref[p * R:(p + 1) * R, :] = in_ref[p]

    return pl.pallas_call(
        body, name="flatten_w",
        grid=(C // tc,),
        in_specs=[pl.BlockSpec((n, R, tc), lambda i: (0, 0, i))],
        out_specs=pl.BlockSpec((n * R, tc), lambda i: (0, i)),
        out_shape=jax.ShapeDtypeStruct((n * R, C), blocks.dtype),
        compiler_params=_cparams(("arbitrary",)),
    )(blocks)


_PARTS_BANDS = 4


def _pair_sum_call(dmain, drank):
    D = dmain.shape[1]
    n = _PARTS_BANDS
    tc = D // n

    def body(dm_ref, dr_ref, sum_ref, laid, got, send_sems, recv_sems):
        x, y, c = _mesh_pos()

        def pushes(k):
            return [_rcopy(laid.at[k % 2, 2 * q + (1 - c)], got.at[k, q], send_sems.at[k, q], recv_sems.at[k, q],
                           (x, y, 1 - c)) for q in range(4)]

        def lay_out(k):
            for p in range(N_DEV):
                lo, hi = p * SHARD_COLS, (p + 1) * SHARD_COLS
                at = 0
                for src, a, b in ((dm_ref, lo, min(hi, RANK_COL)),
                                  (dr_ref, max(lo, RANK_COL) - RANK_COL, min(hi, RANK_COL + GLA_RANK) - RANK_COL),
                                  (dm_ref, max(lo, RANK_COL + GLA_RANK) - GLA_RANK, hi - GLA_RANK)):
                    if b > a:
                        laid[k % 2, p, at:at + (b - a), :] = src[a:b, :]
                        at += b - a

        for k in range(n + 1):
            @pl.when(pl.program_id(0) == k)
            def _(k=k):
                if k < n:
                    if k >= 2:
                        for cp in pushes(k - 2):
                            cp.wait_send()
                    lay_out(k)
                    for cp in pushes(k):
                        cp.start()
                if k >= 1:
                    for cp in pushes(k - 1):
                        cp.wait_recv()
                    for q in range(4):
                        sum_ref[q] = (laid[(k - 1) % 2, 2 * q + c].astype(F32)
                                      + got[k - 1, q].astype(F32)).astype(sum_ref.dtype)
                if k == n:
                    for k_open in range(max(0, n - 2), n):
                        for cp in pushes(k_open):
                            cp.wait_send()

    sems = pltpu.SemaphoreType.DMA((n, 4))
    return pl.pallas_call(
        body, name="pair_sum",
        grid=(n + 1,),
        in_specs=[pl.BlockSpec((dmain.shape[0], tc), lambda k: (0, jnp.minimum(k, n - 1))),
                  pl.BlockSpec((GLA_RANK, tc), lambda k: (0, jnp.minimum(k, n - 1)))],
        out_specs=pl.BlockSpec((4, SHARD_COLS, tc), lambda k: (0, 0, jnp.maximum(k - 1, 0))),
        out_shape=jax.ShapeDtypeStruct((4, SHARD_COLS, D), dmain.dtype),
        scratch_shapes=[pltpu.VMEM((2, N_DEV, SHARD_COLS, tc), dmain.dtype),
                        pltpu.VMEM((n, 4, SHARD_COLS, tc), dmain.dtype), sems, sems],
        compiler_params=_cparams(("arbitrary",)),
    )(dmain, drank)


def _group_row(g):
    return GLA_RANK * (g * (1024 // GLA_RANK) + (g >= RANK_COL // 1024))


def _proj_call(x, norm_g, wt, wr, wp_part):
    T, D = x.shape
    tm = min(1024, T)
    assert tm % TBLK == 0
    n_i = T // tm

    def f_slot(j):
        return ((j >= 2).astype(jnp.int32) + (j >= 6).astype(jnp.int32)
                + (j >= 7).astype(jnp.int32) + (j >= 8).astype(jnp.int32))

    def b_slot(j):
        return (j >= 3).astype(jnp.int32) + (j >= 4).astype(jnp.int32) + (j >= 5).astype(jnp.int32)

    def body(x_ref, g_ref, w_ref, wr_ref, wp_ref, pf_ref, pb_ref, rank_ref, ht_ref, wpall_ref,
             h_scr, send_sems, recv_sems, loc_sem):
        i = pl.program_id(0)
        j = pl.program_id(1)
        own, pairs = _push_copies(wp_ref, wpall_ref, send_sems, recv_sems, loc_sem, scatter=False)

        @pl.when((i == 0) & (j == 0))
        def _():
            _push_start(own, pairs)

        @pl.when(j == 0)
        def _():
            xv = x_ref[...]
            r = lax.rsqrt(jnp.mean(xv * xv, axis=-1, keepdims=True) + EPS)
            h = (xv * r) * g_ref[...]
            hb = _bf(h)
            h_scr[...] = hb
            for b in range(tm // TBLK):
                ht_ref[b] = _bf(h[b * TBLK:(b + 1) * TBLK].T)
            rank_ref[...] = _dot_nt(hb, wr_ref[...])

        is_b = (j == 1) | ((j >= 3) & (j <= 5))

        @pl.when(is_b)
        def _():
            pb_ref[...] = _bf(_dot_nt(h_scr[...], w_ref[...]))

        @pl.when(jnp.logical_not(is_b))
        def _():
            pf_ref[...] = _dot_nt(h_scr[...], w_ref[...])

        @pl.when((i == n_i - 1) & (j == N_GROUPS - 1))
        def _():
            _push_wait(own, pairs)

    return pl.pallas_call(
        body, name="proj",
        grid=(n_i, N_GROUPS),
        in_specs=[pl.BlockSpec((tm, D), lambda i, j: (i, 0)),
                  pl.BlockSpec((1, D), lambda i, j: (0, 0)),
                  pl.BlockSpec((pl.Element(1024), pl.Element(D)), lambda i, j: (_group_row(j), 0)),
                  pl.BlockSpec((128, D), lambda i, j: (0, 0)),
                  _ANY],
        out_specs=(pl.BlockSpec((None, tm, 1024), lambda i, j: (f_slot(j), i, 0)),
                   pl.BlockSpec((None, tm, 1024), lambda i, j: (b_slot(j), i, 0)),
                   pl.BlockSpec((tm, 128), lambda i, j: (i, 0)),
                   pl.BlockSpec((tm // TBLK, D, TBLK), lambda i, j: (i, 0, 0)),
                   _ANY),
        out_shape=(jax.ShapeDtypeStruct((5, T, 1024), F32),
                   jax.ShapeDtypeStruct((4, T, 1024), BF16),
                   jax.ShapeDtypeStruct((T, 128), F32),
                   jax.ShapeDtypeStruct((T // TBLK, D, TBLK), BF16),
                   jax.ShapeDtypeStruct((N_DEV,) + wp_part.shape, wp_part.dtype)),
        scratch_shapes=[pltpu.VMEM((tm, D), BF16)] + _PUSH_SEMS,
        compiler_params=_cparams(("arbitrary", "arbitrary")),
    )(x, norm_g, wt, wr, wp_part)


GLA_STEP_CHUNKS = 4


def _gla_same_chunk(rows):
    return (_iota2(rows, rows, 0) & -GLA_CHUNK) == (_iota2(rows, rows, 1) & -GLA_CHUNK)


def _gla_chunk_terms(la, q, k, n_c):
    C = GLA_CHUNK
    rows = n_c * C
    low = _gla_same_chunk(rows) & (_iota2(rows, rows, 0) >= _iota2(rows, rows, 1))
    b = _tri_left(_bf(low.astype(F32)), la)
    bl = [b[(c + 1) * C - 1:(c + 1) * C, :] for c in range(n_c)]
    bl_rows = jnp.concatenate([jnp.broadcast_to(bl[c], (C, b.shape[1])) for c in range(n_c)], axis=0)
    eb = jnp.exp(b)
    enb = jnp.exp(-b)
    ebl_b = jnp.exp(bl_rows - b)
    scale = GLA_HK ** -0.5
    qe = q * eb * scale
    ke = k * enb
    kd = k * ebl_b
    return bl, eb, enb, ebl_b, qe, ke, kd


def _gla_fwd_call(projf, projb, rank, wdec, bdec):
    T = projf.shape[1]
    C = GLA_CHUNK
    n_chunks = T // C
    n_c = GLA_STEP_CHUNKS
    R = n_c * C
    assert n_chunks % n_c == 0

    def body(qk_ref, v_ref, rank_ref, wd_ref, bd_ref, o_ref, st_ref, la_ref, st_scr):
        @pl.when(pl.program_id(0) == 0)
        def _():
            st_scr[...] = jnp.zeros_like(st_scr)

        dec = _dot(_bf(rank_ref[...]), _bf(wd_ref[...])) + bd_ref[...]
        la = (jnp.minimum(dec, 0.0) - _softplus_neg_abs(dec)) / GLA_TAU
        la_ref[...] = la
        mask = _gla_same_chunk(R) & (_iota2(R, R, 0) >= _iota2(R, R, 1))
        bl, _, _, _, qe, ke, kd = _gla_chunk_terms(la, qk_ref[:, :GLA_DK], qk_ref[:, GLA_DK:], n_c)
        qeb, keb, kdb = _bf(qe), _bf(ke), _bf(kd)
        ebl = [jnp.exp(bl[c]) for c in range(n_c)]
        heads = range(GLA_HEADS)
        ks = [slice(hh * GLA_HK, (hh + 1) * GLA_HK) for hh in heads]
        vs = [slice(hh * GLA_HV, (hh + 1) * GLA_HV) for hh in heads]
        rs = [slice(c * C, (c + 1) * C) for c in range(n_c)]
        p = [_bf(jnp.where(mask, _dot_nt(qeb[:, ks[hh]], keb[:, ks[hh]]), 0.0)) for hh in heads]
        upd = [[_dot_tn(v_ref[rs[c], vs[hh]], kdb[rs[c], ks[hh]]) for hh in heads] for c in range(n_c)]
        intra = [_dot(p[hh], v_ref[:, vs[hh]]) for hh in heads]
        st = [st_scr[hh] for hh in heads]
        for c in range(n_c):
            inter = [_dot_nt(qeb[rs[c], ks[hh]], _bf(st[hh])) for hh in heads]
            for hh in heads:
                st_ref[c, hh] = st[hh]
                o_ref[rs[c], vs[hh]] = intra[hh][rs[c]] + inter[hh]
            st = [st[hh] * ebl[c][:, ks[hh]] + upd[c][hh] for hh in heads]
        for hh in heads:
            st_scr[hh] = st[hh]

    return pl.pallas_call(
        body, name="gla_fwd",
        grid=(n_chunks // n_c,),
        in_specs=[pl.BlockSpec((None, R, 1024), lambda n: (0, n, 0)),
                  pl.BlockSpec((None, R, 1024), lambda n: (0, n, 0)),
                  pl.BlockSpec((R, 128), lambda n: (n, 0)),
                  pl.BlockSpec((128, GLA_DK), lambda n: (0, 0)),
                  pl.BlockSpec((1, GLA_DK), lambda n: (0, 0))],
        out_specs=(pl.BlockSpec((R, 1024), lambda n: (n, 0)),
                   pl.BlockSpec((n_c, GLA_HEADS, GLA_HV, GLA_HK), lambda n: (n, 0, 0, 0)),
                   pl.BlockSpec((R, GLA_DK), lambda n: (n, 0))),
        out_shape=(jax.ShapeDtypeStruct((T, 1024), F32),
                   jax.ShapeDtypeStruct((n_chunks, GLA_HEADS, GLA_HV, GLA_HK), F32),
                   jax.ShapeDtypeStruct((T, GLA_DK), F32)),
        scratch_shapes=[pltpu.VMEM((GLA_HEADS, GLA_HV, GLA_HK), F32)],
        compiler_params=_cparams(("arbitrary",)),
    )(projf, projb, rank, wdec, bdec)


def _gla_bwd_call(projf, projb, la, do_gla, st_all, rank, wdec):
    T = projf.shape[1]
    C = GLA_CHUNK
    n_chunks = T // C
    n_c = GLA_STEP_CHUNKS
    R = n_c * C
    assert n_chunks % n_c == 0
    last = n_chunks // n_c - 1

    def body(qk_ref, v_ref, la_ref, do_ref, st_ref, rank_ref, wd_ref,
             dqk_ref, dv_ref, drank_ref, dwd_ref, dbd_ref, dst_scr):
        @pl.when(pl.program_id(0) == 0)
        def _():
            dst_scr[...] = jnp.zeros_like(dst_scr)
            dwd_ref[...] = jnp.zeros_like(dwd_ref)
            dbd_ref[...] = jnp.zeros_like(dbd_ref)

        same = _gla_same_chunk(R)
        mask = same & (_iota2(R, R, 0) >= _iota2(R, R, 1))
        upp = _bf((same & (_iota2(R, R, 0) <= _iota2(R, R, 1))).astype(F32))
        scale = GLA_HK ** -0.5
        la = la_ref[...]
        bl, eb, enb, ebl_b, qe, ke, kd = _gla_chunk_terms(la, qk_ref[:, :GLA_DK], qk_ref[:, GLA_DK:], n_c)
        qeb, keb, kdb = _bf(qe), _bf(ke), _bf(kd)
        ebl = [jnp.exp(bl[c]) for c in range(n_c)]
        heads = range(GLA_HEADS)
        ks = [slice(hh * GLA_HK, (hh + 1) * GLA_HK) for hh in heads]
        vs = [slice(hh * GLA_HV, (hh + 1) * GLA_HV) for hh in heads]
        rs = [slice(c * C, (c + 1) * C) for c in range(n_c)]
        v = [v_ref[:, vs[hh]] for hh in heads]
        do = [_bf(do_ref[:, vs[hh]]) for hh in heads]
        p = [_bf(jnp.where(mask, _dot_nt(qeb[:, ks[hh]], keb[:, ks[hh]]), 0.0)) for hh in heads]
        dp = [_bf(jnp.where(mask, _dot_nt(do[hh], v[hh]), 0.0)) for hh in heads]
        dst_intra = [[_dot_tn(do[hh][rs[c]], qeb[rs[c], ks[hh]]) for hh in heads] for c in range(n_c)]
        dqe_inter = [[_dot(do[hh][rs[c]], _bf(st_ref[c, hh])) for hh in heads] for c in range(n_c)]
        dv_intra = [_dot_tn(p[hh], do[hh]) for hh in heads]
        dqe_intra = [_dot(dp[hh], keb[:, ks[hh]]) for hh in heads]
        dke = jnp.concatenate([_dot_tn(dp[hh], qeb[:, ks[hh]]) for hh in heads], axis=1)
        dstn = [dst_scr[hh] for hh in heads]
        dkd_c, dv_inter, debl = [None] * n_c, [None] * n_c, [None] * n_c
        for c in reversed(range(n_c)):
            dstnb = [_bf(dstn[hh]) for hh in heads]
            dkd_c[c] = jnp.concatenate([_dot(v[hh][rs[c]], dstnb[hh]) for hh in heads], axis=1)
            dv_inter[c] = [_dot_nt(kdb[rs[c], ks[hh]], dstnb[hh]) for hh in heads]
            debl[c] = jnp.concatenate(
                [jnp.sum(dstn[hh] * st_ref[c, hh], axis=0, keepdims=True) for hh in heads], axis=1)
            dstn = [dst_intra[c][hh] + dstn[hh] * ebl[c][:, ks[hh]] for hh in heads]
        for hh in heads:
            dst_scr[hh] = dstn[hh]
            dv_ref[:, vs[hh]] = _bf(dv_intra[hh] + jnp.concatenate([dv_inter[c][hh] for c in range(n_c)], axis=0))
        dqe = jnp.concatenate(
            [dqe_intra[hh] + jnp.concatenate([dqe_inter[c][hh] for c in range(n_c)], axis=0) for hh in heads], axis=1)
        dkd = jnp.concatenate(dkd_c, axis=0)
        dkd_kd = dkd * kd
        db = dqe * qe - dke * ke - dkd_kd
        dbl = jnp.concatenate(
            [jnp.broadcast_to(jnp.sum(dkd_kd[rs[c]], axis=0, keepdims=True) + ebl[c] * debl[c], (C, GLA_DK))
             for c in range(n_c)], axis=0)
        dla = _tri_left(upp, db) + dbl
        dqk_ref[:, :GLA_DK] = _bf(dqe * eb * scale)
        dqk_ref[:, GLA_DK:] = _bf(dke * enb + dkd * ebl_b)
        ddec = dla * (1.0 / GLA_TAU) * (1.0 - jnp.exp(GLA_TAU * la))
        ddecb = _bf(ddec)
        drank_ref[...] = _bf(_dot_nt(ddecb, _bf(wd_ref[...])))
        dwd_ref[...] += _dot_tn(_bf(rank_ref[...]), ddecb)
        dbd_ref[...] += jnp.sum(ddec, axis=0, keepdims=True)

    return pl.pallas_call(
        body, name="gla_bwd",
        grid=(n_chunks // n_c,),
        in_specs=[pl.BlockSpec((None, R, 1024), lambda n: (0, last - n, 0)),
                  pl.BlockSpec((None, R, 1024), lambda n: (0, last - n, 0)),
                  pl.BlockSpec((R, GLA_DK), lambda n: (last - n, 0)),
                  pl.BlockSpec((R, 1024), lambda n: (last - n, 0)),
                  pl.BlockSpec((n_c, GLA_HEADS, GLA_HV, GLA_HK), lambda n: (last - n, 0, 0, 0)),
                  pl.BlockSpec((R, 128), lambda n: (last - n, 0)),
                  pl.BlockSpec((128, GLA_DK), lambda n: (0, 0))],
        out_specs=(pl.BlockSpec((R, 1024), lambda n: (last - n, 0)),
                   pl.BlockSpec((R, 1024), lambda n: (last - n, 0)),
                   pl.BlockSpec((R, 128), lambda n: (last - n, 0)),
                   pl.BlockSpec((128, GLA_DK), lambda n: (0, 0)),
                   pl.BlockSpec((1, GLA_DK), lambda n: (0, 0))),
        out_shape=(jax.ShapeDtypeStruct((T, 1024), BF16),
                   jax.ShapeDtypeStruct((T, 1024), BF16),
                   jax.ShapeDtypeStruct((T, 128), BF16),
                   jax.ShapeDtypeStruct((128, GLA_DK), F32),
                   jax.ShapeDtypeStruct((1, GLA_DK), F32)),
        scratch_shapes=[pltpu.VMEM((GLA_HEADS, GLA_HV, GLA_HK), F32)],
        compiler_params=_cparams(("arbitrary",)),
    )(projf, projb, la, do_gla, st_all, rank, wdec)


def _sb_logs(z):
    lsz = jnp.minimum(z, 0.0) - _softplus_neg_abs(z)
    return lsz, lsz - z


SB_HG_FWD = 8
SB_HG_BWD = 4
SB_QUERIES = 256
SB_KEYS = 256
SB_DEAD = -105.0


def _sb_fwd_call(projb, wp_shard):
    T = projb.shape[1]
    B = min(SB_QUERIES, T)
    HG = SB_HG_FWD
    W = HG * SB_HD
    scale = 1.0 / math.sqrt(SB_HD)
    KB = min(SB_KEYS, T)
    n_h, n_i = SB_HEADS // HG, T // B

    def body(q_ref, k_ref, v_ref, wp_ref, o_ref, wpall_ref, cb_scr, send_sems, recv_sems, loc_sem):
        i = pl.program_id(1)
        own, pairs = _push_copies(wp_ref, wpall_ref, send_sems, recv_sems, loc_sem, scatter=False)

        @pl.when((pl.program_id(0) == 0) & (i == 0))
        def _():
            _push_start(own, pairs)

        rows = HG * B
        after = (_iota2(KB, KB, 0) > _iota2(KB, KB, 1)).astype(F32)
        tri = _bf(jnp.concatenate([after, jnp.ones((KB, KB), F32)], axis=1))
        o_ref[...] = jnp.zeros_like(o_ref)
        cb_scr[...] = jnp.zeros_like(cb_scr)

        def block(jp, masked):
            off = pl.multiple_of(jp * KB, KB)
            z = jnp.concatenate(
                [_dot_nt(q_ref[:, hh * SB_HD:(hh + 1) * SB_HD], k_ref[pl.ds(off, KB), hh * SB_HD:(hh + 1) * SB_HD])
                 for hh in range(HG)], axis=0) * scale
            lsz, l1m = _sb_logs(z)
            if masked:
                strict = (jp * KB + _iota2(rows, KB, 1)) < (i * B + (_iota2(rows, KB, 0) & (B - 1)))
                l1m = jnp.where(strict, l1m, 0.0)
            r = _tri2_right(l1m, tri)
            cb = cb_scr[...]
            a = jnp.exp(lsz + cb + r[:, :KB])
            if masked:
                a = jnp.where(strict, a, 0.0)
            cb_scr[...] = cb + r[:, KB:]
            ab = _bf(a)
            for hh in range(HG):
                cs = slice(hh * SB_HD, (hh + 1) * SB_HD)
                o_ref[:, cs] += _dot(ab[hh * B:(hh + 1) * B, :], v_ref[pl.ds(off, KB), cs])

        jp0 = (i * B) // KB
        block(jp0, True)

        def live(state):
            jj, dead = state
            return (jj <= jp0) & jnp.logical_not(dead)

        def step(state):
            jj, _ = state
            block(jp0 - jj, False)
            return jj + 1, jnp.max(cb_scr[:, :SB_HD]) < SB_DEAD

        lax.while_loop(live, step, (jnp.int32(1), jnp.max(cb_scr[:, :SB_HD]) < SB_DEAD))

        @pl.when((pl.program_id(0) == n_h - 1) & (i == n_i - 1))
        def _():
            _push_wait(own, pairs)

    return pl.pallas_call(
        body, name="sb_fwd",
        grid=(n_h, n_i),
        in_specs=[pl.BlockSpec((None, B, W), lambda h, i: (1, i, h)),
                  pl.BlockSpec((None, T, W), lambda h, i: (2, 0, h)),
                  pl.BlockSpec((None, T, W), lambda h, i: (3, 0, h)),
                  _ANY],
        out_specs=(pl.BlockSpec((B, W), lambda h, i: (i, h)), _ANY),
        out_shape=(jax.ShapeDtypeStruct((T, 1024), F32),
                   jax.ShapeDtypeStruct((N_DEV,) + wp_shard.shape, wp_shard.dtype)),
        scratch_shapes=[pltpu.VMEM((HG * B, KB), F32)] + _PUSH_SEMS,
        compiler_params=_cparams(("arbitrary", "arbitrary")),
    )(projb, projb, projb, wp_shard)


def _sb_bwd_call(projb, do_sb, g_p):
    T = projb.shape[1]
    B = min(SB_QUERIES, T)
    nb = T // B
    HG = SB_HG_BWD
    W = HG * SB_HD
    WQ = HG * B
    KB = min(SB_KEYS, T)
    nkb = T // KB
    n_h = SB_HEADS // HG
    scale = 1.0 / math.sqrt(SB_HD)

    def body(q_ref, k_ref, v_ref, do_ref, gp_ref, dq_ref, dk_ref, dv_ref, rp_ref,
             dk_scr, dv_scr, kt_scr, beta_scr, g_scr, dqt_scr, send_sems, recv_sems, loc_sem):
        i = pl.program_id(1)
        own, pairs = _push_copies(gp_ref, rp_ref, send_sems, recv_sems, loc_sem, scatter=True)

        @pl.when((pl.program_id(0) == 0) & (i == 0))
        def _():
            _push_start(own, pairs)

        @pl.when(i == 0)
        def _():
            dk_scr[...] = jnp.zeros_like(dk_scr)
            dv_scr[...] = jnp.zeros_like(dv_scr)
            for hh in range(HG):
                for jb in range(nkb):
                    kt_scr[hh, jb] = _bf(
                        k_ref[jb * KB:(jb + 1) * KB, hh * SB_HD:(hh + 1) * SB_HD].astype(F32).T)

        dqt_scr[...] = jnp.zeros_like(dqt_scr)
        later = _bf((_iota2(KB, KB, 1) > _iota2(KB, KB, 0)).astype(F32))
        earlier = _bf((_iota2(KB, KB, 1) < _iota2(KB, KB, 0)).astype(F32))
        dob = _bf(do_ref[...])
        jp0 = (i * B) // KB

        def strict_mask():
            return (jp0 * KB + _iota2(KB, WQ, 0)) < (i * B + (_iota2(KB, WQ, 1) & (B - 1)))

        def heads(fn):
            return [fn(slice(hh * SB_HD, (hh + 1) * SB_HD)) for hh in range(HG)]

        def pass1(jp, cb, masked):
            off = pl.multiple_of(jp * KB, KB)
            z = jnp.concatenate(heads(lambda cs: _dot_nt(k_ref[pl.ds(off, KB), cs], q_ref[:, cs])), axis=1) * scale
            da = jnp.concatenate(heads(lambda cs: _dot_nt(v_ref[pl.ds(off, KB), cs], dob[:, cs])), axis=1)
            lsz, l1m = _sb_logs(z)
            if masked:
                strict = strict_mask()
                l1m = jnp.where(strict, l1m, 0.0)
            a = jnp.exp(lsz + cb + _tri2_left(later, l1m))
            if masked:
                a = jnp.where(strict, a, 0.0)
            g_scr[jp] = a * da
            beta_scr[jp] = jnp.exp(lsz)
            ab = _bf(a)
            for hh in range(HG):
                cs = slice(hh * SB_HD, (hh + 1) * SB_HD)
                dv_scr[pl.ds(off, KB), cs] += _dot(ab[:, hh * B:(hh + 1) * B], dob[:, cs])
            return cb + jnp.sum(l1m, axis=0, keepdims=True)

        zero = jnp.zeros((1, WQ), F32)
        cb = pass1(jp0, zero, True)

        def live(state):
            jj, _, dead = state
            return (jj <= jp0) & jnp.logical_not(dead)

        def step(state):
            jj, cr, _ = state
            cr = pass1(jp0 - jj, cr, False)
            return jj + 1, cr, jnp.max(cr) < SB_DEAD

        n_done, _, _ = lax.while_loop(live, step, (jnp.int32(1), cb, jnp.max(cb) < SB_DEAD))
        jp_first = jp0 - (n_done - 1)

        def pass2(jp, cg, masked):
            off = pl.multiple_of(jp * KB, KB)
            g = g_scr[jp]
            beta = beta_scr[jp]
            dz = g * (1.0 - beta) - beta * (cg + _tri2_left(earlier, g))
            if masked:
                dz = jnp.where(strict_mask(), dz, 0.0)
            dzb = _bf(dz * scale)
            for hh in range(HG):
                cs = slice(hh * SB_HD, (hh + 1) * SB_HD)
                dk_scr[pl.ds(off, KB), cs] += _dot(dzb[:, hh * B:(hh + 1) * B], q_ref[:, cs])
                dqt_scr[hh] += _dot(kt_scr[hh, jp], dzb[:, hh * B:(hh + 1) * B])
            return cg + jnp.sum(g, axis=0, keepdims=True)

        cg = lax.fori_loop(jp_first, jp0, lambda jp, cr: pass2(jp, cr, False), zero)
        pass2(jp0, cg, True)
        for hh in range(HG):
            dq_ref[:, hh * SB_HD:(hh + 1) * SB_HD] = _bf(dqt_scr[hh].T)

        @pl.when(i == nb - 1)
        def _():
            dk_ref[...] = _bf(dk_scr[...])
            dv_ref[...] = _bf(dv_scr[...])

        @pl.when((pl.program_id(0) == n_h - 1) & (i == nb - 1))
        def _():
            _push_wait(own, pairs)

    return pl.pallas_call(
        body, name="sb_bwd",
        grid=(n_h, nb),
        in_specs=[pl.BlockSpec((None, B, W), lambda h, i: (1, i, h)),
                  pl.BlockSpec((None, T, W), lambda h, i: (2, 0, h)),
                  pl.BlockSpec((None, T, W), lambda h, i: (3, 0, h)),
                  pl.BlockSpec((B, W), lambda h, i: (i, h)),
                  _ANY],
        out_specs=(pl.BlockSpec((B, W), lambda h, i: (i, h)),
                   pl.BlockSpec((T, W), lambda h, i: (0, h)),
                   pl.BlockSpec((T, W), lambda h, i: (0, h)),
                   _ANY),
        out_shape=(jax.ShapeDtypeStruct((T, 1024), BF16),
                   jax.ShapeDtypeStruct((T, 1024), BF16),
                   jax.ShapeDtypeStruct((T, 1024), BF16),
                   jax.ShapeDtypeStruct(g_p.shape, g_p.dtype)),
        scratch_shapes=[pltpu.VMEM((T, W), F32), pltpu.VMEM((T, W), F32),
                        pltpu.VMEM((HG, nkb, SB_HD, KB), BF16),
                        pltpu.VMEM((nkb, KB, WQ), F32), pltpu.VMEM((nkb, KB, WQ), F32),
                        pltpu.VMEM((HG, SB_HD, B), F32)] + _PUSH_SEMS,
        compiler_params=_cparams(("arbitrary", "arbitrary")),
    )(projb, projb, projb, do_sb, g_p)


def _mid_call(o_gla, o_sb, projf, x, target, wpa, wpb, wo, gla_g, b_gate, final_g):
    T, D = x.shape
    tm = min(TBLK, T)

    def body(og_ref, ggate_ref, osb_ref, sgate_ref, ma_ref, mb_ref, x_ref, tgt_ref,
             wpa_ref, wpb_ref, wo_ref, glag_ref, bg_ref, fg_ref,
             dx2_ref, dogla_ref, dosb_ref, dggate_ref, dsgate_ref, dm_ref,
             mt_ref, ogt_ref, obt_ref, dx2b_ref, dya_ref, dyb_ref,
             dfg_ref, dbg_ref, dglag_ref, loss_ref):
        @pl.when(pl.program_id(0) == 0)
        def _():
            dfg_ref[...] = jnp.zeros_like(dfg_ref)
            dbg_ref[...] = jnp.zeros_like(dbg_ref)
            dglag_ref[...] = jnp.zeros_like(dglag_ref)
            loss_ref[...] = jnp.zeros_like(loss_ref)

        glag = glag_ref[...]
        ggate = ggate_ref[...]
        sg = _sigmoid(ggate)
        silu_g = ggate * sg
        ohat, rinv, nrm = [], [], []
        for hh in range(GLA_HEADS):
            oh = og_ref[:, hh * GLA_HV:(hh + 1) * GLA_HV]
            r = lax.rsqrt(jnp.mean(oh * oh, axis=-1, keepdims=True) + EPS)
            ohat.append(oh * r)
            rinv.append(r)
            nrm.append(ohat[-1] * glag)
        n_all = jnp.concatenate(nrm, axis=1)
        og = n_all * silu_g
        ogb = _bf(og)
        ya = _dot(ogb, wpa_ref[...])
        sgate = sgate_ref[...]
        ss = _sigmoid(sgate)
        silu_s = sgate * ss
        osb = osb_ref[...]
        ob = osb * silu_s
        obb = _bf(ob)
        yb = _dot(obb, wpb_ref[...])
        ga = _sigmoid(ma_ref[...] + bg_ref[:, :D])
        gb = _sigmoid(mb_ref[...] + bg_ref[:, D:])
        merged = ga * ya + gb * yb
        mgb = _bf(merged)
        x2 = x_ref[...] + _dot(mgb, wo_ref[...])
        r2 = lax.rsqrt(jnp.mean(x2 * x2, axis=-1, keepdims=True) + EPS)
        xh2 = x2 * r2
        fg = fg_ref[...]
        err = xh2 * fg - tgt_ref[...]
        loss_ref[...] += jnp.broadcast_to(
            0.5 * jnp.sum(jnp.mean(err * err, axis=-1, keepdims=True), axis=0, keepdims=True), (1, 128))
        dy = err * (1.0 / D)
        dfg_ref[...] += jnp.sum(dy * xh2, axis=0, keepdims=True)
        dxh = dy * fg
        dx2 = r2 * (dxh - xh2 * jnp.mean(dxh * xh2, axis=-1, keepdims=True))
        dx2_ref[...] = dx2
        dx2b = _bf(dx2)
        dx2b_ref[...] = dx2b
        dmerged = _dot_nt(dx2b, wo_ref[...])
        dya = dmerged * ga
        dyb = dmerged * gb
        dma = dmerged * ya * ga * (1.0 - ga)
        dmb = dmerged * yb * gb * (1.0 - gb)
        dm_ref[:, :D] = _bf(dma)
        dm_ref[:, D:] = _bf(dmb)
        dbg_ref[:, :D] += jnp.sum(dma, axis=0, keepdims=True)
        dbg_ref[:, D:] += jnp.sum(dmb, axis=0, keepdims=True)
        dyab = _bf(dya)
        dybb = _bf(dyb)
        dya_ref[...] = dyab
        dyb_ref[...] = dybb
        dog = _dot_nt(dyab, wpa_ref[...])
        dob = _dot_nt(dybb, wpb_ref[...])
        dosb_ref[...] = dob * silu_s
        dsgate_ref[...] = _bf(dob * osb * (ss * (1.0 + sgate * (1.0 - ss))))
        dn = dog * silu_g
        dggate_ref[...] = _bf(dog * n_all * (sg * (1.0 + ggate * (1.0 - sg))))
        dglag = jnp.zeros((1, GLA_HV), F32)
        for hh in range(GLA_HEADS):
            dnh = dn[:, hh * GLA_HV:(hh + 1) * GLA_HV]
            dglag = dglag + jnp.sum(dnh * ohat[hh], axis=0, keepdims=True)
            dohat = dnh * glag
            dogla_ref[:, hh * GLA_HV:(hh + 1) * GLA_HV] = rinv[hh] * (
                dohat - ohat[hh] * jnp.mean(dohat * ohat[hh], axis=-1, keepdims=True))
        dglag_ref[...] += dglag
        mt_ref[...] = _bf(merged.T)
        ogt_ref[...] = _bf(og.T)
        obt_ref[...] = _bf(ob.T)

    row = lambda i: (i, 0)
    const = lambda i: (0, 0)
    tile = pl.BlockSpec((tm, D), row)
    tile_t = pl.BlockSpec((None, D, tm), lambda i: (i, 0, 0))
    wspec = pl.BlockSpec((D, D), const)
    return pl.pallas_call(
        body, name="mid",
        grid=(T // tm,),
        in_specs=[tile,
                  pl.BlockSpec((None, tm, D), lambda i: (1, i, 0)),
                  tile,
                  pl.BlockSpec((None, tm, D), lambda i: (2, i, 0)),
                  pl.BlockSpec((None, tm, D), lambda i: (3, i, 0)),
                  pl.BlockSpec((None, tm, D), lambda i: (4, i, 0)),
                  tile, tile, wspec, wspec, wspec,
                  pl.BlockSpec((1, GLA_HV), const),
                  pl.BlockSpec((1, 2 * D), const),
                  pl.BlockSpec((1, D), const)],
        out_specs=(tile, tile, tile, tile, tile,
                   pl.BlockSpec((tm, 2 * D), row),
                   tile_t, tile_t, tile_t, tile, tile, tile,
                   pl.BlockSpec((1, D), const),
                   pl.BlockSpec((1, 2 * D), const),
                   pl.BlockSpec((1, GLA_HV), const),
                   pl.BlockSpec((1, 128), const)),
        out_shape=(jax.ShapeDtypeStruct((T, D), F32),
                   jax.ShapeDtypeStruct((T, D), F32),
                   jax.ShapeDtypeStruct((T, D), F32),
                   jax.ShapeDtypeStruct((T, D), BF16),
                   jax.ShapeDtypeStruct((T, D), BF16),
                   jax.ShapeDtypeStruct((T, 2 * D), BF16),
                   jax.ShapeDtypeStruct((T // tm, D, tm), BF16),
                   jax.ShapeDtypeStruct((T // tm, D, tm), BF16),
                   jax.ShapeDtypeStruct((T // tm, D, tm), BF16),
                   jax.ShapeDtypeStruct((T, D), BF16),
                   jax.ShapeDtypeStruct((T, D), BF16),
                   jax.ShapeDtypeStruct((T, D), BF16),
                   jax.ShapeDtypeStruct((1, D), F32),
                   jax.ShapeDtypeStruct((1, 2 * D), F32),
                   jax.ShapeDtypeStruct((1, GLA_HV), F32),
                   jax.ShapeDtypeStruct((1, 128), F32)),
        compiler_params=_cparams(("arbitrary",)),
    )(o_gla, projf, o_sb, projf, projf, projf, x, target, wpa, wpb, wo, gla_g, b_gate, final_g)


def _dh_call(pieces, dmlog, drank, wt, wr, x, dx2, norm_g, s_in, small):
    T, D = x.shape
    tm = min(256, T)
    npc = len(pieces)
    n_main = N_GROUPS * 1024
    n_i = T // tm
    i_forward = 5 * n_i // 8

    def body(*refs):
        pcs = refs[:npc]
        (dm_ref, dr_ref, w_hbm, wr_ref, x_ref, dx2_ref, g_ref, sin_ref, small_ref,
         gx_ref, rin_ref, relayed_ref, rsmall_ref,
         w_scr, sems, dg_ref, small_mine, small_send, small_recv, small_loc, *exchange_scratch) = refs[npc:]
        start, forward, finish = _chip_reduce_steps(sin_ref, rin_ref, relayed_ref, *exchange_scratch)

        @pl.when(pl.program_id(0) == 0)
        def _():
            start()
            lo = pltpu.make_async_copy(w_hbm.at[pl.ds(0, RANK_COL)], w_scr.at[pl.ds(0, RANK_COL)], sems.at[0])
            hi = pltpu.make_async_copy(w_hbm.at[pl.ds(RANK_COL + GLA_RANK, n_main - RANK_COL)],
                                       w_scr.at[pl.ds(RANK_COL, n_main - RANK_COL)], sems.at[1])
            lo.start()
            hi.start()
            dg_ref[...] = jnp.zeros_like(dg_ref)
            lo.wait()
            hi.wait()

        @pl.when(pl.program_id(0) == i_forward)
        def _():
            forward()

        def w_group(g):
            return w_scr[g * 1024:(g + 1) * 1024, :]

        dr = dr_ref[...]
        dh = _dot(dr, wr_ref[...])
        for g in range(npc):
            dh = dh + _dot(pcs[g][...], w_group(g))
        dh = dh + _dot(dm_ref[:, :D], w_group(npc))
        dh = dh + _dot(dm_ref[:, D:], w_group(npc + 1))
        xv = x_ref[...]
        r = lax.rsqrt(jnp.mean(xv * xv, axis=-1, keepdims=True) + EPS)
        xhat = xv * r
        g = g_ref[...]
        dg_ref[...] += jnp.sum(dh * xhat, axis=0, keepdims=True)
        dxhat = dh * g
        gx_ref[...] = r * (dxhat - xhat * jnp.mean(dxhat * xhat, axis=-1, keepdims=True)) + dx2_ref[...]

        @pl.when(pl.program_id(0) == n_i - 1)
        def _():
            small_mine[...] = small_ref[...]
            small_mine[:, _SM_NORM:_SM_NORM + D] = dg_ref[...]
            own, pairs = _push_copies(small_mine, rsmall_ref, small_send, small_recv, small_loc, scatter=False)
            _push_start(own, pairs)
            finish()
            _push_wait(own, pairs)

    row = lambda i: (i, 0)
    const = lambda i: (0, 0)
    tile = pl.BlockSpec((tm, D), row)
    part = s_in.shape[1:]
    return pl.pallas_call(
        body, name="dh",
        grid=(n_i,),
        in_specs=[tile] * npc + [
            pl.BlockSpec((tm, 2 * D), row),
            pl.BlockSpec((tm, 128), row),
            _ANY,
            pl.BlockSpec((128, D), const),
            tile, tile,
            pl.BlockSpec((1, D), const),
            _ANY,
            pl.BlockSpec(small.shape, const)],
        out_specs=(tile, _ANY, _ANY, _ANY),
        out_shape=(jax.ShapeDtypeStruct((T, D), F32),
                   jax.ShapeDtypeStruct((3,) + part, s_in.dtype),
                   jax.ShapeDtypeStruct(part, s_in.dtype),
                   jax.ShapeDtypeStruct((N_DEV,) + small.shape, small.dtype)),
        scratch_shapes=[pltpu.VMEM((n_main, D), BF16), pltpu.SemaphoreType.DMA((2,)),
                        pltpu.VMEM((1, D), F32), pltpu.VMEM(small.shape, small.dtype)]
        + _PUSH_SEMS + _chip_reduce_scratch(*part, s_in.dtype),
        compiler_params=_cparams(("arbitrary",)),
    )(*pieces, dmlog, drank, wt, wr, x, dx2, norm_g, s_in, small)


def _wgrad_call(lhs_list, lhs_of_group, rhs_list, rhs_of_group, n_transposed, name, narrow=None):
    n_groups = len(rhs_of_group)
    n_tb, D, tb = lhs_list[0].shape
    T = n_tb * tb
    per = min(4, n_tb)
    tk = per * tb
    nk = T // tk
    nl = len(lhs_list)
    extra = [] if narrow is None else [narrow]

    def tokens_side_by_side(lref):
        return jnp.concatenate([lref[b] for b in range(per)], axis=1)

    def body(*refs):
        lhs = refs[:nl]
        rhs = refs[nl:nl + n_groups]
        rest = refs[nl + n_groups:]
        g = pl.program_id(0)
        i = pl.program_id(1)
        if narrow is None:
            out_ref, acc = rest
        else:
            narrow_ref, out_ref, narrow_out, acc, narrow_acc = rest

            @pl.when((g == 0) & (i == 0))
            def _():
                narrow_acc[...] = jnp.zeros_like(narrow_acc)

            @pl.when(g == 0)
            def _():
                narrow_acc[...] += _dot(tokens_side_by_side(lhs[lhs_of_group[0]]), narrow_ref[...])

            @pl.when((g == 0) & (i == nk - 1))
            def _():
                narrow_out[...] = _bf(narrow_acc[...].T)

        @pl.when(i == 0)
        def _():
            acc[...] = jnp.zeros_like(acc)

        for p in range(n_groups):
            @pl.when(g == p)
            def _(p=p):
                acc[...] += _dot(tokens_side_by_side(lhs[lhs_of_group[p]]), rhs[p][...])

        @pl.when((i == nk - 1) & (g < n_transposed))
        def _():
            out_ref[...] = _bf(acc[...].T)

        @pl.when((i == nk - 1) & (g >= n_transposed))
        def _():
            out_ref[...] = _bf(acc[...])

    def lhs_spec(a):
        groups = [g for g in range(n_groups) if lhs_of_group[g] == a]
        lo, hi = min(groups), max(groups)
        assert groups == list(range(lo, hi + 1))
        return pl.BlockSpec((per, D, tb), lambda g, i: (jnp.where((g >= lo) & (g <= hi), i, 0), 0, 0))

    def rhs_spec(p):
        cb = rhs_of_group[p][1]
        return pl.BlockSpec((tk, 1024), lambda g, i: (jnp.where(g == p, i, 0), cb))

    res = pl.pallas_call(
        body, name=name,
        grid=(n_groups, nk),
        in_specs=[lhs_spec(a) for a in range(nl)] + [rhs_spec(p) for p in range(n_groups)]
        + [pl.BlockSpec((tk, 128), lambda g, i: (jnp.where(g == 0, i, 0), 0)) for _ in extra],
        out_specs=[pl.BlockSpec((None, D, 1024), lambda g, i: (g, 0, 0))]
        + [pl.BlockSpec((128, D), lambda g, i: (0, 0)) for _ in extra],
        out_shape=[jax.ShapeDtypeStruct((n_groups, D, 1024), BF16)]
        + [jax.ShapeDtypeStruct((128, D), BF16) for _ in extra],
        scratch_shapes=[pltpu.VMEM((D, 1024), F32)] + [pltpu.VMEM((D, 128), F32) for _ in extra],
        compiler_params=_cparams(("arbitrary", "arbitrary")),
    )(*lhs_list, *[rhs_list[rhs_of_group[p][0]] for p in range(n_groups)], *extra)
    return res[0] if narrow is None else res


def _adamw_math(parts, w, m, v):
    g = parts[0].astype(F32)
    for p in parts[1:]:
        g = g + p.astype(F32)
    mm = ADAM_B1 * m + (1.0 - ADAM_B1) * g
    vv = ADAM_B2 * v + (1.0 - ADAM_B2) * (g * g)
    m_hat = mm / (1.0 - ADAM_B1 ** ADAM_STEP)
    v_hat = vv / (1.0 - ADAM_B2 ** ADAM_STEP)
    return g, -ADAM_LR * (m_hat / (jnp.sqrt(v_hat) + ADAM_EPS) + ADAM_WD * w), mm, vv


def _part_order(n_parts):
    return [n_parts - 1] + list(range(n_parts - 1))


def _adamw_call(parts, w, m, v, name):
    R, C = w.shape
    n_parts = parts.shape[0]
    (tr, tc), grid, idx = _tiling_2d(R, C, 512)

    def body(p_ref, w_ref, m_ref, v_ref, g_ref, d_ref, nm_ref, nv_ref):
        g_ref[...], d_ref[...], nm_ref[...], nv_ref[...] = _adamw_math(
            [p_ref[k] for k in _part_order(n_parts)], w_ref[...], m_ref[...], v_ref[...])

    blk = pl.BlockSpec((tr, tc), idx)
    sds = jax.ShapeDtypeStruct((R, C), F32)
    return pl.pallas_call(
        body, name=name,
        grid=grid,
        in_specs=[pl.BlockSpec((n_parts, tr, tc), lambda i: (0,) + idx(i)), blk, blk, blk],
        out_specs=(blk, blk, blk, blk),
        out_shape=(sds, sds, sds, sds),
        compiler_params=_cparams(("arbitrary",)),
    )(parts, w, m, v)


def _adamw_rows_call(parts, ws, ms, vs, name, side_parts, side_jobs, total_lane):
    n = len(ws)
    R, C = ws[0].shape
    n_parts = parts.shape[0]
    n_sp, n_jobs = len(side_parts), len(side_jobs)
    side_w = [job[2] for job in side_jobs]

    def body(*refs):
        p_ref = refs[0]
        w_refs, m_refs, v_refs = refs[1:1 + n], refs[1 + n:1 + 2 * n], refs[1 + 2 * n:1 + 3 * n]
        sp_refs = refs[1 + 3 * n:1 + 3 * n + n_sp]
        sw_refs = refs[1 + 3 * n + n_sp:1 + 3 * n + n_sp + 3 * n_jobs]
        outs = refs[1 + 3 * n + n_sp + 3 * n_jobs:]
        for k in range(n):
            @pl.when(pl.program_id(0) == k)
            def _(k=k):
                res = _adamw_math([p_ref[j] for j in _part_order(n_parts)],
                                  w_refs[k][...], m_refs[k][...], v_refs[k][...])
                for o_ref, val in zip(outs[4 * k:4 * k + 4], res):
                    o_ref[...] = val

        @pl.when(pl.program_id(0) == 0)
        def _():
            for q, (i, lanes, _, _, _) in enumerate(side_jobs):
                src = sp_refs[i]
                order = _part_order(src.shape[0])
                res = _adamw_math([src[j] if lanes is None else src[j, :, lanes] for j in order],
                                  *[r[...] for r in sw_refs[3 * q:3 * q + 3]])
                for o_ref, val in zip(outs[4 * (n + q):4 * (n + q) + 4], res):
                    o_ref[...] = val
            outs[-1][...] = jnp.sum(sp_refs[0][:, :, total_lane:total_lane + 1], axis=0)

    def resident(a):
        return pl.BlockSpec(a.shape, lambda k, nd=a.ndim: (0,) * nd)

    whole = pl.BlockSpec((R, C), lambda k: (0, 0))
    sds = jax.ShapeDtypeStruct((R, C), F32)
    side_wmv = [a for job in side_jobs for a in job[2:]]
    res = pl.pallas_call(
        body, name=name,
        grid=(n,),
        in_specs=[pl.BlockSpec((n_parts, R, C), lambda k: (0, k, 0))] + [whole] * (3 * n)
        + [resident(a) for a in side_parts] + [resident(a) for a in side_wmv],
        out_specs=tuple([whole] * (4 * n) + [resident(w) for w in side_w for _ in range(4)]
                        + [pl.BlockSpec((1, 1), lambda k: (0, 0))]),
        out_shape=tuple([sds] * (4 * n) + [jax.ShapeDtypeStruct(w.shape, F32) for w in side_w for _ in range(4)]
                        + [jax.ShapeDtypeStruct((1, 1), F32)]),
        compiler_params=_cparams(("arbitrary",)),
    )(parts, *ws, *ms, *vs, *side_parts, *side_wmv)
    return [res[4 * k:4 * k + 4] for k in range(n + n_jobs)], res[-1]


def _local_step(x, target, wt, wr, wdec, bdec, wp_shard, norm_g, gla_g, b_gate, final_g):
    D = x.shape[1]
    half = wp_shard.shape[1] // 2
    projf, projb, rank, ht, wp_lo = _proj_call(x, norm_g, wt, wr, wp_shard[:, :half])
    o_gla, st_all, la = _gla_fwd_call(projf, projb, rank, wdec, bdec)
    o_sb, wp_hi = _sb_fwd_call(projb, wp_shard[:, half:])
    wp_full = jnp.concatenate([wp_lo, wp_hi], axis=2).transpose(1, 0, 2, 3).reshape(3, D, D)
    (dx2, do_gla, do_sb, dggate, dsgate, dmlog, mt, ogt, obt, dx2b, dya, dyb,
     dfinal_g, db_gate, dgla_g, loss) = _mid_call(o_gla, o_sb, projf, x, target, wp_full[0], wp_full[1],
                                                 wp_full[2], gla_g, b_gate, final_g)
    dw_p = _wgrad_call([ogt, obt, mt], [0, 1, 2], [dya, dyb, dx2b], [(0, 0), (1, 0), (2, 0)], 0, "wgrad_p")
    g_p = dw_p.reshape(3, N_DEV, D // N_DEV, D).transpose(1, 0, 2, 3).reshape(N_DEV, 3 * (D // N_DEV), D)
    dqk, dgv, drank, dwdec, dbdec = _gla_bwd_call(projf, projb, la, do_gla, st_all, rank, wdec)
    dsq, dsk, dsv, r_p = _sb_bwd_call(projb, do_sb, g_p)
    pieces = [dqk, dgv, dggate, dsq, dsk, dsv, dsgate]
    rhs_of_group = [(g, 0) for g in range(7)] + [(7, 0), (7, 1)]
    dw_in, dwr = _wgrad_call([ht], [0] * N_GROUPS, pieces + [dmlog], rhs_of_group, N_GROUPS, "wgrad_in",
                             narrow=drank)
    s_in = _pair_sum_call(dw_in.reshape(N_GROUPS * 1024, D), dwr)
    small = jnp.concatenate([
        jnp.zeros((D,), F32), dbdec.reshape(-1), dgla_g.reshape(-1), db_gate.reshape(-1), dfinal_g.reshape(-1),
        loss.reshape(-1), dwdec[:GLA_RANK].reshape(-1)]).reshape(1, _SM_LEN)
    grad_x, r_in, _, r_small = _dh_call(pieces, dmlog, drank, wt, wr, x, dx2, norm_g, s_in, small)
    return grad_x, r_in, r_p, r_small


_SM_NORM = 0
_SM_BDEC = _SM_NORM + D_MODEL
_SM_GLAG = _SM_BDEC + GLA_DK
_SM_BGATE = _SM_GLAG + GLA_HV
_SM_FINAL = _SM_BGATE + 2 * D_MODEL
_SM_REPL = _SM_FINAL + D_MODEL
_SM_LOSS = _SM_REPL
_SM_WDEC = _SM_LOSS + 128
_SM_LEN = _SM_WDEC + GLA_RANK * GLA_DK


def kernel(x, norm_g, w_in, w_dec_up, b_dec, gla_norm_g, w_pa, w_pb, b_gate, w_o, final_g, loss_target, m_norm_g, m_w_in, m_w_dec_up, m_b_dec, m_gla_norm_g, m_w_pa, m_w_pb, m_b_gate, m_w_o, m_final_g, v_norm_g, v_w_in, v_w_dec_up, v_b_dec, v_gla_norm_g, v_w_pa, v_w_pb, v_b_gate, v_w_o, v_final_g):
    D = D_MODEL
    me = 4 * lax.axis_index("x") + 2 * lax.axis_index("y") + lax.axis_index("c")

    wp_shard = jnp.stack([w_pa, w_pb, w_o]).astype(BF16)
    n_first = _half_rows(SHARD_COLS)
    win_all, wdec_all = _all_gather([w_in.T.astype(BF16), w_dec_up], "gather_w",
                                    row_pieces=[[(0, n_first), (n_first, SHARD_COLS - n_first)], None])
    wt = _flatten_blocks_call(win_all)
    wr = jnp.pad(wt[RANK_COL:RANK_COL + GLA_RANK], ((0, 128 - GLA_RANK), (0, 0)))
    wdec_full = wdec_all.transpose(1, 0, 2).reshape(GLA_RANK, GLA_DK)
    wdec = jnp.pad(wdec_full, ((0, 128 - GLA_RANK), (0, 0)))

    grad_x, r_in, r_p, r_small = _local_step(
        x[0], loss_target[0], wt, wr, wdec, b_dec.reshape(1, -1), wp_shard,
        norm_g.reshape(1, -1), gla_norm_g.reshape(1, -1), b_gate.reshape(1, -1), final_g.reshape(1, -1))

    gw_in, d_in, nm_in, nv_in = (a.T for a in _adamw_call(r_in, w_in.T, m_w_in.T, v_w_in.T, "adamw_in"))
    wdec_parts = r_small[:, 0, _SM_WDEC:].reshape(N_DEV, GLA_RANK, GLA_DK)
    cols = GLA_DK // N_DEV
    wdec_mine = lax.dynamic_slice_in_dim(wdec_parts, me * cols, cols, axis=2)

    def row(a):
        return a.reshape(1, -1)

    rep_jobs = [(0, slice(at, at + w.size), row(w), row(m), row(v)) for at, w, m, v in (
        (_SM_NORM, norm_g, m_norm_g, v_norm_g), (_SM_BDEC, b_dec, m_b_dec, v_b_dec),
        (_SM_GLAG, gla_norm_g, m_gla_norm_g, v_gla_norm_g), (_SM_BGATE, b_gate, m_b_gate, v_b_gate),
        (_SM_FINAL, final_g, m_final_g, v_final_g))]
    quads, loss_total = _adamw_rows_call(
        r_p, [w_pa, w_pb, w_o], [m_w_pa, m_w_pb, m_w_o], [v_w_pa, v_w_pb, v_w_o], "adamw_p",
        [r_small, wdec_mine], rep_jobs + [(1, None, w_dec_up, m_w_dec_up, v_w_dec_up)], _SM_LOSS)
    (g_pa, d_pa, nm_pa, nv_pa), (g_pb, d_pb, nm_pb, nv_pb), (g_o, d_o, nm_o, nv_o) = quads[:3]
    ((g_norm, d_norm, nm_norm, nv_norm), (g_bdec, d_bdec, nm_bdec, nv_bdec), (g_glag, d_glag, nm_glag, nv_glag),
     (g_bgate, d_bgate, nm_bgate, nv_bgate), (g_final, d_final, nm_final, nv_final)) = [
        tuple(a.reshape(-1) for a in quad) for quad in quads[3:8]]
    g_wdec, d_wdec, nm_wdec, nv_wdec = quads[8]

    return (loss_total.reshape(()), grad_x[None],
            g_norm, gw_in, g_wdec, g_bdec, g_glag, g_pa, g_pb, g_bgate, g_o, g_final,
            d_norm, d_in, d_wdec, d_bdec, d_glag, d_pa, d_pb, d_bgate, d_o, d_final,
            nm_norm, nm_in, nm_wdec, nm_bdec, nm_glag, nm_pa, nm_pb, nm_bgate, nm_o, nm_final,
            nv_norm, nv_in, nv_wdec, nv_bdec, nv_glag, nv_pa, nv_pb, nv_bgate, nv_o, nv_final)
```
